```python
import math
import jax, jax.numpy as jnp
from jax import lax
import numpy as np

D_MODEL = 1024
BATCH = 8
SEQ = 8192
DEPTH = 1

PLE_DIM = 256
HG_HEADS = 8
HG_KEY = 128
HG_VAL = 128
HG_KWIDTH = HG_HEADS * HG_KEY
HG_WIDTH = HG_HEADS * HG_VAL
HG_CHUNK = 64
S5_GROUP = 16
S5_WIDTH = 512
S5_GROUPS = S5_WIDTH // S5_GROUP
S5_STATE = 64
DT_MIN = 0.001
DT_MAX = 0.1
NORM_EPS = 1e-6
IN_SIZES = (HG_KWIDTH, HG_KWIDTH, HG_WIDTH, HG_WIDTH, S5_WIDTH, S5_WIDTH, D_MODEL, D_MODEL)
IN_COLS = HG_KWIDTH * 2 + HG_WIDTH * 2 + S5_WIDTH * 2 + D_MODEL * 2

kernel_name = "hybrid_hgrn2_s5_gated_block"


def _split_points():
    pts, acc = [], 0
    for s in IN_SIZES[:-1]:
        acc += s
        pts.append(acc)
    return pts


def rms_norm(x, g):
    xf = x.astype(jnp.float32)
    y = xf * lax.rsqrt(jnp.mean(xf * xf, axis=-1, keepdims=True) + NORM_EPS)
    return (y * g.astype(jnp.float32)).astype(x.dtype)


def hgrn2_mix(q, f_logit, iv, lb):
    f32 = jnp.float32
    bsz, seq = q.shape[0], q.shape[1]
    nc = seq // HG_CHUNK
    lb = lb.reshape(HG_HEADS, HG_KEY).astype(f32)
    sig = jax.nn.sigmoid(f_logit.astype(f32))
    log_f = jnp.log(lb + (1.0 - lb) * sig)
    k = (1.0 - lb) * (1.0 - sig)
    chunk = lambda t: t.astype(f32).reshape(bsz, nc, HG_CHUNK, HG_HEADS, t.shape[-1])
    q, k, iv, log_f = chunk(q), chunk(k), chunk(iv), chunk(log_f)
    b = jnp.cumsum(log_f, axis=2)
    b_mid = b[:, :, HG_CHUNK // 2 - 1:HG_CHUNK // 2]
    b_last = b[:, :, HG_CHUNK - 1:HG_CHUNK]
    scores = jnp.einsum('bnthk,bnshk->bnhts', q * jnp.exp(b - b_mid), k * jnp.exp(b_mid - b))
    causal = jnp.tril(jnp.ones((HG_CHUNK, HG_CHUNK), dtype=bool))
    scores = jnp.where(causal, scores, 0.0)
    o_intra = jnp.einsum('bnhts,bnshv->bnthv', scores, iv)
    q_dec = q * jnp.exp(b)
    k_dec = k * jnp.exp(b_last - b)
    chunk_decay = jnp.exp(b_last[:, :, 0])

    def step(state, xs):
        qc, kc, ic, dc = xs
        o = jnp.einsum('bthk,bhkv->bthv', qc, state)
        state = dc[..., None] * state + jnp.einsum('bthk,bthv->bhkv', kc, ic)
        return state, o

    s0 = jnp.zeros((bsz, HG_HEADS, HG_KEY, HG_VAL), f32)
    mv = lambda t: jnp.moveaxis(t, 1, 0)
    _, o_inter = lax.scan(step, s0, (mv(q_dec), mv(k_dec), mv(iv), mv(chunk_decay)))
    o = o_intra + jnp.moveaxis(o_inter, 0, 1)
    return o.reshape(bsz, seq, HG_HEADS, HG_VAL)


def _cmul_combine(e1, e2):
    a1r, a1i, x1r, x1i = e1
    a2r, a2i, x2r, x2i = e2
    return (a1r * a2r - a1i * a2i,
            a1r * a2i + a1i * a2r,
            a2r * x1r - a2i * x1i + x2r,
            a2r * x1i + a2i * x1r + x2i)


def s5_mix(u, a_re, a_im, log_dt, b_re, b_im, c_re, c_im, d):
    f32 = jnp.float32
    uf = u.astype(f32)
    seq = u.shape[1]
    dt = jnp.exp(log_dt.astype(f32))[:, None]
    ar, ai = a_re.astype(f32), a_im.astype(f32)
    mag = jnp.exp(ar * dt)
    lr, li = mag * jnp.cos(ai * dt), mag * jnp.sin(ai * dt)
    den = ar * ar + ai * ai
    nr = lr - 1.0
    sr = (nr * ar + li * ai) / den
    si = (li * ar - nr * ai) / den
    br, bi = b_re.astype(f32), b_im.astype(f32)
    bbr = sr[..., None] * br - si[..., None] * bi
    bbi = sr[..., None] * bi + si[..., None] * br
    xr = jnp.einsum('bsgc,gnc->bsgn', uf, bbr)
    xi = jnp.einsum('bsgc,gnc->bsgn', uf, bbi)
    shape = (1, seq) + lr.shape
    lam_r = jnp.broadcast_to(lr[None, None], shape)
    lam_i = jnp.broadcast_to(li[None, None], shape)
    _, _, hr, hi = lax.associative_scan(_cmul_combine, (lam_r, lam_i, xr, xi), axis=1)
    y = (jnp.einsum('bsgn,gcn->bsgc', hr, c_re.astype(f32))
         - jnp.einsum('bsgn,gcn->bsgc', hi, c_im.astype(f32))
         + d.astype(f32) * uf)
    return y


def _fwd_setup_inputs(seed: int = 0) -> dict:
    key = jax.random.key(seed)
    ks = jax.random.split(key, 24)
    n = lambda k, shape, scale: jax.random.normal(k, shape, jnp.float32) * scale
    L, G, N, C = DEPTH, S5_GROUPS, S5_STATE, S5_GROUP
    a_im_base = jnp.pi * jnp.arange(N, dtype=jnp.float32)
    return {
        "x": n(ks[0], (BATCH, SEQ, D_MODEL), 1.0),
        "p": n(ks[1], (DEPTH, BATCH, SEQ, PLE_DIM), 1.0),
        "norm_g": 1.0 + n(ks[2], (L, D_MODEL), 0.02),
        "w_in": n(ks[3], (L, D_MODEL, IN_COLS), D_MODEL ** -0.5),
        "hg_lb": n(ks[4], (DEPTH + 1, HG_KWIDTH), 0.1),
        "hg_norm_g": 1.0 + n(ks[5], (L, HG_WIDTH), 0.02),
        "w_o_hg": n(ks[6], (L, HG_WIDTH, D_MODEL), HG_WIDTH ** -0.5),
        "s5_a_re": -0.5 + n(ks[7], (L, G, N), 0.01),
        "s5_a_im": a_im_base[None, None, :] + n(ks[8], (L, G, N), 0.01),
        "s5_log_dt": jax.random.uniform(ks[9], (L, G), jnp.float32, math.log(DT_MIN), math.log(DT_MAX)),
        "s5_b_re": n(ks[10], (L, G, N, C), (2.0 * C) ** -0.5),
        "s5_b_im": n(ks[11], (L, G, N, C), (2.0 * C) ** -0.5),
        "s5_c_re": n(ks[12], (L, G, C, N), N ** -0.5),
        "s5_c_im": n(ks[13], (L, G, C, N), N ** -0.5),
        "s5_d": n(ks[14], (L, G, C), 1.0),
        "w_glu": n(ks[15], (L, S5_WIDTH, 2 * S5_WIDTH), S5_WIDTH ** -0.5),
        "b_glu": n(ks[16], (L, 2 * S5_WIDTH), 0.01),
        "w_o_s5": n(ks[17], (L, S5_WIDTH, D_MODEL), S5_WIDTH ** -0.5),
        "w_out": n(ks[18], (L, D_MODEL, D_MODEL), D_MODEL ** -0.5),
        "ple_norm_g": 1.0 + n(ks[19], (L, D_MODEL), 0.02),
        "w_ple": n(ks[20], (L, PLE_DIM, D_MODEL), PLE_DIM ** -0.5),
        "w_ple_gate": n(ks[21], (L, D_MODEL, D_MODEL), D_MODEL ** -0.5),
        "final_norm_g": 1.0 + n(ks[22], (D_MODEL,), 0.02),
    }


def _fwd_reference(x, p, norm_g, w_in, hg_lb, hg_norm_g, w_o_hg, s5_a_re, s5_a_im, s5_log_dt,
              s5_b_re, s5_b_im, s5_c_re, s5_c_im, s5_d, w_glu, b_glu, w_o_s5, w_out,
              ple_norm_g, w_ple, w_ple_gate, final_norm_g):
    bsz, seq = x.shape[0], x.shape[1]
    h = x
    lb_all = jnp.cumsum(jax.nn.softmax(hg_lb.astype(jnp.float32), axis=0), axis=0)
    for l in range(DEPTH):
        u = rms_norm(h, norm_g[l])
        proj = u @ w_in[l]
        q, f_logit, iv, g_hg, u_s, z_s, gate_hg, gate_s5 = jnp.split(proj, _split_points(), axis=-1)

        o = hgrn2_mix(q.reshape(bsz, seq, HG_HEADS, HG_KEY),
                      f_logit.reshape(bsz, seq, HG_HEADS, HG_KEY),
                      iv.reshape(bsz, seq, HG_HEADS, HG_VAL), lb_all[l])
        o = rms_norm(o, hg_norm_g[l].reshape(HG_HEADS, HG_VAL)).reshape(bsz, seq, HG_WIDTH)
        y_hg = ((o * jax.nn.silu(g_hg.astype(jnp.float32))).astype(h.dtype) @ w_o_hg[l]).astype(h.dtype)

        ys = s5_mix(u_s.reshape(bsz, seq, S5_GROUPS, S5_GROUP), s5_a_re[l], s5_a_im[l], s5_log_dt[l],
                    s5_b_re[l], s5_b_im[l], s5_c_re[l], s5_c_im[l], s5_d[l]).reshape(bsz, seq, S5_WIDTH)
        ys = jax.nn.gelu(ys).astype(h.dtype)
        glu_a, glu_b = jnp.split(ys @ w_glu[l] + b_glu[l], 2, axis=-1)
        ys = glu_a * jax.nn.sigmoid(glu_b) * jax.nn.silu(z_s)
        y_s5 = (ys @ w_o_s5[l]).astype(h.dtype)

        merged = jax.nn.sigmoid(gate_hg) * y_hg + jax.nn.sigmoid(gate_s5) * y_s5
        h = h + (merged @ w_out[l]).astype(h.dtype)

        pe = p[l] @ w_ple[l]
        gate = jax.nn.sigmoid(rms_norm(h, ple_norm_g[l]) @ w_ple_gate[l])
        h = h + (pe * gate).astype(h.dtype)
    return rms_norm(h, final_norm_g)


import jax as _jax
import jax.numpy as _jnp

TWIN_FORMAT = 'train_step'
FWD_PARAMS = ['x', 'p', 'norm_g', 'w_in', 'hg_lb', 'hg_norm_g', 'w_o_hg', 's5_a_re', 's5_a_im', 's5_log_dt', 's5_b_re', 's5_b_im', 's5_c_re', 's5_c_im', 's5_d', 'w_glu', 'b_glu', 'w_o_s5', 'w_out', 'ple_norm_g', 'w_ple', 'w_ple_gate', 'final_norm_g']
TWIN_WEIGHTS = ['norm_g', 'w_in', 'hg_lb', 'hg_norm_g', 'w_o_hg', 's5_a_re', 's5_a_im', 's5_log_dt', 's5_b_re', 's5_b_im', 's5_c_re', 's5_c_im', 's5_d', 'w_glu', 'b_glu', 'w_o_s5', 'w_out', 'ple_norm_g', 'w_ple', 'w_ple_gate', 'final_norm_g']
TWIN_DIFF_INPUT = 'x'
TWIN_INPUTS = ['x', 'p', 'norm_g', 'w_in', 'hg_lb', 'hg_norm_g', 'w_o_hg', 's5_a_re', 's5_a_im', 's5_log_dt', 's5_b_re', 's5_b_im', 's5_c_re', 's5_c_im', 's5_d', 'w_glu', 'b_glu', 'w_o_s5', 'w_out', 'ple_norm_g', 'w_ple', 'w_ple_gate', 'final_norm_g', 'loss_target', 'm_norm_g', 'm_w_in', 'm_hg_lb', 'm_hg_norm_g', 'm_w_o_hg', 'm_s5_a_re', 'm_s5_a_im', 'm_s5_log_dt', 'm_s5_b_re', 'm_s5_b_im', 'm_s5_c_re', 'm_s5_c_im', 'm_s5_d', 'm_w_glu', 'm_b_glu', 'm_w_o_s5', 'm_w_out', 'm_ple_norm_g', 'm_w_ple', 'm_w_ple_gate', 'm_final_norm_g', 'v_norm_g', 'v_w_in', 'v_hg_lb', 'v_hg_norm_g', 'v_w_o_hg', 'v_s5_a_re', 'v_s5_a_im', 'v_s5_log_dt', 'v_s5_b_re', 'v_s5_b_im', 'v_s5_c_re', 'v_s5_c_im', 'v_s5_d', 'v_w_glu', 'v_b_glu', 'v_w_o_s5', 'v_w_out', 'v_ple_norm_g', 'v_w_ple', 'v_w_ple_gate', 'v_final_norm_g']
TWIN_OUTPUTS = ['loss', 'grad_x', 'grad_norm_g', 'grad_w_in', 'grad_hg_lb', 'grad_hg_norm_g', 'grad_w_o_hg', 'grad_s5_a_re', 'grad_s5_a_im', 'grad_s5_log_dt', 'grad_s5_b_re', 'grad_s5_b_im', 'grad_s5_c_re', 'grad_s5_c_im', 'grad_s5_d', 'grad_w_glu', 'grad_b_glu', 'grad_w_o_s5', 'grad_w_out', 'grad_ple_norm_g', 'grad_w_ple', 'grad_w_ple_gate', 'grad_final_norm_g', 'delta_norm_g', 'delta_w_in', 'delta_hg_lb', 'delta_hg_norm_g', 'delta_w_o_hg', 'delta_s5_a_re', 'delta_s5_a_im', 'delta_s5_log_dt', 'delta_s5_b_re', 'delta_s5_b_im', 'delta_s5_c_re', 'delta_s5_c_im', 'delta_s5_d', 'delta_w_glu', 'delta_b_glu', 'delta_w_o_s5', 'delta_w_out', 'delta_ple_norm_g', 'delta_w_ple', 'delta_w_ple_gate', 'delta_final_norm_g', 'new_m_norm_g', 'new_m_w_in', 'new_m_hg_lb', 'new_m_hg_norm_g', 'new_m_w_o_hg', 'new_m_s5_a_re', 'new_m_s5_a_im', 'new_m_s5_log_dt', 'new_m_s5_b_re', 'new_m_s5_b_im', 'new_m_s5_c_re', 'new_m_s5_c_im', 'new_m_s5_d', 'new_m_w_glu', 'new_m_b_glu', 'new_m_w_o_s5', 'new_m_w_out', 'new_m_ple_norm_g', 'new_m_w_ple', 'new_m_w_ple_gate', 'new_m_final_norm_g', 'new_v_norm_g', 'new_v_w_in', 'new_v_hg_lb', 'new_v_hg_norm_g', 'new_v_w_o_hg', 'new_v_s5_a_re', 'new_v_s5_a_im', 'new_v_s5_log_dt', 'new_v_s5_b_re', 'new_v_s5_b_im', 'new_v_s5_c_re', 'new_v_s5_c_im', 'new_v_s5_d', 'new_v_w_glu', 'new_v_b_glu', 'new_v_w_o_s5', 'new_v_w_out', 'new_v_ple_norm_g', 'new_v_w_ple', 'new_v_w_ple_gate', 'new_v_final_norm_g']
TWIN_LEAF_KINDS = {'loss': 'loss', 'grad_x': 'grad_x', 'grad_norm_g': 'grad_w', 'grad_w_in': 'grad_w', 'grad_hg_lb': 'grad_w', 'grad_hg_norm_g': 'grad_w', 'grad_w_o_hg': 'grad_w', 'grad_s5_a_re': 'grad_w', 'grad_s5_a_im': 'grad_w', 'grad_s5_log_dt': 'grad_w', 'grad_s5_b_re': 'grad_w', 'grad_s5_b_im': 'grad_w', 'grad_s5_c_re': 'grad_w', 'grad_s5_c_im': 'grad_w', 'grad_s5_d': 'grad_w', 'grad_w_glu': 'grad_w', 'grad_b_glu': 'grad_w', 'grad_w_o_s5': 'grad_w', 'grad_w_out': 'grad_w', 'grad_ple_norm_g': 'grad_w', 'grad_w_ple': 'grad_w', 'grad_w_ple_gate': 'grad_w', 'grad_final_norm_g': 'grad_w', 'delta_norm_g': 'delta_w', 'delta_w_in': 'delta_w', 'delta_hg_lb': 'delta_w', 'delta_hg_norm_g': 'delta_w', 'delta_w_o_hg': 'delta_w', 'delta_s5_a_re': 'delta_w', 'delta_s5_a_im': 'delta_w', 'delta_s5_log_dt': 'delta_w', 'delta_s5_b_re': 'delta_w', 'delta_s5_b_im': 'delta_w', 'delta_s5_c_re': 'delta_w', 'delta_s5_c_im': 'delta_w', 'delta_s5_d': 'delta_w', 'delta_w_glu': 'delta_w', 'delta_b_glu': 'delta_w', 'delta_w_o_s5': 'delta_w', 'delta_w_out': 'delta_w', 'delta_ple_norm_g': 'delta_w', 'delta_w_ple': 'delta_w', 'delta_w_ple_gate': 'delta_w', 'delta_final_norm_g': 'delta_w', 'new_m_norm_g': 'new_m', 'new_m_w_in': 'new_m', 'new_m_hg_lb': 'new_m', 'new_m_hg_norm_g': 'new_m', 'new_m_w_o_hg': 'new_m', 'new_m_s5_a_re': 'new_m', 'new_m_s5_a_im': 'new_m', 'new_m_s5_log_dt': 'new_m', 'new_m_s5_b_re': 'new_m', 'new_m_s5_b_im': 'new_m', 'new_m_s5_c_re': 'new_m', 'new_m_s5_c_im': 'new_m', 'new_m_s5_d': 'new_m', 'new_m_w_glu': 'new_m', 'new_m_b_glu': 'new_m', 'new_m_w_o_s5': 'new_m', 'new_m_w_out': 'new_m', 'new_m_ple_norm_g': 'new_m', 'new_m_w_ple': 'new_m', 'new_m_w_ple_gate': 'new_m', 'new_m_final_norm_g': 'new_m', 'new_v_norm_g': 'new_v', 'new_v_w_in': 'new_v', 'new_v_hg_lb': 'new_v', 'new_v_hg_norm_g': 'new_v', 'new_v_w_o_hg': 'new_v', 'new_v_s5_a_re': 'new_v', 'new_v_s5_a_im': 'new_v', 'new_v_s5_log_dt': 'new_v', 'new_v_s5_b_re': 'new_v', 'new_v_s5_b_im': 'new_v', 'new_v_s5_c_re': 'new_v', 'new_v_s5_c_im': 'new_v', 'new_v_s5_d': 'new_v', 'new_v_w_glu': 'new_v', 'new_v_b_glu': 'new_v', 'new_v_w_o_s5': 'new_v', 'new_v_w_out': 'new_v', 'new_v_ple_norm_g': 'new_v', 'new_v_w_ple': 'new_v', 'new_v_w_ple_gate': 'new_v', 'new_v_final_norm_g': 'new_v'}


def _forward(args):
    return _fwd_reference(*[args[k] for k in FWD_PARAMS])


def _output_shape():
    def fwd():
        inp = _fwd_setup_inputs(0)
        return _fwd_reference(*[inp[k] for k in FWD_PARAMS])
    out = _jax.eval_shape(fwd)
    return out.shape, out.dtype

N_MICROBATCH = 1
ADAM_LR = 0.001
ADAM_B1 = 0.9
ADAM_B2 = 0.999
ADAM_EPS = 1e-08
ADAM_WD = 0.01
ADAM_STEP = 10
PER_EXAMPLE_BATCH_AXIS = {'x': 0, 'p': 1, 'loss_target': 0}
SHARED_INPUTS = []
_WEIGHT_DTYPES = {'norm_g': _jnp.float32, 'w_in': _jnp.float32, 'hg_lb': _jnp.float32, 'hg_norm_g': _jnp.float32, 'w_o_hg': _jnp.float32, 's5_a_re': _jnp.float32, 's5_a_im': _jnp.float32, 's5_log_dt': _jnp.float32, 's5_b_re': _jnp.float32, 's5_b_im': _jnp.float32, 's5_c_re': _jnp.float32, 's5_c_im': _jnp.float32, 's5_d': _jnp.float32, 'w_glu': _jnp.float32, 'b_glu': _jnp.float32, 'w_o_s5': _jnp.float32, 'w_out': _jnp.float32, 'ple_norm_g': _jnp.float32, 'w_ple': _jnp.float32, 'w_ple_gate': _jnp.float32, 'final_norm_g': _jnp.float32}
MOMENT_SCALE = {'norm_g': 1.607284e-01, 'w_in': 6.041263e-02, 'hg_lb': 4.763098e-02, 'hg_norm_g': 7.038708e-02, 'w_o_hg': 6.937730e-02, 's5_a_re': 2.836116e-03, 's5_a_im': 2.789450e-03, 's5_log_dt': 3.179551e+00, 's5_b_re': 1.695798e-03, 's5_b_im': 1.717010e-03, 's5_c_re': 2.381749e-03, 's5_c_im': 2.471352e-03, 's5_d': 3.971186e-02, 'w_glu': 2.494544e-02, 'b_glu': 3.878883e-02, 'w_o_s5': 2.382107e-02, 'w_out': 7.350156e-02, 'ple_norm_g': 4.583970e-02, 'w_ple': 1.154055e-01, 'w_ple_gate': 4.450210e-02, 'final_norm_g': 6.397156e+01}


def _to_microbatches(a, axis):
    t = _jnp.moveaxis(a, axis, 0)
    t = t.reshape((N_MICROBATCH, t.shape[0] // N_MICROBATCH) + t.shape[1:])
    return _jnp.moveaxis(t, 1, axis + 1)


def setup_inputs(seed: int = 0) -> dict:
    inp = _fwd_setup_inputs(seed)
    key = _jax.random.fold_in(_jax.random.key(seed), 7919)
    shape, _ = _output_shape()
    out = dict(inp)
    out["loss_target"] = _jax.random.normal(_jax.random.fold_in(key, 0), shape, _jnp.float32)
    for i, name in enumerate(TWIN_WEIGHTS):
        w = inp[name].astype(_jnp.float32)
        if MOMENT_SCALE is None:
            s = _jnp.sqrt(_jnp.mean(_jnp.square(w)) + 1e-30)
        else:
            s = MOMENT_SCALE[name]
        km, kv = _jax.random.split(_jax.random.fold_in(key, i + 1))
        out[name] = w
        out["m_" + name] = s * _jax.random.normal(km, w.shape, _jnp.float32)
        out["v_" + name] = (s * s) * _jax.random.uniform(kv, w.shape, _jnp.float32, 0.5, 1.5)
    if N_MICROBATCH > 1:
        for name, axis in PER_EXAMPLE_BATCH_AXIS.items():
            out[name] = _to_microbatches(out[name], axis)
    return {'x': out['x'], 'p': out['p'], 'norm_g': out['norm_g'], 'w_in': out['w_in'], 'hg_lb': out['hg_lb'], 'hg_norm_g': out['hg_norm_g'], 'w_o_hg': out['w_o_hg'], 's5_a_re': out['s5_a_re'], 's5_a_im': out['s5_a_im'], 's5_log_dt': out['s5_log_dt'], 's5_b_re': out['s5_b_re'], 's5_b_im': out['s5_b_im'], 's5_c_re': out['s5_c_re'], 's5_c_im': out['s5_c_im'], 's5_d': out['s5_d'], 'w_glu': out['w_glu'], 'b_glu': out['b_glu'], 'w_o_s5': out['w_o_s5'], 'w_out': out['w_out'], 'ple_norm_g': out['ple_norm_g'], 'w_ple': out['w_ple'], 'w_ple_gate': out['w_ple_gate'], 'final_norm_g': out['final_norm_g'], 'loss_target': out['loss_target'], 'm_norm_g': out['m_norm_g'], 'm_w_in': out['m_w_in'], 'm_hg_lb': out['m_hg_lb'], 'm_hg_norm_g': out['m_hg_norm_g'], 'm_w_o_hg': out['m_w_o_hg'], 'm_s5_a_re': out['m_s5_a_re'], 'm_s5_a_im': out['m_s5_a_im'], 'm_s5_log_dt': out['m_s5_log_dt'], 'm_s5_b_re': out['m_s5_b_re'], 'm_s5_b_im': out['m_s5_b_im'], 'm_s5_c_re': out['m_s5_c_re'], 'm_s5_c_im': out['m_s5_c_im'], 'm_s5_d': out['m_s5_d'], 'm_w_glu': out['m_w_glu'], 'm_b_glu': out['m_b_glu'], 'm_w_o_s5': out['m_w_o_s5'], 'm_w_out': out['m_w_out'], 'm_ple_norm_g': out['m_ple_norm_g'], 'm_w_ple': out['m_w_ple'], 'm_w_ple_gate': out['m_w_ple_gate'], 'm_final_norm_g': out['m_final_norm_g'], 'v_norm_g': out['v_norm_g'], 'v_w_in': out['v_w_in'], 'v_hg_lb': out['v_hg_lb'], 'v_hg_norm_g': out['v_hg_norm_g'], 'v_w_o_hg': out['v_w_o_hg'], 'v_s5_a_re': out['v_s5_a_re'], 'v_s5_a_im': out['v_s5_a_im'], 'v_s5_log_dt': out['v_s5_log_dt'], 'v_s5_b_re': out['v_s5_b_re'], 'v_s5_b_im': out['v_s5_b_im'], 'v_s5_c_re': out['v_s5_c_re'], 'v_s5_c_im': out['v_s5_c_im'], 'v_s5_d': out['v_s5_d'], 'v_w_glu': out['v_w_glu'], 'v_b_glu': out['v_b_glu'], 'v_w_o_s5': out['v_w_o_s5'], 'v_w_out': out['v_w_out'], 'v_ple_norm_g': out['v_ple_norm_g'], 'v_w_ple': out['v_w_ple'], 'v_w_ple_gate': out['v_w_ple_gate'], 'v_final_norm_g': out['v_final_norm_g']}


def _loss(weights, diff, rest, loss_target):
    with _jax.named_scope("forward"):
        args = {**rest, TWIN_DIFF_INPUT: diff, **{k: w.astype(_WEIGHT_DTYPES[k]) for k, w in weights.items()}}
        y = _forward(args)
    with _jax.named_scope("loss_head"):
        err = _jnp.square(y.astype(_jnp.float32) - loss_target)
        return 0.5 * _jnp.sum(_jnp.mean(err, axis=-1)) if err.ndim else 0.5 * err


def _adamw(w, g, m, v):
    m = ADAM_B1 * m + (1.0 - ADAM_B1) * g
    v = ADAM_B2 * v + (1.0 - ADAM_B2) * _jnp.square(g)
    m_hat = m / (1.0 - ADAM_B1 ** ADAM_STEP)
    v_hat = v / (1.0 - ADAM_B2 ** ADAM_STEP)
    delta = -ADAM_LR * (m_hat / (_jnp.sqrt(v_hat) + ADAM_EPS) + ADAM_WD * w)
    return delta, m, v


def reference(x, p, norm_g, w_in, hg_lb, hg_norm_g, w_o_hg, s5_a_re, s5_a_im, s5_log_dt, s5_b_re, s5_b_im, s5_c_re, s5_c_im, s5_d, w_glu, b_glu, w_o_s5, w_out, ple_norm_g, w_ple, w_ple_gate, final_norm_g, loss_target, m_norm_g, m_w_in, m_hg_lb, m_hg_norm_g, m_w_o_hg, m_s5_a_re, m_s5_a_im, m_s5_log_dt, m_s5_b_re, m_s5_b_im, m_s5_c_re, m_s5_c_im, m_s5_d, m_w_glu, m_b_glu, m_w_o_s5, m_w_out, m_ple_norm_g, m_w_ple, m_w_ple_gate, m_final_norm_g, v_norm_g, v_w_in, v_hg_lb, v_hg_norm_g, v_w_o_hg, v_s5_a_re, v_s5_a_im, v_s5_log_dt, v_s5_b_re, v_s5_b_im, v_s5_c_re, v_s5_c_im, v_s5_d, v_w_glu, v_b_glu, v_w_o_s5, v_w_out, v_ple_norm_g, v_w_ple, v_w_ple_gate, v_final_norm_g):
    given = dict(x=x, p=p, norm_g=norm_g, w_in=w_in, hg_lb=hg_lb, hg_norm_g=hg_norm_g, w_o_hg=w_o_hg, s5_a_re=s5_a_re, s5_a_im=s5_a_im, s5_log_dt=s5_log_dt, s5_b_re=s5_b_re, s5_b_im=s5_b_im, s5_c_re=s5_c_re, s5_c_im=s5_c_im, s5_d=s5_d, w_glu=w_glu, b_glu=b_glu, w_o_s5=w_o_s5, w_out=w_out, ple_norm_g=ple_norm_g, w_ple=w_ple, w_ple_gate=w_ple_gate, final_norm_g=final_norm_g, loss_target=loss_target, m_norm_g=m_norm_g, m_w_in=m_w_in, m_hg_lb=m_hg_lb, m_hg_norm_g=m_hg_norm_g, m_w_o_hg=m_w_o_hg, m_s5_a_re=m_s5_a_re, m_s5_a_im=m_s5_a_im, m_s5_log_dt=m_s5_log_dt, m_s5_b_re=m_s5_b_re, m_s5_b_im=m_s5_b_im, m_s5_c_re=m_s5_c_re, m_s5_c_im=m_s5_c_im, m_s5_d=m_s5_d, m_w_glu=m_w_glu, m_b_glu=m_b_glu, m_w_o_s5=m_w_o_s5, m_w_out=m_w_out, m_ple_norm_g=m_ple_norm_g, m_w_ple=m_w_ple, m_w_ple_gate=m_w_ple_gate, m_final_norm_g=m_final_norm_g, v_norm_g=v_norm_g, v_w_in=v_w_in, v_hg_lb=v_hg_lb, v_hg_norm_g=v_hg_norm_g, v_w_o_hg=v_w_o_hg, v_s5_a_re=v_s5_a_re, v_s5_a_im=v_s5_a_im, v_s5_log_dt=v_s5_log_dt, v_s5_b_re=v_s5_b_re, v_s5_b_im=v_s5_b_im, v_s5_c_re=v_s5_c_re, v_s5_c_im=v_s5_c_im, v_s5_d=v_s5_d, v_w_glu=v_w_glu, v_b_glu=v_b_glu, v_w_o_s5=v_w_o_s5, v_w_out=v_w_out, v_ple_norm_g=v_ple_norm_g, v_w_ple=v_w_ple, v_w_ple_gate=v_w_ple_gate, v_final_norm_g=v_final_norm_g)
    weights = {n: given[n] for n in TWIN_WEIGHTS}
    shared = {n: given[n] for n in SHARED_INPUTS}
    per_example = {n: given[n] for n in ['x', 'p']}
    grad_fn = _jax.value_and_grad(_loss, argnums=(0, 1))

    def one_microbatch(ex, loss_target):
        ex = dict(ex)
        diff = ex.pop(TWIN_DIFF_INPUT)
        return grad_fn(weights, diff, {**shared, **ex}, loss_target)

    if N_MICROBATCH == 1:
        loss, (grad_w, grad_x) = one_microbatch(per_example, given["loss_target"])
    else:
        def body(carry, xs):
            loss_sum, grad_sum = carry
            l_k, (gw_k, gx_k) = one_microbatch(xs[0], xs[1])
            with _jax.named_scope("update"):
                return (loss_sum + l_k, _jax.tree.map(_jnp.add, grad_sum, gw_k)), gx_k

        init = (_jnp.zeros((), _jnp.float32), _jax.tree.map(_jnp.zeros_like, weights))
        (loss, grad_w), grad_x = _jax.lax.scan(body, init, (per_example, given["loss_target"]))
    with _jax.named_scope("update"):
        delta_w, new_m, new_v = {}, {}, {}
        for n in TWIN_WEIGHTS:
            delta_w[n], new_m[n], new_v[n] = _adamw(weights[n], grad_w[n], given["m_" + n], given["v_" + n])
    return (loss, grad_x, *[grad_w[n] for n in TWIN_WEIGHTS], *[delta_w[n] for n in TWIN_WEIGHTS],
            *[new_m[n] for n in TWIN_WEIGHTS], *[new_v[n] for n in TWIN_WEIGHTS])
```

```python
import functools

import jax
import jax.numpy as jnp
from jax import lax
from jax.experimental import pallas as pl
from jax.experimental.pallas import tpu as pltpu

F32 = jnp.float32
MXU_DTYPE = jnp.bfloat16
WIRE_DTYPE = jnp.bfloat16
NORM_EPS = 1e-6
D_MODEL = 1024
HG_HEADS = 8
HG_DIM = 128
HG_CHUNK = 64
S5_WIDTH = 512
S5_GROUPS = 32
S5_GROUP = 16
S5_STATE = 64
S5_LANES = S5_GROUPS * S5_STATE
IN_COLS = 7168
SUBLANES = 8
VMEM_LIMIT = 56 * 1024 * 1024
HIGHEST = lax.Precision.HIGHEST
MESH = pl.DeviceIdType.MESH

ADAM_LR, ADAM_B1, ADAM_B2, ADAM_EPS, ADAM_WD, ADAM_STEP = 0.001, 0.9, 0.999, 1e-08, 0.01, 10

BIG = ("w_in", "w_o_hg", "w_glu", "w_o_s5", "w_out", "w_ple", "w_ple_gate")
BIG_SHAPE = {"w_in": (1024, 7168), "w_o_hg": (1024, 1024), "w_glu": (512, 1024), "w_o_s5": (512, 1024),
             "w_out": (1024, 1024), "w_ple": (256, 1024), "w_ple_gate": (1024, 1024)}
BIG_COL_SHARDED = ("w_in", "w_glu", "w_o_s5", "w_ple")
SMALL = ("norm_g", "hg_lb", "hg_norm_g", "s5_a_re", "s5_a_im", "s5_log_dt", "s5_b_re", "s5_b_im", "s5_c_re",
         "s5_c_im", "s5_d", "b_glu", "ple_norm_g", "final_norm_g")
SMALL_SHAPE = {"norm_g": (1, 1024), "hg_lb": (2, 1024), "hg_norm_g": (1, 1024), "s5_a_re": (1, 32, 64),
               "s5_a_im": (1, 32, 64), "s5_log_dt": (1, 32), "s5_b_re": (1, 32, 64, 16), "s5_b_im": (1, 32, 64, 16),
               "s5_c_re": (1, 32, 16, 64), "s5_c_im": (1, 32, 16, 64), "s5_d": (1, 32, 16), "b_glu": (1, 1024),
               "ple_norm_g": (1, 1024), "final_norm_g": (1024,)}
WEIGHTS = ("norm_g", "w_in", "hg_lb", "hg_norm_g", "w_o_hg", "s5_a_re", "s5_a_im", "s5_log_dt", "s5_b_re", "s5_b_im",
           "s5_c_re", "s5_c_im", "s5_d", "w_glu", "b_glu", "w_o_s5", "w_out", "ple_norm_g", "w_ple", "w_ple_gate",
           "final_norm_g")
N_CHIPS = 4
N_DEV = 8
PACK_W = 1024
SHARD_ROWS = sum(BIG_SHAPE[n][0] * BIG_SHAPE[n][1] for n in BIG) // (N_CHIPS * PACK_W)
HALF_ROWS = SHARD_ROWS // 2
SMALL_ROWS = 144


def _params(*sem):
    return pltpu.CompilerParams(dimension_semantics=sem, vmem_limit_bytes=VMEM_LIMIT)


def _sig(x):
    return 1.0 / (1.0 + jnp.exp(-x))


def _dsilu(z, s):
    return s * (1.0 + z * (1.0 - s))


def _mx(x):
    return x.astype(MXU_DTYPE)


def _dot(a, b, dims=(((1,), (0,)), ((), ()))):
    return lax.dot_general(_mx(a), _mx(b), dims, preferred_element_type=F32)


_NT = (((1,), (1,)), ((), ()))
_TN = (((0,), (0,)), ((), ()))


def _dot32(a, b):
    return jnp.dot(a, b, precision=HIGHEST, preferred_element_type=F32)


def _rms_bwd(dy, x, g):
    r = lax.rsqrt(jnp.mean(x * x, axis=-1, keepdims=True) + NORM_EPS)
    t = dy * g
    dx = r * t - x * (r * r * r) * jnp.mean(t * x, axis=-1, keepdims=True)
    return dx, jnp.sum(dy * x * r, axis=0, keepdims=True)


def _rowwise(name, fn, n_rows_total, tm, rows, consts, outs, accs=(), alias=None):
    n_r, n_c, n_o, n_a = len(rows), len(consts), len(outs), len(accs)

    def body(*refs):
        row_refs = refs[:n_r]
        const_refs = refs[n_r:n_r + n_c]
        pos = n_r + n_c + (1 if alias is not None else 0)
        out_refs = refs[pos:pos + n_o]
        acc_refs = refs[pos + n_o:pos + n_o + n_a]
        res = fn(*[r[...] for r in row_refs], *[r[...] for r in const_refs])
        for r, v in zip(out_refs, res[:n_o]):
            r[...] = v.astype(r.dtype)
        if n_a:
            @pl.when(pl.program_id(0) == 0)
            def _():
                for r in acc_refs:
                    r[...] = jnp.zeros_like(r)
            for r, v in zip(acc_refs, res[n_o:]):
                r[...] += v

    in_specs = [pl.BlockSpec((tm, w), functools.partial(lambda i, cb: (i, cb), cb=cb)) for (_, w, cb) in rows]
    in_specs += [pl.BlockSpec(c.shape, lambda i: (0, 0)) for c in consts]
    args = [a for (a, _, _) in rows] + list(consts)
    out_shape, out_specs = [], []
    for o in outs:
        w, dt = o[0], o[1]
        cb, total = (o[2], o[3]) if len(o) == 4 else (0, w)
        out_shape.append(jax.ShapeDtypeStruct((n_rows_total, total), dt))
        out_specs.append(pl.BlockSpec((tm, w), functools.partial(lambda i, cb: (i, cb), cb=cb)))
    io_alias = {}
    if alias is not None:
        in_specs.append(pl.BlockSpec(memory_space=pl.ANY))
        args.append(alias[0])
        io_alias = {len(args) - 1: alias[1]}
    for (r, w) in accs:
        out_shape.append(jax.ShapeDtypeStruct((r, w), F32))
        out_specs.append(pl.BlockSpec((r, w), lambda i: (0, 0)))
    res = pl.pallas_call(body, name=name, grid=(n_rows_total // tm,), in_specs=in_specs, out_specs=out_specs,
                         out_shape=out_shape, input_output_aliases=io_alias,
                         compiler_params=_params("arbitrary"))(*args)
    return res


def _mm_nn(name, a, b, tm, tn, bias=None, resid=None):
    m, k = a.shape
    n = b.shape[1]

    def body(*refs):
        acc = _dot(refs[0][...], refs[1][...])
        pos = 2
        if bias is not None:
            acc = acc + refs[pos][...]
            pos += 1
        if resid is not None:
            acc = acc + refs[pos][...]
            pos += 1
        refs[pos][...] = acc

    in_specs = [pl.BlockSpec((tm, k), lambda j, i: (i, 0)), pl.BlockSpec((k, tn), lambda j, i: (0, j))]
    args = [a, b]
    if bias is not None:
        in_specs.append(pl.BlockSpec((1, tn), lambda j, i: (0, j)))
        args.append(bias)
    if resid is not None:
        in_specs.append(pl.BlockSpec((tm, tn), lambda j, i: (i, j)))
        args.append(resid)
    return pl.pallas_call(body, name=name, grid=(n // tn, m // tm), in_specs=in_specs,
                          out_specs=pl.BlockSpec((tm, tn), lambda j, i: (i, j)),
                          out_shape=jax.ShapeDtypeStruct((m, n), F32),
                          compiler_params=_params("arbitrary", "arbitrary"))(*args)


def _mm_nt(name, a, b, tm, tn):
    m, n = a.shape
    k = b.shape[0]
    steps = n // tn

    def body(a_ref, b_ref, o_ref, acc_ref):
        s = pl.program_id(1)

        @pl.when(s == 0)
        def _():
            acc_ref[...] = jnp.zeros_like(acc_ref)

        acc_ref[...] += _dot(a_ref[...], b_ref[...], _NT)

        @pl.when(s == steps - 1)
        def _():
            o_ref[...] = acc_ref[...]

    return pl.pallas_call(body, name=name, grid=(m // tm, steps),
                          in_specs=[pl.BlockSpec((tm, tn), lambda i, s: (i, s)),
                                    pl.BlockSpec((k, tn), lambda i, s: (0, s))],
                          out_specs=pl.BlockSpec((tm, k), lambda i, s: (i, 0)),
                          out_shape=jax.ShapeDtypeStruct((m, k), F32),
                          scratch_shapes=[pltpu.VMEM((tm, k), F32)],
                          compiler_params=_params("arbitrary", "arbitrary"))(a, b)


def _mm_tn(name, a, b, tk, tn):
    t, k = a.shape
    n = b.shape[1]
    steps = t // tk

    def body(a_ref, b_ref, o_ref, acc_ref):
        s = pl.program_id(1)

        @pl.when(s == 0)
        def _():
            acc_ref[...] = jnp.zeros_like(acc_ref)

        acc_ref[...] += _dot(a_ref[...], b_ref[...], _TN)

        @pl.when(s == steps - 1)
        def _():
            o_ref[...] = acc_ref[...]

    return pl.pallas_call(body, name=name, grid=(n // tn, steps),
                          in_specs=[pl.BlockSpec((tk, k), lambda j, s: (s, 0)),
                                    pl.BlockSpec((tk, tn), lambda j, s: (s, j))],
                          out_specs=pl.BlockSpec((k, tn), lambda j, s: (0, j)),
                          out_shape=jax.ShapeDtypeStruct((k, n), F32),
                          scratch_shapes=[pltpu.VMEM((k, tn), F32)],
                          compiler_params=_params("arbitrary", "arbitrary"))(a, b)


def _hg_chunk_terms(q, f, lb):
    sig = _sig(f)
    fv = lb + (1.0 - lb) * sig
    kk = (1.0 - lb) * (1.0 - sig)
    row = lax.broadcasted_iota(jnp.int32, (HG_CHUNK, HG_CHUNK), 0)
    col = lax.broadcasted_iota(jnp.int32, (HG_CHUNK, HG_CHUNK), 1)
    b = _dot32((row >= col).astype(F32), jnp.log(fv))
    b_mid = b[HG_CHUNK // 2 - 1:HG_CHUNK // 2, :]
    b_last = b[HG_CHUNK - 1:HG_CHUNK, :]
    e_mid = jnp.exp(b - b_mid)
    e_mid_inv = jnp.exp(b_mid - b)
    e_b = jnp.exp(b)
    e_last = jnp.exp(b_last - b)
    return sig, fv, kk, row >= col, row <= col, q * e_mid, kk * e_mid_inv, e_mid, e_mid_inv, e_b, e_last, jnp.exp(b_last)


def _hgrn2_fwd(proj, hg_lb, hg_norm_g, t_len, tb):
    nck = tb // HG_CHUNK

    def body(p_ref, lb_ref, gn_ref, o_ref, act_ref, sp_ref, st_ref):
        @pl.when(pl.program_id(0) == 0)
        def _():
            st_ref[...] = jnp.zeros_like(st_ref)

        for h in range(HG_HEADS):
            hs = pl.ds(h * HG_DIM, HG_DIM)
            lb = _sig(lb_ref[0:1, hs] - lb_ref[1:2, hs])

            def chunk(c, carry, h=h, hs=hs, lb=lb):
                r = pl.ds(pl.multiple_of(c * HG_CHUNK, HG_CHUNK), HG_CHUNK)
                q = p_ref[r, pl.ds(h * HG_DIM, HG_DIM)]
                f = p_ref[r, pl.ds(1024 + h * HG_DIM, HG_DIM)]
                v = p_ref[r, pl.ds(2048 + h * HG_DIM, HG_DIM)]
                _, _, kk, causal, _, a, bm, _, _, e_b, e_last, dc = _hg_chunk_terms(q, f, lb)
                scores = jnp.where(causal, _dot(a, bm, _NT), 0.0)
                st = st_ref[h]
                o = _dot(scores, v) + _dot(q * e_b, st, _NT)
                sp_ref[h, c] = st
                st_ref[h] = dc * st + _dot(v, kk * e_last, _TN)
                o_ref[r, hs] = o
                return carry

            lax.fori_loop(0, nck, chunk, 0)

        for h in range(HG_HEADS):
            hs = pl.ds(h * HG_DIM, HG_DIM)
            o = o_ref[:, hs]
            rr = lax.rsqrt(jnp.mean(o * o, axis=-1, keepdims=True) + NORM_EPS)
            g = p_ref[:, pl.ds(3072 + h * HG_DIM, HG_DIM)]
            act_ref[:, hs] = (o * rr * gn_ref[:, hs] * (g * _sig(g))).astype(act_ref.dtype)

    nb = t_len // tb
    return pl.pallas_call(
        body, name="hgrn2_fwd", grid=(nb,),
        in_specs=[pl.BlockSpec((tb, 4096), lambda i: (i, 0)),
                  pl.BlockSpec((2, 1024), lambda i: (0, 0)),
                  pl.BlockSpec((1, 1024), lambda i: (0, 0))],
        out_specs=[pl.BlockSpec((tb, 1024), lambda i: (i, 0)),
                   pl.BlockSpec((tb, 1024), lambda i: (i, 0)),
                   pl.BlockSpec((HG_HEADS, nck, HG_DIM, HG_DIM), lambda i: (0, i, 0, 0))],
        out_shape=[jax.ShapeDtypeStruct((t_len, 1024), F32),
                   jax.ShapeDtypeStruct((t_len, 1024), MXU_DTYPE),
                   jax.ShapeDtypeStruct((HG_HEADS, t_len // HG_CHUNK, HG_DIM, HG_DIM), F32)],
        scratch_shapes=[pltpu.VMEM((HG_HEADS, HG_DIM, HG_DIM), F32)],
        compiler_params=_params("arbitrary"))(proj, hg_lb, hg_norm_g)


def _hgrn2_bwd(proj, d_o, s_prev, hg_lb, dproj, t_len, tb):
    nck = tb // HG_CHUNK
    nb = t_len // tb

    def body(p_ref, do_ref, sp_ref, lb_ref, _, dp_ref, dlb_ref, ds_ref, acc_ref):
        @pl.when(pl.program_id(0) == 0)
        def _():
            ds_ref[...] = jnp.zeros_like(ds_ref)
            acc_ref[...] = jnp.zeros_like(acc_ref)

        for h in range(HG_HEADS):
            hs = pl.ds(h * HG_DIM, HG_DIM)
            lb = _sig(lb_ref[0:1, hs] - lb_ref[1:2, hs])

            def chunk(j, carry, h=h, hs=hs, lb=lb):
                c = nck - 1 - j
                r = pl.ds(pl.multiple_of(c * HG_CHUNK, HG_CHUNK), HG_CHUNK)
                q = p_ref[r, pl.ds(h * HG_DIM, HG_DIM)]
                f = p_ref[r, pl.ds(1024 + h * HG_DIM, HG_DIM)]
                v = p_ref[r, pl.ds(2048 + h * HG_DIM, HG_DIM)]
                do = do_ref[r, hs]
                sig, fv, kk, causal, anti, a, bm, e_mid, e_mid_inv, e_b, e_last, dc = _hg_chunk_terms(q, f, lb)
                qd = q * e_b
                kd = kk * e_last
                st = sp_ref[h, c]
                dst = ds_ref[h]
                scores = jnp.where(causal, _dot(a, bm, _NT), 0.0)
                dscores = jnp.where(causal, _dot(do, v, _NT), 0.0)
                dv = _dot(scores, do, _TN) + _dot(kd, dst, _NT)
                da = _dot(dscores, bm)
                dbm = _dot(dscores, a, _TN)
                dqd = _dot(do, st)
                dkd = _dot(v, dst)
                ddc = jnp.sum(dst * st, axis=0, keepdims=True)
                ds_ref[h] = _dot(do, qd, _TN) + dc * dst
                dq = da * e_mid + dqd * e_b
                dk = dbm * e_mid_inv + dkd * e_last
                db = da * a - dbm * bm + dqd * qd - dkd * kd
                extra = jnp.sum(dkd * kd, axis=0, keepdims=True) + ddc * dc
                dlogf = _dot32(anti.astype(F32), db) + extra
                dfv_k = dlogf / fv - dk
                dp_ref[r, pl.ds(h * HG_DIM, HG_DIM)] = dq
                dp_ref[r, pl.ds(1024 + h * HG_DIM, HG_DIM)] = dfv_k * (1.0 - lb) * sig * (1.0 - sig)
                dp_ref[r, pl.ds(2048 + h * HG_DIM, HG_DIM)] = dv
                acc_ref[:, hs] += jnp.sum(dfv_k * (1.0 - sig), axis=0, keepdims=True)
                return carry

            lax.fori_loop(0, nck, chunk, 0)

        @pl.when(pl.program_id(0) == nb - 1)
        def _():
            lb_all = _sig(lb_ref[0:1, :] - lb_ref[1:2, :])
            g0 = acc_ref[...] * lb_all * (1.0 - lb_all)
            dlb_ref[0:1, :] = g0
            dlb_ref[1:2, :] = -g0

    return pl.pallas_call(
        body, name="hgrn2_bwd", grid=(nb,),
        in_specs=[pl.BlockSpec((tb, 3072), lambda i: (nb - 1 - i, 0)),
                  pl.BlockSpec((tb, 1024), lambda i: (nb - 1 - i, 0)),
                  pl.BlockSpec((HG_HEADS, nck, HG_DIM, HG_DIM), lambda i: (0, nb - 1 - i, 0, 0)),
                  pl.BlockSpec((2, 1024), lambda i: (0, 0)),
                  pl.BlockSpec(memory_space=pl.ANY)],
        out_specs=[pl.BlockSpec((tb, 3072), lambda i: (nb - 1 - i, 0)),
                   pl.BlockSpec((2, 1024), lambda i: (0, 0))],
        out_shape=[jax.ShapeDtypeStruct((t_len, IN_COLS), F32), jax.ShapeDtypeStruct((2, 1024), F32)],
        scratch_shapes=[pltpu.VMEM((HG_HEADS, HG_DIM, HG_DIM), F32), pltpu.VMEM((1, 1024), F32)],
        input_output_aliases={4: 0},
        compiler_params=_params("arbitrary"))(proj, d_o, s_prev, hg_lb, dproj)


def _s5_prep(a_re, a_im, log_dt, b_re_t, b_im_t):
    def body(ar_ref, ai_ref, ldt_ref, br_ref, bi_ref, lam_ref, pr_ref, pi_ref, bbr_ref, bbi_ref):
        ar, ai = ar_ref[...], ai_ref[...]
        dt = jnp.exp(ldt_ref[...])
        mag = jnp.exp(ar * dt)
        lr, li = mag * jnp.cos(ai * dt), mag * jnp.sin(ai * dt)
        den = ar * ar + ai * ai
        nr = lr - 1.0
        sr = (nr * ar + li * ai) / den
        si = (li * ar - nr * ai) / den
        lam_ref[0:1, :] = lr
        lam_ref[1:2, :] = li
        cr, ci = lr, li
        for i in range(SUBLANES):
            pr_ref[i:i + 1, :] = cr
            pi_ref[i:i + 1, :] = ci
            cr, ci = cr * lr - ci * li, cr * li + ci * lr
        br, bi = br_ref[...], bi_ref[...]
        bbr_ref[...] = sr * br - si * bi
        bbi_ref[...] = sr * bi + si * br

    whole = pl.BlockSpec(memory_space=pltpu.VMEM)
    return pl.pallas_call(
        body, name="s5_prep", in_specs=[whole] * 5, out_specs=[whole] * 5,
        out_shape=[jax.ShapeDtypeStruct((2, S5_LANES), F32), jax.ShapeDtypeStruct((SUBLANES, S5_LANES), F32),
                   jax.ShapeDtypeStruct((SUBLANES, S5_LANES), F32), jax.ShapeDtypeStruct((S5_GROUP, S5_LANES), F32),
                   jax.ShapeDtypeStruct((S5_GROUP, S5_LANES), F32)])(a_re, a_im, log_dt, b_re_t, b_im_t)


def _s5_prep_bwd(a_re, a_im, log_dt, b_re_t, b_im_t, dlam, dbbr, dbbi):
    def body(ar_ref, ai_ref, ldt_ref, br_ref, bi_ref, dlam_ref, dbbr_ref, dbbi_ref,
             dar_ref, dai_ref, dldt_ref, dbr_ref, dbi_ref):
        ar, ai = ar_ref[...], ai_ref[...]
        dt = jnp.exp(ldt_ref[...])
        mag = jnp.exp(ar * dt)
        cs, sn = jnp.cos(ai * dt), jnp.sin(ai * dt)
        lr, li = mag * cs, mag * sn
        den = ar * ar + ai * ai
        nr = lr - 1.0
        sr = (nr * ar + li * ai) / den
        si = (li * ar - nr * ai) / den
        br, bi = br_ref[...], bi_ref[...]
        gbr, gbi = dbbr_ref[...], dbbi_ref[...]
        dbr_ref[...] = sr * gbr + si * gbi
        dbi_ref[...] = sr * gbi - si * gbr
        dsr = jnp.sum(gbr * br + gbi * bi, axis=0, keepdims=True)
        dsi = jnp.sum(gbi * br - gbr * bi, axis=0, keepdims=True)
        dnr = (dsr * ar - dsi * ai) / den
        dli = dlam_ref[1:2, :] + (dsr * ai + dsi * ar) / den
        dlr = dlam_ref[0:1, :] + dnr
        dden = -(dsr * sr + dsi * si) / den
        dar = (dsr * nr + dsi * li) / den + dden * 2.0 * ar
        dai = (dsr * li - dsi * nr) / den + dden * 2.0 * ai
        dmag = dlr * cs + dli * sn
        dth = mag * (dli * cs - dlr * sn)
        dar_ref[...] = dar + dmag * mag * dt
        dai_ref[...] = dai + dth * dt
        ddt = (dmag * mag * ar + dth * ai) * dt
        lane = lax.broadcasted_iota(jnp.int32, (S5_LANES, 128), 0) // S5_STATE
        grp = lax.broadcasted_iota(jnp.int32, (S5_LANES, 128), 1)
        dldt_ref[...] = _dot32(jnp.broadcast_to(ddt, (SUBLANES, S5_LANES)), (lane == grp).astype(F32))

    whole = pl.BlockSpec(memory_space=pltpu.VMEM)
    return pl.pallas_call(
        body, name="s5_prep_bwd", in_specs=[whole] * 8, out_specs=[whole] * 5,
        out_shape=[jax.ShapeDtypeStruct((1, S5_LANES), F32), jax.ShapeDtypeStruct((1, S5_LANES), F32),
                   jax.ShapeDtypeStruct((SUBLANES, 128), F32), jax.ShapeDtypeStruct((S5_GROUP, S5_LANES), F32),
                   jax.ShapeDtypeStruct((S5_GROUP, S5_LANES), F32)])(a_re, a_im, log_dt, b_re_t, b_im_t, dlam, dbbr,
                                                                      dbbi)


S5_LANE_CHUNK = 512


def _shift_rows(x, s, rowid):
    if s > 0:
        return jnp.where(rowid >= s, pltpu.roll(x, s, 0), 0.0)
    return jnp.where(rowid < SUBLANES + s, pltpu.roll(x, SUBLANES + s, 0), 0.0)


def _scan8(xr, xi, pr, pi, sign, rowid):
    for s, row in ((1, 0), (2, 1), (4, 3)):
        lr, li = pr[row:row + 1, :], pi[row:row + 1, :]
        sr, si = _shift_rows(xr, sign * s, rowid), _shift_rows(xi, sign * s, rowid)
        xr, xi = xr + lr * sr - li * si, xi + lr * si + li * sr
    return xr, xi


def _s5_fwd(proj, pw_re, pw_im, bbr_bd, bbi_bd, crt_bd, cit_bd, d_row, t_len, tb):
    ngrp = tb // SUBLANES

    def body(u_ref, pr_ref, pi_ref, bbr_ref, bbi_ref, crt_ref, cit_ref, d_ref,
             hr_ref, hi_ref, ypre_ref, ys_ref, cr_ref, ci_ref):
        @pl.when(pl.program_id(0) == 0)
        def _():
            cr_ref[...] = jnp.zeros_like(cr_ref)
            ci_ref[...] = jnp.zeros_like(ci_ref)

        u = u_ref[...]
        hr_ref[...] = _dot(u, bbr_ref[...])
        hi_ref[...] = _dot(u, bbi_ref[...])
        rowid = lax.broadcasted_iota(jnp.int32, (SUBLANES, S5_LANE_CHUNK), 0)
        for lc in range(S5_LANES // S5_LANE_CHUNK):
            ls = pl.ds(lc * S5_LANE_CHUNK, S5_LANE_CHUNK)
            pr, pi = pr_ref[:, ls], pi_ref[:, ls]

            def group(g, carry, ls=ls, pr=pr, pi=pi):
                cr, ci = carry
                r = pl.ds(pl.multiple_of(g * SUBLANES, SUBLANES), SUBLANES)
                xr, xi = _scan8(hr_ref[r, ls], hi_ref[r, ls], pr, pi, 1, rowid)
                xr, xi = xr + pr * cr - pi * ci, xi + pr * ci + pi * cr
                hr_ref[r, ls] = xr
                hi_ref[r, ls] = xi
                return xr[SUBLANES - 1:SUBLANES, :], xi[SUBLANES - 1:SUBLANES, :]

            cr, ci = lax.fori_loop(0, ngrp, group, (cr_ref[:, ls], ci_ref[:, ls]))
            cr_ref[:, ls] = cr
            ci_ref[:, ls] = ci
        y = _dot(hr_ref[...], crt_ref[...]) - _dot(hi_ref[...], cit_ref[...]) + d_ref[...] * u
        ypre_ref[...] = y
        ys_ref[...] = jax.nn.gelu(y, approximate=True).astype(ys_ref.dtype)

    whole = pl.BlockSpec(memory_space=pltpu.VMEM)
    return pl.pallas_call(
        body, name="s5_fwd", grid=(t_len // tb,),
        in_specs=[pl.BlockSpec((tb, S5_WIDTH), lambda i: (i, 4096 // S5_WIDTH))] + [whole] * 7,
        out_specs=[pl.BlockSpec((tb, S5_LANES), lambda i: (i, 0)), pl.BlockSpec((tb, S5_LANES), lambda i: (i, 0)),
                   pl.BlockSpec((tb, S5_WIDTH), lambda i: (i, 0)), pl.BlockSpec((tb, S5_WIDTH), lambda i: (i, 0))],
        out_shape=[jax.ShapeDtypeStruct((t_len, S5_LANES), F32), jax.ShapeDtypeStruct((t_len, S5_LANES), F32),
                   jax.ShapeDtypeStruct((t_len, S5_WIDTH), F32), jax.ShapeDtypeStruct((t_len, S5_WIDTH), MXU_DTYPE)],
        scratch_shapes=[pltpu.VMEM((1, S5_LANES), F32), pltpu.VMEM((1, S5_LANES), F32)],
        compiler_params=_params("arbitrary"))(proj, pw_re, pw_im, bbr_bd, bbi_bd, crt_bd, cit_bd, d_row)


def _dgelu(x):
    c, a = 0.7978845608028654, 0.044715
    th = jnp.tanh(c * (x + a * x * x * x))
    return 0.5 * (1.0 + th) + 0.5 * x * (1.0 - th * th) * c * (1.0 + 3.0 * a * x * x)


def _s5_bwd(dgelu, y_pre, proj, h_re, h_im, pwr_re, pwr_im, bbr_bd, bbi_bd, cr_bd, ci_bd, d_row, dproj, t_len, tb):
    ngrp = tb // SUBLANES
    nb = t_len // tb

    def body(dg_ref, yp_ref, u_ref, hr_ref, hi_ref, pr_ref, pi_ref, bbr_ref, bbi_ref, cr_ref, ci_ref, d_ref, _,
             du_ref, dbbr_ref, dbbi_ref, dcr_ref, dci_ref, dd_ref, dlam_ref,
             gr_ref, gi_ref, car_ref, cai_ref, abr_ref, abi_ref, acr_ref, aci_ref, ad_ref, alr_ref, ali_ref, sem):
        @pl.when(pl.program_id(0) == 0)
        def _():
            for ref in (car_ref, cai_ref, abr_ref, abi_ref, acr_ref, aci_ref, ad_ref, alr_ref, ali_ref):
                ref[...] = jnp.zeros_like(ref)

        u = u_ref[...]
        dy = dg_ref[...] * _dgelu(yp_ref[...])
        gr_ref[...] = _dot(dy, cr_ref[...])
        gi_ref[...] = -_dot(dy, ci_ref[...])
        rowid = lax.broadcasted_iota(jnp.int32, (SUBLANES, S5_LANE_CHUNK), 0)
        for lc in range(S5_LANES // S5_LANE_CHUNK):
            ls = pl.ds(lc * S5_LANE_CHUNK, S5_LANE_CHUNK)
            pr, pi = pr_ref[:, ls], pi_ref[:, ls]
            fwd_rows_r = jnp.concatenate([pr[7:8], pr[6:7], pr[6:7], pr[4:5]], axis=0)
            fwd_rows_i = jnp.concatenate([pi[7:8], pi[6:7], pi[6:7], pi[4:5]], axis=0)

            def group(j, carry, ls=ls, pr=pr, pi=pi, fr=fwd_rows_r, fi=fwd_rows_i):
                cr, ci, slr, sli = carry
                g = ngrp - 1 - j
                r = pl.ds(pl.multiple_of(g * SUBLANES, SUBLANES), SUBLANES)
                xr, xi = _scan8(gr_ref[r, ls], gi_ref[r, ls], fr, fi, -1, rowid)
                xr, xi = xr + pr * cr - pi * ci, xi + pr * ci + pi * cr
                gr_ref[r, ls] = xr
                gi_ref[r, ls] = xi
                nr = jnp.where(rowid == SUBLANES - 1, cr, pltpu.roll(xr, SUBLANES - 1, 0))
                ni = jnp.where(rowid == SUBLANES - 1, ci, pltpu.roll(xi, SUBLANES - 1, 0))
                hr, hi = hr_ref[r, ls], hi_ref[r, ls]
                slr = slr + nr * hr + ni * hi
                sli = sli + ni * hr - nr * hi
                return xr[0:1, :], xi[0:1, :], slr, sli

            zero = jnp.zeros((SUBLANES, S5_LANE_CHUNK), F32)
            cr, ci, slr, sli = lax.fori_loop(0, ngrp, group, (car_ref[:, ls], cai_ref[:, ls], zero, zero))
            car_ref[:, ls] = cr
            cai_ref[:, ls] = ci
            alr_ref[:, ls] += jnp.sum(slr, axis=0, keepdims=True)
            ali_ref[:, ls] += jnp.sum(sli, axis=0, keepdims=True)
        gr, gi = gr_ref[...], gi_ref[...]
        du_ref[...] = _dot(gr, bbr_ref[...], _NT) + _dot(gi, bbi_ref[...], _NT) + d_ref[...] * dy
        abr_ref[...] += _dot(u, gr, _TN)
        abi_ref[...] += _dot(u, gi, _TN)
        acr_ref[...] += _dot(hr_ref[...], dy, _TN)
        aci_ref[...] -= _dot(hi_ref[...], dy, _TN)
        ad_ref[...] += jnp.sum(dy * u, axis=0, keepdims=True)

        @pl.when(pl.program_id(0) == nb - 1)
        def _():
            dd_ref[...] = ad_ref[...]
            dlam_ref[0:1, :] = alr_ref[...]
            dlam_ref[1:2, :] = ali_ref[...]
            copies = [pltpu.make_async_copy(s, d, sem.at[k]) for k, (s, d) in enumerate(
                ((abr_ref, dbbr_ref), (abi_ref, dbbi_ref), (acr_ref, dcr_ref), (aci_ref, dci_ref)))]
            for cp in copies:
                cp.start()
            for cp in copies:
                cp.wait()

    whole = pl.BlockSpec(memory_space=pltpu.VMEM)
    hbm = pl.BlockSpec(memory_space=pl.ANY)
    rev = lambda i: (nb - 1 - i, 0)
    return pl.pallas_call(
        body, name="s5_bwd", grid=(nb,),
        in_specs=[pl.BlockSpec((tb, S5_WIDTH), rev), pl.BlockSpec((tb, S5_WIDTH), rev),
                  pl.BlockSpec((tb, S5_WIDTH), lambda i: (nb - 1 - i, 4096 // S5_WIDTH)),
                  pl.BlockSpec((tb, S5_LANES), rev), pl.BlockSpec((tb, S5_LANES), rev)] + [whole] * 7 + [hbm],
        out_specs=[pl.BlockSpec((tb, S5_WIDTH), lambda i: (nb - 1 - i, 4096 // S5_WIDTH)), hbm, hbm, hbm, hbm,
                   pl.BlockSpec((1, S5_WIDTH), lambda i: (0, 0)), pl.BlockSpec((2, S5_LANES), lambda i: (0, 0))],
        out_shape=[jax.ShapeDtypeStruct((t_len, IN_COLS), F32),
                   jax.ShapeDtypeStruct((S5_WIDTH, S5_LANES), F32), jax.ShapeDtypeStruct((S5_WIDTH, S5_LANES), F32),
                   jax.ShapeDtypeStruct((S5_LANES, S5_WIDTH), F32), jax.ShapeDtypeStruct((S5_LANES, S5_WIDTH), F32),
                   jax.ShapeDtypeStruct((1, S5_WIDTH), F32), jax.ShapeDtypeStruct((2, S5_LANES), F32)],
        scratch_shapes=[pltpu.VMEM((tb, S5_LANES), F32), pltpu.VMEM((tb, S5_LANES), F32),
                        pltpu.VMEM((1, S5_LANES), F32), pltpu.VMEM((1, S5_LANES), F32),
                        pltpu.VMEM((S5_WIDTH, S5_LANES), F32), pltpu.VMEM((S5_WIDTH, S5_LANES), F32),
                        pltpu.VMEM((S5_LANES, S5_WIDTH), F32), pltpu.VMEM((S5_LANES, S5_WIDTH), F32),
                        pltpu.VMEM((1, S5_WIDTH), F32), pltpu.VMEM((1, S5_LANES), F32),
                        pltpu.VMEM((1, S5_LANES), F32), pltpu.SemaphoreType.DMA((4,))],
        input_output_aliases={12: 0},
        compiler_params=_params("arbitrary"))(dgelu, y_pre, proj, h_re, h_im, pwr_re, pwr_im, bbr_bd, bbi_bd, cr_bd,
                                              ci_bd, d_row, dproj)


def _block_diag(per_group):
    eye = jnp.eye(S5_GROUPS, dtype=bool)[:, None, :, None]
    dense = jnp.where(eye, per_group[:, :, None, :], 0.0)
    return dense.reshape(S5_WIDTH, S5_LANES)


def _diag_blocks(dense):
    ar = jnp.arange(S5_GROUPS)
    return dense.reshape(S5_GROUPS, S5_GROUP, S5_GROUPS, S5_STATE)[ar, :, ar, :]


def _local_step(x, p, target, w, sm):
    t_len = x.shape[0]
    tm = min(256, t_len)
    tmm = min(512, t_len)
    tb_hg = min(256, t_len)
    tb_s5 = min(128, t_len)
    g1, g2, g3, ghn = sm["norm_g"], sm["ple_norm_g"], sm["final_norm_g"].reshape(1, D_MODEL), sm["hg_norm_g"]

    def rms_f(xv, g):
        r = lax.rsqrt(jnp.mean(xv * xv, axis=-1, keepdims=True) + NORM_EPS)
        return (xv * r * g,)

    (u,) = _rowwise("rms_in", rms_f, t_len, tm, [(x, 1024, 0)], [g1], [(1024, MXU_DTYPE)])
    proj = _mm_nn("mm_in", u, w["w_in"], tmm, 1024)
    o_hg, act_hg, s_prev = _hgrn2_fwd(proj, sm["hg_lb"], ghn, t_len, tb_hg)
    y_hg = _mm_nn("mm_o_hg", act_hg, w["w_o_hg"], tmm, 1024)

    lanes = lambda a: a.reshape(1, S5_LANES)
    a_re, a_im = lanes(sm["s5_a_re"]), lanes(sm["s5_a_im"])
    ldt = lanes(jnp.broadcast_to(sm["s5_log_dt"].reshape(S5_GROUPS, 1), (S5_GROUPS, S5_STATE)))
    to_t = lambda b: b.reshape(S5_GROUPS, S5_STATE, S5_GROUP).transpose(2, 0, 1).reshape(S5_GROUP, S5_LANES)
    b_re_t, b_im_t = to_t(sm["s5_b_re"]), to_t(sm["s5_b_im"])
    lam, pw_re, pw_im, bbr_t, bbi_t = _s5_prep(a_re, a_im, ldt, b_re_t, b_im_t)
    from_t = lambda b: b.reshape(S5_GROUP, S5_GROUPS, S5_STATE).transpose(1, 0, 2)
    bbr_bd = _block_diag(from_t(bbr_t)).astype(MXU_DTYPE)
    bbi_bd = _block_diag(from_t(bbi_t)).astype(MXU_DTYPE)
    cr_bd = _block_diag(sm["s5_c_re"].reshape(S5_GROUPS, S5_GROUP, S5_STATE)).astype(MXU_DTYPE)
    ci_bd = _block_diag(sm["s5_c_im"].reshape(S5_GROUPS, S5_GROUP, S5_STATE)).astype(MXU_DTYPE)
    d_row = sm["s5_d"].reshape(1, S5_WIDTH)
    h_re, h_im, y_pre, ys_gelu = _s5_fwd(proj, pw_re, pw_im, bbr_bd, bbi_bd, cr_bd.T, ci_bd.T, d_row, t_len, tb_s5)
    glu = _mm_nn("mm_glu", ys_gelu, w["w_glu"], tmm, 1024, bias=sm["b_glu"])

    def glu_f(gl, z):
        a, b = gl[:, :S5_WIDTH], gl[:, S5_WIDTH:]
        return (a * _sig(b) * (z * _sig(z)),)

    (ys2,) = _rowwise("glu_gate", glu_f, t_len, tm, [(glu, 1024, 0), (proj, 512, 4608 // 512)], [],
                      [(512, MXU_DTYPE)])
    y_s5 = _mm_nn("mm_o_s5", ys2, w["w_o_s5"], tmm, 1024)

    def merge_f(yh, ys, gh, gs):
        return (_sig(gh) * yh + _sig(gs) * ys,)

    (merged,) = _rowwise("merge", merge_f, t_len, tm,
                         [(y_hg, 1024, 0), (y_s5, 1024, 0), (proj, 1024, 5), (proj, 1024, 6)], [],
                         [(1024, MXU_DTYPE)])
    h1 = _mm_nn("mm_out", merged, w["w_out"], tmm, 1024, resid=x)
    (n2,) = _rowwise("rms_ple", rms_f, t_len, tm, [(h1, 1024, 0)], [g2], [(1024, MXU_DTYPE)])
    gl = _mm_nn("mm_ple_gate", n2, w["w_ple_gate"], tmm, 1024)
    pe = _mm_nn("mm_ple", p, w["w_ple"], tmm, 1024)

    def head_f(h1v, pev, glv, tgt, g):
        gate = _sig(glv)
        h2 = h1v + pev * gate
        r = lax.rsqrt(jnp.mean(h2 * h2, axis=-1, keepdims=True) + NORM_EPS)
        e = h2 * r * g - tgt
        loss = 0.5 * jnp.sum(jnp.mean(e * e, axis=-1, keepdims=True), axis=0, keepdims=True)
        dy = e * (1.0 / D_MODEL)
        dg = jnp.sum(dy * h2 * r, axis=0, keepdims=True)
        t = dy * g
        dh2 = r * t - h2 * (r * r * r) * jnp.mean(t * h2, axis=-1, keepdims=True)
        return (dh2, dh2 * gate, dh2 * pev * gate * (1.0 - gate), jnp.broadcast_to(loss, (1, 128)), dg)

    dh2, dpe, dgl, loss_row, d_g3 = _rowwise(
        "loss_head", head_f, t_len, tm, [(h1, 1024, 0), (pe, 1024, 0), (gl, 1024, 0), (target, 1024, 0)], [g3],
        [(1024, F32), (1024, MXU_DTYPE), (1024, MXU_DTYPE)], accs=[(1, 128), (1, 1024)])

    gb = {}
    gb["w_ple"] = _mm_tn("mm_d_w_ple", p, dpe, tmm, 1024)
    gb["w_ple_gate"] = _mm_tn("mm_d_w_ple_gate", n2, dgl, tmm, 1024)
    dn2 = _mm_nt("mm_d_n2", dgl, w["w_ple_gate"], tmm, 1024)

    def ple_b(dn, h1v, dh, g):
        dx, dg = _rms_bwd(dn, h1v, g)
        return (dh + dx, dg)

    dh1, d_g2 = _rowwise("rms_ple_bwd", ple_b, t_len, tm, [(dn2, 1024, 0), (h1, 1024, 0), (dh2, 1024, 0)], [g2],
                         [(1024, F32)], accs=[(1, 1024)])
    gb["w_out"] = _mm_tn("mm_d_w_out", merged, dh1, tmm, 1024)
    dmerged = _mm_nt("mm_d_merged", dh1, w["w_out"], tmm, 1024)

    def gate_b(dm, y, gt):
        s = _sig(gt)
        return (dm * s, dm * y * s * (1.0 - s))

    dy_hg, dproj = _rowwise("gate_hg_bwd", gate_b, t_len, tm, [(dmerged, 1024, 0), (y_hg, 1024, 0), (proj, 1024, 5)],
                            [], [(1024, MXU_DTYPE), (1024, F32, 5, IN_COLS)])
    dy_s5, dproj = _rowwise("gate_s5_bwd", gate_b, t_len, tm, [(dmerged, 1024, 0), (y_s5, 1024, 0), (proj, 1024, 6)],
                            [], [(1024, MXU_DTYPE), (1024, F32, 6, IN_COLS)], alias=(dproj, 1))
    gb["w_o_s5"] = _mm_tn("mm_d_w_o_s5", ys2, dy_s5, tmm, 1024)
    dys2 = _mm_nt("mm_d_ys2", dy_s5, w["w_o_s5"], tmm, 1024)

    def glu_b(dys, gl_, z):
        a, b = gl_[:, :S5_WIDTH], gl_[:, S5_WIDTH:]
        sb, sz = _sig(b), _sig(z)
        silu = z * sz
        dglu = jnp.concatenate([dys * sb * silu, dys * a * silu * sb * (1.0 - sb)], axis=1)
        return (dglu, dys * a * sb * _dsilu(z, sz), jnp.sum(dglu, axis=0, keepdims=True))

    dglu, dproj, d_bglu = _rowwise("glu_bwd", glu_b, t_len, tm,
                                   [(dys2, 512, 0), (glu, 1024, 0), (proj, 512, 4608 // 512)], [],
                                   [(1024, MXU_DTYPE), (512, F32, 4608 // 512, IN_COLS)], accs=[(1, 1024)],
                                   alias=(dproj, 1))
    gb["w_glu"] = _mm_tn("mm_d_w_glu", ys_gelu, dglu, tmm, 1024)
    dgelu = _mm_nt("mm_d_gelu", dglu, w["w_glu"], tmm, 1024)
    pwr_re, pwr_im = pw_re[::-1], -pw_im[::-1]
    dproj, d_bbr, d_bbi, d_crt, d_cit, d_d, d_lam = _s5_bwd(dgelu, y_pre, proj, h_re, h_im, pwr_re, pwr_im, bbr_bd,
                                                           bbi_bd, cr_bd, ci_bd, d_row, dproj, t_len, tb_s5)
    to_t3 = lambda b: b.transpose(1, 0, 2).reshape(S5_GROUP, S5_LANES)
    d_are, d_aim, d_ldt, d_br_t, d_bi_t = _s5_prep_bwd(a_re, a_im, ldt, b_re_t, b_im_t, d_lam,
                                                       to_t3(_diag_blocks(d_bbr)), to_t3(_diag_blocks(d_bbi)))
    gb["w_o_hg"] = _mm_tn("mm_d_w_o_hg", act_hg, dy_hg, tmm, 1024)
    dact = _mm_nt("mm_d_act_hg", dy_hg, w["w_o_hg"], tmm, 1024)

    def hg_gate_b(da, o, g, gn):
        dos, dgs, dgns = [], [], []
        for h in range(HG_HEADS):
            sl = slice(h * HG_DIM, (h + 1) * HG_DIM)
            oh, gh, dah, gnh = o[:, sl], g[:, sl], da[:, sl], gn[:, sl]
            rr = lax.rsqrt(jnp.mean(oh * oh, axis=-1, keepdims=True) + NORM_EPS)
            sg = _sig(gh)
            dgs.append(dah * (oh * rr * gnh) * _dsilu(gh, sg))
            don = dah * (gh * sg)
            t = don * gnh
            dos.append(rr * t - oh * (rr * rr * rr) * jnp.mean(t * oh, axis=-1, keepdims=True))
            dgns.append(jnp.sum(don * oh * rr, axis=0, keepdims=True))
        return (jnp.concatenate(dos, axis=1), jnp.concatenate(dgs, axis=1), jnp.concatenate(dgns, axis=1))

    d_o, dproj, d_ghn = _rowwise("hg_gate_bwd", hg_gate_b, t_len, tm,
                                 [(dact, 1024, 0), (o_hg, 1024, 0), (proj, 1024, 3)], [ghn],
                                 [(1024, F32), (1024, F32, 3, IN_COLS)], accs=[(1, 1024)], alias=(dproj, 1))
    dproj, d_lb = _hgrn2_bwd(proj, d_o, s_prev, sm["hg_lb"], dproj, t_len, tb_hg)
    gb["w_in"] = _mm_tn("mm_d_w_in", u, dproj, tmm, 1024)
    du = _mm_nt("mm_d_u", dproj, w["w_in"], tmm, 1024)

    def in_b(duv, xv, dh, g):
        dx, dg = _rms_bwd(duv, xv, g)
        return (dh + dx, dg)

    grad_x, d_g1 = _rowwise("rms_in_bwd", in_b, t_len, tm, [(du, 1024, 0), (x, 1024, 0), (dh1, 1024, 0)], [g1],
                            [(1024, F32)], accs=[(1, 1024)])

    back_t = lambda b: b.reshape(S5_GROUP, S5_GROUPS, S5_STATE).transpose(1, 2, 0).reshape(1, S5_GROUPS, S5_STATE,
                                                                                           S5_GROUP)
    gs = {
        "norm_g": d_g1, "hg_lb": d_lb, "hg_norm_g": d_ghn,
        "s5_a_re": d_are.reshape(1, S5_GROUPS, S5_STATE), "s5_a_im": d_aim.reshape(1, S5_GROUPS, S5_STATE),
        "s5_log_dt": d_ldt[0:1, :S5_GROUPS],
        "s5_b_re": back_t(d_br_t), "s5_b_im": back_t(d_bi_t),
        "s5_c_re": _diag_blocks(d_crt.T).reshape(1, S5_GROUPS, S5_GROUP, S5_STATE),
        "s5_c_im": _diag_blocks(d_cit.T).reshape(1, S5_GROUPS, S5_GROUP, S5_STATE),
        "s5_d": d_d.reshape(1, S5_GROUPS, S5_GROUP), "b_glu": d_bglu, "ple_norm_g": d_g2,
        "final_norm_g": d_g3.reshape(D_MODEL),
    }
    return loss_row, grad_x, gb, gs


def _shard_shape(name):
    r, c = BIG_SHAPE[name]
    return (r, c // N_CHIPS) if name in BIG_COL_SHARDED else (r // N_CHIPS, c)


def _pack_shard(parts):
    return jnp.concatenate([parts[n].reshape(-1, PACK_W) for n in BIG], axis=0)


def _unpack_shard(packed):
    out, off = {}, 0
    for n in BIG:
        r, c = _shard_shape(n)
        rows = r * c // PACK_W
        out[n] = packed[off:off + rows].reshape(1, r, c)
        off += rows
    return out


def _unpack_full(gathered):
    out, off = {}, 0
    for n in BIG:
        r, c = _shard_shape(n)
        rows = r * c // PACK_W
        sh = gathered[:, off:off + rows].reshape(N_CHIPS, r, c)
        out[n] = sh.transpose(1, 0, 2).reshape(BIG_SHAPE[n]) if n in BIG_COL_SHARDED else sh.reshape(BIG_SHAPE[n])
        off += rows
    return out


def _pack_full(full):
    parts = []
    for n in BIG:
        r, c = _shard_shape(n)
        g = full[n]
        sh = g.reshape(BIG_SHAPE[n][0], N_CHIPS, c).transpose(1, 0, 2) if n in BIG_COL_SHARDED else g
        parts.append(sh.reshape(N_CHIPS, r * c // PACK_W, PACK_W))
    packed = jnp.concatenate(parts, axis=1)
    return packed.reshape(N_CHIPS, 2, HALF_ROWS, PACK_W).transpose(1, 0, 2, 3)


def _pack_small(parts, last):
    flat = jnp.concatenate([parts[n].reshape(-1) for n in SMALL] + [last.reshape(-1)])
    return jnp.pad(flat, (0, SMALL_ROWS * PACK_W - flat.shape[0])).reshape(SMALL_ROWS, PACK_W)


def _unpack_small(packed):
    flat, out, off = packed.reshape(-1), {}, 0
    for n in SMALL:
        size = 1
        for d in SMALL_SHAPE[n]:
            size *= d
        out[n] = flat[off:off + size].reshape(SMALL_SHAPE[n])
        off += size
    return out, flat[off]


def _place():
    x, y, c = lax.axis_index("x"), lax.axis_index("y"), lax.axis_index("c")
    return x, y, c, [(1 - x, y), (x, 1 - y), (1 - x, 1 - y)]


def _remote(src, dst, send_sems, recv_sems, k, to):
    return pltpu.make_async_remote_copy(src_ref=src, dst_ref=dst, send_sem=send_sems.at[k], recv_sem=recv_sems.at[k],
                                        device_id=to, device_id_type=MESH)


_HBM = pl.BlockSpec(memory_space=pl.ANY)


def _all_gather_weights(wp):
    def body(wp_ref, out_ref, send_sems, recv_sems, local_sem):
        x, y, c, chips = _place()
        k = 2 * x + y
        sibling = (x, y, 1 - c)
        mine = pltpu.make_async_copy(wp_ref, out_ref.at[k], local_sem)
        mine.start()
        first = [_remote(wp_ref.at[c], out_ref.at[k, c], send_sems, recv_sems, j, (cx, cy, c))
                 for j, (cx, cy) in enumerate(chips)]
        for cp in first:
            cp.start()
        passed = []
        for j, (cx, cy) in enumerate(chips):
            kj = 2 * cx + cy
            _remote(wp_ref.at[c], out_ref.at[kj, c], send_sems, recv_sems, j, (cx, cy, c)).wait_recv()
            cp = _remote(out_ref.at[kj, c], out_ref.at[kj, c], send_sems, recv_sems, 3 + j, sibling)
            cp.start()
            passed.append(cp)
        for j, (cx, cy) in enumerate(chips):
            kj = 2 * cx + cy
            _remote(wp_ref.at[c], out_ref.at[kj, 1 - c], send_sems, recv_sems, 3 + j, sibling).wait_recv()
        for cp in first + passed:
            cp.wait_send()
        mine.wait()

    return pl.pallas_call(
        body, name="all_gather_weights", in_specs=[_HBM], out_specs=_HBM,
        out_shape=jax.ShapeDtypeStruct((N_CHIPS, 2, HALF_ROWS, PACK_W), wp.dtype),
        scratch_shapes=[pltpu.SemaphoreType.DMA((6,)), pltpu.SemaphoreType.DMA((6,)), pltpu.SemaphoreType.DMA])(wp)


def _exchange_halves(pg):
    def body(pg_ref, out_ref, send_sems, recv_sems):
        x, y, c, _ = _place()
        cp = _remote(pg_ref.at[1 - c], out_ref, send_sems, recv_sems, 0, (x, y, 1 - c))
        cp.start()
        cp.wait()

    return pl.pallas_call(
        body, name="exchange_halves", in_specs=[_HBM], out_specs=_HBM,
        out_shape=jax.ShapeDtypeStruct((N_CHIPS, HALF_ROWS, PACK_W), pg.dtype),
        scratch_shapes=[pltpu.SemaphoreType.DMA((1,)), pltpu.SemaphoreType.DMA((1,))])(pg)


def _scatter_chip_sums(ps):
    def body(ps_ref, out_ref, send_sems, recv_sems):
        x, y, c, chips = _place()
        cps = [_remote(ps_ref.at[2 * cx + cy], out_ref.at[j], send_sems, recv_sems, j, (cx, cy, c))
               for j, (cx, cy) in enumerate(chips)]
        for cp in cps:
            cp.start()
        for cp in cps:
            cp.wait()

    return pl.pallas_call(
        body, name="scatter_chip_sums", in_specs=[_HBM], out_specs=_HBM,
        out_shape=jax.ShapeDtypeStruct((3, HALF_ROWS, PACK_W), ps.dtype),
        scratch_shapes=[pltpu.SemaphoreType.DMA((3,)), pltpu.SemaphoreType.DMA((3,))])(ps)


def _share_half(g_half):
    def body(g_ref, out_ref, send_sems, recv_sems, local_sem):
        x, y, c, _ = _place()
        mine = pltpu.make_async_copy(g_ref, out_ref.at[c], local_sem)
        mine.start()
        cp = _remote(g_ref, out_ref.at[c], send_sems, recv_sems, 0, (x, y, 1 - c))
        cp.start()
        _remote(g_ref, out_ref.at[1 - c], send_sems, recv_sems, 0, (x, y, 1 - c)).wait_recv()
        cp.wait_send()
        mine.wait()

    return pl.pallas_call(
        body, name="share_half", in_specs=[_HBM], out_specs=_HBM,
        out_shape=jax.ShapeDtypeStruct((2, HALF_ROWS, PACK_W), g_half.dtype),
        scratch_shapes=[pltpu.SemaphoreType.DMA((1,)), pltpu.SemaphoreType.DMA((1,)), pltpu.SemaphoreType.DMA])(g_half)


REDUCE_ROWS = 480


def _sum_pair(pg, theirs, c):
    def body(c_ref, a_ref, b_ref, o_ref):
        o_ref[...] = (a_ref[...] + b_ref[...]).astype(o_ref.dtype)

    return pl.pallas_call(
        body, name="sum_pair",
        grid_spec=pltpu.PrefetchScalarGridSpec(
            num_scalar_prefetch=1, grid=(N_CHIPS, HALF_ROWS // REDUCE_ROWS),
            in_specs=[pl.BlockSpec((None, None, REDUCE_ROWS, PACK_W), lambda j, i, c_ref: (c_ref[0], j, i, 0)),
                      pl.BlockSpec((None, REDUCE_ROWS, PACK_W), lambda j, i, c_ref: (j, i, 0))],
            out_specs=pl.BlockSpec((None, REDUCE_ROWS, PACK_W), lambda j, i, c_ref: (j, i, 0))),
        out_shape=jax.ShapeDtypeStruct((N_CHIPS, HALF_ROWS, PACK_W), WIRE_DTYPE),
        compiler_params=_params("arbitrary", "arbitrary"))(c.reshape(1), pg, theirs)


def _sum_chips(ps, others, k):
    def body(k_ref, a_ref, b_ref, o_ref):
        o_ref[...] = ((a_ref[...].astype(F32) + b_ref[0].astype(F32)) + b_ref[1].astype(F32)) + b_ref[2].astype(F32)

    return pl.pallas_call(
        body, name="sum_chips",
        grid_spec=pltpu.PrefetchScalarGridSpec(
            num_scalar_prefetch=1, grid=(HALF_ROWS // REDUCE_ROWS,),
            in_specs=[pl.BlockSpec((None, REDUCE_ROWS, PACK_W), lambda i, k_ref: (k_ref[0], i, 0)),
                      pl.BlockSpec((3, REDUCE_ROWS, PACK_W), lambda i, k_ref: (0, i, 0))],
            out_specs=pl.BlockSpec((REDUCE_ROWS, PACK_W), lambda i, k_ref: (i, 0))),
        out_shape=jax.ShapeDtypeStruct((HALF_ROWS, PACK_W), F32),
        compiler_params=_params("arbitrary"))(k.reshape(1), ps, others)


def _adamw(w, g, m, v):
    m = ADAM_B1 * m + (1.0 - ADAM_B1) * g
    v = ADAM_B2 * v + (1.0 - ADAM_B2) * (g * g)
    m_hat = m / (1.0 - ADAM_B1 ** ADAM_STEP)
    v_hat = v / (1.0 - ADAM_B2 ** ADAM_STEP)
    return -ADAM_LR * (m_hat / (jnp.sqrt(v_hat) + ADAM_EPS) + ADAM_WD * w), m, v


def _small_reduce_adamw(part, w, m, v):
    def body(part_ref, w_ref, m_ref, v_ref, g_ref, d_ref, nm_ref, nv_ref, all_ref, send_sems, recv_sems):
        x, y, c, chips = _place()
        me, sibling = (x, y, c), (x, y, 1 - c)

        def rows(px, py, pc):
            return all_ref.at[4 * px + 2 * py + pc]

        all_ref[4 * x + 2 * y + c] = part_ref[...]
        first = [_remote(part_ref, rows(*me), send_sems, recv_sems, 0, sibling)]
        first += [_remote(part_ref, rows(*me), send_sems, recv_sems, 1 + j, (cx, cy, c))
                  for j, (cx, cy) in enumerate(chips)]
        for cp in first:
            cp.start()
        passed = []
        for j, (cx, cy) in enumerate(chips):
            _remote(part_ref, rows(cx, cy, c), send_sems, recv_sems, 1 + j, me).wait_recv()
            cp = _remote(rows(cx, cy, c), rows(cx, cy, c), send_sems, recv_sems, 4 + j, sibling)
            cp.start()
            passed.append(cp)
        _remote(part_ref, rows(*sibling), send_sems, recv_sems, 0, me).wait_recv()
        for j, (cx, cy) in enumerate(chips):
            _remote(part_ref, rows(cx, cy, 1 - c), send_sems, recv_sems, 4 + j, me).wait_recv()
        for cp in first + passed:
            cp.wait_send()
        g = all_ref[0]
        for dev in range(1, N_DEV):
            g = g + all_ref[dev]
        delta, nm, nv = _adamw(w_ref[...], g, m_ref[...], v_ref[...])
        g_ref[...] = g
        d_ref[...] = delta
        nm_ref[...] = nm
        nv_ref[...] = nv

    whole = pl.BlockSpec(memory_space=pltpu.VMEM)
    shape = jax.ShapeDtypeStruct((SMALL_ROWS, PACK_W), F32)
    return pl.pallas_call(
        body, name="small_reduce_adamw", in_specs=[whole] * 4, out_specs=[whole] * 4, out_shape=[shape] * 4,
        scratch_shapes=[pltpu.VMEM((N_DEV, SMALL_ROWS, PACK_W), F32), pltpu.SemaphoreType.DMA((7,)),
                        pltpu.SemaphoreType.DMA((7,))],
        compiler_params=pltpu.CompilerParams(vmem_limit_bytes=VMEM_LIMIT))(part, w, m, v)


def kernel(x, p, norm_g, w_in, hg_lb, hg_norm_g, w_o_hg, s5_a_re, s5_a_im, s5_log_dt, s5_b_re, s5_b_im, s5_c_re, s5_c_im, s5_d, w_glu, b_glu, w_o_s5, w_out, ple_norm_g, w_ple, w_ple_gate, final_norm_g, loss_target, m_norm_g, m_w_in, m_hg_lb, m_hg_norm_g, m_w_o_hg, m_s5_a_re, m_s5_a_im, m_s5_log_dt, m_s5_b_re, m_s5_b_im, m_s5_c_re, m_s5_c_im, m_s5_d, m_w_glu, m_b_glu, m_w_o_s5, m_w_out, m_ple_norm_g, m_w_ple, m_w_ple_gate, m_final_norm_g, v_norm_g, v_w_in, v_hg_lb, v_hg_norm_g, v_w_o_hg, v_s5_a_re, v_s5_a_im, v_s5_log_dt, v_s5_b_re, v_s5_b_im, v_s5_c_re, v_s5_c_im, v_s5_d, v_w_glu, v_b_glu, v_w_o_s5, v_w_out, v_ple_norm_g, v_w_ple, v_w_ple_gate, v_final_norm_g):
    given = dict(locals())
    wts = {n: given[n] for n in WEIGHTS}
    mom = {n: given["m_" + n] for n in WEIGHTS}
    var = {n: given["v_" + n] for n in WEIGHTS}
    cx, cy, cc = lax.axis_index("x"), lax.axis_index("y"), lax.axis_index("c")
    chip = (2 * cx + cy).astype(jnp.int32)

    w_shard = _pack_shard({n: wts[n][0] for n in BIG})
    gathered = _all_gather_weights(w_shard.astype(MXU_DTYPE).reshape(2, HALF_ROWS, PACK_W))
    w_full = _unpack_full(gathered.reshape(N_CHIPS, SHARD_ROWS, PACK_W))

    t_len = x.shape[1]
    loss_row, grad_x, g_big, g_small = _local_step(x.reshape(t_len, D_MODEL), p.reshape(t_len, -1),
                                                   loss_target.reshape(t_len, D_MODEL), w_full,
                                                   {n: wts[n] for n in SMALL})

    zero = jnp.zeros((), F32)
    sg, sd, snm, snv = _small_reduce_adamw(_pack_small(g_small, loss_row[0, 0]),
                                           _pack_small({n: wts[n] for n in SMALL}, zero),
                                           _pack_small({n: mom[n] for n in SMALL}, zero),
                                           _pack_small({n: var[n] for n in SMALL}, zero))
    (sg, loss), (sd, _), (snm, _), (snv, _) = (_unpack_small(a) for a in (sg, sd, snm, snv))

    pg = _pack_full(g_big)
    ps = _sum_pair(pg, _exchange_halves(pg), cc.astype(jnp.int32))
    g_half = _sum_chips(ps, _scatter_chip_sums(ps), chip)
    g_shard = _share_half(g_half).reshape(SHARD_ROWS, PACK_W)

    def adam_f(wv, gv, mv, vv):
        return _adamw(wv, gv, mv, vv)

    bd, bnm, bnv = _rowwise("adamw_big", adam_f, SHARD_ROWS, REDUCE_ROWS,
                            [(w_shard, PACK_W, 0), (g_shard, PACK_W, 0),
                             (_pack_shard({n: mom[n][0] for n in BIG}), PACK_W, 0),
                             (_pack_shard({n: var[n][0] for n in BIG}), PACK_W, 0)], [],
                            [(PACK_W, F32), (PACK_W, F32), (PACK_W, F32)])
    bg, bd, bnm, bnv = (_unpack_shard(a) for a in (g_shard, bd, bnm, bnv))

    outs = [loss, grad_x.reshape(x.shape)]
    for small, big in ((sg, bg), (sd, bd), (snm, bnm), (snv, bnv)):
        outs += [big[n] if n in BIG else small[n] for n in WEIGHTS]
    return tuple(outs)
```

```python
import functools

import jax
import jax.numpy as jnp
from jax import lax
from jax.experimental import pallas as pl
from jax.experimental.pallas import tpu as pltpu

F32 = jnp.float32
MXU_DTYPE = jnp.bfloat16
WIRE_DTYPE = jnp.bfloat16
NORM_EPS = 1e-6
D_MODEL = 1024
HG_HEADS = 8
HG_DIM = 128
HG_CHUNK = 64
S5_WIDTH = 512
S5_GROUPS = 32
S5_GROUP = 16
S5_STATE = 64
S5_LANES = S5_GROUPS * S5_STATE
IN_COLS = 7168
SUBLANES = 8
VMEM_LIMIT = 56 * 1024 * 1024
HIGHEST = lax.Precision.HIGHEST
MESH = pl.DeviceIdType.MESH

ADAM_LR, ADAM_B1, ADAM_B2, ADAM_EPS, ADAM_WD, ADAM_STEP = 0.001, 0.9, 0.999, 1e-08, 0.01, 10

BIG = ("w_in", "w_o_hg", "w_glu", "w_o_s5", "w_out", "w_ple", "w_ple_gate")
BIG_SHAPE = {"w_in": (1024, 7168), "w_o_hg": (1024, 1024), "w_glu": (512, 1024), "w_o_s5": (512, 1024),
             "w_out": (1024, 1024), "w_ple": (256, 1024), "w_ple_gate": (1024, 1024)}
BIG_COL_SHARDED = ("w_in", "w_glu", "w_o_s5", "w_ple")
SMALL = ("norm_g", "hg_lb", "hg_norm_g", "s5_a_re", "s5_a_im", "s5_log_dt", "s5_b_re", "s5_b_im", "s5_c_re",
         "s5_c_im", "s5_d", "b_glu", "ple_norm_g", "final_norm_g")
SMALL_SHAPE = {"norm_g": (1, 1024), "hg_lb": (2, 1024), "hg_norm_g": (1, 1024), "s5_a_re": (1, 32, 64),
               "s5_a_im": (1, 32, 64), "s5_log_dt": (1, 32), "s5_b_re": (1, 32, 64, 16), "s5_b_im": (1, 32, 64, 16),
               "s5_c_re": (1, 32, 16, 64), "s5_c_im": (1, 32, 16, 64), "s5_d": (1, 32, 16), "b_glu": (1, 1024),
               "ple_norm_g": (1, 1024), "final_norm_g": (1024,)}
WEIGHTS = ("norm_g", "w_in", "hg_lb", "hg_norm_g", "w_o_hg", "s5_a_re", "s5_a_im", "s5_log_dt", "s5_b_re", "s5_b_im",
           "s5_c_re", "s5_c_im", "s5_d", "w_glu", "b_glu", "w_o_s5", "w_out", "ple_norm_g", "w_ple", "w_ple_gate",
           "final_norm_g")
N_CHIPS = 4
N_DEV = 8
PACK_W = 1024
SHARD_ROWS = sum(BIG_SHAPE[n][0] * BIG_SHAPE[n][1] for n in BIG) // (N_CHIPS * PACK_W)
HALF_ROWS = SHARD_ROWS // 2
SMALL_ROWS = 144


def _params(*sem):
    return pltpu.CompilerParams(dimension_semantics=sem, vmem_limit_bytes=VMEM_LIMIT)


def _sig(x):
    return 1.0 / (1.0 + jnp.exp(-x))


def _dsilu(z, s):
    return s * (1.0 + z * (1.0 - s))


def _mx(x):
    return x.astype(MXU_DTYPE)


def _dot(a, b, dims=(((1,), (0,)), ((), ()))):
    return lax.dot_general(_mx(a), _mx(b), dims, preferred_element_type=F32)


_NT = (((1,), (1,)), ((), ()))
_TN = (((0,), (0,)), ((), ()))


def _dot32(a, b):
    return jnp.dot(a, b, precision=HIGHEST, preferred_element_type=F32)


def _rms_bwd(dy, x, g):
    r = lax.rsqrt(jnp.mean(x * x, axis=-1, keepdims=True) + NORM_EPS)
    t = dy * g
    dx = r * t - x * (r * r * r) * jnp.mean(t * x, axis=-1, keepdims=True)
    return dx, jnp.sum(dy * x * r, axis=0, keepdims=True)


def _rowwise(name, fn, n_rows_total, tm, rows, consts, outs, accs=(), alias=None):
    n_r, n_c, n_o, n_a = len(rows), len(consts), len(outs), len(accs)

    def body(*refs):
        row_refs = refs[:n_r]
        const_refs = refs[n_r:n_r + n_c]
        pos = n_r + n_c + (1 if alias is not None else 0)
        out_refs = refs[pos:pos + n_o]
        acc_refs = refs[pos + n_o:pos + n_o + n_a]
        res = fn(*[r[...] for r in row_refs], *[r[...] for r in const_refs])
        for r, v in zip(out_refs, res[:n_o]):
            r[...] = v.astype(r.dtype)
        if n_a:
            @pl.when(pl.program_id(0) == 0)
            def _():
                for r in acc_refs:
                    r[...] = jnp.zeros_like(r)
            for r, v in zip(acc_refs, res[n_o:]):
                r[...] += v

    in_specs = [pl.BlockSpec((tm, w), functools.partial(lambda i, cb: (i, cb), cb=cb)) for (_, w, cb) in rows]
    in_specs += [pl.BlockSpec(c.shape, lambda i: (0, 0)) for c in consts]
    args = [a for (a, _, _) in rows] + list(consts)
    out_shape, out_specs = [], []
    for o in outs:
        w, dt = o[0], o[1]
        cb, total = (o[2], o[3]) if len(o) == 4 else (0, w)
        out_shape.append(jax.ShapeDtypeStruct((n_rows_total, total), dt))
        out_specs.append(pl.BlockSpec((tm, w), functools.partial(lambda i, cb: (i, cb), cb=cb)))
    io_alias = {}
    if alias is not None:
        in_specs.append(pl.BlockSpec(memory_space=pl.ANY))
        args.append(alias[0])
        io_alias = {len(args) - 1: alias[1]}
    for (r, w) in accs:
        out_shape.append(jax.ShapeDtypeStruct((r, w), F32))
        out_specs.append(pl.BlockSpec((r, w), lambda i: (0, 0)))
    res = pl.pallas_call(body, name=name, grid=(n_rows_total // tm,), in_specs=in_specs, out_specs=out_specs,
                         out_shape=out_shape, input_output_aliases=io_alias,
                         compiler_params=_params("arbitrary"))(*args)
    return res


def _mm_nn(name, a, b, tm, tn, bias=None, resid=None):
    m, k = a.shape
    n = b.shape[1]

    def body(*refs):
        acc = _dot(refs[0][...], refs[1][...])
        pos = 2
        if bias is not None:
            acc = acc + refs[pos][...]
            pos += 1
        if resid is not None:
            acc = acc + refs[pos][...]
            pos += 1
        refs[pos][...] = acc

    in_specs = [pl.BlockSpec((tm, k), lambda j, i: (i, 0)), pl.BlockSpec((k, tn), lambda j, i: (0, j))]
    args = [a, b]
    if bias is not None:
        in_specs.append(pl.BlockSpec((1, tn), lambda j, i: (0, j)))
        args.append(bias)
    if resid is not None:
        in_specs.append(pl.BlockSpec((tm, tn), lambda j, i: (i, j)))
        args.append(resid)
    return pl.pallas_call(body, name=name, grid=(n // tn, m // tm), in_specs=in_specs,
                          out_specs=pl.BlockSpec((tm, tn), lambda j, i: (i, j)),
                          out_shape=jax.ShapeDtypeStruct((m, n), F32),
                          compiler_params=_params("arbitrary", "arbitrary"))(*args)


def _mm_nt(name, a, b, tm, tn):
    m, n = a.shape
    k = b.shape[0]
    steps = n // tn

    def body(a_ref, b_ref, o_ref, acc_ref):
        s = pl.program_id(1)

        @pl.when(s == 0)
        def _():
            acc_ref[...] = jnp.zeros_like(acc_ref)

        acc_ref[...] += _dot(a_ref[...], b_ref[...], _NT)

        @pl.when(s == steps - 1)
        def _():
            o_ref[...] = acc_ref[...]

    return pl.pallas_call(body, name=name, grid=(m // tm, steps),
                          in_specs=[pl.BlockSpec((tm, tn), lambda i, s: (i, s)),
                                    pl.BlockSpec((k, tn), lambda i, s: (0, s))],
                          out_specs=pl.BlockSpec((tm, k), lambda i, s: (i, 0)),
                          out_shape=jax.ShapeDtypeStruct((m, k), F32),
                          scratch_shapes=[pltpu.VMEM((tm, k), F32)],
                          compiler_params=_params("arbitrary", "arbitrary"))(a, b)


def _mm_tn(name, a, b, tk, tn):
    t, k = a.shape
    n = b.shape[1]
    steps = t // tk

    def body(a_ref, b_ref, o_ref, acc_ref):
        s = pl.program_id(1)

        @pl.when(s == 0)
        def _():
            acc_ref[...] = jnp.zeros_like(acc_ref)

        acc_ref[...] += _dot(a_ref[...], b_ref[...], _TN)

        @pl.when(s == steps - 1)
        def _():
            o_ref[...] = acc_ref[...]

    return pl.pallas_call(body, name=name, grid=(n // tn, steps),
                          in_specs=[pl.BlockSpec((tk, k), lambda j, s: (s, 0)),
                                    pl.BlockSpec((tk, tn), lambda j, s: (s, j))],
                          out_specs=pl.BlockSpec((k, tn), lambda j, s: (0, j)),
                          out_shape=jax.ShapeDtypeStruct((k, n), F32),
                          scratch_shapes=[pltpu.VMEM((k, tn), F32)],
                          compiler_params=_params("arbitrary", "arbitrary"))(a, b)


def _hg_chunk_terms(q, f, lb):
    sig = _sig(f)
    fv = lb + (1.0 - lb) * sig
    kk = (1.0 - lb) * (1.0 - sig)
    row = lax.broadcasted_iota(jnp.int32, (HG_CHUNK, HG_CHUNK), 0)
    col = lax.broadcasted_iota(jnp.int32, (HG_CHUNK, HG_CHUNK), 1)
    b = _dot32((row >= col).astype(F32), jnp.log(fv))
    b_mid = b[HG_CHUNK // 2 - 1:HG_CHUNK // 2, :]
    b_last = b[HG_CHUNK - 1:HG_CHUNK, :]
    e_mid = jnp.exp(b - b_mid)
    e_mid_inv = jnp.exp(b_mid - b)
    e_b = jnp.exp(b)
    e_last = jnp.exp(b_last - b)
    return sig, fv, kk, row >= col, row <= col, q * e_mid, kk * e_mid_inv, e_mid, e_mid_inv, e_b, e_last, jnp.exp(b_last)


def _hgrn2_fwd(proj, hg_lb, hg_norm_g, t_len, tb):
    nck = tb // HG_CHUNK

    def body(p_ref, lb_ref, gn_ref, o_ref, act_ref, sp_ref, st_ref):
        @pl.when(pl.program_id(0) == 0)
        def _():
            st_ref[...] = jnp.zeros_like(st_ref)

        for c in range(nck):
            r = pl.ds(c * HG_CHUNK, HG_CHUNK)
            for h in range(HG_HEADS):
                hs = pl.ds(h * HG_DIM, HG_DIM)
                lb = _sig(lb_ref[0:1, hs] - lb_ref[1:2, hs])
                q = p_ref[r, pl.ds(h * HG_DIM, HG_DIM)]
                f = p_ref[r, pl.ds(1024 + h * HG_DIM, HG_DIM)]
                v = p_ref[r, pl.ds(2048 + h * HG_DIM, HG_DIM)]
                _, _, kk, causal, _, a, bm, _, _, e_b, e_last, dc = _hg_chunk_terms(q, f, lb)
                scores = jnp.where(causal, _dot(a, bm, _NT), 0.0)
                st = st_ref[h]
                o = _dot(scores, v) + _dot(q * e_b, st, _NT)
                sp_ref[h, c] = st
                st_ref[h] = dc * st + _dot(v, kk * e_last, _TN)
                o_ref[r, hs] = o

        for h in range(HG_HEADS):
            hs = pl.ds(h * HG_DIM, HG_DIM)
            o = o_ref[:, hs]
            rr = lax.rsqrt(jnp.mean(o * o, axis=-1, keepdims=True) + NORM_EPS)
            g = p_ref[:, pl.ds(3072 + h * HG_DIM, HG_DIM)]
            act_ref[:, hs] = (o * rr * gn_ref[:, hs] * (g * _sig(g))).astype(act_ref.dtype)

    nb = t_len // tb
    return pl.pallas_call(
        body, name="hgrn2_fwd", grid=(nb,),
        in_specs=[pl.BlockSpec((tb, 4096), lambda i: (i, 0)),
                  pl.BlockSpec((2, 1024), lambda i: (0, 0)),
                  pl.BlockSpec((1, 1024), lambda i: (0, 0))],
        out_specs=[pl.BlockSpec((tb, 1024), lambda i: (i, 0)),
                   pl.BlockSpec((tb, 1024), lambda i: (i, 0)),
                   pl.BlockSpec((HG_HEADS, nck, HG_DIM, HG_DIM), lambda i: (0, i, 0, 0))],
        out_shape=[jax.ShapeDtypeStruct((t_len, 1024), F32),
                   jax.ShapeDtypeStruct((t_len, 1024), MXU_DTYPE),
                   jax.ShapeDtypeStruct((HG_HEADS, t_len // HG_CHUNK, HG_DIM, HG_DIM), F32)],
        scratch_shapes=[pltpu.VMEM((HG_HEADS, HG_DIM, HG_DIM), F32)],
        compiler_params=_params("arbitrary"))(proj, hg_lb, hg_norm_g)


def _hgrn2_bwd(proj, d_o, s_prev, hg_lb, dproj, t_len, tb):
    nck = tb // HG_CHUNK
    nb = t_len // tb

    def body(p_ref, do_ref, sp_ref, lb_ref, _, dp_ref, dlb_ref, ds_ref, acc_ref):
        @pl.when(pl.program_id(0) == 0)
        def _():
            ds_ref[...] = jnp.zeros_like(ds_ref)
            acc_ref[...] = jnp.zeros_like(acc_ref)

        for c in reversed(range(nck)):
            r = pl.ds(c * HG_CHUNK, HG_CHUNK)
            for h in range(HG_HEADS):
                hs = pl.ds(h * HG_DIM, HG_DIM)
                lb = _sig(lb_ref[0:1, hs] - lb_ref[1:2, hs])
                q = p_ref[r, pl.ds(h * HG_DIM, HG_DIM)]
                f = p_ref[r, pl.ds(1024 + h * HG_DIM, HG_DIM)]
                v = p_ref[r, pl.ds(2048 + h * HG_DIM, HG_DIM)]
                do = do_ref[r, hs]
                sig, fv, kk, causal, anti, a, bm, e_mid, e_mid_inv, e_b, e_last, dc = _hg_chunk_terms(q, f, lb)
                qd = q * e_b
                kd = kk * e_last
                st = sp_ref[h, c]
                dst = ds_ref[h]
                scores = jnp.where(causal, _dot(a, bm, _NT), 0.0)
                dscores = jnp.where(causal, _dot(do, v, _NT), 0.0)
                dv = _dot(scores, do, _TN) + _dot(kd, dst, _NT)
                da = _dot(dscores, bm)
                dbm = _dot(dscores, a, _TN)
                dqd = _dot(do, st)
                dkd = _dot(v, dst)
                ddc = jnp.sum(dst * st, axis=0, keepdims=True)
                ds_ref[h] = _dot(do, qd, _TN) + dc * dst
                dq = da * e_mid + dqd * e_b
                dk = dbm * e_mid_inv + dkd * e_last
                db = da * a - dbm * bm + dqd * qd - dkd * kd
                extra = jnp.sum(dkd * kd, axis=0, keepdims=True) + ddc * dc
                dlogf = _dot32(anti.astype(F32), db) + extra
                dfv_k = dlogf / fv - dk
                dp_ref[r, pl.ds(h * HG_DIM, HG_DIM)] = dq
                dp_ref[r, pl.ds(1024 + h * HG_DIM, HG_DIM)] = dfv_k * (1.0 - lb) * sig * (1.0 - sig)
                dp_ref[r, pl.ds(2048 + h * HG_DIM, HG_DIM)] = dv
                acc_ref[:, hs] += jnp.sum(dfv_k * (1.0 - sig), axis=0, keepdims=True)

        @pl.when(pl.program_id(0) == nb - 1)
        def _():
            lb_all = _sig(lb_ref[0:1, :] - lb_ref[1:2, :])
            g0 = acc_ref[...] * lb_all * (1.0 - lb_all)
            dlb_ref[0:1, :] = g0
            dlb_ref[1:2, :] = -g0

    return pl.pallas_call(
        body, name="hgrn2_bwd", grid=(nb,),
        in_specs=[pl.BlockSpec((tb, 3072), lambda i: (nb - 1 - i, 0)),
                  pl.BlockSpec((tb, 1024), lambda i: (nb - 1 - i, 0)),
                  pl.BlockSpec((HG_HEADS, nck, HG_DIM, HG_DIM), lambda i: (0, nb - 1 - i, 0, 0)),
                  pl.BlockSpec((2, 1024), lambda i: (0, 0)),
                  pl.BlockSpec(memory_space=pl.ANY)],
        out_specs=[pl.BlockSpec((tb, 3072), lambda i: (nb - 1 - i, 0)),
                   pl.BlockSpec((2, 1024), lambda i: (0, 0))],
        out_shape=[jax.ShapeDtypeStruct((t_len, IN_COLS), F32), jax.ShapeDtypeStruct((2, 1024), F32)],
        scratch_shapes=[pltpu.VMEM((HG_HEADS, HG_DIM, HG_DIM), F32), pltpu.VMEM((1, 1024), F32)],
        input_output_aliases={4: 0},
        compiler_params=_params("arbitrary"))(proj, d_o, s_prev, hg_lb, dproj)


def _s5_prep(a_re, a_im, log_dt, b_re_t, b_im_t):
    def body(ar_ref, ai_ref, ldt_ref, br_ref, bi_ref, lam_ref, pr_ref, pi_ref, bbr_ref, bbi_ref):
        ar, ai = ar_ref[...], ai_ref[...]
        dt = jnp.exp(ldt_ref[...])
        mag = jnp.exp(ar * dt)
        lr, li = mag * jnp.cos(ai * dt), mag * jnp.sin(ai * dt)
        den = ar * ar + ai * ai
        nr = lr - 1.0
        sr = (nr * ar + li * ai) / den
        si = (li * ar - nr * ai) / den
        lam_ref[0:1, :] = lr
        lam_ref[1:2, :] = li
        cr, ci = lr, li
        for i in range(SUBLANES):
            pr_ref[i:i + 1, :] = cr
            pi_ref[i:i + 1, :] = ci
            cr, ci = cr * lr - ci * li, cr * li + ci * lr
        br, bi = br_ref[...], bi_ref[...]
        bbr_ref[...] = sr * br - si * bi
        bbi_ref[...] = sr * bi + si * br

    whole = pl.BlockSpec(memory_space=pltpu.VMEM)
    return pl.pallas_call(
        body, name="s5_prep", in_specs=[whole] * 5, out_specs=[whole] * 5,
        out_shape=[jax.ShapeDtypeStruct((2, S5_LANES), F32), jax.ShapeDtypeStruct((SUBLANES, S5_LANES), F32),
                   jax.ShapeDtypeStruct((SUBLANES, S5_LANES), F32), jax.ShapeDtypeStruct((S5_GROUP, S5_LANES), F32),
                   jax.ShapeDtypeStruct((S5_GROUP, S5_LANES), F32)])(a_re, a_im, log_dt, b_re_t, b_im_t)


def _s5_prep_bwd(a_re, a_im, log_dt, b_re_t, b_im_t, dlam, dbbr, dbbi):
    def body(ar_ref, ai_ref, ldt_ref, br_ref, bi_ref, dlam_ref, dbbr_ref, dbbi_ref,
             dar_ref, dai_ref, dldt_ref, dbr_ref, dbi_ref):
        ar, ai = ar_ref[...], ai_ref[...]
        dt = jnp.exp(ldt_ref[...])
        mag = jnp.exp(ar * dt)
        cs, sn = jnp.cos(ai * dt), jnp.sin(ai * dt)
        lr, li = mag * cs, mag * sn
        den = ar * ar + ai * ai
        nr = lr - 1.0
        sr = (nr * ar + li * ai) / den
        si = (li * ar - nr * ai) / den
        br, bi = br_ref[...], bi_ref[...]
        gbr, gbi = dbbr_ref[...], dbbi_ref[...]
        dbr_ref[...] = sr * gbr + si * gbi
        dbi_ref[...] = sr * gbi - si * gbr
        dsr = jnp.sum(gbr * br + gbi * bi, axis=0, keepdims=True)
        dsi = jnp.sum(gbi * br - gbr * bi, axis=0, keepdims=True)
        dnr = (dsr * ar - dsi * ai) / den
        dli = dlam_ref[1:2, :] + (dsr * ai + dsi * ar) / den
        dlr = dlam_ref[0:1, :] + dnr
        dden = -(dsr * sr + dsi * si) / den
        dar = (dsr * nr + dsi * li) / den + dden * 2.0 * ar
        dai = (dsr * li - dsi * nr) / den + dden * 2.0 * ai
        dmag = dlr * cs + dli * sn
        dth = mag * (dli * cs - dlr * sn)
        dar_ref[...] = dar + dmag * mag * dt
        dai_ref[...] = dai + dth * dt
        ddt = (dmag * mag * ar + dth * ai) * dt
        lane = lax.broadcasted_iota(jnp.int32, (S5_LANES, 128), 0) // S5_STATE
        grp = lax.broadcasted_iota(jnp.int32, (S5_LANES, 128), 1)
        dldt_ref[...] = _dot32(jnp.broadcast_to(ddt, (SUBLANES, S5_LANES)), (lane == grp).astype(F32))

    whole = pl.BlockSpec(memory_space=pltpu.VMEM)
    return pl.pallas_call(
        body, name="s5_prep_bwd", in_specs=[whole] * 8, out_specs=[whole] * 5,
        out_shape=[jax.ShapeDtypeStruct((1, S5_LANES), F32), jax.ShapeDtypeStruct((1, S5_LANES), F32),
                   jax.ShapeDtypeStruct((SUBLANES, 128), F32), jax.ShapeDtypeStruct((S5_GROUP, S5_LANES), F32),
                   jax.ShapeDtypeStruct((S5_GROUP, S5_LANES), F32)])(a_re, a_im, log_dt, b_re_t, b_im_t, dlam, dbbr,
                                                                      dbbi)


S5_LANE_CHUNK = 512


def _shift_rows(x, s, rowid):
    if s > 0:
        return jnp.where(rowid >= s, pltpu.roll(x, s, 0), 0.0)
    return jnp.where(rowid < SUBLANES + s, pltpu.roll(x, SUBLANES + s, 0), 0.0)


def _scan8(xr, xi, pr, pi, sign, rowid):
    for s, row in ((1, 0), (2, 1), (4, 3)):
        lr, li = pr[row:row + 1, :], pi[row:row + 1, :]
        sr, si = _shift_rows(xr, sign * s, rowid), _shift_rows(xi, sign * s, rowid)
        xr, xi = xr + lr * sr - li * si, xi + lr * si + li * sr
    return xr, xi


def _s5_fwd(proj, pw_re, pw_im, bbr_bd, bbi_bd, crt_bd, cit_bd, d_row, t_len, tb):
    ngrp = tb // SUBLANES

    def body(u_ref, pr_ref, pi_ref, bbr_ref, bbi_ref, crt_ref, cit_ref, d_ref,
             hr_ref, hi_ref, ypre_ref, ys_ref, cr_ref, ci_ref):
        @pl.when(pl.program_id(0) == 0)
        def _():
            cr_ref[...] = jnp.zeros_like(cr_ref)
            ci_ref[...] = jnp.zeros_like(ci_ref)

        u = u_ref[...]
        hr_ref[...] = _dot(u, bbr_ref[...])
        hi_ref[...] = _dot(u, bbi_ref[...])
        rowid = lax.broadcasted_iota(jnp.int32, (SUBLANES, S5_LANE_CHUNK), 0)
        for lc in range(S5_LANES // S5_LANE_CHUNK):
            ls = pl.ds(lc * S5_LANE_CHUNK, S5_LANE_CHUNK)
            pr, pi = pr_ref[:, ls], pi_ref[:, ls]

            def group(g, carry, ls=ls, pr=pr, pi=pi):
                cr, ci = carry
                r = pl.ds(pl.multiple_of(g * SUBLANES, SUBLANES), SUBLANES)
                xr, xi = _scan8(hr_ref[r, ls], hi_ref[r, ls], pr, pi, 1, rowid)
                xr, xi = xr + pr * cr - pi * ci, xi + pr * ci + pi * cr
                hr_ref[r, ls] = xr
                hi_ref[r, ls] = xi
                return xr[SUBLANES - 1:SUBLANES, :], xi[SUBLANES - 1:SUBLANES, :]

            cr, ci = lax.fori_loop(0, ngrp, group, (cr_ref[:, ls], ci_ref[:, ls]))
            cr_ref[:, ls] = cr
            ci_ref[:, ls] = ci
        y = _dot(hr_ref[...], crt_ref[...]) - _dot(hi_ref[...], cit_ref[...]) + d_ref[...] * u
        ypre_ref[...] = y
        ys_ref[...] = jax.nn.gelu(y, approximate=True).astype(ys_ref.dtype)

    whole = pl.BlockSpec(memory_space=pltpu.VMEM)
    return pl.pallas_call(
        body, name="s5_fwd", grid=(t_len // tb,),
        in_specs=[pl.BlockSpec((tb, S5_WIDTH), lambda i: (i, 4096 // S5_WIDTH))] + [whole] * 7,
        out_specs=[pl.BlockSpec((tb, S5_LANES), lambda i: (i, 0)), pl.BlockSpec((tb, S5_LANES), lambda i: (i, 0)),
                   pl.BlockSpec((tb, S5_WIDTH), lambda i: (i, 0)), pl.BlockSpec((tb, S5_WIDTH), lambda i: (i, 0))],
        out_shape=[jax.ShapeDtypeStruct((t_len, S5_LANES), F32), jax.ShapeDtypeStruct((t_len, S5_LANES), F32),
                   jax.ShapeDtypeStruct((t_len, S5_WIDTH), F32), jax.ShapeDtypeStruct((t_len, S5_WIDTH), MXU_DTYPE)],
        scratch_shapes=[pltpu.VMEM((1, S5_LANES), F32), pltpu.VMEM((1, S5_LANES), F32)],
        compiler_params=_params("arbitrary"))(proj, pw_re, pw_im, bbr_bd, bbi_bd, crt_bd, cit_bd, d_row)


def _dgelu(x):
    c, a = 0.7978845608028654, 0.044715
    th = jnp.tanh(c * (x + a * x * x * x))
    return 0.5 * (1.0 + th) + 0.5 * x * (1.0 - th * th) * c * (1.0 + 3.0 * a * x * x)


def _s5_bwd(dgelu, y_pre, proj, h_re, h_im, pwr_re, pwr_im, bbr_bd, bbi_bd, cr_bd, ci_bd, d_row, dproj, t_len, tb):
    ngrp = tb // SUBLANES
    nb = t_len // tb

    def body(dg_ref, yp_ref, u_ref, hr_ref, hi_ref, pr_ref, pi_ref, bbr_ref, bbi_ref, cr_ref, ci_ref, d_ref, _,
             du_ref, dbbr_ref, dbbi_ref, dcr_ref, dci_ref, dd_ref, dlam_ref,
             gr_ref, gi_ref, car_ref, cai_ref, abr_ref, abi_ref, acr_ref, aci_ref, ad_ref, alr_ref, ali_ref, sem):
        @pl.when(pl.program_id(0) == 0)
        def _():
            for ref in (car_ref, cai_ref, abr_ref, abi_ref, acr_ref, aci_ref, ad_ref, alr_ref, ali_ref):
                ref[...] = jnp.zeros_like(ref)

        u = u_ref[...]
        dy = dg_ref[...] * _dgelu(yp_ref[...])
        gr_ref[...] = _dot(dy, cr_ref[...])
        gi_ref[...] = -_dot(dy, ci_ref[...])
        rowid = lax.broadcasted_iota(jnp.int32, (SUBLANES, S5_LANE_CHUNK), 0)
        for lc in range(S5_LANES // S5_LANE_CHUNK):
            ls = pl.ds(lc * S5_LANE_CHUNK, S5_LANE_CHUNK)
            pr, pi = pr_ref[:, ls], pi_ref[:, ls]
            fwd_rows_r = jnp.concatenate([pr[7:8], pr[6:7], pr[6:7], pr[4:5]], axis=0)
            fwd_rows_i = jnp.concatenate([pi[7:8], pi[6:7], pi[6:7], pi[4:5]], axis=0)

            def group(j, carry, ls=ls, pr=pr, pi=pi, fr=fwd_rows_r, fi=fwd_rows_i):
                cr, ci, slr, sli = carry
                g = ngrp - 1 - j
                r = pl.ds(pl.multiple_of(g * SUBLANES, SUBLANES), SUBLANES)
                xr, xi = _scan8(gr_ref[r, ls], gi_ref[r, ls], fr, fi, -1, rowid)
                xr, xi = xr + pr * cr - pi * ci, xi + pr * ci + pi * cr
                gr_ref[r, ls] = xr
                gi_ref[r, ls] = xi
                nr = jnp.where(rowid == SUBLANES - 1, cr, pltpu.roll(xr, SUBLANES - 1, 0))
                ni = jnp.where(rowid == SUBLANES - 1, ci, pltpu.roll(xi, SUBLANES - 1, 0))
                hr, hi = hr_ref[r, ls], hi_ref[r, ls]
                slr = slr + nr * hr + ni * hi
                sli = sli + ni * hr - nr * hi
                return xr[0:1, :], xi[0:1, :], slr, sli

            zero = jnp.zeros((SUBLANES, S5_LANE_CHUNK), F32)
            cr, ci, slr, sli = lax.fori_loop(0, ngrp, group, (car_ref[:, ls], cai_ref[:, ls], zero, zero))
            car_ref[:, ls] = cr
            cai_ref[:, ls] = ci
            alr_ref[:, ls] += jnp.sum(slr, axis=0, keepdims=True)
            ali_ref[:, ls] += jnp.sum(sli, axis=0, keepdims=True)
        gr, gi = gr_ref[...], gi_ref[...]
        du_ref[...] = _dot(gr, bbr_ref[...], _NT) + _dot(gi, bbi_ref[...], _NT) + d_ref[...] * dy
        abr_ref[...] += _dot(u, gr, _TN)
        abi_ref[...] += _dot(u, gi, _TN)
        acr_ref[...] += _dot(hr_ref[...], dy, _TN)
        aci_ref[...] -= _dot(hi_ref[...], dy, _TN)
        ad_ref[...] += jnp.sum(dy * u, axis=0, keepdims=True)

        @pl.when(pl.program_id(0) == nb - 1)
        def _():
            dd_ref[...] = ad_ref[...]
            dlam_ref[0:1, :] = alr_ref[...]
            dlam_ref[1:2, :] = ali_ref[...]
            copies = [pltpu.make_async_copy(s, d, sem.at[k]) for k, (s, d) in enumerate(
                ((abr_ref, dbbr_ref), (abi_ref, dbbi_ref), (acr_ref, dcr_ref), (aci_ref, dci_ref)))]
            for cp in copies:
                cp.start()
            for cp in copies:
                cp.wait()

    whole = pl.BlockSpec(memory_space=pltpu.VMEM)
    hbm = pl.BlockSpec(memory_space=pl.ANY)
    rev = lambda i: (nb - 1 - i, 0)
    return pl.pallas_call(
        body, name="s5_bwd", grid=(nb,),
        in_specs=[pl.BlockSpec((tb, S5_WIDTH), rev), pl.BlockSpec((tb, S5_WIDTH), rev),
                  pl.BlockSpec((tb, S5_WIDTH), lambda i: (nb - 1 - i, 4096 // S5_WIDTH)),
                  pl.BlockSpec((tb, S5_LANES), rev), pl.BlockSpec((tb, S5_LANES), rev)] + [whole] * 7 + [hbm],
        out_specs=[pl.BlockSpec((tb, S5_WIDTH), lambda i: (nb - 1 - i, 4096 // S5_WIDTH)), hbm, hbm, hbm, hbm,
                   pl.BlockSpec((1, S5_WIDTH), lambda i: (0, 0)), pl.BlockSpec((2, S5_LANES), lambda i: (0, 0))],
        out_shape=[jax.ShapeDtypeStruct((t_len, IN_COLS), F32),
                   jax.ShapeDtypeStruct((S5_WIDTH, S5_LANES), F32), jax.ShapeDtypeStruct((S5_WIDTH, S5_LANES), F32),
                   jax.ShapeDtypeStruct((S5_LANES, S5_WIDTH), F32), jax.ShapeDtypeStruct((S5_LANES, S5_WIDTH), F32),
                   jax.ShapeDtypeStruct((1, S5_WIDTH), F32), jax.ShapeDtypeStruct((2, S5_LANES), F32)],
        scratch_shapes=[pltpu.VMEM((tb, S5_LANES), F32), pltpu.VMEM((tb, S5_LANES), F32),
                        pltpu.VMEM((1, S5_LANES), F32), pltpu.VMEM((1, S5_LANES), F32),
                        pltpu.VMEM((S5_WIDTH, S5_LANES), F32), pltpu.VMEM((S5_WIDTH, S5_LANES), F32),
                        pltpu.VMEM((S5_LANES, S5_WIDTH), F32), pltpu.VMEM((S5_LANES, S5_WIDTH), F32),
                        pltpu.VMEM((1, S5_WIDTH), F32), pltpu.VMEM((1, S5_LANES), F32),
                        pltpu.VMEM((1, S5_LANES), F32), pltpu.SemaphoreType.DMA((4,))],
        input_output_aliases={12: 0},
        compiler_params=_params("arbitrary"))(dgelu, y_pre, proj, h_re, h_im, pwr_re, pwr_im, bbr_bd, bbi_bd, cr_bd,
                                              ci_bd, d_row, dproj)


S5_BLOCKS = 4
S5_BW = S5_WIDTH // S5_BLOCKS
S5_BL = S5_LANES // S5_BLOCKS
S5_LANE_BLOCKS = S5_LANES // 128
S5_SCAN_BLOCKS = 4


def _s5_powers(a_re, a_im, log_dt, b_re_t, b_im_t, seg):
    def body(ar_ref, ai_ref, ldt_ref, br_ref, bi_ref, pr_ref, pi_ref, bbr_ref, bbi_ref):
        ar, ai = ar_ref[...], ai_ref[...]
        dt = jnp.exp(ldt_ref[...])
        mag = jnp.exp(ar * dt)
        lr, li = mag * jnp.cos(ai * dt), mag * jnp.sin(ai * dt)
        den = ar * ar + ai * ai
        nr = lr - 1.0
        sr = (nr * ar + li * ai) / den
        si = (li * ar - nr * ai) / den
        cr, ci = lr, li
        for i in range(seg):
            pr_ref[i:i + 1, :] = cr
            pi_ref[i:i + 1, :] = ci
            cr, ci = cr * lr - ci * li, cr * li + ci * lr
        br, bi = br_ref[...], bi_ref[...]
        bbr_ref[...] = sr * br - si * bi
        bbi_ref[...] = sr * bi + si * br

    whole = pl.BlockSpec(memory_space=pltpu.VMEM)
    return pl.pallas_call(
        body, name="s5_prep", in_specs=[whole] * 5, out_specs=[whole] * 4,
        out_shape=[jax.ShapeDtypeStruct((seg, S5_LANES), F32), jax.ShapeDtypeStruct((seg, S5_LANES), F32),
                   jax.ShapeDtypeStruct((S5_GROUP, S5_LANES), F32),
                   jax.ShapeDtypeStruct((S5_GROUP, S5_LANES), F32)])(a_re, a_im, log_dt, b_re_t, b_im_t)


def _scan_tables(pw_re, pw_im, reverse):
    seg = pw_re.shape[0]
    if reverse:
        pw_re, pw_im = pw_re[::-1], -pw_im[::-1]
        one, full = seg - 1, 0
    else:
        one, full = 0, seg - 1
    rows = jnp.stack([pw_re[one], pw_im[one], pw_re[full], pw_im[full]])
    wide = lambda t: jnp.broadcast_to(t[:, None, :], (seg, SUBLANES, S5_LANES))
    return rows, wide(pw_re), wide(pw_im)


def _lanes(j):
    return pl.ds(j * 128, 128)


def _segment_scan(xr_ref, xi_ref, lam_ref, car_ref, cai_ref, cn_r, cn_i, blocks, seg, reverse):
    shape = (SUBLANES, 128)
    lrs = [jnp.broadcast_to(lam_ref[0:1, _lanes(j)], shape) for j in blocks]
    lis = [jnp.broadcast_to(lam_ref[1:2, _lanes(j)], shape) for j in blocks]

    def step(k, carry):
        idx = pl.ds(seg - 1 - k if reverse else k, SUBLANES, stride=seg)
        out = []
        for n, j in enumerate(blocks):
            cr, ci = carry[2 * n], carry[2 * n + 1]
            nr = lrs[n] * cr - lis[n] * ci + xr_ref[j, idx, :]
            ni = lrs[n] * ci + lis[n] * cr + xi_ref[j, idx, :]
            xr_ref[j, idx, :] = nr
            xi_ref[j, idx, :] = ni
            out += [nr, ni]
        return tuple(out)

    zero = jnp.zeros(shape, F32)
    fin = lax.fori_loop(0, seg, step, (zero,) * (2 * len(blocks)), unroll=2)
    for n, j in enumerate(blocks):
        ls = _lanes(j)
        fr, fi = fin[2 * n], fin[2 * n + 1]
        sr, si = lam_ref[2:3, ls], lam_ref[3:4, ls]
        pr, pi = car_ref[:, ls], cai_ref[:, ls]
        for s in (reversed(range(SUBLANES)) if reverse else range(SUBLANES)):
            cn_r[s:s + 1, ls] = pr
            cn_i[s:s + 1, ls] = pi
            pr, pi = fr[s:s + 1, :] + sr * pr - si * pi, fi[s:s + 1, :] + sr * pi + si * pr
        car_ref[:, ls] = pr
        cai_ref[:, ls] = pi


def _s5_fwd2(proj, lam_rows, p3_re, p3_im, bbr4, bbi4, crt4, cit4, d_row, t_len, tb):
    seg = tb // SUBLANES

    def body(u_ref, lam_ref, p3r_ref, p3i_ref, bbr_ref, bbi_ref, crt_ref, cit_ref, d_ref,
             hr_ref, hi_ref, ypre_ref, ys_ref, car_ref, cai_ref, cn_r, cn_i):
        @pl.when(pl.program_id(0) == 0)
        def _():
            car_ref[...] = jnp.zeros_like(car_ref)
            cai_ref[...] = jnp.zeros_like(cai_ref)

        u = u_ref[...]
        for i in range(S5_BLOCKS):
            ui = u[:, i * S5_BW:(i + 1) * S5_BW]
            xr, xi = _dot(ui, bbr_ref[i]), _dot(ui, bbi_ref[i])
            for jj in range(S5_BL // 128):
                hr_ref[i * (S5_BL // 128) + jj] = xr[:, jj * 128:(jj + 1) * 128]
                hi_ref[i * (S5_BL // 128) + jj] = xi[:, jj * 128:(jj + 1) * 128]
        for lc in range(S5_LANE_BLOCKS // S5_SCAN_BLOCKS):
            blocks = range(lc * S5_SCAN_BLOCKS, (lc + 1) * S5_SCAN_BLOCKS)
            _segment_scan(hr_ref, hi_ref, lam_ref, car_ref, cai_ref, cn_r, cn_i, blocks, seg, False)
            crs = [cn_r[:, _lanes(j)] for j in blocks]
            cis = [cn_i[:, _lanes(j)] for j in blocks]

            def fix(t, carry, blocks=blocks, crs=crs, cis=cis):
                idx = pl.ds(t, SUBLANES, stride=seg)
                for n, j in enumerate(blocks):
                    pr, pi = p3r_ref[t, :, _lanes(j)], p3i_ref[t, :, _lanes(j)]
                    hr_ref[j, idx, :] += pr * crs[n] - pi * cis[n]
                    hi_ref[j, idx, :] += pr * cis[n] + pi * crs[n]
                return carry

            lax.fori_loop(0, seg, fix, 0, unroll=2)
        for i in range(S5_BLOCKS):
            ws = pl.ds(i * S5_BW, S5_BW)
            js = range(i * (S5_BL // 128), (i + 1) * (S5_BL // 128))
            hr = jnp.concatenate([hr_ref[j] for j in js], axis=1)
            hi = jnp.concatenate([hi_ref[j] for j in js], axis=1)
            y = _dot(hr, crt_ref[i]) - _dot(hi, cit_ref[i]) + d_ref[:, ws] * u[:, i * S5_BW:(i + 1) * S5_BW]
            ypre_ref[:, ws] = y
            ys_ref[:, ws] = jax.nn.gelu(y, approximate=True).astype(ys_ref.dtype)

    whole = pl.BlockSpec(memory_space=pltpu.VMEM)
    h_spec = pl.BlockSpec((S5_LANE_BLOCKS, tb, 128), lambda i: (0, i, 0))
    return pl.pallas_call(
        body, name="s5_fwd", grid=(t_len // tb,),
        in_specs=[pl.BlockSpec((tb, S5_WIDTH), lambda i: (i, 4096 // S5_WIDTH))] + [whole] * 8,
        out_specs=[h_spec, h_spec,
                   pl.BlockSpec((tb, S5_WIDTH), lambda i: (i, 0)), pl.BlockSpec((tb, S5_WIDTH), lambda i: (i, 0))],
        out_shape=[jax.ShapeDtypeStruct((S5_LANE_BLOCKS, t_len, 128), F32),
                   jax.ShapeDtypeStruct((S5_LANE_BLOCKS, t_len, 128), F32),
                   jax.ShapeDtypeStruct((t_len, S5_WIDTH), F32), jax.ShapeDtypeStruct((t_len, S5_WIDTH), MXU_DTYPE)],
        scratch_shapes=[pltpu.VMEM((1, S5_LANES), F32), pltpu.VMEM((1, S5_LANES), F32),
                        pltpu.VMEM((SUBLANES, S5_LANES), F32), pltpu.VMEM((SUBLANES, S5_LANES), F32)],
        compiler_params=_params("arbitrary"))(proj, lam_rows, p3_re, p3_im, bbr4, bbi4, crt4, cit4, d_row)


def _s5_bwd2(dgelu, y_pre, proj, h_re, h_im, lam_rows, p3_re, p3_im, bbr4, bbi4, cr4, ci4, d_row, dproj, t_len, tb):
    seg = tb // SUBLANES
    nb = t_len // tb

    def body(dg_ref, yp_ref, u_ref, hr_ref, hi_ref, lam_ref, p3r_ref, p3i_ref, bbr_ref, bbi_ref, cr_ref, ci_ref,
             d_ref, _, du_ref, dbbr_ref, dbbi_ref, dcr_ref, dci_ref, dd_ref, dlam_ref,
             gr_ref, gi_ref, car_ref, cai_ref, cn_r, cn_i):
        @pl.when(pl.program_id(0) == 0)
        def _():
            for ref in (car_ref, cai_ref, dbbr_ref, dbbi_ref, dcr_ref, dci_ref, dd_ref, dlam_ref):
                ref[...] = jnp.zeros_like(ref)

        u = u_ref[...]
        dy = dg_ref[...] * _dgelu(yp_ref[...])
        nlb = S5_BL // 128
        for i in range(S5_BLOCKS):
            dyi = dy[:, i * S5_BW:(i + 1) * S5_BW]
            xr, xi = _dot(dyi, cr_ref[i]), -_dot(dyi, ci_ref[i])
            for jj in range(nlb):
                gr_ref[i * nlb + jj] = xr[:, jj * 128:(jj + 1) * 128]
                gi_ref[i * nlb + jj] = xi[:, jj * 128:(jj + 1) * 128]
        for lc in range(S5_LANE_BLOCKS // S5_SCAN_BLOCKS):
            blocks = range(lc * S5_SCAN_BLOCKS, (lc + 1) * S5_SCAN_BLOCKS)
            _segment_scan(gr_ref, gi_ref, lam_ref, car_ref, cai_ref, cn_r, cn_i, blocks, seg, True)
            crs = [cn_r[:, _lanes(j)] for j in blocks]
            cis = [cn_i[:, _lanes(j)] for j in blocks]

            def fix(k, carry, blocks=blocks, crs=crs, cis=cis):
                t = seg - 1 - k
                idx = pl.ds(t, SUBLANES, stride=seg)
                out = []
                for n, j in enumerate(blocks):
                    nr, ni, slr, sli = carry[4 * n:4 * n + 4]
                    pr, pi = p3r_ref[t, :, _lanes(j)], p3i_ref[t, :, _lanes(j)]
                    g_r = gr_ref[j, idx, :] + pr * crs[n] - pi * cis[n]
                    g_i = gi_ref[j, idx, :] + pr * cis[n] + pi * crs[n]
                    gr_ref[j, idx, :] = g_r
                    gi_ref[j, idx, :] = g_i
                    hr, hi = hr_ref[j, idx, :], hi_ref[j, idx, :]
                    out += [g_r, g_i, slr + nr * hr + ni * hi, sli + ni * hr - nr * hi]
                return tuple(out)

            zero = jnp.zeros((SUBLANES, 128), F32)
            init = []
            for n in range(len(blocks)):
                init += [crs[n], cis[n], zero, zero]
            fin = lax.fori_loop(0, seg, fix, tuple(init), unroll=2)
            for n, j in enumerate(blocks):
                dlam_ref[0:1, _lanes(j)] += jnp.sum(fin[4 * n + 2], axis=0, keepdims=True)
                dlam_ref[1:2, _lanes(j)] += jnp.sum(fin[4 * n + 3], axis=0, keepdims=True)
        for i in range(S5_BLOCKS):
            ws = pl.ds(i * S5_BW, S5_BW)
            js = range(i * nlb, (i + 1) * nlb)
            ui, dyi = u[:, i * S5_BW:(i + 1) * S5_BW], dy[:, i * S5_BW:(i + 1) * S5_BW]
            gr = jnp.concatenate([gr_ref[j] for j in js], axis=1)
            gi = jnp.concatenate([gi_ref[j] for j in js], axis=1)
            du_ref[:, ws] = _dot(gr, bbr_ref[i], _NT) + _dot(gi, bbi_ref[i], _NT) + d_ref[:, ws] * dyi
            dbbr_ref[i] += _dot(ui, gr, _TN)
            dbbi_ref[i] += _dot(ui, gi, _TN)
            dcr_ref[i] += _dot(jnp.concatenate([hr_ref[j] for j in js], axis=1), dyi, _TN)
            dci_ref[i] -= _dot(jnp.concatenate([hi_ref[j] for j in js], axis=1), dyi, _TN)
        dd_ref[...] += jnp.sum(dy * u, axis=0, keepdims=True)

    whole = pl.BlockSpec(memory_space=pltpu.VMEM)
    rev = lambda i: (nb - 1 - i, 0)
    const3 = lambda i: (0, 0, 0)
    h_spec = pl.BlockSpec((S5_LANE_BLOCKS, tb, 128), lambda i: (0, nb - 1 - i, 0))
    return pl.pallas_call(
        body, name="s5_bwd", grid=(nb,),
        in_specs=[pl.BlockSpec((tb, S5_WIDTH), rev), pl.BlockSpec((tb, S5_WIDTH), rev),
                  pl.BlockSpec((tb, S5_WIDTH), lambda i: (nb - 1 - i, 4096 // S5_WIDTH)),
                  h_spec, h_spec] + [whole] * 8
                 + [pl.BlockSpec(memory_space=pl.ANY)],
        out_specs=[pl.BlockSpec((tb, S5_WIDTH), lambda i: (nb - 1 - i, 4096 // S5_WIDTH)),
                   pl.BlockSpec((S5_BLOCKS, S5_BW, S5_BL), const3), pl.BlockSpec((S5_BLOCKS, S5_BW, S5_BL), const3),
                   pl.BlockSpec((S5_BLOCKS, S5_BL, S5_BW), const3), pl.BlockSpec((S5_BLOCKS, S5_BL, S5_BW), const3),
                   pl.BlockSpec((1, S5_WIDTH), lambda i: (0, 0)), pl.BlockSpec((2, S5_LANES), lambda i: (0, 0))],
        out_shape=[jax.ShapeDtypeStruct((t_len, IN_COLS), F32),
                   jax.ShapeDtypeStruct((S5_BLOCKS, S5_BW, S5_BL), F32),
                   jax.ShapeDtypeStruct((S5_BLOCKS, S5_BW, S5_BL), F32),
                   jax.ShapeDtypeStruct((S5_BLOCKS, S5_BL, S5_BW), F32),
                   jax.ShapeDtypeStruct((S5_BLOCKS, S5_BL, S5_BW), F32),
                   jax.ShapeDtypeStruct((1, S5_WIDTH), F32), jax.ShapeDtypeStruct((2, S5_LANES), F32)],
        scratch_shapes=[pltpu.VMEM((S5_LANE_BLOCKS, tb, 128), F32), pltpu.VMEM((S5_LANE_BLOCKS, tb, 128), F32),
                        pltpu.VMEM((1, S5_LANES), F32), pltpu.VMEM((1, S5_LANES), F32),
                        pltpu.VMEM((SUBLANES, S5_LANES), F32), pltpu.VMEM((SUBLANES, S5_LANES), F32)],
        input_output_aliases={13: 0},
        compiler_params=_params("arbitrary"))(dgelu, y_pre, proj, h_re, h_im, lam_rows, p3_re, p3_im, bbr4, bbi4,
                                              cr4, ci4, d_row, dproj)


def _to_segment_order(v, stage_ref, out_ref, seg):
    nbl = v.shape[1] // 128
    for b in range(nbl):
        stage_ref[b] = v[:, b * 128:(b + 1) * 128]

    def body(t, carry):
        rows = pl.ds(pl.multiple_of(t * SUBLANES, SUBLANES), SUBLANES)
        for b in range(nbl):
            out_ref[rows, _lanes(b)] = stage_ref[b, pl.ds(t, SUBLANES, stride=seg), :]
        return carry

    lax.fori_loop(0, seg, body, 0)


def _from_segment_order(v, stage_ref, out_ref, seg):
    nbl = v.shape[1] // 128
    for b in range(nbl):
        stage_ref[b] = v[:, b * 128:(b + 1) * 128]
    for s in range(SUBLANES):
        def body(k, carry, s=s):
            rows = pl.ds(pl.multiple_of(s * seg + k * SUBLANES, SUBLANES), SUBLANES)
            for b in range(nbl):
                out_ref[rows, _lanes(b)] = stage_ref[b, pl.ds(k * SUBLANES * SUBLANES + s, SUBLANES,
                                                              stride=SUBLANES), :]
            return carry

        lax.fori_loop(0, seg // SUBLANES, body, 0)


def _tile_scan(xr_ref, xi_ref, lam_ref, car_ref, cai_ref, cn_r, cn_i, blocks, seg, reverse):
    shape = (SUBLANES, 128)
    lrs = [jnp.broadcast_to(lam_ref[0:1, _lanes(j)], shape) for j in blocks]
    lis = [jnp.broadcast_to(lam_ref[1:2, _lanes(j)], shape) for j in blocks]

    def step(k, carry):
        t = seg - 1 - k if reverse else k
        rows = pl.ds(pl.multiple_of(t * SUBLANES, SUBLANES), SUBLANES)
        out = []
        for n, j in enumerate(blocks):
            cr, ci = carry[2 * n], carry[2 * n + 1]
            nr = lrs[n] * cr - lis[n] * ci + xr_ref[rows, _lanes(j)]
            ni = lrs[n] * ci + lis[n] * cr + xi_ref[rows, _lanes(j)]
            xr_ref[rows, _lanes(j)] = nr
            xi_ref[rows, _lanes(j)] = ni
            out += [nr, ni]
        return tuple(out)

    zero = jnp.zeros(shape, F32)
    fin = lax.fori_loop(0, seg, step, (zero,) * (2 * len(blocks)), unroll=2)
    for n, j in enumerate(blocks):
        ls = _lanes(j)
        fr, fi = fin[2 * n], fin[2 * n + 1]
        sr, si = lam_ref[2:3, ls], lam_ref[3:4, ls]
        pr, pi = car_ref[:, ls], cai_ref[:, ls]
        for s in (reversed(range(SUBLANES)) if reverse else range(SUBLANES)):
            cn_r[s:s + 1, ls] = pr
            cn_i[s:s + 1, ls] = pi
            pr, pi = fr[s:s + 1, :] + sr * pr - si * pi, fi[s:s + 1, :] + sr * pi + si * pr
        car_ref[:, ls] = pr
        cai_ref[:, ls] = pi


def _s5_fwd3(proj, lam_rows, p3_re, p3_im, bbr4, bbi4, crt4, cit4, d_row, t_len, tb):
    seg = tb // SUBLANES

    def body(u_ref, lam_ref, p3r_ref, p3i_ref, bbr_ref, bbi_ref, crt_ref, cit_ref, d_ref,
             hr_ref, hi_ref, ypre_ref, ys_ref, car_ref, cai_ref, cn_r, cn_i, stage_ref, us_ref, yseg_ref):
        @pl.when(pl.program_id(0) == 0)
        def _():
            car_ref[...] = jnp.zeros_like(car_ref)
            cai_ref[...] = jnp.zeros_like(cai_ref)

        _to_segment_order(u_ref[...], stage_ref, us_ref, seg)
        u = us_ref[...]
        for i in range(S5_BLOCKS):
            ui = u[:, i * S5_BW:(i + 1) * S5_BW]
            hr_ref[:, pl.ds(i * S5_BL, S5_BL)] = _dot(ui, bbr_ref[i])
            hi_ref[:, pl.ds(i * S5_BL, S5_BL)] = _dot(ui, bbi_ref[i])
        for lc in range(S5_LANE_BLOCKS // S5_SCAN_BLOCKS):
            blocks = range(lc * S5_SCAN_BLOCKS, (lc + 1) * S5_SCAN_BLOCKS)
            _tile_scan(hr_ref, hi_ref, lam_ref, car_ref, cai_ref, cn_r, cn_i, blocks, seg, False)
            crs = [cn_r[:, _lanes(j)] for j in blocks]
            cis = [cn_i[:, _lanes(j)] for j in blocks]

            def fix(t, carry, blocks=blocks, crs=crs, cis=cis):
                rows = pl.ds(pl.multiple_of(t * SUBLANES, SUBLANES), SUBLANES)
                for n, j in enumerate(blocks):
                    pr, pi = p3r_ref[t, :, _lanes(j)], p3i_ref[t, :, _lanes(j)]
                    hr_ref[rows, _lanes(j)] += pr * crs[n] - pi * cis[n]
                    hi_ref[rows, _lanes(j)] += pr * cis[n] + pi * crs[n]
                return carry

            lax.fori_loop(0, seg, fix, 0, unroll=2)
        for i in range(S5_BLOCKS):
            ws = pl.ds(i * S5_BW, S5_BW)
            bl = pl.ds(i * S5_BL, S5_BL)
            yseg_ref[:, ws] = (_dot(hr_ref[:, bl], crt_ref[i]) - _dot(hi_ref[:, bl], cit_ref[i])
                               + d_ref[:, ws] * u[:, i * S5_BW:(i + 1) * S5_BW])
        _from_segment_order(yseg_ref[...], stage_ref, ypre_ref, seg)
        ys_ref[...] = jax.nn.gelu(ypre_ref[...], approximate=True).astype(ys_ref.dtype)

    whole = pl.BlockSpec(memory_space=pltpu.VMEM)
    return pl.pallas_call(
        body, name="s5_fwd", grid=(t_len // tb,),
        in_specs=[pl.BlockSpec((tb, S5_WIDTH), lambda i: (i, 4096 // S5_WIDTH))] + [whole] * 8,
        out_specs=[pl.BlockSpec((tb, S5_LANES), lambda i: (i, 0)), pl.BlockSpec((tb, S5_LANES), lambda i: (i, 0)),
                   pl.BlockSpec((tb, S5_WIDTH), lambda i: (i, 0)), pl.BlockSpec((tb, S5_WIDTH), lambda i: (i, 0))],
        out_shape=[jax.ShapeDtypeStruct((t_len, S5_LANES), F32), jax.ShapeDtypeStruct((t_len, S5_LANES), F32),
                   jax.ShapeDtypeStruct((t_len, S5_WIDTH), F32), jax.ShapeDtypeStruct((t_len, S5_WIDTH), MXU_DTYPE)],
        scratch_shapes=[pltpu.VMEM((1, S5_LANES), F32), pltpu.VMEM((1, S5_LANES), F32),
                        pltpu.VMEM((SUBLANES, S5_LANES), F32), pltpu.VMEM((SUBLANES, S5_LANES), F32),
                        pltpu.VMEM((S5_WIDTH // 128, tb, 128), F32), pltpu.VMEM((tb, S5_WIDTH), F32),
                        pltpu.VMEM((tb, S5_WIDTH), F32)],
        compiler_params=_params("arbitrary"))(proj, lam_rows, p3_re, p3_im, bbr4, bbi4, crt4, cit4, d_row)


def _s5_bwd3(dgelu, y_pre, proj, h_re, h_im, lam_rows, p3_re, p3_im, bbr4, bbi4, cr4, ci4, d_row, dproj, t_len, tb):
    seg = tb // SUBLANES
    nb = t_len // tb

    def body(dg_ref, yp_ref, u_ref, hr_ref, hi_ref, lam_ref, p3r_ref, p3i_ref, bbr_ref, bbi_ref, cr_ref, ci_ref,
             d_ref, _, du_ref, dbbr_ref, dbbi_ref, dcr_ref, dci_ref, dd_ref, dlam_ref,
             gr_ref, gi_ref, car_ref, cai_ref, cn_r, cn_i, stage_ref, us_ref, dys_ref, duseg_ref):
        @pl.when(pl.program_id(0) == 0)
        def _():
            for ref in (car_ref, cai_ref, dbbr_ref, dbbi_ref, dcr_ref, dci_ref, dd_ref, dlam_ref):
                ref[...] = jnp.zeros_like(ref)

        _to_segment_order(u_ref[...], stage_ref, us_ref, seg)
        _to_segment_order(dg_ref[...] * _dgelu(yp_ref[...]), stage_ref, dys_ref, seg)
        u, dy = us_ref[...], dys_ref[...]
        for i in range(S5_BLOCKS):
            dyi = dy[:, i * S5_BW:(i + 1) * S5_BW]
            gr_ref[:, pl.ds(i * S5_BL, S5_BL)] = _dot(dyi, cr_ref[i])
            gi_ref[:, pl.ds(i * S5_BL, S5_BL)] = -_dot(dyi, ci_ref[i])
        for lc in range(S5_LANE_BLOCKS // S5_SCAN_BLOCKS):
            blocks = range(lc * S5_SCAN_BLOCKS, (lc + 1) * S5_SCAN_BLOCKS)
            _tile_scan(gr_ref, gi_ref, lam_ref, car_ref, cai_ref, cn_r, cn_i, blocks, seg, True)
            crs = [cn_r[:, _lanes(j)] for j in blocks]
            cis = [cn_i[:, _lanes(j)] for j in blocks]

            def fix(k, carry, blocks=blocks, crs=crs, cis=cis):
                t = seg - 1 - k
                rows = pl.ds(pl.multiple_of(t * SUBLANES, SUBLANES), SUBLANES)
                out = []
                for n, j in enumerate(blocks):
                    nr, ni, slr, sli = carry[4 * n:4 * n + 4]
                    pr, pi = p3r_ref[t, :, _lanes(j)], p3i_ref[t, :, _lanes(j)]
                    g_r = gr_ref[rows, _lanes(j)] + pr * crs[n] - pi * cis[n]
                    g_i = gi_ref[rows, _lanes(j)] + pr * cis[n] + pi * crs[n]
                    gr_ref[rows, _lanes(j)] = g_r
                    gi_ref[rows, _lanes(j)] = g_i
                    hr, hi = hr_ref[rows, _lanes(j)], hi_ref[rows, _lanes(j)]
                    out += [g_r, g_i, slr + nr * hr + ni * hi, sli + ni * hr - nr * hi]
                return tuple(out)

            zero = jnp.zeros((SUBLANES, 128), F32)
            init = []
            for n in range(len(blocks)):
                init += [crs[n], cis[n], zero, zero]
            fin = lax.fori_loop(0, seg, fix, tuple(init), unroll=2)
            for n, j in enumerate(blocks):
                dlam_ref[0:1, _lanes(j)] += jnp.sum(fin[4 * n + 2], axis=0, keepdims=True)
                dlam_ref[1:2, _lanes(j)] += jnp.sum(fin[4 * n + 3], axis=0, keepdims=True)
        for i in range(S5_BLOCKS):
            ws = pl.ds(i * S5_BW, S5_BW)
            bl = pl.ds(i * S5_BL, S5_BL)
            ui, dyi = u[:, i * S5_BW:(i + 1) * S5_BW], dy[:, i * S5_BW:(i + 1) * S5_BW]
            gr, gi = gr_ref[:, bl], gi_ref[:, bl]
            duseg_ref[:, ws] = _dot(gr, bbr_ref[i], _NT) + _dot(gi, bbi_ref[i], _NT) + d_ref[:, ws] * dyi
            dbbr_ref[i] += _dot(ui, gr, _TN)
            dbbi_ref[i] += _dot(ui, gi, _TN)
            dcr_ref[i] += _dot(hr_ref[:, bl], dyi, _TN)
            dci_ref[i] -= _dot(hi_ref[:, bl], dyi, _TN)
        dd_ref[...] += jnp.sum(dy * u, axis=0, keepdims=True)
        _from_segment_order(duseg_ref[...], stage_ref, du_ref, seg)

    whole = pl.BlockSpec(memory_space=pltpu.VMEM)
    rev = lambda i: (nb - 1 - i, 0)
    const3 = lambda i: (0, 0, 0)
    return pl.pallas_call(
        body, name="s5_bwd", grid=(nb,),
        in_specs=[pl.BlockSpec((tb, S5_WIDTH), rev), pl.BlockSpec((tb, S5_WIDTH), rev),
                  pl.BlockSpec((tb, S5_WIDTH), lambda i: (nb - 1 - i, 4096 // S5_WIDTH)),
                  pl.BlockSpec((tb, S5_LANES), rev), pl.BlockSpec((tb, S5_LANES), rev)] + [whole] * 8
                 + [pl.BlockSpec(memory_space=pl.ANY)],
        out_specs=[pl.BlockSpec((tb, S5_WIDTH), lambda i: (nb - 1 - i, 4096 // S5_WIDTH)),
                   pl.BlockSpec((S5_BLOCKS, S5_BW, S5_BL), const3), pl.BlockSpec((S5_BLOCKS, S5_BW, S5_BL), const3),
                   pl.BlockSpec((S5_BLOCKS, S5_BL, S5_BW), const3), pl.BlockSpec((S5_BLOCKS, S5_BL, S5_BW), const3),
                   pl.BlockSpec((1, S5_WIDTH), lambda i: (0, 0)), pl.BlockSpec((2, S5_LANES), lambda i: (0, 0))],
        out_shape=[jax.ShapeDtypeStruct((t_len, IN_COLS), F32),
                   jax.ShapeDtypeStruct((S5_BLOCKS, S5_BW, S5_BL), F32),
                   jax.ShapeDtypeStruct((S5_BLOCKS, S5_BW, S5_BL), F32),
                   jax.ShapeDtypeStruct((S5_BLOCKS, S5_BL, S5_BW), F32),
                   jax.ShapeDtypeStruct((S5_BLOCKS, S5_BL, S5_BW), F32),
                   jax.ShapeDtypeStruct((1, S5_WIDTH), F32), jax.ShapeDtypeStruct((2, S5_LANES), F32)],
        scratch_shapes=[pltpu.VMEM((tb, S5_LANES), F32), pltpu.VMEM((tb, S5_LANES), F32),
                        pltpu.VMEM((1, S5_LANES), F32), pltpu.VMEM((1, S5_LANES), F32),
                        pltpu.VMEM((SUBLANES, S5_LANES), F32), pltpu.VMEM((SUBLANES, S5_LANES), F32),
                        pltpu.VMEM((S5_WIDTH // 128, tb, 128), F32), pltpu.VMEM((tb, S5_WIDTH), F32),
                        pltpu.VMEM((tb, S5_WIDTH), F32), pltpu.VMEM((tb, S5_WIDTH), F32)],
        input_output_aliases={13: 0},
        compiler_params=_params("arbitrary"))(dgelu, y_pre, proj, h_re, h_im, lam_rows, p3_re, p3_im, bbr4, bbi4,
                                              cr4, ci4, d_row, dproj)


def _block_diag4(per_group):
    g8 = S5_GROUPS // S5_BLOCKS
    eye = jnp.eye(g8, dtype=bool)[None, :, None, :, None]
    dense = jnp.where(eye, per_group.reshape(S5_BLOCKS, g8, S5_GROUP, 1, S5_STATE), 0.0)
    return dense.reshape(S5_BLOCKS, S5_BW, S5_BL)


def _diag_blocks4(dense):
    g8 = S5_GROUPS // S5_BLOCKS
    ar = jnp.arange(g8)
    d5 = dense.reshape(S5_BLOCKS, g8, S5_GROUP, g8, S5_STATE)
    return d5[:, ar, :, ar, :].transpose(1, 0, 2, 3).reshape(S5_GROUPS, S5_GROUP, S5_STATE)


def _block_diag(per_group):
    eye = jnp.eye(S5_GROUPS, dtype=bool)[:, None, :, None]
    dense = jnp.where(eye, per_group[:, :, None, :], 0.0)
    return dense.reshape(S5_WIDTH, S5_LANES)


def _diag_blocks(dense):
    ar = jnp.arange(S5_GROUPS)
    return dense.reshape(S5_GROUPS, S5_GROUP, S5_GROUPS, S5_STATE)[ar, :, ar, :]


def _local_step(x, p, target, w, sm):
    t_len = x.shape[0]
    tm = min(256, t_len)
    tmm = min(512, t_len)
    tb_hg = min(256, t_len)
    tb_s5 = min(256, t_len)
    g1, g2, g3, ghn = sm["norm_g"], sm["ple_norm_g"], sm["final_norm_g"].reshape(1, D_MODEL), sm["hg_norm_g"]

    def rms_f(xv, g):
        r = lax.rsqrt(jnp.mean(xv * xv, axis=-1, keepdims=True) + NORM_EPS)
        return (xv * r * g,)

    (u,) = _rowwise("rms_in", rms_f, t_len, tm, [(x, 1024, 0)], [g1], [(1024, MXU_DTYPE)])
    proj = _mm_nn("mm_in", u, w["w_in"], tmm, 1024)
    o_hg, act_hg, s_prev = _hgrn2_fwd(proj, sm["hg_lb"], ghn, t_len, tb_hg)
    y_hg = _mm_nn("mm_o_hg", act_hg, w["w_o_hg"], tmm, 1024)

    lanes = lambda a: a.reshape(1, S5_LANES)
    a_re, a_im = lanes(sm["s5_a_re"]), lanes(sm["s5_a_im"])
    ldt = lanes(jnp.broadcast_to(sm["s5_log_dt"].reshape(S5_GROUPS, 1), (S5_GROUPS, S5_STATE)))
    to_t = lambda b: b.reshape(S5_GROUPS, S5_STATE, S5_GROUP).transpose(2, 0, 1).reshape(S5_GROUP, S5_LANES)
    b_re_t, b_im_t = to_t(sm["s5_b_re"]), to_t(sm["s5_b_im"])
    pw_re, pw_im, bbr_t, bbi_t = _s5_powers(a_re, a_im, ldt, b_re_t, b_im_t, tb_s5 // SUBLANES)
    from_t = lambda b: b.reshape(S5_GROUP, S5_GROUPS, S5_STATE).transpose(1, 0, 2)
    bbr_bd = _block_diag4(from_t(bbr_t)).astype(MXU_DTYPE)
    bbi_bd = _block_diag4(from_t(bbi_t)).astype(MXU_DTYPE)
    cr_bd = _block_diag4(sm["s5_c_re"].reshape(S5_GROUPS, S5_GROUP, S5_STATE)).astype(MXU_DTYPE)
    ci_bd = _block_diag4(sm["s5_c_im"].reshape(S5_GROUPS, S5_GROUP, S5_STATE)).astype(MXU_DTYPE)
    d_row = sm["s5_d"].reshape(1, S5_WIDTH)
    h_re, h_im, y_pre, ys_gelu = _s5_fwd3(proj, *_scan_tables(pw_re, pw_im, False), bbr_bd, bbi_bd,
                                          cr_bd.transpose(0, 2, 1), ci_bd.transpose(0, 2, 1), d_row, t_len, tb_s5)
    glu = _mm_nn("mm_glu", ys_gelu, w["w_glu"], tmm, 1024, bias=sm["b_glu"])

    def glu_f(gl, z):
        a, b = gl[:, :S5_WIDTH], gl[:, S5_WIDTH:]
        return (a * _sig(b) * (z * _sig(z)),)

    (ys2,) = _rowwise("glu_gate", glu_f, t_len, tm, [(glu, 1024, 0), (proj, 512, 4608 // 512)], [],
                      [(512, MXU_DTYPE)])
    y_s5 = _mm_nn("mm_o_s5", ys2, w["w_o_s5"], tmm, 1024)

    def merge_f(yh, ys, gh, gs):
        return (_sig(gh) * yh + _sig(gs) * ys,)

    (merged,) = _rowwise("merge", merge_f, t_len, tm,
                         [(y_hg, 1024, 0), (y_s5, 1024, 0), (proj, 1024, 5), (proj, 1024, 6)], [],
                         [(1024, MXU_DTYPE)])
    h1 = _mm_nn("mm_out", merged, w["w_out"], tmm, 1024, resid=x)
    (n2,) = _rowwise("rms_ple", rms_f, t_len, tm, [(h1, 1024, 0)], [g2], [(1024, MXU_DTYPE)])
    gl = _mm_nn("mm_ple_gate", n2, w["w_ple_gate"], tmm, 1024)
    pe = _mm_nn("mm_ple", p, w["w_ple"], tmm, 1024)

    def head_f(h1v, pev, glv, tgt, g):
        gate = _sig(glv)
        h2 = h1v + pev * gate
        r = lax.rsqrt(jnp.mean(h2 * h2, axis=-1, keepdims=True) + NORM_EPS)
        e = h2 * r * g - tgt
        loss = 0.5 * jnp.sum(jnp.mean(e * e, axis=-1, keepdims=True), axis=0, keepdims=True)
        dy = e * (1.0 / D_MODEL)
        dg = jnp.sum(dy * h2 * r, axis=0, keepdims=True)
        t = dy * g
        dh2 = r * t - h2 * (r * r * r) * jnp.mean(t * h2, axis=-1, keepdims=True)
        return (dh2, dh2 * gate, dh2 * pev * gate * (1.0 - gate), jnp.broadcast_to(loss, (1, 128)), dg)

    dh2, dpe, dgl, loss_row, d_g3 = _rowwise(
        "loss_head", head_f, t_len, tm, [(h1, 1024, 0), (pe, 1024, 0), (gl, 1024, 0), (target, 1024, 0)], [g3],
        [(1024, F32), (1024, MXU_DTYPE), (1024, MXU_DTYPE)], accs=[(1, 128), (1, 1024)])

    gb = {}
    gb["w_ple"] = _mm_tn("mm_d_w_ple", p, dpe, tmm, 1024)
    gb["w_ple_gate"] = _mm_tn("mm_d_w_ple_gate", n2, dgl, tmm, 1024)
    dn2 = _mm_nt("mm_d_n2", dgl, w["w_ple_gate"], tmm, 1024)

    def ple_b(dn, h1v, dh, g):
        dx, dg = _rms_bwd(dn, h1v, g)
        return (dh + dx, dg)

    dh1, d_g2 = _rowwise("rms_ple_bwd", ple_b, t_len, tm, [(dn2, 1024, 0), (h1, 1024, 0), (dh2, 1024, 0)], [g2],
                         [(1024, F32)], accs=[(1, 1024)])
    gb["w_out"] = _mm_tn("mm_d_w_out", merged, dh1, tmm, 1024)
    dmerged = _mm_nt("mm_d_merged", dh1, w["w_out"], tmm, 1024)

    def gate_b(dm, y, gt):
        s = _sig(gt)
        return (dm * s, dm * y * s * (1.0 - s))

    dy_hg, dproj = _rowwise("gate_hg_bwd", gate_b, t_len, tm, [(dmerged, 1024, 0), (y_hg, 1024, 0), (proj, 1024, 5)],
                            [], [(1024, MXU_DTYPE), (1024, F32, 5, IN_COLS)])
    dy_s5, dproj = _rowwise("gate_s5_bwd", gate_b, t_len, tm, [(dmerged, 1024, 0), (y_s5, 1024, 0), (proj, 1024, 6)],
                            [], [(1024, MXU_DTYPE), (1024, F32, 6, IN_COLS)], alias=(dproj, 1))
    gb["w_o_s5"] = _mm_tn("mm_d_w_o_s5", ys2, dy_s5, tmm, 1024)
    dys2 = _mm_nt("mm_d_ys2", dy_s5, w["w_o_s5"], tmm, 1024)

    def glu_b(dys, gl_, z):
        a, b = gl_[:, :S5_WIDTH], gl_[:, S5_WIDTH:]
        sb, sz = _sig(b), _sig(z)
        silu = z * sz
        dglu = jnp.concatenate([dys * sb * silu, dys * a * silu * sb * (1.0 - sb)], axis=1)
        return (dglu, dys * a * sb * _dsilu(z, sz), jnp.sum(dglu, axis=0, keepdims=True))

    dglu, dproj, d_bglu = _rowwise("glu_bwd", glu_b, t_len, tm,
                                   [(dys2, 512, 0), (glu, 1024, 0), (proj, 512, 4608 // 512)], [],
                                   [(1024, MXU_DTYPE), (512, F32, 4608 // 512, IN_COLS)], accs=[(1, 1024)],
                                   alias=(dproj, 1))
    gb["w_glu"] = _mm_tn("mm_d_w_glu", ys_gelu, dglu, tmm, 1024)
    dgelu = _mm_nt("mm_d_gelu", dglu, w["w_glu"], tmm, 1024)
    dproj, d_bbr, d_bbi, d_crt, d_cit, d_d, d_lam = _s5_bwd3(dgelu, y_pre, proj, h_re, h_im,
                                                            *_scan_tables(pw_re, pw_im, True), bbr_bd, bbi_bd, cr_bd,
                                                            ci_bd, d_row, dproj, t_len, tb_s5)
    to_t3 = lambda b: b.transpose(1, 0, 2).reshape(S5_GROUP, S5_LANES)
    d_are, d_aim, d_ldt, d_br_t, d_bi_t = _s5_prep_bwd(a_re, a_im, ldt, b_re_t, b_im_t, d_lam,
                                                       to_t3(_diag_blocks4(d_bbr)), to_t3(_diag_blocks4(d_bbi)))
    gb["w_o_hg"] = _mm_tn("mm_d_w_o_hg", act_hg, dy_hg, tmm, 1024)
    dact = _mm_nt("mm_d_act_hg", dy_hg, w["w_o_hg"], tmm, 1024)

    def hg_gate_b(da, o, g, gn):
        dos, dgs, dgns = [], [], []
        for h in range(HG_HEADS):
            sl = slice(h * HG_DIM, (h + 1) * HG_DIM)
            oh, gh, dah, gnh = o[:, sl], g[:, sl], da[:, sl], gn[:, sl]
            rr = lax.rsqrt(jnp.mean(oh * oh, axis=-1, keepdims=True) + NORM_EPS)
            sg = _sig(gh)
            dgs.append(dah * (oh * rr * gnh) * _dsilu(gh, sg))
            don = dah * (gh * sg)
            t = don * gnh
            dos.append(rr * t - oh * (rr * rr * rr) * jnp.mean(t * oh, axis=-1, keepdims=True))
            dgns.append(jnp.sum(don * oh * rr, axis=0, keepdims=True))
        return (jnp.concatenate(dos, axis=1), jnp.concatenate(dgs, axis=1), jnp.concatenate(dgns, axis=1))

    d_o, dproj, d_ghn = _rowwise("hg_gate_bwd", hg_gate_b, t_len, tm,
                                 [(dact, 1024, 0), (o_hg, 1024, 0), (proj, 1024, 3)], [ghn],
                                 [(1024, F32), (1024, F32, 3, IN_COLS)], accs=[(1, 1024)], alias=(dproj, 1))
    dproj, d_lb = _hgrn2_bwd(proj, d_o, s_prev, sm["hg_lb"], dproj, t_len, tb_hg)
    gb["w_in"] = _mm_tn("mm_d_w_in", u, dproj, tmm, 1024)
    du = _mm_nt("mm_d_u", dproj, w["w_in"], tmm, 1024)

    def in_b(duv, xv, dh, g):
        dx, dg = _rms_bwd(duv, xv, g)
        return (dh + dx, dg)

    grad_x, d_g1 = _rowwise("rms_in_bwd", in_b, t_len, tm, [(du, 1024, 0), (x, 1024, 0), (dh1, 1024, 0)], [g1],
                            [(1024, F32)], accs=[(1, 1024)])

    back_t = lambda b: b.reshape(S5_GROUP, S5_GROUPS, S5_STATE).transpose(1, 2, 0).reshape(1, S5_GROUPS, S5_STATE,
                                                                                           S5_GROUP)
    gs = {
        "norm_g": d_g1, "hg_lb": d_lb, "hg_norm_g": d_ghn,
        "s5_a_re": d_are.reshape(1, S5_GROUPS, S5_STATE), "s5_a_im": d_aim.reshape(1, S5_GROUPS, S5_STATE),
        "s5_log_dt": d_ldt[0:1, :S5_GROUPS],
        "s5_b_re": back_t(d_br_t), "s5_b_im": back_t(d_bi_t),
        "s5_c_re": _diag_blocks4(d_crt.transpose(0, 2, 1)).reshape(1, S5_GROUPS, S5_GROUP, S5_STATE),
        "s5_c_im": _diag_blocks4(d_cit.transpose(0, 2, 1)).reshape(1, S5_GROUPS, S5_GROUP, S5_STATE),
        "s5_d": d_d.reshape(1, S5_GROUPS, S5_GROUP), "b_glu": d_bglu, "ple_norm_g": d_g2,
        "final_norm_g": d_g3.reshape(D_MODEL),
    }
    return loss_row, grad_x, gb, gs


def _shard_shape(name):
    r, c = BIG_SHAPE[name]
    return (r, c // N_CHIPS) if name in BIG_COL_SHARDED else (r // N_CHIPS, c)


def _pack_shard(parts):
    return jnp.concatenate([parts[n].reshape(-1, PACK_W) for n in BIG], axis=0)


def _unpack_shard(packed):
    out, off = {}, 0
    for n in BIG:
        r, c = _shard_shape(n)
        rows = r * c // PACK_W
        out[n] = packed[off:off + rows].reshape(1, r, c)
        off += rows
    return out


def _unpack_full(gathered):
    out, off = {}, 0
    for n in BIG:
        r, c = _shard_shape(n)
        rows = r * c // PACK_W
        sh = gathered[:, off:off + rows].reshape(N_CHIPS, r, c)
        out[n] = sh.transpose(1, 0, 2).reshape(BIG_SHAPE[n]) if n in BIG_COL_SHARDED else sh.reshape(BIG_SHAPE[n])
        off += rows
    return out


def _pack_full(full):
    parts = []
    for n in BIG:
        r, c = _shard_shape(n)
        g = full[n]
        sh = g.reshape(BIG_SHAPE[n][0], N_CHIPS, c).transpose(1, 0, 2) if n in BIG_COL_SHARDED else g
        parts.append(sh.reshape(N_CHIPS, r * c // PACK_W, PACK_W))
    packed = jnp.concatenate(parts, axis=1)
    return packed.reshape(N_CHIPS, 2, HALF_ROWS, PACK_W).transpose(1, 0, 2, 3)


def _pack_small(parts, last):
    flat = jnp.concatenate([parts[n].reshape(-1) for n in SMALL] + [last.reshape(-1)])
    return jnp.pad(flat, (0, SMALL_ROWS * PACK_W - flat.shape[0])).reshape(SMALL_ROWS, PACK_W)


def _unpack_small(packed):
    flat, out, off = packed.reshape(-1), {}, 0
    for n in SMALL:
        size = 1
        for d in SMALL_SHAPE[n]:
            size *= d
        out[n] = flat[off:off + size].reshape(SMALL_SHAPE[n])
        off += size
    return out, flat[off]


def _place():
    x, y, c = lax.axis_index("x"), lax.axis_index("y"), lax.axis_index("c")
    return x, y, c, [(1 - x, y), (x, 1 - y), (1 - x, 1 - y)]


def _remote(src, dst, send_sems, recv_sems, k, to):
    return pltpu.make_async_remote_copy(src_ref=src, dst_ref=dst, send_sem=send_sems.at[k], recv_sem=recv_sems.at[k],
                                        device_id=to, device_id_type=MESH)


_HBM = pl.BlockSpec(memory_space=pl.ANY)


def _all_gather_weights(wp):
    def body(wp_ref, out_ref, send_sems, recv_sems, local_sem):
        x, y, c, chips = _place()
        k = 2 * x + y
        sibling = (x, y, 1 - c)
        mine = pltpu.make_async_copy(wp_ref, out_ref.at[k], local_sem)
        mine.start()
        first = [_remote(wp_ref.at[c], out_ref.at[k, c], send_sems, recv_sems, j, (cx, cy, c))
                 for j, (cx, cy) in enumerate(chips)]
        for cp in first:
            cp.start()
        passed = []
        for j, (cx, cy) in enumerate(chips):
            kj = 2 * cx + cy
            _remote(wp_ref.at[c], out_ref.at[kj, c], send_sems, recv_sems, j, (cx, cy, c)).wait_recv()
            cp = _remote(out_ref.at[kj, c], out_ref.at[kj, c], send_sems, recv_sems, 3 + j, sibling)
            cp.start()
            passed.append(cp)
        for j, (cx, cy) in enumerate(chips):
            kj = 2 * cx + cy
            _remote(wp_ref.at[c], out_ref.at[kj, 1 - c], send_sems, recv_sems, 3 + j, sibling).wait_recv()
        for cp in first + passed:
            cp.wait_send()
        mine.wait()

    return pl.pallas_call(
        body, name="all_gather_weights", in_specs=[_HBM], out_specs=_HBM,
        out_shape=jax.ShapeDtypeStruct((N_CHIPS, 2, HALF_ROWS, PACK_W), wp.dtype),
        scratch_shapes=[pltpu.SemaphoreType.DMA((6,)), pltpu.SemaphoreType.DMA((6,)), pltpu.SemaphoreType.DMA])(wp)


def _exchange_halves(pg):
    def body(pg_ref, out_ref, send_sems, recv_sems):
        x, y, c, _ = _place()
        cp = _remote(pg_ref.at[1 - c], out_ref, send_sems, recv_sems, 0, (x, y, 1 - c))
        cp.start()
        cp.wait()

    return pl.pallas_call(
        body, name="exchange_halves", in_specs=[_HBM], out_specs=_HBM,
        out_shape=jax.ShapeDtypeStruct((N_CHIPS, HALF_ROWS, PACK_W), pg.dtype),
        scratch_shapes=[pltpu.SemaphoreType.DMA((1,)), pltpu.SemaphoreType.DMA((1,))])(pg)


def _scatter_chip_sums(ps):
    def body(ps_ref, out_ref, send_sems, recv_sems):
        x, y, c, chips = _place()
        cps = [_remote(ps_ref.at[2 * cx + cy], out_ref.at[j], send_sems, recv_sems, j, (cx, cy, c))
               for j, (cx, cy) in enumerate(chips)]
        for cp in cps:
            cp.start()
        for cp in cps:
            cp.wait()

    return pl.pallas_call(
        body, name="scatter_chip_sums", in_specs=[_HBM], out_specs=_HBM,
        out_shape=jax.ShapeDtypeStruct((3, HALF_ROWS, PACK_W), ps.dtype),
        scratch_shapes=[pltpu.SemaphoreType.DMA((3,)), pltpu.SemaphoreType.DMA((3,))])(ps)


def _share_half(g_half):
    def body(g_ref, out_ref, send_sems, recv_sems, local_sem):
        x, y, c, _ = _place()
        mine = pltpu.make_async_copy(g_ref, out_ref.at[c], local_sem)
        mine.start()
        cp = _remote(g_ref, out_ref.at[c], send_sems, recv_sems, 0, (x, y, 1 - c))
        cp.start()
        _remote(g_ref, out_ref.at[1 - c], send_sems, recv_sems, 0, (x, y, 1 - c)).wait_recv()
        cp.wait_send()
        mine.wait()

    return pl.pallas_call(
        body, name="share_half", in_specs=[_HBM], out_specs=_HBM,
        out_shape=jax.ShapeDtypeStruct((2, HALF_ROWS, PACK_W), g_half.dtype),
        scratch_shapes=[pltpu.SemaphoreType.DMA((1,)), pltpu.SemaphoreType.DMA((1,)), pltpu.SemaphoreType.DMA])(g_half)


REDUCE_ROWS = 480


def _sum_pair(pg, theirs, c):
    def body(c_ref, a_ref, b_ref, o_ref):
        o_ref[...] = (a_ref[...] + b_ref[...]).astype(o_ref.dtype)

    return pl.pallas_call(
        body, name="sum_pair",
        grid_spec=pltpu.PrefetchScalarGridSpec(
            num_scalar_prefetch=1, grid=(N_CHIPS, HALF_ROWS // REDUCE_ROWS),
            in_specs=[pl.BlockSpec((None, None, REDUCE_ROWS, PACK_W), lambda j, i, c_ref: (c_ref[0], j, i, 0)),
                      pl.BlockSpec((None, REDUCE_ROWS, PACK_W), lambda j, i, c_ref: (j, i, 0))],
            out_specs=pl.BlockSpec((None, REDUCE_ROWS, PACK_W), lambda j, i, c_ref: (j, i, 0))),
        out_shape=jax.ShapeDtypeStruct((N_CHIPS, HALF_ROWS, PACK_W), WIRE_DTYPE),
        compiler_params=_params("arbitrary", "arbitrary"))(c.reshape(1), pg, theirs)


def _sum_chips(ps, others, k):
    def body(k_ref, a_ref, b_ref, o_ref):
        o_ref[...] = ((a_ref[...].astype(F32) + b_ref[0].astype(F32)) + b_ref[1].astype(F32)) + b_ref[2].astype(F32)

    return pl.pallas_call(
        body, name="sum_chips",
        grid_spec=pltpu.PrefetchScalarGridSpec(
            num_scalar_prefetch=1, grid=(HALF_ROWS // REDUCE_ROWS,),
            in_specs=[pl.BlockSpec((None, REDUCE_ROWS, PACK_W), lambda i, k_ref: (k_ref[0], i, 0)),
                      pl.BlockSpec((3, REDUCE_ROWS, PACK_W), lambda i, k_ref: (0, i, 0))],
            out_specs=pl.BlockSpec((REDUCE_ROWS, PACK_W), lambda i, k_ref: (i, 0))),
        out_shape=jax.ShapeDtypeStruct((HALF_ROWS, PACK_W), F32),
        compiler_params=_params("arbitrary"))(k.reshape(1), ps, others)


def _adamw(w, g, m, v):
    m = ADAM_B1 * m + (1.0 - ADAM_B1) * g
    v = ADAM_B2 * v + (1.0 - ADAM_B2) * (g * g)
    m_hat = m / (1.0 - ADAM_B1 ** ADAM_STEP)
    v_hat = v / (1.0 - ADAM_B2 ** ADAM_STEP)
    return -ADAM_LR * (m_hat / (jnp.sqrt(v_hat) + ADAM_EPS) + ADAM_WD * w), m, v


def _small_reduce_adamw(part, w, m, v):
    def body(part_ref, w_ref, m_ref, v_ref, g_ref, d_ref, nm_ref, nv_ref, all_ref, send_sems, recv_sems):
        x, y, c, chips = _place()
        me, sibling = (x, y, c), (x, y, 1 - c)

        def rows(px, py, pc):
            return all_ref.at[4 * px + 2 * py + pc]

        all_ref[4 * x + 2 * y + c] = part_ref[...]
        first = [_remote(part_ref, rows(*me), send_sems, recv_sems, 0, sibling)]
        first += [_remote(part_ref, rows(*me), send_sems, recv_sems, 1 + j, (cx, cy, c))
                  for j, (cx, cy) in enumerate(chips)]
        for cp in first:
            cp.start()
        passed = []
        for j, (cx, cy) in enumerate(chips):
            _remote(part_ref, rows(cx, cy, c), send_sems, recv_sems, 1 + j, me).wait_recv()
            cp = _remote(rows(cx, cy, c), rows(cx, cy, c), send_sems, recv_sems, 4 + j, sibling)
            cp.start()
            passed.append(cp)
        _remote(part_ref, rows(*sibling), send_sems, recv_sems, 0, me).wait_recv()
        for j, (cx, cy) in enumerate(chips):
            _remote(part_ref, rows(cx, cy, 1 - c), send_sems, recv_sems, 4 + j, me).wait_recv()
        for cp in first + passed:
            cp.wait_send()
        g = all_ref[0]
        for dev in range(1, N_DEV):
            g = g + all_ref[dev]
        delta, nm, nv = _adamw(w_ref[...], g, m_ref[...], v_ref[...])
        g_ref[...] = g
        d_ref[...] = delta
        nm_ref[...] = nm
        nv_ref[...] = nv

    whole = pl.BlockSpec(memory_space=pltpu.VMEM)
    shape = jax.ShapeDtypeStruct((SMALL_ROWS, PACK_W), F32)
    return pl.pallas_call(
        body, name="small_reduce_adamw", in_specs=[whole] * 4, out_specs=[whole] * 4, out_shape=[shape] * 4,
        scratch_shapes=[pltpu.VMEM((N_DEV, SMALL_ROWS, PACK_W), F32), pltpu.SemaphoreType.DMA((7,)),
                        pltpu.SemaphoreType.DMA((7,))],
        compiler_params=pltpu.CompilerParams(vmem_limit_bytes=VMEM_LIMIT))(part, w, m, v)


def kernel(x, p, norm_g, w_in, hg_lb, hg_norm_g, w_o_hg, s5_a_re, s5_a_im, s5_log_dt, s5_b_re, s5_b_im, s5_c_re, s5_c_im, s5_d, w_glu, b_glu, w_o_s5, w_out, ple_norm_g, w_ple, w_ple_gate, final_norm_g, loss_target, m_norm_g, m_w_in, m_hg_lb, m_hg_norm_g, m_w_o_hg, m_s5_a_re, m_s5_a_im, m_s5_log_dt, m_s5_b_re, m_s5_b_im, m_s5_c_re, m_s5_c_im, m_s5_d, m_w_glu, m_b_glu, m_w_o_s5, m_w_out, m_ple_norm_g, m_w_ple, m_w_ple_gate, m_final_norm_g, v_norm_g, v_w_in, v_hg_lb, v_hg_norm_g, v_w_o_hg, v_s5_a_re, v_s5_a_im, v_s5_log_dt, v_s5_b_re, v_s5_b_im, v_s5_c_re, v_s5_c_im, v_s5_d, v_w_glu, v_b_glu, v_w_o_s5, v_w_out, v_ple_norm_g, v_w_ple, v_w_ple_gate, v_final_norm_g):
    given = dict(locals())
    wts = {n: given[n] for n in WEIGHTS}
    mom = {n: given["m_" + n] for n in WEIGHTS}
    var = {n: given["v_" + n] for n in WEIGHTS}
    cx, cy, cc = lax.axis_index("x"), lax.axis_index("y"), lax.axis_index("c")
    chip = (2 * cx + cy).astype(jnp.int32)

    w_shard = _pack_shard({n: wts[n][0] for n in BIG})
    gathered = _all_gather_weights(w_shard.astype(MXU_DTYPE).reshape(2, HALF_ROWS, PACK_W))
    w_full = _unpack_full(gathered.reshape(N_CHIPS, SHARD_ROWS, PACK_W))

    t_len = x.shape[1]
    loss_row, grad_x, g_big, g_small = _local_step(x.reshape(t_len, D_MODEL), p.reshape(t_len, -1),
                                                   loss_target.reshape(t_len, D_MODEL), w_full,
                                                   {n: wts[n] for n in SMALL})

    zero = jnp.zeros((), F32)
    sg, sd, snm, snv = _small_reduce_adamw(_pack_small(g_small, loss_row[0, 0]),
                                           _pack_small({n: wts[n] for n in SMALL}, zero),
                                           _pack_small({n: mom[n] for n in SMALL}, zero),
                                           _pack_small({n: var[n] for n in SMALL}, zero))
    (sg, loss), (sd, _), (snm, _), (snv, _) = (_unpack_small(a) for a in (sg, sd, snm, snv))

    pg = _pack_full(g_big)
    ps = _sum_pair(pg, _exchange_halves(pg), cc.astype(jnp.int32))
    g_half = _sum_chips(ps, _scatter_chip_sums(ps), chip)
    g_shard = _share_half(g_half).reshape(SHARD_ROWS, PACK_W)

    def adam_f(wv, gv, mv, vv):
        return _adamw(wv, gv, mv, vv)

    bd, bnm, bnv = _rowwise("adamw_big", adam_f, SHARD_ROWS, REDUCE_ROWS,
                            [(w_shard, PACK_W, 0), (g_shard, PACK_W, 0),
                             (_pack_shard({n: mom[n][0] for n in BIG}), PACK_W, 0),
                             (_pack_shard({n: var[n][0] for n in BIG}), PACK_W, 0)], [],
                            [(PACK_W, F32), (PACK_W, F32), (PACK_W, F32)])
    bg, bd, bnm, bnv = (_unpack_shard(a) for a in (g_shard, bd, bnm, bnv))

    outs = [loss, grad_x.reshape(x.shape)]
    for small, big in ((sg, bg), (sd, bd), (snm, bnm), (snv, bnv)):
        outs += [big[n] if n in BIG else small[n] for n in WEIGHTS]
    return tuple(outs)
```

```python
import functools

import jax
import jax.numpy as jnp
from jax import lax
from jax.experimental import pallas as pl
from jax.experimental.pallas import tpu as pltpu

F32 = jnp.float32
MXU_DTYPE = jnp.bfloat16
WIRE_DTYPE = jnp.bfloat16
NORM_EPS = 1e-6
D_MODEL = 1024
HG_HEADS = 8
HG_DIM = 128
HG_CHUNK = 64
S5_WIDTH = 512
S5_GROUPS = 32
S5_GROUP = 16
S5_STATE = 64
S5_LANES = S5_GROUPS * S5_STATE
IN_COLS = 7168
SUBLANES = 8
VMEM_LIMIT = 56 * 1024 * 1024
HIGHEST = lax.Precision.HIGHEST
MESH = pl.DeviceIdType.MESH

ADAM_LR, ADAM_B1, ADAM_B2, ADAM_EPS, ADAM_WD, ADAM_STEP = 0.001, 0.9, 0.999, 1e-08, 0.01, 10

BIG = ("w_in", "w_o_hg", "w_glu", "w_o_s5", "w_out", "w_ple", "w_ple_gate")
BIG_SHAPE = {"w_in": (1024, 7168), "w_o_hg": (1024, 1024), "w_glu": (512, 1024), "w_o_s5": (512, 1024),
             "w_out": (1024, 1024), "w_ple": (256, 1024), "w_ple_gate": (1024, 1024)}
BIG_COL_SHARDED = ("w_in", "w_glu", "w_o_s5", "w_ple")
SMALL = ("norm_g", "hg_lb", "hg_norm_g", "s5_a_re", "s5_a_im", "s5_log_dt", "s5_b_re", "s5_b_im", "s5_c_re",
         "s5_c_im", "s5_d", "b_glu", "ple_norm_g", "final_norm_g")
SMALL_SHAPE = {"norm_g": (1, 1024), "hg_lb": (2, 1024), "hg_norm_g": (1, 1024), "s5_a_re": (1, 32, 64),
               "s5_a_im": (1, 32, 64), "s5_log_dt": (1, 32), "s5_b_re": (1, 32, 64, 16), "s5_b_im": (1, 32, 64, 16),
               "s5_c_re": (1, 32, 16, 64), "s5_c_im": (1, 32, 16, 64), "s5_d": (1, 32, 16), "b_glu": (1, 1024),
               "ple_norm_g": (1, 1024), "final_norm_g": (1024,)}
WEIGHTS = ("norm_g", "w_in", "hg_lb", "hg_norm_g", "w_o_hg", "s5_a_re", "s5_a_im", "s5_log_dt", "s5_b_re", "s5_b_im",
           "s5_c_re", "s5_c_im", "s5_d", "w_glu", "b_glu", "w_o_s5", "w_out", "ple_norm_g", "w_ple", "w_ple_gate",
           "final_norm_g")
N_CHIPS = 4
N_DEV = 8
PACK_W = 1024
SHARD_ROWS = sum(BIG_SHAPE[n][0] * BIG_SHAPE[n][1] for n in BIG) // (N_CHIPS * PACK_W)
HALF_ROWS = SHARD_ROWS // 2
SMALL_ROWS = 144


def _params(*sem):
    return pltpu.CompilerParams(dimension_semantics=sem, vmem_limit_bytes=VMEM_LIMIT)


def _sig(x):
    return 1.0 / (1.0 + jnp.exp(-x))


def _dsilu(z, s):
    return s * (1.0 + z * (1.0 - s))


def _mx(x):
    return x.astype(MXU_DTYPE)


def _dot(a, b, dims=(((1,), (0,)), ((), ()))):
    return lax.dot_general(_mx(a), _mx(b), dims, preferred_element_type=F32)


_NT = (((1,), (1,)), ((), ()))
_TN = (((0,), (0,)), ((), ()))


def _dot32(a, b):
    return jnp.dot(a, b, precision=HIGHEST, preferred_element_type=F32)


def _rms_bwd(dy, x, g):
    r = lax.rsqrt(jnp.mean(x * x, axis=-1, keepdims=True) + NORM_EPS)
    t = dy * g
    dx = r * t - x * (r * r * r) * jnp.mean(t * x, axis=-1, keepdims=True)
    return dx, jnp.sum(dy * x * r, axis=0, keepdims=True)


def _rowwise(name, fn, n_rows_total, tm, rows, consts, outs, accs=(), alias=None):
    n_r, n_c, n_o, n_a = len(rows), len(consts), len(outs), len(accs)

    def body(*refs):
        row_refs = refs[:n_r]
        const_refs = refs[n_r:n_r + n_c]
        pos = n_r + n_c + (1 if alias is not None else 0)
        out_refs = refs[pos:pos + n_o]
        acc_refs = refs[pos + n_o:pos + n_o + n_a]
        res = fn(*[r[...] for r in row_refs], *[r[...] for r in const_refs])
        for r, v in zip(out_refs, res[:n_o]):
            r[...] = v.astype(r.dtype)
        if n_a:
            @pl.when(pl.program_id(0) == 0)
            def _():
                for r in acc_refs:
                    r[...] = jnp.zeros_like(r)
            for r, v in zip(acc_refs, res[n_o:]):
                r[...] += v

    in_specs = [pl.BlockSpec((tm, w), functools.partial(lambda i, cb: (i, cb), cb=cb)) for (_, w, cb) in rows]
    in_specs += [pl.BlockSpec(c.shape, lambda i: (0, 0)) for c in consts]
    args = [a for (a, _, _) in rows] + list(consts)
    out_shape, out_specs = [], []
    for o in outs:
        w, dt = o[0], o[1]
        cb, total = (o[2], o[3]) if len(o) == 4 else (0, w)
        out_shape.append(jax.ShapeDtypeStruct((n_rows_total, total), dt))
        out_specs.append(pl.BlockSpec((tm, w), functools.partial(lambda i, cb: (i, cb), cb=cb)))
    io_alias = {}
    if alias is not None:
        in_specs.append(pl.BlockSpec(memory_space=pl.ANY))
        args.append(alias[0])
        io_alias = {len(args) - 1: alias[1]}
    for (r, w) in accs:
        out_shape.append(jax.ShapeDtypeStruct((r, w), F32))
        out_specs.append(pl.BlockSpec((r, w), lambda i: (0, 0)))
    res = pl.pallas_call(body, name=name, grid=(n_rows_total // tm,), in_specs=in_specs, out_specs=out_specs,
                         out_shape=out_shape, input_output_aliases=io_alias,
                         compiler_params=_params("arbitrary"))(*args)
    return res


def _mm_nn(name, a, b, tm, tn, bias=None, resid=None):
    m, k = a.shape
    n = b.shape[1]

    def body(*refs):
        acc = _dot(refs[0][...], refs[1][...])
        pos = 2
        if bias is not None:
            acc = acc + refs[pos][...]
            pos += 1
        if resid is not None:
            acc = acc + refs[pos][...]
            pos += 1
        refs[pos][...] = acc

    in_specs = [pl.BlockSpec((tm, k), lambda j, i: (i, 0)), pl.BlockSpec((k, tn), lambda j, i: (0, j))]
    args = [a, b]
    if bias is not None:
        in_specs.append(pl.BlockSpec((1, tn), lambda j, i: (0, j)))
        args.append(bias)
    if resid is not None:
        in_specs.append(pl.BlockSpec((tm, tn), lambda j, i: (i, j)))
        args.append(resid)
    return pl.pallas_call(body, name=name, grid=(n // tn, m // tm), in_specs=in_specs,
                          out_specs=pl.BlockSpec((tm, tn), lambda j, i: (i, j)),
                          out_shape=jax.ShapeDtypeStruct((m, n), F32),
                          compiler_params=_params("arbitrary", "arbitrary"))(*args)


def _mm_nt(name, a, b, tm, tn):
    m, n = a.shape
    k = b.shape[0]
    steps = n // tn

    def body(a_ref, b_ref, o_ref, acc_ref):
        s = pl.program_id(1)

        @pl.when(s == 0)
        def _():
            acc_ref[...] = jnp.zeros_like(acc_ref)

        acc_ref[...] += _dot(a_ref[...], b_ref[...], _NT)

        @pl.when(s == steps - 1)
        def _():
            o_ref[...] = acc_ref[...]

    return pl.pallas_call(body, name=name, grid=(m // tm, steps),
                          in_specs=[pl.BlockSpec((tm, tn), lambda i, s: (i, s)),
                                    pl.BlockSpec((k, tn), lambda i, s: (0, s))],
                          out_specs=pl.BlockSpec((tm, k), lambda i, s: (i, 0)),
                          out_shape=jax.ShapeDtypeStruct((m, k), F32),
                          scratch_shapes=[pltpu.VMEM((tm, k), F32)],
                          compiler_params=_params("arbitrary", "arbitrary"))(a, b)


def _mm_nt_then(name, a, b, tm, tn, fn, rows, consts, outs, accs=(), alias=None):
    m, n = a.shape
    k = b.shape[0]
    steps = n // tn
    n_r, n_c, n_o, n_a = len(rows), len(consts), len(outs), len(accs)

    def body(*refs):
        a_ref, b_ref = refs[:2]
        row_refs = refs[2:2 + n_r]
        const_refs = refs[2 + n_r:2 + n_r + n_c]
        pos = 2 + n_r + n_c + (1 if alias is not None else 0)
        out_refs = refs[pos:pos + n_o]
        acc_refs = refs[pos + n_o:pos + n_o + n_a]
        mm_ref = refs[pos + n_o + n_a]
        i, s = pl.program_id(0), pl.program_id(1)
        part = _dot(a_ref[...], b_ref[...], _NT)
        if steps > 1:
            @pl.when(s == 0)
            def _():
                mm_ref[...] = jnp.zeros_like(mm_ref)
            mm_ref[...] += part

        @pl.when(s == steps - 1)
        def _():
            res = fn(mm_ref[...] if steps > 1 else part, *[r[...] for r in row_refs], *[r[...] for r in const_refs])
            for r, v in zip(out_refs, res[:n_o]):
                r[...] = v.astype(r.dtype)
            if n_a:
                @pl.when(i == 0)
                def _():
                    for r in acc_refs:
                        r[...] = jnp.zeros_like(r)
                for r, v in zip(acc_refs, res[n_o:]):
                    r[...] += v

    in_specs = [pl.BlockSpec((tm, tn), lambda i, s: (i, s)), pl.BlockSpec((k, tn), lambda i, s: (0, s))]
    in_specs += [pl.BlockSpec((tm, w), functools.partial(lambda i, s, cb: (i, cb), cb=cb)) for (_, w, cb) in rows]
    in_specs += [pl.BlockSpec(c.shape, lambda i, s: (0, 0)) for c in consts]
    args = [a, b] + [r[0] for r in rows] + list(consts)
    out_shape, out_specs = [], []
    for o in outs:
        w, dt = o[0], o[1]
        cb, total = (o[2], o[3]) if len(o) == 4 else (0, w)
        out_shape.append(jax.ShapeDtypeStruct((m, total), dt))
        out_specs.append(pl.BlockSpec((tm, w), functools.partial(lambda i, s, cb: (i, cb), cb=cb)))
    io_alias = {}
    if alias is not None:
        in_specs.append(pl.BlockSpec(memory_space=pl.ANY))
        args.append(alias[0])
        io_alias = {len(args) - 1: alias[1]}
    for (r, w) in accs:
        out_shape.append(jax.ShapeDtypeStruct((r, w), F32))
        out_specs.append(pl.BlockSpec((r, w), lambda i, s: (0, 0)))
    return pl.pallas_call(body, name=name, grid=(m // tm, steps), in_specs=in_specs, out_specs=out_specs,
                          out_shape=out_shape, input_output_aliases=io_alias,
                          scratch_shapes=[pltpu.VMEM((tm, k), F32)],
                          compiler_params=_params("arbitrary", "arbitrary"))(*args)


def _mm_tn(name, a, b, tk, tn):
    t, k = a.shape
    n = b.shape[1]
    steps = t // tk

    def body(a_ref, b_ref, o_ref, acc_ref):
        s = pl.program_id(1)

        @pl.when(s == 0)
        def _():
            acc_ref[...] = jnp.zeros_like(acc_ref)

        acc_ref[...] += _dot(a_ref[...], b_ref[...], _TN)

        @pl.when(s == steps - 1)
        def _():
            o_ref[...] = acc_ref[...]

    return pl.pallas_call(body, name=name, grid=(n // tn, steps),
                          in_specs=[pl.BlockSpec((tk, k), lambda j, s: (s, 0)),
                                    pl.BlockSpec((tk, tn), lambda j, s: (s, j))],
                          out_specs=pl.BlockSpec((k, tn), lambda j, s: (0, j)),
                          out_shape=jax.ShapeDtypeStruct((k, n), F32),
                          scratch_shapes=[pltpu.VMEM((k, tn), F32)],
                          compiler_params=_params("arbitrary", "arbitrary"))(a, b)


def _hg_chunk_terms(q, f, lb):
    sig = _sig(f)
    fv = lb + (1.0 - lb) * sig
    kk = (1.0 - lb) * (1.0 - sig)
    row = lax.broadcasted_iota(jnp.int32, (HG_CHUNK, HG_CHUNK), 0)
    col = lax.broadcasted_iota(jnp.int32, (HG_CHUNK, HG_CHUNK), 1)
    b = _dot32((row >= col).astype(F32), jnp.log(fv))
    b_mid = b[HG_CHUNK // 2 - 1:HG_CHUNK // 2, :]
    b_last = b[HG_CHUNK - 1:HG_CHUNK, :]
    e_mid = jnp.exp(b - b_mid)
    e_mid_inv = jnp.exp(b_mid - b)
    e_b = jnp.exp(b)
    e_last = jnp.exp(b_last - b)
    return sig, fv, kk, row >= col, row <= col, q * e_mid, kk * e_mid_inv, e_mid, e_mid_inv, e_b, e_last, jnp.exp(b_last)


def _hgrn2_fwd(proj, hg_lb, hg_norm_g, t_len, tb):
    nck = tb // HG_CHUNK

    def body(p_ref, lb_ref, gn_ref, o_ref, act_ref, sp_ref, st_ref):
        @pl.when(pl.program_id(0) == 0)
        def _():
            st_ref[...] = jnp.zeros_like(st_ref)

        for c in range(nck):
            r = pl.ds(c * HG_CHUNK, HG_CHUNK)
            for h in range(HG_HEADS):
                hs = pl.ds(h * HG_DIM, HG_DIM)
                lb = _sig(lb_ref[0:1, hs] - lb_ref[1:2, hs])
                q = p_ref[r, pl.ds(h * HG_DIM, HG_DIM)]
                f = p_ref[r, pl.ds(1024 + h * HG_DIM, HG_DIM)]
                v = p_ref[r, pl.ds(2048 + h * HG_DIM, HG_DIM)]
                _, _, kk, causal, _, a, bm, _, _, e_b, e_last, dc = _hg_chunk_terms(q, f, lb)
                scores = jnp.where(causal, _dot(a, bm, _NT), 0.0)
                st = st_ref[h]
                o = _dot(scores, v) + _dot(q * e_b, st, _NT)
                sp_ref[h, c] = st
                st_ref[h] = dc * st + _dot(v, kk * e_last, _TN)
                o_ref[r, hs] = o

        for h in range(HG_HEADS):
            hs = pl.ds(h * HG_DIM, HG_DIM)
            o = o_ref[:, hs]
            rr = lax.rsqrt(jnp.mean(o * o, axis=-1, keepdims=True) + NORM_EPS)
            g = p_ref[:, pl.ds(3072 + h * HG_DIM, HG_DIM)]
            act_ref[:, hs] = (o * rr * gn_ref[:, hs] * (g * _sig(g))).astype(act_ref.dtype)

    nb = t_len // tb
    return pl.pallas_call(
        body, name="hgrn2_fwd", grid=(nb,),
        in_specs=[pl.BlockSpec((tb, 4096), lambda i: (i, 0)),
                  pl.BlockSpec((2, 1024), lambda i: (0, 0)),
                  pl.BlockSpec((1, 1024), lambda i: (0, 0))],
        out_specs=[pl.BlockSpec((tb, 1024), lambda i: (i, 0)),
                   pl.BlockSpec((tb, 1024), lambda i: (i, 0)),
                   pl.BlockSpec((HG_HEADS, nck, HG_DIM, HG_DIM), lambda i: (0, i, 0, 0))],
        out_shape=[jax.ShapeDtypeStruct((t_len, 1024), F32),
                   jax.ShapeDtypeStruct((t_len, 1024), MXU_DTYPE),
                   jax.ShapeDtypeStruct((HG_HEADS, t_len // HG_CHUNK, HG_DIM, HG_DIM), F32)],
        scratch_shapes=[pltpu.VMEM((HG_HEADS, HG_DIM, HG_DIM), F32)],
        compiler_params=_params("arbitrary"))(proj, hg_lb, hg_norm_g)


def _hgrn2_bwd(proj, d_o, s_prev, hg_lb, dproj, t_len, tb):
    nck = tb // HG_CHUNK
    nb = t_len // tb

    def body(p_ref, do_ref, sp_ref, lb_ref, _, dp_ref, dlb_ref, ds_ref, acc_ref):
        @pl.when(pl.program_id(0) == 0)
        def _():
            ds_ref[...] = jnp.zeros_like(ds_ref)
            acc_ref[...] = jnp.zeros_like(acc_ref)

        for c in reversed(range(nck)):
            r = pl.ds(c * HG_CHUNK, HG_CHUNK)
            for h in range(HG_HEADS):
                hs = pl.ds(h * HG_DIM, HG_DIM)
                lb = _sig(lb_ref[0:1, hs] - lb_ref[1:2, hs])
                q = p_ref[r, pl.ds(h * HG_DIM, HG_DIM)]
                f = p_ref[r, pl.ds(1024 + h * HG_DIM, HG_DIM)]
                v = p_ref[r, pl.ds(2048 + h * HG_DIM, HG_DIM)]
                do = do_ref[r, hs]
                sig, fv, kk, causal, anti, a, bm, e_mid, e_mid_inv, e_b, e_last, dc = _hg_chunk_terms(q, f, lb)
                qd = q * e_b
                kd = kk * e_last
                st = sp_ref[h, c]
                dst = ds_ref[h]
                scores = jnp.where(causal, _dot(a, bm, _NT), 0.0)
                dscores = jnp.where(causal, _dot(do, v, _NT), 0.0)
                dv = _dot(scores, do, _TN) + _dot(kd, dst, _NT)
                da = _dot(dscores, bm)
                dbm = _dot(dscores, a, _TN)
                dqd = _dot(do, st)
                dkd = _dot(v, dst)
                ddc = jnp.sum(dst * st, axis=0, keepdims=True)
                ds_ref[h] = _dot(do, qd, _TN) + dc * dst
                dq = da * e_mid + dqd * e_b
                dk = dbm * e_mid_inv + dkd * e_last
                db = da * a - dbm * bm + dqd * qd - dkd * kd
                extra = jnp.sum(dkd * kd, axis=0, keepdims=True) + ddc * dc
                dlogf = _dot32(anti.astype(F32), db) + extra
                dfv_k = dlogf / fv - dk
                dp_ref[r, pl.ds(h * HG_DIM, HG_DIM)] = dq
                dp_ref[r, pl.ds(1024 + h * HG_DIM, HG_DIM)] = dfv_k * (1.0 - lb) * sig * (1.0 - sig)
                dp_ref[r, pl.ds(2048 + h * HG_DIM, HG_DIM)] = dv
                acc_ref[:, hs] += jnp.sum(dfv_k * (1.0 - sig), axis=0, keepdims=True)

        @pl.when(pl.program_id(0) == nb - 1)
        def _():
            lb_all = _sig(lb_ref[0:1, :] - lb_ref[1:2, :])
            g0 = acc_ref[...] * lb_all * (1.0 - lb_all)
            dlb_ref[0:1, :] = g0
            dlb_ref[1:2, :] = -g0

    return pl.pallas_call(
        body, name="hgrn2_bwd", grid=(nb,),
        in_specs=[pl.BlockSpec((tb, 3072), lambda i: (nb - 1 - i, 0)),
                  pl.BlockSpec((tb, 1024), lambda i: (nb - 1 - i, 0)),
                  pl.BlockSpec((HG_HEADS, nck, HG_DIM, HG_DIM), lambda i: (0, nb - 1 - i, 0, 0)),
                  pl.BlockSpec((2, 1024), lambda i: (0, 0)),
                  pl.BlockSpec(memory_space=pl.ANY)],
        out_specs=[pl.BlockSpec((tb, 3072), lambda i: (nb - 1 - i, 0)),
                   pl.BlockSpec((2, 1024), lambda i: (0, 0))],
        out_shape=[jax.ShapeDtypeStruct((t_len, IN_COLS), F32), jax.ShapeDtypeStruct((2, 1024), F32)],
        scratch_shapes=[pltpu.VMEM((HG_HEADS, HG_DIM, HG_DIM), F32), pltpu.VMEM((1, 1024), F32)],
        input_output_aliases={4: 0},
        compiler_params=_params("arbitrary"))(proj, d_o, s_prev, hg_lb, dproj)


def _s5_prep(a_re, a_im, log_dt, b_re_t, b_im_t):
    def body(ar_ref, ai_ref, ldt_ref, br_ref, bi_ref, lam_ref, pr_ref, pi_ref, bbr_ref, bbi_ref):
        ar, ai = ar_ref[...], ai_ref[...]
        dt = jnp.exp(ldt_ref[...])
        mag = jnp.exp(ar * dt)
        lr, li = mag * jnp.cos(ai * dt), mag * jnp.sin(ai * dt)
        den = ar * ar + ai * ai
        nr = lr - 1.0
        sr = (nr * ar + li * ai) / den
        si = (li * ar - nr * ai) / den
        lam_ref[0:1, :] = lr
        lam_ref[1:2, :] = li
        cr, ci = lr, li
        for i in range(SUBLANES):
            pr_ref[i:i + 1, :] = cr
            pi_ref[i:i + 1, :] = ci
            cr, ci = cr * lr - ci * li, cr * li + ci * lr
        br, bi = br_ref[...], bi_ref[...]
        bbr_ref[...] = sr * br - si * bi
        bbi_ref[...] = sr * bi + si * br

    whole = pl.BlockSpec(memory_space=pltpu.VMEM)
    return pl.pallas_call(
        body, name="s5_prep", in_specs=[whole] * 5, out_specs=[whole] * 5,
        out_shape=[jax.ShapeDtypeStruct((2, S5_LANES), F32), jax.ShapeDtypeStruct((SUBLANES, S5_LANES), F32),
                   jax.ShapeDtypeStruct((SUBLANES, S5_LANES), F32), jax.ShapeDtypeStruct((S5_GROUP, S5_LANES), F32),
                   jax.ShapeDtypeStruct((S5_GROUP, S5_LANES), F32)])(a_re, a_im, log_dt, b_re_t, b_im_t)


def _s5_prep_bwd(a_re, a_im, log_dt, b_re_t, b_im_t, dlam, dbbr, dbbi):
    def body(ar_ref, ai_ref, ldt_ref, br_ref, bi_ref, dlam_ref, dbbr_ref, dbbi_ref,
             dar_ref, dai_ref, dldt_ref, dbr_ref, dbi_ref):
        ar, ai = ar_ref[...], ai_ref[...]
        dt = jnp.exp(ldt_ref[...])
        mag = jnp.exp(ar * dt)
        cs, sn = jnp.cos(ai * dt), jnp.sin(ai * dt)
        lr, li = mag * cs, mag * sn
        den = ar * ar + ai * ai
        nr = lr - 1.0
        sr = (nr * ar + li * ai) / den
        si = (li * ar - nr * ai) / den
        br, bi = br_ref[...], bi_ref[...]
        gbr, gbi = dbbr_ref[...], dbbi_ref[...]
        dbr_ref[...] = sr * gbr + si * gbi
        dbi_ref[...] = sr * gbi - si * gbr
        dsr = jnp.sum(gbr * br + gbi * bi, axis=0, keepdims=True)
        dsi = jnp.sum(gbi * br - gbr * bi, axis=0, keepdims=True)
        dnr = (dsr * ar - dsi * ai) / den
        dli = dlam_ref[1:2, :] + (dsr * ai + dsi * ar) / den
        dlr = dlam_ref[0:1, :] + dnr
        dden = -(dsr * sr + dsi * si) / den
        dar = (dsr * nr + dsi * li) / den + dden * 2.0 * ar
        dai = (dsr * li - dsi * nr) / den + dden * 2.0 * ai
        dmag = dlr * cs + dli * sn
        dth = mag * (dli * cs - dlr * sn)
        dar_ref[...] = dar + dmag * mag * dt
        dai_ref[...] = dai + dth * dt
        ddt = (dmag * mag * ar + dth * ai) * dt
        lane = lax.broadcasted_iota(jnp.int32, (S5_LANES, 128), 0) // S5_STATE
        grp = lax.broadcasted_iota(jnp.int32, (S5_LANES, 128), 1)
        dldt_ref[...] = _dot32(jnp.broadcast_to(ddt, (SUBLANES, S5_LANES)), (lane == grp).astype(F32))

    whole = pl.BlockSpec(memory_space=pltpu.VMEM)
    return pl.pallas_call(
        body, name="s5_prep_bwd", in_specs=[whole] * 8, out_specs=[whole] * 5,
        out_shape=[jax.ShapeDtypeStruct((1, S5_LANES), F32), jax.ShapeDtypeStruct((1, S5_LANES), F32),
                   jax.ShapeDtypeStruct((SUBLANES, 128), F32), jax.ShapeDtypeStruct((S5_GROUP, S5_LANES), F32),
                   jax.ShapeDtypeStruct((S5_GROUP, S5_LANES), F32)])(a_re, a_im, log_dt, b_re_t, b_im_t, dlam, dbbr,
                                                                      dbbi)


S5_LANE_CHUNK = 512


def _shift_rows(x, s, rowid):
    if s > 0:
        return jnp.where(rowid >= s, pltpu.roll(x, s, 0), 0.0)
    return jnp.where(rowid < SUBLANES + s, pltpu.roll(x, SUBLANES + s, 0), 0.0)


def _scan8(xr, xi, pr, pi, sign, rowid):
    for s, row in ((1, 0), (2, 1), (4, 3)):
        lr, li = pr[row:row + 1, :], pi[row:row + 1, :]
        sr, si = _shift_rows(xr, sign * s, rowid), _shift_rows(xi, sign * s, rowid)
        xr, xi = xr + lr * sr - li * si, xi + lr * si + li * sr
    return xr, xi


def _s5_fwd(proj, pw_re, pw_im, bbr_bd, bbi_bd, crt_bd, cit_bd, d_row, t_len, tb):
    ngrp = tb // SUBLANES

    def body(u_ref, pr_ref, pi_ref, bbr_ref, bbi_ref, crt_ref, cit_ref, d_ref,
             hr_ref, hi_ref, ypre_ref, ys_ref, cr_ref, ci_ref):
        @pl.when(pl.program_id(0) == 0)
        def _():
            cr_ref[...] = jnp.zeros_like(cr_ref)
            ci_ref[...] = jnp.zeros_like(ci_ref)

        u = u_ref[...]
        hr_ref[...] = _dot(u, bbr_ref[...])
        hi_ref[...] = _dot(u, bbi_ref[...])
        rowid = lax.broadcasted_iota(jnp.int32, (SUBLANES, S5_LANE_CHUNK), 0)
        for lc in range(S5_LANES // S5_LANE_CHUNK):
            ls = pl.ds(lc * S5_LANE_CHUNK, S5_LANE_CHUNK)
            pr, pi = pr_ref[:, ls], pi_ref[:, ls]

            def group(g, carry, ls=ls, pr=pr, pi=pi):
                cr, ci = carry
                r = pl.ds(pl.multiple_of(g * SUBLANES, SUBLANES), SUBLANES)
                xr, xi = _scan8(hr_ref[r, ls], hi_ref[r, ls], pr, pi, 1, rowid)
                xr, xi = xr + pr * cr - pi * ci, xi + pr * ci + pi * cr
                hr_ref[r, ls] = xr
                hi_ref[r, ls] = xi
                return xr[SUBLANES - 1:SUBLANES, :], xi[SUBLANES - 1:SUBLANES, :]

            cr, ci = lax.fori_loop(0, ngrp, group, (cr_ref[:, ls], ci_ref[:, ls]))
            cr_ref[:, ls] = cr
            ci_ref[:, ls] = ci
        y = _dot(hr_ref[...], crt_ref[...]) - _dot(hi_ref[...], cit_ref[...]) + d_ref[...] * u
        ypre_ref[...] = y
        ys_ref[...] = jax.nn.gelu(y, approximate=True).astype(ys_ref.dtype)

    whole = pl.BlockSpec(memory_space=pltpu.VMEM)
    return pl.pallas_call(
        body, name="s5_fwd", grid=(t_len // tb,),
        in_specs=[pl.BlockSpec((tb, S5_WIDTH), lambda i: (i, 4096 // S5_WIDTH))] + [whole] * 7,
        out_specs=[pl.BlockSpec((tb, S5_LANES), lambda i: (i, 0)), pl.BlockSpec((tb, S5_LANES), lambda i: (i, 0)),
                   pl.BlockSpec((tb, S5_WIDTH), lambda i: (i, 0)), pl.BlockSpec((tb, S5_WIDTH), lambda i: (i, 0))],
        out_shape=[jax.ShapeDtypeStruct((t_len, S5_LANES), F32), jax.ShapeDtypeStruct((t_len, S5_LANES), F32),
                   jax.ShapeDtypeStruct((t_len, S5_WIDTH), F32), jax.ShapeDtypeStruct((t_len, S5_WIDTH), MXU_DTYPE)],
        scratch_shapes=[pltpu.VMEM((1, S5_LANES), F32), pltpu.VMEM((1, S5_LANES), F32)],
        compiler_params=_params("arbitrary"))(proj, pw_re, pw_im, bbr_bd, bbi_bd, crt_bd, cit_bd, d_row)


def _dgelu(x):
    c, a = 0.7978845608028654, 0.044715
    th = jnp.tanh(c * (x + a * x * x * x))
    return 0.5 * (1.0 + th) + 0.5 * x * (1.0 - th * th) * c * (1.0 + 3.0 * a * x * x)


def _s5_bwd(dgelu, y_pre, proj, h_re, h_im, pwr_re, pwr_im, bbr_bd, bbi_bd, cr_bd, ci_bd, d_row, dproj, t_len, tb):
    ngrp = tb // SUBLANES
    nb = t_len // tb

    def body(dg_ref, yp_ref, u_ref, hr_ref, hi_ref, pr_ref, pi_ref, bbr_ref, bbi_ref, cr_ref, ci_ref, d_ref, _,
             du_ref, dbbr_ref, dbbi_ref, dcr_ref, dci_ref, dd_ref, dlam_ref,
             gr_ref, gi_ref, car_ref, cai_ref, abr_ref, abi_ref, acr_ref, aci_ref, ad_ref, alr_ref, ali_ref, sem):
        @pl.when(pl.program_id(0) == 0)
        def _():
            for ref in (car_ref, cai_ref, abr_ref, abi_ref, acr_ref, aci_ref, ad_ref, alr_ref, ali_ref):
                ref[...] = jnp.zeros_like(ref)

        u = u_ref[...]
        dy = dg_ref[...] * _dgelu(yp_ref[...])
        gr_ref[...] = _dot(dy, cr_ref[...])
        gi_ref[...] = -_dot(dy, ci_ref[...])
        rowid = lax.broadcasted_iota(jnp.int32, (SUBLANES, S5_LANE_CHUNK), 0)
        for lc in range(S5_LANES // S5_LANE_CHUNK):
            ls = pl.ds(lc * S5_LANE_CHUNK, S5_LANE_CHUNK)
            pr, pi = pr_ref[:, ls], pi_ref[:, ls]
            fwd_rows_r = jnp.concatenate([pr[7:8], pr[6:7], pr[6:7], pr[4:5]], axis=0)
            fwd_rows_i = jnp.concatenate([pi[7:8], pi[6:7], pi[6:7], pi[4:5]], axis=0)

            def group(j, carry, ls=ls, pr=pr, pi=pi, fr=fwd_rows_r, fi=fwd_rows_i):
                cr, ci, slr, sli = carry
                g = ngrp - 1 - j
                r = pl.ds(pl.multiple_of(g * SUBLANES, SUBLANES), SUBLANES)
                xr, xi = _scan8(gr_ref[r, ls], gi_ref[r, ls], fr, fi, -1, rowid)
                xr, xi = xr + pr * cr - pi * ci, xi + pr * ci + pi * cr
                gr_ref[r, ls] = xr
                gi_ref[r, ls] = xi
                nr = jnp.where(rowid == SUBLANES - 1, cr, pltpu.roll(xr, SUBLANES - 1, 0))
                ni = jnp.where(rowid == SUBLANES - 1, ci, pltpu.roll(xi, SUBLANES - 1, 0))
                hr, hi = hr_ref[r, ls], hi_ref[r, ls]
                slr = slr + nr * hr + ni * hi
                sli = sli + ni * hr - nr * hi
                return xr[0:1, :], xi[0:1, :], slr, sli

            zero = jnp.zeros((SUBLANES, S5_LANE_CHUNK), F32)
            cr, ci, slr, sli = lax.fori_loop(0, ngrp, group, (car_ref[:, ls], cai_ref[:, ls], zero, zero))
            car_ref[:, ls] = cr
            cai_ref[:, ls] = ci
            alr_ref[:, ls] += jnp.sum(slr, axis=0, keepdims=True)
            ali_ref[:, ls] += jnp.sum(sli, axis=0, keepdims=True)
        gr, gi = gr_ref[...], gi_ref[...]
        du_ref[...] = _dot(gr, bbr_ref[...], _NT) + _dot(gi, bbi_ref[...], _NT) + d_ref[...] * dy
        abr_ref[...] += _dot(u, gr, _TN)
        abi_ref[...] += _dot(u, gi, _TN)
        acr_ref[...] += _dot(hr_ref[...], dy, _TN)
        aci_ref[...] -= _dot(hi_ref[...], dy, _TN)
        ad_ref[...] += jnp.sum(dy * u, axis=0, keepdims=True)

        @pl.when(pl.program_id(0) == nb - 1)
        def _():
            dd_ref[...] = ad_ref[...]
            dlam_ref[0:1, :] = alr_ref[...]
            dlam_ref[1:2, :] = ali_ref[...]
            copies = [pltpu.make_async_copy(s, d, sem.at[k]) for k, (s, d) in enumerate(
                ((abr_ref, dbbr_ref), (abi_ref, dbbi_ref), (acr_ref, dcr_ref), (aci_ref, dci_ref)))]
            for cp in copies:
                cp.start()
            for cp in copies:
                cp.wait()

    whole = pl.BlockSpec(memory_space=pltpu.VMEM)
    hbm = pl.BlockSpec(memory_space=pl.ANY)
    rev = lambda i: (nb - 1 - i, 0)
    return pl.pallas_call(
        body, name="s5_bwd", grid=(nb,),
        in_specs=[pl.BlockSpec((tb, S5_WIDTH), rev), pl.BlockSpec((tb, S5_WIDTH), rev),
                  pl.BlockSpec((tb, S5_WIDTH), lambda i: (nb - 1 - i, 4096 // S5_WIDTH)),
                  pl.BlockSpec((tb, S5_LANES), rev), pl.BlockSpec((tb, S5_LANES), rev)] + [whole] * 7 + [hbm],
        out_specs=[pl.BlockSpec((tb, S5_WIDTH), lambda i: (nb - 1 - i, 4096 // S5_WIDTH)), hbm, hbm, hbm, hbm,
                   pl.BlockSpec((1, S5_WIDTH), lambda i: (0, 0)), pl.BlockSpec((2, S5_LANES), lambda i: (0, 0))],
        out_shape=[jax.ShapeDtypeStruct((t_len, IN_COLS), F32),
                   jax.ShapeDtypeStruct((S5_WIDTH, S5_LANES), F32), jax.ShapeDtypeStruct((S5_WIDTH, S5_LANES), F32),
                   jax.ShapeDtypeStruct((S5_LANES, S5_WIDTH), F32), jax.ShapeDtypeStruct((S5_LANES, S5_WIDTH), F32),
                   jax.ShapeDtypeStruct((1, S5_WIDTH), F32), jax.ShapeDtypeStruct((2, S5_LANES), F32)],
        scratch_shapes=[pltpu.VMEM((tb, S5_LANES), F32), pltpu.VMEM((tb, S5_LANES), F32),
                        pltpu.VMEM((1, S5_LANES), F32), pltpu.VMEM((1, S5_LANES), F32),
                        pltpu.VMEM((S5_WIDTH, S5_LANES), F32), pltpu.VMEM((S5_WIDTH, S5_LANES), F32),
                        pltpu.VMEM((S5_LANES, S5_WIDTH), F32), pltpu.VMEM((S5_LANES, S5_WIDTH), F32),
                        pltpu.VMEM((1, S5_WIDTH), F32), pltpu.VMEM((1, S5_LANES), F32),
                        pltpu.VMEM((1, S5_LANES), F32), pltpu.SemaphoreType.DMA((4,))],
        input_output_aliases={12: 0},
        compiler_params=_params("arbitrary"))(dgelu, y_pre, proj, h_re, h_im, pwr_re, pwr_im, bbr_bd, bbi_bd, cr_bd,
                                              ci_bd, d_row, dproj)


S5_BLOCKS = 4
S5_BW = S5_WIDTH // S5_BLOCKS
S5_BL = S5_LANES // S5_BLOCKS
S5_LANE_BLOCKS = S5_LANES // 128
S5_SCAN_BLOCKS = 4


def _s5_powers(a_re, a_im, log_dt, b_re_t, b_im_t, seg):
    def body(ar_ref, ai_ref, ldt_ref, br_ref, bi_ref, pr_ref, pi_ref, bbr_ref, bbi_ref):
        ar, ai = ar_ref[...], ai_ref[...]
        dt = jnp.exp(ldt_ref[...])
        mag = jnp.exp(ar * dt)
        lr, li = mag * jnp.cos(ai * dt), mag * jnp.sin(ai * dt)
        den = ar * ar + ai * ai
        nr = lr - 1.0
        sr = (nr * ar + li * ai) / den
        si = (li * ar - nr * ai) / den
        cr, ci = lr, li
        for i in range(seg):
            pr_ref[i:i + 1, :] = cr
            pi_ref[i:i + 1, :] = ci
            cr, ci = cr * lr - ci * li, cr * li + ci * lr
        br, bi = br_ref[...], bi_ref[...]
        bbr_ref[...] = sr * br - si * bi
        bbi_ref[...] = sr * bi + si * br

    whole = pl.BlockSpec(memory_space=pltpu.VMEM)
    return pl.pallas_call(
        body, name="s5_prep", in_specs=[whole] * 5, out_specs=[whole] * 4,
        out_shape=[jax.ShapeDtypeStruct((seg, S5_LANES), F32), jax.ShapeDtypeStruct((seg, S5_LANES), F32),
                   jax.ShapeDtypeStruct((S5_GROUP, S5_LANES), F32),
                   jax.ShapeDtypeStruct((S5_GROUP, S5_LANES), F32)])(a_re, a_im, log_dt, b_re_t, b_im_t)


def _scan_tables(pw_re, pw_im, reverse):
    seg = pw_re.shape[0]
    if reverse:
        pw_re, pw_im = pw_re[::-1], -pw_im[::-1]
        one, full = seg - 1, 0
    else:
        one, full = 0, seg - 1
    rows = jnp.stack([pw_re[one], pw_im[one], pw_re[full], pw_im[full]])
    wide = lambda t: jnp.broadcast_to(t[:, None, :], (seg, SUBLANES, S5_LANES))
    return rows, wide(pw_re), wide(pw_im)


def _lanes(j):
    return pl.ds(j * 128, 128)


def _segment_scan(xr_ref, xi_ref, lam_ref, car_ref, cai_ref, cn_r, cn_i, blocks, seg, reverse):
    shape = (SUBLANES, 128)
    lrs = [jnp.broadcast_to(lam_ref[0:1, _lanes(j)], shape) for j in blocks]
    lis = [jnp.broadcast_to(lam_ref[1:2, _lanes(j)], shape) for j in blocks]

    def step(k, carry):
        idx = pl.ds(seg - 1 - k if reverse else k, SUBLANES, stride=seg)
        out = []
        for n, j in enumerate(blocks):
            cr, ci = carry[2 * n], carry[2 * n + 1]
            nr = lrs[n] * cr - lis[n] * ci + xr_ref[j, idx, :]
            ni = lrs[n] * ci + lis[n] * cr + xi_ref[j, idx, :]
            xr_ref[j, idx, :] = nr
            xi_ref[j, idx, :] = ni
            out += [nr, ni]
        return tuple(out)

    zero = jnp.zeros(shape, F32)
    fin = lax.fori_loop(0, seg, step, (zero,) * (2 * len(blocks)), unroll=2)
    for n, j in enumerate(blocks):
        ls = _lanes(j)
        fr, fi = fin[2 * n], fin[2 * n + 1]
        sr, si = lam_ref[2:3, ls], lam_ref[3:4, ls]
        pr, pi = car_ref[:, ls], cai_ref[:, ls]
        for s in (reversed(range(SUBLANES)) if reverse else range(SUBLANES)):
            cn_r[s:s + 1, ls] = pr
            cn_i[s:s + 1, ls] = pi
            pr, pi = fr[s:s + 1, :] + sr * pr - si * pi, fi[s:s + 1, :] + sr * pi + si * pr
        car_ref[:, ls] = pr
        cai_ref[:, ls] = pi


def _s5_fwd2(proj, lam_rows, p3_re, p3_im, bbr4, bbi4, crt4, cit4, d_row, t_len, tb):
    seg = tb // SUBLANES

    def body(u_ref, lam_ref, p3r_ref, p3i_ref, bbr_ref, bbi_ref, crt_ref, cit_ref, d_ref,
             hr_ref, hi_ref, ypre_ref, ys_ref, car_ref, cai_ref, cn_r, cn_i):
        @pl.when(pl.program_id(0) == 0)
        def _():
            car_ref[...] = jnp.zeros_like(car_ref)
            cai_ref[...] = jnp.zeros_like(cai_ref)

        u = u_ref[...]
        for i in range(S5_BLOCKS):
            ui = u[:, i * S5_BW:(i + 1) * S5_BW]
            xr, xi = _dot(ui, bbr_ref[i]), _dot(ui, bbi_ref[i])
            for jj in range(S5_BL // 128):
                hr_ref[i * (S5_BL // 128) + jj] = xr[:, jj * 128:(jj + 1) * 128]
                hi_ref[i * (S5_BL // 128) + jj] = xi[:, jj * 128:(jj + 1) * 128]
        for lc in range(S5_LANE_BLOCKS // S5_SCAN_BLOCKS):
            blocks = range(lc * S5_SCAN_BLOCKS, (lc + 1) * S5_SCAN_BLOCKS)
            _segment_scan(hr_ref, hi_ref, lam_ref, car_ref, cai_ref, cn_r, cn_i, blocks, seg, False)
            crs = [cn_r[:, _lanes(j)] for j in blocks]
            cis = [cn_i[:, _lanes(j)] for j in blocks]

            def fix(t, carry, blocks=blocks, crs=crs, cis=cis):
                idx = pl.ds(t, SUBLANES, stride=seg)
                for n, j in enumerate(blocks):
                    pr, pi = p3r_ref[t, :, _lanes(j)], p3i_ref[t, :, _lanes(j)]
                    hr_ref[j, idx, :] += pr * crs[n] - pi * cis[n]
                    hi_ref[j, idx, :] += pr * cis[n] + pi * crs[n]
                return carry

            lax.fori_loop(0, seg, fix, 0, unroll=2)
        for i in range(S5_BLOCKS):
            ws = pl.ds(i * S5_BW, S5_BW)
            js = range(i * (S5_BL // 128), (i + 1) * (S5_BL // 128))
            hr = jnp.concatenate([hr_ref[j] for j in js], axis=1)
            hi = jnp.concatenate([hi_ref[j] for j in js], axis=1)
            y = _dot(hr, crt_ref[i]) - _dot(hi, cit_ref[i]) + d_ref[:, ws] * u[:, i * S5_BW:(i + 1) * S5_BW]
            ypre_ref[:, ws] = y
            ys_ref[:, ws] = jax.nn.gelu(y, approximate=True).astype(ys_ref.dtype)

    whole = pl.BlockSpec(memory_space=pltpu.VMEM)
    h_spec = pl.BlockSpec((S5_LANE_BLOCKS, tb, 128), lambda i: (0, i, 0))
    return pl.pallas_call(
        body, name="s5_fwd", grid=(t_len // tb,),
        in_specs=[pl.BlockSpec((tb, S5_WIDTH), lambda i: (i, 4096 // S5_WIDTH))] + [whole] * 8,
        out_specs=[h_spec, h_spec,
                   pl.BlockSpec((tb, S5_WIDTH), lambda i: (i, 0)), pl.BlockSpec((tb, S5_WIDTH), lambda i: (i, 0))],
        out_shape=[jax.ShapeDtypeStruct((S5_LANE_BLOCKS, t_len, 128), F32),
                   jax.ShapeDtypeStruct((S5_LANE_BLOCKS, t_len, 128), F32),
                   jax.ShapeDtypeStruct((t_len, S5_WIDTH), F32), jax.ShapeDtypeStruct((t_len, S5_WIDTH), MXU_DTYPE)],
        scratch_shapes=[pltpu.VMEM((1, S5_LANES), F32), pltpu.VMEM((1, S5_LANES), F32),
                        pltpu.VMEM((SUBLANES, S5_LANES), F32), pltpu.VMEM((SUBLANES, S5_LANES), F32)],
        compiler_params=_params("arbitrary"))(proj, lam_rows, p3_re, p3_im, bbr4, bbi4, crt4, cit4, d_row)


def _s5_bwd2(dgelu, y_pre, proj, h_re, h_im, lam_rows, p3_re, p3_im, bbr4, bbi4, cr4, ci4, d_row, dproj, t_len, tb):
    seg = tb // SUBLANES
    nb = t_len // tb

    def body(dg_ref, yp_ref, u_ref, hr_ref, hi_ref, lam_ref, p3r_ref, p3i_ref, bbr_ref, bbi_ref, cr_ref, ci_ref,
             d_ref, _, du_ref, dbbr_ref, dbbi_ref, dcr_ref, dci_ref, dd_ref, dlam_ref,
             gr_ref, gi_ref, car_ref, cai_ref, cn_r, cn_i):
        @pl.when(pl.program_id(0) == 0)
        def _():
            for ref in (car_ref, cai_ref, dbbr_ref, dbbi_ref, dcr_ref, dci_ref, dd_ref, dlam_ref):
                ref[...] = jnp.zeros_like(ref)

        u = u_ref[...]
        dy = dg_ref[...] * _dgelu(yp_ref[...])
        nlb = S5_BL // 128
        for i in range(S5_BLOCKS):
            dyi = dy[:, i * S5_BW:(i + 1) * S5_BW]
            xr, xi = _dot(dyi, cr_ref[i]), -_dot(dyi, ci_ref[i])
            for jj in range(nlb):
                gr_ref[i * nlb + jj] = xr[:, jj * 128:(jj + 1) * 128]
                gi_ref[i * nlb + jj] = xi[:, jj * 128:(jj + 1) * 128]
        for lc in range(S5_LANE_BLOCKS // S5_SCAN_BLOCKS):
            blocks = range(lc * S5_SCAN_BLOCKS, (lc + 1) * S5_SCAN_BLOCKS)
            _segment_scan(gr_ref, gi_ref, lam_ref, car_ref, cai_ref, cn_r, cn_i, blocks, seg, True)
            crs = [cn_r[:, _lanes(j)] for j in blocks]
            cis = [cn_i[:, _lanes(j)] for j in blocks]

            def fix(k, carry, blocks=blocks, crs=crs, cis=cis):
                t = seg - 1 - k
                idx = pl.ds(t, SUBLANES, stride=seg)
                out = []
                for n, j in enumerate(blocks):
                    nr, ni, slr, sli = carry[4 * n:4 * n + 4]
                    pr, pi = p3r_ref[t, :, _lanes(j)], p3i_ref[t, :, _lanes(j)]
                    g_r = gr_ref[j, idx, :] + pr * crs[n] - pi * cis[n]
                    g_i = gi_ref[j, idx, :] + pr * cis[n] + pi * crs[n]
                    gr_ref[j, idx, :] = g_r
                    gi_ref[j, idx, :] = g_i
                    hr, hi = hr_ref[j, idx, :], hi_ref[j, idx, :]
                    out += [g_r, g_i, slr + nr * hr + ni * hi, sli + ni * hr - nr * hi]
                return tuple(out)

            zero = jnp.zeros((SUBLANES, 128), F32)
            init = []
            for n in range(len(blocks)):
                init += [crs[n], cis[n], zero, zero]
            fin = lax.fori_loop(0, seg, fix, tuple(init), unroll=2)
            for n, j in enumerate(blocks):
                dlam_ref[0:1, _lanes(j)] += jnp.sum(fin[4 * n + 2], axis=0, keepdims=True)
                dlam_ref[1:2, _lanes(j)] += jnp.sum(fin[4 * n + 3], axis=0, keepdims=True)
        for i in range(S5_BLOCKS):
            ws = pl.ds(i * S5_BW, S5_BW)
            js = range(i * nlb, (i + 1) * nlb)
            ui, dyi = u[:, i * S5_BW:(i + 1) * S5_BW], dy[:, i * S5_BW:(i + 1) * S5_BW]
            gr = jnp.concatenate([gr_ref[j] for j in js], axis=1)
            gi = jnp.concatenate([gi_ref[j] for j in js], axis=1)
            du_ref[:, ws] = _dot(gr, bbr_ref[i], _NT) + _dot(gi, bbi_ref[i], _NT) + d_ref[:, ws] * dyi
            dbbr_ref[i] += _dot(ui, gr, _TN)
            dbbi_ref[i] += _dot(ui, gi, _TN)
            dcr_ref[i] += _dot(jnp.concatenate([hr_ref[j] for j in js], axis=1), dyi, _TN)
            dci_ref[i] -= _dot(jnp.concatenate([hi_ref[j] for j in js], axis=1), dyi, _TN)
        dd_ref[...] += jnp.sum(dy * u, axis=0, keepdims=True)

    whole = pl.BlockSpec(memory_space=pltpu.VMEM)
    rev = lambda i: (nb - 1 - i, 0)
    const3 = lambda i: (0, 0, 0)
    h_spec = pl.BlockSpec((S5_LANE_BLOCKS, tb, 128), lambda i: (0, nb - 1 - i, 0))
    return pl.pallas_call(
        body, name="s5_bwd", grid=(nb,),
        in_specs=[pl.BlockSpec((tb, S5_WIDTH), rev), pl.BlockSpec((tb, S5_WIDTH), rev),
                  pl.BlockSpec((tb, S5_WIDTH), lambda i: (nb - 1 - i, 4096 // S5_WIDTH)),
                  h_spec, h_spec] + [whole] * 8
                 + [pl.BlockSpec(memory_space=pl.ANY)],
        out_specs=[pl.BlockSpec((tb, S5_WIDTH), lambda i: (nb - 1 - i, 4096 // S5_WIDTH)),
                   pl.BlockSpec((S5_BLOCKS, S5_BW, S5_BL), const3), pl.BlockSpec((S5_BLOCKS, S5_BW, S5_BL), const3),
                   pl.BlockSpec((S5_BLOCKS, S5_BL, S5_BW), const3), pl.BlockSpec((S5_BLOCKS, S5_BL, S5_BW), const3),
                   pl.BlockSpec((1, S5_WIDTH), lambda i: (0, 0)), pl.BlockSpec((2, S5_LANES), lambda i: (0, 0))],
        out_shape=[jax.ShapeDtypeStruct((t_len, IN_COLS), F32),
                   jax.ShapeDtypeStruct((S5_BLOCKS, S5_BW, S5_BL), F32),
                   jax.ShapeDtypeStruct((S5_BLOCKS, S5_BW, S5_BL), F32),
                   jax.ShapeDtypeStruct((S5_BLOCKS, S5_BL, S5_BW), F32),
                   jax.ShapeDtypeStruct((S5_BLOCKS, S5_BL, S5_BW), F32),
                   jax.ShapeDtypeStruct((1, S5_WIDTH), F32), jax.ShapeDtypeStruct((2, S5_LANES), F32)],
        scratch_shapes=[pltpu.VMEM((S5_LANE_BLOCKS, tb, 128), F32), pltpu.VMEM((S5_LANE_BLOCKS, tb, 128), F32),
                        pltpu.VMEM((1, S5_LANES), F32), pltpu.VMEM((1, S5_LANES), F32),
                        pltpu.VMEM((SUBLANES, S5_LANES), F32), pltpu.VMEM((SUBLANES, S5_LANES), F32)],
        input_output_aliases={13: 0},
        compiler_params=_params("arbitrary"))(dgelu, y_pre, proj, h_re, h_im, lam_rows, p3_re, p3_im, bbr4, bbi4,
                                              cr4, ci4, d_row, dproj)


def _to_segment_order(v, stage_ref, out_ref, seg):
    nbl = v.shape[1] // 128
    for b in range(nbl):
        stage_ref[b] = v[:, b * 128:(b + 1) * 128]

    def body(t, carry):
        rows = pl.ds(pl.multiple_of(t * SUBLANES, SUBLANES), SUBLANES)
        for b in range(nbl):
            out_ref[rows, _lanes(b)] = stage_ref[b, pl.ds(t, SUBLANES, stride=seg), :]
        return carry

    lax.fori_loop(0, seg, body, 0)


def _from_segment_order(v, stage_ref, out_ref, seg):
    nbl = v.shape[1] // 128
    for b in range(nbl):
        stage_ref[b] = v[:, b * 128:(b + 1) * 128]
    for s in range(SUBLANES):
        def body(k, carry, s=s):
            rows = pl.ds(pl.multiple_of(s * seg + k * SUBLANES, SUBLANES), SUBLANES)
            for b in range(nbl):
                out_ref[rows, _lanes(b)] = stage_ref[b, pl.ds(k * SUBLANES * SUBLANES + s, SUBLANES,
                                                              stride=SUBLANES), :]
            return carry

        lax.fori_loop(0, seg // SUBLANES, body, 0)


def _tile_scan(xr_ref, xi_ref, lam_ref, car_ref, cai_ref, cn_r, cn_i, blocks, seg, reverse):
    shape = (SUBLANES, 128)
    lrs = [jnp.broadcast_to(lam_ref[0:1, _lanes(j)], shape) for j in blocks]
    lis = [jnp.broadcast_to(lam_ref[1:2, _lanes(j)], shape) for j in blocks]

    def step(k, carry):
        t = seg - 1 - k if reverse else k
        rows = pl.ds(pl.multiple_of(t * SUBLANES, SUBLANES), SUBLANES)
        out = []
        for n, j in enumerate(blocks):
            cr, ci = carry[2 * n], carry[2 * n + 1]
            nr = lrs[n] * cr - lis[n] * ci + xr_ref[rows, _lanes(j)]
            ni = lrs[n] * ci + lis[n] * cr + xi_ref[rows, _lanes(j)]
            xr_ref[rows, _lanes(j)] = nr
            xi_ref[rows, _lanes(j)] = ni
            out += [nr, ni]
        return tuple(out)

    zero = jnp.zeros(shape, F32)
    fin = lax.fori_loop(0, seg, step, (zero,) * (2 * len(blocks)), unroll=2)
    for n, j in enumerate(blocks):
        ls = _lanes(j)
        fr, fi = fin[2 * n], fin[2 * n + 1]
        sr, si = lam_ref[2:3, ls], lam_ref[3:4, ls]
        pr, pi = car_ref[:, ls], cai_ref[:, ls]
        for s in (reversed(range(SUBLANES)) if reverse else range(SUBLANES)):
            cn_r[s:s + 1, ls] = pr
            cn_i[s:s + 1, ls] = pi
            pr, pi = fr[s:s + 1, :] + sr * pr - si * pi, fi[s:s + 1, :] + sr * pi + si * pr
        car_ref[:, ls] = pr
        cai_ref[:, ls] = pi


def _s5_fwd3(proj, lam_rows, p3_re, p3_im, bbr4, bbi4, crt4, cit4, d_row, t_len, tb):
    seg = tb // SUBLANES

    def body(u_ref, lam_ref, p3r_ref, p3i_ref, bbr_ref, bbi_ref, crt_ref, cit_ref, d_ref,
             hr_ref, hi_ref, ypre_ref, ys_ref, car_ref, cai_ref, cn_r, cn_i, stage_ref, us_ref, yseg_ref):
        @pl.when(pl.program_id(0) == 0)
        def _():
            car_ref[...] = jnp.zeros_like(car_ref)
            cai_ref[...] = jnp.zeros_like(cai_ref)

        _to_segment_order(u_ref[...], stage_ref, us_ref, seg)
        u = us_ref[...]
        for i in range(S5_BLOCKS):
            ui = u[:, i * S5_BW:(i + 1) * S5_BW]
            hr_ref[:, pl.ds(i * S5_BL, S5_BL)] = _dot(ui, bbr_ref[i])
            hi_ref[:, pl.ds(i * S5_BL, S5_BL)] = _dot(ui, bbi_ref[i])
        for lc in range(S5_LANE_BLOCKS // S5_SCAN_BLOCKS):
            blocks = range(lc * S5_SCAN_BLOCKS, (lc + 1) * S5_SCAN_BLOCKS)
            _tile_scan(hr_ref, hi_ref, lam_ref, car_ref, cai_ref, cn_r, cn_i, blocks, seg, False)
            crs = [cn_r[:, _lanes(j)] for j in blocks]
            cis = [cn_i[:, _lanes(j)] for j in blocks]

            def fix(t, carry, blocks=blocks, crs=crs, cis=cis):
                rows = pl.ds(pl.multiple_of(t * SUBLANES, SUBLANES), SUBLANES)
                for n, j in enumerate(blocks):
                    pr, pi = p3r_ref[t, :, _lanes(j)], p3i_ref[t, :, _lanes(j)]
                    hr_ref[rows, _lanes(j)] += pr * crs[n] - pi * cis[n]
                    hi_ref[rows, _lanes(j)] += pr * cis[n] + pi * crs[n]
                return carry

            lax.fori_loop(0, seg, fix, 0, unroll=2)
        for i in range(S5_BLOCKS):
            ws = pl.ds(i * S5_BW, S5_BW)
            bl = pl.ds(i * S5_BL, S5_BL)
            yseg_ref[:, ws] = (_dot(hr_ref[:, bl], crt_ref[i]) - _dot(hi_ref[:, bl], cit_ref[i])
                               + d_ref[:, ws] * u[:, i * S5_BW:(i + 1) * S5_BW])
        _from_segment_order(yseg_ref[...], stage_ref, ypre_ref, seg)
        ys_ref[...] = jax.nn.gelu(ypre_ref[...], approximate=True).astype(ys_ref.dtype)

    whole = pl.BlockSpec(memory_space=pltpu.VMEM)
    return pl.pallas_call(
        body, name="s5_fwd", grid=(t_len // tb,),
        in_specs=[pl.BlockSpec((tb, S5_WIDTH), lambda i: (i, 4096 // S5_WIDTH))] + [whole] * 8,
        out_specs=[pl.BlockSpec((tb, S5_LANES), lambda i: (i, 0)), pl.BlockSpec((tb, S5_LANES), lambda i: (i, 0)),
                   pl.BlockSpec((tb, S5_WIDTH), lambda i: (i, 0)), pl.BlockSpec((tb, S5_WIDTH), lambda i: (i, 0))],
        out_shape=[jax.ShapeDtypeStruct((t_len, S5_LANES), F32), jax.ShapeDtypeStruct((t_len, S5_LANES), F32),
                   jax.ShapeDtypeStruct((t_len, S5_WIDTH), F32), jax.ShapeDtypeStruct((t_len, S5_WIDTH), MXU_DTYPE)],
        scratch_shapes=[pltpu.VMEM((1, S5_LANES), F32), pltpu.VMEM((1, S5_LANES), F32),
                        pltpu.VMEM((SUBLANES, S5_LANES), F32), pltpu.VMEM((SUBLANES, S5_LANES), F32),
                        pltpu.VMEM((S5_WIDTH // 128, tb, 128), F32), pltpu.VMEM((tb, S5_WIDTH), F32),
                        pltpu.VMEM((tb, S5_WIDTH), F32)],
        compiler_params=_params("arbitrary"))(proj, lam_rows, p3_re, p3_im, bbr4, bbi4, crt4, cit4, d_row)


def _s5_bwd3(dgelu, y_pre, proj, h_re, h_im, lam_rows, p3_re, p3_im, bbr4, bbi4, cr4, ci4, d_row, dproj, t_len, tb):
    seg = tb // SUBLANES
    nb = t_len // tb

    def body(dg_ref, yp_ref, u_ref, hr_ref, hi_ref, lam_ref, p3r_ref, p3i_ref, bbr_ref, bbi_ref, cr_ref, ci_ref,
             d_ref, _, du_ref, dbbr_ref, dbbi_ref, dcr_ref, dci_ref, dd_ref, dlam_ref,
             gr_ref, gi_ref, car_ref, cai_ref, cn_r, cn_i, stage_ref, us_ref, dys_ref, duseg_ref):
        @pl.when(pl.program_id(0) == 0)
        def _():
            for ref in (car_ref, cai_ref, dbbr_ref, dbbi_ref, dcr_ref, dci_ref, dd_ref, dlam_ref):
                ref[...] = jnp.zeros_like(ref)

        _to_segment_order(u_ref[...], stage_ref, us_ref, seg)
        _to_segment_order(dg_ref[...] * _dgelu(yp_ref[...]), stage_ref, dys_ref, seg)
        u, dy = us_ref[...], dys_ref[...]
        for i in range(S5_BLOCKS):
            dyi = dy[:, i * S5_BW:(i + 1) * S5_BW]
            gr_ref[:, pl.ds(i * S5_BL, S5_BL)] = _dot(dyi, cr_ref[i])
            gi_ref[:, pl.ds(i * S5_BL, S5_BL)] = -_dot(dyi, ci_ref[i])
        for lc in range(S5_LANE_BLOCKS // S5_SCAN_BLOCKS):
            blocks = range(lc * S5_SCAN_BLOCKS, (lc + 1) * S5_SCAN_BLOCKS)
            _tile_scan(gr_ref, gi_ref, lam_ref, car_ref, cai_ref, cn_r, cn_i, blocks, seg, True)
            crs = [cn_r[:, _lanes(j)] for j in blocks]
            cis = [cn_i[:, _lanes(j)] for j in blocks]

            def fix(k, carry, blocks=blocks, crs=crs, cis=cis):
                t = seg - 1 - k
                rows = pl.ds(pl.multiple_of(t * SUBLANES, SUBLANES), SUBLANES)
                out = []
                for n, j in enumerate(blocks):
                    nr, ni, slr, sli = carry[4 * n:4 * n + 4]
                    pr, pi = p3r_ref[t, :, _lanes(j)], p3i_ref[t, :, _lanes(j)]
                    g_r = gr_ref[rows, _lanes(j)] + pr * crs[n] - pi * cis[n]
                    g_i = gi_ref[rows, _lanes(j)] + pr * cis[n] + pi * crs[n]
                    gr_ref[rows, _lanes(j)] = g_r
                    gi_ref[rows, _lanes(j)] = g_i
                    hr, hi = hr_ref[rows, _lanes(j)], hi_ref[rows, _lanes(j)]
                    out += [g_r, g_i, slr + nr * hr + ni * hi, sli + ni * hr - nr * hi]
                return tuple(out)

            zero = jnp.zeros((SUBLANES, 128), F32)
            init = []
            for n in range(len(blocks)):
                init += [crs[n], cis[n], zero, zero]
            fin = lax.fori_loop(0, seg, fix, tuple(init), unroll=2)
            for n, j in enumerate(blocks):
                dlam_ref[0:1, _lanes(j)] += jnp.sum(fin[4 * n + 2], axis=0, keepdims=True)
                dlam_ref[1:2, _lanes(j)] += jnp.sum(fin[4 * n + 3], axis=0, keepdims=True)
        for i in range(S5_BLOCKS):
            ws = pl.ds(i * S5_BW, S5_BW)
            bl = pl.ds(i * S5_BL, S5_BL)
            ui, dyi = u[:, i * S5_BW:(i + 1) * S5_BW], dy[:, i * S5_BW:(i + 1) * S5_BW]
            gr, gi = gr_ref[:, bl], gi_ref[:, bl]
            duseg_ref[:, ws] = _dot(gr, bbr_ref[i], _NT) + _dot(gi, bbi_ref[i], _NT) + d_ref[:, ws] * dyi
            dbbr_ref[i] += _dot(ui, gr, _TN)
            dbbi_ref[i] += _dot(ui, gi, _TN)
            dcr_ref[i] += _dot(hr_ref[:, bl], dyi, _TN)
            dci_ref[i] -= _dot(hi_ref[:, bl], dyi, _TN)
        dd_ref[...] += jnp.sum(dy * u, axis=0, keepdims=True)
        _from_segment_order(duseg_ref[...], stage_ref, du_ref, seg)

    whole = pl.BlockSpec(memory_space=pltpu.VMEM)
    rev = lambda i: (nb - 1 - i, 0)
    const3 = lambda i: (0, 0, 0)
    return pl.pallas_call(
        body, name="s5_bwd", grid=(nb,),
        in_specs=[pl.BlockSpec((tb, S5_WIDTH), rev), pl.BlockSpec((tb, S5_WIDTH), rev),
                  pl.BlockSpec((tb, S5_WIDTH), lambda i: (nb - 1 - i, 4096 // S5_WIDTH)),
                  pl.BlockSpec((tb, S5_LANES), rev), pl.BlockSpec((tb, S5_LANES), rev)] + [whole] * 8
                 + [pl.BlockSpec(memory_space=pl.ANY)],
        out_specs=[pl.BlockSpec((tb, S5_WIDTH), lambda i: (nb - 1 - i, 4096 // S5_WIDTH)),
                   pl.BlockSpec((S5_BLOCKS, S5_BW, S5_BL), const3), pl.BlockSpec((S5_BLOCKS, S5_BW, S5_BL), const3),
                   pl.BlockSpec((S5_BLOCKS, S5_BL, S5_BW), const3), pl.BlockSpec((S5_BLOCKS, S5_BL, S5_BW), const3),
                   pl.BlockSpec((1, S5_WIDTH), lambda i: (0, 0)), pl.BlockSpec((2, S5_LANES), lambda i: (0, 0))],
        out_shape=[jax.ShapeDtypeStruct((t_len, IN_COLS), F32),
                   jax.ShapeDtypeStruct((S5_BLOCKS, S5_BW, S5_BL), F32),
                   jax.ShapeDtypeStruct((S5_BLOCKS, S5_BW, S5_BL), F32),
                   jax.ShapeDtypeStruct((S5_BLOCKS, S5_BL, S5_BW), F32),
                   jax.ShapeDtypeStruct((S5_BLOCKS, S5_BL, S5_BW), F32),
                   jax.ShapeDtypeStruct((1, S5_WIDTH), F32), jax.ShapeDtypeStruct((2, S5_LANES), F32)],
        scratch_shapes=[pltpu.VMEM((tb, S5_LANES), F32), pltpu.VMEM((tb, S5_LANES), F32),
                        pltpu.VMEM((1, S5_LANES), F32), pltpu.VMEM((1, S5_LANES), F32),
                        pltpu.VMEM((SUBLANES, S5_LANES), F32), pltpu.VMEM((SUBLANES, S5_LANES), F32),
                        pltpu.VMEM((S5_WIDTH // 128, tb, 128), F32), pltpu.VMEM((tb, S5_WIDTH), F32),
                        pltpu.VMEM((tb, S5_WIDTH), F32), pltpu.VMEM((tb, S5_WIDTH), F32)],
        input_output_aliases={13: 0},
        compiler_params=_params("arbitrary"))(dgelu, y_pre, proj, h_re, h_im, lam_rows, p3_re, p3_im, bbr4, bbi4,
                                              cr4, ci4, d_row, dproj)


def _block_diag4(per_group):
    g8 = S5_GROUPS // S5_BLOCKS
    eye = jnp.eye(g8, dtype=bool)[None, :, None, :, None]
    dense = jnp.where(eye, per_group.reshape(S5_BLOCKS, g8, S5_GROUP, 1, S5_STATE), 0.0)
    return dense.reshape(S5_BLOCKS, S5_BW, S5_BL)


def _diag_blocks4(dense):
    g8 = S5_GROUPS // S5_BLOCKS
    ar = jnp.arange(g8)
    d5 = dense.reshape(S5_BLOCKS, g8, S5_GROUP, g8, S5_STATE)
    return d5[:, ar, :, ar, :].transpose(1, 0, 2, 3).reshape(S5_GROUPS, S5_GROUP, S5_STATE)


def _block_diag(per_group):
    eye = jnp.eye(S5_GROUPS, dtype=bool)[:, None, :, None]
    dense = jnp.where(eye, per_group[:, :, None, :], 0.0)
    return dense.reshape(S5_WIDTH, S5_LANES)


def _diag_blocks(dense):
    ar = jnp.arange(S5_GROUPS)
    return dense.reshape(S5_GROUPS, S5_GROUP, S5_GROUPS, S5_STATE)[ar, :, ar, :]


def _local_step(x, p, target, w, sm):
    t_len = x.shape[0]
    tm = min(256, t_len)
    tmm = min(512, t_len)
    tb_hg = min(256, t_len)
    tb_s5 = min(256, t_len)
    g1, g2, g3, ghn = sm["norm_g"], sm["ple_norm_g"], sm["final_norm_g"].reshape(1, D_MODEL), sm["hg_norm_g"]

    def rms_f(xv, g):
        r = lax.rsqrt(jnp.mean(xv * xv, axis=-1, keepdims=True) + NORM_EPS)
        return (xv * r * g,)

    (u,) = _rowwise("rms_in", rms_f, t_len, tm, [(x, 1024, 0)], [g1], [(1024, MXU_DTYPE)])
    proj = _mm_nn("mm_in", u, w["w_in"], tmm, 1024)
    o_hg, act_hg, s_prev = _hgrn2_fwd(proj, sm["hg_lb"], ghn, t_len, tb_hg)

    lanes = lambda a: a.reshape(1, S5_LANES)
    a_re, a_im = lanes(sm["s5_a_re"]), lanes(sm["s5_a_im"])
    ldt = lanes(jnp.broadcast_to(sm["s5_log_dt"].reshape(S5_GROUPS, 1), (S5_GROUPS, S5_STATE)))
    to_t = lambda b: b.reshape(S5_GROUPS, S5_STATE, S5_GROUP).transpose(2, 0, 1).reshape(S5_GROUP, S5_LANES)
    b_re_t, b_im_t = to_t(sm["s5_b_re"]), to_t(sm["s5_b_im"])
    pw_re, pw_im, bbr_t, bbi_t = _s5_powers(a_re, a_im, ldt, b_re_t, b_im_t, tb_s5 // SUBLANES)
    from_t = lambda b: b.reshape(S5_GROUP, S5_GROUPS, S5_STATE).transpose(1, 0, 2)
    bbr_bd = _block_diag4(from_t(bbr_t)).astype(MXU_DTYPE)
    bbi_bd = _block_diag4(from_t(bbi_t)).astype(MXU_DTYPE)
    cr_bd = _block_diag4(sm["s5_c_re"].reshape(S5_GROUPS, S5_GROUP, S5_STATE)).astype(MXU_DTYPE)
    ci_bd = _block_diag4(sm["s5_c_im"].reshape(S5_GROUPS, S5_GROUP, S5_STATE)).astype(MXU_DTYPE)
    d_row = sm["s5_d"].reshape(1, S5_WIDTH)
    h_re, h_im, y_pre, ys_gelu = _s5_fwd3(proj, *_scan_tables(pw_re, pw_im, False), bbr_bd, bbi_bd,
                                          cr_bd.transpose(0, 2, 1), ci_bd.transpose(0, 2, 1), d_row, t_len, tb_s5)
    def mix_f(act, ysg, z, gh, gs, xv, w_glu, b_glu, w_o_hg, w_o_s5, w_out):
        gl_ = _dot(ysg, w_glu) + b_glu
        a, b = gl_[:, :S5_WIDTH], gl_[:, S5_WIDTH:]
        ys2_ = (a * _sig(b) * (z * _sig(z))).astype(MXU_DTYPE)
        yh, ys = _dot(act, w_o_hg), _dot(ys2_, w_o_s5)
        mg = (_sig(gh) * yh + _sig(gs) * ys).astype(MXU_DTYPE)
        return (gl_, ys2_, yh, ys, mg, xv + _dot(mg, w_out))

    glu, ys2, y_hg, y_s5, merged, h1 = _rowwise(
        "mix_out", mix_f, t_len, tm,
        [(act_hg, 1024, 0), (ys_gelu, 512, 0), (proj, 512, 4608 // 512), (proj, 1024, 5), (proj, 1024, 6),
         (x, 1024, 0)], [w["w_glu"], sm["b_glu"], w["w_o_hg"], w["w_o_s5"], w["w_out"]],
        [(1024, F32), (512, MXU_DTYPE), (1024, F32), (1024, F32), (1024, MXU_DTYPE), (1024, F32)])

    def head_f(h1v, pv, tgt, g_ple, g, w_ple, w_gate):
        r2 = lax.rsqrt(jnp.mean(h1v * h1v, axis=-1, keepdims=True) + NORM_EPS)
        n2_ = (h1v * r2 * g_ple).astype(MXU_DTYPE)
        glv, pev = _dot(n2_, w_gate), _dot(pv, w_ple)
        gate = _sig(glv)
        h2 = h1v + pev * gate
        r = lax.rsqrt(jnp.mean(h2 * h2, axis=-1, keepdims=True) + NORM_EPS)
        e = h2 * r * g - tgt
        loss = 0.5 * jnp.sum(jnp.mean(e * e, axis=-1, keepdims=True), axis=0, keepdims=True)
        dy = e * (1.0 / D_MODEL)
        dg = jnp.sum(dy * h2 * r, axis=0, keepdims=True)
        t = dy * g
        dh2 = r * t - h2 * (r * r * r) * jnp.mean(t * h2, axis=-1, keepdims=True)
        return (n2_, dh2, dh2 * gate, dh2 * pev * gate * (1.0 - gate), jnp.broadcast_to(loss, (1, 128)), dg)

    n2, dh2, dpe, dgl, loss_row, d_g3 = _rowwise(
        "ple_loss_head", head_f, t_len, tm, [(h1, 1024, 0), (p, 256, 0), (target, 1024, 0)],
        [g2, g3, w["w_ple"], w["w_ple_gate"]],
        [(1024, MXU_DTYPE), (1024, F32), (1024, MXU_DTYPE), (1024, MXU_DTYPE)], accs=[(1, 128), (1, 1024)])

    gb = {}
    gb["w_ple"] = _mm_tn("mm_d_w_ple", p, dpe, tmm, 1024)
    gb["w_ple_gate"] = _mm_tn("mm_d_w_ple_gate", n2, dgl, tmm, 1024)
    def ple_b(dn, h1v, dh, g):
        dx, dg = _rms_bwd(dn, h1v, g)
        return (dh + dx, dg)

    dh1, d_g2 = _mm_nt_then("mm_d_n2_rms_ple_bwd", dgl, w["w_ple_gate"], tm, 1024, ple_b,
                            [(h1, 1024, 0), (dh2, 1024, 0)], [g2], [(1024, F32)], accs=[(1, 1024)])
    gb["w_out"] = _mm_tn("mm_d_w_out", merged, dh1, tmm, 1024)
    dmerged = _mm_nt("mm_d_merged", dh1, w["w_out"], tmm, 1024)

    def gate_b(dm, y, gt):
        s = _sig(gt)
        return (dm * s, dm * y * s * (1.0 - s))

    dy_hg, dproj = _rowwise("gate_hg_bwd", gate_b, t_len, tm, [(dmerged, 1024, 0), (y_hg, 1024, 0), (proj, 1024, 5)],
                            [], [(1024, MXU_DTYPE), (1024, F32, 5, IN_COLS)])
    dy_s5, dproj = _rowwise("gate_s5_bwd", gate_b, t_len, tm, [(dmerged, 1024, 0), (y_s5, 1024, 0), (proj, 1024, 6)],
                            [], [(1024, MXU_DTYPE), (1024, F32, 6, IN_COLS)], alias=(dproj, 1))
    gb["w_o_s5"] = _mm_tn("mm_d_w_o_s5", ys2, dy_s5, tmm, 1024)
    def glu_b(dys, gl_, z):
        a, b = gl_[:, :S5_WIDTH], gl_[:, S5_WIDTH:]
        sb, sz = _sig(b), _sig(z)
        silu = z * sz
        dglu = jnp.concatenate([dys * sb * silu, dys * a * silu * sb * (1.0 - sb)], axis=1)
        return (dglu, dys * a * sb * _dsilu(z, sz), jnp.sum(dglu, axis=0, keepdims=True))

    dglu, dproj, d_bglu = _mm_nt_then("mm_d_ys2_glu_bwd", dy_s5, w["w_o_s5"], tm, 1024, glu_b,
                                      [(glu, 1024, 0), (proj, 512, 4608 // 512)], [],
                                      [(1024, MXU_DTYPE), (512, F32, 4608 // 512, IN_COLS)], accs=[(1, 1024)],
                                      alias=(dproj, 1))
    gb["w_glu"] = _mm_tn("mm_d_w_glu", ys_gelu, dglu, tmm, 1024)
    dgelu = _mm_nt("mm_d_gelu", dglu, w["w_glu"], tmm, 1024)
    dproj, d_bbr, d_bbi, d_crt, d_cit, d_d, d_lam = _s5_bwd3(dgelu, y_pre, proj, h_re, h_im,
                                                            *_scan_tables(pw_re, pw_im, True), bbr_bd, bbi_bd, cr_bd,
                                                            ci_bd, d_row, dproj, t_len, tb_s5)
    to_t3 = lambda b: b.transpose(1, 0, 2).reshape(S5_GROUP, S5_LANES)
    d_are, d_aim, d_ldt, d_br_t, d_bi_t = _s5_prep_bwd(a_re, a_im, ldt, b_re_t, b_im_t, d_lam,
                                                       to_t3(_diag_blocks4(d_bbr)), to_t3(_diag_blocks4(d_bbi)))
    gb["w_o_hg"] = _mm_tn("mm_d_w_o_hg", act_hg, dy_hg, tmm, 1024)
    def hg_gate_b(da, o, g, gn):
        dos, dgs, dgns = [], [], []
        for h in range(HG_HEADS):
            sl = slice(h * HG_DIM, (h + 1) * HG_DIM)
            oh, gh, dah, gnh = o[:, sl], g[:, sl], da[:, sl], gn[:, sl]
            rr = lax.rsqrt(jnp.mean(oh * oh, axis=-1, keepdims=True) + NORM_EPS)
            sg = _sig(gh)
            dgs.append(dah * (oh * rr * gnh) * _dsilu(gh, sg))
            don = dah * (gh * sg)
            t = don * gnh
            dos.append(rr * t - oh * (rr * rr * rr) * jnp.mean(t * oh, axis=-1, keepdims=True))
            dgns.append(jnp.sum(don * oh * rr, axis=0, keepdims=True))
        return (jnp.concatenate(dos, axis=1), jnp.concatenate(dgs, axis=1), jnp.concatenate(dgns, axis=1))

    d_o, dproj, d_ghn = _mm_nt_then("mm_d_act_hg_gate_bwd", dy_hg, w["w_o_hg"], tm, 1024, hg_gate_b,
                                    [(o_hg, 1024, 0), (proj, 1024, 3)], [ghn],
                                    [(1024, F32), (1024, F32, 3, IN_COLS)], accs=[(1, 1024)], alias=(dproj, 1))
    dproj, d_lb = _hgrn2_bwd(proj, d_o, s_prev, sm["hg_lb"], dproj, t_len, tb_hg)
    gb["w_in"] = _mm_tn("mm_d_w_in", u, dproj, tmm, 1024)
    def in_b(duv, xv, dh, g):
        dx, dg = _rms_bwd(duv, xv, g)
        return (dh + dx, dg)

    grad_x, d_g1 = _mm_nt_then("mm_d_u_rms_in_bwd", dproj, w["w_in"], tmm, 1024, in_b,
                               [(x, 1024, 0), (dh1, 1024, 0)], [g1], [(1024, F32)], accs=[(1, 1024)])

    back_t = lambda b: b.reshape(S5_GROUP, S5_GROUPS, S5_STATE).transpose(1, 2, 0).reshape(1, S5_GROUPS, S5_STATE,
                                                                                           S5_GROUP)
    gs = {
        "norm_g": d_g1, "hg_lb": d_lb, "hg_norm_g": d_ghn,
        "s5_a_re": d_are.reshape(1, S5_GROUPS, S5_STATE), "s5_a_im": d_aim.reshape(1, S5_GROUPS, S5_STATE),
        "s5_log_dt": d_ldt[0:1, :S5_GROUPS],
        "s5_b_re": back_t(d_br_t), "s5_b_im": back_t(d_bi_t),
        "s5_c_re": _diag_blocks4(d_crt.transpose(0, 2, 1)).reshape(1, S5_GROUPS, S5_GROUP, S5_STATE),
        "s5_c_im": _diag_blocks4(d_cit.transpose(0, 2, 1)).reshape(1, S5_GROUPS, S5_GROUP, S5_STATE),
        "s5_d": d_d.reshape(1, S5_GROUPS, S5_GROUP), "b_glu": d_bglu, "ple_norm_g": d_g2,
        "final_norm_g": d_g3.reshape(D_MODEL),
    }
    return loss_row, grad_x, gb, gs


def _shard_shape(name):
    r, c = BIG_SHAPE[name]
    return (r, c // N_CHIPS) if name in BIG_COL_SHARDED else (r // N_CHIPS, c)


def _pack_shard(parts):
    return jnp.concatenate([parts[n].reshape(-1, PACK_W) for n in BIG], axis=0)


def _unpack_shard(packed):
    out, off = {}, 0
    for n in BIG:
        r, c = _shard_shape(n)
        rows = r * c // PACK_W
        out[n] = packed[off:off + rows].reshape(1, r, c)
        off += rows
    return out


def _unpack_full(gathered):
    out, off = {}, 0
    for n in BIG:
        r, c = _shard_shape(n)
        rows = r * c // PACK_W
        sh = gathered[:, off:off + rows].reshape(N_CHIPS, r, c)
        out[n] = sh.transpose(1, 0, 2).reshape(BIG_SHAPE[n]) if n in BIG_COL_SHARDED else sh.reshape(BIG_SHAPE[n])
        off += rows
    return out


def _pack_full(full):
    parts = []
    for n in BIG:
        r, c = _shard_shape(n)
        g = full[n]
        sh = g.reshape(BIG_SHAPE[n][0], N_CHIPS, c).transpose(1, 0, 2) if n in BIG_COL_SHARDED else g
        parts.append(sh.reshape(N_CHIPS, r * c // PACK_W, PACK_W))
    packed = jnp.concatenate(parts, axis=1)
    return packed.reshape(N_CHIPS, 2, HALF_ROWS, PACK_W).transpose(1, 0, 2, 3)


def _pack_small(parts, last):
    flat = jnp.concatenate([parts[n].reshape(-1) for n in SMALL] + [last.reshape(-1)])
    return jnp.pad(flat, (0, SMALL_ROWS * PACK_W - flat.shape[0])).reshape(SMALL_ROWS, PACK_W)


def _unpack_small(packed):
    flat, out, off = packed.reshape(-1), {}, 0
    for n in SMALL:
        size = 1
        for d in SMALL_SHAPE[n]:
            size *= d
        out[n] = flat[off:off + size].reshape(SMALL_SHAPE[n])
        off += size
    return out, flat[off]


def _place():
    x, y, c = lax.axis_index("x"), lax.axis_index("y"), lax.axis_index("c")
    return x, y, c, [(1 - x, y), (x, 1 - y), (1 - x, 1 - y)]


def _remote(src, dst, send_sems, recv_sems, k, to):
    return pltpu.make_async_remote_copy(src_ref=src, dst_ref=dst, send_sem=send_sems.at[k], recv_sem=recv_sems.at[k],
                                        device_id=to, device_id_type=MESH)


_HBM = pl.BlockSpec(memory_space=pl.ANY)


def _all_gather_weights(wp):
    def body(wp_ref, out_ref, send_sems, recv_sems):
        x, y, c, chips = _place()
        k = 2 * x + y
        sibling = (x, y, 1 - c)
        first =[_remote(wp_ref.at[c], out_ref.at[k, c], send_sems, recv_sems, j, (cx, cy, c))
                 for j, (cx, cy) in enumerate(chips)]
        for cp in first:
            cp.start()
        passed = []
        for j, (cx, cy) in enumerate(chips):
            kj = 2 * cx + cy
            _remote(wp_ref.at[c], out_ref.at[kj, c], send_sems, recv_sems, j, (cx, cy, c)).wait_recv()
            cp = _remote(out_ref.at[kj, c], out_ref.at[kj, c], send_sems, recv_sems, 3 + j, sibling)
            cp.start()
            passed.append(cp)
        for j, (cx, cy) in enumerate(chips):
            kj = 2 * cx + cy
            _remote(wp_ref.at[c], out_ref.at[kj, 1 - c], send_sems, recv_sems, 3 + j, sibling).wait_recv()
        for cp in first + passed:
            cp.wait_send()

    return pl.pallas_call(
        body, name="all_gather_weights", in_specs=[_HBM], out_specs=_HBM,
        out_shape=jax.ShapeDtypeStruct((N_CHIPS, 2, HALF_ROWS, PACK_W), wp.dtype),
        scratch_shapes=[pltpu.SemaphoreType.DMA((6,)), pltpu.SemaphoreType.DMA((6,))])(wp)


def _exchange_halves(pg):
    def body(pg_ref, out_ref, send_sems, recv_sems):
        x, y, c, _ = _place()
        cp = _remote(pg_ref.at[1 - c], out_ref, send_sems, recv_sems, 0, (x, y, 1 - c))
        cp.start()
        cp.wait()

    return pl.pallas_call(
        body, name="exchange_halves", in_specs=[_HBM], out_specs=_HBM,
        out_shape=jax.ShapeDtypeStruct((N_CHIPS, HALF_ROWS, PACK_W), pg.dtype),
        scratch_shapes=[pltpu.SemaphoreType.DMA((1,)), pltpu.SemaphoreType.DMA((1,))])(pg)


def _scatter_chip_sums(ps):
    def body(ps_ref, out_ref, send_sems, recv_sems):
        x, y, c, chips = _place()
        cps = [_remote(ps_ref.at[2 * cx + cy], out_ref.at[j], send_sems, recv_sems, j, (cx, cy, c))
               for j, (cx, cy) in enumerate(chips)]
        for cp in cps:
            cp.start()
        for cp in cps:
            cp.wait()

    return pl.pallas_call(
        body, name="scatter_chip_sums", in_specs=[_HBM], out_specs=_HBM,
        out_shape=jax.ShapeDtypeStruct((3, HALF_ROWS, PACK_W), ps.dtype),
        scratch_shapes=[pltpu.SemaphoreType.DMA((3,)), pltpu.SemaphoreType.DMA((3,))])(ps)


def _share_half(g_half):
    def body(g_ref, out_ref, send_sems, recv_sems):
        x, y, c, _ = _place()
        cp = _remote(g_ref, out_ref.at[c], send_sems, recv_sems, 0, (x, y, 1 - c))
        cp.start()
        _remote(g_ref, out_ref.at[1 - c], send_sems, recv_sems, 0, (x, y, 1 - c)).wait_recv()
        cp.wait_send()

    return pl.pallas_call(
        body, name="share_half", in_specs=[_HBM], out_specs=_HBM,
        out_shape=jax.ShapeDtypeStruct((2, HALF_ROWS, PACK_W), g_half.dtype),
        scratch_shapes=[pltpu.SemaphoreType.DMA((1,)), pltpu.SemaphoreType.DMA((1,))])(g_half)


REDUCE_ROWS = 480


def _sum_pair(pg, theirs, c):
    def body(c_ref, a_ref, b_ref, o_ref):
        o_ref[...] = (a_ref[...] + b_ref[...]).astype(o_ref.dtype)

    return pl.pallas_call(
        body, name="sum_pair",
        grid_spec=pltpu.PrefetchScalarGridSpec(
            num_scalar_prefetch=1, grid=(N_CHIPS, HALF_ROWS // REDUCE_ROWS),
            in_specs=[pl.BlockSpec((None, None, REDUCE_ROWS, PACK_W), lambda j, i, c_ref: (c_ref[0], j, i, 0)),
                      pl.BlockSpec((None, REDUCE_ROWS, PACK_W), lambda j, i, c_ref: (j, i, 0))],
            out_specs=pl.BlockSpec((None, REDUCE_ROWS, PACK_W), lambda j, i, c_ref: (j, i, 0))),
        out_shape=jax.ShapeDtypeStruct((N_CHIPS, HALF_ROWS, PACK_W), WIRE_DTYPE),
        compiler_params=_params("arbitrary", "arbitrary"))(c.reshape(1), pg, theirs)


def _sum_chips(ps, others, k):
    def body(k_ref, a_ref, b_ref, o_ref):
        o_ref[...] = ((a_ref[...].astype(F32) + b_ref[0].astype(F32)) + b_ref[1].astype(F32)) + b_ref[2].astype(F32)

    return pl.pallas_call(
        body, name="sum_chips",
        grid_spec=pltpu.PrefetchScalarGridSpec(
            num_scalar_prefetch=1, grid=(HALF_ROWS // REDUCE_ROWS,),
            in_specs=[pl.BlockSpec((None, REDUCE_ROWS, PACK_W), lambda i, k_ref: (k_ref[0], i, 0)),
                      pl.BlockSpec((3, REDUCE_ROWS, PACK_W), lambda i, k_ref: (0, i, 0))],
            out_specs=pl.BlockSpec((REDUCE_ROWS, PACK_W), lambda i, k_ref: (i, 0))),
        out_shape=jax.ShapeDtypeStruct((HALF_ROWS, PACK_W), F32),
        compiler_params=_params("arbitrary"))(k.reshape(1), ps, others)


def _adamw(w, g, m, v):
    m = ADAM_B1 * m + (1.0 - ADAM_B1) * g
    v = ADAM_B2 * v + (1.0 - ADAM_B2) * (g * g)
    m_hat = m / (1.0 - ADAM_B1 ** ADAM_STEP)
    v_hat = v / (1.0 - ADAM_B2 ** ADAM_STEP)
    return -ADAM_LR * (m_hat / (jnp.sqrt(v_hat) + ADAM_EPS) + ADAM_WD * w), m, v


def _small_reduce_adamw(part, w, m, v):
    def body(part_ref, w_ref, m_ref, v_ref, g_ref, d_ref, nm_ref, nv_ref, all_ref, send_sems, recv_sems):
        x, y, c, chips = _place()
        me, sibling = (x, y, c), (x, y, 1 - c)

        def rows(px, py, pc):
            return all_ref.at[4 * px + 2 * py + pc]

        all_ref[4 * x + 2 * y + c] = part_ref[...]
        first = [_remote(part_ref, rows(*me), send_sems, recv_sems, 0, sibling)]
        first += [_remote(part_ref, rows(*me), send_sems, recv_sems, 1 + j, (cx, cy, c))
                  for j, (cx, cy) in enumerate(chips)]
        for cp in first:
            cp.start()
        passed = []
        for j, (cx, cy) in enumerate(chips):
            _remote(part_ref, rows(cx, cy, c), send_sems, recv_sems, 1 + j, me).wait_recv()
            cp = _remote(rows(cx, cy, c), rows(cx, cy, c), send_sems, recv_sems, 4 + j, sibling)
            cp.start()
            passed.append(cp)
        _remote(part_ref, rows(*sibling), send_sems, recv_sems, 0, me).wait_recv()
        for j, (cx, cy) in enumerate(chips):
            _remote(part_ref, rows(cx, cy, 1 - c), send_sems, recv_sems, 4 + j, me).wait_recv()
        for cp in first + passed:
            cp.wait_send()
        g = all_ref[0]
        for dev in range(1, N_DEV):
            g = g + all_ref[dev]
        delta, nm, nv = _adamw(w_ref[...], g, m_ref[...], v_ref[...])
        g_ref[...] = g
        d_ref[...] = delta
        nm_ref[...] = nm
        nv_ref[...] = nv

    whole = pl.BlockSpec(memory_space=pltpu.VMEM)
    shape = jax.ShapeDtypeStruct((SMALL_ROWS, PACK_W), F32)
    return pl.pallas_call(
        body, name="small_reduce_adamw", in_specs=[whole] * 4, out_specs=[whole] * 4, out_shape=[shape] * 4,
        scratch_shapes=[pltpu.VMEM((N_DEV, SMALL_ROWS, PACK_W), F32), pltpu.SemaphoreType.DMA((7,)),
                        pltpu.SemaphoreType.DMA((7,))],
        compiler_params=pltpu.CompilerParams(vmem_limit_bytes=VMEM_LIMIT))(part, w, m, v)


def kernel(x, p, norm_g, w_in, hg_lb, hg_norm_g, w_o_hg, s5_a_re, s5_a_im, s5_log_dt, s5_b_re, s5_b_im, s5_c_re, s5_c_im, s5_d, w_glu, b_glu, w_o_s5, w_out, ple_norm_g, w_ple, w_ple_gate, final_norm_g, loss_target, m_norm_g, m_w_in, m_hg_lb, m_hg_norm_g, m_w_o_hg, m_s5_a_re, m_s5_a_im, m_s5_log_dt, m_s5_b_re, m_s5_b_im, m_s5_c_re, m_s5_c_im, m_s5_d, m_w_glu, m_b_glu, m_w_o_s5, m_w_out, m_ple_norm_g, m_w_ple, m_w_ple_gate, m_final_norm_g, v_norm_g, v_w_in, v_hg_lb, v_hg_norm_g, v_w_o_hg, v_s5_a_re, v_s5_a_im, v_s5_log_dt, v_s5_b_re, v_s5_b_im, v_s5_c_re, v_s5_c_im, v_s5_d, v_w_glu, v_b_glu, v_w_o_s5, v_w_out, v_ple_norm_g, v_w_ple, v_w_ple_gate, v_final_norm_g):
    given = dict(locals())
    wts = {n: given[n] for n in WEIGHTS}
    mom = {n: given["m_" + n] for n in WEIGHTS}
    var = {n: given["v_" + n] for n in WEIGHTS}
    cx, cy, cc = lax.axis_index("x"), lax.axis_index("y"), lax.axis_index("c")
    chip = (2 * cx + cy).astype(jnp.int32)

    w_shard = _pack_shard({n: wts[n][0] for n in BIG})
    w_wire = w_shard.astype(MXU_DTYPE).reshape(2, HALF_ROWS, PACK_W)
    gathered = lax.dynamic_update_slice(_all_gather_weights(w_wire), w_wire[None], (chip, 0, 0, 0))
    w_full = _unpack_full(gathered.reshape(N_CHIPS, SHARD_ROWS, PACK_W))

    t_len = x.shape[1]
    loss_row, grad_x, g_big, g_small = _local_step(x.reshape(t_len, D_MODEL), p.reshape(t_len, -1),
                                                   loss_target.reshape(t_len, D_MODEL), w_full,
                                                   {n: wts[n] for n in SMALL})

    zero = jnp.zeros((), F32)
    sg, sd, snm, snv = _small_reduce_adamw(_pack_small(g_small, loss_row[0, 0]),
                                           _pack_small({n: wts[n] for n in SMALL}, zero),
                                           _pack_small({n: mom[n] for n in SMALL}, zero),
                                           _pack_small({n: var[n] for n in SMALL}, zero))
    (sg, loss), (sd, _), (snm, _), (snv, _) = (_unpack_small(a) for a in (sg, sd, snm, snv))

    pg = _pack_full(g_big)
    ps = _sum_pair(pg, _exchange_halves(pg), cc.astype(jnp.int32))
    g_half = _sum_chips(ps, _scatter_chip_sums(ps), chip)
    g_shard = lax.dynamic_update_slice(_share_half(g_half), g_half[None], (cc.astype(jnp.int32), 0, 0))
    g_shard = g_shard.reshape(SHARD_ROWS, PACK_W)

    def adam_f(wv, gv, mv, vv):
        return _adamw(wv, gv, mv, vv)

    bd, bnm, bnv = _rowwise("adamw_big", adam_f, SHARD_ROWS, REDUCE_ROWS,
                            [(w_shard, PACK_W, 0), (g_shard, PACK_W, 0),
                             (_pack_shard({n: mom[n][0] for n in BIG}), PACK_W, 0),
                             (_pack_shard({n: var[n][0] for n in BIG}), PACK_W, 0)], [],
                            [(PACK_W, F32), (PACK_W, F32), (PACK_W, F32)])
    bg, bd, bnm, bnv = (_unpack_shard(a) for a in (g_shard, bd, bnm, bnv))

    outs = [loss, grad_x.reshape(x.shape)]
    for small, big in ((sg, bg), (sd, bd), (snm, bnm), (snv, bnv)):
        outs += [big[n] if n in BIG else small[n] for n in WEIGHTS]
    return tuple(outs)
```

```python
import functools

import jax
import jax.numpy as jnp
from jax import lax
from jax.experimental import pallas as pl
from jax.experimental.pallas import tpu as pltpu

F32 = jnp.float32
MXU_DTYPE = jnp.bfloat16
WIRE_DTYPE = jnp.bfloat16
NORM_EPS = 1e-6
D_MODEL = 1024
HG_HEADS = 8
HG_DIM = 128
HG_CHUNK = 64
S5_WIDTH = 512
S5_GROUPS = 32
S5_GROUP = 16
S5_STATE = 64
S5_LANES = S5_GROUPS * S5_STATE
IN_COLS = 7168
SUBLANES = 8
VMEM_LIMIT = 56 * 1024 * 1024
HIGHEST = lax.Precision.HIGHEST
MESH = pl.DeviceIdType.MESH

ADAM_LR, ADAM_B1, ADAM_B2, ADAM_EPS, ADAM_WD, ADAM_STEP = 0.001, 0.9, 0.999, 1e-08, 0.01, 10

BIG = ("w_in", "w_o_hg", "w_glu", "w_o_s5", "w_out", "w_ple", "w_ple_gate")
BIG_SHAPE = {"w_in": (1024, 7168), "w_o_hg": (1024, 1024), "w_glu": (512, 1024), "w_o_s5": (512, 1024),
             "w_out": (1024, 1024), "w_ple": (256, 1024), "w_ple_gate": (1024, 1024)}
BIG_COL_SHARDED = ("w_in", "w_glu", "w_o_s5", "w_ple")
SMALL = ("norm_g", "hg_lb", "hg_norm_g", "s5_a_re", "s5_a_im", "s5_log_dt", "s5_b_re", "s5_b_im", "s5_c_re",
         "s5_c_im", "s5_d", "b_glu", "ple_norm_g", "final_norm_g")
SMALL_SHAPE = {"norm_g": (1, 1024), "hg_lb": (2, 1024), "hg_norm_g": (1, 1024), "s5_a_re": (1, 32, 64),
               "s5_a_im": (1, 32, 64), "s5_log_dt": (1, 32), "s5_b_re": (1, 32, 64, 16), "s5_b_im": (1, 32, 64, 16),
               "s5_c_re": (1, 32, 16, 64), "s5_c_im": (1, 32, 16, 64), "s5_d": (1, 32, 16), "b_glu": (1, 1024),
               "ple_norm_g": (1, 1024), "final_norm_g": (1024,)}
WEIGHTS = ("norm_g", "w_in", "hg_lb", "hg_norm_g", "w_o_hg", "s5_a_re", "s5_a_im", "s5_log_dt", "s5_b_re", "s5_b_im",
           "s5_c_re", "s5_c_im", "s5_d", "w_glu", "b_glu", "w_o_s5", "w_out", "ple_norm_g", "w_ple", "w_ple_gate",
           "final_norm_g")
N_CHIPS = 4
N_DEV = 8
PACK_W = 1024
SHARD_ROWS = sum(BIG_SHAPE[n][0] * BIG_SHAPE[n][1] for n in BIG) // (N_CHIPS * PACK_W)
HALF_ROWS = SHARD_ROWS // 2
SMALL_ROWS = 144


def _params(*sem):
    return pltpu.CompilerParams(dimension_semantics=sem, vmem_limit_bytes=VMEM_LIMIT)


def _sig(x):
    return 1.0 / (1.0 + jnp.exp(-x))


def _dsilu(z, s):
    return s * (1.0 + z * (1.0 - s))


def _mx(x):
    return x.astype(MXU_DTYPE)


def _dot(a, b, dims=(((1,), (0,)), ((), ()))):
    return lax.dot_general(_mx(a), _mx(b), dims, preferred_element_type=F32)


_NT = (((1,), (1,)), ((), ()))
_TN = (((0,), (0,)), ((), ()))


def _dot32(a, b):
    return jnp.dot(a, b, precision=HIGHEST, preferred_element_type=F32)


def _rms_bwd(dy, x, g):
    r = lax.rsqrt(jnp.mean(x * x, axis=-1, keepdims=True) + NORM_EPS)
    t = dy * g
    dx = r * t - x * (r * r * r) * jnp.mean(t * x, axis=-1, keepdims=True)
    return dx, jnp.sum(dy * x * r, axis=0, keepdims=True)


def _rowwise(name, fn, n_rows_total, tm, rows, consts, outs, accs=(), alias=None):
    n_r, n_c, n_o, n_a = len(rows), len(consts), len(outs), len(accs)

    def body(*refs):
        row_refs = refs[:n_r]
        const_refs = refs[n_r:n_r + n_c]
        pos = n_r + n_c + (1 if alias is not None else 0)
        out_refs = refs[pos:pos + n_o]
        acc_refs = refs[pos + n_o:pos + n_o + n_a]
        res = fn(*[r[...] for r in row_refs], *[r[...] for r in const_refs])
        for r, v in zip(out_refs, res[:n_o]):
            r[...] = v.astype(r.dtype)
        if n_a:
            @pl.when(pl.program_id(0) == 0)
            def _():
                for r in acc_refs:
                    r[...] = jnp.zeros_like(r)
            for r, v in zip(acc_refs, res[n_o:]):
                r[...] += v

    in_specs = [pl.BlockSpec((tm, w), functools.partial(lambda i, cb: (i, cb), cb=cb)) for (_, w, cb) in rows]
    in_specs += [pl.BlockSpec(c.shape, lambda i: (0, 0)) for c in consts]
    args = [a for (a, _, _) in rows] + list(consts)
    out_shape, out_specs = [], []
    for o in outs:
        w, dt = o[0], o[1]
        cb, total = (o[2], o[3]) if len(o) == 4 else (0, w)
        out_shape.append(jax.ShapeDtypeStruct((n_rows_total, total), dt))
        out_specs.append(pl.BlockSpec((tm, w), functools.partial(lambda i, cb: (i, cb), cb=cb)))
    io_alias = {}
    if alias is not None:
        in_specs.append(pl.BlockSpec(memory_space=pl.ANY))
        args.append(alias[0])
        io_alias = {len(args) - 1: alias[1]}
    for (r, w) in accs:
        out_shape.append(jax.ShapeDtypeStruct((r, w), F32))
        out_specs.append(pl.BlockSpec((r, w), lambda i: (0, 0)))
    res = pl.pallas_call(body, name=name, grid=(n_rows_total // tm,), in_specs=in_specs, out_specs=out_specs,
                         out_shape=out_shape, input_output_aliases=io_alias,
                         compiler_params=_params("arbitrary"))(*args)
    return res


def _mm_nn(name, a, b, tm, tn, bias=None, resid=None):
    m, k = a.shape
    n = b.shape[1]

    def body(*refs):
        acc = _dot(refs[0][...], refs[1][...])
        pos = 2
        if bias is not None:
            acc = acc + refs[pos][...]
            pos += 1
        if resid is not None:
            acc = acc + refs[pos][...]
            pos += 1
        refs[pos][...] = acc

    in_specs = [pl.BlockSpec((tm, k), lambda j, i: (i, 0)), pl.BlockSpec((k, tn), lambda j, i: (0, j))]
    args = [a, b]
    if bias is not None:
        in_specs.append(pl.BlockSpec((1, tn), lambda j, i: (0, j)))
        args.append(bias)
    if resid is not None:
        in_specs.append(pl.BlockSpec((tm, tn), lambda j, i: (i, j)))
        args.append(resid)
    return pl.pallas_call(body, name=name, grid=(n // tn, m // tm), in_specs=in_specs,
                          out_specs=pl.BlockSpec((tm, tn), lambda j, i: (i, j)),
                          out_shape=jax.ShapeDtypeStruct((m, n), F32),
                          compiler_params=_params("arbitrary", "arbitrary"))(*args)


def _mm_nt(name, a, b, tm, tn):
    m, n = a.shape
    k = b.shape[0]
    steps = n // tn

    def body(a_ref, b_ref, o_ref, acc_ref):
        s = pl.program_id(1)

        @pl.when(s == 0)
        def _():
            acc_ref[...] = jnp.zeros_like(acc_ref)

        acc_ref[...] += _dot(a_ref[...], b_ref[...], _NT)

        @pl.when(s == steps - 1)
        def _():
            o_ref[...] = acc_ref[...]

    return pl.pallas_call(body, name=name, grid=(m // tm, steps),
                          in_specs=[pl.BlockSpec((tm, tn), lambda i, s: (i, s)),
                                    pl.BlockSpec((k, tn), lambda i, s: (0, s))],
                          out_specs=pl.BlockSpec((tm, k), lambda i, s: (i, 0)),
                          out_shape=jax.ShapeDtypeStruct((m, k), F32),
                          scratch_shapes=[pltpu.VMEM((tm, k), F32)],
                          compiler_params=_params("arbitrary", "arbitrary"))(a, b)


def _mm_nt_then(name, a, b, tm, tn, fn, rows, consts, outs, accs=(), alias=None):
    m, n = a.shape
    k = b.shape[0]
    steps = n // tn
    n_r, n_c, n_o, n_a = len(rows), len(consts), len(outs), len(accs)

    def body(*refs):
        a_ref, b_ref = refs[:2]
        row_refs = refs[2:2 + n_r]
        const_refs = refs[2 + n_r:2 + n_r + n_c]
        pos = 2 + n_r + n_c + (1 if alias is not None else 0)
        out_refs = refs[pos:pos + n_o]
        acc_refs = refs[pos + n_o:pos + n_o + n_a]
        mm_ref = refs[pos + n_o + n_a]
        i, s = pl.program_id(0), pl.program_id(1)
        part = _dot(a_ref[...], b_ref[...], _NT)
        if steps > 1:
            @pl.when(s == 0)
            def _():
                mm_ref[...] = jnp.zeros_like(mm_ref)
            mm_ref[...] += part

        @pl.when(s == steps - 1)
        def _():
            res = fn(mm_ref[...] if steps > 1 else part, *[r[...] for r in row_refs], *[r[...] for r in const_refs])
            for r, v in zip(out_refs, res[:n_o]):
                r[...] = v.astype(r.dtype)
            if n_a:
                @pl.when(i == 0)
                def _():
                    for r in acc_refs:
                        r[...] = jnp.zeros_like(r)
                for r, v in zip(acc_refs, res[n_o:]):
                    r[...] += v

    in_specs = [pl.BlockSpec((tm, tn), lambda i, s: (i, s)), pl.BlockSpec((k, tn), lambda i, s: (0, s))]
    in_specs += [pl.BlockSpec((tm, w), functools.partial(lambda i, s, cb: (i, cb), cb=cb)) for (_, w, cb) in rows]
    in_specs += [pl.BlockSpec(c.shape, lambda i, s: (0, 0)) for c in consts]
    args = [a, b] + [r[0] for r in rows] + list(consts)
    out_shape, out_specs = [], []
    for o in outs:
        w, dt = o[0], o[1]
        cb, total = (o[2], o[3]) if len(o) == 4 else (0, w)
        out_shape.append(jax.ShapeDtypeStruct((m, total), dt))
        out_specs.append(pl.BlockSpec((tm, w), functools.partial(lambda i, s, cb: (i, cb), cb=cb)))
    io_alias = {}
    if alias is not None:
        in_specs.append(pl.BlockSpec(memory_space=pl.ANY))
        args.append(alias[0])
        io_alias = {len(args) - 1: alias[1]}
    for (r, w) in accs:
        out_shape.append(jax.ShapeDtypeStruct((r, w), F32))
        out_specs.append(pl.BlockSpec((r, w), lambda i, s: (0, 0)))
    return pl.pallas_call(body, name=name, grid=(m // tm, steps), in_specs=in_specs, out_specs=out_specs,
                          out_shape=out_shape, input_output_aliases=io_alias,
                          scratch_shapes=[pltpu.VMEM((tm, k), F32)],
                          compiler_params=_params("arbitrary", "arbitrary"))(*args)


def _mm_tn(name, a, b, tk, tn):
    t, k = a.shape
    n = b.shape[1]
    steps = t // tk

    def body(a_ref, b_ref, o_ref, acc_ref):
        s = pl.program_id(1)

        @pl.when(s == 0)
        def _():
            acc_ref[...] = jnp.zeros_like(acc_ref)

        acc_ref[...] += _dot(a_ref[...], b_ref[...], _TN)

        @pl.when(s == steps - 1)
        def _():
            o_ref[...] = acc_ref[...]

    return pl.pallas_call(body, name=name, grid=(n // tn, steps),
                          in_specs=[pl.BlockSpec((tk, k), lambda j, s: (s, 0)),
                                    pl.BlockSpec((tk, tn), lambda j, s: (s, j))],
                          out_specs=pl.BlockSpec((k, tn), lambda j, s: (0, j)),
                          out_shape=jax.ShapeDtypeStruct((k, n), F32),
                          scratch_shapes=[pltpu.VMEM((k, tn), F32)],
                          compiler_params=_params("arbitrary", "arbitrary"))(a, b)


def _hg_chunk_terms(q, f, lb):
    sig = _sig(f)
    fv = lb + (1.0 - lb) * sig
    kk = (1.0 - lb) * (1.0 - sig)
    row = lax.broadcasted_iota(jnp.int32, (HG_CHUNK, HG_CHUNK), 0)
    col = lax.broadcasted_iota(jnp.int32, (HG_CHUNK, HG_CHUNK), 1)
    b = _dot32((row >= col).astype(F32), jnp.log(fv))
    b_mid = b[HG_CHUNK // 2 - 1:HG_CHUNK // 2, :]
    b_last = b[HG_CHUNK - 1:HG_CHUNK, :]
    e_mid = jnp.exp(b - b_mid)
    e_mid_inv = jnp.exp(b_mid - b)
    e_b = jnp.exp(b)
    e_last = jnp.exp(b_last - b)
    return sig, fv, kk, row >= col, row <= col, q * e_mid, kk * e_mid_inv, e_mid, e_mid_inv, e_b, e_last, jnp.exp(b_last)


def _hgrn2_fwd(proj, hg_lb, hg_norm_g, t_len, tb):
    nck = tb // HG_CHUNK

    def body(p_ref, lb_ref, gn_ref, o_ref, act_ref, sp_ref, st_ref):
        @pl.when(pl.program_id(0) == 0)
        def _():
            st_ref[...] = jnp.zeros_like(st_ref)

        for c in range(nck):
            r = pl.ds(c * HG_CHUNK, HG_CHUNK)
            for h in range(HG_HEADS):
                hs = pl.ds(h * HG_DIM, HG_DIM)
                lb = _sig(lb_ref[0:1, hs] - lb_ref[1:2, hs])
                q = p_ref[r, pl.ds(h * HG_DIM, HG_DIM)]
                f = p_ref[r, pl.ds(1024 + h * HG_DIM, HG_DIM)]
                v = p_ref[r, pl.ds(2048 + h * HG_DIM, HG_DIM)]
                _, _, kk, causal, _, a, bm, _, _, e_b, e_last, dc = _hg_chunk_terms(q, f, lb)
                scores = jnp.where(causal, _dot(a, bm, _NT), 0.0)
                st = st_ref[h]
                o = _dot(scores, v) + _dot(q * e_b, st, _NT)
                sp_ref[h, c] = st
                st_ref[h] = dc * st + _dot(v, kk * e_last, _TN)
                o_ref[r, hs] = o

        for h in range(HG_HEADS):
            hs = pl.ds(h * HG_DIM, HG_DIM)
            o = o_ref[:, hs]
            rr = lax.rsqrt(jnp.mean(o * o, axis=-1, keepdims=True) + NORM_EPS)
            g = p_ref[:, pl.ds(3072 + h * HG_DIM, HG_DIM)]
            act_ref[:, hs] = (o * rr * gn_ref[:, hs] * (g * _sig(g))).astype(act_ref.dtype)

    nb = t_len // tb
    return pl.pallas_call(
        body, name="hgrn2_fwd", grid=(nb,),
        in_specs=[pl.BlockSpec((tb, 4096), lambda i: (i, 0)),
                  pl.BlockSpec((2, 1024), lambda i: (0, 0)),
                  pl.BlockSpec((1, 1024), lambda i: (0, 0))],
        out_specs=[pl.BlockSpec((tb, 1024), lambda i: (i, 0)),
                   pl.BlockSpec((tb, 1024), lambda i: (i, 0)),
                   pl.BlockSpec((HG_HEADS, nck, HG_DIM, HG_DIM), lambda i: (0, i, 0, 0))],
        out_shape=[jax.ShapeDtypeStruct((t_len, 1024), F32),
                   jax.ShapeDtypeStruct((t_len, 1024), MXU_DTYPE),
                   jax.ShapeDtypeStruct((HG_HEADS, t_len // HG_CHUNK, HG_DIM, HG_DIM), F32)],
        scratch_shapes=[pltpu.VMEM((HG_HEADS, HG_DIM, HG_DIM), F32)],
        compiler_params=_params("arbitrary"))(proj, hg_lb, hg_norm_g)


def _hgrn2_bwd(proj, d_o, s_prev, hg_lb, dproj, t_len, tb):
    nck = tb // HG_CHUNK
    nb = t_len // tb

    def body(p_ref, do_ref, sp_ref, lb_ref, _, dp_ref, dlb_ref, ds_ref, acc_ref):
        @pl.when(pl.program_id(0) == 0)
        def _():
            ds_ref[...] = jnp.zeros_like(ds_ref)
            acc_ref[...] = jnp.zeros_like(acc_ref)

        for c in reversed(range(nck)):
            r = pl.ds(c * HG_CHUNK, HG_CHUNK)
            for h in range(HG_HEADS):
                hs = pl.ds(h * HG_DIM, HG_DIM)
                lb = _sig(lb_ref[0:1, hs] - lb_ref[1:2, hs])
                q = p_ref[r, pl.ds(h * HG_DIM, HG_DIM)]
                f = p_ref[r, pl.ds(1024 + h * HG_DIM, HG_DIM)]
                v = p_ref[r, pl.ds(2048 + h * HG_DIM, HG_DIM)]
                do = do_ref[r, hs]
                sig, fv, kk, causal, anti, a, bm, e_mid, e_mid_inv, e_b, e_last, dc = _hg_chunk_terms(q, f, lb)
                qd = q * e_b
                kd = kk * e_last
                st = sp_ref[h, c]
                dst = ds_ref[h]
                scores = jnp.where(causal, _dot(a, bm, _NT), 0.0)
                dscores = jnp.where(causal, _dot(do, v, _NT), 0.0)
                dv = _dot(scores, do, _TN) + _dot(kd, dst, _NT)
                da = _dot(dscores, bm)
                dbm = _dot(dscores, a, _TN)
                dqd = _dot(do, st)
                dkd = _dot(v, dst)
                ddc = jnp.sum(dst * st, axis=0, keepdims=True)
                ds_ref[h] = _dot(do, qd, _TN) + dc * dst
                dq = da * e_mid + dqd * e_b
                dk = dbm * e_mid_inv + dkd * e_last
                db = da * a - dbm * bm + dqd * qd - dkd * kd
                extra = jnp.sum(dkd * kd, axis=0, keepdims=True) + ddc * dc
                dlogf = _dot32(anti.astype(F32), db) + extra
                dfv_k = dlogf / fv - dk
                dp_ref[r, pl.ds(h * HG_DIM, HG_DIM)] = dq
                dp_ref[r, pl.ds(1024 + h * HG_DIM, HG_DIM)] = dfv_k * (1.0 - lb) * sig * (1.0 - sig)
                dp_ref[r, pl.ds(2048 + h * HG_DIM, HG_DIM)] = dv
                acc_ref[:, hs] += jnp.sum(dfv_k * (1.0 - sig), axis=0, keepdims=True)

        @pl.when(pl.program_id(0) == nb - 1)
        def _():
            lb_all = _sig(lb_ref[0:1, :] - lb_ref[1:2, :])
            g0 = acc_ref[...] * lb_all * (1.0 - lb_all)
            dlb_ref[0:1, :] = g0
            dlb_ref[1:2, :] = -g0

    return pl.pallas_call(
        body, name="hgrn2_bwd", grid=(nb,),
        in_specs=[pl.BlockSpec((tb, 3072), lambda i: (nb - 1 - i, 0)),
                  pl.BlockSpec((tb, 1024), lambda i: (nb - 1 - i, 0)),
                  pl.BlockSpec((HG_HEADS, nck, HG_DIM, HG_DIM), lambda i: (0, nb - 1 - i, 0, 0)),
                  pl.BlockSpec((2, 1024), lambda i: (0, 0)),
                  pl.BlockSpec(memory_space=pl.ANY)],
        out_specs=[pl.BlockSpec((tb, 3072), lambda i: (nb - 1 - i, 0)),
                   pl.BlockSpec((2, 1024), lambda i: (0, 0))],
        out_shape=[jax.ShapeDtypeStruct((t_len, IN_COLS), F32), jax.ShapeDtypeStruct((2, 1024), F32)],
        scratch_shapes=[pltpu.VMEM((HG_HEADS, HG_DIM, HG_DIM), F32), pltpu.VMEM((1, 1024), F32)],
        input_output_aliases={4: 0},
        compiler_params=_params("arbitrary"))(proj, d_o, s_prev, hg_lb, dproj)


def _dot01(m01, x):
    m = m01.astype(MXU_DTYPE)
    hi = x.astype(MXU_DTYPE)
    r1 = x - hi.astype(F32)
    mid = r1.astype(MXU_DTYPE)
    lo = (r1 - mid.astype(F32)).astype(MXU_DTYPE)
    dot = lambda v: jnp.dot(m, v, preferred_element_type=F32)
    return dot(hi) + dot(mid) + dot(lo)


def _chunk_rows(x, offset, nck):
    return jnp.concatenate([jnp.broadcast_to(x[c * HG_CHUNK + offset:c * HG_CHUNK + offset + 1, :],
                                             (HG_CHUNK, x.shape[1])) for c in range(nck)], axis=0)


def _hg_block_terms(q, f, lb, tb):
    nck = tb // HG_CHUNK
    sig = _sig(f)
    fv = lb + (1.0 - lb) * sig
    kk = (1.0 - lb) * (1.0 - sig)
    row = lax.broadcasted_iota(jnp.int32, (tb, tb), 0)
    col = lax.broadcasted_iota(jnp.int32, (tb, tb), 1)
    same = jnp.right_shift(row, 6) == jnp.right_shift(col, 6)
    causal, anti = same & (row >= col), same & (row <= col)
    b = _dot01(causal, jnp.log(fv))
    b_mid, b_last = _chunk_rows(b, HG_CHUNK // 2 - 1, nck), _chunk_rows(b, HG_CHUNK - 1, nck)
    e_mid, e_mid_inv = jnp.exp(b - b_mid), jnp.exp(b_mid - b)
    e_b, e_last = jnp.exp(b), jnp.exp(b_last - b)
    dcs = [jnp.exp(b[c * HG_CHUNK + HG_CHUNK - 1:(c + 1) * HG_CHUNK, :]) for c in range(nck)]
    return sig, fv, kk, causal, anti, e_mid, e_mid_inv, e_b, e_last, dcs


def _hgrn2_fwd2(proj, hg_lb, hg_norm_g, t_len, tb):
    nck = tb // HG_CHUNK

    def body(p_ref, lb_ref, gn_ref, o_ref, act_ref, sp_ref, st_ref, a_s, bm_s, qd_s, kd_s, v_s):
        @pl.when(pl.program_id(0) == 0)
        def _():
            st_ref[...] = jnp.zeros_like(st_ref)

        lb = _sig(lb_ref[0:1, :] - lb_ref[1:2, :])
        q = p_ref[:, pl.ds(0, 1024)]
        _, _, kk, causal, _, e_mid, e_mid_inv, e_b, e_last, dcs = _hg_block_terms(q, p_ref[:, pl.ds(1024, 1024)],
                                                                                   lb, tb)
        a_s[...] = _mx(q * e_mid)
        bm_s[...] = _mx(kk * e_mid_inv)
        qd_s[...] = _mx(q * e_b)
        kd_s[...] = _mx(kk * e_last)
        v_s[...] = _mx(p_ref[:, pl.ds(2048, 1024)])
        for h in range(HG_HEADS):
            hs = pl.ds(h * HG_DIM, HG_DIM)
            scores = jnp.where(causal, _dot(a_s[:, hs], bm_s[:, hs], _NT), 0.0)
            o_ref[:, hs] = _dot(scores, v_s[:, hs])
        for h in range(HG_HEADS):
            hs = pl.ds(h * HG_DIM, HG_DIM)
            incs = [_dot(v_s[pl.ds(c * HG_CHUNK, HG_CHUNK), hs], kd_s[pl.ds(c * HG_CHUNK, HG_CHUNK), hs], _TN)
                    for c in range(nck)]
            st = st_ref[h]
            for c in range(nck):
                sp_ref[h, c] = st
                st = dcs[c][:, h * HG_DIM:(h + 1) * HG_DIM] * st + incs[c]
            st_ref[h] = st
        for h in range(HG_HEADS):
            hs = pl.ds(h * HG_DIM, HG_DIM)
            for c in range(nck):
                r = pl.ds(c * HG_CHUNK, HG_CHUNK)
                o_ref[r, hs] += _dot(qd_s[r, hs], sp_ref[h, c], _NT)
        for h in range(HG_HEADS):
            hs = pl.ds(h * HG_DIM, HG_DIM)
            o = o_ref[:, hs]
            rr = lax.rsqrt(jnp.mean(o * o, axis=-1, keepdims=True) + NORM_EPS)
            g = p_ref[:, pl.ds(3072 + h * HG_DIM, HG_DIM)]
            act_ref[:, hs] = (o * rr * gn_ref[:, hs] * (g * _sig(g))).astype(act_ref.dtype)

    nb = t_len // tb
    return pl.pallas_call(
        body, name="hgrn2_fwd", grid=(nb,),
        in_specs=[pl.BlockSpec((tb, 4096), lambda i: (i, 0)),
                  pl.BlockSpec((2, 1024), lambda i: (0, 0)),
                  pl.BlockSpec((1, 1024), lambda i: (0, 0))],
        out_specs=[pl.BlockSpec((tb, 1024), lambda i: (i, 0)),
                   pl.BlockSpec((tb, 1024), lambda i: (i, 0)),
                   pl.BlockSpec((HG_HEADS, nck, HG_DIM, HG_DIM), lambda i: (0, i, 0, 0))],
        out_shape=[jax.ShapeDtypeStruct((t_len, 1024), F32),
                   jax.ShapeDtypeStruct((t_len, 1024), MXU_DTYPE),
                   jax.ShapeDtypeStruct((HG_HEADS, t_len // HG_CHUNK, HG_DIM, HG_DIM), F32)],
        scratch_shapes=[pltpu.VMEM((HG_HEADS, HG_DIM, HG_DIM), F32)] + [pltpu.VMEM((tb, 1024), MXU_DTYPE)] * 5,
        compiler_params=_params("arbitrary"))(proj, hg_lb, hg_norm_g)


def _hgrn2_bwd2(proj, d_o, s_prev, hg_lb, dproj, t_len, tb):
    nck = tb // HG_CHUNK
    nb = t_len // tb

    def body(p_ref, do_ref, sp_ref, lb_ref, _, dp_ref, dlb_ref, ds_ref, acc_ref,
             a_s, bm_s, qd_s, kd_s, v_s, do_s, da_s, dbm_s, dqd_s, dkd_s, ex_s):
        @pl.when(pl.program_id(0) == 0)
        def _():
            ds_ref[...] = jnp.zeros_like(ds_ref)
            acc_ref[...] = jnp.zeros_like(acc_ref)

        lb = _sig(lb_ref[0:1, :] - lb_ref[1:2, :])
        q = p_ref[:, pl.ds(0, 1024)]
        sig, fv, kk, causal, anti, e_mid, e_mid_inv, e_b, e_last, dcs = _hg_block_terms(
            q, p_ref[:, pl.ds(1024, 1024)], lb, tb)
        a, bm, qd, kd = q * e_mid, kk * e_mid_inv, q * e_b, kk * e_last
        a_s[...] = _mx(a)
        bm_s[...] = _mx(bm)
        qd_s[...] = _mx(qd)
        kd_s[...] = _mx(kd)
        v_s[...] = _mx(p_ref[:, pl.ds(2048, 1024)])
        do_s[...] = _mx(do_ref[...])
        for h in range(HG_HEADS):
            hs = pl.ds(h * HG_DIM, HG_DIM)
            scores = jnp.where(causal, _dot(a_s[:, hs], bm_s[:, hs], _NT), 0.0)
            dscores = _mx(jnp.where(causal, _dot(do_s[:, hs], v_s[:, hs], _NT), 0.0))
            dp_ref[:, pl.ds(2048 + h * HG_DIM, HG_DIM)] = _dot(scores, do_s[:, hs], _TN)
            da_s[:, hs] = _dot(dscores, bm_s[:, hs])
            dbm_s[:, hs] = _dot(dscores, a_s[:, hs], _TN)
        for h in range(HG_HEADS):
            hs = pl.ds(h * HG_DIM, HG_DIM)
            ups = [_dot(do_s[pl.ds(c * HG_CHUNK, HG_CHUNK), hs], qd_s[pl.ds(c * HG_CHUNK, HG_CHUNK), hs], _TN)
                   for c in range(nck)]
            dst = ds_ref[h]
            for c in reversed(range(nck)):
                r = pl.ds(c * HG_CHUNK, HG_CHUNK)
                st = sp_ref[h, c]
                dc = dcs[c][:, h * HG_DIM:(h + 1) * HG_DIM]
                dp_ref[r, pl.ds(2048 + h * HG_DIM, HG_DIM)] += _dot(kd_s[r, hs], dst, _NT)
                dqd_s[r, hs] = _dot(do_s[r, hs], st)
                dkd_s[r, hs] = _dot(v_s[r, hs], dst)
                ex_s[c:c + 1, hs] = jnp.sum(dst * st, axis=0, keepdims=True) * dc
                dst = ups[c] + dc * dst
            ds_ref[h] = dst
        da, dbm, dqd, dkd = da_s[...], dbm_s[...], dqd_s[...], dkd_s[...]
        dq = da * e_mid + dqd * e_b
        dk = dbm * e_mid_inv + dkd * e_last
        db = da * a - dbm * bm + dqd * qd - dkd * kd
        dkk = dkd * kd
        extra = jnp.concatenate(
            [jnp.broadcast_to(jnp.sum(dkk[c * HG_CHUNK:(c + 1) * HG_CHUNK], axis=0, keepdims=True)
                              + ex_s[c:c + 1, :], (HG_CHUNK, 1024)) for c in range(nck)], axis=0)
        dlogf = _dot01(anti, db) + extra
        dfv_k = dlogf / fv - dk
        dp_ref[:, pl.ds(0, 1024)] = dq
        dp_ref[:, pl.ds(1024, 1024)] = dfv_k * (1.0 - lb) * sig * (1.0 - sig)
        acc_ref[...] += jnp.sum(dfv_k * (1.0 - sig), axis=0, keepdims=True)

        @pl.when(pl.program_id(0) == nb - 1)
        def _():
            g0 = acc_ref[...] * lb * (1.0 - lb)
            dlb_ref[0:1, :] = g0
            dlb_ref[1:2, :] = -g0

    return pl.pallas_call(
        body, name="hgrn2_bwd", grid=(nb,),
        in_specs=[pl.BlockSpec((tb, 3072), lambda i: (nb - 1 - i, 0)),
                  pl.BlockSpec((tb, 1024), lambda i: (nb - 1 - i, 0)),
                  pl.BlockSpec((HG_HEADS, nck, HG_DIM, HG_DIM), lambda i: (0, nb - 1 - i, 0, 0)),
                  pl.BlockSpec((2, 1024), lambda i: (0, 0)),
                  pl.BlockSpec(memory_space=pl.ANY)],
        out_specs=[pl.BlockSpec((tb, 3072), lambda i: (nb - 1 - i, 0)),
                   pl.BlockSpec((2, 1024), lambda i: (0, 0))],
        out_shape=[jax.ShapeDtypeStruct((t_len, IN_COLS), F32), jax.ShapeDtypeStruct((2, 1024), F32)],
        scratch_shapes=[pltpu.VMEM((HG_HEADS, HG_DIM, HG_DIM), F32), pltpu.VMEM((1, 1024), F32)]
                       + [pltpu.VMEM((tb, 1024), MXU_DTYPE)] * 6 + [pltpu.VMEM((tb, 1024), F32)] * 4
                       + [pltpu.VMEM((SUBLANES, 1024), F32)],
        input_output_aliases={4: 0},
        compiler_params=_params("arbitrary"))(proj, d_o, s_prev, hg_lb, dproj)


def _s5_prep(a_re, a_im, log_dt, b_re_t, b_im_t):
    def body(ar_ref, ai_ref, ldt_ref, br_ref, bi_ref, lam_ref, pr_ref, pi_ref, bbr_ref, bbi_ref):
        ar, ai = ar_ref[...], ai_ref[...]
        dt = jnp.exp(ldt_ref[...])
        mag = jnp.exp(ar * dt)
        lr, li = mag * jnp.cos(ai * dt), mag * jnp.sin(ai * dt)
        den = ar * ar + ai * ai
        nr = lr - 1.0
        sr = (nr * ar + li * ai) / den
        si = (li * ar - nr * ai) / den
        lam_ref[0:1, :] = lr
        lam_ref[1:2, :] = li
        cr, ci = lr, li
        for i in range(SUBLANES):
            pr_ref[i:i + 1, :] = cr
            pi_ref[i:i + 1, :] = ci
            cr, ci = cr * lr - ci * li, cr * li + ci * lr
        br, bi = br_ref[...], bi_ref[...]
        bbr_ref[...] = sr * br - si * bi
        bbi_ref[...] = sr * bi + si * br

    whole = pl.BlockSpec(memory_space=pltpu.VMEM)
    return pl.pallas_call(
        body, name="s5_prep", in_specs=[whole] * 5, out_specs=[whole] * 5,
        out_shape=[jax.ShapeDtypeStruct((2, S5_LANES), F32), jax.ShapeDtypeStruct((SUBLANES, S5_LANES), F32),
                   jax.ShapeDtypeStruct((SUBLANES, S5_LANES), F32), jax.ShapeDtypeStruct((S5_GROUP, S5_LANES), F32),
                   jax.ShapeDtypeStruct((S5_GROUP, S5_LANES), F32)])(a_re, a_im, log_dt, b_re_t, b_im_t)


def _s5_prep_bwd(a_re, a_im, log_dt, b_re_t, b_im_t, dlam, dbbr, dbbi):
    def body(ar_ref, ai_ref, ldt_ref, br_ref, bi_ref, dlam_ref, dbbr_ref, dbbi_ref,
             dar_ref, dai_ref, dldt_ref, dbr_ref, dbi_ref):
        ar, ai = ar_ref[...], ai_ref[...]
        dt = jnp.exp(ldt_ref[...])
        mag = jnp.exp(ar * dt)
        cs, sn = jnp.cos(ai * dt), jnp.sin(ai * dt)
        lr, li = mag * cs, mag * sn
        den = ar * ar + ai * ai
        nr = lr - 1.0
        sr = (nr * ar + li * ai) / den
        si = (li * ar - nr * ai) / den
        br, bi = br_ref[...], bi_ref[...]
        gbr, gbi = dbbr_ref[...], dbbi_ref[...]
        dbr_ref[...] = sr * gbr + si * gbi
        dbi_ref[...] = sr * gbi - si * gbr
        dsr = jnp.sum(gbr * br + gbi * bi, axis=0, keepdims=True)
        dsi = jnp.sum(gbi * br - gbr * bi, axis=0, keepdims=True)
        dnr = (dsr * ar - dsi * ai) / den
        dli = dlam_ref[1:2, :] + (dsr * ai + dsi * ar) / den
        dlr = dlam_ref[0:1, :] + dnr
        dden = -(dsr * sr + dsi * si) / den
        dar = (dsr * nr + dsi * li) / den + dden * 2.0 * ar
        dai = (dsr * li - dsi * nr) / den + dden * 2.0 * ai
        dmag = dlr * cs + dli * sn
        dth = mag * (dli * cs - dlr * sn)
        dar_ref[...] = dar + dmag * mag * dt
        dai_ref[...] = dai + dth * dt
        ddt = (dmag * mag * ar + dth * ai) * dt
        lane = lax.broadcasted_iota(jnp.int32, (S5_LANES, 128), 0) // S5_STATE
        grp = lax.broadcasted_iota(jnp.int32, (S5_LANES, 128), 1)
        dldt_ref[...] = _dot32(jnp.broadcast_to(ddt, (SUBLANES, S5_LANES)), (lane == grp).astype(F32))

    whole = pl.BlockSpec(memory_space=pltpu.VMEM)
    return pl.pallas_call(
        body, name="s5_prep_bwd", in_specs=[whole] * 8, out_specs=[whole] * 5,
        out_shape=[jax.ShapeDtypeStruct((1, S5_LANES), F32), jax.ShapeDtypeStruct((1, S5_LANES), F32),
                   jax.ShapeDtypeStruct((SUBLANES, 128), F32), jax.ShapeDtypeStruct((S5_GROUP, S5_LANES), F32),
                   jax.ShapeDtypeStruct((S5_GROUP, S5_LANES), F32)])(a_re, a_im, log_dt, b_re_t, b_im_t, dlam, dbbr,
                                                                      dbbi)


S5_LANE_CHUNK = 512


def _shift_rows(x, s, rowid):
    if s > 0:
        return jnp.where(rowid >= s, pltpu.roll(x, s, 0), 0.0)
    return jnp.where(rowid < SUBLANES + s, pltpu.roll(x, SUBLANES + s, 0), 0.0)


def _scan8(xr, xi, pr, pi, sign, rowid):
    for s, row in ((1, 0), (2, 1), (4, 3)):
        lr, li = pr[row:row + 1, :], pi[row:row + 1, :]
        sr, si = _shift_rows(xr, sign * s, rowid), _shift_rows(xi, sign * s, rowid)
        xr, xi = xr + lr * sr - li * si, xi + lr * si + li * sr
    return xr, xi


def _s5_fwd(proj, pw_re, pw_im, bbr_bd, bbi_bd, crt_bd, cit_bd, d_row, t_len, tb):
    ngrp = tb // SUBLANES

    def body(u_ref, pr_ref, pi_ref, bbr_ref, bbi_ref, crt_ref, cit_ref, d_ref,
             hr_ref, hi_ref, ypre_ref, ys_ref, cr_ref, ci_ref):
        @pl.when(pl.program_id(0) == 0)
        def _():
            cr_ref[...] = jnp.zeros_like(cr_ref)
            ci_ref[...] = jnp.zeros_like(ci_ref)

        u = u_ref[...]
        hr_ref[...] = _dot(u, bbr_ref[...])
        hi_ref[...] = _dot(u, bbi_ref[...])
        rowid = lax.broadcasted_iota(jnp.int32, (SUBLANES, S5_LANE_CHUNK), 0)
        for lc in range(S5_LANES // S5_LANE_CHUNK):
            ls = pl.ds(lc * S5_LANE_CHUNK, S5_LANE_CHUNK)
            pr, pi = pr_ref[:, ls], pi_ref[:, ls]

            def group(g, carry, ls=ls, pr=pr, pi=pi):
                cr, ci = carry
                r = pl.ds(pl.multiple_of(g * SUBLANES, SUBLANES), SUBLANES)
                xr, xi = _scan8(hr_ref[r, ls], hi_ref[r, ls], pr, pi, 1, rowid)
                xr, xi = xr + pr * cr - pi * ci, xi + pr * ci + pi * cr
                hr_ref[r, ls] = xr
                hi_ref[r, ls] = xi
                return xr[SUBLANES - 1:SUBLANES, :], xi[SUBLANES - 1:SUBLANES, :]

            cr, ci = lax.fori_loop(0, ngrp, group, (cr_ref[:, ls], ci_ref[:, ls]))
            cr_ref[:, ls] = cr
            ci_ref[:, ls] = ci
        y = _dot(hr_ref[...], crt_ref[...]) - _dot(hi_ref[...], cit_ref[...]) + d_ref[...] * u
        ypre_ref[...] = y
        ys_ref[...] = jax.nn.gelu(y, approximate=True).astype(ys_ref.dtype)

    whole = pl.BlockSpec(memory_space=pltpu.VMEM)
    return pl.pallas_call(
        body, name="s5_fwd", grid=(t_len // tb,),
        in_specs=[pl.BlockSpec((tb, S5_WIDTH), lambda i: (i, 4096 // S5_WIDTH))] + [whole] * 7,
        out_specs=[pl.BlockSpec((tb, S5_LANES), lambda i: (i, 0)), pl.BlockSpec((tb, S5_LANES), lambda i: (i, 0)),
                   pl.BlockSpec((tb, S5_WIDTH), lambda i: (i, 0)), pl.BlockSpec((tb, S5_WIDTH), lambda i: (i, 0))],
        out_shape=[jax.ShapeDtypeStruct((t_len, S5_LANES), F32), jax.ShapeDtypeStruct((t_len, S5_LANES), F32),
                   jax.ShapeDtypeStruct((t_len, S5_WIDTH), F32), jax.ShapeDtypeStruct((t_len, S5_WIDTH), MXU_DTYPE)],
        scratch_shapes=[pltpu.VMEM((1, S5_LANES), F32), pltpu.VMEM((1, S5_LANES), F32)],
        compiler_params=_params("arbitrary"))(proj, pw_re, pw_im, bbr_bd, bbi_bd, crt_bd, cit_bd, d_row)


def _dgelu(x):
    c, a = 0.7978845608028654, 0.044715
    th = jnp.tanh(c * (x + a * x * x * x))
    return 0.5 * (1.0 + th) + 0.5 * x * (1.0 - th * th) * c * (1.0 + 3.0 * a * x * x)


def _s5_bwd(dgelu, y_pre, proj, h_re, h_im, pwr_re, pwr_im, bbr_bd, bbi_bd, cr_bd, ci_bd, d_row, dproj, t_len, tb):
    ngrp = tb // SUBLANES
    nb = t_len // tb

    def body(dg_ref, yp_ref, u_ref, hr_ref, hi_ref, pr_ref, pi_ref, bbr_ref, bbi_ref, cr_ref, ci_ref, d_ref, _,
             du_ref, dbbr_ref, dbbi_ref, dcr_ref, dci_ref, dd_ref, dlam_ref,
             gr_ref, gi_ref, car_ref, cai_ref, abr_ref, abi_ref, acr_ref, aci_ref, ad_ref, alr_ref, ali_ref, sem):
        @pl.when(pl.program_id(0) == 0)
        def _():
            for ref in (car_ref, cai_ref, abr_ref, abi_ref, acr_ref, aci_ref, ad_ref, alr_ref, ali_ref):
                ref[...] = jnp.zeros_like(ref)

        u = u_ref[...]
        dy = dg_ref[...] * _dgelu(yp_ref[...])
        gr_ref[...] = _dot(dy, cr_ref[...])
        gi_ref[...] = -_dot(dy, ci_ref[...])
        rowid = lax.broadcasted_iota(jnp.int32, (SUBLANES, S5_LANE_CHUNK), 0)
        for lc in range(S5_LANES // S5_LANE_CHUNK):
            ls = pl.ds(lc * S5_LANE_CHUNK, S5_LANE_CHUNK)
            pr, pi = pr_ref[:, ls], pi_ref[:, ls]
            fwd_rows_r = jnp.concatenate([pr[7:8], pr[6:7], pr[6:7], pr[4:5]], axis=0)
            fwd_rows_i = jnp.concatenate([pi[7:8], pi[6:7], pi[6:7], pi[4:5]], axis=0)

            def group(j, carry, ls=ls, pr=pr, pi=pi, fr=fwd_rows_r, fi=fwd_rows_i):
                cr, ci, slr, sli = carry
                g = ngrp - 1 - j
                r = pl.ds(pl.multiple_of(g * SUBLANES, SUBLANES), SUBLANES)
                xr, xi = _scan8(gr_ref[r, ls], gi_ref[r, ls], fr, fi, -1, rowid)
                xr, xi = xr + pr * cr - pi * ci, xi + pr * ci + pi * cr
                gr_ref[r, ls] = xr
                gi_ref[r, ls] = xi
                nr = jnp.where(rowid == SUBLANES - 1, cr, pltpu.roll(xr, SUBLANES - 1, 0))
                ni = jnp.where(rowid == SUBLANES - 1, ci, pltpu.roll(xi, SUBLANES - 1, 0))
                hr, hi = hr_ref[r, ls], hi_ref[r, ls]
                slr = slr + nr * hr + ni * hi
                sli = sli + ni * hr - nr * hi
                return xr[0:1, :], xi[0:1, :], slr, sli

            zero = jnp.zeros((SUBLANES, S5_LANE_CHUNK), F32)
            cr, ci, slr, sli = lax.fori_loop(0, ngrp, group, (car_ref[:, ls], cai_ref[:, ls], zero, zero))
            car_ref[:, ls] = cr
            cai_ref[:, ls] = ci
            alr_ref[:, ls] += jnp.sum(slr, axis=0, keepdims=True)
            ali_ref[:, ls] += jnp.sum(sli, axis=0, keepdims=True)
        gr, gi = gr_ref[...], gi_ref[...]
        du_ref[...] = _dot(gr, bbr_ref[...], _NT) + _dot(gi, bbi_ref[...], _NT) + d_ref[...] * dy
        abr_ref[...] += _dot(u, gr, _TN)
        abi_ref[...] += _dot(u, gi, _TN)
        acr_ref[...] += _dot(hr_ref[...], dy, _TN)
        aci_ref[...] -= _dot(hi_ref[...], dy, _TN)
        ad_ref[...] += jnp.sum(dy * u, axis=0, keepdims=True)

        @pl.when(pl.program_id(0) == nb - 1)
        def _():
            dd_ref[...] = ad_ref[...]
            dlam_ref[0:1, :] = alr_ref[...]
            dlam_ref[1:2, :] = ali_ref[...]
            copies = [pltpu.make_async_copy(s, d, sem.at[k]) for k, (s, d) in enumerate(
                ((abr_ref, dbbr_ref), (abi_ref, dbbi_ref), (acr_ref, dcr_ref), (aci_ref, dci_ref)))]
            for cp in copies:
                cp.start()
            for cp in copies:
                cp.wait()

    whole = pl.BlockSpec(memory_space=pltpu.VMEM)
    hbm = pl.BlockSpec(memory_space=pl.ANY)
    rev = lambda i: (nb - 1 - i, 0)
    return pl.pallas_call(
        body, name="s5_bwd", grid=(nb,),
        in_specs=[pl.BlockSpec((tb, S5_WIDTH), rev), pl.BlockSpec((tb, S5_WIDTH), rev),
                  pl.BlockSpec((tb, S5_WIDTH), lambda i: (nb - 1 - i, 4096 // S5_WIDTH)),
                  pl.BlockSpec((tb, S5_LANES), rev), pl.BlockSpec((tb, S5_LANES), rev)] + [whole] * 7 + [hbm],
        out_specs=[pl.BlockSpec((tb, S5_WIDTH), lambda i: (nb - 1 - i, 4096 // S5_WIDTH)), hbm, hbm, hbm, hbm,
                   pl.BlockSpec((1, S5_WIDTH), lambda i: (0, 0)), pl.BlockSpec((2, S5_LANES), lambda i: (0, 0))],
        out_shape=[jax.ShapeDtypeStruct((t_len, IN_COLS), F32),
                   jax.ShapeDtypeStruct((S5_WIDTH, S5_LANES), F32), jax.ShapeDtypeStruct((S5_WIDTH, S5_LANES), F32),
                   jax.ShapeDtypeStruct((S5_LANES, S5_WIDTH), F32), jax.ShapeDtypeStruct((S5_LANES, S5_WIDTH), F32),
                   jax.ShapeDtypeStruct((1, S5_WIDTH), F32), jax.ShapeDtypeStruct((2, S5_LANES), F32)],
        scratch_shapes=[pltpu.VMEM((tb, S5_LANES), F32), pltpu.VMEM((tb, S5_LANES), F32),
                        pltpu.VMEM((1, S5_LANES), F32), pltpu.VMEM((1, S5_LANES), F32),
                        pltpu.VMEM((S5_WIDTH, S5_LANES), F32), pltpu.VMEM((S5_WIDTH, S5_LANES), F32),
                        pltpu.VMEM((S5_LANES, S5_WIDTH), F32), pltpu.VMEM((S5_LANES, S5_WIDTH), F32),
                        pltpu.VMEM((1, S5_WIDTH), F32), pltpu.VMEM((1, S5_LANES), F32),
                        pltpu.VMEM((1, S5_LANES), F32), pltpu.SemaphoreType.DMA((4,))],
        input_output_aliases={12: 0},
        compiler_params=_params("arbitrary"))(dgelu, y_pre, proj, h_re, h_im, pwr_re, pwr_im, bbr_bd, bbi_bd, cr_bd,
                                              ci_bd, d_row, dproj)


S5_BLOCKS = 4
S5_BW = S5_WIDTH // S5_BLOCKS
S5_BL = S5_LANES // S5_BLOCKS
S5_LANE_BLOCKS = S5_LANES // 128
S5_SCAN_BLOCKS = 4


def _s5_powers(a_re, a_im, log_dt, b_re_t, b_im_t, seg):
    def body(ar_ref, ai_ref, ldt_ref, br_ref, bi_ref, pr_ref, pi_ref, bbr_ref, bbi_ref):
        ar, ai = ar_ref[...], ai_ref[...]
        dt = jnp.exp(ldt_ref[...])
        mag = jnp.exp(ar * dt)
        lr, li = mag * jnp.cos(ai * dt), mag * jnp.sin(ai * dt)
        den = ar * ar + ai * ai
        nr = lr - 1.0
        sr = (nr * ar + li * ai) / den
        si = (li * ar - nr * ai) / den
        cr, ci = lr, li
        for i in range(seg):
            pr_ref[i:i + 1, :] = cr
            pi_ref[i:i + 1, :] = ci
            cr, ci = cr * lr - ci * li, cr * li + ci * lr
        br, bi = br_ref[...], bi_ref[...]
        bbr_ref[...] = sr * br - si * bi
        bbi_ref[...] = sr * bi + si * br

    whole = pl.BlockSpec(memory_space=pltpu.VMEM)
    return pl.pallas_call(
        body, name="s5_prep", in_specs=[whole] * 5, out_specs=[whole] * 4,
        out_shape=[jax.ShapeDtypeStruct((seg, S5_LANES), F32), jax.ShapeDtypeStruct((seg, S5_LANES), F32),
                   jax.ShapeDtypeStruct((S5_GROUP, S5_LANES), F32),
                   jax.ShapeDtypeStruct((S5_GROUP, S5_LANES), F32)])(a_re, a_im, log_dt, b_re_t, b_im_t)


def _scan_tables(pw_re, pw_im, reverse):
    seg = pw_re.shape[0]
    if reverse:
        pw_re, pw_im = pw_re[::-1], -pw_im[::-1]
        one, full = seg - 1, 0
    else:
        one, full = 0, seg - 1
    rows = jnp.stack([pw_re[one], pw_im[one], pw_re[full], pw_im[full]])
    wide = lambda t: jnp.broadcast_to(t[:, None, :], (seg, SUBLANES, S5_LANES))
    return rows, wide(pw_re), wide(pw_im)


def _lanes(j):
    return pl.ds(j * 128, 128)


def _segment_scan(xr_ref, xi_ref, lam_ref, car_ref, cai_ref, cn_r, cn_i, blocks, seg, reverse):
    shape = (SUBLANES, 128)
    lrs = [jnp.broadcast_to(lam_ref[0:1, _lanes(j)], shape) for j in blocks]
    lis = [jnp.broadcast_to(lam_ref[1:2, _lanes(j)], shape) for j in blocks]

    def step(k, carry):
        idx = pl.ds(seg - 1 - k if reverse else k, SUBLANES, stride=seg)
        out = []
        for n, j in enumerate(blocks):
            cr, ci = carry[2 * n], carry[2 * n + 1]
            nr = lrs[n] * cr - lis[n] * ci + xr_ref[j, idx, :]
            ni = lrs[n] * ci + lis[n] * cr + xi_ref[j, idx, :]
            xr_ref[j, idx, :] = nr
            xi_ref[j, idx, :] = ni
            out += [nr, ni]
        return tuple(out)

    zero = jnp.zeros(shape, F32)
    fin = lax.fori_loop(0, seg, step, (zero,) * (2 * len(blocks)), unroll=2)
    for n, j in enumerate(blocks):
        ls = _lanes(j)
        fr, fi = fin[2 * n], fin[2 * n + 1]
        sr, si = lam_ref[2:3, ls], lam_ref[3:4, ls]
        pr, pi = car_ref[:, ls], cai_ref[:, ls]
        for s in (reversed(range(SUBLANES)) if reverse else range(SUBLANES)):
            cn_r[s:s + 1, ls] = pr
            cn_i[s:s + 1, ls] = pi
            pr, pi = fr[s:s + 1, :] + sr * pr - si * pi, fi[s:s + 1, :] + sr * pi + si * pr
        car_ref[:, ls] = pr
        cai_ref[:, ls] = pi


def _s5_fwd2(proj, lam_rows, p3_re, p3_im, bbr4, bbi4, crt4, cit4, d_row, t_len, tb):
    seg = tb // SUBLANES

    def body(u_ref, lam_ref, p3r_ref, p3i_ref, bbr_ref, bbi_ref, crt_ref, cit_ref, d_ref,
             hr_ref, hi_ref, ypre_ref, ys_ref, car_ref, cai_ref, cn_r, cn_i):
        @pl.when(pl.program_id(0) == 0)
        def _():
            car_ref[...] = jnp.zeros_like(car_ref)
            cai_ref[...] = jnp.zeros_like(cai_ref)

        u = u_ref[...]
        for i in range(S5_BLOCKS):
            ui = u[:, i * S5_BW:(i + 1) * S5_BW]
            xr, xi = _dot(ui, bbr_ref[i]), _dot(ui, bbi_ref[i])
            for jj in range(S5_BL // 128):
                hr_ref[i * (S5_BL // 128) + jj] = xr[:, jj * 128:(jj + 1) * 128]
                hi_ref[i * (S5_BL // 128) + jj] = xi[:, jj * 128:(jj + 1) * 128]
        for lc in range(S5_LANE_BLOCKS // S5_SCAN_BLOCKS):
            blocks = range(lc * S5_SCAN_BLOCKS, (lc + 1) * S5_SCAN_BLOCKS)
            _segment_scan(hr_ref, hi_ref, lam_ref, car_ref, cai_ref, cn_r, cn_i, blocks, seg, False)
            crs = [cn_r[:, _lanes(j)] for j in blocks]
            cis = [cn_i[:, _lanes(j)] for j in blocks]

            def fix(t, carry, blocks=blocks, crs=crs, cis=cis):
                idx = pl.ds(t, SUBLANES, stride=seg)
                for n, j in enumerate(blocks):
                    pr, pi = p3r_ref[t, :, _lanes(j)], p3i_ref[t, :, _lanes(j)]
                    hr_ref[j, idx, :] += pr * crs[n] - pi * cis[n]
                    hi_ref[j, idx, :] += pr * cis[n] + pi * crs[n]
                return carry

            lax.fori_loop(0, seg, fix, 0, unroll=2)
        for i in range(S5_BLOCKS):
            ws = pl.ds(i * S5_BW, S5_BW)
            js = range(i * (S5_BL // 128), (i + 1) * (S5_BL // 128))
            hr = jnp.concatenate([hr_ref[j] for j in js], axis=1)
            hi = jnp.concatenate([hi_ref[j] for j in js], axis=1)
            y = _dot(hr, crt_ref[i]) - _dot(hi, cit_ref[i]) + d_ref[:, ws] * u[:, i * S5_BW:(i + 1) * S5_BW]
            ypre_ref[:, ws] = y
            ys_ref[:, ws] = jax.nn.gelu(y, approximate=True).astype(ys_ref.dtype)

    whole = pl.BlockSpec(memory_space=pltpu.VMEM)
    h_spec = pl.BlockSpec((S5_LANE_BLOCKS, tb, 128), lambda i: (0, i, 0))
    return pl.pallas_call(
        body, name="s5_fwd", grid=(t_len // tb,),
        in_specs=[pl.BlockSpec((tb, S5_WIDTH), lambda i: (i, 4096 // S5_WIDTH))] + [whole] * 8,
        out_specs=[h_spec, h_spec,
                   pl.BlockSpec((tb, S5_WIDTH), lambda i: (i, 0)), pl.BlockSpec((tb, S5_WIDTH), lambda i: (i, 0))],
        out_shape=[jax.ShapeDtypeStruct((S5_LANE_BLOCKS, t_len, 128), F32),
                   jax.ShapeDtypeStruct((S5_LANE_BLOCKS, t_len, 128), F32),
                   jax.ShapeDtypeStruct((t_len, S5_WIDTH), F32), jax.ShapeDtypeStruct((t_len, S5_WIDTH), MXU_DTYPE)],
        scratch_shapes=[pltpu.VMEM((1, S5_LANES), F32), pltpu.VMEM((1, S5_LANES), F32),
                        pltpu.VMEM((SUBLANES, S5_LANES), F32), pltpu.VMEM((SUBLANES, S5_LANES), F32)],
        compiler_params=_params("arbitrary"))(proj, lam_rows, p3_re, p3_im, bbr4, bbi4, crt4, cit4, d_row)


def _s5_bwd2(dgelu, y_pre, proj, h_re, h_im, lam_rows, p3_re, p3_im, bbr4, bbi4, cr4, ci4, d_row, dproj, t_len, tb):
    seg = tb // SUBLANES
    nb = t_len // tb

    def body(dg_ref, yp_ref, u_ref, hr_ref, hi_ref, lam_ref, p3r_ref, p3i_ref, bbr_ref, bbi_ref, cr_ref, ci_ref,
             d_ref, _, du_ref, dbbr_ref, dbbi_ref, dcr_ref, dci_ref, dd_ref, dlam_ref,
             gr_ref, gi_ref, car_ref, cai_ref, cn_r, cn_i):
        @pl.when(pl.program_id(0) == 0)
        def _():
            for ref in (car_ref, cai_ref, dbbr_ref, dbbi_ref, dcr_ref, dci_ref, dd_ref, dlam_ref):
                ref[...] = jnp.zeros_like(ref)

        u = u_ref[...]
        dy = dg_ref[...] * _dgelu(yp_ref[...])
        nlb = S5_BL // 128
        for i in range(S5_BLOCKS):
            dyi = dy[:, i * S5_BW:(i + 1) * S5_BW]
            xr, xi = _dot(dyi, cr_ref[i]), -_dot(dyi, ci_ref[i])
            for jj in range(nlb):
                gr_ref[i * nlb + jj] = xr[:, jj * 128:(jj + 1) * 128]
                gi_ref[i * nlb + jj] = xi[:, jj * 128:(jj + 1) * 128]
        for lc in range(S5_LANE_BLOCKS // S5_SCAN_BLOCKS):
            blocks = range(lc * S5_SCAN_BLOCKS, (lc + 1) * S5_SCAN_BLOCKS)
            _segment_scan(gr_ref, gi_ref, lam_ref, car_ref, cai_ref, cn_r, cn_i, blocks, seg, True)
            crs = [cn_r[:, _lanes(j)] for j in blocks]
            cis = [cn_i[:, _lanes(j)] for j in blocks]

            def fix(k, carry, blocks=blocks, crs=crs, cis=cis):
                t = seg - 1 - k
                idx = pl.ds(t, SUBLANES, stride=seg)
                out = []
                for n, j in enumerate(blocks):
                    nr, ni, slr, sli = carry[4 * n:4 * n + 4]
                    pr, pi = p3r_ref[t, :, _lanes(j)], p3i_ref[t, :, _lanes(j)]
                    g_r = gr_ref[j, idx, :] + pr * crs[n] - pi * cis[n]
                    g_i = gi_ref[j, idx, :] + pr * cis[n] + pi * crs[n]
                    gr_ref[j, idx, :] = g_r
                    gi_ref[j, idx, :] = g_i
                    hr, hi = hr_ref[j, idx, :], hi_ref[j, idx, :]
                    out += [g_r, g_i, slr + nr * hr + ni * hi, sli + ni * hr - nr * hi]
                return tuple(out)

            zero = jnp.zeros((SUBLANES, 128), F32)
            init = []
            for n in range(len(blocks)):
                init += [crs[n], cis[n], zero, zero]
            fin = lax.fori_loop(0, seg, fix, tuple(init), unroll=2)
            for n, j in enumerate(blocks):
                dlam_ref[0:1, _lanes(j)] += jnp.sum(fin[4 * n + 2], axis=0, keepdims=True)
                dlam_ref[1:2, _lanes(j)] += jnp.sum(fin[4 * n + 3], axis=0, keepdims=True)
        for i in range(S5_BLOCKS):
            ws = pl.ds(i * S5_BW, S5_BW)
            js = range(i * nlb, (i + 1) * nlb)
            ui, dyi = u[:, i * S5_BW:(i + 1) * S5_BW], dy[:, i * S5_BW:(i + 1) * S5_BW]
            gr = jnp.concatenate([gr_ref[j] for j in js], axis=1)
            gi = jnp.concatenate([gi_ref[j] for j in js], axis=1)
            du_ref[:, ws] = _dot(gr, bbr_ref[i], _NT) + _dot(gi, bbi_ref[i], _NT) + d_ref[:, ws] * dyi
            dbbr_ref[i] += _dot(ui, gr, _TN)
            dbbi_ref[i] += _dot(ui, gi, _TN)
            dcr_ref[i] += _dot(jnp.concatenate([hr_ref[j] for j in js], axis=1), dyi, _TN)
            dci_ref[i] -= _dot(jnp.concatenate([hi_ref[j] for j in js], axis=1), dyi, _TN)
        dd_ref[...] += jnp.sum(dy * u, axis=0, keepdims=True)

    whole = pl.BlockSpec(memory_space=pltpu.VMEM)
    rev = lambda i: (nb - 1 - i, 0)
    const3 = lambda i: (0, 0, 0)
    h_spec = pl.BlockSpec((S5_LANE_BLOCKS, tb, 128), lambda i: (0, nb - 1 - i, 0))
    return pl.pallas_call(
        body, name="s5_bwd", grid=(nb,),
        in_specs=[pl.BlockSpec((tb, S5_WIDTH), rev), pl.BlockSpec((tb, S5_WIDTH), rev),
                  pl.BlockSpec((tb, S5_WIDTH), lambda i: (nb - 1 - i, 4096 // S5_WIDTH)),
                  h_spec, h_spec] + [whole] * 8
                 + [pl.BlockSpec(memory_space=pl.ANY)],
        out_specs=[pl.BlockSpec((tb, S5_WIDTH), lambda i: (nb - 1 - i, 4096 // S5_WIDTH)),
                   pl.BlockSpec((S5_BLOCKS, S5_BW, S5_BL), const3), pl.BlockSpec((S5_BLOCKS, S5_BW, S5_BL), const3),
                   pl.BlockSpec((S5_BLOCKS, S5_BL, S5_BW), const3), pl.BlockSpec((S5_BLOCKS, S5_BL, S5_BW), const3),
                   pl.BlockSpec((1, S5_WIDTH), lambda i: (0, 0)), pl.BlockSpec((2, S5_LANES), lambda i: (0, 0))],
        out_shape=[jax.ShapeDtypeStruct((t_len, IN_COLS), F32),
                   jax.ShapeDtypeStruct((S5_BLOCKS, S5_BW, S5_BL), F32),
                   jax.ShapeDtypeStruct((S5_BLOCKS, S5_BW, S5_BL), F32),
                   jax.ShapeDtypeStruct((S5_BLOCKS, S5_BL, S5_BW), F32),
                   jax.ShapeDtypeStruct((S5_BLOCKS, S5_BL, S5_BW), F32),
                   jax.ShapeDtypeStruct((1, S5_WIDTH), F32), jax.ShapeDtypeStruct((2, S5_LANES), F32)],
        scratch_shapes=[pltpu.VMEM((S5_LANE_BLOCKS, tb, 128), F32), pltpu.VMEM((S5_LANE_BLOCKS, tb, 128), F32),
                        pltpu.VMEM((1, S5_LANES), F32), pltpu.VMEM((1, S5_LANES), F32),
                        pltpu.VMEM((SUBLANES, S5_LANES), F32), pltpu.VMEM((SUBLANES, S5_LANES), F32)],
        input_output_aliases={13: 0},
        compiler_params=_params("arbitrary"))(dgelu, y_pre, proj, h_re, h_im, lam_rows, p3_re, p3_im, bbr4, bbi4,
                                              cr4, ci4, d_row, dproj)


def _to_segment_order(v, stage_ref, out_ref, seg):
    nbl = v.shape[1] // 128
    for b in range(nbl):
        stage_ref[b] = v[:, b * 128:(b + 1) * 128]

    def body(t, carry):
        rows = pl.ds(pl.multiple_of(t * SUBLANES, SUBLANES), SUBLANES)
        for b in range(nbl):
            out_ref[rows, _lanes(b)] = stage_ref[b, pl.ds(t, SUBLANES, stride=seg), :]
        return carry

    lax.fori_loop(0, seg, body, 0)


def _from_segment_order(v, stage_ref, out_ref, seg):
    nbl = v.shape[1] // 128
    for b in range(nbl):
        stage_ref[b] = v[:, b * 128:(b + 1) * 128]
    for s in range(SUBLANES):
        def body(k, carry, s=s):
            rows = pl.ds(pl.multiple_of(s * seg + k * SUBLANES, SUBLANES), SUBLANES)
            for b in range(nbl):
                out_ref[rows, _lanes(b)] = stage_ref[b, pl.ds(k * SUBLANES * SUBLANES + s, SUBLANES,
                                                              stride=SUBLANES), :]
            return carry

        lax.fori_loop(0, seg // SUBLANES, body, 0)


def _tile_scan(xr_ref, xi_ref, lam_ref, car_ref, cai_ref, cn_r, cn_i, blocks, seg, reverse):
    shape = (SUBLANES, 128)
    lrs = [jnp.broadcast_to(lam_ref[0:1, _lanes(j)], shape) for j in blocks]
    lis = [jnp.broadcast_to(lam_ref[1:2, _lanes(j)], shape) for j in blocks]

    def step(k, carry):
        t = seg - 1 - k if reverse else k
        rows = pl.ds(pl.multiple_of(t * SUBLANES, SUBLANES), SUBLANES)
        out = []
        for n, j in enumerate(blocks):
            cr, ci = carry[2 * n], carry[2 * n + 1]
            nr = lrs[n] * cr - lis[n] * ci + xr_ref[rows, _lanes(j)]
            ni = lrs[n] * ci + lis[n] * cr + xi_ref[rows, _lanes(j)]
            xr_ref[rows, _lanes(j)] = nr
            xi_ref[rows, _lanes(j)] = ni
            out += [nr, ni]
        return tuple(out)

    zero = jnp.zeros(shape, F32)
    fin = lax.fori_loop(0, seg, step, (zero,) * (2 * len(blocks)), unroll=2)
    for n, j in enumerate(blocks):
        ls = _lanes(j)
        fr, fi = fin[2 * n], fin[2 * n + 1]
        sr, si = lam_ref[2:3, ls], lam_ref[3:4, ls]
        pr, pi = car_ref[:, ls], cai_ref[:, ls]
        for s in (reversed(range(SUBLANES)) if reverse else range(SUBLANES)):
            cn_r[s:s + 1, ls] = pr
            cn_i[s:s + 1, ls] = pi
            pr, pi = fr[s:s + 1, :] + sr * pr - si * pi, fi[s:s + 1, :] + sr * pi + si * pr
        car_ref[:, ls] = pr
        cai_ref[:, ls] = pi


def _s5_fwd3(proj, lam_rows, p3_re, p3_im, bbr4, bbi4, crt4, cit4, d_row, t_len, tb):
    seg = tb // SUBLANES

    def body(u_ref, lam_ref, p3r_ref, p3i_ref, bbr_ref, bbi_ref, crt_ref, cit_ref, d_ref,
             hr_ref, hi_ref, ypre_ref, ys_ref, car_ref, cai_ref, cn_r, cn_i, stage_ref, us_ref, yseg_ref):
        @pl.when(pl.program_id(0) == 0)
        def _():
            car_ref[...] = jnp.zeros_like(car_ref)
            cai_ref[...] = jnp.zeros_like(cai_ref)

        _to_segment_order(u_ref[...], stage_ref, us_ref, seg)
        u = us_ref[...]
        for i in range(S5_BLOCKS):
            ui = u[:, i * S5_BW:(i + 1) * S5_BW]
            hr_ref[:, pl.ds(i * S5_BL, S5_BL)] = _dot(ui, bbr_ref[i])
            hi_ref[:, pl.ds(i * S5_BL, S5_BL)] = _dot(ui, bbi_ref[i])
        for lc in range(S5_LANE_BLOCKS // S5_SCAN_BLOCKS):
            blocks = range(lc * S5_SCAN_BLOCKS, (lc + 1) * S5_SCAN_BLOCKS)
            _tile_scan(hr_ref, hi_ref, lam_ref, car_ref, cai_ref, cn_r, cn_i, blocks, seg, False)
            crs = [cn_r[:, _lanes(j)] for j in blocks]
            cis = [cn_i[:, _lanes(j)] for j in blocks]

            def fix(t, carry, blocks=blocks, crs=crs, cis=cis):
                rows = pl.ds(pl.multiple_of(t * SUBLANES, SUBLANES), SUBLANES)
                for n, j in enumerate(blocks):
                    pr, pi = p3r_ref[t, :, _lanes(j)], p3i_ref[t, :, _lanes(j)]
                    hr_ref[rows, _lanes(j)] += pr * crs[n] - pi * cis[n]
                    hi_ref[rows, _lanes(j)] += pr * cis[n] + pi * crs[n]
                return carry

            lax.fori_loop(0, seg, fix, 0, unroll=2)
        for i in range(S5_BLOCKS):
            ws = pl.ds(i * S5_BW, S5_BW)
            bl = pl.ds(i * S5_BL, S5_BL)
            yseg_ref[:, ws] = (_dot(hr_ref[:, bl], crt_ref[i]) - _dot(hi_ref[:, bl], cit_ref[i])
                               + d_ref[:, ws] * u[:, i * S5_BW:(i + 1) * S5_BW])
        _from_segment_order(yseg_ref[...], stage_ref, ypre_ref, seg)
        ys_ref[...] = jax.nn.gelu(ypre_ref[...], approximate=True).astype(ys_ref.dtype)

    whole = pl.BlockSpec(memory_space=pltpu.VMEM)
    return pl.pallas_call(
        body, name="s5_fwd", grid=(t_len // tb,),
        in_specs=[pl.BlockSpec((tb, S5_WIDTH), lambda i: (i, 4096 // S5_WIDTH))] + [whole] * 8,
        out_specs=[pl.BlockSpec((tb, S5_LANES), lambda i: (i, 0)), pl.BlockSpec((tb, S5_LANES), lambda i: (i, 0)),
                   pl.BlockSpec((tb, S5_WIDTH), lambda i: (i, 0)), pl.BlockSpec((tb, S5_WIDTH), lambda i: (i, 0))],
        out_shape=[jax.ShapeDtypeStruct((t_len, S5_LANES), F32), jax.ShapeDtypeStruct((t_len, S5_LANES), F32),
                   jax.ShapeDtypeStruct((t_len, S5_WIDTH), F32), jax.ShapeDtypeStruct((t_len, S5_WIDTH), MXU_DTYPE)],
        scratch_shapes=[pltpu.VMEM((1, S5_LANES), F32), pltpu.VMEM((1, S5_LANES), F32),
                        pltpu.VMEM((SUBLANES, S5_LANES), F32), pltpu.VMEM((SUBLANES, S5_LANES), F32),
                        pltpu.VMEM((S5_WIDTH // 128, tb, 128), F32), pltpu.VMEM((tb, S5_WIDTH), F32),
                        pltpu.VMEM((tb, S5_WIDTH), F32)],
        compiler_params=_params("arbitrary"))(proj, lam_rows, p3_re, p3_im, bbr4, bbi4, crt4, cit4, d_row)


def _s5_bwd3(dgelu, y_pre, proj, h_re, h_im, lam_rows, p3_re, p3_im, bbr4, bbi4, cr4, ci4, d_row, dproj, t_len, tb):
    seg = tb // SUBLANES
    nb = t_len // tb

    def body(dg_ref, yp_ref, u_ref, hr_ref, hi_ref, lam_ref, p3r_ref, p3i_ref, bbr_ref, bbi_ref, cr_ref, ci_ref,
             d_ref, _, du_ref, dbbr_ref, dbbi_ref, dcr_ref, dci_ref, dd_ref, dlam_ref,
             gr_ref, gi_ref, car_ref, cai_ref, cn_r, cn_i, stage_ref, us_ref, dys_ref, duseg_ref):
        @pl.when(pl.program_id(0) == 0)
        def _():
            for ref in (car_ref, cai_ref, dbbr_ref, dbbi_ref, dcr_ref, dci_ref, dd_ref, dlam_ref):
                ref[...] = jnp.zeros_like(ref)

        _to_segment_order(u_ref[...], stage_ref, us_ref, seg)
        _to_segment_order(dg_ref[...] * _dgelu(yp_ref[...]), stage_ref, dys_ref, seg)
        u, dy = us_ref[...], dys_ref[...]
        for i in range(S5_BLOCKS):
            dyi = dy[:, i * S5_BW:(i + 1) * S5_BW]
            gr_ref[:, pl.ds(i * S5_BL, S5_BL)] = _dot(dyi, cr_ref[i])
            gi_ref[:, pl.ds(i * S5_BL, S5_BL)] = -_dot(dyi, ci_ref[i])
        for lc in range(S5_LANE_BLOCKS // S5_SCAN_BLOCKS):
            blocks = range(lc * S5_SCAN_BLOCKS, (lc + 1) * S5_SCAN_BLOCKS)
            _tile_scan(gr_ref, gi_ref, lam_ref, car_ref, cai_ref, cn_r, cn_i, blocks, seg, True)
            crs = [cn_r[:, _lanes(j)] for j in blocks]
            cis = [cn_i[:, _lanes(j)] for j in blocks]

            def fix(k, carry, blocks=blocks, crs=crs, cis=cis):
                t = seg - 1 - k
                rows = pl.ds(pl.multiple_of(t * SUBLANES, SUBLANES), SUBLANES)
                out = []
                for n, j in enumerate(blocks):
                    nr, ni, slr, sli = carry[4 * n:4 * n + 4]
                    pr, pi = p3r_ref[t, :, _lanes(j)], p3i_ref[t, :, _lanes(j)]
                    g_r = gr_ref[rows, _lanes(j)] + pr * crs[n] - pi * cis[n]
                    g_i = gi_ref[rows, _lanes(j)] + pr * cis[n] + pi * crs[n]
                    gr_ref[rows, _lanes(j)] = g_r
                    gi_ref[rows, _lanes(j)] = g_i
                    hr, hi = hr_ref[rows, _lanes(j)], hi_ref[rows, _lanes(j)]
                    out += [g_r, g_i, slr + nr * hr + ni * hi, sli + ni * hr - nr * hi]
                return tuple(out)

            zero = jnp.zeros((SUBLANES, 128), F32)
            init = []
            for n in range(len(blocks)):
                init += [crs[n], cis[n], zero, zero]
            fin = lax.fori_loop(0, seg, fix, tuple(init), unroll=2)
            for n, j in enumerate(blocks):
                dlam_ref[0:1, _lanes(j)] += jnp.sum(fin[4 * n + 2], axis=0, keepdims=True)
                dlam_ref[1:2, _lanes(j)] += jnp.sum(fin[4 * n + 3], axis=0, keepdims=True)
        for i in range(S5_BLOCKS):
            ws = pl.ds(i * S5_BW, S5_BW)
            bl = pl.ds(i * S5_BL, S5_BL)
            ui, dyi = u[:, i * S5_BW:(i + 1) * S5_BW], dy[:, i * S5_BW:(i + 1) * S5_BW]
            gr, gi = gr_ref[:, bl], gi_ref[:, bl]
            duseg_ref[:, ws] = _dot(gr, bbr_ref[i], _NT) + _dot(gi, bbi_ref[i], _NT) + d_ref[:, ws] * dyi
            dbbr_ref[i] += _dot(ui, gr, _TN)
            dbbi_ref[i] += _dot(ui, gi, _TN)
            dcr_ref[i] += _dot(hr_ref[:, bl], dyi, _TN)
            dci_ref[i] -= _dot(hi_ref[:, bl], dyi, _TN)
        dd_ref[...] += jnp.sum(dy * u, axis=0, keepdims=True)
        _from_segment_order(duseg_ref[...], stage_ref, du_ref, seg)

    whole = pl.BlockSpec(memory_space=pltpu.VMEM)
    rev = lambda i: (nb - 1 - i, 0)
    const3 = lambda i: (0, 0, 0)
    return pl.pallas_call(
        body, name="s5_bwd", grid=(nb,),
        in_specs=[pl.BlockSpec((tb, S5_WIDTH), rev), pl.BlockSpec((tb, S5_WIDTH), rev),
                  pl.BlockSpec((tb, S5_WIDTH), lambda i: (nb - 1 - i, 4096 // S5_WIDTH)),
                  pl.BlockSpec((tb, S5_LANES), rev), pl.BlockSpec((tb, S5_LANES), rev)] + [whole] * 8
                 + [pl.BlockSpec(memory_space=pl.ANY)],
        out_specs=[pl.BlockSpec((tb, S5_WIDTH), lambda i: (nb - 1 - i, 4096 // S5_WIDTH)),
                   pl.BlockSpec((S5_BLOCKS, S5_BW, S5_BL), const3), pl.BlockSpec((S5_BLOCKS, S5_BW, S5_BL), const3),
                   pl.BlockSpec((S5_BLOCKS, S5_BL, S5_BW), const3), pl.BlockSpec((S5_BLOCKS, S5_BL, S5_BW), const3),
                   pl.BlockSpec((1, S5_WIDTH), lambda i: (0, 0)), pl.BlockSpec((2, S5_LANES), lambda i: (0, 0))],
        out_shape=[jax.ShapeDtypeStruct((t_len, IN_COLS), F32),
                   jax.ShapeDtypeStruct((S5_BLOCKS, S5_BW, S5_BL), F32),
                   jax.ShapeDtypeStruct((S5_BLOCKS, S5_BW, S5_BL), F32),
                   jax.ShapeDtypeStruct((S5_BLOCKS, S5_BL, S5_BW), F32),
                   jax.ShapeDtypeStruct((S5_BLOCKS, S5_BL, S5_BW), F32),
                   jax.ShapeDtypeStruct((1, S5_WIDTH), F32), jax.ShapeDtypeStruct((2, S5_LANES), F32)],
        scratch_shapes=[pltpu.VMEM((tb, S5_LANES), F32), pltpu.VMEM((tb, S5_LANES), F32),
                        pltpu.VMEM((1, S5_LANES), F32), pltpu.VMEM((1, S5_LANES), F32),
                        pltpu.VMEM((SUBLANES, S5_LANES), F32), pltpu.VMEM((SUBLANES, S5_LANES), F32),
                        pltpu.VMEM((S5_WIDTH // 128, tb, 128), F32), pltpu.VMEM((tb, S5_WIDTH), F32),
                        pltpu.VMEM((tb, S5_WIDTH), F32), pltpu.VMEM((tb, S5_WIDTH), F32)],
        input_output_aliases={13: 0},
        compiler_params=_params("arbitrary"))(dgelu, y_pre, proj, h_re, h_im, lam_rows, p3_re, p3_im, bbr4, bbi4,
                                              cr4, ci4, d_row, dproj)


def _block_diag4(per_group):
    g8 = S5_GROUPS // S5_BLOCKS
    eye = jnp.eye(g8, dtype=bool)[None, :, None, :, None]
    dense = jnp.where(eye, per_group.reshape(S5_BLOCKS, g8, S5_GROUP, 1, S5_STATE), 0.0)
    return dense.reshape(S5_BLOCKS, S5_BW, S5_BL)


def _diag_blocks4(dense):
    g8 = S5_GROUPS // S5_BLOCKS
    ar = jnp.arange(g8)
    d5 = dense.reshape(S5_BLOCKS, g8, S5_GROUP, g8, S5_STATE)
    return d5[:, ar, :, ar, :].transpose(1, 0, 2, 3).reshape(S5_GROUPS, S5_GROUP, S5_STATE)


def _block_diag(per_group):
    eye = jnp.eye(S5_GROUPS, dtype=bool)[:, None, :, None]
    dense = jnp.where(eye, per_group[:, :, None, :], 0.0)
    return dense.reshape(S5_WIDTH, S5_LANES)


def _diag_blocks(dense):
    ar = jnp.arange(S5_GROUPS)
    return dense.reshape(S5_GROUPS, S5_GROUP, S5_GROUPS, S5_STATE)[ar, :, ar, :]


def _local_step(x, p, target, w, sm):
    t_len = x.shape[0]
    tm = min(256, t_len)
    tmm = min(512, t_len)
    tb_hg = min(256, t_len)
    tb_s5 = min(256, t_len)
    g1, g2, g3, ghn = sm["norm_g"], sm["ple_norm_g"], sm["final_norm_g"].reshape(1, D_MODEL), sm["hg_norm_g"]

    def rms_f(xv, g):
        r = lax.rsqrt(jnp.mean(xv * xv, axis=-1, keepdims=True) + NORM_EPS)
        return (xv * r * g,)

    (u,) = _rowwise("rms_in", rms_f, t_len, tm, [(x, 1024, 0)], [g1], [(1024, MXU_DTYPE)])
    proj = _mm_nn("mm_in", u, w["w_in"], tmm, 1024)
    o_hg, act_hg, s_prev = _hgrn2_fwd2(proj, sm["hg_lb"], ghn, t_len, tb_hg)

    lanes = lambda a: a.reshape(1, S5_LANES)
    a_re, a_im = lanes(sm["s5_a_re"]), lanes(sm["s5_a_im"])
    ldt = lanes(jnp.broadcast_to(sm["s5_log_dt"].reshape(S5_GROUPS, 1), (S5_GROUPS, S5_STATE)))
    to_t = lambda b: b.reshape(S5_GROUPS, S5_STATE, S5_GROUP).transpose(2, 0, 1).reshape(S5_GROUP, S5_LANES)
    b_re_t, b_im_t = to_t(sm["s5_b_re"]), to_t(sm["s5_b_im"])
    pw_re, pw_im, bbr_t, bbi_t = _s5_powers(a_re, a_im, ldt, b_re_t, b_im_t, tb_s5 // SUBLANES)
    from_t = lambda b: b.reshape(S5_GROUP, S5_GROUPS, S5_STATE).transpose(1, 0, 2)
    bbr_bd = _block_diag4(from_t(bbr_t)).astype(MXU_DTYPE)
    bbi_bd = _block_diag4(from_t(bbi_t)).astype(MXU_DTYPE)
    cr_bd = _block_diag4(sm["s5_c_re"].reshape(S5_GROUPS, S5_GROUP, S5_STATE)).astype(MXU_DTYPE)
    ci_bd = _block_diag4(sm["s5_c_im"].reshape(S5_GROUPS, S5_GROUP, S5_STATE)).astype(MXU_DTYPE)
    d_row = sm["s5_d"].reshape(1, S5_WIDTH)
    h_re, h_im, y_pre, ys_gelu = _s5_fwd3(proj, *_scan_tables(pw_re, pw_im, False), bbr_bd, bbi_bd,
                                          cr_bd.transpose(0, 2, 1), ci_bd.transpose(0, 2, 1), d_row, t_len, tb_s5)
    def mix_f(act, ysg, z, gh, gs, xv, w_glu, b_glu, w_o_hg, w_o_s5, w_out):
        gl_ = _dot(ysg, w_glu) + b_glu
        a, b = gl_[:, :S5_WIDTH], gl_[:, S5_WIDTH:]
        ys2_ = (a * _sig(b) * (z * _sig(z))).astype(MXU_DTYPE)
        yh, ys = _dot(act, w_o_hg), _dot(ys2_, w_o_s5)
        mg = (_sig(gh) * yh + _sig(gs) * ys).astype(MXU_DTYPE)
        return (gl_, ys2_, yh, ys, mg, xv + _dot(mg, w_out))

    glu, ys2, y_hg, y_s5, merged, h1 = _rowwise(
        "mix_out", mix_f, t_len, tm,
        [(act_hg, 1024, 0), (ys_gelu, 512, 0), (proj, 512, 4608 // 512), (proj, 1024, 5), (proj, 1024, 6),
         (x, 1024, 0)], [w["w_glu"], sm["b_glu"], w["w_o_hg"], w["w_o_s5"], w["w_out"]],
        [(1024, F32), (512, MXU_DTYPE), (1024, F32), (1024, F32), (1024, MXU_DTYPE), (1024, F32)])

    def head_f(h1v, pv, tgt, g_ple, g, w_ple, w_gate):
        r2 = lax.rsqrt(jnp.mean(h1v * h1v, axis=-1, keepdims=True) + NORM_EPS)
        n2_ = (h1v * r2 * g_ple).astype(MXU_DTYPE)
        glv, pev = _dot(n2_, w_gate), _dot(pv, w_ple)
        gate = _sig(glv)
        h2 = h1v + pev * gate
        r = lax.rsqrt(jnp.mean(h2 * h2, axis=-1, keepdims=True) + NORM_EPS)
        e = h2 * r * g - tgt
        loss = 0.5 * jnp.sum(jnp.mean(e * e, axis=-1, keepdims=True), axis=0, keepdims=True)
        dy = e * (1.0 / D_MODEL)
        dg = jnp.sum(dy * h2 * r, axis=0, keepdims=True)
        t = dy * g
        dh2 = r * t - h2 * (r * r * r) * jnp.mean(t * h2, axis=-1, keepdims=True)
        return (n2_, dh2, dh2 * gate, dh2 * pev * gate * (1.0 - gate), jnp.broadcast_to(loss, (1, 128)), dg)

    n2, dh2, dpe, dgl, loss_row, d_g3 = _rowwise(
        "ple_loss_head", head_f, t_len, tm, [(h1, 1024, 0), (p, 256, 0), (target, 1024, 0)],
        [g2, g3, w["w_ple"], w["w_ple_gate"]],
        [(1024, MXU_DTYPE), (1024, F32), (1024, MXU_DTYPE), (1024, MXU_DTYPE)], accs=[(1, 128), (1, 1024)])

    gb = {}
    gb["w_ple"] = _mm_tn("mm_d_w_ple", p, dpe, tmm, 1024)
    gb["w_ple_gate"] = _mm_tn("mm_d_w_ple_gate", n2, dgl, tmm, 1024)
    def ple_b(dn, h1v, dh, g):
        dx, dg = _rms_bwd(dn, h1v, g)
        return (dh + dx, dg)

    dh1, d_g2 = _mm_nt_then("mm_d_n2_rms_ple_bwd", dgl, w["w_ple_gate"], tm, 1024, ple_b,
                            [(h1, 1024, 0), (dh2, 1024, 0)], [g2], [(1024, F32)], accs=[(1, 1024)])
    gb["w_out"] = _mm_tn("mm_d_w_out", merged, dh1, tmm, 1024)
    dmerged = _mm_nt("mm_d_merged", dh1, w["w_out"], tmm, 1024)

    def gate_b(dm, y, gt):
        s = _sig(gt)
        return (dm * s, dm * y * s * (1.0 - s))

    dy_hg, dproj = _rowwise("gate_hg_bwd", gate_b, t_len, tm, [(dmerged, 1024, 0), (y_hg, 1024, 0), (proj, 1024, 5)],
                            [], [(1024, MXU_DTYPE), (1024, F32, 5, IN_COLS)])
    dy_s5, dproj = _rowwise("gate_s5_bwd", gate_b, t_len, tm, [(dmerged, 1024, 0), (y_s5, 1024, 0), (proj, 1024, 6)],
                            [], [(1024, MXU_DTYPE), (1024, F32, 6, IN_COLS)], alias=(dproj, 1))
    gb["w_o_s5"] = _mm_tn("mm_d_w_o_s5", ys2, dy_s5, tmm, 1024)
    def glu_b(dys, gl_, z):
        a, b = gl_[:, :S5_WIDTH], gl_[:, S5_WIDTH:]
        sb, sz = _sig(b), _sig(z)
        silu = z * sz
        dglu = jnp.concatenate([dys * sb * silu, dys * a * silu * sb * (1.0 - sb)], axis=1)
        return (dglu, dys * a * sb * _dsilu(z, sz), jnp.sum(dglu, axis=0, keepdims=True))

    dglu, dproj, d_bglu = _mm_nt_then("mm_d_ys2_glu_bwd", dy_s5, w["w_o_s5"], tm, 1024, glu_b,
                                      [(glu, 1024, 0), (proj, 512, 4608 // 512)], [],
                                      [(1024, MXU_DTYPE), (512, F32, 4608 // 512, IN_COLS)], accs=[(1, 1024)],
                                      alias=(dproj, 1))
    gb["w_glu"] = _mm_tn("mm_d_w_glu", ys_gelu, dglu, tmm, 1024)
    dgelu = _mm_nt("mm_d_gelu", dglu, w["w_glu"], tmm, 1024)
    dproj, d_bbr, d_bbi, d_crt, d_cit, d_d, d_lam = _s5_bwd3(dgelu, y_pre, proj, h_re, h_im,
                                                            *_scan_tables(pw_re, pw_im, True), bbr_bd, bbi_bd, cr_bd,
                                                            ci_bd, d_row, dproj, t_len, tb_s5)
    to_t3 = lambda b: b.transpose(1, 0, 2).reshape(S5_GROUP, S5_LANES)
    d_are, d_aim, d_ldt, d_br_t, d_bi_t = _s5_prep_bwd(a_re, a_im, ldt, b_re_t, b_im_t, d_lam,
                                                       to_t3(_diag_blocks4(d_bbr)), to_t3(_diag_blocks4(d_bbi)))
    gb["w_o_hg"] = _mm_tn("mm_d_w_o_hg", act_hg, dy_hg, tmm, 1024)
    def hg_gate_b(da, o, g, gn):
        dos, dgs, dgns = [], [], []
        for h in range(HG_HEADS):
            sl = slice(h * HG_DIM, (h + 1) * HG_DIM)
            oh, gh, dah, gnh = o[:, sl], g[:, sl], da[:, sl], gn[:, sl]
            rr = lax.rsqrt(jnp.mean(oh * oh, axis=-1, keepdims=True) + NORM_EPS)
            sg = _sig(gh)
            dgs.append(dah * (oh * rr * gnh) * _dsilu(gh, sg))
            don = dah * (gh * sg)
            t = don * gnh
            dos.append(rr * t - oh * (rr * rr * rr) * jnp.mean(t * oh, axis=-1, keepdims=True))
            dgns.append(jnp.sum(don * oh * rr, axis=0, keepdims=True))
        return (jnp.concatenate(dos, axis=1), jnp.concatenate(dgs, axis=1), jnp.concatenate(dgns, axis=1))

    d_o, dproj, d_ghn = _mm_nt_then("mm_d_act_hg_gate_bwd", dy_hg, w["w_o_hg"], tm, 1024, hg_gate_b,
                                    [(o_hg, 1024, 0), (proj, 1024, 3)], [ghn],
                                    [(1024, F32), (1024, F32, 3, IN_COLS)], accs=[(1, 1024)], alias=(dproj, 1))
    dproj, d_lb = _hgrn2_bwd2(proj, d_o, s_prev, sm["hg_lb"], dproj, t_len, tb_hg)
    gb["w_in"] = _mm_tn("mm_d_w_in", u, dproj, tmm, 1024)
    def in_b(duv, xv, dh, g):
        dx, dg = _rms_bwd(duv, xv, g)
        return (dh + dx, dg)

    grad_x, d_g1 = _mm_nt_then("mm_d_u_rms_in_bwd", dproj, w["w_in"], tmm, 1024, in_b,
                               [(x, 1024, 0), (dh1, 1024, 0)], [g1], [(1024, F32)], accs=[(1, 1024)])

    back_t = lambda b: b.reshape(S5_GROUP, S5_GROUPS, S5_STATE).transpose(1, 2, 0).reshape(1, S5_GROUPS, S5_STATE,
                                                                                           S5_GROUP)
    gs = {
        "norm_g": d_g1, "hg_lb": d_lb, "hg_norm_g": d_ghn,
        "s5_a_re": d_are.reshape(1, S5_GROUPS, S5_STATE), "s5_a_im": d_aim.reshape(1, S5_GROUPS, S5_STATE),
        "s5_log_dt": d_ldt[0:1, :S5_GROUPS],
        "s5_b_re": back_t(d_br_t), "s5_b_im": back_t(d_bi_t),
        "s5_c_re": _diag_blocks4(d_crt.transpose(0, 2, 1)).reshape(1, S5_GROUPS, S5_GROUP, S5_STATE),
        "s5_c_im": _diag_blocks4(d_cit.transpose(0, 2, 1)).reshape(1, S5_GROUPS, S5_GROUP, S5_STATE),
        "s5_d": d_d.reshape(1, S5_GROUPS, S5_GROUP), "b_glu": d_bglu, "ple_norm_g": d_g2,
        "final_norm_g": d_g3.reshape(D_MODEL),
    }
    return loss_row, grad_x, gb, gs


def _shard_shape(name):
    r, c = BIG_SHAPE[name]
    return (r, c // N_CHIPS) if name in BIG_COL_SHARDED else (r // N_CHIPS, c)


def _pack_shard(parts):
    return jnp.concatenate([parts[n].reshape(-1, PACK_W) for n in BIG], axis=0)


def _unpack_shard(packed):
    out, off = {}, 0
    for n in BIG:
        r, c = _shard_shape(n)
        rows = r * c // PACK_W
        out[n] = packed[off:off + rows].reshape(1, r, c)
        off += rows
    return out


def _unpack_full(gathered):
    out, off = {}, 0
    for n in BIG:
        r, c = _shard_shape(n)
        rows = r * c // PACK_W
        sh = gathered[:, off:off + rows].reshape(N_CHIPS, r, c)
        out[n] = sh.transpose(1, 0, 2).reshape(BIG_SHAPE[n]) if n in BIG_COL_SHARDED else sh.reshape(BIG_SHAPE[n])
        off += rows
    return out


def _pack_full(full):
    parts = []
    for n in BIG:
        r, c = _shard_shape(n)
        g = full[n]
        sh = g.reshape(BIG_SHAPE[n][0], N_CHIPS, c).transpose(1, 0, 2) if n in BIG_COL_SHARDED else g
        parts.append(sh.reshape(N_CHIPS, r * c // PACK_W, PACK_W))
    packed = jnp.concatenate(parts, axis=1)
    return packed.reshape(N_CHIPS, 2, HALF_ROWS, PACK_W).transpose(1, 0, 2, 3)


def _pack_small(parts, last):
    flat = jnp.concatenate([parts[n].reshape(-1) for n in SMALL] + [last.reshape(-1)])
    return jnp.pad(flat, (0, SMALL_ROWS * PACK_W - flat.shape[0])).reshape(SMALL_ROWS, PACK_W)


def _unpack_small(packed):
    flat, out, off = packed.reshape(-1), {}, 0
    for n in SMALL:
        size = 1
        for d in SMALL_SHAPE[n]:
            size *= d
        out[n] = flat[off:off + size].reshape(SMALL_SHAPE[n])
        off += size
    return out, flat[off]


def _place():
    x, y, c = lax.axis_index("x"), lax.axis_index("y"), lax.axis_index("c")
    return x, y, c, [(1 - x, y), (x, 1 - y), (1 - x, 1 - y)]


def _remote(src, dst, send_sems, recv_sems, k, to):
    return pltpu.make_async_remote_copy(src_ref=src, dst_ref=dst, send_sem=send_sems.at[k], recv_sem=recv_sems.at[k],
                                        device_id=to, device_id_type=MESH)


_HBM = pl.BlockSpec(memory_space=pl.ANY)


def _all_gather_weights(wp):
    def body(wp_ref, out_ref, send_sems, recv_sems):
        x, y, c, chips = _place()
        k = 2 * x + y
        sibling = (x, y, 1 - c)
        first =[_remote(wp_ref.at[c], out_ref.at[k, c], send_sems, recv_sems, j, (cx, cy, c))
                 for j, (cx, cy) in enumerate(chips)]
        for cp in first:
            cp.start()
        passed = []
        for j, (cx, cy) in enumerate(chips):
            kj = 2 * cx + cy
            _remote(wp_ref.at[c], out_ref.at[kj, c], send_sems, recv_sems, j, (cx, cy, c)).wait_recv()
            cp = _remote(out_ref.at[kj, c], out_ref.at[kj, c], send_sems, recv_sems, 3 + j, sibling)
            cp.start()
            passed.append(cp)
        for j, (cx, cy) in enumerate(chips):
            kj = 2 * cx + cy
            _remote(wp_ref.at[c], out_ref.at[kj, 1 - c], send_sems, recv_sems, 3 + j, sibling).wait_recv()
        for cp in first + passed:
            cp.wait_send()

    return pl.pallas_call(
        body, name="all_gather_weights", in_specs=[_HBM], out_specs=_HBM,
        out_shape=jax.ShapeDtypeStruct((N_CHIPS, 2, HALF_ROWS, PACK_W), wp.dtype),
        scratch_shapes=[pltpu.SemaphoreType.DMA((6,)), pltpu.SemaphoreType.DMA((6,))])(wp)


def _exchange_halves(pg):
    def body(pg_ref, out_ref, send_sems, recv_sems):
        x, y, c, _ = _place()
        cp = _remote(pg_ref.at[1 - c], out_ref, send_sems, recv_sems, 0, (x, y, 1 - c))
        cp.start()
        cp.wait()

    return pl.pallas_call(
        body, name="exchange_halves", in_specs=[_HBM], out_specs=_HBM,
        out_shape=jax.ShapeDtypeStruct((N_CHIPS, HALF_ROWS, PACK_W), pg.dtype),
        scratch_shapes=[pltpu.SemaphoreType.DMA((1,)), pltpu.SemaphoreType.DMA((1,))])(pg)


def _scatter_chip_sums(ps):
    def body(ps_ref, out_ref, send_sems, recv_sems):
        x, y, c, chips = _place()
        cps = [_remote(ps_ref.at[2 * cx + cy], out_ref.at[j], send_sems, recv_sems, j, (cx, cy, c))
               for j, (cx, cy) in enumerate(chips)]
        for cp in cps:
            cp.start()
        for cp in cps:
            cp.wait()

    return pl.pallas_call(
        body, name="scatter_chip_sums", in_specs=[_HBM], out_specs=_HBM,
        out_shape=jax.ShapeDtypeStruct((3, HALF_ROWS, PACK_W), ps.dtype),
        scratch_shapes=[pltpu.SemaphoreType.DMA((3,)), pltpu.SemaphoreType.DMA((3,))])(ps)


def _share_half(g_half):
    def body(g_ref, out_ref, send_sems, recv_sems):
        x, y, c, _ = _place()
        cp = _remote(g_ref, out_ref.at[c], send_sems, recv_sems, 0, (x, y, 1 - c))
        cp.start()
        _remote(g_ref, out_ref.at[1 - c], send_sems, recv_sems, 0, (x, y, 1 - c)).wait_recv()
        cp.wait_send()

    return pl.pallas_call(
        body, name="share_half", in_specs=[_HBM], out_specs=_HBM,
        out_shape=jax.ShapeDtypeStruct((2, HALF_ROWS, PACK_W), g_half.dtype),
        scratch_shapes=[pltpu.SemaphoreType.DMA((1,)), pltpu.SemaphoreType.DMA((1,))])(g_half)


REDUCE_ROWS = 480


def _sum_pair(pg, theirs, c):
    def body(c_ref, a_ref, b_ref, o_ref):
        o_ref[...] = (a_ref[...] + b_ref[...]).astype(o_ref.dtype)

    return pl.pallas_call(
        body, name="sum_pair",
        grid_spec=pltpu.PrefetchScalarGridSpec(
            num_scalar_prefetch=1, grid=(N_CHIPS, HALF_ROWS // REDUCE_ROWS),
            in_specs=[pl.BlockSpec((None, None, REDUCE_ROWS, PACK_W), lambda j, i, c_ref: (c_ref[0], j, i, 0)),
                      pl.BlockSpec((None, REDUCE_ROWS, PACK_W), lambda j, i, c_ref: (j, i, 0))],
            out_specs=pl.BlockSpec((None, REDUCE_ROWS, PACK_W), lambda j, i, c_ref: (j, i, 0))),
        out_shape=jax.ShapeDtypeStruct((N_CHIPS, HALF_ROWS, PACK_W), WIRE_DTYPE),
        compiler_params=_params("arbitrary", "arbitrary"))(c.reshape(1), pg, theirs)


def _sum_chips(ps, others, k):
    def body(k_ref, a_ref, b_ref, o_ref):
        o_ref[...] = ((a_ref[...].astype(F32) + b_ref[0].astype(F32)) + b_ref[1].astype(F32)) + b_ref[2].astype(F32)

    return pl.pallas_call(
        body, name="sum_chips",
        grid_spec=pltpu.PrefetchScalarGridSpec(
            num_scalar_prefetch=1, grid=(HALF_ROWS // REDUCE_ROWS,),
            in_specs=[pl.BlockSpec((None, REDUCE_ROWS, PACK_W), lambda i, k_ref: (k_ref[0], i, 0)),
                      pl.BlockSpec((3, REDUCE_ROWS, PACK_W), lambda i, k_ref: (0, i, 0))],
            out_specs=pl.BlockSpec((REDUCE_ROWS, PACK_W), lambda i, k_ref: (i, 0))),
        out_shape=jax.ShapeDtypeStruct((HALF_ROWS, PACK_W), F32),
        compiler_params=_params("arbitrary"))(k.reshape(1), ps, others)


def _adamw(w, g, m, v):
    m = ADAM_B1 * m + (1.0 - ADAM_B1) * g
    v = ADAM_B2 * v + (1.0 - ADAM_B2) * (g * g)
    m_hat = m / (1.0 - ADAM_B1 ** ADAM_STEP)
    v_hat = v / (1.0 - ADAM_B2 ** ADAM_STEP)
    return -ADAM_LR * (m_hat / (jnp.sqrt(v_hat) + ADAM_EPS) + ADAM_WD * w), m, v


def _small_reduce_adamw(part, w, m, v):
    def body(part_ref, w_ref, m_ref, v_ref, g_ref, d_ref, nm_ref, nv_ref, all_ref, send_sems, recv_sems):
        x, y, c, chips = _place()
        me, sibling = (x, y, c), (x, y, 1 - c)

        def rows(px, py, pc):
            return all_ref.at[4 * px + 2 * py + pc]

        all_ref[4 * x + 2 * y + c] = part_ref[...]
        first = [_remote(part_ref, rows(*me), send_sems, recv_sems, 0, sibling)]
        first += [_remote(part_ref, rows(*me), send_sems, recv_sems, 1 + j, (cx, cy, c))
                  for j, (cx, cy) in enumerate(chips)]
        for cp in first:
            cp.start()
        passed = []
        for j, (cx, cy) in enumerate(chips):
            _remote(part_ref, rows(cx, cy, c), send_sems, recv_sems, 1 + j, me).wait_recv()
            cp = _remote(rows(cx, cy, c), rows(cx, cy, c), send_sems, recv_sems, 4 + j, sibling)
            cp.start()
            passed.append(cp)
        _remote(part_ref, rows(*sibling), send_sems, recv_sems, 0, me).wait_recv()
        for j, (cx, cy) in enumerate(chips):
            _remote(part_ref, rows(cx, cy, 1 - c), send_sems, recv_sems, 4 + j, me).wait_recv()
        for cp in first + passed:
            cp.wait_send()
        g = all_ref[0]
        for dev in range(1, N_DEV):
            g = g + all_ref[dev]
        delta, nm, nv = _adamw(w_ref[...], g, m_ref[...], v_ref[...])
        g_ref[...] = g
        d_ref[...] = delta
        nm_ref[...] = nm
        nv_ref[...] = nv

    whole = pl.BlockSpec(memory_space=pltpu.VMEM)
    shape = jax.ShapeDtypeStruct((SMALL_ROWS, PACK_W), F32)
    return pl.pallas_call(
        body, name="small_reduce_adamw", in_specs=[whole] * 4, out_specs=[whole] * 4, out_shape=[shape] * 4,
        scratch_shapes=[pltpu.VMEM((N_DEV, SMALL_ROWS, PACK_W), F32), pltpu.SemaphoreType.DMA((7,)),
                        pltpu.SemaphoreType.DMA((7,))],
        compiler_params=pltpu.CompilerParams(vmem_limit_bytes=VMEM_LIMIT))(part, w, m, v)


def kernel(x, p, norm_g, w_in, hg_lb, hg_norm_g, w_o_hg, s5_a_re, s5_a_im, s5_log_dt, s5_b_re, s5_b_im, s5_c_re, s5_c_im, s5_d, w_glu, b_glu, w_o_s5, w_out, ple_norm_g, w_ple, w_ple_gate, final_norm_g, loss_target, m_norm_g, m_w_in, m_hg_lb, m_hg_norm_g, m_w_o_hg, m_s5_a_re, m_s5_a_im, m_s5_log_dt, m_s5_b_re, m_s5_b_im, m_s5_c_re, m_s5_c_im, m_s5_d, m_w_glu, m_b_glu, m_w_o_s5, m_w_out, m_ple_norm_g, m_w_ple, m_w_ple_gate, m_final_norm_g, v_norm_g, v_w_in, v_hg_lb, v_hg_norm_g, v_w_o_hg, v_s5_a_re, v_s5_a_im, v_s5_log_dt, v_s5_b_re, v_s5_b_im, v_s5_c_re, v_s5_c_im, v_s5_d, v_w_glu, v_b_glu, v_w_o_s5, v_w_out, v_ple_norm_g, v_w_ple, v_w_ple_gate, v_final_norm_g):
    given = dict(locals())
    wts = {n: given[n] for n in WEIGHTS}
    mom = {n: given["m_" + n] for n in WEIGHTS}
    var = {n: given["v_" + n] for n in WEIGHTS}
    cx, cy, cc = lax.axis_index("x"), lax.axis_index("y"), lax.axis_index("c")
    chip = (2 * cx + cy).astype(jnp.int32)

    w_shard = _pack_shard({n: wts[n][0] for n in BIG})
    w_wire = w_shard.astype(MXU_DTYPE).reshape(2, HALF_ROWS, PACK_W)
    gathered = lax.dynamic_update_slice(_all_gather_weights(w_wire), w_wire[None], (chip, 0, 0, 0))
    w_full = _unpack_full(gathered.reshape(N_CHIPS, SHARD_ROWS, PACK_W))

    t_len = x.shape[1]
    loss_row, grad_x, g_big, g_small = _local_step(x.reshape(t_len, D_MODEL), p.reshape(t_len, -1),
                                                   loss_target.reshape(t_len, D_MODEL), w_full,
                                                   {n: wts[n] for n in SMALL})

    zero = jnp.zeros((), F32)
    sg, sd, snm, snv = _small_reduce_adamw(_pack_small(g_small, loss_row[0, 0]),
                                           _pack_small({n: wts[n] for n in SMALL}, zero),
                                           _pack_small({n: mom[n] for n in SMALL}, zero),
                                           _pack_small({n: var[n] for n in SMALL}, zero))
    (sg, loss), (sd, _), (snm, _), (snv, _) = (_unpack_small(a) for a in (sg, sd, snm, snv))

    pg = _pack_full(g_big)
    ps = _sum_pair(pg, _exchange_halves(pg), cc.astype(jnp.int32))
    g_half = _sum_chips(ps, _scatter_chip_sums(ps), chip)
    g_shard = lax.dynamic_update_slice(_share_half(g_half), g_half[None], (cc.astype(jnp.int32), 0, 0))
    g_shard = g_shard.reshape(SHARD_ROWS, PACK_W)

    def adam_f(wv, gv, mv, vv):
        return _adamw(wv, gv, mv, vv)

    bd, bnm, bnv = _rowwise("adamw_big", adam_f, SHARD_ROWS, REDUCE_ROWS,
                            [(w_shard, PACK_W, 0), (g_shard, PACK_W, 0),
                             (_pack_shard({n: mom[n][0] for n in BIG}), PACK_W, 0),
                             (_pack_shard({n: var[n][0] for n in BIG}), PACK_W, 0)], [],
                            [(PACK_W, F32), (PACK_W, F32), (PACK_W, F32)])
    bg, bd, bnm, bnv = (_unpack_shard(a) for a in (g_shard, bd, bnm, bnv))

    outs = [loss, grad_x.reshape(x.shape)]
    for small, big in ((sg, bg), (sd, bd), (snm, bnm), (snv, bnv)):
        outs += [big[n] if n in BIG else small[n] for n in WEIGHTS]
    return tuple(outs)
```

```python
import functools

import jax
import jax.numpy as jnp
from jax import lax
from jax.experimental import pallas as pl
from jax.experimental.pallas import tpu as pltpu

F32 = jnp.float32
MXU_DTYPE = jnp.bfloat16
WIRE_DTYPE = jnp.bfloat16
NORM_EPS = 1e-6
D_MODEL = 1024
HG_HEADS = 8
HG_DIM = 128
HG_CHUNK = 64
S5_WIDTH = 512
S5_GROUPS = 32
S5_GROUP = 16
S5_STATE = 64
S5_LANES = S5_GROUPS * S5_STATE
IN_COLS = 7168
SUBLANES = 8
VMEM_LIMIT = 56 * 1024 * 1024
HIGHEST = lax.Precision.HIGHEST
MESH = pl.DeviceIdType.MESH

ADAM_LR, ADAM_B1, ADAM_B2, ADAM_EPS, ADAM_WD, ADAM_STEP = 0.001, 0.9, 0.999, 1e-08, 0.01, 10

BIG = ("w_in", "w_o_hg", "w_glu", "w_o_s5", "w_out", "w_ple", "w_ple_gate")
BIG_SHAPE = {"w_in": (1024, 7168), "w_o_hg": (1024, 1024), "w_glu": (512, 1024), "w_o_s5": (512, 1024),
             "w_out": (1024, 1024), "w_ple": (256, 1024), "w_ple_gate": (1024, 1024)}
BIG_COL_SHARDED = ("w_in", "w_glu", "w_o_s5", "w_ple")
SMALL = ("norm_g", "hg_lb", "hg_norm_g", "s5_a_re", "s5_a_im", "s5_log_dt", "s5_b_re", "s5_b_im", "s5_c_re",
         "s5_c_im", "s5_d", "b_glu", "ple_norm_g", "final_norm_g")
SMALL_SHAPE = {"norm_g": (1, 1024), "hg_lb": (2, 1024), "hg_norm_g": (1, 1024), "s5_a_re": (1, 32, 64),
               "s5_a_im": (1, 32, 64), "s5_log_dt": (1, 32), "s5_b_re": (1, 32, 64, 16), "s5_b_im": (1, 32, 64, 16),
               "s5_c_re": (1, 32, 16, 64), "s5_c_im": (1, 32, 16, 64), "s5_d": (1, 32, 16), "b_glu": (1, 1024),
               "ple_norm_g": (1, 1024), "final_norm_g": (1024,)}
WEIGHTS = ("norm_g", "w_in", "hg_lb", "hg_norm_g", "w_o_hg", "s5_a_re", "s5_a_im", "s5_log_dt", "s5_b_re", "s5_b_im",
           "s5_c_re", "s5_c_im", "s5_d", "w_glu", "b_glu", "w_o_s5", "w_out", "ple_norm_g", "w_ple", "w_ple_gate",
           "final_norm_g")
N_CHIPS = 4
N_DEV = 8
PACK_W = 1024
SHARD_ROWS = sum(BIG_SHAPE[n][0] * BIG_SHAPE[n][1] for n in BIG) // (N_CHIPS * PACK_W)
HALF_ROWS = SHARD_ROWS // 2
SMALL_ROWS = 144


def _params(*sem):
    return pltpu.CompilerParams(dimension_semantics=sem, vmem_limit_bytes=VMEM_LIMIT)


def _sig(x):
    return 1.0 / (1.0 + jnp.exp(-x))


def _dsilu(z, s):
    return s * (1.0 + z * (1.0 - s))


def _mx(x):
    return x.astype(MXU_DTYPE)


def _dot(a, b, dims=(((1,), (0,)), ((), ()))):
    return lax.dot_general(_mx(a), _mx(b), dims, preferred_element_type=F32)


_NT = (((1,), (1,)), ((), ()))
_TN = (((0,), (0,)), ((), ()))


def _dot32(a, b):
    return jnp.dot(a, b, precision=HIGHEST, preferred_element_type=F32)


def _rms_bwd(dy, x, g):
    r = lax.rsqrt(jnp.mean(x * x, axis=-1, keepdims=True) + NORM_EPS)
    t = dy * g
    dx = r * t - x * (r * r * r) * jnp.mean(t * x, axis=-1, keepdims=True)
    return dx, jnp.sum(dy * x * r, axis=0, keepdims=True)


def _rowwise(name, fn, n_rows_total, tm, rows, consts, outs, accs=(), alias=None):
    n_r, n_c, n_o, n_a = len(rows), len(consts), len(outs), len(accs)

    def body(*refs):
        row_refs = refs[:n_r]
        const_refs = refs[n_r:n_r + n_c]
        pos = n_r + n_c + (1 if alias is not None else 0)
        out_refs = refs[pos:pos + n_o]
        acc_refs = refs[pos + n_o:pos + n_o + n_a]
        res = fn(*[r[...] for r in row_refs], *[r[...] for r in const_refs])
        for r, v in zip(out_refs, res[:n_o]):
            r[...] = v.astype(r.dtype)
        if n_a:
            @pl.when(pl.program_id(0) == 0)
            def _():
                for r in acc_refs:
                    r[...] = jnp.zeros_like(r)
            for r, v in zip(acc_refs, res[n_o:]):
                r[...] += v

    in_specs = [pl.BlockSpec((tm, w), functools.partial(lambda i, cb: (i, cb), cb=cb)) for (_, w, cb) in rows]
    in_specs += [pl.BlockSpec(c.shape, lambda i: (0, 0)) for c in consts]
    args = [a for (a, _, _) in rows] + list(consts)
    out_shape, out_specs = [], []
    for o in outs:
        w, dt = o[0], o[1]
        cb, total = (o[2], o[3]) if len(o) == 4 else (0, w)
        out_shape.append(jax.ShapeDtypeStruct((n_rows_total, total), dt))
        out_specs.append(pl.BlockSpec((tm, w), functools.partial(lambda i, cb: (i, cb), cb=cb)))
    io_alias = {}
    if alias is not None:
        in_specs.append(pl.BlockSpec(memory_space=pl.ANY))
        args.append(alias[0])
        io_alias = {len(args) - 1: alias[1]}
    for (r, w) in accs:
        out_shape.append(jax.ShapeDtypeStruct((r, w), F32))
        out_specs.append(pl.BlockSpec((r, w), lambda i: (0, 0)))
    res = pl.pallas_call(body, name=name, grid=(n_rows_total // tm,), in_specs=in_specs, out_specs=out_specs,
                         out_shape=out_shape, input_output_aliases=io_alias,
                         compiler_params=_params("arbitrary"))(*args)
    return res


def _mm_nn(name, a, b, tm, tn, bias=None, resid=None):
    m, k = a.shape
    n = b.shape[1] if b.ndim == 2 else b.shape[0] * b.shape[2]

    def body(*refs):
        acc = _dot(refs[0][...], refs[1][...])
        pos = 2
        if bias is not None:
            acc = acc + refs[pos][...]
            pos += 1
        if resid is not None:
            acc = acc + refs[pos][...]
            pos += 1
        refs[pos][...] = acc

    b_spec = (pl.BlockSpec((k, tn), lambda j, i: (0, j)) if b.ndim == 2
              else pl.BlockSpec((None, k, tn), lambda j, i: (j, 0, 0)))
    in_specs = [pl.BlockSpec((tm, k), lambda j, i: (i, 0)), b_spec]
    args = [a, b]
    if bias is not None:
        in_specs.append(pl.BlockSpec((1, tn), lambda j, i: (0, j)))
        args.append(bias)
    if resid is not None:
        in_specs.append(pl.BlockSpec((tm, tn), lambda j, i: (i, j)))
        args.append(resid)
    return pl.pallas_call(body, name=name, grid=(n // tn, m // tm), in_specs=in_specs,
                          out_specs=pl.BlockSpec((tm, tn), lambda j, i: (i, j)),
                          out_shape=jax.ShapeDtypeStruct((m, n), F32),
                          compiler_params=_params("arbitrary", "arbitrary"))(*args)


def _mm_nt(name, a, b, tm, tn):
    m, n = a.shape
    k = b.shape[0]
    steps = n // tn

    def body(a_ref, b_ref, o_ref, acc_ref):
        s = pl.program_id(1)

        @pl.when(s == 0)
        def _():
            acc_ref[...] = jnp.zeros_like(acc_ref)

        acc_ref[...] += _dot(a_ref[...], b_ref[...], _NT)

        @pl.when(s == steps - 1)
        def _():
            o_ref[...] = acc_ref[...]

    return pl.pallas_call(body, name=name, grid=(m // tm, steps),
                          in_specs=[pl.BlockSpec((tm, tn), lambda i, s: (i, s)),
                                    pl.BlockSpec((k, tn), lambda i, s: (0, s))],
                          out_specs=pl.BlockSpec((tm, k), lambda i, s: (i, 0)),
                          out_shape=jax.ShapeDtypeStruct((m, k), F32),
                          scratch_shapes=[pltpu.VMEM((tm, k), F32)],
                          compiler_params=_params("arbitrary", "arbitrary"))(a, b)


def _mm_nt_then(name, a, b, tm, tn, fn, rows, consts, outs, accs=(), alias=None):
    m, n = a.shape
    k = b.shape[-2]
    steps = n // tn
    n_r, n_c, n_o, n_a = len(rows), len(consts), len(outs), len(accs)

    def body(*refs):
        a_ref, b_ref = refs[:2]
        row_refs = refs[2:2 + n_r]
        const_refs = refs[2 + n_r:2 + n_r + n_c]
        pos = 2 + n_r + n_c + (1 if alias is not None else 0)
        out_refs = refs[pos:pos + n_o]
        acc_refs = refs[pos + n_o:pos + n_o + n_a]
        mm_ref = refs[pos + n_o + n_a]
        i, s = pl.program_id(0), pl.program_id(1)
        part = _dot(a_ref[...], b_ref[...], _NT)
        if steps > 1:
            @pl.when(s == 0)
            def _():
                mm_ref[...] = jnp.zeros_like(mm_ref)
            mm_ref[...] += part

        @pl.when(s == steps - 1)
        def _():
            res = fn(mm_ref[...] if steps > 1 else part, *[r[...] for r in row_refs], *[r[...] for r in const_refs])
            for r, v in zip(out_refs, res[:n_o]):
                r[...] = v.astype(r.dtype)
            if n_a:
                @pl.when(i == 0)
                def _():
                    for r in acc_refs:
                        r[...] = jnp.zeros_like(r)
                for r, v in zip(acc_refs, res[n_o:]):
                    r[...] += v

    b_spec = (pl.BlockSpec((k, tn), lambda i, s: (0, s)) if b.ndim == 2
              else pl.BlockSpec((None, k, tn), lambda i, s: (s, 0, 0)))
    in_specs = [pl.BlockSpec((tm, tn), lambda i, s: (i, s)), b_spec]
    in_specs += [pl.BlockSpec((tm, w), functools.partial(lambda i, s, cb: (i, cb), cb=cb)) for (_, w, cb) in rows]
    in_specs += [pl.BlockSpec(c.shape, lambda i, s: (0, 0)) for c in consts]
    args = [a, b] + [r[0] for r in rows] + list(consts)
    out_shape, out_specs = [], []
    for o in outs:
        w, dt = o[0], o[1]
        cb, total = (o[2], o[3]) if len(o) == 4 else (0, w)
        out_shape.append(jax.ShapeDtypeStruct((m, total), dt))
        out_specs.append(pl.BlockSpec((tm, w), functools.partial(lambda i, s, cb: (i, cb), cb=cb)))
    io_alias = {}
    if alias is not None:
        in_specs.append(pl.BlockSpec(memory_space=pl.ANY))
        args.append(alias[0])
        io_alias = {len(args) - 1: alias[1]}
    for (r, w) in accs:
        out_shape.append(jax.ShapeDtypeStruct((r, w), F32))
        out_specs.append(pl.BlockSpec((r, w), lambda i, s: (0, 0)))
    return pl.pallas_call(body, name=name, grid=(m // tm, steps), in_specs=in_specs, out_specs=out_specs,
                          out_shape=out_shape, input_output_aliases=io_alias,
                          scratch_shapes=[pltpu.VMEM((tm, k), F32)],
                          compiler_params=_params("arbitrary", "arbitrary"))(*args)


def _mm_tn(name, a, b, tk, tn, col_shards=False):
    t, k = a.shape
    n = b.shape[1]
    steps = t // tk

    def body(a_ref, b_ref, o_ref, acc_ref):
        s = pl.program_id(1)

        @pl.when(s == 0)
        def _():
            acc_ref[...] = jnp.zeros_like(acc_ref)

        acc_ref[...] += _dot(a_ref[...], b_ref[...], _TN)

        @pl.when(s == steps - 1)
        def _():
            o_ref[...] = acc_ref[...]

    if col_shards:
        out_spec = pl.BlockSpec((None, k, tn), lambda j, s: (j, 0, 0))
        out_shape = jax.ShapeDtypeStruct((n // tn, k, tn), F32)
    else:
        out_spec = pl.BlockSpec((k, tn), lambda j, s: (0, j))
        out_shape = jax.ShapeDtypeStruct((k, n), F32)
    return pl.pallas_call(body, name=name, grid=(n // tn, steps),
                          in_specs=[pl.BlockSpec((tk, k), lambda j, s: (s, 0)),
                                    pl.BlockSpec((tk, tn), lambda j, s: (s, j))],
                          out_specs=out_spec, out_shape=out_shape,
                          scratch_shapes=[pltpu.VMEM((k, tn), F32)],
                          compiler_params=_params("arbitrary", "arbitrary"))(a, b)


def _hg_chunk_terms(q, f, lb):
    sig = _sig(f)
    fv = lb + (1.0 - lb) * sig
    kk = (1.0 - lb) * (1.0 - sig)
    row = lax.broadcasted_iota(jnp.int32, (HG_CHUNK, HG_CHUNK), 0)
    col = lax.broadcasted_iota(jnp.int32, (HG_CHUNK, HG_CHUNK), 1)
    b = _dot32((row >= col).astype(F32), jnp.log(fv))
    b_mid = b[HG_CHUNK // 2 - 1:HG_CHUNK // 2, :]
    b_last = b[HG_CHUNK - 1:HG_CHUNK, :]
    e_mid = jnp.exp(b - b_mid)
    e_mid_inv = jnp.exp(b_mid - b)
    e_b = jnp.exp(b)
    e_last = jnp.exp(b_last - b)
    return sig, fv, kk, row >= col, row <= col, q * e_mid, kk * e_mid_inv, e_mid, e_mid_inv, e_b, e_last, jnp.exp(b_last)


def _hgrn2_fwd(proj, hg_lb, hg_norm_g, t_len, tb):
    nck = tb // HG_CHUNK

    def body(p_ref, lb_ref, gn_ref, o_ref, act_ref, sp_ref, st_ref):
        @pl.when(pl.program_id(0) == 0)
        def _():
            st_ref[...] = jnp.zeros_like(st_ref)

        for c in range(nck):
            r = pl.ds(c * HG_CHUNK, HG_CHUNK)
            for h in range(HG_HEADS):
                hs = pl.ds(h * HG_DIM, HG_DIM)
                lb = _sig(lb_ref[0:1, hs] - lb_ref[1:2, hs])
                q = p_ref[r, pl.ds(h * HG_DIM, HG_DIM)]
                f = p_ref[r, pl.ds(1024 + h * HG_DIM, HG_DIM)]
                v = p_ref[r, pl.ds(2048 + h * HG_DIM, HG_DIM)]
                _, _, kk, causal, _, a, bm, _, _, e_b, e_last, dc = _hg_chunk_terms(q, f, lb)
                scores = jnp.where(causal, _dot(a, bm, _NT), 0.0)
                st = st_ref[h]
                o = _dot(scores, v) + _dot(q * e_b, st, _NT)
                sp_ref[h, c] = st
                st_ref[h] = dc * st + _dot(v, kk * e_last, _TN)
                o_ref[r, hs] = o

        for h in range(HG_HEADS):
            hs = pl.ds(h * HG_DIM, HG_DIM)
            o = o_ref[:, hs]
            rr = lax.rsqrt(jnp.mean(o * o, axis=-1, keepdims=True) + NORM_EPS)
            g = p_ref[:, pl.ds(3072 + h * HG_DIM, HG_DIM)]
            act_ref[:, hs] = (o * rr * gn_ref[:, hs] * (g * _sig(g))).astype(act_ref.dtype)

    nb = t_len // tb
    return pl.pallas_call(
        body, name="hgrn2_fwd", grid=(nb,),
        in_specs=[pl.BlockSpec((tb, 4096), lambda i: (i, 0)),
                  pl.BlockSpec((2, 1024), lambda i: (0, 0)),
                  pl.BlockSpec((1, 1024), lambda i: (0, 0))],
        out_specs=[pl.BlockSpec((tb, 1024), lambda i: (i, 0)),
                   pl.BlockSpec((tb, 1024), lambda i: (i, 0)),
                   pl.BlockSpec((HG_HEADS, nck, HG_DIM, HG_DIM), lambda i: (0, i, 0, 0))],
        out_shape=[jax.ShapeDtypeStruct((t_len, 1024), F32),
                   jax.ShapeDtypeStruct((t_len, 1024), MXU_DTYPE),
                   jax.ShapeDtypeStruct((HG_HEADS, t_len // HG_CHUNK, HG_DIM, HG_DIM), F32)],
        scratch_shapes=[pltpu.VMEM((HG_HEADS, HG_DIM, HG_DIM), F32)],
        compiler_params=_params("arbitrary"))(proj, hg_lb, hg_norm_g)


def _hgrn2_bwd(proj, d_o, s_prev, hg_lb, dproj, t_len, tb):
    nck = tb // HG_CHUNK
    nb = t_len // tb

    def body(p_ref, do_ref, sp_ref, lb_ref, _, dp_ref, dlb_ref, ds_ref, acc_ref):
        @pl.when(pl.program_id(0) == 0)
        def _():
            ds_ref[...] = jnp.zeros_like(ds_ref)
            acc_ref[...] = jnp.zeros_like(acc_ref)

        for c in reversed(range(nck)):
            r = pl.ds(c * HG_CHUNK, HG_CHUNK)
            for h in range(HG_HEADS):
                hs = pl.ds(h * HG_DIM, HG_DIM)
                lb = _sig(lb_ref[0:1, hs] - lb_ref[1:2, hs])
                q = p_ref[r, pl.ds(h * HG_DIM, HG_DIM)]
                f = p_ref[r, pl.ds(1024 + h * HG_DIM, HG_DIM)]
                v = p_ref[r, pl.ds(2048 + h * HG_DIM, HG_DIM)]
                do = do_ref[r, hs]
                sig, fv, kk, causal, anti, a, bm, e_mid, e_mid_inv, e_b, e_last, dc = _hg_chunk_terms(q, f, lb)
                qd = q * e_b
                kd = kk * e_last
                st = sp_ref[h, c]
                dst = ds_ref[h]
                scores = jnp.where(causal, _dot(a, bm, _NT), 0.0)
                dscores = jnp.where(causal, _dot(do, v, _NT), 0.0)
                dv = _dot(scores, do, _TN) + _dot(kd, dst, _NT)
                da = _dot(dscores, bm)
                dbm = _dot(dscores, a, _TN)
                dqd = _dot(do, st)
                dkd = _dot(v, dst)
                ddc = jnp.sum(dst * st, axis=0, keepdims=True)
                ds_ref[h] = _dot(do, qd, _TN) + dc * dst
                dq = da * e_mid + dqd * e_b
                dk = dbm * e_mid_inv + dkd * e_last
                db = da * a - dbm * bm + dqd * qd - dkd * kd
                extra = jnp.sum(dkd * kd, axis=0, keepdims=True) + ddc * dc
                dlogf = _dot32(anti.astype(F32), db) + extra
                dfv_k = dlogf / fv - dk
                dp_ref[r, pl.ds(h * HG_DIM, HG_DIM)] = dq
                dp_ref[r, pl.ds(1024 + h * HG_DIM, HG_DIM)] = dfv_k * (1.0 - lb) * sig * (1.0 - sig)
                dp_ref[r, pl.ds(2048 + h * HG_DIM, HG_DIM)] = dv
                acc_ref[:, hs] += jnp.sum(dfv_k * (1.0 - sig), axis=0, keepdims=True)

        @pl.when(pl.program_id(0) == nb - 1)
        def _():
            lb_all = _sig(lb_ref[0:1, :] - lb_ref[1:2, :])
            g0 = acc_ref[...] * lb_all * (1.0 - lb_all)
            dlb_ref[0:1, :] = g0
            dlb_ref[1:2, :] = -g0

    return pl.pallas_call(
        body, name="hgrn2_bwd", grid=(nb,),
        in_specs=[pl.BlockSpec((tb, 3072), lambda i: (nb - 1 - i, 0)),
                  pl.BlockSpec((tb, 1024), lambda i: (nb - 1 - i, 0)),
                  pl.BlockSpec((HG_HEADS, nck, HG_DIM, HG_DIM), lambda i: (0, nb - 1 - i, 0, 0)),
                  pl.BlockSpec((2, 1024), lambda i: (0, 0)),
                  pl.BlockSpec(memory_space=pl.ANY)],
        out_specs=[pl.BlockSpec((tb, 3072), lambda i: (nb - 1 - i, 0)),
                   pl.BlockSpec((2, 1024), lambda i: (0, 0))],
        out_shape=[jax.ShapeDtypeStruct((t_len, IN_COLS), F32), jax.ShapeDtypeStruct((2, 1024), F32)],
        scratch_shapes=[pltpu.VMEM((HG_HEADS, HG_DIM, HG_DIM), F32), pltpu.VMEM((1, 1024), F32)],
        input_output_aliases={4: 0},
        compiler_params=_params("arbitrary"))(proj, d_o, s_prev, hg_lb, dproj)


def _dot01(m01, x):
    m = m01.astype(MXU_DTYPE)
    hi = x.astype(MXU_DTYPE)
    r1 = x - hi.astype(F32)
    mid = r1.astype(MXU_DTYPE)
    lo = (r1 - mid.astype(F32)).astype(MXU_DTYPE)
    dot = lambda v: jnp.dot(m, v, preferred_element_type=F32)
    return dot(hi) + dot(mid) + dot(lo)


def _chunk_rows(x, offset, nck):
    return jnp.concatenate([jnp.broadcast_to(x[c * HG_CHUNK + offset:c * HG_CHUNK + offset + 1, :],
                                             (HG_CHUNK, x.shape[1])) for c in range(nck)], axis=0)


def _hg_block_terms(q, f, lb, tb):
    nck = tb // HG_CHUNK
    sig = _sig(f)
    fv = lb + (1.0 - lb) * sig
    kk = (1.0 - lb) * (1.0 - sig)
    row = lax.broadcasted_iota(jnp.int32, (tb, tb), 0)
    col = lax.broadcasted_iota(jnp.int32, (tb, tb), 1)
    same = jnp.right_shift(row, 6) == jnp.right_shift(col, 6)
    causal, anti = same & (row >= col), same & (row <= col)
    b = _dot01(causal, jnp.log(fv))
    b_mid, b_last = _chunk_rows(b, HG_CHUNK // 2 - 1, nck), _chunk_rows(b, HG_CHUNK - 1, nck)
    e_mid, e_mid_inv = jnp.exp(b - b_mid), jnp.exp(b_mid - b)
    e_b, e_last = jnp.exp(b), jnp.exp(b_last - b)
    dcs = [jnp.exp(b[c * HG_CHUNK + HG_CHUNK - 1:(c + 1) * HG_CHUNK, :]) for c in range(nck)]
    return sig, fv, kk, causal, anti, e_mid, e_mid_inv, e_b, e_last, dcs


def _hgrn2_fwd2(proj, hg_lb, hg_norm_g, t_len, tb):
    nck = tb // HG_CHUNK

    def body(p_ref, lb_ref, gn_ref, o_ref, act_ref, sp_ref, st_ref, a_s, bm_s, qd_s, kd_s, v_s):
        @pl.when(pl.program_id(0) == 0)
        def _():
            st_ref[...] = jnp.zeros_like(st_ref)

        lb = _sig(lb_ref[0:1, :] - lb_ref[1:2, :])
        q = p_ref[:, pl.ds(0, 1024)]
        _, _, kk, causal, _, e_mid, e_mid_inv, e_b, e_last, dcs = _hg_block_terms(q, p_ref[:, pl.ds(1024, 1024)],
                                                                                   lb, tb)
        a_s[...] = _mx(q * e_mid)
        bm_s[...] = _mx(kk * e_mid_inv)
        qd_s[...] = _mx(q * e_b)
        kd_s[...] = _mx(kk * e_last)
        v_s[...] = _mx(p_ref[:, pl.ds(2048, 1024)])
        for h in range(HG_HEADS):
            hs = pl.ds(h * HG_DIM, HG_DIM)
            scores = jnp.where(causal, _dot(a_s[:, hs], bm_s[:, hs], _NT), 0.0)
            o_ref[:, hs] = _dot(scores, v_s[:, hs])
        for h in range(HG_HEADS):
            hs = pl.ds(h * HG_DIM, HG_DIM)
            incs = [_dot(v_s[pl.ds(c * HG_CHUNK, HG_CHUNK), hs], kd_s[pl.ds(c * HG_CHUNK, HG_CHUNK), hs], _TN)
                    for c in range(nck)]
            st = st_ref[h]
            for c in range(nck):
                sp_ref[h, c] = st
                st = dcs[c][:, h * HG_DIM:(h + 1) * HG_DIM] * st + incs[c]
            st_ref[h] = st
        for h in range(HG_HEADS):
            hs = pl.ds(h * HG_DIM, HG_DIM)
            for c in range(nck):
                r = pl.ds(c * HG_CHUNK, HG_CHUNK)
                o_ref[r, hs] += _dot(qd_s[r, hs], sp_ref[h, c], _NT)
        for h in range(HG_HEADS):
            hs = pl.ds(h * HG_DIM, HG_DIM)
            o = o_ref[:, hs]
            rr = lax.rsqrt(jnp.mean(o * o, axis=-1, keepdims=True) + NORM_EPS)
            g = p_ref[:, pl.ds(3072 + h * HG_DIM, HG_DIM)]
            act_ref[:, hs] = (o * rr * gn_ref[:, hs] * (g * _sig(g))).astype(act_ref.dtype)

    nb = t_len // tb
    return pl.pallas_call(
        body, name="hgrn2_fwd", grid=(nb,),
        in_specs=[pl.BlockSpec((tb, 4096), lambda i: (i, 0)),
                  pl.BlockSpec((2, 1024), lambda i: (0, 0)),
                  pl.BlockSpec((1, 1024), lambda i: (0, 0))],
        out_specs=[pl.BlockSpec((tb, 1024), lambda i: (i, 0)),
                   pl.BlockSpec((tb, 1024), lambda i: (i, 0)),
                   pl.BlockSpec((HG_HEADS, nck, HG_DIM, HG_DIM), lambda i: (0, i, 0, 0))],
        out_shape=[jax.ShapeDtypeStruct((t_len, 1024), F32),
                   jax.ShapeDtypeStruct((t_len, 1024), MXU_DTYPE),
                   jax.ShapeDtypeStruct((HG_HEADS, t_len // HG_CHUNK, HG_DIM, HG_DIM), F32)],
        scratch_shapes=[pltpu.VMEM((HG_HEADS, HG_DIM, HG_DIM), F32)] + [pltpu.VMEM((tb, 1024), MXU_DTYPE)] * 5,
        compiler_params=_params("arbitrary"))(proj, hg_lb, hg_norm_g)


def _hgrn2_bwd2(proj, d_o, s_prev, hg_lb, dproj, t_len, tb):
    nck = tb // HG_CHUNK
    nb = t_len // tb

    def body(p_ref, do_ref, sp_ref, lb_ref, _, dp_ref, dlb_ref, ds_ref, acc_ref,
             a_s, bm_s, qd_s, kd_s, v_s, do_s, da_s, dbm_s, dqd_s, dkd_s, ex_s):
        @pl.when(pl.program_id(0) == 0)
        def _():
            ds_ref[...] = jnp.zeros_like(ds_ref)
            acc_ref[...] = jnp.zeros_like(acc_ref)

        lb = _sig(lb_ref[0:1, :] - lb_ref[1:2, :])
        q = p_ref[:, pl.ds(0, 1024)]
        sig, fv, kk, causal, anti, e_mid, e_mid_inv, e_b, e_last, dcs = _hg_block_terms(
            q, p_ref[:, pl.ds(1024, 1024)], lb, tb)
        a, bm, qd, kd = q * e_mid, kk * e_mid_inv, q * e_b, kk * e_last
        a_s[...] = _mx(a)
        bm_s[...] = _mx(bm)
        qd_s[...] = _mx(qd)
        kd_s[...] = _mx(kd)
        v_s[...] = _mx(p_ref[:, pl.ds(2048, 1024)])
        do_s[...] = _mx(do_ref[...])
        for h in range(HG_HEADS):
            hs = pl.ds(h * HG_DIM, HG_DIM)
            scores = jnp.where(causal, _dot(a_s[:, hs], bm_s[:, hs], _NT), 0.0)
            dscores = _mx(jnp.where(causal, _dot(do_s[:, hs], v_s[:, hs], _NT), 0.0))
            dp_ref[:, pl.ds(2048 + h * HG_DIM, HG_DIM)] = _dot(scores, do_s[:, hs], _TN)
            da_s[:, hs] = _dot(dscores, bm_s[:, hs])
            dbm_s[:, hs] = _dot(dscores, a_s[:, hs], _TN)
        for h in range(HG_HEADS):
            hs = pl.ds(h * HG_DIM, HG_DIM)
            ups = [_dot(do_s[pl.ds(c * HG_CHUNK, HG_CHUNK), hs], qd_s[pl.ds(c * HG_CHUNK, HG_CHUNK), hs], _TN)
                   for c in range(nck)]
            dst = ds_ref[h]
            for c in reversed(range(nck)):
                r = pl.ds(c * HG_CHUNK, HG_CHUNK)
                st = sp_ref[h, c]
                dc = dcs[c][:, h * HG_DIM:(h + 1) * HG_DIM]
                dp_ref[r, pl.ds(2048 + h * HG_DIM, HG_DIM)] += _dot(kd_s[r, hs], dst, _NT)
                dqd_s[r, hs] = _dot(do_s[r, hs], st)
                dkd_s[r, hs] = _dot(v_s[r, hs], dst)
                ex_s[c:c + 1, hs] = jnp.sum(dst * st, axis=0, keepdims=True) * dc
                dst = ups[c] + dc * dst
            ds_ref[h] = dst
        da, dbm, dqd, dkd = da_s[...], dbm_s[...], dqd_s[...], dkd_s[...]
        dq = da * e_mid + dqd * e_b
        dk = dbm * e_mid_inv + dkd * e_last
        db = da * a - dbm * bm + dqd * qd - dkd * kd
        dkk = dkd * kd
        extra = jnp.concatenate(
            [jnp.broadcast_to(jnp.sum(dkk[c * HG_CHUNK:(c + 1) * HG_CHUNK], axis=0, keepdims=True)
                              + ex_s[c:c + 1, :], (HG_CHUNK, 1024)) for c in range(nck)], axis=0)
        dlogf = _dot01(anti, db) + extra
        dfv_k = dlogf / fv - dk
        dp_ref[:, pl.ds(0, 1024)] = dq
        dp_ref[:, pl.ds(1024, 1024)] = dfv_k * (1.0 - lb) * sig * (1.0 - sig)
        acc_ref[...] += jnp.sum(dfv_k * (1.0 - sig), axis=0, keepdims=True)

        @pl.when(pl.program_id(0) == nb - 1)
        def _():
            g0 = acc_ref[...] * lb * (1.0 - lb)
            dlb_ref[0:1, :] = g0
            dlb_ref[1:2, :] = -g0

    return pl.pallas_call(
        body, name="hgrn2_bwd", grid=(nb,),
        in_specs=[pl.BlockSpec((tb, 3072), lambda i: (nb - 1 - i, 0)),
                  pl.BlockSpec((tb, 1024), lambda i: (nb - 1 - i, 0)),
                  pl.BlockSpec((HG_HEADS, nck, HG_DIM, HG_DIM), lambda i: (0, nb - 1 - i, 0, 0)),
                  pl.BlockSpec((2, 1024), lambda i: (0, 0)),
                  pl.BlockSpec(memory_space=pl.ANY)],
        out_specs=[pl.BlockSpec((tb, 3072), lambda i: (nb - 1 - i, 0)),
                   pl.BlockSpec((2, 1024), lambda i: (0, 0))],
        out_shape=[jax.ShapeDtypeStruct((t_len, IN_COLS), F32), jax.ShapeDtypeStruct((2, 1024), F32)],
        scratch_shapes=[pltpu.VMEM((HG_HEADS, HG_DIM, HG_DIM), F32), pltpu.VMEM((1, 1024), F32)]
                       + [pltpu.VMEM((tb, 1024), MXU_DTYPE)] * 6 + [pltpu.VMEM((tb, 1024), F32)] * 4
                       + [pltpu.VMEM((SUBLANES, 1024), F32)],
        input_output_aliases={4: 0},
        compiler_params=_params("arbitrary"))(proj, d_o, s_prev, hg_lb, dproj)


def _s5_prep(a_re, a_im, log_dt, b_re_t, b_im_t):
    def body(ar_ref, ai_ref, ldt_ref, br_ref, bi_ref, lam_ref, pr_ref, pi_ref, bbr_ref, bbi_ref):
        ar, ai = ar_ref[...], ai_ref[...]
        dt = jnp.exp(ldt_ref[...])
        mag = jnp.exp(ar * dt)
        lr, li = mag * jnp.cos(ai * dt), mag * jnp.sin(ai * dt)
        den = ar * ar + ai * ai
        nr = lr - 1.0
        sr = (nr * ar + li * ai) / den
        si = (li * ar - nr * ai) / den
        lam_ref[0:1, :] = lr
        lam_ref[1:2, :] = li
        cr, ci = lr, li
        for i in range(SUBLANES):
            pr_ref[i:i + 1, :] = cr
            pi_ref[i:i + 1, :] = ci
            cr, ci = cr * lr - ci * li, cr * li + ci * lr
        br, bi = br_ref[...], bi_ref[...]
        bbr_ref[...] = sr * br - si * bi
        bbi_ref[...] = sr * bi + si * br

    whole = pl.BlockSpec(memory_space=pltpu.VMEM)
    return pl.pallas_call(
        body, name="s5_prep", in_specs=[whole] * 5, out_specs=[whole] * 5,
        out_shape=[jax.ShapeDtypeStruct((2, S5_LANES), F32), jax.ShapeDtypeStruct((SUBLANES, S5_LANES), F32),
                   jax.ShapeDtypeStruct((SUBLANES, S5_LANES), F32), jax.ShapeDtypeStruct((S5_GROUP, S5_LANES), F32),
                   jax.ShapeDtypeStruct((S5_GROUP, S5_LANES), F32)])(a_re, a_im, log_dt, b_re_t, b_im_t)


def _s5_prep_bwd(a_re, a_im, log_dt, b_re_t, b_im_t, dlam, dbbr, dbbi):
    def body(ar_ref, ai_ref, ldt_ref, br_ref, bi_ref, dlam_ref, dbbr_ref, dbbi_ref,
             dar_ref, dai_ref, dldt_ref, dbr_ref, dbi_ref):
        ar, ai = ar_ref[...], ai_ref[...]
        dt = jnp.exp(ldt_ref[...])
        mag = jnp.exp(ar * dt)
        cs, sn = jnp.cos(ai * dt), jnp.sin(ai * dt)
        lr, li = mag * cs, mag * sn
        den = ar * ar + ai * ai
        nr = lr - 1.0
        sr = (nr * ar + li * ai) / den
        si = (li * ar - nr * ai) / den
        br, bi = br_ref[...], bi_ref[...]
        gbr, gbi = dbbr_ref[...], dbbi_ref[...]
        dbr_ref[...] = sr * gbr + si * gbi
        dbi_ref[...] = sr * gbi - si * gbr
        dsr = jnp.sum(gbr * br + gbi * bi, axis=0, keepdims=True)
        dsi = jnp.sum(gbi * br - gbr * bi, axis=0, keepdims=True)
        dnr = (dsr * ar - dsi * ai) / den
        dli = dlam_ref[1:2, :] + (dsr * ai + dsi * ar) / den
        dlr = dlam_ref[0:1, :] + dnr
        dden = -(dsr * sr + dsi * si) / den
        dar = (dsr * nr + dsi * li) / den + dden * 2.0 * ar
        dai = (dsr * li - dsi * nr) / den + dden * 2.0 * ai
        dmag = dlr * cs + dli * sn
        dth = mag * (dli * cs - dlr * sn)
        dar_ref[...] = dar + dmag * mag * dt
        dai_ref[...] = dai + dth * dt
        ddt = (dmag * mag * ar + dth * ai) * dt
        lane = lax.broadcasted_iota(jnp.int32, (S5_LANES, 128), 0) // S5_STATE
        grp = lax.broadcasted_iota(jnp.int32, (S5_LANES, 128), 1)
        dldt_ref[...] = _dot32(jnp.broadcast_to(ddt, (SUBLANES, S5_LANES)), (lane == grp).astype(F32))

    whole = pl.BlockSpec(memory_space=pltpu.VMEM)
    return pl.pallas_call(
        body, name="s5_prep_bwd", in_specs=[whole] * 8, out_specs=[whole] * 5,
        out_shape=[jax.ShapeDtypeStruct((1, S5_LANES), F32), jax.ShapeDtypeStruct((1, S5_LANES), F32),
                   jax.ShapeDtypeStruct((SUBLANES, 128), F32), jax.ShapeDtypeStruct((S5_GROUP, S5_LANES), F32),
                   jax.ShapeDtypeStruct((S5_GROUP, S5_LANES), F32)])(a_re, a_im, log_dt, b_re_t, b_im_t, dlam, dbbr,
                                                                      dbbi)


S5_LANE_CHUNK = 512


def _shift_rows(x, s, rowid):
    if s > 0:
        return jnp.where(rowid >= s, pltpu.roll(x, s, 0), 0.0)
    return jnp.where(rowid < SUBLANES + s, pltpu.roll(x, SUBLANES + s, 0), 0.0)


def _scan8(xr, xi, pr, pi, sign, rowid):
    for s, row in ((1, 0), (2, 1), (4, 3)):
        lr, li = pr[row:row + 1, :], pi[row:row + 1, :]
        sr, si = _shift_rows(xr, sign * s, rowid), _shift_rows(xi, sign * s, rowid)
        xr, xi = xr + lr * sr - li * si, xi + lr * si + li * sr
    return xr, xi


def _s5_fwd(proj, pw_re, pw_im, bbr_bd, bbi_bd, crt_bd, cit_bd, d_row, t_len, tb):
    ngrp = tb // SUBLANES

    def body(u_ref, pr_ref, pi_ref, bbr_ref, bbi_ref, crt_ref, cit_ref, d_ref,
             hr_ref, hi_ref, ypre_ref, ys_ref, cr_ref, ci_ref):
        @pl.when(pl.program_id(0) == 0)
        def _():
            cr_ref[...] = jnp.zeros_like(cr_ref)
            ci_ref[...] = jnp.zeros_like(ci_ref)

        u = u_ref[...]
        hr_ref[...] = _dot(u, bbr_ref[...])
        hi_ref[...] = _dot(u, bbi_ref[...])
        rowid = lax.broadcasted_iota(jnp.int32, (SUBLANES, S5_LANE_CHUNK), 0)
        for lc in range(S5_LANES // S5_LANE_CHUNK):
            ls = pl.ds(lc * S5_LANE_CHUNK, S5_LANE_CHUNK)
            pr, pi = pr_ref[:, ls], pi_ref[:, ls]

            def group(g, carry, ls=ls, pr=pr, pi=pi):
                cr, ci = carry
                r = pl.ds(pl.multiple_of(g * SUBLANES, SUBLANES), SUBLANES)
                xr, xi = _scan8(hr_ref[r, ls], hi_ref[r, ls], pr, pi, 1, rowid)
                xr, xi = xr + pr * cr - pi * ci, xi + pr * ci + pi * cr
                hr_ref[r, ls] = xr
                hi_ref[r, ls] = xi
                return xr[SUBLANES - 1:SUBLANES, :], xi[SUBLANES - 1:SUBLANES, :]

            cr, ci = lax.fori_loop(0, ngrp, group, (cr_ref[:, ls], ci_ref[:, ls]))
            cr_ref[:, ls] = cr
            ci_ref[:, ls] = ci
        y = _dot(hr_ref[...], crt_ref[...]) - _dot(hi_ref[...], cit_ref[...]) + d_ref[...] * u
        ypre_ref[...] = y
        ys_ref[...] = jax.nn.gelu(y, approximate=True).astype(ys_ref.dtype)

    whole = pl.BlockSpec(memory_space=pltpu.VMEM)
    return pl.pallas_call(
        body, name="s5_fwd", grid=(t_len // tb,),
        in_specs=[pl.BlockSpec((tb, S5_WIDTH), lambda i: (i, 4096 // S5_WIDTH))] + [whole] * 7,
        out_specs=[pl.BlockSpec((tb, S5_LANES), lambda i: (i, 0)), pl.BlockSpec((tb, S5_LANES), lambda i: (i, 0)),
                   pl.BlockSpec((tb, S5_WIDTH), lambda i: (i, 0)), pl.BlockSpec((tb, S5_WIDTH), lambda i: (i, 0))],
        out_shape=[jax.ShapeDtypeStruct((t_len, S5_LANES), F32), jax.ShapeDtypeStruct((t_len, S5_LANES), F32),
                   jax.ShapeDtypeStruct((t_len, S5_WIDTH), F32), jax.ShapeDtypeStruct((t_len, S5_WIDTH), MXU_DTYPE)],
        scratch_shapes=[pltpu.VMEM((1, S5_LANES), F32), pltpu.VMEM((1, S5_LANES), F32)],
        compiler_params=_params("arbitrary"))(proj, pw_re, pw_im, bbr_bd, bbi_bd, crt_bd, cit_bd, d_row)


def _dgelu(x):
    c, a = 0.7978845608028654, 0.044715
    th = jnp.tanh(c * (x + a * x * x * x))
    return 0.5 * (1.0 + th) + 0.5 * x * (1.0 - th * th) * c * (1.0 + 3.0 * a * x * x)


def _s5_bwd(dgelu, y_pre, proj, h_re, h_im, pwr_re, pwr_im, bbr_bd, bbi_bd, cr_bd, ci_bd, d_row, dproj, t_len, tb):
    ngrp = tb // SUBLANES
    nb = t_len // tb

    def body(dg_ref, yp_ref, u_ref, hr_ref, hi_ref, pr_ref, pi_ref, bbr_ref, bbi_ref, cr_ref, ci_ref, d_ref, _,
             du_ref, dbbr_ref, dbbi_ref, dcr_ref, dci_ref, dd_ref, dlam_ref,
             gr_ref, gi_ref, car_ref, cai_ref, abr_ref, abi_ref, acr_ref, aci_ref, ad_ref, alr_ref, ali_ref, sem):
        @pl.when(pl.program_id(0) == 0)
        def _():
            for ref in (car_ref, cai_ref, abr_ref, abi_ref, acr_ref, aci_ref, ad_ref, alr_ref, ali_ref):
                ref[...] = jnp.zeros_like(ref)

        u = u_ref[...]
        dy = dg_ref[...] * _dgelu(yp_ref[...])
        gr_ref[...] = _dot(dy, cr_ref[...])
        gi_ref[...] = -_dot(dy, ci_ref[...])
        rowid = lax.broadcasted_iota(jnp.int32, (SUBLANES, S5_LANE_CHUNK), 0)
        for lc in range(S5_LANES // S5_LANE_CHUNK):
            ls = pl.ds(lc * S5_LANE_CHUNK, S5_LANE_CHUNK)
            pr, pi = pr_ref[:, ls], pi_ref[:, ls]
            fwd_rows_r = jnp.concatenate([pr[7:8], pr[6:7], pr[6:7], pr[4:5]], axis=0)
            fwd_rows_i = jnp.concatenate([pi[7:8], pi[6:7], pi[6:7], pi[4:5]], axis=0)

            def group(j, carry, ls=ls, pr=pr, pi=pi, fr=fwd_rows_r, fi=fwd_rows_i):
                cr, ci, slr, sli = carry
                g = ngrp - 1 - j
                r = pl.ds(pl.multiple_of(g * SUBLANES, SUBLANES), SUBLANES)
                xr, xi = _scan8(gr_ref[r, ls], gi_ref[r, ls], fr, fi, -1, rowid)
                xr, xi = xr + pr * cr - pi * ci, xi + pr * ci + pi * cr
                gr_ref[r, ls] = xr
                gi_ref[r, ls] = xi
                nr = jnp.where(rowid == SUBLANES - 1, cr, pltpu.roll(xr, SUBLANES - 1, 0))
                ni = jnp.where(rowid == SUBLANES - 1, ci, pltpu.roll(xi, SUBLANES - 1, 0))
                hr, hi = hr_ref[r, ls], hi_ref[r, ls]
                slr = slr + nr * hr + ni * hi
                sli = sli + ni * hr - nr * hi
                return xr[0:1, :], xi[0:1, :], slr, sli

            zero = jnp.zeros((SUBLANES, S5_LANE_CHUNK), F32)
            cr, ci, slr, sli = lax.fori_loop(0, ngrp, group, (car_ref[:, ls], cai_ref[:, ls], zero, zero))
            car_ref[:, ls] = cr
            cai_ref[:, ls] = ci
            alr_ref[:, ls] += jnp.sum(slr, axis=0, keepdims=True)
            ali_ref[:, ls] += jnp.sum(sli, axis=0, keepdims=True)
        gr, gi = gr_ref[...], gi_ref[...]
        du_ref[...] = _dot(gr, bbr_ref[...], _NT) + _dot(gi, bbi_ref[...], _NT) + d_ref[...] * dy
        abr_ref[...] += _dot(u, gr, _TN)
        abi_ref[...] += _dot(u, gi, _TN)
        acr_ref[...] += _dot(hr_ref[...], dy, _TN)
        aci_ref[...] -= _dot(hi_ref[...], dy, _TN)
        ad_ref[...] += jnp.sum(dy * u, axis=0, keepdims=True)

        @pl.when(pl.program_id(0) == nb - 1)
        def _():
            dd_ref[...] = ad_ref[...]
            dlam_ref[0:1, :] = alr_ref[...]
            dlam_ref[1:2, :] = ali_ref[...]
            copies = [pltpu.make_async_copy(s, d, sem.at[k]) for k, (s, d) in enumerate(
                ((abr_ref, dbbr_ref), (abi_ref, dbbi_ref), (acr_ref, dcr_ref), (aci_ref, dci_ref)))]
            for cp in copies:
                cp.start()
            for cp in copies:
                cp.wait()

    whole = pl.BlockSpec(memory_space=pltpu.VMEM)
    hbm = pl.BlockSpec(memory_space=pl.ANY)
    rev = lambda i: (nb - 1 - i, 0)
    return pl.pallas_call(
        body, name="s5_bwd", grid=(nb,),
        in_specs=[pl.BlockSpec((tb, S5_WIDTH), rev), pl.BlockSpec((tb, S5_WIDTH), rev),
                  pl.BlockSpec((tb, S5_WIDTH), lambda i: (nb - 1 - i, 4096 // S5_WIDTH)),
                  pl.BlockSpec((tb, S5_LANES), rev), pl.BlockSpec((tb, S5_LANES), rev)] + [whole] * 7 + [hbm],
        out_specs=[pl.BlockSpec((tb, S5_WIDTH), lambda i: (nb - 1 - i, 4096 // S5_WIDTH)), hbm, hbm, hbm, hbm,
                   pl.BlockSpec((1, S5_WIDTH), lambda i: (0, 0)), pl.BlockSpec((2, S5_LANES), lambda i: (0, 0))],
        out_shape=[jax.ShapeDtypeStruct((t_len, IN_COLS), F32),
                   jax.ShapeDtypeStruct((S5_WIDTH, S5_LANES), F32), jax.ShapeDtypeStruct((S5_WIDTH, S5_LANES), F32),
                   jax.ShapeDtypeStruct((S5_LANES, S5_WIDTH), F32), jax.ShapeDtypeStruct((S5_LANES, S5_WIDTH), F32),
                   jax.ShapeDtypeStruct((1, S5_WIDTH), F32), jax.ShapeDtypeStruct((2, S5_LANES), F32)],
        scratch_shapes=[pltpu.VMEM((tb, S5_LANES), F32), pltpu.VMEM((tb, S5_LANES), F32),
                        pltpu.VMEM((1, S5_LANES), F32), pltpu.VMEM((1, S5_LANES), F32),
                        pltpu.VMEM((S5_WIDTH, S5_LANES), F32), pltpu.VMEM((S5_WIDTH, S5_LANES), F32),
                        pltpu.VMEM((S5_LANES, S5_WIDTH), F32), pltpu.VMEM((S5_LANES, S5_WIDTH), F32),
                        pltpu.VMEM((1, S5_WIDTH), F32), pltpu.VMEM((1, S5_LANES), F32),
                        pltpu.VMEM((1, S5_LANES), F32), pltpu.SemaphoreType.DMA((4,))],
        input_output_aliases={12: 0},
        compiler_params=_params("arbitrary"))(dgelu, y_pre, proj, h_re, h_im, pwr_re, pwr_im, bbr_bd, bbi_bd, cr_bd,
                                              ci_bd, d_row, dproj)


S5_BLOCKS = 4
S5_BW = S5_WIDTH // S5_BLOCKS
S5_BL = S5_LANES // S5_BLOCKS
S5_LANE_BLOCKS = S5_LANES // 128
S5_SCAN_BLOCKS = 4


def _s5_powers(a_re, a_im, log_dt, b_re_t, b_im_t, seg):
    def body(ar_ref, ai_ref, ldt_ref, br_ref, bi_ref, pr_ref, pi_ref, bbr_ref, bbi_ref):
        ar, ai = ar_ref[...], ai_ref[...]
        dt = jnp.exp(ldt_ref[...])
        mag = jnp.exp(ar * dt)
        lr, li = mag * jnp.cos(ai * dt), mag * jnp.sin(ai * dt)
        den = ar * ar + ai * ai
        nr = lr - 1.0
        sr = (nr * ar + li * ai) / den
        si = (li * ar - nr * ai) / den
        cr, ci = lr, li
        for i in range(seg):
            pr_ref[i:i + 1, :] = cr
            pi_ref[i:i + 1, :] = ci
            cr, ci = cr * lr - ci * li, cr * li + ci * lr
        br, bi = br_ref[...], bi_ref[...]
        bbr_ref[...] = sr * br - si * bi
        bbi_ref[...] = sr * bi + si * br

    whole = pl.BlockSpec(memory_space=pltpu.VMEM)
    return pl.pallas_call(
        body, name="s5_prep", in_specs=[whole] * 5, out_specs=[whole] * 4,
        out_shape=[jax.ShapeDtypeStruct((seg, S5_LANES), F32), jax.ShapeDtypeStruct((seg, S5_LANES), F32),
                   jax.ShapeDtypeStruct((S5_GROUP, S5_LANES), F32),
                   jax.ShapeDtypeStruct((S5_GROUP, S5_LANES), F32)])(a_re, a_im, log_dt, b_re_t, b_im_t)


def _scan_tables(pw_re, pw_im, reverse):
    seg = pw_re.shape[0]
    if reverse:
        pw_re, pw_im = pw_re[::-1], -pw_im[::-1]
        one, full = seg - 1, 0
    else:
        one, full = 0, seg - 1
    rows = jnp.stack([pw_re[one], pw_im[one], pw_re[full], pw_im[full]])
    wide = lambda t: jnp.broadcast_to(t[:, None, :], (seg, SUBLANES, S5_LANES))
    return rows, wide(pw_re), wide(pw_im)


def _lanes(j):
    return pl.ds(j * 128, 128)


def _segment_scan(xr_ref, xi_ref, lam_ref, car_ref, cai_ref, cn_r, cn_i, blocks, seg, reverse):
    shape = (SUBLANES, 128)
    lrs = [jnp.broadcast_to(lam_ref[0:1, _lanes(j)], shape) for j in blocks]
    lis = [jnp.broadcast_to(lam_ref[1:2, _lanes(j)], shape) for j in blocks]

    def step(k, carry):
        idx = pl.ds(seg - 1 - k if reverse else k, SUBLANES, stride=seg)
        out = []
        for n, j in enumerate(blocks):
            cr, ci = carry[2 * n], carry[2 * n + 1]
            nr = lrs[n] * cr - lis[n] * ci + xr_ref[j, idx, :]
            ni = lrs[n] * ci + lis[n] * cr + xi_ref[j, idx, :]
            xr_ref[j, idx, :] = nr
            xi_ref[j, idx, :] = ni
            out += [nr, ni]
        return tuple(out)

    zero = jnp.zeros(shape, F32)
    fin = lax.fori_loop(0, seg, step, (zero,) * (2 * len(blocks)), unroll=2)
    for n, j in enumerate(blocks):
        ls = _lanes(j)
        fr, fi = fin[2 * n], fin[2 * n + 1]
        sr, si = lam_ref[2:3, ls], lam_ref[3:4, ls]
        pr, pi = car_ref[:, ls], cai_ref[:, ls]
        for s in (reversed(range(SUBLANES)) if reverse else range(SUBLANES)):
            cn_r[s:s + 1, ls] = pr
            cn_i[s:s + 1, ls] = pi
            pr, pi = fr[s:s + 1, :] + sr * pr - si * pi, fi[s:s + 1, :] + sr * pi + si * pr
        car_ref[:, ls] = pr
        cai_ref[:, ls] = pi


def _s5_fwd2(proj, lam_rows, p3_re, p3_im, bbr4, bbi4, crt4, cit4, d_row, t_len, tb):
    seg = tb // SUBLANES

    def body(u_ref, lam_ref, p3r_ref, p3i_ref, bbr_ref, bbi_ref, crt_ref, cit_ref, d_ref,
             hr_ref, hi_ref, ypre_ref, ys_ref, car_ref, cai_ref, cn_r, cn_i):
        @pl.when(pl.program_id(0) == 0)
        def _():
            car_ref[...] = jnp.zeros_like(car_ref)
            cai_ref[...] = jnp.zeros_like(cai_ref)

        u = u_ref[...]
        for i in range(S5_BLOCKS):
            ui = u[:, i * S5_BW:(i + 1) * S5_BW]
            xr, xi = _dot(ui, bbr_ref[i]), _dot(ui, bbi_ref[i])
            for jj in range(S5_BL // 128):
                hr_ref[i * (S5_BL // 128) + jj] = xr[:, jj * 128:(jj + 1) * 128]
                hi_ref[i * (S5_BL // 128) + jj] = xi[:, jj * 128:(jj + 1) * 128]
        for lc in range(S5_LANE_BLOCKS // S5_SCAN_BLOCKS):
            blocks = range(lc * S5_SCAN_BLOCKS, (lc + 1) * S5_SCAN_BLOCKS)
            _segment_scan(hr_ref, hi_ref, lam_ref, car_ref, cai_ref, cn_r, cn_i, blocks, seg, False)
            crs = [cn_r[:, _lanes(j)] for j in blocks]
            cis = [cn_i[:, _lanes(j)] for j in blocks]

            def fix(t, carry, blocks=blocks, crs=crs, cis=cis):
                idx = pl.ds(t, SUBLANES, stride=seg)
                for n, j in enumerate(blocks):
                    pr, pi = p3r_ref[t, :, _lanes(j)], p3i_ref[t, :, _lanes(j)]
                    hr_ref[j, idx, :] += pr * crs[n] - pi * cis[n]
                    hi_ref[j, idx, :] += pr * cis[n] + pi * crs[n]
                return carry

            lax.fori_loop(0, seg, fix, 0, unroll=2)
        for i in range(S5_BLOCKS):
            ws = pl.ds(i * S5_BW, S5_BW)
            js = range(i * (S5_BL // 128), (i + 1) * (S5_BL // 128))
            hr = jnp.concatenate([hr_ref[j] for j in js], axis=1)
            hi = jnp.concatenate([hi_ref[j] for j in js], axis=1)
            y = _dot(hr, crt_ref[i]) - _dot(hi, cit_ref[i]) + d_ref[:, ws] * u[:, i * S5_BW:(i + 1) * S5_BW]
            ypre_ref[:, ws] = y
            ys_ref[:, ws] = jax.nn.gelu(y, approximate=True).astype(ys_ref.dtype)

    whole = pl.BlockSpec(memory_space=pltpu.VMEM)
    h_spec = pl.BlockSpec((S5_LANE_BLOCKS, tb, 128), lambda i: (0, i, 0))
    return pl.pallas_call(
        body, name="s5_fwd", grid=(t_len // tb,),
        in_specs=[pl.BlockSpec((tb, S5_WIDTH), lambda i: (i, 4096 // S5_WIDTH))] + [whole] * 8,
        out_specs=[h_spec, h_spec,
                   pl.BlockSpec((tb, S5_WIDTH), lambda i: (i, 0)), pl.BlockSpec((tb, S5_WIDTH), lambda i: (i, 0))],
        out_shape=[jax.ShapeDtypeStruct((S5_LANE_BLOCKS, t_len, 128), F32),
                   jax.ShapeDtypeStruct((S5_LANE_BLOCKS, t_len, 128), F32),
                   jax.ShapeDtypeStruct((t_len, S5_WIDTH), F32), jax.ShapeDtypeStruct((t_len, S5_WIDTH), MXU_DTYPE)],
        scratch_shapes=[pltpu.VMEM((1, S5_LANES), F32), pltpu.VMEM((1, S5_LANES), F32),
                        pltpu.VMEM((SUBLANES, S5_LANES), F32), pltpu.VMEM((SUBLANES, S5_LANES), F32)],
        compiler_params=_params("arbitrary"))(proj, lam_rows, p3_re, p3_im, bbr4, bbi4, crt4, cit4, d_row)


def _s5_bwd2(dgelu, y_pre, proj, h_re, h_im, lam_rows, p3_re, p3_im, bbr4, bbi4, cr4, ci4, d_row, dproj, t_len, tb):
    seg = tb // SUBLANES
    nb = t_len // tb

    def body(dg_ref, yp_ref, u_ref, hr_ref, hi_ref, lam_ref, p3r_ref, p3i_ref, bbr_ref, bbi_ref, cr_ref, ci_ref,
             d_ref, _, du_ref, dbbr_ref, dbbi_ref, dcr_ref, dci_ref, dd_ref, dlam_ref,
             gr_ref, gi_ref, car_ref, cai_ref, cn_r, cn_i):
        @pl.when(pl.program_id(0) == 0)
        def _():
            for ref in (car_ref, cai_ref, dbbr_ref, dbbi_ref, dcr_ref, dci_ref, dd_ref, dlam_ref):
                ref[...] = jnp.zeros_like(ref)

        u = u_ref[...]
        dy = dg_ref[...] * _dgelu(yp_ref[...])
        nlb = S5_BL // 128
        for i in range(S5_BLOCKS):
            dyi = dy[:, i * S5_BW:(i + 1) * S5_BW]
            xr, xi = _dot(dyi, cr_ref[i]), -_dot(dyi, ci_ref[i])
            for jj in range(nlb):
                gr_ref[i * nlb + jj] = xr[:, jj * 128:(jj + 1) * 128]
                gi_ref[i * nlb + jj] = xi[:, jj * 128:(jj + 1) * 128]
        for lc in range(S5_LANE_BLOCKS // S5_SCAN_BLOCKS):
            blocks = range(lc * S5_SCAN_BLOCKS, (lc + 1) * S5_SCAN_BLOCKS)
            _segment_scan(gr_ref, gi_ref, lam_ref, car_ref, cai_ref, cn_r, cn_i, blocks, seg, True)
            crs = [cn_r[:, _lanes(j)] for j in blocks]
            cis = [cn_i[:, _lanes(j)] for j in blocks]

            def fix(k, carry, blocks=blocks, crs=crs, cis=cis):
                t = seg - 1 - k
                idx = pl.ds(t, SUBLANES, stride=seg)
                out = []
                for n, j in enumerate(blocks):
                    nr, ni, slr, sli = carry[4 * n:4 * n + 4]
                    pr, pi = p3r_ref[t, :, _lanes(j)], p3i_ref[t, :, _lanes(j)]
                    g_r = gr_ref[j, idx, :] + pr * crs[n] - pi * cis[n]
                    g_i = gi_ref[j, idx, :] + pr * cis[n] + pi * crs[n]
                    gr_ref[j, idx, :] = g_r
                    gi_ref[j, idx, :] = g_i
                    hr, hi = hr_ref[j, idx, :], hi_ref[j, idx, :]
                    out += [g_r, g_i, slr + nr * hr + ni * hi, sli + ni * hr - nr * hi]
                return tuple(out)

            zero = jnp.zeros((SUBLANES, 128), F32)
            init = []
            for n in range(len(blocks)):
                init += [crs[n], cis[n], zero, zero]
            fin = lax.fori_loop(0, seg, fix, tuple(init), unroll=2)
            for n, j in enumerate(blocks):
                dlam_ref[0:1, _lanes(j)] += jnp.sum(fin[4 * n + 2], axis=0, keepdims=True)
                dlam_ref[1:2, _lanes(j)] += jnp.sum(fin[4 * n + 3], axis=0, keepdims=True)
        for i in range(S5_BLOCKS):
            ws = pl.ds(i * S5_BW, S5_BW)
            js = range(i * nlb, (i + 1) * nlb)
            ui, dyi = u[:, i * S5_BW:(i + 1) * S5_BW], dy[:, i * S5_BW:(i + 1) * S5_BW]
            gr = jnp.concatenate([gr_ref[j] for j in js], axis=1)
            gi = jnp.concatenate([gi_ref[j] for j in js], axis=1)
            du_ref[:, ws] = _dot(gr, bbr_ref[i], _NT) + _dot(gi, bbi_ref[i], _NT) + d_ref[:, ws] * dyi
            dbbr_ref[i] += _dot(ui, gr, _TN)
            dbbi_ref[i] += _dot(ui, gi, _TN)
            dcr_ref[i] += _dot(jnp.concatenate([hr_ref[j] for j in js], axis=1), dyi, _TN)
            dci_ref[i] -= _dot(jnp.concatenate([hi_ref[j] for j in js], axis=1), dyi, _TN)
        dd_ref[...] += jnp.sum(dy * u, axis=0, keepdims=True)

    whole = pl.BlockSpec(memory_space=pltpu.VMEM)
    rev = lambda i: (nb - 1 - i, 0)
    const3 = lambda i: (0, 0, 0)
    h_spec = pl.BlockSpec((S5_LANE_BLOCKS, tb, 128), lambda i: (0, nb - 1 - i, 0))
    return pl.pallas_call(
        body, name="s5_bwd", grid=(nb,),
        in_specs=[pl.BlockSpec((tb, S5_WIDTH), rev), pl.BlockSpec((tb, S5_WIDTH), rev),
                  pl.BlockSpec((tb, S5_WIDTH), lambda i: (nb - 1 - i, 4096 // S5_WIDTH)),
                  h_spec, h_spec] + [whole] * 8
                 + [pl.BlockSpec(memory_space=pl.ANY)],
        out_specs=[pl.BlockSpec((tb, S5_WIDTH), lambda i: (nb - 1 - i, 4096 // S5_WIDTH)),
                   pl.BlockSpec((S5_BLOCKS, S5_BW, S5_BL), const3), pl.BlockSpec((S5_BLOCKS, S5_BW, S5_BL), const3),
                   pl.BlockSpec((S5_BLOCKS, S5_BL, S5_BW), const3), pl.BlockSpec((S5_BLOCKS, S5_BL, S5_BW), const3),
                   pl.BlockSpec((1, S5_WIDTH), lambda i: (0, 0)), pl.BlockSpec((2, S5_LANES), lambda i: (0, 0))],
        out_shape=[jax.ShapeDtypeStruct((t_len, IN_COLS), F32),
                   jax.ShapeDtypeStruct((S5_BLOCKS, S5_BW, S5_BL), F32),
                   jax.ShapeDtypeStruct((S5_BLOCKS, S5_BW, S5_BL), F32),
                   jax.ShapeDtypeStruct((S5_BLOCKS, S5_BL, S5_BW), F32),
                   jax.ShapeDtypeStruct((S5_BLOCKS, S5_BL, S5_BW), F32),
                   jax.ShapeDtypeStruct((1, S5_WIDTH), F32), jax.ShapeDtypeStruct((2, S5_LANES), F32)],
        scratch_shapes=[pltpu.VMEM((S5_LANE_BLOCKS, tb, 128), F32), pltpu.VMEM((S5_LANE_BLOCKS, tb, 128), F32),
                        pltpu.VMEM((1, S5_LANES), F32), pltpu.VMEM((1, S5_LANES), F32),
                        pltpu.VMEM((SUBLANES, S5_LANES), F32), pltpu.VMEM((SUBLANES, S5_LANES), F32)],
        input_output_aliases={13: 0},
        compiler_params=_params("arbitrary"))(dgelu, y_pre, proj, h_re, h_im, lam_rows, p3_re, p3_im, bbr4, bbi4,
                                              cr4, ci4, d_row, dproj)


def _to_segment_order(v, stage_ref, out_ref, seg):
    nbl = v.shape[1] // 128
    for b in range(nbl):
        stage_ref[b] = v[:, b * 128:(b + 1) * 128]

    def body(t, carry):
        rows = pl.ds(pl.multiple_of(t * SUBLANES, SUBLANES), SUBLANES)
        for b in range(nbl):
            out_ref[rows, _lanes(b)] = stage_ref[b, pl.ds(t, SUBLANES, stride=seg), :]
        return carry

    lax.fori_loop(0, seg, body, 0)


def _from_segment_order(v, stage_ref, out_ref, seg):
    nbl = v.shape[1] // 128
    for b in range(nbl):
        stage_ref[b] = v[:, b * 128:(b + 1) * 128]
    for s in range(SUBLANES):
        def body(k, carry, s=s):
            rows = pl.ds(pl.multiple_of(s * seg + k * SUBLANES, SUBLANES), SUBLANES)
            for b in range(nbl):
                out_ref[rows, _lanes(b)] = stage_ref[b, pl.ds(k * SUBLANES * SUBLANES + s, SUBLANES,
                                                              stride=SUBLANES), :]
            return carry

        lax.fori_loop(0, seg // SUBLANES, body, 0)


def _tile_scan(xr_ref, xi_ref, lam_ref, car_ref, cai_ref, cn_r, cn_i, blocks, seg, reverse):
    shape = (SUBLANES, 128)
    lrs = [jnp.broadcast_to(lam_ref[0:1, _lanes(j)], shape) for j in blocks]
    lis = [jnp.broadcast_to(lam_ref[1:2, _lanes(j)], shape) for j in blocks]

    def step(k, carry):
        t = seg - 1 - k if reverse else k
        rows = pl.ds(pl.multiple_of(t * SUBLANES, SUBLANES), SUBLANES)
        out = []
        for n, j in enumerate(blocks):
            cr, ci = carry[2 * n], carry[2 * n + 1]
            nr = lrs[n] * cr - lis[n] * ci + xr_ref[rows, _lanes(j)]
            ni = lrs[n] * ci + lis[n] * cr + xi_ref[rows, _lanes(j)]
            xr_ref[rows, _lanes(j)] = nr
            xi_ref[rows, _lanes(j)] = ni
            out += [nr, ni]
        return tuple(out)

    zero = jnp.zeros(shape, F32)
    fin = lax.fori_loop(0, seg, step, (zero,) * (2 * len(blocks)), unroll=2)
    for n, j in enumerate(blocks):
        ls = _lanes(j)
        fr, fi = fin[2 * n], fin[2 * n + 1]
        sr, si = lam_ref[2:3, ls], lam_ref[3:4, ls]
        pr, pi = car_ref[:, ls], cai_ref[:, ls]
        for s in (reversed(range(SUBLANES)) if reverse else range(SUBLANES)):
            cn_r[s:s + 1, ls] = pr
            cn_i[s:s + 1, ls] = pi
            pr, pi = fr[s:s + 1, :] + sr * pr - si * pi, fi[s:s + 1, :] + sr * pi + si * pr
        car_ref[:, ls] = pr
        cai_ref[:, ls] = pi


def _s5_fwd3(proj, lam_rows, p3_re, p3_im, bbr4, bbi4, crt4, cit4, d_row, t_len, tb):
    seg = tb // SUBLANES

    def body(u_ref, lam_ref, p3r_ref, p3i_ref, bbr_ref, bbi_ref, crt_ref, cit_ref, d_ref,
             hr_ref, hi_ref, ypre_ref, ys_ref, car_ref, cai_ref, cn_r, cn_i, stage_ref, us_ref, yseg_ref):
        @pl.when(pl.program_id(0) == 0)
        def _():
            car_ref[...] = jnp.zeros_like(car_ref)
            cai_ref[...] = jnp.zeros_like(cai_ref)

        _to_segment_order(u_ref[...], stage_ref, us_ref, seg)
        u = us_ref[...]
        for i in range(S5_BLOCKS):
            ui = u[:, i * S5_BW:(i + 1) * S5_BW]
            hr_ref[:, pl.ds(i * S5_BL, S5_BL)] = _dot(ui, bbr_ref[i])
            hi_ref[:, pl.ds(i * S5_BL, S5_BL)] = _dot(ui, bbi_ref[i])
        for lc in range(S5_LANE_BLOCKS // S5_SCAN_BLOCKS):
            blocks = range(lc * S5_SCAN_BLOCKS, (lc + 1) * S5_SCAN_BLOCKS)
            _tile_scan(hr_ref, hi_ref, lam_ref, car_ref, cai_ref, cn_r, cn_i, blocks, seg, False)
            crs = [cn_r[:, _lanes(j)] for j in blocks]
            cis = [cn_i[:, _lanes(j)] for j in blocks]

            def fix(t, carry, blocks=blocks, crs=crs, cis=cis):
                rows = pl.ds(pl.multiple_of(t * SUBLANES, SUBLANES), SUBLANES)
                for n, j in enumerate(blocks):
                    pr, pi = p3r_ref[t, :, _lanes(j)], p3i_ref[t, :, _lanes(j)]
                    hr_ref[rows, _lanes(j)] += pr * crs[n] - pi * cis[n]
                    hi_ref[rows, _lanes(j)] += pr * cis[n] + pi * crs[n]
                return carry

            lax.fori_loop(0, seg, fix, 0, unroll=2)
        for i in range(S5_BLOCKS):
            ws = pl.ds(i * S5_BW, S5_BW)
            bl = pl.ds(i * S5_BL, S5_BL)
            yseg_ref[:, ws] = (_dot(hr_ref[:, bl], crt_ref[i]) - _dot(hi_ref[:, bl], cit_ref[i])
                               + d_ref[:, ws] * u[:, i * S5_BW:(i + 1) * S5_BW])
        _from_segment_order(yseg_ref[...], stage_ref, ypre_ref, seg)
        ys_ref[...] = jax.nn.gelu(ypre_ref[...], approximate=True).astype(ys_ref.dtype)

    whole = pl.BlockSpec(memory_space=pltpu.VMEM)
    return pl.pallas_call(
        body, name="s5_fwd", grid=(t_len // tb,),
        in_specs=[pl.BlockSpec((tb, S5_WIDTH), lambda i: (i, 4096 // S5_WIDTH))] + [whole] * 8,
        out_specs=[pl.BlockSpec((tb, S5_LANES), lambda i: (i, 0)), pl.BlockSpec((tb, S5_LANES), lambda i: (i, 0)),
                   pl.BlockSpec((tb, S5_WIDTH), lambda i: (i, 0)), pl.BlockSpec((tb, S5_WIDTH), lambda i: (i, 0))],
        out_shape=[jax.ShapeDtypeStruct((t_len, S5_LANES), F32), jax.ShapeDtypeStruct((t_len, S5_LANES), F32),
                   jax.ShapeDtypeStruct((t_len, S5_WIDTH), F32), jax.ShapeDtypeStruct((t_len, S5_WIDTH), MXU_DTYPE)],
        scratch_shapes=[pltpu.VMEM((1, S5_LANES), F32), pltpu.VMEM((1, S5_LANES), F32),
                        pltpu.VMEM((SUBLANES, S5_LANES), F32), pltpu.VMEM((SUBLANES, S5_LANES), F32),
                        pltpu.VMEM((S5_WIDTH // 128, tb, 128), F32), pltpu.VMEM((tb, S5_WIDTH), F32),
                        pltpu.VMEM((tb, S5_WIDTH), F32)],
        compiler_params=_params("arbitrary"))(proj, lam_rows, p3_re, p3_im, bbr4, bbi4, crt4, cit4, d_row)


def _s5_bwd3(dgelu, y_pre, proj, h_re, h_im, lam_rows, p3_re, p3_im, bbr4, bbi4, cr4, ci4, d_row, dproj, t_len, tb):
    seg = tb // SUBLANES
    nb = t_len // tb

    def body(dg_ref, yp_ref, u_ref, hr_ref, hi_ref, lam_ref, p3r_ref, p3i_ref, bbr_ref, bbi_ref, cr_ref, ci_ref,
             d_ref, _, du_ref, dbbr_ref, dbbi_ref, dcr_ref, dci_ref, dd_ref, dlam_ref,
             gr_ref, gi_ref, car_ref, cai_ref, cn_r, cn_i, stage_ref, us_ref, dys_ref, duseg_ref):
        @pl.when(pl.program_id(0) == 0)
        def _():
            for ref in (car_ref, cai_ref, dbbr_ref, dbbi_ref, dcr_ref, dci_ref, dd_ref, dlam_ref):
                ref[...] = jnp.zeros_like(ref)

        _to_segment_order(u_ref[...], stage_ref, us_ref, seg)
        _to_segment_order(dg_ref[...] * _dgelu(yp_ref[...]), stage_ref, dys_ref, seg)
        u, dy = us_ref[...], dys_ref[...]
        for i in range(S5_BLOCKS):
            dyi = dy[:, i * S5_BW:(i + 1) * S5_BW]
            gr_ref[:, pl.ds(i * S5_BL, S5_BL)] = _dot(dyi, cr_ref[i])
            gi_ref[:, pl.ds(i * S5_BL, S5_BL)] = -_dot(dyi, ci_ref[i])
        for lc in range(S5_LANE_BLOCKS // S5_SCAN_BLOCKS):
            blocks = range(lc * S5_SCAN_BLOCKS, (lc + 1) * S5_SCAN_BLOCKS)
            _tile_scan(gr_ref, gi_ref, lam_ref, car_ref, cai_ref, cn_r, cn_i, blocks, seg, True)
            crs = [cn_r[:, _lanes(j)] for j in blocks]
            cis = [cn_i[:, _lanes(j)] for j in blocks]

            def fix(k, carry, blocks=blocks, crs=crs, cis=cis):
                t = seg - 1 - k
                rows = pl.ds(pl.multiple_of(t * SUBLANES, SUBLANES), SUBLANES)
                out = []
                for n, j in enumerate(blocks):
                    nr, ni, slr, sli = carry[4 * n:4 * n + 4]
                    pr, pi = p3r_ref[t, :, _lanes(j)], p3i_ref[t, :, _lanes(j)]
                    g_r = gr_ref[rows, _lanes(j)] + pr * crs[n] - pi * cis[n]
                    g_i = gi_ref[rows, _lanes(j)] + pr * cis[n] + pi * crs[n]
                    gr_ref[rows, _lanes(j)] = g_r
                    gi_ref[rows, _lanes(j)] = g_i
                    hr, hi = hr_ref[rows, _lanes(j)], hi_ref[rows, _lanes(j)]
                    out += [g_r, g_i, slr + nr * hr + ni * hi, sli + ni * hr - nr * hi]
                return tuple(out)

            zero = jnp.zeros((SUBLANES, 128), F32)
            init = []
            for n in range(len(blocks)):
                init += [crs[n], cis[n], zero, zero]
            fin = lax.fori_loop(0, seg, fix, tuple(init), unroll=2)
            for n, j in enumerate(blocks):
                dlam_ref[0:1, _lanes(j)] += jnp.sum(fin[4 * n + 2], axis=0, keepdims=True)
                dlam_ref[1:2, _lanes(j)] += jnp.sum(fin[4 * n + 3], axis=0, keepdims=True)
        for i in range(S5_BLOCKS):
            ws = pl.ds(i * S5_BW, S5_BW)
            bl = pl.ds(i * S5_BL, S5_BL)
            ui, dyi = u[:, i * S5_BW:(i + 1) * S5_BW], dy[:, i * S5_BW:(i + 1) * S5_BW]
            gr, gi = gr_ref[:, bl], gi_ref[:, bl]
            duseg_ref[:, ws] = _dot(gr, bbr_ref[i], _NT) + _dot(gi, bbi_ref[i], _NT) + d_ref[:, ws] * dyi
            dbbr_ref[i] += _dot(ui, gr, _TN)
            dbbi_ref[i] += _dot(ui, gi, _TN)
            dcr_ref[i] += _dot(hr_ref[:, bl], dyi, _TN)
            dci_ref[i] -= _dot(hi_ref[:, bl], dyi, _TN)
        dd_ref[...] += jnp.sum(dy * u, axis=0, keepdims=True)
        _from_segment_order(duseg_ref[...], stage_ref, du_ref, seg)

    whole = pl.BlockSpec(memory_space=pltpu.VMEM)
    rev = lambda i: (nb - 1 - i, 0)
    const3 = lambda i: (0, 0, 0)
    return pl.pallas_call(
        body, name="s5_bwd", grid=(nb,),
        in_specs=[pl.BlockSpec((tb, S5_WIDTH), rev), pl.BlockSpec((tb, S5_WIDTH), rev),
                  pl.BlockSpec((tb, S5_WIDTH), lambda i: (nb - 1 - i, 4096 // S5_WIDTH)),
                  pl.BlockSpec((tb, S5_LANES), rev), pl.BlockSpec((tb, S5_LANES), rev)] + [whole] * 8
                 + [pl.BlockSpec(memory_space=pl.ANY)],
        out_specs=[pl.BlockSpec((tb, S5_WIDTH), lambda i: (nb - 1 - i, 4096 // S5_WIDTH)),
                   pl.BlockSpec((S5_BLOCKS, S5_BW, S5_BL), const3), pl.BlockSpec((S5_BLOCKS, S5_BW, S5_BL), const3),
                   pl.BlockSpec((S5_BLOCKS, S5_BL, S5_BW), const3), pl.BlockSpec((S5_BLOCKS, S5_BL, S5_BW), const3),
                   pl.BlockSpec((1, S5_WIDTH), lambda i: (0, 0)), pl.BlockSpec((2, S5_LANES), lambda i: (0, 0))],
        out_shape=[jax.ShapeDtypeStruct((t_len, IN_COLS), F32),
                   jax.ShapeDtypeStruct((S5_BLOCKS, S5_BW, S5_BL), F32),
                   jax.ShapeDtypeStruct((S5_BLOCKS, S5_BW, S5_BL), F32),
                   jax.ShapeDtypeStruct((S5_BLOCKS, S5_BL, S5_BW), F32),
                   jax.ShapeDtypeStruct((S5_BLOCKS, S5_BL, S5_BW), F32),
                   jax.ShapeDtypeStruct((1, S5_WIDTH), F32), jax.ShapeDtypeStruct((2, S5_LANES), F32)],
        scratch_shapes=[pltpu.VMEM((tb, S5_LANES), F32), pltpu.VMEM((tb, S5_LANES), F32),
                        pltpu.VMEM((1, S5_LANES), F32), pltpu.VMEM((1, S5_LANES), F32),
                        pltpu.VMEM((SUBLANES, S5_LANES), F32), pltpu.VMEM((SUBLANES, S5_LANES), F32),
                        pltpu.VMEM((S5_WIDTH // 128, tb, 128), F32), pltpu.VMEM((tb, S5_WIDTH), F32),
                        pltpu.VMEM((tb, S5_WIDTH), F32), pltpu.VMEM((tb, S5_WIDTH), F32)],
        input_output_aliases={13: 0},
        compiler_params=_params("arbitrary"))(dgelu, y_pre, proj, h_re, h_im, lam_rows, p3_re, p3_im, bbr4, bbi4,
                                              cr4, ci4, d_row, dproj)


def _block_diag4(per_group):
    g8 = S5_GROUPS // S5_BLOCKS
    eye = jnp.eye(g8, dtype=bool)[None, :, None, :, None]
    dense = jnp.where(eye, per_group.reshape(S5_BLOCKS, g8, S5_GROUP, 1, S5_STATE), 0.0)
    return dense.reshape(S5_BLOCKS, S5_BW, S5_BL)


def _diag_blocks4(dense):
    g8 = S5_GROUPS // S5_BLOCKS
    ar = jnp.arange(g8)
    d5 = dense.reshape(S5_BLOCKS, g8, S5_GROUP, g8, S5_STATE)
    return d5[:, ar, :, ar, :].transpose(1, 0, 2, 3).reshape(S5_GROUPS, S5_GROUP, S5_STATE)


def _block_diag(per_group):
    eye = jnp.eye(S5_GROUPS, dtype=bool)[:, None, :, None]
    dense = jnp.where(eye, per_group[:, :, None, :], 0.0)
    return dense.reshape(S5_WIDTH, S5_LANES)


def _diag_blocks(dense):
    ar = jnp.arange(S5_GROUPS)
    return dense.reshape(S5_GROUPS, S5_GROUP, S5_GROUPS, S5_STATE)[ar, :, ar, :]


def _local_step(x, p, target, w, sm):
    t_len = x.shape[0]
    tm = min(256, t_len)
    tmm = min(512, t_len)
    tb_hg = min(256, t_len)
    tb_s5 = min(256, t_len)
    g1, g2, g3, ghn = sm["norm_g"], sm["ple_norm_g"], sm["final_norm_g"].reshape(1, D_MODEL), sm["hg_norm_g"]

    def rms_f(xv, g):
        r = lax.rsqrt(jnp.mean(xv * xv, axis=-1, keepdims=True) + NORM_EPS)
        return (xv * r * g,)

    (u,) = _rowwise("rms_in", rms_f, t_len, tm, [(x, 1024, 0)], [g1], [(1024, MXU_DTYPE)])
    in_shard = IN_COLS // N_CHIPS
    proj = _mm_nn("mm_in", u, w["w_in"], tmm, in_shard)
    o_hg, act_hg, s_prev = _hgrn2_fwd2(proj, sm["hg_lb"], ghn, t_len, tb_hg)

    lanes = lambda a: a.reshape(1, S5_LANES)
    a_re, a_im = lanes(sm["s5_a_re"]), lanes(sm["s5_a_im"])
    ldt = lanes(jnp.broadcast_to(sm["s5_log_dt"].reshape(S5_GROUPS, 1), (S5_GROUPS, S5_STATE)))
    to_t = lambda b: b.reshape(S5_GROUPS, S5_STATE, S5_GROUP).transpose(2, 0, 1).reshape(S5_GROUP, S5_LANES)
    b_re_t, b_im_t = to_t(sm["s5_b_re"]), to_t(sm["s5_b_im"])
    pw_re, pw_im, bbr_t, bbi_t = _s5_powers(a_re, a_im, ldt, b_re_t, b_im_t, tb_s5 // SUBLANES)
    from_t = lambda b: b.reshape(S5_GROUP, S5_GROUPS, S5_STATE).transpose(1, 0, 2)
    bbr_bd = _block_diag4(from_t(bbr_t)).astype(MXU_DTYPE)
    bbi_bd = _block_diag4(from_t(bbi_t)).astype(MXU_DTYPE)
    cr_bd = _block_diag4(sm["s5_c_re"].reshape(S5_GROUPS, S5_GROUP, S5_STATE)).astype(MXU_DTYPE)
    ci_bd = _block_diag4(sm["s5_c_im"].reshape(S5_GROUPS, S5_GROUP, S5_STATE)).astype(MXU_DTYPE)
    d_row = sm["s5_d"].reshape(1, S5_WIDTH)
    h_re, h_im, y_pre, ys_gelu = _s5_fwd3(proj, *_scan_tables(pw_re, pw_im, False), bbr_bd, bbi_bd,
                                          cr_bd.transpose(0, 2, 1), ci_bd.transpose(0, 2, 1), d_row, t_len, tb_s5)
    def mix_f(act, ysg, z, gh, gs, xv, w_glu, b_glu, w_o_hg, w_o_s5, w_out):
        gl_ = _dot(ysg, w_glu) + b_glu
        a, b = gl_[:, :S5_WIDTH], gl_[:, S5_WIDTH:]
        ys2_ = (a * _sig(b) * (z * _sig(z))).astype(MXU_DTYPE)
        yh, ys = _dot(act, w_o_hg), _dot(ys2_, w_o_s5)
        mg = (_sig(gh) * yh + _sig(gs) * ys).astype(MXU_DTYPE)
        return (gl_, ys2_, yh, ys, mg, xv + _dot(mg, w_out))

    glu, ys2, y_hg, y_s5, merged, h1 = _rowwise(
        "mix_out", mix_f, t_len, tm,
        [(act_hg, 1024, 0), (ys_gelu, 512, 0), (proj, 512, 4608 // 512), (proj, 1024, 5), (proj, 1024, 6),
         (x, 1024, 0)], [w["w_glu"], sm["b_glu"], w["w_o_hg"], w["w_o_s5"], w["w_out"]],
        [(1024, F32), (512, MXU_DTYPE), (1024, F32), (1024, F32), (1024, MXU_DTYPE), (1024, F32)])

    def head_f(h1v, pv, tgt, g_ple, g, w_ple, w_gate):
        r2 = lax.rsqrt(jnp.mean(h1v * h1v, axis=-1, keepdims=True) + NORM_EPS)
        n2_ = (h1v * r2 * g_ple).astype(MXU_DTYPE)
        glv, pev = _dot(n2_, w_gate), _dot(pv, w_ple)
        gate = _sig(glv)
        h2 = h1v + pev * gate
        r = lax.rsqrt(jnp.mean(h2 * h2, axis=-1, keepdims=True) + NORM_EPS)
        e = h2 * r * g - tgt
        loss = 0.5 * jnp.sum(jnp.mean(e * e, axis=-1, keepdims=True), axis=0, keepdims=True)
        dy = e * (1.0 / D_MODEL)
        dg = jnp.sum(dy * h2 * r, axis=0, keepdims=True)
        t = dy * g
        dh2 = r * t - h2 * (r * r * r) * jnp.mean(t * h2, axis=-1, keepdims=True)
        return (n2_, dh2, dh2 * gate, dh2 * pev * gate * (1.0 - gate), jnp.broadcast_to(loss, (1, 128)), dg)

    n2, dh2, dpe, dgl, loss_row, d_g3 = _rowwise(
        "ple_loss_head", head_f, t_len, tm, [(h1, 1024, 0), (p, 256, 0), (target, 1024, 0)],
        [g2, g3, w["w_ple"], w["w_ple_gate"]],
        [(1024, MXU_DTYPE), (1024, F32), (1024, MXU_DTYPE), (1024, MXU_DTYPE)], accs=[(1, 128), (1, 1024)])

    gb = {}
    gb["w_ple"] = _mm_tn("mm_d_w_ple", p, dpe, tmm, 1024)
    gb["w_ple_gate"] = _mm_tn("mm_d_w_ple_gate", n2, dgl, tmm, 1024)
    def ple_b(dn, h1v, dh, g):
        dx, dg = _rms_bwd(dn, h1v, g)
        return (dh + dx, dg)

    dh1, d_g2 = _mm_nt_then("mm_d_n2_rms_ple_bwd", dgl, w["w_ple_gate"], tm, 1024, ple_b,
                            [(h1, 1024, 0), (dh2, 1024, 0)], [g2], [(1024, F32)], accs=[(1, 1024)])
    gb["w_out"] = _mm_tn("mm_d_w_out", merged, dh1, tmm, 1024)
    dmerged = _mm_nt("mm_d_merged", dh1, w["w_out"], tmm, 1024)

    def gate_b(dm, y, gt):
        s = _sig(gt)
        return (dm * s, dm * y * s * (1.0 - s))

    dy_hg, dproj = _rowwise("gate_hg_bwd", gate_b, t_len, tm, [(dmerged, 1024, 0), (y_hg, 1024, 0), (proj, 1024, 5)],
                            [], [(1024, MXU_DTYPE), (1024, F32, 5, IN_COLS)])
    dy_s5, dproj = _rowwise("gate_s5_bwd", gate_b, t_len, tm, [(dmerged, 1024, 0), (y_s5, 1024, 0), (proj, 1024, 6)],
                            [], [(1024, MXU_DTYPE), (1024, F32, 6, IN_COLS)], alias=(dproj, 1))
    gb["w_o_s5"] = _mm_tn("mm_d_w_o_s5", ys2, dy_s5, tmm, 1024)
    def glu_b(dys, gl_, z):
        a, b = gl_[:, :S5_WIDTH], gl_[:, S5_WIDTH:]
        sb, sz = _sig(b), _sig(z)
        silu = z * sz
        dglu = jnp.concatenate([dys * sb * silu, dys * a * silu * sb * (1.0 - sb)], axis=1)
        return (dglu, dys * a * sb * _dsilu(z, sz), jnp.sum(dglu, axis=0, keepdims=True))

    dglu, dproj, d_bglu = _mm_nt_then("mm_d_ys2_glu_bwd", dy_s5, w["w_o_s5"], tm, 1024, glu_b,
                                      [(glu, 1024, 0), (proj, 512, 4608 // 512)], [],
                                      [(1024, MXU_DTYPE), (512, F32, 4608 // 512, IN_COLS)], accs=[(1, 1024)],
                                      alias=(dproj, 1))
    gb["w_glu"] = _mm_tn("mm_d_w_glu", ys_gelu, dglu, tmm, 1024)
    dgelu = _mm_nt("mm_d_gelu", dglu, w["w_glu"], tmm, 1024)
    dproj, d_bbr, d_bbi, d_crt, d_cit, d_d, d_lam = _s5_bwd3(dgelu, y_pre, proj, h_re, h_im,
                                                            *_scan_tables(pw_re, pw_im, True), bbr_bd, bbi_bd, cr_bd,
                                                            ci_bd, d_row, dproj, t_len, tb_s5)
    to_t3 = lambda b: b.transpose(1, 0, 2).reshape(S5_GROUP, S5_LANES)
    d_are, d_aim, d_ldt, d_br_t, d_bi_t = _s5_prep_bwd(a_re, a_im, ldt, b_re_t, b_im_t, d_lam,
                                                       to_t3(_diag_blocks4(d_bbr)), to_t3(_diag_blocks4(d_bbi)))
    gb["w_o_hg"] = _mm_tn("mm_d_w_o_hg", act_hg, dy_hg, tmm, 1024)
    def hg_gate_b(da, o, g, gn):
        dos, dgs, dgns = [], [], []
        for h in range(HG_HEADS):
            sl = slice(h * HG_DIM, (h + 1) * HG_DIM)
            oh, gh, dah, gnh = o[:, sl], g[:, sl], da[:, sl], gn[:, sl]
            rr = lax.rsqrt(jnp.mean(oh * oh, axis=-1, keepdims=True) + NORM_EPS)
            sg = _sig(gh)
            dgs.append(dah * (oh * rr * gnh) * _dsilu(gh, sg))
            don = dah * (gh * sg)
            t = don * gnh
            dos.append(rr * t - oh * (rr * rr * rr) * jnp.mean(t * oh, axis=-1, keepdims=True))
            dgns.append(jnp.sum(don * oh * rr, axis=0, keepdims=True))
        return (jnp.concatenate(dos, axis=1), jnp.concatenate(dgs, axis=1), jnp.concatenate(dgns, axis=1))

    d_o, dproj, d_ghn = _mm_nt_then("mm_d_act_hg_gate_bwd", dy_hg, w["w_o_hg"], tm, 1024, hg_gate_b,
                                    [(o_hg, 1024, 0), (proj, 1024, 3)], [ghn],
                                    [(1024, F32), (1024, F32, 3, IN_COLS)], accs=[(1, 1024)], alias=(dproj, 1))
    dproj, d_lb = _hgrn2_bwd2(proj, d_o, s_prev, sm["hg_lb"], dproj, t_len, tb_hg)
    gb["w_in"] = _mm_tn("mm_d_w_in", u, dproj, tmm, in_shard, col_shards=True)
    def in_b(duv, xv, dh, g):
        dx, dg = _rms_bwd(duv, xv, g)
        return (dh + dx, dg)

    grad_x, d_g1 = _mm_nt_then("mm_d_u_rms_in_bwd", dproj, w["w_in"], tmm, in_shard, in_b,
                               [(x, 1024, 0), (dh1, 1024, 0)], [g1], [(1024, F32)], accs=[(1, 1024)])

    back_t = lambda b: b.reshape(S5_GROUP, S5_GROUPS, S5_STATE).transpose(1, 2, 0).reshape(1, S5_GROUPS, S5_STATE,
                                                                                           S5_GROUP)
    gs = {
        "norm_g": d_g1, "hg_lb": d_lb, "hg_norm_g": d_ghn,
        "s5_a_re": d_are.reshape(1, S5_GROUPS, S5_STATE), "s5_a_im": d_aim.reshape(1, S5_GROUPS, S5_STATE),
        "s5_log_dt": d_ldt[0:1, :S5_GROUPS],
        "s5_b_re": back_t(d_br_t), "s5_b_im": back_t(d_bi_t),
        "s5_c_re": _diag_blocks4(d_crt.transpose(0, 2, 1)).reshape(1, S5_GROUPS, S5_GROUP, S5_STATE),
        "s5_c_im": _diag_blocks4(d_cit.transpose(0, 2, 1)).reshape(1, S5_GROUPS, S5_GROUP, S5_STATE),
        "s5_d": d_d.reshape(1, S5_GROUPS, S5_GROUP), "b_glu": d_bglu, "ple_norm_g": d_g2,
        "final_norm_g": d_g3.reshape(D_MODEL),
    }
    return loss_row, grad_x, gb, gs


def _shard_shape(name):
    r, c = BIG_SHAPE[name]
    return (r, c // N_CHIPS) if name in BIG_COL_SHARDED else (r // N_CHIPS, c)


def _pack_shard(parts):
    return jnp.concatenate([parts[n].reshape(-1, PACK_W) for n in BIG], axis=0)


def _unpack_shard(packed):
    out, off = {}, 0
    for n in BIG:
        r, c = _shard_shape(n)
        rows = r * c // PACK_W
        out[n] = packed[off:off + rows].reshape(1, r, c)
        off += rows
    return out


def _unpack_full(gathered):
    out, off = {}, 0
    for n in BIG:
        r, c = _shard_shape(n)
        rows = r * c // PACK_W
        sh = gathered[:, off:off + rows].reshape(N_CHIPS, r, c)
        out[n] = sh.transpose(1, 0, 2).reshape(BIG_SHAPE[n]) if n in BIG_COL_SHARDED else sh.reshape(BIG_SHAPE[n])
        off += rows
    return out


def _pack_full(full):
    parts = []
    for n in BIG:
        r, c = _shard_shape(n)
        g = full[n]
        sh = g.reshape(BIG_SHAPE[n][0], N_CHIPS, c).transpose(1, 0, 2) if n in BIG_COL_SHARDED else g
        parts.append(sh.reshape(N_CHIPS, r * c // PACK_W, PACK_W))
    packed = jnp.concatenate(parts, axis=1)
    return packed.reshape(N_CHIPS, 2, HALF_ROWS, PACK_W).transpose(1, 0, 2, 3)


def _pack_small(parts, last):
    flat = jnp.concatenate([parts[n].reshape(-1) for n in SMALL] + [last.reshape(-1)])
    return jnp.pad(flat, (0, SMALL_ROWS * PACK_W - flat.shape[0])).reshape(SMALL_ROWS, PACK_W)


def _unpack_small(packed):
    flat, out, off = packed.reshape(-1), {}, 0
    for n in SMALL:
        size = 1
        for d in SMALL_SHAPE[n]:
            size *= d
        out[n] = flat[off:off + size].reshape(SMALL_SHAPE[n])
        off += size
    return out, flat[off]


def _place():
    x, y, c = lax.axis_index("x"), lax.axis_index("y"), lax.axis_index("c")
    return x, y, c, [(1 - x, y), (x, 1 - y), (1 - x, 1 - y)]


def _remote(src, dst, send_sems, recv_sems, k, to):
    return pltpu.make_async_remote_copy(src_ref=src, dst_ref=dst, send_sem=send_sems.at[k], recv_sem=recv_sems.at[k],
                                        device_id=to, device_id_type=MESH)


_HBM = pl.BlockSpec(memory_space=pl.ANY)


def _all_gather_weights(wp):
    def body(wp_ref, out_ref, send_sems, recv_sems):
        x, y, c, chips = _place()
        k = 2 * x + y
        sibling = (x, y, 1 - c)
        first =[_remote(wp_ref.at[c], out_ref.at[k, c], send_sems, recv_sems, j, (cx, cy, c))
                 for j, (cx, cy) in enumerate(chips)]
        for cp in first:
            cp.start()
        passed = []
        for j, (cx, cy) in enumerate(chips):
            kj = 2 * cx + cy
            _remote(wp_ref.at[c], out_ref.at[kj, c], send_sems, recv_sems, j, (cx, cy, c)).wait_recv()
            cp = _remote(out_ref.at[kj, c], out_ref.at[kj, c], send_sems, recv_sems, 3 + j, sibling)
            cp.start()
            passed.append(cp)
        for j, (cx, cy) in enumerate(chips):
            kj = 2 * cx + cy
            _remote(wp_ref.at[c], out_ref.at[kj, 1 - c], send_sems, recv_sems, 3 + j, sibling).wait_recv()
        for cp in first + passed:
            cp.wait_send()

    return pl.pallas_call(
        body, name="all_gather_weights", in_specs=[_HBM], out_specs=_HBM,
        out_shape=jax.ShapeDtypeStruct((N_CHIPS, 2, HALF_ROWS, PACK_W), wp.dtype),
        scratch_shapes=[pltpu.SemaphoreType.DMA((6,)), pltpu.SemaphoreType.DMA((6,))])(wp)


def _exchange_halves(pg):
    def body(pg_ref, out_ref, send_sems, recv_sems):
        x, y, c, _ = _place()
        cp = _remote(pg_ref.at[1 - c], out_ref, send_sems, recv_sems, 0, (x, y, 1 - c))
        cp.start()
        cp.wait()

    return pl.pallas_call(
        body, name="exchange_halves", in_specs=[_HBM], out_specs=_HBM,
        out_shape=jax.ShapeDtypeStruct((N_CHIPS, HALF_ROWS, PACK_W), pg.dtype),
        scratch_shapes=[pltpu.SemaphoreType.DMA((1,)), pltpu.SemaphoreType.DMA((1,))])(pg)


def _scatter_chip_sums(ps):
    def body(ps_ref, out_ref, send_sems, recv_sems):
        x, y, c, chips = _place()
        cps = [_remote(ps_ref.at[2 * cx + cy], out_ref.at[j], send_sems, recv_sems, j, (cx, cy, c))
               for j, (cx, cy) in enumerate(chips)]
        for cp in cps:
            cp.start()
        for cp in cps:
            cp.wait()

    return pl.pallas_call(
        body, name="scatter_chip_sums", in_specs=[_HBM], out_specs=_HBM,
        out_shape=jax.ShapeDtypeStruct((3, HALF_ROWS, PACK_W), ps.dtype),
        scratch_shapes=[pltpu.SemaphoreType.DMA((3,)), pltpu.SemaphoreType.DMA((3,))])(ps)


def _share_half(g_half):
    def body(g_ref, out_ref, send_sems, recv_sems):
        x, y, c, _ = _place()
        cp = _remote(g_ref, out_ref.at[c], send_sems, recv_sems, 0, (x, y, 1 - c))
        cp.start()
        _remote(g_ref, out_ref.at[1 - c], send_sems, recv_sems, 0, (x, y, 1 - c)).wait_recv()
        cp.wait_send()

    return pl.pallas_call(
        body, name="share_half", in_specs=[_HBM], out_specs=_HBM,
        out_shape=jax.ShapeDtypeStruct((2, HALF_ROWS, PACK_W), g_half.dtype),
        scratch_shapes=[pltpu.SemaphoreType.DMA((1,)), pltpu.SemaphoreType.DMA((1,))])(g_half)


REDUCE_ROWS = 480


def _sum_pair(pg, theirs, c):
    def body(c_ref, a_ref, b_ref, o_ref):
        o_ref[...] = (a_ref[...] + b_ref[...]).astype(o_ref.dtype)

    return pl.pallas_call(
        body, name="sum_pair",
        grid_spec=pltpu.PrefetchScalarGridSpec(
            num_scalar_prefetch=1, grid=(N_CHIPS, HALF_ROWS // REDUCE_ROWS),
            in_specs=[pl.BlockSpec((None, None, REDUCE_ROWS, PACK_W), lambda j, i, c_ref: (c_ref[0], j, i, 0)),
                      pl.BlockSpec((None, REDUCE_ROWS, PACK_W), lambda j, i, c_ref: (j, i, 0))],
            out_specs=pl.BlockSpec((None, REDUCE_ROWS, PACK_W), lambda j, i, c_ref: (j, i, 0))),
        out_shape=jax.ShapeDtypeStruct((N_CHIPS, HALF_ROWS, PACK_W), WIRE_DTYPE),
        compiler_params=_params("arbitrary", "arbitrary"))(c.reshape(1), pg, theirs)


def _sum_chips(ps, others, k):
    def body(k_ref, a_ref, b_ref, o_ref):
        o_ref[...] = ((a_ref[...].astype(F32) + b_ref[0].astype(F32)) + b_ref[1].astype(F32)) + b_ref[2].astype(F32)

    return pl.pallas_call(
        body, name="sum_chips",
        grid_spec=pltpu.PrefetchScalarGridSpec(
            num_scalar_prefetch=1, grid=(HALF_ROWS // REDUCE_ROWS,),
            in_specs=[pl.BlockSpec((None, REDUCE_ROWS, PACK_W), lambda i, k_ref: (k_ref[0], i, 0)),
                      pl.BlockSpec((3, REDUCE_ROWS, PACK_W), lambda i, k_ref: (0, i, 0))],
            out_specs=pl.BlockSpec((REDUCE_ROWS, PACK_W), lambda i, k_ref: (i, 0))),
        out_shape=jax.ShapeDtypeStruct((HALF_ROWS, PACK_W), F32),
        compiler_params=_params("arbitrary"))(k.reshape(1), ps, others)


REST = tuple(n for n in BIG if n != "w_in")
REST_ROWS = sum(BIG_SHAPE[n][0] * BIG_SHAPE[n][1] for n in REST) // (N_CHIPS * PACK_W)
IN_SHARD = IN_COLS // N_CHIPS
IN_TILE, REST_TILE = 256, 272


def _pack_rest(parts):
    return jnp.concatenate([parts[n].reshape(-1, PACK_W) for n in REST], axis=0)


def _unpack_rest(packed):
    out, off = {}, 0
    for n in REST:
        r, c = _shard_shape(n)
        rows = r * c // PACK_W
        out[n] = packed[off:off + rows].reshape(1, r, c)
        off += rows
    return out


def _unpack_rest_full(gathered):
    out, off = {}, 0
    for n in REST:
        r, c = _shard_shape(n)
        rows = r * c // PACK_W
        sh = gathered[:, off:off + rows].reshape(N_CHIPS, r, c)
        out[n] = sh.transpose(1, 0, 2).reshape(BIG_SHAPE[n]) if n in BIG_COL_SHARDED else sh.reshape(BIG_SHAPE[n])
        off += rows
    return out


def _pack_rest_full(full):
    parts = []
    for n in REST:
        r, c = _shard_shape(n)
        g = full[n]
        sh = g.reshape(BIG_SHAPE[n][0], N_CHIPS, c).transpose(1, 0, 2) if n in BIG_COL_SHARDED else g
        parts.append(sh.reshape(N_CHIPS, r * c // PACK_W, PACK_W))
    return jnp.concatenate(parts, axis=1).reshape(N_CHIPS, 2, REST_ROWS // 2, PACK_W)


def _gather_shards(ws):
    n = len(ws)

    def body(*refs):
        w_refs, out_refs, (send_sems, recv_sems) = refs[:n], refs[n:2 * n], refs[2 * n:]
        x, y, c, chips = _place()
        k = 2 * x + y
        sibling = (x, y, 1 - c)
        first = [_remote(w_ref.at[c], out_ref.at[k, c], send_sems, recv_sems, 6 * g + j, (cx, cy, c))
                 for j, (cx, cy) in enumerate(chips) for g, (w_ref, out_ref) in enumerate(zip(w_refs, out_refs))]
        for cp in first:
            cp.start()
        passed = []
        for j, (cx, cy) in enumerate(chips):
            kj = 2 * cx + cy
            for g, (w_ref, out_ref) in enumerate(zip(w_refs, out_refs)):
                _remote(w_ref.at[c], out_ref.at[kj, c], send_sems, recv_sems, 6 * g + j, (cx, cy, c)).wait_recv()
                cp = _remote(out_ref.at[kj, c], out_ref.at[kj, c], send_sems, recv_sems, 6 * g + 3 + j, sibling)
                cp.start()
                passed.append(cp)
        for j, (cx, cy) in enumerate(chips):
            kj = 2 * cx + cy
            for g, (w_ref, out_ref) in enumerate(zip(w_refs, out_refs)):
                _remote(w_ref.at[c], out_ref.at[kj, 1 - c], send_sems, recv_sems, 6 * g + 3 + j, sibling).wait_recv()
        for cp in first + passed:
            cp.wait_send()

    return pl.pallas_call(
        body, name="all_gather_weights", in_specs=[_HBM] * n, out_specs=[_HBM] * n,
        out_shape=[jax.ShapeDtypeStruct((N_CHIPS,) + w.shape, w.dtype) for w in ws],
        scratch_shapes=[pltpu.SemaphoreType.DMA((6 * n,)), pltpu.SemaphoreType.DMA((6 * n,))])(*ws)


def _swap_halves(pgs):
    n = len(pgs)

    def body(*refs):
        pg_refs, out_refs, (send_sems, recv_sems) = refs[:n], refs[n:2 * n], refs[2 * n:]
        x, y, c, _ = _place()
        cps = [_remote(pg_ref.at[j, 1 - c], out_ref.at[j], send_sems, recv_sems, N_CHIPS * g + j, (x, y, 1 - c))
               for g, (pg_ref, out_ref) in enumerate(zip(pg_refs, out_refs)) for j in range(N_CHIPS)]
        for cp in cps:
            cp.start()
        for cp in cps:
            cp.wait()

    return pl.pallas_call(
        body, name="exchange_halves", in_specs=[_HBM] * n, out_specs=[_HBM] * n,
        out_shape=[jax.ShapeDtypeStruct((N_CHIPS,) + pg.shape[2:], pg.dtype) for pg in pgs],
        scratch_shapes=[pltpu.SemaphoreType.DMA((N_CHIPS * n,)), pltpu.SemaphoreType.DMA((N_CHIPS * n,))])(*pgs)


def _scatter_sums(pss):
    n = len(pss)

    def body(*refs):
        ps_refs, out_refs, (send_sems, recv_sems) = refs[:n], refs[n:2 * n], refs[2 * n:]
        x, y, c, chips = _place()
        cps = [_remote(ps_ref.at[2 * cx + cy], out_ref.at[j], send_sems, recv_sems, 3 * g + j, (cx, cy, c))
               for j, (cx, cy) in enumerate(chips) for g, (ps_ref, out_ref) in enumerate(zip(ps_refs, out_refs))]
        for cp in cps:
            cp.start()
        for cp in cps:
            cp.wait()

    return pl.pallas_call(
        body, name="scatter_chip_sums", in_specs=[_HBM] * n, out_specs=[_HBM] * n,
        out_shape=[jax.ShapeDtypeStruct((3,) + ps.shape[1:], ps.dtype) for ps in pss],
        scratch_shapes=[pltpu.SemaphoreType.DMA((3 * n,)), pltpu.SemaphoreType.DMA((3 * n,))])(*pss)


def _share_halves(gs):
    n = len(gs)

    def body(*refs):
        g_refs, out_refs, (send_sems, recv_sems) = refs[:n], refs[n:2 * n], refs[2 * n:]
        x, y, c, _ = _place()
        cps = [_remote(g_ref, out_ref.at[c], send_sems, recv_sems, g, (x, y, 1 - c))
               for g, (g_ref, out_ref) in enumerate(zip(g_refs, out_refs))]
        for cp in cps:
            cp.start()
        for g, (g_ref, out_ref) in enumerate(zip(g_refs, out_refs)):
            _remote(g_ref, out_ref.at[1 - c], send_sems, recv_sems, g, (x, y, 1 - c)).wait_recv()
        for cp in cps:
            cp.wait_send()

    return pl.pallas_call(
        body, name="share_half", in_specs=[_HBM] * n, out_specs=[_HBM] * n,
        out_shape=[jax.ShapeDtypeStruct((2,) + g.shape, g.dtype) for g in gs],
        scratch_shapes=[pltpu.SemaphoreType.DMA((n,)), pltpu.SemaphoreType.DMA((n,))])(*gs)


def _pair_sum(name, pg, theirs, c, tile):
    _, _, rows, width = pg.shape

    def body(c_ref, a_ref, b_ref, o_ref):
        o_ref[...] = (a_ref[...] + b_ref[...]).astype(o_ref.dtype)

    return pl.pallas_call(
        body, name=name,
        grid_spec=pltpu.PrefetchScalarGridSpec(
            num_scalar_prefetch=1, grid=(N_CHIPS, rows // tile),
            in_specs=[pl.BlockSpec((None, None, tile, width), lambda j, i, c_ref: (j, c_ref[0], i, 0)),
                      pl.BlockSpec((None, tile, width), lambda j, i, c_ref: (j, i, 0))],
            out_specs=pl.BlockSpec((None, tile, width), lambda j, i, c_ref: (j, i, 0))),
        out_shape=jax.ShapeDtypeStruct((N_CHIPS, rows, width), WIRE_DTYPE),
        compiler_params=_params("arbitrary", "arbitrary"))(c.reshape(1), pg, theirs)


def _chip_sum(name, ps, others, k, tile):
    _, rows, width = ps.shape

    def body(k_ref, a_ref, b_ref, o_ref):
        o_ref[...] = ((a_ref[...].astype(F32) + b_ref[0].astype(F32)) + b_ref[1].astype(F32)) + b_ref[2].astype(F32)

    return pl.pallas_call(
        body, name=name,
        grid_spec=pltpu.PrefetchScalarGridSpec(
            num_scalar_prefetch=1, grid=(rows // tile,),
            in_specs=[pl.BlockSpec((None, tile, width), lambda i, k_ref: (k_ref[0], i, 0)),
                      pl.BlockSpec((3, tile, width), lambda i, k_ref: (0, i, 0))],
            out_specs=pl.BlockSpec((tile, width), lambda i, k_ref: (i, 0))),
        out_shape=jax.ShapeDtypeStruct((rows, width), F32),
        compiler_params=_params("arbitrary"))(k.reshape(1), ps, others)


def _adamw(w, g, m, v):
    m = ADAM_B1 * m + (1.0 - ADAM_B1) * g
    v = ADAM_B2 * v + (1.0 - ADAM_B2) * (g * g)
    m_hat = m / (1.0 - ADAM_B1 ** ADAM_STEP)
    v_hat = v / (1.0 - ADAM_B2 ** ADAM_STEP)
    return -ADAM_LR * (m_hat / (jnp.sqrt(v_hat) + ADAM_EPS) + ADAM_WD * w), m, v


def _small_reduce_adamw(part, w, m, v):
    def body(part_ref, w_ref, m_ref, v_ref, g_ref, d_ref, nm_ref, nv_ref, all_ref, send_sems, recv_sems):
        x, y, c, chips = _place()
        me, sibling = (x, y, c), (x, y, 1 - c)

        def rows(px, py, pc):
            return all_ref.at[4 * px + 2 * py + pc]

        all_ref[4 * x + 2 * y + c] = part_ref[...]
        first = [_remote(part_ref, rows(*me), send_sems, recv_sems, 0, sibling)]
        first += [_remote(part_ref, rows(*me), send_sems, recv_sems, 1 + j, (cx, cy, c))
                  for j, (cx, cy) in enumerate(chips)]
        for cp in first:
            cp.start()
        passed = []
        for j, (cx, cy) in enumerate(chips):
            _remote(part_ref, rows(cx, cy, c), send_sems, recv_sems, 1 + j, me).wait_recv()
            cp = _remote(rows(cx, cy, c), rows(cx, cy, c), send_sems, recv_sems, 4 + j, sibling)
            cp.start()
            passed.append(cp)
        _remote(part_ref, rows(*sibling), send_sems, recv_sems, 0, me).wait_recv()
        for j, (cx, cy) in enumerate(chips):
            _remote(part_ref, rows(cx, cy, 1 - c), send_sems, recv_sems, 4 + j, me).wait_recv()
        for cp in first + passed:
            cp.wait_send()
        g = all_ref[0]
        for dev in range(1, N_DEV):
            g = g + all_ref[dev]
        delta, nm, nv = _adamw(w_ref[...], g, m_ref[...], v_ref[...])
        g_ref[...] = g
        d_ref[...] = delta
        nm_ref[...] = nm
        nv_ref[...] = nv

    whole = pl.BlockSpec(memory_space=pltpu.VMEM)
    shape = jax.ShapeDtypeStruct((SMALL_ROWS, PACK_W), F32)
    return pl.pallas_call(
        body, name="small_reduce_adamw", in_specs=[whole] * 4, out_specs=[whole] * 4, out_shape=[shape] * 4,
        scratch_shapes=[pltpu.VMEM((N_DEV, SMALL_ROWS, PACK_W), F32), pltpu.SemaphoreType.DMA((7,)),
                        pltpu.SemaphoreType.DMA((7,))],
        compiler_params=pltpu.CompilerParams(vmem_limit_bytes=VMEM_LIMIT))(part, w, m, v)


def kernel(x, p, norm_g, w_in, hg_lb, hg_norm_g, w_o_hg, s5_a_re, s5_a_im, s5_log_dt, s5_b_re, s5_b_im, s5_c_re, s5_c_im, s5_d, w_glu, b_glu, w_o_s5, w_out, ple_norm_g, w_ple, w_ple_gate, final_norm_g, loss_target, m_norm_g, m_w_in, m_hg_lb, m_hg_norm_g, m_w_o_hg, m_s5_a_re, m_s5_a_im, m_s5_log_dt, m_s5_b_re, m_s5_b_im, m_s5_c_re, m_s5_c_im, m_s5_d, m_w_glu, m_b_glu, m_w_o_s5, m_w_out, m_ple_norm_g, m_w_ple, m_w_ple_gate, m_final_norm_g, v_norm_g, v_w_in, v_hg_lb, v_hg_norm_g, v_w_o_hg, v_s5_a_re, v_s5_a_im, v_s5_log_dt, v_s5_b_re, v_s5_b_im, v_s5_c_re, v_s5_c_im, v_s5_d, v_w_glu, v_b_glu, v_w_o_s5, v_w_out, v_ple_norm_g, v_w_ple, v_w_ple_gate, v_final_norm_g):
    given = dict(locals())
    wts = {n: given[n] for n in WEIGHTS}
    mom = {n: given["m_" + n] for n in WEIGHTS}
    var = {n: given["v_" + n] for n in WEIGHTS}
    cx, cy, cc = lax.axis_index("x"), lax.axis_index("y"), lax.axis_index("c")
    chip = (2 * cx + cy).astype(jnp.int32)

    core = cc.astype(jnp.int32)
    rest_shard = _pack_rest({n: wts[n][0] for n in REST})
    wires = [wts["w_in"][0].astype(MXU_DTYPE).reshape(2, D_MODEL // 2, IN_SHARD),
             rest_shard.astype(MXU_DTYPE).reshape(2, REST_ROWS // 2, PACK_W)]
    gathered = [lax.dynamic_update_slice(got, mine[None], (chip, 0, 0, 0))
                for got, mine in zip(_gather_shards(wires), wires)]
    w_full = _unpack_rest_full(gathered[1].reshape(N_CHIPS, REST_ROWS, PACK_W))
    w_full["w_in"] = gathered[0].reshape(N_CHIPS, D_MODEL, IN_SHARD)

    t_len = x.shape[1]
    loss_row, grad_x, g_big, g_small = _local_step(x.reshape(t_len, D_MODEL), p.reshape(t_len, -1),
                                                   loss_target.reshape(t_len, D_MODEL), w_full,
                                                   {n: wts[n] for n in SMALL})

    zero = jnp.zeros((), F32)
    sg, sd, snm, snv = _small_reduce_adamw(_pack_small(g_small, loss_row[0, 0]),
                                           _pack_small({n: wts[n] for n in SMALL}, zero),
                                           _pack_small({n: mom[n] for n in SMALL}, zero),
                                           _pack_small({n: var[n] for n in SMALL}, zero))
    (sg, loss), (sd, _), (snm, _), (snv, _) = (_unpack_small(a) for a in (sg, sd, snm, snv))

    pgs = [g_big["w_in"].reshape(N_CHIPS, 2, D_MODEL // 2, IN_SHARD), _pack_rest_full(g_big)]
    names, tiles = ("in", "rest"), (IN_TILE, REST_TILE)
    pss = [_pair_sum("sum_pair_" + nm, pg, got, core, tl) for nm, pg, got, tl in zip(names, pgs, _swap_halves(pgs), tiles)]
    halves = [_chip_sum("sum_chips_" + nm, ps, got, chip, tl)
              for nm, ps, got, tl in zip(names, pss, _scatter_sums(pss), tiles)]
    g_in, g_rest = [lax.dynamic_update_slice(got, mine[None], (core, 0, 0))
                    for got, mine in zip(_share_halves(halves), halves)]
    g_in, g_rest = g_in.reshape(D_MODEL, IN_SHARD), g_rest.reshape(REST_ROWS, PACK_W)

    def adam_f(wv, gv, mv, vv):
        return _adamw(wv, gv, mv, vv)

    d_in, nm_in, nv_in = _rowwise("adamw_in", adam_f, D_MODEL, IN_TILE,
                                  [(wts["w_in"][0], IN_SHARD, 0), (g_in, IN_SHARD, 0), (mom["w_in"][0], IN_SHARD, 0),
                                   (var["w_in"][0], IN_SHARD, 0)], [], [(IN_SHARD, F32)] * 3)
    d_rest, nm_rest, nv_rest = _rowwise("adamw_rest", adam_f, REST_ROWS, REST_TILE,
                                        [(rest_shard, PACK_W, 0), (g_rest, PACK_W, 0),
                                         (_pack_rest({n: mom[n][0] for n in REST}), PACK_W, 0),
                                         (_pack_rest({n: var[n][0] for n in REST}), PACK_W, 0)], [],
                                        [(PACK_W, F32)] * 3)
    bg, bd, bnm, bnv = (dict(_unpack_rest(rest), w_in=a.reshape(1, D_MODEL, IN_SHARD))
                        for rest, a in ((g_rest, g_in), (d_rest, d_in), (nm_rest, nm_in), (nv_rest, nv_in)))

    outs = [loss, grad_x.reshape(x.shape)]
    for small, big in ((sg, bg), (sd, bd), (snm, bnm), (snv, bnv)):
        outs += [big[n] if n in BIG else small[n] for n in WEIGHTS]
    return tuple(outs)
```

```python
import functools
from typing import Callable, NamedTuple

import jax
import jax.numpy as jnp
from jax import lax
from jax.experimental import pallas as pl
from jax.experimental.pallas import tpu as pltpu

F32 = jnp.float32
MXU_DTYPE = jnp.bfloat16
WIRE_DTYPE = jnp.bfloat16
NORM_EPS = 1e-6
D_MODEL = 1024
HG_HEADS = 8
HG_DIM = 128
HG_CHUNK = 64
S5_WIDTH = 512
S5_GROUPS = 32
S5_GROUP = 16
S5_STATE = 64
S5_LANES = S5_GROUPS * S5_STATE
IN_COLS = 7168
SUBLANES = 8
VMEM_LIMIT = 56 * 1024 * 1024
HIGHEST = lax.Precision.HIGHEST
MESH = pl.DeviceIdType.MESH

ADAM_LR, ADAM_B1, ADAM_B2, ADAM_EPS, ADAM_WD, ADAM_STEP = 0.001, 0.9, 0.999, 1e-08, 0.01, 10

BIG = ("w_in", "w_o_hg", "w_glu", "w_o_s5", "w_out", "w_ple", "w_ple_gate")
BIG_SHAPE = {"w_in": (1024, 7168), "w_o_hg": (1024, 1024), "w_glu": (512, 1024), "w_o_s5": (512, 1024),
             "w_out": (1024, 1024), "w_ple": (256, 1024), "w_ple_gate": (1024, 1024)}
BIG_COL_SHARDED = ("w_in", "w_glu", "w_o_s5", "w_ple")
SMALL = ("norm_g", "hg_lb", "hg_norm_g", "s5_a_re", "s5_a_im", "s5_log_dt", "s5_b_re", "s5_b_im", "s5_c_re",
         "s5_c_im", "s5_d", "b_glu", "ple_norm_g", "final_norm_g")
SMALL_SHAPE = {"norm_g": (1, 1024), "hg_lb": (2, 1024), "hg_norm_g": (1, 1024), "s5_a_re": (1, 32, 64),
               "s5_a_im": (1, 32, 64), "s5_log_dt": (1, 32), "s5_b_re": (1, 32, 64, 16), "s5_b_im": (1, 32, 64, 16),
               "s5_c_re": (1, 32, 16, 64), "s5_c_im": (1, 32, 16, 64), "s5_d": (1, 32, 16), "b_glu": (1, 1024),
               "ple_norm_g": (1, 1024), "final_norm_g": (1024,)}
WEIGHTS = ("norm_g", "w_in", "hg_lb", "hg_norm_g", "w_o_hg", "s5_a_re", "s5_a_im", "s5_log_dt", "s5_b_re", "s5_b_im",
           "s5_c_re", "s5_c_im", "s5_d", "w_glu", "b_glu", "w_o_s5", "w_out", "ple_norm_g", "w_ple", "w_ple_gate",
           "final_norm_g")
N_CHIPS = 4
N_DEV = 8
PACK_W = 1024
SHARD_ROWS = sum(BIG_SHAPE[n][0] * BIG_SHAPE[n][1] for n in BIG) // (N_CHIPS * PACK_W)
HALF_ROWS = SHARD_ROWS // 2
SMALL_ROWS = 144


def _params(*sem):
    return pltpu.CompilerParams(dimension_semantics=sem, vmem_limit_bytes=VMEM_LIMIT)


def _sig(x):
    return 1.0 / (1.0 + jnp.exp(-x))


def _dsilu(z, s):
    return s * (1.0 + z * (1.0 - s))


def _mx(x):
    return x.astype(MXU_DTYPE)


def _dot(a, b, dims=(((1,), (0,)), ((), ()))):
    return lax.dot_general(_mx(a), _mx(b), dims, preferred_element_type=F32)


_NT = (((1,), (1,)), ((), ()))
_TN = (((0,), (0,)), ((), ()))


def _dot32(a, b):
    return jnp.dot(a, b, precision=HIGHEST, preferred_element_type=F32)


def _rms_bwd(dy, x, g):
    r = lax.rsqrt(jnp.mean(x * x, axis=-1, keepdims=True) + NORM_EPS)
    t = dy * g
    dx = r * t - x * (r * r * r) * jnp.mean(t * x, axis=-1, keepdims=True)
    return dx, jnp.sum(dy * x * r, axis=0, keepdims=True)


def _rowwise(name, fn, n_rows_total, tm, rows, consts, outs, accs=(), alias=None):
    n_r, n_c, n_o, n_a = len(rows), len(consts), len(outs), len(accs)

    def body(*refs):
        row_refs = refs[:n_r]
        const_refs = refs[n_r:n_r + n_c]
        pos = n_r + n_c + (1 if alias is not None else 0)
        out_refs = refs[pos:pos + n_o]
        acc_refs = refs[pos + n_o:pos + n_o + n_a]
        res = fn(*[r[...] for r in row_refs], *[r[...] for r in const_refs])
        for r, v in zip(out_refs, res[:n_o]):
            r[...] = v.astype(r.dtype)
        if n_a:
            @pl.when(pl.program_id(0) == 0)
            def _():
                for r in acc_refs:
                    r[...] = jnp.zeros_like(r)
            for r, v in zip(acc_refs, res[n_o:]):
                r[...] += v

    in_specs = [pl.BlockSpec((tm, w), functools.partial(lambda i, cb: (i, cb), cb=cb)) for (_, w, cb) in rows]
    in_specs += [pl.BlockSpec(c.shape, lambda i: (0, 0)) for c in consts]
    args = [a for (a, _, _) in rows] + list(consts)
    out_shape, out_specs = [], []
    for o in outs:
        w, dt = o[0], o[1]
        cb, total = (o[2], o[3]) if len(o) == 4 else (0, w)
        out_shape.append(jax.ShapeDtypeStruct((n_rows_total, total), dt))
        out_specs.append(pl.BlockSpec((tm, w), functools.partial(lambda i, cb: (i, cb), cb=cb)))
    io_alias = {}
    if alias is not None:
        in_specs.append(pl.BlockSpec(memory_space=pl.ANY))
        args.append(alias[0])
        io_alias = {len(args) - 1: alias[1]}
    for (r, w) in accs:
        out_shape.append(jax.ShapeDtypeStruct((r, w), F32))
        out_specs.append(pl.BlockSpec((r, w), lambda i: (0, 0)))
    res = pl.pallas_call(body, name=name, grid=(n_rows_total // tm,), in_specs=in_specs, out_specs=out_specs,
                         out_shape=out_shape, input_output_aliases=io_alias,
                         compiler_params=_params("arbitrary"))(*args)
    return res


class _Riding(NamedTuple):
    ins: tuple
    outs: tuple
    n_sems: int
    start: Callable
    wait: Callable


_HBM = pl.BlockSpec(memory_space=pl.ANY)


def _ride(riding, refs, n_in, n_out, n_scratch, first, last):
    if riding is None:
        return refs[:n_in], refs[n_in:n_in + n_out], refs[n_in + n_out:]
    r_in, r_out = len(riding.ins), len(riding.outs)
    ins, rins = refs[:n_in], refs[n_in:n_in + r_in]
    pos = n_in + r_in
    outs, routs = refs[pos:pos + n_out], refs[pos + n_out:pos + n_out + r_out]
    pos += n_out + r_out
    scratch, (send_sems, recv_sems) = refs[pos:pos + n_scratch], refs[pos + n_scratch:]

    @pl.when(first)
    def _():
        riding.start(rins, routs, send_sems, recv_sems)

    @pl.when(last)
    def _():
        riding.wait(rins, routs, send_sems, recv_sems)

    return ins, outs, scratch


def _riding_call(riding, body, name, grid, in_specs, args, out_specs, out_shape, scratch, io_alias=None):
    if riding is not None:
        in_specs = list(in_specs) + [_HBM] * len(riding.ins)
        args = list(args) + list(riding.ins)
        out_specs = list(out_specs) + [_HBM] * len(riding.outs)
        out_shape = list(out_shape) + list(riding.outs)
        scratch = list(scratch) + [pltpu.SemaphoreType.DMA((riding.n_sems,))] * 2
    return pl.pallas_call(body, name=name, grid=grid, in_specs=in_specs, out_specs=out_specs, out_shape=out_shape,
                          scratch_shapes=scratch, input_output_aliases=io_alias or {},
                          compiler_params=_params("arbitrary", "arbitrary"))(*args)


def _mm_nn(name, a, b, tm, tn, riding=None):
    m, k = a.shape
    n = b.shape[1] if b.ndim == 2 else b.shape[0] * b.shape[2]
    grid = (n // tn, m // tm)

    def body(*refs):
        j, i = pl.program_id(0), pl.program_id(1)
        (a_ref, b_ref), (o_ref,), _ = _ride(riding, refs, 2, 1, 0, (j == 0) & (i == 0),
                                            (j == grid[0] - 1) & (i == grid[1] - 1))
        o_ref[...] = _dot(a_ref[...], b_ref[...])

    b_spec = (pl.BlockSpec((k, tn), lambda j, i: (0, j)) if b.ndim == 2
              else pl.BlockSpec((None, k, tn), lambda j, i: (j, 0, 0)))
    res = _riding_call(riding, body, name, grid, [pl.BlockSpec((tm, k), lambda j, i: (i, 0)), b_spec], [a, b],
                       [pl.BlockSpec((tm, tn), lambda j, i: (i, j))], [jax.ShapeDtypeStruct((m, n), F32)], [])
    return res[0] if riding is None else res


def _mm_nt(name, a, b, tm, tn):
    m, n = a.shape
    k = b.shape[0]
    steps = n // tn

    def body(a_ref, b_ref, o_ref, acc_ref):
        s = pl.program_id(1)

        @pl.when(s == 0)
        def _():
            acc_ref[...] = jnp.zeros_like(acc_ref)

        acc_ref[...] += _dot(a_ref[...], b_ref[...], _NT)

        @pl.when(s == steps - 1)
        def _():
            o_ref[...] = acc_ref[...]

    return pl.pallas_call(body, name=name, grid=(m // tm, steps),
                          in_specs=[pl.BlockSpec((tm, tn), lambda i, s: (i, s)),
                                    pl.BlockSpec((k, tn), lambda i, s: (0, s))],
                          out_specs=pl.BlockSpec((tm, k), lambda i, s: (i, 0)),
                          out_shape=jax.ShapeDtypeStruct((m, k), F32),
                          scratch_shapes=[pltpu.VMEM((tm, k), F32)],
                          compiler_params=_params("arbitrary", "arbitrary"))(a, b)


def _mm_nt_then(name, a, b, tm, tn, fn, rows, consts, outs, accs=(), alias=None, riding=None):
    m, n = a.shape
    k = b.shape[-2]
    steps = n // tn
    n_r, n_c, n_o, n_a = len(rows), len(consts), len(outs), len(accs)

    def body(*refs):
        a_ref, b_ref = refs[:2]
        row_refs = refs[2:2 + n_r]
        const_refs = refs[2 + n_r:2 + n_r + n_c]
        i, s = pl.program_id(0), pl.program_id(1)
        n_in = 2 + n_r + n_c + (1 if alias is not None else 0)
        _, outs_, (mm_ref,) = _ride(riding, refs, n_in, n_o + n_a, 1, (i == 0) & (s == 0),
                                    (i == m // tm - 1) & (s == steps - 1))
        out_refs, acc_refs = outs_[:n_o], outs_[n_o:]
        part = _dot(a_ref[...], b_ref[...], _NT)
        if steps > 1:
            @pl.when(s == 0)
            def _():
                mm_ref[...] = jnp.zeros_like(mm_ref)
            mm_ref[...] += part

        @pl.when(s == steps - 1)
        def _():
            res = fn(mm_ref[...] if steps > 1 else part, *[r[...] for r in row_refs], *[r[...] for r in const_refs])
            for r, v in zip(out_refs, res[:n_o]):
                r[...] = v.astype(r.dtype)
            if n_a:
                @pl.when(i == 0)
                def _():
                    for r in acc_refs:
                        r[...] = jnp.zeros_like(r)
                for r, v in zip(acc_refs, res[n_o:]):
                    r[...] += v

    b_spec = (pl.BlockSpec((k, tn), lambda i, s: (0, s)) if b.ndim == 2
              else pl.BlockSpec((None, k, tn), lambda i, s: (s, 0, 0)))
    in_specs = [pl.BlockSpec((tm, tn), lambda i, s: (i, s)), b_spec]
    in_specs += [pl.BlockSpec((tm, w), functools.partial(lambda i, s, cb: (i, cb), cb=cb)) for (_, w, cb) in rows]
    in_specs += [pl.BlockSpec(c.shape, lambda i, s: (0, 0)) for c in consts]
    args = [a, b] + [r[0] for r in rows] + list(consts)
    out_shape, out_specs = [], []
    for o in outs:
        w, dt = o[0], o[1]
        cb, total = (o[2], o[3]) if len(o) == 4 else (0, w)
        out_shape.append(jax.ShapeDtypeStruct((m, total), dt))
        out_specs.append(pl.BlockSpec((tm, w), functools.partial(lambda i, s, cb: (i, cb), cb=cb)))
    io_alias = {}
    if alias is not None:
        in_specs.append(pl.BlockSpec(memory_space=pl.ANY))
        args.append(alias[0])
        io_alias = {len(args) - 1: alias[1]}
    for (r, w) in accs:
        out_shape.append(jax.ShapeDtypeStruct((r, w), F32))
        out_specs.append(pl.BlockSpec((r, w), lambda i, s: (0, 0)))
    return _riding_call(riding, body, name, (m // tm, steps), in_specs, args, out_specs, out_shape,
                        [pltpu.VMEM((tm, k), F32)], io_alias)


def _mm_tn(name, a, b, tk, tn, col_shards=False, riding=None):
    t, k = a.shape
    n = b.shape[1]
    steps = t // tk

    def body(*refs):
        j, s = pl.program_id(0), pl.program_id(1)
        (a_ref, b_ref), (o_ref,), (acc_ref,) = _ride(riding, refs, 2, 1, 1, (j == 0) & (s == 0),
                                                     (j == n // tn - 1) & (s == steps - 1))

        @pl.when(s == 0)
        def _():
            acc_ref[...] = jnp.zeros_like(acc_ref)

        acc_ref[...] += _dot(a_ref[...], b_ref[...], _TN)

        @pl.when(s == steps - 1)
        def _():
            o_ref[...] = acc_ref[...]

    if col_shards:
        out_spec = pl.BlockSpec((None, k, tn), lambda j, s: (j, 0, 0))
        out_shape = jax.ShapeDtypeStruct((n // tn, k, tn), F32)
    else:
        out_spec = pl.BlockSpec((k, tn), lambda j, s: (0, j))
        out_shape = jax.ShapeDtypeStruct((k, n), F32)
    res = _riding_call(riding, body, name, (n // tn, steps),
                       [pl.BlockSpec((tk, k), lambda j, s: (s, 0)), pl.BlockSpec((tk, tn), lambda j, s: (s, j))],
                       [a, b], [out_spec], [out_shape], [pltpu.VMEM((k, tn), F32)])
    return res[0] if riding is None else res


def _hg_chunk_terms(q, f, lb):
    sig = _sig(f)
    fv = lb + (1.0 - lb) * sig
    kk = (1.0 - lb) * (1.0 - sig)
    row = lax.broadcasted_iota(jnp.int32, (HG_CHUNK, HG_CHUNK), 0)
    col = lax.broadcasted_iota(jnp.int32, (HG_CHUNK, HG_CHUNK), 1)
    b = _dot32((row >= col).astype(F32), jnp.log(fv))
    b_mid = b[HG_CHUNK // 2 - 1:HG_CHUNK // 2, :]
    b_last = b[HG_CHUNK - 1:HG_CHUNK, :]
    e_mid = jnp.exp(b - b_mid)
    e_mid_inv = jnp.exp(b_mid - b)
    e_b = jnp.exp(b)
    e_last = jnp.exp(b_last - b)
    return sig, fv, kk, row >= col, row <= col, q * e_mid, kk * e_mid_inv, e_mid, e_mid_inv, e_b, e_last, jnp.exp(b_last)


def _hgrn2_fwd(proj, hg_lb, hg_norm_g, t_len, tb):
    nck = tb // HG_CHUNK

    def body(p_ref, lb_ref, gn_ref, o_ref, act_ref, sp_ref, st_ref):
        @pl.when(pl.program_id(0) == 0)
        def _():
            st_ref[...] = jnp.zeros_like(st_ref)

        for c in range(nck):
            r = pl.ds(c * HG_CHUNK, HG_CHUNK)
            for h in range(HG_HEADS):
                hs = pl.ds(h * HG_DIM, HG_DIM)
                lb = _sig(lb_ref[0:1, hs] - lb_ref[1:2, hs])
                q = p_ref[r, pl.ds(h * HG_DIM, HG_DIM)]
                f = p_ref[r, pl.ds(1024 + h * HG_DIM, HG_DIM)]
                v = p_ref[r, pl.ds(2048 + h * HG_DIM, HG_DIM)]
                _, _, kk, causal, _, a, bm, _, _, e_b, e_last, dc = _hg_chunk_terms(q, f, lb)
                scores = jnp.where(causal, _dot(a, bm, _NT), 0.0)
                st = st_ref[h]
                o = _dot(scores, v) + _dot(q * e_b, st, _NT)
                sp_ref[h, c] = st
                st_ref[h] = dc * st + _dot(v, kk * e_last, _TN)
                o_ref[r, hs] = o

        for h in range(HG_HEADS):
            hs = pl.ds(h * HG_DIM, HG_DIM)
            o = o_ref[:, hs]
            rr = lax.rsqrt(jnp.mean(o * o, axis=-1, keepdims=True) + NORM_EPS)
            g = p_ref[:, pl.ds(3072 + h * HG_DIM, HG_DIM)]
            act_ref[:, hs] = (o * rr * gn_ref[:, hs] * (g * _sig(g))).astype(act_ref.dtype)

    nb = t_len // tb
    return pl.pallas_call(
        body, name="hgrn2_fwd", grid=(nb,),
        in_specs=[pl.BlockSpec((tb, 4096), lambda i: (i, 0)),
                  pl.BlockSpec((2, 1024), lambda i: (0, 0)),
                  pl.BlockSpec((1, 1024), lambda i: (0, 0))],
        out_specs=[pl.BlockSpec((tb, 1024), lambda i: (i, 0)),
                   pl.BlockSpec((tb, 1024), lambda i: (i, 0)),
                   pl.BlockSpec((HG_HEADS, nck, HG_DIM, HG_DIM), lambda i: (0, i, 0, 0))],
        out_shape=[jax.ShapeDtypeStruct((t_len, 1024), F32),
                   jax.ShapeDtypeStruct((t_len, 1024), MXU_DTYPE),
                   jax.ShapeDtypeStruct((HG_HEADS, t_len // HG_CHUNK, HG_DIM, HG_DIM), F32)],
        scratch_shapes=[pltpu.VMEM((HG_HEADS, HG_DIM, HG_DIM), F32)],
        compiler_params=_params("arbitrary"))(proj, hg_lb, hg_norm_g)


def _hgrn2_bwd(proj, d_o, s_prev, hg_lb, dproj, t_len, tb):
    nck = tb // HG_CHUNK
    nb = t_len // tb

    def body(p_ref, do_ref, sp_ref, lb_ref, _, dp_ref, dlb_ref, ds_ref, acc_ref):
        @pl.when(pl.program_id(0) == 0)
        def _():
            ds_ref[...] = jnp.zeros_like(ds_ref)
            acc_ref[...] = jnp.zeros_like(acc_ref)

        for c in reversed(range(nck)):
            r = pl.ds(c * HG_CHUNK, HG_CHUNK)
            for h in range(HG_HEADS):
                hs = pl.ds(h * HG_DIM, HG_DIM)
                lb = _sig(lb_ref[0:1, hs] - lb_ref[1:2, hs])
                q = p_ref[r, pl.ds(h * HG_DIM, HG_DIM)]
                f = p_ref[r, pl.ds(1024 + h * HG_DIM, HG_DIM)]
                v = p_ref[r, pl.ds(2048 + h * HG_DIM, HG_DIM)]
                do = do_ref[r, hs]
                sig, fv, kk, causal, anti, a, bm, e_mid, e_mid_inv, e_b, e_last, dc = _hg_chunk_terms(q, f, lb)
                qd = q * e_b
                kd = kk * e_last
                st = sp_ref[h, c]
                dst = ds_ref[h]
                scores = jnp.where(causal, _dot(a, bm, _NT), 0.0)
                dscores = jnp.where(causal, _dot(do, v, _NT), 0.0)
                dv = _dot(scores, do, _TN) + _dot(kd, dst, _NT)
                da = _dot(dscores, bm)
                dbm = _dot(dscores, a, _TN)
                dqd = _dot(do, st)
                dkd = _dot(v, dst)
                ddc = jnp.sum(dst * st, axis=0, keepdims=True)
                ds_ref[h] = _dot(do, qd, _TN) + dc * dst
                dq = da * e_mid + dqd * e_b
                dk = dbm * e_mid_inv + dkd * e_last
                db = da * a - dbm * bm + dqd * qd - dkd * kd
                extra = jnp.sum(dkd * kd, axis=0, keepdims=True) + ddc * dc
                dlogf = _dot32(anti.astype(F32), db) + extra
                dfv_k = dlogf / fv - dk
                dp_ref[r, pl.ds(h * HG_DIM, HG_DIM)] = dq
                dp_ref[r, pl.ds(1024 + h * HG_DIM, HG_DIM)] = dfv_k * (1.0 - lb) * sig * (1.0 - sig)
                dp_ref[r, pl.ds(2048 + h * HG_DIM, HG_DIM)] = dv
                acc_ref[:, hs] += jnp.sum(dfv_k * (1.0 - sig), axis=0, keepdims=True)

        @pl.when(pl.program_id(0) == nb - 1)
        def _():
            lb_all = _sig(lb_ref[0:1, :] - lb_ref[1:2, :])
            g0 = acc_ref[...] * lb_all * (1.0 - lb_all)
            dlb_ref[0:1, :] = g0
            dlb_ref[1:2, :] = -g0

    return pl.pallas_call(
        body, name="hgrn2_bwd", grid=(nb,),
        in_specs=[pl.BlockSpec((tb, 3072), lambda i: (nb - 1 - i, 0)),
                  pl.BlockSpec((tb, 1024), lambda i: (nb - 1 - i, 0)),
                  pl.BlockSpec((HG_HEADS, nck, HG_DIM, HG_DIM), lambda i: (0, nb - 1 - i, 0, 0)),
                  pl.BlockSpec((2, 1024), lambda i: (0, 0)),
                  pl.BlockSpec(memory_space=pl.ANY)],
        out_specs=[pl.BlockSpec((tb, 3072), lambda i: (nb - 1 - i, 0)),
                   pl.BlockSpec((2, 1024), lambda i: (0, 0))],
        out_shape=[jax.ShapeDtypeStruct((t_len, IN_COLS), F32), jax.ShapeDtypeStruct((2, 1024), F32)],
        scratch_shapes=[pltpu.VMEM((HG_HEADS, HG_DIM, HG_DIM), F32), pltpu.VMEM((1, 1024), F32)],
        input_output_aliases={4: 0},
        compiler_params=_params("arbitrary"))(proj, d_o, s_prev, hg_lb, dproj)


def _dot01(m01, x):
    m = m01.astype(MXU_DTYPE)
    hi = x.astype(MXU_DTYPE)
    r1 = x - hi.astype(F32)
    mid = r1.astype(MXU_DTYPE)
    lo = (r1 - mid.astype(F32)).astype(MXU_DTYPE)
    dot = lambda v: jnp.dot(m, v, preferred_element_type=F32)
    return dot(hi) + dot(mid) + dot(lo)


def _chunk_rows(x, offset, nck):
    return jnp.concatenate([jnp.broadcast_to(x[c * HG_CHUNK + offset:c * HG_CHUNK + offset + 1, :],
                                             (HG_CHUNK, x.shape[1])) for c in range(nck)], axis=0)


def _hg_block_terms(q, f, lb, tb):
    nck = tb // HG_CHUNK
    sig = _sig(f)
    fv = lb + (1.0 - lb) * sig
    kk = (1.0 - lb) * (1.0 - sig)
    row = lax.broadcasted_iota(jnp.int32, (tb, tb), 0)
    col = lax.broadcasted_iota(jnp.int32, (tb, tb), 1)
    same = jnp.right_shift(row, 6) == jnp.right_shift(col, 6)
    causal, anti = same & (row >= col), same & (row <= col)
    b = _dot01(causal, jnp.log(fv))
    b_mid, b_last = _chunk_rows(b, HG_CHUNK // 2 - 1, nck), _chunk_rows(b, HG_CHUNK - 1, nck)
    e_mid, e_mid_inv = jnp.exp(b - b_mid), jnp.exp(b_mid - b)
    e_b, e_last = jnp.exp(b), jnp.exp(b_last - b)
    dcs = [jnp.exp(b[c * HG_CHUNK + HG_CHUNK - 1:(c + 1) * HG_CHUNK, :]) for c in range(nck)]
    return sig, fv, kk, causal, anti, e_mid, e_mid_inv, e_b, e_last, dcs


def _hgrn2_fwd2(proj, hg_lb, hg_norm_g, t_len, tb):
    nck = tb // HG_CHUNK

    def body(p_ref, lb_ref, gn_ref, o_ref, act_ref, sp_ref, st_ref, a_s, bm_s, qd_s, kd_s, v_s):
        @pl.when(pl.program_id(0) == 0)
        def _():
            st_ref[...] = jnp.zeros_like(st_ref)

        lb = _sig(lb_ref[0:1, :] - lb_ref[1:2, :])
        q = p_ref[:, pl.ds(0, 1024)]
        _, _, kk, causal, _, e_mid, e_mid_inv, e_b, e_last, dcs = _hg_block_terms(q, p_ref[:, pl.ds(1024, 1024)],
                                                                                   lb, tb)
        a_s[...] = _mx(q * e_mid)
        bm_s[...] = _mx(kk * e_mid_inv)
        qd_s[...] = _mx(q * e_b)
        kd_s[...] = _mx(kk * e_last)
        v_s[...] = _mx(p_ref[:, pl.ds(2048, 1024)])
        for h in range(HG_HEADS):
            hs = pl.ds(h * HG_DIM, HG_DIM)
            scores = jnp.where(causal, _dot(a_s[:, hs], bm_s[:, hs], _NT), 0.0)
            o_ref[:, hs] = _dot(scores, v_s[:, hs])
        for h in range(HG_HEADS):
            hs = pl.ds(h * HG_DIM, HG_DIM)
            incs = [_dot(v_s[pl.ds(c * HG_CHUNK, HG_CHUNK), hs], kd_s[pl.ds(c * HG_CHUNK, HG_CHUNK), hs], _TN)
                    for c in range(nck)]
            st = st_ref[h]
            for c in range(nck):
                sp_ref[h, c] = st
                st = dcs[c][:, h * HG_DIM:(h + 1) * HG_DIM] * st + incs[c]
            st_ref[h] = st
        for h in range(HG_HEADS):
            hs = pl.ds(h * HG_DIM, HG_DIM)
            for c in range(nck):
                r = pl.ds(c * HG_CHUNK, HG_CHUNK)
                o_ref[r, hs] += _dot(qd_s[r, hs], sp_ref[h, c], _NT)
        for h in range(HG_HEADS):
            hs = pl.ds(h * HG_DIM, HG_DIM)
            o = o_ref[:, hs]
            rr = lax.rsqrt(jnp.mean(o * o, axis=-1, keepdims=True) + NORM_EPS)
            g = p_ref[:, pl.ds(3072 + h * HG_DIM, HG_DIM)]
            act_ref[:, hs] = (o * rr * gn_ref[:, hs] * (g * _sig(g))).astype(act_ref.dtype)

    nb = t_len // tb
    return pl.pallas_call(
        body, name="hgrn2_fwd", grid=(nb,),
        in_specs=[pl.BlockSpec((tb, 4096), lambda i: (i, 0)),
                  pl.BlockSpec((2, 1024), lambda i: (0, 0)),
                  pl.BlockSpec((1, 1024), lambda i: (0, 0))],
        out_specs=[pl.BlockSpec((tb, 1024), lambda i: (i, 0)),
                   pl.BlockSpec((tb, 1024), lambda i: (i, 0)),
                   pl.BlockSpec((HG_HEADS, nck, HG_DIM, HG_DIM), lambda i: (0, i, 0, 0))],
        out_shape=[jax.ShapeDtypeStruct((t_len, 1024), F32),
                   jax.ShapeDtypeStruct((t_len, 1024), MXU_DTYPE),
                   jax.ShapeDtypeStruct((HG_HEADS, t_len // HG_CHUNK, HG_DIM, HG_DIM), F32)],
        scratch_shapes=[pltpu.VMEM((HG_HEADS, HG_DIM, HG_DIM), F32)] + [pltpu.VMEM((tb, 1024), MXU_DTYPE)] * 5,
        compiler_params=_params("arbitrary"))(proj, hg_lb, hg_norm_g)


def _hgrn2_bwd2(proj, d_o, s_prev, hg_lb, dproj, t_len, tb):
    nck = tb // HG_CHUNK
    nb = t_len // tb

    def body(p_ref, do_ref, sp_ref, lb_ref, _, dp_ref, dlb_ref, ds_ref, acc_ref,
             a_s, bm_s, qd_s, kd_s, v_s, do_s, da_s, dbm_s, dqd_s, dkd_s, ex_s):
        @pl.when(pl.program_id(0) == 0)
        def _():
            ds_ref[...] = jnp.zeros_like(ds_ref)
            acc_ref[...] = jnp.zeros_like(acc_ref)

        lb = _sig(lb_ref[0:1, :] - lb_ref[1:2, :])
        q = p_ref[:, pl.ds(0, 1024)]
        sig, fv, kk, causal, anti, e_mid, e_mid_inv, e_b, e_last, dcs = _hg_block_terms(
            q, p_ref[:, pl.ds(1024, 1024)], lb, tb)
        a, bm, qd, kd = q * e_mid, kk * e_mid_inv, q * e_b, kk * e_last
        a_s[...] = _mx(a)
        bm_s[...] = _mx(bm)
        qd_s[...] = _mx(qd)
        kd_s[...] = _mx(kd)
        v_s[...] = _mx(p_ref[:, pl.ds(2048, 1024)])
        do_s[...] = _mx(do_ref[...])
        for h in range(HG_HEADS):
            hs = pl.ds(h * HG_DIM, HG_DIM)
            scores = jnp.where(causal, _dot(a_s[:, hs], bm_s[:, hs], _NT), 0.0)
            dscores = _mx(jnp.where(causal, _dot(do_s[:, hs], v_s[:, hs], _NT), 0.0))
            dp_ref[:, pl.ds(2048 + h * HG_DIM, HG_DIM)] = _dot(scores, do_s[:, hs], _TN)
            da_s[:, hs] = _dot(dscores, bm_s[:, hs])
            dbm_s[:, hs] = _dot(dscores, a_s[:, hs], _TN)
        for h in range(HG_HEADS):
            hs = pl.ds(h * HG_DIM, HG_DIM)
            ups = [_dot(do_s[pl.ds(c * HG_CHUNK, HG_CHUNK), hs], qd_s[pl.ds(c * HG_CHUNK, HG_CHUNK), hs], _TN)
                   for c in range(nck)]
            dst = ds_ref[h]
            for c in reversed(range(nck)):
                r = pl.ds(c * HG_CHUNK, HG_CHUNK)
                st = sp_ref[h, c]
                dc = dcs[c][:, h * HG_DIM:(h + 1) * HG_DIM]
                dp_ref[r, pl.ds(2048 + h * HG_DIM, HG_DIM)] += _dot(kd_s[r, hs], dst, _NT)
                dqd_s[r, hs] = _dot(do_s[r, hs], st)
                dkd_s[r, hs] = _dot(v_s[r, hs], dst)
                ex_s[c:c + 1, hs] = jnp.sum(dst * st, axis=0, keepdims=True) * dc
                dst = ups[c] + dc * dst
            ds_ref[h] = dst
        da, dbm, dqd, dkd = da_s[...], dbm_s[...], dqd_s[...], dkd_s[...]
        dq = da * e_mid + dqd * e_b
        dk = dbm * e_mid_inv + dkd * e_last
        db = da * a - dbm * bm + dqd * qd - dkd * kd
        dkk = dkd * kd
        extra = jnp.concatenate(
            [jnp.broadcast_to(jnp.sum(dkk[c * HG_CHUNK:(c + 1) * HG_CHUNK], axis=0, keepdims=True)
                              + ex_s[c:c + 1, :], (HG_CHUNK, 1024)) for c in range(nck)], axis=0)
        dlogf = _dot01(anti, db) + extra
        dfv_k = dlogf / fv - dk
        dp_ref[:, pl.ds(0, 1024)] = dq
        dp_ref[:, pl.ds(1024, 1024)] = dfv_k * (1.0 - lb) * sig * (1.0 - sig)
        acc_ref[...] += jnp.sum(dfv_k * (1.0 - sig), axis=0, keepdims=True)

        @pl.when(pl.program_id(0) == nb - 1)
        def _():
            g0 = acc_ref[...] * lb * (1.0 - lb)
            dlb_ref[0:1, :] = g0
            dlb_ref[1:2, :] = -g0

    return pl.pallas_call(
        body, name="hgrn2_bwd", grid=(nb,),
        in_specs=[pl.BlockSpec((tb, 3072), lambda i: (nb - 1 - i, 0)),
                  pl.BlockSpec((tb, 1024), lambda i: (nb - 1 - i, 0)),
                  pl.BlockSpec((HG_HEADS, nck, HG_DIM, HG_DIM), lambda i: (0, nb - 1 - i, 0, 0)),
                  pl.BlockSpec((2, 1024), lambda i: (0, 0)),
                  pl.BlockSpec(memory_space=pl.ANY)],
        out_specs=[pl.BlockSpec((tb, 3072), lambda i: (nb - 1 - i, 0)),
                   pl.BlockSpec((2, 1024), lambda i: (0, 0))],
        out_shape=[jax.ShapeDtypeStruct((t_len, IN_COLS), F32), jax.ShapeDtypeStruct((2, 1024), F32)],
        scratch_shapes=[pltpu.VMEM((HG_HEADS, HG_DIM, HG_DIM), F32), pltpu.VMEM((1, 1024), F32)]
                       + [pltpu.VMEM((tb, 1024), MXU_DTYPE)] * 6 + [pltpu.VMEM((tb, 1024), F32)] * 4
                       + [pltpu.VMEM((SUBLANES, 1024), F32)],
        input_output_aliases={4: 0},
        compiler_params=_params("arbitrary"))(proj, d_o, s_prev, hg_lb, dproj)


def _s5_prep(a_re, a_im, log_dt, b_re_t, b_im_t):
    def body(ar_ref, ai_ref, ldt_ref, br_ref, bi_ref, lam_ref, pr_ref, pi_ref, bbr_ref, bbi_ref):
        ar, ai = ar_ref[...], ai_ref[...]
        dt = jnp.exp(ldt_ref[...])
        mag = jnp.exp(ar * dt)
        lr, li = mag * jnp.cos(ai * dt), mag * jnp.sin(ai * dt)
        den = ar * ar + ai * ai
        nr = lr - 1.0
        sr = (nr * ar + li * ai) / den
        si = (li * ar - nr * ai) / den
        lam_ref[0:1, :] = lr
        lam_ref[1:2, :] = li
        cr, ci = lr, li
        for i in range(SUBLANES):
            pr_ref[i:i + 1, :] = cr
            pi_ref[i:i + 1, :] = ci
            cr, ci = cr * lr - ci * li, cr * li + ci * lr
        br, bi = br_ref[...], bi_ref[...]
        bbr_ref[...] = sr * br - si * bi
        bbi_ref[...] = sr * bi + si * br

    whole = pl.BlockSpec(memory_space=pltpu.VMEM)
    return pl.pallas_call(
        body, name="s5_prep", in_specs=[whole] * 5, out_specs=[whole] * 5,
        out_shape=[jax.ShapeDtypeStruct((2, S5_LANES), F32), jax.ShapeDtypeStruct((SUBLANES, S5_LANES), F32),
                   jax.ShapeDtypeStruct((SUBLANES, S5_LANES), F32), jax.ShapeDtypeStruct((S5_GROUP, S5_LANES), F32),
                   jax.ShapeDtypeStruct((S5_GROUP, S5_LANES), F32)])(a_re, a_im, log_dt, b_re_t, b_im_t)


def _s5_prep_bwd(a_re, a_im, log_dt, b_re_t, b_im_t, dlam, dbbr, dbbi):
    def body(ar_ref, ai_ref, ldt_ref, br_ref, bi_ref, dlam_ref, dbbr_ref, dbbi_ref,
             dar_ref, dai_ref, dldt_ref, dbr_ref, dbi_ref):
        ar, ai = ar_ref[...], ai_ref[...]
        dt = jnp.exp(ldt_ref[...])
        mag = jnp.exp(ar * dt)
        cs, sn = jnp.cos(ai * dt), jnp.sin(ai * dt)
        lr, li = mag * cs, mag * sn
        den = ar * ar + ai * ai
        nr = lr - 1.0
        sr = (nr * ar + li * ai) / den
        si = (li * ar - nr * ai) / den
        br, bi = br_ref[...], bi_ref[...]
        gbr, gbi = dbbr_ref[...], dbbi_ref[...]
        dbr_ref[...] = sr * gbr + si * gbi
        dbi_ref[...] = sr * gbi - si * gbr
        dsr = jnp.sum(gbr * br + gbi * bi, axis=0, keepdims=True)
        dsi = jnp.sum(gbi * br - gbr * bi, axis=0, keepdims=True)
        dnr = (dsr * ar - dsi * ai) / den
        dli = dlam_ref[1:2, :] + (dsr * ai + dsi * ar) / den
        dlr = dlam_ref[0:1, :] + dnr
        dden = -(dsr * sr + dsi * si) / den
        dar = (dsr * nr + dsi * li) / den + dden * 2.0 * ar
        dai = (dsr * li - dsi * nr) / den + dden * 2.0 * ai
        dmag = dlr * cs + dli * sn
        dth = mag * (dli * cs - dlr * sn)
        dar_ref[...] = dar + dmag * mag * dt
        dai_ref[...] = dai + dth * dt
        ddt = (dmag * mag * ar + dth * ai) * dt
        lane = lax.broadcasted_iota(jnp.int32, (S5_LANES, 128), 0) // S5_STATE
        grp = lax.broadcasted_iota(jnp.int32, (S5_LANES, 128), 1)
        dldt_ref[...] = _dot32(jnp.broadcast_to(ddt, (SUBLANES, S5_LANES)), (lane == grp).astype(F32))

    whole = pl.BlockSpec(memory_space=pltpu.VMEM)
    return pl.pallas_call(
        body, name="s5_prep_bwd", in_specs=[whole] * 8, out_specs=[whole] * 5,
        out_shape=[jax.ShapeDtypeStruct((1, S5_LANES), F32), jax.ShapeDtypeStruct((1, S5_LANES), F32),
                   jax.ShapeDtypeStruct((SUBLANES, 128), F32), jax.ShapeDtypeStruct((S5_GROUP, S5_LANES), F32),
                   jax.ShapeDtypeStruct((S5_GROUP, S5_LANES), F32)])(a_re, a_im, log_dt, b_re_t, b_im_t, dlam, dbbr,
                                                                      dbbi)


S5_LANE_CHUNK = 512


def _shift_rows(x, s, rowid):
    if s > 0:
        return jnp.where(rowid >= s, pltpu.roll(x, s, 0), 0.0)
    return jnp.where(rowid < SUBLANES + s, pltpu.roll(x, SUBLANES + s, 0), 0.0)


def _scan8(xr, xi, pr, pi, sign, rowid):
    for s, row in ((1, 0), (2, 1), (4, 3)):
        lr, li = pr[row:row + 1, :], pi[row:row + 1, :]
        sr, si = _shift_rows(xr, sign * s, rowid), _shift_rows(xi, sign * s, rowid)
        xr, xi = xr + lr * sr - li * si, xi + lr * si + li * sr
    return xr, xi


def _s5_fwd(proj, pw_re, pw_im, bbr_bd, bbi_bd, crt_bd, cit_bd, d_row, t_len, tb):
    ngrp = tb // SUBLANES

    def body(u_ref, pr_ref, pi_ref, bbr_ref, bbi_ref, crt_ref, cit_ref, d_ref,
             hr_ref, hi_ref, ypre_ref, ys_ref, cr_ref, ci_ref):
        @pl.when(pl.program_id(0) == 0)
        def _():
            cr_ref[...] = jnp.zeros_like(cr_ref)
            ci_ref[...] = jnp.zeros_like(ci_ref)

        u = u_ref[...]
        hr_ref[...] = _dot(u, bbr_ref[...])
        hi_ref[...] = _dot(u, bbi_ref[...])
        rowid = lax.broadcasted_iota(jnp.int32, (SUBLANES, S5_LANE_CHUNK), 0)
        for lc in range(S5_LANES // S5_LANE_CHUNK):
            ls = pl.ds(lc * S5_LANE_CHUNK, S5_LANE_CHUNK)
            pr, pi = pr_ref[:, ls], pi_ref[:, ls]

            def group(g, carry, ls=ls, pr=pr, pi=pi):
                cr, ci = carry
                r = pl.ds(pl.multiple_of(g * SUBLANES, SUBLANES), SUBLANES)
                xr, xi = _scan8(hr_ref[r, ls], hi_ref[r, ls], pr, pi, 1, rowid)
                xr, xi = xr + pr * cr - pi * ci, xi + pr * ci + pi * cr
                hr_ref[r, ls] = xr
                hi_ref[r, ls] = xi
                return xr[SUBLANES - 1:SUBLANES, :], xi[SUBLANES - 1:SUBLANES, :]

            cr, ci = lax.fori_loop(0, ngrp, group, (cr_ref[:, ls], ci_ref[:, ls]))
            cr_ref[:, ls] = cr
            ci_ref[:, ls] = ci
        y = _dot(hr_ref[...], crt_ref[...]) - _dot(hi_ref[...], cit_ref[...]) + d_ref[...] * u
        ypre_ref[...] = y
        ys_ref[...] = jax.nn.gelu(y, approximate=True).astype(ys_ref.dtype)

    whole = pl.BlockSpec(memory_space=pltpu.VMEM)
    return pl.pallas_call(
        body, name="s5_fwd", grid=(t_len // tb,),
        in_specs=[pl.BlockSpec((tb, S5_WIDTH), lambda i: (i, 4096 // S5_WIDTH))] + [whole] * 7,
        out_specs=[pl.BlockSpec((tb, S5_LANES), lambda i: (i, 0)), pl.BlockSpec((tb, S5_LANES), lambda i: (i, 0)),
                   pl.BlockSpec((tb, S5_WIDTH), lambda i: (i, 0)), pl.BlockSpec((tb, S5_WIDTH), lambda i: (i, 0))],
        out_shape=[jax.ShapeDtypeStruct((t_len, S5_LANES), F32), jax.ShapeDtypeStruct((t_len, S5_LANES), F32),
                   jax.ShapeDtypeStruct((t_len, S5_WIDTH), F32), jax.ShapeDtypeStruct((t_len, S5_WIDTH), MXU_DTYPE)],
        scratch_shapes=[pltpu.VMEM((1, S5_LANES), F32), pltpu.VMEM((1, S5_LANES), F32)],
        compiler_params=_params("arbitrary"))(proj, pw_re, pw_im, bbr_bd, bbi_bd, crt_bd, cit_bd, d_row)


def _dgelu(x):
    c, a = 0.7978845608028654, 0.044715
    th = jnp.tanh(c * (x + a * x * x * x))
    return 0.5 * (1.0 + th) + 0.5 * x * (1.0 - th * th) * c * (1.0 + 3.0 * a * x * x)


def _s5_bwd(dgelu, y_pre, proj, h_re, h_im, pwr_re, pwr_im, bbr_bd, bbi_bd, cr_bd, ci_bd, d_row, dproj, t_len, tb):
    ngrp = tb // SUBLANES
    nb = t_len // tb

    def body(dg_ref, yp_ref, u_ref, hr_ref, hi_ref, pr_ref, pi_ref, bbr_ref, bbi_ref, cr_ref, ci_ref, d_ref, _,
             du_ref, dbbr_ref, dbbi_ref, dcr_ref, dci_ref, dd_ref, dlam_ref,
             gr_ref, gi_ref, car_ref, cai_ref, abr_ref, abi_ref, acr_ref, aci_ref, ad_ref, alr_ref, ali_ref, sem):
        @pl.when(pl.program_id(0) == 0)
        def _():
            for ref in (car_ref, cai_ref, abr_ref, abi_ref, acr_ref, aci_ref, ad_ref, alr_ref, ali_ref):
                ref[...] = jnp.zeros_like(ref)

        u = u_ref[...]
        dy = dg_ref[...] * _dgelu(yp_ref[...])
        gr_ref[...] = _dot(dy, cr_ref[...])
        gi_ref[...] = -_dot(dy, ci_ref[...])
        rowid = lax.broadcasted_iota(jnp.int32, (SUBLANES, S5_LANE_CHUNK), 0)
        for lc in range(S5_LANES // S5_LANE_CHUNK):
            ls = pl.ds(lc * S5_LANE_CHUNK, S5_LANE_CHUNK)
            pr, pi = pr_ref[:, ls], pi_ref[:, ls]
            fwd_rows_r = jnp.concatenate([pr[7:8], pr[6:7], pr[6:7], pr[4:5]], axis=0)
            fwd_rows_i = jnp.concatenate([pi[7:8], pi[6:7], pi[6:7], pi[4:5]], axis=0)

            def group(j, carry, ls=ls, pr=pr, pi=pi, fr=fwd_rows_r, fi=fwd_rows_i):
                cr, ci, slr, sli = carry
                g = ngrp - 1 - j
                r = pl.ds(pl.multiple_of(g * SUBLANES, SUBLANES), SUBLANES)
                xr, xi = _scan8(gr_ref[r, ls], gi_ref[r, ls], fr, fi, -1, rowid)
                xr, xi = xr + pr * cr - pi * ci, xi + pr * ci + pi * cr
                gr_ref[r, ls] = xr
                gi_ref[r, ls] = xi
                nr = jnp.where(rowid == SUBLANES - 1, cr, pltpu.roll(xr, SUBLANES - 1, 0))
                ni = jnp.where(rowid == SUBLANES - 1, ci, pltpu.roll(xi, SUBLANES - 1, 0))
                hr, hi = hr_ref[r, ls], hi_ref[r, ls]
                slr = slr + nr * hr + ni * hi
                sli = sli + ni * hr - nr * hi
                return xr[0:1, :], xi[0:1, :], slr, sli

            zero = jnp.zeros((SUBLANES, S5_LANE_CHUNK), F32)
            cr, ci, slr, sli = lax.fori_loop(0, ngrp, group, (car_ref[:, ls], cai_ref[:, ls], zero, zero))
            car_ref[:, ls] = cr
            cai_ref[:, ls] = ci
            alr_ref[:, ls] += jnp.sum(slr, axis=0, keepdims=True)
            ali_ref[:, ls] += jnp.sum(sli, axis=0, keepdims=True)
        gr, gi = gr_ref[...], gi_ref[...]
        du_ref[...] = _dot(gr, bbr_ref[...], _NT) + _dot(gi, bbi_ref[...], _NT) + d_ref[...] * dy
        abr_ref[...] += _dot(u, gr, _TN)
        abi_ref[...] += _dot(u, gi, _TN)
        acr_ref[...] += _dot(hr_ref[...], dy, _TN)
        aci_ref[...] -= _dot(hi_ref[...], dy, _TN)
        ad_ref[...] += jnp.sum(dy * u, axis=0, keepdims=True)

        @pl.when(pl.program_id(0) == nb - 1)
        def _():
            dd_ref[...] = ad_ref[...]
            dlam_ref[0:1, :] = alr_ref[...]
            dlam_ref[1:2, :] = ali_ref[...]
            copies = [pltpu.make_async_copy(s, d, sem.at[k]) for k, (s, d) in enumerate(
                ((abr_ref, dbbr_ref), (abi_ref, dbbi_ref), (acr_ref, dcr_ref), (aci_ref, dci_ref)))]
            for cp in copies:
                cp.start()
            for cp in copies:
                cp.wait()

    whole = pl.BlockSpec(memory_space=pltpu.VMEM)
    hbm = pl.BlockSpec(memory_space=pl.ANY)
    rev = lambda i: (nb - 1 - i, 0)
    return pl.pallas_call(
        body, name="s5_bwd", grid=(nb,),
        in_specs=[pl.BlockSpec((tb, S5_WIDTH), rev), pl.BlockSpec((tb, S5_WIDTH), rev),
                  pl.BlockSpec((tb, S5_WIDTH), lambda i: (nb - 1 - i, 4096 // S5_WIDTH)),
                  pl.BlockSpec((tb, S5_LANES), rev), pl.BlockSpec((tb, S5_LANES), rev)] + [whole] * 7 + [hbm],
        out_specs=[pl.BlockSpec((tb, S5_WIDTH), lambda i: (nb - 1 - i, 4096 // S5_WIDTH)), hbm, hbm, hbm, hbm,
                   pl.BlockSpec((1, S5_WIDTH), lambda i: (0, 0)), pl.BlockSpec((2, S5_LANES), lambda i: (0, 0))],
        out_shape=[jax.ShapeDtypeStruct((t_len, IN_COLS), F32),
                   jax.ShapeDtypeStruct((S5_WIDTH, S5_LANES), F32), jax.ShapeDtypeStruct((S5_WIDTH, S5_LANES), F32),
                   jax.ShapeDtypeStruct((S5_LANES, S5_WIDTH), F32), jax.ShapeDtypeStruct((S5_LANES, S5_WIDTH), F32),
                   jax.ShapeDtypeStruct((1, S5_WIDTH), F32), jax.ShapeDtypeStruct((2, S5_LANES), F32)],
        scratch_shapes=[pltpu.VMEM((tb, S5_LANES), F32), pltpu.VMEM((tb, S5_LANES), F32),
                        pltpu.VMEM((1, S5_LANES), F32), pltpu.VMEM((1, S5_LANES), F32),
                        pltpu.VMEM((S5_WIDTH, S5_LANES), F32), pltpu.VMEM((S5_WIDTH, S5_LANES), F32),
                        pltpu.VMEM((S5_LANES, S5_WIDTH), F32), pltpu.VMEM((S5_LANES, S5_WIDTH), F32),
                        pltpu.VMEM((1, S5_WIDTH), F32), pltpu.VMEM((1, S5_LANES), F32),
                        pltpu.VMEM((1, S5_LANES), F32), pltpu.SemaphoreType.DMA((4,))],
        input_output_aliases={12: 0},
        compiler_params=_params("arbitrary"))(dgelu, y_pre, proj, h_re, h_im, pwr_re, pwr_im, bbr_bd, bbi_bd, cr_bd,
                                              ci_bd, d_row, dproj)


S5_BLOCKS = 4
S5_BW = S5_WIDTH // S5_BLOCKS
S5_BL = S5_LANES // S5_BLOCKS
S5_LANE_BLOCKS = S5_LANES // 128
S5_SCAN_BLOCKS = 4


def _s5_powers(a_re, a_im, log_dt, b_re_t, b_im_t, seg):
    def body(ar_ref, ai_ref, ldt_ref, br_ref, bi_ref, pr_ref, pi_ref, bbr_ref, bbi_ref):
        ar, ai = ar_ref[...], ai_ref[...]
        dt = jnp.exp(ldt_ref[...])
        mag = jnp.exp(ar * dt)
        lr, li = mag * jnp.cos(ai * dt), mag * jnp.sin(ai * dt)
        den = ar * ar + ai * ai
        nr = lr - 1.0
        sr = (nr * ar + li * ai) / den
        si = (li * ar - nr * ai) / den
        cr, ci = lr, li
        for i in range(seg):
            pr_ref[i:i + 1, :] = cr
            pi_ref[i:i + 1, :] = ci
            cr, ci = cr * lr - ci * li, cr * li + ci * lr
        br, bi = br_ref[...], bi_ref[...]
        bbr_ref[...] = sr * br - si * bi
        bbi_ref[...] = sr * bi + si * br

    whole = pl.BlockSpec(memory_space=pltpu.VMEM)
    return pl.pallas_call(
        body, name="s5_prep", in_specs=[whole] * 5, out_specs=[whole] * 4,
        out_shape=[jax.ShapeDtypeStruct((seg, S5_LANES), F32), jax.ShapeDtypeStruct((seg, S5_LANES), F32),
                   jax.ShapeDtypeStruct((S5_GROUP, S5_LANES), F32),
                   jax.ShapeDtypeStruct((S5_GROUP, S5_LANES), F32)])(a_re, a_im, log_dt, b_re_t, b_im_t)


def _scan_tables(pw_re, pw_im, reverse):
    seg = pw_re.shape[0]
    if reverse:
        pw_re, pw_im = pw_re[::-1], -pw_im[::-1]
        one, full = seg - 1, 0
    else:
        one, full = 0, seg - 1
    rows = jnp.stack([pw_re[one], pw_im[one], pw_re[full], pw_im[full]])
    wide = lambda t: jnp.broadcast_to(t[:, None, :], (seg, SUBLANES, S5_LANES))
    return rows, wide(pw_re), wide(pw_im)


def _lanes(j):
    return pl.ds(j * 128, 128)


def _segment_scan(xr_ref, xi_ref, lam_ref, car_ref, cai_ref, cn_r, cn_i, blocks, seg, reverse):
    shape = (SUBLANES, 128)
    lrs = [jnp.broadcast_to(lam_ref[0:1, _lanes(j)], shape) for j in blocks]
    lis = [jnp.broadcast_to(lam_ref[1:2, _lanes(j)], shape) for j in blocks]

    def step(k, carry):
        idx = pl.ds(seg - 1 - k if reverse else k, SUBLANES, stride=seg)
        out = []
        for n, j in enumerate(blocks):
            cr, ci = carry[2 * n], carry[2 * n + 1]
            nr = lrs[n] * cr - lis[n] * ci + xr_ref[j, idx, :]
            ni = lrs[n] * ci + lis[n] * cr + xi_ref[j, idx, :]
            xr_ref[j, idx, :] = nr
            xi_ref[j, idx, :] = ni
            out += [nr, ni]
        return tuple(out)

    zero = jnp.zeros(shape, F32)
    fin = lax.fori_loop(0, seg, step, (zero,) * (2 * len(blocks)), unroll=2)
    for n, j in enumerate(blocks):
        ls = _lanes(j)
        fr, fi = fin[2 * n], fin[2 * n + 1]
        sr, si = lam_ref[2:3, ls], lam_ref[3:4, ls]
        pr, pi = car_ref[:, ls], cai_ref[:, ls]
        for s in (reversed(range(SUBLANES)) if reverse else range(SUBLANES)):
            cn_r[s:s + 1, ls] = pr
            cn_i[s:s + 1, ls] = pi
            pr, pi = fr[s:s + 1, :] + sr * pr - si * pi, fi[s:s + 1, :] + sr * pi + si * pr
        car_ref[:, ls] = pr
        cai_ref[:, ls] = pi


def _s5_fwd2(proj, lam_rows, p3_re, p3_im, bbr4, bbi4, crt4, cit4, d_row, t_len, tb):
    seg = tb // SUBLANES

    def body(u_ref, lam_ref, p3r_ref, p3i_ref, bbr_ref, bbi_ref, crt_ref, cit_ref, d_ref,
             hr_ref, hi_ref, ypre_ref, ys_ref, car_ref, cai_ref, cn_r, cn_i):
        @pl.when(pl.program_id(0) == 0)
        def _():
            car_ref[...] = jnp.zeros_like(car_ref)
            cai_ref[...] = jnp.zeros_like(cai_ref)

        u = u_ref[...]
        for i in range(S5_BLOCKS):
            ui = u[:, i * S5_BW:(i + 1) * S5_BW]
            xr, xi = _dot(ui, bbr_ref[i]), _dot(ui, bbi_ref[i])
            for jj in range(S5_BL // 128):
                hr_ref[i * (S5_BL // 128) + jj] = xr[:, jj * 128:(jj + 1) * 128]
                hi_ref[i * (S5_BL // 128) + jj] = xi[:, jj * 128:(jj + 1) * 128]
        for lc in range(S5_LANE_BLOCKS // S5_SCAN_BLOCKS):
            blocks = range(lc * S5_SCAN_BLOCKS, (lc + 1) * S5_SCAN_BLOCKS)
            _segment_scan(hr_ref, hi_ref, lam_ref, car_ref, cai_ref, cn_r, cn_i, blocks, seg, False)
            crs = [cn_r[:, _lanes(j)] for j in blocks]
            cis = [cn_i[:, _lanes(j)] for j in blocks]

            def fix(t, carry, blocks=blocks, crs=crs, cis=cis):
                idx = pl.ds(t, SUBLANES, stride=seg)
                for n, j in enumerate(blocks):
                    pr, pi = p3r_ref[t, :, _lanes(j)], p3i_ref[t, :, _lanes(j)]
                    hr_ref[j, idx, :] += pr * crs[n] - pi * cis[n]
                    hi_ref[j, idx, :] += pr * cis[n] + pi * crs[n]
                return carry

            lax.fori_loop(0, seg, fix, 0, unroll=2)
        for i in range(S5_BLOCKS):
            ws = pl.ds(i * S5_BW, S5_BW)
            js = range(i * (S5_BL // 128), (i + 1) * (S5_BL // 128))
            hr = jnp.concatenate([hr_ref[j] for j in js], axis=1)
            hi = jnp.concatenate([hi_ref[j] for j in js], axis=1)
            y = _dot(hr, crt_ref[i]) - _dot(hi, cit_ref[i]) + d_ref[:, ws] * u[:, i * S5_BW:(i + 1) * S5_BW]
            ypre_ref[:, ws] = y
            ys_ref[:, ws] = jax.nn.gelu(y, approximate=True).astype(ys_ref.dtype)

    whole = pl.BlockSpec(memory_space=pltpu.VMEM)
    h_spec = pl.BlockSpec((S5_LANE_BLOCKS, tb, 128), lambda i: (0, i, 0))
    return pl.pallas_call(
        body, name="s5_fwd", grid=(t_len // tb,),
        in_specs=[pl.BlockSpec((tb, S5_WIDTH), lambda i: (i, 4096 // S5_WIDTH))] + [whole] * 8,
        out_specs=[h_spec, h_spec,
                   pl.BlockSpec((tb, S5_WIDTH), lambda i: (i, 0)), pl.BlockSpec((tb, S5_WIDTH), lambda i: (i, 0))],
        out_shape=[jax.ShapeDtypeStruct((S5_LANE_BLOCKS, t_len, 128), F32),
                   jax.ShapeDtypeStruct((S5_LANE_BLOCKS, t_len, 128), F32),
                   jax.ShapeDtypeStruct((t_len, S5_WIDTH), F32), jax.ShapeDtypeStruct((t_len, S5_WIDTH), MXU_DTYPE)],
        scratch_shapes=[pltpu.VMEM((1, S5_LANES), F32), pltpu.VMEM((1, S5_LANES), F32),
                        pltpu.VMEM((SUBLANES, S5_LANES), F32), pltpu.VMEM((SUBLANES, S5_LANES), F32)],
        compiler_params=_params("arbitrary"))(proj, lam_rows, p3_re, p3_im, bbr4, bbi4, crt4, cit4, d_row)


def _s5_bwd2(dgelu, y_pre, proj, h_re, h_im, lam_rows, p3_re, p3_im, bbr4, bbi4, cr4, ci4, d_row, dproj, t_len, tb):
    seg = tb // SUBLANES
    nb = t_len // tb

    def body(dg_ref, yp_ref, u_ref, hr_ref, hi_ref, lam_ref, p3r_ref, p3i_ref, bbr_ref, bbi_ref, cr_ref, ci_ref,
             d_ref, _, du_ref, dbbr_ref, dbbi_ref, dcr_ref, dci_ref, dd_ref, dlam_ref,
             gr_ref, gi_ref, car_ref, cai_ref, cn_r, cn_i):
        @pl.when(pl.program_id(0) == 0)
        def _():
            for ref in (car_ref, cai_ref, dbbr_ref, dbbi_ref, dcr_ref, dci_ref, dd_ref, dlam_ref):
                ref[...] = jnp.zeros_like(ref)

        u = u_ref[...]
        dy = dg_ref[...] * _dgelu(yp_ref[...])
        nlb = S5_BL // 128
        for i in range(S5_BLOCKS):
            dyi = dy[:, i * S5_BW:(i + 1) * S5_BW]
            xr, xi = _dot(dyi, cr_ref[i]), -_dot(dyi, ci_ref[i])
            for jj in range(nlb):
                gr_ref[i * nlb + jj] = xr[:, jj * 128:(jj + 1) * 128]
                gi_ref[i * nlb + jj] = xi[:, jj * 128:(jj + 1) * 128]
        for lc in range(S5_LANE_BLOCKS // S5_SCAN_BLOCKS):
            blocks = range(lc * S5_SCAN_BLOCKS, (lc + 1) * S5_SCAN_BLOCKS)
            _segment_scan(gr_ref, gi_ref, lam_ref, car_ref, cai_ref, cn_r, cn_i, blocks, seg, True)
            crs = [cn_r[:, _lanes(j)] for j in blocks]
            cis = [cn_i[:, _lanes(j)] for j in blocks]

            def fix(k, carry, blocks=blocks, crs=crs, cis=cis):
                t = seg - 1 - k
                idx = pl.ds(t, SUBLANES, stride=seg)
                out = []
                for n, j in enumerate(blocks):
                    nr, ni, slr, sli = carry[4 * n:4 * n + 4]
                    pr, pi = p3r_ref[t, :, _lanes(j)], p3i_ref[t, :, _lanes(j)]
                    g_r = gr_ref[j, idx, :] + pr * crs[n] - pi * cis[n]
                    g_i = gi_ref[j, idx, :] + pr * cis[n] + pi * crs[n]
                    gr_ref[j, idx, :] = g_r
                    gi_ref[j, idx, :] = g_i
                    hr, hi = hr_ref[j, idx, :], hi_ref[j, idx, :]
                    out += [g_r, g_i, slr + nr * hr + ni * hi, sli + ni * hr - nr * hi]
                return tuple(out)

            zero = jnp.zeros((SUBLANES, 128), F32)
            init = []
            for n in range(len(blocks)):
                init += [crs[n], cis[n], zero, zero]
            fin = lax.fori_loop(0, seg, fix, tuple(init), unroll=2)
            for n, j in enumerate(blocks):
                dlam_ref[0:1, _lanes(j)] += jnp.sum(fin[4 * n + 2], axis=0, keepdims=True)
                dlam_ref[1:2, _lanes(j)] += jnp.sum(fin[4 * n + 3], axis=0, keepdims=True)
        for i in range(S5_BLOCKS):
            ws = pl.ds(i * S5_BW, S5_BW)
            js = range(i * nlb, (i + 1) * nlb)
            ui, dyi = u[:, i * S5_BW:(i + 1) * S5_BW], dy[:, i * S5_BW:(i + 1) * S5_BW]
            gr = jnp.concatenate([gr_ref[j] for j in js], axis=1)
            gi = jnp.concatenate([gi_ref[j] for j in js], axis=1)
            du_ref[:, ws] = _dot(gr, bbr_ref[i], _NT) + _dot(gi, bbi_ref[i], _NT) + d_ref[:, ws] * dyi
            dbbr_ref[i] += _dot(ui, gr, _TN)
            dbbi_ref[i] += _dot(ui, gi, _TN)
            dcr_ref[i] += _dot(jnp.concatenate([hr_ref[j] for j in js], axis=1), dyi, _TN)
            dci_ref[i] -= _dot(jnp.concatenate([hi_ref[j] for j in js], axis=1), dyi, _TN)
        dd_ref[...] += jnp.sum(dy * u, axis=0, keepdims=True)

    whole = pl.BlockSpec(memory_space=pltpu.VMEM)
    rev = lambda i: (nb - 1 - i, 0)
    const3 = lambda i: (0, 0, 0)
    h_spec = pl.BlockSpec((S5_LANE_BLOCKS, tb, 128), lambda i: (0, nb - 1 - i, 0))
    return pl.pallas_call(
        body, name="s5_bwd", grid=(nb,),
        in_specs=[pl.BlockSpec((tb, S5_WIDTH), rev), pl.BlockSpec((tb, S5_WIDTH), rev),
                  pl.BlockSpec((tb, S5_WIDTH), lambda i: (nb - 1 - i, 4096 // S5_WIDTH)),
                  h_spec, h_spec] + [whole] * 8
                 + [pl.BlockSpec(memory_space=pl.ANY)],
        out_specs=[pl.BlockSpec((tb, S5_WIDTH), lambda i: (nb - 1 - i, 4096 // S5_WIDTH)),
                   pl.BlockSpec((S5_BLOCKS, S5_BW, S5_BL), const3), pl.BlockSpec((S5_BLOCKS, S5_BW, S5_BL), const3),
                   pl.BlockSpec((S5_BLOCKS, S5_BL, S5_BW), const3), pl.BlockSpec((S5_BLOCKS, S5_BL, S5_BW), const3),
                   pl.BlockSpec((1, S5_WIDTH), lambda i: (0, 0)), pl.BlockSpec((2, S5_LANES), lambda i: (0, 0))],
        out_shape=[jax.ShapeDtypeStruct((t_len, IN_COLS), F32),
                   jax.ShapeDtypeStruct((S5_BLOCKS, S5_BW, S5_BL), F32),
                   jax.ShapeDtypeStruct((S5_BLOCKS, S5_BW, S5_BL), F32),
                   jax.ShapeDtypeStruct((S5_BLOCKS, S5_BL, S5_BW), F32),
                   jax.ShapeDtypeStruct((S5_BLOCKS, S5_BL, S5_BW), F32),
                   jax.ShapeDtypeStruct((1, S5_WIDTH), F32), jax.ShapeDtypeStruct((2, S5_LANES), F32)],
        scratch_shapes=[pltpu.VMEM((S5_LANE_BLOCKS, tb, 128), F32), pltpu.VMEM((S5_LANE_BLOCKS, tb, 128), F32),
                        pltpu.VMEM((1, S5_LANES), F32), pltpu.VMEM((1, S5_LANES), F32),
                        pltpu.VMEM((SUBLANES, S5_LANES), F32), pltpu.VMEM((SUBLANES, S5_LANES), F32)],
        input_output_aliases={13: 0},
        compiler_params=_params("arbitrary"))(dgelu, y_pre, proj, h_re, h_im, lam_rows, p3_re, p3_im, bbr4, bbi4,
                                              cr4, ci4, d_row, dproj)


def _to_segment_order(v, stage_ref, out_ref, seg):
    nbl = v.shape[1] // 128
    for b in range(nbl):
        stage_ref[b] = v[:, b * 128:(b + 1) * 128]

    def body(t, carry):
        rows = pl.ds(pl.multiple_of(t * SUBLANES, SUBLANES), SUBLANES)
        for b in range(nbl):
            out_ref[rows, _lanes(b)] = stage_ref[b, pl.ds(t, SUBLANES, stride=seg), :]
        return carry

    lax.fori_loop(0, seg, body, 0)


def _from_segment_order(v, stage_ref, out_ref, seg):
    nbl = v.shape[1] // 128
    for b in range(nbl):
        stage_ref[b] = v[:, b * 128:(b + 1) * 128]
    for s in range(SUBLANES):
        def body(k, carry, s=s):
            rows = pl.ds(pl.multiple_of(s * seg + k * SUBLANES, SUBLANES), SUBLANES)
            for b in range(nbl):
                out_ref[rows, _lanes(b)] = stage_ref[b, pl.ds(k * SUBLANES * SUBLANES + s, SUBLANES,
                                                              stride=SUBLANES), :]
            return carry

        lax.fori_loop(0, seg // SUBLANES, body, 0)


def _tile_scan(xr_ref, xi_ref, lam_ref, car_ref, cai_ref, cn_r, cn_i, blocks, seg, reverse):
    shape = (SUBLANES, 128)
    lrs = [jnp.broadcast_to(lam_ref[0:1, _lanes(j)], shape) for j in blocks]
    lis = [jnp.broadcast_to(lam_ref[1:2, _lanes(j)], shape) for j in blocks]

    def step(k, carry):
        t = seg - 1 - k if reverse else k
        rows = pl.ds(pl.multiple_of(t * SUBLANES, SUBLANES), SUBLANES)
        out = []
        for n, j in enumerate(blocks):
            cr, ci = carry[2 * n], carry[2 * n + 1]
            nr = lrs[n] * cr - lis[n] * ci + xr_ref[rows, _lanes(j)]
            ni = lrs[n] * ci + lis[n] * cr + xi_ref[rows, _lanes(j)]
            xr_ref[rows, _lanes(j)] = nr
            xi_ref[rows, _lanes(j)] = ni
            out += [nr, ni]
        return tuple(out)

    zero = jnp.zeros(shape, F32)
    fin = lax.fori_loop(0, seg, step, (zero,) * (2 * len(blocks)), unroll=2)
    for n, j in enumerate(blocks):
        ls = _lanes(j)
        fr, fi = fin[2 * n], fin[2 * n + 1]
        sr, si = lam_ref[2:3, ls], lam_ref[3:4, ls]
        pr, pi = car_ref[:, ls], cai_ref[:, ls]
        for s in (reversed(range(SUBLANES)) if reverse else range(SUBLANES)):
            cn_r[s:s + 1, ls] = pr
            cn_i[s:s + 1, ls] = pi
            pr, pi = fr[s:s + 1, :] + sr * pr - si * pi, fi[s:s + 1, :] + sr * pi + si * pr
        car_ref[:, ls] = pr
        cai_ref[:, ls] = pi


def _s5_fwd3(proj, lam_rows, p3_re, p3_im, bbr4, bbi4, crt4, cit4, d_row, t_len, tb):
    seg = tb // SUBLANES

    def body(u_ref, lam_ref, p3r_ref, p3i_ref, bbr_ref, bbi_ref, crt_ref, cit_ref, d_ref,
             hr_ref, hi_ref, ypre_ref, ys_ref, car_ref, cai_ref, cn_r, cn_i, stage_ref, us_ref, yseg_ref):
        @pl.when(pl.program_id(0) == 0)
        def _():
            car_ref[...] = jnp.zeros_like(car_ref)
            cai_ref[...] = jnp.zeros_like(cai_ref)

        _to_segment_order(u_ref[...], stage_ref, us_ref, seg)
        u = us_ref[...]
        for i in range(S5_BLOCKS):
            ui = u[:, i * S5_BW:(i + 1) * S5_BW]
            hr_ref[:, pl.ds(i * S5_BL, S5_BL)] = _dot(ui, bbr_ref[i])
            hi_ref[:, pl.ds(i * S5_BL, S5_BL)] = _dot(ui, bbi_ref[i])
        for lc in range(S5_LANE_BLOCKS // S5_SCAN_BLOCKS):
            blocks = range(lc * S5_SCAN_BLOCKS, (lc + 1) * S5_SCAN_BLOCKS)
            _tile_scan(hr_ref, hi_ref, lam_ref, car_ref, cai_ref, cn_r, cn_i, blocks, seg, False)
            crs = [cn_r[:, _lanes(j)] for j in blocks]
            cis = [cn_i[:, _lanes(j)] for j in blocks]

            def fix(t, carry, blocks=blocks, crs=crs, cis=cis):
                rows = pl.ds(pl.multiple_of(t * SUBLANES, SUBLANES), SUBLANES)
                for n, j in enumerate(blocks):
                    pr, pi = p3r_ref[t, :, _lanes(j)], p3i_ref[t, :, _lanes(j)]
                    hr_ref[rows, _lanes(j)] += pr * crs[n] - pi * cis[n]
                    hi_ref[rows, _lanes(j)] += pr * cis[n] + pi * crs[n]
                return carry

            lax.fori_loop(0, seg, fix, 0, unroll=2)
        for i in range(S5_BLOCKS):
            ws = pl.ds(i * S5_BW, S5_BW)
            bl = pl.ds(i * S5_BL, S5_BL)
            yseg_ref[:, ws] = (_dot(hr_ref[:, bl], crt_ref[i]) - _dot(hi_ref[:, bl], cit_ref[i])
                               + d_ref[:, ws] * u[:, i * S5_BW:(i + 1) * S5_BW])
        _from_segment_order(yseg_ref[...], stage_ref, ypre_ref, seg)
        ys_ref[...] = jax.nn.gelu(ypre_ref[...], approximate=True).astype(ys_ref.dtype)

    whole = pl.BlockSpec(memory_space=pltpu.VMEM)
    return pl.pallas_call(
        body, name="s5_fwd", grid=(t_len // tb,),
        in_specs=[pl.BlockSpec((tb, S5_WIDTH), lambda i: (i, 4096 // S5_WIDTH))] + [whole] * 8,
        out_specs=[pl.BlockSpec((tb, S5_LANES), lambda i: (i, 0)), pl.BlockSpec((tb, S5_LANES), lambda i: (i, 0)),
                   pl.BlockSpec((tb, S5_WIDTH), lambda i: (i, 0)), pl.BlockSpec((tb, S5_WIDTH), lambda i: (i, 0))],
        out_shape=[jax.ShapeDtypeStruct((t_len, S5_LANES), F32), jax.ShapeDtypeStruct((t_len, S5_LANES), F32),
                   jax.ShapeDtypeStruct((t_len, S5_WIDTH), F32), jax.ShapeDtypeStruct((t_len, S5_WIDTH), MXU_DTYPE)],
        scratch_shapes=[pltpu.VMEM((1, S5_LANES), F32), pltpu.VMEM((1, S5_LANES), F32),
                        pltpu.VMEM((SUBLANES, S5_LANES), F32), pltpu.VMEM((SUBLANES, S5_LANES), F32),
                        pltpu.VMEM((S5_WIDTH // 128, tb, 128), F32), pltpu.VMEM((tb, S5_WIDTH), F32),
                        pltpu.VMEM((tb, S5_WIDTH), F32)],
        compiler_params=_params("arbitrary"))(proj, lam_rows, p3_re, p3_im, bbr4, bbi4, crt4, cit4, d_row)


def _s5_bwd3(dgelu, y_pre, proj, h_re, h_im, lam_rows, p3_re, p3_im, bbr4, bbi4, cr4, ci4, d_row, dproj, t_len, tb):
    seg = tb // SUBLANES
    nb = t_len // tb

    def body(dg_ref, yp_ref, u_ref, hr_ref, hi_ref, lam_ref, p3r_ref, p3i_ref, bbr_ref, bbi_ref, cr_ref, ci_ref,
             d_ref, _, du_ref, dbbr_ref, dbbi_ref, dcr_ref, dci_ref, dd_ref, dlam_ref,
             gr_ref, gi_ref, car_ref, cai_ref, cn_r, cn_i, stage_ref, us_ref, dys_ref, duseg_ref):
        @pl.when(pl.program_id(0) == 0)
        def _():
            for ref in (car_ref, cai_ref, dbbr_ref, dbbi_ref, dcr_ref, dci_ref, dd_ref, dlam_ref):
                ref[...] = jnp.zeros_like(ref)

        _to_segment_order(u_ref[...], stage_ref, us_ref, seg)
        _to_segment_order(dg_ref[...] * _dgelu(yp_ref[...]), stage_ref, dys_ref, seg)
        u, dy = us_ref[...], dys_ref[...]
        for i in range(S5_BLOCKS):
            dyi = dy[:, i * S5_BW:(i + 1) * S5_BW]
            gr_ref[:, pl.ds(i * S5_BL, S5_BL)] = _dot(dyi, cr_ref[i])
            gi_ref[:, pl.ds(i * S5_BL, S5_BL)] = -_dot(dyi, ci_ref[i])
        for lc in range(S5_LANE_BLOCKS // S5_SCAN_BLOCKS):
            blocks = range(lc * S5_SCAN_BLOCKS, (lc + 1) * S5_SCAN_BLOCKS)
            _tile_scan(gr_ref, gi_ref, lam_ref, car_ref, cai_ref, cn_r, cn_i, blocks, seg, True)
            crs = [cn_r[:, _lanes(j)] for j in blocks]
            cis = [cn_i[:, _lanes(j)] for j in blocks]

            def fix(k, carry, blocks=blocks, crs=crs, cis=cis):
                t = seg - 1 - k
                rows = pl.ds(pl.multiple_of(t * SUBLANES, SUBLANES), SUBLANES)
                out = []
                for n, j in enumerate(blocks):
                    nr, ni, slr, sli = carry[4 * n:4 * n + 4]
                    pr, pi = p3r_ref[t, :, _lanes(j)], p3i_ref[t, :, _lanes(j)]
                    g_r = gr_ref[rows, _lanes(j)] + pr * crs[n] - pi * cis[n]
                    g_i = gi_ref[rows, _lanes(j)] + pr * cis[n] + pi * crs[n]
                    gr_ref[rows, _lanes(j)] = g_r
                    gi_ref[rows, _lanes(j)] = g_i
                    hr, hi = hr_ref[rows, _lanes(j)], hi_ref[rows, _lanes(j)]
                    out += [g_r, g_i, slr + nr * hr + ni * hi, sli + ni * hr - nr * hi]
                return tuple(out)

            zero = jnp.zeros((SUBLANES, 128), F32)
            init = []
            for n in range(len(blocks)):
                init += [crs[n], cis[n], zero, zero]
            fin = lax.fori_loop(0, seg, fix, tuple(init), unroll=2)
            for n, j in enumerate(blocks):
                dlam_ref[0:1, _lanes(j)] += jnp.sum(fin[4 * n + 2], axis=0, keepdims=True)
                dlam_ref[1:2, _lanes(j)] += jnp.sum(fin[4 * n + 3], axis=0, keepdims=True)
        for i in range(S5_BLOCKS):
            ws = pl.ds(i * S5_BW, S5_BW)
            bl = pl.ds(i * S5_BL, S5_BL)
            ui, dyi = u[:, i * S5_BW:(i + 1) * S5_BW], dy[:, i * S5_BW:(i + 1) * S5_BW]
            gr, gi = gr_ref[:, bl], gi_ref[:, bl]
            duseg_ref[:, ws] = _dot(gr, bbr_ref[i], _NT) + _dot(gi, bbi_ref[i], _NT) + d_ref[:, ws] * dyi
            dbbr_ref[i] += _dot(ui, gr, _TN)
            dbbi_ref[i] += _dot(ui, gi, _TN)
            dcr_ref[i] += _dot(hr_ref[:, bl], dyi, _TN)
            dci_ref[i] -= _dot(hi_ref[:, bl], dyi, _TN)
        dd_ref[...] += jnp.sum(dy * u, axis=0, keepdims=True)
        _from_segment_order(duseg_ref[...], stage_ref, du_ref, seg)

    whole = pl.BlockSpec(memory_space=pltpu.VMEM)
    rev = lambda i: (nb - 1 - i, 0)
    const3 = lambda i: (0, 0, 0)
    return pl.pallas_call(
        body, name="s5_bwd", grid=(nb,),
        in_specs=[pl.BlockSpec((tb, S5_WIDTH), rev), pl.BlockSpec((tb, S5_WIDTH), rev),
                  pl.BlockSpec((tb, S5_WIDTH), lambda i: (nb - 1 - i, 4096 // S5_WIDTH)),
                  pl.BlockSpec((tb, S5_LANES), rev), pl.BlockSpec((tb, S5_LANES), rev)] + [whole] * 8
                 + [pl.BlockSpec(memory_space=pl.ANY)],
        out_specs=[pl.BlockSpec((tb, S5_WIDTH), lambda i: (nb - 1 - i, 4096 // S5_WIDTH)),
                   pl.BlockSpec((S5_BLOCKS, S5_BW, S5_BL), const3), pl.BlockSpec((S5_BLOCKS, S5_BW, S5_BL), const3),
                   pl.BlockSpec((S5_BLOCKS, S5_BL, S5_BW), const3), pl.BlockSpec((S5_BLOCKS, S5_BL, S5_BW), const3),
                   pl.BlockSpec((1, S5_WIDTH), lambda i: (0, 0)), pl.BlockSpec((2, S5_LANES), lambda i: (0, 0))],
        out_shape=[jax.ShapeDtypeStruct((t_len, IN_COLS), F32),
                   jax.ShapeDtypeStruct((S5_BLOCKS, S5_BW, S5_BL), F32),
                   jax.ShapeDtypeStruct((S5_BLOCKS, S5_BW, S5_BL), F32),
                   jax.ShapeDtypeStruct((S5_BLOCKS, S5_BL, S5_BW), F32),
                   jax.ShapeDtypeStruct((S5_BLOCKS, S5_BL, S5_BW), F32),
                   jax.ShapeDtypeStruct((1, S5_WIDTH), F32), jax.ShapeDtypeStruct((2, S5_LANES), F32)],
        scratch_shapes=[pltpu.VMEM((tb, S5_LANES), F32), pltpu.VMEM((tb, S5_LANES), F32),
                        pltpu.VMEM((1, S5_LANES), F32), pltpu.VMEM((1, S5_LANES), F32),
                        pltpu.VMEM((SUBLANES, S5_LANES), F32), pltpu.VMEM((SUBLANES, S5_LANES), F32),
                        pltpu.VMEM((S5_WIDTH // 128, tb, 128), F32), pltpu.VMEM((tb, S5_WIDTH), F32),
                        pltpu.VMEM((tb, S5_WIDTH), F32), pltpu.VMEM((tb, S5_WIDTH), F32)],
        input_output_aliases={13: 0},
        compiler_params=_params("arbitrary"))(dgelu, y_pre, proj, h_re, h_im, lam_rows, p3_re, p3_im, bbr4, bbi4,
                                              cr4, ci4, d_row, dproj)


def _block_diag4(per_group):
    g8 = S5_GROUPS // S5_BLOCKS
    eye = jnp.eye(g8, dtype=bool)[None, :, None, :, None]
    dense = jnp.where(eye, per_group.reshape(S5_BLOCKS, g8, S5_GROUP, 1, S5_STATE), 0.0)
    return dense.reshape(S5_BLOCKS, S5_BW, S5_BL)


def _diag_blocks4(dense):
    g8 = S5_GROUPS // S5_BLOCKS
    ar = jnp.arange(g8)
    d5 = dense.reshape(S5_BLOCKS, g8, S5_GROUP, g8, S5_STATE)
    return d5[:, ar, :, ar, :].transpose(1, 0, 2, 3).reshape(S5_GROUPS, S5_GROUP, S5_STATE)


def _block_diag(per_group):
    eye = jnp.eye(S5_GROUPS, dtype=bool)[:, None, :, None]
    dense = jnp.where(eye, per_group[:, :, None, :], 0.0)
    return dense.reshape(S5_WIDTH, S5_LANES)


def _diag_blocks(dense):
    ar = jnp.arange(S5_GROUPS)
    return dense.reshape(S5_GROUPS, S5_GROUP, S5_GROUPS, S5_STATE)[ar, :, ar, :]


def _local_step(x, p, target, w, sm, comm=None):
    t_len = x.shape[0]
    tm = min(256, t_len)
    tmm = min(512, t_len)
    tb_hg = min(256, t_len)
    tb_s5 = min(256, t_len)
    g1, g2, g3, ghn = sm["norm_g"], sm["ple_norm_g"], sm["final_norm_g"].reshape(1, D_MODEL), sm["hg_norm_g"]

    def rms_f(xv, g):
        r = lax.rsqrt(jnp.mean(xv * xv, axis=-1, keepdims=True) + NORM_EPS)
        return (xv * r * g,)

    (u,) = _rowwise("rms_in", rms_f, t_len, tm, [(x, 1024, 0)], [g1], [(1024, MXU_DTYPE)])
    in_shard = IN_COLS // N_CHIPS
    w_in = w["w_in"]
    if comm is None:
        proj = _mm_nn("mm_in", u, w_in, tmm, in_shard)
    else:
        proj, landed = _mm_nn("mm_in", u, w_in, tmm, in_shard, riding=comm.gather_rest())
        w = comm.rest_weights(landed)
    o_hg, act_hg, s_prev = _hgrn2_fwd2(proj, sm["hg_lb"], ghn, t_len, tb_hg)

    lanes = lambda a: a.reshape(1, S5_LANES)
    a_re, a_im = lanes(sm["s5_a_re"]), lanes(sm["s5_a_im"])
    ldt = lanes(jnp.broadcast_to(sm["s5_log_dt"].reshape(S5_GROUPS, 1), (S5_GROUPS, S5_STATE)))
    to_t = lambda b: b.reshape(S5_GROUPS, S5_STATE, S5_GROUP).transpose(2, 0, 1).reshape(S5_GROUP, S5_LANES)
    b_re_t, b_im_t = to_t(sm["s5_b_re"]), to_t(sm["s5_b_im"])
    pw_re, pw_im, bbr_t, bbi_t = _s5_powers(a_re, a_im, ldt, b_re_t, b_im_t, tb_s5 // SUBLANES)
    from_t = lambda b: b.reshape(S5_GROUP, S5_GROUPS, S5_STATE).transpose(1, 0, 2)
    bbr_bd = _block_diag4(from_t(bbr_t)).astype(MXU_DTYPE)
    bbi_bd = _block_diag4(from_t(bbi_t)).astype(MXU_DTYPE)
    cr_bd = _block_diag4(sm["s5_c_re"].reshape(S5_GROUPS, S5_GROUP, S5_STATE)).astype(MXU_DTYPE)
    ci_bd = _block_diag4(sm["s5_c_im"].reshape(S5_GROUPS, S5_GROUP, S5_STATE)).astype(MXU_DTYPE)
    d_row = sm["s5_d"].reshape(1, S5_WIDTH)
    h_re, h_im, y_pre, ys_gelu = _s5_fwd3(proj, *_scan_tables(pw_re, pw_im, False), bbr_bd, bbi_bd,
                                          cr_bd.transpose(0, 2, 1), ci_bd.transpose(0, 2, 1), d_row, t_len, tb_s5)
    def mix_f(act, ysg, z, gh, gs, xv, w_glu, b_glu, w_o_hg, w_o_s5, w_out):
        gl_ = _dot(ysg, w_glu) + b_glu
        a, b = gl_[:, :S5_WIDTH], gl_[:, S5_WIDTH:]
        ys2_ = (a * _sig(b) * (z * _sig(z))).astype(MXU_DTYPE)
        yh, ys = _dot(act, w_o_hg), _dot(ys2_, w_o_s5)
        mg = (_sig(gh) * yh + _sig(gs) * ys).astype(MXU_DTYPE)
        return (gl_, ys2_, yh, ys, mg, xv + _dot(mg, w_out))

    glu, ys2, y_hg, y_s5, merged, h1 = _rowwise(
        "mix_out", mix_f, t_len, tm,
        [(act_hg, 1024, 0), (ys_gelu, 512, 0), (proj, 512, 4608 // 512), (proj, 1024, 5), (proj, 1024, 6),
         (x, 1024, 0)], [w["w_glu"], sm["b_glu"], w["w_o_hg"], w["w_o_s5"], w["w_out"]],
        [(1024, F32), (512, MXU_DTYPE), (1024, F32), (1024, F32), (1024, MXU_DTYPE), (1024, F32)])

    def head_f(h1v, pv, tgt, g_ple, g, w_ple, w_gate):
        r2 = lax.rsqrt(jnp.mean(h1v * h1v, axis=-1, keepdims=True) + NORM_EPS)
        n2_ = (h1v * r2 * g_ple).astype(MXU_DTYPE)
        glv, pev = _dot(n2_, w_gate), _dot(pv, w_ple)
        gate = _sig(glv)
        h2 = h1v + pev * gate
        r = lax.rsqrt(jnp.mean(h2 * h2, axis=-1, keepdims=True) + NORM_EPS)
        e = h2 * r * g - tgt
        loss = 0.5 * jnp.sum(jnp.mean(e * e, axis=-1, keepdims=True), axis=0, keepdims=True)
        dy = e * (1.0 / D_MODEL)
        dg = jnp.sum(dy * h2 * r, axis=0, keepdims=True)
        t = dy * g
        dh2 = r * t - h2 * (r * r * r) * jnp.mean(t * h2, axis=-1, keepdims=True)
        return (n2_, dh2, dh2 * gate, dh2 * pev * gate * (1.0 - gate), jnp.broadcast_to(loss, (1, 128)), dg)

    n2, dh2, dpe, dgl, loss_row, d_g3 = _rowwise(
        "ple_loss_head", head_f, t_len, tm, [(h1, 1024, 0), (p, 256, 0), (target, 1024, 0)],
        [g2, g3, w["w_ple"], w["w_ple_gate"]],
        [(1024, MXU_DTYPE), (1024, F32), (1024, MXU_DTYPE), (1024, MXU_DTYPE)], accs=[(1, 128), (1, 1024)])

    gb = {}
    gb["w_ple"] = _mm_tn("mm_d_w_ple", p, dpe, tmm, 1024)
    gb["w_ple_gate"] = _mm_tn("mm_d_w_ple_gate", n2, dgl, tmm, 1024)
    def ple_b(dn, h1v, dh, g):
        dx, dg = _rms_bwd(dn, h1v, g)
        return (dh + dx, dg)

    dh1, d_g2 = _mm_nt_then("mm_d_n2_rms_ple_bwd", dgl, w["w_ple_gate"], tm, 1024, ple_b,
                            [(h1, 1024, 0), (dh2, 1024, 0)], [g2], [(1024, F32)], accs=[(1, 1024)])
    gb["w_out"] = _mm_tn("mm_d_w_out", merged, dh1, tmm, 1024)
    dmerged = _mm_nt("mm_d_merged", dh1, w["w_out"], tmm, 1024)

    def gate_b(dm, y, gt):
        s = _sig(gt)
        return (dm * s, dm * y * s * (1.0 - s))

    dy_hg, dproj = _rowwise("gate_hg_bwd", gate_b, t_len, tm, [(dmerged, 1024, 0), (y_hg, 1024, 0), (proj, 1024, 5)],
                            [], [(1024, MXU_DTYPE), (1024, F32, 5, IN_COLS)])
    dy_s5, dproj = _rowwise("gate_s5_bwd", gate_b, t_len, tm, [(dmerged, 1024, 0), (y_s5, 1024, 0), (proj, 1024, 6)],
                            [], [(1024, MXU_DTYPE), (1024, F32, 6, IN_COLS)], alias=(dproj, 1))
    gb["w_o_s5"] = _mm_tn("mm_d_w_o_s5", ys2, dy_s5, tmm, 1024)
    def glu_b(dys, gl_, z):
        a, b = gl_[:, :S5_WIDTH], gl_[:, S5_WIDTH:]
        sb, sz = _sig(b), _sig(z)
        silu = z * sz
        dglu = jnp.concatenate([dys * sb * silu, dys * a * silu * sb * (1.0 - sb)], axis=1)
        return (dglu, dys * a * sb * _dsilu(z, sz), jnp.sum(dglu, axis=0, keepdims=True))

    dglu, dproj, d_bglu = _mm_nt_then("mm_d_ys2_glu_bwd", dy_s5, w["w_o_s5"], tm, 1024, glu_b,
                                      [(glu, 1024, 0), (proj, 512, 4608 // 512)], [],
                                      [(1024, MXU_DTYPE), (512, F32, 4608 // 512, IN_COLS)], accs=[(1, 1024)],
                                      alias=(dproj, 1))
    gb["w_glu"] = _mm_tn("mm_d_w_glu", ys_gelu, dglu, tmm, 1024)
    dgelu = _mm_nt("mm_d_gelu", dglu, w["w_glu"], tmm, 1024)
    dproj, d_bbr, d_bbi, d_crt, d_cit, d_d, d_lam = _s5_bwd3(dgelu, y_pre, proj, h_re, h_im,
                                                            *_scan_tables(pw_re, pw_im, True), bbr_bd, bbi_bd, cr_bd,
                                                            ci_bd, d_row, dproj, t_len, tb_s5)
    to_t3 = lambda b: b.transpose(1, 0, 2).reshape(S5_GROUP, S5_LANES)
    d_are, d_aim, d_ldt, d_br_t, d_bi_t = _s5_prep_bwd(a_re, a_im, ldt, b_re_t, b_im_t, d_lam,
                                                       to_t3(_diag_blocks4(d_bbr)), to_t3(_diag_blocks4(d_bbi)))
    gb["w_o_hg"] = _mm_tn("mm_d_w_o_hg", act_hg, dy_hg, tmm, 1024)
    def hg_gate_b(da, o, g, gn):
        dos, dgs, dgns = [], [], []
        for h in range(HG_HEADS):
            sl = slice(h * HG_DIM, (h + 1) * HG_DIM)
            oh, gh, dah, gnh = o[:, sl], g[:, sl], da[:, sl], gn[:, sl]
            rr = lax.rsqrt(jnp.mean(oh * oh, axis=-1, keepdims=True) + NORM_EPS)
            sg = _sig(gh)
            dgs.append(dah * (oh * rr * gnh) * _dsilu(gh, sg))
            don = dah * (gh * sg)
            t = don * gnh
            dos.append(rr * t - oh * (rr * rr * rr) * jnp.mean(t * oh, axis=-1, keepdims=True))
            dgns.append(jnp.sum(don * oh * rr, axis=0, keepdims=True))
        return (jnp.concatenate(dos, axis=1), jnp.concatenate(dgs, axis=1), jnp.concatenate(dgns, axis=1))

    d_o, dproj, d_ghn = _mm_nt_then("mm_d_act_hg_gate_bwd", dy_hg, w["w_o_hg"], tm, 1024, hg_gate_b,
                                    [(o_hg, 1024, 0), (proj, 1024, 3)], [ghn],
                                    [(1024, F32), (1024, F32, 3, IN_COLS)], accs=[(1, 1024)], alias=(dproj, 1))
    dproj, d_lb = _hgrn2_bwd2(proj, d_o, s_prev, sm["hg_lb"], dproj, t_len, tb_hg)

    def in_b(duv, xv, dh, g):
        dx, dg = _rms_bwd(duv, xv, g)
        return (dh + dx, dg)

    in_args = ("mm_d_u_rms_in_bwd", dproj, w_in, tmm, in_shard, in_b, [(x, 1024, 0), (dh1, 1024, 0)], [g1],
               [(1024, F32)])
    if comm is None:
        gb["w_in"] = _mm_tn("mm_d_w_in", u, dproj, tmm, in_shard, col_shards=True)
        grad_x, d_g1 = _mm_nt_then(*in_args, accs=[(1, 1024)])
    else:
        gb["w_in"], landed = _mm_tn("mm_d_w_in", u, dproj, tmm, in_shard, col_shards=True,
                                    riding=comm.scatter("rest", _pack_rest_full(gb)))
        comm.landed["rest"] = landed
        grad_x, d_g1, landed = _mm_nt_then(*in_args, accs=[(1, 1024)], riding=comm.scatter(
            "in", gb["w_in"].reshape(N_CHIPS, 2, D_MODEL // 2, in_shard)))
        comm.landed["in"] = landed

    back_t = lambda b: b.reshape(S5_GROUP, S5_GROUPS, S5_STATE).transpose(1, 2, 0).reshape(1, S5_GROUPS, S5_STATE,
                                                                                           S5_GROUP)
    gs = {
        "norm_g": d_g1, "hg_lb": d_lb, "hg_norm_g": d_ghn,
        "s5_a_re": d_are.reshape(1, S5_GROUPS, S5_STATE), "s5_a_im": d_aim.reshape(1, S5_GROUPS, S5_STATE),
        "s5_log_dt": d_ldt[0:1, :S5_GROUPS],
        "s5_b_re": back_t(d_br_t), "s5_b_im": back_t(d_bi_t),
        "s5_c_re": _diag_blocks4(d_crt.transpose(0, 2, 1)).reshape(1, S5_GROUPS, S5_GROUP, S5_STATE),
        "s5_c_im": _diag_blocks4(d_cit.transpose(0, 2, 1)).reshape(1, S5_GROUPS, S5_GROUP, S5_STATE),
        "s5_d": d_d.reshape(1, S5_GROUPS, S5_GROUP), "b_glu": d_bglu, "ple_norm_g": d_g2,
        "final_norm_g": d_g3.reshape(D_MODEL),
    }
    return loss_row, grad_x, gb, gs


def _shard_shape(name):
    r, c = BIG_SHAPE[name]
    return (r, c // N_CHIPS) if name in BIG_COL_SHARDED else (r // N_CHIPS, c)


def _pack_shard(parts):
    return jnp.concatenate([parts[n].reshape(-1, PACK_W) for n in BIG], axis=0)


def _unpack_shard(packed):
    out, off = {}, 0
    for n in BIG:
        r, c = _shard_shape(n)
        rows = r * c // PACK_W
        out[n] = packed[off:off + rows].reshape(1, r, c)
        off += rows
    return out


def _unpack_full(gathered):
    out, off = {}, 0
    for n in BIG:
        r, c = _shard_shape(n)
        rows = r * c // PACK_W
        sh = gathered[:, off:off + rows].reshape(N_CHIPS, r, c)
        out[n] = sh.transpose(1, 0, 2).reshape(BIG_SHAPE[n]) if n in BIG_COL_SHARDED else sh.reshape(BIG_SHAPE[n])
        off += rows
    return out


def _pack_full(full):
    parts = []
    for n in BIG:
        r, c = _shard_shape(n)
        g = full[n]
        sh = g.reshape(BIG_SHAPE[n][0], N_CHIPS, c).transpose(1, 0, 2) if n in BIG_COL_SHARDED else g
        parts.append(sh.reshape(N_CHIPS, r * c // PACK_W, PACK_W))
    packed = jnp.concatenate(parts, axis=1)
    return packed.reshape(N_CHIPS, 2, HALF_ROWS, PACK_W).transpose(1, 0, 2, 3)


def _pack_small(parts, last):
    flat = jnp.concatenate([parts[n].reshape(-1) for n in SMALL] + [last.reshape(-1)])
    return jnp.pad(flat, (0, SMALL_ROWS * PACK_W - flat.shape[0])).reshape(SMALL_ROWS, PACK_W)


def _unpack_small(packed):
    flat, out, off = packed.reshape(-1), {}, 0
    for n in SMALL:
        size = 1
        for d in SMALL_SHAPE[n]:
            size *= d
        out[n] = flat[off:off + size].reshape(SMALL_SHAPE[n])
        off += size
    return out, flat[off]


def _place():
    x, y, c = lax.axis_index("x"), lax.axis_index("y"), lax.axis_index("c")
    return x, y, c, [(1 - x, y), (x, 1 - y), (1 - x, 1 - y)]


def _remote(src, dst, send_sems, recv_sems, k, to):
    return pltpu.make_async_remote_copy(src_ref=src, dst_ref=dst, send_sem=send_sems.at[k], recv_sem=recv_sems.at[k],
                                        device_id=to, device_id_type=MESH)


_HBM = pl.BlockSpec(memory_space=pl.ANY)


def _all_gather_weights(wp):
    def body(wp_ref, out_ref, send_sems, recv_sems):
        x, y, c, chips = _place()
        k = 2 * x + y
        sibling = (x, y, 1 - c)
        first =[_remote(wp_ref.at[c], out_ref.at[k, c], send_sems, recv_sems, j, (cx, cy, c))
                 for j, (cx, cy) in enumerate(chips)]
        for cp in first:
            cp.start()
        passed = []
        for j, (cx, cy) in enumerate(chips):
            kj = 2 * cx + cy
            _remote(wp_ref.at[c], out_ref.at[kj, c], send_sems, recv_sems, j, (cx, cy, c)).wait_recv()
            cp = _remote(out_ref.at[kj, c], out_ref.at[kj, c], send_sems, recv_sems, 3 + j, sibling)
            cp.start()
            passed.append(cp)
        for j, (cx, cy) in enumerate(chips):
            kj = 2 * cx + cy
            _remote(wp_ref.at[c], out_ref.at[kj, 1 - c], send_sems, recv_sems, 3 + j, sibling).wait_recv()
        for cp in first + passed:
            cp.wait_send()

    return pl.pallas_call(
        body, name="all_gather_weights", in_specs=[_HBM], out_specs=_HBM,
        out_shape=jax.ShapeDtypeStruct((N_CHIPS, 2, HALF_ROWS, PACK_W), wp.dtype),
        scratch_shapes=[pltpu.SemaphoreType.DMA((6,)), pltpu.SemaphoreType.DMA((6,))])(wp)


def _exchange_halves(pg):
    def body(pg_ref, out_ref, send_sems, recv_sems):
        x, y, c, _ = _place()
        cp = _remote(pg_ref.at[1 - c], out_ref, send_sems, recv_sems, 0, (x, y, 1 - c))
        cp.start()
        cp.wait()

    return pl.pallas_call(
        body, name="exchange_halves", in_specs=[_HBM], out_specs=_HBM,
        out_shape=jax.ShapeDtypeStruct((N_CHIPS, HALF_ROWS, PACK_W), pg.dtype),
        scratch_shapes=[pltpu.SemaphoreType.DMA((1,)), pltpu.SemaphoreType.DMA((1,))])(pg)


def _scatter_chip_sums(ps):
    def body(ps_ref, out_ref, send_sems, recv_sems):
        x, y, c, chips = _place()
        cps = [_remote(ps_ref.at[2 * cx + cy], out_ref.at[j], send_sems, recv_sems, j, (cx, cy, c))
               for j, (cx, cy) in enumerate(chips)]
        for cp in cps:
            cp.start()
        for cp in cps:
            cp.wait()

    return pl.pallas_call(
        body, name="scatter_chip_sums", in_specs=[_HBM], out_specs=_HBM,
        out_shape=jax.ShapeDtypeStruct((3, HALF_ROWS, PACK_W), ps.dtype),
        scratch_shapes=[pltpu.SemaphoreType.DMA((3,)), pltpu.SemaphoreType.DMA((3,))])(ps)


def _share_half(g_half):
    def body(g_ref, out_ref, send_sems, recv_sems):
        x, y, c, _ = _place()
        cp = _remote(g_ref, out_ref.at[c], send_sems, recv_sems, 0, (x, y, 1 - c))
        cp.start()
        _remote(g_ref, out_ref.at[1 - c], send_sems, recv_sems, 0, (x, y, 1 - c)).wait_recv()
        cp.wait_send()

    return pl.pallas_call(
        body, name="share_half", in_specs=[_HBM], out_specs=_HBM,
        out_shape=jax.ShapeDtypeStruct((2, HALF_ROWS, PACK_W), g_half.dtype),
        scratch_shapes=[pltpu.SemaphoreType.DMA((1,)), pltpu.SemaphoreType.DMA((1,))])(g_half)


REDUCE_ROWS = 480


def _sum_pair(pg, theirs, c):
    def body(c_ref, a_ref, b_ref, o_ref):
        o_ref[...] = (a_ref[...] + b_ref[...]).astype(o_ref.dtype)

    return pl.pallas_call(
        body, name="sum_pair",
        grid_spec=pltpu.PrefetchScalarGridSpec(
            num_scalar_prefetch=1, grid=(N_CHIPS, HALF_ROWS // REDUCE_ROWS),
            in_specs=[pl.BlockSpec((None, None, REDUCE_ROWS, PACK_W), lambda j, i, c_ref: (c_ref[0], j, i, 0)),
                      pl.BlockSpec((None, REDUCE_ROWS, PACK_W), lambda j, i, c_ref: (j, i, 0))],
            out_specs=pl.BlockSpec((None, REDUCE_ROWS, PACK_W), lambda j, i, c_ref: (j, i, 0))),
        out_shape=jax.ShapeDtypeStruct((N_CHIPS, HALF_ROWS, PACK_W), WIRE_DTYPE),
        compiler_params=_params("arbitrary", "arbitrary"))(c.reshape(1), pg, theirs)


def _sum_chips(ps, others, k):
    def body(k_ref, a_ref, b_ref, o_ref):
        o_ref[...] = ((a_ref[...].astype(F32) + b_ref[0].astype(F32)) + b_ref[1].astype(F32)) + b_ref[2].astype(F32)

    return pl.pallas_call(
        body, name="sum_chips",
        grid_spec=pltpu.PrefetchScalarGridSpec(
            num_scalar_prefetch=1, grid=(HALF_ROWS // REDUCE_ROWS,),
            in_specs=[pl.BlockSpec((None, REDUCE_ROWS, PACK_W), lambda i, k_ref: (k_ref[0], i, 0)),
                      pl.BlockSpec((3, REDUCE_ROWS, PACK_W), lambda i, k_ref: (0, i, 0))],
            out_specs=pl.BlockSpec((REDUCE_ROWS, PACK_W), lambda i, k_ref: (i, 0))),
        out_shape=jax.ShapeDtypeStruct((HALF_ROWS, PACK_W), F32),
        compiler_params=_params("arbitrary"))(k.reshape(1), ps, others)


REST = tuple(n for n in BIG if n != "w_in")
REST_ROWS = sum(BIG_SHAPE[n][0] * BIG_SHAPE[n][1] for n in REST) // (N_CHIPS * PACK_W)
IN_SHARD = IN_COLS // N_CHIPS
IN_TILE, REST_TILE = 256, 272


def _pack_rest(parts):
    return jnp.concatenate([parts[n].reshape(-1, PACK_W) for n in REST], axis=0)


def _unpack_rest(packed):
    out, off = {}, 0
    for n in REST:
        r, c = _shard_shape(n)
        rows = r * c // PACK_W
        out[n] = packed[off:off + rows].reshape(1, r, c)
        off += rows
    return out


def _unpack_rest_full(gathered):
    out, off = {}, 0
    for n in REST:
        r, c = _shard_shape(n)
        rows = r * c // PACK_W
        sh = gathered[:, off:off + rows].reshape(N_CHIPS, r, c)
        out[n] = sh.transpose(1, 0, 2).reshape(BIG_SHAPE[n]) if n in BIG_COL_SHARDED else sh.reshape(BIG_SHAPE[n])
        off += rows
    return out


def _pack_rest_full(full):
    parts = []
    for n in REST:
        r, c = _shard_shape(n)
        g = full[n]
        sh = g.reshape(BIG_SHAPE[n][0], N_CHIPS, c).transpose(1, 0, 2) if n in BIG_COL_SHARDED else g
        parts.append(sh.reshape(N_CHIPS, r * c // PACK_W, PACK_W))
    return jnp.concatenate(parts, axis=1).reshape(N_CHIPS, 2, REST_ROWS // 2, PACK_W)


def _gather_shards(ws):
    n = len(ws)

    def body(*refs):
        w_refs, out_refs, (send_sems, recv_sems) = refs[:n], refs[n:2 * n], refs[2 * n:]
        x, y, c, chips = _place()
        k = 2 * x + y
        sibling = (x, y, 1 - c)
        first = [_remote(w_ref.at[c], out_ref.at[k, c], send_sems, recv_sems, 6 * g + j, (cx, cy, c))
                 for j, (cx, cy) in enumerate(chips) for g, (w_ref, out_ref) in enumerate(zip(w_refs, out_refs))]
        for cp in first:
            cp.start()
        passed = []
        for j, (cx, cy) in enumerate(chips):
            kj = 2 * cx + cy
            for g, (w_ref, out_ref) in enumerate(zip(w_refs, out_refs)):
                _remote(w_ref.at[c], out_ref.at[kj, c], send_sems, recv_sems, 6 * g + j, (cx, cy, c)).wait_recv()
                cp = _remote(out_ref.at[kj, c], out_ref.at[kj, c], send_sems, recv_sems, 6 * g + 3 + j, sibling)
                cp.start()
                passed.append(cp)
        for j, (cx, cy) in enumerate(chips):
            kj = 2 * cx + cy
            for g, (w_ref, out_ref) in enumerate(zip(w_refs, out_refs)):
                _remote(w_ref.at[c], out_ref.at[kj, 1 - c], send_sems, recv_sems, 6 * g + 3 + j, sibling).wait_recv()
        for cp in first + passed:
            cp.wait_send()

    return pl.pallas_call(
        body, name="all_gather_weights", in_specs=[_HBM] * n, out_specs=[_HBM] * n,
        out_shape=[jax.ShapeDtypeStruct((N_CHIPS,) + w.shape, w.dtype) for w in ws],
        scratch_shapes=[pltpu.SemaphoreType.DMA((6 * n,)), pltpu.SemaphoreType.DMA((6 * n,))])(*ws)


def _swap_halves(pgs, name="exchange_halves"):
    n = len(pgs)

    def body(*refs):
        pg_refs, out_refs, (send_sems, recv_sems) = refs[:n], refs[n:2 * n], refs[2 * n:]
        x, y, c, _ = _place()
        cps = [_remote(pg_ref.at[j, 1 - c], out_ref.at[j], send_sems, recv_sems, N_CHIPS * g + j, (x, y, 1 - c))
               for g, (pg_ref, out_ref) in enumerate(zip(pg_refs, out_refs)) for j in range(N_CHIPS)]
        for cp in cps:
            cp.start()
        for cp in cps:
            cp.wait()

    return pl.pallas_call(
        body, name=name, in_specs=[_HBM] * n, out_specs=[_HBM] * n,
        out_shape=[jax.ShapeDtypeStruct((N_CHIPS,) + pg.shape[2:], pg.dtype) for pg in pgs],
        scratch_shapes=[pltpu.SemaphoreType.DMA((N_CHIPS * n,)), pltpu.SemaphoreType.DMA((N_CHIPS * n,))])(*pgs)


def _scatter_sums(pss):
    n = len(pss)

    def body(*refs):
        ps_refs, out_refs, (send_sems, recv_sems) = refs[:n], refs[n:2 * n], refs[2 * n:]
        x, y, c, chips = _place()
        cps = [_remote(ps_ref.at[2 * cx + cy], out_ref.at[j], send_sems, recv_sems, 3 * g + j, (cx, cy, c))
               for j, (cx, cy) in enumerate(chips) for g, (ps_ref, out_ref) in enumerate(zip(ps_refs, out_refs))]
        for cp in cps:
            cp.start()
        for cp in cps:
            cp.wait()

    return pl.pallas_call(
        body, name="scatter_chip_sums", in_specs=[_HBM] * n, out_specs=[_HBM] * n,
        out_shape=[jax.ShapeDtypeStruct((3,) + ps.shape[1:], ps.dtype) for ps in pss],
        scratch_shapes=[pltpu.SemaphoreType.DMA((3 * n,)), pltpu.SemaphoreType.DMA((3 * n,))])(*pss)


def _share_halves(gs):
    n = len(gs)

    def body(*refs):
        g_refs, out_refs, (send_sems, recv_sems) = refs[:n], refs[n:2 * n], refs[2 * n:]
        x, y, c, _ = _place()
        cps = [_remote(g_ref, out_ref.at[c], send_sems, recv_sems, g, (x, y, 1 - c))
               for g, (g_ref, out_ref) in enumerate(zip(g_refs, out_refs))]
        for cp in cps:
            cp.start()
        for g, (g_ref, out_ref) in enumerate(zip(g_refs, out_refs)):
            _remote(g_ref, out_ref.at[1 - c], send_sems, recv_sems, g, (x, y, 1 - c)).wait_recv()
        for cp in cps:
            cp.wait_send()

    return pl.pallas_call(
        body, name="share_half", in_specs=[_HBM] * n, out_specs=[_HBM] * n,
        out_shape=[jax.ShapeDtypeStruct((2,) + g.shape, g.dtype) for g in gs],
        scratch_shapes=[pltpu.SemaphoreType.DMA((n,)), pltpu.SemaphoreType.DMA((n,))])(*gs)


def _pair_sum(name, pg, theirs, c, tile):
    _, _, rows, width = pg.shape

    def body(c_ref, a_ref, b_ref, o_ref):
        o_ref[...] = (a_ref[...] + b_ref[...]).astype(o_ref.dtype)

    return pl.pallas_call(
        body, name=name,
        grid_spec=pltpu.PrefetchScalarGridSpec(
            num_scalar_prefetch=1, grid=(N_CHIPS, rows // tile),
            in_specs=[pl.BlockSpec((None, None, tile, width), lambda j, i, c_ref: (j, c_ref[0], i, 0)),
                      pl.BlockSpec((None, tile, width), lambda j, i, c_ref: (j, i, 0))],
            out_specs=pl.BlockSpec((None, tile, width), lambda j, i, c_ref: (j, i, 0))),
        out_shape=jax.ShapeDtypeStruct((N_CHIPS, rows, width), WIRE_DTYPE),
        compiler_params=_params("arbitrary", "arbitrary"))(c.reshape(1), pg, theirs)


def _chip_sum(name, ps, others, k, tile):
    _, rows, width = ps.shape

    def body(k_ref, a_ref, b_ref, o_ref):
        o_ref[...] = ((a_ref[...].astype(F32) + b_ref[0].astype(F32)) + b_ref[1].astype(F32)) + b_ref[2].astype(F32)

    return pl.pallas_call(
        body, name=name,
        grid_spec=pltpu.PrefetchScalarGridSpec(
            num_scalar_prefetch=1, grid=(rows // tile,),
            in_specs=[pl.BlockSpec((None, tile, width), lambda i, k_ref: (k_ref[0], i, 0)),
                      pl.BlockSpec((3, tile, width), lambda i, k_ref: (0, i, 0))],
            out_specs=pl.BlockSpec((tile, width), lambda i, k_ref: (i, 0))),
        out_shape=jax.ShapeDtypeStruct((rows, width), F32),
        compiler_params=_params("arbitrary"))(k.reshape(1), ps, others)


class _StepComm:
    TILES = {"in": IN_TILE, "rest": REST_TILE}

    def __init__(self, rest_wire, chip, core):
        self.rest_wire, self.chip, self.core = rest_wire, chip, core
        self.sums, self.landed = {}, {}

    def gather_rest(self):
        wire = self.rest_wire

        def sends(ins, outs, send_sems, recv_sems):
            (w_ref,), (out_ref,) = ins, outs
            x, y, c, chips = _place()
            return [_remote(w_ref.at[c], out_ref.at[2 * x + y, c], send_sems, recv_sems, 4 * j + 2 * c + to,
                            (cx, cy, to)) for j, (cx, cy) in enumerate(chips) for to in (0, 1)]

        def recvs(ins, outs, send_sems, recv_sems):
            (w_ref,), (out_ref,) = ins, outs
            _, _, c, chips = _place()
            return [_remote(w_ref.at[c], out_ref.at[2 * cx + cy, by], send_sems, recv_sems, 4 * j + 2 * by + c,
                            (cx, cy, by)) for j, (cx, cy) in enumerate(chips) for by in (0, 1)]

        def start(*refs):
            for cp in sends(*refs):
                cp.start()

        def wait(*refs):
            for cp in recvs(*refs):
                cp.wait_recv()
            for cp in sends(*refs):
                cp.wait_send()

        return _Riding((wire,), (jax.ShapeDtypeStruct((N_CHIPS,) + wire.shape, wire.dtype),), 12, start, wait)

    def rest_weights(self, landed):
        full = lax.dynamic_update_slice(landed, self.rest_wire[None], (self.chip, 0, 0, 0))
        return _unpack_rest_full(full.reshape(N_CHIPS, REST_ROWS, PACK_W))

    def scatter(self, group, pg):
        (theirs,) = _swap_halves([pg], "exchange_halves_" + group)
        ps = _pair_sum("sum_pair_" + group, pg, theirs, self.core, self.TILES[group])
        self.sums[group] = ps

        def copies(ins, outs, send_sems, recv_sems):
            (ps_ref,), (out_ref,) = ins, outs
            _, _, c, chips = _place()
            return [_remote(ps_ref.at[2 * cx + cy], out_ref.at[j], send_sems, recv_sems, j, (cx, cy, c))
                    for j, (cx, cy) in enumerate(chips)]

        def start(*refs):
            for cp in copies(*refs):
                cp.start()

        def wait(*refs):
            for cp in copies(*refs):
                cp.wait()

        return _Riding((ps,), (jax.ShapeDtypeStruct((3,) + ps.shape[1:], ps.dtype),), 3, start, wait)

    def reduced(self, group):
        return _chip_sum("sum_chips_" + group, self.sums[group], self.landed[group], self.chip, self.TILES[group])


def _adamw(w, g, m, v):
    m = ADAM_B1 * m + (1.0 - ADAM_B1) * g
    v = ADAM_B2 * v + (1.0 - ADAM_B2) * (g * g)
    m_hat = m / (1.0 - ADAM_B1 ** ADAM_STEP)
    v_hat = v / (1.0 - ADAM_B2 ** ADAM_STEP)
    return -ADAM_LR * (m_hat / (jnp.sqrt(v_hat) + ADAM_EPS) + ADAM_WD * w), m, v


def _small_reduce_adamw(part, w, m, v):
    def body(part_ref, w_ref, m_ref, v_ref, g_ref, d_ref, nm_ref, nv_ref, all_ref, send_sems, recv_sems):
        x, y, c, chips = _place()
        me, sibling = (x, y, c), (x, y, 1 - c)

        def rows(px, py, pc):
            return all_ref.at[4 * px + 2 * py + pc]

        all_ref[4 * x + 2 * y + c] = part_ref[...]
        first = [_remote(part_ref, rows(*me), send_sems, recv_sems, 0, sibling)]
        first += [_remote(part_ref, rows(*me), send_sems, recv_sems, 1 + j, (cx, cy, c))
                  for j, (cx, cy) in enumerate(chips)]
        for cp in first:
            cp.start()
        passed = []
        for j, (cx, cy) in enumerate(chips):
            _remote(part_ref, rows(cx, cy, c), send_sems, recv_sems, 1 + j, me).wait_recv()
            cp = _remote(rows(cx, cy, c), rows(cx, cy, c), send_sems, recv_sems, 4 + j, sibling)
            cp.start()
            passed.append(cp)
        _remote(part_ref, rows(*sibling), send_sems, recv_sems, 0, me).wait_recv()
        for j, (cx, cy) in enumerate(chips):
            _remote(part_ref, rows(cx, cy, 1 - c), send_sems, recv_sems, 4 + j, me).wait_recv()
        for cp in first + passed:
            cp.wait_send()
        g = all_ref[0]
        for dev in range(1, N_DEV):
            g = g + all_ref[dev]
        delta, nm, nv = _adamw(w_ref[...], g, m_ref[...], v_ref[...])
        g_ref[...] = g
        d_ref[...] = delta
        nm_ref[...] = nm
        nv_ref[...] = nv

    whole = pl.BlockSpec(memory_space=pltpu.VMEM)
    shape = jax.ShapeDtypeStruct((SMALL_ROWS, PACK_W), F32)
    return pl.pallas_call(
        body, name="small_reduce_adamw", in_specs=[whole] * 4, out_specs=[whole] * 4, out_shape=[shape] * 4,
        scratch_shapes=[pltpu.VMEM((N_DEV, SMALL_ROWS, PACK_W), F32), pltpu.SemaphoreType.DMA((7,)),
                        pltpu.SemaphoreType.DMA((7,))],
        compiler_params=pltpu.CompilerParams(vmem_limit_bytes=VMEM_LIMIT))(part, w, m, v)


def kernel(x, p, norm_g, w_in, hg_lb, hg_norm_g, w_o_hg, s5_a_re, s5_a_im, s5_log_dt, s5_b_re, s5_b_im, s5_c_re, s5_c_im, s5_d, w_glu, b_glu, w_o_s5, w_out, ple_norm_g, w_ple, w_ple_gate, final_norm_g, loss_target, m_norm_g, m_w_in, m_hg_lb, m_hg_norm_g, m_w_o_hg, m_s5_a_re, m_s5_a_im, m_s5_log_dt, m_s5_b_re, m_s5_b_im, m_s5_c_re, m_s5_c_im, m_s5_d, m_w_glu, m_b_glu, m_w_o_s5, m_w_out, m_ple_norm_g, m_w_ple, m_w_ple_gate, m_final_norm_g, v_norm_g, v_w_in, v_hg_lb, v_hg_norm_g, v_w_o_hg, v_s5_a_re, v_s5_a_im, v_s5_log_dt, v_s5_b_re, v_s5_b_im, v_s5_c_re, v_s5_c_im, v_s5_d, v_w_glu, v_b_glu, v_w_o_s5, v_w_out, v_ple_norm_g, v_w_ple, v_w_ple_gate, v_final_norm_g):
    given = dict(locals())
    wts = {n: given[n] for n in WEIGHTS}
    mom = {n: given["m_" + n] for n in WEIGHTS}
    var = {n: given["v_" + n] for n in WEIGHTS}
    cx, cy, cc = lax.axis_index("x"), lax.axis_index("y"), lax.axis_index("c")
    chip = (2 * cx + cy).astype(jnp.int32)

    core = cc.astype(jnp.int32)
    rest_shard = _pack_rest({n: wts[n][0] for n in REST})
    in_wire = wts["w_in"][0].astype(MXU_DTYPE).reshape(2, D_MODEL // 2, IN_SHARD)
    (w_in_all,) = _gather_shards([in_wire])
    w_in_all = lax.dynamic_update_slice(w_in_all, in_wire[None], (chip, 0, 0, 0)).reshape(N_CHIPS, D_MODEL, IN_SHARD)
    comm = _StepComm(rest_shard.astype(MXU_DTYPE).reshape(2, REST_ROWS // 2, PACK_W), chip, core)

    t_len = x.shape[1]
    loss_row, grad_x, g_big, g_small = _local_step(x.reshape(t_len, D_MODEL), p.reshape(t_len, -1),
                                                   loss_target.reshape(t_len, D_MODEL), {"w_in": w_in_all},
                                                   {n: wts[n] for n in SMALL}, comm)

    zero = jnp.zeros((), F32)
    sg, sd, snm, snv = _small_reduce_adamw(_pack_small(g_small, loss_row[0, 0]),
                                           _pack_small({n: wts[n] for n in SMALL}, zero),
                                           _pack_small({n: mom[n] for n in SMALL}, zero),
                                           _pack_small({n: var[n] for n in SMALL}, zero))
    (sg, loss), (sd, _), (snm, _), (snv, _) = (_unpack_small(a) for a in (sg, sd, snm, snv))

    halves = [comm.reduced("in"), comm.reduced("rest")]
    g_in, g_rest = [lax.dynamic_update_slice(got, mine[None], (core, 0, 0))
                    for got, mine in zip(_share_halves(halves), halves)]
    g_in, g_rest = g_in.reshape(D_MODEL, IN_SHARD), g_rest.reshape(REST_ROWS, PACK_W)

    def adam_f(wv, gv, mv, vv):
        return _adamw(wv, gv, mv, vv)

    d_in, nm_in, nv_in = _rowwise("adamw_in", adam_f, D_MODEL, IN_TILE,
                                  [(wts["w_in"][0], IN_SHARD, 0), (g_in, IN_SHARD, 0), (mom["w_in"][0], IN_SHARD, 0),
                                   (var["w_in"][0], IN_SHARD, 0)], [], [(IN_SHARD, F32)] * 3)
    d_rest, nm_rest, nv_rest = _rowwise("adamw_rest", adam_f, REST_ROWS, REST_TILE,
                                        [(rest_shard, PACK_W, 0), (g_rest, PACK_W, 0),
                                         (_pack_rest({n: mom[n][0] for n in REST}), PACK_W, 0),
                                         (_pack_rest({n: var[n][0] for n in REST}), PACK_W, 0)], [],
                                        [(PACK_W, F32)] * 3)
    bg, bd, bnm, bnv = (dict(_unpack_rest(rest), w_in=a.reshape(1, D_MODEL, IN_SHARD))
                        for rest, a in ((g_rest, g_in), (d_rest, d_in), (nm_rest, nm_in), (nv_rest, nv_in)))

    outs = [loss, grad_x.reshape(x.shape)]
    for small, big in ((sg, bg), (sd, bd), (snm, bnm), (snv, bnv)):
        outs += [big[n] if n in BIG else small[n] for n in WEIGHTS]
    return tuple(outs)
```

```python
import functools
from typing import Callable, NamedTuple

import jax
import jax.numpy as jnp
from jax import lax
from jax.experimental import pallas as pl
from jax.experimental.pallas import tpu as pltpu

F32 = jnp.float32
MXU_DTYPE = jnp.bfloat16
WIRE_DTYPE = jnp.bfloat16
NORM_EPS = 1e-6
D_MODEL = 1024
HG_HEADS = 8
HG_DIM = 128
HG_CHUNK = 64
S5_WIDTH = 512
S5_GROUPS = 32
S5_GROUP = 16
S5_STATE = 64
S5_LANES = S5_GROUPS * S5_STATE
IN_COLS = 7168
SUBLANES = 8
VMEM_LIMIT = 56 * 1024 * 1024
HIGHEST = lax.Precision.HIGHEST
MESH = pl.DeviceIdType.MESH

ADAM_LR, ADAM_B1, ADAM_B2, ADAM_EPS, ADAM_WD, ADAM_STEP = 0.001, 0.9, 0.999, 1e-08, 0.01, 10

BIG = ("w_in", "w_o_hg", "w_glu", "w_o_s5", "w_out", "w_ple", "w_ple_gate")
BIG_SHAPE = {"w_in": (1024, 7168), "w_o_hg": (1024, 1024), "w_glu": (512, 1024), "w_o_s5": (512, 1024),
             "w_out": (1024, 1024), "w_ple": (256, 1024), "w_ple_gate": (1024, 1024)}
BIG_COL_SHARDED = ("w_in", "w_glu", "w_o_s5", "w_ple")
SMALL = ("norm_g", "hg_lb", "hg_norm_g", "s5_a_re", "s5_a_im", "s5_log_dt", "s5_b_re", "s5_b_im", "s5_c_re",
         "s5_c_im", "s5_d", "b_glu", "ple_norm_g", "final_norm_g")
SMALL_SHAPE = {"norm_g": (1, 1024), "hg_lb": (2, 1024), "hg_norm_g": (1, 1024), "s5_a_re": (1, 32, 64),
               "s5_a_im": (1, 32, 64), "s5_log_dt": (1, 32), "s5_b_re": (1, 32, 64, 16), "s5_b_im": (1, 32, 64, 16),
               "s5_c_re": (1, 32, 16, 64), "s5_c_im": (1, 32, 16, 64), "s5_d": (1, 32, 16), "b_glu": (1, 1024),
               "ple_norm_g": (1, 1024), "final_norm_g": (1024,)}
WEIGHTS = ("norm_g", "w_in", "hg_lb", "hg_norm_g", "w_o_hg", "s5_a_re", "s5_a_im", "s5_log_dt", "s5_b_re", "s5_b_im",
           "s5_c_re", "s5_c_im", "s5_d", "w_glu", "b_glu", "w_o_s5", "w_out", "ple_norm_g", "w_ple", "w_ple_gate",
           "final_norm_g")
N_CHIPS = 4
N_DEV = 8
PACK_W = 1024
SHARD_ROWS = sum(BIG_SHAPE[n][0] * BIG_SHAPE[n][1] for n in BIG) // (N_CHIPS * PACK_W)
HALF_ROWS = SHARD_ROWS // 2
SMALL_ROWS = 144


def _params(*sem):
    return pltpu.CompilerParams(dimension_semantics=sem, vmem_limit_bytes=VMEM_LIMIT)


def _sig(x):
    return 1.0 / (1.0 + jnp.exp(-x))


def _dsilu(z, s):
    return s * (1.0 + z * (1.0 - s))


def _mx(x):
    return x.astype(MXU_DTYPE)


def _dot(a, b, dims=(((1,), (0,)), ((), ()))):
    return lax.dot_general(_mx(a), _mx(b), dims, preferred_element_type=F32)


_NT = (((1,), (1,)), ((), ()))
_TN = (((0,), (0,)), ((), ()))


def _dot32(a, b):
    return jnp.dot(a, b, precision=HIGHEST, preferred_element_type=F32)


def _rms_bwd(dy, x, g):
    r = lax.rsqrt(jnp.mean(x * x, axis=-1, keepdims=True) + NORM_EPS)
    t = dy * g
    dx = r * t - x * (r * r * r) * jnp.mean(t * x, axis=-1, keepdims=True)
    return dx, jnp.sum(dy * x * r, axis=0, keepdims=True)


def _rowwise(name, fn, n_rows_total, tm, rows, consts, outs, accs=(), alias=None):
    n_r, n_c, n_o, n_a = len(rows), len(consts), len(outs), len(accs)

    def body(*refs):
        row_refs = refs[:n_r]
        const_refs = refs[n_r:n_r + n_c]
        pos = n_r + n_c + (1 if alias is not None else 0)
        out_refs = refs[pos:pos + n_o]
        acc_refs = refs[pos + n_o:pos + n_o + n_a]
        res = fn(*[r[...] for r in row_refs], *[r[...] for r in const_refs])
        for r, v in zip(out_refs, res[:n_o]):
            r[...] = v.astype(r.dtype)
        if n_a:
            @pl.when(pl.program_id(0) == 0)
            def _():
                for r in acc_refs:
                    r[...] = jnp.zeros_like(r)
            for r, v in zip(acc_refs, res[n_o:]):
                r[...] += v

    in_specs = [pl.BlockSpec((tm, w), functools.partial(lambda i, cb: (i, cb), cb=cb)) for (_, w, cb) in rows]
    in_specs += [pl.BlockSpec(c.shape, lambda i: (0, 0)) for c in consts]
    args = [a for (a, _, _) in rows] + list(consts)
    out_shape, out_specs = [], []
    for o in outs:
        w, dt = o[0], o[1]
        cb, total = (o[2], o[3]) if len(o) == 4 else (0, w)
        out_shape.append(jax.ShapeDtypeStruct((n_rows_total, total), dt))
        out_specs.append(pl.BlockSpec((tm, w), functools.partial(lambda i, cb: (i, cb), cb=cb)))
    io_alias = {}
    if alias is not None:
        in_specs.append(pl.BlockSpec(memory_space=pl.ANY))
        args.append(alias[0])
        io_alias = {len(args) - 1: alias[1]}
    for (r, w) in accs:
        out_shape.append(jax.ShapeDtypeStruct((r, w), F32))
        out_specs.append(pl.BlockSpec((r, w), lambda i: (0, 0)))
    res = pl.pallas_call(body, name=name, grid=(n_rows_total // tm,), in_specs=in_specs, out_specs=out_specs,
                         out_shape=out_shape, input_output_aliases=io_alias,
                         compiler_params=_params("arbitrary"))(*args)
    return res


class _Riding(NamedTuple):
    ins: tuple
    outs: tuple
    n_sems: int
    start: Callable
    wait: Callable


_HBM = pl.BlockSpec(memory_space=pl.ANY)


def _ride(riding, refs, n_in, n_out, n_scratch, first, last):
    if riding is None:
        return refs[:n_in], refs[n_in:n_in + n_out], refs[n_in + n_out:]
    r_in, r_out = len(riding.ins), len(riding.outs)
    ins, rins = refs[:n_in], refs[n_in:n_in + r_in]
    pos = n_in + r_in
    outs, routs = refs[pos:pos + n_out], refs[pos + n_out:pos + n_out + r_out]
    pos += n_out + r_out
    scratch, (send_sems, recv_sems) = refs[pos:pos + n_scratch], refs[pos + n_scratch:]

    @pl.when(first)
    def _():
        riding.start(rins, routs, send_sems, recv_sems)

    @pl.when(last)
    def _():
        riding.wait(rins, routs, send_sems, recv_sems)

    return ins, outs, scratch


def _riding_call(riding, body, name, grid, in_specs, args, out_specs, out_shape, scratch, io_alias=None):
    if riding is not None:
        in_specs = list(in_specs) + [_HBM] * len(riding.ins)
        args = list(args) + list(riding.ins)
        out_specs = list(out_specs) + [_HBM] * len(riding.outs)
        out_shape = list(out_shape) + list(riding.outs)
        scratch = list(scratch) + [pltpu.SemaphoreType.DMA((riding.n_sems,))] * 2
    return pl.pallas_call(body, name=name, grid=grid, in_specs=in_specs, out_specs=out_specs, out_shape=out_shape,
                          scratch_shapes=scratch, input_output_aliases=io_alias or {},
                          compiler_params=_params("arbitrary", "arbitrary"))(*args)


def _mm_nn(name, a, b, tm, tn, riding=None):
    m, k = a.shape
    n = b.shape[1] if b.ndim == 2 else b.shape[0] * b.shape[2]
    grid = (n // tn, m // tm)

    def body(*refs):
        j, i = pl.program_id(0), pl.program_id(1)
        (a_ref, b_ref), (o_ref,), _ = _ride(riding, refs, 2, 1, 0, (j == 0) & (i == 0),
                                            (j == grid[0] - 1) & (i == grid[1] - 1))
        o_ref[...] = _dot(a_ref[...], b_ref[...])

    b_spec = (pl.BlockSpec((k, tn), lambda j, i: (0, j)) if b.ndim == 2
              else pl.BlockSpec((None, k, tn), lambda j, i: (j, 0, 0)))
    res = _riding_call(riding, body, name, grid, [pl.BlockSpec((tm, k), lambda j, i: (i, 0)), b_spec], [a, b],
                       [pl.BlockSpec((tm, tn), lambda j, i: (i, j))], [jax.ShapeDtypeStruct((m, n), F32)], [])
    return res[0] if riding is None else res


def _mm_nt(name, a, b, tm, tn):
    m, n = a.shape
    k = b.shape[0]
    steps = n // tn

    def body(a_ref, b_ref, o_ref, acc_ref):
        s = pl.program_id(1)

        @pl.when(s == 0)
        def _():
            acc_ref[...] = jnp.zeros_like(acc_ref)

        acc_ref[...] += _dot(a_ref[...], b_ref[...], _NT)

        @pl.when(s == steps - 1)
        def _():
            o_ref[...] = acc_ref[...]

    return pl.pallas_call(body, name=name, grid=(m // tm, steps),
                          in_specs=[pl.BlockSpec((tm, tn), lambda i, s: (i, s)),
                                    pl.BlockSpec((k, tn), lambda i, s: (0, s))],
                          out_specs=pl.BlockSpec((tm, k), lambda i, s: (i, 0)),
                          out_shape=jax.ShapeDtypeStruct((m, k), F32),
                          scratch_shapes=[pltpu.VMEM((tm, k), F32)],
                          compiler_params=_params("arbitrary", "arbitrary"))(a, b)


def _mm_nt_then(name, a, b, tm, tn, fn, rows, consts, outs, accs=(), alias=None, riding=None):
    m, n = a.shape
    k = b.shape[-2]
    steps = n // tn
    n_r, n_c, n_o, n_a = len(rows), len(consts), len(outs), len(accs)

    def body(*refs):
        a_ref, b_ref = refs[:2]
        row_refs = refs[2:2 + n_r]
        const_refs = refs[2 + n_r:2 + n_r + n_c]
        i, s = pl.program_id(0), pl.program_id(1)
        n_in = 2 + n_r + n_c + (1 if alias is not None else 0)
        _, outs_, (mm_ref,) = _ride(riding, refs, n_in, n_o + n_a, 1, (i == 0) & (s == 0),
                                    (i == m // tm - 1) & (s == steps - 1))
        out_refs, acc_refs = outs_[:n_o], outs_[n_o:]
        part = _dot(a_ref[...], b_ref[...], _NT)
        if steps > 1:
            @pl.when(s == 0)
            def _():
                mm_ref[...] = jnp.zeros_like(mm_ref)
            mm_ref[...] += part

        @pl.when(s == steps - 1)
        def _():
            res = fn(mm_ref[...] if steps > 1 else part, *[r[...] for r in row_refs], *[r[...] for r in const_refs])
            for r, v in zip(out_refs, res[:n_o]):
                r[...] = v.astype(r.dtype)
            if n_a:
                @pl.when(i == 0)
                def _():
                    for r in acc_refs:
                        r[...] = jnp.zeros_like(r)
                for r, v in zip(acc_refs, res[n_o:]):
                    r[...] += v

    b_spec = (pl.BlockSpec((k, tn), lambda i, s: (0, s)) if b.ndim == 2
              else pl.BlockSpec((None, k, tn), lambda i, s: (s, 0, 0)))
    in_specs = [pl.BlockSpec((tm, tn), lambda i, s: (i, s)), b_spec]
    in_specs += [pl.BlockSpec((tm, w), functools.partial(lambda i, s, cb: (i, cb), cb=cb)) for (_, w, cb) in rows]
    in_specs += [pl.BlockSpec(c.shape, lambda i, s: (0, 0)) for c in consts]
    args = [a, b] + [r[0] for r in rows] + list(consts)
    out_shape, out_specs = [], []
    for o in outs:
        w, dt = o[0], o[1]
        cb, total = (o[2], o[3]) if len(o) == 4 else (0, w)
        out_shape.append(jax.ShapeDtypeStruct((m, total), dt))
        out_specs.append(pl.BlockSpec((tm, w), functools.partial(lambda i, s, cb: (i, cb), cb=cb)))
    io_alias = {}
    if alias is not None:
        in_specs.append(pl.BlockSpec(memory_space=pl.ANY))
        args.append(alias[0])
        io_alias = {len(args) - 1: alias[1]}
    for (r, w) in accs:
        out_shape.append(jax.ShapeDtypeStruct((r, w), F32))
        out_specs.append(pl.BlockSpec((r, w), lambda i, s: (0, 0)))
    return _riding_call(riding, body, name, (m // tm, steps), in_specs, args, out_specs, out_shape,
                        [pltpu.VMEM((tm, k), F32)], io_alias)


def _mm_tn(name, a, b, tk, tn, col_shards=False, riding=None):
    t, k = a.shape
    n = b.shape[1]
    steps = t // tk

    def body(*refs):
        j, s = pl.program_id(0), pl.program_id(1)
        (a_ref, b_ref), (o_ref,), (acc_ref,) = _ride(riding, refs, 2, 1, 1, (j == 0) & (s == 0),
                                                     (j == n // tn - 1) & (s == steps - 1))

        @pl.when(s == 0)
        def _():
            acc_ref[...] = jnp.zeros_like(acc_ref)

        acc_ref[...] += _dot(a_ref[...], b_ref[...], _TN)

        @pl.when(s == steps - 1)
        def _():
            o_ref[...] = acc_ref[...]

    if col_shards:
        out_spec = pl.BlockSpec((None, k, tn), lambda j, s: (j, 0, 0))
        out_shape = jax.ShapeDtypeStruct((n // tn, k, tn), F32)
    else:
        out_spec = pl.BlockSpec((k, tn), lambda j, s: (0, j))
        out_shape = jax.ShapeDtypeStruct((k, n), F32)
    res = _riding_call(riding, body, name, (n // tn, steps),
                       [pl.BlockSpec((tk, k), lambda j, s: (s, 0)), pl.BlockSpec((tk, tn), lambda j, s: (s, j))],
                       [a, b], [out_spec], [out_shape], [pltpu.VMEM((k, tn), F32)])
    return res[0] if riding is None else res


def _hg_chunk_terms(q, f, lb):
    sig = _sig(f)
    fv = lb + (1.0 - lb) * sig
    kk = (1.0 - lb) * (1.0 - sig)
    row = lax.broadcasted_iota(jnp.int32, (HG_CHUNK, HG_CHUNK), 0)
    col = lax.broadcasted_iota(jnp.int32, (HG_CHUNK, HG_CHUNK), 1)
    b = _dot32((row >= col).astype(F32), jnp.log(fv))
    b_mid = b[HG_CHUNK // 2 - 1:HG_CHUNK // 2, :]
    b_last = b[HG_CHUNK - 1:HG_CHUNK, :]
    e_mid = jnp.exp(b - b_mid)
    e_mid_inv = jnp.exp(b_mid - b)
    e_b = jnp.exp(b)
    e_last = jnp.exp(b_last - b)
    return sig, fv, kk, row >= col, row <= col, q * e_mid, kk * e_mid_inv, e_mid, e_mid_inv, e_b, e_last, jnp.exp(b_last)


def _hgrn2_fwd(proj, hg_lb, hg_norm_g, t_len, tb):
    nck = tb // HG_CHUNK

    def body(p_ref, lb_ref, gn_ref, o_ref, act_ref, sp_ref, st_ref):
        @pl.when(pl.program_id(0) == 0)
        def _():
            st_ref[...] = jnp.zeros_like(st_ref)

        for c in range(nck):
            r = pl.ds(c * HG_CHUNK, HG_CHUNK)
            for h in range(HG_HEADS):
                hs = pl.ds(h * HG_DIM, HG_DIM)
                lb = _sig(lb_ref[0:1, hs] - lb_ref[1:2, hs])
                q = p_ref[r, pl.ds(h * HG_DIM, HG_DIM)]
                f = p_ref[r, pl.ds(1024 + h * HG_DIM, HG_DIM)]
                v = p_ref[r, pl.ds(2048 + h * HG_DIM, HG_DIM)]
                _, _, kk, causal, _, a, bm, _, _, e_b, e_last, dc = _hg_chunk_terms(q, f, lb)
                scores = jnp.where(causal, _dot(a, bm, _NT), 0.0)
                st = st_ref[h]
                o = _dot(scores, v) + _dot(q * e_b, st, _NT)
                sp_ref[h, c] = st
                st_ref[h] = dc * st + _dot(v, kk * e_last, _TN)
                o_ref[r, hs] = o

        for h in range(HG_HEADS):
            hs = pl.ds(h * HG_DIM, HG_DIM)
            o = o_ref[:, hs]
            rr = lax.rsqrt(jnp.mean(o * o, axis=-1, keepdims=True) + NORM_EPS)
            g = p_ref[:, pl.ds(3072 + h * HG_DIM, HG_DIM)]
            act_ref[:, hs] = (o * rr * gn_ref[:, hs] * (g * _sig(g))).astype(act_ref.dtype)

    nb = t_len // tb
    return pl.pallas_call(
        body, name="hgrn2_fwd", grid=(nb,),
        in_specs=[pl.BlockSpec((tb, 4096), lambda i: (i, 0)),
                  pl.BlockSpec((2, 1024), lambda i: (0, 0)),
                  pl.BlockSpec((1, 1024), lambda i: (0, 0))],
        out_specs=[pl.BlockSpec((tb, 1024), lambda i: (i, 0)),
                   pl.BlockSpec((tb, 1024), lambda i: (i, 0)),
                   pl.BlockSpec((HG_HEADS, nck, HG_DIM, HG_DIM), lambda i: (0, i, 0, 0))],
        out_shape=[jax.ShapeDtypeStruct((t_len, 1024), F32),
                   jax.ShapeDtypeStruct((t_len, 1024), MXU_DTYPE),
                   jax.ShapeDtypeStruct((HG_HEADS, t_len // HG_CHUNK, HG_DIM, HG_DIM), F32)],
        scratch_shapes=[pltpu.VMEM((HG_HEADS, HG_DIM, HG_DIM), F32)],
        compiler_params=_params("arbitrary"))(proj, hg_lb, hg_norm_g)


def _hgrn2_bwd(proj, d_o, s_prev, hg_lb, dproj, t_len, tb):
    nck = tb // HG_CHUNK
    nb = t_len // tb

    def body(p_ref, do_ref, sp_ref, lb_ref, _, dp_ref, dlb_ref, ds_ref, acc_ref):
        @pl.when(pl.program_id(0) == 0)
        def _():
            ds_ref[...] = jnp.zeros_like(ds_ref)
            acc_ref[...] = jnp.zeros_like(acc_ref)

        for c in reversed(range(nck)):
            r = pl.ds(c * HG_CHUNK, HG_CHUNK)
            for h in range(HG_HEADS):
                hs = pl.ds(h * HG_DIM, HG_DIM)
                lb = _sig(lb_ref[0:1, hs] - lb_ref[1:2, hs])
                q = p_ref[r, pl.ds(h * HG_DIM, HG_DIM)]
                f = p_ref[r, pl.ds(1024 + h * HG_DIM, HG_DIM)]
                v = p_ref[r, pl.ds(2048 + h * HG_DIM, HG_DIM)]
                do = do_ref[r, hs]
                sig, fv, kk, causal, anti, a, bm, e_mid, e_mid_inv, e_b, e_last, dc = _hg_chunk_terms(q, f, lb)
                qd = q * e_b
                kd = kk * e_last
                st = sp_ref[h, c]
                dst = ds_ref[h]
                scores = jnp.where(causal, _dot(a, bm, _NT), 0.0)
                dscores = jnp.where(causal, _dot(do, v, _NT), 0.0)
                dv = _dot(scores, do, _TN) + _dot(kd, dst, _NT)
                da = _dot(dscores, bm)
                dbm = _dot(dscores, a, _TN)
                dqd = _dot(do, st)
                dkd = _dot(v, dst)
                ddc = jnp.sum(dst * st, axis=0, keepdims=True)
                ds_ref[h] = _dot(do, qd, _TN) + dc * dst
                dq = da * e_mid + dqd * e_b
                dk = dbm * e_mid_inv + dkd * e_last
                db = da * a - dbm * bm + dqd * qd - dkd * kd
                extra = jnp.sum(dkd * kd, axis=0, keepdims=True) + ddc * dc
                dlogf = _dot32(anti.astype(F32), db) + extra
                dfv_k = dlogf / fv - dk
                dp_ref[r, pl.ds(h * HG_DIM, HG_DIM)] = dq
                dp_ref[r, pl.ds(1024 + h * HG_DIM, HG_DIM)] = dfv_k * (1.0 - lb) * sig * (1.0 - sig)
                dp_ref[r, pl.ds(2048 + h * HG_DIM, HG_DIM)] = dv
                acc_ref[:, hs] += jnp.sum(dfv_k * (1.0 - sig), axis=0, keepdims=True)

        @pl.when(pl.program_id(0) == nb - 1)
        def _():
            lb_all = _sig(lb_ref[0:1, :] - lb_ref[1:2, :])
            g0 = acc_ref[...] * lb_all * (1.0 - lb_all)
            dlb_ref[0:1, :] = g0
            dlb_ref[1:2, :] = -g0

    return pl.pallas_call(
        body, name="hgrn2_bwd", grid=(nb,),
        in_specs=[pl.BlockSpec((tb, 3072), lambda i: (nb - 1 - i, 0)),
                  pl.BlockSpec((tb, 1024), lambda i: (nb - 1 - i, 0)),
                  pl.BlockSpec((HG_HEADS, nck, HG_DIM, HG_DIM), lambda i: (0, nb - 1 - i, 0, 0)),
                  pl.BlockSpec((2, 1024), lambda i: (0, 0)),
                  pl.BlockSpec(memory_space=pl.ANY)],
        out_specs=[pl.BlockSpec((tb, 3072), lambda i: (nb - 1 - i, 0)),
                   pl.BlockSpec((2, 1024), lambda i: (0, 0))],
        out_shape=[jax.ShapeDtypeStruct((t_len, IN_COLS), F32), jax.ShapeDtypeStruct((2, 1024), F32)],
        scratch_shapes=[pltpu.VMEM((HG_HEADS, HG_DIM, HG_DIM), F32), pltpu.VMEM((1, 1024), F32)],
        input_output_aliases={4: 0},
        compiler_params=_params("arbitrary"))(proj, d_o, s_prev, hg_lb, dproj)


def _dot01(m01, x):
    m = m01.astype(MXU_DTYPE)
    hi = x.astype(MXU_DTYPE)
    r1 = x - hi.astype(F32)
    mid = r1.astype(MXU_DTYPE)
    lo = (r1 - mid.astype(F32)).astype(MXU_DTYPE)
    dot = lambda v: jnp.dot(m, v, preferred_element_type=F32)
    return dot(hi) + dot(mid) + dot(lo)


def _chunk_rows(x, offset, nck):
    return jnp.concatenate([jnp.broadcast_to(x[c * HG_CHUNK + offset:c * HG_CHUNK + offset + 1, :],
                                             (HG_CHUNK, x.shape[1])) for c in range(nck)], axis=0)


def _hg_block_terms(q, f, lb, tb):
    nck = tb // HG_CHUNK
    sig = _sig(f)
    fv = lb + (1.0 - lb) * sig
    kk = (1.0 - lb) * (1.0 - sig)
    row = lax.broadcasted_iota(jnp.int32, (tb, tb), 0)
    col = lax.broadcasted_iota(jnp.int32, (tb, tb), 1)
    same = jnp.right_shift(row, 6) == jnp.right_shift(col, 6)
    causal, anti = same & (row >= col), same & (row <= col)
    b = _dot01(causal, jnp.log(fv))
    b_mid, b_last = _chunk_rows(b, HG_CHUNK // 2 - 1, nck), _chunk_rows(b, HG_CHUNK - 1, nck)
    e_mid, e_mid_inv = jnp.exp(b - b_mid), jnp.exp(b_mid - b)
    e_b, e_last = jnp.exp(b), jnp.exp(b_last - b)
    dcs = [jnp.exp(b[c * HG_CHUNK + HG_CHUNK - 1:(c + 1) * HG_CHUNK, :]) for c in range(nck)]
    return sig, fv, kk, causal, anti, e_mid, e_mid_inv, e_b, e_last, dcs


def _hgrn2_fwd2(proj, hg_lb, hg_norm_g, t_len, tb):
    nck = tb // HG_CHUNK

    def body(p_ref, lb_ref, gn_ref, o_ref, act_ref, sp_ref, st_ref, a_s, bm_s, qd_s, kd_s, v_s):
        @pl.when(pl.program_id(0) == 0)
        def _():
            st_ref[...] = jnp.zeros_like(st_ref)

        lb = _sig(lb_ref[0:1, :] - lb_ref[1:2, :])
        q = p_ref[:, pl.ds(0, 1024)]
        _, _, kk, causal, _, e_mid, e_mid_inv, e_b, e_last, dcs = _hg_block_terms(q, p_ref[:, pl.ds(1024, 1024)],
                                                                                   lb, tb)
        a_s[...] = _mx(q * e_mid)
        bm_s[...] = _mx(kk * e_mid_inv)
        qd_s[...] = _mx(q * e_b)
        kd_s[...] = _mx(kk * e_last)
        v_s[...] = _mx(p_ref[:, pl.ds(2048, 1024)])
        for h in range(HG_HEADS):
            hs = pl.ds(h * HG_DIM, HG_DIM)
            scores = jnp.where(causal, _dot(a_s[:, hs], bm_s[:, hs], _NT), 0.0)
            o_ref[:, hs] = _dot(scores, v_s[:, hs])
        for h in range(HG_HEADS):
            hs = pl.ds(h * HG_DIM, HG_DIM)
            incs = [_dot(v_s[pl.ds(c * HG_CHUNK, HG_CHUNK), hs], kd_s[pl.ds(c * HG_CHUNK, HG_CHUNK), hs], _TN)
                    for c in range(nck)]
            st = st_ref[h]
            for c in range(nck):
                sp_ref[h, c] = st
                st = dcs[c][:, h * HG_DIM:(h + 1) * HG_DIM] * st + incs[c]
            st_ref[h] = st
        for h in range(HG_HEADS):
            hs = pl.ds(h * HG_DIM, HG_DIM)
            for c in range(nck):
                r = pl.ds(c * HG_CHUNK, HG_CHUNK)
                o_ref[r, hs] += _dot(qd_s[r, hs], sp_ref[h, c], _NT)
        for h in range(HG_HEADS):
            hs = pl.ds(h * HG_DIM, HG_DIM)
            o = o_ref[:, hs]
            rr = lax.rsqrt(jnp.mean(o * o, axis=-1, keepdims=True) + NORM_EPS)
            g = p_ref[:, pl.ds(3072 + h * HG_DIM, HG_DIM)]
            act_ref[:, hs] = (o * rr * gn_ref[:, hs] * (g * _sig(g))).astype(act_ref.dtype)

    nb = t_len // tb
    return pl.pallas_call(
        body, name="hgrn2_fwd", grid=(nb,),
        in_specs=[pl.BlockSpec((tb, 4096), lambda i: (i, 0)),
                  pl.BlockSpec((2, 1024), lambda i: (0, 0)),
                  pl.BlockSpec((1, 1024), lambda i: (0, 0))],
        out_specs=[pl.BlockSpec((tb, 1024), lambda i: (i, 0)),
                   pl.BlockSpec((tb, 1024), lambda i: (i, 0)),
                   pl.BlockSpec((HG_HEADS, nck, HG_DIM, HG_DIM), lambda i: (0, i, 0, 0))],
        out_shape=[jax.ShapeDtypeStruct((t_len, 1024), F32),
                   jax.ShapeDtypeStruct((t_len, 1024), MXU_DTYPE),
                   jax.ShapeDtypeStruct((HG_HEADS, t_len // HG_CHUNK, HG_DIM, HG_DIM), F32)],
        scratch_shapes=[pltpu.VMEM((HG_HEADS, HG_DIM, HG_DIM), F32)] + [pltpu.VMEM((tb, 1024), MXU_DTYPE)] * 5,
        compiler_params=_params("arbitrary"))(proj, hg_lb, hg_norm_g)


def _hgrn2_bwd2(proj, d_o, s_prev, hg_lb, dproj, t_len, tb):
    nck = tb // HG_CHUNK
    nb = t_len // tb

    def body(p_ref, do_ref, sp_ref, lb_ref, _, dp_ref, dlb_ref, ds_ref, acc_ref,
             a_s, bm_s, qd_s, kd_s, v_s, do_s, da_s, dbm_s, dqd_s, dkd_s, ex_s):
        @pl.when(pl.program_id(0) == 0)
        def _():
            ds_ref[...] = jnp.zeros_like(ds_ref)
            acc_ref[...] = jnp.zeros_like(acc_ref)

        lb = _sig(lb_ref[0:1, :] - lb_ref[1:2, :])
        q = p_ref[:, pl.ds(0, 1024)]
        sig, fv, kk, causal, anti, e_mid, e_mid_inv, e_b, e_last, dcs = _hg_block_terms(
            q, p_ref[:, pl.ds(1024, 1024)], lb, tb)
        a, bm, qd, kd = q * e_mid, kk * e_mid_inv, q * e_b, kk * e_last
        a_s[...] = _mx(a)
        bm_s[...] = _mx(bm)
        qd_s[...] = _mx(qd)
        kd_s[...] = _mx(kd)
        v_s[...] = _mx(p_ref[:, pl.ds(2048, 1024)])
        do_s[...] = _mx(do_ref[...])
        for h in range(HG_HEADS):
            hs = pl.ds(h * HG_DIM, HG_DIM)
            scores = jnp.where(causal, _dot(a_s[:, hs], bm_s[:, hs], _NT), 0.0)
            dscores = _mx(jnp.where(causal, _dot(do_s[:, hs], v_s[:, hs], _NT), 0.0))
            dp_ref[:, pl.ds(2048 + h * HG_DIM, HG_DIM)] = _dot(scores, do_s[:, hs], _TN)
            da_s[:, hs] = _dot(dscores, bm_s[:, hs])
            dbm_s[:, hs] = _dot(dscores, a_s[:, hs], _TN)
        for h in range(HG_HEADS):
            hs = pl.ds(h * HG_DIM, HG_DIM)
            ups = [_dot(do_s[pl.ds(c * HG_CHUNK, HG_CHUNK), hs], qd_s[pl.ds(c * HG_CHUNK, HG_CHUNK), hs], _TN)
                   for c in range(nck)]
            dst = ds_ref[h]
            for c in reversed(range(nck)):
                r = pl.ds(c * HG_CHUNK, HG_CHUNK)
                st = sp_ref[h, c]
                dc = dcs[c][:, h * HG_DIM:(h + 1) * HG_DIM]
                dp_ref[r, pl.ds(2048 + h * HG_DIM, HG_DIM)] += _dot(kd_s[r, hs], dst, _NT)
                dqd_s[r, hs] = _dot(do_s[r, hs], st)
                dkd_s[r, hs] = _dot(v_s[r, hs], dst)
                ex_s[c:c + 1, hs] = jnp.sum(dst * st, axis=0, keepdims=True) * dc
                dst = ups[c] + dc * dst
            ds_ref[h] = dst
        da, dbm, dqd, dkd = da_s[...], dbm_s[...], dqd_s[...], dkd_s[...]
        dq = da * e_mid + dqd * e_b
        dk = dbm * e_mid_inv + dkd * e_last
        db = da * a - dbm * bm + dqd * qd - dkd * kd
        dkk = dkd * kd
        extra = jnp.concatenate(
            [jnp.broadcast_to(jnp.sum(dkk[c * HG_CHUNK:(c + 1) * HG_CHUNK], axis=0, keepdims=True)
                              + ex_s[c:c + 1, :], (HG_CHUNK, 1024)) for c in range(nck)], axis=0)
        dlogf = _dot01(anti, db) + extra
        dfv_k = dlogf / fv - dk
        dp_ref[:, pl.ds(0, 1024)] = dq
        dp_ref[:, pl.ds(1024, 1024)] = dfv_k * (1.0 - lb) * sig * (1.0 - sig)
        acc_ref[...] += jnp.sum(dfv_k * (1.0 - sig), axis=0, keepdims=True)

        @pl.when(pl.program_id(0) == nb - 1)
        def _():
            g0 = acc_ref[...] * lb * (1.0 - lb)
            dlb_ref[0:1, :] = g0
            dlb_ref[1:2, :] = -g0

    return pl.pallas_call(
        body, name="hgrn2_bwd", grid=(nb,),
        in_specs=[pl.BlockSpec((tb, 3072), lambda i: (nb - 1 - i, 0)),
                  pl.BlockSpec((tb, 1024), lambda i: (nb - 1 - i, 0)),
                  pl.BlockSpec((HG_HEADS, nck, HG_DIM, HG_DIM), lambda i: (0, nb - 1 - i, 0, 0)),
                  pl.BlockSpec((2, 1024), lambda i: (0, 0)),
                  pl.BlockSpec(memory_space=pl.ANY)],
        out_specs=[pl.BlockSpec((tb, 3072), lambda i: (nb - 1 - i, 0)),
                   pl.BlockSpec((2, 1024), lambda i: (0, 0))],
        out_shape=[jax.ShapeDtypeStruct((t_len, IN_COLS), F32), jax.ShapeDtypeStruct((2, 1024), F32)],
        scratch_shapes=[pltpu.VMEM((HG_HEADS, HG_DIM, HG_DIM), F32), pltpu.VMEM((1, 1024), F32)]
                       + [pltpu.VMEM((tb, 1024), MXU_DTYPE)] * 6 + [pltpu.VMEM((tb, 1024), F32)] * 4
                       + [pltpu.VMEM((SUBLANES, 1024), F32)],
        input_output_aliases={4: 0},
        compiler_params=_params("arbitrary"))(proj, d_o, s_prev, hg_lb, dproj)


def _s5_prep(a_re, a_im, log_dt, b_re_t, b_im_t):
    def body(ar_ref, ai_ref, ldt_ref, br_ref, bi_ref, lam_ref, pr_ref, pi_ref, bbr_ref, bbi_ref):
        ar, ai = ar_ref[...], ai_ref[...]
        dt = jnp.exp(ldt_ref[...])
        mag = jnp.exp(ar * dt)
        lr, li = mag * jnp.cos(ai * dt), mag * jnp.sin(ai * dt)
        den = ar * ar + ai * ai
        nr = lr - 1.0
        sr = (nr * ar + li * ai) / den
        si = (li * ar - nr * ai) / den
        lam_ref[0:1, :] = lr
        lam_ref[1:2, :] = li
        cr, ci = lr, li
        for i in range(SUBLANES):
            pr_ref[i:i + 1, :] = cr
            pi_ref[i:i + 1, :] = ci
            cr, ci = cr * lr - ci * li, cr * li + ci * lr
        br, bi = br_ref[...], bi_ref[...]
        bbr_ref[...] = sr * br - si * bi
        bbi_ref[...] = sr * bi + si * br

    whole = pl.BlockSpec(memory_space=pltpu.VMEM)
    return pl.pallas_call(
        body, name="s5_prep", in_specs=[whole] * 5, out_specs=[whole] * 5,
        out_shape=[jax.ShapeDtypeStruct((2, S5_LANES), F32), jax.ShapeDtypeStruct((SUBLANES, S5_LANES), F32),
                   jax.ShapeDtypeStruct((SUBLANES, S5_LANES), F32), jax.ShapeDtypeStruct((S5_GROUP, S5_LANES), F32),
                   jax.ShapeDtypeStruct((S5_GROUP, S5_LANES), F32)])(a_re, a_im, log_dt, b_re_t, b_im_t)


def _s5_prep_bwd(a_re, a_im, log_dt, b_re_t, b_im_t, dlam, dbbr, dbbi):
    def body(ar_ref, ai_ref, ldt_ref, br_ref, bi_ref, dlam_ref, dbbr_ref, dbbi_ref,
             dar_ref, dai_ref, dldt_ref, dbr_ref, dbi_ref):
        ar, ai = ar_ref[...], ai_ref[...]
        dt = jnp.exp(ldt_ref[...])
        mag = jnp.exp(ar * dt)
        cs, sn = jnp.cos(ai * dt), jnp.sin(ai * dt)
        lr, li = mag * cs, mag * sn
        den = ar * ar + ai * ai
        nr = lr - 1.0
        sr = (nr * ar + li * ai) / den
        si = (li * ar - nr * ai) / den
        br, bi = br_ref[...], bi_ref[...]
        gbr, gbi = dbbr_ref[...], dbbi_ref[...]
        dbr_ref[...] = sr * gbr + si * gbi
        dbi_ref[...] = sr * gbi - si * gbr
        dsr = jnp.sum(gbr * br + gbi * bi, axis=0, keepdims=True)
        dsi = jnp.sum(gbi * br - gbr * bi, axis=0, keepdims=True)
        dnr = (dsr * ar - dsi * ai) / den
        dli = dlam_ref[1:2, :] + (dsr * ai + dsi * ar) / den
        dlr = dlam_ref[0:1, :] + dnr
        dden = -(dsr * sr + dsi * si) / den
        dar = (dsr * nr + dsi * li) / den + dden * 2.0 * ar
        dai = (dsr * li - dsi * nr) / den + dden * 2.0 * ai
        dmag = dlr * cs + dli * sn
        dth = mag * (dli * cs - dlr * sn)
        dar_ref[...] = dar + dmag * mag * dt
        dai_ref[...] = dai + dth * dt
        ddt = (dmag * mag * ar + dth * ai) * dt
        lane = lax.broadcasted_iota(jnp.int32, (S5_LANES, 128), 0) // S5_STATE
        grp = lax.broadcasted_iota(jnp.int32, (S5_LANES, 128), 1)
        dldt_ref[...] = _dot32(jnp.broadcast_to(ddt, (SUBLANES, S5_LANES)), (lane == grp).astype(F32))

    whole = pl.BlockSpec(memory_space=pltpu.VMEM)
    return pl.pallas_call(
        body, name="s5_prep_bwd", in_specs=[whole] * 8, out_specs=[whole] * 5,
        out_shape=[jax.ShapeDtypeStruct((1, S5_LANES), F32), jax.ShapeDtypeStruct((1, S5_LANES), F32),
                   jax.ShapeDtypeStruct((SUBLANES, 128), F32), jax.ShapeDtypeStruct((S5_GROUP, S5_LANES), F32),
                   jax.ShapeDtypeStruct((S5_GROUP, S5_LANES), F32)])(a_re, a_im, log_dt, b_re_t, b_im_t, dlam, dbbr,
                                                                      dbbi)


S5_LANE_CHUNK = 512


def _shift_rows(x, s, rowid):
    if s > 0:
        return jnp.where(rowid >= s, pltpu.roll(x, s, 0), 0.0)
    return jnp.where(rowid < SUBLANES + s, pltpu.roll(x, SUBLANES + s, 0), 0.0)


def _scan8(xr, xi, pr, pi, sign, rowid):
    for s, row in ((1, 0), (2, 1), (4, 3)):
        lr, li = pr[row:row + 1, :], pi[row:row + 1, :]
        sr, si = _shift_rows(xr, sign * s, rowid), _shift_rows(xi, sign * s, rowid)
        xr, xi = xr + lr * sr - li * si, xi + lr * si + li * sr
    return xr, xi


def _s5_fwd(proj, pw_re, pw_im, bbr_bd, bbi_bd, crt_bd, cit_bd, d_row, t_len, tb):
    ngrp = tb // SUBLANES

    def body(u_ref, pr_ref, pi_ref, bbr_ref, bbi_ref, crt_ref, cit_ref, d_ref,
             hr_ref, hi_ref, ypre_ref, ys_ref, cr_ref, ci_ref):
        @pl.when(pl.program_id(0) == 0)
        def _():
            cr_ref[...] = jnp.zeros_like(cr_ref)
            ci_ref[...] = jnp.zeros_like(ci_ref)

        u = u_ref[...]
        hr_ref[...] = _dot(u, bbr_ref[...])
        hi_ref[...] = _dot(u, bbi_ref[...])
        rowid = lax.broadcasted_iota(jnp.int32, (SUBLANES, S5_LANE_CHUNK), 0)
        for lc in range(S5_LANES // S5_LANE_CHUNK):
            ls = pl.ds(lc * S5_LANE_CHUNK, S5_LANE_CHUNK)
            pr, pi = pr_ref[:, ls], pi_ref[:, ls]

            def group(g, carry, ls=ls, pr=pr, pi=pi):
                cr, ci = carry
                r = pl.ds(pl.multiple_of(g * SUBLANES, SUBLANES), SUBLANES)
                xr, xi = _scan8(hr_ref[r, ls], hi_ref[r, ls], pr, pi, 1, rowid)
                xr, xi = xr + pr * cr - pi * ci, xi + pr * ci + pi * cr
                hr_ref[r, ls] = xr
                hi_ref[r, ls] = xi
                return xr[SUBLANES - 1:SUBLANES, :], xi[SUBLANES - 1:SUBLANES, :]

            cr, ci = lax.fori_loop(0, ngrp, group, (cr_ref[:, ls], ci_ref[:, ls]))
            cr_ref[:, ls] = cr
            ci_ref[:, ls] = ci
        y = _dot(hr_ref[...], crt_ref[...]) - _dot(hi_ref[...], cit_ref[...]) + d_ref[...] * u
        ypre_ref[...] = y
        ys_ref[...] = jax.nn.gelu(y, approximate=True).astype(ys_ref.dtype)

    whole = pl.BlockSpec(memory_space=pltpu.VMEM)
    return pl.pallas_call(
        body, name="s5_fwd", grid=(t_len // tb,),
        in_specs=[pl.BlockSpec((tb, S5_WIDTH), lambda i: (i, 4096 // S5_WIDTH))] + [whole] * 7,
        out_specs=[pl.BlockSpec((tb, S5_LANES), lambda i: (i, 0)), pl.BlockSpec((tb, S5_LANES), lambda i: (i, 0)),
                   pl.BlockSpec((tb, S5_WIDTH), lambda i: (i, 0)), pl.BlockSpec((tb, S5_WIDTH), lambda i: (i, 0))],
        out_shape=[jax.ShapeDtypeStruct((t_len, S5_LANES), F32), jax.ShapeDtypeStruct((t_len, S5_LANES), F32),
                   jax.ShapeDtypeStruct((t_len, S5_WIDTH), F32), jax.ShapeDtypeStruct((t_len, S5_WIDTH), MXU_DTYPE)],
        scratch_shapes=[pltpu.VMEM((1, S5_LANES), F32), pltpu.VMEM((1, S5_LANES), F32)],
        compiler_params=_params("arbitrary"))(proj, pw_re, pw_im, bbr_bd, bbi_bd, crt_bd, cit_bd, d_row)


def _dgelu(x):
    c, a = 0.7978845608028654, 0.044715
    th = jnp.tanh(c * (x + a * x * x * x))
    return 0.5 * (1.0 + th) + 0.5 * x * (1.0 - th * th) * c * (1.0 + 3.0 * a * x * x)


def _s5_bwd(dgelu, y_pre, proj, h_re, h_im, pwr_re, pwr_im, bbr_bd, bbi_bd, cr_bd, ci_bd, d_row, dproj, t_len, tb):
    ngrp = tb // SUBLANES
    nb = t_len // tb

    def body(dg_ref, yp_ref, u_ref, hr_ref, hi_ref, pr_ref, pi_ref, bbr_ref, bbi_ref, cr_ref, ci_ref, d_ref, _,
             du_ref, dbbr_ref, dbbi_ref, dcr_ref, dci_ref, dd_ref, dlam_ref,
             gr_ref, gi_ref, car_ref, cai_ref, abr_ref, abi_ref, acr_ref, aci_ref, ad_ref, alr_ref, ali_ref, sem):
        @pl.when(pl.program_id(0) == 0)
        def _():
            for ref in (car_ref, cai_ref, abr_ref, abi_ref, acr_ref, aci_ref, ad_ref, alr_ref, ali_ref):
                ref[...] = jnp.zeros_like(ref)

        u = u_ref[...]
        dy = dg_ref[...] * _dgelu(yp_ref[...])
        gr_ref[...] = _dot(dy, cr_ref[...])
        gi_ref[...] = -_dot(dy, ci_ref[...])
        rowid = lax.broadcasted_iota(jnp.int32, (SUBLANES, S5_LANE_CHUNK), 0)
        for lc in range(S5_LANES // S5_LANE_CHUNK):
            ls = pl.ds(lc * S5_LANE_CHUNK, S5_LANE_CHUNK)
            pr, pi = pr_ref[:, ls], pi_ref[:, ls]
            fwd_rows_r = jnp.concatenate([pr[7:8], pr[6:7], pr[6:7], pr[4:5]], axis=0)
            fwd_rows_i = jnp.concatenate([pi[7:8], pi[6:7], pi[6:7], pi[4:5]], axis=0)

            def group(j, carry, ls=ls, pr=pr, pi=pi, fr=fwd_rows_r, fi=fwd_rows_i):
                cr, ci, slr, sli = carry
                g = ngrp - 1 - j
                r = pl.ds(pl.multiple_of(g * SUBLANES, SUBLANES), SUBLANES)
                xr, xi = _scan8(gr_ref[r, ls], gi_ref[r, ls], fr, fi, -1, rowid)
                xr, xi = xr + pr * cr - pi * ci, xi + pr * ci + pi * cr
                gr_ref[r, ls] = xr
                gi_ref[r, ls] = xi
                nr = jnp.where(rowid == SUBLANES - 1, cr, pltpu.roll(xr, SUBLANES - 1, 0))
                ni = jnp.where(rowid == SUBLANES - 1, ci, pltpu.roll(xi, SUBLANES - 1, 0))
                hr, hi = hr_ref[r, ls], hi_ref[r, ls]
                slr = slr + nr * hr + ni * hi
                sli = sli + ni * hr - nr * hi
                return xr[0:1, :], xi[0:1, :], slr, sli

            zero = jnp.zeros((SUBLANES, S5_LANE_CHUNK), F32)
            cr, ci, slr, sli = lax.fori_loop(0, ngrp, group, (car_ref[:, ls], cai_ref[:, ls], zero, zero))
            car_ref[:, ls] = cr
            cai_ref[:, ls] = ci
            alr_ref[:, ls] += jnp.sum(slr, axis=0, keepdims=True)
            ali_ref[:, ls] += jnp.sum(sli, axis=0, keepdims=True)
        gr, gi = gr_ref[...], gi_ref[...]
        du_ref[...] = _dot(gr, bbr_ref[...], _NT) + _dot(gi, bbi_ref[...], _NT) + d_ref[...] * dy
        abr_ref[...] += _dot(u, gr, _TN)
        abi_ref[...] += _dot(u, gi, _TN)
        acr_ref[...] += _dot(hr_ref[...], dy, _TN)
        aci_ref[...] -= _dot(hi_ref[...], dy, _TN)
        ad_ref[...] += jnp.sum(dy * u, axis=0, keepdims=True)

        @pl.when(pl.program_id(0) == nb - 1)
        def _():
            dd_ref[...] = ad_ref[...]
            dlam_ref[0:1, :] = alr_ref[...]
            dlam_ref[1:2, :] = ali_ref[...]
            copies = [pltpu.make_async_copy(s, d, sem.at[k]) for k, (s, d) in enumerate(
                ((abr_ref, dbbr_ref), (abi_ref, dbbi_ref), (acr_ref, dcr_ref), (aci_ref, dci_ref)))]
            for cp in copies:
                cp.start()
            for cp in copies:
                cp.wait()

    whole = pl.BlockSpec(memory_space=pltpu.VMEM)
    hbm = pl.BlockSpec(memory_space=pl.ANY)
    rev = lambda i: (nb - 1 - i, 0)
    return pl.pallas_call(
        body, name="s5_bwd", grid=(nb,),
        in_specs=[pl.BlockSpec((tb, S5_WIDTH), rev), pl.BlockSpec((tb, S5_WIDTH), rev),
                  pl.BlockSpec((tb, S5_WIDTH), lambda i: (nb - 1 - i, 4096 // S5_WIDTH)),
                  pl.BlockSpec((tb, S5_LANES), rev), pl.BlockSpec((tb, S5_LANES), rev)] + [whole] * 7 + [hbm],
        out_specs=[pl.BlockSpec((tb, S5_WIDTH), lambda i: (nb - 1 - i, 4096 // S5_WIDTH)), hbm, hbm, hbm, hbm,
                   pl.BlockSpec((1, S5_WIDTH), lambda i: (0, 0)), pl.BlockSpec((2, S5_LANES), lambda i: (0, 0))],
        out_shape=[jax.ShapeDtypeStruct((t_len, IN_COLS), F32),
                   jax.ShapeDtypeStruct((S5_WIDTH, S5_LANES), F32), jax.ShapeDtypeStruct((S5_WIDTH, S5_LANES), F32),
                   jax.ShapeDtypeStruct((S5_LANES, S5_WIDTH), F32), jax.ShapeDtypeStruct((S5_LANES, S5_WIDTH), F32),
                   jax.ShapeDtypeStruct((1, S5_WIDTH), F32), jax.ShapeDtypeStruct((2, S5_LANES), F32)],
        scratch_shapes=[pltpu.VMEM((tb, S5_LANES), F32), pltpu.VMEM((tb, S5_LANES), F32),
                        pltpu.VMEM((1, S5_LANES), F32), pltpu.VMEM((1, S5_LANES), F32),
                        pltpu.VMEM((S5_WIDTH, S5_LANES), F32), pltpu.VMEM((S5_WIDTH, S5_LANES), F32),
                        pltpu.VMEM((S5_LANES, S5_WIDTH), F32), pltpu.VMEM((S5_LANES, S5_WIDTH), F32),
                        pltpu.VMEM((1, S5_WIDTH), F32), pltpu.VMEM((1, S5_LANES), F32),
                        pltpu.VMEM((1, S5_LANES), F32), pltpu.SemaphoreType.DMA((4,))],
        input_output_aliases={12: 0},
        compiler_params=_params("arbitrary"))(dgelu, y_pre, proj, h_re, h_im, pwr_re, pwr_im, bbr_bd, bbi_bd, cr_bd,
                                              ci_bd, d_row, dproj)


S5_BLOCKS = 4
S5_BW = S5_WIDTH // S5_BLOCKS
S5_BL = S5_LANES // S5_BLOCKS
S5_LANE_BLOCKS = S5_LANES // 128
S5_SCAN_BLOCKS = 4


def _s5_powers(a_re, a_im, log_dt, b_re_t, b_im_t, seg):
    def body(ar_ref, ai_ref, ldt_ref, br_ref, bi_ref, pr_ref, pi_ref, bbr_ref, bbi_ref):
        ar, ai = ar_ref[...], ai_ref[...]
        dt = jnp.exp(ldt_ref[...])
        mag = jnp.exp(ar * dt)
        lr, li = mag * jnp.cos(ai * dt), mag * jnp.sin(ai * dt)
        den = ar * ar + ai * ai
        nr = lr - 1.0
        sr = (nr * ar + li * ai) / den
        si = (li * ar - nr * ai) / den
        cr, ci = lr, li
        for i in range(seg):
            pr_ref[i:i + 1, :] = cr
            pi_ref[i:i + 1, :] = ci
            cr, ci = cr * lr - ci * li, cr * li + ci * lr
        br, bi = br_ref[...], bi_ref[...]
        bbr_ref[...] = sr * br - si * bi
        bbi_ref[...] = sr * bi + si * br

    whole = pl.BlockSpec(memory_space=pltpu.VMEM)
    return pl.pallas_call(
        body, name="s5_prep", in_specs=[whole] * 5, out_specs=[whole] * 4,
        out_shape=[jax.ShapeDtypeStruct((seg, S5_LANES), F32), jax.ShapeDtypeStruct((seg, S5_LANES), F32),
                   jax.ShapeDtypeStruct((S5_GROUP, S5_LANES), F32),
                   jax.ShapeDtypeStruct((S5_GROUP, S5_LANES), F32)])(a_re, a_im, log_dt, b_re_t, b_im_t)


def _scan_tables(pw_re, pw_im, reverse):
    seg = pw_re.shape[0]
    if reverse:
        pw_re, pw_im = pw_re[::-1], -pw_im[::-1]
        one, full = seg - 1, 0
    else:
        one, full = 0, seg - 1
    rows = jnp.stack([pw_re[one], pw_im[one], pw_re[full], pw_im[full]])
    wide = lambda t: jnp.broadcast_to(t[:, None, :], (seg, SUBLANES, S5_LANES))
    return rows, wide(pw_re), wide(pw_im)


def _lanes(j):
    return pl.ds(j * 128, 128)


def _segment_scan(xr_ref, xi_ref, lam_ref, car_ref, cai_ref, cn_r, cn_i, blocks, seg, reverse):
    shape = (SUBLANES, 128)
    lrs = [jnp.broadcast_to(lam_ref[0:1, _lanes(j)], shape) for j in blocks]
    lis = [jnp.broadcast_to(lam_ref[1:2, _lanes(j)], shape) for j in blocks]

    def step(k, carry):
        idx = pl.ds(seg - 1 - k if reverse else k, SUBLANES, stride=seg)
        out = []
        for n, j in enumerate(blocks):
            cr, ci = carry[2 * n], carry[2 * n + 1]
            nr = lrs[n] * cr - lis[n] * ci + xr_ref[j, idx, :]
            ni = lrs[n] * ci + lis[n] * cr + xi_ref[j, idx, :]
            xr_ref[j, idx, :] = nr
            xi_ref[j, idx, :] = ni
            out += [nr, ni]
        return tuple(out)

    zero = jnp.zeros(shape, F32)
    fin = lax.fori_loop(0, seg, step, (zero,) * (2 * len(blocks)), unroll=2)
    for n, j in enumerate(blocks):
        ls = _lanes(j)
        fr, fi = fin[2 * n], fin[2 * n + 1]
        sr, si = lam_ref[2:3, ls], lam_ref[3:4, ls]
        pr, pi = car_ref[:, ls], cai_ref[:, ls]
        for s in (reversed(range(SUBLANES)) if reverse else range(SUBLANES)):
            cn_r[s:s + 1, ls] = pr
            cn_i[s:s + 1, ls] = pi
            pr, pi = fr[s:s + 1, :] + sr * pr - si * pi, fi[s:s + 1, :] + sr * pi + si * pr
        car_ref[:, ls] = pr
        cai_ref[:, ls] = pi


def _s5_fwd2(proj, lam_rows, p3_re, p3_im, bbr4, bbi4, crt4, cit4, d_row, t_len, tb):
    seg = tb // SUBLANES

    def body(u_ref, lam_ref, p3r_ref, p3i_ref, bbr_ref, bbi_ref, crt_ref, cit_ref, d_ref,
             hr_ref, hi_ref, ypre_ref, ys_ref, car_ref, cai_ref, cn_r, cn_i):
        @pl.when(pl.program_id(0) == 0)
        def _():
            car_ref[...] = jnp.zeros_like(car_ref)
            cai_ref[...] = jnp.zeros_like(cai_ref)

        u = u_ref[...]
        for i in range(S5_BLOCKS):
            ui = u[:, i * S5_BW:(i + 1) * S5_BW]
            xr, xi = _dot(ui, bbr_ref[i]), _dot(ui, bbi_ref[i])
            for jj in range(S5_BL // 128):
                hr_ref[i * (S5_BL // 128) + jj] = xr[:, jj * 128:(jj + 1) * 128]
                hi_ref[i * (S5_BL // 128) + jj] = xi[:, jj * 128:(jj + 1) * 128]
        for lc in range(S5_LANE_BLOCKS // S5_SCAN_BLOCKS):
            blocks = range(lc * S5_SCAN_BLOCKS, (lc + 1) * S5_SCAN_BLOCKS)
            _segment_scan(hr_ref, hi_ref, lam_ref, car_ref, cai_ref, cn_r, cn_i, blocks, seg, False)
            crs = [cn_r[:, _lanes(j)] for j in blocks]
            cis = [cn_i[:, _lanes(j)] for j in blocks]

            def fix(t, carry, blocks=blocks, crs=crs, cis=cis):
                idx = pl.ds(t, SUBLANES, stride=seg)
                for n, j in enumerate(blocks):
                    pr, pi = p3r_ref[t, :, _lanes(j)], p3i_ref[t, :, _lanes(j)]
                    hr_ref[j, idx, :] += pr * crs[n] - pi * cis[n]
                    hi_ref[j, idx, :] += pr * cis[n] + pi * crs[n]
                return carry

            lax.fori_loop(0, seg, fix, 0, unroll=2)
        for i in range(S5_BLOCKS):
            ws = pl.ds(i * S5_BW, S5_BW)
            js = range(i * (S5_BL // 128), (i + 1) * (S5_BL // 128))
            hr = jnp.concatenate([hr_ref[j] for j in js], axis=1)
            hi = jnp.concatenate([hi_ref[j] for j in js], axis=1)
            y = _dot(hr, crt_ref[i]) - _dot(hi, cit_ref[i]) + d_ref[:, ws] * u[:, i * S5_BW:(i + 1) * S5_BW]
            ypre_ref[:, ws] = y
            ys_ref[:, ws] = jax.nn.gelu(y, approximate=True).astype(ys_ref.dtype)

    whole = pl.BlockSpec(memory_space=pltpu.VMEM)
    h_spec = pl.BlockSpec((S5_LANE_BLOCKS, tb, 128), lambda i: (0, i, 0))
    return pl.pallas_call(
        body, name="s5_fwd", grid=(t_len // tb,),
        in_specs=[pl.BlockSpec((tb, S5_WIDTH), lambda i: (i, 4096 // S5_WIDTH))] + [whole] * 8,
        out_specs=[h_spec, h_spec,
                   pl.BlockSpec((tb, S5_WIDTH), lambda i: (i, 0)), pl.BlockSpec((tb, S5_WIDTH), lambda i: (i, 0))],
        out_shape=[jax.ShapeDtypeStruct((S5_LANE_BLOCKS, t_len, 128), F32),
                   jax.ShapeDtypeStruct((S5_LANE_BLOCKS, t_len, 128), F32),
                   jax.ShapeDtypeStruct((t_len, S5_WIDTH), F32), jax.ShapeDtypeStruct((t_len, S5_WIDTH), MXU_DTYPE)],
        scratch_shapes=[pltpu.VMEM((1, S5_LANES), F32), pltpu.VMEM((1, S5_LANES), F32),
                        pltpu.VMEM((SUBLANES, S5_LANES), F32), pltpu.VMEM((SUBLANES, S5_LANES), F32)],
        compiler_params=_params("arbitrary"))(proj, lam_rows, p3_re, p3_im, bbr4, bbi4, crt4, cit4, d_row)


def _s5_bwd2(dgelu, y_pre, proj, h_re, h_im, lam_rows, p3_re, p3_im, bbr4, bbi4, cr4, ci4, d_row, dproj, t_len, tb):
    seg = tb // SUBLANES
    nb = t_len // tb

    def body(dg_ref, yp_ref, u_ref, hr_ref, hi_ref, lam_ref, p3r_ref, p3i_ref, bbr_ref, bbi_ref, cr_ref, ci_ref,
             d_ref, _, du_ref, dbbr_ref, dbbi_ref, dcr_ref, dci_ref, dd_ref, dlam_ref,
             gr_ref, gi_ref, car_ref, cai_ref, cn_r, cn_i):
        @pl.when(pl.program_id(0) == 0)
        def _():
            for ref in (car_ref, cai_ref, dbbr_ref, dbbi_ref, dcr_ref, dci_ref, dd_ref, dlam_ref):
                ref[...] = jnp.zeros_like(ref)

        u = u_ref[...]
        dy = dg_ref[...] * _dgelu(yp_ref[...])
        nlb = S5_BL // 128
        for i in range(S5_BLOCKS):
            dyi = dy[:, i * S5_BW:(i + 1) * S5_BW]
            xr, xi = _dot(dyi, cr_ref[i]), -_dot(dyi, ci_ref[i])
            for jj in range(nlb):
                gr_ref[i * nlb + jj] = xr[:, jj * 128:(jj + 1) * 128]
                gi_ref[i * nlb + jj] = xi[:, jj * 128:(jj + 1) * 128]
        for lc in range(S5_LANE_BLOCKS // S5_SCAN_BLOCKS):
            blocks = range(lc * S5_SCAN_BLOCKS, (lc + 1) * S5_SCAN_BLOCKS)
            _segment_scan(gr_ref, gi_ref, lam_ref, car_ref, cai_ref, cn_r, cn_i, blocks, seg, True)
            crs = [cn_r[:, _lanes(j)] for j in blocks]
            cis = [cn_i[:, _lanes(j)] for j in blocks]

            def fix(k, carry, blocks=blocks, crs=crs, cis=cis):
                t = seg - 1 - k
                idx = pl.ds(t, SUBLANES, stride=seg)
                out = []
                for n, j in enumerate(blocks):
                    nr, ni, slr, sli = carry[4 * n:4 * n + 4]
                    pr, pi = p3r_ref[t, :, _lanes(j)], p3i_ref[t, :, _lanes(j)]
                    g_r = gr_ref[j, idx, :] + pr * crs[n] - pi * cis[n]
                    g_i = gi_ref[j, idx, :] + pr * cis[n] + pi * crs[n]
                    gr_ref[j, idx, :] = g_r
                    gi_ref[j, idx, :] = g_i
                    hr, hi = hr_ref[j, idx, :], hi_ref[j, idx, :]
                    out += [g_r, g_i, slr + nr * hr + ni * hi, sli + ni * hr - nr * hi]
                return tuple(out)

            zero = jnp.zeros((SUBLANES, 128), F32)
            init = []
            for n in range(len(blocks)):
                init += [crs[n], cis[n], zero, zero]
            fin = lax.fori_loop(0, seg, fix, tuple(init), unroll=2)
            for n, j in enumerate(blocks):
                dlam_ref[0:1, _lanes(j)] += jnp.sum(fin[4 * n + 2], axis=0, keepdims=True)
                dlam_ref[1:2, _lanes(j)] += jnp.sum(fin[4 * n + 3], axis=0, keepdims=True)
        for i in range(S5_BLOCKS):
            ws = pl.ds(i * S5_BW, S5_BW)
            js = range(i * nlb, (i + 1) * nlb)
            ui, dyi = u[:, i * S5_BW:(i + 1) * S5_BW], dy[:, i * S5_BW:(i + 1) * S5_BW]
            gr = jnp.concatenate([gr_ref[j] for j in js], axis=1)
            gi = jnp.concatenate([gi_ref[j] for j in js], axis=1)
            du_ref[:, ws] = _dot(gr, bbr_ref[i], _NT) + _dot(gi, bbi_ref[i], _NT) + d_ref[:, ws] * dyi
            dbbr_ref[i] += _dot(ui, gr, _TN)
            dbbi_ref[i] += _dot(ui, gi, _TN)
            dcr_ref[i] += _dot(jnp.concatenate([hr_ref[j] for j in js], axis=1), dyi, _TN)
            dci_ref[i] -= _dot(jnp.concatenate([hi_ref[j] for j in js], axis=1), dyi, _TN)
        dd_ref[...] += jnp.sum(dy * u, axis=0, keepdims=True)

    whole = pl.BlockSpec(memory_space=pltpu.VMEM)
    rev = lambda i: (nb - 1 - i, 0)
    const3 = lambda i: (0, 0, 0)
    h_spec = pl.BlockSpec((S5_LANE_BLOCKS, tb, 128), lambda i: (0, nb - 1 - i, 0))
    return pl.pallas_call(
        body, name="s5_bwd", grid=(nb,),
        in_specs=[pl.BlockSpec((tb, S5_WIDTH), rev), pl.BlockSpec((tb, S5_WIDTH), rev),
                  pl.BlockSpec((tb, S5_WIDTH), lambda i: (nb - 1 - i, 4096 // S5_WIDTH)),
                  h_spec, h_spec] + [whole] * 8
                 + [pl.BlockSpec(memory_space=pl.ANY)],
        out_specs=[pl.BlockSpec((tb, S5_WIDTH), lambda i: (nb - 1 - i, 4096 // S5_WIDTH)),
                   pl.BlockSpec((S5_BLOCKS, S5_BW, S5_BL), const3), pl.BlockSpec((S5_BLOCKS, S5_BW, S5_BL), const3),
                   pl.BlockSpec((S5_BLOCKS, S5_BL, S5_BW), const3), pl.BlockSpec((S5_BLOCKS, S5_BL, S5_BW), const3),
                   pl.BlockSpec((1, S5_WIDTH), lambda i: (0, 0)), pl.BlockSpec((2, S5_LANES), lambda i: (0, 0))],
        out_shape=[jax.ShapeDtypeStruct((t_len, IN_COLS), F32),
                   jax.ShapeDtypeStruct((S5_BLOCKS, S5_BW, S5_BL), F32),
                   jax.ShapeDtypeStruct((S5_BLOCKS, S5_BW, S5_BL), F32),
                   jax.ShapeDtypeStruct((S5_BLOCKS, S5_BL, S5_BW), F32),
                   jax.ShapeDtypeStruct((S5_BLOCKS, S5_BL, S5_BW), F32),
                   jax.ShapeDtypeStruct((1, S5_WIDTH), F32), jax.ShapeDtypeStruct((2, S5_LANES), F32)],
        scratch_shapes=[pltpu.VMEM((S5_LANE_BLOCKS, tb, 128), F32), pltpu.VMEM((S5_LANE_BLOCKS, tb, 128), F32),
                        pltpu.VMEM((1, S5_LANES), F32), pltpu.VMEM((1, S5_LANES), F32),
                        pltpu.VMEM((SUBLANES, S5_LANES), F32), pltpu.VMEM((SUBLANES, S5_LANES), F32)],
        input_output_aliases={13: 0},
        compiler_params=_params("arbitrary"))(dgelu, y_pre, proj, h_re, h_im, lam_rows, p3_re, p3_im, bbr4, bbi4,
                                              cr4, ci4, d_row, dproj)


def _to_segment_order(v, stage_ref, out_ref, seg):
    nbl = v.shape[1] // 128
    for b in range(nbl):
        stage_ref[b] = v[:, b * 128:(b + 1) * 128]

    def body(t, carry):
        rows = pl.ds(pl.multiple_of(t * SUBLANES, SUBLANES), SUBLANES)
        for b in range(nbl):
            out_ref[rows, _lanes(b)] = stage_ref[b, pl.ds(t, SUBLANES, stride=seg), :]
        return carry

    lax.fori_loop(0, seg, body, 0)


def _from_segment_order(v, stage_ref, out_ref, seg):
    nbl = v.shape[1] // 128
    for b in range(nbl):
        stage_ref[b] = v[:, b * 128:(b + 1) * 128]
    for s in range(SUBLANES):
        def body(k, carry, s=s):
            rows = pl.ds(pl.multiple_of(s * seg + k * SUBLANES, SUBLANES), SUBLANES)
            for b in range(nbl):
                out_ref[rows, _lanes(b)] = stage_ref[b, pl.ds(k * SUBLANES * SUBLANES + s, SUBLANES,
                                                              stride=SUBLANES), :]
            return carry

        lax.fori_loop(0, seg // SUBLANES, body, 0)


def _tile_scan(xr_ref, xi_ref, lam_ref, car_ref, cai_ref, cn_r, cn_i, blocks, seg, reverse):
    shape = (SUBLANES, 128)
    lrs = [jnp.broadcast_to(lam_ref[0:1, _lanes(j)], shape) for j in blocks]
    lis = [jnp.broadcast_to(lam_ref[1:2, _lanes(j)], shape) for j in blocks]

    def step(k, carry):
        t = seg - 1 - k if reverse else k
        rows = pl.ds(pl.multiple_of(t * SUBLANES, SUBLANES), SUBLANES)
        out = []
        for n, j in enumerate(blocks):
            cr, ci = carry[2 * n], carry[2 * n + 1]
            nr = lrs[n] * cr - lis[n] * ci + xr_ref[rows, _lanes(j)]
            ni = lrs[n] * ci + lis[n] * cr + xi_ref[rows, _lanes(j)]
            xr_ref[rows, _lanes(j)] = nr
            xi_ref[rows, _lanes(j)] = ni
            out += [nr, ni]
        return tuple(out)

    zero = jnp.zeros(shape, F32)
    fin = lax.fori_loop(0, seg, step, (zero,) * (2 * len(blocks)), unroll=2)
    for n, j in enumerate(blocks):
        ls = _lanes(j)
        fr, fi = fin[2 * n], fin[2 * n + 1]
        sr, si = lam_ref[2:3, ls], lam_ref[3:4, ls]
        pr, pi = car_ref[:, ls], cai_ref[:, ls]
        for s in (reversed(range(SUBLANES)) if reverse else range(SUBLANES)):
            cn_r[s:s + 1, ls] = pr
            cn_i[s:s + 1, ls] = pi
            pr, pi = fr[s:s + 1, :] + sr * pr - si * pi, fi[s:s + 1, :] + sr * pi + si * pr
        car_ref[:, ls] = pr
        cai_ref[:, ls] = pi


def _s5_fwd3(proj, lam_rows, p3_re, p3_im, bbr4, bbi4, crt4, cit4, d_row, t_len, tb):
    seg = tb // SUBLANES

    def body(u_ref, lam_ref, p3r_ref, p3i_ref, bbr_ref, bbi_ref, crt_ref, cit_ref, d_ref,
             hr_ref, hi_ref, ypre_ref, ys_ref, car_ref, cai_ref, cn_r, cn_i, stage_ref, us_ref, yseg_ref):
        @pl.when(pl.program_id(0) == 0)
        def _():
            car_ref[...] = jnp.zeros_like(car_ref)
            cai_ref[...] = jnp.zeros_like(cai_ref)

        _to_segment_order(u_ref[...], stage_ref, us_ref, seg)
        u = us_ref[...]
        for i in range(S5_BLOCKS):
            ui = u[:, i * S5_BW:(i + 1) * S5_BW]
            hr_ref[:, pl.ds(i * S5_BL, S5_BL)] = _dot(ui, bbr_ref[i])
            hi_ref[:, pl.ds(i * S5_BL, S5_BL)] = _dot(ui, bbi_ref[i])
        for lc in range(S5_LANE_BLOCKS // S5_SCAN_BLOCKS):
            blocks = range(lc * S5_SCAN_BLOCKS, (lc + 1) * S5_SCAN_BLOCKS)
            _tile_scan(hr_ref, hi_ref, lam_ref, car_ref, cai_ref, cn_r, cn_i, blocks, seg, False)
            crs = [cn_r[:, _lanes(j)] for j in blocks]
            cis = [cn_i[:, _lanes(j)] for j in blocks]

            def fix(t, carry, blocks=blocks, crs=crs, cis=cis):
                rows = pl.ds(pl.multiple_of(t * SUBLANES, SUBLANES), SUBLANES)
                for n, j in enumerate(blocks):
                    pr, pi = p3r_ref[t, :, _lanes(j)], p3i_ref[t, :, _lanes(j)]
                    hr_ref[rows, _lanes(j)] += pr * crs[n] - pi * cis[n]
                    hi_ref[rows, _lanes(j)] += pr * cis[n] + pi * crs[n]
                return carry

            lax.fori_loop(0, seg, fix, 0, unroll=2)
        for i in range(S5_BLOCKS):
            ws = pl.ds(i * S5_BW, S5_BW)
            bl = pl.ds(i * S5_BL, S5_BL)
            yseg_ref[:, ws] = (_dot(hr_ref[:, bl], crt_ref[i]) - _dot(hi_ref[:, bl], cit_ref[i])
                               + d_ref[:, ws] * u[:, i * S5_BW:(i + 1) * S5_BW])
        _from_segment_order(yseg_ref[...], stage_ref, ypre_ref, seg)
        ys_ref[...] = jax.nn.gelu(ypre_ref[...], approximate=True).astype(ys_ref.dtype)

    whole = pl.BlockSpec(memory_space=pltpu.VMEM)
    return pl.pallas_call(
        body, name="s5_fwd", grid=(t_len // tb,),
        in_specs=[pl.BlockSpec((tb, S5_WIDTH), lambda i: (i, 4096 // S5_WIDTH))] + [whole] * 8,
        out_specs=[pl.BlockSpec((tb, S5_LANES), lambda i: (i, 0)), pl.BlockSpec((tb, S5_LANES), lambda i: (i, 0)),
                   pl.BlockSpec((tb, S5_WIDTH), lambda i: (i, 0)), pl.BlockSpec((tb, S5_WIDTH), lambda i: (i, 0))],
        out_shape=[jax.ShapeDtypeStruct((t_len, S5_LANES), F32), jax.ShapeDtypeStruct((t_len, S5_LANES), F32),
                   jax.ShapeDtypeStruct((t_len, S5_WIDTH), F32), jax.ShapeDtypeStruct((t_len, S5_WIDTH), MXU_DTYPE)],
        scratch_shapes=[pltpu.VMEM((1, S5_LANES), F32), pltpu.VMEM((1, S5_LANES), F32),
                        pltpu.VMEM((SUBLANES, S5_LANES), F32), pltpu.VMEM((SUBLANES, S5_LANES), F32),
                        pltpu.VMEM((S5_WIDTH // 128, tb, 128), F32), pltpu.VMEM((tb, S5_WIDTH), F32),
                        pltpu.VMEM((tb, S5_WIDTH), F32)],
        compiler_params=_params("arbitrary"))(proj, lam_rows, p3_re, p3_im, bbr4, bbi4, crt4, cit4, d_row)


def _s5_bwd3(dgelu, y_pre, proj, h_re, h_im, lam_rows, p3_re, p3_im, bbr4, bbi4, cr4, ci4, d_row, dproj, t_len, tb):
    seg = tb // SUBLANES
    nb = t_len // tb

    def body(dg_ref, yp_ref, u_ref, hr_ref, hi_ref, lam_ref, p3r_ref, p3i_ref, bbr_ref, bbi_ref, cr_ref, ci_ref,
             d_ref, _, du_ref, dbbr_ref, dbbi_ref, dcr_ref, dci_ref, dd_ref, dlam_ref,
             gr_ref, gi_ref, car_ref, cai_ref, cn_r, cn_i, stage_ref, us_ref, dys_ref, duseg_ref):
        @pl.when(pl.program_id(0) == 0)
        def _():
            for ref in (car_ref, cai_ref, dbbr_ref, dbbi_ref, dcr_ref, dci_ref, dd_ref, dlam_ref):
                ref[...] = jnp.zeros_like(ref)

        _to_segment_order(u_ref[...], stage_ref, us_ref, seg)
        _to_segment_order(dg_ref[...] * _dgelu(yp_ref[...]), stage_ref, dys_ref, seg)
        u, dy = us_ref[...], dys_ref[...]
        for i in range(S5_BLOCKS):
            dyi = dy[:, i * S5_BW:(i + 1) * S5_BW]
            gr_ref[:, pl.ds(i * S5_BL, S5_BL)] = _dot(dyi, cr_ref[i])
            gi_ref[:, pl.ds(i * S5_BL, S5_BL)] = -_dot(dyi, ci_ref[i])
        for lc in range(S5_LANE_BLOCKS // S5_SCAN_BLOCKS):
            blocks = range(lc * S5_SCAN_BLOCKS, (lc + 1) * S5_SCAN_BLOCKS)
            _tile_scan(gr_ref, gi_ref, lam_ref, car_ref, cai_ref, cn_r, cn_i, blocks, seg, True)
            crs = [cn_r[:, _lanes(j)] for j in blocks]
            cis = [cn_i[:, _lanes(j)] for j in blocks]

            def fix(k, carry, blocks=blocks, crs=crs, cis=cis):
                t = seg - 1 - k
                rows = pl.ds(pl.multiple_of(t * SUBLANES, SUBLANES), SUBLANES)
                out = []
                for n, j in enumerate(blocks):
                    nr, ni, slr, sli = carry[4 * n:4 * n + 4]
                    pr, pi = p3r_ref[t, :, _lanes(j)], p3i_ref[t, :, _lanes(j)]
                    g_r = gr_ref[rows, _lanes(j)] + pr * crs[n] - pi * cis[n]
                    g_i = gi_ref[rows, _lanes(j)] + pr * cis[n] + pi * crs[n]
                    gr_ref[rows, _lanes(j)] = g_r
                    gi_ref[rows, _lanes(j)] = g_i
                    hr, hi = hr_ref[rows, _lanes(j)], hi_ref[rows, _lanes(j)]
                    out += [g_r, g_i, slr + nr * hr + ni * hi, sli + ni * hr - nr * hi]
                return tuple(out)

            zero = jnp.zeros((SUBLANES, 128), F32)
            init = []
            for n in range(len(blocks)):
                init += [crs[n], cis[n], zero, zero]
            fin = lax.fori_loop(0, seg, fix, tuple(init), unroll=2)
            for n, j in enumerate(blocks):
                dlam_ref[0:1, _lanes(j)] += jnp.sum(fin[4 * n + 2], axis=0, keepdims=True)
                dlam_ref[1:2, _lanes(j)] += jnp.sum(fin[4 * n + 3], axis=0, keepdims=True)
        for i in range(S5_BLOCKS):
            ws = pl.ds(i * S5_BW, S5_BW)
            bl = pl.ds(i * S5_BL, S5_BL)
            ui, dyi = u[:, i * S5_BW:(i + 1) * S5_BW], dy[:, i * S5_BW:(i + 1) * S5_BW]
            gr, gi = gr_ref[:, bl], gi_ref[:, bl]
            duseg_ref[:, ws] = _dot(gr, bbr_ref[i], _NT) + _dot(gi, bbi_ref[i], _NT) + d_ref[:, ws] * dyi
            dbbr_ref[i] += _dot(ui, gr, _TN)
            dbbi_ref[i] += _dot(ui, gi, _TN)
            dcr_ref[i] += _dot(hr_ref[:, bl], dyi, _TN)
            dci_ref[i] -= _dot(hi_ref[:, bl], dyi, _TN)
        dd_ref[...] += jnp.sum(dy * u, axis=0, keepdims=True)
        _from_segment_order(duseg_ref[...], stage_ref, du_ref, seg)

    whole = pl.BlockSpec(memory_space=pltpu.VMEM)
    rev = lambda i: (nb - 1 - i, 0)
    const3 = lambda i: (0, 0, 0)
    return pl.pallas_call(
        body, name="s5_bwd", grid=(nb,),
        in_specs=[pl.BlockSpec((tb, S5_WIDTH), rev), pl.BlockSpec((tb, S5_WIDTH), rev),
                  pl.BlockSpec((tb, S5_WIDTH), lambda i: (nb - 1 - i, 4096 // S5_WIDTH)),
                  pl.BlockSpec((tb, S5_LANES), rev), pl.BlockSpec((tb, S5_LANES), rev)] + [whole] * 8
                 + [pl.BlockSpec(memory_space=pl.ANY)],
        out_specs=[pl.BlockSpec((tb, S5_WIDTH), lambda i: (nb - 1 - i, 4096 // S5_WIDTH)),
                   pl.BlockSpec((S5_BLOCKS, S5_BW, S5_BL), const3), pl.BlockSpec((S5_BLOCKS, S5_BW, S5_BL), const3),
                   pl.BlockSpec((S5_BLOCKS, S5_BL, S5_BW), const3), pl.BlockSpec((S5_BLOCKS, S5_BL, S5_BW), const3),
                   pl.BlockSpec((1, S5_WIDTH), lambda i: (0, 0)), pl.BlockSpec((2, S5_LANES), lambda i: (0, 0))],
        out_shape=[jax.ShapeDtypeStruct((t_len, IN_COLS), F32),
                   jax.ShapeDtypeStruct((S5_BLOCKS, S5_BW, S5_BL), F32),
                   jax.ShapeDtypeStruct((S5_BLOCKS, S5_BW, S5_BL), F32),
                   jax.ShapeDtypeStruct((S5_BLOCKS, S5_BL, S5_BW), F32),
                   jax.ShapeDtypeStruct((S5_BLOCKS, S5_BL, S5_BW), F32),
                   jax.ShapeDtypeStruct((1, S5_WIDTH), F32), jax.ShapeDtypeStruct((2, S5_LANES), F32)],
        scratch_shapes=[pltpu.VMEM((tb, S5_LANES), F32), pltpu.VMEM((tb, S5_LANES), F32),
                        pltpu.VMEM((1, S5_LANES), F32), pltpu.VMEM((1, S5_LANES), F32),
                        pltpu.VMEM((SUBLANES, S5_LANES), F32), pltpu.VMEM((SUBLANES, S5_LANES), F32),
                        pltpu.VMEM((S5_WIDTH // 128, tb, 128), F32), pltpu.VMEM((tb, S5_WIDTH), F32),
                        pltpu.VMEM((tb, S5_WIDTH), F32), pltpu.VMEM((tb, S5_WIDTH), F32)],
        input_output_aliases={13: 0},
        compiler_params=_params("arbitrary"))(dgelu, y_pre, proj, h_re, h_im, lam_rows, p3_re, p3_im, bbr4, bbi4,
                                              cr4, ci4, d_row, dproj)


def _block_diag4(per_group):
    g8 = S5_GROUPS // S5_BLOCKS
    eye = jnp.eye(g8, dtype=bool)[None, :, None, :, None]
    dense = jnp.where(eye, per_group.reshape(S5_BLOCKS, g8, S5_GROUP, 1, S5_STATE), 0.0)
    return dense.reshape(S5_BLOCKS, S5_BW, S5_BL)


def _diag_blocks4(dense):
    g8 = S5_GROUPS // S5_BLOCKS
    ar = jnp.arange(g8)
    d5 = dense.reshape(S5_BLOCKS, g8, S5_GROUP, g8, S5_STATE)
    return d5[:, ar, :, ar, :].transpose(1, 0, 2, 3).reshape(S5_GROUPS, S5_GROUP, S5_STATE)


def _block_diag(per_group):
    eye = jnp.eye(S5_GROUPS, dtype=bool)[:, None, :, None]
    dense = jnp.where(eye, per_group[:, :, None, :], 0.0)
    return dense.reshape(S5_WIDTH, S5_LANES)


def _diag_blocks(dense):
    ar = jnp.arange(S5_GROUPS)
    return dense.reshape(S5_GROUPS, S5_GROUP, S5_GROUPS, S5_STATE)[ar, :, ar, :]


def _hg_gate_bwd(da, o, g, gn):
    dos, dgs, dgns = [], [], []
    for h in range(HG_HEADS):
        sl = slice(h * HG_DIM, (h + 1) * HG_DIM)
        oh, gh, dah, gnh = o[:, sl], g[:, sl], da[:, sl], gn[:, sl]
        rr = lax.rsqrt(jnp.mean(oh * oh, axis=-1, keepdims=True) + NORM_EPS)
        sg = _sig(gh)
        dgs.append(dah * (oh * rr * gnh) * _dsilu(gh, sg))
        don = dah * (gh * sg)
        t = don * gnh
        dos.append(rr * t - oh * (rr * rr * rr) * jnp.mean(t * oh, axis=-1, keepdims=True))
        dgns.append(jnp.sum(don * oh * rr, axis=0, keepdims=True))
    return jnp.concatenate(dos, axis=1), jnp.concatenate(dgs, axis=1), jnp.concatenate(dgns, axis=1)


MIX_BWD_COLS = ((3072, 1024), (4608, 512), (5120, 1024), (6144, 1024))


def _mix_bwd(dgl, h1, dh2, y_hg, y_s5, proj, glu, o_hg, g2, ghn, w, t_len, tm):
    nb = t_len // tm

    def body(dgl_ref, h1_ref, dh2_ref, yh_ref, ys_ref, ghg_ref, z_ref, gh_ref, gs_ref, glu_ref, o_ref, g2_ref, gn_ref,
             wg_ref, wo_ref, ws5_ref, whg_ref, wglu_ref,
             dh1_ref, dyh_ref, dys_ref, dglu_ref, dgelu_ref, do_ref, dg2_ref, dbglu_ref, dgn_ref, dproj_ref,
             st0, st1, st2, st3, sems):
        i = pl.program_id(0)
        stages = (st0, st1, st2, st3)

        def writes(step):
            rows = pl.ds(pl.multiple_of(step * tm, tm), tm)
            return [pltpu.make_async_copy(st, dproj_ref.at[rows, pl.ds(c0, wd)], sems.at[k])
                    for k, (st, (c0, wd)) in enumerate(zip(stages, MIX_BWD_COLS))]

        @pl.when(i > 0)
        def _():
            for cp in writes(i - 1):
                cp.wait()

        @pl.when(i == 0)
        def _():
            for ref in (dg2_ref, dbglu_ref, dgn_ref):
                ref[...] = jnp.zeros_like(ref)

        dx, dg2 = _rms_bwd(_dot(dgl_ref[...], wg_ref[...], _NT), h1_ref[...], g2_ref[...])
        dh1 = dh2_ref[...] + dx
        dh1_ref[...] = dh1
        dg2_ref[...] += dg2
        dm = _dot(dh1, wo_ref[...], _NT)
        sh, ss = _sig(gh_ref[...]), _sig(gs_ref[...])
        dyh, dys = _mx(dm * sh), _mx(dm * ss)
        dyh_ref[...] = dyh
        dys_ref[...] = dys
        st2[...] = dm * yh_ref[...] * sh * (1.0 - sh)
        st3[...] = dm * ys_ref[...] * ss * (1.0 - ss)
        dys2 = _dot(dys, ws5_ref[...], _NT)
        gl_, z = glu_ref[...], z_ref[...]
        a, b = gl_[:, :S5_WIDTH], gl_[:, S5_WIDTH:]
        sb, sz = _sig(b), _sig(z)
        silu = z * sz
        dglu = jnp.concatenate([dys2 * sb * silu, dys2 * a * silu * sb * (1.0 - sb)], axis=1)
        st1[...] = dys2 * a * sb * _dsilu(z, sz)
        dbglu_ref[...] += jnp.sum(dglu, axis=0, keepdims=True)
        dglu_ref[...] = _mx(dglu)
        dgelu_ref[...] = _dot(dglu, wglu_ref[...], _NT)
        d_o, dg, dgn = _hg_gate_bwd(_dot(dyh, whg_ref[...], _NT), o_ref[...], ghg_ref[...], gn_ref[...])
        do_ref[...] = d_o
        st0[...] = dg
        dgn_ref[...] += dgn
        for cp in writes(i):
            cp.start()

        @pl.when(i == nb - 1)
        def _():
            for cp in writes(i):
                cp.wait()

    tile = lambda wd, cb=0: pl.BlockSpec((tm, wd), functools.partial(lambda i, cb: (i, cb), cb=cb))
    row = lambda wd: pl.BlockSpec((1, wd), lambda i: (0, 0))
    whole = pl.BlockSpec(memory_space=pltpu.VMEM)
    return pl.pallas_call(
        body, name="mix_bwd", grid=(nb,),
        in_specs=[tile(1024), tile(1024), tile(1024), tile(1024), tile(1024), tile(1024, 3), tile(512, 4608 // 512),
                  tile(1024, 5), tile(1024, 6), tile(1024), tile(1024), row(1024), row(1024)] + [whole] * 5,
        out_specs=[tile(1024), tile(1024), tile(1024), tile(1024), tile(512), tile(1024), row(1024), row(1024),
                   row(1024), _HBM],
        out_shape=[jax.ShapeDtypeStruct((t_len, 1024), F32), jax.ShapeDtypeStruct((t_len, 1024), MXU_DTYPE),
                   jax.ShapeDtypeStruct((t_len, 1024), MXU_DTYPE), jax.ShapeDtypeStruct((t_len, 1024), MXU_DTYPE),
                   jax.ShapeDtypeStruct((t_len, 512), F32), jax.ShapeDtypeStruct((t_len, 1024), F32),
                   jax.ShapeDtypeStruct((1, 1024), F32), jax.ShapeDtypeStruct((1, 1024), F32),
                   jax.ShapeDtypeStruct((1, 1024), F32), jax.ShapeDtypeStruct((t_len, IN_COLS), F32)],
        scratch_shapes=[pltpu.VMEM((tm, wd), F32) for _, wd in MIX_BWD_COLS] + [pltpu.SemaphoreType.DMA((4,))],
        compiler_params=_params("arbitrary"))(dgl, h1, dh2, y_hg, y_s5, proj, proj, proj, proj, glu, o_hg, g2, ghn,
                                              w["w_ple_gate"], w["w_out"], w["w_o_s5"], w["w_o_hg"], w["w_glu"])


def _local_step(x, p, target, w, sm, comm=None):
    t_len = x.shape[0]
    tm = min(256, t_len)
    tmm = min(512, t_len)
    tb_hg = min(256, t_len)
    tb_s5 = min(256, t_len)
    g1, g2, g3, ghn = sm["norm_g"], sm["ple_norm_g"], sm["final_norm_g"].reshape(1, D_MODEL), sm["hg_norm_g"]

    def rms_f(xv, g):
        r = lax.rsqrt(jnp.mean(xv * xv, axis=-1, keepdims=True) + NORM_EPS)
        return (xv * r * g,)

    (u,) = _rowwise("rms_in", rms_f, t_len, tm, [(x, 1024, 0)], [g1], [(1024, MXU_DTYPE)])
    in_shard = IN_COLS // N_CHIPS
    w_in = w["w_in"]
    if comm is None:
        proj = _mm_nn("mm_in", u, w_in, tmm, in_shard)
    else:
        proj, landed = _mm_nn("mm_in", u, w_in, tmm, in_shard, riding=comm.gather_rest())
        w = comm.rest_weights(landed)
    o_hg, act_hg, s_prev = _hgrn2_fwd2(proj, sm["hg_lb"], ghn, t_len, tb_hg)

    lanes = lambda a: a.reshape(1, S5_LANES)
    a_re, a_im = lanes(sm["s5_a_re"]), lanes(sm["s5_a_im"])
    ldt = lanes(jnp.broadcast_to(sm["s5_log_dt"].reshape(S5_GROUPS, 1), (S5_GROUPS, S5_STATE)))
    to_t = lambda b: b.reshape(S5_GROUPS, S5_STATE, S5_GROUP).transpose(2, 0, 1).reshape(S5_GROUP, S5_LANES)
    b_re_t, b_im_t = to_t(sm["s5_b_re"]), to_t(sm["s5_b_im"])
    pw_re, pw_im, bbr_t, bbi_t = _s5_powers(a_re, a_im, ldt, b_re_t, b_im_t, tb_s5 // SUBLANES)
    from_t = lambda b: b.reshape(S5_GROUP, S5_GROUPS, S5_STATE).transpose(1, 0, 2)
    bbr_bd = _block_diag4(from_t(bbr_t)).astype(MXU_DTYPE)
    bbi_bd = _block_diag4(from_t(bbi_t)).astype(MXU_DTYPE)
    cr_bd = _block_diag4(sm["s5_c_re"].reshape(S5_GROUPS, S5_GROUP, S5_STATE)).astype(MXU_DTYPE)
    ci_bd = _block_diag4(sm["s5_c_im"].reshape(S5_GROUPS, S5_GROUP, S5_STATE)).astype(MXU_DTYPE)
    d_row = sm["s5_d"].reshape(1, S5_WIDTH)
    h_re, h_im, y_pre, ys_gelu = _s5_fwd3(proj, *_scan_tables(pw_re, pw_im, False), bbr_bd, bbi_bd,
                                          cr_bd.transpose(0, 2, 1), ci_bd.transpose(0, 2, 1), d_row, t_len, tb_s5)
    def mix_f(act, ysg, z, gh, gs, xv, w_glu, b_glu, w_o_hg, w_o_s5, w_out):
        gl_ = _dot(ysg, w_glu) + b_glu
        a, b = gl_[:, :S5_WIDTH], gl_[:, S5_WIDTH:]
        ys2_ = (a * _sig(b) * (z * _sig(z))).astype(MXU_DTYPE)
        yh, ys = _dot(act, w_o_hg), _dot(ys2_, w_o_s5)
        mg = (_sig(gh) * yh + _sig(gs) * ys).astype(MXU_DTYPE)
        return (gl_, ys2_, yh, ys, mg, xv + _dot(mg, w_out))

    glu, ys2, y_hg, y_s5, merged, h1 = _rowwise(
        "mix_out", mix_f, t_len, tm,
        [(act_hg, 1024, 0), (ys_gelu, 512, 0), (proj, 512, 4608 // 512), (proj, 1024, 5), (proj, 1024, 6),
         (x, 1024, 0)], [w["w_glu"], sm["b_glu"], w["w_o_hg"], w["w_o_s5"], w["w_out"]],
        [(1024, F32), (512, MXU_DTYPE), (1024, F32), (1024, F32), (1024, MXU_DTYPE), (1024, F32)])

    def head_f(h1v, pv, tgt, g_ple, g, w_ple, w_gate):
        r2 = lax.rsqrt(jnp.mean(h1v * h1v, axis=-1, keepdims=True) + NORM_EPS)
        n2_ = (h1v * r2 * g_ple).astype(MXU_DTYPE)
        glv, pev = _dot(n2_, w_gate), _dot(pv, w_ple)
        gate = _sig(glv)
        h2 = h1v + pev * gate
        r = lax.rsqrt(jnp.mean(h2 * h2, axis=-1, keepdims=True) + NORM_EPS)
        e = h2 * r * g - tgt
        loss = 0.5 * jnp.sum(jnp.mean(e * e, axis=-1, keepdims=True), axis=0, keepdims=True)
        dy = e * (1.0 / D_MODEL)
        dg = jnp.sum(dy * h2 * r, axis=0, keepdims=True)
        t = dy * g
        dh2 = r * t - h2 * (r * r * r) * jnp.mean(t * h2, axis=-1, keepdims=True)
        return (n2_, dh2, dh2 * gate, dh2 * pev * gate * (1.0 - gate), jnp.broadcast_to(loss, (1, 128)), dg)

    n2, dh2, dpe, dgl, loss_row, d_g3 = _rowwise(
        "ple_loss_head", head_f, t_len, tm, [(h1, 1024, 0), (p, 256, 0), (target, 1024, 0)],
        [g2, g3, w["w_ple"], w["w_ple_gate"]],
        [(1024, MXU_DTYPE), (1024, F32), (1024, MXU_DTYPE), (1024, MXU_DTYPE)], accs=[(1, 128), (1, 1024)])

    gb = {}
    gb["w_ple"] = _mm_tn("mm_d_w_ple", p, dpe, tmm, 1024)
    gb["w_ple_gate"] = _mm_tn("mm_d_w_ple_gate", n2, dgl, tmm, 1024)
    dh1, dy_hg, dy_s5, dglu, dgelu, d_o, d_g2, d_bglu, d_ghn, dproj = _mix_bwd(
        dgl, h1, dh2, y_hg, y_s5, proj, glu, o_hg, g2, ghn, w, t_len, tm)
    gb["w_out"] = _mm_tn("mm_d_w_out", merged, dh1, tmm, 1024)
    gb["w_o_s5"] = _mm_tn("mm_d_w_o_s5", ys2, dy_s5, tmm, 1024)
    gb["w_glu"] = _mm_tn("mm_d_w_glu", ys_gelu, dglu, tmm, 1024)
    dproj, d_bbr, d_bbi, d_crt, d_cit, d_d, d_lam = _s5_bwd3(dgelu, y_pre, proj, h_re, h_im,
                                                            *_scan_tables(pw_re, pw_im, True), bbr_bd, bbi_bd, cr_bd,
                                                            ci_bd, d_row, dproj, t_len, tb_s5)
    to_t3 = lambda b: b.transpose(1, 0, 2).reshape(S5_GROUP, S5_LANES)
    d_are, d_aim, d_ldt, d_br_t, d_bi_t = _s5_prep_bwd(a_re, a_im, ldt, b_re_t, b_im_t, d_lam,
                                                       to_t3(_diag_blocks4(d_bbr)), to_t3(_diag_blocks4(d_bbi)))
    gb["w_o_hg"] = _mm_tn("mm_d_w_o_hg", act_hg, dy_hg, tmm, 1024)
    dproj, d_lb = _hgrn2_bwd2(proj, d_o, s_prev, sm["hg_lb"], dproj, t_len, tb_hg)

    def in_b(duv, xv, dh, g):
        dx, dg = _rms_bwd(duv, xv, g)
        return (dh + dx, dg)

    in_args = ("mm_d_u_rms_in_bwd", dproj, w_in, tmm, in_shard, in_b, [(x, 1024, 0), (dh1, 1024, 0)], [g1],
               [(1024, F32)])
    if comm is None:
        gb["w_in"] = _mm_tn("mm_d_w_in", u, dproj, tmm, in_shard, col_shards=True)
        grad_x, d_g1 = _mm_nt_then(*in_args, accs=[(1, 1024)])
    else:
        gb["w_in"], landed = _mm_tn("mm_d_w_in", u, dproj, tmm, in_shard, col_shards=True,
                                    riding=comm.scatter("rest", _pack_rest_full(gb)))
        comm.landed["rest"] = landed
        grad_x, d_g1, landed = _mm_nt_then(*in_args, accs=[(1, 1024)], riding=comm.scatter(
            "in", gb["w_in"].reshape(N_CHIPS, 2, D_MODEL // 2, in_shard)))
        comm.landed["in"] = landed

    back_t = lambda b: b.reshape(S5_GROUP, S5_GROUPS, S5_STATE).transpose(1, 2, 0).reshape(1, S5_GROUPS, S5_STATE,
                                                                                           S5_GROUP)
    gs = {
        "norm_g": d_g1, "hg_lb": d_lb, "hg_norm_g": d_ghn,
        "s5_a_re": d_are.reshape(1, S5_GROUPS, S5_STATE), "s5_a_im": d_aim.reshape(1, S5_GROUPS, S5_STATE),
        "s5_log_dt": d_ldt[0:1, :S5_GROUPS],
        "s5_b_re": back_t(d_br_t), "s5_b_im": back_t(d_bi_t),
        "s5_c_re": _diag_blocks4(d_crt.transpose(0, 2, 1)).reshape(1, S5_GROUPS, S5_GROUP, S5_STATE),
        "s5_c_im": _diag_blocks4(d_cit.transpose(0, 2, 1)).reshape(1, S5_GROUPS, S5_GROUP, S5_STATE),
        "s5_d": d_d.reshape(1, S5_GROUPS, S5_GROUP), "b_glu": d_bglu, "ple_norm_g": d_g2,
        "final_norm_g": d_g3.reshape(D_MODEL),
    }
    return loss_row, grad_x, gb, gs


def _shard_shape(name):
    r, c = BIG_SHAPE[name]
    return (r, c // N_CHIPS) if name in BIG_COL_SHARDED else (r // N_CHIPS, c)


def _pack_shard(parts):
    return jnp.concatenate([parts[n].reshape(-1, PACK_W) for n in BIG], axis=0)


def _unpack_shard(packed):
    out, off = {}, 0
    for n in BIG:
        r, c = _shard_shape(n)
        rows = r * c // PACK_W
        out[n] = packed[off:off + rows].reshape(1, r, c)
        off += rows
    return out


def _unpack_full(gathered):
    out, off = {}, 0
    for n in BIG:
        r, c = _shard_shape(n)
        rows = r * c // PACK_W
        sh = gathered[:, off:off + rows].reshape(N_CHIPS, r, c)
        out[n] = sh.transpose(1, 0, 2).reshape(BIG_SHAPE[n]) if n in BIG_COL_SHARDED else sh.reshape(BIG_SHAPE[n])
        off += rows
    return out


def _pack_full(full):
    parts = []
    for n in BIG:
        r, c = _shard_shape(n)
        g = full[n]
        sh = g.reshape(BIG_SHAPE[n][0], N_CHIPS, c).transpose(1, 0, 2) if n in BIG_COL_SHARDED else g
        parts.append(sh.reshape(N_CHIPS, r * c // PACK_W, PACK_W))
    packed = jnp.concatenate(parts, axis=1)
    return packed.reshape(N_CHIPS, 2, HALF_ROWS, PACK_W).transpose(1, 0, 2, 3)


def _pack_small(parts, last):
    flat = jnp.concatenate([parts[n].reshape(-1) for n in SMALL] + [last.reshape(-1)])
    return jnp.pad(flat, (0, SMALL_ROWS * PACK_W - flat.shape[0])).reshape(SMALL_ROWS, PACK_W)


def _unpack_small(packed):
    flat, out, off = packed.reshape(-1), {}, 0
    for n in SMALL:
        size = 1
        for d in SMALL_SHAPE[n]:
            size *= d
        out[n] = flat[off:off + size].reshape(SMALL_SHAPE[n])
        off += size
    return out, flat[off]


def _place():
    x, y, c = lax.axis_index("x"), lax.axis_index("y"), lax.axis_index("c")
    return x, y, c, [(1 - x, y), (x, 1 - y), (1 - x, 1 - y)]


def _remote(src, dst, send_sems, recv_sems, k, to):
    return pltpu.make_async_remote_copy(src_ref=src, dst_ref=dst, send_sem=send_sems.at[k], recv_sem=recv_sems.at[k],
                                        device_id=to, device_id_type=MESH)


_HBM = pl.BlockSpec(memory_space=pl.ANY)


def _all_gather_weights(wp):
    def body(wp_ref, out_ref, send_sems, recv_sems):
        x, y, c, chips = _place()
        k = 2 * x + y
        sibling = (x, y, 1 - c)
        first =[_remote(wp_ref.at[c], out_ref.at[k, c], send_sems, recv_sems, j, (cx, cy, c))
                 for j, (cx, cy) in enumerate(chips)]
        for cp in first:
            cp.start()
        passed = []
        for j, (cx, cy) in enumerate(chips):
            kj = 2 * cx + cy
            _remote(wp_ref.at[c], out_ref.at[kj, c], send_sems, recv_sems, j, (cx, cy, c)).wait_recv()
            cp = _remote(out_ref.at[kj, c], out_ref.at[kj, c], send_sems, recv_sems, 3 + j, sibling)
            cp.start()
            passed.append(cp)
        for j, (cx, cy) in enumerate(chips):
            kj = 2 * cx + cy
            _remote(wp_ref.at[c], out_ref.at[kj, 1 - c], send_sems, recv_sems, 3 + j, sibling).wait_recv()
        for cp in first + passed:
            cp.wait_send()

    return pl.pallas_call(
        body, name="all_gather_weights", in_specs=[_HBM], out_specs=_HBM,
        out_shape=jax.ShapeDtypeStruct((N_CHIPS, 2, HALF_ROWS, PACK_W), wp.dtype),
        scratch_shapes=[pltpu.SemaphoreType.DMA((6,)), pltpu.SemaphoreType.DMA((6,))])(wp)


def _exchange_halves(pg):
    def body(pg_ref, out_ref, send_sems, recv_sems):
        x, y, c, _ = _place()
        cp = _remote(pg_ref.at[1 - c], out_ref, send_sems, recv_sems, 0, (x, y, 1 - c))
        cp.start()
        cp.wait()

    return pl.pallas_call(
        body, name="exchange_halves", in_specs=[_HBM], out_specs=_HBM,
        out_shape=jax.ShapeDtypeStruct((N_CHIPS, HALF_ROWS, PACK_W), pg.dtype),
        scratch_shapes=[pltpu.SemaphoreType.DMA((1,)), pltpu.SemaphoreType.DMA((1,))])(pg)


def _scatter_chip_sums(ps):
    def body(ps_ref, out_ref, send_sems, recv_sems):
        x, y, c, chips = _place()
        cps = [_remote(ps_ref.at[2 * cx + cy], out_ref.at[j], send_sems, recv_sems, j, (cx, cy, c))
               for j, (cx, cy) in enumerate(chips)]
        for cp in cps:
            cp.start()
        for cp in cps:
            cp.wait()

    return pl.pallas_call(
        body, name="scatter_chip_sums", in_specs=[_HBM], out_specs=_HBM,
        out_shape=jax.ShapeDtypeStruct((3, HALF_ROWS, PACK_W), ps.dtype),
        scratch_shapes=[pltpu.SemaphoreType.DMA((3,)), pltpu.SemaphoreType.DMA((3,))])(ps)


def _share_half(g_half):
    def body(g_ref, out_ref, send_sems, recv_sems):
        x, y, c, _ = _place()
        cp = _remote(g_ref, out_ref.at[c], send_sems, recv_sems, 0, (x, y, 1 - c))
        cp.start()
        _remote(g_ref, out_ref.at[1 - c], send_sems, recv_sems, 0, (x, y, 1 - c)).wait_recv()
        cp.wait_send()

    return pl.pallas_call(
        body, name="share_half", in_specs=[_HBM], out_specs=_HBM,
        out_shape=jax.ShapeDtypeStruct((2, HALF_ROWS, PACK_W), g_half.dtype),
        scratch_shapes=[pltpu.SemaphoreType.DMA((1,)), pltpu.SemaphoreType.DMA((1,))])(g_half)


REDUCE_ROWS = 480


def _sum_pair(pg, theirs, c):
    def body(c_ref, a_ref, b_ref, o_ref):
        o_ref[...] = (a_ref[...] + b_ref[...]).astype(o_ref.dtype)

    return pl.pallas_call(
        body, name="sum_pair",
        grid_spec=pltpu.PrefetchScalarGridSpec(
            num_scalar_prefetch=1, grid=(N_CHIPS, HALF_ROWS // REDUCE_ROWS),
            in_specs=[pl.BlockSpec((None, None, REDUCE_ROWS, PACK_W), lambda j, i, c_ref: (c_ref[0], j, i, 0)),
                      pl.BlockSpec((None, REDUCE_ROWS, PACK_W), lambda j, i, c_ref: (j, i, 0))],
            out_specs=pl.BlockSpec((None, REDUCE_ROWS, PACK_W), lambda j, i, c_ref: (j, i, 0))),
        out_shape=jax.ShapeDtypeStruct((N_CHIPS, HALF_ROWS, PACK_W), WIRE_DTYPE),
        compiler_params=_params("arbitrary", "arbitrary"))(c.reshape(1), pg, theirs)


def _sum_chips(ps, others, k):
    def body(k_ref, a_ref, b_ref, o_ref):
        o_ref[...] = ((a_ref[...].astype(F32) + b_ref[0].astype(F32)) + b_ref[1].astype(F32)) + b_ref[2].astype(F32)

    return pl.pallas_call(
        body, name="sum_chips",
        grid_spec=pltpu.PrefetchScalarGridSpec(
            num_scalar_prefetch=1, grid=(HALF_ROWS // REDUCE_ROWS,),
            in_specs=[pl.BlockSpec((None, REDUCE_ROWS, PACK_W), lambda i, k_ref: (k_ref[0], i, 0)),
                      pl.BlockSpec((3, REDUCE_ROWS, PACK_W), lambda i, k_ref: (0, i, 0))],
            out_specs=pl.BlockSpec((REDUCE_ROWS, PACK_W), lambda i, k_ref: (i, 0))),
        out_shape=jax.ShapeDtypeStruct((HALF_ROWS, PACK_W), F32),
        compiler_params=_params("arbitrary"))(k.reshape(1), ps, others)


REST = tuple(n for n in BIG if n != "w_in")
REST_ROWS = sum(BIG_SHAPE[n][0] * BIG_SHAPE[n][1] for n in REST) // (N_CHIPS * PACK_W)
IN_SHARD = IN_COLS // N_CHIPS
IN_TILE, REST_TILE = 256, 272


def _pack_rest(parts):
    return jnp.concatenate([parts[n].reshape(-1, PACK_W) for n in REST], axis=0)


def _unpack_rest(packed):
    out, off = {}, 0
    for n in REST:
        r, c = _shard_shape(n)
        rows = r * c // PACK_W
        out[n] = packed[off:off + rows].reshape(1, r, c)
        off += rows
    return out


def _unpack_rest_full(gathered):
    out, off = {}, 0
    for n in REST:
        r, c = _shard_shape(n)
        rows = r * c // PACK_W
        sh = gathered[:, off:off + rows].reshape(N_CHIPS, r, c)
        out[n] = sh.transpose(1, 0, 2).reshape(BIG_SHAPE[n]) if n in BIG_COL_SHARDED else sh.reshape(BIG_SHAPE[n])
        off += rows
    return out


def _pack_rest_full(full):
    parts = []
    for n in REST:
        r, c = _shard_shape(n)
        g = full[n]
        sh = g.reshape(BIG_SHAPE[n][0], N_CHIPS, c).transpose(1, 0, 2) if n in BIG_COL_SHARDED else g
        parts.append(sh.reshape(N_CHIPS, r * c // PACK_W, PACK_W))
    return jnp.concatenate(parts, axis=1).reshape(N_CHIPS, 2, REST_ROWS // 2, PACK_W)


def _gather_shards(ws):
    n = len(ws)

    def body(*refs):
        w_refs, out_refs, (send_sems, recv_sems) = refs[:n], refs[n:2 * n], refs[2 * n:]
        x, y, c, chips = _place()
        k = 2 * x + y
        sibling = (x, y, 1 - c)
        first = [_remote(w_ref.at[c], out_ref.at[k, c], send_sems, recv_sems, 6 * g + j, (cx, cy, c))
                 for j, (cx, cy) in enumerate(chips) for g, (w_ref, out_ref) in enumerate(zip(w_refs, out_refs))]
        for cp in first:
            cp.start()
        passed = []
        for j, (cx, cy) in enumerate(chips):
            kj = 2 * cx + cy
            for g, (w_ref, out_ref) in enumerate(zip(w_refs, out_refs)):
                _remote(w_ref.at[c], out_ref.at[kj, c], send_sems, recv_sems, 6 * g + j, (cx, cy, c)).wait_recv()
                cp = _remote(out_ref.at[kj, c], out_ref.at[kj, c], send_sems, recv_sems, 6 * g + 3 + j, sibling)
                cp.start()
                passed.append(cp)
        for j, (cx, cy) in enumerate(chips):
            kj = 2 * cx + cy
            for g, (w_ref, out_ref) in enumerate(zip(w_refs, out_refs)):
                _remote(w_ref.at[c], out_ref.at[kj, 1 - c], send_sems, recv_sems, 6 * g + 3 + j, sibling).wait_recv()
        for cp in first + passed:
            cp.wait_send()

    return pl.pallas_call(
        body, name="all_gather_weights", in_specs=[_HBM] * n, out_specs=[_HBM] * n,
        out_shape=[jax.ShapeDtypeStruct((N_CHIPS,) + w.shape, w.dtype) for w in ws],
        scratch_shapes=[pltpu.SemaphoreType.DMA((6 * n,)), pltpu.SemaphoreType.DMA((6 * n,))])(*ws)


def _swap_halves(pgs, name="exchange_halves"):
    n = len(pgs)

    def body(*refs):
        pg_refs, out_refs, (send_sems, recv_sems) = refs[:n], refs[n:2 * n], refs[2 * n:]
        x, y, c, _ = _place()
        cps = [_remote(pg_ref.at[j, 1 - c], out_ref.at[j], send_sems, recv_sems, N_CHIPS * g + j, (x, y, 1 - c))
               for g, (pg_ref, out_ref) in enumerate(zip(pg_refs, out_refs)) for j in range(N_CHIPS)]
        for cp in cps:
            cp.start()
        for cp in cps:
            cp.wait()

    return pl.pallas_call(
        body, name=name, in_specs=[_HBM] * n, out_specs=[_HBM] * n,
        out_shape=[jax.ShapeDtypeStruct((N_CHIPS,) + pg.shape[2:], pg.dtype) for pg in pgs],
        scratch_shapes=[pltpu.SemaphoreType.DMA((N_CHIPS * n,)), pltpu.SemaphoreType.DMA((N_CHIPS * n,))])(*pgs)


def _scatter_sums(pss):
    n = len(pss)

    def body(*refs):
        ps_refs, out_refs, (send_sems, recv_sems) = refs[:n], refs[n:2 * n], refs[2 * n:]
        x, y, c, chips = _place()
        cps = [_remote(ps_ref.at[2 * cx + cy], out_ref.at[j], send_sems, recv_sems, 3 * g + j, (cx, cy, c))
               for j, (cx, cy) in enumerate(chips) for g, (ps_ref, out_ref) in enumerate(zip(ps_refs, out_refs))]
        for cp in cps:
            cp.start()
        for cp in cps:
            cp.wait()

    return pl.pallas_call(
        body, name="scatter_chip_sums", in_specs=[_HBM] * n, out_specs=[_HBM] * n,
        out_shape=[jax.ShapeDtypeStruct((3,) + ps.shape[1:], ps.dtype) for ps in pss],
        scratch_shapes=[pltpu.SemaphoreType.DMA((3 * n,)), pltpu.SemaphoreType.DMA((3 * n,))])(*pss)


def _share_halves(gs):
    n = len(gs)

    def body(*refs):
        g_refs, out_refs, (send_sems, recv_sems) = refs[:n], refs[n:2 * n], refs[2 * n:]
        x, y, c, _ = _place()
        cps = [_remote(g_ref, out_ref.at[c], send_sems, recv_sems, g, (x, y, 1 - c))
               for g, (g_ref, out_ref) in enumerate(zip(g_refs, out_refs))]
        for cp in cps:
            cp.start()
        for g, (g_ref, out_ref) in enumerate(zip(g_refs, out_refs)):
            _remote(g_ref, out_ref.at[1 - c], send_sems, recv_sems, g, (x, y, 1 - c)).wait_recv()
        for cp in cps:
            cp.wait_send()

    return pl.pallas_call(
        body, name="share_half", in_specs=[_HBM] * n, out_specs=[_HBM] * n,
        out_shape=[jax.ShapeDtypeStruct((2,) + g.shape, g.dtype) for g in gs],
        scratch_shapes=[pltpu.SemaphoreType.DMA((n,)), pltpu.SemaphoreType.DMA((n,))])(*gs)


def _pair_sum(name, pg, theirs, c, tile):
    _, _, rows, width = pg.shape

    def body(c_ref, a_ref, b_ref, o_ref):
        o_ref[...] = (a_ref[...] + b_ref[...]).astype(o_ref.dtype)

    return pl.pallas_call(
        body, name=name,
        grid_spec=pltpu.PrefetchScalarGridSpec(
            num_scalar_prefetch=1, grid=(N_CHIPS, rows // tile),
            in_specs=[pl.BlockSpec((None, None, tile, width), lambda j, i, c_ref: (j, c_ref[0], i, 0)),
                      pl.BlockSpec((None, tile, width), lambda j, i, c_ref: (j, i, 0))],
            out_specs=pl.BlockSpec((None, tile, width), lambda j, i, c_ref: (j, i, 0))),
        out_shape=jax.ShapeDtypeStruct((N_CHIPS, rows, width), WIRE_DTYPE),
        compiler_params=_params("arbitrary", "arbitrary"))(c.reshape(1), pg, theirs)


def _chip_sum(name, ps, others, k, tile):
    _, rows, width = ps.shape

    def body(k_ref, a_ref, b_ref, o_ref):
        o_ref[...] = ((a_ref[...].astype(F32) + b_ref[0].astype(F32)) + b_ref[1].astype(F32)) + b_ref[2].astype(F32)

    return pl.pallas_call(
        body, name=name,
        grid_spec=pltpu.PrefetchScalarGridSpec(
            num_scalar_prefetch=1, grid=(rows // tile,),
            in_specs=[pl.BlockSpec((None, tile, width), lambda i, k_ref: (k_ref[0], i, 0)),
                      pl.BlockSpec((3, tile, width), lambda i, k_ref: (0, i, 0))],
            out_specs=pl.BlockSpec((tile, width), lambda i, k_ref: (i, 0))),
        out_shape=jax.ShapeDtypeStruct((rows, width), F32),
        compiler_params=_params("arbitrary"))(k.reshape(1), ps, others)


class _StepComm:
    TILES = {"in": IN_TILE, "rest": REST_TILE}

    def __init__(self, rest_wire, chip, core):
        self.rest_wire, self.chip, self.core = rest_wire, chip, core
        self.sums, self.landed = {}, {}

    def gather_rest(self):
        wire = self.rest_wire

        def sends(ins, outs, send_sems, recv_sems):
            (w_ref,), (out_ref,) = ins, outs
            x, y, c, chips = _place()
            return [_remote(w_ref.at[c], out_ref.at[2 * x + y, c], send_sems, recv_sems, 4 * j + 2 * c + to,
                            (cx, cy, to)) for j, (cx, cy) in enumerate(chips) for to in (0, 1)]

        def recvs(ins, outs, send_sems, recv_sems):
            (w_ref,), (out_ref,) = ins, outs
            _, _, c, chips = _place()
            return [_remote(w_ref.at[c], out_ref.at[2 * cx + cy, by], send_sems, recv_sems, 4 * j + 2 * by + c,
                            (cx, cy, by)) for j, (cx, cy) in enumerate(chips) for by in (0, 1)]

        def start(*refs):
            for cp in sends(*refs):
                cp.start()

        def wait(*refs):
            for cp in recvs(*refs):
                cp.wait_recv()
            for cp in sends(*refs):
                cp.wait_send()

        return _Riding((wire,), (jax.ShapeDtypeStruct((N_CHIPS,) + wire.shape, wire.dtype),), 12, start, wait)

    def rest_weights(self, landed):
        full = lax.dynamic_update_slice(landed, self.rest_wire[None], (self.chip, 0, 0, 0))
        return _unpack_rest_full(full.reshape(N_CHIPS, REST_ROWS, PACK_W))

    def scatter(self, group, pg):
        (theirs,) = _swap_halves([pg], "exchange_halves_" + group)
        ps = _pair_sum("sum_pair_" + group, pg, theirs, self.core, self.TILES[group])
        self.sums[group] = ps

        def copies(ins, outs, send_sems, recv_sems):
            (ps_ref,), (out_ref,) = ins, outs
            _, _, c, chips = _place()
            return [_remote(ps_ref.at[2 * cx + cy], out_ref.at[j], send_sems, recv_sems, j, (cx, cy, c))
                    for j, (cx, cy) in enumerate(chips)]

        def start(*refs):
            for cp in copies(*refs):
                cp.start()

        def wait(*refs):
            for cp in copies(*refs):
                cp.wait()

        return _Riding((ps,), (jax.ShapeDtypeStruct((3,) + ps.shape[1:], ps.dtype),), 3, start, wait)

    def reduced(self, group):
        return _chip_sum("sum_chips_" + group, self.sums[group], self.landed[group], self.chip, self.TILES[group])


def _adamw(w, g, m, v):
    m = ADAM_B1 * m + (1.0 - ADAM_B1) * g
    v = ADAM_B2 * v + (1.0 - ADAM_B2) * (g * g)
    m_hat = m / (1.0 - ADAM_B1 ** ADAM_STEP)
    v_hat = v / (1.0 - ADAM_B2 ** ADAM_STEP)
    return -ADAM_LR * (m_hat / (jnp.sqrt(v_hat) + ADAM_EPS) + ADAM_WD * w), m, v


def _small_reduce_adamw(part, w, m, v):
    def body(part_ref, w_ref, m_ref, v_ref, g_ref, d_ref, nm_ref, nv_ref, all_ref, send_sems, recv_sems):
        x, y, c, chips = _place()
        me, sibling = (x, y, c), (x, y, 1 - c)

        def rows(px, py, pc):
            return all_ref.at[4 * px + 2 * py + pc]

        all_ref[4 * x + 2 * y + c] = part_ref[...]
        first = [_remote(part_ref, rows(*me), send_sems, recv_sems, 0, sibling)]
        first += [_remote(part_ref, rows(*me), send_sems, recv_sems, 1 + j, (cx, cy, c))
                  for j, (cx, cy) in enumerate(chips)]
        for cp in first:
            cp.start()
        passed = []
        for j, (cx, cy) in enumerate(chips):
            _remote(part_ref, rows(cx, cy, c), send_sems, recv_sems, 1 + j, me).wait_recv()
            cp = _remote(rows(cx, cy, c), rows(cx, cy, c), send_sems, recv_sems, 4 + j, sibling)
            cp.start()
            passed.append(cp)
        _remote(part_ref, rows(*sibling), send_sems, recv_sems, 0, me).wait_recv()
        for j, (cx, cy) in enumerate(chips):
            _remote(part_ref, rows(cx, cy, 1 - c), send_sems, recv_sems, 4 + j, me).wait_recv()
        for cp in first + passed:
            cp.wait_send()
        g = all_ref[0]
        for dev in range(1, N_DEV):
            g = g + all_ref[dev]
        delta, nm, nv = _adamw(w_ref[...], g, m_ref[...], v_ref[...])
        g_ref[...] = g
        d_ref[...] = delta
        nm_ref[...] = nm
        nv_ref[...] = nv

    whole = pl.BlockSpec(memory_space=pltpu.VMEM)
    shape = jax.ShapeDtypeStruct((SMALL_ROWS, PACK_W), F32)
    return pl.pallas_call(
        body, name="small_reduce_adamw", in_specs=[whole] * 4, out_specs=[whole] * 4, out_shape=[shape] * 4,
        scratch_shapes=[pltpu.VMEM((N_DEV, SMALL_ROWS, PACK_W), F32), pltpu.SemaphoreType.DMA((7,)),
                        pltpu.SemaphoreType.DMA((7,))],
        compiler_params=pltpu.CompilerParams(vmem_limit_bytes=VMEM_LIMIT))(part, w, m, v)


def kernel(x, p, norm_g, w_in, hg_lb, hg_norm_g, w_o_hg, s5_a_re, s5_a_im, s5_log_dt, s5_b_re, s5_b_im, s5_c_re, s5_c_im, s5_d, w_glu, b_glu, w_o_s5, w_out, ple_norm_g, w_ple, w_ple_gate, final_norm_g, loss_target, m_norm_g, m_w_in, m_hg_lb, m_hg_norm_g, m_w_o_hg, m_s5_a_re, m_s5_a_im, m_s5_log_dt, m_s5_b_re, m_s5_b_im, m_s5_c_re, m_s5_c_im, m_s5_d, m_w_glu, m_b_glu, m_w_o_s5, m_w_out, m_ple_norm_g, m_w_ple, m_w_ple_gate, m_final_norm_g, v_norm_g, v_w_in, v_hg_lb, v_hg_norm_g, v_w_o_hg, v_s5_a_re, v_s5_a_im, v_s5_log_dt, v_s5_b_re, v_s5_b_im, v_s5_c_re, v_s5_c_im, v_s5_d, v_w_glu, v_b_glu, v_w_o_s5, v_w_out, v_ple_norm_g, v_w_ple, v_w_ple_gate, v_final_norm_g):
    given = dict(locals())
    wts = {n: given[n] for n in WEIGHTS}
    mom = {n: given["m_" + n] for n in WEIGHTS}
    var = {n: given["v_" + n] for n in WEIGHTS}
    cx, cy, cc = lax.axis_index("x"), lax.axis_index("y"), lax.axis_index("c")
    chip = (2 * cx + cy).astype(jnp.int32)

    core = cc.astype(jnp.int32)
    rest_shard = _pack_rest({n: wts[n][0] for n in REST})
    in_wire = wts["w_in"][0].astype(MXU_DTYPE).reshape(2, D_MODEL // 2, IN_SHARD)
    (w_in_all,) = _gather_shards([in_wire])
    w_in_all = lax.dynamic_update_slice(w_in_all, in_wire[None], (chip, 0, 0, 0)).reshape(N_CHIPS, D_MODEL, IN_SHARD)
    comm = _StepComm(rest_shard.astype(MXU_DTYPE).reshape(2, REST_ROWS // 2, PACK_W), chip, core)

    t_len = x.shape[1]
    loss_row, grad_x, g_big, g_small = _local_step(x.reshape(t_len, D_MODEL), p.reshape(t_len, -1),
                                                   loss_target.reshape(t_len, D_MODEL), {"w_in": w_in_all},
                                                   {n: wts[n] for n in SMALL}, comm)

    zero = jnp.zeros((), F32)
    sg, sd, snm, snv = _small_reduce_adamw(_pack_small(g_small, loss_row[0, 0]),
                                           _pack_small({n: wts[n] for n in SMALL}, zero),
                                           _pack_small({n: mom[n] for n in SMALL}, zero),
                                           _pack_small({n: var[n] for n in SMALL}, zero))
    (sg, loss), (sd, _), (snm, _), (snv, _) = (_unpack_small(a) for a in (sg, sd, snm, snv))

    halves = [comm.reduced("in"), comm.reduced("rest")]
    g_in, g_rest = [lax.dynamic_update_slice(got, mine[None], (core, 0, 0))
                    for got, mine in zip(_share_halves(halves), halves)]
    g_in, g_rest = g_in.reshape(D_MODEL, IN_SHARD), g_rest.reshape(REST_ROWS, PACK_W)

    def adam_f(wv, gv, mv, vv):
        return _adamw(wv, gv, mv, vv)

    d_in, nm_in, nv_in = _rowwise("adamw_in", adam_f, D_MODEL, IN_TILE,
                                  [(wts["w_in"][0], IN_SHARD, 0), (g_in, IN_SHARD, 0), (mom["w_in"][0], IN_SHARD, 0),
                                   (var["w_in"][0], IN_SHARD, 0)], [], [(IN_SHARD, F32)] * 3)
    d_rest, nm_rest, nv_rest = _rowwise("adamw_rest", adam_f, REST_ROWS, REST_TILE,
                                        [(rest_shard, PACK_W, 0), (g_rest, PACK_W, 0),
                                         (_pack_rest({n: mom[n][0] for n in REST}), PACK_W, 0),
                                         (_pack_rest({n: var[n][0] for n in REST}), PACK_W, 0)], [],
                                        [(PACK_W, F32)] * 3)
    bg, bd, bnm, bnv = (dict(_unpack_rest(rest), w_in=a.reshape(1, D_MODEL, IN_SHARD))
                        for rest, a in ((g_rest, g_in), (d_rest, d_in), (nm_rest, nm_in), (nv_rest, nv_in)))

    outs = [loss, grad_x.reshape(x.shape)]
    for small, big in ((sg, bg), (sd, bd), (snm, bnm), (snv, bnv)):
        outs += [big[n] if n in BIG else small[n] for n in WEIGHTS]
    return tuple(outs)
```

```python
import functools
from typing import Callable, NamedTuple

import jax
import jax.numpy as jnp
from jax import lax
from jax.experimental import pallas as pl
from jax.experimental.pallas import tpu as pltpu

F32 = jnp.float32
MXU_DTYPE = jnp.bfloat16
WIRE_DTYPE = jnp.bfloat16
NORM_EPS = 1e-6
D_MODEL = 1024
HG_HEADS = 8
HG_DIM = 128
HG_CHUNK = 64
S5_WIDTH = 512
S5_GROUPS = 32
S5_GROUP = 16
S5_STATE = 64
S5_LANES = S5_GROUPS * S5_STATE
IN_COLS = 7168
SUBLANES = 8
VMEM_LIMIT = 56 * 1024 * 1024
HIGHEST = lax.Precision.HIGHEST
MESH = pl.DeviceIdType.MESH

ADAM_LR, ADAM_B1, ADAM_B2, ADAM_EPS, ADAM_WD, ADAM_STEP = 0.001, 0.9, 0.999, 1e-08, 0.01, 10

BIG = ("w_in", "w_o_hg", "w_glu", "w_o_s5", "w_out", "w_ple", "w_ple_gate")
BIG_SHAPE = {"w_in": (1024, 7168), "w_o_hg": (1024, 1024), "w_glu": (512, 1024), "w_o_s5": (512, 1024),
             "w_out": (1024, 1024), "w_ple": (256, 1024), "w_ple_gate": (1024, 1024)}
BIG_COL_SHARDED = ("w_in", "w_glu", "w_o_s5", "w_ple")
SMALL = ("norm_g", "hg_lb", "hg_norm_g", "s5_a_re", "s5_a_im", "s5_log_dt", "s5_b_re", "s5_b_im", "s5_c_re",
         "s5_c_im", "s5_d", "b_glu", "ple_norm_g", "final_norm_g")
SMALL_SHAPE = {"norm_g": (1, 1024), "hg_lb": (2, 1024), "hg_norm_g": (1, 1024), "s5_a_re": (1, 32, 64),
               "s5_a_im": (1, 32, 64), "s5_log_dt": (1, 32), "s5_b_re": (1, 32, 64, 16), "s5_b_im": (1, 32, 64, 16),
               "s5_c_re": (1, 32, 16, 64), "s5_c_im": (1, 32, 16, 64), "s5_d": (1, 32, 16), "b_glu": (1, 1024),
               "ple_norm_g": (1, 1024), "final_norm_g": (1024,)}
WEIGHTS = ("norm_g", "w_in", "hg_lb", "hg_norm_g", "w_o_hg", "s5_a_re", "s5_a_im", "s5_log_dt", "s5_b_re", "s5_b_im",
           "s5_c_re", "s5_c_im", "s5_d", "w_glu", "b_glu", "w_o_s5", "w_out", "ple_norm_g", "w_ple", "w_ple_gate",
           "final_norm_g")
N_CHIPS = 4
N_DEV = 8
PACK_W = 1024
SHARD_ROWS = sum(BIG_SHAPE[n][0] * BIG_SHAPE[n][1] for n in BIG) // (N_CHIPS * PACK_W)
HALF_ROWS = SHARD_ROWS // 2
SMALL_ROWS = 144


def _params(*sem):
    return pltpu.CompilerParams(dimension_semantics=sem, vmem_limit_bytes=VMEM_LIMIT)


def _sig(x):
    return 1.0 / (1.0 + jnp.exp(-x))


def _dsilu(z, s):
    return s * (1.0 + z * (1.0 - s))


def _mx(x):
    return x.astype(MXU_DTYPE)


def _dot(a, b, dims=(((1,), (0,)), ((), ()))):
    return lax.dot_general(_mx(a), _mx(b), dims, preferred_element_type=F32)


_NT = (((1,), (1,)), ((), ()))
_TN = (((0,), (0,)), ((), ()))


def _dot32(a, b):
    return jnp.dot(a, b, precision=HIGHEST, preferred_element_type=F32)


def _rms_bwd(dy, x, g):
    r = lax.rsqrt(jnp.mean(x * x, axis=-1, keepdims=True) + NORM_EPS)
    t = dy * g
    dx = r * t - x * (r * r * r) * jnp.mean(t * x, axis=-1, keepdims=True)
    return dx, jnp.sum(dy * x * r, axis=0, keepdims=True)


def _rowwise(name, fn, n_rows_total, tm, rows, consts, outs, accs=(), alias=None):
    n_r, n_c, n_o, n_a = len(rows), len(consts), len(outs), len(accs)

    def body(*refs):
        row_refs = refs[:n_r]
        const_refs = refs[n_r:n_r + n_c]
        pos = n_r + n_c + (1 if alias is not None else 0)
        out_refs = refs[pos:pos + n_o]
        acc_refs = refs[pos + n_o:pos + n_o + n_a]
        res = fn(*[r[...] for r in row_refs], *[r[...] for r in const_refs])
        for r, v in zip(out_refs, res[:n_o]):
            r[...] = v.astype(r.dtype)
        if n_a:
            @pl.when(pl.program_id(0) == 0)
            def _():
                for r in acc_refs:
                    r[...] = jnp.zeros_like(r)
            for r, v in zip(acc_refs, res[n_o:]):
                r[...] += v

    in_specs = [pl.BlockSpec((tm, w), functools.partial(lambda i, cb: (i, cb), cb=cb)) for (_, w, cb) in rows]
    in_specs += [pl.BlockSpec(c.shape, lambda i: (0, 0)) for c in consts]
    args = [a for (a, _, _) in rows] + list(consts)
    out_shape, out_specs = [], []
    for o in outs:
        w, dt = o[0], o[1]
        cb, total = (o[2], o[3]) if len(o) == 4 else (0, w)
        out_shape.append(jax.ShapeDtypeStruct((n_rows_total, total), dt))
        out_specs.append(pl.BlockSpec((tm, w), functools.partial(lambda i, cb: (i, cb), cb=cb)))
    io_alias = {}
    if alias is not None:
        in_specs.append(pl.BlockSpec(memory_space=pl.ANY))
        args.append(alias[0])
        io_alias = {len(args) - 1: alias[1]}
    for (r, w) in accs:
        out_shape.append(jax.ShapeDtypeStruct((r, w), F32))
        out_specs.append(pl.BlockSpec((r, w), lambda i: (0, 0)))
    res = pl.pallas_call(body, name=name, grid=(n_rows_total // tm,), in_specs=in_specs, out_specs=out_specs,
                         out_shape=out_shape, input_output_aliases=io_alias,
                         compiler_params=_params("arbitrary"))(*args)
    return res


class _Riding(NamedTuple):
    ins: tuple
    outs: tuple
    n_sems: int
    start: Callable
    wait: Callable


_HBM = pl.BlockSpec(memory_space=pl.ANY)


def _ride(riding, refs, n_in, n_out, n_scratch, first, last):
    if riding is None:
        return refs[:n_in], refs[n_in:n_in + n_out], refs[n_in + n_out:]
    r_in, r_out = len(riding.ins), len(riding.outs)
    ins, rins = refs[:n_in], refs[n_in:n_in + r_in]
    pos = n_in + r_in
    outs, routs = refs[pos:pos + n_out], refs[pos + n_out:pos + n_out + r_out]
    pos += n_out + r_out
    scratch, (send_sems, recv_sems) = refs[pos:pos + n_scratch], refs[pos + n_scratch:]

    @pl.when(first)
    def _():
        riding.start(rins, routs, send_sems, recv_sems)

    @pl.when(last)
    def _():
        riding.wait(rins, routs, send_sems, recv_sems)

    return ins, outs, scratch


def _riding_call(riding, body, name, grid, in_specs, args, out_specs, out_shape, scratch, io_alias=None):
    if riding is not None:
        in_specs = list(in_specs) + [_HBM] * len(riding.ins)
        args = list(args) + list(riding.ins)
        out_specs = list(out_specs) + [_HBM] * len(riding.outs)
        out_shape = list(out_shape) + list(riding.outs)
        scratch = list(scratch) + [pltpu.SemaphoreType.DMA((riding.n_sems,))] * 2
    return pl.pallas_call(body, name=name, grid=grid, in_specs=in_specs, out_specs=out_specs, out_shape=out_shape,
                          scratch_shapes=scratch, input_output_aliases=io_alias or {},
                          compiler_params=_params("arbitrary", "arbitrary"))(*args)


def _mm_nn(name, a, b, tm, tn, riding=None, prologue=None, consts=()):
    m, k = a.shape
    n = b.shape[1] if b.ndim == 2 else b.shape[0] * b.shape[2]
    grid = (n // tn, m // tm)
    n_out = 1 if prologue is None else 2

    def body(*refs):
        j, i = pl.program_id(0), pl.program_id(1)
        ins, outs, _ = _ride(riding, refs, 2 + len(consts), n_out, 0, (j == 0) & (i == 0),
                             (j == grid[0] - 1) & (i == grid[1] - 1))
        left = ins[0][...]
        if prologue is not None:
            left = _mx(prologue(left, *[c[...] for c in ins[2:]]))

            @pl.when(j == 0)
            def _():
                outs[1][...] = left
        outs[0][...] = _dot(left, ins[1][...])

    b_spec = (pl.BlockSpec((k, tn), lambda j, i: (0, j)) if b.ndim == 2
              else pl.BlockSpec((None, k, tn), lambda j, i: (j, 0, 0)))
    in_specs = [pl.BlockSpec((tm, k), lambda j, i: (i, 0)), b_spec]
    in_specs += [pl.BlockSpec(c.shape, lambda j, i: (0, 0)) for c in consts]
    out_specs = [pl.BlockSpec((tm, tn), lambda j, i: (i, j))]
    out_shape = [jax.ShapeDtypeStruct((m, n), F32)]
    if prologue is not None:
        out_specs.append(pl.BlockSpec((tm, k), lambda j, i: (jnp.where(j == 0, i, grid[1] - 1), 0)))
        out_shape.append(jax.ShapeDtypeStruct((m, k), MXU_DTYPE))
    res = _riding_call(riding, body, name, grid, in_specs, [a, b] + list(consts), out_specs, out_shape, [])
    return res[0] if riding is None and prologue is None else res


def _mm_nt(name, a, b, tm, tn):
    m, n = a.shape
    k = b.shape[0]
    steps = n // tn

    def body(a_ref, b_ref, o_ref, acc_ref):
        s = pl.program_id(1)

        @pl.when(s == 0)
        def _():
            acc_ref[...] = jnp.zeros_like(acc_ref)

        acc_ref[...] += _dot(a_ref[...], b_ref[...], _NT)

        @pl.when(s == steps - 1)
        def _():
            o_ref[...] = acc_ref[...]

    return pl.pallas_call(body, name=name, grid=(m // tm, steps),
                          in_specs=[pl.BlockSpec((tm, tn), lambda i, s: (i, s)),
                                    pl.BlockSpec((k, tn), lambda i, s: (0, s))],
                          out_specs=pl.BlockSpec((tm, k), lambda i, s: (i, 0)),
                          out_shape=jax.ShapeDtypeStruct((m, k), F32),
                          scratch_shapes=[pltpu.VMEM((tm, k), F32)],
                          compiler_params=_params("arbitrary", "arbitrary"))(a, b)


def _mm_nt_then(name, a, b, tm, tn, fn, rows, consts, outs, accs=(), alias=None, riding=None):
    m, n = a.shape
    k = b.shape[-2]
    steps = n // tn
    n_r, n_c, n_o, n_a = len(rows), len(consts), len(outs), len(accs)

    def body(*refs):
        a_ref, b_ref = refs[:2]
        row_refs = refs[2:2 + n_r]
        const_refs = refs[2 + n_r:2 + n_r + n_c]
        i, s = pl.program_id(0), pl.program_id(1)
        n_in = 2 + n_r + n_c + (1 if alias is not None else 0)
        _, outs_, (mm_ref,) = _ride(riding, refs, n_in, n_o + n_a, 1, (i == 0) & (s == 0),
                                    (i == m // tm - 1) & (s == steps - 1))
        out_refs, acc_refs = outs_[:n_o], outs_[n_o:]
        part = _dot(a_ref[...], b_ref[...], _NT)
        if steps > 1:
            @pl.when(s == 0)
            def _():
                mm_ref[...] = jnp.zeros_like(mm_ref)
            mm_ref[...] += part

        @pl.when(s == steps - 1)
        def _():
            res = fn(mm_ref[...] if steps > 1 else part, *[r[...] for r in row_refs], *[r[...] for r in const_refs])
            for r, v in zip(out_refs, res[:n_o]):
                r[...] = v.astype(r.dtype)
            if n_a:
                @pl.when(i == 0)
                def _():
                    for r in acc_refs:
                        r[...] = jnp.zeros_like(r)
                for r, v in zip(acc_refs, res[n_o:]):
                    r[...] += v

    b_spec = (pl.BlockSpec((k, tn), lambda i, s: (0, s)) if b.ndim == 2
              else pl.BlockSpec((None, k, tn), lambda i, s: (s, 0, 0)))
    in_specs = [pl.BlockSpec((tm, tn), lambda i, s: (i, s)), b_spec]
    in_specs += [pl.BlockSpec((tm, w), functools.partial(lambda i, s, cb: (i, cb), cb=cb)) for (_, w, cb) in rows]
    in_specs += [pl.BlockSpec(c.shape, lambda i, s: (0, 0)) for c in consts]
    args = [a, b] + [r[0] for r in rows] + list(consts)
    out_shape, out_specs = [], []
    for o in outs:
        w, dt = o[0], o[1]
        cb, total = (o[2], o[3]) if len(o) == 4 else (0, w)
        out_shape.append(jax.ShapeDtypeStruct((m, total), dt))
        out_specs.append(pl.BlockSpec((tm, w), functools.partial(lambda i, s, cb: (i, cb), cb=cb)))
    io_alias = {}
    if alias is not None:
        in_specs.append(pl.BlockSpec(memory_space=pl.ANY))
        args.append(alias[0])
        io_alias = {len(args) - 1: alias[1]}
    for (r, w) in accs:
        out_shape.append(jax.ShapeDtypeStruct((r, w), F32))
        out_specs.append(pl.BlockSpec((r, w), lambda i, s: (0, 0)))
    return _riding_call(riding, body, name, (m // tm, steps), in_specs, args, out_specs, out_shape,
                        [pltpu.VMEM((tm, k), F32)], io_alias)


def _mm_tn(name, a, b, tk, tn, col_shards=False, riding=None):
    t, k = a.shape
    n = b.shape[1]
    steps = t // tk

    def body(*refs):
        j, s = pl.program_id(0), pl.program_id(1)
        (a_ref, b_ref), (o_ref,), (acc_ref,) = _ride(riding, refs, 2, 1, 1, (j == 0) & (s == 0),
                                                     (j == n // tn - 1) & (s == steps - 1))

        @pl.when(s == 0)
        def _():
            acc_ref[...] = jnp.zeros_like(acc_ref)

        acc_ref[...] += _dot(a_ref[...], b_ref[...], _TN)

        @pl.when(s == steps - 1)
        def _():
            o_ref[...] = acc_ref[...]

    if col_shards:
        out_spec = pl.BlockSpec((None, k, tn), lambda j, s: (j, 0, 0))
        out_shape = jax.ShapeDtypeStruct((n // tn, k, tn), F32)
    else:
        out_spec = pl.BlockSpec((k, tn), lambda j, s: (0, j))
        out_shape = jax.ShapeDtypeStruct((k, n), F32)
    res = _riding_call(riding, body, name, (n // tn, steps),
                       [pl.BlockSpec((tk, k), lambda j, s: (s, 0)), pl.BlockSpec((tk, tn), lambda j, s: (s, j))],
                       [a, b], [out_spec], [out_shape], [pltpu.VMEM((k, tn), F32)])
    return res[0] if riding is None else res


def _hg_chunk_terms(q, f, lb):
    sig = _sig(f)
    fv = lb + (1.0 - lb) * sig
    kk = (1.0 - lb) * (1.0 - sig)
    row = lax.broadcasted_iota(jnp.int32, (HG_CHUNK, HG_CHUNK), 0)
    col = lax.broadcasted_iota(jnp.int32, (HG_CHUNK, HG_CHUNK), 1)
    b = _dot32((row >= col).astype(F32), jnp.log(fv))
    b_mid = b[HG_CHUNK // 2 - 1:HG_CHUNK // 2, :]
    b_last = b[HG_CHUNK - 1:HG_CHUNK, :]
    e_mid = jnp.exp(b - b_mid)
    e_mid_inv = jnp.exp(b_mid - b)
    e_b = jnp.exp(b)
    e_last = jnp.exp(b_last - b)
    return sig, fv, kk, row >= col, row <= col, q * e_mid, kk * e_mid_inv, e_mid, e_mid_inv, e_b, e_last, jnp.exp(b_last)


def _hgrn2_fwd(proj, hg_lb, hg_norm_g, t_len, tb):
    nck = tb // HG_CHUNK

    def body(p_ref, lb_ref, gn_ref, o_ref, act_ref, sp_ref, st_ref):
        @pl.when(pl.program_id(0) == 0)
        def _():
            st_ref[...] = jnp.zeros_like(st_ref)

        for c in range(nck):
            r = pl.ds(c * HG_CHUNK, HG_CHUNK)
            for h in range(HG_HEADS):
                hs = pl.ds(h * HG_DIM, HG_DIM)
                lb = _sig(lb_ref[0:1, hs] - lb_ref[1:2, hs])
                q = p_ref[r, pl.ds(h * HG_DIM, HG_DIM)]
                f = p_ref[r, pl.ds(1024 + h * HG_DIM, HG_DIM)]
                v = p_ref[r, pl.ds(2048 + h * HG_DIM, HG_DIM)]
                _, _, kk, causal, _, a, bm, _, _, e_b, e_last, dc = _hg_chunk_terms(q, f, lb)
                scores = jnp.where(causal, _dot(a, bm, _NT), 0.0)
                st = st_ref[h]
                o = _dot(scores, v) + _dot(q * e_b, st, _NT)
                sp_ref[h, c] = st
                st_ref[h] = dc * st + _dot(v, kk * e_last, _TN)
                o_ref[r, hs] = o

        for h in range(HG_HEADS):
            hs = pl.ds(h * HG_DIM, HG_DIM)
            o = o_ref[:, hs]
            rr = lax.rsqrt(jnp.mean(o * o, axis=-1, keepdims=True) + NORM_EPS)
            g = p_ref[:, pl.ds(3072 + h * HG_DIM, HG_DIM)]
            act_ref[:, hs] = (o * rr * gn_ref[:, hs] * (g * _sig(g))).astype(act_ref.dtype)

    nb = t_len // tb
    return pl.pallas_call(
        body, name="hgrn2_fwd", grid=(nb,),
        in_specs=[pl.BlockSpec((tb, 4096), lambda i: (i, 0)),
                  pl.BlockSpec((2, 1024), lambda i: (0, 0)),
                  pl.BlockSpec((1, 1024), lambda i: (0, 0))],
        out_specs=[pl.BlockSpec((tb, 1024), lambda i: (i, 0)),
                   pl.BlockSpec((tb, 1024), lambda i: (i, 0)),
                   pl.BlockSpec((HG_HEADS, nck, HG_DIM, HG_DIM), lambda i: (0, i, 0, 0))],
        out_shape=[jax.ShapeDtypeStruct((t_len, 1024), F32),
                   jax.ShapeDtypeStruct((t_len, 1024), MXU_DTYPE),
                   jax.ShapeDtypeStruct((HG_HEADS, t_len // HG_CHUNK, HG_DIM, HG_DIM), F32)],
        scratch_shapes=[pltpu.VMEM((HG_HEADS, HG_DIM, HG_DIM), F32)],
        compiler_params=_params("arbitrary"))(proj, hg_lb, hg_norm_g)


def _hgrn2_bwd(proj, d_o, s_prev, hg_lb, dproj, t_len, tb):
    nck = tb // HG_CHUNK
    nb = t_len // tb

    def body(p_ref, do_ref, sp_ref, lb_ref, _, dp_ref, dlb_ref, ds_ref, acc_ref):
        @pl.when(pl.program_id(0) == 0)
        def _():
            ds_ref[...] = jnp.zeros_like(ds_ref)
            acc_ref[...] = jnp.zeros_like(acc_ref)

        for c in reversed(range(nck)):
            r = pl.ds(c * HG_CHUNK, HG_CHUNK)
            for h in range(HG_HEADS):
                hs = pl.ds(h * HG_DIM, HG_DIM)
                lb = _sig(lb_ref[0:1, hs] - lb_ref[1:2, hs])
                q = p_ref[r, pl.ds(h * HG_DIM, HG_DIM)]
                f = p_ref[r, pl.ds(1024 + h * HG_DIM, HG_DIM)]
                v = p_ref[r, pl.ds(2048 + h * HG_DIM, HG_DIM)]
                do = do_ref[r, hs]
                sig, fv, kk, causal, anti, a, bm, e_mid, e_mid_inv, e_b, e_last, dc = _hg_chunk_terms(q, f, lb)
                qd = q * e_b
                kd = kk * e_last
                st = sp_ref[h, c]
                dst = ds_ref[h]
                scores = jnp.where(causal, _dot(a, bm, _NT), 0.0)
                dscores = jnp.where(causal, _dot(do, v, _NT), 0.0)
                dv = _dot(scores, do, _TN) + _dot(kd, dst, _NT)
                da = _dot(dscores, bm)
                dbm = _dot(dscores, a, _TN)
                dqd = _dot(do, st)
                dkd = _dot(v, dst)
                ddc = jnp.sum(dst * st, axis=0, keepdims=True)
                ds_ref[h] = _dot(do, qd, _TN) + dc * dst
                dq = da * e_mid + dqd * e_b
                dk = dbm * e_mid_inv + dkd * e_last
                db = da * a - dbm * bm + dqd * qd - dkd * kd
                extra = jnp.sum(dkd * kd, axis=0, keepdims=True) + ddc * dc
                dlogf = _dot32(anti.astype(F32), db) + extra
                dfv_k = dlogf / fv - dk
                dp_ref[r, pl.ds(h * HG_DIM, HG_DIM)] = dq
                dp_ref[r, pl.ds(1024 + h * HG_DIM, HG_DIM)] = dfv_k * (1.0 - lb) * sig * (1.0 - sig)
                dp_ref[r, pl.ds(2048 + h * HG_DIM, HG_DIM)] = dv
                acc_ref[:, hs] += jnp.sum(dfv_k * (1.0 - sig), axis=0, keepdims=True)

        @pl.when(pl.program_id(0) == nb - 1)
        def _():
            lb_all = _sig(lb_ref[0:1, :] - lb_ref[1:2, :])
            g0 = acc_ref[...] * lb_all * (1.0 - lb_all)
            dlb_ref[0:1, :] = g0
            dlb_ref[1:2, :] = -g0

    return pl.pallas_call(
        body, name="hgrn2_bwd", grid=(nb,),
        in_specs=[pl.BlockSpec((tb, 3072), lambda i: (nb - 1 - i, 0)),
                  pl.BlockSpec((tb, 1024), lambda i: (nb - 1 - i, 0)),
                  pl.BlockSpec((HG_HEADS, nck, HG_DIM, HG_DIM), lambda i: (0, nb - 1 - i, 0, 0)),
                  pl.BlockSpec((2, 1024), lambda i: (0, 0)),
                  pl.BlockSpec(memory_space=pl.ANY)],
        out_specs=[pl.BlockSpec((tb, 3072), lambda i: (nb - 1 - i, 0)),
                   pl.BlockSpec((2, 1024), lambda i: (0, 0))],
        out_shape=[jax.ShapeDtypeStruct((t_len, IN_COLS), F32), jax.ShapeDtypeStruct((2, 1024), F32)],
        scratch_shapes=[pltpu.VMEM((HG_HEADS, HG_DIM, HG_DIM), F32), pltpu.VMEM((1, 1024), F32)],
        input_output_aliases={4: 0},
        compiler_params=_params("arbitrary"))(proj, d_o, s_prev, hg_lb, dproj)


def _dot01(m01, x):
    m = m01.astype(MXU_DTYPE)
    hi = x.astype(MXU_DTYPE)
    r1 = x - hi.astype(F32)
    mid = r1.astype(MXU_DTYPE)
    lo = (r1 - mid.astype(F32)).astype(MXU_DTYPE)
    dot = lambda v: jnp.dot(m, v, preferred_element_type=F32)
    return dot(hi) + dot(mid) + dot(lo)


def _chunk_rows(x, offset, nck):
    return jnp.concatenate([jnp.broadcast_to(x[c * HG_CHUNK + offset:c * HG_CHUNK + offset + 1, :],
                                             (HG_CHUNK, x.shape[1])) for c in range(nck)], axis=0)


def _hg_block_terms(q, f, lb, tb):
    nck = tb // HG_CHUNK
    sig = _sig(f)
    fv = lb + (1.0 - lb) * sig
    kk = (1.0 - lb) * (1.0 - sig)
    row = lax.broadcasted_iota(jnp.int32, (tb, tb), 0)
    col = lax.broadcasted_iota(jnp.int32, (tb, tb), 1)
    same = jnp.right_shift(row, 6) == jnp.right_shift(col, 6)
    causal, anti = same & (row >= col), same & (row <= col)
    b = _dot01(causal, jnp.log(fv))
    b_mid, b_last = _chunk_rows(b, HG_CHUNK // 2 - 1, nck), _chunk_rows(b, HG_CHUNK - 1, nck)
    e_mid, e_mid_inv = jnp.exp(b - b_mid), jnp.exp(b_mid - b)
    e_b, e_last = jnp.exp(b), jnp.exp(b_last - b)
    dcs = [jnp.exp(b[c * HG_CHUNK + HG_CHUNK - 1:(c + 1) * HG_CHUNK, :]) for c in range(nck)]
    return sig, fv, kk, causal, anti, e_mid, e_mid_inv, e_b, e_last, dcs


def _hgrn2_fwd2(proj, hg_lb, hg_norm_g, t_len, tb):
    nck = tb // HG_CHUNK

    def body(p_ref, lb_ref, gn_ref, o_ref, act_ref, sp_ref, st_ref, a_s, bm_s, qd_s, kd_s, v_s):
        @pl.when(pl.program_id(0) == 0)
        def _():
            st_ref[...] = jnp.zeros_like(st_ref)

        lb = _sig(lb_ref[0:1, :] - lb_ref[1:2, :])
        q = p_ref[:, pl.ds(0, 1024)]
        _, _, kk, causal, _, e_mid, e_mid_inv, e_b, e_last, dcs = _hg_block_terms(q, p_ref[:, pl.ds(1024, 1024)],
                                                                                   lb, tb)
        a_s[...] = _mx(q * e_mid)
        bm_s[...] = _mx(kk * e_mid_inv)
        qd_s[...] = _mx(q * e_b)
        kd_s[...] = _mx(kk * e_last)
        v_s[...] = _mx(p_ref[:, pl.ds(2048, 1024)])
        for h in range(HG_HEADS):
            hs = pl.ds(h * HG_DIM, HG_DIM)
            scores = jnp.where(causal, _dot(a_s[:, hs], bm_s[:, hs], _NT), 0.0)
            o_ref[:, hs] = _dot(scores, v_s[:, hs])
        for h in range(HG_HEADS):
            hs = pl.ds(h * HG_DIM, HG_DIM)
            incs = [_dot(v_s[pl.ds(c * HG_CHUNK, HG_CHUNK), hs], kd_s[pl.ds(c * HG_CHUNK, HG_CHUNK), hs], _TN)
                    for c in range(nck)]
            st = st_ref[h]
            for c in range(nck):
                sp_ref[h, c] = st
                st = dcs[c][:, h * HG_DIM:(h + 1) * HG_DIM] * st + incs[c]
            st_ref[h] = st
        for h in range(HG_HEADS):
            hs = pl.ds(h * HG_DIM, HG_DIM)
            for c in range(nck):
                r = pl.ds(c * HG_CHUNK, HG_CHUNK)
                o_ref[r, hs] += _dot(qd_s[r, hs], sp_ref[h, c], _NT)
        for h in range(HG_HEADS):
            hs = pl.ds(h * HG_DIM, HG_DIM)
            o = o_ref[:, hs]
            rr = lax.rsqrt(jnp.mean(o * o, axis=-1, keepdims=True) + NORM_EPS)
            g = p_ref[:, pl.ds(3072 + h * HG_DIM, HG_DIM)]
            act_ref[:, hs] = (o * rr * gn_ref[:, hs] * (g * _sig(g))).astype(act_ref.dtype)

    nb = t_len // tb
    return pl.pallas_call(
        body, name="hgrn2_fwd", grid=(nb,),
        in_specs=[pl.BlockSpec((tb, 4096), lambda i: (i, 0)),
                  pl.BlockSpec((2, 1024), lambda i: (0, 0)),
                  pl.BlockSpec((1, 1024), lambda i: (0, 0))],
        out_specs=[pl.BlockSpec((tb, 1024), lambda i: (i, 0)),
                   pl.BlockSpec((tb, 1024), lambda i: (i, 0)),
                   pl.BlockSpec((HG_HEADS, nck, HG_DIM, HG_DIM), lambda i: (0, i, 0, 0))],
        out_shape=[jax.ShapeDtypeStruct((t_len, 1024), F32),
                   jax.ShapeDtypeStruct((t_len, 1024), MXU_DTYPE),
                   jax.ShapeDtypeStruct((HG_HEADS, t_len // HG_CHUNK, HG_DIM, HG_DIM), F32)],
        scratch_shapes=[pltpu.VMEM((HG_HEADS, HG_DIM, HG_DIM), F32)] + [pltpu.VMEM((tb, 1024), MXU_DTYPE)] * 5,
        compiler_params=_params("arbitrary"))(proj, hg_lb, hg_norm_g)


def _hgrn2_bwd2(proj, d_o, s_prev, hg_lb, dproj, t_len, tb):
    nck = tb // HG_CHUNK
    nb = t_len // tb

    def body(p_ref, do_ref, sp_ref, lb_ref, _, dp_ref, dlb_ref, ds_ref, acc_ref,
             a_s, bm_s, qd_s, kd_s, v_s, do_s, da_s, dbm_s, dqd_s, dkd_s, dv_s, ex_s):
        @pl.when(pl.program_id(0) == 0)
        def _():
            ds_ref[...] = jnp.zeros_like(ds_ref)
            acc_ref[...] = jnp.zeros_like(acc_ref)

        lb = _sig(lb_ref[0:1, :] - lb_ref[1:2, :])
        q = p_ref[:, pl.ds(0, 1024)]
        sig, fv, kk, causal, anti, e_mid, e_mid_inv, e_b, e_last, dcs = _hg_block_terms(
            q, p_ref[:, pl.ds(1024, 1024)], lb, tb)
        a, bm, qd, kd = q * e_mid, kk * e_mid_inv, q * e_b, kk * e_last
        a_s[...] = _mx(a)
        bm_s[...] = _mx(bm)
        qd_s[...] = _mx(qd)
        kd_s[...] = _mx(kd)
        v_s[...] = _mx(p_ref[:, pl.ds(2048, 1024)])
        do_s[...] = _mx(do_ref[...])
        for h in range(HG_HEADS):
            hs = pl.ds(h * HG_DIM, HG_DIM)
            scores = jnp.where(causal, _dot(a_s[:, hs], bm_s[:, hs], _NT), 0.0)
            dscores = _mx(jnp.where(causal, _dot(do_s[:, hs], v_s[:, hs], _NT), 0.0))
            dv_s[:, hs] = _dot(scores, do_s[:, hs], _TN)
            da_s[:, hs] = _dot(dscores, bm_s[:, hs])
            dbm_s[:, hs] = _dot(dscores, a_s[:, hs], _TN)
        for h in range(HG_HEADS):
            hs = pl.ds(h * HG_DIM, HG_DIM)
            ups = [_dot(do_s[pl.ds(c * HG_CHUNK, HG_CHUNK), hs], qd_s[pl.ds(c * HG_CHUNK, HG_CHUNK), hs], _TN)
                   for c in range(nck)]
            dst = ds_ref[h]
            for c in reversed(range(nck)):
                r = pl.ds(c * HG_CHUNK, HG_CHUNK)
                st = sp_ref[h, c]
                dc = dcs[c][:, h * HG_DIM:(h + 1) * HG_DIM]
                dv_s[r, hs] += _dot(kd_s[r, hs], dst, _NT)
                dqd_s[r, hs] = _dot(do_s[r, hs], st)
                dkd_s[r, hs] = _dot(v_s[r, hs], dst)
                ex_s[c:c + 1, hs] = jnp.sum(dst * st, axis=0, keepdims=True) * dc
                dst = ups[c] + dc * dst
            ds_ref[h] = dst
        da, dbm, dqd, dkd = da_s[...], dbm_s[...], dqd_s[...], dkd_s[...]
        dq = da * e_mid + dqd * e_b
        dk = dbm * e_mid_inv + dkd * e_last
        db = da * a - dbm * bm + dqd * qd - dkd * kd
        dkk = dkd * kd
        extra = jnp.concatenate(
            [jnp.broadcast_to(jnp.sum(dkk[c * HG_CHUNK:(c + 1) * HG_CHUNK], axis=0, keepdims=True)
                              + ex_s[c:c + 1, :], (HG_CHUNK, 1024)) for c in range(nck)], axis=0)
        dlogf = _dot01(anti, db) + extra
        dfv_k = dlogf / fv - dk
        dp_ref[:, pl.ds(0, 1024)] = dq.astype(dp_ref.dtype)
        dp_ref[:, pl.ds(1024, 1024)] = (dfv_k * (1.0 - lb) * sig * (1.0 - sig)).astype(dp_ref.dtype)
        dp_ref[:, pl.ds(2048, 1024)] = dv_s[...].astype(dp_ref.dtype)
        acc_ref[...] += jnp.sum(dfv_k * (1.0 - sig), axis=0, keepdims=True)

        @pl.when(pl.program_id(0) == nb - 1)
        def _():
            g0 = acc_ref[...] * lb * (1.0 - lb)
            dlb_ref[0:1, :] = g0
            dlb_ref[1:2, :] = -g0

    return pl.pallas_call(
        body, name="hgrn2_bwd", grid=(nb,),
        in_specs=[pl.BlockSpec((tb, 3072), lambda i: (nb - 1 - i, 0)),
                  pl.BlockSpec((tb, 1024), lambda i: (nb - 1 - i, 0)),
                  pl.BlockSpec((HG_HEADS, nck, HG_DIM, HG_DIM), lambda i: (0, nb - 1 - i, 0, 0)),
                  pl.BlockSpec((2, 1024), lambda i: (0, 0)),
                  pl.BlockSpec(memory_space=pl.ANY)],
        out_specs=[pl.BlockSpec((tb, 3072), lambda i: (nb - 1 - i, 0)),
                   pl.BlockSpec((2, 1024), lambda i: (0, 0))],
        out_shape=[jax.ShapeDtypeStruct((t_len, IN_COLS), dproj.dtype), jax.ShapeDtypeStruct((2, 1024), F32)],
        scratch_shapes=[pltpu.VMEM((HG_HEADS, HG_DIM, HG_DIM), F32), pltpu.VMEM((1, 1024), F32)]
                       + [pltpu.VMEM((tb, 1024), MXU_DTYPE)] * 6 + [pltpu.VMEM((tb, 1024), F32)] * 5
                       + [pltpu.VMEM((SUBLANES, 1024), F32)],
        input_output_aliases={4: 0},
        compiler_params=_params("arbitrary"))(proj, d_o, s_prev, hg_lb, dproj)


def _s5_prep(a_re, a_im, log_dt, b_re_t, b_im_t):
    def body(ar_ref, ai_ref, ldt_ref, br_ref, bi_ref, lam_ref, pr_ref, pi_ref, bbr_ref, bbi_ref):
        ar, ai = ar_ref[...], ai_ref[...]
        dt = jnp.exp(ldt_ref[...])
        mag = jnp.exp(ar * dt)
        lr, li = mag * jnp.cos(ai * dt), mag * jnp.sin(ai * dt)
        den = ar * ar + ai * ai
        nr = lr - 1.0
        sr = (nr * ar + li * ai) / den
        si = (li * ar - nr * ai) / den
        lam_ref[0:1, :] = lr
        lam_ref[1:2, :] = li
        cr, ci = lr, li
        for i in range(SUBLANES):
            pr_ref[i:i + 1, :] = cr
            pi_ref[i:i + 1, :] = ci
            cr, ci = cr * lr - ci * li, cr * li + ci * lr
        br, bi = br_ref[...], bi_ref[...]
        bbr_ref[...] = sr * br - si * bi
        bbi_ref[...] = sr * bi + si * br

    whole = pl.BlockSpec(memory_space=pltpu.VMEM)
    return pl.pallas_call(
        body, name="s5_prep", in_specs=[whole] * 5, out_specs=[whole] * 5,
        out_shape=[jax.ShapeDtypeStruct((2, S5_LANES), F32), jax.ShapeDtypeStruct((SUBLANES, S5_LANES), F32),
                   jax.ShapeDtypeStruct((SUBLANES, S5_LANES), F32), jax.ShapeDtypeStruct((S5_GROUP, S5_LANES), F32),
                   jax.ShapeDtypeStruct((S5_GROUP, S5_LANES), F32)])(a_re, a_im, log_dt, b_re_t, b_im_t)


def _s5_prep_bwd(a_re, a_im, log_dt, b_re_t, b_im_t, dlam, dbbr, dbbi):
    def body(ar_ref, ai_ref, ldt_ref, br_ref, bi_ref, dlam_ref, dbbr_ref, dbbi_ref,
             dar_ref, dai_ref, dldt_ref, dbr_ref, dbi_ref):
        ar, ai = ar_ref[...], ai_ref[...]
        dt = jnp.exp(ldt_ref[...])
        mag = jnp.exp(ar * dt)
        cs, sn = jnp.cos(ai * dt), jnp.sin(ai * dt)
        lr, li = mag * cs, mag * sn
        den = ar * ar + ai * ai
        nr = lr - 1.0
        sr = (nr * ar + li * ai) / den
        si = (li * ar - nr * ai) / den
        br, bi = br_ref[...], bi_ref[...]
        gbr, gbi = dbbr_ref[...], dbbi_ref[...]
        dbr_ref[...] = sr * gbr + si * gbi
        dbi_ref[...] = sr * gbi - si * gbr
        dsr = jnp.sum(gbr * br + gbi * bi, axis=0, keepdims=True)
        dsi = jnp.sum(gbi * br - gbr * bi, axis=0, keepdims=True)
        dnr = (dsr * ar - dsi * ai) / den
        dli = dlam_ref[1:2, :] + (dsr * ai + dsi * ar) / den
        dlr = dlam_ref[0:1, :] + dnr
        dden = -(dsr * sr + dsi * si) / den
        dar = (dsr * nr + dsi * li) / den + dden * 2.0 * ar
        dai = (dsr * li - dsi * nr) / den + dden * 2.0 * ai
        dmag = dlr * cs + dli * sn
        dth = mag * (dli * cs - dlr * sn)
        dar_ref[...] = dar + dmag * mag * dt
        dai_ref[...] = dai + dth * dt
        ddt = (dmag * mag * ar + dth * ai) * dt
        lane = lax.broadcasted_iota(jnp.int32, (S5_LANES, 128), 0) // S5_STATE
        grp = lax.broadcasted_iota(jnp.int32, (S5_LANES, 128), 1)
        dldt_ref[...] = _dot32(jnp.broadcast_to(ddt, (SUBLANES, S5_LANES)), (lane == grp).astype(F32))

    whole = pl.BlockSpec(memory_space=pltpu.VMEM)
    return pl.pallas_call(
        body, name="s5_prep_bwd", in_specs=[whole] * 8, out_specs=[whole] * 5,
        out_shape=[jax.ShapeDtypeStruct((1, S5_LANES), F32), jax.ShapeDtypeStruct((1, S5_LANES), F32),
                   jax.ShapeDtypeStruct((SUBLANES, 128), F32), jax.ShapeDtypeStruct((S5_GROUP, S5_LANES), F32),
                   jax.ShapeDtypeStruct((S5_GROUP, S5_LANES), F32)])(a_re, a_im, log_dt, b_re_t, b_im_t, dlam, dbbr,
                                                                      dbbi)


S5_LANE_CHUNK = 512


def _shift_rows(x, s, rowid):
    if s > 0:
        return jnp.where(rowid >= s, pltpu.roll(x, s, 0), 0.0)
    return jnp.where(rowid < SUBLANES + s, pltpu.roll(x, SUBLANES + s, 0), 0.0)


def _scan8(xr, xi, pr, pi, sign, rowid):
    for s, row in ((1, 0), (2, 1), (4, 3)):
        lr, li = pr[row:row + 1, :], pi[row:row + 1, :]
        sr, si = _shift_rows(xr, sign * s, rowid), _shift_rows(xi, sign * s, rowid)
        xr, xi = xr + lr * sr - li * si, xi + lr * si + li * sr
    return xr, xi


def _s5_fwd(proj, pw_re, pw_im, bbr_bd, bbi_bd, crt_bd, cit_bd, d_row, t_len, tb):
    ngrp = tb // SUBLANES

    def body(u_ref, pr_ref, pi_ref, bbr_ref, bbi_ref, crt_ref, cit_ref, d_ref,
             hr_ref, hi_ref, ypre_ref, ys_ref, cr_ref, ci_ref):
        @pl.when(pl.program_id(0) == 0)
        def _():
            cr_ref[...] = jnp.zeros_like(cr_ref)
            ci_ref[...] = jnp.zeros_like(ci_ref)

        u = u_ref[...]
        hr_ref[...] = _dot(u, bbr_ref[...])
        hi_ref[...] = _dot(u, bbi_ref[...])
        rowid = lax.broadcasted_iota(jnp.int32, (SUBLANES, S5_LANE_CHUNK), 0)
        for lc in range(S5_LANES // S5_LANE_CHUNK):
            ls = pl.ds(lc * S5_LANE_CHUNK, S5_LANE_CHUNK)
            pr, pi = pr_ref[:, ls], pi_ref[:, ls]

            def group(g, carry, ls=ls, pr=pr, pi=pi):
                cr, ci = carry
                r = pl.ds(pl.multiple_of(g * SUBLANES, SUBLANES), SUBLANES)
                xr, xi = _scan8(hr_ref[r, ls], hi_ref[r, ls], pr, pi, 1, rowid)
                xr, xi = xr + pr * cr - pi * ci, xi + pr * ci + pi * cr
                hr_ref[r, ls] = xr
                hi_ref[r, ls] = xi
                return xr[SUBLANES - 1:SUBLANES, :], xi[SUBLANES - 1:SUBLANES, :]

            cr, ci = lax.fori_loop(0, ngrp, group, (cr_ref[:, ls], ci_ref[:, ls]))
            cr_ref[:, ls] = cr
            ci_ref[:, ls] = ci
        y = _dot(hr_ref[...], crt_ref[...]) - _dot(hi_ref[...], cit_ref[...]) + d_ref[...] * u
        ypre_ref[...] = y
        ys_ref[...] = jax.nn.gelu(y, approximate=True).astype(ys_ref.dtype)

    whole = pl.BlockSpec(memory_space=pltpu.VMEM)
    return pl.pallas_call(
        body, name="s5_fwd", grid=(t_len // tb,),
        in_specs=[pl.BlockSpec((tb, S5_WIDTH), lambda i: (i, 4096 // S5_WIDTH))] + [whole] * 7,
        out_specs=[pl.BlockSpec((tb, S5_LANES), lambda i: (i, 0)), pl.BlockSpec((tb, S5_LANES), lambda i: (i, 0)),
                   pl.BlockSpec((tb, S5_WIDTH), lambda i: (i, 0)), pl.BlockSpec((tb, S5_WIDTH), lambda i: (i, 0))],
        out_shape=[jax.ShapeDtypeStruct((t_len, S5_LANES), F32), jax.ShapeDtypeStruct((t_len, S5_LANES), F32),
                   jax.ShapeDtypeStruct((t_len, S5_WIDTH), F32), jax.ShapeDtypeStruct((t_len, S5_WIDTH), MXU_DTYPE)],
        scratch_shapes=[pltpu.VMEM((1, S5_LANES), F32), pltpu.VMEM((1, S5_LANES), F32)],
        compiler_params=_params("arbitrary"))(proj, pw_re, pw_im, bbr_bd, bbi_bd, crt_bd, cit_bd, d_row)


def _dgelu(x):
    c, a = 0.7978845608028654, 0.044715
    th = jnp.tanh(c * (x + a * x * x * x))
    return 0.5 * (1.0 + th) + 0.5 * x * (1.0 - th * th) * c * (1.0 + 3.0 * a * x * x)


def _s5_bwd(dgelu, y_pre, proj, h_re, h_im, pwr_re, pwr_im, bbr_bd, bbi_bd, cr_bd, ci_bd, d_row, dproj, t_len, tb):
    ngrp = tb // SUBLANES
    nb = t_len // tb

    def body(dg_ref, yp_ref, u_ref, hr_ref, hi_ref, pr_ref, pi_ref, bbr_ref, bbi_ref, cr_ref, ci_ref, d_ref, _,
             du_ref, dbbr_ref, dbbi_ref, dcr_ref, dci_ref, dd_ref, dlam_ref,
             gr_ref, gi_ref, car_ref, cai_ref, abr_ref, abi_ref, acr_ref, aci_ref, ad_ref, alr_ref, ali_ref, sem):
        @pl.when(pl.program_id(0) == 0)
        def _():
            for ref in (car_ref, cai_ref, abr_ref, abi_ref, acr_ref, aci_ref, ad_ref, alr_ref, ali_ref):
                ref[...] = jnp.zeros_like(ref)

        u = u_ref[...]
        dy = dg_ref[...] * _dgelu(yp_ref[...])
        gr_ref[...] = _dot(dy, cr_ref[...])
        gi_ref[...] = -_dot(dy, ci_ref[...])
        rowid = lax.broadcasted_iota(jnp.int32, (SUBLANES, S5_LANE_CHUNK), 0)
        for lc in range(S5_LANES // S5_LANE_CHUNK):
            ls = pl.ds(lc * S5_LANE_CHUNK, S5_LANE_CHUNK)
            pr, pi = pr_ref[:, ls], pi_ref[:, ls]
            fwd_rows_r = jnp.concatenate([pr[7:8], pr[6:7], pr[6:7], pr[4:5]], axis=0)
            fwd_rows_i = jnp.concatenate([pi[7:8], pi[6:7], pi[6:7], pi[4:5]], axis=0)

            def group(j, carry, ls=ls, pr=pr, pi=pi, fr=fwd_rows_r, fi=fwd_rows_i):
                cr, ci, slr, sli = carry
                g = ngrp - 1 - j
                r = pl.ds(pl.multiple_of(g * SUBLANES, SUBLANES), SUBLANES)
                xr, xi = _scan8(gr_ref[r, ls], gi_ref[r, ls], fr, fi, -1, rowid)
                xr, xi = xr + pr * cr - pi * ci, xi + pr * ci + pi * cr
                gr_ref[r, ls] = xr
                gi_ref[r, ls] = xi
                nr = jnp.where(rowid == SUBLANES - 1, cr, pltpu.roll(xr, SUBLANES - 1, 0))
                ni = jnp.where(rowid == SUBLANES - 1, ci, pltpu.roll(xi, SUBLANES - 1, 0))
                hr, hi = hr_ref[r, ls], hi_ref[r, ls]
                slr = slr + nr * hr + ni * hi
                sli = sli + ni * hr - nr * hi
                return xr[0:1, :], xi[0:1, :], slr, sli

            zero = jnp.zeros((SUBLANES, S5_LANE_CHUNK), F32)
            cr, ci, slr, sli = lax.fori_loop(0, ngrp, group, (car_ref[:, ls], cai_ref[:, ls], zero, zero))
            car_ref[:, ls] = cr
            cai_ref[:, ls] = ci
            alr_ref[:, ls] += jnp.sum(slr, axis=0, keepdims=True)
            ali_ref[:, ls] += jnp.sum(sli, axis=0, keepdims=True)
        gr, gi = gr_ref[...], gi_ref[...]
        du_ref[...] = _dot(gr, bbr_ref[...], _NT) + _dot(gi, bbi_ref[...], _NT) + d_ref[...] * dy
        abr_ref[...] += _dot(u, gr, _TN)
        abi_ref[...] += _dot(u, gi, _TN)
        acr_ref[...] += _dot(hr_ref[...], dy, _TN)
        aci_ref[...] -= _dot(hi_ref[...], dy, _TN)
        ad_ref[...] += jnp.sum(dy * u, axis=0, keepdims=True)

        @pl.when(pl.program_id(0) == nb - 1)
        def _():
            dd_ref[...] = ad_ref[...]
            dlam_ref[0:1, :] = alr_ref[...]
            dlam_ref[1:2, :] = ali_ref[...]
            copies = [pltpu.make_async_copy(s, d, sem.at[k]) for k, (s, d) in enumerate(
                ((abr_ref, dbbr_ref), (abi_ref, dbbi_ref), (acr_ref, dcr_ref), (aci_ref, dci_ref)))]
            for cp in copies:
                cp.start()
            for cp in copies:
                cp.wait()

    whole = pl.BlockSpec(memory_space=pltpu.VMEM)
    hbm = pl.BlockSpec(memory_space=pl.ANY)
    rev = lambda i: (nb - 1 - i, 0)
    return pl.pallas_call(
        body, name="s5_bwd", grid=(nb,),
        in_specs=[pl.BlockSpec((tb, S5_WIDTH), rev), pl.BlockSpec((tb, S5_WIDTH), rev),
                  pl.BlockSpec((tb, S5_WIDTH), lambda i: (nb - 1 - i, 4096 // S5_WIDTH)),
                  pl.BlockSpec((tb, S5_LANES), rev), pl.BlockSpec((tb, S5_LANES), rev)] + [whole] * 7 + [hbm],
        out_specs=[pl.BlockSpec((tb, S5_WIDTH), lambda i: (nb - 1 - i, 4096 // S5_WIDTH)), hbm, hbm, hbm, hbm,
                   pl.BlockSpec((1, S5_WIDTH), lambda i: (0, 0)), pl.BlockSpec((2, S5_LANES), lambda i: (0, 0))],
        out_shape=[jax.ShapeDtypeStruct((t_len, IN_COLS), F32),
                   jax.ShapeDtypeStruct((S5_WIDTH, S5_LANES), F32), jax.ShapeDtypeStruct((S5_WIDTH, S5_LANES), F32),
                   jax.ShapeDtypeStruct((S5_LANES, S5_WIDTH), F32), jax.ShapeDtypeStruct((S5_LANES, S5_WIDTH), F32),
                   jax.ShapeDtypeStruct((1, S5_WIDTH), F32), jax.ShapeDtypeStruct((2, S5_LANES), F32)],
        scratch_shapes=[pltpu.VMEM((tb, S5_LANES), F32), pltpu.VMEM((tb, S5_LANES), F32),
                        pltpu.VMEM((1, S5_LANES), F32), pltpu.VMEM((1, S5_LANES), F32),
                        pltpu.VMEM((S5_WIDTH, S5_LANES), F32), pltpu.VMEM((S5_WIDTH, S5_LANES), F32),
                        pltpu.VMEM((S5_LANES, S5_WIDTH), F32), pltpu.VMEM((S5_LANES, S5_WIDTH), F32),
                        pltpu.VMEM((1, S5_WIDTH), F32), pltpu.VMEM((1, S5_LANES), F32),
                        pltpu.VMEM((1, S5_LANES), F32), pltpu.SemaphoreType.DMA((4,))],
        input_output_aliases={12: 0},
        compiler_params=_params("arbitrary"))(dgelu, y_pre, proj, h_re, h_im, pwr_re, pwr_im, bbr_bd, bbi_bd, cr_bd,
                                              ci_bd, d_row, dproj)


S5_BLOCKS = 4
S5_BW = S5_WIDTH // S5_BLOCKS
S5_BL = S5_LANES // S5_BLOCKS
S5_LANE_BLOCKS = S5_LANES // 128
S5_SCAN_BLOCKS = 4


def _s5_powers(a_re, a_im, log_dt, b_re_t, b_im_t, seg):
    def body(ar_ref, ai_ref, ldt_ref, br_ref, bi_ref,
             rows_f, pfr_ref, pfi_ref, rows_r, prr_ref, pri_ref, bbr_ref, bbi_ref):
        ar, ai = ar_ref[...], ai_ref[...]
        dt = jnp.exp(ldt_ref[...])
        mag = jnp.exp(ar * dt)
        lr, li = mag * jnp.cos(ai * dt), mag * jnp.sin(ai * dt)
        den = ar * ar + ai * ai
        nr = lr - 1.0
        sr = (nr * ar + li * ai) / den
        si = (li * ar - nr * ai) / den
        wide = (SUBLANES, S5_LANES)
        cr, ci = lr, li
        for i in range(seg):
            pfr_ref[i] = jnp.broadcast_to(cr, wide)
            pfi_ref[i] = jnp.broadcast_to(ci, wide)
            prr_ref[seg - 1 - i] = jnp.broadcast_to(cr, wide)
            pri_ref[seg - 1 - i] = jnp.broadcast_to(-ci, wide)
            if i == seg - 1:
                for rows, sign in ((rows_f, 1.0), (rows_r, -1.0)):
                    rows[0:1, :] = lr
                    rows[1:2, :] = sign * li
                    rows[2:3, :] = cr
                    rows[3:4, :] = sign * ci
            cr, ci = cr * lr - ci * li, cr * li + ci * lr
        br, bi = br_ref[...], bi_ref[...]
        bbr_ref[...] = sr * br - si * bi
        bbi_ref[...] = sr * bi + si * br

    whole = pl.BlockSpec(memory_space=pltpu.VMEM)
    tables = [jax.ShapeDtypeStruct((4, S5_LANES), F32)] + [jax.ShapeDtypeStruct((seg, SUBLANES, S5_LANES), F32)] * 2
    bbar = [jax.ShapeDtypeStruct((S5_GROUP, S5_LANES), F32)] * 2
    res = pl.pallas_call(body, name="s5_prep", in_specs=[whole] * 5, out_specs=[whole] * 8,
                         out_shape=tables + tables + bbar)(a_re, a_im, log_dt, b_re_t, b_im_t)
    return res[0:3], res[3:6], res[6], res[7]


def _scan_tables(pw_re, pw_im, reverse):
    seg = pw_re.shape[0]
    if reverse:
        pw_re, pw_im = pw_re[::-1], -pw_im[::-1]
        one, full = seg - 1, 0
    else:
        one, full = 0, seg - 1
    rows = jnp.stack([pw_re[one], pw_im[one], pw_re[full], pw_im[full]])
    wide = lambda t: jnp.broadcast_to(t[:, None, :], (seg, SUBLANES, S5_LANES))
    return rows, wide(pw_re), wide(pw_im)


def _lanes(j):
    return pl.ds(j * 128, 128)


def _segment_scan(xr_ref, xi_ref, lam_ref, car_ref, cai_ref, cn_r, cn_i, blocks, seg, reverse):
    shape = (SUBLANES, 128)
    lrs = [jnp.broadcast_to(lam_ref[0:1, _lanes(j)], shape) for j in blocks]
    lis = [jnp.broadcast_to(lam_ref[1:2, _lanes(j)], shape) for j in blocks]

    def step(k, carry):
        idx = pl.ds(seg - 1 - k if reverse else k, SUBLANES, stride=seg)
        out = []
        for n, j in enumerate(blocks):
            cr, ci = carry[2 * n], carry[2 * n + 1]
            nr = lrs[n] * cr - lis[n] * ci + xr_ref[j, idx, :]
            ni = lrs[n] * ci + lis[n] * cr + xi_ref[j, idx, :]
            xr_ref[j, idx, :] = nr
            xi_ref[j, idx, :] = ni
            out += [nr, ni]
        return tuple(out)

    zero = jnp.zeros(shape, F32)
    fin = lax.fori_loop(0, seg, step, (zero,) * (2 * len(blocks)), unroll=2)
    for n, j in enumerate(blocks):
        ls = _lanes(j)
        fr, fi = fin[2 * n], fin[2 * n + 1]
        sr, si = lam_ref[2:3, ls], lam_ref[3:4, ls]
        pr, pi = car_ref[:, ls], cai_ref[:, ls]
        for s in (reversed(range(SUBLANES)) if reverse else range(SUBLANES)):
            cn_r[s:s + 1, ls] = pr
            cn_i[s:s + 1, ls] = pi
            pr, pi = fr[s:s + 1, :] + sr * pr - si * pi, fi[s:s + 1, :] + sr * pi + si * pr
        car_ref[:, ls] = pr
        cai_ref[:, ls] = pi


def _s5_fwd2(proj, lam_rows, p3_re, p3_im, bbr4, bbi4, crt4, cit4, d_row, t_len, tb):
    seg = tb // SUBLANES

    def body(u_ref, lam_ref, p3r_ref, p3i_ref, bbr_ref, bbi_ref, crt_ref, cit_ref, d_ref,
             hr_ref, hi_ref, ypre_ref, ys_ref, car_ref, cai_ref, cn_r, cn_i):
        @pl.when(pl.program_id(0) == 0)
        def _():
            car_ref[...] = jnp.zeros_like(car_ref)
            cai_ref[...] = jnp.zeros_like(cai_ref)

        u = u_ref[...]
        for i in range(S5_BLOCKS):
            ui = u[:, i * S5_BW:(i + 1) * S5_BW]
            xr, xi = _dot(ui, bbr_ref[i]), _dot(ui, bbi_ref[i])
            for jj in range(S5_BL // 128):
                hr_ref[i * (S5_BL // 128) + jj] = xr[:, jj * 128:(jj + 1) * 128]
                hi_ref[i * (S5_BL // 128) + jj] = xi[:, jj * 128:(jj + 1) * 128]
        for lc in range(S5_LANE_BLOCKS // S5_SCAN_BLOCKS):
            blocks = range(lc * S5_SCAN_BLOCKS, (lc + 1) * S5_SCAN_BLOCKS)
            _segment_scan(hr_ref, hi_ref, lam_ref, car_ref, cai_ref, cn_r, cn_i, blocks, seg, False)
            crs = [cn_r[:, _lanes(j)] for j in blocks]
            cis = [cn_i[:, _lanes(j)] for j in blocks]

            def fix(t, carry, blocks=blocks, crs=crs, cis=cis):
                idx = pl.ds(t, SUBLANES, stride=seg)
                for n, j in enumerate(blocks):
                    pr, pi = p3r_ref[t, :, _lanes(j)], p3i_ref[t, :, _lanes(j)]
                    hr_ref[j, idx, :] += pr * crs[n] - pi * cis[n]
                    hi_ref[j, idx, :] += pr * cis[n] + pi * crs[n]
                return carry

            lax.fori_loop(0, seg, fix, 0, unroll=2)
        for i in range(S5_BLOCKS):
            ws = pl.ds(i * S5_BW, S5_BW)
            js = range(i * (S5_BL // 128), (i + 1) * (S5_BL // 128))
            hr = jnp.concatenate([hr_ref[j] for j in js], axis=1)
            hi = jnp.concatenate([hi_ref[j] for j in js], axis=1)
            y = _dot(hr, crt_ref[i]) - _dot(hi, cit_ref[i]) + d_ref[:, ws] * u[:, i * S5_BW:(i + 1) * S5_BW]
            ypre_ref[:, ws] = y
            ys_ref[:, ws] = jax.nn.gelu(y, approximate=True).astype(ys_ref.dtype)

    whole = pl.BlockSpec(memory_space=pltpu.VMEM)
    h_spec = pl.BlockSpec((S5_LANE_BLOCKS, tb, 128), lambda i: (0, i, 0))
    return pl.pallas_call(
        body, name="s5_fwd", grid=(t_len // tb,),
        in_specs=[pl.BlockSpec((tb, S5_WIDTH), lambda i: (i, 4096 // S5_WIDTH))] + [whole] * 8,
        out_specs=[h_spec, h_spec,
                   pl.BlockSpec((tb, S5_WIDTH), lambda i: (i, 0)), pl.BlockSpec((tb, S5_WIDTH), lambda i: (i, 0))],
        out_shape=[jax.ShapeDtypeStruct((S5_LANE_BLOCKS, t_len, 128), F32),
                   jax.ShapeDtypeStruct((S5_LANE_BLOCKS, t_len, 128), F32),
                   jax.ShapeDtypeStruct((t_len, S5_WIDTH), F32), jax.ShapeDtypeStruct((t_len, S5_WIDTH), MXU_DTYPE)],
        scratch_shapes=[pltpu.VMEM((1, S5_LANES), F32), pltpu.VMEM((1, S5_LANES), F32),
                        pltpu.VMEM((SUBLANES, S5_LANES), F32), pltpu.VMEM((SUBLANES, S5_LANES), F32)],
        compiler_params=_params("arbitrary"))(proj, lam_rows, p3_re, p3_im, bbr4, bbi4, crt4, cit4, d_row)


def _s5_bwd2(dgelu, y_pre, proj, h_re, h_im, lam_rows, p3_re, p3_im, bbr4, bbi4, cr4, ci4, d_row, dproj, t_len, tb):
    seg = tb // SUBLANES
    nb = t_len // tb

    def body(dg_ref, yp_ref, u_ref, hr_ref, hi_ref, lam_ref, p3r_ref, p3i_ref, bbr_ref, bbi_ref, cr_ref, ci_ref,
             d_ref, _, du_ref, dbbr_ref, dbbi_ref, dcr_ref, dci_ref, dd_ref, dlam_ref,
             gr_ref, gi_ref, car_ref, cai_ref, cn_r, cn_i):
        @pl.when(pl.program_id(0) == 0)
        def _():
            for ref in (car_ref, cai_ref, dbbr_ref, dbbi_ref, dcr_ref, dci_ref, dd_ref, dlam_ref):
                ref[...] = jnp.zeros_like(ref)

        u = u_ref[...]
        dy = dg_ref[...] * _dgelu(yp_ref[...])
        nlb = S5_BL // 128
        for i in range(S5_BLOCKS):
            dyi = dy[:, i * S5_BW:(i + 1) * S5_BW]
            xr, xi = _dot(dyi, cr_ref[i]), -_dot(dyi, ci_ref[i])
            for jj in range(nlb):
                gr_ref[i * nlb + jj] = xr[:, jj * 128:(jj + 1) * 128]
                gi_ref[i * nlb + jj] = xi[:, jj * 128:(jj + 1) * 128]
        for lc in range(S5_LANE_BLOCKS // S5_SCAN_BLOCKS):
            blocks = range(lc * S5_SCAN_BLOCKS, (lc + 1) * S5_SCAN_BLOCKS)
            _segment_scan(gr_ref, gi_ref, lam_ref, car_ref, cai_ref, cn_r, cn_i, blocks, seg, True)
            crs = [cn_r[:, _lanes(j)] for j in blocks]
            cis = [cn_i[:, _lanes(j)] for j in blocks]

            def fix(k, carry, blocks=blocks, crs=crs, cis=cis):
                t = seg - 1 - k
                idx = pl.ds(t, SUBLANES, stride=seg)
                out = []
                for n, j in enumerate(blocks):
                    nr, ni, slr, sli = carry[4 * n:4 * n + 4]
                    pr, pi = p3r_ref[t, :, _lanes(j)], p3i_ref[t, :, _lanes(j)]
                    g_r = gr_ref[j, idx, :] + pr * crs[n] - pi * cis[n]
                    g_i = gi_ref[j, idx, :] + pr * cis[n] + pi * crs[n]
                    gr_ref[j, idx, :] = g_r
                    gi_ref[j, idx, :] = g_i
                    hr, hi = hr_ref[j, idx, :], hi_ref[j, idx, :]
                    out += [g_r, g_i, slr + nr * hr + ni * hi, sli + ni * hr - nr * hi]
                return tuple(out)

            zero = jnp.zeros((SUBLANES, 128), F32)
            init = []
            for n in range(len(blocks)):
                init += [crs[n], cis[n], zero, zero]
            fin = lax.fori_loop(0, seg, fix, tuple(init), unroll=2)
            for n, j in enumerate(blocks):
                dlam_ref[0:1, _lanes(j)] += jnp.sum(fin[4 * n + 2], axis=0, keepdims=True)
                dlam_ref[1:2, _lanes(j)] += jnp.sum(fin[4 * n + 3], axis=0, keepdims=True)
        for i in range(S5_BLOCKS):
            ws = pl.ds(i * S5_BW, S5_BW)
            js = range(i * nlb, (i + 1) * nlb)
            ui, dyi = u[:, i * S5_BW:(i + 1) * S5_BW], dy[:, i * S5_BW:(i + 1) * S5_BW]
            gr = jnp.concatenate([gr_ref[j] for j in js], axis=1)
            gi = jnp.concatenate([gi_ref[j] for j in js], axis=1)
            du_ref[:, ws] = _dot(gr, bbr_ref[i], _NT) + _dot(gi, bbi_ref[i], _NT) + d_ref[:, ws] * dyi
            dbbr_ref[i] += _dot(ui, gr, _TN)
            dbbi_ref[i] += _dot(ui, gi, _TN)
            dcr_ref[i] += _dot(jnp.concatenate([hr_ref[j] for j in js], axis=1), dyi, _TN)
            dci_ref[i] -= _dot(jnp.concatenate([hi_ref[j] for j in js], axis=1), dyi, _TN)
        dd_ref[...] += jnp.sum(dy * u, axis=0, keepdims=True)

    whole = pl.BlockSpec(memory_space=pltpu.VMEM)
    rev = lambda i: (nb - 1 - i, 0)
    const3 = lambda i: (0, 0, 0)
    h_spec = pl.BlockSpec((S5_LANE_BLOCKS, tb, 128), lambda i: (0, nb - 1 - i, 0))
    return pl.pallas_call(
        body, name="s5_bwd", grid=(nb,),
        in_specs=[pl.BlockSpec((tb, S5_WIDTH), rev), pl.BlockSpec((tb, S5_WIDTH), rev),
                  pl.BlockSpec((tb, S5_WIDTH), lambda i: (nb - 1 - i, 4096 // S5_WIDTH)),
                  h_spec, h_spec] + [whole] * 8
                 + [pl.BlockSpec(memory_space=pl.ANY)],
        out_specs=[pl.BlockSpec((tb, S5_WIDTH), lambda i: (nb - 1 - i, 4096 // S5_WIDTH)),
                   pl.BlockSpec((S5_BLOCKS, S5_BW, S5_BL), const3), pl.BlockSpec((S5_BLOCKS, S5_BW, S5_BL), const3),
                   pl.BlockSpec((S5_BLOCKS, S5_BL, S5_BW), const3), pl.BlockSpec((S5_BLOCKS, S5_BL, S5_BW), const3),
                   pl.BlockSpec((1, S5_WIDTH), lambda i: (0, 0)), pl.BlockSpec((2, S5_LANES), lambda i: (0, 0))],
        out_shape=[jax.ShapeDtypeStruct((t_len, IN_COLS), F32),
                   jax.ShapeDtypeStruct((S5_BLOCKS, S5_BW, S5_BL), F32),
                   jax.ShapeDtypeStruct((S5_BLOCKS, S5_BW, S5_BL), F32),
                   jax.ShapeDtypeStruct((S5_BLOCKS, S5_BL, S5_BW), F32),
                   jax.ShapeDtypeStruct((S5_BLOCKS, S5_BL, S5_BW), F32),
                   jax.ShapeDtypeStruct((1, S5_WIDTH), F32), jax.ShapeDtypeStruct((2, S5_LANES), F32)],
        scratch_shapes=[pltpu.VMEM((S5_LANE_BLOCKS, tb, 128), F32), pltpu.VMEM((S5_LANE_BLOCKS, tb, 128), F32),
                        pltpu.VMEM((1, S5_LANES), F32), pltpu.VMEM((1, S5_LANES), F32),
                        pltpu.VMEM((SUBLANES, S5_LANES), F32), pltpu.VMEM((SUBLANES, S5_LANES), F32)],
        input_output_aliases={13: 0},
        compiler_params=_params("arbitrary"))(dgelu, y_pre, proj, h_re, h_im, lam_rows, p3_re, p3_im, bbr4, bbi4,
                                              cr4, ci4, d_row, dproj)


def _to_segment_order(v, stage_ref, out_ref, seg):
    nbl = v.shape[1] // 128
    for b in range(nbl):
        stage_ref[b] = v[:, b * 128:(b + 1) * 128]

    def body(t, carry):
        rows = pl.ds(pl.multiple_of(t * SUBLANES, SUBLANES), SUBLANES)
        for b in range(nbl):
            out_ref[rows, _lanes(b)] = stage_ref[b, pl.ds(t, SUBLANES, stride=seg), :]
        return carry

    lax.fori_loop(0, seg, body, 0)


def _from_segment_order(v, stage_ref, out_ref, seg):
    nbl = v.shape[1] // 128
    for b in range(nbl):
        stage_ref[b] = v[:, b * 128:(b + 1) * 128]
    for s in range(SUBLANES):
        def body(k, carry, s=s):
            rows = pl.ds(pl.multiple_of(s * seg + k * SUBLANES, SUBLANES), SUBLANES)
            for b in range(nbl):
                out_ref[rows, _lanes(b)] = stage_ref[b, pl.ds(k * SUBLANES * SUBLANES + s, SUBLANES,
                                                              stride=SUBLANES), :]
            return carry

        lax.fori_loop(0, seg // SUBLANES, body, 0)


def _tile_scan(xr_ref, xi_ref, lam_ref, car_ref, cai_ref, cn_r, cn_i, blocks, seg, reverse):
    shape = (SUBLANES, 128)
    lrs = [jnp.broadcast_to(lam_ref[0:1, _lanes(j)], shape) for j in blocks]
    lis = [jnp.broadcast_to(lam_ref[1:2, _lanes(j)], shape) for j in blocks]

    def step(k, carry):
        t = seg - 1 - k if reverse else k
        rows = pl.ds(pl.multiple_of(t * SUBLANES, SUBLANES), SUBLANES)
        out = []
        for n, j in enumerate(blocks):
            cr, ci = carry[2 * n], carry[2 * n + 1]
            nr = lrs[n] * cr - lis[n] * ci + xr_ref[rows, _lanes(j)]
            ni = lrs[n] * ci + lis[n] * cr + xi_ref[rows, _lanes(j)]
            xr_ref[rows, _lanes(j)] = nr
            xi_ref[rows, _lanes(j)] = ni
            out += [nr, ni]
        return tuple(out)

    zero = jnp.zeros(shape, F32)
    fin = lax.fori_loop(0, seg, step, (zero,) * (2 * len(blocks)), unroll=2)
    for n, j in enumerate(blocks):
        ls = _lanes(j)
        fr, fi = fin[2 * n], fin[2 * n + 1]
        sr, si = lam_ref[2:3, ls], lam_ref[3:4, ls]
        pr, pi = car_ref[:, ls], cai_ref[:, ls]
        for s in (reversed(range(SUBLANES)) if reverse else range(SUBLANES)):
            cn_r[s:s + 1, ls] = pr
            cn_i[s:s + 1, ls] = pi
            pr, pi = fr[s:s + 1, :] + sr * pr - si * pi, fi[s:s + 1, :] + sr * pi + si * pr
        car_ref[:, ls] = pr
        cai_ref[:, ls] = pi


def _s5_fwd3(proj, lam_rows, p3_re, p3_im, bbr4, bbi4, crt4, cit4, d_row, t_len, tb):
    seg = tb // SUBLANES

    def body(u_ref, lam_ref, p3r_ref, p3i_ref, bbr_ref, bbi_ref, crt_ref, cit_ref, d_ref,
             hr_ref, hi_ref, ypre_ref, ys_ref, car_ref, cai_ref, cn_r, cn_i, stage_ref, us_ref, yseg_ref):
        @pl.when(pl.program_id(0) == 0)
        def _():
            car_ref[...] = jnp.zeros_like(car_ref)
            cai_ref[...] = jnp.zeros_like(cai_ref)

        _to_segment_order(u_ref[...], stage_ref, us_ref, seg)
        u = us_ref[...]
        for i in range(S5_BLOCKS):
            ui = u[:, i * S5_BW:(i + 1) * S5_BW]
            hr_ref[:, pl.ds(i * S5_BL, S5_BL)] = _dot(ui, bbr_ref[i])
            hi_ref[:, pl.ds(i * S5_BL, S5_BL)] = _dot(ui, bbi_ref[i])
        for lc in range(S5_LANE_BLOCKS // S5_SCAN_BLOCKS):
            blocks = range(lc * S5_SCAN_BLOCKS, (lc + 1) * S5_SCAN_BLOCKS)
            _tile_scan(hr_ref, hi_ref, lam_ref, car_ref, cai_ref, cn_r, cn_i, blocks, seg, False)
            crs = [cn_r[:, _lanes(j)] for j in blocks]
            cis = [cn_i[:, _lanes(j)] for j in blocks]

            def fix(t, carry, blocks=blocks, crs=crs, cis=cis):
                rows = pl.ds(pl.multiple_of(t * SUBLANES, SUBLANES), SUBLANES)
                for n, j in enumerate(blocks):
                    pr, pi = p3r_ref[t, :, _lanes(j)], p3i_ref[t, :, _lanes(j)]
                    hr_ref[rows, _lanes(j)] += pr * crs[n] - pi * cis[n]
                    hi_ref[rows, _lanes(j)] += pr * cis[n] + pi * crs[n]
                return carry

            lax.fori_loop(0, seg, fix, 0, unroll=2)
        for i in range(S5_BLOCKS):
            ws = pl.ds(i * S5_BW, S5_BW)
            bl = pl.ds(i * S5_BL, S5_BL)
            yseg_ref[:, ws] = (_dot(hr_ref[:, bl], crt_ref[i]) - _dot(hi_ref[:, bl], cit_ref[i])
                               + d_ref[:, ws] * u[:, i * S5_BW:(i + 1) * S5_BW])
        _from_segment_order(yseg_ref[...], stage_ref, ypre_ref, seg)
        ys_ref[...] = jax.nn.gelu(ypre_ref[...], approximate=True).astype(ys_ref.dtype)

    whole = pl.BlockSpec(memory_space=pltpu.VMEM)
    return pl.pallas_call(
        body, name="s5_fwd", grid=(t_len // tb,),
        in_specs=[pl.BlockSpec((tb, S5_WIDTH), lambda i: (i, 4096 // S5_WIDTH))] + [whole] * 8,
        out_specs=[pl.BlockSpec((tb, S5_LANES), lambda i: (i, 0)), pl.BlockSpec((tb, S5_LANES), lambda i: (i, 0)),
                   pl.BlockSpec((tb, S5_WIDTH), lambda i: (i, 0)), pl.BlockSpec((tb, S5_WIDTH), lambda i: (i, 0))],
        out_shape=[jax.ShapeDtypeStruct((t_len, S5_LANES), F32), jax.ShapeDtypeStruct((t_len, S5_LANES), F32),
                   jax.ShapeDtypeStruct((t_len, S5_WIDTH), F32), jax.ShapeDtypeStruct((t_len, S5_WIDTH), MXU_DTYPE)],
        scratch_shapes=[pltpu.VMEM((1, S5_LANES), F32), pltpu.VMEM((1, S5_LANES), F32),
                        pltpu.VMEM((SUBLANES, S5_LANES), F32), pltpu.VMEM((SUBLANES, S5_LANES), F32),
                        pltpu.VMEM((S5_WIDTH // 128, tb, 128), F32), pltpu.VMEM((tb, S5_WIDTH), F32),
                        pltpu.VMEM((tb, S5_WIDTH), F32)],
        compiler_params=_params("arbitrary"))(proj, lam_rows, p3_re, p3_im, bbr4, bbi4, crt4, cit4, d_row)


def _s5_bwd3(dgelu, y_pre, proj, h_re, h_im, lam_rows, p3_re, p3_im, bbr4, bbi4, cr4, ci4, d_row, dproj, t_len, tb):
    seg = tb // SUBLANES
    nb = t_len // tb

    def body(dg_ref, yp_ref, u_ref, hr_ref, hi_ref, lam_ref, p3r_ref, p3i_ref, bbr_ref, bbi_ref, cr_ref, ci_ref,
             d_ref, _, du_ref, dbbr_ref, dbbi_ref, dcr_ref, dci_ref, dd_ref, dlam_ref,
             gr_ref, gi_ref, car_ref, cai_ref, cn_r, cn_i, stage_ref, us_ref, dys_ref, duseg_ref):
        @pl.when(pl.program_id(0) == 0)
        def _():
            for ref in (car_ref, cai_ref, dbbr_ref, dbbi_ref, dcr_ref, dci_ref, dd_ref, dlam_ref):
                ref[...] = jnp.zeros_like(ref)

        _to_segment_order(u_ref[...], stage_ref, us_ref, seg)
        _to_segment_order(dg_ref[...] * _dgelu(yp_ref[...]), stage_ref, dys_ref, seg)
        u, dy = us_ref[...], dys_ref[...]
        for i in range(S5_BLOCKS):
            dyi = dy[:, i * S5_BW:(i + 1) * S5_BW]
            gr_ref[:, pl.ds(i * S5_BL, S5_BL)] = _dot(dyi, cr_ref[i])
            gi_ref[:, pl.ds(i * S5_BL, S5_BL)] = -_dot(dyi, ci_ref[i])
        for lc in range(S5_LANE_BLOCKS // S5_SCAN_BLOCKS):
            blocks = range(lc * S5_SCAN_BLOCKS, (lc + 1) * S5_SCAN_BLOCKS)
            _tile_scan(gr_ref, gi_ref, lam_ref, car_ref, cai_ref, cn_r, cn_i, blocks, seg, True)
            crs = [cn_r[:, _lanes(j)] for j in blocks]
            cis = [cn_i[:, _lanes(j)] for j in blocks]

            def fix(k, carry, blocks=blocks, crs=crs, cis=cis):
                t = seg - 1 - k
                rows = pl.ds(pl.multiple_of(t * SUBLANES, SUBLANES), SUBLANES)
                out = []
                for n, j in enumerate(blocks):
                    nr, ni, slr, sli = carry[4 * n:4 * n + 4]
                    pr, pi = p3r_ref[t, :, _lanes(j)], p3i_ref[t, :, _lanes(j)]
                    g_r = gr_ref[rows, _lanes(j)] + pr * crs[n] - pi * cis[n]
                    g_i = gi_ref[rows, _lanes(j)] + pr * cis[n] + pi * crs[n]
                    gr_ref[rows, _lanes(j)] = g_r
                    gi_ref[rows, _lanes(j)] = g_i
                    hr, hi = hr_ref[rows, _lanes(j)], hi_ref[rows, _lanes(j)]
                    out += [g_r, g_i, slr + nr * hr + ni * hi, sli + ni * hr - nr * hi]
                return tuple(out)

            zero = jnp.zeros((SUBLANES, 128), F32)
            init = []
            for n in range(len(blocks)):
                init += [crs[n], cis[n], zero, zero]
            fin = lax.fori_loop(0, seg, fix, tuple(init), unroll=2)
            for n, j in enumerate(blocks):
                dlam_ref[0:1, _lanes(j)] += jnp.sum(fin[4 * n + 2], axis=0, keepdims=True)
                dlam_ref[1:2, _lanes(j)] += jnp.sum(fin[4 * n + 3], axis=0, keepdims=True)
        for i in range(S5_BLOCKS):
            ws = pl.ds(i * S5_BW, S5_BW)
            bl = pl.ds(i * S5_BL, S5_BL)
            ui, dyi = u[:, i * S5_BW:(i + 1) * S5_BW], dy[:, i * S5_BW:(i + 1) * S5_BW]
            gr, gi = gr_ref[:, bl], gi_ref[:, bl]
            duseg_ref[:, ws] = _dot(gr, bbr_ref[i], _NT) + _dot(gi, bbi_ref[i], _NT) + d_ref[:, ws] * dyi
            dbbr_ref[i] += _dot(ui, gr, _TN)
            dbbi_ref[i] += _dot(ui, gi, _TN)
            dcr_ref[i] += _dot(hr_ref[:, bl], dyi, _TN)
            dci_ref[i] -= _dot(hi_ref[:, bl], dyi, _TN)
        dd_ref[...] += jnp.sum(dy * u, axis=0, keepdims=True)
        _from_segment_order(duseg_ref[...], stage_ref, duseg_ref, seg)
        du_ref[...] = duseg_ref[...].astype(du_ref.dtype)

    whole = pl.BlockSpec(memory_space=pltpu.VMEM)
    rev = lambda i: (nb - 1 - i, 0)
    const3 = lambda i: (0, 0, 0)
    return pl.pallas_call(
        body, name="s5_bwd", grid=(nb,),
        in_specs=[pl.BlockSpec((tb, S5_WIDTH), rev), pl.BlockSpec((tb, S5_WIDTH), rev),
                  pl.BlockSpec((tb, S5_WIDTH), lambda i: (nb - 1 - i, 4096 // S5_WIDTH)),
                  pl.BlockSpec((tb, S5_LANES), rev), pl.BlockSpec((tb, S5_LANES), rev)] + [whole] * 8
                 + [pl.BlockSpec(memory_space=pl.ANY)],
        out_specs=[pl.BlockSpec((tb, S5_WIDTH), lambda i: (nb - 1 - i, 4096 // S5_WIDTH)),
                   pl.BlockSpec((S5_BLOCKS, S5_BW, S5_BL), const3), pl.BlockSpec((S5_BLOCKS, S5_BW, S5_BL), const3),
                   pl.BlockSpec((S5_BLOCKS, S5_BL, S5_BW), const3), pl.BlockSpec((S5_BLOCKS, S5_BL, S5_BW), const3),
                   pl.BlockSpec((1, S5_WIDTH), lambda i: (0, 0)), pl.BlockSpec((2, S5_LANES), lambda i: (0, 0))],
        out_shape=[jax.ShapeDtypeStruct((t_len, IN_COLS), dproj.dtype),
                   jax.ShapeDtypeStruct((S5_BLOCKS, S5_BW, S5_BL), F32),
                   jax.ShapeDtypeStruct((S5_BLOCKS, S5_BW, S5_BL), F32),
                   jax.ShapeDtypeStruct((S5_BLOCKS, S5_BL, S5_BW), F32),
                   jax.ShapeDtypeStruct((S5_BLOCKS, S5_BL, S5_BW), F32),
                   jax.ShapeDtypeStruct((1, S5_WIDTH), F32), jax.ShapeDtypeStruct((2, S5_LANES), F32)],
        scratch_shapes=[pltpu.VMEM((tb, S5_LANES), F32), pltpu.VMEM((tb, S5_LANES), F32),
                        pltpu.VMEM((1, S5_LANES), F32), pltpu.VMEM((1, S5_LANES), F32),
                        pltpu.VMEM((SUBLANES, S5_LANES), F32), pltpu.VMEM((SUBLANES, S5_LANES), F32),
                        pltpu.VMEM((S5_WIDTH // 128, tb, 128), F32), pltpu.VMEM((tb, S5_WIDTH), F32),
                        pltpu.VMEM((tb, S5_WIDTH), F32), pltpu.VMEM((tb, S5_WIDTH), F32)],
        input_output_aliases={13: 0},
        compiler_params=_params("arbitrary"))(dgelu, y_pre, proj, h_re, h_im, lam_rows, p3_re, p3_im, bbr4, bbi4,
                                              cr4, ci4, d_row, dproj)


def _block_diag4(per_group):
    g8 = S5_GROUPS // S5_BLOCKS
    eye = jnp.eye(g8, dtype=bool)[None, :, None, :, None]
    dense = jnp.where(eye, per_group.reshape(S5_BLOCKS, g8, S5_GROUP, 1, S5_STATE), 0.0)
    return dense.reshape(S5_BLOCKS, S5_BW, S5_BL)


def _diag_blocks4(dense):
    g8 = S5_GROUPS // S5_BLOCKS
    ar = jnp.arange(g8)
    d5 = dense.reshape(S5_BLOCKS, g8, S5_GROUP, g8, S5_STATE)
    return d5[:, ar, :, ar, :].transpose(1, 0, 2, 3).reshape(S5_GROUPS, S5_GROUP, S5_STATE)


def _block_diag(per_group):
    eye = jnp.eye(S5_GROUPS, dtype=bool)[:, None, :, None]
    dense = jnp.where(eye, per_group[:, :, None, :], 0.0)
    return dense.reshape(S5_WIDTH, S5_LANES)


def _diag_blocks(dense):
    ar = jnp.arange(S5_GROUPS)
    return dense.reshape(S5_GROUPS, S5_GROUP, S5_GROUPS, S5_STATE)[ar, :, ar, :]


def _hg_gate_bwd(da, o, g, gn):
    dos, dgs, dgns = [], [], []
    for h in range(HG_HEADS):
        sl = slice(h * HG_DIM, (h + 1) * HG_DIM)
        oh, gh, dah, gnh = o[:, sl], g[:, sl], da[:, sl], gn[:, sl]
        rr = lax.rsqrt(jnp.mean(oh * oh, axis=-1, keepdims=True) + NORM_EPS)
        sg = _sig(gh)
        dgs.append(dah * (oh * rr * gnh) * _dsilu(gh, sg))
        don = dah * (gh * sg)
        t = don * gnh
        dos.append(rr * t - oh * (rr * rr * rr) * jnp.mean(t * oh, axis=-1, keepdims=True))
        dgns.append(jnp.sum(don * oh * rr, axis=0, keepdims=True))
    return jnp.concatenate(dos, axis=1), jnp.concatenate(dgs, axis=1), jnp.concatenate(dgns, axis=1)


MIX_BWD_COLS = ((3072, 1024), (4608, 512), (5120, 1024), (6144, 1024))


def _mix_bwd(dgl, h1, dh2, y_hg, y_s5, proj, glu, o_hg, g2, ghn, w, t_len, tm):
    nb = t_len // tm

    def body(dgl_ref, h1_ref, dh2_ref, yh_ref, ys_ref, ghg_ref, z_ref, gh_ref, gs_ref, glu_ref, o_ref, g2_ref, gn_ref,
             wg_ref, wo_ref, ws5_ref, whg_ref, wglu_ref,
             dh1_ref, dyh_ref, dys_ref, dglu_ref, dgelu_ref, do_ref, dg2_ref, dbglu_ref, dgn_ref, dproj_ref,
             st0, st1, st2, st3, sems):
        i = pl.program_id(0)
        stages = (st0, st1, st2, st3)

        def writes(step):
            rows = pl.ds(pl.multiple_of(step * tm, tm), tm)
            return [pltpu.make_async_copy(st, dproj_ref.at[rows, pl.ds(c0, wd)], sems.at[k])
                    for k, (st, (c0, wd)) in enumerate(zip(stages, MIX_BWD_COLS))]

        @pl.when(i > 0)
        def _():
            for cp in writes(i - 1):
                cp.wait()

        @pl.when(i == 0)
        def _():
            for ref in (dg2_ref, dbglu_ref, dgn_ref):
                ref[...] = jnp.zeros_like(ref)

        dx, dg2 = _rms_bwd(_dot(dgl_ref[...], wg_ref[...], _NT), h1_ref[...], g2_ref[...])
        dh1 = dh2_ref[...] + dx
        dh1_ref[...] = dh1
        dg2_ref[...] += dg2
        dm = _dot(dh1, wo_ref[...], _NT)
        sh, ss = _sig(gh_ref[...]), _sig(gs_ref[...])
        dyh, dys = _mx(dm * sh), _mx(dm * ss)
        dyh_ref[...] = dyh
        dys_ref[...] = dys
        st2[...] = (dm * yh_ref[...] * sh * (1.0 - sh)).astype(st2.dtype)
        st3[...] = (dm * ys_ref[...] * ss * (1.0 - ss)).astype(st3.dtype)
        dys2 = _dot(dys, ws5_ref[...], _NT)
        gl_, z = glu_ref[...], z_ref[...]
        a, b = gl_[:, :S5_WIDTH], gl_[:, S5_WIDTH:]
        sb, sz = _sig(b), _sig(z)
        silu = z * sz
        dglu = jnp.concatenate([dys2 * sb * silu, dys2 * a * silu * sb * (1.0 - sb)], axis=1)
        st1[...] = (dys2 * a * sb * _dsilu(z, sz)).astype(st1.dtype)
        dbglu_ref[...] += jnp.sum(dglu, axis=0, keepdims=True)
        dglu_ref[...] = _mx(dglu)
        dgelu_ref[...] = _dot(dglu, wglu_ref[...], _NT)
        d_o, dg, dgn = _hg_gate_bwd(_dot(dyh, whg_ref[...], _NT), o_ref[...], ghg_ref[...], gn_ref[...])
        do_ref[...] = d_o.astype(do_ref.dtype)
        st0[...] = dg.astype(st0.dtype)
        dgn_ref[...] += dgn
        for cp in writes(i):
            cp.start()

        @pl.when(i == nb - 1)
        def _():
            for cp in writes(i):
                cp.wait()

    tile = lambda wd, cb=0: pl.BlockSpec((tm, wd), functools.partial(lambda i, cb: (i, cb), cb=cb))
    row = lambda wd: pl.BlockSpec((1, wd), lambda i: (0, 0))
    whole = pl.BlockSpec(memory_space=pltpu.VMEM)
    return pl.pallas_call(
        body, name="mix_bwd", grid=(nb,),
        in_specs=[tile(1024), tile(1024), tile(1024), tile(1024), tile(1024), tile(1024, 3), tile(512, 4608 // 512),
                  tile(1024, 5), tile(1024, 6), tile(1024), tile(1024), row(1024), row(1024)] + [whole] * 5,
        out_specs=[tile(1024), tile(1024), tile(1024), tile(1024), tile(512), tile(1024), row(1024), row(1024),
                   row(1024), _HBM],
        out_shape=[jax.ShapeDtypeStruct((t_len, 1024), F32), jax.ShapeDtypeStruct((t_len, 1024), MXU_DTYPE),
                   jax.ShapeDtypeStruct((t_len, 1024), MXU_DTYPE), jax.ShapeDtypeStruct((t_len, 1024), MXU_DTYPE),
                   jax.ShapeDtypeStruct((t_len, 512), F32), jax.ShapeDtypeStruct((t_len, 1024), MXU_DTYPE),
                   jax.ShapeDtypeStruct((1, 1024), F32), jax.ShapeDtypeStruct((1, 1024), F32),
                   jax.ShapeDtypeStruct((1, 1024), F32), jax.ShapeDtypeStruct((t_len, IN_COLS), MXU_DTYPE)],
        scratch_shapes=[pltpu.VMEM((tm, wd), MXU_DTYPE) for _, wd in MIX_BWD_COLS] + [pltpu.SemaphoreType.DMA((4,))],
        compiler_params=_params("arbitrary"))(dgl, h1, dh2, y_hg, y_s5, proj, proj, proj, proj, glu, o_hg, g2, ghn,
                                              w["w_ple_gate"], w["w_out"], w["w_o_s5"], w["w_o_hg"], w["w_glu"])


def _local_step(x, p, target, w, sm, comm=None):
    t_len = x.shape[0]
    tm = min(256, t_len)
    tmm = min(512, t_len)
    tb_hg = min(256, t_len)
    tb_s5 = min(256, t_len)
    g1, g2, g3, ghn = sm["norm_g"], sm["ple_norm_g"], sm["final_norm_g"].reshape(1, D_MODEL), sm["hg_norm_g"]

    def rms_in(xv, g):
        return xv * lax.rsqrt(jnp.mean(xv * xv, axis=-1, keepdims=True) + NORM_EPS) * g

    in_shard = IN_COLS // N_CHIPS
    w_in = w["w_in"]
    if comm is None:
        proj, u = _mm_nn("mm_in", x, w_in, tmm, in_shard, prologue=rms_in, consts=[g1])
    else:
        proj, u, landed = _mm_nn("mm_in", x, w_in, tmm, in_shard, riding=comm.gather_rest(), prologue=rms_in,
                                 consts=[g1])
        w = comm.rest_weights(landed)
    o_hg, act_hg, s_prev = _hgrn2_fwd2(proj, sm["hg_lb"], ghn, t_len, tb_hg)

    lanes = lambda a: a.reshape(1, S5_LANES)
    a_re, a_im = lanes(sm["s5_a_re"]), lanes(sm["s5_a_im"])
    ldt = lanes(jnp.broadcast_to(sm["s5_log_dt"].reshape(S5_GROUPS, 1), (S5_GROUPS, S5_STATE)))
    to_t = lambda b: b.reshape(S5_GROUPS, S5_STATE, S5_GROUP).transpose(2, 0, 1).reshape(S5_GROUP, S5_LANES)
    b_re_t, b_im_t = to_t(sm["s5_b_re"]), to_t(sm["s5_b_im"])
    scan_fwd, scan_rev, bbr_t, bbi_t = _s5_powers(a_re, a_im, ldt, b_re_t, b_im_t, tb_s5 // SUBLANES)
    from_t = lambda b: b.reshape(S5_GROUP, S5_GROUPS, S5_STATE).transpose(1, 0, 2)
    bbr_bd = _block_diag4(from_t(bbr_t)).astype(MXU_DTYPE)
    bbi_bd = _block_diag4(from_t(bbi_t)).astype(MXU_DTYPE)
    cr_bd = _block_diag4(sm["s5_c_re"].reshape(S5_GROUPS, S5_GROUP, S5_STATE)).astype(MXU_DTYPE)
    ci_bd = _block_diag4(sm["s5_c_im"].reshape(S5_GROUPS, S5_GROUP, S5_STATE)).astype(MXU_DTYPE)
    d_row = sm["s5_d"].reshape(1, S5_WIDTH)
    h_re, h_im, y_pre, ys_gelu = _s5_fwd3(proj, *scan_fwd, bbr_bd, bbi_bd,
                                          cr_bd.transpose(0, 2, 1), ci_bd.transpose(0, 2, 1), d_row, t_len, tb_s5)
    def mix_f(act, ysg, z, gh, gs, xv, w_glu, b_glu, w_o_hg, w_o_s5, w_out):
        gl_ = _dot(ysg, w_glu) + b_glu
        a, b = gl_[:, :S5_WIDTH], gl_[:, S5_WIDTH:]
        ys2_ = (a * _sig(b) * (z * _sig(z))).astype(MXU_DTYPE)
        yh, ys = _dot(act, w_o_hg), _dot(ys2_, w_o_s5)
        mg = (_sig(gh) * yh + _sig(gs) * ys).astype(MXU_DTYPE)
        return (gl_, ys2_, yh, ys, mg, xv + _dot(mg, w_out))

    glu, ys2, y_hg, y_s5, merged, h1 = _rowwise(
        "mix_out", mix_f, t_len, tm,
        [(act_hg, 1024, 0), (ys_gelu, 512, 0), (proj, 512, 4608 // 512), (proj, 1024, 5), (proj, 1024, 6),
         (x, 1024, 0)], [w["w_glu"], sm["b_glu"], w["w_o_hg"], w["w_o_s5"], w["w_out"]],
        [(1024, F32), (512, MXU_DTYPE), (1024, F32), (1024, F32), (1024, MXU_DTYPE), (1024, F32)])

    def head_f(h1v, pv, tgt, g_ple, g, w_ple, w_gate):
        r2 = lax.rsqrt(jnp.mean(h1v * h1v, axis=-1, keepdims=True) + NORM_EPS)
        n2_ = (h1v * r2 * g_ple).astype(MXU_DTYPE)
        glv, pev = _dot(n2_, w_gate), _dot(pv, w_ple)
        gate = _sig(glv)
        h2 = h1v + pev * gate
        r = lax.rsqrt(jnp.mean(h2 * h2, axis=-1, keepdims=True) + NORM_EPS)
        e = h2 * r * g - tgt
        loss = 0.5 * jnp.sum(jnp.mean(e * e, axis=-1, keepdims=True), axis=0, keepdims=True)
        dy = e * (1.0 / D_MODEL)
        dg = jnp.sum(dy * h2 * r, axis=0, keepdims=True)
        t = dy * g
        dh2 = r * t - h2 * (r * r * r) * jnp.mean(t * h2, axis=-1, keepdims=True)
        return (n2_, dh2, dh2 * gate, dh2 * pev * gate * (1.0 - gate), jnp.broadcast_to(loss, (1, 128)), dg)

    n2, dh2, dpe, dgl, loss_row, d_g3 = _rowwise(
        "ple_loss_head", head_f, t_len, tm, [(h1, 1024, 0), (p, 256, 0), (target, 1024, 0)],
        [g2, g3, w["w_ple"], w["w_ple_gate"]],
        [(1024, MXU_DTYPE), (1024, F32), (1024, MXU_DTYPE), (1024, MXU_DTYPE)], accs=[(1, 128), (1, 1024)])

    gb = {}
    gb["w_ple"] = _mm_tn("mm_d_w_ple", p, dpe, tmm, 1024)
    gb["w_ple_gate"] = _mm_tn("mm_d_w_ple_gate", n2, dgl, tmm, 1024)
    dh1, dy_hg, dy_s5, dglu, dgelu, d_o, d_g2, d_bglu, d_ghn, dproj = _mix_bwd(
        dgl, h1, dh2, y_hg, y_s5, proj, glu, o_hg, g2, ghn, w, t_len, tm)
    gb["w_out"] = _mm_tn("mm_d_w_out", merged, dh1, tmm, 1024)
    gb["w_o_s5"] = _mm_tn("mm_d_w_o_s5", ys2, dy_s5, tmm, 1024)
    gb["w_glu"] = _mm_tn("mm_d_w_glu", ys_gelu, dglu, tmm, 1024)
    dproj, d_bbr, d_bbi, d_crt, d_cit, d_d, d_lam = _s5_bwd3(dgelu, y_pre, proj, h_re, h_im,
                                                            *scan_rev, bbr_bd, bbi_bd, cr_bd,
                                                            ci_bd, d_row, dproj, t_len, tb_s5)
    to_t3 = lambda b: b.transpose(1, 0, 2).reshape(S5_GROUP, S5_LANES)
    d_are, d_aim, d_ldt, d_br_t, d_bi_t = _s5_prep_bwd(a_re, a_im, ldt, b_re_t, b_im_t, d_lam,
                                                       to_t3(_diag_blocks4(d_bbr)), to_t3(_diag_blocks4(d_bbi)))
    gb["w_o_hg"] = _mm_tn("mm_d_w_o_hg", act_hg, dy_hg, tmm, 1024)
    dproj, d_lb = _hgrn2_bwd2(proj, d_o, s_prev, sm["hg_lb"], dproj, t_len, tb_hg)

    def in_b(duv, xv, dh, g):
        dx, dg = _rms_bwd(duv, xv, g)
        return (dh + dx, dg)

    in_args = ("mm_d_u_rms_in_bwd", dproj, w_in, tmm, in_shard, in_b, [(x, 1024, 0), (dh1, 1024, 0)], [g1],
               [(1024, F32)])
    if comm is None:
        gb["w_in"] = _mm_tn("mm_d_w_in", u, dproj, tmm, in_shard, col_shards=True)
        grad_x, d_g1 = _mm_nt_then(*in_args, accs=[(1, 1024)])
    else:
        gb["w_in"], landed = _mm_tn("mm_d_w_in", u, dproj, tmm, in_shard, col_shards=True,
                                    riding=comm.scatter("rest", _pack_rest_full(gb)))
        comm.landed["rest"] = landed
        grad_x, d_g1, landed = _mm_nt_then(*in_args, accs=[(1, 1024)], riding=comm.scatter(
            "in", gb["w_in"].reshape(N_CHIPS, 2, D_MODEL // 2, in_shard)))
        comm.landed["in"] = landed

    back_t = lambda b: b.reshape(S5_GROUP, S5_GROUPS, S5_STATE).transpose(1, 2, 0).reshape(1, S5_GROUPS, S5_STATE,
                                                                                           S5_GROUP)
    gs = {
        "norm_g": d_g1, "hg_lb": d_lb, "hg_norm_g": d_ghn,
        "s5_a_re": d_are.reshape(1, S5_GROUPS, S5_STATE), "s5_a_im": d_aim.reshape(1, S5_GROUPS, S5_STATE),
        "s5_log_dt": d_ldt[0:1, :S5_GROUPS],
        "s5_b_re": back_t(d_br_t), "s5_b_im": back_t(d_bi_t),
        "s5_c_re": _diag_blocks4(d_crt.transpose(0, 2, 1)).reshape(1, S5_GROUPS, S5_GROUP, S5_STATE),
        "s5_c_im": _diag_blocks4(d_cit.transpose(0, 2, 1)).reshape(1, S5_GROUPS, S5_GROUP, S5_STATE),
        "s5_d": d_d.reshape(1, S5_GROUPS, S5_GROUP), "b_glu": d_bglu, "ple_norm_g": d_g2,
        "final_norm_g": d_g3.reshape(D_MODEL),
    }
    return loss_row, grad_x, gb, gs


def _shard_shape(name):
    r, c = BIG_SHAPE[name]
    return (r, c // N_CHIPS) if name in BIG_COL_SHARDED else (r // N_CHIPS, c)


def _pack_shard(parts):
    return jnp.concatenate([parts[n].reshape(-1, PACK_W) for n in BIG], axis=0)


def _unpack_shard(packed):
    out, off = {}, 0
    for n in BIG:
        r, c = _shard_shape(n)
        rows = r * c // PACK_W
        out[n] = packed[off:off + rows].reshape(1, r, c)
        off += rows
    return out


def _unpack_full(gathered):
    out, off = {}, 0
    for n in BIG:
        r, c = _shard_shape(n)
        rows = r * c // PACK_W
        sh = gathered[:, off:off + rows].reshape(N_CHIPS, r, c)
        out[n] = sh.transpose(1, 0, 2).reshape(BIG_SHAPE[n]) if n in BIG_COL_SHARDED else sh.reshape(BIG_SHAPE[n])
        off += rows
    return out


def _pack_full(full):
    parts = []
    for n in BIG:
        r, c = _shard_shape(n)
        g = full[n]
        sh = g.reshape(BIG_SHAPE[n][0], N_CHIPS, c).transpose(1, 0, 2) if n in BIG_COL_SHARDED else g
        parts.append(sh.reshape(N_CHIPS, r * c // PACK_W, PACK_W))
    packed = jnp.concatenate(parts, axis=1)
    return packed.reshape(N_CHIPS, 2, HALF_ROWS, PACK_W).transpose(1, 0, 2, 3)


def _pack_small(parts, last):
    flat = jnp.concatenate([parts[n].reshape(-1) for n in SMALL] + [last.reshape(-1)])
    return jnp.pad(flat, (0, SMALL_ROWS * PACK_W - flat.shape[0])).reshape(SMALL_ROWS, PACK_W)


def _unpack_small(packed):
    flat, out, off = packed.reshape(-1), {}, 0
    for n in SMALL:
        size = 1
        for d in SMALL_SHAPE[n]:
            size *= d
        out[n] = flat[off:off + size].reshape(SMALL_SHAPE[n])
        off += size
    return out, flat[off]


def _place():
    x, y, c = lax.axis_index("x"), lax.axis_index("y"), lax.axis_index("c")
    return x, y, c, [(1 - x, y), (x, 1 - y), (1 - x, 1 - y)]


def _remote(src, dst, send_sems, recv_sems, k, to):
    return pltpu.make_async_remote_copy(src_ref=src, dst_ref=dst, send_sem=send_sems.at[k], recv_sem=recv_sems.at[k],
                                        device_id=to, device_id_type=MESH)


_HBM = pl.BlockSpec(memory_space=pl.ANY)


def _all_gather_weights(wp):
    def body(wp_ref, out_ref, send_sems, recv_sems):
        x, y, c, chips = _place()
        k = 2 * x + y
        sibling = (x, y, 1 - c)
        first =[_remote(wp_ref.at[c], out_ref.at[k, c], send_sems, recv_sems, j, (cx, cy, c))
                 for j, (cx, cy) in enumerate(chips)]
        for cp in first:
            cp.start()
        passed = []
        for j, (cx, cy) in enumerate(chips):
            kj = 2 * cx + cy
            _remote(wp_ref.at[c], out_ref.at[kj, c], send_sems, recv_sems, j, (cx, cy, c)).wait_recv()
            cp = _remote(out_ref.at[kj, c], out_ref.at[kj, c], send_sems, recv_sems, 3 + j, sibling)
            cp.start()
            passed.append(cp)
        for j, (cx, cy) in enumerate(chips):
            kj = 2 * cx + cy
            _remote(wp_ref.at[c], out_ref.at[kj, 1 - c], send_sems, recv_sems, 3 + j, sibling).wait_recv()
        for cp in first + passed:
            cp.wait_send()

    return pl.pallas_call(
        body, name="all_gather_weights", in_specs=[_HBM], out_specs=_HBM,
        out_shape=jax.ShapeDtypeStruct((N_CHIPS, 2, HALF_ROWS, PACK_W), wp.dtype),
        scratch_shapes=[pltpu.SemaphoreType.DMA((6,)), pltpu.SemaphoreType.DMA((6,))])(wp)


def _exchange_halves(pg):
    def body(pg_ref, out_ref, send_sems, recv_sems):
        x, y, c, _ = _place()
        cp = _remote(pg_ref.at[1 - c], out_ref, send_sems, recv_sems, 0, (x, y, 1 - c))
        cp.start()
        cp.wait()

    return pl.pallas_call(
        body, name="exchange_halves", in_specs=[_HBM], out_specs=_HBM,
        out_shape=jax.ShapeDtypeStruct((N_CHIPS, HALF_ROWS, PACK_W), pg.dtype),
        scratch_shapes=[pltpu.SemaphoreType.DMA((1,)), pltpu.SemaphoreType.DMA((1,))])(pg)


def _scatter_chip_sums(ps):
    def body(ps_ref, out_ref, send_sems, recv_sems):
        x, y, c, chips = _place()
        cps = [_remote(ps_ref.at[2 * cx + cy], out_ref.at[j], send_sems, recv_sems, j, (cx, cy, c))
               for j, (cx, cy) in enumerate(chips)]
        for cp in cps:
            cp.start()
        for cp in cps:
            cp.wait()

    return pl.pallas_call(
        body, name="scatter_chip_sums", in_specs=[_HBM], out_specs=_HBM,
        out_shape=jax.ShapeDtypeStruct((3, HALF_ROWS, PACK_W), ps.dtype),
        scratch_shapes=[pltpu.SemaphoreType.DMA((3,)), pltpu.SemaphoreType.DMA((3,))])(ps)


def _share_half(g_half):
    def body(g_ref, out_ref, send_sems, recv_sems):
        x, y, c, _ = _place()
        cp = _remote(g_ref, out_ref.at[c], send_sems, recv_sems, 0, (x, y, 1 - c))
        cp.start()
        _remote(g_ref, out_ref.at[1 - c], send_sems, recv_sems, 0, (x, y, 1 - c)).wait_recv()
        cp.wait_send()

    return pl.pallas_call(
        body, name="share_half", in_specs=[_HBM], out_specs=_HBM,
        out_shape=jax.ShapeDtypeStruct((2, HALF_ROWS, PACK_W), g_half.dtype),
        scratch_shapes=[pltpu.SemaphoreType.DMA((1,)), pltpu.SemaphoreType.DMA((1,))])(g_half)


REDUCE_ROWS = 480


def _sum_pair(pg, theirs, c):
    def body(c_ref, a_ref, b_ref, o_ref):
        o_ref[...] = (a_ref[...] + b_ref[...]).astype(o_ref.dtype)

    return pl.pallas_call(
        body, name="sum_pair",
        grid_spec=pltpu.PrefetchScalarGridSpec(
            num_scalar_prefetch=1, grid=(N_CHIPS, HALF_ROWS // REDUCE_ROWS),
            in_specs=[pl.BlockSpec((None, None, REDUCE_ROWS, PACK_W), lambda j, i, c_ref: (c_ref[0], j, i, 0)),
                      pl.BlockSpec((None, REDUCE_ROWS, PACK_W), lambda j, i, c_ref: (j, i, 0))],
            out_specs=pl.BlockSpec((None, REDUCE_ROWS, PACK_W), lambda j, i, c_ref: (j, i, 0))),
        out_shape=jax.ShapeDtypeStruct((N_CHIPS, HALF_ROWS, PACK_W), WIRE_DTYPE),
        compiler_params=_params("arbitrary", "arbitrary"))(c.reshape(1), pg, theirs)


def _sum_chips(ps, others, k):
    def body(k_ref, a_ref, b_ref, o_ref):
        o_ref[...] = ((a_ref[...].astype(F32) + b_ref[0].astype(F32)) + b_ref[1].astype(F32)) + b_ref[2].astype(F32)

    return pl.pallas_call(
        body, name="sum_chips",
        grid_spec=pltpu.PrefetchScalarGridSpec(
            num_scalar_prefetch=1, grid=(HALF_ROWS // REDUCE_ROWS,),
            in_specs=[pl.BlockSpec((None, REDUCE_ROWS, PACK_W), lambda i, k_ref: (k_ref[0], i, 0)),
                      pl.BlockSpec((3, REDUCE_ROWS, PACK_W), lambda i, k_ref: (0, i, 0))],
            out_specs=pl.BlockSpec((REDUCE_ROWS, PACK_W), lambda i, k_ref: (i, 0))),
        out_shape=jax.ShapeDtypeStruct((HALF_ROWS, PACK_W), F32),
        compiler_params=_params("arbitrary"))(k.reshape(1), ps, others)


REST = tuple(n for n in BIG if n != "w_in")
REST_ROWS = sum(BIG_SHAPE[n][0] * BIG_SHAPE[n][1] for n in REST) // (N_CHIPS * PACK_W)
IN_SHARD = IN_COLS // N_CHIPS
IN_TILE, REST_TILE = 256, 272


def _pack_rest(parts):
    return jnp.concatenate([parts[n].reshape(-1, PACK_W) for n in REST], axis=0)


def _unpack_rest(packed):
    out, off = {}, 0
    for n in REST:
        r, c = _shard_shape(n)
        rows = r * c // PACK_W
        out[n] = packed[off:off + rows].reshape(1, r, c)
        off += rows
    return out


def _unpack_rest_full(gathered):
    out, off = {}, 0
    for n in REST:
        r, c = _shard_shape(n)
        rows = r * c // PACK_W
        sh = gathered[:, off:off + rows].reshape(N_CHIPS, r, c)
        out[n] = sh.transpose(1, 0, 2).reshape(BIG_SHAPE[n]) if n in BIG_COL_SHARDED else sh.reshape(BIG_SHAPE[n])
        off += rows
    return out


def _pack_rest_full(full):
    parts = []
    for n in REST:
        r, c = _shard_shape(n)
        g = full[n]
        sh = g.reshape(BIG_SHAPE[n][0], N_CHIPS, c).transpose(1, 0, 2) if n in BIG_COL_SHARDED else g
        parts.append(sh.reshape(N_CHIPS, r * c // PACK_W, PACK_W))
    return jnp.concatenate(parts, axis=1).reshape(N_CHIPS, 2, REST_ROWS // 2, PACK_W)


def _gather_shards(ws):
    n = len(ws)

    def body(*refs):
        w_refs, out_refs, (send_sems, recv_sems) = refs[:n], refs[n:2 * n], refs[2 * n:]
        x, y, c, chips = _place()
        k = 2 * x + y
        sibling = (x, y, 1 - c)
        first = [_remote(w_ref.at[c], out_ref.at[k, c], send_sems, recv_sems, 6 * g + j, (cx, cy, c))
                 for j, (cx, cy) in enumerate(chips) for g, (w_ref, out_ref) in enumerate(zip(w_refs, out_refs))]
        for cp in first:
            cp.start()
        passed = []
        for j, (cx, cy) in enumerate(chips):
            kj = 2 * cx + cy
            for g, (w_ref, out_ref) in enumerate(zip(w_refs, out_refs)):
                _remote(w_ref.at[c], out_ref.at[kj, c], send_sems, recv_sems, 6 * g + j, (cx, cy, c)).wait_recv()
                cp = _remote(out_ref.at[kj, c], out_ref.at[kj, c], send_sems, recv_sems, 6 * g + 3 + j, sibling)
                cp.start()
                passed.append(cp)
        for j, (cx, cy) in enumerate(chips):
            kj = 2 * cx + cy
            for g, (w_ref, out_ref) in enumerate(zip(w_refs, out_refs)):
                _remote(w_ref.at[c], out_ref.at[kj, 1 - c], send_sems, recv_sems, 6 * g + 3 + j, sibling).wait_recv()
        for cp in first + passed:
            cp.wait_send()

    return pl.pallas_call(
        body, name="all_gather_weights", in_specs=[_HBM] * n, out_specs=[_HBM] * n,
        out_shape=[jax.ShapeDtypeStruct((N_CHIPS,) + w.shape, w.dtype) for w in ws],
        scratch_shapes=[pltpu.SemaphoreType.DMA((6 * n,)), pltpu.SemaphoreType.DMA((6 * n,))])(*ws)


def _swap_halves(pgs, name="exchange_halves"):
    n = len(pgs)

    def body(*refs):
        pg_refs, out_refs, (send_sems, recv_sems) = refs[:n], refs[n:2 * n], refs[2 * n:]
        x, y, c, _ = _place()
        cps = [_remote(pg_ref.at[j, 1 - c], out_ref.at[j], send_sems, recv_sems, N_CHIPS * g + j, (x, y, 1 - c))
               for g, (pg_ref, out_ref) in enumerate(zip(pg_refs, out_refs)) for j in range(N_CHIPS)]
        for cp in cps:
            cp.start()
        for cp in cps:
            cp.wait()

    return pl.pallas_call(
        body, name=name, in_specs=[_HBM] * n, out_specs=[_HBM] * n,
        out_shape=[jax.ShapeDtypeStruct((N_CHIPS,) + pg.shape[2:], pg.dtype) for pg in pgs],
        scratch_shapes=[pltpu.SemaphoreType.DMA((N_CHIPS * n,)), pltpu.SemaphoreType.DMA((N_CHIPS * n,))])(*pgs)


def _scatter_sums(pss):
    n = len(pss)

    def body(*refs):
        ps_refs, out_refs, (send_sems, recv_sems) = refs[:n], refs[n:2 * n], refs[2 * n:]
        x, y, c, chips = _place()
        cps = [_remote(ps_ref.at[2 * cx + cy], out_ref.at[j], send_sems, recv_sems, 3 * g + j, (cx, cy, c))
               for j, (cx, cy) in enumerate(chips) for g, (ps_ref, out_ref) in enumerate(zip(ps_refs, out_refs))]
        for cp in cps:
            cp.start()
        for cp in cps:
            cp.wait()

    return pl.pallas_call(
        body, name="scatter_chip_sums", in_specs=[_HBM] * n, out_specs=[_HBM] * n,
        out_shape=[jax.ShapeDtypeStruct((3,) + ps.shape[1:], ps.dtype) for ps in pss],
        scratch_shapes=[pltpu.SemaphoreType.DMA((3 * n,)), pltpu.SemaphoreType.DMA((3 * n,))])(*pss)


def _share_halves(gs):
    n = len(gs)

    def body(*refs):
        g_refs, out_refs, (send_sems, recv_sems) = refs[:n], refs[n:2 * n], refs[2 * n:]
        x, y, c, _ = _place()
        cps = [_remote(g_ref, out_ref.at[c], send_sems, recv_sems, g, (x, y, 1 - c))
               for g, (g_ref, out_ref) in enumerate(zip(g_refs, out_refs))]
        for cp in cps:
            cp.start()
        for g, (g_ref, out_ref) in enumerate(zip(g_refs, out_refs)):
            _remote(g_ref, out_ref.at[1 - c], send_sems, recv_sems, g, (x, y, 1 - c)).wait_recv()
        for cp in cps:
            cp.wait_send()

    return pl.pallas_call(
        body, name="share_half", in_specs=[_HBM] * n, out_specs=[_HBM] * n,
        out_shape=[jax.ShapeDtypeStruct((2,) + g.shape, g.dtype) for g in gs],
        scratch_shapes=[pltpu.SemaphoreType.DMA((n,)), pltpu.SemaphoreType.DMA((n,))])(*gs)


def _pair_sum(name, pg, theirs, c, tile):
    _, _, rows, width = pg.shape

    def body(c_ref, a_ref, b_ref, o_ref):
        o_ref[...] = (a_ref[...] + b_ref[...]).astype(o_ref.dtype)

    return pl.pallas_call(
        body, name=name,
        grid_spec=pltpu.PrefetchScalarGridSpec(
            num_scalar_prefetch=1, grid=(N_CHIPS, rows // tile),
            in_specs=[pl.BlockSpec((None, None, tile, width), lambda j, i, c_ref: (j, c_ref[0], i, 0)),
                      pl.BlockSpec((None, tile, width), lambda j, i, c_ref: (j, i, 0))],
            out_specs=pl.BlockSpec((None, tile, width), lambda j, i, c_ref: (j, i, 0))),
        out_shape=jax.ShapeDtypeStruct((N_CHIPS, rows, width), WIRE_DTYPE),
        compiler_params=_params("arbitrary", "arbitrary"))(c.reshape(1), pg, theirs)


def _chip_sum(name, ps, others, k, tile):
    _, rows, width = ps.shape

    def body(k_ref, a_ref, b_ref, o_ref):
        o_ref[...] = ((a_ref[...].astype(F32) + b_ref[0].astype(F32)) + b_ref[1].astype(F32)) + b_ref[2].astype(F32)

    return pl.pallas_call(
        body, name=name,
        grid_spec=pltpu.PrefetchScalarGridSpec(
            num_scalar_prefetch=1, grid=(rows // tile,),
            in_specs=[pl.BlockSpec((None, tile, width), lambda i, k_ref: (k_ref[0], i, 0)),
                      pl.BlockSpec((3, tile, width), lambda i, k_ref: (0, i, 0))],
            out_specs=pl.BlockSpec((tile, width), lambda i, k_ref: (i, 0))),
        out_shape=jax.ShapeDtypeStruct((rows, width), F32),
        compiler_params=_params("arbitrary"))(k.reshape(1), ps, others)


class _StepComm:
    TILES = {"in": IN_TILE, "rest": REST_TILE}

    def __init__(self, rest_wire, chip, core):
        self.rest_wire, self.chip, self.core = rest_wire, chip, core
        self.sums, self.landed = {}, {}

    def gather_rest(self):
        wire = self.rest_wire

        def sends(ins, outs, send_sems, recv_sems):
            (w_ref,), (out_ref,) = ins, outs
            x, y, c, chips = _place()
            return [_remote(w_ref.at[c], out_ref.at[2 * x + y, c], send_sems, recv_sems, 4 * j + 2 * c + to,
                            (cx, cy, to)) for j, (cx, cy) in enumerate(chips) for to in (0, 1)]

        def recvs(ins, outs, send_sems, recv_sems):
            (w_ref,), (out_ref,) = ins, outs
            _, _, c, chips = _place()
            return [_remote(w_ref.at[c], out_ref.at[2 * cx + cy, by], send_sems, recv_sems, 4 * j + 2 * by + c,
                            (cx, cy, by)) for j, (cx, cy) in enumerate(chips) for by in (0, 1)]

        def start(*refs):
            for cp in sends(*refs):
                cp.start()

        def wait(*refs):
            for cp in recvs(*refs):
                cp.wait_recv()
            for cp in sends(*refs):
                cp.wait_send()

        return _Riding((wire,), (jax.ShapeDtypeStruct((N_CHIPS,) + wire.shape, wire.dtype),), 12, start, wait)

    def rest_weights(self, landed):
        full = lax.dynamic_update_slice(landed, self.rest_wire[None], (self.chip, 0, 0, 0))
        return _unpack_rest_full(full.reshape(N_CHIPS, REST_ROWS, PACK_W))

    def scatter(self, group, pg):
        (theirs,) = _swap_halves([pg], "exchange_halves_" + group)
        ps = _pair_sum("sum_pair_" + group, pg, theirs, self.core, self.TILES[group])
        self.sums[group] = ps

        def copies(ins, outs, send_sems, recv_sems):
            (ps_ref,), (out_ref,) = ins, outs
            _, _, c, chips = _place()
            return [_remote(ps_ref.at[2 * cx + cy], out_ref.at[j], send_sems, recv_sems, j, (cx, cy, c))
                    for j, (cx, cy) in enumerate(chips)]

        def start(*refs):
            for cp in copies(*refs):
                cp.start()

        def wait(*refs):
            for cp in copies(*refs):
                cp.wait()

        return _Riding((ps,), (jax.ShapeDtypeStruct((3,) + ps.shape[1:], ps.dtype),), 3, start, wait)

    def reduced(self, group):
        return _chip_sum("sum_chips_" + group, self.sums[group], self.landed[group], self.chip, self.TILES[group])


def _adamw(w, g, m, v):
    m = ADAM_B1 * m + (1.0 - ADAM_B1) * g
    v = ADAM_B2 * v + (1.0 - ADAM_B2) * (g * g)
    m_hat = m / (1.0 - ADAM_B1 ** ADAM_STEP)
    v_hat = v / (1.0 - ADAM_B2 ** ADAM_STEP)
    return -ADAM_LR * (m_hat / (jnp.sqrt(v_hat) + ADAM_EPS) + ADAM_WD * w), m, v


def _small_reduce_adamw(part, w, m, v):
    def body(part_ref, w_ref, m_ref, v_ref, g_ref, d_ref, nm_ref, nv_ref, all_ref, send_sems, recv_sems):
        x, y, c, chips = _place()
        me, sibling = (x, y, c), (x, y, 1 - c)

        def rows(px, py, pc):
            return all_ref.at[4 * px + 2 * py + pc]

        all_ref[4 * x + 2 * y + c] = part_ref[...]
        first = [_remote(part_ref, rows(*me), send_sems, recv_sems, 0, sibling)]
        first += [_remote(part_ref, rows(*me), send_sems, recv_sems, 1 + j, (cx, cy, c))
                  for j, (cx, cy) in enumerate(chips)]
        for cp in first:
            cp.start()
        passed = []
        for j, (cx, cy) in enumerate(chips):
            _remote(part_ref, rows(cx, cy, c), send_sems, recv_sems, 1 + j, me).wait_recv()
            cp = _remote(rows(cx, cy, c), rows(cx, cy, c), send_sems, recv_sems, 4 + j, sibling)
            cp.start()
            passed.append(cp)
        _remote(part_ref, rows(*sibling), send_sems, recv_sems, 0, me).wait_recv()
        for j, (cx, cy) in enumerate(chips):
            _remote(part_ref, rows(cx, cy, 1 - c), send_sems, recv_sems, 4 + j, me).wait_recv()
        for cp in first + passed:
            cp.wait_send()
        g = all_ref[0]
        for dev in range(1, N_DEV):
            g = g + all_ref[dev]
        delta, nm, nv = _adamw(w_ref[...], g, m_ref[...], v_ref[...])
        g_ref[...] = g
        d_ref[...] = delta
        nm_ref[...] = nm
        nv_ref[...] = nv

    whole = pl.BlockSpec(memory_space=pltpu.VMEM)
    shape = jax.ShapeDtypeStruct((SMALL_ROWS, PACK_W), F32)
    return pl.pallas_call(
        body, name="small_reduce_adamw", in_specs=[whole] * 4, out_specs=[whole] * 4, out_shape=[shape] * 4,
        scratch_shapes=[pltpu.VMEM((N_DEV, SMALL_ROWS, PACK_W), F32), pltpu.SemaphoreType.DMA((7,)),
                        pltpu.SemaphoreType.DMA((7,))],
        compiler_params=pltpu.CompilerParams(vmem_limit_bytes=VMEM_LIMIT))(part, w, m, v)


def kernel(x, p, norm_g, w_in, hg_lb, hg_norm_g, w_o_hg, s5_a_re, s5_a_im, s5_log_dt, s5_b_re, s5_b_im, s5_c_re, s5_c_im, s5_d, w_glu, b_glu, w_o_s5, w_out, ple_norm_g, w_ple, w_ple_gate, final_norm_g, loss_target, m_norm_g, m_w_in, m_hg_lb, m_hg_norm_g, m_w_o_hg, m_s5_a_re, m_s5_a_im, m_s5_log_dt, m_s5_b_re, m_s5_b_im, m_s5_c_re, m_s5_c_im, m_s5_d, m_w_glu, m_b_glu, m_w_o_s5, m_w_out, m_ple_norm_g, m_w_ple, m_w_ple_gate, m_final_norm_g, v_norm_g, v_w_in, v_hg_lb, v_hg_norm_g, v_w_o_hg, v_s5_a_re, v_s5_a_im, v_s5_log_dt, v_s5_b_re, v_s5_b_im, v_s5_c_re, v_s5_c_im, v_s5_d, v_w_glu, v_b_glu, v_w_o_s5, v_w_out, v_ple_norm_g, v_w_ple, v_w_ple_gate, v_final_norm_g):
    given = dict(locals())
    wts = {n: given[n] for n in WEIGHTS}
    mom = {n: given["m_" + n] for n in WEIGHTS}
    var = {n: given["v_" + n] for n in WEIGHTS}
    cx, cy, cc = lax.axis_index("x"), lax.axis_index("y"), lax.axis_index("c")
    chip = (2 * cx + cy).astype(jnp.int32)

    core = cc.astype(jnp.int32)
    rest_shard = _pack_rest({n: wts[n][0] for n in REST})
    in_wire = wts["w_in"][0].astype(MXU_DTYPE).reshape(2, D_MODEL // 2, IN_SHARD)
    (w_in_all,) = _gather_shards([in_wire])
    w_in_all = lax.dynamic_update_slice(w_in_all, in_wire[None], (chip, 0, 0, 0)).reshape(N_CHIPS, D_MODEL, IN_SHARD)
    comm = _StepComm(rest_shard.astype(MXU_DTYPE).reshape(2, REST_ROWS // 2, PACK_W), chip, core)

    t_len = x.shape[1]
    loss_row, grad_x, g_big, g_small = _local_step(x.reshape(t_len, D_MODEL), p.reshape(t_len, -1),
                                                   loss_target.reshape(t_len, D_MODEL), {"w_in": w_in_all},
                                                   {n: wts[n] for n in SMALL}, comm)

    zero = jnp.zeros((), F32)
    sg, sd, snm, snv = _small_reduce_adamw(_pack_small(g_small, loss_row[0, 0]),
                                           _pack_small({n: wts[n] for n in SMALL}, zero),
                                           _pack_small({n: mom[n] for n in SMALL}, zero),
                                           _pack_small({n: var[n] for n in SMALL}, zero))
    (sg, loss), (sd, _), (snm, _), (snv, _) = (_unpack_small(a) for a in (sg, sd, snm, snv))

    halves = [comm.reduced("in"), comm.reduced("rest")]
    g_in, g_rest = [lax.dynamic_update_slice(got, mine[None], (core, 0, 0))
                    for got, mine in zip(_share_halves(halves), halves)]
    g_in, g_rest = g_in.reshape(D_MODEL, IN_SHARD), g_rest.reshape(REST_ROWS, PACK_W)

    def adam_f(wv, gv, mv, vv):
        return _adamw(wv, gv, mv, vv)

    d_in, nm_in, nv_in = _rowwise("adamw_in", adam_f, D_MODEL, IN_TILE,
                                  [(wts["w_in"][0], IN_SHARD, 0), (g_in, IN_SHARD, 0), (mom["w_in"][0], IN_SHARD, 0),
                                   (var["w_in"][0], IN_SHARD, 0)], [], [(IN_SHARD, F32)] * 3)
    d_rest, nm_rest, nv_rest = _rowwise("adamw_rest", adam_f, REST_ROWS, REST_TILE,
                                        [(rest_shard, PACK_W, 0), (g_rest, PACK_W, 0),
                                         (_pack_rest({n: mom[n][0] for n in REST}), PACK_W, 0),
                                         (_pack_rest({n: var[n][0] for n in REST}), PACK_W, 0)], [],
                                        [(PACK_W, F32)] * 3)
    bg, bd, bnm, bnv = (dict(_unpack_rest(rest), w_in=a.reshape(1, D_MODEL, IN_SHARD))
                        for rest, a in ((g_rest, g_in), (d_rest, d_in), (nm_rest, nm_in), (nv_rest, nv_in)))

    outs = [loss, grad_x.reshape(x.shape)]
    for small, big in ((sg, bg), (sd, bd), (snm, bnm), (snv, bnv)):
        outs += [big[n] if n in BIG else small[n] for n in WEIGHTS]
    return tuple(outs)
```

```python
import functools
from typing import Callable, NamedTuple

import jax
import jax.numpy as jnp
from jax import lax
from jax.experimental import pallas as pl
from jax.experimental.pallas import tpu as pltpu

F32 = jnp.float32
MXU_DTYPE = jnp.bfloat16
WIRE_DTYPE = jnp.bfloat16
NORM_EPS = 1e-6
D_MODEL = 1024
HG_HEADS = 8
HG_DIM = 128
HG_CHUNK = 64
S5_WIDTH = 512
S5_GROUPS = 32
S5_GROUP = 16
S5_STATE = 64
S5_LANES = S5_GROUPS * S5_STATE
IN_COLS = 7168
SUBLANES = 8
VMEM_LIMIT = 56 * 1024 * 1024
HIGHEST = lax.Precision.HIGHEST
MESH = pl.DeviceIdType.MESH

ADAM_LR, ADAM_B1, ADAM_B2, ADAM_EPS, ADAM_WD, ADAM_STEP = 0.001, 0.9, 0.999, 1e-08, 0.01, 10

BIG = ("w_in", "w_o_hg", "w_glu", "w_o_s5", "w_out", "w_ple", "w_ple_gate")
BIG_SHAPE = {"w_in": (1024, 7168), "w_o_hg": (1024, 1024), "w_glu": (512, 1024), "w_o_s5": (512, 1024),
             "w_out": (1024, 1024), "w_ple": (256, 1024), "w_ple_gate": (1024, 1024)}
BIG_COL_SHARDED = ("w_in", "w_glu", "w_o_s5", "w_ple")
SMALL = ("norm_g", "hg_lb", "hg_norm_g", "s5_a_re", "s5_a_im", "s5_log_dt", "s5_b_re", "s5_b_im", "s5_c_re",
         "s5_c_im", "s5_d", "b_glu", "ple_norm_g", "final_norm_g")
SMALL_SHAPE = {"norm_g": (1, 1024), "hg_lb": (2, 1024), "hg_norm_g": (1, 1024), "s5_a_re": (1, 32, 64),
               "s5_a_im": (1, 32, 64), "s5_log_dt": (1, 32), "s5_b_re": (1, 32, 64, 16), "s5_b_im": (1, 32, 64, 16),
               "s5_c_re": (1, 32, 16, 64), "s5_c_im": (1, 32, 16, 64), "s5_d": (1, 32, 16), "b_glu": (1, 1024),
               "ple_norm_g": (1, 1024), "final_norm_g": (1024,)}
WEIGHTS = ("norm_g", "w_in", "hg_lb", "hg_norm_g", "w_o_hg", "s5_a_re", "s5_a_im", "s5_log_dt", "s5_b_re", "s5_b_im",
           "s5_c_re", "s5_c_im", "s5_d", "w_glu", "b_glu", "w_o_s5", "w_out", "ple_norm_g", "w_ple", "w_ple_gate",
           "final_norm_g")
N_CHIPS = 4
N_DEV = 8
PACK_W = 1024
SHARD_ROWS = sum(BIG_SHAPE[n][0] * BIG_SHAPE[n][1] for n in BIG) // (N_CHIPS * PACK_W)
HALF_ROWS = SHARD_ROWS // 2
SMALL_ROWS = 144


def _params(*sem):
    return pltpu.CompilerParams(dimension_semantics=sem, vmem_limit_bytes=VMEM_LIMIT)


def _sig(x):
    return 1.0 / (1.0 + jnp.exp(-x))


def _dsilu(z, s):
    return s * (1.0 + z * (1.0 - s))


def _mx(x):
    return x.astype(MXU_DTYPE)


def _dot(a, b, dims=(((1,), (0,)), ((), ()))):
    return lax.dot_general(_mx(a), _mx(b), dims, preferred_element_type=F32)


_NT = (((1,), (1,)), ((), ()))
_TN = (((0,), (0,)), ((), ()))


def _dot32(a, b):
    return jnp.dot(a, b, precision=HIGHEST, preferred_element_type=F32)


def _rms_bwd(dy, x, g):
    r = lax.rsqrt(jnp.mean(x * x, axis=-1, keepdims=True) + NORM_EPS)
    t = dy * g
    dx = r * t - x * (r * r * r) * jnp.mean(t * x, axis=-1, keepdims=True)
    return dx, jnp.sum(dy * x * r, axis=0, keepdims=True)


def _rowwise(name, fn, n_rows_total, tm, rows, consts, outs, accs=(), alias=None):
    n_r, n_c, n_o, n_a = len(rows), len(consts), len(outs), len(accs)

    def body(*refs):
        row_refs = refs[:n_r]
        const_refs = refs[n_r:n_r + n_c]
        pos = n_r + n_c + (1 if alias is not None else 0)
        out_refs = refs[pos:pos + n_o]
        acc_refs = refs[pos + n_o:pos + n_o + n_a]
        res = fn(*[r[...] for r in row_refs], *[r[...] for r in const_refs])
        for r, v in zip(out_refs, res[:n_o]):
            r[...] = v.astype(r.dtype)
        if n_a:
            @pl.when(pl.program_id(0) == 0)
            def _():
                for r in acc_refs:
                    r[...] = jnp.zeros_like(r)
            for r, v in zip(acc_refs, res[n_o:]):
                r[...] += v

    in_specs = [pl.BlockSpec((tm, w), functools.partial(lambda i, cb: (i, cb), cb=cb)) for (_, w, cb) in rows]
    in_specs += [pl.BlockSpec(c.shape, lambda i: (0, 0)) for c in consts]
    args = [a for (a, _, _) in rows] + list(consts)
    out_shape, out_specs = [], []
    for o in outs:
        w, dt = o[0], o[1]
        cb, total = (o[2], o[3]) if len(o) == 4 else (0, w)
        out_shape.append(jax.ShapeDtypeStruct((n_rows_total, total), dt))
        out_specs.append(pl.BlockSpec((tm, w), functools.partial(lambda i, cb: (i, cb), cb=cb)))
    io_alias = {}
    if alias is not None:
        in_specs.append(pl.BlockSpec(memory_space=pl.ANY))
        args.append(alias[0])
        io_alias = {len(args) - 1: alias[1]}
    for (r, w) in accs:
        out_shape.append(jax.ShapeDtypeStruct((r, w), F32))
        out_specs.append(pl.BlockSpec((r, w), lambda i: (0, 0)))
    res = pl.pallas_call(body, name=name, grid=(n_rows_total // tm,), in_specs=in_specs, out_specs=out_specs,
                         out_shape=out_shape, input_output_aliases=io_alias,
                         compiler_params=_params("arbitrary"))(*args)
    return res


class _Riding(NamedTuple):
    ins: tuple
    outs: tuple
    n_sems: int
    start: Callable
    wait: Callable


_HBM = pl.BlockSpec(memory_space=pl.ANY)


def _ride(riding, refs, n_in, n_out, n_scratch, first, last):
    if riding is None:
        return refs[:n_in], refs[n_in:n_in + n_out], refs[n_in + n_out:]
    r_in, r_out = len(riding.ins), len(riding.outs)
    ins, rins = refs[:n_in], refs[n_in:n_in + r_in]
    pos = n_in + r_in
    outs, routs = refs[pos:pos + n_out], refs[pos + n_out:pos + n_out + r_out]
    pos += n_out + r_out
    scratch, (send_sems, recv_sems) = refs[pos:pos + n_scratch], refs[pos + n_scratch:]

    @pl.when(first)
    def _():
        riding.start(rins, routs, send_sems, recv_sems)

    @pl.when(last)
    def _():
        riding.wait(rins, routs, send_sems, recv_sems)

    return ins, outs, scratch


def _riding_call(riding, body, name, grid, in_specs, args, out_specs, out_shape, scratch, io_alias=None):
    if riding is not None:
        in_specs = list(in_specs) + [_HBM] * len(riding.ins)
        args = list(args) + list(riding.ins)
        out_specs = list(out_specs) + [_HBM] * len(riding.outs)
        out_shape = list(out_shape) + list(riding.outs)
        scratch = list(scratch) + [pltpu.SemaphoreType.DMA((riding.n_sems,))] * 2
    return pl.pallas_call(body, name=name, grid=grid, in_specs=in_specs, out_specs=out_specs, out_shape=out_shape,
                          scratch_shapes=scratch, input_output_aliases=io_alias or {},
                          compiler_params=_params(*(["arbitrary"] * len(grid))))(*args)


def _mm_nn(name, a, b, tm, tn, riding=None, prologue=None, consts=()):
    m, k = a.shape
    n = b.shape[1] if b.ndim == 2 else b.shape[0] * b.shape[2]
    grid = (n // tn, m // tm)
    n_out, scratch = (1, []) if prologue is None else (2, [pltpu.VMEM((m, k), MXU_DTYPE)])

    def body(*refs):
        j, i = pl.program_id(0), pl.program_id(1)
        ins, outs, kept = _ride(riding, refs, 2 + len(consts), n_out, len(scratch), (j == 0) & (i == 0),
                                (j == grid[0] - 1) & (i == grid[1] - 1))
        if prologue is None:
            left = ins[0][...]
        else:
            rows = pl.ds(pl.multiple_of(i * tm, tm), tm)

            @pl.when(j == 0)
            def _():
                tile = _mx(prologue(ins[0][...], *[c[...] for c in ins[2:]]))
                kept[0][rows, :] = tile
                outs[1][...] = tile

            left = kept[0][rows, :]
        outs[0][...] = _dot(left, ins[1][...])

    once = (lambda j, i: (i, 0)) if prologue is None else (lambda j, i: (jnp.where(j == 0, i, grid[1] - 1), 0))
    b_spec = (pl.BlockSpec((k, tn), lambda j, i: (0, j)) if b.ndim == 2
              else pl.BlockSpec((None, k, tn), lambda j, i: (j, 0, 0)))
    in_specs = [pl.BlockSpec((tm, k), once), b_spec]
    in_specs += [pl.BlockSpec(c.shape, lambda j, i: (0, 0)) for c in consts]
    out_specs = [pl.BlockSpec((tm, tn), lambda j, i: (i, j))]
    out_shape = [jax.ShapeDtypeStruct((m, n), F32)]
    if prologue is not None:
        out_specs.append(pl.BlockSpec((tm, k), once))
        out_shape.append(jax.ShapeDtypeStruct((m, k), MXU_DTYPE))
    res = _riding_call(riding, body, name, grid, in_specs, [a, b] + list(consts), out_specs, out_shape, scratch)
    return res[0] if riding is None and prologue is None else res


def _mm_nt(name, a, b, tm, tn):
    m, n = a.shape
    k = b.shape[0]
    steps = n // tn

    def body(a_ref, b_ref, o_ref, acc_ref):
        s = pl.program_id(1)

        @pl.when(s == 0)
        def _():
            acc_ref[...] = jnp.zeros_like(acc_ref)

        acc_ref[...] += _dot(a_ref[...], b_ref[...], _NT)

        @pl.when(s == steps - 1)
        def _():
            o_ref[...] = acc_ref[...]

    return pl.pallas_call(body, name=name, grid=(m // tm, steps),
                          in_specs=[pl.BlockSpec((tm, tn), lambda i, s: (i, s)),
                                    pl.BlockSpec((k, tn), lambda i, s: (0, s))],
                          out_specs=pl.BlockSpec((tm, k), lambda i, s: (i, 0)),
                          out_shape=jax.ShapeDtypeStruct((m, k), F32),
                          scratch_shapes=[pltpu.VMEM((tm, k), F32)],
                          compiler_params=_params("arbitrary", "arbitrary"))(a, b)


def _mm_nt_then(name, a, b, tm, tn, fn, rows, consts, outs, accs=(), alias=None, riding=None):
    m, n = a.shape
    k = b.shape[-2]
    steps = n // tn
    n_r, n_c, n_o, n_a = len(rows), len(consts), len(outs), len(accs)

    def body(*refs):
        a_ref, b_ref = refs[:2]
        row_refs = refs[2:2 + n_r]
        const_refs = refs[2 + n_r:2 + n_r + n_c]
        i, s = pl.program_id(0), pl.program_id(1)
        n_in = 2 + n_r + n_c + (1 if alias is not None else 0)
        _, outs_, (mm_ref,) = _ride(riding, refs, n_in, n_o + n_a, 1, (i == 0) & (s == 0),
                                    (i == m // tm - 1) & (s == steps - 1))
        out_refs, acc_refs = outs_[:n_o], outs_[n_o:]
        part = _dot(a_ref[...], b_ref[...], _NT)
        if steps > 1:
            @pl.when(s == 0)
            def _():
                mm_ref[...] = jnp.zeros_like(mm_ref)
            mm_ref[...] += part

        @pl.when(s == steps - 1)
        def _():
            res = fn(mm_ref[...] if steps > 1 else part, *[r[...] for r in row_refs], *[r[...] for r in const_refs])
            for r, v in zip(out_refs, res[:n_o]):
                r[...] = v.astype(r.dtype)
            if n_a:
                @pl.when(i == 0)
                def _():
                    for r in acc_refs:
                        r[...] = jnp.zeros_like(r)
                for r, v in zip(acc_refs, res[n_o:]):
                    r[...] += v

    b_spec = (pl.BlockSpec((k, tn), lambda i, s: (0, s)) if b.ndim == 2
              else pl.BlockSpec((None, k, tn), lambda i, s: (s, 0, 0)))
    in_specs = [pl.BlockSpec((tm, tn), lambda i, s: (i, s)), b_spec]
    in_specs += [pl.BlockSpec((tm, w), functools.partial(lambda i, s, cb: (i, cb), cb=cb)) for (_, w, cb) in rows]
    in_specs += [pl.BlockSpec(c.shape, lambda i, s: (0, 0)) for c in consts]
    args = [a, b] + [r[0] for r in rows] + list(consts)
    out_shape, out_specs = [], []
    for o in outs:
        w, dt = o[0], o[1]
        cb, total = (o[2], o[3]) if len(o) == 4 else (0, w)
        out_shape.append(jax.ShapeDtypeStruct((m, total), dt))
        out_specs.append(pl.BlockSpec((tm, w), functools.partial(lambda i, s, cb: (i, cb), cb=cb)))
    io_alias = {}
    if alias is not None:
        in_specs.append(pl.BlockSpec(memory_space=pl.ANY))
        args.append(alias[0])
        io_alias = {len(args) - 1: alias[1]}
    for (r, w) in accs:
        out_shape.append(jax.ShapeDtypeStruct((r, w), F32))
        out_specs.append(pl.BlockSpec((r, w), lambda i, s: (0, 0)))
    return _riding_call(riding, body, name, (m // tm, steps), in_specs, args, out_specs, out_shape,
                        [pltpu.VMEM((tm, k), F32)], io_alias)


def _mm_tn(name, a, b, tk, tn, col_shards=False, riding=None):
    t, k = a.shape
    n = b.shape[1]
    steps = t // tk

    def body(*refs):
        j, s = pl.program_id(0), pl.program_id(1)
        (a_ref, b_ref), (o_ref,), (acc_ref,) = _ride(riding, refs, 2, 1, 1, (j == 0) & (s == 0),
                                                     (j == n // tn - 1) & (s == steps - 1))

        @pl.when(s == 0)
        def _():
            acc_ref[...] = jnp.zeros_like(acc_ref)

        acc_ref[...] += _dot(a_ref[...], b_ref[...], _TN)

        @pl.when(s == steps - 1)
        def _():
            o_ref[...] = acc_ref[...]

    if col_shards:
        out_spec = pl.BlockSpec((None, k, tn), lambda j, s: (j, 0, 0))
        out_shape = jax.ShapeDtypeStruct((n // tn, k, tn), F32)
    else:
        out_spec = pl.BlockSpec((k, tn), lambda j, s: (0, j))
        out_shape = jax.ShapeDtypeStruct((k, n), F32)
    res = _riding_call(riding, body, name, (n // tn, steps),
                       [pl.BlockSpec((tk, k), lambda j, s: (s, 0)), pl.BlockSpec((tk, tn), lambda j, s: (s, j))],
                       [a, b], [out_spec], [out_shape], [pltpu.VMEM((k, tn), F32)])
    return res[0] if riding is None else res


def _hg_chunk_terms(q, f, lb):
    sig = _sig(f)
    fv = lb + (1.0 - lb) * sig
    kk = (1.0 - lb) * (1.0 - sig)
    row = lax.broadcasted_iota(jnp.int32, (HG_CHUNK, HG_CHUNK), 0)
    col = lax.broadcasted_iota(jnp.int32, (HG_CHUNK, HG_CHUNK), 1)
    b = _dot32((row >= col).astype(F32), jnp.log(fv))
    b_mid = b[HG_CHUNK // 2 - 1:HG_CHUNK // 2, :]
    b_last = b[HG_CHUNK - 1:HG_CHUNK, :]
    e_mid = jnp.exp(b - b_mid)
    e_mid_inv = jnp.exp(b_mid - b)
    e_b = jnp.exp(b)
    e_last = jnp.exp(b_last - b)
    return sig, fv, kk, row >= col, row <= col, q * e_mid, kk * e_mid_inv, e_mid, e_mid_inv, e_b, e_last, jnp.exp(b_last)


def _hgrn2_fwd(proj, hg_lb, hg_norm_g, t_len, tb):
    nck = tb // HG_CHUNK

    def body(p_ref, lb_ref, gn_ref, o_ref, act_ref, sp_ref, st_ref):
        @pl.when(pl.program_id(0) == 0)
        def _():
            st_ref[...] = jnp.zeros_like(st_ref)

        for c in range(nck):
            r = pl.ds(c * HG_CHUNK, HG_CHUNK)
            for h in range(HG_HEADS):
                hs = pl.ds(h * HG_DIM, HG_DIM)
                lb = _sig(lb_ref[0:1, hs] - lb_ref[1:2, hs])
                q = p_ref[r, pl.ds(h * HG_DIM, HG_DIM)]
                f = p_ref[r, pl.ds(1024 + h * HG_DIM, HG_DIM)]
                v = p_ref[r, pl.ds(2048 + h * HG_DIM, HG_DIM)]
                _, _, kk, causal, _, a, bm, _, _, e_b, e_last, dc = _hg_chunk_terms(q, f, lb)
                scores = jnp.where(causal, _dot(a, bm, _NT), 0.0)
                st = st_ref[h]
                o = _dot(scores, v) + _dot(q * e_b, st, _NT)
                sp_ref[h, c] = st
                st_ref[h] = dc * st + _dot(v, kk * e_last, _TN)
                o_ref[r, hs] = o

        for h in range(HG_HEADS):
            hs = pl.ds(h * HG_DIM, HG_DIM)
            o = o_ref[:, hs]
            rr = lax.rsqrt(jnp.mean(o * o, axis=-1, keepdims=True) + NORM_EPS)
            g = p_ref[:, pl.ds(3072 + h * HG_DIM, HG_DIM)]
            act_ref[:, hs] = (o * rr * gn_ref[:, hs] * (g * _sig(g))).astype(act_ref.dtype)

    nb = t_len // tb
    return pl.pallas_call(
        body, name="hgrn2_fwd", grid=(nb,),
        in_specs=[pl.BlockSpec((tb, 4096), lambda i: (i, 0)),
                  pl.BlockSpec((2, 1024), lambda i: (0, 0)),
                  pl.BlockSpec((1, 1024), lambda i: (0, 0))],
        out_specs=[pl.BlockSpec((tb, 1024), lambda i: (i, 0)),
                   pl.BlockSpec((tb, 1024), lambda i: (i, 0)),
                   pl.BlockSpec((HG_HEADS, nck, HG_DIM, HG_DIM), lambda i: (0, i, 0, 0))],
        out_shape=[jax.ShapeDtypeStruct((t_len, 1024), F32),
                   jax.ShapeDtypeStruct((t_len, 1024), MXU_DTYPE),
                   jax.ShapeDtypeStruct((HG_HEADS, t_len // HG_CHUNK, HG_DIM, HG_DIM), F32)],
        scratch_shapes=[pltpu.VMEM((HG_HEADS, HG_DIM, HG_DIM), F32)],
        compiler_params=_params("arbitrary"))(proj, hg_lb, hg_norm_g)


def _hgrn2_bwd(proj, d_o, s_prev, hg_lb, dproj, t_len, tb):
    nck = tb // HG_CHUNK
    nb = t_len // tb

    def body(p_ref, do_ref, sp_ref, lb_ref, _, dp_ref, dlb_ref, ds_ref, acc_ref):
        @pl.when(pl.program_id(0) == 0)
        def _():
            ds_ref[...] = jnp.zeros_like(ds_ref)
            acc_ref[...] = jnp.zeros_like(acc_ref)

        for c in reversed(range(nck)):
            r = pl.ds(c * HG_CHUNK, HG_CHUNK)
            for h in range(HG_HEADS):
                hs = pl.ds(h * HG_DIM, HG_DIM)
                lb = _sig(lb_ref[0:1, hs] - lb_ref[1:2, hs])
                q = p_ref[r, pl.ds(h * HG_DIM, HG_DIM)]
                f = p_ref[r, pl.ds(1024 + h * HG_DIM, HG_DIM)]
                v = p_ref[r, pl.ds(2048 + h * HG_DIM, HG_DIM)]
                do = do_ref[r, hs]
                sig, fv, kk, causal, anti, a, bm, e_mid, e_mid_inv, e_b, e_last, dc = _hg_chunk_terms(q, f, lb)
                qd = q * e_b
                kd = kk * e_last
                st = sp_ref[h, c]
                dst = ds_ref[h]
                scores = jnp.where(causal, _dot(a, bm, _NT), 0.0)
                dscores = jnp.where(causal, _dot(do, v, _NT), 0.0)
                dv = _dot(scores, do, _TN) + _dot(kd, dst, _NT)
                da = _dot(dscores, bm)
                dbm = _dot(dscores, a, _TN)
                dqd = _dot(do, st)
                dkd = _dot(v, dst)
                ddc = jnp.sum(dst * st, axis=0, keepdims=True)
                ds_ref[h] = _dot(do, qd, _TN) + dc * dst
                dq = da * e_mid + dqd * e_b
                dk = dbm * e_mid_inv + dkd * e_last
                db = da * a - dbm * bm + dqd * qd - dkd * kd
                extra = jnp.sum(dkd * kd, axis=0, keepdims=True) + ddc * dc
                dlogf = _dot32(anti.astype(F32), db) + extra
                dfv_k = dlogf / fv - dk
                dp_ref[r, pl.ds(h * HG_DIM, HG_DIM)] = dq
                dp_ref[r, pl.ds(1024 + h * HG_DIM, HG_DIM)] = dfv_k * (1.0 - lb) * sig * (1.0 - sig)
                dp_ref[r, pl.ds(2048 + h * HG_DIM, HG_DIM)] = dv
                acc_ref[:, hs] += jnp.sum(dfv_k * (1.0 - sig), axis=0, keepdims=True)

        @pl.when(pl.program_id(0) == nb - 1)
        def _():
            lb_all = _sig(lb_ref[0:1, :] - lb_ref[1:2, :])
            g0 = acc_ref[...] * lb_all * (1.0 - lb_all)
            dlb_ref[0:1, :] = g0
            dlb_ref[1:2, :] = -g0

    return pl.pallas_call(
        body, name="hgrn2_bwd", grid=(nb,),
        in_specs=[pl.BlockSpec((tb, 3072), lambda i: (nb - 1 - i, 0)),
                  pl.BlockSpec((tb, 1024), lambda i: (nb - 1 - i, 0)),
                  pl.BlockSpec((HG_HEADS, nck, HG_DIM, HG_DIM), lambda i: (0, nb - 1 - i, 0, 0)),
                  pl.BlockSpec((2, 1024), lambda i: (0, 0)),
                  pl.BlockSpec(memory_space=pl.ANY)],
        out_specs=[pl.BlockSpec((tb, 3072), lambda i: (nb - 1 - i, 0)),
                   pl.BlockSpec((2, 1024), lambda i: (0, 0))],
        out_shape=[jax.ShapeDtypeStruct((t_len, IN_COLS), F32), jax.ShapeDtypeStruct((2, 1024), F32)],
        scratch_shapes=[pltpu.VMEM((HG_HEADS, HG_DIM, HG_DIM), F32), pltpu.VMEM((1, 1024), F32)],
        input_output_aliases={4: 0},
        compiler_params=_params("arbitrary"))(proj, d_o, s_prev, hg_lb, dproj)


def _dot01(m01, x):
    m = m01.astype(MXU_DTYPE)
    hi = x.astype(MXU_DTYPE)
    r1 = x - hi.astype(F32)
    mid = r1.astype(MXU_DTYPE)
    lo = (r1 - mid.astype(F32)).astype(MXU_DTYPE)
    dot = lambda v: jnp.dot(m, v, preferred_element_type=F32)
    return dot(hi) + dot(mid) + dot(lo)


def _chunk_rows(x, offset, nck):
    return jnp.concatenate([jnp.broadcast_to(x[c * HG_CHUNK + offset:c * HG_CHUNK + offset + 1, :],
                                             (HG_CHUNK, x.shape[1])) for c in range(nck)], axis=0)


def _hg_block_terms(q, f, lb, tb):
    nck = tb // HG_CHUNK
    sig = _sig(f)
    fv = lb + (1.0 - lb) * sig
    kk = (1.0 - lb) * (1.0 - sig)
    row = lax.broadcasted_iota(jnp.int32, (tb, tb), 0)
    col = lax.broadcasted_iota(jnp.int32, (tb, tb), 1)
    same = jnp.right_shift(row, 6) == jnp.right_shift(col, 6)
    causal, anti = same & (row >= col), same & (row <= col)
    b = _dot01(causal, jnp.log(fv))
    b_mid, b_last = _chunk_rows(b, HG_CHUNK // 2 - 1, nck), _chunk_rows(b, HG_CHUNK - 1, nck)
    e_mid, e_mid_inv = jnp.exp(b - b_mid), jnp.exp(b_mid - b)
    e_b, e_last = jnp.exp(b), jnp.exp(b_last - b)
    dcs = [jnp.exp(b[c * HG_CHUNK + HG_CHUNK - 1:(c + 1) * HG_CHUNK, :]) for c in range(nck)]
    return sig, fv, kk, causal, anti, e_mid, e_mid_inv, e_b, e_last, dcs


def _hgrn2_fwd2(proj, hg_lb, hg_norm_g, t_len, tb):
    nck = tb // HG_CHUNK

    def body(p_ref, lb_ref, gn_ref, o_ref, act_ref, sp_ref, st_ref, a_s, bm_s, qd_s, kd_s, v_s):
        @pl.when(pl.program_id(0) == 0)
        def _():
            st_ref[...] = jnp.zeros_like(st_ref)

        lb = _sig(lb_ref[0:1, :] - lb_ref[1:2, :])
        q = p_ref[:, pl.ds(0, 1024)]
        _, _, kk, causal, _, e_mid, e_mid_inv, e_b, e_last, dcs = _hg_block_terms(q, p_ref[:, pl.ds(1024, 1024)],
                                                                                   lb, tb)
        a_s[...] = _mx(q * e_mid)
        bm_s[...] = _mx(kk * e_mid_inv)
        qd_s[...] = _mx(q * e_b)
        kd_s[...] = _mx(kk * e_last)
        v_s[...] = _mx(p_ref[:, pl.ds(2048, 1024)])
        for h in range(HG_HEADS):
            hs = pl.ds(h * HG_DIM, HG_DIM)
            scores = jnp.where(causal, _dot(a_s[:, hs], bm_s[:, hs], _NT), 0.0)
            o_ref[:, hs] = _dot(scores, v_s[:, hs])
        for h in range(HG_HEADS):
            hs = pl.ds(h * HG_DIM, HG_DIM)
            incs = [_dot(v_s[pl.ds(c * HG_CHUNK, HG_CHUNK), hs], kd_s[pl.ds(c * HG_CHUNK, HG_CHUNK), hs], _TN)
                    for c in range(nck)]
            st = st_ref[h]
            for c in range(nck):
                sp_ref[h, c] = st
                st = dcs[c][:, h * HG_DIM:(h + 1) * HG_DIM] * st + incs[c]
            st_ref[h] = st
        for h in range(HG_HEADS):
            hs = pl.ds(h * HG_DIM, HG_DIM)
            for c in range(nck):
                r = pl.ds(c * HG_CHUNK, HG_CHUNK)
                o_ref[r, hs] += _dot(qd_s[r, hs], sp_ref[h, c], _NT)
        for h in range(HG_HEADS):
            hs = pl.ds(h * HG_DIM, HG_DIM)
            o = o_ref[:, hs]
            rr = lax.rsqrt(jnp.mean(o * o, axis=-1, keepdims=True) + NORM_EPS)
            g = p_ref[:, pl.ds(3072 + h * HG_DIM, HG_DIM)]
            act_ref[:, hs] = (o * rr * gn_ref[:, hs] * (g * _sig(g))).astype(act_ref.dtype)

    nb = t_len // tb
    return pl.pallas_call(
        body, name="hgrn2_fwd", grid=(nb,),
        in_specs=[pl.BlockSpec((tb, 4096), lambda i: (i, 0)),
                  pl.BlockSpec((2, 1024), lambda i: (0, 0)),
                  pl.BlockSpec((1, 1024), lambda i: (0, 0))],
        out_specs=[pl.BlockSpec((tb, 1024), lambda i: (i, 0)),
                   pl.BlockSpec((tb, 1024), lambda i: (i, 0)),
                   pl.BlockSpec((HG_HEADS, nck, HG_DIM, HG_DIM), lambda i: (0, i, 0, 0))],
        out_shape=[jax.ShapeDtypeStruct((t_len, 1024), F32),
                   jax.ShapeDtypeStruct((t_len, 1024), MXU_DTYPE),
                   jax.ShapeDtypeStruct((HG_HEADS, t_len // HG_CHUNK, HG_DIM, HG_DIM), F32)],
        scratch_shapes=[pltpu.VMEM((HG_HEADS, HG_DIM, HG_DIM), F32)] + [pltpu.VMEM((tb, 1024), MXU_DTYPE)] * 5,
        compiler_params=_params("arbitrary"))(proj, hg_lb, hg_norm_g)


def _hgrn2_bwd2(proj, d_o, s_prev, hg_lb, dproj, t_len, tb, riding=None):
    nck = tb // HG_CHUNK
    nb = t_len // tb

    def body(*refs):
        step = pl.program_id(0)
        ((p_ref, do_ref, sp_ref, lb_ref, _), (dp_ref, dlb_ref),
         (ds_ref, acc_ref, a_s, bm_s, qd_s, kd_s, v_s, do_s, da_s, dbm_s, dqd_s, dkd_s, dv_s, ex_s)) = _ride(
            riding, refs, 5, 2, 14, step == 0, step == nb - 1)

        @pl.when(pl.program_id(0) == 0)
        def _():
            ds_ref[...] = jnp.zeros_like(ds_ref)
            acc_ref[...] = jnp.zeros_like(acc_ref)

        lb = _sig(lb_ref[0:1, :] - lb_ref[1:2, :])
        q = p_ref[:, pl.ds(0, 1024)]
        sig, fv, kk, causal, anti, e_mid, e_mid_inv, e_b, e_last, dcs = _hg_block_terms(
            q, p_ref[:, pl.ds(1024, 1024)], lb, tb)
        a, bm, qd, kd = q * e_mid, kk * e_mid_inv, q * e_b, kk * e_last
        a_s[...] = _mx(a)
        bm_s[...] = _mx(bm)
        qd_s[...] = _mx(qd)
        kd_s[...] = _mx(kd)
        v_s[...] = _mx(p_ref[:, pl.ds(2048, 1024)])
        do_s[...] = _mx(do_ref[...])
        for h in range(HG_HEADS):
            hs = pl.ds(h * HG_DIM, HG_DIM)
            scores = jnp.where(causal, _dot(a_s[:, hs], bm_s[:, hs], _NT), 0.0)
            dscores = _mx(jnp.where(causal, _dot(do_s[:, hs], v_s[:, hs], _NT), 0.0))
            dv_s[:, hs] = _dot(scores, do_s[:, hs], _TN)
            da_s[:, hs] = _dot(dscores, bm_s[:, hs])
            dbm_s[:, hs] = _dot(dscores, a_s[:, hs], _TN)
        for h in range(HG_HEADS):
            hs = pl.ds(h * HG_DIM, HG_DIM)
            ups = [_dot(do_s[pl.ds(c * HG_CHUNK, HG_CHUNK), hs], qd_s[pl.ds(c * HG_CHUNK, HG_CHUNK), hs], _TN)
                   for c in range(nck)]
            dst = ds_ref[h]
            for c in reversed(range(nck)):
                r = pl.ds(c * HG_CHUNK, HG_CHUNK)
                st = sp_ref[h, c]
                dc = dcs[c][:, h * HG_DIM:(h + 1) * HG_DIM]
                dv_s[r, hs] += _dot(kd_s[r, hs], dst, _NT)
                dqd_s[r, hs] = _dot(do_s[r, hs], st)
                dkd_s[r, hs] = _dot(v_s[r, hs], dst)
                ex_s[c:c + 1, hs] = jnp.sum(dst * st, axis=0, keepdims=True) * dc
                dst = ups[c] + dc * dst
            ds_ref[h] = dst
        da, dbm, dqd, dkd = da_s[...], dbm_s[...], dqd_s[...], dkd_s[...]
        dq = da * e_mid + dqd * e_b
        dk = dbm * e_mid_inv + dkd * e_last
        db = da * a - dbm * bm + dqd * qd - dkd * kd
        dkk = dkd * kd
        extra = jnp.concatenate(
            [jnp.broadcast_to(jnp.sum(dkk[c * HG_CHUNK:(c + 1) * HG_CHUNK], axis=0, keepdims=True)
                              + ex_s[c:c + 1, :], (HG_CHUNK, 1024)) for c in range(nck)], axis=0)
        dlogf = _dot01(anti, db) + extra
        dfv_k = dlogf / fv - dk
        dp_ref[:, pl.ds(0, 1024)] = dq.astype(dp_ref.dtype)
        dp_ref[:, pl.ds(1024, 1024)] = (dfv_k * (1.0 - lb) * sig * (1.0 - sig)).astype(dp_ref.dtype)
        dp_ref[:, pl.ds(2048, 1024)] = dv_s[...].astype(dp_ref.dtype)
        acc_ref[...] += jnp.sum(dfv_k * (1.0 - sig), axis=0, keepdims=True)

        @pl.when(pl.program_id(0) == nb - 1)
        def _():
            g0 = acc_ref[...] * lb * (1.0 - lb)
            dlb_ref[0:1, :] = g0
            dlb_ref[1:2, :] = -g0

    return _riding_call(
        riding, body, "hgrn2_bwd", (nb,),
        [pl.BlockSpec((tb, 3072), lambda i: (nb - 1 - i, 0)),
         pl.BlockSpec((tb, 1024), lambda i: (nb - 1 - i, 0)),
         pl.BlockSpec((HG_HEADS, nck, HG_DIM, HG_DIM), lambda i: (0, nb - 1 - i, 0, 0)),
         pl.BlockSpec((2, 1024), lambda i: (0, 0)),
         pl.BlockSpec(memory_space=pl.ANY)],
        [proj, d_o, s_prev, hg_lb, dproj],
        [pl.BlockSpec((tb, 3072), lambda i: (nb - 1 - i, 0)), pl.BlockSpec((2, 1024), lambda i: (0, 0))],
        [jax.ShapeDtypeStruct((t_len, IN_COLS), dproj.dtype), jax.ShapeDtypeStruct((2, 1024), F32)],
        [pltpu.VMEM((HG_HEADS, HG_DIM, HG_DIM), F32), pltpu.VMEM((1, 1024), F32)]
        + [pltpu.VMEM((tb, 1024), MXU_DTYPE)] * 6 + [pltpu.VMEM((tb, 1024), F32)] * 5
        + [pltpu.VMEM((SUBLANES, 1024), F32)], {4: 0})


def _s5_prep(a_re, a_im, log_dt, b_re_t, b_im_t):
    def body(ar_ref, ai_ref, ldt_ref, br_ref, bi_ref, lam_ref, pr_ref, pi_ref, bbr_ref, bbi_ref):
        ar, ai = ar_ref[...], ai_ref[...]
        dt = jnp.exp(ldt_ref[...])
        mag = jnp.exp(ar * dt)
        lr, li = mag * jnp.cos(ai * dt), mag * jnp.sin(ai * dt)
        den = ar * ar + ai * ai
        nr = lr - 1.0
        sr = (nr * ar + li * ai) / den
        si = (li * ar - nr * ai) / den
        lam_ref[0:1, :] = lr
        lam_ref[1:2, :] = li
        cr, ci = lr, li
        for i in range(SUBLANES):
            pr_ref[i:i + 1, :] = cr
            pi_ref[i:i + 1, :] = ci
            cr, ci = cr * lr - ci * li, cr * li + ci * lr
        br, bi = br_ref[...], bi_ref[...]
        bbr_ref[...] = sr * br - si * bi
        bbi_ref[...] = sr * bi + si * br

    whole = pl.BlockSpec(memory_space=pltpu.VMEM)
    return pl.pallas_call(
        body, name="s5_prep", in_specs=[whole] * 5, out_specs=[whole] * 5,
        out_shape=[jax.ShapeDtypeStruct((2, S5_LANES), F32), jax.ShapeDtypeStruct((SUBLANES, S5_LANES), F32),
                   jax.ShapeDtypeStruct((SUBLANES, S5_LANES), F32), jax.ShapeDtypeStruct((S5_GROUP, S5_LANES), F32),
                   jax.ShapeDtypeStruct((S5_GROUP, S5_LANES), F32)])(a_re, a_im, log_dt, b_re_t, b_im_t)


def _s5_prep_bwd(a_re, a_im, log_dt, b_re_t, b_im_t, dlam, dbbr, dbbi):
    def body(ar_ref, ai_ref, ldt_ref, br_ref, bi_ref, dlam_ref, dbbr_ref, dbbi_ref,
             dar_ref, dai_ref, dldt_ref, dbr_ref, dbi_ref):
        ar, ai = ar_ref[...], ai_ref[...]
        dt = jnp.exp(ldt_ref[...])
        mag = jnp.exp(ar * dt)
        cs, sn = jnp.cos(ai * dt), jnp.sin(ai * dt)
        lr, li = mag * cs, mag * sn
        den = ar * ar + ai * ai
        nr = lr - 1.0
        sr = (nr * ar + li * ai) / den
        si = (li * ar - nr * ai) / den
        br, bi = br_ref[...], bi_ref[...]
        gbr, gbi = dbbr_ref[...], dbbi_ref[...]
        dbr_ref[...] = sr * gbr + si * gbi
        dbi_ref[...] = sr * gbi - si * gbr
        dsr = jnp.sum(gbr * br + gbi * bi, axis=0, keepdims=True)
        dsi = jnp.sum(gbi * br - gbr * bi, axis=0, keepdims=True)
        dnr = (dsr * ar - dsi * ai) / den
        dli = dlam_ref[1:2, :] + (dsr * ai + dsi * ar) / den
        dlr = dlam_ref[0:1, :] + dnr
        dden = -(dsr * sr + dsi * si) / den
        dar = (dsr * nr + dsi * li) / den + dden * 2.0 * ar
        dai = (dsr * li - dsi * nr) / den + dden * 2.0 * ai
        dmag = dlr * cs + dli * sn
        dth = mag * (dli * cs - dlr * sn)
        dar_ref[...] = dar + dmag * mag * dt
        dai_ref[...] = dai + dth * dt
        ddt = (dmag * mag * ar + dth * ai) * dt
        lane = lax.broadcasted_iota(jnp.int32, (S5_LANES, 128), 0) // S5_STATE
        grp = lax.broadcasted_iota(jnp.int32, (S5_LANES, 128), 1)
        dldt_ref[...] = _dot32(jnp.broadcast_to(ddt, (SUBLANES, S5_LANES)), (lane == grp).astype(F32))

    whole = pl.BlockSpec(memory_space=pltpu.VMEM)
    return pl.pallas_call(
        body, name="s5_prep_bwd", in_specs=[whole] * 8, out_specs=[whole] * 5,
        out_shape=[jax.ShapeDtypeStruct((1, S5_LANES), F32), jax.ShapeDtypeStruct((1, S5_LANES), F32),
                   jax.ShapeDtypeStruct((SUBLANES, 128), F32), jax.ShapeDtypeStruct((S5_GROUP, S5_LANES), F32),
                   jax.ShapeDtypeStruct((S5_GROUP, S5_LANES), F32)])(a_re, a_im, log_dt, b_re_t, b_im_t, dlam, dbbr,
                                                                      dbbi)


S5_LANE_CHUNK = 512


def _shift_rows(x, s, rowid):
    if s > 0:
        return jnp.where(rowid >= s, pltpu.roll(x, s, 0), 0.0)
    return jnp.where(rowid < SUBLANES + s, pltpu.roll(x, SUBLANES + s, 0), 0.0)


def _scan8(xr, xi, pr, pi, sign, rowid):
    for s, row in ((1, 0), (2, 1), (4, 3)):
        lr, li = pr[row:row + 1, :], pi[row:row + 1, :]
        sr, si = _shift_rows(xr, sign * s, rowid), _shift_rows(xi, sign * s, rowid)
        xr, xi = xr + lr * sr - li * si, xi + lr * si + li * sr
    return xr, xi


def _s5_fwd(proj, pw_re, pw_im, bbr_bd, bbi_bd, crt_bd, cit_bd, d_row, t_len, tb):
    ngrp = tb // SUBLANES

    def body(u_ref, pr_ref, pi_ref, bbr_ref, bbi_ref, crt_ref, cit_ref, d_ref,
             hr_ref, hi_ref, ypre_ref, ys_ref, cr_ref, ci_ref):
        @pl.when(pl.program_id(0) == 0)
        def _():
            cr_ref[...] = jnp.zeros_like(cr_ref)
            ci_ref[...] = jnp.zeros_like(ci_ref)

        u = u_ref[...]
        hr_ref[...] = _dot(u, bbr_ref[...])
        hi_ref[...] = _dot(u, bbi_ref[...])
        rowid = lax.broadcasted_iota(jnp.int32, (SUBLANES, S5_LANE_CHUNK), 0)
        for lc in range(S5_LANES // S5_LANE_CHUNK):
            ls = pl.ds(lc * S5_LANE_CHUNK, S5_LANE_CHUNK)
            pr, pi = pr_ref[:, ls], pi_ref[:, ls]

            def group(g, carry, ls=ls, pr=pr, pi=pi):
                cr, ci = carry
                r = pl.ds(pl.multiple_of(g * SUBLANES, SUBLANES), SUBLANES)
                xr, xi = _scan8(hr_ref[r, ls], hi_ref[r, ls], pr, pi, 1, rowid)
                xr, xi = xr + pr * cr - pi * ci, xi + pr * ci + pi * cr
                hr_ref[r, ls] = xr
                hi_ref[r, ls] = xi
                return xr[SUBLANES - 1:SUBLANES, :], xi[SUBLANES - 1:SUBLANES, :]

            cr, ci = lax.fori_loop(0, ngrp, group, (cr_ref[:, ls], ci_ref[:, ls]))
            cr_ref[:, ls] = cr
            ci_ref[:, ls] = ci
        y = _dot(hr_ref[...], crt_ref[...]) - _dot(hi_ref[...], cit_ref[...]) + d_ref[...] * u
        ypre_ref[...] = y
        ys_ref[...] = jax.nn.gelu(y, approximate=True).astype(ys_ref.dtype)

    whole = pl.BlockSpec(memory_space=pltpu.VMEM)
    return pl.pallas_call(
        body, name="s5_fwd", grid=(t_len // tb,),
        in_specs=[pl.BlockSpec((tb, S5_WIDTH), lambda i: (i, 4096 // S5_WIDTH))] + [whole] * 7,
        out_specs=[pl.BlockSpec((tb, S5_LANES), lambda i: (i, 0)), pl.BlockSpec((tb, S5_LANES), lambda i: (i, 0)),
                   pl.BlockSpec((tb, S5_WIDTH), lambda i: (i, 0)), pl.BlockSpec((tb, S5_WIDTH), lambda i: (i, 0))],
        out_shape=[jax.ShapeDtypeStruct((t_len, S5_LANES), F32), jax.ShapeDtypeStruct((t_len, S5_LANES), F32),
                   jax.ShapeDtypeStruct((t_len, S5_WIDTH), F32), jax.ShapeDtypeStruct((t_len, S5_WIDTH), MXU_DTYPE)],
        scratch_shapes=[pltpu.VMEM((1, S5_LANES), F32), pltpu.VMEM((1, S5_LANES), F32)],
        compiler_params=_params("arbitrary"))(proj, pw_re, pw_im, bbr_bd, bbi_bd, crt_bd, cit_bd, d_row)


def _dgelu(x):
    c, a = 0.7978845608028654, 0.044715
    th = jnp.tanh(c * (x + a * x * x * x))
    return 0.5 * (1.0 + th) + 0.5 * x * (1.0 - th * th) * c * (1.0 + 3.0 * a * x * x)


def _s5_bwd(dgelu, y_pre, proj, h_re, h_im, pwr_re, pwr_im, bbr_bd, bbi_bd, cr_bd, ci_bd, d_row, dproj, t_len, tb):
    ngrp = tb // SUBLANES
    nb = t_len // tb

    def body(dg_ref, yp_ref, u_ref, hr_ref, hi_ref, pr_ref, pi_ref, bbr_ref, bbi_ref, cr_ref, ci_ref, d_ref, _,
             du_ref, dbbr_ref, dbbi_ref, dcr_ref, dci_ref, dd_ref, dlam_ref,
             gr_ref, gi_ref, car_ref, cai_ref, abr_ref, abi_ref, acr_ref, aci_ref, ad_ref, alr_ref, ali_ref, sem):
        @pl.when(pl.program_id(0) == 0)
        def _():
            for ref in (car_ref, cai_ref, abr_ref, abi_ref, acr_ref, aci_ref, ad_ref, alr_ref, ali_ref):
                ref[...] = jnp.zeros_like(ref)

        u = u_ref[...]
        dy = dg_ref[...] * _dgelu(yp_ref[...])
        gr_ref[...] = _dot(dy, cr_ref[...])
        gi_ref[...] = -_dot(dy, ci_ref[...])
        rowid = lax.broadcasted_iota(jnp.int32, (SUBLANES, S5_LANE_CHUNK), 0)
        for lc in range(S5_LANES // S5_LANE_CHUNK):
            ls = pl.ds(lc * S5_LANE_CHUNK, S5_LANE_CHUNK)
            pr, pi = pr_ref[:, ls], pi_ref[:, ls]
            fwd_rows_r = jnp.concatenate([pr[7:8], pr[6:7], pr[6:7], pr[4:5]], axis=0)
            fwd_rows_i = jnp.concatenate([pi[7:8], pi[6:7], pi[6:7], pi[4:5]], axis=0)

            def group(j, carry, ls=ls, pr=pr, pi=pi, fr=fwd_rows_r, fi=fwd_rows_i):
                cr, ci, slr, sli = carry
                g = ngrp - 1 - j
                r = pl.ds(pl.multiple_of(g * SUBLANES, SUBLANES), SUBLANES)
                xr, xi = _scan8(gr_ref[r, ls], gi_ref[r, ls], fr, fi, -1, rowid)
                xr, xi = xr + pr * cr - pi * ci, xi + pr * ci + pi * cr
                gr_ref[r, ls] = xr
                gi_ref[r, ls] = xi
                nr = jnp.where(rowid == SUBLANES - 1, cr, pltpu.roll(xr, SUBLANES - 1, 0))
                ni = jnp.where(rowid == SUBLANES - 1, ci, pltpu.roll(xi, SUBLANES - 1, 0))
                hr, hi = hr_ref[r, ls], hi_ref[r, ls]
                slr = slr + nr * hr + ni * hi
                sli = sli + ni * hr - nr * hi
                return xr[0:1, :], xi[0:1, :], slr, sli

            zero = jnp.zeros((SUBLANES, S5_LANE_CHUNK), F32)
            cr, ci, slr, sli = lax.fori_loop(0, ngrp, group, (car_ref[:, ls], cai_ref[:, ls], zero, zero))
            car_ref[:, ls] = cr
            cai_ref[:, ls] = ci
            alr_ref[:, ls] += jnp.sum(slr, axis=0, keepdims=True)
            ali_ref[:, ls] += jnp.sum(sli, axis=0, keepdims=True)
        gr, gi = gr_ref[...], gi_ref[...]
        du_ref[...] = _dot(gr, bbr_ref[...], _NT) + _dot(gi, bbi_ref[...], _NT) + d_ref[...] * dy
        abr_ref[...] += _dot(u, gr, _TN)
        abi_ref[...] += _dot(u, gi, _TN)
        acr_ref[...] += _dot(hr_ref[...], dy, _TN)
        aci_ref[...] -= _dot(hi_ref[...], dy, _TN)
        ad_ref[...] += jnp.sum(dy * u, axis=0, keepdims=True)

        @pl.when(pl.program_id(0) == nb - 1)
        def _():
            dd_ref[...] = ad_ref[...]
            dlam_ref[0:1, :] = alr_ref[...]
            dlam_ref[1:2, :] = ali_ref[...]
            copies = [pltpu.make_async_copy(s, d, sem.at[k]) for k, (s, d) in enumerate(
                ((abr_ref, dbbr_ref), (abi_ref, dbbi_ref), (acr_ref, dcr_ref), (aci_ref, dci_ref)))]
            for cp in copies:
                cp.start()
            for cp in copies:
                cp.wait()

    whole = pl.BlockSpec(memory_space=pltpu.VMEM)
    hbm = pl.BlockSpec(memory_space=pl.ANY)
    rev = lambda i: (nb - 1 - i, 0)
    return pl.pallas_call(
        body, name="s5_bwd", grid=(nb,),
        in_specs=[pl.BlockSpec((tb, S5_WIDTH), rev), pl.BlockSpec((tb, S5_WIDTH), rev),
                  pl.BlockSpec((tb, S5_WIDTH), lambda i: (nb - 1 - i, 4096 // S5_WIDTH)),
                  pl.BlockSpec((tb, S5_LANES), rev), pl.BlockSpec((tb, S5_LANES), rev)] + [whole] * 7 + [hbm],
        out_specs=[pl.BlockSpec((tb, S5_WIDTH), lambda i: (nb - 1 - i, 4096 // S5_WIDTH)), hbm, hbm, hbm, hbm,
                   pl.BlockSpec((1, S5_WIDTH), lambda i: (0, 0)), pl.BlockSpec((2, S5_LANES), lambda i: (0, 0))],
        out_shape=[jax.ShapeDtypeStruct((t_len, IN_COLS), F32),
                   jax.ShapeDtypeStruct((S5_WIDTH, S5_LANES), F32), jax.ShapeDtypeStruct((S5_WIDTH, S5_LANES), F32),
                   jax.ShapeDtypeStruct((S5_LANES, S5_WIDTH), F32), jax.ShapeDtypeStruct((S5_LANES, S5_WIDTH), F32),
                   jax.ShapeDtypeStruct((1, S5_WIDTH), F32), jax.ShapeDtypeStruct((2, S5_LANES), F32)],
        scratch_shapes=[pltpu.VMEM((tb, S5_LANES), F32), pltpu.VMEM((tb, S5_LANES), F32),
                        pltpu.VMEM((1, S5_LANES), F32), pltpu.VMEM((1, S5_LANES), F32),
                        pltpu.VMEM((S5_WIDTH, S5_LANES), F32), pltpu.VMEM((S5_WIDTH, S5_LANES), F32),
                        pltpu.VMEM((S5_LANES, S5_WIDTH), F32), pltpu.VMEM((S5_LANES, S5_WIDTH), F32),
                        pltpu.VMEM((1, S5_WIDTH), F32), pltpu.VMEM((1, S5_LANES), F32),
                        pltpu.VMEM((1, S5_LANES), F32), pltpu.SemaphoreType.DMA((4,))],
        input_output_aliases={12: 0},
        compiler_params=_params("arbitrary"))(dgelu, y_pre, proj, h_re, h_im, pwr_re, pwr_im, bbr_bd, bbi_bd, cr_bd,
                                              ci_bd, d_row, dproj)


S5_BLOCKS = 4
S5_BW = S5_WIDTH // S5_BLOCKS
S5_BL = S5_LANES // S5_BLOCKS
S5_LANE_BLOCKS = S5_LANES // 128
S5_SCAN_BLOCKS = 4


def _s5_powers(a_re, a_im, log_dt, b_re_t, b_im_t, seg):
    def body(ar_ref, ai_ref, ldt_ref, br_ref, bi_ref,
             rows_f, pfr_ref, pfi_ref, rows_r, prr_ref, pri_ref, bbr_ref, bbi_ref):
        ar, ai = ar_ref[...], ai_ref[...]
        dt = jnp.exp(ldt_ref[...])
        mag = jnp.exp(ar * dt)
        lr, li = mag * jnp.cos(ai * dt), mag * jnp.sin(ai * dt)
        den = ar * ar + ai * ai
        nr = lr - 1.0
        sr = (nr * ar + li * ai) / den
        si = (li * ar - nr * ai) / den
        wide = (SUBLANES, S5_LANES)
        cr, ci = lr, li
        for i in range(seg):
            pfr_ref[i] = jnp.broadcast_to(cr, wide)
            pfi_ref[i] = jnp.broadcast_to(ci, wide)
            prr_ref[seg - 1 - i] = jnp.broadcast_to(cr, wide)
            pri_ref[seg - 1 - i] = jnp.broadcast_to(-ci, wide)
            if i == seg - 1:
                for rows, sign in ((rows_f, 1.0), (rows_r, -1.0)):
                    rows[0:1, :] = lr
                    rows[1:2, :] = sign * li
                    rows[2:3, :] = cr
                    rows[3:4, :] = sign * ci
            cr, ci = cr * lr - ci * li, cr * li + ci * lr
        br, bi = br_ref[...], bi_ref[...]
        bbr_ref[...] = sr * br - si * bi
        bbi_ref[...] = sr * bi + si * br

    whole = pl.BlockSpec(memory_space=pltpu.VMEM)
    tables = [jax.ShapeDtypeStruct((4, S5_LANES), F32)] + [jax.ShapeDtypeStruct((seg, SUBLANES, S5_LANES), F32)] * 2
    bbar = [jax.ShapeDtypeStruct((S5_GROUP, S5_LANES), F32)] * 2
    res = pl.pallas_call(body, name="s5_prep", in_specs=[whole] * 5, out_specs=[whole] * 8,
                         out_shape=tables + tables + bbar)(a_re, a_im, log_dt, b_re_t, b_im_t)
    return res[0:3], res[3:6], res[6], res[7]


def _scan_tables(pw_re, pw_im, reverse):
    seg = pw_re.shape[0]
    if reverse:
        pw_re, pw_im = pw_re[::-1], -pw_im[::-1]
        one, full = seg - 1, 0
    else:
        one, full = 0, seg - 1
    rows = jnp.stack([pw_re[one], pw_im[one], pw_re[full], pw_im[full]])
    wide = lambda t: jnp.broadcast_to(t[:, None, :], (seg, SUBLANES, S5_LANES))
    return rows, wide(pw_re), wide(pw_im)


def _lanes(j):
    return pl.ds(j * 128, 128)


def _segment_scan(xr_ref, xi_ref, lam_ref, car_ref, cai_ref, cn_r, cn_i, blocks, seg, reverse):
    shape = (SUBLANES, 128)
    lrs = [jnp.broadcast_to(lam_ref[0:1, _lanes(j)], shape) for j in blocks]
    lis = [jnp.broadcast_to(lam_ref[1:2, _lanes(j)], shape) for j in blocks]

    def step(k, carry):
        idx = pl.ds(seg - 1 - k if reverse else k, SUBLANES, stride=seg)
        out = []
        for n, j in enumerate(blocks):
            cr, ci = carry[2 * n], carry[2 * n + 1]
            nr = lrs[n] * cr - lis[n] * ci + xr_ref[j, idx, :]
            ni = lrs[n] * ci + lis[n] * cr + xi_ref[j, idx, :]
            xr_ref[j, idx, :] = nr
            xi_ref[j, idx, :] = ni
            out += [nr, ni]
        return tuple(out)

    zero = jnp.zeros(shape, F32)
    fin = lax.fori_loop(0, seg, step, (zero,) * (2 * len(blocks)), unroll=2)
    for n, j in enumerate(blocks):
        ls = _lanes(j)
        fr, fi = fin[2 * n], fin[2 * n + 1]
        sr, si = lam_ref[2:3, ls], lam_ref[3:4, ls]
        pr, pi = car_ref[:, ls], cai_ref[:, ls]
        for s in (reversed(range(SUBLANES)) if reverse else range(SUBLANES)):
            cn_r[s:s + 1, ls] = pr
            cn_i[s:s + 1, ls] = pi
            pr, pi = fr[s:s + 1, :] + sr * pr - si * pi, fi[s:s + 1, :] + sr * pi + si * pr
        car_ref[:, ls] = pr
        cai_ref[:, ls] = pi


def _s5_fwd2(proj, lam_rows, p3_re, p3_im, bbr4, bbi4, crt4, cit4, d_row, t_len, tb):
    seg = tb // SUBLANES

    def body(u_ref, lam_ref, p3r_ref, p3i_ref, bbr_ref, bbi_ref, crt_ref, cit_ref, d_ref,
             hr_ref, hi_ref, ypre_ref, ys_ref, car_ref, cai_ref, cn_r, cn_i):
        @pl.when(pl.program_id(0) == 0)
        def _():
            car_ref[...] = jnp.zeros_like(car_ref)
            cai_ref[...] = jnp.zeros_like(cai_ref)

        u = u_ref[...]
        for i in range(S5_BLOCKS):
            ui = u[:, i * S5_BW:(i + 1) * S5_BW]
            xr, xi = _dot(ui, bbr_ref[i]), _dot(ui, bbi_ref[i])
            for jj in range(S5_BL // 128):
                hr_ref[i * (S5_BL // 128) + jj] = xr[:, jj * 128:(jj + 1) * 128]
                hi_ref[i * (S5_BL // 128) + jj] = xi[:, jj * 128:(jj + 1) * 128]
        for lc in range(S5_LANE_BLOCKS // S5_SCAN_BLOCKS):
            blocks = range(lc * S5_SCAN_BLOCKS, (lc + 1) * S5_SCAN_BLOCKS)
            _segment_scan(hr_ref, hi_ref, lam_ref, car_ref, cai_ref, cn_r, cn_i, blocks, seg, False)
            crs = [cn_r[:, _lanes(j)] for j in blocks]
            cis = [cn_i[:, _lanes(j)] for j in blocks]

            def fix(t, carry, blocks=blocks, crs=crs, cis=cis):
                idx = pl.ds(t, SUBLANES, stride=seg)
                for n, j in enumerate(blocks):
                    pr, pi = p3r_ref[t, :, _lanes(j)], p3i_ref[t, :, _lanes(j)]
                    hr_ref[j, idx, :] += pr * crs[n] - pi * cis[n]
                    hi_ref[j, idx, :] += pr * cis[n] + pi * crs[n]
                return carry

            lax.fori_loop(0, seg, fix, 0, unroll=2)
        for i in range(S5_BLOCKS):
            ws = pl.ds(i * S5_BW, S5_BW)
            js = range(i * (S5_BL // 128), (i + 1) * (S5_BL // 128))
            hr = jnp.concatenate([hr_ref[j] for j in js], axis=1)
            hi = jnp.concatenate([hi_ref[j] for j in js], axis=1)
            y = _dot(hr, crt_ref[i]) - _dot(hi, cit_ref[i]) + d_ref[:, ws] * u[:, i * S5_BW:(i + 1) * S5_BW]
            ypre_ref[:, ws] = y
            ys_ref[:, ws] = jax.nn.gelu(y, approximate=True).astype(ys_ref.dtype)

    whole = pl.BlockSpec(memory_space=pltpu.VMEM)
    h_spec = pl.BlockSpec((S5_LANE_BLOCKS, tb, 128), lambda i: (0, i, 0))
    return pl.pallas_call(
        body, name="s5_fwd", grid=(t_len // tb,),
        in_specs=[pl.BlockSpec((tb, S5_WIDTH), lambda i: (i, 4096 // S5_WIDTH))] + [whole] * 8,
        out_specs=[h_spec, h_spec,
                   pl.BlockSpec((tb, S5_WIDTH), lambda i: (i, 0)), pl.BlockSpec((tb, S5_WIDTH), lambda i: (i, 0))],
        out_shape=[jax.ShapeDtypeStruct((S5_LANE_BLOCKS, t_len, 128), F32),
                   jax.ShapeDtypeStruct((S5_LANE_BLOCKS, t_len, 128), F32),
                   jax.ShapeDtypeStruct((t_len, S5_WIDTH), F32), jax.ShapeDtypeStruct((t_len, S5_WIDTH), MXU_DTYPE)],
        scratch_shapes=[pltpu.VMEM((1, S5_LANES), F32), pltpu.VMEM((1, S5_LANES), F32),
                        pltpu.VMEM((SUBLANES, S5_LANES), F32), pltpu.VMEM((SUBLANES, S5_LANES), F32)],
        compiler_params=_params("arbitrary"))(proj, lam_rows, p3_re, p3_im, bbr4, bbi4, crt4, cit4, d_row)


def _s5_bwd2(dgelu, y_pre, proj, h_re, h_im, lam_rows, p3_re, p3_im, bbr4, bbi4, cr4, ci4, d_row, dproj, t_len, tb):
    seg = tb // SUBLANES
    nb = t_len // tb

    def body(dg_ref, yp_ref, u_ref, hr_ref, hi_ref, lam_ref, p3r_ref, p3i_ref, bbr_ref, bbi_ref, cr_ref, ci_ref,
             d_ref, _, du_ref, dbbr_ref, dbbi_ref, dcr_ref, dci_ref, dd_ref, dlam_ref,
             gr_ref, gi_ref, car_ref, cai_ref, cn_r, cn_i):
        @pl.when(pl.program_id(0) == 0)
        def _():
            for ref in (car_ref, cai_ref, dbbr_ref, dbbi_ref, dcr_ref, dci_ref, dd_ref, dlam_ref):
                ref[...] = jnp.zeros_like(ref)

        u = u_ref[...]
        dy = dg_ref[...] * _dgelu(yp_ref[...])
        nlb = S5_BL // 128
        for i in range(S5_BLOCKS):
            dyi = dy[:, i * S5_BW:(i + 1) * S5_BW]
            xr, xi = _dot(dyi, cr_ref[i]), -_dot(dyi, ci_ref[i])
            for jj in range(nlb):
                gr_ref[i * nlb + jj] = xr[:, jj * 128:(jj + 1) * 128]
                gi_ref[i * nlb + jj] = xi[:, jj * 128:(jj + 1) * 128]
        for lc in range(S5_LANE_BLOCKS // S5_SCAN_BLOCKS):
            blocks = range(lc * S5_SCAN_BLOCKS, (lc + 1) * S5_SCAN_BLOCKS)
            _segment_scan(gr_ref, gi_ref, lam_ref, car_ref, cai_ref, cn_r, cn_i, blocks, seg, True)
            crs = [cn_r[:, _lanes(j)] for j in blocks]
            cis = [cn_i[:, _lanes(j)] for j in blocks]

            def fix(k, carry, blocks=blocks, crs=crs, cis=cis):
                t = seg - 1 - k
                idx = pl.ds(t, SUBLANES, stride=seg)
                out = []
                for n, j in enumerate(blocks):
                    nr, ni, slr, sli = carry[4 * n:4 * n + 4]
                    pr, pi = p3r_ref[t, :, _lanes(j)], p3i_ref[t, :, _lanes(j)]
                    g_r = gr_ref[j, idx, :] + pr * crs[n] - pi * cis[n]
                    g_i = gi_ref[j, idx, :] + pr * cis[n] + pi * crs[n]
                    gr_ref[j, idx, :] = g_r
                    gi_ref[j, idx, :] = g_i
                    hr, hi = hr_ref[j, idx, :], hi_ref[j, idx, :]
                    out += [g_r, g_i, slr + nr * hr + ni * hi, sli + ni * hr - nr * hi]
                return tuple(out)

            zero = jnp.zeros((SUBLANES, 128), F32)
            init = []
            for n in range(len(blocks)):
                init += [crs[n], cis[n], zero, zero]
            fin = lax.fori_loop(0, seg, fix, tuple(init), unroll=2)
            for n, j in enumerate(blocks):
                dlam_ref[0:1, _lanes(j)] += jnp.sum(fin[4 * n + 2], axis=0, keepdims=True)
                dlam_ref[1:2, _lanes(j)] += jnp.sum(fin[4 * n + 3], axis=0, keepdims=True)
        for i in range(S5_BLOCKS):
            ws = pl.ds(i * S5_BW, S5_BW)
            js = range(i * nlb, (i + 1) * nlb)
            ui, dyi = u[:, i * S5_BW:(i + 1) * S5_BW], dy[:, i * S5_BW:(i + 1) * S5_BW]
            gr = jnp.concatenate([gr_ref[j] for j in js], axis=1)
            gi = jnp.concatenate([gi_ref[j] for j in js], axis=1)
            du_ref[:, ws] = _dot(gr, bbr_ref[i], _NT) + _dot(gi, bbi_ref[i], _NT) + d_ref[:, ws] * dyi
            dbbr_ref[i] += _dot(ui, gr, _TN)
            dbbi_ref[i] += _dot(ui, gi, _TN)
            dcr_ref[i] += _dot(jnp.concatenate([hr_ref[j] for j in js], axis=1), dyi, _TN)
            dci_ref[i] -= _dot(jnp.concatenate([hi_ref[j] for j in js], axis=1), dyi, _TN)
        dd_ref[...] += jnp.sum(dy * u, axis=0, keepdims=True)

    whole = pl.BlockSpec(memory_space=pltpu.VMEM)
    rev = lambda i: (nb - 1 - i, 0)
    const3 = lambda i: (0, 0, 0)
    h_spec = pl.BlockSpec((S5_LANE_BLOCKS, tb, 128), lambda i: (0, nb - 1 - i, 0))
    return pl.pallas_call(
        body, name="s5_bwd", grid=(nb,),
        in_specs=[pl.BlockSpec((tb, S5_WIDTH), rev), pl.BlockSpec((tb, S5_WIDTH), rev),
                  pl.BlockSpec((tb, S5_WIDTH), lambda i: (nb - 1 - i, 4096 // S5_WIDTH)),
                  h_spec, h_spec] + [whole] * 8
                 + [pl.BlockSpec(memory_space=pl.ANY)],
        out_specs=[pl.BlockSpec((tb, S5_WIDTH), lambda i: (nb - 1 - i, 4096 // S5_WIDTH)),
                   pl.BlockSpec((S5_BLOCKS, S5_BW, S5_BL), const3), pl.BlockSpec((S5_BLOCKS, S5_BW, S5_BL), const3),
                   pl.BlockSpec((S5_BLOCKS, S5_BL, S5_BW), const3), pl.BlockSpec((S5_BLOCKS, S5_BL, S5_BW), const3),
                   pl.BlockSpec((1, S5_WIDTH), lambda i: (0, 0)), pl.BlockSpec((2, S5_LANES), lambda i: (0, 0))],
        out_shape=[jax.ShapeDtypeStruct((t_len, IN_COLS), F32),
                   jax.ShapeDtypeStruct((S5_BLOCKS, S5_BW, S5_BL), F32),
                   jax.ShapeDtypeStruct((S5_BLOCKS, S5_BW, S5_BL), F32),
                   jax.ShapeDtypeStruct((S5_BLOCKS, S5_BL, S5_BW), F32),
                   jax.ShapeDtypeStruct((S5_BLOCKS, S5_BL, S5_BW), F32),
                   jax.ShapeDtypeStruct((1, S5_WIDTH), F32), jax.ShapeDtypeStruct((2, S5_LANES), F32)],
        scratch_shapes=[pltpu.VMEM((S5_LANE_BLOCKS, tb, 128), F32), pltpu.VMEM((S5_LANE_BLOCKS, tb, 128), F32),
                        pltpu.VMEM((1, S5_LANES), F32), pltpu.VMEM((1, S5_LANES), F32),
                        pltpu.VMEM((SUBLANES, S5_LANES), F32), pltpu.VMEM((SUBLANES, S5_LANES), F32)],
        input_output_aliases={13: 0},
        compiler_params=_params("arbitrary"))(dgelu, y_pre, proj, h_re, h_im, lam_rows, p3_re, p3_im, bbr4, bbi4,
                                              cr4, ci4, d_row, dproj)


def _to_segment_order(v, stage_ref, out_ref, seg):
    nbl = v.shape[1] // 128
    for b in range(nbl):
        stage_ref[b] = v[:, b * 128:(b + 1) * 128]

    def body(t, carry):
        rows = pl.ds(pl.multiple_of(t * SUBLANES, SUBLANES), SUBLANES)
        for b in range(nbl):
            out_ref[rows, _lanes(b)] = stage_ref[b, pl.ds(t, SUBLANES, stride=seg), :]
        return carry

    lax.fori_loop(0, seg, body, 0)


def _from_segment_order(v, stage_ref, out_ref, seg):
    nbl = v.shape[1] // 128
    for b in range(nbl):
        stage_ref[b] = v[:, b * 128:(b + 1) * 128]
    for s in range(SUBLANES):
        def body(k, carry, s=s):
            rows = pl.ds(pl.multiple_of(s * seg + k * SUBLANES, SUBLANES), SUBLANES)
            for b in range(nbl):
                out_ref[rows, _lanes(b)] = stage_ref[b, pl.ds(k * SUBLANES * SUBLANES + s, SUBLANES,
                                                              stride=SUBLANES), :]
            return carry

        lax.fori_loop(0, seg // SUBLANES, body, 0)


def _tile_scan(xr_ref, xi_ref, lam_ref, car_ref, cai_ref, cn_r, cn_i, blocks, seg, reverse):
    shape = (SUBLANES, 128)
    lrs = [jnp.broadcast_to(lam_ref[0:1, _lanes(j)], shape) for j in blocks]
    lis = [jnp.broadcast_to(lam_ref[1:2, _lanes(j)], shape) for j in blocks]

    def step(k, carry):
        t = seg - 1 - k if reverse else k
        rows = pl.ds(pl.multiple_of(t * SUBLANES, SUBLANES), SUBLANES)
        out = []
        for n, j in enumerate(blocks):
            cr, ci = carry[2 * n], carry[2 * n + 1]
            nr = lrs[n] * cr - lis[n] * ci + xr_ref[rows, _lanes(j)]
            ni = lrs[n] * ci + lis[n] * cr + xi_ref[rows, _lanes(j)]
            xr_ref[rows, _lanes(j)] = nr
            xi_ref[rows, _lanes(j)] = ni
            out += [nr, ni]
        return tuple(out)

    zero = jnp.zeros(shape, F32)
    fin = lax.fori_loop(0, seg, step, (zero,) * (2 * len(blocks)), unroll=2)
    for n, j in enumerate(blocks):
        ls = _lanes(j)
        fr, fi = fin[2 * n], fin[2 * n + 1]
        sr, si = lam_ref[2:3, ls], lam_ref[3:4, ls]
        pr, pi = car_ref[:, ls], cai_ref[:, ls]
        for s in (reversed(range(SUBLANES)) if reverse else range(SUBLANES)):
            cn_r[s:s + 1, ls] = pr
            cn_i[s:s + 1, ls] = pi
            pr, pi = fr[s:s + 1, :] + sr * pr - si * pi, fi[s:s + 1, :] + sr * pi + si * pr
        car_ref[:, ls] = pr
        cai_ref[:, ls] = pi


def _s5_fwd3(proj, lam_rows, p3_re, p3_im, bbr4, bbi4, crt4, cit4, d_row, t_len, tb):
    seg = tb // SUBLANES

    def body(u_ref, lam_ref, p3r_ref, p3i_ref, bbr_ref, bbi_ref, crt_ref, cit_ref, d_ref,
             hr_ref, hi_ref, ypre_ref, ys_ref, car_ref, cai_ref, cn_r, cn_i, stage_ref, us_ref, yseg_ref):
        @pl.when(pl.program_id(0) == 0)
        def _():
            car_ref[...] = jnp.zeros_like(car_ref)
            cai_ref[...] = jnp.zeros_like(cai_ref)

        _to_segment_order(u_ref[...], stage_ref, us_ref, seg)
        u = us_ref[...]
        for i in range(S5_BLOCKS):
            ui = u[:, i * S5_BW:(i + 1) * S5_BW]
            hr_ref[:, pl.ds(i * S5_BL, S5_BL)] = _dot(ui, bbr_ref[i])
            hi_ref[:, pl.ds(i * S5_BL, S5_BL)] = _dot(ui, bbi_ref[i])
        for lc in range(S5_LANE_BLOCKS // S5_SCAN_BLOCKS):
            blocks = range(lc * S5_SCAN_BLOCKS, (lc + 1) * S5_SCAN_BLOCKS)
            _tile_scan(hr_ref, hi_ref, lam_ref, car_ref, cai_ref, cn_r, cn_i, blocks, seg, False)
            crs = [cn_r[:, _lanes(j)] for j in blocks]
            cis = [cn_i[:, _lanes(j)] for j in blocks]

            def fix(t, carry, blocks=blocks, crs=crs, cis=cis):
                rows = pl.ds(pl.multiple_of(t * SUBLANES, SUBLANES), SUBLANES)
                for n, j in enumerate(blocks):
                    pr, pi = p3r_ref[t, :, _lanes(j)], p3i_ref[t, :, _lanes(j)]
                    hr_ref[rows, _lanes(j)] += pr * crs[n] - pi * cis[n]
                    hi_ref[rows, _lanes(j)] += pr * cis[n] + pi * crs[n]
                return carry

            lax.fori_loop(0, seg, fix, 0, unroll=2)
        for i in range(S5_BLOCKS):
            ws = pl.ds(i * S5_BW, S5_BW)
            bl = pl.ds(i * S5_BL, S5_BL)
            yseg_ref[:, ws] = (_dot(hr_ref[:, bl], crt_ref[i]) - _dot(hi_ref[:, bl], cit_ref[i])
                               + d_ref[:, ws] * u[:, i * S5_BW:(i + 1) * S5_BW])
        _from_segment_order(yseg_ref[...], stage_ref, ypre_ref, seg)
        ys_ref[...] = jax.nn.gelu(ypre_ref[...], approximate=True).astype(ys_ref.dtype)

    whole = pl.BlockSpec(memory_space=pltpu.VMEM)
    return pl.pallas_call(
        body, name="s5_fwd", grid=(t_len // tb,),
        in_specs=[pl.BlockSpec((tb, S5_WIDTH), lambda i: (i, 4096 // S5_WIDTH))] + [whole] * 8,
        out_specs=[pl.BlockSpec((tb, S5_LANES), lambda i: (i, 0)), pl.BlockSpec((tb, S5_LANES), lambda i: (i, 0)),
                   pl.BlockSpec((tb, S5_WIDTH), lambda i: (i, 0)), pl.BlockSpec((tb, S5_WIDTH), lambda i: (i, 0))],
        out_shape=[jax.ShapeDtypeStruct((t_len, S5_LANES), F32), jax.ShapeDtypeStruct((t_len, S5_LANES), F32),
                   jax.ShapeDtypeStruct((t_len, S5_WIDTH), F32), jax.ShapeDtypeStruct((t_len, S5_WIDTH), MXU_DTYPE)],
        scratch_shapes=[pltpu.VMEM((1, S5_LANES), F32), pltpu.VMEM((1, S5_LANES), F32),
                        pltpu.VMEM((SUBLANES, S5_LANES), F32), pltpu.VMEM((SUBLANES, S5_LANES), F32),
                        pltpu.VMEM((S5_WIDTH // 128, tb, 128), F32), pltpu.VMEM((tb, S5_WIDTH), F32),
                        pltpu.VMEM((tb, S5_WIDTH), F32)],
        compiler_params=_params("arbitrary"))(proj, lam_rows, p3_re, p3_im, bbr4, bbi4, crt4, cit4, d_row)


def _s5_bwd3(dgelu, y_pre, proj, h_re, h_im, lam_rows, p3_re, p3_im, bbr4, bbi4, cr4, ci4, d_row, dproj, t_len, tb):
    seg = tb // SUBLANES
    nb = t_len // tb

    def body(dg_ref, yp_ref, u_ref, hr_ref, hi_ref, lam_ref, p3r_ref, p3i_ref, bbr_ref, bbi_ref, cr_ref, ci_ref,
             d_ref, _, du_ref, dbbr_ref, dbbi_ref, dcr_ref, dci_ref, dd_ref, dlam_ref,
             gr_ref, gi_ref, car_ref, cai_ref, cn_r, cn_i, stage_ref, us_ref, dys_ref, duseg_ref):
        @pl.when(pl.program_id(0) == 0)
        def _():
            for ref in (car_ref, cai_ref, dbbr_ref, dbbi_ref, dcr_ref, dci_ref, dd_ref, dlam_ref):
                ref[...] = jnp.zeros_like(ref)

        _to_segment_order(u_ref[...], stage_ref, us_ref, seg)
        _to_segment_order(dg_ref[...] * _dgelu(yp_ref[...]), stage_ref, dys_ref, seg)
        u, dy = us_ref[...], dys_ref[...]
        for i in range(S5_BLOCKS):
            dyi = dy[:, i * S5_BW:(i + 1) * S5_BW]
            gr_ref[:, pl.ds(i * S5_BL, S5_BL)] = _dot(dyi, cr_ref[i])
            gi_ref[:, pl.ds(i * S5_BL, S5_BL)] = -_dot(dyi, ci_ref[i])
        for lc in range(S5_LANE_BLOCKS // S5_SCAN_BLOCKS):
            blocks = range(lc * S5_SCAN_BLOCKS, (lc + 1) * S5_SCAN_BLOCKS)
            _tile_scan(gr_ref, gi_ref, lam_ref, car_ref, cai_ref, cn_r, cn_i, blocks, seg, True)
            crs = [cn_r[:, _lanes(j)] for j in blocks]
            cis = [cn_i[:, _lanes(j)] for j in blocks]

            def fix(k, carry, blocks=blocks, crs=crs, cis=cis):
                t = seg - 1 - k
                rows = pl.ds(pl.multiple_of(t * SUBLANES, SUBLANES), SUBLANES)
                out = []
                for n, j in enumerate(blocks):
                    nr, ni, slr, sli = carry[4 * n:4 * n + 4]
                    pr, pi = p3r_ref[t, :, _lanes(j)], p3i_ref[t, :, _lanes(j)]
                    g_r = gr_ref[rows, _lanes(j)] + pr * crs[n] - pi * cis[n]
                    g_i = gi_ref[rows, _lanes(j)] + pr * cis[n] + pi * crs[n]
                    gr_ref[rows, _lanes(j)] = g_r
                    gi_ref[rows, _lanes(j)] = g_i
                    hr, hi = hr_ref[rows, _lanes(j)], hi_ref[rows, _lanes(j)]
                    out += [g_r, g_i, slr + nr * hr + ni * hi, sli + ni * hr - nr * hi]
                return tuple(out)

            zero = jnp.zeros((SUBLANES, 128), F32)
            init = []
            for n in range(len(blocks)):
                init += [crs[n], cis[n], zero, zero]
            fin = lax.fori_loop(0, seg, fix, tuple(init), unroll=2)
            for n, j in enumerate(blocks):
                dlam_ref[0:1, _lanes(j)] += jnp.sum(fin[4 * n + 2], axis=0, keepdims=True)
                dlam_ref[1:2, _lanes(j)] += jnp.sum(fin[4 * n + 3], axis=0, keepdims=True)
        for i in range(S5_BLOCKS):
            ws = pl.ds(i * S5_BW, S5_BW)
            bl = pl.ds(i * S5_BL, S5_BL)
            ui, dyi = u[:, i * S5_BW:(i + 1) * S5_BW], dy[:, i * S5_BW:(i + 1) * S5_BW]
            gr, gi = gr_ref[:, bl], gi_ref[:, bl]
            duseg_ref[:, ws] = _dot(gr, bbr_ref[i], _NT) + _dot(gi, bbi_ref[i], _NT) + d_ref[:, ws] * dyi
            dbbr_ref[i] += _dot(ui, gr, _TN)
            dbbi_ref[i] += _dot(ui, gi, _TN)
            dcr_ref[i] += _dot(hr_ref[:, bl], dyi, _TN)
            dci_ref[i] -= _dot(hi_ref[:, bl], dyi, _TN)
        dd_ref[...] += jnp.sum(dy * u, axis=0, keepdims=True)
        _from_segment_order(duseg_ref[...], stage_ref, duseg_ref, seg)
        du_ref[...] = duseg_ref[...].astype(du_ref.dtype)

    whole = pl.BlockSpec(memory_space=pltpu.VMEM)
    rev = lambda i: (nb - 1 - i, 0)
    const3 = lambda i: (0, 0, 0)
    return pl.pallas_call(
        body, name="s5_bwd", grid=(nb,),
        in_specs=[pl.BlockSpec((tb, S5_WIDTH), rev), pl.BlockSpec((tb, S5_WIDTH), rev),
                  pl.BlockSpec((tb, S5_WIDTH), lambda i: (nb - 1 - i, 4096 // S5_WIDTH)),
                  pl.BlockSpec((tb, S5_LANES), rev), pl.BlockSpec((tb, S5_LANES), rev)] + [whole] * 8
                 + [pl.BlockSpec(memory_space=pl.ANY)],
        out_specs=[pl.BlockSpec((tb, S5_WIDTH), lambda i: (nb - 1 - i, 4096 // S5_WIDTH)),
                   pl.BlockSpec((S5_BLOCKS, S5_BW, S5_BL), const3), pl.BlockSpec((S5_BLOCKS, S5_BW, S5_BL), const3),
                   pl.BlockSpec((S5_BLOCKS, S5_BL, S5_BW), const3), pl.BlockSpec((S5_BLOCKS, S5_BL, S5_BW), const3),
                   pl.BlockSpec((1, S5_WIDTH), lambda i: (0, 0)), pl.BlockSpec((2, S5_LANES), lambda i: (0, 0))],
        out_shape=[jax.ShapeDtypeStruct((t_len, IN_COLS), dproj.dtype),
                   jax.ShapeDtypeStruct((S5_BLOCKS, S5_BW, S5_BL), F32),
                   jax.ShapeDtypeStruct((S5_BLOCKS, S5_BW, S5_BL), F32),
                   jax.ShapeDtypeStruct((S5_BLOCKS, S5_BL, S5_BW), F32),
                   jax.ShapeDtypeStruct((S5_BLOCKS, S5_BL, S5_BW), F32),
                   jax.ShapeDtypeStruct((1, S5_WIDTH), F32), jax.ShapeDtypeStruct((2, S5_LANES), F32)],
        scratch_shapes=[pltpu.VMEM((tb, S5_LANES), F32), pltpu.VMEM((tb, S5_LANES), F32),
                        pltpu.VMEM((1, S5_LANES), F32), pltpu.VMEM((1, S5_LANES), F32),
                        pltpu.VMEM((SUBLANES, S5_LANES), F32), pltpu.VMEM((SUBLANES, S5_LANES), F32),
                        pltpu.VMEM((S5_WIDTH // 128, tb, 128), F32), pltpu.VMEM((tb, S5_WIDTH), F32),
                        pltpu.VMEM((tb, S5_WIDTH), F32), pltpu.VMEM((tb, S5_WIDTH), F32)],
        input_output_aliases={13: 0},
        compiler_params=_params("arbitrary"))(dgelu, y_pre, proj, h_re, h_im, lam_rows, p3_re, p3_im, bbr4, bbi4,
                                              cr4, ci4, d_row, dproj)


def _block_diag4(per_group):
    g8 = S5_GROUPS // S5_BLOCKS
    eye = jnp.eye(g8, dtype=bool)[None, :, None, :, None]
    dense = jnp.where(eye, per_group.reshape(S5_BLOCKS, g8, S5_GROUP, 1, S5_STATE), 0.0)
    return dense.reshape(S5_BLOCKS, S5_BW, S5_BL)


def _diag_blocks4(dense):
    g8 = S5_GROUPS // S5_BLOCKS
    ar = jnp.arange(g8)
    d5 = dense.reshape(S5_BLOCKS, g8, S5_GROUP, g8, S5_STATE)
    return d5[:, ar, :, ar, :].transpose(1, 0, 2, 3).reshape(S5_GROUPS, S5_GROUP, S5_STATE)


def _block_diag(per_group):
    eye = jnp.eye(S5_GROUPS, dtype=bool)[:, None, :, None]
    dense = jnp.where(eye, per_group[:, :, None, :], 0.0)
    return dense.reshape(S5_WIDTH, S5_LANES)


def _diag_blocks(dense):
    ar = jnp.arange(S5_GROUPS)
    return dense.reshape(S5_GROUPS, S5_GROUP, S5_GROUPS, S5_STATE)[ar, :, ar, :]


def _hg_gate_bwd(da, o, g, gn):
    dos, dgs, dgns = [], [], []
    for h in range(HG_HEADS):
        sl = slice(h * HG_DIM, (h + 1) * HG_DIM)
        oh, gh, dah, gnh = o[:, sl], g[:, sl], da[:, sl], gn[:, sl]
        rr = lax.rsqrt(jnp.mean(oh * oh, axis=-1, keepdims=True) + NORM_EPS)
        sg = _sig(gh)
        dgs.append(dah * (oh * rr * gnh) * _dsilu(gh, sg))
        don = dah * (gh * sg)
        t = don * gnh
        dos.append(rr * t - oh * (rr * rr * rr) * jnp.mean(t * oh, axis=-1, keepdims=True))
        dgns.append(jnp.sum(don * oh * rr, axis=0, keepdims=True))
    return jnp.concatenate(dos, axis=1), jnp.concatenate(dgs, axis=1), jnp.concatenate(dgns, axis=1)


MIX_BWD_COLS = ((3072, 1024), (4608, 512), (5120, 1024), (6144, 1024))


def _mix_bwd(dgl, h1, dh2, y_hg, y_s5, proj, glu, o_hg, g2, ghn, w, t_len, tm):
    nb = t_len // tm

    def body(dgl_ref, h1_ref, dh2_ref, yh_ref, ys_ref, ghg_ref, z_ref, gh_ref, gs_ref, glu_ref, o_ref, g2_ref, gn_ref,
             wg_ref, wo_ref, ws5_ref, whg_ref, wglu_ref,
             dh1_ref, dyh_ref, dys_ref, dglu_ref, dgelu_ref, do_ref, dg2_ref, dbglu_ref, dgn_ref, dproj_ref,
             st0, st1, st2, st3, sems):
        i = pl.program_id(0)
        stages = (st0, st1, st2, st3)

        def writes(step):
            rows = pl.ds(pl.multiple_of(step * tm, tm), tm)
            return [pltpu.make_async_copy(st, dproj_ref.at[rows, pl.ds(c0, wd)], sems.at[k])
                    for k, (st, (c0, wd)) in enumerate(zip(stages, MIX_BWD_COLS))]

        @pl.when(i > 0)
        def _():
            for cp in writes(i - 1):
                cp.wait()

        @pl.when(i == 0)
        def _():
            for ref in (dg2_ref, dbglu_ref, dgn_ref):
                ref[...] = jnp.zeros_like(ref)

        dx, dg2 = _rms_bwd(_dot(dgl_ref[...], wg_ref[...], _NT), h1_ref[...], g2_ref[...])
        dh1 = dh2_ref[...] + dx
        dh1_ref[...] = dh1
        dg2_ref[...] += dg2
        dm = _dot(dh1, wo_ref[...], _NT)
        sh, ss = _sig(gh_ref[...]), _sig(gs_ref[...])
        dyh, dys = _mx(dm * sh), _mx(dm * ss)
        dyh_ref[...] = dyh
        dys_ref[...] = dys
        st2[...] = (dm * yh_ref[...] * sh * (1.0 - sh)).astype(st2.dtype)
        st3[...] = (dm * ys_ref[...] * ss * (1.0 - ss)).astype(st3.dtype)
        dys2 = _dot(dys, ws5_ref[...], _NT)
        gl_, z = glu_ref[...], z_ref[...]
        a, b = gl_[:, :S5_WIDTH], gl_[:, S5_WIDTH:]
        sb, sz = _sig(b), _sig(z)
        silu = z * sz
        dglu = jnp.concatenate([dys2 * sb * silu, dys2 * a * silu * sb * (1.0 - sb)], axis=1)
        st1[...] = (dys2 * a * sb * _dsilu(z, sz)).astype(st1.dtype)
        dbglu_ref[...] += jnp.sum(dglu, axis=0, keepdims=True)
        dglu_ref[...] = _mx(dglu)
        dgelu_ref[...] = _dot(dglu, wglu_ref[...], _NT)
        d_o, dg, dgn = _hg_gate_bwd(_dot(dyh, whg_ref[...], _NT), o_ref[...], ghg_ref[...], gn_ref[...])
        do_ref[...] = d_o.astype(do_ref.dtype)
        st0[...] = dg.astype(st0.dtype)
        dgn_ref[...] += dgn
        for cp in writes(i):
            cp.start()

        @pl.when(i == nb - 1)
        def _():
            for cp in writes(i):
                cp.wait()

    tile = lambda wd, cb=0: pl.BlockSpec((tm, wd), functools.partial(lambda i, cb: (i, cb), cb=cb))
    row = lambda wd: pl.BlockSpec((1, wd), lambda i: (0, 0))
    whole = pl.BlockSpec(memory_space=pltpu.VMEM)
    return pl.pallas_call(
        body, name="mix_bwd", grid=(nb,),
        in_specs=[tile(1024), tile(1024), tile(1024), tile(1024), tile(1024), tile(1024, 3), tile(512, 4608 // 512),
                  tile(1024, 5), tile(1024, 6), tile(1024), tile(1024), row(1024), row(1024)] + [whole] * 5,
        out_specs=[tile(1024), tile(1024), tile(1024), tile(1024), tile(512), tile(1024), row(1024), row(1024),
                   row(1024), _HBM],
        out_shape=[jax.ShapeDtypeStruct((t_len, 1024), F32), jax.ShapeDtypeStruct((t_len, 1024), MXU_DTYPE),
                   jax.ShapeDtypeStruct((t_len, 1024), MXU_DTYPE), jax.ShapeDtypeStruct((t_len, 1024), MXU_DTYPE),
                   jax.ShapeDtypeStruct((t_len, 512), F32), jax.ShapeDtypeStruct((t_len, 1024), MXU_DTYPE),
                   jax.ShapeDtypeStruct((1, 1024), F32), jax.ShapeDtypeStruct((1, 1024), F32),
                   jax.ShapeDtypeStruct((1, 1024), F32), jax.ShapeDtypeStruct((t_len, IN_COLS), MXU_DTYPE)],
        scratch_shapes=[pltpu.VMEM((tm, wd), MXU_DTYPE) for _, wd in MIX_BWD_COLS] + [pltpu.SemaphoreType.DMA((4,))],
        compiler_params=_params("arbitrary"))(dgl, h1, dh2, y_hg, y_s5, proj, proj, proj, proj, glu, o_hg, g2, ghn,
                                              w["w_ple_gate"], w["w_out"], w["w_o_s5"], w["w_o_hg"], w["w_glu"])


def _local_step(x, p, target, w, sm, comm=None):
    t_len = x.shape[0]
    tm = min(256, t_len)
    tmm = min(512, t_len)
    tb_hg = min(256, t_len)
    tb_s5 = min(256, t_len)
    g1, g2, g3, ghn = sm["norm_g"], sm["ple_norm_g"], sm["final_norm_g"].reshape(1, D_MODEL), sm["hg_norm_g"]

    def rms_in(xv, g):
        return xv * lax.rsqrt(jnp.mean(xv * xv, axis=-1, keepdims=True) + NORM_EPS) * g

    in_shard = IN_COLS // N_CHIPS
    w_in = w["w_in"]
    if comm is None:
        proj, u = _mm_nn("mm_in", x, w_in, tmm, in_shard, prologue=rms_in, consts=[g1])
    else:
        proj, u, landed = _mm_nn("mm_in", x, w_in, tmm, in_shard, riding=comm.gather_rest(), prologue=rms_in,
                                 consts=[g1])
        w = comm.rest_weights(landed)
    o_hg, act_hg, s_prev = _hgrn2_fwd2(proj, sm["hg_lb"], ghn, t_len, tb_hg)

    lanes = lambda a: a.reshape(1, S5_LANES)
    a_re, a_im = lanes(sm["s5_a_re"]), lanes(sm["s5_a_im"])
    ldt = lanes(jnp.broadcast_to(sm["s5_log_dt"].reshape(S5_GROUPS, 1), (S5_GROUPS, S5_STATE)))
    to_t = lambda b: b.reshape(S5_GROUPS, S5_STATE, S5_GROUP).transpose(2, 0, 1).reshape(S5_GROUP, S5_LANES)
    b_re_t, b_im_t = to_t(sm["s5_b_re"]), to_t(sm["s5_b_im"])
    scan_fwd, scan_rev, bbr_t, bbi_t = _s5_powers(a_re, a_im, ldt, b_re_t, b_im_t, tb_s5 // SUBLANES)
    from_t = lambda b: b.reshape(S5_GROUP, S5_GROUPS, S5_STATE).transpose(1, 0, 2)
    bbr_bd = _block_diag4(from_t(bbr_t)).astype(MXU_DTYPE)
    bbi_bd = _block_diag4(from_t(bbi_t)).astype(MXU_DTYPE)
    cr_bd = _block_diag4(sm["s5_c_re"].reshape(S5_GROUPS, S5_GROUP, S5_STATE)).astype(MXU_DTYPE)
    ci_bd = _block_diag4(sm["s5_c_im"].reshape(S5_GROUPS, S5_GROUP, S5_STATE)).astype(MXU_DTYPE)
    d_row = sm["s5_d"].reshape(1, S5_WIDTH)
    h_re, h_im, y_pre, ys_gelu = _s5_fwd3(proj, *scan_fwd, bbr_bd, bbi_bd,
                                          cr_bd.transpose(0, 2, 1), ci_bd.transpose(0, 2, 1), d_row, t_len, tb_s5)
    def mix_f(act, ysg, z, gh, gs, xv, w_glu, b_glu, w_o_hg, w_o_s5, w_out):
        gl_ = _dot(ysg, w_glu) + b_glu
        a, b = gl_[:, :S5_WIDTH], gl_[:, S5_WIDTH:]
        ys2_ = (a * _sig(b) * (z * _sig(z))).astype(MXU_DTYPE)
        yh, ys = _dot(act, w_o_hg), _dot(ys2_, w_o_s5)
        mg = (_sig(gh) * yh + _sig(gs) * ys).astype(MXU_DTYPE)
        return (gl_, ys2_, yh, ys, mg, xv + _dot(mg, w_out))

    glu, ys2, y_hg, y_s5, merged, h1 = _rowwise(
        "mix_out", mix_f, t_len, tm,
        [(act_hg, 1024, 0), (ys_gelu, 512, 0), (proj, 512, 4608 // 512), (proj, 1024, 5), (proj, 1024, 6),
         (x, 1024, 0)], [w["w_glu"], sm["b_glu"], w["w_o_hg"], w["w_o_s5"], w["w_out"]],
        [(1024, F32), (512, MXU_DTYPE), (1024, F32), (1024, F32), (1024, MXU_DTYPE), (1024, F32)])

    def head_f(h1v, pv, tgt, g_ple, g, w_ple, w_gate):
        r2 = lax.rsqrt(jnp.mean(h1v * h1v, axis=-1, keepdims=True) + NORM_EPS)
        n2_ = (h1v * r2 * g_ple).astype(MXU_DTYPE)
        glv, pev = _dot(n2_, w_gate), _dot(pv, w_ple)
        gate = _sig(glv)
        h2 = h1v + pev * gate
        r = lax.rsqrt(jnp.mean(h2 * h2, axis=-1, keepdims=True) + NORM_EPS)
        e = h2 * r * g - tgt
        loss = 0.5 * jnp.sum(jnp.mean(e * e, axis=-1, keepdims=True), axis=0, keepdims=True)
        dy = e * (1.0 / D_MODEL)
        dg = jnp.sum(dy * h2 * r, axis=0, keepdims=True)
        t = dy * g
        dh2 = r * t - h2 * (r * r * r) * jnp.mean(t * h2, axis=-1, keepdims=True)
        return (n2_, dh2, dh2 * gate, dh2 * pev * gate * (1.0 - gate), jnp.broadcast_to(loss, (1, 128)), dg)

    n2, dh2, dpe, dgl, loss_row, d_g3 = _rowwise(
        "ple_loss_head", head_f, t_len, tm, [(h1, 1024, 0), (p, 256, 0), (target, 1024, 0)],
        [g2, g3, w["w_ple"], w["w_ple_gate"]],
        [(1024, MXU_DTYPE), (1024, F32), (1024, MXU_DTYPE), (1024, MXU_DTYPE)], accs=[(1, 128), (1, 1024)])

    gb = {}
    gb["w_ple"] = _mm_tn("mm_d_w_ple", p, dpe, tmm, 1024)
    gb["w_ple_gate"] = _mm_tn("mm_d_w_ple_gate", n2, dgl, tmm, 1024)
    dh1, dy_hg, dy_s5, dglu, dgelu, d_o, d_g2, d_bglu, d_ghn, dproj = _mix_bwd(
        dgl, h1, dh2, y_hg, y_s5, proj, glu, o_hg, g2, ghn, w, t_len, tm)
    gb["w_out"] = _mm_tn("mm_d_w_out", merged, dh1, tmm, 1024)
    gb["w_o_s5"] = _mm_tn("mm_d_w_o_s5", ys2, dy_s5, tmm, 1024)
    gb["w_glu"] = _mm_tn("mm_d_w_glu", ys_gelu, dglu, tmm, 1024)
    dproj, d_bbr, d_bbi, d_crt, d_cit, d_d, d_lam = _s5_bwd3(dgelu, y_pre, proj, h_re, h_im,
                                                            *scan_rev, bbr_bd, bbi_bd, cr_bd,
                                                            ci_bd, d_row, dproj, t_len, tb_s5)
    to_t3 = lambda b: b.transpose(1, 0, 2).reshape(S5_GROUP, S5_LANES)
    d_are, d_aim, d_ldt, d_br_t, d_bi_t = _s5_prep_bwd(a_re, a_im, ldt, b_re_t, b_im_t, d_lam,
                                                       to_t3(_diag_blocks4(d_bbr)), to_t3(_diag_blocks4(d_bbi)))
    gb["w_o_hg"] = _mm_tn("mm_d_w_o_hg", act_hg, dy_hg, tmm, 1024)
    if comm is None:
        dproj, d_lb = _hgrn2_bwd2(proj, d_o, s_prev, sm["hg_lb"], dproj, t_len, tb_hg)
    else:
        rest_grads = _pack_rest_full(gb)
        dproj, d_lb, rest_theirs = _hgrn2_bwd2(proj, d_o, s_prev, sm["hg_lb"], dproj, t_len, tb_hg,
                                               riding=comm.swap(rest_grads))

    def in_b(duv, xv, dh, g):
        dx, dg = _rms_bwd(duv, xv, g)
        return (dh + dx, dg)

    in_args = ("mm_d_u_rms_in_bwd", dproj, w_in, tmm, in_shard, in_b, [(x, 1024, 0), (dh1, 1024, 0)], [g1],
               [(1024, F32)])
    if comm is None:
        gb["w_in"] = _mm_tn("mm_d_w_in", u, dproj, tmm, in_shard, col_shards=True)
        grad_x, d_g1 = _mm_nt_then(*in_args, accs=[(1, 1024)])
    else:
        gb["w_in"], landed = _mm_tn("mm_d_w_in", u, dproj, tmm, in_shard, col_shards=True,
                                    riding=comm.scatter("rest", rest_grads, rest_theirs))
        comm.landed["rest"] = landed
        grad_x, d_g1, landed = _mm_nt_then(*in_args, accs=[(1, 1024)], riding=comm.scatter(
            "in", gb["w_in"].reshape(N_CHIPS, 2, D_MODEL // 2, in_shard)))
        comm.landed["in"] = landed

    back_t = lambda b: b.reshape(S5_GROUP, S5_GROUPS, S5_STATE).transpose(1, 2, 0).reshape(1, S5_GROUPS, S5_STATE,
                                                                                           S5_GROUP)
    gs = {
        "norm_g": d_g1, "hg_lb": d_lb, "hg_norm_g": d_ghn,
        "s5_a_re": d_are.reshape(1, S5_GROUPS, S5_STATE), "s5_a_im": d_aim.reshape(1, S5_GROUPS, S5_STATE),
        "s5_log_dt": d_ldt[0:1, :S5_GROUPS],
        "s5_b_re": back_t(d_br_t), "s5_b_im": back_t(d_bi_t),
        "s5_c_re": _diag_blocks4(d_crt.transpose(0, 2, 1)).reshape(1, S5_GROUPS, S5_GROUP, S5_STATE),
        "s5_c_im": _diag_blocks4(d_cit.transpose(0, 2, 1)).reshape(1, S5_GROUPS, S5_GROUP, S5_STATE),
        "s5_d": d_d.reshape(1, S5_GROUPS, S5_GROUP), "b_glu": d_bglu, "ple_norm_g": d_g2,
        "final_norm_g": d_g3.reshape(D_MODEL),
    }
    return loss_row, grad_x, gb, gs


def _shard_shape(name):
    r, c = BIG_SHAPE[name]
    return (r, c // N_CHIPS) if name in BIG_COL_SHARDED else (r // N_CHIPS, c)


def _pack_shard(parts):
    return jnp.concatenate([parts[n].reshape(-1, PACK_W) for n in BIG], axis=0)


def _unpack_shard(packed):
    out, off = {}, 0
    for n in BIG:
        r, c = _shard_shape(n)
        rows = r * c // PACK_W
        out[n] = packed[off:off + rows].reshape(1, r, c)
        off += rows
    return out


def _unpack_full(gathered):
    out, off = {}, 0
    for n in BIG:
        r, c = _shard_shape(n)
        rows = r * c // PACK_W
        sh = gathered[:, off:off + rows].reshape(N_CHIPS, r, c)
        out[n] = sh.transpose(1, 0, 2).reshape(BIG_SHAPE[n]) if n in BIG_COL_SHARDED else sh.reshape(BIG_SHAPE[n])
        off += rows
    return out


def _pack_full(full):
    parts = []
    for n in BIG:
        r, c = _shard_shape(n)
        g = full[n]
        sh = g.reshape(BIG_SHAPE[n][0], N_CHIPS, c).transpose(1, 0, 2) if n in BIG_COL_SHARDED else g
        parts.append(sh.reshape(N_CHIPS, r * c // PACK_W, PACK_W))
    packed = jnp.concatenate(parts, axis=1)
    return packed.reshape(N_CHIPS, 2, HALF_ROWS, PACK_W).transpose(1, 0, 2, 3)


def _pack_small(parts, last):
    flat = jnp.concatenate([parts[n].reshape(-1) for n in SMALL] + [last.reshape(-1)])
    return jnp.pad(flat, (0, SMALL_ROWS * PACK_W - flat.shape[0])).reshape(SMALL_ROWS, PACK_W)


def _unpack_small(packed):
    flat, out, off = packed.reshape(-1), {}, 0
    for n in SMALL:
        size = 1
        for d in SMALL_SHAPE[n]:
            size *= d
        out[n] = flat[off:off + size].reshape(SMALL_SHAPE[n])
        off += size
    return out, flat[off]


def _place():
    x, y, c = lax.axis_index("x"), lax.axis_index("y"), lax.axis_index("c")
    return x, y, c, [(1 - x, y), (x, 1 - y), (1 - x, 1 - y)]


def _remote(src, dst, send_sems, recv_sems, k, to):
    return pltpu.make_async_remote_copy(src_ref=src, dst_ref=dst, send_sem=send_sems.at[k], recv_sem=recv_sems.at[k],
                                        device_id=to, device_id_type=MESH)


_HBM = pl.BlockSpec(memory_space=pl.ANY)


def _all_gather_weights(wp):
    def body(wp_ref, out_ref, send_sems, recv_sems):
        x, y, c, chips = _place()
        k = 2 * x + y
        sibling = (x, y, 1 - c)
        first =[_remote(wp_ref.at[c], out_ref.at[k, c], send_sems, recv_sems, j, (cx, cy, c))
                 for j, (cx, cy) in enumerate(chips)]
        for cp in first:
            cp.start()
        passed = []
        for j, (cx, cy) in enumerate(chips):
            kj = 2 * cx + cy
            _remote(wp_ref.at[c], out_ref.at[kj, c], send_sems, recv_sems, j, (cx, cy, c)).wait_recv()
            cp = _remote(out_ref.at[kj, c], out_ref.at[kj, c], send_sems, recv_sems, 3 + j, sibling)
            cp.start()
            passed.append(cp)
        for j, (cx, cy) in enumerate(chips):
            kj = 2 * cx + cy
            _remote(wp_ref.at[c], out_ref.at[kj, 1 - c], send_sems, recv_sems, 3 + j, sibling).wait_recv()
        for cp in first + passed:
            cp.wait_send()

    return pl.pallas_call(
        body, name="all_gather_weights", in_specs=[_HBM], out_specs=_HBM,
        out_shape=jax.ShapeDtypeStruct((N_CHIPS, 2, HALF_ROWS, PACK_W), wp.dtype),
        scratch_shapes=[pltpu.SemaphoreType.DMA((6,)), pltpu.SemaphoreType.DMA((6,))])(wp)


def _exchange_halves(pg):
    def body(pg_ref, out_ref, send_sems, recv_sems):
        x, y, c, _ = _place()
        cp = _remote(pg_ref.at[1 - c], out_ref, send_sems, recv_sems, 0, (x, y, 1 - c))
        cp.start()
        cp.wait()

    return pl.pallas_call(
        body, name="exchange_halves", in_specs=[_HBM], out_specs=_HBM,
        out_shape=jax.ShapeDtypeStruct((N_CHIPS, HALF_ROWS, PACK_W), pg.dtype),
        scratch_shapes=[pltpu.SemaphoreType.DMA((1,)), pltpu.SemaphoreType.DMA((1,))])(pg)


def _scatter_chip_sums(ps):
    def body(ps_ref, out_ref, send_sems, recv_sems):
        x, y, c, chips = _place()
        cps = [_remote(ps_ref.at[2 * cx + cy], out_ref.at[j], send_sems, recv_sems, j, (cx, cy, c))
               for j, (cx, cy) in enumerate(chips)]
        for cp in cps:
            cp.start()
        for cp in cps:
            cp.wait()

    return pl.pallas_call(
        body, name="scatter_chip_sums", in_specs=[_HBM], out_specs=_HBM,
        out_shape=jax.ShapeDtypeStruct((3, HALF_ROWS, PACK_W), ps.dtype),
        scratch_shapes=[pltpu.SemaphoreType.DMA((3,)), pltpu.SemaphoreType.DMA((3,))])(ps)


def _share_half(g_half):
    def body(g_ref, out_ref, send_sems, recv_sems):
        x, y, c, _ = _place()
        cp = _remote(g_ref, out_ref.at[c], send_sems, recv_sems, 0, (x, y, 1 - c))
        cp.start()
        _remote(g_ref, out_ref.at[1 - c], send_sems, recv_sems, 0, (x, y, 1 - c)).wait_recv()
        cp.wait_send()

    return pl.pallas_call(
        body, name="share_half", in_specs=[_HBM], out_specs=_HBM,
        out_shape=jax.ShapeDtypeStruct((2, HALF_ROWS, PACK_W), g_half.dtype),
        scratch_shapes=[pltpu.SemaphoreType.DMA((1,)), pltpu.SemaphoreType.DMA((1,))])(g_half)


REDUCE_ROWS = 480


def _sum_pair(pg, theirs, c):
    def body(c_ref, a_ref, b_ref, o_ref):
        o_ref[...] = (a_ref[...] + b_ref[...]).astype(o_ref.dtype)

    return pl.pallas_call(
        body, name="sum_pair",
        grid_spec=pltpu.PrefetchScalarGridSpec(
            num_scalar_prefetch=1, grid=(N_CHIPS, HALF_ROWS // REDUCE_ROWS),
            in_specs=[pl.BlockSpec((None, None, REDUCE_ROWS, PACK_W), lambda j, i, c_ref: (c_ref[0], j, i, 0)),
                      pl.BlockSpec((None, REDUCE_ROWS, PACK_W), lambda j, i, c_ref: (j, i, 0))],
            out_specs=pl.BlockSpec((None, REDUCE_ROWS, PACK_W), lambda j, i, c_ref: (j, i, 0))),
        out_shape=jax.ShapeDtypeStruct((N_CHIPS, HALF_ROWS, PACK_W), WIRE_DTYPE),
        compiler_params=_params("arbitrary", "arbitrary"))(c.reshape(1), pg, theirs)


def _sum_chips(ps, others, k):
    def body(k_ref, a_ref, b_ref, o_ref):
        o_ref[...] = ((a_ref[...].astype(F32) + b_ref[0].astype(F32)) + b_ref[1].astype(F32)) + b_ref[2].astype(F32)

    return pl.pallas_call(
        body, name="sum_chips",
        grid_spec=pltpu.PrefetchScalarGridSpec(
            num_scalar_prefetch=1, grid=(HALF_ROWS // REDUCE_ROWS,),
            in_specs=[pl.BlockSpec((None, REDUCE_ROWS, PACK_W), lambda i, k_ref: (k_ref[0], i, 0)),
                      pl.BlockSpec((3, REDUCE_ROWS, PACK_W), lambda i, k_ref: (0, i, 0))],
            out_specs=pl.BlockSpec((REDUCE_ROWS, PACK_W), lambda i, k_ref: (i, 0))),
        out_shape=jax.ShapeDtypeStruct((HALF_ROWS, PACK_W), F32),
        compiler_params=_params("arbitrary"))(k.reshape(1), ps, others)


REST = tuple(n for n in BIG if n != "w_in")
REST_ROWS = sum(BIG_SHAPE[n][0] * BIG_SHAPE[n][1] for n in REST) // (N_CHIPS * PACK_W)
IN_SHARD = IN_COLS // N_CHIPS
IN_TILE, REST_TILE = 256, 272


def _pack_rest(parts):
    return jnp.concatenate([parts[n].reshape(-1, PACK_W) for n in REST], axis=0)


def _unpack_rest(packed):
    out, off = {}, 0
    for n in REST:
        r, c = _shard_shape(n)
        rows = r * c // PACK_W
        out[n] = packed[off:off + rows].reshape(1, r, c)
        off += rows
    return out


def _unpack_rest_full(gathered):
    out, off = {}, 0
    for n in REST:
        r, c = _shard_shape(n)
        rows = r * c // PACK_W
        sh = gathered[:, off:off + rows].reshape(N_CHIPS, r, c)
        out[n] = sh.transpose(1, 0, 2).reshape(BIG_SHAPE[n]) if n in BIG_COL_SHARDED else sh.reshape(BIG_SHAPE[n])
        off += rows
    return out


def _pack_rest_full(full):
    parts = []
    for n in REST:
        r, c = _shard_shape(n)
        g = full[n]
        sh = g.reshape(BIG_SHAPE[n][0], N_CHIPS, c).transpose(1, 0, 2) if n in BIG_COL_SHARDED else g
        parts.append(sh.reshape(N_CHIPS, r * c // PACK_W, PACK_W))
    return jnp.concatenate(parts, axis=1).reshape(N_CHIPS, 2, REST_ROWS // 2, PACK_W)


def _gather_shards(ws):
    n = len(ws)

    def body(*refs):
        w_refs, out_refs, (send_sems, recv_sems) = refs[:n], refs[n:2 * n], refs[2 * n:]
        x, y, c, chips = _place()
        k = 2 * x + y
        sibling = (x, y, 1 - c)
        first = [_remote(w_ref.at[c], out_ref.at[k, c], send_sems, recv_sems, 6 * g + j, (cx, cy, c))
                 for j, (cx, cy) in enumerate(chips) for g, (w_ref, out_ref) in enumerate(zip(w_refs, out_refs))]
        for cp in first:
            cp.start()
        passed = []
        for j, (cx, cy) in enumerate(chips):
            kj = 2 * cx + cy
            for g, (w_ref, out_ref) in enumerate(zip(w_refs, out_refs)):
                _remote(w_ref.at[c], out_ref.at[kj, c], send_sems, recv_sems, 6 * g + j, (cx, cy, c)).wait_recv()
                cp = _remote(out_ref.at[kj, c], out_ref.at[kj, c], send_sems, recv_sems, 6 * g + 3 + j, sibling)
                cp.start()
                passed.append(cp)
        for j, (cx, cy) in enumerate(chips):
            kj = 2 * cx + cy
            for g, (w_ref, out_ref) in enumerate(zip(w_refs, out_refs)):
                _remote(w_ref.at[c], out_ref.at[kj, 1 - c], send_sems, recv_sems, 6 * g + 3 + j, sibling).wait_recv()
        for cp in first + passed:
            cp.wait_send()

    return pl.pallas_call(
        body, name="all_gather_weights", in_specs=[_HBM] * n, out_specs=[_HBM] * n,
        out_shape=[jax.ShapeDtypeStruct((N_CHIPS,) + w.shape, w.dtype) for w in ws],
        scratch_shapes=[pltpu.SemaphoreType.DMA((6 * n,)), pltpu.SemaphoreType.DMA((6 * n,))])(*ws)


def _swap_halves(pgs, name="exchange_halves"):
    n = len(pgs)

    def body(*refs):
        pg_refs, out_refs, (send_sems, recv_sems) = refs[:n], refs[n:2 * n], refs[2 * n:]
        x, y, c, _ = _place()
        cps = [_remote(pg_ref.at[j, 1 - c], out_ref.at[j], send_sems, recv_sems, N_CHIPS * g + j, (x, y, 1 - c))
               for g, (pg_ref, out_ref) in enumerate(zip(pg_refs, out_refs)) for j in range(N_CHIPS)]
        for cp in cps:
            cp.start()
        for cp in cps:
            cp.wait()

    return pl.pallas_call(
        body, name=name, in_specs=[_HBM] * n, out_specs=[_HBM] * n,
        out_shape=[jax.ShapeDtypeStruct((N_CHIPS,) + pg.shape[2:], pg.dtype) for pg in pgs],
        scratch_shapes=[pltpu.SemaphoreType.DMA((N_CHIPS * n,)), pltpu.SemaphoreType.DMA((N_CHIPS * n,))])(*pgs)


def _scatter_sums(pss):
    n = len(pss)

    def body(*refs):
        ps_refs, out_refs, (send_sems, recv_sems) = refs[:n], refs[n:2 * n], refs[2 * n:]
        x, y, c, chips = _place()
        cps = [_remote(ps_ref.at[2 * cx + cy], out_ref.at[j], send_sems, recv_sems, 3 * g + j, (cx, cy, c))
               for j, (cx, cy) in enumerate(chips) for g, (ps_ref, out_ref) in enumerate(zip(ps_refs, out_refs))]
        for cp in cps:
            cp.start()
        for cp in cps:
            cp.wait()

    return pl.pallas_call(
        body, name="scatter_chip_sums", in_specs=[_HBM] * n, out_specs=[_HBM] * n,
        out_shape=[jax.ShapeDtypeStruct((3,) + ps.shape[1:], ps.dtype) for ps in pss],
        scratch_shapes=[pltpu.SemaphoreType.DMA((3 * n,)), pltpu.SemaphoreType.DMA((3 * n,))])(*pss)


def _share_halves(gs):
    n = len(gs)

    def body(*refs):
        g_refs, out_refs, (send_sems, recv_sems) = refs[:n], refs[n:2 * n], refs[2 * n:]
        x, y, c, _ = _place()
        cps = [_remote(g_ref, out_ref.at[c], send_sems, recv_sems, g, (x, y, 1 - c))
               for g, (g_ref, out_ref) in enumerate(zip(g_refs, out_refs))]
        for cp in cps:
            cp.start()
        for g, (g_ref, out_ref) in enumerate(zip(g_refs, out_refs)):
            _remote(g_ref, out_ref.at[1 - c], send_sems, recv_sems, g, (x, y, 1 - c)).wait_recv()
        for cp in cps:
            cp.wait_send()

    return pl.pallas_call(
        body, name="share_half", in_specs=[_HBM] * n, out_specs=[_HBM] * n,
        out_shape=[jax.ShapeDtypeStruct((2,) + g.shape, g.dtype) for g in gs],
        scratch_shapes=[pltpu.SemaphoreType.DMA((n,)), pltpu.SemaphoreType.DMA((n,))])(*gs)


def _pair_sum(name, pg, theirs, c, tile):
    _, _, rows, width = pg.shape

    def body(c_ref, a_ref, b_ref, o_ref):
        o_ref[...] = (a_ref[...] + b_ref[...]).astype(o_ref.dtype)

    return pl.pallas_call(
        body, name=name,
        grid_spec=pltpu.PrefetchScalarGridSpec(
            num_scalar_prefetch=1, grid=(N_CHIPS, rows // tile),
            in_specs=[pl.BlockSpec((None, None, tile, width), lambda j, i, c_ref: (j, c_ref[0], i, 0)),
                      pl.BlockSpec((None, tile, width), lambda j, i, c_ref: (j, i, 0))],
            out_specs=pl.BlockSpec((None, tile, width), lambda j, i, c_ref: (j, i, 0))),
        out_shape=jax.ShapeDtypeStruct((N_CHIPS, rows, width), WIRE_DTYPE),
        compiler_params=_params("arbitrary", "arbitrary"))(c.reshape(1), pg, theirs)


def _chip_sum(name, ps, others, k, tile):
    _, rows, width = ps.shape

    def body(k_ref, a_ref, b_ref, o_ref):
        o_ref[...] = ((a_ref[...].astype(F32) + b_ref[0].astype(F32)) + b_ref[1].astype(F32)) + b_ref[2].astype(F32)

    return pl.pallas_call(
        body, name=name,
        grid_spec=pltpu.PrefetchScalarGridSpec(
            num_scalar_prefetch=1, grid=(rows // tile,),
            in_specs=[pl.BlockSpec((None, tile, width), lambda i, k_ref: (k_ref[0], i, 0)),
                      pl.BlockSpec((3, tile, width), lambda i, k_ref: (0, i, 0))],
            out_specs=pl.BlockSpec((tile, width), lambda i, k_ref: (i, 0))),
        out_shape=jax.ShapeDtypeStruct((rows, width), F32),
        compiler_params=_params("arbitrary"))(k.reshape(1), ps, others)


class _StepComm:
    TILES = {"in": IN_TILE, "rest": REST_TILE}

    def __init__(self, rest_wire, chip, core):
        self.rest_wire, self.chip, self.core = rest_wire, chip, core
        self.sums, self.landed = {}, {}

    def gather_rest(self):
        wire = self.rest_wire

        def sends(ins, outs, send_sems, recv_sems):
            (w_ref,), (out_ref,) = ins, outs
            x, y, c, chips = _place()
            return [_remote(w_ref.at[c], out_ref.at[2 * x + y, c], send_sems, recv_sems, 4 * j + 2 * c + to,
                            (cx, cy, to)) for j, (cx, cy) in enumerate(chips) for to in (0, 1)]

        def recvs(ins, outs, send_sems, recv_sems):
            (w_ref,), (out_ref,) = ins, outs
            _, _, c, chips = _place()
            return [_remote(w_ref.at[c], out_ref.at[2 * cx + cy, by], send_sems, recv_sems, 4 * j + 2 * by + c,
                            (cx, cy, by)) for j, (cx, cy) in enumerate(chips) for by in (0, 1)]

        def start(*refs):
            for cp in sends(*refs):
                cp.start()

        def wait(*refs):
            for cp in recvs(*refs):
                cp.wait_recv()
            for cp in sends(*refs):
                cp.wait_send()

        return _Riding((wire,), (jax.ShapeDtypeStruct((N_CHIPS,) + wire.shape, wire.dtype),), 12, start, wait)

    def rest_weights(self, landed):
        full = lax.dynamic_update_slice(landed, self.rest_wire[None], (self.chip, 0, 0, 0))
        return _unpack_rest_full(full.reshape(N_CHIPS, REST_ROWS, PACK_W))

    def swap(self, pg):
        def copies(ins, outs, send_sems, recv_sems):
            (pg_ref,), (out_ref,) = ins, outs
            x, y, c, _ = _place()
            return [_remote(pg_ref.at[j, 1 - c], out_ref.at[j], send_sems, recv_sems, j, (x, y, 1 - c))
                    for j in range(N_CHIPS)]

        def start(*refs):
            for cp in copies(*refs):
                cp.start()

        def wait(*refs):
            for cp in copies(*refs):
                cp.wait()

        return _Riding((pg,), (jax.ShapeDtypeStruct((N_CHIPS,) + pg.shape[2:], pg.dtype),), N_CHIPS, start, wait)

    def scatter(self, group, pg, theirs=None):
        if theirs is None:
            (theirs,) = _swap_halves([pg], "exchange_halves_" + group)
        ps = _pair_sum("sum_pair_" + group, pg, theirs, self.core, self.TILES[group])
        self.sums[group] = ps

        def copies(ins, outs, send_sems, recv_sems):
            (ps_ref,), (out_ref,) = ins, outs
            _, _, c, chips = _place()
            return [_remote(ps_ref.at[2 * cx + cy], out_ref.at[j], send_sems, recv_sems, j, (cx, cy, c))
                    for j, (cx, cy) in enumerate(chips)]

        def start(*refs):
            for cp in copies(*refs):
                cp.start()

        def wait(*refs):
            for cp in copies(*refs):
                cp.wait()

        return _Riding((ps,), (jax.ShapeDtypeStruct((3,) + ps.shape[1:], ps.dtype),), 3, start, wait)

    def reduced(self, group):
        return _chip_sum("sum_chips_" + group, self.sums[group], self.landed[group], self.chip, self.TILES[group])


def _adamw(w, g, m, v):
    m = ADAM_B1 * m + (1.0 - ADAM_B1) * g
    v = ADAM_B2 * v + (1.0 - ADAM_B2) * (g * g)
    m_hat = m / (1.0 - ADAM_B1 ** ADAM_STEP)
    v_hat = v / (1.0 - ADAM_B2 ** ADAM_STEP)
    return -ADAM_LR * (m_hat / (jnp.sqrt(v_hat) + ADAM_EPS) + ADAM_WD * w), m, v


def _small_reduce_adamw(part, w, m, v):
    def body(part_ref, w_ref, m_ref, v_ref, g_ref, d_ref, nm_ref, nv_ref, all_ref, send_sems, recv_sems):
        x, y, c, chips = _place()
        me, sibling = (x, y, c), (x, y, 1 - c)

        def rows(px, py, pc):
            return all_ref.at[4 * px + 2 * py + pc]

        all_ref[4 * x + 2 * y + c] = part_ref[...]
        first = [_remote(part_ref, rows(*me), send_sems, recv_sems, 0, sibling)]
        first += [_remote(part_ref, rows(*me), send_sems, recv_sems, 1 + j, (cx, cy, c))
                  for j, (cx, cy) in enumerate(chips)]
        for cp in first:
            cp.start()
        passed = []
        for j, (cx, cy) in enumerate(chips):
            _remote(part_ref, rows(cx, cy, c), send_sems, recv_sems, 1 + j, me).wait_recv()
            cp = _remote(rows(cx, cy, c), rows(cx, cy, c), send_sems, recv_sems, 4 + j, sibling)
            cp.start()
            passed.append(cp)
        _remote(part_ref, rows(*sibling), send_sems, recv_sems, 0, me).wait_recv()
        for j, (cx, cy) in enumerate(chips):
            _remote(part_ref, rows(cx, cy, 1 - c), send_sems, recv_sems, 4 + j, me).wait_recv()
        for cp in first + passed:
            cp.wait_send()
        g = all_ref[0]
        for dev in range(1, N_DEV):
            g = g + all_ref[dev]
        delta, nm, nv = _adamw(w_ref[...], g, m_ref[...], v_ref[...])
        g_ref[...] = g
        d_ref[...] = delta
        nm_ref[...] = nm
        nv_ref[...] = nv

    whole = pl.BlockSpec(memory_space=pltpu.VMEM)
    shape = jax.ShapeDtypeStruct((SMALL_ROWS, PACK_W), F32)
    return pl.pallas_call(
        body, name="small_reduce_adamw", in_specs=[whole] * 4, out_specs=[whole] * 4, out_shape=[shape] * 4,
        scratch_shapes=[pltpu.VMEM((N_DEV, SMALL_ROWS, PACK_W), F32), pltpu.SemaphoreType.DMA((7,)),
                        pltpu.SemaphoreType.DMA((7,))],
        compiler_params=pltpu.CompilerParams(vmem_limit_bytes=VMEM_LIMIT))(part, w, m, v)


def kernel(x, p, norm_g, w_in, hg_lb, hg_norm_g, w_o_hg, s5_a_re, s5_a_im, s5_log_dt, s5_b_re, s5_b_im, s5_c_re, s5_c_im, s5_d, w_glu, b_glu, w_o_s5, w_out, ple_norm_g, w_ple, w_ple_gate, final_norm_g, loss_target, m_norm_g, m_w_in, m_hg_lb, m_hg_norm_g, m_w_o_hg, m_s5_a_re, m_s5_a_im, m_s5_log_dt, m_s5_b_re, m_s5_b_im, m_s5_c_re, m_s5_c_im, m_s5_d, m_w_glu, m_b_glu, m_w_o_s5, m_w_out, m_ple_norm_g, m_w_ple, m_w_ple_gate, m_final_norm_g, v_norm_g, v_w_in, v_hg_lb, v_hg_norm_g, v_w_o_hg, v_s5_a_re, v_s5_a_im, v_s5_log_dt, v_s5_b_re, v_s5_b_im, v_s5_c_re, v_s5_c_im, v_s5_d, v_w_glu, v_b_glu, v_w_o_s5, v_w_out, v_ple_norm_g, v_w_ple, v_w_ple_gate, v_final_norm_g):
    given = dict(locals())
    wts = {n: given[n] for n in WEIGHTS}
    mom = {n: given["m_" + n] for n in WEIGHTS}
    var = {n: given["v_" + n] for n in WEIGHTS}
    cx, cy, cc = lax.axis_index("x"), lax.axis_index("y"), lax.axis_index("c")
    chip = (2 * cx + cy).astype(jnp.int32)

    core = cc.astype(jnp.int32)
    rest_shard = _pack_rest({n: wts[n][0] for n in REST})
    in_wire = wts["w_in"][0].astype(MXU_DTYPE).reshape(2, D_MODEL // 2, IN_SHARD)
    (w_in_all,) = _gather_shards([in_wire])
    w_in_all = lax.dynamic_update_slice(w_in_all, in_wire[None], (chip, 0, 0, 0)).reshape(N_CHIPS, D_MODEL, IN_SHARD)
    comm = _StepComm(rest_shard.astype(MXU_DTYPE).reshape(2, REST_ROWS // 2, PACK_W), chip, core)

    t_len = x.shape[1]
    loss_row, grad_x, g_big, g_small = _local_step(x.reshape(t_len, D_MODEL), p.reshape(t_len, -1),
                                                   loss_target.reshape(t_len, D_MODEL), {"w_in": w_in_all},
                                                   {n: wts[n] for n in SMALL}, comm)

    zero = jnp.zeros((), F32)
    sg, sd, snm, snv = _small_reduce_adamw(_pack_small(g_small, loss_row[0, 0]),
                                           _pack_small({n: wts[n] for n in SMALL}, zero),
                                           _pack_small({n: mom[n] for n in SMALL}, zero),
                                           _pack_small({n: var[n] for n in SMALL}, zero))
    (sg, loss), (sd, _), (snm, _), (snv, _) = (_unpack_small(a) for a in (sg, sd, snm, snv))

    halves = [comm.reduced("in"), comm.reduced("rest")]
    g_in, g_rest = [lax.dynamic_update_slice(got, mine[None], (core, 0, 0))
                    for got, mine in zip(_share_halves(halves), halves)]
    g_in, g_rest = g_in.reshape(D_MODEL, IN_SHARD), g_rest.reshape(REST_ROWS, PACK_W)

    def adam_f(wv, gv, mv, vv):
        return _adamw(wv, gv, mv, vv)

    d_in, nm_in, nv_in = _rowwise("adamw_in", adam_f, D_MODEL, IN_TILE,
                                  [(wts["w_in"][0], IN_SHARD, 0), (g_in, IN_SHARD, 0), (mom["w_in"][0], IN_SHARD, 0),
                                   (var["w_in"][0], IN_SHARD, 0)], [], [(IN_SHARD, F32)] * 3)
    d_rest, nm_rest, nv_rest = _rowwise("adamw_rest", adam_f, REST_ROWS, REST_TILE,
                                        [(rest_shard, PACK_W, 0), (g_rest, PACK_W, 0),
                                         (_pack_rest({n: mom[n][0] for n in REST}), PACK_W, 0),
                                         (_pack_rest({n: var[n][0] for n in REST}), PACK_W, 0)], [],
                                        [(PACK_W, F32)] * 3)
    bg, bd, bnm, bnv = (dict(_unpack_rest(rest), w_in=a.reshape(1, D_MODEL, IN_SHARD))
                        for rest, a in ((g_rest, g_in), (d_rest, d_in), (nm_rest, nm_in), (nv_rest, nv_in)))

    outs = [loss, grad_x.reshape(x.shape)]
    for small, big in ((sg, bg), (sd, bd), (snm, bnm), (snv, bnv)):
        outs += [big[n] if n in BIG else small[n] for n in WEIGHTS]
    return tuple(outs)
```

```python
import functools
from typing import Callable, NamedTuple

import jax
import jax.numpy as jnp
from jax import lax
from jax.experimental import pallas as pl
from jax.experimental.pallas import tpu as pltpu

F32 = jnp.float32
MXU_DTYPE = jnp.bfloat16
WIRE_DTYPE = jnp.bfloat16
NORM_EPS = 1e-6
D_MODEL = 1024
HG_HEADS = 8
HG_DIM = 128
HG_CHUNK = 64
S5_WIDTH = 512
S5_GROUPS = 32
S5_GROUP = 16
S5_STATE = 64
S5_LANES = S5_GROUPS * S5_STATE
IN_COLS = 7168
SUBLANES = 8
VMEM_LIMIT = 56 * 1024 * 1024
HIGHEST = lax.Precision.HIGHEST
MESH = pl.DeviceIdType.MESH

ADAM_LR, ADAM_B1, ADAM_B2, ADAM_EPS, ADAM_WD, ADAM_STEP = 0.001, 0.9, 0.999, 1e-08, 0.01, 10

BIG = ("w_in", "w_o_hg", "w_glu", "w_o_s5", "w_out", "w_ple", "w_ple_gate")
BIG_SHAPE = {"w_in": (1024, 7168), "w_o_hg": (1024, 1024), "w_glu": (512, 1024), "w_o_s5": (512, 1024),
             "w_out": (1024, 1024), "w_ple": (256, 1024), "w_ple_gate": (1024, 1024)}
BIG_COL_SHARDED = ("w_in", "w_glu", "w_o_s5", "w_ple")
SMALL = ("norm_g", "hg_lb", "hg_norm_g", "s5_a_re", "s5_a_im", "s5_log_dt", "s5_b_re", "s5_b_im", "s5_c_re",
         "s5_c_im", "s5_d", "b_glu", "ple_norm_g", "final_norm_g")
SMALL_SHAPE = {"norm_g": (1, 1024), "hg_lb": (2, 1024), "hg_norm_g": (1, 1024), "s5_a_re": (1, 32, 64),
               "s5_a_im": (1, 32, 64), "s5_log_dt": (1, 32), "s5_b_re": (1, 32, 64, 16), "s5_b_im": (1, 32, 64, 16),
               "s5_c_re": (1, 32, 16, 64), "s5_c_im": (1, 32, 16, 64), "s5_d": (1, 32, 16), "b_glu": (1, 1024),
               "ple_norm_g": (1, 1024), "final_norm_g": (1024,)}
WEIGHTS = ("norm_g", "w_in", "hg_lb", "hg_norm_g", "w_o_hg", "s5_a_re", "s5_a_im", "s5_log_dt", "s5_b_re", "s5_b_im",
           "s5_c_re", "s5_c_im", "s5_d", "w_glu", "b_glu", "w_o_s5", "w_out", "ple_norm_g", "w_ple", "w_ple_gate",
           "final_norm_g")
N_CHIPS = 4
N_DEV = 8
PACK_W = 1024
SHARD_ROWS = sum(BIG_SHAPE[n][0] * BIG_SHAPE[n][1] for n in BIG) // (N_CHIPS * PACK_W)
HALF_ROWS = SHARD_ROWS // 2
SMALL_ROWS = 144


def _params(*sem):
    return pltpu.CompilerParams(dimension_semantics=sem, vmem_limit_bytes=VMEM_LIMIT)


def _sig(x):
    return 1.0 / (1.0 + jnp.exp(-x))


def _dsilu(z, s):
    return s * (1.0 + z * (1.0 - s))


def _mx(x):
    return x.astype(MXU_DTYPE)


def _dot(a, b, dims=(((1,), (0,)), ((), ()))):
    return lax.dot_general(_mx(a), _mx(b), dims, preferred_element_type=F32)


_NT = (((1,), (1,)), ((), ()))
_TN = (((0,), (0,)), ((), ()))


def _dot32(a, b):
    return jnp.dot(a, b, precision=HIGHEST, preferred_element_type=F32)


def _rms_bwd(dy, x, g):
    r = lax.rsqrt(jnp.mean(x * x, axis=-1, keepdims=True) + NORM_EPS)
    t = dy * g
    dx = r * t - x * (r * r * r) * jnp.mean(t * x, axis=-1, keepdims=True)
    return dx, jnp.sum(dy * x * r, axis=0, keepdims=True)


def _rowwise(name, fn, n_rows_total, tm, rows, consts, outs, accs=(), alias=None):
    n_r, n_c, n_o, n_a = len(rows), len(consts), len(outs), len(accs)

    def body(*refs):
        row_refs = refs[:n_r]
        const_refs = refs[n_r:n_r + n_c]
        pos = n_r + n_c + (1 if alias is not None else 0)
        out_refs = refs[pos:pos + n_o]
        acc_refs = refs[pos + n_o:pos + n_o + n_a]
        res = fn(*[r[...] for r in row_refs], *[r[...] for r in const_refs])
        for r, v in zip(out_refs, res[:n_o]):
            r[...] = v.astype(r.dtype)
        if n_a:
            @pl.when(pl.program_id(0) == 0)
            def _():
                for r in acc_refs:
                    r[...] = jnp.zeros_like(r)
            for r, v in zip(acc_refs, res[n_o:]):
                r[...] += v

    in_specs = [pl.BlockSpec((tm, w), functools.partial(lambda i, cb: (i, cb), cb=cb)) for (_, w, cb) in rows]
    in_specs += [pl.BlockSpec(c.shape, lambda i: (0, 0)) for c in consts]
    args = [a for (a, _, _) in rows] + list(consts)
    out_shape, out_specs = [], []
    for o in outs:
        w, dt = o[0], o[1]
        cb, total = (o[2], o[3]) if len(o) == 4 else (0, w)
        out_shape.append(jax.ShapeDtypeStruct((n_rows_total, total), dt))
        out_specs.append(pl.BlockSpec((tm, w), functools.partial(lambda i, cb: (i, cb), cb=cb)))
    io_alias = {}
    if alias is not None:
        in_specs.append(pl.BlockSpec(memory_space=pl.ANY))
        args.append(alias[0])
        io_alias = {len(args) - 1: alias[1]}
    for (r, w) in accs:
        out_shape.append(jax.ShapeDtypeStruct((r, w), F32))
        out_specs.append(pl.BlockSpec((r, w), lambda i: (0, 0)))
    res = pl.pallas_call(body, name=name, grid=(n_rows_total // tm,), in_specs=in_specs, out_specs=out_specs,
                         out_shape=out_shape, input_output_aliases=io_alias,
                         compiler_params=_params("arbitrary"))(*args)
    return res


class _Riding(NamedTuple):
    ins: tuple
    outs: tuple
    n_sems: int
    start: Callable
    wait: Callable


_HBM = pl.BlockSpec(memory_space=pl.ANY)


def _ride(riding, refs, n_in, n_out, n_scratch, first, last):
    if riding is None:
        return refs[:n_in], refs[n_in:n_in + n_out], refs[n_in + n_out:]
    r_in, r_out = len(riding.ins), len(riding.outs)
    ins, rins = refs[:n_in], refs[n_in:n_in + r_in]
    pos = n_in + r_in
    outs, routs = refs[pos:pos + n_out], refs[pos + n_out:pos + n_out + r_out]
    pos += n_out + r_out
    scratch, (send_sems, recv_sems) = refs[pos:pos + n_scratch], refs[pos + n_scratch:]

    @pl.when(first)
    def _():
        riding.start(rins, routs, send_sems, recv_sems)

    @pl.when(last)
    def _():
        riding.wait(rins, routs, send_sems, recv_sems)

    return ins, outs, scratch


def _riding_call(riding, body, name, grid, in_specs, args, out_specs, out_shape, scratch, io_alias=None):
    if riding is not None:
        in_specs = list(in_specs) + [_HBM] * len(riding.ins)
        args = list(args) + list(riding.ins)
        out_specs = list(out_specs) + [_HBM] * len(riding.outs)
        out_shape = list(out_shape) + list(riding.outs)
        scratch = list(scratch) + [pltpu.SemaphoreType.DMA((riding.n_sems,))] * 2
    return pl.pallas_call(body, name=name, grid=grid, in_specs=in_specs, out_specs=out_specs, out_shape=out_shape,
                          scratch_shapes=scratch, input_output_aliases=io_alias or {},
                          compiler_params=_params(*(["arbitrary"] * len(grid))))(*args)


def _mm_nn(name, a, b, tm, tn, riding=None, prologue=None, consts=()):
    m, k = a.shape
    n = b.shape[1] if b.ndim == 2 else b.shape[0] * b.shape[2]
    grid = (n // tn, m // tm)
    n_out, scratch = (1, []) if prologue is None else (2, [pltpu.VMEM((m, k), MXU_DTYPE)])

    def body(*refs):
        j, i = pl.program_id(0), pl.program_id(1)
        ins, outs, kept = _ride(riding, refs, 2 + len(consts), n_out, len(scratch), (j == 0) & (i == 0),
                                (j == grid[0] - 1) & (i == grid[1] - 1))
        if prologue is None:
            left = ins[0][...]
        else:
            rows = pl.ds(pl.multiple_of(i * tm, tm), tm)

            @pl.when(j == 0)
            def _():
                tile = _mx(prologue(ins[0][...], *[c[...] for c in ins[2:]]))
                kept[0][rows, :] = tile
                outs[1][...] = tile

            left = kept[0][rows, :]
        outs[0][...] = _dot(left, ins[1][...])

    once = (lambda j, i: (i, 0)) if prologue is None else (lambda j, i: (jnp.where(j == 0, i, grid[1] - 1), 0))
    b_spec = (pl.BlockSpec((k, tn), lambda j, i: (0, j)) if b.ndim == 2
              else pl.BlockSpec((None, k, tn), lambda j, i: (j, 0, 0)))
    in_specs = [pl.BlockSpec((tm, k), once), b_spec]
    in_specs += [pl.BlockSpec(c.shape, lambda j, i: (0, 0)) for c in consts]
    out_specs = [pl.BlockSpec((tm, tn), lambda j, i: (i, j))]
    out_shape = [jax.ShapeDtypeStruct((m, n), F32)]
    if prologue is not None:
        out_specs.append(pl.BlockSpec((tm, k), once))
        out_shape.append(jax.ShapeDtypeStruct((m, k), MXU_DTYPE))
    res = _riding_call(riding, body, name, grid, in_specs, [a, b] + list(consts), out_specs, out_shape, scratch)
    return res[0] if riding is None and prologue is None else res


def _mm_nt(name, a, b, tm, tn):
    m, n = a.shape
    k = b.shape[0]
    steps = n // tn

    def body(a_ref, b_ref, o_ref, acc_ref):
        s = pl.program_id(1)

        @pl.when(s == 0)
        def _():
            acc_ref[...] = jnp.zeros_like(acc_ref)

        acc_ref[...] += _dot(a_ref[...], b_ref[...], _NT)

        @pl.when(s == steps - 1)
        def _():
            o_ref[...] = acc_ref[...]

    return pl.pallas_call(body, name=name, grid=(m // tm, steps),
                          in_specs=[pl.BlockSpec((tm, tn), lambda i, s: (i, s)),
                                    pl.BlockSpec((k, tn), lambda i, s: (0, s))],
                          out_specs=pl.BlockSpec((tm, k), lambda i, s: (i, 0)),
                          out_shape=jax.ShapeDtypeStruct((m, k), F32),
                          scratch_shapes=[pltpu.VMEM((tm, k), F32)],
                          compiler_params=_params("arbitrary", "arbitrary"))(a, b)


def _mm_nt_then(name, a, b, tm, tn, fn, rows, consts, outs, accs=(), alias=None, riding=None):
    m, n = a.shape
    k = b.shape[-2]
    steps = n // tn
    n_r, n_c, n_o, n_a = len(rows), len(consts), len(outs), len(accs)

    def body(*refs):
        a_ref, b_ref = refs[:2]
        row_refs = refs[2:2 + n_r]
        const_refs = refs[2 + n_r:2 + n_r + n_c]
        i, s = pl.program_id(0), pl.program_id(1)
        n_in = 2 + n_r + n_c + (1 if alias is not None else 0)
        _, outs_, (mm_ref,) = _ride(riding, refs, n_in, n_o + n_a, 1, (i == 0) & (s == 0),
                                    (i == m // tm - 1) & (s == steps - 1))
        out_refs, acc_refs = outs_[:n_o], outs_[n_o:]
        part = _dot(a_ref[...], b_ref[...], _NT)
        if steps > 1:
            @pl.when(s == 0)
            def _():
                mm_ref[...] = jnp.zeros_like(mm_ref)
            mm_ref[...] += part

        @pl.when(s == steps - 1)
        def _():
            res = fn(mm_ref[...] if steps > 1 else part, *[r[...] for r in row_refs], *[r[...] for r in const_refs])
            for r, v in zip(out_refs, res[:n_o]):
                r[...] = v.astype(r.dtype)
            if n_a:
                @pl.when(i == 0)
                def _():
                    for r in acc_refs:
                        r[...] = jnp.zeros_like(r)
                for r, v in zip(acc_refs, res[n_o:]):
                    r[...] += v

    b_spec = (pl.BlockSpec((k, tn), lambda i, s: (0, s)) if b.ndim == 2
              else pl.BlockSpec((None, k, tn), lambda i, s: (s, 0, 0)))
    in_specs = [pl.BlockSpec((tm, tn), lambda i, s: (i, s)), b_spec]
    in_specs += [pl.BlockSpec((tm, w), functools.partial(lambda i, s, cb: (i, cb), cb=cb)) for (_, w, cb) in rows]
    in_specs += [pl.BlockSpec(c.shape, lambda i, s: (0, 0)) for c in consts]
    args = [a, b] + [r[0] for r in rows] + list(consts)
    out_shape, out_specs = [], []
    for o in outs:
        w, dt = o[0], o[1]
        cb, total = (o[2], o[3]) if len(o) == 4 else (0, w)
        out_shape.append(jax.ShapeDtypeStruct((m, total), dt))
        out_specs.append(pl.BlockSpec((tm, w), functools.partial(lambda i, s, cb: (i, cb), cb=cb)))
    io_alias = {}
    if alias is not None:
        in_specs.append(pl.BlockSpec(memory_space=pl.ANY))
        args.append(alias[0])
        io_alias = {len(args) - 1: alias[1]}
    for (r, w) in accs:
        out_shape.append(jax.ShapeDtypeStruct((r, w), F32))
        out_specs.append(pl.BlockSpec((r, w), lambda i, s: (0, 0)))
    return _riding_call(riding, body, name, (m // tm, steps), in_specs, args, out_specs, out_shape,
                        [pltpu.VMEM((tm, k), F32)], io_alias)


def _mm_tn(name, a, b, tk, tn, col_shards=False, riding=None):
    t, k = a.shape
    n = b.shape[1]
    steps = t // tk

    def body(*refs):
        j, s = pl.program_id(0), pl.program_id(1)
        (a_ref, b_ref), (o_ref,), (acc_ref,) = _ride(riding, refs, 2, 1, 1, (j == 0) & (s == 0),
                                                     (j == n // tn - 1) & (s == steps - 1))

        @pl.when(s == 0)
        def _():
            acc_ref[...] = jnp.zeros_like(acc_ref)

        acc_ref[...] += _dot(a_ref[...], b_ref[...], _TN)

        @pl.when(s == steps - 1)
        def _():
            o_ref[...] = acc_ref[...]

    if col_shards:
        out_spec = pl.BlockSpec((None, k, tn), lambda j, s: (j, 0, 0))
        out_shape = jax.ShapeDtypeStruct((n // tn, k, tn), F32)
    else:
        out_spec = pl.BlockSpec((k, tn), lambda j, s: (0, j))
        out_shape = jax.ShapeDtypeStruct((k, n), F32)
    res = _riding_call(riding, body, name, (n // tn, steps),
                       [pl.BlockSpec((tk, k), lambda j, s: (s, 0)), pl.BlockSpec((tk, tn), lambda j, s: (s, j))],
                       [a, b], [out_spec], [out_shape], [pltpu.VMEM((k, tn), F32)])
    return res[0] if riding is None else res


def _hg_chunk_terms(q, f, lb):
    sig = _sig(f)
    fv = lb + (1.0 - lb) * sig
    kk = (1.0 - lb) * (1.0 - sig)
    row = lax.broadcasted_iota(jnp.int32, (HG_CHUNK, HG_CHUNK), 0)
    col = lax.broadcasted_iota(jnp.int32, (HG_CHUNK, HG_CHUNK), 1)
    b = _dot32((row >= col).astype(F32), jnp.log(fv))
    b_mid = b[HG_CHUNK // 2 - 1:HG_CHUNK // 2, :]
    b_last = b[HG_CHUNK - 1:HG_CHUNK, :]
    e_mid = jnp.exp(b - b_mid)
    e_mid_inv = jnp.exp(b_mid - b)
    e_b = jnp.exp(b)
    e_last = jnp.exp(b_last - b)
    return sig, fv, kk, row >= col, row <= col, q * e_mid, kk * e_mid_inv, e_mid, e_mid_inv, e_b, e_last, jnp.exp(b_last)


def _hgrn2_fwd(proj, hg_lb, hg_norm_g, t_len, tb):
    nck = tb // HG_CHUNK

    def body(p_ref, lb_ref, gn_ref, o_ref, act_ref, sp_ref, st_ref):
        @pl.when(pl.program_id(0) == 0)
        def _():
            st_ref[...] = jnp.zeros_like(st_ref)

        for c in range(nck):
            r = pl.ds(c * HG_CHUNK, HG_CHUNK)
            for h in range(HG_HEADS):
                hs = pl.ds(h * HG_DIM, HG_DIM)
                lb = _sig(lb_ref[0:1, hs] - lb_ref[1:2, hs])
                q = p_ref[r, pl.ds(h * HG_DIM, HG_DIM)]
                f = p_ref[r, pl.ds(1024 + h * HG_DIM, HG_DIM)]
                v = p_ref[r, pl.ds(2048 + h * HG_DIM, HG_DIM)]
                _, _, kk, causal, _, a, bm, _, _, e_b, e_last, dc = _hg_chunk_terms(q, f, lb)
                scores = jnp.where(causal, _dot(a, bm, _NT), 0.0)
                st = st_ref[h]
                o = _dot(scores, v) + _dot(q * e_b, st, _NT)
                sp_ref[h, c] = st
                st_ref[h] = dc * st + _dot(v, kk * e_last, _TN)
                o_ref[r, hs] = o

        for h in range(HG_HEADS):
            hs = pl.ds(h * HG_DIM, HG_DIM)
            o = o_ref[:, hs]
            rr = lax.rsqrt(jnp.mean(o * o, axis=-1, keepdims=True) + NORM_EPS)
            g = p_ref[:, pl.ds(3072 + h * HG_DIM, HG_DIM)]
            act_ref[:, hs] = (o * rr * gn_ref[:, hs] * (g * _sig(g))).astype(act_ref.dtype)

    nb = t_len // tb
    return pl.pallas_call(
        body, name="hgrn2_fwd", grid=(nb,),
        in_specs=[pl.BlockSpec((tb, 4096), lambda i: (i, 0)),
                  pl.BlockSpec((2, 1024), lambda i: (0, 0)),
                  pl.BlockSpec((1, 1024), lambda i: (0, 0))],
        out_specs=[pl.BlockSpec((tb, 1024), lambda i: (i, 0)),
                   pl.BlockSpec((tb, 1024), lambda i: (i, 0)),
                   pl.BlockSpec((HG_HEADS, nck, HG_DIM, HG_DIM), lambda i: (0, i, 0, 0))],
        out_shape=[jax.ShapeDtypeStruct((t_len, 1024), F32),
                   jax.ShapeDtypeStruct((t_len, 1024), MXU_DTYPE),
                   jax.ShapeDtypeStruct((HG_HEADS, t_len // HG_CHUNK, HG_DIM, HG_DIM), F32)],
        scratch_shapes=[pltpu.VMEM((HG_HEADS, HG_DIM, HG_DIM), F32)],
        compiler_params=_params("arbitrary"))(proj, hg_lb, hg_norm_g)


def _hgrn2_bwd(proj, d_o, s_prev, hg_lb, dproj, t_len, tb):
    nck = tb // HG_CHUNK
    nb = t_len // tb

    def body(p_ref, do_ref, sp_ref, lb_ref, _, dp_ref, dlb_ref, ds_ref, acc_ref):
        @pl.when(pl.program_id(0) == 0)
        def _():
            ds_ref[...] = jnp.zeros_like(ds_ref)
            acc_ref[...] = jnp.zeros_like(acc_ref)

        for c in reversed(range(nck)):
            r = pl.ds(c * HG_CHUNK, HG_CHUNK)
            for h in range(HG_HEADS):
                hs = pl.ds(h * HG_DIM, HG_DIM)
                lb = _sig(lb_ref[0:1, hs] - lb_ref[1:2, hs])
                q = p_ref[r, pl.ds(h * HG_DIM, HG_DIM)]
                f = p_ref[r, pl.ds(1024 + h * HG_DIM, HG_DIM)]
                v = p_ref[r, pl.ds(2048 + h * HG_DIM, HG_DIM)]
                do = do_ref[r, hs]
                sig, fv, kk, causal, anti, a, bm, e_mid, e_mid_inv, e_b, e_last, dc = _hg_chunk_terms(q, f, lb)
                qd = q * e_b
                kd = kk * e_last
                st = sp_ref[h, c]
                dst = ds_ref[h]
                scores = jnp.where(causal, _dot(a, bm, _NT), 0.0)
                dscores = jnp.where(causal, _dot(do, v, _NT), 0.0)
                dv = _dot(scores, do, _TN) + _dot(kd, dst, _NT)
                da = _dot(dscores, bm)
                dbm = _dot(dscores, a, _TN)
                dqd = _dot(do, st)
                dkd = _dot(v, dst)
                ddc = jnp.sum(dst * st, axis=0, keepdims=True)
                ds_ref[h] = _dot(do, qd, _TN) + dc * dst
                dq = da * e_mid + dqd * e_b
                dk = dbm * e_mid_inv + dkd * e_last
                db = da * a - dbm * bm + dqd * qd - dkd * kd
                extra = jnp.sum(dkd * kd, axis=0, keepdims=True) + ddc * dc
                dlogf = _dot32(anti.astype(F32), db) + extra
                dfv_k = dlogf / fv - dk
                dp_ref[r, pl.ds(h * HG_DIM, HG_DIM)] = dq
                dp_ref[r, pl.ds(1024 + h * HG_DIM, HG_DIM)] = dfv_k * (1.0 - lb) * sig * (1.0 - sig)
                dp_ref[r, pl.ds(2048 + h * HG_DIM, HG_DIM)] = dv
                acc_ref[:, hs] += jnp.sum(dfv_k * (1.0 - sig), axis=0, keepdims=True)

        @pl.when(pl.program_id(0) == nb - 1)
        def _():
            lb_all = _sig(lb_ref[0:1, :] - lb_ref[1:2, :])
            g0 = acc_ref[...] * lb_all * (1.0 - lb_all)
            dlb_ref[0:1, :] = g0
            dlb_ref[1:2, :] = -g0

    return pl.pallas_call(
        body, name="hgrn2_bwd", grid=(nb,),
        in_specs=[pl.BlockSpec((tb, 3072), lambda i: (nb - 1 - i, 0)),
                  pl.BlockSpec((tb, 1024), lambda i: (nb - 1 - i, 0)),
                  pl.BlockSpec((HG_HEADS, nck, HG_DIM, HG_DIM), lambda i: (0, nb - 1 - i, 0, 0)),
                  pl.BlockSpec((2, 1024), lambda i: (0, 0)),
                  pl.BlockSpec(memory_space=pl.ANY)],
        out_specs=[pl.BlockSpec((tb, 3072), lambda i: (nb - 1 - i, 0)),
                   pl.BlockSpec((2, 1024), lambda i: (0, 0))],
        out_shape=[jax.ShapeDtypeStruct((t_len, IN_COLS), F32), jax.ShapeDtypeStruct((2, 1024), F32)],
        scratch_shapes=[pltpu.VMEM((HG_HEADS, HG_DIM, HG_DIM), F32), pltpu.VMEM((1, 1024), F32)],
        input_output_aliases={4: 0},
        compiler_params=_params("arbitrary"))(proj, d_o, s_prev, hg_lb, dproj)


def _dot01(m01, x):
    m = m01.astype(MXU_DTYPE)
    hi = x.astype(MXU_DTYPE)
    r1 = x - hi.astype(F32)
    mid = r1.astype(MXU_DTYPE)
    lo = (r1 - mid.astype(F32)).astype(MXU_DTYPE)
    dot = lambda v: jnp.dot(m, v, preferred_element_type=F32)
    return dot(hi) + dot(mid) + dot(lo)


def _chunk_rows(x, offset, nck):
    return jnp.concatenate([jnp.broadcast_to(x[c * HG_CHUNK + offset:c * HG_CHUNK + offset + 1, :],
                                             (HG_CHUNK, x.shape[1])) for c in range(nck)], axis=0)


def _hg_block_terms(q, f, lb, tb):
    nck = tb // HG_CHUNK
    sig = _sig(f)
    fv = lb + (1.0 - lb) * sig
    kk = (1.0 - lb) * (1.0 - sig)
    row = lax.broadcasted_iota(jnp.int32, (tb, tb), 0)
    col = lax.broadcasted_iota(jnp.int32, (tb, tb), 1)
    same = jnp.right_shift(row, 6) == jnp.right_shift(col, 6)
    causal, anti = same & (row >= col), same & (row <= col)
    b = _dot01(causal, jnp.log(fv))
    b_mid, b_last = _chunk_rows(b, HG_CHUNK // 2 - 1, nck), _chunk_rows(b, HG_CHUNK - 1, nck)
    e_mid, e_mid_inv = jnp.exp(b - b_mid), jnp.exp(b_mid - b)
    e_b, e_last = jnp.exp(b), jnp.exp(b_last - b)
    dcs = [jnp.exp(b[c * HG_CHUNK + HG_CHUNK - 1:(c + 1) * HG_CHUNK, :]) for c in range(nck)]
    return sig, fv, kk, causal, anti, e_mid, e_mid_inv, e_b, e_last, dcs


def _hgrn2_fwd2(proj, hg_lb, hg_norm_g, t_len, tb):
    nck = tb // HG_CHUNK

    def body(p_ref, lb_ref, gn_ref, o_ref, act_ref, sp_ref, st_ref, a_s, bm_s, qd_s, kd_s, v_s):
        @pl.when(pl.program_id(0) == 0)
        def _():
            st_ref[...] = jnp.zeros_like(st_ref)

        lb = _sig(lb_ref[0:1, :] - lb_ref[1:2, :])
        q = p_ref[:, pl.ds(0, 1024)]
        _, _, kk, causal, _, e_mid, e_mid_inv, e_b, e_last, dcs = _hg_block_terms(q, p_ref[:, pl.ds(1024, 1024)],
                                                                                   lb, tb)
        a_s[...] = _mx(q * e_mid)
        bm_s[...] = _mx(kk * e_mid_inv)
        qd_s[...] = _mx(q * e_b)
        kd_s[...] = _mx(kk * e_last)
        v_s[...] = _mx(p_ref[:, pl.ds(2048, 1024)])
        for h in range(HG_HEADS):
            hs = pl.ds(h * HG_DIM, HG_DIM)
            scores = jnp.where(causal, _dot(a_s[:, hs], bm_s[:, hs], _NT), 0.0)
            o_ref[:, hs] = _dot(scores, v_s[:, hs])
        for h in range(HG_HEADS):
            hs = pl.ds(h * HG_DIM, HG_DIM)
            incs = [_dot(v_s[pl.ds(c * HG_CHUNK, HG_CHUNK), hs], kd_s[pl.ds(c * HG_CHUNK, HG_CHUNK), hs], _TN)
                    for c in range(nck)]
            st = st_ref[h]
            for c in range(nck):
                sp_ref[h, c] = st
                st = dcs[c][:, h * HG_DIM:(h + 1) * HG_DIM] * st + incs[c]
            st_ref[h] = st
        for h in range(HG_HEADS):
            hs = pl.ds(h * HG_DIM, HG_DIM)
            for c in range(nck):
                r = pl.ds(c * HG_CHUNK, HG_CHUNK)
                o_ref[r, hs] += _dot(qd_s[r, hs], sp_ref[h, c], _NT)
        for h in range(HG_HEADS):
            hs = pl.ds(h * HG_DIM, HG_DIM)
            o = o_ref[:, hs]
            rr = lax.rsqrt(jnp.mean(o * o, axis=-1, keepdims=True) + NORM_EPS)
            g = p_ref[:, pl.ds(3072 + h * HG_DIM, HG_DIM)]
            act_ref[:, hs] = (o * rr * gn_ref[:, hs] * (g * _sig(g))).astype(act_ref.dtype)

    nb = t_len // tb
    return pl.pallas_call(
        body, name="hgrn2_fwd", grid=(nb,),
        in_specs=[pl.BlockSpec((tb, 4096), lambda i: (i, 0)),
                  pl.BlockSpec((2, 1024), lambda i: (0, 0)),
                  pl.BlockSpec((1, 1024), lambda i: (0, 0))],
        out_specs=[pl.BlockSpec((tb, 1024), lambda i: (i, 0)),
                   pl.BlockSpec((tb, 1024), lambda i: (i, 0)),
                   pl.BlockSpec((HG_HEADS, nck, HG_DIM, HG_DIM), lambda i: (0, i, 0, 0))],
        out_shape=[jax.ShapeDtypeStruct((t_len, 1024), F32),
                   jax.ShapeDtypeStruct((t_len, 1024), MXU_DTYPE),
                   jax.ShapeDtypeStruct((HG_HEADS, t_len // HG_CHUNK, HG_DIM, HG_DIM), F32)],
        scratch_shapes=[pltpu.VMEM((HG_HEADS, HG_DIM, HG_DIM), F32)] + [pltpu.VMEM((tb, 1024), MXU_DTYPE)] * 5,
        compiler_params=_params("arbitrary"))(proj, hg_lb, hg_norm_g)


def _hgrn2_bwd2(proj, d_o, s_prev, hg_lb, dproj, t_len, tb, riding=None):
    nck = tb // HG_CHUNK
    nb = t_len // tb

    def body(*refs):
        step = pl.program_id(0)
        ((p_ref, do_ref, sp_ref, lb_ref, _), (dp_ref, dlb_ref),
         (ds_ref, acc_ref, a_s, bm_s, qd_s, kd_s, v_s, do_s, da_s, dbm_s, dqd_s, dkd_s, dv_s, ex_s)) = _ride(
            riding, refs, 5, 2, 14, step == 0, step == nb - 1)

        @pl.when(pl.program_id(0) == 0)
        def _():
            ds_ref[...] = jnp.zeros_like(ds_ref)
            acc_ref[...] = jnp.zeros_like(acc_ref)

        lb = _sig(lb_ref[0:1, :] - lb_ref[1:2, :])
        q = p_ref[:, pl.ds(0, 1024)]
        sig, fv, kk, causal, anti, e_mid, e_mid_inv, e_b, e_last, dcs = _hg_block_terms(
            q, p_ref[:, pl.ds(1024, 1024)], lb, tb)
        a, bm, qd, kd = q * e_mid, kk * e_mid_inv, q * e_b, kk * e_last
        a_s[...] = _mx(a)
        bm_s[...] = _mx(bm)
        qd_s[...] = _mx(qd)
        kd_s[...] = _mx(kd)
        v_s[...] = _mx(p_ref[:, pl.ds(2048, 1024)])
        do_s[...] = _mx(do_ref[...])
        for h in range(HG_HEADS):
            hs = pl.ds(h * HG_DIM, HG_DIM)
            scores = jnp.where(causal, _dot(a_s[:, hs], bm_s[:, hs], _NT), 0.0)
            dscores = _mx(jnp.where(causal, _dot(do_s[:, hs], v_s[:, hs], _NT), 0.0))
            dv_s[:, hs] = _dot(scores, do_s[:, hs], _TN)
            da_s[:, hs] = _dot(dscores, bm_s[:, hs])
            dbm_s[:, hs] = _dot(dscores, a_s[:, hs], _TN)
        for h in range(HG_HEADS):
            hs = pl.ds(h * HG_DIM, HG_DIM)
            ups = [_dot(do_s[pl.ds(c * HG_CHUNK, HG_CHUNK), hs], qd_s[pl.ds(c * HG_CHUNK, HG_CHUNK), hs], _TN)
                   for c in range(nck)]
            dst = ds_ref[h]
            for c in reversed(range(nck)):
                r = pl.ds(c * HG_CHUNK, HG_CHUNK)
                st = sp_ref[h, c]
                dc = dcs[c][:, h * HG_DIM:(h + 1) * HG_DIM]
                dv_s[r, hs] += _dot(kd_s[r, hs], dst, _NT)
                dqd_s[r, hs] = _dot(do_s[r, hs], st)
                dkd_s[r, hs] = _dot(v_s[r, hs], dst)
                ex_s[c:c + 1, hs] = jnp.sum(dst * st, axis=0, keepdims=True) * dc
                dst = ups[c] + dc * dst
            ds_ref[h] = dst
        da, dbm, dqd, dkd = da_s[...], dbm_s[...], dqd_s[...], dkd_s[...]
        dq = da * e_mid + dqd * e_b
        dk = dbm * e_mid_inv + dkd * e_last
        db = da * a - dbm * bm + dqd * qd - dkd * kd
        dkk = dkd * kd
        extra = jnp.concatenate(
            [jnp.broadcast_to(jnp.sum(dkk[c * HG_CHUNK:(c + 1) * HG_CHUNK], axis=0, keepdims=True)
                              + ex_s[c:c + 1, :], (HG_CHUNK, 1024)) for c in range(nck)], axis=0)
        dlogf = _dot01(anti, db) + extra
        dfv_k = dlogf / fv - dk
        dp_ref[:, pl.ds(0, 1024)] = dq.astype(dp_ref.dtype)
        dp_ref[:, pl.ds(1024, 1024)] = (dfv_k * (1.0 - lb) * sig * (1.0 - sig)).astype(dp_ref.dtype)
        dp_ref[:, pl.ds(2048, 1024)] = dv_s[...].astype(dp_ref.dtype)
        acc_ref[...] += jnp.sum(dfv_k * (1.0 - sig), axis=0, keepdims=True)

        @pl.when(pl.program_id(0) == nb - 1)
        def _():
            g0 = acc_ref[...] * lb * (1.0 - lb)
            dlb_ref[0:1, :] = g0
            dlb_ref[1:2, :] = -g0

    return _riding_call(
        riding, body, "hgrn2_bwd", (nb,),
        [pl.BlockSpec((tb, 3072), lambda i: (nb - 1 - i, 0)),
         pl.BlockSpec((tb, 1024), lambda i: (nb - 1 - i, 0)),
         pl.BlockSpec((HG_HEADS, nck, HG_DIM, HG_DIM), lambda i: (0, nb - 1 - i, 0, 0)),
         pl.BlockSpec((2, 1024), lambda i: (0, 0)),
         pl.BlockSpec(memory_space=pl.ANY)],
        [proj, d_o, s_prev, hg_lb, dproj],
        [pl.BlockSpec((tb, 3072), lambda i: (nb - 1 - i, 0)), pl.BlockSpec((2, 1024), lambda i: (0, 0))],
        [jax.ShapeDtypeStruct((t_len, IN_COLS), dproj.dtype), jax.ShapeDtypeStruct((2, 1024), F32)],
        [pltpu.VMEM((HG_HEADS, HG_DIM, HG_DIM), F32), pltpu.VMEM((1, 1024), F32)]
        + [pltpu.VMEM((tb, 1024), MXU_DTYPE)] * 6 + [pltpu.VMEM((tb, 1024), F32)] * 5
        + [pltpu.VMEM((SUBLANES, 1024), F32)], {4: 0})


def _s5_prep(a_re, a_im, log_dt, b_re_t, b_im_t):
    def body(ar_ref, ai_ref, ldt_ref, br_ref, bi_ref, lam_ref, pr_ref, pi_ref, bbr_ref, bbi_ref):
        ar, ai = ar_ref[...], ai_ref[...]
        dt = jnp.exp(ldt_ref[...])
        mag = jnp.exp(ar * dt)
        lr, li = mag * jnp.cos(ai * dt), mag * jnp.sin(ai * dt)
        den = ar * ar + ai * ai
        nr = lr - 1.0
        sr = (nr * ar + li * ai) / den
        si = (li * ar - nr * ai) / den
        lam_ref[0:1, :] = lr
        lam_ref[1:2, :] = li
        cr, ci = lr, li
        for i in range(SUBLANES):
            pr_ref[i:i + 1, :] = cr
            pi_ref[i:i + 1, :] = ci
            cr, ci = cr * lr - ci * li, cr * li + ci * lr
        br, bi = br_ref[...], bi_ref[...]
        bbr_ref[...] = sr * br - si * bi
        bbi_ref[...] = sr * bi + si * br

    whole = pl.BlockSpec(memory_space=pltpu.VMEM)
    return pl.pallas_call(
        body, name="s5_prep", in_specs=[whole] * 5, out_specs=[whole] * 5,
        out_shape=[jax.ShapeDtypeStruct((2, S5_LANES), F32), jax.ShapeDtypeStruct((SUBLANES, S5_LANES), F32),
                   jax.ShapeDtypeStruct((SUBLANES, S5_LANES), F32), jax.ShapeDtypeStruct((S5_GROUP, S5_LANES), F32),
                   jax.ShapeDtypeStruct((S5_GROUP, S5_LANES), F32)])(a_re, a_im, log_dt, b_re_t, b_im_t)


def _s5_prep_bwd(a_re, a_im, log_dt, b_re_t, b_im_t, dlam, dbbr, dbbi):
    def body(ar_ref, ai_ref, ldt_ref, br_ref, bi_ref, dlam_ref, dbbr_ref, dbbi_ref,
             dar_ref, dai_ref, dldt_ref, dbr_ref, dbi_ref):
        ar, ai = ar_ref[...], ai_ref[...]
        dt = jnp.exp(ldt_ref[...])
        mag = jnp.exp(ar * dt)
        cs, sn = jnp.cos(ai * dt), jnp.sin(ai * dt)
        lr, li = mag * cs, mag * sn
        den = ar * ar + ai * ai
        nr = lr - 1.0
        sr = (nr * ar + li * ai) / den
        si = (li * ar - nr * ai) / den
        br, bi = br_ref[...], bi_ref[...]
        gbr, gbi = dbbr_ref[...], dbbi_ref[...]
        dbr_ref[...] = sr * gbr + si * gbi
        dbi_ref[...] = sr * gbi - si * gbr
        dsr = jnp.sum(gbr * br + gbi * bi, axis=0, keepdims=True)
        dsi = jnp.sum(gbi * br - gbr * bi, axis=0, keepdims=True)
        dnr = (dsr * ar - dsi * ai) / den
        dli = dlam_ref[1:2, :] + (dsr * ai + dsi * ar) / den
        dlr = dlam_ref[0:1, :] + dnr
        dden = -(dsr * sr + dsi * si) / den
        dar = (dsr * nr + dsi * li) / den + dden * 2.0 * ar
        dai = (dsr * li - dsi * nr) / den + dden * 2.0 * ai
        dmag = dlr * cs + dli * sn
        dth = mag * (dli * cs - dlr * sn)
        dar_ref[...] = dar + dmag * mag * dt
        dai_ref[...] = dai + dth * dt
        ddt = (dmag * mag * ar + dth * ai) * dt
        lane = lax.broadcasted_iota(jnp.int32, (S5_LANES, 128), 0) // S5_STATE
        grp = lax.broadcasted_iota(jnp.int32, (S5_LANES, 128), 1)
        dldt_ref[...] = _dot32(jnp.broadcast_to(ddt, (SUBLANES, S5_LANES)), (lane == grp).astype(F32))

    whole = pl.BlockSpec(memory_space=pltpu.VMEM)
    return pl.pallas_call(
        body, name="s5_prep_bwd", in_specs=[whole] * 8, out_specs=[whole] * 5,
        out_shape=[jax.ShapeDtypeStruct((1, S5_LANES), F32), jax.ShapeDtypeStruct((1, S5_LANES), F32),
                   jax.ShapeDtypeStruct((SUBLANES, 128), F32), jax.ShapeDtypeStruct((S5_GROUP, S5_LANES), F32),
                   jax.ShapeDtypeStruct((S5_GROUP, S5_LANES), F32)])(a_re, a_im, log_dt, b_re_t, b_im_t, dlam, dbbr,
                                                                      dbbi)


S5_LANE_CHUNK = 512


def _shift_rows(x, s, rowid):
    if s > 0:
        return jnp.where(rowid >= s, pltpu.roll(x, s, 0), 0.0)
    return jnp.where(rowid < SUBLANES + s, pltpu.roll(x, SUBLANES + s, 0), 0.0)


def _scan8(xr, xi, pr, pi, sign, rowid):
    for s, row in ((1, 0), (2, 1), (4, 3)):
        lr, li = pr[row:row + 1, :], pi[row:row + 1, :]
        sr, si = _shift_rows(xr, sign * s, rowid), _shift_rows(xi, sign * s, rowid)
        xr, xi = xr + lr * sr - li * si, xi + lr * si + li * sr
    return xr, xi


def _s5_fwd(proj, pw_re, pw_im, bbr_bd, bbi_bd, crt_bd, cit_bd, d_row, t_len, tb):
    ngrp = tb // SUBLANES

    def body(u_ref, pr_ref, pi_ref, bbr_ref, bbi_ref, crt_ref, cit_ref, d_ref,
             hr_ref, hi_ref, ypre_ref, ys_ref, cr_ref, ci_ref):
        @pl.when(pl.program_id(0) == 0)
        def _():
            cr_ref[...] = jnp.zeros_like(cr_ref)
            ci_ref[...] = jnp.zeros_like(ci_ref)

        u = u_ref[...]
        hr_ref[...] = _dot(u, bbr_ref[...])
        hi_ref[...] = _dot(u, bbi_ref[...])
        rowid = lax.broadcasted_iota(jnp.int32, (SUBLANES, S5_LANE_CHUNK), 0)
        for lc in range(S5_LANES // S5_LANE_CHUNK):
            ls = pl.ds(lc * S5_LANE_CHUNK, S5_LANE_CHUNK)
            pr, pi = pr_ref[:, ls], pi_ref[:, ls]

            def group(g, carry, ls=ls, pr=pr, pi=pi):
                cr, ci = carry
                r = pl.ds(pl.multiple_of(g * SUBLANES, SUBLANES), SUBLANES)
                xr, xi = _scan8(hr_ref[r, ls], hi_ref[r, ls], pr, pi, 1, rowid)
                xr, xi = xr + pr * cr - pi * ci, xi + pr * ci + pi * cr
                hr_ref[r, ls] = xr
                hi_ref[r, ls] = xi
                return xr[SUBLANES - 1:SUBLANES, :], xi[SUBLANES - 1:SUBLANES, :]

            cr, ci = lax.fori_loop(0, ngrp, group, (cr_ref[:, ls], ci_ref[:, ls]))
            cr_ref[:, ls] = cr
            ci_ref[:, ls] = ci
        y = _dot(hr_ref[...], crt_ref[...]) - _dot(hi_ref[...], cit_ref[...]) + d_ref[...] * u
        ypre_ref[...] = y
        ys_ref[...] = jax.nn.gelu(y, approximate=True).astype(ys_ref.dtype)

    whole = pl.BlockSpec(memory_space=pltpu.VMEM)
    return pl.pallas_call(
        body, name="s5_fwd", grid=(t_len // tb,),
        in_specs=[pl.BlockSpec((tb, S5_WIDTH), lambda i: (i, 4096 // S5_WIDTH))] + [whole] * 7,
        out_specs=[pl.BlockSpec((tb, S5_LANES), lambda i: (i, 0)), pl.BlockSpec((tb, S5_LANES), lambda i: (i, 0)),
                   pl.BlockSpec((tb, S5_WIDTH), lambda i: (i, 0)), pl.BlockSpec((tb, S5_WIDTH), lambda i: (i, 0))],
        out_shape=[jax.ShapeDtypeStruct((t_len, S5_LANES), F32), jax.ShapeDtypeStruct((t_len, S5_LANES), F32),
                   jax.ShapeDtypeStruct((t_len, S5_WIDTH), F32), jax.ShapeDtypeStruct((t_len, S5_WIDTH), MXU_DTYPE)],
        scratch_shapes=[pltpu.VMEM((1, S5_LANES), F32), pltpu.VMEM((1, S5_LANES), F32)],
        compiler_params=_params("arbitrary"))(proj, pw_re, pw_im, bbr_bd, bbi_bd, crt_bd, cit_bd, d_row)


def _dgelu(x):
    c, a = 0.7978845608028654, 0.044715
    th = jnp.tanh(c * (x + a * x * x * x))
    return 0.5 * (1.0 + th) + 0.5 * x * (1.0 - th * th) * c * (1.0 + 3.0 * a * x * x)


def _s5_bwd(dgelu, y_pre, proj, h_re, h_im, pwr_re, pwr_im, bbr_bd, bbi_bd, cr_bd, ci_bd, d_row, dproj, t_len, tb):
    ngrp = tb // SUBLANES
    nb = t_len // tb

    def body(dg_ref, yp_ref, u_ref, hr_ref, hi_ref, pr_ref, pi_ref, bbr_ref, bbi_ref, cr_ref, ci_ref, d_ref, _,
             du_ref, dbbr_ref, dbbi_ref, dcr_ref, dci_ref, dd_ref, dlam_ref,
             gr_ref, gi_ref, car_ref, cai_ref, abr_ref, abi_ref, acr_ref, aci_ref, ad_ref, alr_ref, ali_ref, sem):
        @pl.when(pl.program_id(0) == 0)
        def _():
            for ref in (car_ref, cai_ref, abr_ref, abi_ref, acr_ref, aci_ref, ad_ref, alr_ref, ali_ref):
                ref[...] = jnp.zeros_like(ref)

        u = u_ref[...]
        dy = dg_ref[...] * _dgelu(yp_ref[...])
        gr_ref[...] = _dot(dy, cr_ref[...])
        gi_ref[...] = -_dot(dy, ci_ref[...])
        rowid = lax.broadcasted_iota(jnp.int32, (SUBLANES, S5_LANE_CHUNK), 0)
        for lc in range(S5_LANES // S5_LANE_CHUNK):
            ls = pl.ds(lc * S5_LANE_CHUNK, S5_LANE_CHUNK)
            pr, pi = pr_ref[:, ls], pi_ref[:, ls]
            fwd_rows_r = jnp.concatenate([pr[7:8], pr[6:7], pr[6:7], pr[4:5]], axis=0)
            fwd_rows_i = jnp.concatenate([pi[7:8], pi[6:7], pi[6:7], pi[4:5]], axis=0)

            def group(j, carry, ls=ls, pr=pr, pi=pi, fr=fwd_rows_r, fi=fwd_rows_i):
                cr, ci, slr, sli = carry
                g = ngrp - 1 - j
                r = pl.ds(pl.multiple_of(g * SUBLANES, SUBLANES), SUBLANES)
                xr, xi = _scan8(gr_ref[r, ls], gi_ref[r, ls], fr, fi, -1, rowid)
                xr, xi = xr + pr * cr - pi * ci, xi + pr * ci + pi * cr
                gr_ref[r, ls] = xr
                gi_ref[r, ls] = xi
                nr = jnp.where(rowid == SUBLANES - 1, cr, pltpu.roll(xr, SUBLANES - 1, 0))
                ni = jnp.where(rowid == SUBLANES - 1, ci, pltpu.roll(xi, SUBLANES - 1, 0))
                hr, hi = hr_ref[r, ls], hi_ref[r, ls]
                slr = slr + nr * hr + ni * hi
                sli = sli + ni * hr - nr * hi
                return xr[0:1, :], xi[0:1, :], slr, sli

            zero = jnp.zeros((SUBLANES, S5_LANE_CHUNK), F32)
            cr, ci, slr, sli = lax.fori_loop(0, ngrp, group, (car_ref[:, ls], cai_ref[:, ls], zero, zero))
            car_ref[:, ls] = cr
            cai_ref[:, ls] = ci
            alr_ref[:, ls] += jnp.sum(slr, axis=0, keepdims=True)
            ali_ref[:, ls] += jnp.sum(sli, axis=0, keepdims=True)
        gr, gi = gr_ref[...], gi_ref[...]
        du_ref[...] = _dot(gr, bbr_ref[...], _NT) + _dot(gi, bbi_ref[...], _NT) + d_ref[...] * dy
        abr_ref[...] += _dot(u, gr, _TN)
        abi_ref[...] += _dot(u, gi, _TN)
        acr_ref[...] += _dot(hr_ref[...], dy, _TN)
        aci_ref[...] -= _dot(hi_ref[...], dy, _TN)
        ad_ref[...] += jnp.sum(dy * u, axis=0, keepdims=True)

        @pl.when(pl.program_id(0) == nb - 1)
        def _():
            dd_ref[...] = ad_ref[...]
            dlam_ref[0:1, :] = alr_ref[...]
            dlam_ref[1:2, :] = ali_ref[...]
            copies = [pltpu.make_async_copy(s, d, sem.at[k]) for k, (s, d) in enumerate(
                ((abr_ref, dbbr_ref), (abi_ref, dbbi_ref), (acr_ref, dcr_ref), (aci_ref, dci_ref)))]
            for cp in copies:
                cp.start()
            for cp in copies:
                cp.wait()

    whole = pl.BlockSpec(memory_space=pltpu.VMEM)
    hbm = pl.BlockSpec(memory_space=pl.ANY)
    rev = lambda i: (nb - 1 - i, 0)
    return pl.pallas_call(
        body, name="s5_bwd", grid=(nb,),
        in_specs=[pl.BlockSpec((tb, S5_WIDTH), rev), pl.BlockSpec((tb, S5_WIDTH), rev),
                  pl.BlockSpec((tb, S5_WIDTH), lambda i: (nb - 1 - i, 4096 // S5_WIDTH)),
                  pl.BlockSpec((tb, S5_LANES), rev), pl.BlockSpec((tb, S5_LANES), rev)] + [whole] * 7 + [hbm],
        out_specs=[pl.BlockSpec((tb, S5_WIDTH), lambda i: (nb - 1 - i, 4096 // S5_WIDTH)), hbm, hbm, hbm, hbm,
                   pl.BlockSpec((1, S5_WIDTH), lambda i: (0, 0)), pl.BlockSpec((2, S5_LANES), lambda i: (0, 0))],
        out_shape=[jax.ShapeDtypeStruct((t_len, IN_COLS), F32),
                   jax.ShapeDtypeStruct((S5_WIDTH, S5_LANES), F32), jax.ShapeDtypeStruct((S5_WIDTH, S5_LANES), F32),
                   jax.ShapeDtypeStruct((S5_LANES, S5_WIDTH), F32), jax.ShapeDtypeStruct((S5_LANES, S5_WIDTH), F32),
                   jax.ShapeDtypeStruct((1, S5_WIDTH), F32), jax.ShapeDtypeStruct((2, S5_LANES), F32)],
        scratch_shapes=[pltpu.VMEM((tb, S5_LANES), F32), pltpu.VMEM((tb, S5_LANES), F32),
                        pltpu.VMEM((1, S5_LANES), F32), pltpu.VMEM((1, S5_LANES), F32),
                        pltpu.VMEM((S5_WIDTH, S5_LANES), F32), pltpu.VMEM((S5_WIDTH, S5_LANES), F32),
                        pltpu.VMEM((S5_LANES, S5_WIDTH), F32), pltpu.VMEM((S5_LANES, S5_WIDTH), F32),
                        pltpu.VMEM((1, S5_WIDTH), F32), pltpu.VMEM((1, S5_LANES), F32),
                        pltpu.VMEM((1, S5_LANES), F32), pltpu.SemaphoreType.DMA((4,))],
        input_output_aliases={12: 0},
        compiler_params=_params("arbitrary"))(dgelu, y_pre, proj, h_re, h_im, pwr_re, pwr_im, bbr_bd, bbi_bd, cr_bd,
                                              ci_bd, d_row, dproj)


S5_BLOCKS = 4
S5_BW = S5_WIDTH // S5_BLOCKS
S5_BL = S5_LANES // S5_BLOCKS
S5_LANE_BLOCKS = S5_LANES // 128
S5_SCAN_BLOCKS = 4


def _s5_powers(a_re, a_im, log_dt, b_re_t, b_im_t, seg):
    def body(ar_ref, ai_ref, ldt_ref, br_ref, bi_ref,
             rows_f, pfr_ref, pfi_ref, rows_r, prr_ref, pri_ref, bbr_ref, bbi_ref):
        ar, ai = ar_ref[...], ai_ref[...]
        dt = jnp.exp(ldt_ref[...])
        mag = jnp.exp(ar * dt)
        lr, li = mag * jnp.cos(ai * dt), mag * jnp.sin(ai * dt)
        den = ar * ar + ai * ai
        nr = lr - 1.0
        sr = (nr * ar + li * ai) / den
        si = (li * ar - nr * ai) / den
        wide = (SUBLANES, S5_LANES)
        cr, ci = lr, li
        for i in range(seg):
            pfr_ref[i] = jnp.broadcast_to(cr, wide)
            pfi_ref[i] = jnp.broadcast_to(ci, wide)
            prr_ref[seg - 1 - i] = jnp.broadcast_to(cr, wide)
            pri_ref[seg - 1 - i] = jnp.broadcast_to(-ci, wide)
            if i == seg - 1:
                for rows, sign in ((rows_f, 1.0), (rows_r, -1.0)):
                    rows[0:1, :] = lr
                    rows[1:2, :] = sign * li
                    rows[2:3, :] = cr
                    rows[3:4, :] = sign * ci
            cr, ci = cr * lr - ci * li, cr * li + ci * lr
        br, bi = br_ref[...], bi_ref[...]
        bbr_ref[...] = sr * br - si * bi
        bbi_ref[...] = sr * bi + si * br

    whole = pl.BlockSpec(memory_space=pltpu.VMEM)
    tables = [jax.ShapeDtypeStruct((4, S5_LANES), F32)] + [jax.ShapeDtypeStruct((seg, SUBLANES, S5_LANES), F32)] * 2
    bbar = [jax.ShapeDtypeStruct((S5_GROUP, S5_LANES), F32)] * 2
    res = pl.pallas_call(body, name="s5_prep", in_specs=[whole] * 5, out_specs=[whole] * 8,
                         out_shape=tables + tables + bbar)(a_re, a_im, log_dt, b_re_t, b_im_t)
    return res[0:3], res[3:6], res[6], res[7]


def _scan_tables(pw_re, pw_im, reverse):
    seg = pw_re.shape[0]
    if reverse:
        pw_re, pw_im = pw_re[::-1], -pw_im[::-1]
        one, full = seg - 1, 0
    else:
        one, full = 0, seg - 1
    rows = jnp.stack([pw_re[one], pw_im[one], pw_re[full], pw_im[full]])
    wide = lambda t: jnp.broadcast_to(t[:, None, :], (seg, SUBLANES, S5_LANES))
    return rows, wide(pw_re), wide(pw_im)


def _lanes(j):
    return pl.ds(j * 128, 128)


def _segment_scan(xr_ref, xi_ref, lam_ref, car_ref, cai_ref, cn_r, cn_i, blocks, seg, reverse):
    shape = (SUBLANES, 128)
    lrs = [jnp.broadcast_to(lam_ref[0:1, _lanes(j)], shape) for j in blocks]
    lis = [jnp.broadcast_to(lam_ref[1:2, _lanes(j)], shape) for j in blocks]

    def step(k, carry):
        idx = pl.ds(seg - 1 - k if reverse else k, SUBLANES, stride=seg)
        out = []
        for n, j in enumerate(blocks):
            cr, ci = carry[2 * n], carry[2 * n + 1]
            nr = lrs[n] * cr - lis[n] * ci + xr_ref[j, idx, :]
            ni = lrs[n] * ci + lis[n] * cr + xi_ref[j, idx, :]
            xr_ref[j, idx, :] = nr
            xi_ref[j, idx, :] = ni
            out += [nr, ni]
        return tuple(out)

    zero = jnp.zeros(shape, F32)
    fin = lax.fori_loop(0, seg, step, (zero,) * (2 * len(blocks)), unroll=2)
    for n, j in enumerate(blocks):
        ls = _lanes(j)
        fr, fi = fin[2 * n], fin[2 * n + 1]
        sr, si = lam_ref[2:3, ls], lam_ref[3:4, ls]
        pr, pi = car_ref[:, ls], cai_ref[:, ls]
        for s in (reversed(range(SUBLANES)) if reverse else range(SUBLANES)):
            cn_r[s:s + 1, ls] = pr
            cn_i[s:s + 1, ls] = pi
            pr, pi = fr[s:s + 1, :] + sr * pr - si * pi, fi[s:s + 1, :] + sr * pi + si * pr
        car_ref[:, ls] = pr
        cai_ref[:, ls] = pi


def _s5_fwd2(proj, lam_rows, p3_re, p3_im, bbr4, bbi4, crt4, cit4, d_row, t_len, tb):
    seg = tb // SUBLANES

    def body(u_ref, lam_ref, p3r_ref, p3i_ref, bbr_ref, bbi_ref, crt_ref, cit_ref, d_ref,
             hr_ref, hi_ref, ypre_ref, ys_ref, car_ref, cai_ref, cn_r, cn_i):
        @pl.when(pl.program_id(0) == 0)
        def _():
            car_ref[...] = jnp.zeros_like(car_ref)
            cai_ref[...] = jnp.zeros_like(cai_ref)

        u = u_ref[...]
        for i in range(S5_BLOCKS):
            ui = u[:, i * S5_BW:(i + 1) * S5_BW]
            xr, xi = _dot(ui, bbr_ref[i]), _dot(ui, bbi_ref[i])
            for jj in range(S5_BL // 128):
                hr_ref[i * (S5_BL // 128) + jj] = xr[:, jj * 128:(jj + 1) * 128]
                hi_ref[i * (S5_BL // 128) + jj] = xi[:, jj * 128:(jj + 1) * 128]
        for lc in range(S5_LANE_BLOCKS // S5_SCAN_BLOCKS):
            blocks = range(lc * S5_SCAN_BLOCKS, (lc + 1) * S5_SCAN_BLOCKS)
            _segment_scan(hr_ref, hi_ref, lam_ref, car_ref, cai_ref, cn_r, cn_i, blocks, seg, False)
            crs = [cn_r[:, _lanes(j)] for j in blocks]
            cis = [cn_i[:, _lanes(j)] for j in blocks]

            def fix(t, carry, blocks=blocks, crs=crs, cis=cis):
                idx = pl.ds(t, SUBLANES, stride=seg)
                for n, j in enumerate(blocks):
                    pr, pi = p3r_ref[t, :, _lanes(j)], p3i_ref[t, :, _lanes(j)]
                    hr_ref[j, idx, :] += pr * crs[n] - pi * cis[n]
                    hi_ref[j, idx, :] += pr * cis[n] + pi * crs[n]
                return carry

            lax.fori_loop(0, seg, fix, 0, unroll=2)
        for i in range(S5_BLOCKS):
            ws = pl.ds(i * S5_BW, S5_BW)
            js = range(i * (S5_BL // 128), (i + 1) * (S5_BL // 128))
            hr = jnp.concatenate([hr_ref[j] for j in js], axis=1)
            hi = jnp.concatenate([hi_ref[j] for j in js], axis=1)
            y = _dot(hr, crt_ref[i]) - _dot(hi, cit_ref[i]) + d_ref[:, ws] * u[:, i * S5_BW:(i + 1) * S5_BW]
            ypre_ref[:, ws] = y
            ys_ref[:, ws] = jax.nn.gelu(y, approximate=True).astype(ys_ref.dtype)

    whole = pl.BlockSpec(memory_space=pltpu.VMEM)
    h_spec = pl.BlockSpec((S5_LANE_BLOCKS, tb, 128), lambda i: (0, i, 0))
    return pl.pallas_call(
        body, name="s5_fwd", grid=(t_len // tb,),
        in_specs=[pl.BlockSpec((tb, S5_WIDTH), lambda i: (i, 4096 // S5_WIDTH))] + [whole] * 8,
        out_specs=[h_spec, h_spec,
                   pl.BlockSpec((tb, S5_WIDTH), lambda i: (i, 0)), pl.BlockSpec((tb, S5_WIDTH), lambda i: (i, 0))],
        out_shape=[jax.ShapeDtypeStruct((S5_LANE_BLOCKS, t_len, 128), F32),
                   jax.ShapeDtypeStruct((S5_LANE_BLOCKS, t_len, 128), F32),
                   jax.ShapeDtypeStruct((t_len, S5_WIDTH), F32), jax.ShapeDtypeStruct((t_len, S5_WIDTH), MXU_DTYPE)],
        scratch_shapes=[pltpu.VMEM((1, S5_LANES), F32), pltpu.VMEM((1, S5_LANES), F32),
                        pltpu.VMEM((SUBLANES, S5_LANES), F32), pltpu.VMEM((SUBLANES, S5_LANES), F32)],
        compiler_params=_params("arbitrary"))(proj, lam_rows, p3_re, p3_im, bbr4, bbi4, crt4, cit4, d_row)


def _s5_bwd2(dgelu, y_pre, proj, h_re, h_im, lam_rows, p3_re, p3_im, bbr4, bbi4, cr4, ci4, d_row, dproj, t_len, tb):
    seg = tb // SUBLANES
    nb = t_len // tb

    def body(dg_ref, yp_ref, u_ref, hr_ref, hi_ref, lam_ref, p3r_ref, p3i_ref, bbr_ref, bbi_ref, cr_ref, ci_ref,
             d_ref, _, du_ref, dbbr_ref, dbbi_ref, dcr_ref, dci_ref, dd_ref, dlam_ref,
             gr_ref, gi_ref, car_ref, cai_ref, cn_r, cn_i):
        @pl.when(pl.program_id(0) == 0)
        def _():
            for ref in (car_ref, cai_ref, dbbr_ref, dbbi_ref, dcr_ref, dci_ref, dd_ref, dlam_ref):
                ref[...] = jnp.zeros_like(ref)

        u = u_ref[...]
        dy = dg_ref[...] * _dgelu(yp_ref[...])
        nlb = S5_BL // 128
        for i in range(S5_BLOCKS):
            dyi = dy[:, i * S5_BW:(i + 1) * S5_BW]
            xr, xi = _dot(dyi, cr_ref[i]), -_dot(dyi, ci_ref[i])
            for jj in range(nlb):
                gr_ref[i * nlb + jj] = xr[:, jj * 128:(jj + 1) * 128]
                gi_ref[i * nlb + jj] = xi[:, jj * 128:(jj + 1) * 128]
        for lc in range(S5_LANE_BLOCKS // S5_SCAN_BLOCKS):
            blocks = range(lc * S5_SCAN_BLOCKS, (lc + 1) * S5_SCAN_BLOCKS)
            _segment_scan(gr_ref, gi_ref, lam_ref, car_ref, cai_ref, cn_r, cn_i, blocks, seg, True)
            crs = [cn_r[:, _lanes(j)] for j in blocks]
            cis = [cn_i[:, _lanes(j)] for j in blocks]

            def fix(k, carry, blocks=blocks, crs=crs, cis=cis):
                t = seg - 1 - k
                idx = pl.ds(t, SUBLANES, stride=seg)
                out = []
                for n, j in enumerate(blocks):
                    nr, ni, slr, sli = carry[4 * n:4 * n + 4]
                    pr, pi = p3r_ref[t, :, _lanes(j)], p3i_ref[t, :, _lanes(j)]
                    g_r = gr_ref[j, idx, :] + pr * crs[n] - pi * cis[n]
                    g_i = gi_ref[j, idx, :] + pr * cis[n] + pi * crs[n]
                    gr_ref[j, idx, :] = g_r
                    gi_ref[j, idx, :] = g_i
                    hr, hi = hr_ref[j, idx, :], hi_ref[j, idx, :]
                    out += [g_r, g_i, slr + nr * hr + ni * hi, sli + ni * hr - nr * hi]
                return tuple(out)

            zero = jnp.zeros((SUBLANES, 128), F32)
            init = []
            for n in range(len(blocks)):
                init += [crs[n], cis[n], zero, zero]
            fin = lax.fori_loop(0, seg, fix, tuple(init), unroll=2)
            for n, j in enumerate(blocks):
                dlam_ref[0:1, _lanes(j)] += jnp.sum(fin[4 * n + 2], axis=0, keepdims=True)
                dlam_ref[1:2, _lanes(j)] += jnp.sum(fin[4 * n + 3], axis=0, keepdims=True)
        for i in range(S5_BLOCKS):
            ws = pl.ds(i * S5_BW, S5_BW)
            js = range(i * nlb, (i + 1) * nlb)
            ui, dyi = u[:, i * S5_BW:(i + 1) * S5_BW], dy[:, i * S5_BW:(i + 1) * S5_BW]
            gr = jnp.concatenate([gr_ref[j] for j in js], axis=1)
            gi = jnp.concatenate([gi_ref[j] for j in js], axis=1)
            du_ref[:, ws] = _dot(gr, bbr_ref[i], _NT) + _dot(gi, bbi_ref[i], _NT) + d_ref[:, ws] * dyi
            dbbr_ref[i] += _dot(ui, gr, _TN)
            dbbi_ref[i] += _dot(ui, gi, _TN)
            dcr_ref[i] += _dot(jnp.concatenate([hr_ref[j] for j in js], axis=1), dyi, _TN)
            dci_ref[i] -= _dot(jnp.concatenate([hi_ref[j] for j in js], axis=1), dyi, _TN)
        dd_ref[...] += jnp.sum(dy * u, axis=0, keepdims=True)

    whole = pl.BlockSpec(memory_space=pltpu.VMEM)
    rev = lambda i: (nb - 1 - i, 0)
    const3 = lambda i: (0, 0, 0)
    h_spec = pl.BlockSpec((S5_LANE_BLOCKS, tb, 128), lambda i: (0, nb - 1 - i, 0))
    return pl.pallas_call(
        body, name="s5_bwd", grid=(nb,),
        in_specs=[pl.BlockSpec((tb, S5_WIDTH), rev), pl.BlockSpec((tb, S5_WIDTH), rev),
                  pl.BlockSpec((tb, S5_WIDTH), lambda i: (nb - 1 - i, 4096 // S5_WIDTH)),
                  h_spec, h_spec] + [whole] * 8
                 + [pl.BlockSpec(memory_space=pl.ANY)],
        out_specs=[pl.BlockSpec((tb, S5_WIDTH), lambda i: (nb - 1 - i, 4096 // S5_WIDTH)),
                   pl.BlockSpec((S5_BLOCKS, S5_BW, S5_BL), const3), pl.BlockSpec((S5_BLOCKS, S5_BW, S5_BL), const3),
                   pl.BlockSpec((S5_BLOCKS, S5_BL, S5_BW), const3), pl.BlockSpec((S5_BLOCKS, S5_BL, S5_BW), const3),
                   pl.BlockSpec((1, S5_WIDTH), lambda i: (0, 0)), pl.BlockSpec((2, S5_LANES), lambda i: (0, 0))],
        out_shape=[jax.ShapeDtypeStruct((t_len, IN_COLS), F32),
                   jax.ShapeDtypeStruct((S5_BLOCKS, S5_BW, S5_BL), F32),
                   jax.ShapeDtypeStruct((S5_BLOCKS, S5_BW, S5_BL), F32),
                   jax.ShapeDtypeStruct((S5_BLOCKS, S5_BL, S5_BW), F32),
                   jax.ShapeDtypeStruct((S5_BLOCKS, S5_BL, S5_BW), F32),
                   jax.ShapeDtypeStruct((1, S5_WIDTH), F32), jax.ShapeDtypeStruct((2, S5_LANES), F32)],
        scratch_shapes=[pltpu.VMEM((S5_LANE_BLOCKS, tb, 128), F32), pltpu.VMEM((S5_LANE_BLOCKS, tb, 128), F32),
                        pltpu.VMEM((1, S5_LANES), F32), pltpu.VMEM((1, S5_LANES), F32),
                        pltpu.VMEM((SUBLANES, S5_LANES), F32), pltpu.VMEM((SUBLANES, S5_LANES), F32)],
        input_output_aliases={13: 0},
        compiler_params=_params("arbitrary"))(dgelu, y_pre, proj, h_re, h_im, lam_rows, p3_re, p3_im, bbr4, bbi4,
                                              cr4, ci4, d_row, dproj)


def _to_segment_order(v, stage_ref, out_ref, seg):
    nbl = v.shape[1] // 128
    for b in range(nbl):
        stage_ref[b] = v[:, b * 128:(b + 1) * 128]

    def body(t, carry):
        rows = pl.ds(pl.multiple_of(t * SUBLANES, SUBLANES), SUBLANES)
        for b in range(nbl):
            out_ref[rows, _lanes(b)] = stage_ref[b, pl.ds(t, SUBLANES, stride=seg), :]
        return carry

    lax.fori_loop(0, seg, body, 0)


def _from_segment_order(v, stage_ref, out_ref, seg):
    nbl = v.shape[1] // 128
    for b in range(nbl):
        stage_ref[b] = v[:, b * 128:(b + 1) * 128]
    for s in range(SUBLANES):
        def body(k, carry, s=s):
            rows = pl.ds(pl.multiple_of(s * seg + k * SUBLANES, SUBLANES), SUBLANES)
            for b in range(nbl):
                out_ref[rows, _lanes(b)] = stage_ref[b, pl.ds(k * SUBLANES * SUBLANES + s, SUBLANES,
                                                              stride=SUBLANES), :]
            return carry

        lax.fori_loop(0, seg // SUBLANES, body, 0)


def _tile_scan(xr_ref, xi_ref, lam_ref, car_ref, cai_ref, cn_r, cn_i, blocks, seg, reverse):
    shape = (SUBLANES, 128)
    lrs = [jnp.broadcast_to(lam_ref[0:1, _lanes(j)], shape) for j in blocks]
    lis = [jnp.broadcast_to(lam_ref[1:2, _lanes(j)], shape) for j in blocks]

    def step(k, carry):
        t = seg - 1 - k if reverse else k
        rows = pl.ds(pl.multiple_of(t * SUBLANES, SUBLANES), SUBLANES)
        out = []
        for n, j in enumerate(blocks):
            cr, ci = carry[2 * n], carry[2 * n + 1]
            nr = lrs[n] * cr - lis[n] * ci + xr_ref[rows, _lanes(j)]
            ni = lrs[n] * ci + lis[n] * cr + xi_ref[rows, _lanes(j)]
            xr_ref[rows, _lanes(j)] = nr
            xi_ref[rows, _lanes(j)] = ni
            out += [nr, ni]
        return tuple(out)

    zero = jnp.zeros(shape, F32)
    fin = lax.fori_loop(0, seg, step, (zero,) * (2 * len(blocks)), unroll=2)
    for n, j in enumerate(blocks):
        ls = _lanes(j)
        fr, fi = fin[2 * n], fin[2 * n + 1]
        sr, si = lam_ref[2:3, ls], lam_ref[3:4, ls]
        pr, pi = car_ref[:, ls], cai_ref[:, ls]
        for s in (reversed(range(SUBLANES)) if reverse else range(SUBLANES)):
            cn_r[s:s + 1, ls] = pr
            cn_i[s:s + 1, ls] = pi
            pr, pi = fr[s:s + 1, :] + sr * pr - si * pi, fi[s:s + 1, :] + sr * pi + si * pr
        car_ref[:, ls] = pr
        cai_ref[:, ls] = pi


def _s5_fwd3(proj, lam_rows, p3_re, p3_im, bbr4, bbi4, crt4, cit4, d_row, t_len, tb):
    seg = tb // SUBLANES

    def body(u_ref, lam_ref, p3r_ref, p3i_ref, bbr_ref, bbi_ref, crt_ref, cit_ref, d_ref,
             hr_ref, hi_ref, ypre_ref, ys_ref, car_ref, cai_ref, cn_r, cn_i, stage_ref, us_ref, yseg_ref):
        @pl.when(pl.program_id(0) == 0)
        def _():
            car_ref[...] = jnp.zeros_like(car_ref)
            cai_ref[...] = jnp.zeros_like(cai_ref)

        _to_segment_order(u_ref[...], stage_ref, us_ref, seg)
        u = us_ref[...]
        for i in range(S5_BLOCKS):
            ui = u[:, i * S5_BW:(i + 1) * S5_BW]
            hr_ref[:, pl.ds(i * S5_BL, S5_BL)] = _dot(ui, bbr_ref[i])
            hi_ref[:, pl.ds(i * S5_BL, S5_BL)] = _dot(ui, bbi_ref[i])
        for lc in range(S5_LANE_BLOCKS // S5_SCAN_BLOCKS):
            blocks = range(lc * S5_SCAN_BLOCKS, (lc + 1) * S5_SCAN_BLOCKS)
            _tile_scan(hr_ref, hi_ref, lam_ref, car_ref, cai_ref, cn_r, cn_i, blocks, seg, False)
            crs = [cn_r[:, _lanes(j)] for j in blocks]
            cis = [cn_i[:, _lanes(j)] for j in blocks]

            def fix(t, carry, blocks=blocks, crs=crs, cis=cis):
                rows = pl.ds(pl.multiple_of(t * SUBLANES, SUBLANES), SUBLANES)
                for n, j in enumerate(blocks):
                    pr, pi = p3r_ref[t, :, _lanes(j)], p3i_ref[t, :, _lanes(j)]
                    hr_ref[rows, _lanes(j)] += pr * crs[n] - pi * cis[n]
                    hi_ref[rows, _lanes(j)] += pr * cis[n] + pi * crs[n]
                return carry

            lax.fori_loop(0, seg, fix, 0, unroll=2)
        for i in range(S5_BLOCKS):
            ws = pl.ds(i * S5_BW, S5_BW)
            bl = pl.ds(i * S5_BL, S5_BL)
            yseg_ref[:, ws] = (_dot(hr_ref[:, bl], crt_ref[i]) - _dot(hi_ref[:, bl], cit_ref[i])
                               + d_ref[:, ws] * u[:, i * S5_BW:(i + 1) * S5_BW])
        _from_segment_order(yseg_ref[...], stage_ref, ypre_ref, seg)
        ys_ref[...] = jax.nn.gelu(ypre_ref[...], approximate=True).astype(ys_ref.dtype)

    whole = pl.BlockSpec(memory_space=pltpu.VMEM)
    return pl.pallas_call(
        body, name="s5_fwd", grid=(t_len // tb,),
        in_specs=[pl.BlockSpec((tb, S5_WIDTH), lambda i: (i, 4096 // S5_WIDTH))] + [whole] * 8,
        out_specs=[pl.BlockSpec((tb, S5_LANES), lambda i: (i, 0)), pl.BlockSpec((tb, S5_LANES), lambda i: (i, 0)),
                   pl.BlockSpec((tb, S5_WIDTH), lambda i: (i, 0)), pl.BlockSpec((tb, S5_WIDTH), lambda i: (i, 0))],
        out_shape=[jax.ShapeDtypeStruct((t_len, S5_LANES), F32), jax.ShapeDtypeStruct((t_len, S5_LANES), F32),
                   jax.ShapeDtypeStruct((t_len, S5_WIDTH), F32), jax.ShapeDtypeStruct((t_len, S5_WIDTH), MXU_DTYPE)],
        scratch_shapes=[pltpu.VMEM((1, S5_LANES), F32), pltpu.VMEM((1, S5_LANES), F32),
                        pltpu.VMEM((SUBLANES, S5_LANES), F32), pltpu.VMEM((SUBLANES, S5_LANES), F32),
                        pltpu.VMEM((S5_WIDTH // 128, tb, 128), F32), pltpu.VMEM((tb, S5_WIDTH), F32),
                        pltpu.VMEM((tb, S5_WIDTH), F32)],
        compiler_params=_params("arbitrary"))(proj, lam_rows, p3_re, p3_im, bbr4, bbi4, crt4, cit4, d_row)


def _s5_bwd3(dgelu, y_pre, proj, h_re, h_im, lam_rows, p3_re, p3_im, bbr4, bbi4, cr4, ci4, d_row, dproj, t_len, tb):
    seg = tb // SUBLANES
    nb = t_len // tb

    def body(dg_ref, yp_ref, u_ref, hr_ref, hi_ref, lam_ref, p3r_ref, p3i_ref, bbr_ref, bbi_ref, cr_ref, ci_ref,
             d_ref, _, du_ref, dbbr_ref, dbbi_ref, dcr_ref, dci_ref, dd_ref, dlam_ref,
             gr_ref, gi_ref, car_ref, cai_ref, cn_r, cn_i, stage_ref, us_ref, dys_ref, duseg_ref):
        @pl.when(pl.program_id(0) == 0)
        def _():
            for ref in (car_ref, cai_ref, dbbr_ref, dbbi_ref, dcr_ref, dci_ref, dd_ref, dlam_ref):
                ref[...] = jnp.zeros_like(ref)

        _to_segment_order(u_ref[...], stage_ref, us_ref, seg)
        _to_segment_order(dg_ref[...] * _dgelu(yp_ref[...]), stage_ref, dys_ref, seg)
        u, dy = us_ref[...], dys_ref[...]
        for i in range(S5_BLOCKS):
            dyi = dy[:, i * S5_BW:(i + 1) * S5_BW]
            gr_ref[:, pl.ds(i * S5_BL, S5_BL)] = _dot(dyi, cr_ref[i])
            gi_ref[:, pl.ds(i * S5_BL, S5_BL)] = -_dot(dyi, ci_ref[i])
        for lc in range(S5_LANE_BLOCKS // S5_SCAN_BLOCKS):
            blocks = range(lc * S5_SCAN_BLOCKS, (lc + 1) * S5_SCAN_BLOCKS)
            _tile_scan(gr_ref, gi_ref, lam_ref, car_ref, cai_ref, cn_r, cn_i, blocks, seg, True)
            crs = [cn_r[:, _lanes(j)] for j in blocks]
            cis = [cn_i[:, _lanes(j)] for j in blocks]

            def fix(k, carry, blocks=blocks, crs=crs, cis=cis):
                t = seg - 1 - k
                rows = pl.ds(pl.multiple_of(t * SUBLANES, SUBLANES), SUBLANES)
                out = []
                for n, j in enumerate(blocks):
                    nr, ni, slr, sli = carry[4 * n:4 * n + 4]
                    pr, pi = p3r_ref[t, :, _lanes(j)], p3i_ref[t, :, _lanes(j)]
                    g_r = gr_ref[rows, _lanes(j)] + pr * crs[n] - pi * cis[n]
                    g_i = gi_ref[rows, _lanes(j)] + pr * cis[n] + pi * crs[n]
                    gr_ref[rows, _lanes(j)] = g_r
                    gi_ref[rows, _lanes(j)] = g_i
                    hr, hi = hr_ref[rows, _lanes(j)], hi_ref[rows, _lanes(j)]
                    out += [g_r, g_i, slr + nr * hr + ni * hi, sli + ni * hr - nr * hi]
                return tuple(out)

            zero = jnp.zeros((SUBLANES, 128), F32)
            init = []
            for n in range(len(blocks)):
                init += [crs[n], cis[n], zero, zero]
            fin = lax.fori_loop(0, seg, fix, tuple(init), unroll=2)
            for n, j in enumerate(blocks):
                dlam_ref[0:1, _lanes(j)] += jnp.sum(fin[4 * n + 2], axis=0, keepdims=True)
                dlam_ref[1:2, _lanes(j)] += jnp.sum(fin[4 * n + 3], axis=0, keepdims=True)
        for i in range(S5_BLOCKS):
            ws = pl.ds(i * S5_BW, S5_BW)
            bl = pl.ds(i * S5_BL, S5_BL)
            ui, dyi = u[:, i * S5_BW:(i + 1) * S5_BW], dy[:, i * S5_BW:(i + 1) * S5_BW]
            gr, gi = gr_ref[:, bl], gi_ref[:, bl]
            duseg_ref[:, ws] = _dot(gr, bbr_ref[i], _NT) + _dot(gi, bbi_ref[i], _NT) + d_ref[:, ws] * dyi
            dbbr_ref[i] += _dot(ui, gr, _TN)
            dbbi_ref[i] += _dot(ui, gi, _TN)
            dcr_ref[i] += _dot(hr_ref[:, bl], dyi, _TN)
            dci_ref[i] -= _dot(hi_ref[:, bl], dyi, _TN)
        dd_ref[...] += jnp.sum(dy * u, axis=0, keepdims=True)
        _from_segment_order(duseg_ref[...], stage_ref, duseg_ref, seg)
        du_ref[...] = duseg_ref[...].astype(du_ref.dtype)

    whole = pl.BlockSpec(memory_space=pltpu.VMEM)
    rev = lambda i: (nb - 1 - i, 0)
    const3 = lambda i: (0, 0, 0)
    return pl.pallas_call(
        body, name="s5_bwd", grid=(nb,),
        in_specs=[pl.BlockSpec((tb, S5_WIDTH), rev), pl.BlockSpec((tb, S5_WIDTH), rev),
                  pl.BlockSpec((tb, S5_WIDTH), lambda i: (nb - 1 - i, 4096 // S5_WIDTH)),
                  pl.BlockSpec((tb, S5_LANES), rev), pl.BlockSpec((tb, S5_LANES), rev)] + [whole] * 8
                 + [pl.BlockSpec(memory_space=pl.ANY)],
        out_specs=[pl.BlockSpec((tb, S5_WIDTH), lambda i: (nb - 1 - i, 4096 // S5_WIDTH)),
                   pl.BlockSpec((S5_BLOCKS, S5_BW, S5_BL), const3), pl.BlockSpec((S5_BLOCKS, S5_BW, S5_BL), const3),
                   pl.BlockSpec((S5_BLOCKS, S5_BL, S5_BW), const3), pl.BlockSpec((S5_BLOCKS, S5_BL, S5_BW), const3),
                   pl.BlockSpec((1, S5_WIDTH), lambda i: (0, 0)), pl.BlockSpec((2, S5_LANES), lambda i: (0, 0))],
        out_shape=[jax.ShapeDtypeStruct((t_len, IN_COLS), dproj.dtype),
                   jax.ShapeDtypeStruct((S5_BLOCKS, S5_BW, S5_BL), F32),
                   jax.ShapeDtypeStruct((S5_BLOCKS, S5_BW, S5_BL), F32),
                   jax.ShapeDtypeStruct((S5_BLOCKS, S5_BL, S5_BW), F32),
                   jax.ShapeDtypeStruct((S5_BLOCKS, S5_BL, S5_BW), F32),
                   jax.ShapeDtypeStruct((1, S5_WIDTH), F32), jax.ShapeDtypeStruct((2, S5_LANES), F32)],
        scratch_shapes=[pltpu.VMEM((tb, S5_LANES), F32), pltpu.VMEM((tb, S5_LANES), F32),
                        pltpu.VMEM((1, S5_LANES), F32), pltpu.VMEM((1, S5_LANES), F32),
                        pltpu.VMEM((SUBLANES, S5_LANES), F32), pltpu.VMEM((SUBLANES, S5_LANES), F32),
                        pltpu.VMEM((S5_WIDTH // 128, tb, 128), F32), pltpu.VMEM((tb, S5_WIDTH), F32),
                        pltpu.VMEM((tb, S5_WIDTH), F32), pltpu.VMEM((tb, S5_WIDTH), F32)],
        input_output_aliases={13: 0},
        compiler_params=_params("arbitrary"))(dgelu, y_pre, proj, h_re, h_im, lam_rows, p3_re, p3_im, bbr4, bbi4,
                                              cr4, ci4, d_row, dproj)


def _block_diag4(per_group):
    g8 = S5_GROUPS // S5_BLOCKS
    eye = jnp.eye(g8, dtype=bool)[None, :, None, :, None]
    dense = jnp.where(eye, per_group.reshape(S5_BLOCKS, g8, S5_GROUP, 1, S5_STATE), 0.0)
    return dense.reshape(S5_BLOCKS, S5_BW, S5_BL)


def _diag_blocks4(dense):
    g8 = S5_GROUPS // S5_BLOCKS
    ar = jnp.arange(g8)
    d5 = dense.reshape(S5_BLOCKS, g8, S5_GROUP, g8, S5_STATE)
    return d5[:, ar, :, ar, :].transpose(1, 0, 2, 3).reshape(S5_GROUPS, S5_GROUP, S5_STATE)


def _block_diag(per_group):
    eye = jnp.eye(S5_GROUPS, dtype=bool)[:, None, :, None]
    dense = jnp.where(eye, per_group[:, :, None, :], 0.0)
    return dense.reshape(S5_WIDTH, S5_LANES)


def _diag_blocks(dense):
    ar = jnp.arange(S5_GROUPS)
    return dense.reshape(S5_GROUPS, S5_GROUP, S5_GROUPS, S5_STATE)[ar, :, ar, :]


def _hg_gate_bwd(da, o, g, gn):
    dos, dgs, dgns = [], [], []
    for h in range(HG_HEADS):
        sl = slice(h * HG_DIM, (h + 1) * HG_DIM)
        oh, gh, dah, gnh = o[:, sl], g[:, sl], da[:, sl], gn[:, sl]
        rr = lax.rsqrt(jnp.mean(oh * oh, axis=-1, keepdims=True) + NORM_EPS)
        sg = _sig(gh)
        dgs.append(dah * (oh * rr * gnh) * _dsilu(gh, sg))
        don = dah * (gh * sg)
        t = don * gnh
        dos.append(rr * t - oh * (rr * rr * rr) * jnp.mean(t * oh, axis=-1, keepdims=True))
        dgns.append(jnp.sum(don * oh * rr, axis=0, keepdims=True))
    return jnp.concatenate(dos, axis=1), jnp.concatenate(dgs, axis=1), jnp.concatenate(dgns, axis=1)


MIX_BWD_COLS = ((3072, 1024), (4608, 512), (5120, 1024), (6144, 1024))


def _mix_bwd(dgl, h1, dh2, act_hg, ys2, ys_gelu, proj, o_hg, g2, ghn, b_glu, w, t_len, tm):
    nb = t_len // tm

    def body(dgl_ref, h1_ref, dh2_ref, act_ref, ys2_ref, ysg_ref, ghg_ref, z_ref, gh_ref, gs_ref, o_ref, g2_ref, gn_ref,
             bglu_ref, wg_ref, wo_ref, ws5_ref, whg_ref, wglu_ref,
             dh1_ref, dyh_ref, dys_ref, dglu_ref, dgelu_ref, do_ref, dg2_ref, dbglu_ref, dgn_ref, dproj_ref,
             st0, st1, st2, st3, sems):
        i = pl.program_id(0)
        stages = (st0, st1, st2, st3)

        def writes(step):
            rows = pl.ds(pl.multiple_of(step * tm, tm), tm)
            return [pltpu.make_async_copy(st, dproj_ref.at[rows, pl.ds(c0, wd)], sems.at[k])
                    for k, (st, (c0, wd)) in enumerate(zip(stages, MIX_BWD_COLS))]

        @pl.when(i > 0)
        def _():
            for cp in writes(i - 1):
                cp.wait()

        @pl.when(i == 0)
        def _():
            for ref in (dg2_ref, dbglu_ref, dgn_ref):
                ref[...] = jnp.zeros_like(ref)

        dx, dg2 = _rms_bwd(_dot(dgl_ref[...], wg_ref[...], _NT), h1_ref[...], g2_ref[...])
        dh1 = dh2_ref[...] + dx
        dh1_ref[...] = dh1
        dg2_ref[...] += dg2
        dm = _dot(dh1, wo_ref[...], _NT)
        sh, ss = _sig(gh_ref[...]), _sig(gs_ref[...])
        dyh, dys = _mx(dm * sh), _mx(dm * ss)
        dyh_ref[...] = dyh
        dys_ref[...] = dys
        st2[...] = (dm * _dot(act_ref[...], whg_ref[...]) * sh * (1.0 - sh)).astype(st2.dtype)
        st3[...] = (dm * _dot(ys2_ref[...], ws5_ref[...]) * ss * (1.0 - ss)).astype(st3.dtype)
        dys2 = _dot(dys, ws5_ref[...], _NT)
        gl_, z = _dot(ysg_ref[...], wglu_ref[...]) + bglu_ref[...], z_ref[...]
        a, b = gl_[:, :S5_WIDTH], gl_[:, S5_WIDTH:]
        sb, sz = _sig(b), _sig(z)
        silu = z * sz
        dglu = jnp.concatenate([dys2 * sb * silu, dys2 * a * silu * sb * (1.0 - sb)], axis=1)
        st1[...] = (dys2 * a * sb * _dsilu(z, sz)).astype(st1.dtype)
        dbglu_ref[...] += jnp.sum(dglu, axis=0, keepdims=True)
        dglu_ref[...] = _mx(dglu)
        dgelu_ref[...] = _dot(dglu, wglu_ref[...], _NT)
        d_o, dg, dgn = _hg_gate_bwd(_dot(dyh, whg_ref[...], _NT), o_ref[...], ghg_ref[...], gn_ref[...])
        do_ref[...] = d_o.astype(do_ref.dtype)
        st0[...] = dg.astype(st0.dtype)
        dgn_ref[...] += dgn
        for cp in writes(i):
            cp.start()

        @pl.when(i == nb - 1)
        def _():
            for cp in writes(i):
                cp.wait()

    tile = lambda wd, cb=0: pl.BlockSpec((tm, wd), functools.partial(lambda i, cb: (i, cb), cb=cb))
    row = lambda wd: pl.BlockSpec((1, wd), lambda i: (0, 0))
    whole = pl.BlockSpec(memory_space=pltpu.VMEM)
    return pl.pallas_call(
        body, name="mix_bwd", grid=(nb,),
        in_specs=[tile(1024), tile(1024), tile(1024), tile(1024), tile(512), tile(512), tile(1024, 3),
                  tile(512, 4608 // 512), tile(1024, 5), tile(1024, 6), tile(1024), row(1024), row(1024), row(1024)]
                 + [whole] * 5,
        out_specs=[tile(1024), tile(1024), tile(1024), tile(1024), tile(512), tile(1024), row(1024), row(1024),
                   row(1024), _HBM],
        out_shape=[jax.ShapeDtypeStruct((t_len, 1024), F32), jax.ShapeDtypeStruct((t_len, 1024), MXU_DTYPE),
                   jax.ShapeDtypeStruct((t_len, 1024), MXU_DTYPE), jax.ShapeDtypeStruct((t_len, 1024), MXU_DTYPE),
                   jax.ShapeDtypeStruct((t_len, 512), F32), jax.ShapeDtypeStruct((t_len, 1024), MXU_DTYPE),
                   jax.ShapeDtypeStruct((1, 1024), F32), jax.ShapeDtypeStruct((1, 1024), F32),
                   jax.ShapeDtypeStruct((1, 1024), F32), jax.ShapeDtypeStruct((t_len, IN_COLS), MXU_DTYPE)],
        scratch_shapes=[pltpu.VMEM((tm, wd), MXU_DTYPE) for _, wd in MIX_BWD_COLS] + [pltpu.SemaphoreType.DMA((4,))],
        compiler_params=_params("arbitrary"))(dgl, h1, dh2, act_hg, ys2, ys_gelu, proj, proj, proj, proj, o_hg, g2, ghn,
                                              b_glu, w["w_ple_gate"], w["w_out"], w["w_o_s5"], w["w_o_hg"],
                                              w["w_glu"])


def _local_step(x, p, target, w, sm, comm=None):
    t_len = x.shape[0]
    tm = min(256, t_len)
    tmm = min(512, t_len)
    tb_hg = min(256, t_len)
    tb_s5 = min(256, t_len)
    g1, g2, g3, ghn = sm["norm_g"], sm["ple_norm_g"], sm["final_norm_g"].reshape(1, D_MODEL), sm["hg_norm_g"]

    def rms_in(xv, g):
        return xv * lax.rsqrt(jnp.mean(xv * xv, axis=-1, keepdims=True) + NORM_EPS) * g

    in_shard = IN_COLS // N_CHIPS
    w_in = w["w_in"]
    if comm is None:
        proj, u = _mm_nn("mm_in", x, w_in, tmm, in_shard, prologue=rms_in, consts=[g1])
    else:
        proj, u, landed = _mm_nn("mm_in", x, w_in, tmm, in_shard, riding=comm.gather_rest(), prologue=rms_in,
                                 consts=[g1])
        w = comm.rest_weights(landed)
    o_hg, act_hg, s_prev = _hgrn2_fwd2(proj, sm["hg_lb"], ghn, t_len, tb_hg)

    lanes = lambda a: a.reshape(1, S5_LANES)
    a_re, a_im = lanes(sm["s5_a_re"]), lanes(sm["s5_a_im"])
    ldt = lanes(jnp.broadcast_to(sm["s5_log_dt"].reshape(S5_GROUPS, 1), (S5_GROUPS, S5_STATE)))
    to_t = lambda b: b.reshape(S5_GROUPS, S5_STATE, S5_GROUP).transpose(2, 0, 1).reshape(S5_GROUP, S5_LANES)
    b_re_t, b_im_t = to_t(sm["s5_b_re"]), to_t(sm["s5_b_im"])
    scan_fwd, scan_rev, bbr_t, bbi_t = _s5_powers(a_re, a_im, ldt, b_re_t, b_im_t, tb_s5 // SUBLANES)
    from_t = lambda b: b.reshape(S5_GROUP, S5_GROUPS, S5_STATE).transpose(1, 0, 2)
    bbr_bd = _block_diag4(from_t(bbr_t)).astype(MXU_DTYPE)
    bbi_bd = _block_diag4(from_t(bbi_t)).astype(MXU_DTYPE)
    cr_bd = _block_diag4(sm["s5_c_re"].reshape(S5_GROUPS, S5_GROUP, S5_STATE)).astype(MXU_DTYPE)
    ci_bd = _block_diag4(sm["s5_c_im"].reshape(S5_GROUPS, S5_GROUP, S5_STATE)).astype(MXU_DTYPE)
    d_row = sm["s5_d"].reshape(1, S5_WIDTH)
    h_re, h_im, y_pre, ys_gelu = _s5_fwd3(proj, *scan_fwd, bbr_bd, bbi_bd,
                                          cr_bd.transpose(0, 2, 1), ci_bd.transpose(0, 2, 1), d_row, t_len, tb_s5)
    def mix_f(act, ysg, z, gh, gs, xv, w_glu, b_glu, w_o_hg, w_o_s5, w_out):
        gl_ = _dot(ysg, w_glu) + b_glu
        a, b = gl_[:, :S5_WIDTH], gl_[:, S5_WIDTH:]
        ys2_ = (a * _sig(b) * (z * _sig(z))).astype(MXU_DTYPE)
        yh, ys = _dot(act, w_o_hg), _dot(ys2_, w_o_s5)
        mg = (_sig(gh) * yh + _sig(gs) * ys).astype(MXU_DTYPE)
        return (ys2_, mg, xv + _dot(mg, w_out))

    ys2, merged, h1 = _rowwise(
        "mix_out", mix_f, t_len, tm,
        [(act_hg, 1024, 0), (ys_gelu, 512, 0), (proj, 512, 4608 // 512), (proj, 1024, 5), (proj, 1024, 6),
         (x, 1024, 0)], [w["w_glu"], sm["b_glu"], w["w_o_hg"], w["w_o_s5"], w["w_out"]],
        [(512, MXU_DTYPE), (1024, MXU_DTYPE), (1024, F32)])

    def head_f(h1v, pv, tgt, g_ple, g, w_ple, w_gate):
        r2 = lax.rsqrt(jnp.mean(h1v * h1v, axis=-1, keepdims=True) + NORM_EPS)
        n2_ = (h1v * r2 * g_ple).astype(MXU_DTYPE)
        glv, pev = _dot(n2_, w_gate), _dot(pv, w_ple)
        gate = _sig(glv)
        h2 = h1v + pev * gate
        r = lax.rsqrt(jnp.mean(h2 * h2, axis=-1, keepdims=True) + NORM_EPS)
        e = h2 * r * g - tgt
        loss = 0.5 * jnp.sum(jnp.mean(e * e, axis=-1, keepdims=True), axis=0, keepdims=True)
        dy = e * (1.0 / D_MODEL)
        dg = jnp.sum(dy * h2 * r, axis=0, keepdims=True)
        t = dy * g
        dh2 = r * t - h2 * (r * r * r) * jnp.mean(t * h2, axis=-1, keepdims=True)
        return (n2_, dh2, dh2 * gate, dh2 * pev * gate * (1.0 - gate), jnp.broadcast_to(loss, (1, 128)), dg)

    n2, dh2, dpe, dgl, loss_row, d_g3 = _rowwise(
        "ple_loss_head", head_f, t_len, tm, [(h1, 1024, 0), (p, 256, 0), (target, 1024, 0)],
        [g2, g3, w["w_ple"], w["w_ple_gate"]],
        [(1024, MXU_DTYPE), (1024, F32), (1024, MXU_DTYPE), (1024, MXU_DTYPE)], accs=[(1, 128), (1, 1024)])

    gb = {}
    gb["w_ple"] = _mm_tn("mm_d_w_ple", p, dpe, tmm, 1024)
    gb["w_ple_gate"] = _mm_tn("mm_d_w_ple_gate", n2, dgl, tmm, 1024)
    dh1, dy_hg, dy_s5, dglu, dgelu, d_o, d_g2, d_bglu, d_ghn, dproj = _mix_bwd(
        dgl, h1, dh2, act_hg, ys2, ys_gelu, proj, o_hg, g2, ghn, sm["b_glu"], w, t_len, tm)
    gb["w_out"] = _mm_tn("mm_d_w_out", merged, dh1, tmm, 1024)
    gb["w_o_s5"] = _mm_tn("mm_d_w_o_s5", ys2, dy_s5, tmm, 1024)
    gb["w_glu"] = _mm_tn("mm_d_w_glu", ys_gelu, dglu, tmm, 1024)
    dproj, d_bbr, d_bbi, d_crt, d_cit, d_d, d_lam = _s5_bwd3(dgelu, y_pre, proj, h_re, h_im,
                                                            *scan_rev, bbr_bd, bbi_bd, cr_bd,
                                                            ci_bd, d_row, dproj, t_len, tb_s5)
    to_t3 = lambda b: b.transpose(1, 0, 2).reshape(S5_GROUP, S5_LANES)
    d_are, d_aim, d_ldt, d_br_t, d_bi_t = _s5_prep_bwd(a_re, a_im, ldt, b_re_t, b_im_t, d_lam,
                                                       to_t3(_diag_blocks4(d_bbr)), to_t3(_diag_blocks4(d_bbi)))
    gb["w_o_hg"] = _mm_tn("mm_d_w_o_hg", act_hg, dy_hg, tmm, 1024)
    if comm is None:
        dproj, d_lb = _hgrn2_bwd2(proj, d_o, s_prev, sm["hg_lb"], dproj, t_len, tb_hg)
    else:
        rest_grads = _pack_rest_full(gb)
        dproj, d_lb, rest_theirs = _hgrn2_bwd2(proj, d_o, s_prev, sm["hg_lb"], dproj, t_len, tb_hg,
                                               riding=comm.swap(rest_grads))

    def in_b(duv, xv, dh, g):
        dx, dg = _rms_bwd(duv, xv, g)
        return (dh + dx, dg)

    in_args = ("mm_d_u_rms_in_bwd", dproj, w_in, tmm, in_shard, in_b, [(x, 1024, 0), (dh1, 1024, 0)], [g1],
               [(1024, F32)])
    if comm is None:
        gb["w_in"] = _mm_tn("mm_d_w_in", u, dproj, tmm, in_shard, col_shards=True)
        grad_x, d_g1 = _mm_nt_then(*in_args, accs=[(1, 1024)])
    else:
        gb["w_in"], landed = _mm_tn("mm_d_w_in", u, dproj, tmm, in_shard, col_shards=True,
                                    riding=comm.scatter("rest", rest_grads, rest_theirs))
        comm.landed["rest"] = landed
        grad_x, d_g1, landed = _mm_nt_then(*in_args, accs=[(1, 1024)], riding=comm.scatter(
            "in", gb["w_in"].reshape(N_CHIPS, 2, D_MODEL // 2, in_shard)))
        comm.landed["in"] = landed

    back_t = lambda b: b.reshape(S5_GROUP, S5_GROUPS, S5_STATE).transpose(1, 2, 0).reshape(1, S5_GROUPS, S5_STATE,
                                                                                           S5_GROUP)
    gs = {
        "norm_g": d_g1, "hg_lb": d_lb, "hg_norm_g": d_ghn,
        "s5_a_re": d_are.reshape(1, S5_GROUPS, S5_STATE), "s5_a_im": d_aim.reshape(1, S5_GROUPS, S5_STATE),
        "s5_log_dt": d_ldt[0:1, :S5_GROUPS],
        "s5_b_re": back_t(d_br_t), "s5_b_im": back_t(d_bi_t),
        "s5_c_re": _diag_blocks4(d_crt.transpose(0, 2, 1)).reshape(1, S5_GROUPS, S5_GROUP, S5_STATE),
        "s5_c_im": _diag_blocks4(d_cit.transpose(0, 2, 1)).reshape(1, S5_GROUPS, S5_GROUP, S5_STATE),
        "s5_d": d_d.reshape(1, S5_GROUPS, S5_GROUP), "b_glu": d_bglu, "ple_norm_g": d_g2,
        "final_norm_g": d_g3.reshape(D_MODEL),
    }
    return loss_row, grad_x, gb, gs


def _shard_shape(name):
    r, c = BIG_SHAPE[name]
    return (r, c // N_CHIPS) if name in BIG_COL_SHARDED else (r // N_CHIPS, c)


def _pack_shard(parts):
    return jnp.concatenate([parts[n].reshape(-1, PACK_W) for n in BIG], axis=0)


def _unpack_shard(packed):
    out, off = {}, 0
    for n in BIG:
        r, c = _shard_shape(n)
        rows = r * c // PACK_W
        out[n] = packed[off:off + rows].reshape(1, r, c)
        off += rows
    return out


def _unpack_full(gathered):
    out, off = {}, 0
    for n in BIG:
        r, c = _shard_shape(n)
        rows = r * c // PACK_W
        sh = gathered[:, off:off + rows].reshape(N_CHIPS, r, c)
        out[n] = sh.transpose(1, 0, 2).reshape(BIG_SHAPE[n]) if n in BIG_COL_SHARDED else sh.reshape(BIG_SHAPE[n])
        off += rows
    return out


def _pack_full(full):
    parts = []
    for n in BIG:
        r, c = _shard_shape(n)
        g = full[n]
        sh = g.reshape(BIG_SHAPE[n][0], N_CHIPS, c).transpose(1, 0, 2) if n in BIG_COL_SHARDED else g
        parts.append(sh.reshape(N_CHIPS, r * c // PACK_W, PACK_W))
    packed = jnp.concatenate(parts, axis=1)
    return packed.reshape(N_CHIPS, 2, HALF_ROWS, PACK_W).transpose(1, 0, 2, 3)


def _pack_small(parts, last):
    flat = jnp.concatenate([parts[n].reshape(-1) for n in SMALL] + [last.reshape(-1)])
    return jnp.pad(flat, (0, SMALL_ROWS * PACK_W - flat.shape[0])).reshape(SMALL_ROWS, PACK_W)


def _unpack_small(packed):
    flat, out, off = packed.reshape(-1), {}, 0
    for n in SMALL:
        size = 1
        for d in SMALL_SHAPE[n]:
            size *= d
        out[n] = flat[off:off + size].reshape(SMALL_SHAPE[n])
        off += size
    return out, flat[off]


def _place():
    x, y, c = lax.axis_index("x"), lax.axis_index("y"), lax.axis_index("c")
    return x, y, c, [(1 - x, y), (x, 1 - y), (1 - x, 1 - y)]


def _remote(src, dst, send_sems, recv_sems, k, to):
    return pltpu.make_async_remote_copy(src_ref=src, dst_ref=dst, send_sem=send_sems.at[k], recv_sem=recv_sems.at[k],
                                        device_id=to, device_id_type=MESH)


_HBM = pl.BlockSpec(memory_space=pl.ANY)


def _all_gather_weights(wp):
    def body(wp_ref, out_ref, send_sems, recv_sems):
        x, y, c, chips = _place()
        k = 2 * x + y
        sibling = (x, y, 1 - c)
        first =[_remote(wp_ref.at[c], out_ref.at[k, c], send_sems, recv_sems, j, (cx, cy, c))
                 for j, (cx, cy) in enumerate(chips)]
        for cp in first:
            cp.start()
        passed = []
        for j, (cx, cy) in enumerate(chips):
            kj = 2 * cx + cy
            _remote(wp_ref.at[c], out_ref.at[kj, c], send_sems, recv_sems, j, (cx, cy, c)).wait_recv()
            cp = _remote(out_ref.at[kj, c], out_ref.at[kj, c], send_sems, recv_sems, 3 + j, sibling)
            cp.start()
            passed.append(cp)
        for j, (cx, cy) in enumerate(chips):
            kj = 2 * cx + cy
            _remote(wp_ref.at[c], out_ref.at[kj, 1 - c], send_sems, recv_sems, 3 + j, sibling).wait_recv()
        for cp in first + passed:
            cp.wait_send()

    return pl.pallas_call(
        body, name="all_gather_weights", in_specs=[_HBM], out_specs=_HBM,
        out_shape=jax.ShapeDtypeStruct((N_CHIPS, 2, HALF_ROWS, PACK_W), wp.dtype),
        scratch_shapes=[pltpu.SemaphoreType.DMA((6,)), pltpu.SemaphoreType.DMA((6,))])(wp)


def _exchange_halves(pg):
    def body(pg_ref, out_ref, send_sems, recv_sems):
        x, y, c, _ = _place()
        cp = _remote(pg_ref.at[1 - c], out_ref, send_sems, recv_sems, 0, (x, y, 1 - c))
        cp.start()
        cp.wait()

    return pl.pallas_call(
        body, name="exchange_halves", in_specs=[_HBM], out_specs=_HBM,
        out_shape=jax.ShapeDtypeStruct((N_CHIPS, HALF_ROWS, PACK_W), pg.dtype),
        scratch_shapes=[pltpu.SemaphoreType.DMA((1,)), pltpu.SemaphoreType.DMA((1,))])(pg)


def _scatter_chip_sums(ps):
    def body(ps_ref, out_ref, send_sems, recv_sems):
        x, y, c, chips = _place()
        cps = [_remote(ps_ref.at[2 * cx + cy], out_ref.at[j], send_sems, recv_sems, j, (cx, cy, c))
               for j, (cx, cy) in enumerate(chips)]
        for cp in cps:
            cp.start()
        for cp in cps:
            cp.wait()

    return pl.pallas_call(
        body, name="scatter_chip_sums", in_specs=[_HBM], out_specs=_HBM,
        out_shape=jax.ShapeDtypeStruct((3, HALF_ROWS, PACK_W), ps.dtype),
        scratch_shapes=[pltpu.SemaphoreType.DMA((3,)), pltpu.SemaphoreType.DMA((3,))])(ps)


def _share_half(g_half):
    def body(g_ref, out_ref, send_sems, recv_sems):
        x, y, c, _ = _place()
        cp = _remote(g_ref, out_ref.at[c], send_sems, recv_sems, 0, (x, y, 1 - c))
        cp.start()
        _remote(g_ref, out_ref.at[1 - c], send_sems, recv_sems, 0, (x, y, 1 - c)).wait_recv()
        cp.wait_send()

    return pl.pallas_call(
        body, name="share_half", in_specs=[_HBM], out_specs=_HBM,
        out_shape=jax.ShapeDtypeStruct((2, HALF_ROWS, PACK_W), g_half.dtype),
        scratch_shapes=[pltpu.SemaphoreType.DMA((1,)), pltpu.SemaphoreType.DMA((1,))])(g_half)


REDUCE_ROWS = 480


def _sum_pair(pg, theirs, c):
    def body(c_ref, a_ref, b_ref, o_ref):
        o_ref[...] = (a_ref[...] + b_ref[...]).astype(o_ref.dtype)

    return pl.pallas_call(
        body, name="sum_pair",
        grid_spec=pltpu.PrefetchScalarGridSpec(
            num_scalar_prefetch=1, grid=(N_CHIPS, HALF_ROWS // REDUCE_ROWS),
            in_specs=[pl.BlockSpec((None, None, REDUCE_ROWS, PACK_W), lambda j, i, c_ref: (c_ref[0], j, i, 0)),
                      pl.BlockSpec((None, REDUCE_ROWS, PACK_W), lambda j, i, c_ref: (j, i, 0))],
            out_specs=pl.BlockSpec((None, REDUCE_ROWS, PACK_W), lambda j, i, c_ref: (j, i, 0))),
        out_shape=jax.ShapeDtypeStruct((N_CHIPS, HALF_ROWS, PACK_W), WIRE_DTYPE),
        compiler_params=_params("arbitrary", "arbitrary"))(c.reshape(1), pg, theirs)


def _sum_chips(ps, others, k):
    def body(k_ref, a_ref, b_ref, o_ref):
        o_ref[...] = ((a_ref[...].astype(F32) + b_ref[0].astype(F32)) + b_ref[1].astype(F32)) + b_ref[2].astype(F32)

    return pl.pallas_call(
        body, name="sum_chips",
        grid_spec=pltpu.PrefetchScalarGridSpec(
            num_scalar_prefetch=1, grid=(HALF_ROWS // REDUCE_ROWS,),
            in_specs=[pl.BlockSpec((None, REDUCE_ROWS, PACK_W), lambda i, k_ref: (k_ref[0], i, 0)),
                      pl.BlockSpec((3, REDUCE_ROWS, PACK_W), lambda i, k_ref: (0, i, 0))],
            out_specs=pl.BlockSpec((REDUCE_ROWS, PACK_W), lambda i, k_ref: (i, 0))),
        out_shape=jax.ShapeDtypeStruct((HALF_ROWS, PACK_W), F32),
        compiler_params=_params("arbitrary"))(k.reshape(1), ps, others)


REST = tuple(n for n in BIG if n != "w_in")
REST_ROWS = sum(BIG_SHAPE[n][0] * BIG_SHAPE[n][1] for n in REST) // (N_CHIPS * PACK_W)
IN_SHARD = IN_COLS // N_CHIPS
IN_TILE, REST_TILE = 256, 272


def _pack_rest(parts):
    return jnp.concatenate([parts[n].reshape(-1, PACK_W) for n in REST], axis=0)


def _unpack_rest(packed):
    out, off = {}, 0
    for n in REST:
        r, c = _shard_shape(n)
        rows = r * c // PACK_W
        out[n] = packed[off:off + rows].reshape(1, r, c)
        off += rows
    return out


def _unpack_rest_full(gathered):
    out, off = {}, 0
    for n in REST:
        r, c = _shard_shape(n)
        rows = r * c // PACK_W
        sh = gathered[:, off:off + rows].reshape(N_CHIPS, r, c)
        out[n] = sh.transpose(1, 0, 2).reshape(BIG_SHAPE[n]) if n in BIG_COL_SHARDED else sh.reshape(BIG_SHAPE[n])
        off += rows
    return out


def _pack_rest_full(full):
    parts = []
    for n in REST:
        r, c = _shard_shape(n)
        g = full[n]
        sh = g.reshape(BIG_SHAPE[n][0], N_CHIPS, c).transpose(1, 0, 2) if n in BIG_COL_SHARDED else g
        parts.append(sh.reshape(N_CHIPS, r * c // PACK_W, PACK_W))
    return jnp.concatenate(parts, axis=1).reshape(N_CHIPS, 2, REST_ROWS // 2, PACK_W)


def _gather_shards(ws):
    n = len(ws)

    def body(*refs):
        w_refs, out_refs, (send_sems, recv_sems) = refs[:n], refs[n:2 * n], refs[2 * n:]
        x, y, c, chips = _place()
        k = 2 * x + y
        sibling = (x, y, 1 - c)
        first = [_remote(w_ref.at[c], out_ref.at[k, c], send_sems, recv_sems, 6 * g + j, (cx, cy, c))
                 for j, (cx, cy) in enumerate(chips) for g, (w_ref, out_ref) in enumerate(zip(w_refs, out_refs))]
        for cp in first:
            cp.start()
        passed = []
        for j, (cx, cy) in enumerate(chips):
            kj = 2 * cx + cy
            for g, (w_ref, out_ref) in enumerate(zip(w_refs, out_refs)):
                _remote(w_ref.at[c], out_ref.at[kj, c], send_sems, recv_sems, 6 * g + j, (cx, cy, c)).wait_recv()
                cp = _remote(out_ref.at[kj, c], out_ref.at[kj, c], send_sems, recv_sems, 6 * g + 3 + j, sibling)
                cp.start()
                passed.append(cp)
        for j, (cx, cy) in enumerate(chips):
            kj = 2 * cx + cy
            for g, (w_ref, out_ref) in enumerate(zip(w_refs, out_refs)):
                _remote(w_ref.at[c], out_ref.at[kj, 1 - c], send_sems, recv_sems, 6 * g + 3 + j, sibling).wait_recv()
        for cp in first + passed:
            cp.wait_send()

    return pl.pallas_call(
        body, name="all_gather_weights", in_specs=[_HBM] * n, out_specs=[_HBM] * n,
        out_shape=[jax.ShapeDtypeStruct((N_CHIPS,) + w.shape, w.dtype) for w in ws],
        scratch_shapes=[pltpu.SemaphoreType.DMA((6 * n,)), pltpu.SemaphoreType.DMA((6 * n,))])(*ws)


def _swap_halves(pgs, name="exchange_halves"):
    n = len(pgs)

    def body(*refs):
        pg_refs, out_refs, (send_sems, recv_sems) = refs[:n], refs[n:2 * n], refs[2 * n:]
        x, y, c, _ = _place()
        cps = [_remote(pg_ref.at[j, 1 - c], out_ref.at[j], send_sems, recv_sems, N_CHIPS * g + j, (x, y, 1 - c))
               for g, (pg_ref, out_ref) in enumerate(zip(pg_refs, out_refs)) for j in range(N_CHIPS)]
        for cp in cps:
            cp.start()
        for cp in cps:
            cp.wait()

    return pl.pallas_call(
        body, name=name, in_specs=[_HBM] * n, out_specs=[_HBM] * n,
        out_shape=[jax.ShapeDtypeStruct((N_CHIPS,) + pg.shape[2:], pg.dtype) for pg in pgs],
        scratch_shapes=[pltpu.SemaphoreType.DMA((N_CHIPS * n,)), pltpu.SemaphoreType.DMA((N_CHIPS * n,))])(*pgs)


def _scatter_sums(pss):
    n = len(pss)

    def body(*refs):
        ps_refs, out_refs, (send_sems, recv_sems) = refs[:n], refs[n:2 * n], refs[2 * n:]
        x, y, c, chips = _place()
        cps = [_remote(ps_ref.at[2 * cx + cy], out_ref.at[j], send_sems, recv_sems, 3 * g + j, (cx, cy, c))
               for j, (cx, cy) in enumerate(chips) for g, (ps_ref, out_ref) in enumerate(zip(ps_refs, out_refs))]
        for cp in cps:
            cp.start()
        for cp in cps:
            cp.wait()

    return pl.pallas_call(
        body, name="scatter_chip_sums", in_specs=[_HBM] * n, out_specs=[_HBM] * n,
        out_shape=[jax.ShapeDtypeStruct((3,) + ps.shape[1:], ps.dtype) for ps in pss],
        scratch_shapes=[pltpu.SemaphoreType.DMA((3 * n,)), pltpu.SemaphoreType.DMA((3 * n,))])(*pss)


def _share_halves(gs):
    n = len(gs)

    def body(*refs):
        g_refs, out_refs, (send_sems, recv_sems) = refs[:n], refs[n:2 * n], refs[2 * n:]
        x, y, c, _ = _place()
        cps = [_remote(g_ref, out_ref.at[c], send_sems, recv_sems, g, (x, y, 1 - c))
               for g, (g_ref, out_ref) in enumerate(zip(g_refs, out_refs))]
        for cp in cps:
            cp.start()
        for g, (g_ref, out_ref) in enumerate(zip(g_refs, out_refs)):
            _remote(g_ref, out_ref.at[1 - c], send_sems, recv_sems, g, (x, y, 1 - c)).wait_recv()
        for cp in cps:
            cp.wait_send()

    return pl.pallas_call(
        body, name="share_half", in_specs=[_HBM] * n, out_specs=[_HBM] * n,
        out_shape=[jax.ShapeDtypeStruct((2,) + g.shape, g.dtype) for g in gs],
        scratch_shapes=[pltpu.SemaphoreType.DMA((n,)), pltpu.SemaphoreType.DMA((n,))])(*gs)


def _pair_sum(name, pg, theirs, c, tile):
    _, _, rows, width = pg.shape

    def body(c_ref, a_ref, b_ref, o_ref):
        o_ref[...] = (a_ref[...] + b_ref[...]).astype(o_ref.dtype)

    return pl.pallas_call(
        body, name=name,
        grid_spec=pltpu.PrefetchScalarGridSpec(
            num_scalar_prefetch=1, grid=(N_CHIPS, rows // tile),
            in_specs=[pl.BlockSpec((None, None, tile, width), lambda j, i, c_ref: (j, c_ref[0], i, 0)),
                      pl.BlockSpec((None, tile, width), lambda j, i, c_ref: (j, i, 0))],
            out_specs=pl.BlockSpec((None, tile, width), lambda j, i, c_ref: (j, i, 0))),
        out_shape=jax.ShapeDtypeStruct((N_CHIPS, rows, width), WIRE_DTYPE),
        compiler_params=_params("arbitrary", "arbitrary"))(c.reshape(1), pg, theirs)


def _chip_sum(name, ps, others, k, tile):
    _, rows, width = ps.shape

    def body(k_ref, a_ref, b_ref, o_ref):
        o_ref[...] = ((a_ref[...].astype(F32) + b_ref[0].astype(F32)) + b_ref[1].astype(F32)) + b_ref[2].astype(F32)

    return pl.pallas_call(
        body, name=name,
        grid_spec=pltpu.PrefetchScalarGridSpec(
            num_scalar_prefetch=1, grid=(rows // tile,),
            in_specs=[pl.BlockSpec((None, tile, width), lambda i, k_ref: (k_ref[0], i, 0)),
                      pl.BlockSpec((3, tile, width), lambda i, k_ref: (0, i, 0))],
            out_specs=pl.BlockSpec((tile, width), lambda i, k_ref: (i, 0))),
        out_shape=jax.ShapeDtypeStruct((rows, width), F32),
        compiler_params=_params("arbitrary"))(k.reshape(1), ps, others)


class _StepComm:
    TILES = {"in": IN_TILE, "rest": REST_TILE}

    def __init__(self, rest_wire, chip, core):
        self.rest_wire, self.chip, self.core = rest_wire, chip, core
        self.sums, self.landed = {}, {}

    def gather_rest(self):
        wire = self.rest_wire

        def sends(ins, outs, send_sems, recv_sems):
            (w_ref,), (out_ref,) = ins, outs
            x, y, c, chips = _place()
            return [_remote(w_ref.at[c], out_ref.at[2 * x + y, c], send_sems, recv_sems, 4 * j + 2 * c + to,
                            (cx, cy, to)) for j, (cx, cy) in enumerate(chips) for to in (0, 1)]

        def recvs(ins, outs, send_sems, recv_sems):
            (w_ref,), (out_ref,) = ins, outs
            _, _, c, chips = _place()
            return [_remote(w_ref.at[c], out_ref.at[2 * cx + cy, by], send_sems, recv_sems, 4 * j + 2 * by + c,
                            (cx, cy, by)) for j, (cx, cy) in enumerate(chips) for by in (0, 1)]

        def start(*refs):
            for cp in sends(*refs):
                cp.start()

        def wait(*refs):
            for cp in recvs(*refs):
                cp.wait_recv()
            for cp in sends(*refs):
                cp.wait_send()

        return _Riding((wire,), (jax.ShapeDtypeStruct((N_CHIPS,) + wire.shape, wire.dtype),), 12, start, wait)

    def rest_weights(self, landed):
        full = lax.dynamic_update_slice(landed, self.rest_wire[None], (self.chip, 0, 0, 0))
        return _unpack_rest_full(full.reshape(N_CHIPS, REST_ROWS, PACK_W))

    def swap(self, pg):
        def copies(ins, outs, send_sems, recv_sems):
            (pg_ref,), (out_ref,) = ins, outs
            x, y, c, _ = _place()
            return [_remote(pg_ref.at[j, 1 - c], out_ref.at[j], send_sems, recv_sems, j, (x, y, 1 - c))
                    for j in range(N_CHIPS)]

        def start(*refs):
            for cp in copies(*refs):
                cp.start()

        def wait(*refs):
            for cp in copies(*refs):
                cp.wait()

        return _Riding((pg,), (jax.ShapeDtypeStruct((N_CHIPS,) + pg.shape[2:], pg.dtype),), N_CHIPS, start, wait)

    def scatter(self, group, pg, theirs=None):
        if theirs is None:
            (theirs,) = _swap_halves([pg], "exchange_halves_" + group)
        ps = _pair_sum("sum_pair_" + group, pg, theirs, self.core, self.TILES[group])
        self.sums[group] = ps

        def copies(ins, outs, send_sems, recv_sems):
            (ps_ref,), (out_ref,) = ins, outs
            _, _, c, chips = _place()
            return [_remote(ps_ref.at[2 * cx + cy], out_ref.at[j], send_sems, recv_sems, j, (cx, cy, c))
                    for j, (cx, cy) in enumerate(chips)]

        def start(*refs):
            for cp in copies(*refs):
                cp.start()

        def wait(*refs):
            for cp in copies(*refs):
                cp.wait()

        return _Riding((ps,), (jax.ShapeDtypeStruct((3,) + ps.shape[1:], ps.dtype),), 3, start, wait)

    def reduced(self, group):
        return _chip_sum("sum_chips_" + group, self.sums[group], self.landed[group], self.chip, self.TILES[group])


def _adamw(w, g, m, v):
    m = ADAM_B1 * m + (1.0 - ADAM_B1) * g
    v = ADAM_B2 * v + (1.0 - ADAM_B2) * (g * g)
    m_hat = m / (1.0 - ADAM_B1 ** ADAM_STEP)
    v_hat = v / (1.0 - ADAM_B2 ** ADAM_STEP)
    return -ADAM_LR * (m_hat / (jnp.sqrt(v_hat) + ADAM_EPS) + ADAM_WD * w), m, v


def _small_reduce_adamw(part, w, m, v):
    def body(part_ref, w_ref, m_ref, v_ref, g_ref, d_ref, nm_ref, nv_ref, all_ref, send_sems, recv_sems):
        x, y, c, chips = _place()
        me, sibling = (x, y, c), (x, y, 1 - c)

        def rows(px, py, pc):
            return all_ref.at[4 * px + 2 * py + pc]

        all_ref[4 * x + 2 * y + c] = part_ref[...]
        first = [_remote(part_ref, rows(*me), send_sems, recv_sems, 0, sibling)]
        first += [_remote(part_ref, rows(*me), send_sems, recv_sems, 1 + j, (cx, cy, c))
                  for j, (cx, cy) in enumerate(chips)]
        for cp in first:
            cp.start()
        passed = []
        for j, (cx, cy) in enumerate(chips):
            _remote(part_ref, rows(cx, cy, c), send_sems, recv_sems, 1 + j, me).wait_recv()
            cp = _remote(rows(cx, cy, c), rows(cx, cy, c), send_sems, recv_sems, 4 + j, sibling)
            cp.start()
            passed.append(cp)
        _remote(part_ref, rows(*sibling), send_sems, recv_sems, 0, me).wait_recv()
        for j, (cx, cy) in enumerate(chips):
            _remote(part_ref, rows(cx, cy, 1 - c), send_sems, recv_sems, 4 + j, me).wait_recv()
        for cp in first + passed:
            cp.wait_send()
        g = all_ref[0]
        for dev in range(1, N_DEV):
            g = g + all_ref[dev]
        delta, nm, nv = _adamw(w_ref[...], g, m_ref[...], v_ref[...])
        g_ref[...] = g
        d_ref[...] = delta
        nm_ref[...] = nm
        nv_ref[...] = nv

    whole = pl.BlockSpec(memory_space=pltpu.VMEM)
    shape = jax.ShapeDtypeStruct((SMALL_ROWS, PACK_W), F32)
    return pl.pallas_call(
        body, name="small_reduce_adamw", in_specs=[whole] * 4, out_specs=[whole] * 4, out_shape=[shape] * 4,
        scratch_shapes=[pltpu.VMEM((N_DEV, SMALL_ROWS, PACK_W), F32), pltpu.SemaphoreType.DMA((7,)),
                        pltpu.SemaphoreType.DMA((7,))],
        compiler_params=pltpu.CompilerParams(vmem_limit_bytes=VMEM_LIMIT))(part, w, m, v)


def kernel(x, p, norm_g, w_in, hg_lb, hg_norm_g, w_o_hg, s5_a_re, s5_a_im, s5_log_dt, s5_b_re, s5_b_im, s5_c_re, s5_c_im, s5_d, w_glu, b_glu, w_o_s5, w_out, ple_norm_g, w_ple, w_ple_gate, final_norm_g, loss_target, m_norm_g, m_w_in, m_hg_lb, m_hg_norm_g, m_w_o_hg, m_s5_a_re, m_s5_a_im, m_s5_log_dt, m_s5_b_re, m_s5_b_im, m_s5_c_re, m_s5_c_im, m_s5_d, m_w_glu, m_b_glu, m_w_o_s5, m_w_out, m_ple_norm_g, m_w_ple, m_w_ple_gate, m_final_norm_g, v_norm_g, v_w_in, v_hg_lb, v_hg_norm_g, v_w_o_hg, v_s5_a_re, v_s5_a_im, v_s5_log_dt, v_s5_b_re, v_s5_b_im, v_s5_c_re, v_s5_c_im, v_s5_d, v_w_glu, v_b_glu, v_w_o_s5, v_w_out, v_ple_norm_g, v_w_ple, v_w_ple_gate, v_final_norm_g):
    given = dict(locals())
    wts = {n: given[n] for n in WEIGHTS}
    mom = {n: given["m_" + n] for n in WEIGHTS}
    var = {n: given["v_" + n] for n in WEIGHTS}
    cx, cy, cc = lax.axis_index("x"), lax.axis_index("y"), lax.axis_index("c")
    chip = (2 * cx + cy).astype(jnp.int32)

    core = cc.astype(jnp.int32)
    rest_shard = _pack_rest({n: wts[n][0] for n in REST})
    in_wire = wts["w_in"][0].astype(MXU_DTYPE).reshape(2, D_MODEL // 2, IN_SHARD)
    (w_in_all,) = _gather_shards([in_wire])
    w_in_all = lax.dynamic_update_slice(w_in_all, in_wire[None], (chip, 0, 0, 0)).reshape(N_CHIPS, D_MODEL, IN_SHARD)
    comm = _StepComm(rest_shard.astype(MXU_DTYPE).reshape(2, REST_ROWS // 2, PACK_W), chip, core)

    t_len = x.shape[1]
    loss_row, grad_x, g_big, g_small = _local_step(x.reshape(t_len, D_MODEL), p.reshape(t_len, -1),
                                                   loss_target.reshape(t_len, D_MODEL), {"w_in": w_in_all},
                                                   {n: wts[n] for n in SMALL}, comm)

    zero = jnp.zeros((), F32)
    sg, sd, snm, snv = _small_reduce_adamw(_pack_small(g_small, loss_row[0, 0]),
                                           _pack_small({n: wts[n] for n in SMALL}, zero),
                                           _pack_small({n: mom[n] for n in SMALL}, zero),
                                           _pack_small({n: var[n] for n in SMALL}, zero))
    (sg, loss), (sd, _), (snm, _), (snv, _) = (_unpack_small(a) for a in (sg, sd, snm, snv))

    halves = [comm.reduced("in"), comm.reduced("rest")]
    g_in, g_rest = [lax.dynamic_update_slice(got, mine[None], (core, 0, 0))
                    for got, mine in zip(_share_halves(halves), halves)]
    g_in, g_rest = g_in.reshape(D_MODEL, IN_SHARD), g_rest.reshape(REST_ROWS, PACK_W)

    def adam_f(wv, gv, mv, vv):
        return _adamw(wv, gv, mv, vv)

    d_in, nm_in, nv_in = _rowwise("adamw_in", adam_f, D_MODEL, IN_TILE,
                                  [(wts["w_in"][0], IN_SHARD, 0), (g_in, IN_SHARD, 0), (mom["w_in"][0], IN_SHARD, 0),
                                   (var["w_in"][0], IN_SHARD, 0)], [], [(IN_SHARD, F32)] * 3)
    d_rest, nm_rest, nv_rest = _rowwise("adamw_rest", adam_f, REST_ROWS, REST_TILE,
                                        [(rest_shard, PACK_W, 0), (g_rest, PACK_W, 0),
                                         (_pack_rest({n: mom[n][0] for n in REST}), PACK_W, 0),
                                         (_pack_rest({n: var[n][0] for n in REST}), PACK_W, 0)], [],
                                        [(PACK_W, F32)] * 3)
    bg, bd, bnm, bnv = (dict(_unpack_rest(rest), w_in=a.reshape(1, D_MODEL, IN_SHARD))
                        for rest, a in ((g_rest, g_in), (d_rest, d_in), (nm_rest, nm_in), (nv_rest, nv_in)))

    outs = [loss, grad_x.reshape(x.shape)]
    for small, big in ((sg, bg), (sd, bd), (snm, bnm), (snv, bnv)):
        outs += [big[n] if n in BIG else small[n] for n in WEIGHTS]
    return tuple(outs)
```

```python
import functools
from typing import Callable, NamedTuple

import jax
import jax.numpy as jnp
from jax import lax
from jax.experimental import pallas as pl
from jax.experimental.pallas import tpu as pltpu

F32 = jnp.float32
MXU_DTYPE = jnp.bfloat16
WIRE_DTYPE = jnp.bfloat16
NORM_EPS = 1e-6
D_MODEL = 1024
HG_HEADS = 8
HG_DIM = 128
HG_CHUNK = 64
S5_WIDTH = 512
S5_GROUPS = 32
S5_GROUP = 16
S5_STATE = 64
S5_LANES = S5_GROUPS * S5_STATE
IN_COLS = 7168
SUBLANES = 8
VMEM_LIMIT = 56 * 1024 * 1024
HIGHEST = lax.Precision.HIGHEST
MESH = pl.DeviceIdType.MESH

ADAM_LR, ADAM_B1, ADAM_B2, ADAM_EPS, ADAM_WD, ADAM_STEP = 0.001, 0.9, 0.999, 1e-08, 0.01, 10

BIG = ("w_in", "w_o_hg", "w_glu", "w_o_s5", "w_out", "w_ple", "w_ple_gate")
BIG_SHAPE = {"w_in": (1024, 7168), "w_o_hg": (1024, 1024), "w_glu": (512, 1024), "w_o_s5": (512, 1024),
             "w_out": (1024, 1024), "w_ple": (256, 1024), "w_ple_gate": (1024, 1024)}
BIG_COL_SHARDED = ("w_in", "w_glu", "w_o_s5", "w_ple")
SMALL = ("norm_g", "hg_lb", "hg_norm_g", "s5_a_re", "s5_a_im", "s5_log_dt", "s5_b_re", "s5_b_im", "s5_c_re",
         "s5_c_im", "s5_d", "b_glu", "ple_norm_g", "final_norm_g")
SMALL_SHAPE = {"norm_g": (1, 1024), "hg_lb": (2, 1024), "hg_norm_g": (1, 1024), "s5_a_re": (1, 32, 64),
               "s5_a_im": (1, 32, 64), "s5_log_dt": (1, 32), "s5_b_re": (1, 32, 64, 16), "s5_b_im": (1, 32, 64, 16),
               "s5_c_re": (1, 32, 16, 64), "s5_c_im": (1, 32, 16, 64), "s5_d": (1, 32, 16), "b_glu": (1, 1024),
               "ple_norm_g": (1, 1024), "final_norm_g": (1024,)}
WEIGHTS = ("norm_g", "w_in", "hg_lb", "hg_norm_g", "w_o_hg", "s5_a_re", "s5_a_im", "s5_log_dt", "s5_b_re", "s5_b_im",
           "s5_c_re", "s5_c_im", "s5_d", "w_glu", "b_glu", "w_o_s5", "w_out", "ple_norm_g", "w_ple", "w_ple_gate",
           "final_norm_g")
N_CHIPS = 4
N_DEV = 8
PACK_W = 1024
SHARD_ROWS = sum(BIG_SHAPE[n][0] * BIG_SHAPE[n][1] for n in BIG) // (N_CHIPS * PACK_W)
HALF_ROWS = SHARD_ROWS // 2
SMALL_ROWS = 144


def _params(*sem):
    return pltpu.CompilerParams(dimension_semantics=sem, vmem_limit_bytes=VMEM_LIMIT)


def _sig(x):
    return 1.0 / (1.0 + jnp.exp(-x))


def _dsilu(z, s):
    return s * (1.0 + z * (1.0 - s))


def _mx(x):
    return x.astype(MXU_DTYPE)


def _dot(a, b, dims=(((1,), (0,)), ((), ()))):
    return lax.dot_general(_mx(a), _mx(b), dims, preferred_element_type=F32)


_NT = (((1,), (1,)), ((), ()))
_TN = (((0,), (0,)), ((), ()))


def _dot32(a, b):
    return jnp.dot(a, b, precision=HIGHEST, preferred_element_type=F32)


def _rms_bwd(dy, x, g):
    r = lax.rsqrt(jnp.mean(x * x, axis=-1, keepdims=True) + NORM_EPS)
    t = dy * g
    dx = r * t - x * (r * r * r) * jnp.mean(t * x, axis=-1, keepdims=True)
    return dx, jnp.sum(dy * x * r, axis=0, keepdims=True)


def _rowwise(name, fn, n_rows_total, tm, rows, consts, outs, accs=(), alias=None):
    n_r, n_c, n_o, n_a = len(rows), len(consts), len(outs), len(accs)

    def body(*refs):
        row_refs = refs[:n_r]
        const_refs = refs[n_r:n_r + n_c]
        pos = n_r + n_c + (1 if alias is not None else 0)
        out_refs = refs[pos:pos + n_o]
        acc_refs = refs[pos + n_o:pos + n_o + n_a]
        res = fn(*[r[...] for r in row_refs], *[r[...] for r in const_refs])
        for r, v in zip(out_refs, res[:n_o]):
            r[...] = v.astype(r.dtype)
        if n_a:
            @pl.when(pl.program_id(0) == 0)
            def _():
                for r in acc_refs:
                    r[...] = jnp.zeros_like(r)
            for r, v in zip(acc_refs, res[n_o:]):
                r[...] += v

    in_specs = [pl.BlockSpec((tm, w), functools.partial(lambda i, cb: (i, cb), cb=cb)) for (_, w, cb) in rows]
    in_specs += [pl.BlockSpec(c.shape, lambda i: (0, 0)) for c in consts]
    args = [a for (a, _, _) in rows] + list(consts)
    out_shape, out_specs = [], []
    for o in outs:
        w, dt = o[0], o[1]
        cb, total = (o[2], o[3]) if len(o) == 4 else (0, w)
        out_shape.append(jax.ShapeDtypeStruct((n_rows_total, total), dt))
        out_specs.append(pl.BlockSpec((tm, w), functools.partial(lambda i, cb: (i, cb), cb=cb)))
    io_alias = {}
    if alias is not None:
        in_specs.append(pl.BlockSpec(memory_space=pl.ANY))
        args.append(alias[0])
        io_alias = {len(args) - 1: alias[1]}
    for (r, w) in accs:
        out_shape.append(jax.ShapeDtypeStruct((r, w), F32))
        out_specs.append(pl.BlockSpec((r, w), lambda i: (0, 0)))
    res = pl.pallas_call(body, name=name, grid=(n_rows_total // tm,), in_specs=in_specs, out_specs=out_specs,
                         out_shape=out_shape, input_output_aliases=io_alias,
                         compiler_params=_params("arbitrary"))(*args)
    return res


class _Riding(NamedTuple):
    ins: tuple
    outs: tuple
    n_sems: int
    start: Callable
    wait: Callable


_HBM = pl.BlockSpec(memory_space=pl.ANY)


def _ride(riding, refs, n_in, n_out, n_scratch, first, last):
    if riding is None:
        return refs[:n_in], refs[n_in:n_in + n_out], refs[n_in + n_out:]
    r_in, r_out = len(riding.ins), len(riding.outs)
    ins, rins = refs[:n_in], refs[n_in:n_in + r_in]
    pos = n_in + r_in
    outs, routs = refs[pos:pos + n_out], refs[pos + n_out:pos + n_out + r_out]
    pos += n_out + r_out
    scratch, (send_sems, recv_sems) = refs[pos:pos + n_scratch], refs[pos + n_scratch:]

    @pl.when(first)
    def _():
        riding.start(rins, routs, send_sems, recv_sems)

    @pl.when(last)
    def _():
        riding.wait(rins, routs, send_sems, recv_sems)

    return ins, outs, scratch


def _riding_call(riding, body, name, grid, in_specs, args, out_specs, out_shape, scratch, io_alias=None):
    if riding is not None:
        in_specs = list(in_specs) + [_HBM] * len(riding.ins)
        args = list(args) + list(riding.ins)
        out_specs = list(out_specs) + [_HBM] * len(riding.outs)
        out_shape = list(out_shape) + list(riding.outs)
        scratch = list(scratch) + [pltpu.SemaphoreType.DMA((riding.n_sems,))] * 2
    return pl.pallas_call(body, name=name, grid=grid, in_specs=in_specs, out_specs=out_specs, out_shape=out_shape,
                          scratch_shapes=scratch, input_output_aliases=io_alias or {},
                          compiler_params=_params(*(["arbitrary"] * len(grid))))(*args)


def _mm_nn(name, a, b, tm, tn, riding=None, prologue=None, consts=()):
    m, k = a.shape
    n = b.shape[1] if b.ndim == 2 else b.shape[0] * b.shape[2]
    grid = (n // tn, m // tm)
    n_out, scratch = (1, []) if prologue is None else (2, [pltpu.VMEM((m, k), MXU_DTYPE)])

    def body(*refs):
        j, i = pl.program_id(0), pl.program_id(1)
        ins, outs, kept = _ride(riding, refs, 2 + len(consts), n_out, len(scratch), (j == 0) & (i == 0),
                                (j == grid[0] - 1) & (i == grid[1] - 1))
        if prologue is None:
            left = ins[0][...]
        else:
            rows = pl.ds(pl.multiple_of(i * tm, tm), tm)

            @pl.when(j == 0)
            def _():
                tile = _mx(prologue(ins[0][...], *[c[...] for c in ins[2:]]))
                kept[0][rows, :] = tile
                outs[1][...] = tile

            left = kept[0][rows, :]
        outs[0][...] = _dot(left, ins[1][...])

    once = (lambda j, i: (i, 0)) if prologue is None else (lambda j, i: (jnp.where(j == 0, i, grid[1] - 1), 0))
    b_spec = (pl.BlockSpec((k, tn), lambda j, i: (0, j)) if b.ndim == 2
              else pl.BlockSpec((None, k, tn), lambda j, i: (j, 0, 0)))
    in_specs = [pl.BlockSpec((tm, k), once), b_spec]
    in_specs += [pl.BlockSpec(c.shape, lambda j, i: (0, 0)) for c in consts]
    out_specs = [pl.BlockSpec((tm, tn), lambda j, i: (i, j))]
    out_shape = [jax.ShapeDtypeStruct((m, n), F32)]
    if prologue is not None:
        out_specs.append(pl.BlockSpec((tm, k), once))
        out_shape.append(jax.ShapeDtypeStruct((m, k), MXU_DTYPE))
    res = _riding_call(riding, body, name, grid, in_specs, [a, b] + list(consts), out_specs, out_shape, scratch)
    return res[0] if riding is None and prologue is None else res


def _mm_nt(name, a, b, tm, tn):
    m, n = a.shape
    k = b.shape[0]
    steps = n // tn

    def body(a_ref, b_ref, o_ref, acc_ref):
        s = pl.program_id(1)

        @pl.when(s == 0)
        def _():
            acc_ref[...] = jnp.zeros_like(acc_ref)

        acc_ref[...] += _dot(a_ref[...], b_ref[...], _NT)

        @pl.when(s == steps - 1)
        def _():
            o_ref[...] = acc_ref[...]

    return pl.pallas_call(body, name=name, grid=(m // tm, steps),
                          in_specs=[pl.BlockSpec((tm, tn), lambda i, s: (i, s)),
                                    pl.BlockSpec((k, tn), lambda i, s: (0, s))],
                          out_specs=pl.BlockSpec((tm, k), lambda i, s: (i, 0)),
                          out_shape=jax.ShapeDtypeStruct((m, k), F32),
                          scratch_shapes=[pltpu.VMEM((tm, k), F32)],
                          compiler_params=_params("arbitrary", "arbitrary"))(a, b)


def _mm_nt_then(name, a, b, tm, tn, fn, rows, consts, outs, accs=(), alias=None, riding=None):
    m, n = a.shape
    k = b.shape[-2]
    steps = n // tn
    n_r, n_c, n_o, n_a = len(rows), len(consts), len(outs), len(accs)

    def body(*refs):
        a_ref, b_ref = refs[:2]
        row_refs = refs[2:2 + n_r]
        const_refs = refs[2 + n_r:2 + n_r + n_c]
        i, s = pl.program_id(0), pl.program_id(1)
        n_in = 2 + n_r + n_c + (1 if alias is not None else 0)
        _, outs_, (mm_ref,) = _ride(riding, refs, n_in, n_o + n_a, 1, (i == 0) & (s == 0),
                                    (i == m // tm - 1) & (s == steps - 1))
        out_refs, acc_refs = outs_[:n_o], outs_[n_o:]
        part = _dot(a_ref[...], b_ref[...], _NT)
        if steps > 1:
            @pl.when(s == 0)
            def _():
                mm_ref[...] = jnp.zeros_like(mm_ref)
            mm_ref[...] += part

        @pl.when(s == steps - 1)
        def _():
            res = fn(mm_ref[...] if steps > 1 else part, *[r[...] for r in row_refs], *[r[...] for r in const_refs])
            for r, v in zip(out_refs, res[:n_o]):
                r[...] = v.astype(r.dtype)
            if n_a:
                @pl.when(i == 0)
                def _():
                    for r in acc_refs:
                        r[...] = jnp.zeros_like(r)
                for r, v in zip(acc_refs, res[n_o:]):
                    r[...] += v

    b_spec = (pl.BlockSpec((k, tn), lambda i, s: (0, s)) if b.ndim == 2
              else pl.BlockSpec((None, k, tn), lambda i, s: (s, 0, 0)))
    in_specs = [pl.BlockSpec((tm, tn), lambda i, s: (i, s)), b_spec]
    in_specs += [pl.BlockSpec((tm, w), functools.partial(lambda i, s, cb: (i, cb), cb=cb)) for (_, w, cb) in rows]
    in_specs += [pl.BlockSpec(c.shape, lambda i, s: (0, 0)) for c in consts]
    args = [a, b] + [r[0] for r in rows] + list(consts)
    out_shape, out_specs = [], []
    for o in outs:
        w, dt = o[0], o[1]
        cb, total = (o[2], o[3]) if len(o) == 4 else (0, w)
        out_shape.append(jax.ShapeDtypeStruct((m, total), dt))
        out_specs.append(pl.BlockSpec((tm, w), functools.partial(lambda i, s, cb: (i, cb), cb=cb)))
    io_alias = {}
    if alias is not None:
        in_specs.append(pl.BlockSpec(memory_space=pl.ANY))
        args.append(alias[0])
        io_alias = {len(args) - 1: alias[1]}
    for (r, w) in accs:
        out_shape.append(jax.ShapeDtypeStruct((r, w), F32))
        out_specs.append(pl.BlockSpec((r, w), lambda i, s: (0, 0)))
    return _riding_call(riding, body, name, (m // tm, steps), in_specs, args, out_specs, out_shape,
                        [pltpu.VMEM((tm, k), F32)], io_alias)


def _mm_tn(name, a, b, tk, tn, col_shards=False, riding=None):
    t, k = a.shape
    n = b.shape[1]
    steps = t // tk

    def body(*refs):
        j, s = pl.program_id(0), pl.program_id(1)
        (a_ref, b_ref), (o_ref,), (acc_ref,) = _ride(riding, refs, 2, 1, 1, (j == 0) & (s == 0),
                                                     (j == n // tn - 1) & (s == steps - 1))

        @pl.when(s == 0)
        def _():
            acc_ref[...] = jnp.zeros_like(acc_ref)

        acc_ref[...] += _dot(a_ref[...], b_ref[...], _TN)

        @pl.when(s == steps - 1)
        def _():
            o_ref[...] = acc_ref[...]

    if col_shards:
        out_spec = pl.BlockSpec((None, k, tn), lambda j, s: (j, 0, 0))
        out_shape = jax.ShapeDtypeStruct((n // tn, k, tn), F32)
    else:
        out_spec = pl.BlockSpec((k, tn), lambda j, s: (0, j))
        out_shape = jax.ShapeDtypeStruct((k, n), F32)
    res = _riding_call(riding, body, name, (n // tn, steps),
                       [pl.BlockSpec((tk, k), lambda j, s: (s, 0)), pl.BlockSpec((tk, tn), lambda j, s: (s, j))],
                       [a, b], [out_spec], [out_shape], [pltpu.VMEM((k, tn), F32)])
    return res[0] if riding is None else res


def _hg_chunk_terms(q, f, lb):
    sig = _sig(f)
    fv = lb + (1.0 - lb) * sig
    kk = (1.0 - lb) * (1.0 - sig)
    row = lax.broadcasted_iota(jnp.int32, (HG_CHUNK, HG_CHUNK), 0)
    col = lax.broadcasted_iota(jnp.int32, (HG_CHUNK, HG_CHUNK), 1)
    b = _dot32((row >= col).astype(F32), jnp.log(fv))
    b_mid = b[HG_CHUNK // 2 - 1:HG_CHUNK // 2, :]
    b_last = b[HG_CHUNK - 1:HG_CHUNK, :]
    e_mid = jnp.exp(b - b_mid)
    e_mid_inv = jnp.exp(b_mid - b)
    e_b = jnp.exp(b)
    e_last = jnp.exp(b_last - b)
    return sig, fv, kk, row >= col, row <= col, q * e_mid, kk * e_mid_inv, e_mid, e_mid_inv, e_b, e_last, jnp.exp(b_last)


def _hgrn2_fwd(proj, hg_lb, hg_norm_g, t_len, tb):
    nck = tb // HG_CHUNK

    def body(p_ref, lb_ref, gn_ref, o_ref, act_ref, sp_ref, st_ref):
        @pl.when(pl.program_id(0) == 0)
        def _():
            st_ref[...] = jnp.zeros_like(st_ref)

        for c in range(nck):
            r = pl.ds(c * HG_CHUNK, HG_CHUNK)
            for h in range(HG_HEADS):
                hs = pl.ds(h * HG_DIM, HG_DIM)
                lb = _sig(lb_ref[0:1, hs] - lb_ref[1:2, hs])
                q = p_ref[r, pl.ds(h * HG_DIM, HG_DIM)]
                f = p_ref[r, pl.ds(1024 + h * HG_DIM, HG_DIM)]
                v = p_ref[r, pl.ds(2048 + h * HG_DIM, HG_DIM)]
                _, _, kk, causal, _, a, bm, _, _, e_b, e_last, dc = _hg_chunk_terms(q, f, lb)
                scores = jnp.where(causal, _dot(a, bm, _NT), 0.0)
                st = st_ref[h]
                o = _dot(scores, v) + _dot(q * e_b, st, _NT)
                sp_ref[h, c] = st
                st_ref[h] = dc * st + _dot(v, kk * e_last, _TN)
                o_ref[r, hs] = o

        for h in range(HG_HEADS):
            hs = pl.ds(h * HG_DIM, HG_DIM)
            o = o_ref[:, hs]
            rr = lax.rsqrt(jnp.mean(o * o, axis=-1, keepdims=True) + NORM_EPS)
            g = p_ref[:, pl.ds(3072 + h * HG_DIM, HG_DIM)]
            act_ref[:, hs] = (o * rr * gn_ref[:, hs] * (g * _sig(g))).astype(act_ref.dtype)

    nb = t_len // tb
    return pl.pallas_call(
        body, name="hgrn2_fwd", grid=(nb,),
        in_specs=[pl.BlockSpec((tb, 4096), lambda i: (i, 0)),
                  pl.BlockSpec((2, 1024), lambda i: (0, 0)),
                  pl.BlockSpec((1, 1024), lambda i: (0, 0))],
        out_specs=[pl.BlockSpec((tb, 1024), lambda i: (i, 0)),
                   pl.BlockSpec((tb, 1024), lambda i: (i, 0)),
                   pl.BlockSpec((HG_HEADS, nck, HG_DIM, HG_DIM), lambda i: (0, i, 0, 0))],
        out_shape=[jax.ShapeDtypeStruct((t_len, 1024), F32),
                   jax.ShapeDtypeStruct((t_len, 1024), MXU_DTYPE),
                   jax.ShapeDtypeStruct((HG_HEADS, t_len // HG_CHUNK, HG_DIM, HG_DIM), F32)],
        scratch_shapes=[pltpu.VMEM((HG_HEADS, HG_DIM, HG_DIM), F32)],
        compiler_params=_params("arbitrary"))(proj, hg_lb, hg_norm_g)


def _hgrn2_bwd(proj, d_o, s_prev, hg_lb, dproj, t_len, tb):
    nck = tb // HG_CHUNK
    nb = t_len // tb

    def body(p_ref, do_ref, sp_ref, lb_ref, _, dp_ref, dlb_ref, ds_ref, acc_ref):
        @pl.when(pl.program_id(0) == 0)
        def _():
            ds_ref[...] = jnp.zeros_like(ds_ref)
            acc_ref[...] = jnp.zeros_like(acc_ref)

        for c in reversed(range(nck)):
            r = pl.ds(c * HG_CHUNK, HG_CHUNK)
            for h in range(HG_HEADS):
                hs = pl.ds(h * HG_DIM, HG_DIM)
                lb = _sig(lb_ref[0:1, hs] - lb_ref[1:2, hs])
                q = p_ref[r, pl.ds(h * HG_DIM, HG_DIM)]
                f = p_ref[r, pl.ds(1024 + h * HG_DIM, HG_DIM)]
                v = p_ref[r, pl.ds(2048 + h * HG_DIM, HG_DIM)]
                do = do_ref[r, hs]
                sig, fv, kk, causal, anti, a, bm, e_mid, e_mid_inv, e_b, e_last, dc = _hg_chunk_terms(q, f, lb)
                qd = q * e_b
                kd = kk * e_last
                st = sp_ref[h, c]
                dst = ds_ref[h]
                scores = jnp.where(causal, _dot(a, bm, _NT), 0.0)
                dscores = jnp.where(causal, _dot(do, v, _NT), 0.0)
                dv = _dot(scores, do, _TN) + _dot(kd, dst, _NT)
                da = _dot(dscores, bm)
                dbm = _dot(dscores, a, _TN)
                dqd = _dot(do, st)
                dkd = _dot(v, dst)
                ddc = jnp.sum(dst * st, axis=0, keepdims=True)
                ds_ref[h] = _dot(do, qd, _TN) + dc * dst
                dq = da * e_mid + dqd * e_b
                dk = dbm * e_mid_inv + dkd * e_last
                db = da * a - dbm * bm + dqd * qd - dkd * kd
                extra = jnp.sum(dkd * kd, axis=0, keepdims=True) + ddc * dc
                dlogf = _dot32(anti.astype(F32), db) + extra
                dfv_k = dlogf / fv - dk
                dp_ref[r, pl.ds(h * HG_DIM, HG_DIM)] = dq
                dp_ref[r, pl.ds(1024 + h * HG_DIM, HG_DIM)] = dfv_k * (1.0 - lb) * sig * (1.0 - sig)
                dp_ref[r, pl.ds(2048 + h * HG_DIM, HG_DIM)] = dv
                acc_ref[:, hs] += jnp.sum(dfv_k * (1.0 - sig), axis=0, keepdims=True)

        @pl.when(pl.program_id(0) == nb - 1)
        def _():
            lb_all = _sig(lb_ref[0:1, :] - lb_ref[1:2, :])
            g0 = acc_ref[...] * lb_all * (1.0 - lb_all)
            dlb_ref[0:1, :] = g0
            dlb_ref[1:2, :] = -g0

    return pl.pallas_call(
        body, name="hgrn2_bwd", grid=(nb,),
        in_specs=[pl.BlockSpec((tb, 3072), lambda i: (nb - 1 - i, 0)),
                  pl.BlockSpec((tb, 1024), lambda i: (nb - 1 - i, 0)),
                  pl.BlockSpec((HG_HEADS, nck, HG_DIM, HG_DIM), lambda i: (0, nb - 1 - i, 0, 0)),
                  pl.BlockSpec((2, 1024), lambda i: (0, 0)),
                  pl.BlockSpec(memory_space=pl.ANY)],
        out_specs=[pl.BlockSpec((tb, 3072), lambda i: (nb - 1 - i, 0)),
                   pl.BlockSpec((2, 1024), lambda i: (0, 0))],
        out_shape=[jax.ShapeDtypeStruct((t_len, IN_COLS), F32), jax.ShapeDtypeStruct((2, 1024), F32)],
        scratch_shapes=[pltpu.VMEM((HG_HEADS, HG_DIM, HG_DIM), F32), pltpu.VMEM((1, 1024), F32)],
        input_output_aliases={4: 0},
        compiler_params=_params("arbitrary"))(proj, d_o, s_prev, hg_lb, dproj)


def _dot01(m01, x):
    m = m01.astype(MXU_DTYPE)
    hi = x.astype(MXU_DTYPE)
    r1 = x - hi.astype(F32)
    mid = r1.astype(MXU_DTYPE)
    lo = (r1 - mid.astype(F32)).astype(MXU_DTYPE)
    dot = lambda v: jnp.dot(m, v, preferred_element_type=F32)
    return dot(hi) + dot(mid) + dot(lo)


def _chunk_rows(x, offset, nck):
    return jnp.concatenate([jnp.broadcast_to(x[c * HG_CHUNK + offset:c * HG_CHUNK + offset + 1, :],
                                             (HG_CHUNK, x.shape[1])) for c in range(nck)], axis=0)


def _hg_block_terms(q, f, lb, tb):
    nck = tb // HG_CHUNK
    sig = _sig(f)
    fv = lb + (1.0 - lb) * sig
    kk = (1.0 - lb) * (1.0 - sig)
    row = lax.broadcasted_iota(jnp.int32, (tb, tb), 0)
    col = lax.broadcasted_iota(jnp.int32, (tb, tb), 1)
    same = jnp.right_shift(row, 6) == jnp.right_shift(col, 6)
    causal, anti = same & (row >= col), same & (row <= col)
    b = _dot01(causal, jnp.log(fv))
    b_mid, b_last = _chunk_rows(b, HG_CHUNK // 2 - 1, nck), _chunk_rows(b, HG_CHUNK - 1, nck)
    e_mid, e_mid_inv = jnp.exp(b - b_mid), jnp.exp(b_mid - b)
    e_b, e_last = jnp.exp(b), jnp.exp(b_last - b)
    dcs = [jnp.exp(b[c * HG_CHUNK + HG_CHUNK - 1:(c + 1) * HG_CHUNK, :]) for c in range(nck)]
    return sig, fv, kk, causal, anti, e_mid, e_mid_inv, e_b, e_last, dcs


def _hgrn2_fwd2(proj, hg_lb, hg_norm_g, t_len, tb):
    nck = tb // HG_CHUNK

    def body(p_ref, lb_ref, gn_ref, o_ref, act_ref, sp_ref, st_ref, a_s, bm_s, qd_s, kd_s, v_s, sc_s, inc_s):
        @pl.when(pl.program_id(0) == 0)
        def _():
            st_ref[...] = jnp.zeros_like(st_ref)

        lb = _sig(lb_ref[0:1, :] - lb_ref[1:2, :])
        q = p_ref[:, pl.ds(0, 1024)]
        _, _, kk, causal, _, e_mid, e_mid_inv, e_b, e_last, dcs = _hg_block_terms(q, p_ref[:, pl.ds(1024, 1024)],
                                                                                   lb, tb)
        a_s[...] = _mx(q * e_mid)
        bm_s[...] = _mx(kk * e_mid_inv)
        qd_s[...] = _mx(q * e_b)
        kd_s[...] = _mx(kk * e_last)
        v_s[...] = _mx(p_ref[:, pl.ds(2048, 1024)])
        heads = [pl.ds(h * HG_DIM, HG_DIM) for h in range(HG_HEADS)]
        chunks = [pl.ds(c * HG_CHUNK, HG_CHUNK) for c in range(nck)]
        for h, hs in enumerate(heads):
            sc_s[h] = _mx(jnp.where(causal, _dot(a_s[:, hs], bm_s[:, hs], _NT), 0.0))
        for h, hs in enumerate(heads):
            o_ref[:, hs] = _dot(sc_s[h], v_s[:, hs])
        for h, hs in enumerate(heads):
            for c, r in enumerate(chunks):
                inc_s[h, c] = _dot(v_s[r, hs], kd_s[r, hs], _TN)
        for c in range(nck):
            for h in range(HG_HEADS):
                st = st_ref[h]
                sp_ref[h, c] = st
                st_ref[h] = dcs[c][:, h * HG_DIM:(h + 1) * HG_DIM] * st + inc_s[h, c]
        for c, r in enumerate(chunks):
            for h, hs in enumerate(heads):
                o_ref[r, hs] += _dot(qd_s[r, hs], sp_ref[h, c], _NT)
        for h, hs in enumerate(heads):
            o = o_ref[:, hs]
            rr = lax.rsqrt(jnp.mean(o * o, axis=-1, keepdims=True) + NORM_EPS)
            g = p_ref[:, pl.ds(3072 + h * HG_DIM, HG_DIM)]
            act_ref[:, hs] = (o * rr * gn_ref[:, hs] * (g * _sig(g))).astype(act_ref.dtype)

    nb = t_len // tb
    return pl.pallas_call(
        body, name="hgrn2_fwd", grid=(nb,),
        in_specs=[pl.BlockSpec((tb, 4096), lambda i: (i, 0)),
                  pl.BlockSpec((2, 1024), lambda i: (0, 0)),
                  pl.BlockSpec((1, 1024), lambda i: (0, 0))],
        out_specs=[pl.BlockSpec((tb, 1024), lambda i: (i, 0)),
                   pl.BlockSpec((tb, 1024), lambda i: (i, 0)),
                   pl.BlockSpec((HG_HEADS, nck, HG_DIM, HG_DIM), lambda i: (0, i, 0, 0))],
        out_shape=[jax.ShapeDtypeStruct((t_len, 1024), F32),
                   jax.ShapeDtypeStruct((t_len, 1024), MXU_DTYPE),
                   jax.ShapeDtypeStruct((HG_HEADS, t_len // HG_CHUNK, HG_DIM, HG_DIM), F32)],
        scratch_shapes=[pltpu.VMEM((HG_HEADS, HG_DIM, HG_DIM), F32)] + [pltpu.VMEM((tb, 1024), MXU_DTYPE)] * 5
                       + [pltpu.VMEM((HG_HEADS, tb, tb), MXU_DTYPE), pltpu.VMEM((HG_HEADS, nck, HG_DIM, HG_DIM), F32)],
        compiler_params=_params("arbitrary"))(proj, hg_lb, hg_norm_g)


def _hgrn2_bwd2(proj, d_o, s_prev, hg_lb, dproj, t_len, tb, riding=None):
    nck = tb // HG_CHUNK
    nb = t_len // tb

    def body(*refs):
        step = pl.program_id(0)
        ((p_ref, do_ref, sp_ref, lb_ref, _), (dp_ref, dlb_ref),
         (ds_ref, acc_ref, a_s, bm_s, qd_s, kd_s, v_s, do_s, da_s, dbm_s, dqd_s, dkd_s, dv_s, ex_s, sc_s, dsc_s,
          up_s)) = _ride(riding, refs, 5, 2, 17, step == 0, step == nb - 1)

        @pl.when(pl.program_id(0) == 0)
        def _():
            ds_ref[...] = jnp.zeros_like(ds_ref)
            acc_ref[...] = jnp.zeros_like(acc_ref)

        lb = _sig(lb_ref[0:1, :] - lb_ref[1:2, :])
        q = p_ref[:, pl.ds(0, 1024)]
        sig, fv, kk, causal, anti, e_mid, e_mid_inv, e_b, e_last, dcs = _hg_block_terms(
            q, p_ref[:, pl.ds(1024, 1024)], lb, tb)
        a, bm, qd, kd = q * e_mid, kk * e_mid_inv, q * e_b, kk * e_last
        a_s[...] = _mx(a)
        bm_s[...] = _mx(bm)
        qd_s[...] = _mx(qd)
        kd_s[...] = _mx(kd)
        v_s[...] = _mx(p_ref[:, pl.ds(2048, 1024)])
        do_s[...] = _mx(do_ref[...])
        heads = [pl.ds(h * HG_DIM, HG_DIM) for h in range(HG_HEADS)]
        chunks = [pl.ds(c * HG_CHUNK, HG_CHUNK) for c in range(nck)]
        for h, hs in enumerate(heads):
            sc_s[h] = _mx(jnp.where(causal, _dot(a_s[:, hs], bm_s[:, hs], _NT), 0.0))
            dsc_s[h] = _mx(jnp.where(causal, _dot(do_s[:, hs], v_s[:, hs], _NT), 0.0))
        for h, hs in enumerate(heads):
            dv_s[:, hs] = _dot(sc_s[h], do_s[:, hs], _TN)
            da_s[:, hs] = _dot(dsc_s[h], bm_s[:, hs])
            dbm_s[:, hs] = _dot(dsc_s[h], a_s[:, hs], _TN)
        for h, hs in enumerate(heads):
            for c, r in enumerate(chunks):
                up_s[h, c] = _dot(do_s[r, hs], qd_s[r, hs], _TN)
                dqd_s[r, hs] = _dot(do_s[r, hs], sp_ref[h, c])
        for c in reversed(range(nck)):
            r = chunks[c]
            for h, hs in enumerate(heads):
                dst = ds_ref[h]
                dc = dcs[c][:, h * HG_DIM:(h + 1) * HG_DIM]
                dv_s[r, hs] += _dot(kd_s[r, hs], dst, _NT)
                dkd_s[r, hs] = _dot(v_s[r, hs], dst)
                ex_s[c:c + 1, hs] = jnp.sum(dst * sp_ref[h, c], axis=0, keepdims=True) * dc
                ds_ref[h] = up_s[h, c] + dc * dst
        da, dbm, dqd, dkd = da_s[...], dbm_s[...], dqd_s[...], dkd_s[...]
        dq = da * e_mid + dqd * e_b
        dk = dbm * e_mid_inv + dkd * e_last
        db = da * a - dbm * bm + dqd * qd - dkd * kd
        dkk = dkd * kd
        extra = jnp.concatenate(
            [jnp.broadcast_to(jnp.sum(dkk[c * HG_CHUNK:(c + 1) * HG_CHUNK], axis=0, keepdims=True)
                              + ex_s[c:c + 1, :], (HG_CHUNK, 1024)) for c in range(nck)], axis=0)
        dlogf = _dot01(anti, db) + extra
        dfv_k = dlogf / fv - dk
        dp_ref[:, pl.ds(0, 1024)] = dq.astype(dp_ref.dtype)
        dp_ref[:, pl.ds(1024, 1024)] = (dfv_k * (1.0 - lb) * sig * (1.0 - sig)).astype(dp_ref.dtype)
        dp_ref[:, pl.ds(2048, 1024)] = dv_s[...].astype(dp_ref.dtype)
        acc_ref[...] += jnp.sum(dfv_k * (1.0 - sig), axis=0, keepdims=True)

        @pl.when(pl.program_id(0) == nb - 1)
        def _():
            g0 = acc_ref[...] * lb * (1.0 - lb)
            dlb_ref[0:1, :] = g0
            dlb_ref[1:2, :] = -g0

    return _riding_call(
        riding, body, "hgrn2_bwd", (nb,),
        [pl.BlockSpec((tb, 3072), lambda i: (nb - 1 - i, 0)),
         pl.BlockSpec((tb, 1024), lambda i: (nb - 1 - i, 0)),
         pl.BlockSpec((HG_HEADS, nck, HG_DIM, HG_DIM), lambda i: (0, nb - 1 - i, 0, 0)),
         pl.BlockSpec((2, 1024), lambda i: (0, 0)),
         pl.BlockSpec(memory_space=pl.ANY)],
        [proj, d_o, s_prev, hg_lb, dproj],
        [pl.BlockSpec((tb, 3072), lambda i: (nb - 1 - i, 0)), pl.BlockSpec((2, 1024), lambda i: (0, 0))],
        [jax.ShapeDtypeStruct((t_len, IN_COLS), dproj.dtype), jax.ShapeDtypeStruct((2, 1024), F32)],
        [pltpu.VMEM((HG_HEADS, HG_DIM, HG_DIM), F32), pltpu.VMEM((1, 1024), F32)]
        + [pltpu.VMEM((tb, 1024), MXU_DTYPE)] * 6 + [pltpu.VMEM((tb, 1024), F32)] * 5
        + [pltpu.VMEM((SUBLANES, 1024), F32)] + [pltpu.VMEM((HG_HEADS, tb, tb), MXU_DTYPE)] * 2
        + [pltpu.VMEM((HG_HEADS, nck, HG_DIM, HG_DIM), F32)], {4: 0})


def _s5_prep(a_re, a_im, log_dt, b_re_t, b_im_t):
    def body(ar_ref, ai_ref, ldt_ref, br_ref, bi_ref, lam_ref, pr_ref, pi_ref, bbr_ref, bbi_ref):
        ar, ai = ar_ref[...], ai_ref[...]
        dt = jnp.exp(ldt_ref[...])
        mag = jnp.exp(ar * dt)
        lr, li = mag * jnp.cos(ai * dt), mag * jnp.sin(ai * dt)
        den = ar * ar + ai * ai
        nr = lr - 1.0
        sr = (nr * ar + li * ai) / den
        si = (li * ar - nr * ai) / den
        lam_ref[0:1, :] = lr
        lam_ref[1:2, :] = li
        cr, ci = lr, li
        for i in range(SUBLANES):
            pr_ref[i:i + 1, :] = cr
            pi_ref[i:i + 1, :] = ci
            cr, ci = cr * lr - ci * li, cr * li + ci * lr
        br, bi = br_ref[...], bi_ref[...]
        bbr_ref[...] = sr * br - si * bi
        bbi_ref[...] = sr * bi + si * br

    whole = pl.BlockSpec(memory_space=pltpu.VMEM)
    return pl.pallas_call(
        body, name="s5_prep", in_specs=[whole] * 5, out_specs=[whole] * 5,
        out_shape=[jax.ShapeDtypeStruct((2, S5_LANES), F32), jax.ShapeDtypeStruct((SUBLANES, S5_LANES), F32),
                   jax.ShapeDtypeStruct((SUBLANES, S5_LANES), F32), jax.ShapeDtypeStruct((S5_GROUP, S5_LANES), F32),
                   jax.ShapeDtypeStruct((S5_GROUP, S5_LANES), F32)])(a_re, a_im, log_dt, b_re_t, b_im_t)


def _s5_prep_bwd(a_re, a_im, log_dt, b_re_t, b_im_t, dlam, dbbr, dbbi):
    def body(ar_ref, ai_ref, ldt_ref, br_ref, bi_ref, dlam_ref, dbbr_ref, dbbi_ref,
             dar_ref, dai_ref, dldt_ref, dbr_ref, dbi_ref):
        ar, ai = ar_ref[...], ai_ref[...]
        dt = jnp.exp(ldt_ref[...])
        mag = jnp.exp(ar * dt)
        cs, sn = jnp.cos(ai * dt), jnp.sin(ai * dt)
        lr, li = mag * cs, mag * sn
        den = ar * ar + ai * ai
        nr = lr - 1.0
        sr = (nr * ar + li * ai) / den
        si = (li * ar - nr * ai) / den
        br, bi = br_ref[...], bi_ref[...]
        gbr, gbi = dbbr_ref[...], dbbi_ref[...]
        dbr_ref[...] = sr * gbr + si * gbi
        dbi_ref[...] = sr * gbi - si * gbr
        dsr = jnp.sum(gbr * br + gbi * bi, axis=0, keepdims=True)
        dsi = jnp.sum(gbi * br - gbr * bi, axis=0, keepdims=True)
        dnr = (dsr * ar - dsi * ai) / den
        dli = dlam_ref[1:2, :] + (dsr * ai + dsi * ar) / den
        dlr = dlam_ref[0:1, :] + dnr
        dden = -(dsr * sr + dsi * si) / den
        dar = (dsr * nr + dsi * li) / den + dden * 2.0 * ar
        dai = (dsr * li - dsi * nr) / den + dden * 2.0 * ai
        dmag = dlr * cs + dli * sn
        dth = mag * (dli * cs - dlr * sn)
        dar_ref[...] = dar + dmag * mag * dt
        dai_ref[...] = dai + dth * dt
        ddt = (dmag * mag * ar + dth * ai) * dt
        lane = lax.broadcasted_iota(jnp.int32, (S5_LANES, 128), 0) // S5_STATE
        grp = lax.broadcasted_iota(jnp.int32, (S5_LANES, 128), 1)
        dldt_ref[...] = _dot32(jnp.broadcast_to(ddt, (SUBLANES, S5_LANES)), (lane == grp).astype(F32))

    whole = pl.BlockSpec(memory_space=pltpu.VMEM)
    return pl.pallas_call(
        body, name="s5_prep_bwd", in_specs=[whole] * 8, out_specs=[whole] * 5,
        out_shape=[jax.ShapeDtypeStruct((1, S5_LANES), F32), jax.ShapeDtypeStruct((1, S5_LANES), F32),
                   jax.ShapeDtypeStruct((SUBLANES, 128), F32), jax.ShapeDtypeStruct((S5_GROUP, S5_LANES), F32),
                   jax.ShapeDtypeStruct((S5_GROUP, S5_LANES), F32)])(a_re, a_im, log_dt, b_re_t, b_im_t, dlam, dbbr,
                                                                      dbbi)


S5_LANE_CHUNK = 512


def _shift_rows(x, s, rowid):
    if s > 0:
        return jnp.where(rowid >= s, pltpu.roll(x, s, 0), 0.0)
    return jnp.where(rowid < SUBLANES + s, pltpu.roll(x, SUBLANES + s, 0), 0.0)


def _scan8(xr, xi, pr, pi, sign, rowid):
    for s, row in ((1, 0), (2, 1), (4, 3)):
        lr, li = pr[row:row + 1, :], pi[row:row + 1, :]
        sr, si = _shift_rows(xr, sign * s, rowid), _shift_rows(xi, sign * s, rowid)
        xr, xi = xr + lr * sr - li * si, xi + lr * si + li * sr
    return xr, xi


def _s5_fwd(proj, pw_re, pw_im, bbr_bd, bbi_bd, crt_bd, cit_bd, d_row, t_len, tb):
    ngrp = tb // SUBLANES

    def body(u_ref, pr_ref, pi_ref, bbr_ref, bbi_ref, crt_ref, cit_ref, d_ref,
             hr_ref, hi_ref, ypre_ref, ys_ref, cr_ref, ci_ref):
        @pl.when(pl.program_id(0) == 0)
        def _():
            cr_ref[...] = jnp.zeros_like(cr_ref)
            ci_ref[...] = jnp.zeros_like(ci_ref)

        u = u_ref[...]
        hr_ref[...] = _dot(u, bbr_ref[...])
        hi_ref[...] = _dot(u, bbi_ref[...])
        rowid = lax.broadcasted_iota(jnp.int32, (SUBLANES, S5_LANE_CHUNK), 0)
        for lc in range(S5_LANES // S5_LANE_CHUNK):
            ls = pl.ds(lc * S5_LANE_CHUNK, S5_LANE_CHUNK)
            pr, pi = pr_ref[:, ls], pi_ref[:, ls]

            def group(g, carry, ls=ls, pr=pr, pi=pi):
                cr, ci = carry
                r = pl.ds(pl.multiple_of(g * SUBLANES, SUBLANES), SUBLANES)
                xr, xi = _scan8(hr_ref[r, ls], hi_ref[r, ls], pr, pi, 1, rowid)
                xr, xi = xr + pr * cr - pi * ci, xi + pr * ci + pi * cr
                hr_ref[r, ls] = xr
                hi_ref[r, ls] = xi
                return xr[SUBLANES - 1:SUBLANES, :], xi[SUBLANES - 1:SUBLANES, :]

            cr, ci = lax.fori_loop(0, ngrp, group, (cr_ref[:, ls], ci_ref[:, ls]))
            cr_ref[:, ls] = cr
            ci_ref[:, ls] = ci
        y = _dot(hr_ref[...], crt_ref[...]) - _dot(hi_ref[...], cit_ref[...]) + d_ref[...] * u
        ypre_ref[...] = y
        ys_ref[...] = jax.nn.gelu(y, approximate=True).astype(ys_ref.dtype)

    whole = pl.BlockSpec(memory_space=pltpu.VMEM)
    return pl.pallas_call(
        body, name="s5_fwd", grid=(t_len // tb,),
        in_specs=[pl.BlockSpec((tb, S5_WIDTH), lambda i: (i, 4096 // S5_WIDTH))] + [whole] * 7,
        out_specs=[pl.BlockSpec((tb, S5_LANES), lambda i: (i, 0)), pl.BlockSpec((tb, S5_LANES), lambda i: (i, 0)),
                   pl.BlockSpec((tb, S5_WIDTH), lambda i: (i, 0)), pl.BlockSpec((tb, S5_WIDTH), lambda i: (i, 0))],
        out_shape=[jax.ShapeDtypeStruct((t_len, S5_LANES), F32), jax.ShapeDtypeStruct((t_len, S5_LANES), F32),
                   jax.ShapeDtypeStruct((t_len, S5_WIDTH), F32), jax.ShapeDtypeStruct((t_len, S5_WIDTH), MXU_DTYPE)],
        scratch_shapes=[pltpu.VMEM((1, S5_LANES), F32), pltpu.VMEM((1, S5_LANES), F32)],
        compiler_params=_params("arbitrary"))(proj, pw_re, pw_im, bbr_bd, bbi_bd, crt_bd, cit_bd, d_row)


def _dgelu(x):
    c, a = 0.7978845608028654, 0.044715
    th = jnp.tanh(c * (x + a * x * x * x))
    return 0.5 * (1.0 + th) + 0.5 * x * (1.0 - th * th) * c * (1.0 + 3.0 * a * x * x)


def _s5_bwd(dgelu, y_pre, proj, h_re, h_im, pwr_re, pwr_im, bbr_bd, bbi_bd, cr_bd, ci_bd, d_row, dproj, t_len, tb):
    ngrp = tb // SUBLANES
    nb = t_len // tb

    def body(dg_ref, yp_ref, u_ref, hr_ref, hi_ref, pr_ref, pi_ref, bbr_ref, bbi_ref, cr_ref, ci_ref, d_ref, _,
             du_ref, dbbr_ref, dbbi_ref, dcr_ref, dci_ref, dd_ref, dlam_ref,
             gr_ref, gi_ref, car_ref, cai_ref, abr_ref, abi_ref, acr_ref, aci_ref, ad_ref, alr_ref, ali_ref, sem):
        @pl.when(pl.program_id(0) == 0)
        def _():
            for ref in (car_ref, cai_ref, abr_ref, abi_ref, acr_ref, aci_ref, ad_ref, alr_ref, ali_ref):
                ref[...] = jnp.zeros_like(ref)

        u = u_ref[...]
        dy = dg_ref[...] * _dgelu(yp_ref[...])
        gr_ref[...] = _dot(dy, cr_ref[...])
        gi_ref[...] = -_dot(dy, ci_ref[...])
        rowid = lax.broadcasted_iota(jnp.int32, (SUBLANES, S5_LANE_CHUNK), 0)
        for lc in range(S5_LANES // S5_LANE_CHUNK):
            ls = pl.ds(lc * S5_LANE_CHUNK, S5_LANE_CHUNK)
            pr, pi = pr_ref[:, ls], pi_ref[:, ls]
            fwd_rows_r = jnp.concatenate([pr[7:8], pr[6:7], pr[6:7], pr[4:5]], axis=0)
            fwd_rows_i = jnp.concatenate([pi[7:8], pi[6:7], pi[6:7], pi[4:5]], axis=0)

            def group(j, carry, ls=ls, pr=pr, pi=pi, fr=fwd_rows_r, fi=fwd_rows_i):
                cr, ci, slr, sli = carry
                g = ngrp - 1 - j
                r = pl.ds(pl.multiple_of(g * SUBLANES, SUBLANES), SUBLANES)
                xr, xi = _scan8(gr_ref[r, ls], gi_ref[r, ls], fr, fi, -1, rowid)
                xr, xi = xr + pr * cr - pi * ci, xi + pr * ci + pi * cr
                gr_ref[r, ls] = xr
                gi_ref[r, ls] = xi
                nr = jnp.where(rowid == SUBLANES - 1, cr, pltpu.roll(xr, SUBLANES - 1, 0))
                ni = jnp.where(rowid == SUBLANES - 1, ci, pltpu.roll(xi, SUBLANES - 1, 0))
                hr, hi = hr_ref[r, ls], hi_ref[r, ls]
                slr = slr + nr * hr + ni * hi
                sli = sli + ni * hr - nr * hi
                return xr[0:1, :], xi[0:1, :], slr, sli

            zero = jnp.zeros((SUBLANES, S5_LANE_CHUNK), F32)
            cr, ci, slr, sli = lax.fori_loop(0, ngrp, group, (car_ref[:, ls], cai_ref[:, ls], zero, zero))
            car_ref[:, ls] = cr
            cai_ref[:, ls] = ci
            alr_ref[:, ls] += jnp.sum(slr, axis=0, keepdims=True)
            ali_ref[:, ls] += jnp.sum(sli, axis=0, keepdims=True)
        gr, gi = gr_ref[...], gi_ref[...]
        du_ref[...] = _dot(gr, bbr_ref[...], _NT) + _dot(gi, bbi_ref[...], _NT) + d_ref[...] * dy
        abr_ref[...] += _dot(u, gr, _TN)
        abi_ref[...] += _dot(u, gi, _TN)
        acr_ref[...] += _dot(hr_ref[...], dy, _TN)
        aci_ref[...] -= _dot(hi_ref[...], dy, _TN)
        ad_ref[...] += jnp.sum(dy * u, axis=0, keepdims=True)

        @pl.when(pl.program_id(0) == nb - 1)
        def _():
            dd_ref[...] = ad_ref[...]
            dlam_ref[0:1, :] = alr_ref[...]
            dlam_ref[1:2, :] = ali_ref[...]
            copies = [pltpu.make_async_copy(s, d, sem.at[k]) for k, (s, d) in enumerate(
                ((abr_ref, dbbr_ref), (abi_ref, dbbi_ref), (acr_ref, dcr_ref), (aci_ref, dci_ref)))]
            for cp in copies:
                cp.start()
            for cp in copies:
                cp.wait()

    whole = pl.BlockSpec(memory_space=pltpu.VMEM)
    hbm = pl.BlockSpec(memory_space=pl.ANY)
    rev = lambda i: (nb - 1 - i, 0)
    return pl.pallas_call(
        body, name="s5_bwd", grid=(nb,),
        in_specs=[pl.BlockSpec((tb, S5_WIDTH), rev), pl.BlockSpec((tb, S5_WIDTH), rev),
                  pl.BlockSpec((tb, S5_WIDTH), lambda i: (nb - 1 - i, 4096 // S5_WIDTH)),
                  pl.BlockSpec((tb, S5_LANES), rev), pl.BlockSpec((tb, S5_LANES), rev)] + [whole] * 7 + [hbm],
        out_specs=[pl.BlockSpec((tb, S5_WIDTH), lambda i: (nb - 1 - i, 4096 // S5_WIDTH)), hbm, hbm, hbm, hbm,
                   pl.BlockSpec((1, S5_WIDTH), lambda i: (0, 0)), pl.BlockSpec((2, S5_LANES), lambda i: (0, 0))],
        out_shape=[jax.ShapeDtypeStruct((t_len, IN_COLS), F32),
                   jax.ShapeDtypeStruct((S5_WIDTH, S5_LANES), F32), jax.ShapeDtypeStruct((S5_WIDTH, S5_LANES), F32),
                   jax.ShapeDtypeStruct((S5_LANES, S5_WIDTH), F32), jax.ShapeDtypeStruct((S5_LANES, S5_WIDTH), F32),
                   jax.ShapeDtypeStruct((1, S5_WIDTH), F32), jax.ShapeDtypeStruct((2, S5_LANES), F32)],
        scratch_shapes=[pltpu.VMEM((tb, S5_LANES), F32), pltpu.VMEM((tb, S5_LANES), F32),
                        pltpu.VMEM((1, S5_LANES), F32), pltpu.VMEM((1, S5_LANES), F32),
                        pltpu.VMEM((S5_WIDTH, S5_LANES), F32), pltpu.VMEM((S5_WIDTH, S5_LANES), F32),
                        pltpu.VMEM((S5_LANES, S5_WIDTH), F32), pltpu.VMEM((S5_LANES, S5_WIDTH), F32),
                        pltpu.VMEM((1, S5_WIDTH), F32), pltpu.VMEM((1, S5_LANES), F32),
                        pltpu.VMEM((1, S5_LANES), F32), pltpu.SemaphoreType.DMA((4,))],
        input_output_aliases={12: 0},
        compiler_params=_params("arbitrary"))(dgelu, y_pre, proj, h_re, h_im, pwr_re, pwr_im, bbr_bd, bbi_bd, cr_bd,
                                              ci_bd, d_row, dproj)


S5_BLOCKS = 4
S5_BW = S5_WIDTH // S5_BLOCKS
S5_BL = S5_LANES // S5_BLOCKS
S5_LANE_BLOCKS = S5_LANES // 128
S5_SCAN_BLOCKS = 4


def _s5_powers(a_re, a_im, log_dt, b_re_t, b_im_t, seg):
    def body(ar_ref, ai_ref, ldt_ref, br_ref, bi_ref,
             rows_f, pfr_ref, pfi_ref, rows_r, prr_ref, pri_ref, bbr_ref, bbi_ref):
        ar, ai = ar_ref[...], ai_ref[...]
        dt = jnp.exp(ldt_ref[...])
        mag = jnp.exp(ar * dt)
        lr, li = mag * jnp.cos(ai * dt), mag * jnp.sin(ai * dt)
        den = ar * ar + ai * ai
        nr = lr - 1.0
        sr = (nr * ar + li * ai) / den
        si = (li * ar - nr * ai) / den
        wide = (SUBLANES, S5_LANES)
        cr, ci = lr, li
        for i in range(seg):
            pfr_ref[i] = jnp.broadcast_to(cr, wide)
            pfi_ref[i] = jnp.broadcast_to(ci, wide)
            prr_ref[seg - 1 - i] = jnp.broadcast_to(cr, wide)
            pri_ref[seg - 1 - i] = jnp.broadcast_to(-ci, wide)
            if i == seg - 1:
                for rows, sign in ((rows_f, 1.0), (rows_r, -1.0)):
                    rows[0:1, :] = lr
                    rows[1:2, :] = sign * li
                    rows[2:3, :] = cr
                    rows[3:4, :] = sign * ci
            cr, ci = cr * lr - ci * li, cr * li + ci * lr
        br, bi = br_ref[...], bi_ref[...]
        bbr_ref[...] = sr * br - si * bi
        bbi_ref[...] = sr * bi + si * br

    whole = pl.BlockSpec(memory_space=pltpu.VMEM)
    tables = [jax.ShapeDtypeStruct((4, S5_LANES), F32)] + [jax.ShapeDtypeStruct((seg, SUBLANES, S5_LANES), F32)] * 2
    bbar = [jax.ShapeDtypeStruct((S5_GROUP, S5_LANES), F32)] * 2
    res = pl.pallas_call(body, name="s5_prep", in_specs=[whole] * 5, out_specs=[whole] * 8,
                         out_shape=tables + tables + bbar)(a_re, a_im, log_dt, b_re_t, b_im_t)
    return res[0:3], res[3:6], res[6], res[7]


def _scan_tables(pw_re, pw_im, reverse):
    seg = pw_re.shape[0]
    if reverse:
        pw_re, pw_im = pw_re[::-1], -pw_im[::-1]
        one, full = seg - 1, 0
    else:
        one, full = 0, seg - 1
    rows = jnp.stack([pw_re[one], pw_im[one], pw_re[full], pw_im[full]])
    wide = lambda t: jnp.broadcast_to(t[:, None, :], (seg, SUBLANES, S5_LANES))
    return rows, wide(pw_re), wide(pw_im)


def _lanes(j):
    return pl.ds(j * 128, 128)


def _segment_scan(xr_ref, xi_ref, lam_ref, car_ref, cai_ref, cn_r, cn_i, blocks, seg, reverse):
    shape = (SUBLANES, 128)
    lrs = [jnp.broadcast_to(lam_ref[0:1, _lanes(j)], shape) for j in blocks]
    lis = [jnp.broadcast_to(lam_ref[1:2, _lanes(j)], shape) for j in blocks]

    def step(k, carry):
        idx = pl.ds(seg - 1 - k if reverse else k, SUBLANES, stride=seg)
        out = []
        for n, j in enumerate(blocks):
            cr, ci = carry[2 * n], carry[2 * n + 1]
            nr = lrs[n] * cr - lis[n] * ci + xr_ref[j, idx, :]
            ni = lrs[n] * ci + lis[n] * cr + xi_ref[j, idx, :]
            xr_ref[j, idx, :] = nr
            xi_ref[j, idx, :] = ni
            out += [nr, ni]
        return tuple(out)

    zero = jnp.zeros(shape, F32)
    fin = lax.fori_loop(0, seg, step, (zero,) * (2 * len(blocks)), unroll=2)
    for n, j in enumerate(blocks):
        ls = _lanes(j)
        fr, fi = fin[2 * n], fin[2 * n + 1]
        sr, si = lam_ref[2:3, ls], lam_ref[3:4, ls]
        pr, pi = car_ref[:, ls], cai_ref[:, ls]
        for s in (reversed(range(SUBLANES)) if reverse else range(SUBLANES)):
            cn_r[s:s + 1, ls] = pr
            cn_i[s:s + 1, ls] = pi
            pr, pi = fr[s:s + 1, :] + sr * pr - si * pi, fi[s:s + 1, :] + sr * pi + si * pr
        car_ref[:, ls] = pr
        cai_ref[:, ls] = pi


def _s5_fwd2(proj, lam_rows, p3_re, p3_im, bbr4, bbi4, crt4, cit4, d_row, t_len, tb):
    seg = tb // SUBLANES

    def body(u_ref, lam_ref, p3r_ref, p3i_ref, bbr_ref, bbi_ref, crt_ref, cit_ref, d_ref,
             hr_ref, hi_ref, ypre_ref, ys_ref, car_ref, cai_ref, cn_r, cn_i):
        @pl.when(pl.program_id(0) == 0)
        def _():
            car_ref[...] = jnp.zeros_like(car_ref)
            cai_ref[...] = jnp.zeros_like(cai_ref)

        u = u_ref[...]
        for i in range(S5_BLOCKS):
            ui = u[:, i * S5_BW:(i + 1) * S5_BW]
            xr, xi = _dot(ui, bbr_ref[i]), _dot(ui, bbi_ref[i])
            for jj in range(S5_BL // 128):
                hr_ref[i * (S5_BL // 128) + jj] = xr[:, jj * 128:(jj + 1) * 128]
                hi_ref[i * (S5_BL // 128) + jj] = xi[:, jj * 128:(jj + 1) * 128]
        for lc in range(S5_LANE_BLOCKS // S5_SCAN_BLOCKS):
            blocks = range(lc * S5_SCAN_BLOCKS, (lc + 1) * S5_SCAN_BLOCKS)
            _segment_scan(hr_ref, hi_ref, lam_ref, car_ref, cai_ref, cn_r, cn_i, blocks, seg, False)
            crs = [cn_r[:, _lanes(j)] for j in blocks]
            cis = [cn_i[:, _lanes(j)] for j in blocks]

            def fix(t, carry, blocks=blocks, crs=crs, cis=cis):
                idx = pl.ds(t, SUBLANES, stride=seg)
                for n, j in enumerate(blocks):
                    pr, pi = p3r_ref[t, :, _lanes(j)], p3i_ref[t, :, _lanes(j)]
                    hr_ref[j, idx, :] += pr * crs[n] - pi * cis[n]
                    hi_ref[j, idx, :] += pr * cis[n] + pi * crs[n]
                return carry

            lax.fori_loop(0, seg, fix, 0, unroll=2)
        for i in range(S5_BLOCKS):
            ws = pl.ds(i * S5_BW, S5_BW)
            js = range(i * (S5_BL // 128), (i + 1) * (S5_BL // 128))
            hr = jnp.concatenate([hr_ref[j] for j in js], axis=1)
            hi = jnp.concatenate([hi_ref[j] for j in js], axis=1)
            y = _dot(hr, crt_ref[i]) - _dot(hi, cit_ref[i]) + d_ref[:, ws] * u[:, i * S5_BW:(i + 1) * S5_BW]
            ypre_ref[:, ws] = y
            ys_ref[:, ws] = jax.nn.gelu(y, approximate=True).astype(ys_ref.dtype)

    whole = pl.BlockSpec(memory_space=pltpu.VMEM)
    h_spec = pl.BlockSpec((S5_LANE_BLOCKS, tb, 128), lambda i: (0, i, 0))
    return pl.pallas_call(
        body, name="s5_fwd", grid=(t_len // tb,),
        in_specs=[pl.BlockSpec((tb, S5_WIDTH), lambda i: (i, 4096 // S5_WIDTH))] + [whole] * 8,
        out_specs=[h_spec, h_spec,
                   pl.BlockSpec((tb, S5_WIDTH), lambda i: (i, 0)), pl.BlockSpec((tb, S5_WIDTH), lambda i: (i, 0))],
        out_shape=[jax.ShapeDtypeStruct((S5_LANE_BLOCKS, t_len, 128), F32),
                   jax.ShapeDtypeStruct((S5_LANE_BLOCKS, t_len, 128), F32),
                   jax.ShapeDtypeStruct((t_len, S5_WIDTH), F32), jax.ShapeDtypeStruct((t_len, S5_WIDTH), MXU_DTYPE)],
        scratch_shapes=[pltpu.VMEM((1, S5_LANES), F32), pltpu.VMEM((1, S5_LANES), F32),
                        pltpu.VMEM((SUBLANES, S5_LANES), F32), pltpu.VMEM((SUBLANES, S5_LANES), F32)],
        compiler_params=_params("arbitrary"))(proj, lam_rows, p3_re, p3_im, bbr4, bbi4, crt4, cit4, d_row)


def _s5_bwd2(dgelu, y_pre, proj, h_re, h_im, lam_rows, p3_re, p3_im, bbr4, bbi4, cr4, ci4, d_row, dproj, t_len, tb):
    seg = tb // SUBLANES
    nb = t_len // tb

    def body(dg_ref, yp_ref, u_ref, hr_ref, hi_ref, lam_ref, p3r_ref, p3i_ref, bbr_ref, bbi_ref, cr_ref, ci_ref,
             d_ref, _, du_ref, dbbr_ref, dbbi_ref, dcr_ref, dci_ref, dd_ref, dlam_ref,
             gr_ref, gi_ref, car_ref, cai_ref, cn_r, cn_i):
        @pl.when(pl.program_id(0) == 0)
        def _():
            for ref in (car_ref, cai_ref, dbbr_ref, dbbi_ref, dcr_ref, dci_ref, dd_ref, dlam_ref):
                ref[...] = jnp.zeros_like(ref)

        u = u_ref[...]
        dy = dg_ref[...] * _dgelu(yp_ref[...])
        nlb = S5_BL // 128
        for i in range(S5_BLOCKS):
            dyi = dy[:, i * S5_BW:(i + 1) * S5_BW]
            xr, xi = _dot(dyi, cr_ref[i]), -_dot(dyi, ci_ref[i])
            for jj in range(nlb):
                gr_ref[i * nlb + jj] = xr[:, jj * 128:(jj + 1) * 128]
                gi_ref[i * nlb + jj] = xi[:, jj * 128:(jj + 1) * 128]
        for lc in range(S5_LANE_BLOCKS // S5_SCAN_BLOCKS):
            blocks = range(lc * S5_SCAN_BLOCKS, (lc + 1) * S5_SCAN_BLOCKS)
            _segment_scan(gr_ref, gi_ref, lam_ref, car_ref, cai_ref, cn_r, cn_i, blocks, seg, True)
            crs = [cn_r[:, _lanes(j)] for j in blocks]
            cis = [cn_i[:, _lanes(j)] for j in blocks]

            def fix(k, carry, blocks=blocks, crs=crs, cis=cis):
                t = seg - 1 - k
                idx = pl.ds(t, SUBLANES, stride=seg)
                out = []
                for n, j in enumerate(blocks):
                    nr, ni, slr, sli = carry[4 * n:4 * n + 4]
                    pr, pi = p3r_ref[t, :, _lanes(j)], p3i_ref[t, :, _lanes(j)]
                    g_r = gr_ref[j, idx, :] + pr * crs[n] - pi * cis[n]
                    g_i = gi_ref[j, idx, :] + pr * cis[n] + pi * crs[n]
                    gr_ref[j, idx, :] = g_r
                    gi_ref[j, idx, :] = g_i
                    hr, hi = hr_ref[j, idx, :], hi_ref[j, idx, :]
                    out += [g_r, g_i, slr + nr * hr + ni * hi, sli + ni * hr - nr * hi]
                return tuple(out)

            zero = jnp.zeros((SUBLANES, 128), F32)
            init = []
            for n in range(len(blocks)):
                init += [crs[n], cis[n], zero, zero]
            fin = lax.fori_loop(0, seg, fix, tuple(init), unroll=2)
            for n, j in enumerate(blocks):
                dlam_ref[0:1, _lanes(j)] += jnp.sum(fin[4 * n + 2], axis=0, keepdims=True)
                dlam_ref[1:2, _lanes(j)] += jnp.sum(fin[4 * n + 3], axis=0, keepdims=True)
        for i in range(S5_BLOCKS):
            ws = pl.ds(i * S5_BW, S5_BW)
            js = range(i * nlb, (i + 1) * nlb)
            ui, dyi = u[:, i * S5_BW:(i + 1) * S5_BW], dy[:, i * S5_BW:(i + 1) * S5_BW]
            gr = jnp.concatenate([gr_ref[j] for j in js], axis=1)
            gi = jnp.concatenate([gi_ref[j] for j in js], axis=1)
            du_ref[:, ws] = _dot(gr, bbr_ref[i], _NT) + _dot(gi, bbi_ref[i], _NT) + d_ref[:, ws] * dyi
            dbbr_ref[i] += _dot(ui, gr, _TN)
            dbbi_ref[i] += _dot(ui, gi, _TN)
            dcr_ref[i] += _dot(jnp.concatenate([hr_ref[j] for j in js], axis=1), dyi, _TN)
            dci_ref[i] -= _dot(jnp.concatenate([hi_ref[j] for j in js], axis=1), dyi, _TN)
        dd_ref[...] += jnp.sum(dy * u, axis=0, keepdims=True)

    whole = pl.BlockSpec(memory_space=pltpu.VMEM)
    rev = lambda i: (nb - 1 - i, 0)
    const3 = lambda i: (0, 0, 0)
    h_spec = pl.BlockSpec((S5_LANE_BLOCKS, tb, 128), lambda i: (0, nb - 1 - i, 0))
    return pl.pallas_call(
        body, name="s5_bwd", grid=(nb,),
        in_specs=[pl.BlockSpec((tb, S5_WIDTH), rev), pl.BlockSpec((tb, S5_WIDTH), rev),
                  pl.BlockSpec((tb, S5_WIDTH), lambda i: (nb - 1 - i, 4096 // S5_WIDTH)),
                  h_spec, h_spec] + [whole] * 8
                 + [pl.BlockSpec(memory_space=pl.ANY)],
        out_specs=[pl.BlockSpec((tb, S5_WIDTH), lambda i: (nb - 1 - i, 4096 // S5_WIDTH)),
                   pl.BlockSpec((S5_BLOCKS, S5_BW, S5_BL), const3), pl.BlockSpec((S5_BLOCKS, S5_BW, S5_BL), const3),
                   pl.BlockSpec((S5_BLOCKS, S5_BL, S5_BW), const3), pl.BlockSpec((S5_BLOCKS, S5_BL, S5_BW), const3),
                   pl.BlockSpec((1, S5_WIDTH), lambda i: (0, 0)), pl.BlockSpec((2, S5_LANES), lambda i: (0, 0))],
        out_shape=[jax.ShapeDtypeStruct((t_len, IN_COLS), F32),
                   jax.ShapeDtypeStruct((S5_BLOCKS, S5_BW, S5_BL), F32),
                   jax.ShapeDtypeStruct((S5_BLOCKS, S5_BW, S5_BL), F32),
                   jax.ShapeDtypeStruct((S5_BLOCKS, S5_BL, S5_BW), F32),
                   jax.ShapeDtypeStruct((S5_BLOCKS, S5_BL, S5_BW), F32),
                   jax.ShapeDtypeStruct((1, S5_WIDTH), F32), jax.ShapeDtypeStruct((2, S5_LANES), F32)],
        scratch_shapes=[pltpu.VMEM((S5_LANE_BLOCKS, tb, 128), F32), pltpu.VMEM((S5_LANE_BLOCKS, tb, 128), F32),
                        pltpu.VMEM((1, S5_LANES), F32), pltpu.VMEM((1, S5_LANES), F32),
                        pltpu.VMEM((SUBLANES, S5_LANES), F32), pltpu.VMEM((SUBLANES, S5_LANES), F32)],
        input_output_aliases={13: 0},
        compiler_params=_params("arbitrary"))(dgelu, y_pre, proj, h_re, h_im, lam_rows, p3_re, p3_im, bbr4, bbi4,
                                              cr4, ci4, d_row, dproj)


def _to_segment_order(v, stage_ref, out_ref, seg):
    nbl = v.shape[1] // 128
    for b in range(nbl):
        stage_ref[b] = v[:, b * 128:(b + 1) * 128]

    def body(t, carry):
        rows = pl.ds(pl.multiple_of(t * SUBLANES, SUBLANES), SUBLANES)
        for b in range(nbl):
            out_ref[rows, _lanes(b)] = stage_ref[b, pl.ds(t, SUBLANES, stride=seg), :]
        return carry

    lax.fori_loop(0, seg, body, 0)


def _from_segment_order(v, stage_ref, out_ref, seg):
    nbl = v.shape[1] // 128
    for b in range(nbl):
        stage_ref[b] = v[:, b * 128:(b + 1) * 128]
    for s in range(SUBLANES):
        def body(k, carry, s=s):
            rows = pl.ds(pl.multiple_of(s * seg + k * SUBLANES, SUBLANES), SUBLANES)
            for b in range(nbl):
                out_ref[rows, _lanes(b)] = stage_ref[b, pl.ds(k * SUBLANES * SUBLANES + s, SUBLANES,
                                                              stride=SUBLANES), :]
            return carry

        lax.fori_loop(0, seg // SUBLANES, body, 0)


def _tile_scan(xr_ref, xi_ref, lam_ref, car_ref, cai_ref, cn_r, cn_i, blocks, seg, reverse):
    shape = (SUBLANES, 128)
    lrs = [jnp.broadcast_to(lam_ref[0:1, _lanes(j)], shape) for j in blocks]
    lis = [jnp.broadcast_to(lam_ref[1:2, _lanes(j)], shape) for j in blocks]

    def step(k, carry):
        t = seg - 1 - k if reverse else k
        rows = pl.ds(pl.multiple_of(t * SUBLANES, SUBLANES), SUBLANES)
        out = []
        for n, j in enumerate(blocks):
            cr, ci = carry[2 * n], carry[2 * n + 1]
            nr = lrs[n] * cr - lis[n] * ci + xr_ref[rows, _lanes(j)]
            ni = lrs[n] * ci + lis[n] * cr + xi_ref[rows, _lanes(j)]
            xr_ref[rows, _lanes(j)] = nr
            xi_ref[rows, _lanes(j)] = ni
            out += [nr, ni]
        return tuple(out)

    zero = jnp.zeros(shape, F32)
    fin = lax.fori_loop(0, seg, step, (zero,) * (2 * len(blocks)), unroll=2)
    for n, j in enumerate(blocks):
        ls = _lanes(j)
        fr, fi = fin[2 * n], fin[2 * n + 1]
        sr, si = lam_ref[2:3, ls], lam_ref[3:4, ls]
        pr, pi = car_ref[:, ls], cai_ref[:, ls]
        for s in (reversed(range(SUBLANES)) if reverse else range(SUBLANES)):
            cn_r[s:s + 1, ls] = pr
            cn_i[s:s + 1, ls] = pi
            pr, pi = fr[s:s + 1, :] + sr * pr - si * pi, fi[s:s + 1, :] + sr * pi + si * pr
        car_ref[:, ls] = pr
        cai_ref[:, ls] = pi


def _s5_fwd3(proj, lam_rows, p3_re, p3_im, bbr4, bbi4, crt4, cit4, d_row, t_len, tb):
    seg = tb // SUBLANES

    def body(u_ref, lam_ref, p3r_ref, p3i_ref, bbr_ref, bbi_ref, crt_ref, cit_ref, d_ref,
             hr_ref, hi_ref, ypre_ref, ys_ref, car_ref, cai_ref, cn_r, cn_i, stage_ref, us_ref, yseg_ref):
        @pl.when(pl.program_id(0) == 0)
        def _():
            car_ref[...] = jnp.zeros_like(car_ref)
            cai_ref[...] = jnp.zeros_like(cai_ref)

        _to_segment_order(u_ref[...], stage_ref, us_ref, seg)
        u = us_ref[...]
        for i in range(S5_BLOCKS):
            ui = u[:, i * S5_BW:(i + 1) * S5_BW]
            hr_ref[:, pl.ds(i * S5_BL, S5_BL)] = _dot(ui, bbr_ref[i])
            hi_ref[:, pl.ds(i * S5_BL, S5_BL)] = _dot(ui, bbi_ref[i])
        for lc in range(S5_LANE_BLOCKS // S5_SCAN_BLOCKS):
            blocks = range(lc * S5_SCAN_BLOCKS, (lc + 1) * S5_SCAN_BLOCKS)
            _tile_scan(hr_ref, hi_ref, lam_ref, car_ref, cai_ref, cn_r, cn_i, blocks, seg, False)
            crs = [cn_r[:, _lanes(j)] for j in blocks]
            cis = [cn_i[:, _lanes(j)] for j in blocks]

            def fix(t, carry, blocks=blocks, crs=crs, cis=cis):
                rows = pl.ds(pl.multiple_of(t * SUBLANES, SUBLANES), SUBLANES)
                for n, j in enumerate(blocks):
                    pr, pi = p3r_ref[t, :, _lanes(j)], p3i_ref[t, :, _lanes(j)]
                    hr_ref[rows, _lanes(j)] += pr * crs[n] - pi * cis[n]
                    hi_ref[rows, _lanes(j)] += pr * cis[n] + pi * crs[n]
                return carry

            lax.fori_loop(0, seg, fix, 0, unroll=2)
        for i in range(S5_BLOCKS):
            ws = pl.ds(i * S5_BW, S5_BW)
            bl = pl.ds(i * S5_BL, S5_BL)
            yseg_ref[:, ws] = (_dot(hr_ref[:, bl], crt_ref[i]) - _dot(hi_ref[:, bl], cit_ref[i])
                               + d_ref[:, ws] * u[:, i * S5_BW:(i + 1) * S5_BW])
        _from_segment_order(yseg_ref[...], stage_ref, ypre_ref, seg)
        ys_ref[...] = jax.nn.gelu(ypre_ref[...], approximate=True).astype(ys_ref.dtype)

    whole = pl.BlockSpec(memory_space=pltpu.VMEM)
    return pl.pallas_call(
        body, name="s5_fwd", grid=(t_len // tb,),
        in_specs=[pl.BlockSpec((tb, S5_WIDTH), lambda i: (i, 4096 // S5_WIDTH))] + [whole] * 8,
        out_specs=[pl.BlockSpec((tb, S5_LANES), lambda i: (i, 0)), pl.BlockSpec((tb, S5_LANES), lambda i: (i, 0)),
                   pl.BlockSpec((tb, S5_WIDTH), lambda i: (i, 0)), pl.BlockSpec((tb, S5_WIDTH), lambda i: (i, 0))],
        out_shape=[jax.ShapeDtypeStruct((t_len, S5_LANES), F32), jax.ShapeDtypeStruct((t_len, S5_LANES), F32),
                   jax.ShapeDtypeStruct((t_len, S5_WIDTH), F32), jax.ShapeDtypeStruct((t_len, S5_WIDTH), MXU_DTYPE)],
        scratch_shapes=[pltpu.VMEM((1, S5_LANES), F32), pltpu.VMEM((1, S5_LANES), F32),
                        pltpu.VMEM((SUBLANES, S5_LANES), F32), pltpu.VMEM((SUBLANES, S5_LANES), F32),
                        pltpu.VMEM((S5_WIDTH // 128, tb, 128), F32), pltpu.VMEM((tb, S5_WIDTH), F32),
                        pltpu.VMEM((tb, S5_WIDTH), F32)],
        compiler_params=_params("arbitrary"))(proj, lam_rows, p3_re, p3_im, bbr4, bbi4, crt4, cit4, d_row)


def _s5_bwd3(dgelu, y_pre, proj, h_re, h_im, lam_rows, p3_re, p3_im, bbr4, bbi4, cr4, ci4, d_row, dproj, t_len, tb):
    seg = tb // SUBLANES
    nb = t_len // tb

    def body(dg_ref, yp_ref, u_ref, hr_ref, hi_ref, lam_ref, p3r_ref, p3i_ref, bbr_ref, bbi_ref, cr_ref, ci_ref,
             d_ref, _, du_ref, dbbr_ref, dbbi_ref, dcr_ref, dci_ref, dd_ref, dlam_ref,
             gr_ref, gi_ref, car_ref, cai_ref, cn_r, cn_i, stage_ref, us_ref, dys_ref, duseg_ref):
        @pl.when(pl.program_id(0) == 0)
        def _():
            for ref in (car_ref, cai_ref, dbbr_ref, dbbi_ref, dcr_ref, dci_ref, dd_ref, dlam_ref):
                ref[...] = jnp.zeros_like(ref)

        _to_segment_order(u_ref[...], stage_ref, us_ref, seg)
        _to_segment_order(dg_ref[...] * _dgelu(yp_ref[...]), stage_ref, dys_ref, seg)
        u, dy = us_ref[...], dys_ref[...]
        for i in range(S5_BLOCKS):
            dyi = dy[:, i * S5_BW:(i + 1) * S5_BW]
            gr_ref[:, pl.ds(i * S5_BL, S5_BL)] = _dot(dyi, cr_ref[i])
            gi_ref[:, pl.ds(i * S5_BL, S5_BL)] = -_dot(dyi, ci_ref[i])
        for lc in range(S5_LANE_BLOCKS // S5_SCAN_BLOCKS):
            blocks = range(lc * S5_SCAN_BLOCKS, (lc + 1) * S5_SCAN_BLOCKS)
            _tile_scan(gr_ref, gi_ref, lam_ref, car_ref, cai_ref, cn_r, cn_i, blocks, seg, True)
            crs = [cn_r[:, _lanes(j)] for j in blocks]
            cis = [cn_i[:, _lanes(j)] for j in blocks]

            def fix(k, carry, blocks=blocks, crs=crs, cis=cis):
                t = seg - 1 - k
                rows = pl.ds(pl.multiple_of(t * SUBLANES, SUBLANES), SUBLANES)
                out = []
                for n, j in enumerate(blocks):
                    nr, ni, slr, sli = carry[4 * n:4 * n + 4]
                    pr, pi = p3r_ref[t, :, _lanes(j)], p3i_ref[t, :, _lanes(j)]
                    g_r = gr_ref[rows, _lanes(j)] + pr * crs[n] - pi * cis[n]
                    g_i = gi_ref[rows, _lanes(j)] + pr * cis[n] + pi * crs[n]
                    gr_ref[rows, _lanes(j)] = g_r
                    gi_ref[rows, _lanes(j)] = g_i
                    hr, hi = hr_ref[rows, _lanes(j)], hi_ref[rows, _lanes(j)]
                    out += [g_r, g_i, slr + nr * hr + ni * hi, sli + ni * hr - nr * hi]
                return tuple(out)

            zero = jnp.zeros((SUBLANES, 128), F32)
            init = []
            for n in range(len(blocks)):
                init += [crs[n], cis[n], zero, zero]
            fin = lax.fori_loop(0, seg, fix, tuple(init), unroll=2)
            for n, j in enumerate(blocks):
                dlam_ref[0:1, _lanes(j)] += jnp.sum(fin[4 * n + 2], axis=0, keepdims=True)
                dlam_ref[1:2, _lanes(j)] += jnp.sum(fin[4 * n + 3], axis=0, keepdims=True)
        for i in range(S5_BLOCKS):
            ws = pl.ds(i * S5_BW, S5_BW)
            bl = pl.ds(i * S5_BL, S5_BL)
            ui, dyi = u[:, i * S5_BW:(i + 1) * S5_BW], dy[:, i * S5_BW:(i + 1) * S5_BW]
            gr, gi = gr_ref[:, bl], gi_ref[:, bl]
            duseg_ref[:, ws] = _dot(gr, bbr_ref[i], _NT) + _dot(gi, bbi_ref[i], _NT) + d_ref[:, ws] * dyi
            dbbr_ref[i] += _dot(ui, gr, _TN)
            dbbi_ref[i] += _dot(ui, gi, _TN)
            dcr_ref[i] += _dot(hr_ref[:, bl], dyi, _TN)
            dci_ref[i] -= _dot(hi_ref[:, bl], dyi, _TN)
        dd_ref[...] += jnp.sum(dy * u, axis=0, keepdims=True)
        _from_segment_order(duseg_ref[...], stage_ref, duseg_ref, seg)
        du_ref[...] = duseg_ref[...].astype(du_ref.dtype)

    whole = pl.BlockSpec(memory_space=pltpu.VMEM)
    rev = lambda i: (nb - 1 - i, 0)
    const3 = lambda i: (0, 0, 0)
    return pl.pallas_call(
        body, name="s5_bwd", grid=(nb,),
        in_specs=[pl.BlockSpec((tb, S5_WIDTH), rev), pl.BlockSpec((tb, S5_WIDTH), rev),
                  pl.BlockSpec((tb, S5_WIDTH), lambda i: (nb - 1 - i, 4096 // S5_WIDTH)),
                  pl.BlockSpec((tb, S5_LANES), rev), pl.BlockSpec((tb, S5_LANES), rev)] + [whole] * 8
                 + [pl.BlockSpec(memory_space=pl.ANY)],
        out_specs=[pl.BlockSpec((tb, S5_WIDTH), lambda i: (nb - 1 - i, 4096 // S5_WIDTH)),
                   pl.BlockSpec((S5_BLOCKS, S5_BW, S5_BL), const3), pl.BlockSpec((S5_BLOCKS, S5_BW, S5_BL), const3),
                   pl.BlockSpec((S5_BLOCKS, S5_BL, S5_BW), const3), pl.BlockSpec((S5_BLOCKS, S5_BL, S5_BW), const3),
                   pl.BlockSpec((1, S5_WIDTH), lambda i: (0, 0)), pl.BlockSpec((2, S5_LANES), lambda i: (0, 0))],
        out_shape=[jax.ShapeDtypeStruct((t_len, IN_COLS), dproj.dtype),
                   jax.ShapeDtypeStruct((S5_BLOCKS, S5_BW, S5_BL), F32),
                   jax.ShapeDtypeStruct((S5_BLOCKS, S5_BW, S5_BL), F32),
                   jax.ShapeDtypeStruct((S5_BLOCKS, S5_BL, S5_BW), F32),
                   jax.ShapeDtypeStruct((S5_BLOCKS, S5_BL, S5_BW), F32),
                   jax.ShapeDtypeStruct((1, S5_WIDTH), F32), jax.ShapeDtypeStruct((2, S5_LANES), F32)],
        scratch_shapes=[pltpu.VMEM((tb, S5_LANES), F32), pltpu.VMEM((tb, S5_LANES), F32),
                        pltpu.VMEM((1, S5_LANES), F32), pltpu.VMEM((1, S5_LANES), F32),
                        pltpu.VMEM((SUBLANES, S5_LANES), F32), pltpu.VMEM((SUBLANES, S5_LANES), F32),
                        pltpu.VMEM((S5_WIDTH // 128, tb, 128), F32), pltpu.VMEM((tb, S5_WIDTH), F32),
                        pltpu.VMEM((tb, S5_WIDTH), F32), pltpu.VMEM((tb, S5_WIDTH), F32)],
        input_output_aliases={13: 0},
        compiler_params=_params("arbitrary"))(dgelu, y_pre, proj, h_re, h_im, lam_rows, p3_re, p3_im, bbr4, bbi4,
                                              cr4, ci4, d_row, dproj)


def _mixers_fwd(proj, hg_lb, hg_norm_g, lam_rows, p3_re, p3_im, bbr4, bbi4, crt4, cit4, d_row, t_len, tb):
    nck = tb // HG_CHUNK
    seg = tb // SUBLANES

    def body(p_ref, u_ref, lb_ref, gn_ref, lam_ref, p3r_ref, p3i_ref, bbr_ref, bbi_ref, crt_ref, cit_ref, d_ref,
             o_ref, act_ref, sp_ref, hr_ref, hi_ref, ypre_ref, ys_ref,
             st_ref, a_s, bm_s, qd_s, kd_s, v_s, car_ref, cai_ref, cn_r, cn_i, stage_ref, us_ref, yseg_ref):
        @pl.when(pl.program_id(0) == 0)
        def _():
            for ref in (st_ref, car_ref, cai_ref):
                ref[...] = jnp.zeros_like(ref)

        _to_segment_order(u_ref[...], stage_ref, us_ref, seg)
        u = us_ref[...]
        for i in range(S5_BLOCKS):
            ui = u[:, i * S5_BW:(i + 1) * S5_BW]
            hr_ref[:, pl.ds(i * S5_BL, S5_BL)] = _dot(ui, bbr_ref[i])
            hi_ref[:, pl.ds(i * S5_BL, S5_BL)] = _dot(ui, bbi_ref[i])
        lb = _sig(lb_ref[0:1, :] - lb_ref[1:2, :])
        q = p_ref[:, pl.ds(0, 1024)]
        _, _, kk, causal, _, e_mid, e_mid_inv, e_b, e_last, dcs = _hg_block_terms(q, p_ref[:, pl.ds(1024, 1024)],
                                                                                   lb, tb)
        a_s[...] = _mx(q * e_mid)
        bm_s[...] = _mx(kk * e_mid_inv)
        qd_s[...] = _mx(q * e_b)
        kd_s[...] = _mx(kk * e_last)
        v_s[...] = _mx(p_ref[:, pl.ds(2048, 1024)])
        for h in range(HG_HEADS):
            hs = pl.ds(h * HG_DIM, HG_DIM)
            scores = jnp.where(causal, _dot(a_s[:, hs], bm_s[:, hs], _NT), 0.0)
            o_ref[:, hs] = _dot(scores, v_s[:, hs])
        for lc in range(S5_LANE_BLOCKS // S5_SCAN_BLOCKS):
            blocks = range(lc * S5_SCAN_BLOCKS, (lc + 1) * S5_SCAN_BLOCKS)
            _tile_scan(hr_ref, hi_ref, lam_ref, car_ref, cai_ref, cn_r, cn_i, blocks, seg, False)
            crs = [cn_r[:, _lanes(j)] for j in blocks]
            cis = [cn_i[:, _lanes(j)] for j in blocks]

            def fix(t, carry, blocks=blocks, crs=crs, cis=cis):
                rows = pl.ds(pl.multiple_of(t * SUBLANES, SUBLANES), SUBLANES)
                for n, j in enumerate(blocks):
                    pr, pi = p3r_ref[t, :, _lanes(j)], p3i_ref[t, :, _lanes(j)]
                    hr_ref[rows, _lanes(j)] += pr * crs[n] - pi * cis[n]
                    hi_ref[rows, _lanes(j)] += pr * cis[n] + pi * crs[n]
                return carry

            lax.fori_loop(0, seg, fix, 0, unroll=2)
        for i in range(S5_BLOCKS):
            ws = pl.ds(i * S5_BW, S5_BW)
            bl = pl.ds(i * S5_BL, S5_BL)
            yseg_ref[:, ws] = (_dot(hr_ref[:, bl], crt_ref[i]) - _dot(hi_ref[:, bl], cit_ref[i])
                               + d_ref[:, ws] * u[:, i * S5_BW:(i + 1) * S5_BW])
        for h in range(HG_HEADS):
            hs = pl.ds(h * HG_DIM, HG_DIM)
            incs = [_dot(v_s[pl.ds(c * HG_CHUNK, HG_CHUNK), hs], kd_s[pl.ds(c * HG_CHUNK, HG_CHUNK), hs], _TN)
                    for c in range(nck)]
            st = st_ref[h]
            for c in range(nck):
                sp_ref[h, c] = st
                st = dcs[c][:, h * HG_DIM:(h + 1) * HG_DIM] * st + incs[c]
            st_ref[h] = st
        for h in range(HG_HEADS):
            hs = pl.ds(h * HG_DIM, HG_DIM)
            for c in range(nck):
                r = pl.ds(c * HG_CHUNK, HG_CHUNK)
                o_ref[r, hs] += _dot(qd_s[r, hs], sp_ref[h, c], _NT)
        for h in range(HG_HEADS):
            hs = pl.ds(h * HG_DIM, HG_DIM)
            o = o_ref[:, hs]
            rr = lax.rsqrt(jnp.mean(o * o, axis=-1, keepdims=True) + NORM_EPS)
            g = p_ref[:, pl.ds(3072 + h * HG_DIM, HG_DIM)]
            act_ref[:, hs] = (o * rr * gn_ref[:, hs] * (g * _sig(g))).astype(act_ref.dtype)
        _from_segment_order(yseg_ref[...], stage_ref, ypre_ref, seg)
        ys_ref[...] = jax.nn.gelu(ypre_ref[...], approximate=True).astype(ys_ref.dtype)

    nb = t_len // tb
    whole = pl.BlockSpec(memory_space=pltpu.VMEM)
    tile = lambda wd: pl.BlockSpec((tb, wd), lambda i: (i, 0))
    return pl.pallas_call(
        body, name="mixers_fwd", grid=(nb,),
        in_specs=[tile(4096), pl.BlockSpec((tb, S5_WIDTH), lambda i: (i, 4096 // S5_WIDTH)),
                  pl.BlockSpec((2, 1024), lambda i: (0, 0)), pl.BlockSpec((1, 1024), lambda i: (0, 0))] + [whole] * 8,
        out_specs=[tile(1024), tile(1024), pl.BlockSpec((HG_HEADS, nck, HG_DIM, HG_DIM), lambda i: (0, i, 0, 0)),
                   tile(S5_LANES), tile(S5_LANES), tile(S5_WIDTH), tile(S5_WIDTH)],
        out_shape=[jax.ShapeDtypeStruct((t_len, 1024), F32), jax.ShapeDtypeStruct((t_len, 1024), MXU_DTYPE),
                   jax.ShapeDtypeStruct((HG_HEADS, t_len // HG_CHUNK, HG_DIM, HG_DIM), F32),
                   jax.ShapeDtypeStruct((t_len, S5_LANES), F32), jax.ShapeDtypeStruct((t_len, S5_LANES), F32),
                   jax.ShapeDtypeStruct((t_len, S5_WIDTH), F32), jax.ShapeDtypeStruct((t_len, S5_WIDTH), MXU_DTYPE)],
        scratch_shapes=[pltpu.VMEM((HG_HEADS, HG_DIM, HG_DIM), F32)] + [pltpu.VMEM((tb, 1024), MXU_DTYPE)] * 5
                       + [pltpu.VMEM((1, S5_LANES), F32), pltpu.VMEM((1, S5_LANES), F32),
                          pltpu.VMEM((SUBLANES, S5_LANES), F32), pltpu.VMEM((SUBLANES, S5_LANES), F32),
                          pltpu.VMEM((S5_WIDTH // 128, tb, 128), F32), pltpu.VMEM((tb, S5_WIDTH), F32),
                          pltpu.VMEM((tb, S5_WIDTH), F32)],
        compiler_params=_params("arbitrary"))(proj, proj, hg_lb, hg_norm_g, lam_rows, p3_re, p3_im, bbr4, bbi4, crt4,
                                              cit4, d_row)


def _block_diag4(per_group):
    g8 = S5_GROUPS // S5_BLOCKS
    eye = jnp.eye(g8, dtype=bool)[None, :, None, :, None]
    dense = jnp.where(eye, per_group.reshape(S5_BLOCKS, g8, S5_GROUP, 1, S5_STATE), 0.0)
    return dense.reshape(S5_BLOCKS, S5_BW, S5_BL)


def _diag_blocks4(dense):
    g8 = S5_GROUPS // S5_BLOCKS
    ar = jnp.arange(g8)
    d5 = dense.reshape(S5_BLOCKS, g8, S5_GROUP, g8, S5_STATE)
    return d5[:, ar, :, ar, :].transpose(1, 0, 2, 3).reshape(S5_GROUPS, S5_GROUP, S5_STATE)


def _block_diag(per_group):
    eye = jnp.eye(S5_GROUPS, dtype=bool)[:, None, :, None]
    dense = jnp.where(eye, per_group[:, :, None, :], 0.0)
    return dense.reshape(S5_WIDTH, S5_LANES)


def _diag_blocks(dense):
    ar = jnp.arange(S5_GROUPS)
    return dense.reshape(S5_GROUPS, S5_GROUP, S5_GROUPS, S5_STATE)[ar, :, ar, :]


def _hg_gate_bwd(da, o, g, gn):
    dos, dgs, dgns = [], [], []
    for h in range(HG_HEADS):
        sl = slice(h * HG_DIM, (h + 1) * HG_DIM)
        oh, gh, dah, gnh = o[:, sl], g[:, sl], da[:, sl], gn[:, sl]
        rr = lax.rsqrt(jnp.mean(oh * oh, axis=-1, keepdims=True) + NORM_EPS)
        sg = _sig(gh)
        dgs.append(dah * (oh * rr * gnh) * _dsilu(gh, sg))
        don = dah * (gh * sg)
        t = don * gnh
        dos.append(rr * t - oh * (rr * rr * rr) * jnp.mean(t * oh, axis=-1, keepdims=True))
        dgns.append(jnp.sum(don * oh * rr, axis=0, keepdims=True))
    return jnp.concatenate(dos, axis=1), jnp.concatenate(dgs, axis=1), jnp.concatenate(dgns, axis=1)


MIX_BWD_COLS = ((3072, 1024), (4608, 512), (5120, 1024), (6144, 1024))


def _mix_bwd(dgl, h1, dh2, act_hg, ys2, ys_gelu, proj, o_hg, g2, ghn, b_glu, w, t_len, tm):
    nb = t_len // tm

    def body(dgl_ref, h1_ref, dh2_ref, act_ref, ys2_ref, ysg_ref, ghg_ref, z_ref, gh_ref, gs_ref, o_ref, g2_ref, gn_ref,
             bglu_ref, wg_ref, wo_ref, ws5_ref, whg_ref, wglu_ref,
             dh1_ref, dyh_ref, dys_ref, dglu_ref, dgelu_ref, do_ref, dg2_ref, dbglu_ref, dgn_ref, dproj_ref,
             st0, st1, st2, st3, sems):
        i = pl.program_id(0)
        stages = (st0, st1, st2, st3)

        def writes(step):
            rows = pl.ds(pl.multiple_of(step * tm, tm), tm)
            return [pltpu.make_async_copy(st, dproj_ref.at[rows, pl.ds(c0, wd)], sems.at[k])
                    for k, (st, (c0, wd)) in enumerate(zip(stages, MIX_BWD_COLS))]

        @pl.when(i > 0)
        def _():
            for cp in writes(i - 1):
                cp.wait()

        @pl.when(i == 0)
        def _():
            for ref in (dg2_ref, dbglu_ref, dgn_ref):
                ref[...] = jnp.zeros_like(ref)

        dx, dg2 = _rms_bwd(_dot(dgl_ref[...], wg_ref[...], _NT), h1_ref[...], g2_ref[...])
        dh1 = dh2_ref[...] + dx
        dh1_ref[...] = dh1
        dg2_ref[...] += dg2
        dm = _dot(dh1, wo_ref[...], _NT)
        sh, ss = _sig(gh_ref[...]), _sig(gs_ref[...])
        dyh, dys = _mx(dm * sh), _mx(dm * ss)
        dyh_ref[...] = dyh
        dys_ref[...] = dys
        st2[...] = (dm * _dot(act_ref[...], whg_ref[...]) * sh * (1.0 - sh)).astype(st2.dtype)
        st3[...] = (dm * _dot(ys2_ref[...], ws5_ref[...]) * ss * (1.0 - ss)).astype(st3.dtype)
        dys2 = _dot(dys, ws5_ref[...], _NT)
        gl_, z = _dot(ysg_ref[...], wglu_ref[...]) + bglu_ref[...], z_ref[...]
        a, b = gl_[:, :S5_WIDTH], gl_[:, S5_WIDTH:]
        sb, sz = _sig(b), _sig(z)
        silu = z * sz
        dglu = jnp.concatenate([dys2 * sb * silu, dys2 * a * silu * sb * (1.0 - sb)], axis=1)
        st1[...] = (dys2 * a * sb * _dsilu(z, sz)).astype(st1.dtype)
        dbglu_ref[...] += jnp.sum(dglu, axis=0, keepdims=True)
        dglu_ref[...] = _mx(dglu)
        dgelu_ref[...] = _dot(dglu, wglu_ref[...], _NT)
        d_o, dg, dgn = _hg_gate_bwd(_dot(dyh, whg_ref[...], _NT), o_ref[...], ghg_ref[...], gn_ref[...])
        do_ref[...] = d_o.astype(do_ref.dtype)
        st0[...] = dg.astype(st0.dtype)
        dgn_ref[...] += dgn
        for cp in writes(i):
            cp.start()

        @pl.when(i == nb - 1)
        def _():
            for cp in writes(i):
                cp.wait()

    tile = lambda wd, cb=0: pl.BlockSpec((tm, wd), functools.partial(lambda i, cb: (i, cb), cb=cb))
    row = lambda wd: pl.BlockSpec((1, wd), lambda i: (0, 0))
    whole = pl.BlockSpec(memory_space=pltpu.VMEM)
    return pl.pallas_call(
        body, name="mix_bwd", grid=(nb,),
        in_specs=[tile(1024), tile(1024), tile(1024), tile(1024), tile(512), tile(512), tile(1024, 3),
                  tile(512, 4608 // 512), tile(1024, 5), tile(1024, 6), tile(1024), row(1024), row(1024), row(1024)]
                 + [whole] * 5,
        out_specs=[tile(1024), tile(1024), tile(1024), tile(1024), tile(512), tile(1024), row(1024), row(1024),
                   row(1024), _HBM],
        out_shape=[jax.ShapeDtypeStruct((t_len, 1024), F32), jax.ShapeDtypeStruct((t_len, 1024), MXU_DTYPE),
                   jax.ShapeDtypeStruct((t_len, 1024), MXU_DTYPE), jax.ShapeDtypeStruct((t_len, 1024), MXU_DTYPE),
                   jax.ShapeDtypeStruct((t_len, 512), F32), jax.ShapeDtypeStruct((t_len, 1024), MXU_DTYPE),
                   jax.ShapeDtypeStruct((1, 1024), F32), jax.ShapeDtypeStruct((1, 1024), F32),
                   jax.ShapeDtypeStruct((1, 1024), F32), jax.ShapeDtypeStruct((t_len, IN_COLS), MXU_DTYPE)],
        scratch_shapes=[pltpu.VMEM((tm, wd), MXU_DTYPE) for _, wd in MIX_BWD_COLS] + [pltpu.SemaphoreType.DMA((4,))],
        compiler_params=_params("arbitrary"))(dgl, h1, dh2, act_hg, ys2, ys_gelu, proj, proj, proj, proj, o_hg, g2, ghn,
                                              b_glu, w["w_ple_gate"], w["w_out"], w["w_o_s5"], w["w_o_hg"],
                                              w["w_glu"])


def _local_step(x, p, target, w, sm, comm=None):
    t_len = x.shape[0]
    tm = min(256, t_len)
    tmm = min(512, t_len)
    tb_hg = min(256, t_len)
    tb_s5 = min(256, t_len)
    g1, g2, g3, ghn = sm["norm_g"], sm["ple_norm_g"], sm["final_norm_g"].reshape(1, D_MODEL), sm["hg_norm_g"]

    def rms_in(xv, g):
        return xv * lax.rsqrt(jnp.mean(xv * xv, axis=-1, keepdims=True) + NORM_EPS) * g

    in_shard = IN_COLS // N_CHIPS
    w_in = w["w_in"]
    if comm is None:
        proj, u = _mm_nn("mm_in", x, w_in, tmm, in_shard, prologue=rms_in, consts=[g1])
    else:
        proj, u, landed = _mm_nn("mm_in", x, w_in, tmm, in_shard, riding=comm.gather_rest(), prologue=rms_in,
                                 consts=[g1])
        w = comm.rest_weights(landed)

    lanes = lambda a: a.reshape(1, S5_LANES)
    a_re, a_im = lanes(sm["s5_a_re"]), lanes(sm["s5_a_im"])
    ldt = lanes(jnp.broadcast_to(sm["s5_log_dt"].reshape(S5_GROUPS, 1), (S5_GROUPS, S5_STATE)))
    to_t = lambda b: b.reshape(S5_GROUPS, S5_STATE, S5_GROUP).transpose(2, 0, 1).reshape(S5_GROUP, S5_LANES)
    b_re_t, b_im_t = to_t(sm["s5_b_re"]), to_t(sm["s5_b_im"])
    scan_fwd, scan_rev, bbr_t, bbi_t = _s5_powers(a_re, a_im, ldt, b_re_t, b_im_t, tb_s5 // SUBLANES)
    from_t = lambda b: b.reshape(S5_GROUP, S5_GROUPS, S5_STATE).transpose(1, 0, 2)
    bbr_bd = _block_diag4(from_t(bbr_t)).astype(MXU_DTYPE)
    bbi_bd = _block_diag4(from_t(bbi_t)).astype(MXU_DTYPE)
    cr_bd = _block_diag4(sm["s5_c_re"].reshape(S5_GROUPS, S5_GROUP, S5_STATE)).astype(MXU_DTYPE)
    ci_bd = _block_diag4(sm["s5_c_im"].reshape(S5_GROUPS, S5_GROUP, S5_STATE)).astype(MXU_DTYPE)
    d_row = sm["s5_d"].reshape(1, S5_WIDTH)
    o_hg, act_hg, s_prev = _hgrn2_fwd2(proj, sm["hg_lb"], ghn, t_len, tb_hg)
    h_re, h_im, y_pre, ys_gelu = _s5_fwd3(proj, *scan_fwd, bbr_bd, bbi_bd,
                                          cr_bd.transpose(0, 2, 1), ci_bd.transpose(0, 2, 1), d_row, t_len, tb_s5)
    def mix_f(act, ysg, z, gh, gs, xv, w_glu, b_glu, w_o_hg, w_o_s5, w_out):
        gl_ = _dot(ysg, w_glu) + b_glu
        a, b = gl_[:, :S5_WIDTH], gl_[:, S5_WIDTH:]
        ys2_ = (a * _sig(b) * (z * _sig(z))).astype(MXU_DTYPE)
        yh, ys = _dot(act, w_o_hg), _dot(ys2_, w_o_s5)
        mg = (_sig(gh) * yh + _sig(gs) * ys).astype(MXU_DTYPE)
        return (ys2_, mg, xv + _dot(mg, w_out))

    ys2, merged, h1 = _rowwise(
        "mix_out", mix_f, t_len, tm,
        [(act_hg, 1024, 0), (ys_gelu, 512, 0), (proj, 512, 4608 // 512), (proj, 1024, 5), (proj, 1024, 6),
         (x, 1024, 0)], [w["w_glu"], sm["b_glu"], w["w_o_hg"], w["w_o_s5"], w["w_out"]],
        [(512, MXU_DTYPE), (1024, MXU_DTYPE), (1024, F32)])

    def head_f(h1v, pv, tgt, g_ple, g, w_ple, w_gate):
        r2 = lax.rsqrt(jnp.mean(h1v * h1v, axis=-1, keepdims=True) + NORM_EPS)
        n2_ = (h1v * r2 * g_ple).astype(MXU_DTYPE)
        glv, pev = _dot(n2_, w_gate), _dot(pv, w_ple)
        gate = _sig(glv)
        h2 = h1v + pev * gate
        r = lax.rsqrt(jnp.mean(h2 * h2, axis=-1, keepdims=True) + NORM_EPS)
        e = h2 * r * g - tgt
        loss = 0.5 * jnp.sum(jnp.mean(e * e, axis=-1, keepdims=True), axis=0, keepdims=True)
        dy = e * (1.0 / D_MODEL)
        dg = jnp.sum(dy * h2 * r, axis=0, keepdims=True)
        t = dy * g
        dh2 = r * t - h2 * (r * r * r) * jnp.mean(t * h2, axis=-1, keepdims=True)
        return (n2_, dh2, dh2 * gate, dh2 * pev * gate * (1.0 - gate), jnp.broadcast_to(loss, (1, 128)), dg)

    n2, dh2, dpe, dgl, loss_row, d_g3 = _rowwise(
        "ple_loss_head", head_f, t_len, tm, [(h1, 1024, 0), (p, 256, 0), (target, 1024, 0)],
        [g2, g3, w["w_ple"], w["w_ple_gate"]],
        [(1024, MXU_DTYPE), (1024, F32), (1024, MXU_DTYPE), (1024, MXU_DTYPE)], accs=[(1, 128), (1, 1024)])

    gb = {}
    gb["w_ple"] = _mm_tn("mm_d_w_ple", p, dpe, tmm, 1024)
    gb["w_ple_gate"] = _mm_tn("mm_d_w_ple_gate", n2, dgl, tmm, 1024)
    dh1, dy_hg, dy_s5, dglu, dgelu, d_o, d_g2, d_bglu, d_ghn, dproj = _mix_bwd(
        dgl, h1, dh2, act_hg, ys2, ys_gelu, proj, o_hg, g2, ghn, sm["b_glu"], w, t_len, tm)
    gb["w_out"] = _mm_tn("mm_d_w_out", merged, dh1, tmm, 1024)
    gb["w_o_s5"] = _mm_tn("mm_d_w_o_s5", ys2, dy_s5, tmm, 1024)
    gb["w_glu"] = _mm_tn("mm_d_w_glu", ys_gelu, dglu, tmm, 1024)
    dproj, d_bbr, d_bbi, d_crt, d_cit, d_d, d_lam = _s5_bwd3(dgelu, y_pre, proj, h_re, h_im,
                                                            *scan_rev, bbr_bd, bbi_bd, cr_bd,
                                                            ci_bd, d_row, dproj, t_len, tb_s5)
    to_t3 = lambda b: b.transpose(1, 0, 2).reshape(S5_GROUP, S5_LANES)
    d_are, d_aim, d_ldt, d_br_t, d_bi_t = _s5_prep_bwd(a_re, a_im, ldt, b_re_t, b_im_t, d_lam,
                                                       to_t3(_diag_blocks4(d_bbr)), to_t3(_diag_blocks4(d_bbi)))
    gb["w_o_hg"] = _mm_tn("mm_d_w_o_hg", act_hg, dy_hg, tmm, 1024)
    if comm is None:
        dproj, d_lb = _hgrn2_bwd2(proj, d_o, s_prev, sm["hg_lb"], dproj, t_len, tb_hg)
    else:
        rest_grads = _pack_rest_full(gb)
        dproj, d_lb, rest_theirs = _hgrn2_bwd2(proj, d_o, s_prev, sm["hg_lb"], dproj, t_len, tb_hg,
                                               riding=comm.swap(rest_grads))

    def in_b(duv, xv, dh, g):
        dx, dg = _rms_bwd(duv, xv, g)
        return (dh + dx, dg)

    in_args = ("mm_d_u_rms_in_bwd", dproj, w_in, tmm, in_shard, in_b, [(x, 1024, 0), (dh1, 1024, 0)], [g1],
               [(1024, F32)])
    if comm is None:
        gb["w_in"] = _mm_tn("mm_d_w_in", u, dproj, tmm, in_shard, col_shards=True)
        grad_x, d_g1 = _mm_nt_then(*in_args, accs=[(1, 1024)])
    else:
        gb["w_in"], landed = _mm_tn("mm_d_w_in", u, dproj, tmm, in_shard, col_shards=True,
                                    riding=comm.scatter("rest", rest_grads, rest_theirs))
        comm.landed["rest"] = landed
        grad_x, d_g1, landed = _mm_nt_then(*in_args, accs=[(1, 1024)], riding=comm.scatter(
            "in", gb["w_in"].reshape(N_CHIPS, 2, D_MODEL // 2, in_shard)))
        comm.landed["in"] = landed

    back_t = lambda b: b.reshape(S5_GROUP, S5_GROUPS, S5_STATE).transpose(1, 2, 0).reshape(1, S5_GROUPS, S5_STATE,
                                                                                           S5_GROUP)
    gs = {
        "norm_g": d_g1, "hg_lb": d_lb, "hg_norm_g": d_ghn,
        "s5_a_re": d_are.reshape(1, S5_GROUPS, S5_STATE), "s5_a_im": d_aim.reshape(1, S5_GROUPS, S5_STATE),
        "s5_log_dt": d_ldt[0:1, :S5_GROUPS],
        "s5_b_re": back_t(d_br_t), "s5_b_im": back_t(d_bi_t),
        "s5_c_re": _diag_blocks4(d_crt.transpose(0, 2, 1)).reshape(1, S5_GROUPS, S5_GROUP, S5_STATE),
        "s5_c_im": _diag_blocks4(d_cit.transpose(0, 2, 1)).reshape(1, S5_GROUPS, S5_GROUP, S5_STATE),
        "s5_d": d_d.reshape(1, S5_GROUPS, S5_GROUP), "b_glu": d_bglu, "ple_norm_g": d_g2,
        "final_norm_g": d_g3.reshape(D_MODEL),
    }
    return loss_row, grad_x, gb, gs


def _shard_shape(name):
    r, c = BIG_SHAPE[name]
    return (r, c // N_CHIPS) if name in BIG_COL_SHARDED else (r // N_CHIPS, c)


def _pack_shard(parts):
    return jnp.concatenate([parts[n].reshape(-1, PACK_W) for n in BIG], axis=0)


def _unpack_shard(packed):
    out, off = {}, 0
    for n in BIG:
        r, c = _shard_shape(n)
        rows = r * c // PACK_W
        out[n] = packed[off:off + rows].reshape(1, r, c)
        off += rows
    return out


def _unpack_full(gathered):
    out, off = {}, 0
    for n in BIG:
        r, c = _shard_shape(n)
        rows = r * c // PACK_W
        sh = gathered[:, off:off + rows].reshape(N_CHIPS, r, c)
        out[n] = sh.transpose(1, 0, 2).reshape(BIG_SHAPE[n]) if n in BIG_COL_SHARDED else sh.reshape(BIG_SHAPE[n])
        off += rows
    return out


def _pack_full(full):
    parts = []
    for n in BIG:
        r, c = _shard_shape(n)
        g = full[n]
        sh = g.reshape(BIG_SHAPE[n][0], N_CHIPS, c).transpose(1, 0, 2) if n in BIG_COL_SHARDED else g
        parts.append(sh.reshape(N_CHIPS, r * c // PACK_W, PACK_W))
    packed = jnp.concatenate(parts, axis=1)
    return packed.reshape(N_CHIPS, 2, HALF_ROWS, PACK_W).transpose(1, 0, 2, 3)


def _pack_small(parts, last):
    flat = jnp.concatenate([parts[n].reshape(-1) for n in SMALL] + [last.reshape(-1)])
    return jnp.pad(flat, (0, SMALL_ROWS * PACK_W - flat.shape[0])).reshape(SMALL_ROWS, PACK_W)


def _unpack_small(packed):
    flat, out, off = packed.reshape(-1), {}, 0
    for n in SMALL:
        size = 1
        for d in SMALL_SHAPE[n]:
            size *= d
        out[n] = flat[off:off + size].reshape(SMALL_SHAPE[n])
        off += size
    return out, flat[off]


def _place():
    x, y, c = lax.axis_index("x"), lax.axis_index("y"), lax.axis_index("c")
    return x, y, c, [(1 - x, y), (x, 1 - y), (1 - x, 1 - y)]


def _remote(src, dst, send_sems, recv_sems, k, to):
    return pltpu.make_async_remote_copy(src_ref=src, dst_ref=dst, send_sem=send_sems.at[k], recv_sem=recv_sems.at[k],
                                        device_id=to, device_id_type=MESH)


_HBM = pl.BlockSpec(memory_space=pl.ANY)


def _all_gather_weights(wp):
    def body(wp_ref, out_ref, send_sems, recv_sems):
        x, y, c, chips = _place()
        k = 2 * x + y
        sibling = (x, y, 1 - c)
        first =[_remote(wp_ref.at[c], out_ref.at[k, c], send_sems, recv_sems, j, (cx, cy, c))
                 for j, (cx, cy) in enumerate(chips)]
        for cp in first:
            cp.start()
        passed = []
        for j, (cx, cy) in enumerate(chips):
            kj = 2 * cx + cy
            _remote(wp_ref.at[c], out_ref.at[kj, c], send_sems, recv_sems, j, (cx, cy, c)).wait_recv()
            cp = _remote(out_ref.at[kj, c], out_ref.at[kj, c], send_sems, recv_sems, 3 + j, sibling)
            cp.start()
            passed.append(cp)
        for j, (cx, cy) in enumerate(chips):
            kj = 2 * cx + cy
            _remote(wp_ref.at[c], out_ref.at[kj, 1 - c], send_sems, recv_sems, 3 + j, sibling).wait_recv()
        for cp in first + passed:
            cp.wait_send()

    return pl.pallas_call(
        body, name="all_gather_weights", in_specs=[_HBM], out_specs=_HBM,
        out_shape=jax.ShapeDtypeStruct((N_CHIPS, 2, HALF_ROWS, PACK_W), wp.dtype),
        scratch_shapes=[pltpu.SemaphoreType.DMA((6,)), pltpu.SemaphoreType.DMA((6,))])(wp)


def _exchange_halves(pg):
    def body(pg_ref, out_ref, send_sems, recv_sems):
        x, y, c, _ = _place()
        cp = _remote(pg_ref.at[1 - c], out_ref, send_sems, recv_sems, 0, (x, y, 1 - c))
        cp.start()
        cp.wait()

    return pl.pallas_call(
        body, name="exchange_halves", in_specs=[_HBM], out_specs=_HBM,
        out_shape=jax.ShapeDtypeStruct((N_CHIPS, HALF_ROWS, PACK_W), pg.dtype),
        scratch_shapes=[pltpu.SemaphoreType.DMA((1,)), pltpu.SemaphoreType.DMA((1,))])(pg)


def _scatter_chip_sums(ps):
    def body(ps_ref, out_ref, send_sems, recv_sems):
        x, y, c, chips = _place()
        cps = [_remote(ps_ref.at[2 * cx + cy], out_ref.at[j], send_sems, recv_sems, j, (cx, cy, c))
               for j, (cx, cy) in enumerate(chips)]
        for cp in cps:
            cp.start()
        for cp in cps:
            cp.wait()

    return pl.pallas_call(
        body, name="scatter_chip_sums", in_specs=[_HBM], out_specs=_HBM,
        out_shape=jax.ShapeDtypeStruct((3, HALF_ROWS, PACK_W), ps.dtype),
        scratch_shapes=[pltpu.SemaphoreType.DMA((3,)), pltpu.SemaphoreType.DMA((3,))])(ps)


def _share_half(g_half):
    def body(g_ref, out_ref, send_sems, recv_sems):
        x, y, c, _ = _place()
        cp = _remote(g_ref, out_ref.at[c], send_sems, recv_sems, 0, (x, y, 1 - c))
        cp.start()
        _remote(g_ref, out_ref.at[1 - c], send_sems, recv_sems, 0, (x, y, 1 - c)).wait_recv()
        cp.wait_send()

    return pl.pallas_call(
        body, name="share_half", in_specs=[_HBM], out_specs=_HBM,
        out_shape=jax.ShapeDtypeStruct((2, HALF_ROWS, PACK_W), g_half.dtype),
        scratch_shapes=[pltpu.SemaphoreType.DMA((1,)), pltpu.SemaphoreType.DMA((1,))])(g_half)


REDUCE_ROWS = 480


def _sum_pair(pg, theirs, c):
    def body(c_ref, a_ref, b_ref, o_ref):
        o_ref[...] = (a_ref[...] + b_ref[...]).astype(o_ref.dtype)

    return pl.pallas_call(
        body, name="sum_pair",
        grid_spec=pltpu.PrefetchScalarGridSpec(
            num_scalar_prefetch=1, grid=(N_CHIPS, HALF_ROWS // REDUCE_ROWS),
            in_specs=[pl.BlockSpec((None, None, REDUCE_ROWS, PACK_W), lambda j, i, c_ref: (c_ref[0], j, i, 0)),
                      pl.BlockSpec((None, REDUCE_ROWS, PACK_W), lambda j, i, c_ref: (j, i, 0))],
            out_specs=pl.BlockSpec((None, REDUCE_ROWS, PACK_W), lambda j, i, c_ref: (j, i, 0))),
        out_shape=jax.ShapeDtypeStruct((N_CHIPS, HALF_ROWS, PACK_W), WIRE_DTYPE),
        compiler_params=_params("arbitrary", "arbitrary"))(c.reshape(1), pg, theirs)


def _sum_chips(ps, others, k):
    def body(k_ref, a_ref, b_ref, o_ref):
        o_ref[...] = ((a_ref[...].astype(F32) + b_ref[0].astype(F32)) + b_ref[1].astype(F32)) + b_ref[2].astype(F32)

    return pl.pallas_call(
        body, name="sum_chips",
        grid_spec=pltpu.PrefetchScalarGridSpec(
            num_scalar_prefetch=1, grid=(HALF_ROWS // REDUCE_ROWS,),
            in_specs=[pl.BlockSpec((None, REDUCE_ROWS, PACK_W), lambda i, k_ref: (k_ref[0], i, 0)),
                      pl.BlockSpec((3, REDUCE_ROWS, PACK_W), lambda i, k_ref: (0, i, 0))],
            out_specs=pl.BlockSpec((REDUCE_ROWS, PACK_W), lambda i, k_ref: (i, 0))),
        out_shape=jax.ShapeDtypeStruct((HALF_ROWS, PACK_W), F32),
        compiler_params=_params("arbitrary"))(k.reshape(1), ps, others)


REST = tuple(n for n in BIG if n != "w_in")
REST_ROWS = sum(BIG_SHAPE[n][0] * BIG_SHAPE[n][1] for n in REST) // (N_CHIPS * PACK_W)
IN_SHARD = IN_COLS // N_CHIPS
IN_TILE, REST_TILE = 256, 272


def _pack_rest(parts):
    return jnp.concatenate([parts[n].reshape(-1, PACK_W) for n in REST], axis=0)


def _unpack_rest(packed):
    out, off = {}, 0
    for n in REST:
        r, c = _shard_shape(n)
        rows = r * c // PACK_W
        out[n] = packed[off:off + rows].reshape(1, r, c)
        off += rows
    return out


def _unpack_rest_full(gathered):
    out, off = {}, 0
    for n in REST:
        r, c = _shard_shape(n)
        rows = r * c // PACK_W
        sh = gathered[:, off:off + rows].reshape(N_CHIPS, r, c)
        out[n] = sh.transpose(1, 0, 2).reshape(BIG_SHAPE[n]) if n in BIG_COL_SHARDED else sh.reshape(BIG_SHAPE[n])
        off += rows
    return out


def _pack_rest_full(full):
    parts = []
    for n in REST:
        r, c = _shard_shape(n)
        g = full[n]
        sh = g.reshape(BIG_SHAPE[n][0], N_CHIPS, c).transpose(1, 0, 2) if n in BIG_COL_SHARDED else g
        parts.append(sh.reshape(N_CHIPS, r * c // PACK_W, PACK_W))
    return jnp.concatenate(parts, axis=1).reshape(N_CHIPS, 2, REST_ROWS // 2, PACK_W)


def _gather_shards(ws):
    n = len(ws)

    def body(*refs):
        w_refs, out_refs, (send_sems, recv_sems) = refs[:n], refs[n:2 * n], refs[2 * n:]
        x, y, c, chips = _place()
        k = 2 * x + y
        sibling = (x, y, 1 - c)
        first = [_remote(w_ref.at[c], out_ref.at[k, c], send_sems, recv_sems, 6 * g + j, (cx, cy, c))
                 for j, (cx, cy) in enumerate(chips) for g, (w_ref, out_ref) in enumerate(zip(w_refs, out_refs))]
        for cp in first:
            cp.start()
        passed = []
        for j, (cx, cy) in enumerate(chips):
            kj = 2 * cx + cy
            for g, (w_ref, out_ref) in enumerate(zip(w_refs, out_refs)):
                _remote(w_ref.at[c], out_ref.at[kj, c], send_sems, recv_sems, 6 * g + j, (cx, cy, c)).wait_recv()
                cp = _remote(out_ref.at[kj, c], out_ref.at[kj, c], send_sems, recv_sems, 6 * g + 3 + j, sibling)
                cp.start()
                passed.append(cp)
        for j, (cx, cy) in enumerate(chips):
            kj = 2 * cx + cy
            for g, (w_ref, out_ref) in enumerate(zip(w_refs, out_refs)):
                _remote(w_ref.at[c], out_ref.at[kj, 1 - c], send_sems, recv_sems, 6 * g + 3 + j, sibling).wait_recv()
        for cp in first + passed:
            cp.wait_send()

    return pl.pallas_call(
        body, name="all_gather_weights", in_specs=[_HBM] * n, out_specs=[_HBM] * n,
        out_shape=[jax.ShapeDtypeStruct((N_CHIPS,) + w.shape, w.dtype) for w in ws],
        scratch_shapes=[pltpu.SemaphoreType.DMA((6 * n,)), pltpu.SemaphoreType.DMA((6 * n,))])(*ws)


def _swap_halves(pgs, name="exchange_halves"):
    n = len(pgs)

    def body(*refs):
        pg_refs, out_refs, (send_sems, recv_sems) = refs[:n], refs[n:2 * n], refs[2 * n:]
        x, y, c, _ = _place()
        cps = [_remote(pg_ref.at[j, 1 - c], out_ref.at[j], send_sems, recv_sems, N_CHIPS * g + j, (x, y, 1 - c))
               for g, (pg_ref, out_ref) in enumerate(zip(pg_refs, out_refs)) for j in range(N_CHIPS)]
        for cp in cps:
            cp.start()
        for cp in cps:
            cp.wait()

    return pl.pallas_call(
        body, name=name, in_specs=[_HBM] * n, out_specs=[_HBM] * n,
        out_shape=[jax.ShapeDtypeStruct((N_CHIPS,) + pg.shape[2:], pg.dtype) for pg in pgs],
        scratch_shapes=[pltpu.SemaphoreType.DMA((N_CHIPS * n,)), pltpu.SemaphoreType.DMA((N_CHIPS * n,))])(*pgs)


def _scatter_sums(pss):
    n = len(pss)

    def body(*refs):
        ps_refs, out_refs, (send_sems, recv_sems) = refs[:n], refs[n:2 * n], refs[2 * n:]
        x, y, c, chips = _place()
        cps = [_remote(ps_ref.at[2 * cx + cy], out_ref.at[j], send_sems, recv_sems, 3 * g + j, (cx, cy, c))
               for j, (cx, cy) in enumerate(chips) for g, (ps_ref, out_ref) in enumerate(zip(ps_refs, out_refs))]
        for cp in cps:
            cp.start()
        for cp in cps:
            cp.wait()

    return pl.pallas_call(
        body, name="scatter_chip_sums", in_specs=[_HBM] * n, out_specs=[_HBM] * n,
        out_shape=[jax.ShapeDtypeStruct((3,) + ps.shape[1:], ps.dtype) for ps in pss],
        scratch_shapes=[pltpu.SemaphoreType.DMA((3 * n,)), pltpu.SemaphoreType.DMA((3 * n,))])(*pss)


def _share_halves(gs):
    n = len(gs)

    def body(*refs):
        g_refs, out_refs, (send_sems, recv_sems) = refs[:n], refs[n:2 * n], refs[2 * n:]
        x, y, c, _ = _place()
        cps = [_remote(g_ref, out_ref.at[c], send_sems, recv_sems, g, (x, y, 1 - c))
               for g, (g_ref, out_ref) in enumerate(zip(g_refs, out_refs))]
        for cp in cps:
            cp.start()
        for g, (g_ref, out_ref) in enumerate(zip(g_refs, out_refs)):
            _remote(g_ref, out_ref.at[1 - c], send_sems, recv_sems, g, (x, y, 1 - c)).wait_recv()
        for cp in cps:
            cp.wait_send()

    return pl.pallas_call(
        body, name="share_half", in_specs=[_HBM] * n, out_specs=[_HBM] * n,
        out_shape=[jax.ShapeDtypeStruct((2,) + g.shape, g.dtype) for g in gs],
        scratch_shapes=[pltpu.SemaphoreType.DMA((n,)), pltpu.SemaphoreType.DMA((n,))])(*gs)


def _pair_sum(name, pg, theirs, c, tile):
    _, _, rows, width = pg.shape

    def body(c_ref, a_ref, b_ref, o_ref):
        o_ref[...] = (a_ref[...] + b_ref[...]).astype(o_ref.dtype)

    return pl.pallas_call(
        body, name=name,
        grid_spec=pltpu.PrefetchScalarGridSpec(
            num_scalar_prefetch=1, grid=(N_CHIPS, rows // tile),
            in_specs=[pl.BlockSpec((None, None, tile, width), lambda j, i, c_ref: (j, c_ref[0], i, 0)),
                      pl.BlockSpec((None, tile, width), lambda j, i, c_ref: (j, i, 0))],
            out_specs=pl.BlockSpec((None, tile, width), lambda j, i, c_ref: (j, i, 0))),
        out_shape=jax.ShapeDtypeStruct((N_CHIPS, rows, width), WIRE_DTYPE),
        compiler_params=_params("arbitrary", "arbitrary"))(c.reshape(1), pg, theirs)


def _chip_sum(name, ps, others, k, tile):
    _, rows, width = ps.shape

    def body(k_ref, a_ref, b_ref, o_ref):
        o_ref[...] = ((a_ref[...].astype(F32) + b_ref[0].astype(F32)) + b_ref[1].astype(F32)) + b_ref[2].astype(F32)

    return pl.pallas_call(
        body, name=name,
        grid_spec=pltpu.PrefetchScalarGridSpec(
            num_scalar_prefetch=1, grid=(rows // tile,),
            in_specs=[pl.BlockSpec((None, tile, width), lambda i, k_ref: (k_ref[0], i, 0)),
                      pl.BlockSpec((3, tile, width), lambda i, k_ref: (0, i, 0))],
            out_specs=pl.BlockSpec((tile, width), lambda i, k_ref: (i, 0))),
        out_shape=jax.ShapeDtypeStruct((rows, width), F32),
        compiler_params=_params("arbitrary"))(k.reshape(1), ps, others)


class _StepComm:
    TILES = {"in": IN_TILE, "rest": REST_TILE}

    def __init__(self, rest_wire, chip, core):
        self.rest_wire, self.chip, self.core = rest_wire, chip, core
        self.sums, self.landed = {}, {}

    def gather_rest(self):
        wire = self.rest_wire

        def sends(ins, outs, send_sems, recv_sems):
            (w_ref,), (out_ref,) = ins, outs
            x, y, c, chips = _place()
            return [_remote(w_ref.at[c], out_ref.at[2 * x + y, c], send_sems, recv_sems, 4 * j + 2 * c + to,
                            (cx, cy, to)) for j, (cx, cy) in enumerate(chips) for to in (0, 1)]

        def recvs(ins, outs, send_sems, recv_sems):
            (w_ref,), (out_ref,) = ins, outs
            _, _, c, chips = _place()
            return [_remote(w_ref.at[c], out_ref.at[2 * cx + cy, by], send_sems, recv_sems, 4 * j + 2 * by + c,
                            (cx, cy, by)) for j, (cx, cy) in enumerate(chips) for by in (0, 1)]

        def start(*refs):
            for cp in sends(*refs):
                cp.start()

        def wait(*refs):
            for cp in recvs(*refs):
                cp.wait_recv()
            for cp in sends(*refs):
                cp.wait_send()

        return _Riding((wire,), (jax.ShapeDtypeStruct((N_CHIPS,) + wire.shape, wire.dtype),), 12, start, wait)

    def rest_weights(self, landed):
        full = lax.dynamic_update_slice(landed, self.rest_wire[None], (self.chip, 0, 0, 0))
        return _unpack_rest_full(full.reshape(N_CHIPS, REST_ROWS, PACK_W))

    def swap(self, pg):
        def copies(ins, outs, send_sems, recv_sems):
            (pg_ref,), (out_ref,) = ins, outs
            x, y, c, _ = _place()
            return [_remote(pg_ref.at[j, 1 - c], out_ref.at[j], send_sems, recv_sems, j, (x, y, 1 - c))
                    for j in range(N_CHIPS)]

        def start(*refs):
            for cp in copies(*refs):
                cp.start()

        def wait(*refs):
            for cp in copies(*refs):
                cp.wait()

        return _Riding((pg,), (jax.ShapeDtypeStruct((N_CHIPS,) + pg.shape[2:], pg.dtype),), N_CHIPS, start, wait)

    def scatter(self, group, pg, theirs=None):
        if theirs is None:
            (theirs,) = _swap_halves([pg], "exchange_halves_" + group)
        ps = _pair_sum("sum_pair_" + group, pg, theirs, self.core, self.TILES[group])
        self.sums[group] = ps

        def copies(ins, outs, send_sems, recv_sems):
            (ps_ref,), (out_ref,) = ins, outs
            _, _, c, chips = _place()
            return [_remote(ps_ref.at[2 * cx + cy], out_ref.at[j], send_sems, recv_sems, j, (cx, cy, c))
                    for j, (cx, cy) in enumerate(chips)]

        def start(*refs):
            for cp in copies(*refs):
                cp.start()

        def wait(*refs):
            for cp in copies(*refs):
                cp.wait()

        return _Riding((ps,), (jax.ShapeDtypeStruct((3,) + ps.shape[1:], ps.dtype),), 3, start, wait)

    def reduced(self, group):
        return _chip_sum("sum_chips_" + group, self.sums[group], self.landed[group], self.chip, self.TILES[group])


def _adamw(w, g, m, v):
    m = ADAM_B1 * m + (1.0 - ADAM_B1) * g
    v = ADAM_B2 * v + (1.0 - ADAM_B2) * (g * g)
    m_hat = m / (1.0 - ADAM_B1 ** ADAM_STEP)
    v_hat = v / (1.0 - ADAM_B2 ** ADAM_STEP)
    return -ADAM_LR * (m_hat / (jnp.sqrt(v_hat) + ADAM_EPS) + ADAM_WD * w), m, v


def _small_reduce_adamw(part, w, m, v):
    def body(part_ref, w_ref, m_ref, v_ref, g_ref, d_ref, nm_ref, nv_ref, all_ref, send_sems, recv_sems):
        x, y, c, chips = _place()
        me, sibling = (x, y, c), (x, y, 1 - c)

        def rows(px, py, pc):
            return all_ref.at[4 * px + 2 * py + pc]

        all_ref[4 * x + 2 * y + c] = part_ref[...]
        first = [_remote(part_ref, rows(*me), send_sems, recv_sems, 0, sibling)]
        first += [_remote(part_ref, rows(*me), send_sems, recv_sems, 1 + j, (cx, cy, c))
                  for j, (cx, cy) in enumerate(chips)]
        for cp in first:
            cp.start()
        passed = []
        for j, (cx, cy) in enumerate(chips):
            _remote(part_ref, rows(cx, cy, c), send_sems, recv_sems, 1 + j, me).wait_recv()
            cp = _remote(rows(cx, cy, c), rows(cx, cy, c), send_sems, recv_sems, 4 + j, sibling)
            cp.start()
            passed.append(cp)
        _remote(part_ref, rows(*sibling), send_sems, recv_sems, 0, me).wait_recv()
        for j, (cx, cy) in enumerate(chips):
            _remote(part_ref, rows(cx, cy, 1 - c), send_sems, recv_sems, 4 + j, me).wait_recv()
        for cp in first + passed:
            cp.wait_send()
        g = all_ref[0]
        for dev in range(1, N_DEV):
            g = g + all_ref[dev]
        delta, nm, nv = _adamw(w_ref[...], g, m_ref[...], v_ref[...])
        g_ref[...] = g
        d_ref[...] = delta
        nm_ref[...] = nm
        nv_ref[...] = nv

    whole = pl.BlockSpec(memory_space=pltpu.VMEM)
    shape = jax.ShapeDtypeStruct((SMALL_ROWS, PACK_W), F32)
    return pl.pallas_call(
        body, name="small_reduce_adamw", in_specs=[whole] * 4, out_specs=[whole] * 4, out_shape=[shape] * 4,
        scratch_shapes=[pltpu.VMEM((N_DEV, SMALL_ROWS, PACK_W), F32), pltpu.SemaphoreType.DMA((7,)),
                        pltpu.SemaphoreType.DMA((7,))],
        compiler_params=pltpu.CompilerParams(vmem_limit_bytes=VMEM_LIMIT))(part, w, m, v)


def kernel(x, p, norm_g, w_in, hg_lb, hg_norm_g, w_o_hg, s5_a_re, s5_a_im, s5_log_dt, s5_b_re, s5_b_im, s5_c_re, s5_c_im, s5_d, w_glu, b_glu, w_o_s5, w_out, ple_norm_g, w_ple, w_ple_gate, final_norm_g, loss_target, m_norm_g, m_w_in, m_hg_lb, m_hg_norm_g, m_w_o_hg, m_s5_a_re, m_s5_a_im, m_s5_log_dt, m_s5_b_re, m_s5_b_im, m_s5_c_re, m_s5_c_im, m_s5_d, m_w_glu, m_b_glu, m_w_o_s5, m_w_out, m_ple_norm_g, m_w_ple, m_w_ple_gate, m_final_norm_g, v_norm_g, v_w_in, v_hg_lb, v_hg_norm_g, v_w_o_hg, v_s5_a_re, v_s5_a_im, v_s5_log_dt, v_s5_b_re, v_s5_b_im, v_s5_c_re, v_s5_c_im, v_s5_d, v_w_glu, v_b_glu, v_w_o_s5, v_w_out, v_ple_norm_g, v_w_ple, v_w_ple_gate, v_final_norm_g):
    given = dict(locals())
    wts = {n: given[n] for n in WEIGHTS}
    mom = {n: given["m_" + n] for n in WEIGHTS}
    var = {n: given["v_" + n] for n in WEIGHTS}
    cx, cy, cc = lax.axis_index("x"), lax.axis_index("y"), lax.axis_index("c")
    chip = (2 * cx + cy).astype(jnp.int32)

    core = cc.astype(jnp.int32)
    rest_shard = _pack_rest({n: wts[n][0] for n in REST})
    in_wire = wts["w_in"][0].astype(MXU_DTYPE).reshape(2, D_MODEL // 2, IN_SHARD)
    (w_in_all,) = _gather_shards([in_wire])
    w_in_all = lax.dynamic_update_slice(w_in_all, in_wire[None], (chip, 0, 0, 0)).reshape(N_CHIPS, D_MODEL, IN_SHARD)
    comm = _StepComm(rest_shard.astype(MXU_DTYPE).reshape(2, REST_ROWS // 2, PACK_W), chip, core)

    t_len = x.shape[1]
    loss_row, grad_x, g_big, g_small = _local_step(x.reshape(t_len, D_MODEL), p.reshape(t_len, -1),
                                                   loss_target.reshape(t_len, D_MODEL), {"w_in": w_in_all},
                                                   {n: wts[n] for n in SMALL}, comm)

    zero = jnp.zeros((), F32)
    sg, sd, snm, snv = _small_reduce_adamw(_pack_small(g_small, loss_row[0, 0]),
                                           _pack_small({n: wts[n] for n in SMALL}, zero),
                                           _pack_small({n: mom[n] for n in SMALL}, zero),
                                           _pack_small({n: var[n] for n in SMALL}, zero))
    (sg, loss), (sd, _), (snm, _), (snv, _) = (_unpack_small(a) for a in (sg, sd, snm, snv))

    halves = [comm.reduced("in"), comm.reduced("rest")]
    g_in, g_rest = [lax.dynamic_update_slice(got, mine[None], (core, 0, 0))
                    for got, mine in zip(_share_halves(halves), halves)]
    g_in, g_rest = g_in.reshape(D_MODEL, IN_SHARD), g_rest.reshape(REST_ROWS, PACK_W)

    def adam_f(wv, gv, mv, vv):
        return _adamw(wv, gv, mv, vv)

    d_in, nm_in, nv_in = _rowwise("adamw_in", adam_f, D_MODEL, IN_TILE,
                                  [(wts["w_in"][0], IN_SHARD, 0), (g_in, IN_SHARD, 0), (mom["w_in"][0], IN_SHARD, 0),
                                   (var["w_in"][0], IN_SHARD, 0)], [], [(IN_SHARD, F32)] * 3)
    d_rest, nm_rest, nv_rest = _rowwise("adamw_rest", adam_f, REST_ROWS, REST_TILE,
                                        [(rest_shard, PACK_W, 0), (g_rest, PACK_W, 0),
                                         (_pack_rest({n: mom[n][0] for n in REST}), PACK_W, 0),
                                         (_pack_rest({n: var[n][0] for n in REST}), PACK_W, 0)], [],
                                        [(PACK_W, F32)] * 3)
    bg, bd, bnm, bnv = (dict(_unpack_rest(rest), w_in=a.reshape(1, D_MODEL, IN_SHARD))
                        for rest, a in ((g_rest, g_in), (d_rest, d_in), (nm_rest, nm_in), (nv_rest, nv_in)))

    outs = [loss, grad_x.reshape(x.shape)]
    for small, big in ((sg, bg), (sd, bd), (snm, bnm), (snv, bnv)):
        outs += [big[n] if n in BIG else small[n] for n in WEIGHTS]
    return tuple(outs)
```

```python
import functools
from typing import Callable, NamedTuple

import jax
import jax.numpy as jnp
from jax import lax
from jax.experimental import pallas as pl
from jax.experimental.pallas import tpu as pltpu

F32 = jnp.float32
MXU_DTYPE = jnp.bfloat16
WIRE_DTYPE = jnp.bfloat16
NORM_EPS = 1e-6
D_MODEL = 1024
HG_HEADS = 8
HG_DIM = 128
HG_CHUNK = 64
S5_WIDTH = 512
S5_GROUPS = 32
S5_GROUP = 16
S5_STATE = 64
S5_LANES = S5_GROUPS * S5_STATE
IN_COLS = 7168
SUBLANES = 8
VMEM_LIMIT = 56 * 1024 * 1024
HIGHEST = lax.Precision.HIGHEST
MESH = pl.DeviceIdType.MESH

ADAM_LR, ADAM_B1, ADAM_B2, ADAM_EPS, ADAM_WD, ADAM_STEP = 0.001, 0.9, 0.999, 1e-08, 0.01, 10

BIG = ("w_in", "w_o_hg", "w_glu", "w_o_s5", "w_out", "w_ple", "w_ple_gate")
BIG_SHAPE = {"w_in": (1024, 7168), "w_o_hg": (1024, 1024), "w_glu": (512, 1024), "w_o_s5": (512, 1024),
             "w_out": (1024, 1024), "w_ple": (256, 1024), "w_ple_gate": (1024, 1024)}
BIG_COL_SHARDED = ("w_in", "w_glu", "w_o_s5", "w_ple")
SMALL = ("norm_g", "hg_lb", "hg_norm_g", "s5_a_re", "s5_a_im", "s5_log_dt", "s5_b_re", "s5_b_im", "s5_c_re",
         "s5_c_im", "s5_d", "b_glu", "ple_norm_g", "final_norm_g")
SMALL_SHAPE = {"norm_g": (1, 1024), "hg_lb": (2, 1024), "hg_norm_g": (1, 1024), "s5_a_re": (1, 32, 64),
               "s5_a_im": (1, 32, 64), "s5_log_dt": (1, 32), "s5_b_re": (1, 32, 64, 16), "s5_b_im": (1, 32, 64, 16),
               "s5_c_re": (1, 32, 16, 64), "s5_c_im": (1, 32, 16, 64), "s5_d": (1, 32, 16), "b_glu": (1, 1024),
               "ple_norm_g": (1, 1024), "final_norm_g": (1024,)}
WEIGHTS = ("norm_g", "w_in", "hg_lb", "hg_norm_g", "w_o_hg", "s5_a_re", "s5_a_im", "s5_log_dt", "s5_b_re", "s5_b_im",
           "s5_c_re", "s5_c_im", "s5_d", "w_glu", "b_glu", "w_o_s5", "w_out", "ple_norm_g", "w_ple", "w_ple_gate",
           "final_norm_g")
N_CHIPS = 4
N_DEV = 8
PACK_W = 1024
SMALL_ROWS = 144


def _params(*sem):
    return pltpu.CompilerParams(dimension_semantics=sem, vmem_limit_bytes=VMEM_LIMIT)


def _sig(x):
    return 1.0 / (1.0 + jnp.exp(-x))


def _dsilu(z, s):
    return s * (1.0 + z * (1.0 - s))


def _mx(x):
    return x.astype(MXU_DTYPE)


def _dot(a, b, dims=(((1,), (0,)), ((), ()))):
    return lax.dot_general(_mx(a), _mx(b), dims, preferred_element_type=F32)


_NT = (((1,), (1,)), ((), ()))
_TN = (((0,), (0,)), ((), ()))


def _dot32(a, b):
    return jnp.dot(a, b, precision=HIGHEST, preferred_element_type=F32)


def _rms_bwd(dy, x, g):
    r = lax.rsqrt(jnp.mean(x * x, axis=-1, keepdims=True) + NORM_EPS)
    t = dy * g
    dx = r * t - x * (r * r * r) * jnp.mean(t * x, axis=-1, keepdims=True)
    return dx, jnp.sum(dy * x * r, axis=0, keepdims=True)


def _rowwise(name, fn, n_rows_total, tm, rows, consts, outs, accs=(), alias=None):
    n_r, n_c, n_o, n_a = len(rows), len(consts), len(outs), len(accs)

    def body(*refs):
        row_refs = refs[:n_r]
        const_refs = refs[n_r:n_r + n_c]
        pos = n_r + n_c + (1 if alias is not None else 0)
        out_refs = refs[pos:pos + n_o]
        acc_refs = refs[pos + n_o:pos + n_o + n_a]
        res = fn(*[r[...] for r in row_refs], *[r[...] for r in const_refs])
        for r, v in zip(out_refs, res[:n_o]):
            r[...] = v.astype(r.dtype)
        if n_a:
            @pl.when(pl.program_id(0) == 0)
            def _():
                for r in acc_refs:
                    r[...] = jnp.zeros_like(r)
            for r, v in zip(acc_refs, res[n_o:]):
                r[...] += v

    in_specs = [pl.BlockSpec((tm, w), functools.partial(lambda i, cb: (i, cb), cb=cb)) for (_, w, cb) in rows]
    in_specs += [pl.BlockSpec(c.shape, lambda i: (0, 0)) for c in consts]
    args = [a for (a, _, _) in rows] + list(consts)
    out_shape, out_specs = [], []
    for o in outs:
        w, dt = o[0], o[1]
        cb, total = (o[2], o[3]) if len(o) == 4 else (0, w)
        out_shape.append(jax.ShapeDtypeStruct((n_rows_total, total), dt))
        out_specs.append(pl.BlockSpec((tm, w), functools.partial(lambda i, cb: (i, cb), cb=cb)))
    io_alias = {}
    if alias is not None:
        in_specs.append(pl.BlockSpec(memory_space=pl.ANY))
        args.append(alias[0])
        io_alias = {len(args) - 1: alias[1]}
    for (r, w) in accs:
        out_shape.append(jax.ShapeDtypeStruct((r, w), F32))
        out_specs.append(pl.BlockSpec((r, w), lambda i: (0, 0)))
    res = pl.pallas_call(body, name=name, grid=(n_rows_total // tm,), in_specs=in_specs, out_specs=out_specs,
                         out_shape=out_shape, input_output_aliases=io_alias,
                         compiler_params=_params("arbitrary"))(*args)
    return res


class _Riding(NamedTuple):
    ins: tuple
    outs: tuple
    n_sems: int
    start: Callable
    wait: Callable


_HBM = pl.BlockSpec(memory_space=pl.ANY)


def _ride(riding, refs, n_in, n_out, n_scratch, first, last):
    if riding is None:
        return refs[:n_in], refs[n_in:n_in + n_out], refs[n_in + n_out:]
    r_in, r_out = len(riding.ins), len(riding.outs)
    ins, rins = refs[:n_in], refs[n_in:n_in + r_in]
    pos = n_in + r_in
    outs, routs = refs[pos:pos + n_out], refs[pos + n_out:pos + n_out + r_out]
    pos += n_out + r_out
    scratch, (send_sems, recv_sems) = refs[pos:pos + n_scratch], refs[pos + n_scratch:]

    @pl.when(first)
    def _():
        riding.start(rins, routs, send_sems, recv_sems)

    @pl.when(last)
    def _():
        riding.wait(rins, routs, send_sems, recv_sems)

    return ins, outs, scratch


def _riding_call(riding, body, name, grid, in_specs, args, out_specs, out_shape, scratch, io_alias=None):
    if riding is not None:
        in_specs = list(in_specs) + [_HBM] * len(riding.ins)
        args = list(args) + list(riding.ins)
        out_specs = list(out_specs) + [_HBM] * len(riding.outs)
        out_shape = list(out_shape) + list(riding.outs)
        scratch = list(scratch) + [pltpu.SemaphoreType.DMA((riding.n_sems,))] * 2
    return pl.pallas_call(body, name=name, grid=grid, in_specs=in_specs, out_specs=out_specs, out_shape=out_shape,
                          scratch_shapes=scratch, input_output_aliases=io_alias or {},
                          compiler_params=_params(*(["arbitrary"] * len(grid))))(*args)


def _mm_nn(name, a, b, tm, tn, riding=None, prologue=None, consts=()):
    m, k = a.shape
    n = b.shape[1] if b.ndim == 2 else b.shape[0] * b.shape[2]
    grid = (n // tn, m // tm)
    n_out, scratch = (1, []) if prologue is None else (2, [pltpu.VMEM((m, k), MXU_DTYPE)])

    def body(*refs):
        j, i = pl.program_id(0), pl.program_id(1)
        ins, outs, kept = _ride(riding, refs, 2 + len(consts), n_out, len(scratch), (j == 0) & (i == 0),
                                (j == grid[0] - 1) & (i == grid[1] - 1))
        if prologue is None:
            left = ins[0][...]
        else:
            rows = pl.ds(pl.multiple_of(i * tm, tm), tm)

            @pl.when(j == 0)
            def _():
                tile = _mx(prologue(ins[0][...], *[c[...] for c in ins[2:]]))
                kept[0][rows, :] = tile
                outs[1][...] = tile

            left = kept[0][rows, :]
        outs[0][...] = _dot(left, ins[1][...])

    once = (lambda j, i: (i, 0)) if prologue is None else (lambda j, i: (jnp.where(j == 0, i, grid[1] - 1), 0))
    b_spec = (pl.BlockSpec((k, tn), lambda j, i: (0, j)) if b.ndim == 2
              else pl.BlockSpec((None, k, tn), lambda j, i: (j, 0, 0)))
    in_specs = [pl.BlockSpec((tm, k), once), b_spec]
    in_specs += [pl.BlockSpec(c.shape, lambda j, i: (0, 0)) for c in consts]
    out_specs = [pl.BlockSpec((tm, tn), lambda j, i: (i, j))]
    out_shape = [jax.ShapeDtypeStruct((m, n), F32)]
    if prologue is not None:
        out_specs.append(pl.BlockSpec((tm, k), once))
        out_shape.append(jax.ShapeDtypeStruct((m, k), MXU_DTYPE))
    res = _riding_call(riding, body, name, grid, in_specs, [a, b] + list(consts), out_specs, out_shape, scratch)
    return res[0] if riding is None and prologue is None else res


def _mm_nt_then(name, a, b, tm, tn, fn, rows, consts, outs, accs=(), alias=None, riding=None):
    m, n = a.shape
    k = b.shape[-2]
    steps = n // tn
    n_r, n_c, n_o, n_a = len(rows), len(consts), len(outs), len(accs)

    def body(*refs):
        a_ref, b_ref = refs[:2]
        row_refs = refs[2:2 + n_r]
        const_refs = refs[2 + n_r:2 + n_r + n_c]
        i, s = pl.program_id(0), pl.program_id(1)
        n_in = 2 + n_r + n_c + (1 if alias is not None else 0)
        _, outs_, (mm_ref,) = _ride(riding, refs, n_in, n_o + n_a, 1, (i == 0) & (s == 0),
                                    (i == m // tm - 1) & (s == steps - 1))
        out_refs, acc_refs = outs_[:n_o], outs_[n_o:]
        part = _dot(a_ref[...], b_ref[...], _NT)
        if steps > 1:
            @pl.when(s == 0)
            def _():
                mm_ref[...] = jnp.zeros_like(mm_ref)
            mm_ref[...] += part

        @pl.when(s == steps - 1)
        def _():
            res = fn(mm_ref[...] if steps > 1 else part, *[r[...] for r in row_refs], *[r[...] for r in const_refs])
            for r, v in zip(out_refs, res[:n_o]):
                r[...] = v.astype(r.dtype)
            if n_a:
                @pl.when(i == 0)
                def _():
                    for r in acc_refs:
                        r[...] = jnp.zeros_like(r)
                for r, v in zip(acc_refs, res[n_o:]):
                    r[...] += v

    b_spec = (pl.BlockSpec((k, tn), lambda i, s: (0, s)) if b.ndim == 2
              else pl.BlockSpec((None, k, tn), lambda i, s: (s, 0, 0)))
    in_specs = [pl.BlockSpec((tm, tn), lambda i, s: (i, s)), b_spec]
    in_specs += [pl.BlockSpec((tm, w), functools.partial(lambda i, s, cb: (i, cb), cb=cb)) for (_, w, cb) in rows]
    in_specs += [pl.BlockSpec(c.shape, lambda i, s: (0, 0)) for c in consts]
    args = [a, b] + [r[0] for r in rows] + list(consts)
    out_shape, out_specs = [], []
    for o in outs:
        w, dt = o[0], o[1]
        cb, total = (o[2], o[3]) if len(o) == 4 else (0, w)
        out_shape.append(jax.ShapeDtypeStruct((m, total), dt))
        out_specs.append(pl.BlockSpec((tm, w), functools.partial(lambda i, s, cb: (i, cb), cb=cb)))
    io_alias = {}
    if alias is not None:
        in_specs.append(pl.BlockSpec(memory_space=pl.ANY))
        args.append(alias[0])
        io_alias = {len(args) - 1: alias[1]}
    for (r, w) in accs:
        out_shape.append(jax.ShapeDtypeStruct((r, w), F32))
        out_specs.append(pl.BlockSpec((r, w), lambda i, s: (0, 0)))
    return _riding_call(riding, body, name, (m // tm, steps), in_specs, args, out_specs, out_shape,
                        [pltpu.VMEM((tm, k), F32)], io_alias)


def _mm_tn(name, a, b, tk, tn, col_shards=False, riding=None):
    t, k = a.shape
    n = b.shape[1]
    steps = t // tk

    def body(*refs):
        j, s = pl.program_id(0), pl.program_id(1)
        (a_ref, b_ref), (o_ref,), (acc_ref,) = _ride(riding, refs, 2, 1, 1, (j == 0) & (s == 0),
                                                     (j == n // tn - 1) & (s == steps - 1))

        @pl.when(s == 0)
        def _():
            acc_ref[...] = jnp.zeros_like(acc_ref)

        acc_ref[...] += _dot(a_ref[...], b_ref[...], _TN)

        @pl.when(s == steps - 1)
        def _():
            o_ref[...] = acc_ref[...]

    if col_shards:
        out_spec = pl.BlockSpec((None, k, tn), lambda j, s: (j, 0, 0))
        out_shape = jax.ShapeDtypeStruct((n // tn, k, tn), F32)
    else:
        out_spec = pl.BlockSpec((k, tn), lambda j, s: (0, j))
        out_shape = jax.ShapeDtypeStruct((k, n), F32)
    res = _riding_call(riding, body, name, (n // tn, steps),
                       [pl.BlockSpec((tk, k), lambda j, s: (s, 0)), pl.BlockSpec((tk, tn), lambda j, s: (s, j))],
                       [a, b], [out_spec], [out_shape], [pltpu.VMEM((k, tn), F32)])
    return res[0] if riding is None else res


def _dot01(m01, x):
    m = m01.astype(MXU_DTYPE)
    hi = x.astype(MXU_DTYPE)
    r1 = x - hi.astype(F32)
    mid = r1.astype(MXU_DTYPE)
    lo = (r1 - mid.astype(F32)).astype(MXU_DTYPE)
    dot = lambda v: jnp.dot(m, v, preferred_element_type=F32)
    return dot(hi) + dot(mid) + dot(lo)


def _chunk_rows(x, offset, nck):
    return jnp.concatenate([jnp.broadcast_to(x[c * HG_CHUNK + offset:c * HG_CHUNK + offset + 1, :],
                                             (HG_CHUNK, x.shape[1])) for c in range(nck)], axis=0)


def _hg_block_terms(q, f, lb, tb):
    nck = tb // HG_CHUNK
    sig = _sig(f)
    fv = lb + (1.0 - lb) * sig
    kk = (1.0 - lb) * (1.0 - sig)
    row = lax.broadcasted_iota(jnp.int32, (tb, tb), 0)
    col = lax.broadcasted_iota(jnp.int32, (tb, tb), 1)
    same = jnp.right_shift(row, 6) == jnp.right_shift(col, 6)
    causal, anti = same & (row >= col), same & (row <= col)
    b = _dot01(causal, jnp.log(fv))
    b_mid, b_last = _chunk_rows(b, HG_CHUNK // 2 - 1, nck), _chunk_rows(b, HG_CHUNK - 1, nck)
    e_mid, e_mid_inv = jnp.exp(b - b_mid), jnp.exp(b_mid - b)
    e_b, e_last = jnp.exp(b), jnp.exp(b_last - b)
    dcs = [jnp.exp(b[c * HG_CHUNK + HG_CHUNK - 1:(c + 1) * HG_CHUNK, :]) for c in range(nck)]
    return sig, fv, kk, causal, anti, e_mid, e_mid_inv, e_b, e_last, dcs


def _hgrn2_fwd(proj, hg_lb, hg_norm_g, t_len, tb):
    nck = tb // HG_CHUNK

    def body(p_ref, lb_ref, gn_ref, o_ref, act_ref, sp_ref, st_ref, a_s, bm_s, qd_s, kd_s, v_s, sc_s, inc_s):
        @pl.when(pl.program_id(0) == 0)
        def _():
            st_ref[...] = jnp.zeros_like(st_ref)

        lb = _sig(lb_ref[0:1, :] - lb_ref[1:2, :])
        q = p_ref[:, pl.ds(0, 1024)]
        _, _, kk, causal, _, e_mid, e_mid_inv, e_b, e_last, dcs = _hg_block_terms(q, p_ref[:, pl.ds(1024, 1024)],
                                                                                   lb, tb)
        a_s[...] = _mx(q * e_mid)
        bm_s[...] = _mx(kk * e_mid_inv)
        qd_s[...] = _mx(q * e_b)
        kd_s[...] = _mx(kk * e_last)
        v_s[...] = _mx(p_ref[:, pl.ds(2048, 1024)])
        heads = [pl.ds(h * HG_DIM, HG_DIM) for h in range(HG_HEADS)]
        chunks = [pl.ds(c * HG_CHUNK, HG_CHUNK) for c in range(nck)]
        for h, hs in enumerate(heads):
            sc_s[h] = _mx(jnp.where(causal, _dot(a_s[:, hs], bm_s[:, hs], _NT), 0.0))
        for h, hs in enumerate(heads):
            o_ref[:, hs] = _dot(sc_s[h], v_s[:, hs])
        for h, hs in enumerate(heads):
            for c, r in enumerate(chunks):
                inc_s[h, c] = _dot(v_s[r, hs], kd_s[r, hs], _TN)
        for c in range(nck):
            for h in range(HG_HEADS):
                st = st_ref[h]
                sp_ref[h, c] = st
                st_ref[h] = dcs[c][:, h * HG_DIM:(h + 1) * HG_DIM] * st + inc_s[h, c]
        for c, r in enumerate(chunks):
            for h, hs in enumerate(heads):
                o_ref[r, hs] += _dot(qd_s[r, hs], sp_ref[h, c], _NT)
        for h, hs in enumerate(heads):
            o = o_ref[:, hs]
            rr = lax.rsqrt(jnp.mean(o * o, axis=-1, keepdims=True) + NORM_EPS)
            g = p_ref[:, pl.ds(3072 + h * HG_DIM, HG_DIM)]
            act_ref[:, hs] = (o * rr * gn_ref[:, hs] * (g * _sig(g))).astype(act_ref.dtype)

    nb = t_len // tb
    return pl.pallas_call(
        body, name="hgrn2_fwd", grid=(nb,),
        in_specs=[pl.BlockSpec((tb, 4096), lambda i: (i, 0)),
                  pl.BlockSpec((2, 1024), lambda i: (0, 0)),
                  pl.BlockSpec((1, 1024), lambda i: (0, 0))],
        out_specs=[pl.BlockSpec((tb, 1024), lambda i: (i, 0)),
                   pl.BlockSpec((tb, 1024), lambda i: (i, 0)),
                   pl.BlockSpec((HG_HEADS, nck, HG_DIM, HG_DIM), lambda i: (0, i, 0, 0))],
        out_shape=[jax.ShapeDtypeStruct((t_len, 1024), F32),
                   jax.ShapeDtypeStruct((t_len, 1024), MXU_DTYPE),
                   jax.ShapeDtypeStruct((HG_HEADS, t_len // HG_CHUNK, HG_DIM, HG_DIM), F32)],
        scratch_shapes=[pltpu.VMEM((HG_HEADS, HG_DIM, HG_DIM), F32)] + [pltpu.VMEM((tb, 1024), MXU_DTYPE)] * 5
                       + [pltpu.VMEM((HG_HEADS, tb, tb), MXU_DTYPE), pltpu.VMEM((HG_HEADS, nck, HG_DIM, HG_DIM), F32)],
        compiler_params=_params("arbitrary"))(proj, hg_lb, hg_norm_g)


def _hgrn2_bwd(proj, d_o, s_prev, hg_lb, dproj, t_len, tb, riding=None):
    nck = tb // HG_CHUNK
    nb = t_len // tb

    def body(*refs):
        step = pl.program_id(0)
        ((p_ref, do_ref, sp_ref, lb_ref, _), (dp_ref, dlb_ref),
         (ds_ref, acc_ref, a_s, bm_s, qd_s, kd_s, v_s, do_s, da_s, dbm_s, dqd_s, dkd_s, dv_s, ex_s, sc_s, dsc_s,
          up_s)) = _ride(riding, refs, 5, 2, 17, step == 0, step == nb - 1)

        @pl.when(pl.program_id(0) == 0)
        def _():
            ds_ref[...] = jnp.zeros_like(ds_ref)
            acc_ref[...] = jnp.zeros_like(acc_ref)

        lb = _sig(lb_ref[0:1, :] - lb_ref[1:2, :])
        q = p_ref[:, pl.ds(0, 1024)]
        sig, fv, kk, causal, anti, e_mid, e_mid_inv, e_b, e_last, dcs = _hg_block_terms(
            q, p_ref[:, pl.ds(1024, 1024)], lb, tb)
        a, bm, qd, kd = q * e_mid, kk * e_mid_inv, q * e_b, kk * e_last
        a_s[...] = _mx(a)
        bm_s[...] = _mx(bm)
        qd_s[...] = _mx(qd)
        kd_s[...] = _mx(kd)
        v_s[...] = _mx(p_ref[:, pl.ds(2048, 1024)])
        do_s[...] = _mx(do_ref[...])
        heads = [pl.ds(h * HG_DIM, HG_DIM) for h in range(HG_HEADS)]
        chunks = [pl.ds(c * HG_CHUNK, HG_CHUNK) for c in range(nck)]
        for h, hs in enumerate(heads):
            sc_s[h] = _mx(jnp.where(causal, _dot(a_s[:, hs], bm_s[:, hs], _NT), 0.0))
            dsc_s[h] = _mx(jnp.where(causal, _dot(do_s[:, hs], v_s[:, hs], _NT), 0.0))
        for h, hs in enumerate(heads):
            dv_s[:, hs] = _dot(sc_s[h], do_s[:, hs], _TN)
            da_s[:, hs] = _dot(dsc_s[h], bm_s[:, hs])
            dbm_s[:, hs] = _dot(dsc_s[h], a_s[:, hs], _TN)
        for h, hs in enumerate(heads):
            for c, r in enumerate(chunks):
                up_s[h, c] = _dot(do_s[r, hs], qd_s[r, hs], _TN)
                dqd_s[r, hs] = _dot(do_s[r, hs], sp_ref[h, c])
        for c in reversed(range(nck)):
            r = chunks[c]
            for h, hs in enumerate(heads):
                dst = ds_ref[h]
                dc = dcs[c][:, h * HG_DIM:(h + 1) * HG_DIM]
                dv_s[r, hs] += _dot(kd_s[r, hs], dst, _NT)
                dkd_s[r, hs] = _dot(v_s[r, hs], dst)
                ex_s[c:c + 1, hs] = jnp.sum(dst * sp_ref[h, c], axis=0, keepdims=True) * dc
                ds_ref[h] = up_s[h, c] + dc * dst
        da, dbm, dqd, dkd = da_s[...], dbm_s[...], dqd_s[...], dkd_s[...]
        dq = da * e_mid + dqd * e_b
        dk = dbm * e_mid_inv + dkd * e_last
        db = da * a - dbm * bm + dqd * qd - dkd * kd
        dkk = dkd * kd
        extra = jnp.concatenate(
            [jnp.broadcast_to(jnp.sum(dkk[c * HG_CHUNK:(c + 1) * HG_CHUNK], axis=0, keepdims=True)
                              + ex_s[c:c + 1, :], (HG_CHUNK, 1024)) for c in range(nck)], axis=0)
        dlogf = _dot01(anti, db) + extra
        dfv_k = dlogf / fv - dk
        dp_ref[:, pl.ds(0, 1024)] = dq.astype(dp_ref.dtype)
        dp_ref[:, pl.ds(1024, 1024)] = (dfv_k * (1.0 - lb) * sig * (1.0 - sig)).astype(dp_ref.dtype)
        dp_ref[:, pl.ds(2048, 1024)] = dv_s[...].astype(dp_ref.dtype)
        acc_ref[...] += jnp.sum(dfv_k * (1.0 - sig), axis=0, keepdims=True)

        @pl.when(pl.program_id(0) == nb - 1)
        def _():
            g0 = acc_ref[...] * lb * (1.0 - lb)
            dlb_ref[0:1, :] = g0
            dlb_ref[1:2, :] = -g0

    return _riding_call(
        riding, body, "hgrn2_bwd", (nb,),
        [pl.BlockSpec((tb, 3072), lambda i: (nb - 1 - i, 0)),
         pl.BlockSpec((tb, 1024), lambda i: (nb - 1 - i, 0)),
         pl.BlockSpec((HG_HEADS, nck, HG_DIM, HG_DIM), lambda i: (0, nb - 1 - i, 0, 0)),
         pl.BlockSpec((2, 1024), lambda i: (0, 0)),
         pl.BlockSpec(memory_space=pl.ANY)],
        [proj, d_o, s_prev, hg_lb, dproj],
        [pl.BlockSpec((tb, 3072), lambda i: (nb - 1 - i, 0)), pl.BlockSpec((2, 1024), lambda i: (0, 0))],
        [jax.ShapeDtypeStruct((t_len, IN_COLS), dproj.dtype), jax.ShapeDtypeStruct((2, 1024), F32)],
        [pltpu.VMEM((HG_HEADS, HG_DIM, HG_DIM), F32), pltpu.VMEM((1, 1024), F32)]
        + [pltpu.VMEM((tb, 1024), MXU_DTYPE)] * 6 + [pltpu.VMEM((tb, 1024), F32)] * 5
        + [pltpu.VMEM((SUBLANES, 1024), F32)] + [pltpu.VMEM((HG_HEADS, tb, tb), MXU_DTYPE)] * 2
        + [pltpu.VMEM((HG_HEADS, nck, HG_DIM, HG_DIM), F32)], {4: 0})


def _s5_prep_bwd(a_re, a_im, log_dt, b_re_t, b_im_t, dlam, dbbr, dbbi):
    def body(ar_ref, ai_ref, ldt_ref, br_ref, bi_ref, dlam_ref, dbbr_ref, dbbi_ref,
             dar_ref, dai_ref, dldt_ref, dbr_ref, dbi_ref):
        ar, ai = ar_ref[...], ai_ref[...]
        dt = jnp.exp(ldt_ref[...])
        mag = jnp.exp(ar * dt)
        cs, sn = jnp.cos(ai * dt), jnp.sin(ai * dt)
        lr, li = mag * cs, mag * sn
        den = ar * ar + ai * ai
        nr = lr - 1.0
        sr = (nr * ar + li * ai) / den
        si = (li * ar - nr * ai) / den
        br, bi = br_ref[...], bi_ref[...]
        gbr, gbi = dbbr_ref[...], dbbi_ref[...]
        dbr_ref[...] = sr * gbr + si * gbi
        dbi_ref[...] = sr * gbi - si * gbr
        dsr = jnp.sum(gbr * br + gbi * bi, axis=0, keepdims=True)
        dsi = jnp.sum(gbi * br - gbr * bi, axis=0, keepdims=True)
        dnr = (dsr * ar - dsi * ai) / den
        dli = dlam_ref[1:2, :] + (dsr * ai + dsi * ar) / den
        dlr = dlam_ref[0:1, :] + dnr
        dden = -(dsr * sr + dsi * si) / den
        dar = (dsr * nr + dsi * li) / den + dden * 2.0 * ar
        dai = (dsr * li - dsi * nr) / den + dden * 2.0 * ai
        dmag = dlr * cs + dli * sn
        dth = mag * (dli * cs - dlr * sn)
        dar_ref[...] = dar + dmag * mag * dt
        dai_ref[...] = dai + dth * dt
        ddt = (dmag * mag * ar + dth * ai) * dt
        lane = lax.broadcasted_iota(jnp.int32, (S5_LANES, 128), 0) // S5_STATE
        grp = lax.broadcasted_iota(jnp.int32, (S5_LANES, 128), 1)
        dldt_ref[...] = _dot32(jnp.broadcast_to(ddt, (SUBLANES, S5_LANES)), (lane == grp).astype(F32))

    whole = pl.BlockSpec(memory_space=pltpu.VMEM)
    return pl.pallas_call(
        body, name="s5_prep_bwd", in_specs=[whole] * 8, out_specs=[whole] * 5,
        out_shape=[jax.ShapeDtypeStruct((1, S5_LANES), F32), jax.ShapeDtypeStruct((1, S5_LANES), F32),
                   jax.ShapeDtypeStruct((SUBLANES, 128), F32), jax.ShapeDtypeStruct((S5_GROUP, S5_LANES), F32),
                   jax.ShapeDtypeStruct((S5_GROUP, S5_LANES), F32)])(a_re, a_im, log_dt, b_re_t, b_im_t, dlam, dbbr,
                                                                      dbbi)


def _dgelu(x):
    c, a = 0.7978845608028654, 0.044715
    th = jnp.tanh(c * (x + a * x * x * x))
    return 0.5 * (1.0 + th) + 0.5 * x * (1.0 - th * th) * c * (1.0 + 3.0 * a * x * x)


S5_BLOCKS = 4
S5_BW = S5_WIDTH // S5_BLOCKS
S5_BL = S5_LANES // S5_BLOCKS
S5_LANE_BLOCKS = S5_LANES // 128
S5_SCAN_BLOCKS = 4


def _s5_prep(a_re, a_im, log_dt, b_re_t, b_im_t, seg):
    def body(ar_ref, ai_ref, ldt_ref, br_ref, bi_ref,
             rows_f, pfr_ref, pfi_ref, rows_r, prr_ref, pri_ref, bbr_ref, bbi_ref):
        ar, ai = ar_ref[...], ai_ref[...]
        dt = jnp.exp(ldt_ref[...])
        mag = jnp.exp(ar * dt)
        lr, li = mag * jnp.cos(ai * dt), mag * jnp.sin(ai * dt)
        den = ar * ar + ai * ai
        nr = lr - 1.0
        sr = (nr * ar + li * ai) / den
        si = (li * ar - nr * ai) / den
        wide = (SUBLANES, S5_LANES)
        cr, ci = lr, li
        for i in range(seg):
            pfr_ref[i] = jnp.broadcast_to(cr, wide)
            pfi_ref[i] = jnp.broadcast_to(ci, wide)
            prr_ref[seg - 1 - i] = jnp.broadcast_to(cr, wide)
            pri_ref[seg - 1 - i] = jnp.broadcast_to(-ci, wide)
            if i == seg - 1:
                for rows, sign in ((rows_f, 1.0), (rows_r, -1.0)):
                    rows[0:1, :] = lr
                    rows[1:2, :] = sign * li
                    rows[2:3, :] = cr
                    rows[3:4, :] = sign * ci
            cr, ci = cr * lr - ci * li, cr * li + ci * lr
        br, bi = br_ref[...], bi_ref[...]
        bbr_ref[...] = sr * br - si * bi
        bbi_ref[...] = sr * bi + si * br

    whole = pl.BlockSpec(memory_space=pltpu.VMEM)
    tables = [jax.ShapeDtypeStruct((4, S5_LANES), F32)] + [jax.ShapeDtypeStruct((seg, SUBLANES, S5_LANES), F32)] * 2
    bbar = [jax.ShapeDtypeStruct((S5_GROUP, S5_LANES), F32)] * 2
    res = pl.pallas_call(body, name="s5_prep", in_specs=[whole] * 5, out_specs=[whole] * 8,
                         out_shape=tables + tables + bbar)(a_re, a_im, log_dt, b_re_t, b_im_t)
    return res[0:3], res[3:6], res[6], res[7]


def _lanes(j):
    return pl.ds(j * 128, 128)


def _to_segment_order(v, stage_ref, out_ref, seg):
    nbl = v.shape[1] // 128
    for b in range(nbl):
        stage_ref[b] = v[:, b * 128:(b + 1) * 128]

    def body(t, carry):
        rows = pl.ds(pl.multiple_of(t * SUBLANES, SUBLANES), SUBLANES)
        for b in range(nbl):
            out_ref[rows, _lanes(b)] = stage_ref[b, pl.ds(t, SUBLANES, stride=seg), :]
        return carry

    lax.fori_loop(0, seg, body, 0, unroll=True)


def _from_segment_order(v, stage_ref, out_ref, seg):
    nbl = v.shape[1] // 128
    for b in range(nbl):
        stage_ref[b] = v[:, b * 128:(b + 1) * 128]
    for s in range(SUBLANES):
        def body(k, carry, s=s):
            rows = pl.ds(pl.multiple_of(s * seg + k * SUBLANES, SUBLANES), SUBLANES)
            for b in range(nbl):
                out_ref[rows, _lanes(b)] = stage_ref[b, pl.ds(k * SUBLANES * SUBLANES + s, SUBLANES,
                                                              stride=SUBLANES), :]
            return carry

        lax.fori_loop(0, seg // SUBLANES, body, 0, unroll=True)


def _tile_scan(xr_ref, xi_ref, lam_ref, car_ref, cai_ref, cn_r, cn_i, blocks, seg, reverse):
    shape = (SUBLANES, 128)
    lrs = [jnp.broadcast_to(lam_ref[0:1, _lanes(j)], shape) for j in blocks]
    lis = [jnp.broadcast_to(lam_ref[1:2, _lanes(j)], shape) for j in blocks]

    def step(k, carry):
        t = seg - 1 - k if reverse else k
        rows = pl.ds(pl.multiple_of(t * SUBLANES, SUBLANES), SUBLANES)
        out = []
        for n, j in enumerate(blocks):
            cr, ci = carry[2 * n], carry[2 * n + 1]
            nr = lrs[n] * cr - lis[n] * ci + xr_ref[rows, _lanes(j)]
            ni = lrs[n] * ci + lis[n] * cr + xi_ref[rows, _lanes(j)]
            xr_ref[rows, _lanes(j)] = nr
            xi_ref[rows, _lanes(j)] = ni
            out += [nr, ni]
        return tuple(out)

    zero = jnp.zeros(shape, F32)
    fin = lax.fori_loop(0, seg, step, (zero,) * (2 * len(blocks)), unroll=True)
    for n, j in enumerate(blocks):
        ls = _lanes(j)
        fr, fi = fin[2 * n], fin[2 * n + 1]
        sr, si = lam_ref[2:3, ls], lam_ref[3:4, ls]
        pr, pi = car_ref[:, ls], cai_ref[:, ls]
        for s in (reversed(range(SUBLANES)) if reverse else range(SUBLANES)):
            cn_r[s:s + 1, ls] = pr
            cn_i[s:s + 1, ls] = pi
            pr, pi = fr[s:s + 1, :] + sr * pr - si * pi, fi[s:s + 1, :] + sr * pi + si * pr
        car_ref[:, ls] = pr
        cai_ref[:, ls] = pi


def _s5_fwd(proj, lam_rows, p3_re, p3_im, bbr4, bbi4, crt4, cit4, d_row, t_len, tb):
    seg = tb // SUBLANES

    def body(u_ref, lam_ref, p3r_ref, p3i_ref, bbr_ref, bbi_ref, crt_ref, cit_ref, d_ref,
             hr_ref, hi_ref, ypre_ref, ys_ref, car_ref, cai_ref, cn_r, cn_i, stage_ref, us_ref, yseg_ref):
        @pl.when(pl.program_id(0) == 0)
        def _():
            car_ref[...] = jnp.zeros_like(car_ref)
            cai_ref[...] = jnp.zeros_like(cai_ref)

        _to_segment_order(u_ref[...], stage_ref, us_ref, seg)
        u = us_ref[...]
        for i in range(S5_BLOCKS):
            ui = u[:, i * S5_BW:(i + 1) * S5_BW]
            hr_ref[:, pl.ds(i * S5_BL, S5_BL)] = _dot(ui, bbr_ref[i])
            hi_ref[:, pl.ds(i * S5_BL, S5_BL)] = _dot(ui, bbi_ref[i])
        for lc in range(S5_LANE_BLOCKS // S5_SCAN_BLOCKS):
            blocks = range(lc * S5_SCAN_BLOCKS, (lc + 1) * S5_SCAN_BLOCKS)
            _tile_scan(hr_ref, hi_ref, lam_ref, car_ref, cai_ref, cn_r, cn_i, blocks, seg, False)
            crs = [cn_r[:, _lanes(j)] for j in blocks]
            cis = [cn_i[:, _lanes(j)] for j in blocks]

            def fix(t, carry, blocks=blocks, crs=crs, cis=cis):
                rows = pl.ds(pl.multiple_of(t * SUBLANES, SUBLANES), SUBLANES)
                for n, j in enumerate(blocks):
                    pr, pi = p3r_ref[t, :, _lanes(j)], p3i_ref[t, :, _lanes(j)]
                    hr_ref[rows, _lanes(j)] += pr * crs[n] - pi * cis[n]
                    hi_ref[rows, _lanes(j)] += pr * cis[n] + pi * crs[n]
                return carry

            lax.fori_loop(0, seg, fix, 0, unroll=True)
        for i in range(S5_BLOCKS):
            ws = pl.ds(i * S5_BW, S5_BW)
            bl = pl.ds(i * S5_BL, S5_BL)
            yseg_ref[:, ws] = (_dot(hr_ref[:, bl], crt_ref[i]) - _dot(hi_ref[:, bl], cit_ref[i])
                               + d_ref[:, ws] * u[:, i * S5_BW:(i + 1) * S5_BW])
        _from_segment_order(yseg_ref[...], stage_ref, ypre_ref, seg)
        ys_ref[...] = jax.nn.gelu(ypre_ref[...], approximate=True).astype(ys_ref.dtype)

    whole = pl.BlockSpec(memory_space=pltpu.VMEM)
    return pl.pallas_call(
        body, name="s5_fwd", grid=(t_len // tb,),
        in_specs=[pl.BlockSpec((tb, S5_WIDTH), lambda i: (i, 4096 // S5_WIDTH))] + [whole] * 8,
        out_specs=[pl.BlockSpec((tb, S5_LANES), lambda i: (i, 0)), pl.BlockSpec((tb, S5_LANES), lambda i: (i, 0)),
                   pl.BlockSpec((tb, S5_WIDTH), lambda i: (i, 0)), pl.BlockSpec((tb, S5_WIDTH), lambda i: (i, 0))],
        out_shape=[jax.ShapeDtypeStruct((t_len, S5_LANES), F32), jax.ShapeDtypeStruct((t_len, S5_LANES), F32),
                   jax.ShapeDtypeStruct((t_len, S5_WIDTH), F32), jax.ShapeDtypeStruct((t_len, S5_WIDTH), MXU_DTYPE)],
        scratch_shapes=[pltpu.VMEM((1, S5_LANES), F32), pltpu.VMEM((1, S5_LANES), F32),
                        pltpu.VMEM((SUBLANES, S5_LANES), F32), pltpu.VMEM((SUBLANES, S5_LANES), F32),
                        pltpu.VMEM((S5_WIDTH // 128, tb, 128), F32), pltpu.VMEM((tb, S5_WIDTH), F32),
                        pltpu.VMEM((tb, S5_WIDTH), F32)],
        compiler_params=_params("arbitrary"))(proj, lam_rows, p3_re, p3_im, bbr4, bbi4, crt4, cit4, d_row)


def _s5_bwd(dgelu, y_pre, proj, h_re, h_im, lam_rows, p3_re, p3_im, bbr4, bbi4, cr4, ci4, d_row, dproj, t_len, tb):
    seg = tb // SUBLANES
    nb = t_len // tb

    def body(dg_ref, yp_ref, u_ref, hr_ref, hi_ref, lam_ref, p3r_ref, p3i_ref, bbr_ref, bbi_ref, cr_ref, ci_ref,
             d_ref, _, du_ref, dbbr_ref, dbbi_ref, dcr_ref, dci_ref, dd_ref, dlam_ref,
             gr_ref, gi_ref, car_ref, cai_ref, cn_r, cn_i, stage_ref, us_ref, dys_ref, duseg_ref):
        @pl.when(pl.program_id(0) == 0)
        def _():
            for ref in (car_ref, cai_ref, dbbr_ref, dbbi_ref, dcr_ref, dci_ref, dd_ref, dlam_ref):
                ref[...] = jnp.zeros_like(ref)

        _to_segment_order(u_ref[...], stage_ref, us_ref, seg)
        _to_segment_order(dg_ref[...] * _dgelu(yp_ref[...]), stage_ref, dys_ref, seg)
        u, dy = us_ref[...], dys_ref[...]
        for i in range(S5_BLOCKS):
            dyi = dy[:, i * S5_BW:(i + 1) * S5_BW]
            gr_ref[:, pl.ds(i * S5_BL, S5_BL)] = _dot(dyi, cr_ref[i])
            gi_ref[:, pl.ds(i * S5_BL, S5_BL)] = -_dot(dyi, ci_ref[i])
        for lc in range(S5_LANE_BLOCKS // S5_SCAN_BLOCKS):
            blocks = range(lc * S5_SCAN_BLOCKS, (lc + 1) * S5_SCAN_BLOCKS)
            _tile_scan(gr_ref, gi_ref, lam_ref, car_ref, cai_ref, cn_r, cn_i, blocks, seg, True)
            crs = [cn_r[:, _lanes(j)] for j in blocks]
            cis = [cn_i[:, _lanes(j)] for j in blocks]

            def fix(k, carry, blocks=blocks, crs=crs, cis=cis):
                t = seg - 1 - k
                rows = pl.ds(pl.multiple_of(t * SUBLANES, SUBLANES), SUBLANES)
                out = []
                for n, j in enumerate(blocks):
                    nr, ni, slr, sli = carry[4 * n:4 * n + 4]
                    pr, pi = p3r_ref[t, :, _lanes(j)], p3i_ref[t, :, _lanes(j)]
                    g_r = gr_ref[rows, _lanes(j)] + pr * crs[n] - pi * cis[n]
                    g_i = gi_ref[rows, _lanes(j)] + pr * cis[n] + pi * crs[n]
                    gr_ref[rows, _lanes(j)] = g_r
                    gi_ref[rows, _lanes(j)] = g_i
                    hr, hi = hr_ref[rows, _lanes(j)], hi_ref[rows, _lanes(j)]
                    out += [g_r, g_i, slr + nr * hr + ni * hi, sli + ni * hr - nr * hi]
                return tuple(out)

            zero = jnp.zeros((SUBLANES, 128), F32)
            init = []
            for n in range(len(blocks)):
                init += [crs[n], cis[n], zero, zero]
            fin = lax.fori_loop(0, seg, fix, tuple(init), unroll=True)
            for n, j in enumerate(blocks):
                dlam_ref[0:1, _lanes(j)] += jnp.sum(fin[4 * n + 2], axis=0, keepdims=True)
                dlam_ref[1:2, _lanes(j)] += jnp.sum(fin[4 * n + 3], axis=0, keepdims=True)
        for i in range(S5_BLOCKS):
            ws = pl.ds(i * S5_BW, S5_BW)
            bl = pl.ds(i * S5_BL, S5_BL)
            ui, dyi = u[:, i * S5_BW:(i + 1) * S5_BW], dy[:, i * S5_BW:(i + 1) * S5_BW]
            gr, gi = gr_ref[:, bl], gi_ref[:, bl]
            duseg_ref[:, ws] = _dot(gr, bbr_ref[i], _NT) + _dot(gi, bbi_ref[i], _NT) + d_ref[:, ws] * dyi
            dbbr_ref[i] += _dot(ui, gr, _TN)
            dbbi_ref[i] += _dot(ui, gi, _TN)
            dcr_ref[i] += _dot(hr_ref[:, bl], dyi, _TN)
            dci_ref[i] -= _dot(hi_ref[:, bl], dyi, _TN)
        dd_ref[...] += jnp.sum(dy * u, axis=0, keepdims=True)
        _from_segment_order(duseg_ref[...], stage_ref, duseg_ref, seg)
        du_ref[...] = duseg_ref[...].astype(du_ref.dtype)

    whole = pl.BlockSpec(memory_space=pltpu.VMEM)
    rev = lambda i: (nb - 1 - i, 0)
    const3 = lambda i: (0, 0, 0)
    return pl.pallas_call(
        body, name="s5_bwd", grid=(nb,),
        in_specs=[pl.BlockSpec((tb, S5_WIDTH), rev), pl.BlockSpec((tb, S5_WIDTH), rev),
                  pl.BlockSpec((tb, S5_WIDTH), lambda i: (nb - 1 - i, 4096 // S5_WIDTH)),
                  pl.BlockSpec((tb, S5_LANES), rev), pl.BlockSpec((tb, S5_LANES), rev)] + [whole] * 8
                 + [pl.BlockSpec(memory_space=pl.ANY)],
        out_specs=[pl.BlockSpec((tb, S5_WIDTH), lambda i: (nb - 1 - i, 4096 // S5_WIDTH)),
                   pl.BlockSpec((S5_BLOCKS, S5_BW, S5_BL), const3), pl.BlockSpec((S5_BLOCKS, S5_BW, S5_BL), const3),
                   pl.BlockSpec((S5_BLOCKS, S5_BL, S5_BW), const3), pl.BlockSpec((S5_BLOCKS, S5_BL, S5_BW), const3),
                   pl.BlockSpec((1, S5_WIDTH), lambda i: (0, 0)), pl.BlockSpec((2, S5_LANES), lambda i: (0, 0))],
        out_shape=[jax.ShapeDtypeStruct((t_len, IN_COLS), dproj.dtype),
                   jax.ShapeDtypeStruct((S5_BLOCKS, S5_BW, S5_BL), F32),
                   jax.ShapeDtypeStruct((S5_BLOCKS, S5_BW, S5_BL), F32),
                   jax.ShapeDtypeStruct((S5_BLOCKS, S5_BL, S5_BW), F32),
                   jax.ShapeDtypeStruct((S5_BLOCKS, S5_BL, S5_BW), F32),
                   jax.ShapeDtypeStruct((1, S5_WIDTH), F32), jax.ShapeDtypeStruct((2, S5_LANES), F32)],
        scratch_shapes=[pltpu.VMEM((tb, S5_LANES), F32), pltpu.VMEM((tb, S5_LANES), F32),
                        pltpu.VMEM((1, S5_LANES), F32), pltpu.VMEM((1, S5_LANES), F32),
                        pltpu.VMEM((SUBLANES, S5_LANES), F32), pltpu.VMEM((SUBLANES, S5_LANES), F32),
                        pltpu.VMEM((S5_WIDTH // 128, tb, 128), F32), pltpu.VMEM((tb, S5_WIDTH), F32),
                        pltpu.VMEM((tb, S5_WIDTH), F32), pltpu.VMEM((tb, S5_WIDTH), F32)],
        input_output_aliases={13: 0},
        compiler_params=_params("arbitrary"))(dgelu, y_pre, proj, h_re, h_im, lam_rows, p3_re, p3_im, bbr4, bbi4,
                                              cr4, ci4, d_row, dproj)


def _block_diag(per_group):
    g8 = S5_GROUPS // S5_BLOCKS
    eye = jnp.eye(g8, dtype=bool)[None, :, None, :, None]
    dense = jnp.where(eye, per_group.reshape(S5_BLOCKS, g8, S5_GROUP, 1, S5_STATE), 0.0)
    return dense.reshape(S5_BLOCKS, S5_BW, S5_BL)


def _diag_blocks(dense):
    g8 = S5_GROUPS // S5_BLOCKS
    ar = jnp.arange(g8)
    d5 = dense.reshape(S5_BLOCKS, g8, S5_GROUP, g8, S5_STATE)
    return d5[:, ar, :, ar, :].transpose(1, 0, 2, 3).reshape(S5_GROUPS, S5_GROUP, S5_STATE)


def _hg_gate_bwd(da, o, g, gn):
    dos, dgs, dgns = [], [], []
    for h in range(HG_HEADS):
        sl = slice(h * HG_DIM, (h + 1) * HG_DIM)
        oh, gh, dah, gnh = o[:, sl], g[:, sl], da[:, sl], gn[:, sl]
        rr = lax.rsqrt(jnp.mean(oh * oh, axis=-1, keepdims=True) + NORM_EPS)
        sg = _sig(gh)
        dgs.append(dah * (oh * rr * gnh) * _dsilu(gh, sg))
        don = dah * (gh * sg)
        t = don * gnh
        dos.append(rr * t - oh * (rr * rr * rr) * jnp.mean(t * oh, axis=-1, keepdims=True))
        dgns.append(jnp.sum(don * oh * rr, axis=0, keepdims=True))
    return jnp.concatenate(dos, axis=1), jnp.concatenate(dgs, axis=1), jnp.concatenate(dgns, axis=1)


MIX_BWD_COLS = ((3072, 1024), (4608, 512), (5120, 1024), (6144, 1024))


def _mix_bwd(dgl, h1, dh2, act_hg, ys2, ys_gelu, proj, o_hg, g2, ghn, b_glu, w, t_len, tm):
    nb = t_len // tm

    def body(dgl_ref, h1_ref, dh2_ref, act_ref, ys2_ref, ysg_ref, ghg_ref, z_ref, gh_ref, gs_ref, o_ref, g2_ref, gn_ref,
             bglu_ref, wg_ref, wo_ref, ws5_ref, whg_ref, wglu_ref,
             dh1_ref, dyh_ref, dys_ref, dglu_ref, dgelu_ref, do_ref, dg2_ref, dbglu_ref, dgn_ref, dproj_ref,
             st0, st1, st2, st3, sems):
        i = pl.program_id(0)
        stages = (st0, st1, st2, st3)

        def writes(step):
            rows = pl.ds(pl.multiple_of(step * tm, tm), tm)
            return [pltpu.make_async_copy(st, dproj_ref.at[rows, pl.ds(c0, wd)], sems.at[k])
                    for k, (st, (c0, wd)) in enumerate(zip(stages, MIX_BWD_COLS))]

        @pl.when(i > 0)
        def _():
            for cp in writes(i - 1):
                cp.wait()

        @pl.when(i == 0)
        def _():
            for ref in (dg2_ref, dbglu_ref, dgn_ref):
                ref[...] = jnp.zeros_like(ref)

        dx, dg2 = _rms_bwd(_dot(dgl_ref[...], wg_ref[...], _NT), h1_ref[...], g2_ref[...])
        dh1 = dh2_ref[...] + dx
        dh1_ref[...] = dh1
        dg2_ref[...] += dg2
        dm = _dot(dh1, wo_ref[...], _NT)
        sh, ss = _sig(gh_ref[...]), _sig(gs_ref[...])
        dyh, dys = _mx(dm * sh), _mx(dm * ss)
        dyh_ref[...] = dyh
        dys_ref[...] = dys
        st2[...] = (dm * _dot(act_ref[...], whg_ref[...]) * sh * (1.0 - sh)).astype(st2.dtype)
        st3[...] = (dm * _dot(ys2_ref[...], ws5_ref[...]) * ss * (1.0 - ss)).astype(st3.dtype)
        dys2 = _dot(dys, ws5_ref[...], _NT)
        gl_, z = _dot(ysg_ref[...], wglu_ref[...]) + bglu_ref[...], z_ref[...]
        a, b = gl_[:, :S5_WIDTH], gl_[:, S5_WIDTH:]
        sb, sz = _sig(b), _sig(z)
        silu = z * sz
        dglu = jnp.concatenate([dys2 * sb * silu, dys2 * a * silu * sb * (1.0 - sb)], axis=1)
        st1[...] = (dys2 * a * sb * _dsilu(z, sz)).astype(st1.dtype)
        dbglu_ref[...] += jnp.sum(dglu, axis=0, keepdims=True)
        dglu_ref[...] = _mx(dglu)
        dgelu_ref[...] = _dot(dglu, wglu_ref[...], _NT)
        d_o, dg, dgn = _hg_gate_bwd(_dot(dyh, whg_ref[...], _NT), o_ref[...], ghg_ref[...], gn_ref[...])
        do_ref[...] = d_o.astype(do_ref.dtype)
        st0[...] = dg.astype(st0.dtype)
        dgn_ref[...] += dgn
        for cp in writes(i):
            cp.start()

        @pl.when(i == nb - 1)
        def _():
            for cp in writes(i):
                cp.wait()

    tile = lambda wd, cb=0: pl.BlockSpec((tm, wd), functools.partial(lambda i, cb: (i, cb), cb=cb))
    row = lambda wd: pl.BlockSpec((1, wd), lambda i: (0, 0))
    whole = pl.BlockSpec(memory_space=pltpu.VMEM)
    return pl.pallas_call(
        body, name="mix_bwd", grid=(nb,),
        in_specs=[tile(1024), tile(1024), tile(1024), tile(1024), tile(512), tile(512), tile(1024, 3),
                  tile(512, 4608 // 512), tile(1024, 5), tile(1024, 6), tile(1024), row(1024), row(1024), row(1024)]
                 + [whole] * 5,
        out_specs=[tile(1024), tile(1024), tile(1024), tile(1024), tile(512), tile(1024), row(1024), row(1024),
                   row(1024), _HBM],
        out_shape=[jax.ShapeDtypeStruct((t_len, 1024), F32), jax.ShapeDtypeStruct((t_len, 1024), MXU_DTYPE),
                   jax.ShapeDtypeStruct((t_len, 1024), MXU_DTYPE), jax.ShapeDtypeStruct((t_len, 1024), MXU_DTYPE),
                   jax.ShapeDtypeStruct((t_len, 512), F32), jax.ShapeDtypeStruct((t_len, 1024), MXU_DTYPE),
                   jax.ShapeDtypeStruct((1, 1024), F32), jax.ShapeDtypeStruct((1, 1024), F32),
                   jax.ShapeDtypeStruct((1, 1024), F32), jax.ShapeDtypeStruct((t_len, IN_COLS), MXU_DTYPE)],
        scratch_shapes=[pltpu.VMEM((tm, wd), MXU_DTYPE) for _, wd in MIX_BWD_COLS] + [pltpu.SemaphoreType.DMA((4,))],
        compiler_params=_params("arbitrary"))(dgl, h1, dh2, act_hg, ys2, ys_gelu, proj, proj, proj, proj, o_hg, g2, ghn,
                                              b_glu, w["w_ple_gate"], w["w_out"], w["w_o_s5"], w["w_o_hg"],
                                              w["w_glu"])


def _local_step(x, p, target, w, sm, comm=None):
    t_len = x.shape[0]
    tm = min(256, t_len)
    tmm = min(512, t_len)
    tb_hg = min(256, t_len)
    tb_s5 = min(256, t_len)
    g1, g2, g3, ghn = sm["norm_g"], sm["ple_norm_g"], sm["final_norm_g"].reshape(1, D_MODEL), sm["hg_norm_g"]

    def rms_in(xv, g):
        return xv * lax.rsqrt(jnp.mean(xv * xv, axis=-1, keepdims=True) + NORM_EPS) * g

    in_shard = IN_COLS // N_CHIPS
    w_in = w["w_in"]
    if comm is None:
        proj, u = _mm_nn("mm_in", x, w_in, tmm, in_shard, prologue=rms_in, consts=[g1])
    else:
        proj, u, landed = _mm_nn("mm_in", x, w_in, tmm, in_shard, riding=comm.gather_rest(), prologue=rms_in,
                                 consts=[g1])
        w = comm.rest_weights(landed)

    lanes = lambda a: a.reshape(1, S5_LANES)
    a_re, a_im = lanes(sm["s5_a_re"]), lanes(sm["s5_a_im"])
    ldt = lanes(jnp.broadcast_to(sm["s5_log_dt"].reshape(S5_GROUPS, 1), (S5_GROUPS, S5_STATE)))
    to_t = lambda b: b.reshape(S5_GROUPS, S5_STATE, S5_GROUP).transpose(2, 0, 1).reshape(S5_GROUP, S5_LANES)
    b_re_t, b_im_t = to_t(sm["s5_b_re"]), to_t(sm["s5_b_im"])
    scan_fwd, scan_rev, bbr_t, bbi_t = _s5_prep(a_re, a_im, ldt, b_re_t, b_im_t, tb_s5 // SUBLANES)
    from_t = lambda b: b.reshape(S5_GROUP, S5_GROUPS, S5_STATE).transpose(1, 0, 2)
    bbr_bd = _block_diag(from_t(bbr_t)).astype(MXU_DTYPE)
    bbi_bd = _block_diag(from_t(bbi_t)).astype(MXU_DTYPE)
    cr_bd = _block_diag(sm["s5_c_re"].reshape(S5_GROUPS, S5_GROUP, S5_STATE)).astype(MXU_DTYPE)
    ci_bd = _block_diag(sm["s5_c_im"].reshape(S5_GROUPS, S5_GROUP, S5_STATE)).astype(MXU_DTYPE)
    d_row = sm["s5_d"].reshape(1, S5_WIDTH)
    o_hg, act_hg, s_prev = _hgrn2_fwd(proj, sm["hg_lb"], ghn, t_len, tb_hg)
    h_re, h_im, y_pre, ys_gelu = _s5_fwd(proj, *scan_fwd, bbr_bd, bbi_bd,
                                          cr_bd.transpose(0, 2, 1), ci_bd.transpose(0, 2, 1), d_row, t_len, tb_s5)
    def mix_f(act, ysg, z, gh, gs, xv, w_glu, b_glu, w_o_hg, w_o_s5, w_out):
        gl_ = _dot(ysg, w_glu) + b_glu
        a, b = gl_[:, :S5_WIDTH], gl_[:, S5_WIDTH:]
        ys2_ = (a * _sig(b) * (z * _sig(z))).astype(MXU_DTYPE)
        yh, ys = _dot(act, w_o_hg), _dot(ys2_, w_o_s5)
        mg = (_sig(gh) * yh + _sig(gs) * ys).astype(MXU_DTYPE)
        return (ys2_, mg, xv + _dot(mg, w_out))

    ys2, merged, h1 = _rowwise(
        "mix_out", mix_f, t_len, tm,
        [(act_hg, 1024, 0), (ys_gelu, 512, 0), (proj, 512, 4608 // 512), (proj, 1024, 5), (proj, 1024, 6),
         (x, 1024, 0)], [w["w_glu"], sm["b_glu"], w["w_o_hg"], w["w_o_s5"], w["w_out"]],
        [(512, MXU_DTYPE), (1024, MXU_DTYPE), (1024, F32)])

    def head_f(h1v, pv, tgt, g_ple, g, w_ple, w_gate):
        r2 = lax.rsqrt(jnp.mean(h1v * h1v, axis=-1, keepdims=True) + NORM_EPS)
        n2_ = (h1v * r2 * g_ple).astype(MXU_DTYPE)
        glv, pev = _dot(n2_, w_gate), _dot(pv, w_ple)
        gate = _sig(glv)
        h2 = h1v + pev * gate
        r = lax.rsqrt(jnp.mean(h2 * h2, axis=-1, keepdims=True) + NORM_EPS)
        e = h2 * r * g - tgt
        loss = 0.5 * jnp.sum(jnp.mean(e * e, axis=-1, keepdims=True), axis=0, keepdims=True)
        dy = e * (1.0 / D_MODEL)
        dg = jnp.sum(dy * h2 * r, axis=0, keepdims=True)
        t = dy * g
        dh2 = r * t - h2 * (r * r * r) * jnp.mean(t * h2, axis=-1, keepdims=True)
        return (n2_, dh2, dh2 * gate, dh2 * pev * gate * (1.0 - gate), jnp.broadcast_to(loss, (1, 128)), dg)

    n2, dh2, dpe, dgl, loss_row, d_g3 = _rowwise(
        "ple_loss_head", head_f, t_len, tm, [(h1, 1024, 0), (p, 256, 0), (target, 1024, 0)],
        [g2, g3, w["w_ple"], w["w_ple_gate"]],
        [(1024, MXU_DTYPE), (1024, F32), (1024, MXU_DTYPE), (1024, MXU_DTYPE)], accs=[(1, 128), (1, 1024)])

    gb = {}
    gb["w_ple"] = _mm_tn("mm_d_w_ple", p, dpe, tmm, 1024)
    gb["w_ple_gate"] = _mm_tn("mm_d_w_ple_gate", n2, dgl, tmm, 1024)
    dh1, dy_hg, dy_s5, dglu, dgelu, d_o, d_g2, d_bglu, d_ghn, dproj = _mix_bwd(
        dgl, h1, dh2, act_hg, ys2, ys_gelu, proj, o_hg, g2, ghn, sm["b_glu"], w, t_len, tm)
    gb["w_out"] = _mm_tn("mm_d_w_out", merged, dh1, tmm, 1024)
    gb["w_o_s5"] = _mm_tn("mm_d_w_o_s5", ys2, dy_s5, tmm, 1024)
    gb["w_glu"] = _mm_tn("mm_d_w_glu", ys_gelu, dglu, tmm, 1024)
    dproj, d_bbr, d_bbi, d_crt, d_cit, d_d, d_lam = _s5_bwd(dgelu, y_pre, proj, h_re, h_im,
                                                            *scan_rev, bbr_bd, bbi_bd, cr_bd,
                                                            ci_bd, d_row, dproj, t_len, tb_s5)
    to_t3 = lambda b: b.transpose(1, 0, 2).reshape(S5_GROUP, S5_LANES)
    d_are, d_aim, d_ldt, d_br_t, d_bi_t = _s5_prep_bwd(a_re, a_im, ldt, b_re_t, b_im_t, d_lam,
                                                       to_t3(_diag_blocks(d_bbr)), to_t3(_diag_blocks(d_bbi)))
    gb["w_o_hg"] = _mm_tn("mm_d_w_o_hg", act_hg, dy_hg, tmm, 1024)
    if comm is None:
        dproj, d_lb = _hgrn2_bwd(proj, d_o, s_prev, sm["hg_lb"], dproj, t_len, tb_hg)
    else:
        rest_grads = _pack_rest_full(gb)
        dproj, d_lb, rest_theirs = _hgrn2_bwd(proj, d_o, s_prev, sm["hg_lb"], dproj, t_len, tb_hg,
                                               riding=comm.swap(rest_grads))

    def in_b(duv, xv, dh, g):
        dx, dg = _rms_bwd(duv, xv, g)
        return (dh + dx, dg)

    in_args = ("mm_d_u_rms_in_bwd", dproj, w_in, tmm, in_shard, in_b, [(x, 1024, 0), (dh1, 1024, 0)], [g1],
               [(1024, F32)])
    if comm is None:
        gb["w_in"] = _mm_tn("mm_d_w_in", u, dproj, tmm, in_shard, col_shards=True)
        grad_x, d_g1 = _mm_nt_then(*in_args, accs=[(1, 1024)])
    else:
        gb["w_in"], landed = _mm_tn("mm_d_w_in", u, dproj, tmm, in_shard, col_shards=True,
                                    riding=comm.scatter("rest", rest_grads, rest_theirs))
        comm.landed["rest"] = landed
        grad_x, d_g1, landed = _mm_nt_then(*in_args, accs=[(1, 1024)], riding=comm.scatter(
            "in", gb["w_in"].reshape(N_CHIPS, 2, D_MODEL // 2, in_shard)))
        comm.landed["in"] = landed

    back_t = lambda b: b.reshape(S5_GROUP, S5_GROUPS, S5_STATE).transpose(1, 2, 0).reshape(1, S5_GROUPS, S5_STATE,
                                                                                           S5_GROUP)
    gs = {
        "norm_g": d_g1, "hg_lb": d_lb, "hg_norm_g": d_ghn,
        "s5_a_re": d_are.reshape(1, S5_GROUPS, S5_STATE), "s5_a_im": d_aim.reshape(1, S5_GROUPS, S5_STATE),
        "s5_log_dt": d_ldt[0:1, :S5_GROUPS],
        "s5_b_re": back_t(d_br_t), "s5_b_im": back_t(d_bi_t),
        "s5_c_re": _diag_blocks(d_crt.transpose(0, 2, 1)).reshape(1, S5_GROUPS, S5_GROUP, S5_STATE),
        "s5_c_im": _diag_blocks(d_cit.transpose(0, 2, 1)).reshape(1, S5_GROUPS, S5_GROUP, S5_STATE),
        "s5_d": d_d.reshape(1, S5_GROUPS, S5_GROUP), "b_glu": d_bglu, "ple_norm_g": d_g2,
        "final_norm_g": d_g3.reshape(D_MODEL),
    }
    return loss_row, grad_x, gb, gs


def _shard_shape(name):
    r, c = BIG_SHAPE[name]
    return (r, c // N_CHIPS) if name in BIG_COL_SHARDED else (r // N_CHIPS, c)


def _pack_small(parts, last):
    flat = jnp.concatenate([parts[n].reshape(-1) for n in SMALL] + [last.reshape(-1)])
    return jnp.pad(flat, (0, SMALL_ROWS * PACK_W - flat.shape[0])).reshape(SMALL_ROWS, PACK_W)


def _unpack_small(packed):
    flat, out, off = packed.reshape(-1), {}, 0
    for n in SMALL:
        size = 1
        for d in SMALL_SHAPE[n]:
            size *= d
        out[n] = flat[off:off + size].reshape(SMALL_SHAPE[n])
        off += size
    return out, flat[off]


def _place():
    x, y, c = lax.axis_index("x"), lax.axis_index("y"), lax.axis_index("c")
    return x, y, c, [(1 - x, y), (x, 1 - y), (1 - x, 1 - y)]


def _remote(src, dst, send_sems, recv_sems, k, to):
    return pltpu.make_async_remote_copy(src_ref=src, dst_ref=dst, send_sem=send_sems.at[k], recv_sem=recv_sems.at[k],
                                        device_id=to, device_id_type=MESH)


REST = tuple(n for n in BIG if n != "w_in")
REST_ROWS = sum(BIG_SHAPE[n][0] * BIG_SHAPE[n][1] for n in REST) // (N_CHIPS * PACK_W)
IN_SHARD = IN_COLS // N_CHIPS
IN_TILE, REST_TILE = 256, 272


def _pack_rest(parts):
    return jnp.concatenate([parts[n].reshape(-1, PACK_W) for n in REST], axis=0)


def _unpack_rest(packed):
    out, off = {}, 0
    for n in REST:
        r, c = _shard_shape(n)
        rows = r * c // PACK_W
        out[n] = packed[off:off + rows].reshape(1, r, c)
        off += rows
    return out


def _unpack_rest_full(gathered):
    out, off = {}, 0
    for n in REST:
        r, c = _shard_shape(n)
        rows = r * c // PACK_W
        sh = gathered[:, off:off + rows].reshape(N_CHIPS, r, c)
        out[n] = sh.transpose(1, 0, 2).reshape(BIG_SHAPE[n]) if n in BIG_COL_SHARDED else sh.reshape(BIG_SHAPE[n])
        off += rows
    return out


def _pack_rest_full(full):
    parts = []
    for n in REST:
        r, c = _shard_shape(n)
        g = full[n]
        sh = g.reshape(BIG_SHAPE[n][0], N_CHIPS, c).transpose(1, 0, 2) if n in BIG_COL_SHARDED else g
        parts.append(sh.reshape(N_CHIPS, r * c // PACK_W, PACK_W))
    return jnp.concatenate(parts, axis=1).reshape(N_CHIPS, 2, REST_ROWS // 2, PACK_W)


def _gather_shards(ws):
    n = len(ws)

    def body(*refs):
        w_refs, out_refs, (send_sems, recv_sems) = refs[:n], refs[n:2 * n], refs[2 * n:]
        x, y, c, chips = _place()
        k = 2 * x + y
        sibling = (x, y, 1 - c)
        first = [_remote(w_ref.at[c], out_ref.at[k, c], send_sems, recv_sems, 6 * g + j, (cx, cy, c))
                 for j, (cx, cy) in enumerate(chips) for g, (w_ref, out_ref) in enumerate(zip(w_refs, out_refs))]
        for cp in first:
            cp.start()
        passed = []
        for j, (cx, cy) in enumerate(chips):
            kj = 2 * cx + cy
            for g, (w_ref, out_ref) in enumerate(zip(w_refs, out_refs)):
                _remote(w_ref.at[c], out_ref.at[kj, c], send_sems, recv_sems, 6 * g + j, (cx, cy, c)).wait_recv()
                cp = _remote(out_ref.at[kj, c], out_ref.at[kj, c], send_sems, recv_sems, 6 * g + 3 + j, sibling)
                cp.start()
                passed.append(cp)
        for j, (cx, cy) in enumerate(chips):
            kj = 2 * cx + cy
            for g, (w_ref, out_ref) in enumerate(zip(w_refs, out_refs)):
                _remote(w_ref.at[c], out_ref.at[kj, 1 - c], send_sems, recv_sems, 6 * g + 3 + j, sibling).wait_recv()
        for cp in first + passed:
            cp.wait_send()

    return pl.pallas_call(
        body, name="all_gather_weights", in_specs=[_HBM] * n, out_specs=[_HBM] * n,
        out_shape=[jax.ShapeDtypeStruct((N_CHIPS,) + w.shape, w.dtype) for w in ws],
        scratch_shapes=[pltpu.SemaphoreType.DMA((6 * n,)), pltpu.SemaphoreType.DMA((6 * n,))])(*ws)


def _swap_halves(pgs, name="exchange_halves"):
    n = len(pgs)

    def body(*refs):
        pg_refs, out_refs, (send_sems, recv_sems) = refs[:n], refs[n:2 * n], refs[2 * n:]
        x, y, c, _ = _place()
        cps = [_remote(pg_ref.at[j, 1 - c], out_ref.at[j], send_sems, recv_sems, N_CHIPS * g + j, (x, y, 1 - c))
               for g, (pg_ref, out_ref) in enumerate(zip(pg_refs, out_refs)) for j in range(N_CHIPS)]
        for cp in cps:
            cp.start()
        for cp in cps:
            cp.wait()

    return pl.pallas_call(
        body, name=name, in_specs=[_HBM] * n, out_specs=[_HBM] * n,
        out_shape=[jax.ShapeDtypeStruct((N_CHIPS,) + pg.shape[2:], pg.dtype) for pg in pgs],
        scratch_shapes=[pltpu.SemaphoreType.DMA((N_CHIPS * n,)), pltpu.SemaphoreType.DMA((N_CHIPS * n,))])(*pgs)


def _share_halves(gs):
    n = len(gs)

    def body(*refs):
        g_refs, out_refs, (send_sems, recv_sems) = refs[:n], refs[n:2 * n], refs[2 * n:]
        x, y, c, _ = _place()
        cps = [_remote(g_ref, out_ref.at[c], send_sems, recv_sems, g, (x, y, 1 - c))
               for g, (g_ref, out_ref) in enumerate(zip(g_refs, out_refs))]
        for cp in cps:
            cp.start()
        for g, (g_ref, out_ref) in enumerate(zip(g_refs, out_refs)):
            _remote(g_ref, out_ref.at[1 - c], send_sems, recv_sems, g, (x, y, 1 - c)).wait_recv()
        for cp in cps:
            cp.wait_send()

    return pl.pallas_call(
        body, name="share_half", in_specs=[_HBM] * n, out_specs=[_HBM] * n,
        out_shape=[jax.ShapeDtypeStruct((2,) + g.shape, g.dtype) for g in gs],
        scratch_shapes=[pltpu.SemaphoreType.DMA((n,)), pltpu.SemaphoreType.DMA((n,))])(*gs)


def _pair_sum(name, pg, theirs, c, tile):
    _, _, rows, width = pg.shape

    def body(c_ref, a_ref, b_ref, o_ref):
        o_ref[...] = (a_ref[...] + b_ref[...]).astype(o_ref.dtype)

    return pl.pallas_call(
        body, name=name,
        grid_spec=pltpu.PrefetchScalarGridSpec(
            num_scalar_prefetch=1, grid=(N_CHIPS, rows // tile),
            in_specs=[pl.BlockSpec((None, None, tile, width), lambda j, i, c_ref: (j, c_ref[0], i, 0)),
                      pl.BlockSpec((None, tile, width), lambda j, i, c_ref: (j, i, 0))],
            out_specs=pl.BlockSpec((None, tile, width), lambda j, i, c_ref: (j, i, 0))),
        out_shape=jax.ShapeDtypeStruct((N_CHIPS, rows, width), WIRE_DTYPE),
        compiler_params=_params("arbitrary", "arbitrary"))(c.reshape(1), pg, theirs)


def _chip_sum(name, ps, others, k, tile):
    _, rows, width = ps.shape

    def body(k_ref, a_ref, b_ref, o_ref):
        o_ref[...] = ((a_ref[...].astype(F32) + b_ref[0].astype(F32)) + b_ref[1].astype(F32)) + b_ref[2].astype(F32)

    return pl.pallas_call(
        body, name=name,
        grid_spec=pltpu.PrefetchScalarGridSpec(
            num_scalar_prefetch=1, grid=(rows // tile,),
            in_specs=[pl.BlockSpec((None, tile, width), lambda i, k_ref: (k_ref[0], i, 0)),
                      pl.BlockSpec((3, tile, width), lambda i, k_ref: (0, i, 0))],
            out_specs=pl.BlockSpec((tile, width), lambda i, k_ref: (i, 0))),
        out_shape=jax.ShapeDtypeStruct((rows, width), F32),
        compiler_params=_params("arbitrary"))(k.reshape(1), ps, others)


class _StepComm:
    TILES = {"in": IN_TILE, "rest": REST_TILE}

    def __init__(self, rest_wire, chip, core):
        self.rest_wire, self.chip, self.core = rest_wire, chip, core
        self.sums, self.landed = {}, {}

    def gather_rest(self):
        wire = self.rest_wire

        def sends(ins, outs, send_sems, recv_sems):
            (w_ref,), (out_ref,) = ins, outs
            x, y, c, chips = _place()
            return [_remote(w_ref.at[c], out_ref.at[2 * x + y, c], send_sems, recv_sems, 4 * j + 2 * c + to,
                            (cx, cy, to)) for j, (cx, cy) in enumerate(chips) for to in (0, 1)]

        def recvs(ins, outs, send_sems, recv_sems):
            (w_ref,), (out_ref,) = ins, outs
            _, _, c, chips = _place()
            return [_remote(w_ref.at[c], out_ref.at[2 * cx + cy, by], send_sems, recv_sems, 4 * j + 2 * by + c,
                            (cx, cy, by)) for j, (cx, cy) in enumerate(chips) for by in (0, 1)]

        def start(*refs):
            for cp in sends(*refs):
                cp.start()

        def wait(*refs):
            for cp in recvs(*refs):
                cp.wait_recv()
            for cp in sends(*refs):
                cp.wait_send()

        return _Riding((wire,), (jax.ShapeDtypeStruct((N_CHIPS,) + wire.shape, wire.dtype),), 12, start, wait)

    def rest_weights(self, landed):
        full = lax.dynamic_update_slice(landed, self.rest_wire[None], (self.chip, 0, 0, 0))
        return _unpack_rest_full(full.reshape(N_CHIPS, REST_ROWS, PACK_W))

    def swap(self, pg):
        def copies(ins, outs, send_sems, recv_sems):
            (pg_ref,), (out_ref,) = ins, outs
            x, y, c, _ = _place()
            return [_remote(pg_ref.at[j, 1 - c], out_ref.at[j], send_sems, recv_sems, j, (x, y, 1 - c))
                    for j in range(N_CHIPS)]

        def start(*refs):
            for cp in copies(*refs):
                cp.start()

        def wait(*refs):
            for cp in copies(*refs):
                cp.wait()

        return _Riding((pg,), (jax.ShapeDtypeStruct((N_CHIPS,) + pg.shape[2:], pg.dtype),), N_CHIPS, start, wait)

    def scatter(self, group, pg, theirs=None):
        if theirs is None:
            (theirs,) = _swap_halves([pg], "exchange_halves_" + group)
        ps = _pair_sum("sum_pair_" + group, pg, theirs, self.core, self.TILES[group])
        self.sums[group] = ps

        def copies(ins, outs, send_sems, recv_sems):
            (ps_ref,), (out_ref,) = ins, outs
            _, _, c, chips = _place()
            return [_remote(ps_ref.at[2 * cx + cy], out_ref.at[j], send_sems, recv_sems, j, (cx, cy, c))
                    for j, (cx, cy) in enumerate(chips)]

        def start(*refs):
            for cp in copies(*refs):
                cp.start()

        def wait(*refs):
            for cp in copies(*refs):
                cp.wait()

        return _Riding((ps,), (jax.ShapeDtypeStruct((3,) + ps.shape[1:], ps.dtype),), 3, start, wait)

    def reduced(self, group):
        return _chip_sum("sum_chips_" + group, self.sums[group], self.landed[group], self.chip, self.TILES[group])


def _adamw(w, g, m, v):
    m = ADAM_B1 * m + (1.0 - ADAM_B1) * g
    v = ADAM_B2 * v + (1.0 - ADAM_B2) * (g * g)
    m_hat = m / (1.0 - ADAM_B1 ** ADAM_STEP)
    v_hat = v / (1.0 - ADAM_B2 ** ADAM_STEP)
    return -ADAM_LR * (m_hat / (jnp.sqrt(v_hat) + ADAM_EPS) + ADAM_WD * w), m, v


def _small_reduce_adamw(part, w, m, v):
    def body(part_ref, w_ref, m_ref, v_ref, g_ref, d_ref, nm_ref, nv_ref, all_ref, send_sems, recv_sems):
        x, y, c, chips = _place()
        me, sibling = (x, y, c), (x, y, 1 - c)

        def rows(px, py, pc):
            return all_ref.at[4 * px + 2 * py + pc]

        all_ref[4 * x + 2 * y + c] = part_ref[...]
        first = [_remote(part_ref, rows(*me), send_sems, recv_sems, 0, sibling)]
        first += [_remote(part_ref, rows(*me), send_sems, recv_sems, 1 + j, (cx, cy, c))
                  for j, (cx, cy) in enumerate(chips)]
        for cp in first:
            cp.start()
        passed = []
        for j, (cx, cy) in enumerate(chips):
            _remote(part_ref, rows(cx, cy, c), send_sems, recv_sems, 1 + j, me).wait_recv()
            cp = _remote(rows(cx, cy, c), rows(cx, cy, c), send_sems, recv_sems, 4 + j, sibling)
            cp.start()
            passed.append(cp)
        _remote(part_ref, rows(*sibling), send_sems, recv_sems, 0, me).wait_recv()
        for j, (cx, cy) in enumerate(chips):
            _remote(part_ref, rows(cx, cy, 1 - c), send_sems, recv_sems, 4 + j, me).wait_recv()
        for cp in first + passed:
            cp.wait_send()
        g = all_ref[0]
        for dev in range(1, N_DEV):
            g = g + all_ref[dev]
        delta, nm, nv = _adamw(w_ref[...], g, m_ref[...], v_ref[...])
        g_ref[...] = g
        d_ref[...] = delta
        nm_ref[...] = nm
        nv_ref[...] = nv

    whole = pl.BlockSpec(memory_space=pltpu.VMEM)
    shape = jax.ShapeDtypeStruct((SMALL_ROWS, PACK_W), F32)
    return pl.pallas_call(
        body, name="small_reduce_adamw", in_specs=[whole] * 4, out_specs=[whole] * 4, out_shape=[shape] * 4,
        scratch_shapes=[pltpu.VMEM((N_DEV, SMALL_ROWS, PACK_W), F32), pltpu.SemaphoreType.DMA((7,)),
                        pltpu.SemaphoreType.DMA((7,))],
        compiler_params=pltpu.CompilerParams(vmem_limit_bytes=VMEM_LIMIT))(part, w, m, v)


def kernel(x, p, norm_g, w_in, hg_lb, hg_norm_g, w_o_hg, s5_a_re, s5_a_im, s5_log_dt, s5_b_re, s5_b_im, s5_c_re, s5_c_im, s5_d, w_glu, b_glu, w_o_s5, w_out, ple_norm_g, w_ple, w_ple_gate, final_norm_g, loss_target, m_norm_g, m_w_in, m_hg_lb, m_hg_norm_g, m_w_o_hg, m_s5_a_re, m_s5_a_im, m_s5_log_dt, m_s5_b_re, m_s5_b_im, m_s5_c_re, m_s5_c_im, m_s5_d, m_w_glu, m_b_glu, m_w_o_s5, m_w_out, m_ple_norm_g, m_w_ple, m_w_ple_gate, m_final_norm_g, v_norm_g, v_w_in, v_hg_lb, v_hg_norm_g, v_w_o_hg, v_s5_a_re, v_s5_a_im, v_s5_log_dt, v_s5_b_re, v_s5_b_im, v_s5_c_re, v_s5_c_im, v_s5_d, v_w_glu, v_b_glu, v_w_o_s5, v_w_out, v_ple_norm_g, v_w_ple, v_w_ple_gate, v_final_norm_g):
    given = dict(locals())
    wts = {n: given[n] for n in WEIGHTS}
    mom = {n: given["m_" + n] for n in WEIGHTS}
    var = {n: given["v_" + n] for n in WEIGHTS}
    cx, cy, cc = lax.axis_index("x"), lax.axis_index("y"), lax.axis_index("c")
    chip = (2 * cx + cy).astype(jnp.int32)

    core = cc.astype(jnp.int32)
    rest_shard = _pack_rest({n: wts[n][0] for n in REST})
    in_wire = wts["w_in"][0].astype(MXU_DTYPE).reshape(2, D_MODEL // 2, IN_SHARD)
    (w_in_all,) = _gather_shards([in_wire])
    w_in_all = lax.dynamic_update_slice(w_in_all, in_wire[None], (chip, 0, 0, 0)).reshape(N_CHIPS, D_MODEL, IN_SHARD)
    comm = _StepComm(rest_shard.astype(MXU_DTYPE).reshape(2, REST_ROWS // 2, PACK_W), chip, core)

    t_len = x.shape[1]
    loss_row, grad_x, g_big, g_small = _local_step(x.reshape(t_len, D_MODEL), p.reshape(t_len, -1),
                                                   loss_target.reshape(t_len, D_MODEL), {"w_in": w_in_all},
                                                   {n: wts[n] for n in SMALL}, comm)

    zero = jnp.zeros((), F32)
    sg, sd, snm, snv = _small_reduce_adamw(_pack_small(g_small, loss_row[0, 0]),
                                           _pack_small({n: wts[n] for n in SMALL}, zero),
                                           _pack_small({n: mom[n] for n in SMALL}, zero),
                                           _pack_small({n: var[n] for n in SMALL}, zero))
    (sg, loss), (sd, _), (snm, _), (snv, _) = (_unpack_small(a) for a in (sg, sd, snm, snv))

    halves = [comm.reduced("in"), comm.reduced("rest")]
    g_in, g_rest = [lax.dynamic_update_slice(got, mine[None], (core, 0, 0))
                    for got, mine in zip(_share_halves(halves), halves)]
    g_in, g_rest = g_in.reshape(D_MODEL, IN_SHARD), g_rest.reshape(REST_ROWS, PACK_W)

    def adam_f(wv, gv, mv, vv):
        return _adamw(wv, gv, mv, vv)

    d_in, nm_in, nv_in = _rowwise("adamw_in", adam_f, D_MODEL, IN_TILE,
                                  [(wts["w_in"][0], IN_SHARD, 0), (g_in, IN_SHARD, 0), (mom["w_in"][0], IN_SHARD, 0),
                                   (var["w_in"][0], IN_SHARD, 0)], [], [(IN_SHARD, F32)] * 3)
    d_rest, nm_rest, nv_rest = _rowwise("adamw_rest", adam_f, REST_ROWS, REST_TILE,
                                        [(rest_shard, PACK_W, 0), (g_rest, PACK_W, 0),
                                         (_pack_rest({n: mom[n][0] for n in REST}), PACK_W, 0),
                                         (_pack_rest({n: var[n][0] for n in REST}), PACK_W, 0)], [],
                                        [(PACK_W, F32)] * 3)
    bg, bd, bnm, bnv = (dict(_unpack_rest(rest), w_in=a.reshape(1, D_MODEL, IN_SHARD))
                        for rest, a in ((g_rest, g_in), (d_rest, d_in), (nm_rest, nm_in), (nv_rest, nv_in)))

    outs = [loss, grad_x.reshape(x.shape)]
    for small, big in ((sg, bg), (sd, bd), (snm, bnm), (snv, bnv)):
        outs += [big[n] if n in BIG else small[n] for n in WEIGHTS]
    return tuple(outs)
```

```python
import functools
from typing import Callable, NamedTuple

import jax
import jax.numpy as jnp
from jax import lax
from jax.experimental import pallas as pl
from jax.experimental.pallas import tpu as pltpu

F32 = jnp.float32
MXU_DTYPE = jnp.bfloat16
WIRE_DTYPE = jnp.bfloat16
NORM_EPS = 1e-6
D_MODEL = 1024
HG_HEADS = 8
HG_DIM = 128
HG_CHUNK = 64
S5_WIDTH = 512
S5_GROUPS = 32
S5_GROUP = 16
S5_STATE = 64
S5_LANES = S5_GROUPS * S5_STATE
IN_COLS = 7168
SUBLANES = 8
VMEM_LIMIT = 56 * 1024 * 1024
HIGHEST = lax.Precision.HIGHEST
MESH = pl.DeviceIdType.MESH

ADAM_LR, ADAM_B1, ADAM_B2, ADAM_EPS, ADAM_WD, ADAM_STEP = 0.001, 0.9, 0.999, 1e-08, 0.01, 10

BIG = ("w_in", "w_o_hg", "w_glu", "w_o_s5", "w_out", "w_ple", "w_ple_gate")
BIG_SHAPE = {"w_in": (1024, 7168), "w_o_hg": (1024, 1024), "w_glu": (512, 1024), "w_o_s5": (512, 1024),
             "w_out": (1024, 1024), "w_ple": (256, 1024), "w_ple_gate": (1024, 1024)}
BIG_COL_SHARDED = ("w_in", "w_glu", "w_o_s5", "w_ple")
SMALL = ("norm_g", "hg_lb", "hg_norm_g", "s5_a_re", "s5_a_im", "s5_log_dt", "s5_b_re", "s5_b_im", "s5_c_re",
         "s5_c_im", "s5_d", "b_glu", "ple_norm_g", "final_norm_g")
SMALL_SHAPE = {"norm_g": (1, 1024), "hg_lb": (2, 1024), "hg_norm_g": (1, 1024), "s5_a_re": (1, 32, 64),
               "s5_a_im": (1, 32, 64), "s5_log_dt": (1, 32), "s5_b_re": (1, 32, 64, 16), "s5_b_im": (1, 32, 64, 16),
               "s5_c_re": (1, 32, 16, 64), "s5_c_im": (1, 32, 16, 64), "s5_d": (1, 32, 16), "b_glu": (1, 1024),
               "ple_norm_g": (1, 1024), "final_norm_g": (1024,)}
WEIGHTS = ("norm_g", "w_in", "hg_lb", "hg_norm_g", "w_o_hg", "s5_a_re", "s5_a_im", "s5_log_dt", "s5_b_re", "s5_b_im",
           "s5_c_re", "s5_c_im", "s5_d", "w_glu", "b_glu", "w_o_s5", "w_out", "ple_norm_g", "w_ple", "w_ple_gate",
           "final_norm_g")
N_CHIPS = 4
N_DEV = 8
PACK_W = 1024
SMALL_ROWS = 144


def _params(*sem):
    return pltpu.CompilerParams(dimension_semantics=sem, vmem_limit_bytes=VMEM_LIMIT)


def _sig(x):
    return 1.0 / (1.0 + jnp.exp(-x))


def _dsilu(z, s):
    return s * (1.0 + z * (1.0 - s))


def _mx(x):
    return x.astype(MXU_DTYPE)


def _dot(a, b, dims=(((1,), (0,)), ((), ()))):
    return lax.dot_general(_mx(a), _mx(b), dims, preferred_element_type=F32)


_NT = (((1,), (1,)), ((), ()))
_TN = (((0,), (0,)), ((), ()))


def _dot32(a, b):
    return jnp.dot(a, b, precision=HIGHEST, preferred_element_type=F32)


def _rms_bwd(dy, x, g):
    r = lax.rsqrt(jnp.mean(x * x, axis=-1, keepdims=True) + NORM_EPS)
    t = dy * g
    dx = r * t - x * (r * r * r) * jnp.mean(t * x, axis=-1, keepdims=True)
    return dx, jnp.sum(dy * x * r, axis=0, keepdims=True)


def _rowwise(name, fn, n_rows_total, tm, rows, consts, outs, accs=(), alias=None):
    n_r, n_c, n_o, n_a = len(rows), len(consts), len(outs), len(accs)

    def body(*refs):
        row_refs = refs[:n_r]
        const_refs = refs[n_r:n_r + n_c]
        pos = n_r + n_c + (1 if alias is not None else 0)
        out_refs = refs[pos:pos + n_o]
        acc_refs = refs[pos + n_o:pos + n_o + n_a]
        res = fn(*[r[...] for r in row_refs], *[r[...] for r in const_refs])
        for r, v in zip(out_refs, res[:n_o]):
            r[...] = v.astype(r.dtype)
        if n_a:
            @pl.when(pl.program_id(0) == 0)
            def _():
                for r in acc_refs:
                    r[...] = jnp.zeros_like(r)
            for r, v in zip(acc_refs, res[n_o:]):
                r[...] += v

    in_specs = [pl.BlockSpec((tm, w), functools.partial(lambda i, cb: (i, cb), cb=cb)) for (_, w, cb) in rows]
    in_specs += [pl.BlockSpec(c.shape, lambda i: (0, 0)) for c in consts]
    args = [a for (a, _, _) in rows] + list(consts)
    out_shape, out_specs = [], []
    for o in outs:
        w, dt = o[0], o[1]
        cb, total = (o[2], o[3]) if len(o) == 4 else (0, w)
        out_shape.append(jax.ShapeDtypeStruct((n_rows_total, total), dt))
        out_specs.append(pl.BlockSpec((tm, w), functools.partial(lambda i, cb: (i, cb), cb=cb)))
    io_alias = {}
    if alias is not None:
        in_specs.append(pl.BlockSpec(memory_space=pl.ANY))
        args.append(alias[0])
        io_alias = {len(args) - 1: alias[1]}
    for (r, w) in accs:
        out_shape.append(jax.ShapeDtypeStruct((r, w), F32))
        out_specs.append(pl.BlockSpec((r, w), lambda i: (0, 0)))
    res = pl.pallas_call(body, name=name, grid=(n_rows_total // tm,), in_specs=in_specs, out_specs=out_specs,
                         out_shape=out_shape, input_output_aliases=io_alias,
                         compiler_params=_params("arbitrary"))(*args)
    return res


class _Riding(NamedTuple):
    ins: tuple
    outs: tuple
    n_sems: int
    start: Callable
    wait: Callable


_HBM = pl.BlockSpec(memory_space=pl.ANY)


def _ride(riding, refs, n_in, n_out, n_scratch, first, last):
    if riding is None:
        return refs[:n_in], refs[n_in:n_in + n_out], refs[n_in + n_out:]
    r_in, r_out = len(riding.ins), len(riding.outs)
    ins, rins = refs[:n_in], refs[n_in:n_in + r_in]
    pos = n_in + r_in
    outs, routs = refs[pos:pos + n_out], refs[pos + n_out:pos + n_out + r_out]
    pos += n_out + r_out
    scratch, (send_sems, recv_sems) = refs[pos:pos + n_scratch], refs[pos + n_scratch:]

    @pl.when(first)
    def _():
        riding.start(rins, routs, send_sems, recv_sems)

    @pl.when(last)
    def _():
        riding.wait(rins, routs, send_sems, recv_sems)

    return ins, outs, scratch


def _riding_call(riding, body, name, grid, in_specs, args, out_specs, out_shape, scratch, io_alias=None):
    if riding is not None:
        in_specs = list(in_specs) + [_HBM] * len(riding.ins)
        args = list(args) + list(riding.ins)
        out_specs = list(out_specs) + [_HBM] * len(riding.outs)
        out_shape = list(out_shape) + list(riding.outs)
        scratch = list(scratch) + [pltpu.SemaphoreType.DMA((riding.n_sems,))] * 2
    return pl.pallas_call(body, name=name, grid=grid, in_specs=in_specs, out_specs=out_specs, out_shape=out_shape,
                          scratch_shapes=scratch, input_output_aliases=io_alias or {},
                          compiler_params=_params(*(["arbitrary"] * len(grid))))(*args)


def _mm_nn(name, a, b, tm, tn, riding=None, prologue=None, consts=()):
    m, k = a.shape
    n = b.shape[1] if b.ndim == 2 else b.shape[0] * b.shape[2]
    grid = (n // tn, m // tm)
    n_out, scratch = (1, []) if prologue is None else (2, [pltpu.VMEM((m, k), MXU_DTYPE)])

    def body(*refs):
        j, i = pl.program_id(0), pl.program_id(1)
        ins, outs, kept = _ride(riding, refs, 2 + len(consts), n_out, len(scratch), (j == 0) & (i == 0),
                                (j == grid[0] - 1) & (i == grid[1] - 1))
        if prologue is None:
            left = ins[0][...]
        else:
            rows = pl.ds(pl.multiple_of(i * tm, tm), tm)

            @pl.when(j == 0)
            def _():
                tile = _mx(prologue(ins[0][...], *[c[...] for c in ins[2:]]))
                kept[0][rows, :] = tile
                outs[1][...] = tile

            left = kept[0][rows, :]
        outs[0][...] = _dot(left, ins[1][...])

    once = (lambda j, i: (i, 0)) if prologue is None else (lambda j, i: (jnp.where(j == 0, i, grid[1] - 1), 0))
    b_spec = (pl.BlockSpec((k, tn), lambda j, i: (0, j)) if b.ndim == 2
              else pl.BlockSpec((None, k, tn), lambda j, i: (j, 0, 0)))
    in_specs = [pl.BlockSpec((tm, k), once), b_spec]
    in_specs += [pl.BlockSpec(c.shape, lambda j, i: (0, 0)) for c in consts]
    out_specs = [pl.BlockSpec((tm, tn), lambda j, i: (i, j))]
    out_shape = [jax.ShapeDtypeStruct((m, n), F32)]
    if prologue is not None:
        out_specs.append(pl.BlockSpec((tm, k), once))
        out_shape.append(jax.ShapeDtypeStruct((m, k), MXU_DTYPE))
    res = _riding_call(riding, body, name, grid, in_specs, [a, b] + list(consts), out_specs, out_shape, scratch)
    return res[0] if riding is None and prologue is None else res


def _mm_nt_then(name, a, b, tm, tn, fn, rows, consts, outs, accs=(), alias=None, riding=None):
    m, n = a.shape
    k = b.shape[-2]
    steps = n // tn
    n_r, n_c, n_o, n_a = len(rows), len(consts), len(outs), len(accs)

    def body(*refs):
        a_ref, b_ref = refs[:2]
        row_refs = refs[2:2 + n_r]
        const_refs = refs[2 + n_r:2 + n_r + n_c]
        i, s = pl.program_id(0), pl.program_id(1)
        n_in = 2 + n_r + n_c + (1 if alias is not None else 0)
        _, outs_, (mm_ref,) = _ride(riding, refs, n_in, n_o + n_a, 1, (i == 0) & (s == 0),
                                    (i == m // tm - 1) & (s == steps - 1))
        out_refs, acc_refs = outs_[:n_o], outs_[n_o:]
        part = _dot(a_ref[...], b_ref[...], _NT)
        if steps > 1:
            @pl.when(s == 0)
            def _():
                mm_ref[...] = jnp.zeros_like(mm_ref)
            mm_ref[...] += part

        @pl.when(s == steps - 1)
        def _():
            res = fn(mm_ref[...] if steps > 1 else part, *[r[...] for r in row_refs], *[r[...] for r in const_refs])
            for r, v in zip(out_refs, res[:n_o]):
                r[...] = v.astype(r.dtype)
            if n_a:
                @pl.when(i == 0)
                def _():
                    for r in acc_refs:
                        r[...] = jnp.zeros_like(r)
                for r, v in zip(acc_refs, res[n_o:]):
                    r[...] += v

    b_spec = (pl.BlockSpec((k, tn), lambda i, s: (0, s)) if b.ndim == 2
              else pl.BlockSpec((None, k, tn), lambda i, s: (s, 0, 0)))
    in_specs = [pl.BlockSpec((tm, tn), lambda i, s: (i, s)), b_spec]
    in_specs += [pl.BlockSpec((tm, w), functools.partial(lambda i, s, cb: (i, cb), cb=cb)) for (_, w, cb) in rows]
    in_specs += [pl.BlockSpec(c.shape, lambda i, s: (0, 0)) for c in consts]
    args = [a, b] + [r[0] for r in rows] + list(consts)
    out_shape, out_specs = [], []
    for o in outs:
        w, dt = o[0], o[1]
        cb, total = (o[2], o[3]) if len(o) == 4 else (0, w)
        out_shape.append(jax.ShapeDtypeStruct((m, total), dt))
        out_specs.append(pl.BlockSpec((tm, w), functools.partial(lambda i, s, cb: (i, cb), cb=cb)))
    io_alias = {}
    if alias is not None:
        in_specs.append(pl.BlockSpec(memory_space=pl.ANY))
        args.append(alias[0])
        io_alias = {len(args) - 1: alias[1]}
    for (r, w) in accs:
        out_shape.append(jax.ShapeDtypeStruct((r, w), F32))
        out_specs.append(pl.BlockSpec((r, w), lambda i, s: (0, 0)))
    return _riding_call(riding, body, name, (m // tm, steps), in_specs, args, out_specs, out_shape,
                        [pltpu.VMEM((tm, k), F32)], io_alias)


def _mm_tn(name, a, b, tk, tn, col_shards=False, riding=None):
    t, k = a.shape
    n = b.shape[1]
    steps = t // tk

    def body(*refs):
        j, s = pl.program_id(0), pl.program_id(1)
        (a_ref, b_ref), (o_ref,), (acc_ref,) = _ride(riding, refs, 2, 1, 1, (j == 0) & (s == 0),
                                                     (j == n // tn - 1) & (s == steps - 1))

        @pl.when(s == 0)
        def _():
            acc_ref[...] = jnp.zeros_like(acc_ref)

        acc_ref[...] += _dot(a_ref[...], b_ref[...], _TN)

        @pl.when(s == steps - 1)
        def _():
            o_ref[...] = acc_ref[...]

    if col_shards:
        out_spec = pl.BlockSpec((None, k, tn), lambda j, s: (j, 0, 0))
        out_shape = jax.ShapeDtypeStruct((n // tn, k, tn), F32)
    else:
        out_spec = pl.BlockSpec((k, tn), lambda j, s: (0, j))
        out_shape = jax.ShapeDtypeStruct((k, n), F32)
    res = _riding_call(riding, body, name, (n // tn, steps),
                       [pl.BlockSpec((tk, k), lambda j, s: (s, 0)), pl.BlockSpec((tk, tn), lambda j, s: (s, j))],
                       [a, b], [out_spec], [out_shape], [pltpu.VMEM((k, tn), F32)])
    return res[0] if riding is None else res


def _dot01(m01, x):
    m = m01.astype(MXU_DTYPE)
    hi = x.astype(MXU_DTYPE)
    r1 = x - hi.astype(F32)
    mid = r1.astype(MXU_DTYPE)
    lo = (r1 - mid.astype(F32)).astype(MXU_DTYPE)
    dot = lambda v: jnp.dot(m, v, preferred_element_type=F32)
    return dot(hi) + dot(mid) + dot(lo)


def _chunk_rows(x, offset, nck):
    return jnp.concatenate([jnp.broadcast_to(x[c * HG_CHUNK + offset:c * HG_CHUNK + offset + 1, :],
                                             (HG_CHUNK, x.shape[1])) for c in range(nck)], axis=0)


def _hg_block_terms(q, f, lb, tb):
    nck = tb // HG_CHUNK
    sig = _sig(f)
    fv = lb + (1.0 - lb) * sig
    kk = (1.0 - lb) * (1.0 - sig)
    row = lax.broadcasted_iota(jnp.int32, (tb, tb), 0)
    col = lax.broadcasted_iota(jnp.int32, (tb, tb), 1)
    same = jnp.right_shift(row, 6) == jnp.right_shift(col, 6)
    causal, anti = same & (row >= col), same & (row <= col)
    b = _dot01(causal, jnp.log(fv))
    b_mid, b_last = _chunk_rows(b, HG_CHUNK // 2 - 1, nck), _chunk_rows(b, HG_CHUNK - 1, nck)
    e_mid, e_mid_inv = jnp.exp(b - b_mid), jnp.exp(b_mid - b)
    e_b, e_last = jnp.exp(b), jnp.exp(b_last - b)
    dcs = [jnp.exp(b[c * HG_CHUNK + HG_CHUNK - 1:(c + 1) * HG_CHUNK, :]) for c in range(nck)]
    return sig, fv, kk, causal, anti, e_mid, e_mid_inv, e_b, e_last, dcs


def _hgrn2_fwd(proj, hg_lb, hg_norm_g, t_len, tb):
    nck = tb // HG_CHUNK

    def body(p_ref, lb_ref, gn_ref, o_ref, act_ref, sp_ref, st_ref, a_s, bm_s, qd_s, kd_s, v_s, sc_s, inc_s):
        @pl.when(pl.program_id(0) == 0)
        def _():
            st_ref[...] = jnp.zeros_like(st_ref)

        lb = _sig(lb_ref[0:1, :] - lb_ref[1:2, :])
        q = p_ref[:, pl.ds(0, 1024)]
        _, _, kk, causal, _, e_mid, e_mid_inv, e_b, e_last, dcs = _hg_block_terms(q, p_ref[:, pl.ds(1024, 1024)],
                                                                                   lb, tb)
        a_s[...] = _mx(q * e_mid)
        bm_s[...] = _mx(kk * e_mid_inv)
        qd_s[...] = _mx(q * e_b)
        kd_s[...] = _mx(kk * e_last)
        v_s[...] = _mx(p_ref[:, pl.ds(2048, 1024)])
        heads = [pl.ds(h * HG_DIM, HG_DIM) for h in range(HG_HEADS)]
        chunks = [pl.ds(c * HG_CHUNK, HG_CHUNK) for c in range(nck)]
        for h, hs in enumerate(heads):
            sc_s[h] = _mx(jnp.where(causal, _dot(a_s[:, hs], bm_s[:, hs], _NT), 0.0))
        for h, hs in enumerate(heads):
            o_ref[:, hs] = _dot(sc_s[h], v_s[:, hs])
        for h, hs in enumerate(heads):
            for c, r in enumerate(chunks):
                inc_s[h, c] = _dot(v_s[r, hs], kd_s[r, hs], _TN)
        for c in range(nck):
            for h in range(HG_HEADS):
                st = st_ref[h]
                sp_ref[h, c] = st
                st_ref[h] = dcs[c][:, h * HG_DIM:(h + 1) * HG_DIM] * st + inc_s[h, c]
        for c, r in enumerate(chunks):
            for h, hs in enumerate(heads):
                o_ref[r, hs] += _dot(qd_s[r, hs], sp_ref[h, c], _NT)
        for h, hs in enumerate(heads):
            o = o_ref[:, hs]
            rr = lax.rsqrt(jnp.mean(o * o, axis=-1, keepdims=True) + NORM_EPS)
            g = p_ref[:, pl.ds(3072 + h * HG_DIM, HG_DIM)]
            act_ref[:, hs] = (o * rr * gn_ref[:, hs] * (g * _sig(g))).astype(act_ref.dtype)

    nb = t_len // tb
    return pl.pallas_call(
        body, name="hgrn2_fwd", grid=(nb,),
        in_specs=[pl.BlockSpec((tb, 4096), lambda i: (i, 0)),
                  pl.BlockSpec((2, 1024), lambda i: (0, 0)),
                  pl.BlockSpec((1, 1024), lambda i: (0, 0))],
        out_specs=[pl.BlockSpec((tb, 1024), lambda i: (i, 0)),
                   pl.BlockSpec((tb, 1024), lambda i: (i, 0)),
                   pl.BlockSpec((HG_HEADS, nck, HG_DIM, HG_DIM), lambda i: (0, i, 0, 0))],
        out_shape=[jax.ShapeDtypeStruct((t_len, 1024), F32),
                   jax.ShapeDtypeStruct((t_len, 1024), MXU_DTYPE),
                   jax.ShapeDtypeStruct((HG_HEADS, t_len // HG_CHUNK, HG_DIM, HG_DIM), F32)],
        scratch_shapes=[pltpu.VMEM((HG_HEADS, HG_DIM, HG_DIM), F32)] + [pltpu.VMEM((tb, 1024), MXU_DTYPE)] * 5
                       + [pltpu.VMEM((HG_HEADS, tb, tb), MXU_DTYPE), pltpu.VMEM((HG_HEADS, nck, HG_DIM, HG_DIM), F32)],
        compiler_params=_params("arbitrary"))(proj, hg_lb, hg_norm_g)


def _hgrn2_bwd(proj, d_o, s_prev, hg_lb, dproj, t_len, tb, riding=None):
    nck = tb // HG_CHUNK
    nb = t_len // tb

    def body(*refs):
        step = pl.program_id(0)
        ((p_ref, do_ref, sp_ref, lb_ref, _), (dp_ref, dlb_ref),
         (ds_ref, acc_ref, a_s, bm_s, qd_s, kd_s, v_s, do_s, da_s, dbm_s, dqd_s, dkd_s, dv_s, ex_s, sc_s, dsc_s,
          up_s)) = _ride(riding, refs, 5, 2, 17, step == 0, step == nb - 1)

        @pl.when(pl.program_id(0) == 0)
        def _():
            ds_ref[...] = jnp.zeros_like(ds_ref)
            acc_ref[...] = jnp.zeros_like(acc_ref)

        lb = _sig(lb_ref[0:1, :] - lb_ref[1:2, :])
        q = p_ref[:, pl.ds(0, 1024)]
        sig, fv, kk, causal, anti, e_mid, e_mid_inv, e_b, e_last, dcs = _hg_block_terms(
            q, p_ref[:, pl.ds(1024, 1024)], lb, tb)
        a, bm, qd, kd = q * e_mid, kk * e_mid_inv, q * e_b, kk * e_last
        a_s[...] = _mx(a)
        bm_s[...] = _mx(bm)
        qd_s[...] = _mx(qd)
        kd_s[...] = _mx(kd)
        v_s[...] = _mx(p_ref[:, pl.ds(2048, 1024)])
        do_s[...] = _mx(do_ref[...])
        heads = [pl.ds(h * HG_DIM, HG_DIM) for h in range(HG_HEADS)]
        chunks = [pl.ds(c * HG_CHUNK, HG_CHUNK) for c in range(nck)]
        for h, hs in enumerate(heads):
            sc_s[h] = _mx(jnp.where(causal, _dot(a_s[:, hs], bm_s[:, hs], _NT), 0.0))
            dsc_s[h] = _mx(jnp.where(causal, _dot(do_s[:, hs], v_s[:, hs], _NT), 0.0))
        for h, hs in enumerate(heads):
            dv_s[:, hs] = _dot(sc_s[h], do_s[:, hs], _TN)
            da_s[:, hs] = _dot(dsc_s[h], bm_s[:, hs])
            dbm_s[:, hs] = _dot(dsc_s[h], a_s[:, hs], _TN)
        for h, hs in enumerate(heads):
            for c, r in enumerate(chunks):
                up_s[h, c] = _dot(do_s[r, hs], qd_s[r, hs], _TN)
                dqd_s[r, hs] = _dot(do_s[r, hs], sp_ref[h, c])
        for c in reversed(range(nck)):
            r = chunks[c]
            for h, hs in enumerate(heads):
                dst = ds_ref[h]
                dc = dcs[c][:, h * HG_DIM:(h + 1) * HG_DIM]
                dv_s[r, hs] += _dot(kd_s[r, hs], dst, _NT)
                dkd_s[r, hs] = _dot(v_s[r, hs], dst)
                ex_s[c:c + 1, hs] = jnp.sum(dst * sp_ref[h, c], axis=0, keepdims=True) * dc
                ds_ref[h] = up_s[h, c] + dc * dst
        da, dbm, dqd, dkd = da_s[...], dbm_s[...], dqd_s[...], dkd_s[...]
        dq = da * e_mid + dqd * e_b
        dk = dbm * e_mid_inv + dkd * e_last
        db = da * a - dbm * bm + dqd * qd - dkd * kd
        dkk = dkd * kd
        extra = jnp.concatenate(
            [jnp.broadcast_to(jnp.sum(dkk[c * HG_CHUNK:(c + 1) * HG_CHUNK], axis=0, keepdims=True)
                              + ex_s[c:c + 1, :], (HG_CHUNK, 1024)) for c in range(nck)], axis=0)
        dlogf = _dot01(anti, db) + extra
        dfv_k = dlogf / fv - dk
        dp_ref[:, pl.ds(0, 1024)] = dq.astype(dp_ref.dtype)
        dp_ref[:, pl.ds(1024, 1024)] = (dfv_k * (1.0 - lb) * sig * (1.0 - sig)).astype(dp_ref.dtype)
        dp_ref[:, pl.ds(2048, 1024)] = dv_s[...].astype(dp_ref.dtype)
        acc_ref[...] += jnp.sum(dfv_k * (1.0 - sig), axis=0, keepdims=True)

        @pl.when(pl.program_id(0) == nb - 1)
        def _():
            g0 = acc_ref[...] * lb * (1.0 - lb)
            dlb_ref[0:1, :] = g0
            dlb_ref[1:2, :] = -g0

    return _riding_call(
        riding, body, "hgrn2_bwd", (nb,),
        [pl.BlockSpec((tb, 3072), lambda i: (nb - 1 - i, 0)),
         pl.BlockSpec((tb, 1024), lambda i: (nb - 1 - i, 0)),
         pl.BlockSpec((HG_HEADS, nck, HG_DIM, HG_DIM), lambda i: (0, nb - 1 - i, 0, 0)),
         pl.BlockSpec((2, 1024), lambda i: (0, 0)),
         pl.BlockSpec(memory_space=pl.ANY)],
        [proj, d_o, s_prev, hg_lb, dproj],
        [pl.BlockSpec((tb, 3072), lambda i: (nb - 1 - i, 0)), pl.BlockSpec((2, 1024), lambda i: (0, 0))],
        [jax.ShapeDtypeStruct((t_len, IN_COLS), dproj.dtype), jax.ShapeDtypeStruct((2, 1024), F32)],
        [pltpu.VMEM((HG_HEADS, HG_DIM, HG_DIM), F32), pltpu.VMEM((1, 1024), F32)]
        + [pltpu.VMEM((tb, 1024), MXU_DTYPE)] * 6 + [pltpu.VMEM((tb, 1024), F32)] * 5
        + [pltpu.VMEM((SUBLANES, 1024), F32)] + [pltpu.VMEM((HG_HEADS, tb, tb), MXU_DTYPE)] * 2
        + [pltpu.VMEM((HG_HEADS, nck, HG_DIM, HG_DIM), F32)], {4: 0})


def _s5_prep_bwd(a_re, a_im, log_dt, b_re_t, b_im_t, dlam, dbbr, dbbi):
    def body(ar_ref, ai_ref, ldt_ref, br_ref, bi_ref, dlam_ref, dbbr_ref, dbbi_ref,
             dar_ref, dai_ref, dldt_ref, dbr_ref, dbi_ref):
        ar, ai = ar_ref[...], ai_ref[...]
        dt = jnp.exp(ldt_ref[...])
        mag = jnp.exp(ar * dt)
        cs, sn = jnp.cos(ai * dt), jnp.sin(ai * dt)
        lr, li = mag * cs, mag * sn
        den = ar * ar + ai * ai
        nr = lr - 1.0
        sr = (nr * ar + li * ai) / den
        si = (li * ar - nr * ai) / den
        br, bi = br_ref[...], bi_ref[...]
        gbr, gbi = dbbr_ref[...], dbbi_ref[...]
        dbr_ref[...] = sr * gbr + si * gbi
        dbi_ref[...] = sr * gbi - si * gbr
        dsr = jnp.sum(gbr * br + gbi * bi, axis=0, keepdims=True)
        dsi = jnp.sum(gbi * br - gbr * bi, axis=0, keepdims=True)
        dnr = (dsr * ar - dsi * ai) / den
        dli = dlam_ref[1:2, :] + (dsr * ai + dsi * ar) / den
        dlr = dlam_ref[0:1, :] + dnr
        dden = -(dsr * sr + dsi * si) / den
        dar = (dsr * nr + dsi * li) / den + dden * 2.0 * ar
        dai = (dsr * li - dsi * nr) / den + dden * 2.0 * ai
        dmag = dlr * cs + dli * sn
        dth = mag * (dli * cs - dlr * sn)
        dar_ref[...] = dar + dmag * mag * dt
        dai_ref[...] = dai + dth * dt
        ddt = (dmag * mag * ar + dth * ai) * dt
        lane = lax.broadcasted_iota(jnp.int32, (S5_LANES, 128), 0) // S5_STATE
        grp = lax.broadcasted_iota(jnp.int32, (S5_LANES, 128), 1)
        dldt_ref[...] = _dot32(jnp.broadcast_to(ddt, (SUBLANES, S5_LANES)), (lane == grp).astype(F32))

    whole = pl.BlockSpec(memory_space=pltpu.VMEM)
    return pl.pallas_call(
        body, name="s5_prep_bwd", in_specs=[whole] * 8, out_specs=[whole] * 5,
        out_shape=[jax.ShapeDtypeStruct((1, S5_LANES), F32), jax.ShapeDtypeStruct((1, S5_LANES), F32),
                   jax.ShapeDtypeStruct((SUBLANES, 128), F32), jax.ShapeDtypeStruct((S5_GROUP, S5_LANES), F32),
                   jax.ShapeDtypeStruct((S5_GROUP, S5_LANES), F32)])(a_re, a_im, log_dt, b_re_t, b_im_t, dlam, dbbr,
                                                                      dbbi)


def _dgelu(x):
    c, a = 0.7978845608028654, 0.044715
    th = jnp.tanh(c * (x + a * x * x * x))
    return 0.5 * (1.0 + th) + 0.5 * x * (1.0 - th * th) * c * (1.0 + 3.0 * a * x * x)


S5_BLOCKS = 4
S5_BW = S5_WIDTH // S5_BLOCKS
S5_BL = S5_LANES // S5_BLOCKS
S5_LANE_BLOCKS = S5_LANES // 128
S5_SCAN_BLOCKS = 4


def _s5_prep(a_re, a_im, log_dt, b_re_t, b_im_t, seg):
    def body(ar_ref, ai_ref, ldt_ref, br_ref, bi_ref,
             rows_f, pfr_ref, pfi_ref, rows_r, prr_ref, pri_ref, bbr_ref, bbi_ref):
        ar, ai = ar_ref[...], ai_ref[...]
        dt = jnp.exp(ldt_ref[...])
        mag = jnp.exp(ar * dt)
        lr, li = mag * jnp.cos(ai * dt), mag * jnp.sin(ai * dt)
        den = ar * ar + ai * ai
        nr = lr - 1.0
        sr = (nr * ar + li * ai) / den
        si = (li * ar - nr * ai) / den
        wide = (SUBLANES, S5_LANES)
        cr, ci = lr, li
        for i in range(seg):
            pfr_ref[i] = jnp.broadcast_to(cr, wide)
            pfi_ref[i] = jnp.broadcast_to(ci, wide)
            prr_ref[seg - 1 - i] = jnp.broadcast_to(cr, wide)
            pri_ref[seg - 1 - i] = jnp.broadcast_to(-ci, wide)
            if i == seg - 1:
                for rows, sign in ((rows_f, 1.0), (rows_r, -1.0)):
                    rows[0:1, :] = lr
                    rows[1:2, :] = sign * li
                    rows[2:3, :] = cr
                    rows[3:4, :] = sign * ci
            cr, ci = cr * lr - ci * li, cr * li + ci * lr
        br, bi = br_ref[...], bi_ref[...]
        bbr_ref[...] = sr * br - si * bi
        bbi_ref[...] = sr * bi + si * br

    whole = pl.BlockSpec(memory_space=pltpu.VMEM)
    tables = [jax.ShapeDtypeStruct((4, S5_LANES), F32)] + [jax.ShapeDtypeStruct((seg, SUBLANES, S5_LANES), F32)] * 2
    bbar = [jax.ShapeDtypeStruct((S5_GROUP, S5_LANES), F32)] * 2
    res = pl.pallas_call(body, name="s5_prep", in_specs=[whole] * 5, out_specs=[whole] * 8,
                         out_shape=tables + tables + bbar)(a_re, a_im, log_dt, b_re_t, b_im_t)
    return res[0:3], res[3:6], res[6], res[7]


def _lanes(j):
    return pl.ds(j * 128, 128)


def _to_segment_order(v, stage_ref, out_ref, seg):
    nbl = v.shape[1] // 128
    for b in range(nbl):
        stage_ref[b] = v[:, b * 128:(b + 1) * 128]

    def body(t, carry):
        rows = pl.ds(pl.multiple_of(t * SUBLANES, SUBLANES), SUBLANES)
        for b in range(nbl):
            out_ref[rows, _lanes(b)] = stage_ref[b, pl.ds(t, SUBLANES, stride=seg), :]
        return carry

    lax.fori_loop(0, seg, body, 0, unroll=True)


def _from_segment_order(v, stage_ref, out_ref, seg):
    nbl = v.shape[1] // 128
    for b in range(nbl):
        stage_ref[b] = v[:, b * 128:(b + 1) * 128]
    for s in range(SUBLANES):
        def body(k, carry, s=s):
            rows = pl.ds(pl.multiple_of(s * seg + k * SUBLANES, SUBLANES), SUBLANES)
            for b in range(nbl):
                out_ref[rows, _lanes(b)] = stage_ref[b, pl.ds(k * SUBLANES * SUBLANES + s, SUBLANES,
                                                              stride=SUBLANES), :]
            return carry

        lax.fori_loop(0, seg // SUBLANES, body, 0, unroll=True)


def _tile_scan(xr_ref, xi_ref, lam_ref, car_ref, cai_ref, cn_r, cn_i, blocks, seg, reverse):
    shape = (SUBLANES, 128)
    lrs = [jnp.broadcast_to(lam_ref[0:1, _lanes(j)], shape) for j in blocks]
    lis = [jnp.broadcast_to(lam_ref[1:2, _lanes(j)], shape) for j in blocks]

    def step(k, carry):
        t = seg - 1 - k if reverse else k
        rows = pl.ds(pl.multiple_of(t * SUBLANES, SUBLANES), SUBLANES)
        out = []
        for n, j in enumerate(blocks):
            cr, ci = carry[2 * n], carry[2 * n + 1]
            nr = lrs[n] * cr - lis[n] * ci + xr_ref[rows, _lanes(j)]
            ni = lrs[n] * ci + lis[n] * cr + xi_ref[rows, _lanes(j)]
            xr_ref[rows, _lanes(j)] = nr
            xi_ref[rows, _lanes(j)] = ni
            out += [nr, ni]
        return tuple(out)

    zero = jnp.zeros(shape, F32)
    fin = lax.fori_loop(0, seg, step, (zero,) * (2 * len(blocks)), unroll=True)
    for n, j in enumerate(blocks):
        ls = _lanes(j)
        fr, fi = fin[2 * n], fin[2 * n + 1]
        sr, si = lam_ref[2:3, ls], lam_ref[3:4, ls]
        pr, pi = car_ref[:, ls], cai_ref[:, ls]
        for s in (reversed(range(SUBLANES)) if reverse else range(SUBLANES)):
            cn_r[s:s + 1, ls] = pr
            cn_i[s:s + 1, ls] = pi
            pr, pi = fr[s:s + 1, :] + sr * pr - si * pi, fi[s:s + 1, :] + sr * pi + si * pr
        car_ref[:, ls] = pr
        cai_ref[:, ls] = pi


def _s5_fwd(proj, lam_rows, p3_re, p3_im, bbr4, bbi4, crt4, cit4, d_row, t_len, tb):
    seg = tb // SUBLANES

    def body(u_ref, lam_ref, p3r_ref, p3i_ref, bbr_ref, bbi_ref, crt_ref, cit_ref, d_ref,
             hr_ref, hi_ref, ypre_ref, ys_ref, car_ref, cai_ref, cn_r, cn_i, stage_ref, us_ref, yseg_ref):
        @pl.when(pl.program_id(0) == 0)
        def _():
            car_ref[...] = jnp.zeros_like(car_ref)
            cai_ref[...] = jnp.zeros_like(cai_ref)

        _to_segment_order(u_ref[...], stage_ref, us_ref, seg)
        u = us_ref[...]
        for i in range(S5_BLOCKS):
            ui = u[:, i * S5_BW:(i + 1) * S5_BW]
            hr_ref[:, pl.ds(i * S5_BL, S5_BL)] = _dot(ui, bbr_ref[i])
            hi_ref[:, pl.ds(i * S5_BL, S5_BL)] = _dot(ui, bbi_ref[i])
        for lc in range(S5_LANE_BLOCKS // S5_SCAN_BLOCKS):
            blocks = range(lc * S5_SCAN_BLOCKS, (lc + 1) * S5_SCAN_BLOCKS)
            _tile_scan(hr_ref, hi_ref, lam_ref, car_ref, cai_ref, cn_r, cn_i, blocks, seg, False)
            crs = [cn_r[:, _lanes(j)] for j in blocks]
            cis = [cn_i[:, _lanes(j)] for j in blocks]

            def fix(t, carry, blocks=blocks, crs=crs, cis=cis):
                rows = pl.ds(pl.multiple_of(t * SUBLANES, SUBLANES), SUBLANES)
                for n, j in enumerate(blocks):
                    pr, pi = p3r_ref[t, :, _lanes(j)], p3i_ref[t, :, _lanes(j)]
                    hr_ref[rows, _lanes(j)] += pr * crs[n] - pi * cis[n]
                    hi_ref[rows, _lanes(j)] += pr * cis[n] + pi * crs[n]
                return carry

            lax.fori_loop(0, seg, fix, 0, unroll=True)
        for i in range(S5_BLOCKS):
            ws = pl.ds(i * S5_BW, S5_BW)
            bl = pl.ds(i * S5_BL, S5_BL)
            yseg_ref[:, ws] = (_dot(hr_ref[:, bl], crt_ref[i]) - _dot(hi_ref[:, bl], cit_ref[i])
                               + d_ref[:, ws] * u[:, i * S5_BW:(i + 1) * S5_BW])
        _from_segment_order(yseg_ref[...], stage_ref, ypre_ref, seg)
        ys_ref[...] = jax.nn.gelu(ypre_ref[...], approximate=True).astype(ys_ref.dtype)

    whole = pl.BlockSpec(memory_space=pltpu.VMEM)
    return pl.pallas_call(
        body, name="s5_fwd", grid=(t_len // tb,),
        in_specs=[pl.BlockSpec((tb, S5_WIDTH), lambda i: (i, 4096 // S5_WIDTH))] + [whole] * 8,
        out_specs=[pl.BlockSpec((tb, S5_LANES), lambda i: (i, 0)), pl.BlockSpec((tb, S5_LANES), lambda i: (i, 0)),
                   pl.BlockSpec((tb, S5_WIDTH), lambda i: (i, 0)), pl.BlockSpec((tb, S5_WIDTH), lambda i: (i, 0))],
        out_shape=[jax.ShapeDtypeStruct((t_len, S5_LANES), F32), jax.ShapeDtypeStruct((t_len, S5_LANES), F32),
                   jax.ShapeDtypeStruct((t_len, S5_WIDTH), F32), jax.ShapeDtypeStruct((t_len, S5_WIDTH), MXU_DTYPE)],
        scratch_shapes=[pltpu.VMEM((1, S5_LANES), F32), pltpu.VMEM((1, S5_LANES), F32),
                        pltpu.VMEM((SUBLANES, S5_LANES), F32), pltpu.VMEM((SUBLANES, S5_LANES), F32),
                        pltpu.VMEM((S5_WIDTH // 128, tb, 128), F32), pltpu.VMEM((tb, S5_WIDTH), F32),
                        pltpu.VMEM((tb, S5_WIDTH), F32)],
        compiler_params=_params("arbitrary"))(proj, lam_rows, p3_re, p3_im, bbr4, bbi4, crt4, cit4, d_row)


def _s5_bwd(dgelu, y_pre, proj, h_re, h_im, lam_rows, p3_re, p3_im, bbr4, bbi4, cr4, ci4, d_row, dproj, t_len, tb):
    seg = tb // SUBLANES
    nb = t_len // tb

    def body(dg_ref, yp_ref, u_ref, hr_ref, hi_ref, lam_ref, p3r_ref, p3i_ref, bbr_ref, bbi_ref, cr_ref, ci_ref,
             d_ref, _, du_ref, dbbr_ref, dbbi_ref, dcr_ref, dci_ref, dd_ref, dlam_ref,
             gr_ref, gi_ref, car_ref, cai_ref, cn_r, cn_i, stage_ref, us_ref, dys_ref, duseg_ref):
        @pl.when(pl.program_id(0) == 0)
        def _():
            for ref in (car_ref, cai_ref, dbbr_ref, dbbi_ref, dcr_ref, dci_ref, dd_ref, dlam_ref):
                ref[...] = jnp.zeros_like(ref)

        _to_segment_order(u_ref[...], stage_ref, us_ref, seg)
        _to_segment_order(dg_ref[...] * _dgelu(yp_ref[...]), stage_ref, dys_ref, seg)
        u, dy = us_ref[...], dys_ref[...]
        for i in range(S5_BLOCKS):
            dyi = dy[:, i * S5_BW:(i + 1) * S5_BW]
            gr_ref[:, pl.ds(i * S5_BL, S5_BL)] = _dot(dyi, cr_ref[i])
            gi_ref[:, pl.ds(i * S5_BL, S5_BL)] = -_dot(dyi, ci_ref[i])
        for lc in range(S5_LANE_BLOCKS // S5_SCAN_BLOCKS):
            blocks = range(lc * S5_SCAN_BLOCKS, (lc + 1) * S5_SCAN_BLOCKS)
            _tile_scan(gr_ref, gi_ref, lam_ref, car_ref, cai_ref, cn_r, cn_i, blocks, seg, True)
            crs = [cn_r[:, _lanes(j)] for j in blocks]
            cis = [cn_i[:, _lanes(j)] for j in blocks]

            def fix(k, carry, blocks=blocks, crs=crs, cis=cis):
                t = seg - 1 - k
                rows = pl.ds(pl.multiple_of(t * SUBLANES, SUBLANES), SUBLANES)
                out = []
                for n, j in enumerate(blocks):
                    nr, ni, slr, sli = carry[4 * n:4 * n + 4]
                    pr, pi = p3r_ref[t, :, _lanes(j)], p3i_ref[t, :, _lanes(j)]
                    g_r = gr_ref[rows, _lanes(j)] + pr * crs[n] - pi * cis[n]
                    g_i = gi_ref[rows, _lanes(j)] + pr * cis[n] + pi * crs[n]
                    gr_ref[rows, _lanes(j)] = g_r
                    gi_ref[rows, _lanes(j)] = g_i
                    hr, hi = hr_ref[rows, _lanes(j)], hi_ref[rows, _lanes(j)]
                    out += [g_r, g_i, slr + nr * hr + ni * hi, sli + ni * hr - nr * hi]
                return tuple(out)

            zero = jnp.zeros((SUBLANES, 128), F32)
            init = []
            for n in range(len(blocks)):
                init += [crs[n], cis[n], zero, zero]
            fin = lax.fori_loop(0, seg, fix, tuple(init), unroll=True)
            for n, j in enumerate(blocks):
                dlam_ref[0:1, _lanes(j)] += jnp.sum(fin[4 * n + 2], axis=0, keepdims=True)
                dlam_ref[1:2, _lanes(j)] += jnp.sum(fin[4 * n + 3], axis=0, keepdims=True)
        for i in range(S5_BLOCKS):
            ws = pl.ds(i * S5_BW, S5_BW)
            bl = pl.ds(i * S5_BL, S5_BL)
            ui, dyi = u[:, i * S5_BW:(i + 1) * S5_BW], dy[:, i * S5_BW:(i + 1) * S5_BW]
            gr, gi = gr_ref[:, bl], gi_ref[:, bl]
            duseg_ref[:, ws] = _dot(gr, bbr_ref[i], _NT) + _dot(gi, bbi_ref[i], _NT) + d_ref[:, ws] * dyi
            dbbr_ref[i] += _dot(ui, gr, _TN)
            dbbi_ref[i] += _dot(ui, gi, _TN)
            dcr_ref[i] += _dot(hr_ref[:, bl], dyi, _TN)
            dci_ref[i] -= _dot(hi_ref[:, bl], dyi, _TN)
        dd_ref[...] += jnp.sum(dy * u, axis=0, keepdims=True)
        _from_segment_order(duseg_ref[...], stage_ref, duseg_ref, seg)
        du_ref[...] = duseg_ref[...].astype(du_ref.dtype)

    whole = pl.BlockSpec(memory_space=pltpu.VMEM)
    rev = lambda i: (nb - 1 - i, 0)
    const3 = lambda i: (0, 0, 0)
    return pl.pallas_call(
        body, name="s5_bwd", grid=(nb,),
        in_specs=[pl.BlockSpec((tb, S5_WIDTH), rev), pl.BlockSpec((tb, S5_WIDTH), rev),
                  pl.BlockSpec((tb, S5_WIDTH), lambda i: (nb - 1 - i, 4096 // S5_WIDTH)),
                  pl.BlockSpec((tb, S5_LANES), rev), pl.BlockSpec((tb, S5_LANES), rev)] + [whole] * 8
                 + [pl.BlockSpec(memory_space=pl.ANY)],
        out_specs=[pl.BlockSpec((tb, S5_WIDTH), lambda i: (nb - 1 - i, 4096 // S5_WIDTH)),
                   pl.BlockSpec((S5_BLOCKS, S5_BW, S5_BL), const3), pl.BlockSpec((S5_BLOCKS, S5_BW, S5_BL), const3),
                   pl.BlockSpec((S5_BLOCKS, S5_BL, S5_BW), const3), pl.BlockSpec((S5_BLOCKS, S5_BL, S5_BW), const3),
                   pl.BlockSpec((1, S5_WIDTH), lambda i: (0, 0)), pl.BlockSpec((2, S5_LANES), lambda i: (0, 0))],
        out_shape=[jax.ShapeDtypeStruct((t_len, IN_COLS), dproj.dtype),
                   jax.ShapeDtypeStruct((S5_BLOCKS, S5_BW, S5_BL), F32),
                   jax.ShapeDtypeStruct((S5_BLOCKS, S5_BW, S5_BL), F32),
                   jax.ShapeDtypeStruct((S5_BLOCKS, S5_BL, S5_BW), F32),
                   jax.ShapeDtypeStruct((S5_BLOCKS, S5_BL, S5_BW), F32),
                   jax.ShapeDtypeStruct((1, S5_WIDTH), F32), jax.ShapeDtypeStruct((2, S5_LANES), F32)],
        scratch_shapes=[pltpu.VMEM((tb, S5_LANES), F32), pltpu.VMEM((tb, S5_LANES), F32),
                        pltpu.VMEM((1, S5_LANES), F32), pltpu.VMEM((1, S5_LANES), F32),
                        pltpu.VMEM((SUBLANES, S5_LANES), F32), pltpu.VMEM((SUBLANES, S5_LANES), F32),
                        pltpu.VMEM((S5_WIDTH // 128, tb, 128), F32), pltpu.VMEM((tb, S5_WIDTH), F32),
                        pltpu.VMEM((tb, S5_WIDTH), F32), pltpu.VMEM((tb, S5_WIDTH), F32)],
        input_output_aliases={13: 0},
        compiler_params=_params("arbitrary"))(dgelu, y_pre, proj, h_re, h_im, lam_rows, p3_re, p3_im, bbr4, bbi4,
                                              cr4, ci4, d_row, dproj)


def _block_diag(per_group):
    g8 = S5_GROUPS // S5_BLOCKS
    eye = jnp.eye(g8, dtype=bool)[None, :, None, :, None]
    dense = jnp.where(eye, per_group.reshape(S5_BLOCKS, g8, S5_GROUP, 1, S5_STATE), 0.0)
    return dense.reshape(S5_BLOCKS, S5_BW, S5_BL)


def _diag_blocks(dense):
    g8 = S5_GROUPS // S5_BLOCKS
    ar = jnp.arange(g8)
    d5 = dense.reshape(S5_BLOCKS, g8, S5_GROUP, g8, S5_STATE)
    return d5[:, ar, :, ar, :].transpose(1, 0, 2, 3).reshape(S5_GROUPS, S5_GROUP, S5_STATE)


def _hg_gate_bwd(da, o, g, gn):
    dos, dgs, dgns = [], [], []
    for h in range(HG_HEADS):
        sl = slice(h * HG_DIM, (h + 1) * HG_DIM)
        oh, gh, dah, gnh = o[:, sl], g[:, sl], da[:, sl], gn[:, sl]
        rr = lax.rsqrt(jnp.mean(oh * oh, axis=-1, keepdims=True) + NORM_EPS)
        sg = _sig(gh)
        dgs.append(dah * (oh * rr * gnh) * _dsilu(gh, sg))
        don = dah * (gh * sg)
        t = don * gnh
        dos.append(rr * t - oh * (rr * rr * rr) * jnp.mean(t * oh, axis=-1, keepdims=True))
        dgns.append(jnp.sum(don * oh * rr, axis=0, keepdims=True))
    return jnp.concatenate(dos, axis=1), jnp.concatenate(dgs, axis=1), jnp.concatenate(dgns, axis=1)


MIX_BWD_COLS = ((3072, 1024), (4608, 512), (5120, 1024), (6144, 1024))


def _mix_bwd(dgl, h1, dh2, act_hg, ys2, ys_gelu, proj, o_hg, g2, ghn, b_glu, w, t_len, tm):
    nb = t_len // tm

    def body(dgl_ref, h1_ref, dh2_ref, act_ref, ys2_ref, ysg_ref, ghg_ref, z_ref, gh_ref, gs_ref, o_ref, g2_ref, gn_ref,
             bglu_ref, wg_ref, wo_ref, ws5_ref, whg_ref, wglu_ref,
             dh1_ref, dyh_ref, dys_ref, dglu_ref, dgelu_ref, do_ref, dg2_ref, dbglu_ref, dgn_ref, dproj_ref,
             st0, st1, st2, st3, sems):
        i = pl.program_id(0)
        stages = (st0, st1, st2, st3)

        def writes(step):
            rows = pl.ds(pl.multiple_of(step * tm, tm), tm)
            return [pltpu.make_async_copy(st, dproj_ref.at[rows, pl.ds(c0, wd)], sems.at[k])
                    for k, (st, (c0, wd)) in enumerate(zip(stages, MIX_BWD_COLS))]

        @pl.when(i > 0)
        def _():
            for cp in writes(i - 1):
                cp.wait()

        @pl.when(i == 0)
        def _():
            for ref in (dg2_ref, dbglu_ref, dgn_ref):
                ref[...] = jnp.zeros_like(ref)

        dx, dg2 = _rms_bwd(_dot(dgl_ref[...], wg_ref[...], _NT), h1_ref[...], g2_ref[...])
        dh1 = dh2_ref[...] + dx
        dh1_ref[...] = dh1
        dg2_ref[...] += dg2
        dm = _dot(dh1, wo_ref[...], _NT)
        sh, ss = _sig(gh_ref[...]), _sig(gs_ref[...])
        dyh, dys = _mx(dm * sh), _mx(dm * ss)
        dyh_ref[...] = dyh
        dys_ref[...] = dys
        st2[...] = (dm * _dot(act_ref[...], whg_ref[...]) * sh * (1.0 - sh)).astype(st2.dtype)
        st3[...] = (dm * _dot(ys2_ref[...], ws5_ref[...]) * ss * (1.0 - ss)).astype(st3.dtype)
        dys2 = _dot(dys, ws5_ref[...], _NT)
        gl_, z = _dot(ysg_ref[...], wglu_ref[...]) + bglu_ref[...], z_ref[...]
        a, b = gl_[:, :S5_WIDTH], gl_[:, S5_WIDTH:]
        sb, sz = _sig(b), _sig(z)
        silu = z * sz
        dglu = jnp.concatenate([dys2 * sb * silu, dys2 * a * silu * sb * (1.0 - sb)], axis=1)
        st1[...] = (dys2 * a * sb * _dsilu(z, sz)).astype(st1.dtype)
        dbglu_ref[...] += jnp.sum(dglu, axis=0, keepdims=True)
        dglu_ref[...] = _mx(dglu)
        dgelu_ref[...] = _dot(dglu, wglu_ref[...], _NT)
        d_o, dg, dgn = _hg_gate_bwd(_dot(dyh, whg_ref[...], _NT), o_ref[...], ghg_ref[...], gn_ref[...])
        do_ref[...] = d_o.astype(do_ref.dtype)
        st0[...] = dg.astype(st0.dtype)
        dgn_ref[...] += dgn
        for cp in writes(i):
            cp.start()

        @pl.when(i == nb - 1)
        def _():
            for cp in writes(i):
                cp.wait()

    tile = lambda wd, cb=0: pl.BlockSpec((tm, wd), functools.partial(lambda i, cb: (i, cb), cb=cb))
    row = lambda wd: pl.BlockSpec((1, wd), lambda i: (0, 0))
    whole = pl.BlockSpec(memory_space=pltpu.VMEM)
    return pl.pallas_call(
        body, name="mix_bwd", grid=(nb,),
        in_specs=[tile(1024), tile(1024), tile(1024), tile(1024), tile(512), tile(512), tile(1024, 3),
                  tile(512, 4608 // 512), tile(1024, 5), tile(1024, 6), tile(1024), row(1024), row(1024), row(1024)]
                 + [whole] * 5,
        out_specs=[tile(1024), tile(1024), tile(1024), tile(1024), tile(512), tile(1024), row(1024), row(1024),
                   row(1024), _HBM],
        out_shape=[jax.ShapeDtypeStruct((t_len, 1024), F32), jax.ShapeDtypeStruct((t_len, 1024), MXU_DTYPE),
                   jax.ShapeDtypeStruct((t_len, 1024), MXU_DTYPE), jax.ShapeDtypeStruct((t_len, 1024), MXU_DTYPE),
                   jax.ShapeDtypeStruct((t_len, 512), F32), jax.ShapeDtypeStruct((t_len, 1024), MXU_DTYPE),
                   jax.ShapeDtypeStruct((1, 1024), F32), jax.ShapeDtypeStruct((1, 1024), F32),
                   jax.ShapeDtypeStruct((1, 1024), F32), jax.ShapeDtypeStruct((t_len, IN_COLS), MXU_DTYPE)],
        scratch_shapes=[pltpu.VMEM((tm, wd), MXU_DTYPE) for _, wd in MIX_BWD_COLS] + [pltpu.SemaphoreType.DMA((4,))],
        compiler_params=_params("arbitrary"))(dgl, h1, dh2, act_hg, ys2, ys_gelu, proj, proj, proj, proj, o_hg, g2, ghn,
                                              b_glu, w["w_ple_gate"], w["w_out"], w["w_o_s5"], w["w_o_hg"],
                                              w["w_glu"])


def _local_step(x, p, target, w, sm, comm=None):
    t_len = x.shape[0]
    tm = min(256, t_len)
    tmm = min(512, t_len)
    tb_hg = min(256, t_len)
    tb_s5 = min(256, t_len)
    g1, g2, g3, ghn = sm["norm_g"], sm["ple_norm_g"], sm["final_norm_g"].reshape(1, D_MODEL), sm["hg_norm_g"]

    def rms_in(xv, g):
        return xv * lax.rsqrt(jnp.mean(xv * xv, axis=-1, keepdims=True) + NORM_EPS) * g

    in_shard = IN_COLS // N_CHIPS
    if comm is None:
        w_in = w["w_in"]
        proj, u = _mm_nn("mm_in", x, w_in, tmm, in_shard, prologue=rms_in, consts=[g1])
    else:
        proj, u, w_in, w = comm.input_projection(x, g1, rms_in, tmm)

    lanes = lambda a: a.reshape(1, S5_LANES)
    a_re, a_im = lanes(sm["s5_a_re"]), lanes(sm["s5_a_im"])
    ldt = lanes(jnp.broadcast_to(sm["s5_log_dt"].reshape(S5_GROUPS, 1), (S5_GROUPS, S5_STATE)))
    to_t = lambda b: b.reshape(S5_GROUPS, S5_STATE, S5_GROUP).transpose(2, 0, 1).reshape(S5_GROUP, S5_LANES)
    b_re_t, b_im_t = to_t(sm["s5_b_re"]), to_t(sm["s5_b_im"])
    scan_fwd, scan_rev, bbr_t, bbi_t = _s5_prep(a_re, a_im, ldt, b_re_t, b_im_t, tb_s5 // SUBLANES)
    from_t = lambda b: b.reshape(S5_GROUP, S5_GROUPS, S5_STATE).transpose(1, 0, 2)
    bbr_bd = _block_diag(from_t(bbr_t)).astype(MXU_DTYPE)
    bbi_bd = _block_diag(from_t(bbi_t)).astype(MXU_DTYPE)
    cr_bd = _block_diag(sm["s5_c_re"].reshape(S5_GROUPS, S5_GROUP, S5_STATE)).astype(MXU_DTYPE)
    ci_bd = _block_diag(sm["s5_c_im"].reshape(S5_GROUPS, S5_GROUP, S5_STATE)).astype(MXU_DTYPE)
    d_row = sm["s5_d"].reshape(1, S5_WIDTH)
    o_hg, act_hg, s_prev = _hgrn2_fwd(proj, sm["hg_lb"], ghn, t_len, tb_hg)
    h_re, h_im, y_pre, ys_gelu = _s5_fwd(proj, *scan_fwd, bbr_bd, bbi_bd,
                                          cr_bd.transpose(0, 2, 1), ci_bd.transpose(0, 2, 1), d_row, t_len, tb_s5)
    def mix_f(act, ysg, z, gh, gs, xv, w_glu, b_glu, w_o_hg, w_o_s5, w_out):
        gl_ = _dot(ysg, w_glu) + b_glu
        a, b = gl_[:, :S5_WIDTH], gl_[:, S5_WIDTH:]
        ys2_ = (a * _sig(b) * (z * _sig(z))).astype(MXU_DTYPE)
        yh, ys = _dot(act, w_o_hg), _dot(ys2_, w_o_s5)
        mg = (_sig(gh) * yh + _sig(gs) * ys).astype(MXU_DTYPE)
        return (ys2_, mg, xv + _dot(mg, w_out))

    ys2, merged, h1 = _rowwise(
        "mix_out", mix_f, t_len, tm,
        [(act_hg, 1024, 0), (ys_gelu, 512, 0), (proj, 512, 4608 // 512), (proj, 1024, 5), (proj, 1024, 6),
         (x, 1024, 0)], [w["w_glu"], sm["b_glu"], w["w_o_hg"], w["w_o_s5"], w["w_out"]],
        [(512, MXU_DTYPE), (1024, MXU_DTYPE), (1024, F32)])

    def head_f(h1v, pv, tgt, g_ple, g, w_ple, w_gate):
        r2 = lax.rsqrt(jnp.mean(h1v * h1v, axis=-1, keepdims=True) + NORM_EPS)
        n2_ = (h1v * r2 * g_ple).astype(MXU_DTYPE)
        glv, pev = _dot(n2_, w_gate), _dot(pv, w_ple)
        gate = _sig(glv)
        h2 = h1v + pev * gate
        r = lax.rsqrt(jnp.mean(h2 * h2, axis=-1, keepdims=True) + NORM_EPS)
        e = h2 * r * g - tgt
        loss = 0.5 * jnp.sum(jnp.mean(e * e, axis=-1, keepdims=True), axis=0, keepdims=True)
        dy = e * (1.0 / D_MODEL)
        dg = jnp.sum(dy * h2 * r, axis=0, keepdims=True)
        t = dy * g
        dh2 = r * t - h2 * (r * r * r) * jnp.mean(t * h2, axis=-1, keepdims=True)
        return (n2_, dh2, dh2 * gate, dh2 * pev * gate * (1.0 - gate), jnp.broadcast_to(loss, (1, 128)), dg)

    n2, dh2, dpe, dgl, loss_row, d_g3 = _rowwise(
        "ple_loss_head", head_f, t_len, tm, [(h1, 1024, 0), (p, 256, 0), (target, 1024, 0)],
        [g2, g3, w["w_ple"], w["w_ple_gate"]],
        [(1024, MXU_DTYPE), (1024, F32), (1024, MXU_DTYPE), (1024, MXU_DTYPE)], accs=[(1, 128), (1, 1024)])

    gb = {}
    gb["w_ple"] = _mm_tn("mm_d_w_ple", p, dpe, tmm, 1024)
    gb["w_ple_gate"] = _mm_tn("mm_d_w_ple_gate", n2, dgl, tmm, 1024)
    dh1, dy_hg, dy_s5, dglu, dgelu, d_o, d_g2, d_bglu, d_ghn, dproj = _mix_bwd(
        dgl, h1, dh2, act_hg, ys2, ys_gelu, proj, o_hg, g2, ghn, sm["b_glu"], w, t_len, tm)
    gb["w_out"] = _mm_tn("mm_d_w_out", merged, dh1, tmm, 1024)
    gb["w_o_s5"] = _mm_tn("mm_d_w_o_s5", ys2, dy_s5, tmm, 1024)
    gb["w_glu"] = _mm_tn("mm_d_w_glu", ys_gelu, dglu, tmm, 1024)
    dproj, d_bbr, d_bbi, d_crt, d_cit, d_d, d_lam = _s5_bwd(dgelu, y_pre, proj, h_re, h_im,
                                                            *scan_rev, bbr_bd, bbi_bd, cr_bd,
                                                            ci_bd, d_row, dproj, t_len, tb_s5)
    to_t3 = lambda b: b.transpose(1, 0, 2).reshape(S5_GROUP, S5_LANES)
    d_are, d_aim, d_ldt, d_br_t, d_bi_t = _s5_prep_bwd(a_re, a_im, ldt, b_re_t, b_im_t, d_lam,
                                                       to_t3(_diag_blocks(d_bbr)), to_t3(_diag_blocks(d_bbi)))
    gb["w_o_hg"] = _mm_tn("mm_d_w_o_hg", act_hg, dy_hg, tmm, 1024)
    if comm is None:
        dproj, d_lb = _hgrn2_bwd(proj, d_o, s_prev, sm["hg_lb"], dproj, t_len, tb_hg)
    else:
        rest_grads = _pack_rest_full(gb)
        dproj, d_lb, rest_theirs = _hgrn2_bwd(proj, d_o, s_prev, sm["hg_lb"], dproj, t_len, tb_hg,
                                               riding=comm.swap(rest_grads))

    def in_b(duv, xv, dh, g):
        dx, dg = _rms_bwd(duv, xv, g)
        return (dh + dx, dg)

    in_args = ("mm_d_u_rms_in_bwd", dproj, w_in, tmm, in_shard, in_b, [(x, 1024, 0), (dh1, 1024, 0)], [g1],
               [(1024, F32)])
    if comm is None:
        gb["w_in"] = _mm_tn("mm_d_w_in", u, dproj, tmm, in_shard, col_shards=True)
        grad_x, d_g1 = _mm_nt_then(*in_args, accs=[(1, 1024)])
    else:
        gb["w_in"], landed = _mm_tn("mm_d_w_in", u, dproj, tmm, in_shard, col_shards=True,
                                    riding=comm.scatter("rest", rest_grads, rest_theirs))
        comm.landed["rest"] = landed
        grad_x, d_g1, landed = _mm_nt_then(*in_args, accs=[(1, 1024)], riding=comm.scatter(
            "in", gb["w_in"].reshape(N_CHIPS, 2, D_MODEL // 2, in_shard)))
        comm.landed["in"] = landed

    back_t = lambda b: b.reshape(S5_GROUP, S5_GROUPS, S5_STATE).transpose(1, 2, 0).reshape(1, S5_GROUPS, S5_STATE,
                                                                                           S5_GROUP)
    gs = {
        "norm_g": d_g1, "hg_lb": d_lb, "hg_norm_g": d_ghn,
        "s5_a_re": d_are.reshape(1, S5_GROUPS, S5_STATE), "s5_a_im": d_aim.reshape(1, S5_GROUPS, S5_STATE),
        "s5_log_dt": d_ldt[0:1, :S5_GROUPS],
        "s5_b_re": back_t(d_br_t), "s5_b_im": back_t(d_bi_t),
        "s5_c_re": _diag_blocks(d_crt.transpose(0, 2, 1)).reshape(1, S5_GROUPS, S5_GROUP, S5_STATE),
        "s5_c_im": _diag_blocks(d_cit.transpose(0, 2, 1)).reshape(1, S5_GROUPS, S5_GROUP, S5_STATE),
        "s5_d": d_d.reshape(1, S5_GROUPS, S5_GROUP), "b_glu": d_bglu, "ple_norm_g": d_g2,
        "final_norm_g": d_g3.reshape(D_MODEL),
    }
    return loss_row, grad_x, gb, gs


def _shard_shape(name):
    r, c = BIG_SHAPE[name]
    return (r, c // N_CHIPS) if name in BIG_COL_SHARDED else (r // N_CHIPS, c)


def _pack_small(parts, last):
    flat = jnp.concatenate([parts[n].reshape(-1) for n in SMALL] + [last.reshape(-1)])
    return jnp.pad(flat, (0, SMALL_ROWS * PACK_W - flat.shape[0])).reshape(SMALL_ROWS, PACK_W)


def _unpack_small(packed):
    flat, out, off = packed.reshape(-1), {}, 0
    for n in SMALL:
        size = 1
        for d in SMALL_SHAPE[n]:
            size *= d
        out[n] = flat[off:off + size].reshape(SMALL_SHAPE[n])
        off += size
    return out, flat[off]


def _place():
    x, y, c = lax.axis_index("x"), lax.axis_index("y"), lax.axis_index("c")
    return x, y, c, [(1 - x, y), (x, 1 - y), (1 - x, 1 - y)]


def _remote(src, dst, send_sems, recv_sems, k, to):
    return pltpu.make_async_remote_copy(src_ref=src, dst_ref=dst, send_sem=send_sems.at[k], recv_sem=recv_sems.at[k],
                                        device_id=to, device_id_type=MESH)


REST = tuple(n for n in BIG if n != "w_in")
REST_ROWS = sum(BIG_SHAPE[n][0] * BIG_SHAPE[n][1] for n in REST) // (N_CHIPS * PACK_W)
IN_SHARD = IN_COLS // N_CHIPS
IN_TILE, REST_TILE = 256, 272


def _pack_rest(parts):
    return jnp.concatenate([parts[n].reshape(-1, PACK_W) for n in REST], axis=0)


def _unpack_rest(packed):
    out, off = {}, 0
    for n in REST:
        r, c = _shard_shape(n)
        rows = r * c // PACK_W
        out[n] = packed[off:off + rows].reshape(1, r, c)
        off += rows
    return out


def _unpack_rest_full(gathered):
    out, off = {}, 0
    for n in REST:
        r, c = _shard_shape(n)
        rows = r * c // PACK_W
        sh = gathered[:, off:off + rows].reshape(N_CHIPS, r, c)
        out[n] = sh.transpose(1, 0, 2).reshape(BIG_SHAPE[n]) if n in BIG_COL_SHARDED else sh.reshape(BIG_SHAPE[n])
        off += rows
    return out


def _pack_rest_full(full):
    parts = []
    for n in REST:
        r, c = _shard_shape(n)
        g = full[n]
        sh = g.reshape(BIG_SHAPE[n][0], N_CHIPS, c).transpose(1, 0, 2) if n in BIG_COL_SHARDED else g
        parts.append(sh.reshape(N_CHIPS, r * c // PACK_W, PACK_W))
    return jnp.concatenate(parts, axis=1).reshape(N_CHIPS, 2, REST_ROWS // 2, PACK_W)


def _swap_halves(pgs, name="exchange_halves"):
    n = len(pgs)

    def body(*refs):
        pg_refs, out_refs, (send_sems, recv_sems) = refs[:n], refs[n:2 * n], refs[2 * n:]
        x, y, c, _ = _place()
        cps = [_remote(pg_ref.at[j, 1 - c], out_ref.at[j], send_sems, recv_sems, N_CHIPS * g + j, (x, y, 1 - c))
               for g, (pg_ref, out_ref) in enumerate(zip(pg_refs, out_refs)) for j in range(N_CHIPS)]
        for cp in cps:
            cp.start()
        for cp in cps:
            cp.wait()

    return pl.pallas_call(
        body, name=name, in_specs=[_HBM] * n, out_specs=[_HBM] * n,
        out_shape=[jax.ShapeDtypeStruct((N_CHIPS,) + pg.shape[2:], pg.dtype) for pg in pgs],
        scratch_shapes=[pltpu.SemaphoreType.DMA((N_CHIPS * n,)), pltpu.SemaphoreType.DMA((N_CHIPS * n,))])(*pgs)


def _share_halves(gs):
    n = len(gs)

    def body(*refs):
        g_refs, out_refs, (send_sems, recv_sems) = refs[:n], refs[n:2 * n], refs[2 * n:]
        x, y, c, _ = _place()
        cps = [_remote(g_ref, out_ref.at[c], send_sems, recv_sems, g, (x, y, 1 - c))
               for g, (g_ref, out_ref) in enumerate(zip(g_refs, out_refs))]
        for cp in cps:
            cp.start()
        for g, (g_ref, out_ref) in enumerate(zip(g_refs, out_refs)):
            _remote(g_ref, out_ref.at[1 - c], send_sems, recv_sems, g, (x, y, 1 - c)).wait_recv()
        for cp in cps:
            cp.wait_send()

    return pl.pallas_call(
        body, name="share_half", in_specs=[_HBM] * n, out_specs=[_HBM] * n,
        out_shape=[jax.ShapeDtypeStruct((2,) + g.shape, g.dtype) for g in gs],
        scratch_shapes=[pltpu.SemaphoreType.DMA((n,)), pltpu.SemaphoreType.DMA((n,))])(*gs)


def _pair_sum(name, pg, theirs, c, tile):
    _, _, rows, width = pg.shape

    def body(c_ref, a_ref, b_ref, o_ref):
        o_ref[...] = (a_ref[...] + b_ref[...]).astype(o_ref.dtype)

    return pl.pallas_call(
        body, name=name,
        grid_spec=pltpu.PrefetchScalarGridSpec(
            num_scalar_prefetch=1, grid=(N_CHIPS, rows // tile),
            in_specs=[pl.BlockSpec((None, None, tile, width), lambda j, i, c_ref: (j, c_ref[0], i, 0)),
                      pl.BlockSpec((None, tile, width), lambda j, i, c_ref: (j, i, 0))],
            out_specs=pl.BlockSpec((None, tile, width), lambda j, i, c_ref: (j, i, 0))),
        out_shape=jax.ShapeDtypeStruct((N_CHIPS, rows, width), WIRE_DTYPE),
        compiler_params=_params("arbitrary", "arbitrary"))(c.reshape(1), pg, theirs)


def _chip_sum(name, ps, others, k, tile):
    _, rows, width = ps.shape

    def body(k_ref, a_ref, b_ref, o_ref):
        o_ref[...] = ((a_ref[...].astype(F32) + b_ref[0].astype(F32)) + b_ref[1].astype(F32)) + b_ref[2].astype(F32)

    return pl.pallas_call(
        body, name=name,
        grid_spec=pltpu.PrefetchScalarGridSpec(
            num_scalar_prefetch=1, grid=(rows // tile,),
            in_specs=[pl.BlockSpec((None, tile, width), lambda i, k_ref: (k_ref[0], i, 0)),
                      pl.BlockSpec((3, tile, width), lambda i, k_ref: (0, i, 0))],
            out_specs=pl.BlockSpec((tile, width), lambda i, k_ref: (i, 0))),
        out_shape=jax.ShapeDtypeStruct((rows, width), F32),
        compiler_params=_params("arbitrary"))(k.reshape(1), ps, others)


def _mm_in_gathering(x, g1, prologue, in_wire, chip, riding, tm):
    m, k = x.shape
    half, ns = in_wire.shape[1:]
    nrow = m // tm
    r_in, r_out = len(riding.ins), len(riding.outs)

    def flip(j):
        return jnp.where(j == 1, 2, jnp.where(j == 2, 1, j))

    def body(k_ref, x_ref, g_ref, wire_ref, *rest):
        rins, (proj_ref, u_ref, all_ref) = rest[:r_in], rest[r_in:r_in + 3]
        routs = rest[r_in + 3:r_in + 3 + r_out]
        kept, b_ref, load_sems, send_sems, recv_sems, ride_send, ride_recv = rest[r_in + 3 + r_out:]
        j, i = pl.program_id(0), pl.program_id(1)
        px, py, c, chips = _place()
        sibling = (px, py, 1 - c)

        def over_ici(r, chip_slot):
            cx, cy = chips[r]
            return _remote(wire_ref.at[c], all_ref.at[chip_slot, c], send_sems, recv_sems, r, (cx, cy, c))

        def to_sibling(r, half_slot):
            cx, cy = chips[r]
            return _remote(all_ref.at[2 * cx + cy, c], all_ref.at[2 * cx + cy, half_slot], send_sems, recv_sems,
                           3 + r, sibling)

        def load(src):
            cps = [pltpu.make_async_copy(src.at[h], b_ref.at[pl.ds(h * half, half)], load_sems.at[h])
                   for h in range(2)]
            for cp in cps:
                cp.start()
            for cp in cps:
                cp.wait()

        @pl.when((j == 0) & (i == 0))
        def _():
            for r in range(3):
                over_ici(r, 2 * px + py).start()
            riding.start(rins, routs, ride_send, ride_recv)
            load(wire_ref)

        for r in range(3):
            @pl.when((j == r + 1) & (i == 0))
            def _(r=r):
                cx, cy = chips[r]
                over_ici(r, 2 * cx + cy).wait_recv()
                to_sibling(r, c).start()
                to_sibling(r, 1 - c).wait_recv()
                load(all_ref.at[2 * cx + cy])

        rows = pl.ds(pl.multiple_of(i * tm, tm), tm)

        @pl.when(j == 0)
        def _():
            tile = _mx(prologue(x_ref[...], g_ref[...]))
            kept[rows, :] = tile
            u_ref[...] = tile

        proj_ref[...] = _dot(kept[rows, :], b_ref[...])

        @pl.when((j == N_CHIPS - 1) & (i == nrow - 1))
        def _():
            for r in range(3):
                over_ici(r, 2 * px + py).wait_send()
                to_sibling(r, c).wait_send()
            riding.wait(rins, routs, ride_send, ride_recv)

    once = lambda j, i, k_ref: (jnp.where(j == 0, i, nrow - 1), 0)
    return pl.pallas_call(
        body, name="mm_in",
        grid_spec=pltpu.PrefetchScalarGridSpec(
            num_scalar_prefetch=1, grid=(N_CHIPS, nrow),
            in_specs=[pl.BlockSpec((tm, k), once), pl.BlockSpec(g1.shape, lambda j, i, k_ref: (0, 0)), _HBM]
                     + [_HBM] * r_in,
            out_specs=[pl.BlockSpec((tm, ns), lambda j, i, k_ref: (i, jnp.bitwise_xor(k_ref[0], flip(j)))),
                       pl.BlockSpec((tm, k), once), _HBM] + [_HBM] * r_out,
            scratch_shapes=[pltpu.VMEM((m, k), MXU_DTYPE), pltpu.VMEM((2 * half, ns), in_wire.dtype),
                            pltpu.SemaphoreType.DMA((2,)), pltpu.SemaphoreType.DMA((6,)),
                            pltpu.SemaphoreType.DMA((6,)), pltpu.SemaphoreType.DMA((riding.n_sems,)),
                            pltpu.SemaphoreType.DMA((riding.n_sems,))]),
        out_shape=[jax.ShapeDtypeStruct((m, N_CHIPS * ns), F32), jax.ShapeDtypeStruct((m, k), MXU_DTYPE),
                   jax.ShapeDtypeStruct((N_CHIPS,) + in_wire.shape, in_wire.dtype)] + list(riding.outs),
        compiler_params=_params("arbitrary", "arbitrary"))(chip.reshape(1), x, g1, in_wire, *riding.ins)


class _StepComm:
    TILES = {"in": IN_TILE, "rest": REST_TILE}

    def __init__(self, in_wire, rest_wire, chip, core):
        self.in_wire, self.rest_wire, self.chip, self.core = in_wire, rest_wire, chip, core
        self.sums, self.landed = {}, {}

    def input_projection(self, x, g1, prologue, tm):
        proj, u, shards, landed = _mm_in_gathering(x, g1, prologue, self.in_wire, self.chip, self.gather_rest(), tm)
        shards = lax.dynamic_update_slice(shards, self.in_wire[None], (self.chip, 0, 0, 0))
        return proj, u, shards.reshape(N_CHIPS, D_MODEL, IN_SHARD), self.rest_weights(landed)

    def gather_rest(self):
        wire = self.rest_wire

        def sends(ins, outs, send_sems, recv_sems):
            (w_ref,), (out_ref,) = ins, outs
            x, y, c, chips = _place()
            return [_remote(w_ref.at[c], out_ref.at[2 * x + y, c], send_sems, recv_sems, 4 * j + 2 * c + to,
                            (cx, cy, to)) for j, (cx, cy) in enumerate(chips) for to in (0, 1)]

        def recvs(ins, outs, send_sems, recv_sems):
            (w_ref,), (out_ref,) = ins, outs
            _, _, c, chips = _place()
            return [_remote(w_ref.at[c], out_ref.at[2 * cx + cy, by], send_sems, recv_sems, 4 * j + 2 * by + c,
                            (cx, cy, by)) for j, (cx, cy) in enumerate(chips) for by in (0, 1)]

        def start(*refs):
            for cp in sends(*refs):
                cp.start()

        def wait(*refs):
            for cp in recvs(*refs):
                cp.wait_recv()
            for cp in sends(*refs):
                cp.wait_send()

        return _Riding((wire,), (jax.ShapeDtypeStruct((N_CHIPS,) + wire.shape, wire.dtype),), 12, start, wait)

    def rest_weights(self, landed):
        full = lax.dynamic_update_slice(landed, self.rest_wire[None], (self.chip, 0, 0, 0))
        return _unpack_rest_full(full.reshape(N_CHIPS, REST_ROWS, PACK_W))

    def swap(self, pg):
        def copies(ins, outs, send_sems, recv_sems):
            (pg_ref,), (out_ref,) = ins, outs
            x, y, c, _ = _place()
            return [_remote(pg_ref.at[j, 1 - c], out_ref.at[j], send_sems, recv_sems, j, (x, y, 1 - c))
                    for j in range(N_CHIPS)]

        def start(*refs):
            for cp in copies(*refs):
                cp.start()

        def wait(*refs):
            for cp in copies(*refs):
                cp.wait()

        return _Riding((pg,), (jax.ShapeDtypeStruct((N_CHIPS,) + pg.shape[2:], pg.dtype),), N_CHIPS, start, wait)

    def scatter(self, group, pg, theirs=None):
        if theirs is None:
            (theirs,) = _swap_halves([pg], "exchange_halves_" + group)
        ps = _pair_sum("sum_pair_" + group, pg, theirs, self.core, self.TILES[group])
        self.sums[group] = ps

        def copies(ins, outs, send_sems, recv_sems):
            (ps_ref,), (out_ref,) = ins, outs
            _, _, c, chips = _place()
            return [_remote(ps_ref.at[2 * cx + cy], out_ref.at[j], send_sems, recv_sems, j, (cx, cy, c))
                    for j, (cx, cy) in enumerate(chips)]

        def start(*refs):
            for cp in copies(*refs):
                cp.start()

        def wait(*refs):
            for cp in copies(*refs):
                cp.wait()

        return _Riding((ps,), (jax.ShapeDtypeStruct((3,) + ps.shape[1:], ps.dtype),), 3, start, wait)

    def reduced(self, group):
        return _chip_sum("sum_chips_" + group, self.sums[group], self.landed[group], self.chip, self.TILES[group])


def _adamw(w, g, m, v):
    m = ADAM_B1 * m + (1.0 - ADAM_B1) * g
    v = ADAM_B2 * v + (1.0 - ADAM_B2) * (g * g)
    m_hat = m / (1.0 - ADAM_B1 ** ADAM_STEP)
    v_hat = v / (1.0 - ADAM_B2 ** ADAM_STEP)
    return -ADAM_LR * (m_hat / (jnp.sqrt(v_hat) + ADAM_EPS) + ADAM_WD * w), m, v


def _small_reduce_adamw(part, w, m, v):
    def body(part_ref, w_ref, m_ref, v_ref, g_ref, d_ref, nm_ref, nv_ref, all_ref, send_sems, recv_sems):
        x, y, c, chips = _place()
        me, sibling = (x, y, c), (x, y, 1 - c)

        def rows(px, py, pc):
            return all_ref.at[4 * px + 2 * py + pc]

        all_ref[4 * x + 2 * y + c] = part_ref[...]
        first = [_remote(part_ref, rows(*me), send_sems, recv_sems, 0, sibling)]
        first += [_remote(part_ref, rows(*me), send_sems, recv_sems, 1 + j, (cx, cy, c))
                  for j, (cx, cy) in enumerate(chips)]
        for cp in first:
            cp.start()
        passed = []
        for j, (cx, cy) in enumerate(chips):
            _remote(part_ref, rows(cx, cy, c), send_sems, recv_sems, 1 + j, me).wait_recv()
            cp = _remote(rows(cx, cy, c), rows(cx, cy, c), send_sems, recv_sems, 4 + j, sibling)
            cp.start()
            passed.append(cp)
        _remote(part_ref, rows(*sibling), send_sems, recv_sems, 0, me).wait_recv()
        for j, (cx, cy) in enumerate(chips):
            _remote(part_ref, rows(cx, cy, 1 - c), send_sems, recv_sems, 4 + j, me).wait_recv()
        for cp in first + passed:
            cp.wait_send()
        g = all_ref[0]
        for dev in range(1, N_DEV):
            g = g + all_ref[dev]
        delta, nm, nv = _adamw(w_ref[...], g, m_ref[...], v_ref[...])
        g_ref[...] = g
        d_ref[...] = delta
        nm_ref[...] = nm
        nv_ref[...] = nv

    whole = pl.BlockSpec(memory_space=pltpu.VMEM)
    shape = jax.ShapeDtypeStruct((SMALL_ROWS, PACK_W), F32)
    return pl.pallas_call(
        body, name="small_reduce_adamw", in_specs=[whole] * 4, out_specs=[whole] * 4, out_shape=[shape] * 4,
        scratch_shapes=[pltpu.VMEM((N_DEV, SMALL_ROWS, PACK_W), F32), pltpu.SemaphoreType.DMA((7,)),
                        pltpu.SemaphoreType.DMA((7,))],
        compiler_params=pltpu.CompilerParams(vmem_limit_bytes=VMEM_LIMIT))(part, w, m, v)


def kernel(x, p, norm_g, w_in, hg_lb, hg_norm_g, w_o_hg, s5_a_re, s5_a_im, s5_log_dt, s5_b_re, s5_b_im, s5_c_re, s5_c_im, s5_d, w_glu, b_glu, w_o_s5, w_out, ple_norm_g, w_ple, w_ple_gate, final_norm_g, loss_target, m_norm_g, m_w_in, m_hg_lb, m_hg_norm_g, m_w_o_hg, m_s5_a_re, m_s5_a_im, m_s5_log_dt, m_s5_b_re, m_s5_b_im, m_s5_c_re, m_s5_c_im, m_s5_d, m_w_glu, m_b_glu, m_w_o_s5, m_w_out, m_ple_norm_g, m_w_ple, m_w_ple_gate, m_final_norm_g, v_norm_g, v_w_in, v_hg_lb, v_hg_norm_g, v_w_o_hg, v_s5_a_re, v_s5_a_im, v_s5_log_dt, v_s5_b_re, v_s5_b_im, v_s5_c_re, v_s5_c_im, v_s5_d, v_w_glu, v_b_glu, v_w_o_s5, v_w_out, v_ple_norm_g, v_w_ple, v_w_ple_gate, v_final_norm_g):
    given = dict(locals())
    wts = {n: given[n] for n in WEIGHTS}
    mom = {n: given["m_" + n] for n in WEIGHTS}
    var = {n: given["v_" + n] for n in WEIGHTS}
    cx, cy, cc = lax.axis_index("x"), lax.axis_index("y"), lax.axis_index("c")
    chip = (2 * cx + cy).astype(jnp.int32)

    core = cc.astype(jnp.int32)
    rest_shard = _pack_rest({n: wts[n][0] for n in REST})
    comm = _StepComm(wts["w_in"][0].astype(MXU_DTYPE).reshape(2, D_MODEL // 2, IN_SHARD),
                     rest_shard.astype(MXU_DTYPE).reshape(2, REST_ROWS // 2, PACK_W), chip, core)

    t_len = x.shape[1]
    loss_row, grad_x, g_big, g_small = _local_step(x.reshape(t_len, D_MODEL), p.reshape(t_len, -1),
                                                   loss_target.reshape(t_len, D_MODEL), None,
                                                   {n: wts[n] for n in SMALL}, comm)

    zero = jnp.zeros((), F32)
    sg, sd, snm, snv = _small_reduce_adamw(_pack_small(g_small, loss_row[0, 0]),
                                           _pack_small({n: wts[n] for n in SMALL}, zero),
                                           _pack_small({n: mom[n] for n in SMALL}, zero),
                                           _pack_small({n: var[n] for n in SMALL}, zero))
    (sg, loss), (sd, _), (snm, _), (snv, _) = (_unpack_small(a) for a in (sg, sd, snm, snv))

    halves = [comm.reduced("in"), comm.reduced("rest")]
    g_in, g_rest = [lax.dynamic_update_slice(got, mine[None], (core, 0, 0))
                    for got, mine in zip(_share_halves(halves), halves)]
    g_in, g_rest = g_in.reshape(D_MODEL, IN_SHARD), g_rest.reshape(REST_ROWS, PACK_W)

    def adam_f(wv, gv, mv, vv):
        return _adamw(wv, gv, mv, vv)

    d_in, nm_in, nv_in = _rowwise("adamw_in", adam_f, D_MODEL, IN_TILE,
                                  [(wts["w_in"][0], IN_SHARD, 0), (g_in, IN_SHARD, 0), (mom["w_in"][0], IN_SHARD, 0),
                                   (var["w_in"][0], IN_SHARD, 0)], [], [(IN_SHARD, F32)] * 3)
    d_rest, nm_rest, nv_rest = _rowwise("adamw_rest", adam_f, REST_ROWS, REST_TILE,
                                        [(rest_shard, PACK_W, 0), (g_rest, PACK_W, 0),
                                         (_pack_rest({n: mom[n][0] for n in REST}), PACK_W, 0),
                                         (_pack_rest({n: var[n][0] for n in REST}), PACK_W, 0)], [],
                                        [(PACK_W, F32)] * 3)
    bg, bd, bnm, bnv = (dict(_unpack_rest(rest), w_in=a.reshape(1, D_MODEL, IN_SHARD))
                        for rest, a in ((g_rest, g_in), (d_rest, d_in), (nm_rest, nm_in), (nv_rest, nv_in)))

    outs = [loss, grad_x.reshape(x.shape)]
    for small, big in ((sg, bg), (sd, bd), (snm, bnm), (snv, bnv)):
        outs += [big[n] if n in BIG else small[n] for n in WEIGHTS]
    return tuple(outs)
```

```python
import functools
from typing import Callable, NamedTuple

import jax
import jax.numpy as jnp
from jax import lax
from jax.experimental import pallas as pl
from jax.experimental.pallas import tpu as pltpu

F32 = jnp.float32
MXU_DTYPE = jnp.bfloat16
WIRE_DTYPE = jnp.bfloat16
NORM_EPS = 1e-6
D_MODEL = 1024
HG_HEADS = 8
HG_DIM = 128
HG_CHUNK = 64
S5_WIDTH = 512
S5_GROUPS = 32
S5_GROUP = 16
S5_STATE = 64
S5_LANES = S5_GROUPS * S5_STATE
IN_COLS = 7168
SUBLANES = 8
VMEM_LIMIT = 56 * 1024 * 1024
HIGHEST = lax.Precision.HIGHEST
MESH = pl.DeviceIdType.MESH

ADAM_LR, ADAM_B1, ADAM_B2, ADAM_EPS, ADAM_WD, ADAM_STEP = 0.001, 0.9, 0.999, 1e-08, 0.01, 10

BIG = ("w_in", "w_o_hg", "w_glu", "w_o_s5", "w_out", "w_ple", "w_ple_gate")
BIG_SHAPE = {"w_in": (1024, 7168), "w_o_hg": (1024, 1024), "w_glu": (512, 1024), "w_o_s5": (512, 1024),
             "w_out": (1024, 1024), "w_ple": (256, 1024), "w_ple_gate": (1024, 1024)}
BIG_COL_SHARDED = ("w_in", "w_glu", "w_o_s5", "w_ple")
SMALL = ("norm_g", "hg_lb", "hg_norm_g", "s5_a_re", "s5_a_im", "s5_log_dt", "s5_b_re", "s5_b_im", "s5_c_re",
         "s5_c_im", "s5_d", "b_glu", "ple_norm_g", "final_norm_g")
SMALL_SHAPE = {"norm_g": (1, 1024), "hg_lb": (2, 1024), "hg_norm_g": (1, 1024), "s5_a_re": (1, 32, 64),
               "s5_a_im": (1, 32, 64), "s5_log_dt": (1, 32), "s5_b_re": (1, 32, 64, 16), "s5_b_im": (1, 32, 64, 16),
               "s5_c_re": (1, 32, 16, 64), "s5_c_im": (1, 32, 16, 64), "s5_d": (1, 32, 16), "b_glu": (1, 1024),
               "ple_norm_g": (1, 1024), "final_norm_g": (1024,)}
WEIGHTS = ("norm_g", "w_in", "hg_lb", "hg_norm_g", "w_o_hg", "s5_a_re", "s5_a_im", "s5_log_dt", "s5_b_re", "s5_b_im",
           "s5_c_re", "s5_c_im", "s5_d", "w_glu", "b_glu", "w_o_s5", "w_out", "ple_norm_g", "w_ple", "w_ple_gate",
           "final_norm_g")
N_CHIPS = 4
N_DEV = 8
PACK_W = 1024
SMALL_ROWS = 144


def _params(*sem):
    return pltpu.CompilerParams(dimension_semantics=sem, vmem_limit_bytes=VMEM_LIMIT)


def _sig(x):
    return 1.0 / (1.0 + jnp.exp(-x))


def _dsilu(z, s):
    return s * (1.0 + z * (1.0 - s))


def _mx(x):
    return x.astype(MXU_DTYPE)


def _dot(a, b, dims=(((1,), (0,)), ((), ()))):
    return lax.dot_general(_mx(a), _mx(b), dims, preferred_element_type=F32)


_NT = (((1,), (1,)), ((), ()))
_TN = (((0,), (0,)), ((), ()))


def _dot32(a, b):
    return jnp.dot(a, b, precision=HIGHEST, preferred_element_type=F32)


def _rms_bwd(dy, x, g):
    r = lax.rsqrt(jnp.mean(x * x, axis=-1, keepdims=True) + NORM_EPS)
    t = dy * g
    dx = r * t - x * (r * r * r) * jnp.mean(t * x, axis=-1, keepdims=True)
    return dx, jnp.sum(dy * x * r, axis=0, keepdims=True)


def _rowwise(name, fn, n_rows_total, tm, rows, consts, outs, accs=(), alias=None):
    n_r, n_c, n_o, n_a = len(rows), len(consts), len(outs), len(accs)

    def body(*refs):
        row_refs = refs[:n_r]
        const_refs = refs[n_r:n_r + n_c]
        pos = n_r + n_c + (1 if alias is not None else 0)
        out_refs = refs[pos:pos + n_o]
        acc_refs = refs[pos + n_o:pos + n_o + n_a]
        res = fn(*[r[...] for r in row_refs], *[r[...] for r in const_refs])
        for r, v in zip(out_refs, res[:n_o]):
            r[...] = v.astype(r.dtype)
        if n_a:
            @pl.when(pl.program_id(0) == 0)
            def _():
                for r in acc_refs:
                    r[...] = jnp.zeros_like(r)
            for r, v in zip(acc_refs, res[n_o:]):
                r[...] += v

    in_specs = [pl.BlockSpec((tm, w), functools.partial(lambda i, cb: (i, cb), cb=cb)) for (_, w, cb) in rows]
    in_specs += [pl.BlockSpec(c.shape, lambda i: (0, 0)) for c in consts]
    args = [a for (a, _, _) in rows] + list(consts)
    out_shape, out_specs = [], []
    for o in outs:
        w, dt = o[0], o[1]
        cb, total = (o[2], o[3]) if len(o) == 4 else (0, w)
        out_shape.append(jax.ShapeDtypeStruct((n_rows_total, total), dt))
        out_specs.append(pl.BlockSpec((tm, w), functools.partial(lambda i, cb: (i, cb), cb=cb)))
    io_alias = {}
    if alias is not None:
        in_specs.append(pl.BlockSpec(memory_space=pl.ANY))
        args.append(alias[0])
        io_alias = {len(args) - 1: alias[1]}
    for (r, w) in accs:
        out_shape.append(jax.ShapeDtypeStruct((r, w), F32))
        out_specs.append(pl.BlockSpec((r, w), lambda i: (0, 0)))
    res = pl.pallas_call(body, name=name, grid=(n_rows_total // tm,), in_specs=in_specs, out_specs=out_specs,
                         out_shape=out_shape, input_output_aliases=io_alias,
                         compiler_params=_params("arbitrary"))(*args)
    return res


class _Riding(NamedTuple):
    ins: tuple
    outs: tuple
    n_sems: int
    start: Callable
    wait: Callable


_HBM = pl.BlockSpec(memory_space=pl.ANY)


def _ride(riding, refs, n_in, n_out, n_scratch, first, last):
    if riding is None:
        return refs[:n_in], refs[n_in:n_in + n_out], refs[n_in + n_out:]
    r_in, r_out = len(riding.ins), len(riding.outs)
    ins, rins = refs[:n_in], refs[n_in:n_in + r_in]
    pos = n_in + r_in
    outs, routs = refs[pos:pos + n_out], refs[pos + n_out:pos + n_out + r_out]
    pos += n_out + r_out
    scratch, (send_sems, recv_sems) = refs[pos:pos + n_scratch], refs[pos + n_scratch:]

    @pl.when(first)
    def _():
        riding.start(rins, routs, send_sems, recv_sems)

    @pl.when(last)
    def _():
        riding.wait(rins, routs, send_sems, recv_sems)

    return ins, outs, scratch


def _riding_call(riding, body, name, grid, in_specs, args, out_specs, out_shape, scratch, io_alias=None):
    if riding is not None:
        in_specs = list(in_specs) + [_HBM] * len(riding.ins)
        args = list(args) + list(riding.ins)
        out_specs = list(out_specs) + [_HBM] * len(riding.outs)
        out_shape = list(out_shape) + list(riding.outs)
        scratch = list(scratch) + [pltpu.SemaphoreType.DMA((riding.n_sems,))] * 2
    return pl.pallas_call(body, name=name, grid=grid, in_specs=in_specs, out_specs=out_specs, out_shape=out_shape,
                          scratch_shapes=scratch, input_output_aliases=io_alias or {},
                          compiler_params=_params(*(["arbitrary"] * len(grid))))(*args)


def _mm_nn(name, a, b, tm, tn, riding=None, prologue=None, consts=()):
    m, k = a.shape
    n = b.shape[1] if b.ndim == 2 else b.shape[0] * b.shape[2]
    grid = (n // tn, m // tm)
    n_out, scratch = (1, []) if prologue is None else (2, [pltpu.VMEM((m, k), MXU_DTYPE)])

    def body(*refs):
        j, i = pl.program_id(0), pl.program_id(1)
        ins, outs, kept = _ride(riding, refs, 2 + len(consts), n_out, len(scratch), (j == 0) & (i == 0),
                                (j == grid[0] - 1) & (i == grid[1] - 1))
        if prologue is None:
            left = ins[0][...]
        else:
            rows = pl.ds(pl.multiple_of(i * tm, tm), tm)

            @pl.when(j == 0)
            def _():
                tile = _mx(prologue(ins[0][...], *[c[...] for c in ins[2:]]))
                kept[0][rows, :] = tile
                outs[1][...] = tile

            left = kept[0][rows, :]
        outs[0][...] = _dot(left, ins[1][...])

    once = (lambda j, i: (i, 0)) if prologue is None else (lambda j, i: (jnp.where(j == 0, i, grid[1] - 1), 0))
    b_spec = (pl.BlockSpec((k, tn), lambda j, i: (0, j)) if b.ndim == 2
              else pl.BlockSpec((None, k, tn), lambda j, i: (j, 0, 0)))
    in_specs = [pl.BlockSpec((tm, k), once), b_spec]
    in_specs += [pl.BlockSpec(c.shape, lambda j, i: (0, 0)) for c in consts]
    out_specs = [pl.BlockSpec((tm, tn), lambda j, i: (i, j))]
    out_shape = [jax.ShapeDtypeStruct((m, n), F32)]
    if prologue is not None:
        out_specs.append(pl.BlockSpec((tm, k), once))
        out_shape.append(jax.ShapeDtypeStruct((m, k), MXU_DTYPE))
    res = _riding_call(riding, body, name, grid, in_specs, [a, b] + list(consts), out_specs, out_shape, scratch)
    return res[0] if riding is None and prologue is None else res


def _mm_nt_then(name, a, b, tm, tn, fn, rows, consts, outs, accs=(), alias=None, riding=None):
    m, n = a.shape
    k = b.shape[-2]
    steps = n // tn
    n_r, n_c, n_o, n_a = len(rows), len(consts), len(outs), len(accs)

    def body(*refs):
        a_ref, b_ref = refs[:2]
        row_refs = refs[2:2 + n_r]
        const_refs = refs[2 + n_r:2 + n_r + n_c]
        i, s = pl.program_id(0), pl.program_id(1)
        n_in = 2 + n_r + n_c + (1 if alias is not None else 0)
        _, outs_, (mm_ref,) = _ride(riding, refs, n_in, n_o + n_a, 1, (i == 0) & (s == 0),
                                    (i == m // tm - 1) & (s == steps - 1))
        out_refs, acc_refs = outs_[:n_o], outs_[n_o:]
        part = _dot(a_ref[...], b_ref[...], _NT)
        if steps > 1:
            @pl.when(s == 0)
            def _():
                mm_ref[...] = jnp.zeros_like(mm_ref)
            mm_ref[...] += part

        @pl.when(s == steps - 1)
        def _():
            res = fn(mm_ref[...] if steps > 1 else part, *[r[...] for r in row_refs], *[r[...] for r in const_refs])
            for r, v in zip(out_refs, res[:n_o]):
                r[...] = v.astype(r.dtype)
            if n_a:
                @pl.when(i == 0)
                def _():
                    for r in acc_refs:
                        r[...] = jnp.zeros_like(r)
                for r, v in zip(acc_refs, res[n_o:]):
                    r[...] += v

    b_spec = (pl.BlockSpec((k, tn), lambda i, s: (0, s)) if b.ndim == 2
              else pl.BlockSpec((None, k, tn), lambda i, s: (s, 0, 0)))
    in_specs = [pl.BlockSpec((tm, tn), lambda i, s: (i, s)), b_spec]
    in_specs += [pl.BlockSpec((tm, w), functools.partial(lambda i, s, cb: (i, cb), cb=cb)) for (_, w, cb) in rows]
    in_specs += [pl.BlockSpec(c.shape, lambda i, s: (0, 0)) for c in consts]
    args = [a, b] + [r[0] for r in rows] + list(consts)
    out_shape, out_specs = [], []
    for o in outs:
        w, dt = o[0], o[1]
        cb, total = (o[2], o[3]) if len(o) == 4 else (0, w)
        out_shape.append(jax.ShapeDtypeStruct((m, total), dt))
        out_specs.append(pl.BlockSpec((tm, w), functools.partial(lambda i, s, cb: (i, cb), cb=cb)))
    io_alias = {}
    if alias is not None:
        in_specs.append(pl.BlockSpec(memory_space=pl.ANY))
        args.append(alias[0])
        io_alias = {len(args) - 1: alias[1]}
    for (r, w) in accs:
        out_shape.append(jax.ShapeDtypeStruct((r, w), F32))
        out_specs.append(pl.BlockSpec((r, w), lambda i, s: (0, 0)))
    return _riding_call(riding, body, name, (m // tm, steps), in_specs, args, out_specs, out_shape,
                        [pltpu.VMEM((tm, k), F32)], io_alias)


def _mm_tn(name, a, b, tk, tn, col_shards=False, riding=None):
    t, k = a.shape
    n = b.shape[1]
    steps = t // tk

    def body(*refs):
        j, s = pl.program_id(0), pl.program_id(1)
        (a_ref, b_ref), (o_ref,), (acc_ref,) = _ride(riding, refs, 2, 1, 1, (j == 0) & (s == 0),
                                                     (j == n // tn - 1) & (s == steps - 1))

        @pl.when(s == 0)
        def _():
            acc_ref[...] = jnp.zeros_like(acc_ref)

        acc_ref[...] += _dot(a_ref[...], b_ref[...], _TN)

        @pl.when(s == steps - 1)
        def _():
            o_ref[...] = acc_ref[...]

    if col_shards:
        out_spec = pl.BlockSpec((None, k, tn), lambda j, s: (j, 0, 0))
        out_shape = jax.ShapeDtypeStruct((n // tn, k, tn), F32)
    else:
        out_spec = pl.BlockSpec((k, tn), lambda j, s: (0, j))
        out_shape = jax.ShapeDtypeStruct((k, n), F32)
    res = _riding_call(riding, body, name, (n // tn, steps),
                       [pl.BlockSpec((tk, k), lambda j, s: (s, 0)), pl.BlockSpec((tk, tn), lambda j, s: (s, j))],
                       [a, b], [out_spec], [out_shape], [pltpu.VMEM((k, tn), F32)])
    return res[0] if riding is None else res


def _dot01(m01, x):
    m = m01.astype(MXU_DTYPE)
    hi = x.astype(MXU_DTYPE)
    r1 = x - hi.astype(F32)
    mid = r1.astype(MXU_DTYPE)
    lo = (r1 - mid.astype(F32)).astype(MXU_DTYPE)
    dot = lambda v: jnp.dot(m, v, preferred_element_type=F32)
    return dot(hi) + dot(mid) + dot(lo)


def _chunk_rows(x, offset, nck):
    return jnp.concatenate([jnp.broadcast_to(x[c * HG_CHUNK + offset:c * HG_CHUNK + offset + 1, :],
                                             (HG_CHUNK, x.shape[1])) for c in range(nck)], axis=0)


def _hg_block_terms(q, f, lb, tb):
    nck = tb // HG_CHUNK
    sig = _sig(f)
    fv = lb + (1.0 - lb) * sig
    kk = (1.0 - lb) * (1.0 - sig)
    row = lax.broadcasted_iota(jnp.int32, (tb, tb), 0)
    col = lax.broadcasted_iota(jnp.int32, (tb, tb), 1)
    same = jnp.right_shift(row, 6) == jnp.right_shift(col, 6)
    causal, anti = same & (row >= col), same & (row <= col)
    b = _dot01(causal, jnp.log(fv))
    b_mid, b_last = _chunk_rows(b, HG_CHUNK // 2 - 1, nck), _chunk_rows(b, HG_CHUNK - 1, nck)
    e_mid, e_mid_inv = jnp.exp(b - b_mid), jnp.exp(b_mid - b)
    e_b, e_last = jnp.exp(b), jnp.exp(b_last - b)
    dcs = [jnp.exp(b[c * HG_CHUNK + HG_CHUNK - 1:(c + 1) * HG_CHUNK, :]) for c in range(nck)]
    return sig, fv, kk, causal, anti, e_mid, e_mid_inv, e_b, e_last, dcs


def _hgrn2_fwd(proj, hg_lb, hg_norm_g, t_len, tb, riding=None):
    nck = tb // HG_CHUNK
    nb = t_len // tb

    def body(*refs):
        step = pl.program_id(0)
        ((p_ref, lb_ref, gn_ref), (o_ref, act_ref, sp_ref),
         (st_ref, a_s, bm_s, qd_s, kd_s, v_s, sc_s, inc_s)) = _ride(riding, refs, 3, 3, 8, step == 0, step == nb - 1)

        @pl.when(pl.program_id(0) == 0)
        def _():
            st_ref[...] = jnp.zeros_like(st_ref)

        lb = _sig(lb_ref[0:1, :] - lb_ref[1:2, :])
        q = p_ref[:, pl.ds(0, 1024)]
        _, _, kk, causal, _, e_mid, e_mid_inv, e_b, e_last, dcs = _hg_block_terms(q, p_ref[:, pl.ds(1024, 1024)],
                                                                                   lb, tb)
        a_s[...] = _mx(q * e_mid)
        bm_s[...] = _mx(kk * e_mid_inv)
        qd_s[...] = _mx(q * e_b)
        kd_s[...] = _mx(kk * e_last)
        v_s[...] = _mx(p_ref[:, pl.ds(2048, 1024)])
        heads = [pl.ds(h * HG_DIM, HG_DIM) for h in range(HG_HEADS)]
        chunks = [pl.ds(c * HG_CHUNK, HG_CHUNK) for c in range(nck)]
        for h, hs in enumerate(heads):
            sc_s[h] = _mx(jnp.where(causal, _dot(a_s[:, hs], bm_s[:, hs], _NT), 0.0))
        for h, hs in enumerate(heads):
            o_ref[:, hs] = _dot(sc_s[h], v_s[:, hs])
        for h, hs in enumerate(heads):
            for c, r in enumerate(chunks):
                inc_s[h, c] = _dot(v_s[r, hs], kd_s[r, hs], _TN)
        for c in range(nck):
            for h in range(HG_HEADS):
                st = st_ref[h]
                sp_ref[h, c] = st
                st_ref[h] = dcs[c][:, h * HG_DIM:(h + 1) * HG_DIM] * st + inc_s[h, c]
        for c, r in enumerate(chunks):
            for h, hs in enumerate(heads):
                o_ref[r, hs] += _dot(qd_s[r, hs], sp_ref[h, c], _NT)
        for h, hs in enumerate(heads):
            o = o_ref[:, hs]
            rr = lax.rsqrt(jnp.mean(o * o, axis=-1, keepdims=True) + NORM_EPS)
            g = p_ref[:, pl.ds(3072 + h * HG_DIM, HG_DIM)]
            act_ref[:, hs] = (o * rr * gn_ref[:, hs] * (g * _sig(g))).astype(act_ref.dtype)

    return _riding_call(
        riding, body, "hgrn2_fwd", (nb,),
        [pl.BlockSpec((tb, 4096), lambda i: (i, 0)), pl.BlockSpec((2, 1024), lambda i: (0, 0)),
         pl.BlockSpec((1, 1024), lambda i: (0, 0))],
        [proj, hg_lb, hg_norm_g],
        [pl.BlockSpec((tb, 1024), lambda i: (i, 0)), pl.BlockSpec((tb, 1024), lambda i: (i, 0)),
         pl.BlockSpec((HG_HEADS, nck, HG_DIM, HG_DIM), lambda i: (0, i, 0, 0))],
        [jax.ShapeDtypeStruct((t_len, 1024), F32), jax.ShapeDtypeStruct((t_len, 1024), MXU_DTYPE),
         jax.ShapeDtypeStruct((HG_HEADS, t_len // HG_CHUNK, HG_DIM, HG_DIM), F32)],
        [pltpu.VMEM((HG_HEADS, HG_DIM, HG_DIM), F32)] + [pltpu.VMEM((tb, 1024), MXU_DTYPE)] * 5
        + [pltpu.VMEM((HG_HEADS, tb, tb), MXU_DTYPE), pltpu.VMEM((HG_HEADS, nck, HG_DIM, HG_DIM), F32)])


def _hgrn2_bwd(proj, d_o, s_prev, hg_lb, dproj, t_len, tb, riding=None):
    nck = tb // HG_CHUNK
    nb = t_len // tb

    def body(*refs):
        step = pl.program_id(0)
        ((p_ref, do_ref, sp_ref, lb_ref, _), (dp_ref, dlb_ref),
         (ds_ref, acc_ref, a_s, bm_s, qd_s, kd_s, v_s, do_s, da_s, dbm_s, dqd_s, dkd_s, dv_s, ex_s, sc_s, dsc_s,
          up_s)) = _ride(riding, refs, 5, 2, 17, step == 0, step == nb - 1)

        @pl.when(pl.program_id(0) == 0)
        def _():
            ds_ref[...] = jnp.zeros_like(ds_ref)
            acc_ref[...] = jnp.zeros_like(acc_ref)

        lb = _sig(lb_ref[0:1, :] - lb_ref[1:2, :])
        q = p_ref[:, pl.ds(0, 1024)]
        sig, fv, kk, causal, anti, e_mid, e_mid_inv, e_b, e_last, dcs = _hg_block_terms(
            q, p_ref[:, pl.ds(1024, 1024)], lb, tb)
        a, bm, qd, kd = q * e_mid, kk * e_mid_inv, q * e_b, kk * e_last
        a_s[...] = _mx(a)
        bm_s[...] = _mx(bm)
        qd_s[...] = _mx(qd)
        kd_s[...] = _mx(kd)
        v_s[...] = _mx(p_ref[:, pl.ds(2048, 1024)])
        do_s[...] = _mx(do_ref[...])
        heads = [pl.ds(h * HG_DIM, HG_DIM) for h in range(HG_HEADS)]
        chunks = [pl.ds(c * HG_CHUNK, HG_CHUNK) for c in range(nck)]
        for h, hs in enumerate(heads):
            sc_s[h] = _mx(jnp.where(causal, _dot(a_s[:, hs], bm_s[:, hs], _NT), 0.0))
            dsc_s[h] = _mx(jnp.where(causal, _dot(do_s[:, hs], v_s[:, hs], _NT), 0.0))
        for h, hs in enumerate(heads):
            dv_s[:, hs] = _dot(sc_s[h], do_s[:, hs], _TN)
            da_s[:, hs] = _dot(dsc_s[h], bm_s[:, hs])
            dbm_s[:, hs] = _dot(dsc_s[h], a_s[:, hs], _TN)
        for h, hs in enumerate(heads):
            for c, r in enumerate(chunks):
                up_s[h, c] = _dot(do_s[r, hs], qd_s[r, hs], _TN)
                dqd_s[r, hs] = _dot(do_s[r, hs], sp_ref[h, c])
        for c in reversed(range(nck)):
            r = chunks[c]
            for h, hs in enumerate(heads):
                dst = ds_ref[h]
                dc = dcs[c][:, h * HG_DIM:(h + 1) * HG_DIM]
                dv_s[r, hs] += _dot(kd_s[r, hs], dst, _NT)
                dkd_s[r, hs] = _dot(v_s[r, hs], dst)
                ex_s[c:c + 1, hs] = jnp.sum(dst * sp_ref[h, c], axis=0, keepdims=True) * dc
                ds_ref[h] = up_s[h, c] + dc * dst
        da, dbm, dqd, dkd = da_s[...], dbm_s[...], dqd_s[...], dkd_s[...]
        dq = da * e_mid + dqd * e_b
        dk = dbm * e_mid_inv + dkd * e_last
        db = da * a - dbm * bm + dqd * qd - dkd * kd
        dkk = dkd * kd
        extra = jnp.concatenate(
            [jnp.broadcast_to(jnp.sum(dkk[c * HG_CHUNK:(c + 1) * HG_CHUNK], axis=0, keepdims=True)
                              + ex_s[c:c + 1, :], (HG_CHUNK, 1024)) for c in range(nck)], axis=0)
        dlogf = _dot01(anti, db) + extra
        dfv_k = dlogf / fv - dk
        dp_ref[:, pl.ds(0, 1024)] = dq.astype(dp_ref.dtype)
        dp_ref[:, pl.ds(1024, 1024)] = (dfv_k * (1.0 - lb) * sig * (1.0 - sig)).astype(dp_ref.dtype)
        dp_ref[:, pl.ds(2048, 1024)] = dv_s[...].astype(dp_ref.dtype)
        acc_ref[...] += jnp.sum(dfv_k * (1.0 - sig), axis=0, keepdims=True)

        @pl.when(pl.program_id(0) == nb - 1)
        def _():
            g0 = acc_ref[...] * lb * (1.0 - lb)
            dlb_ref[0:1, :] = g0
            dlb_ref[1:2, :] = -g0

    return _riding_call(
        riding, body, "hgrn2_bwd", (nb,),
        [pl.BlockSpec((tb, 3072), lambda i: (nb - 1 - i, 0)),
         pl.BlockSpec((tb, 1024), lambda i: (nb - 1 - i, 0)),
         pl.BlockSpec((HG_HEADS, nck, HG_DIM, HG_DIM), lambda i: (0, nb - 1 - i, 0, 0)),
         pl.BlockSpec((2, 1024), lambda i: (0, 0)),
         pl.BlockSpec(memory_space=pl.ANY)],
        [proj, d_o, s_prev, hg_lb, dproj],
        [pl.BlockSpec((tb, 3072), lambda i: (nb - 1 - i, 0)), pl.BlockSpec((2, 1024), lambda i: (0, 0))],
        [jax.ShapeDtypeStruct((t_len, IN_COLS), dproj.dtype), jax.ShapeDtypeStruct((2, 1024), F32)],
        [pltpu.VMEM((HG_HEADS, HG_DIM, HG_DIM), F32), pltpu.VMEM((1, 1024), F32)]
        + [pltpu.VMEM((tb, 1024), MXU_DTYPE)] * 6 + [pltpu.VMEM((tb, 1024), F32)] * 5
        + [pltpu.VMEM((SUBLANES, 1024), F32)] + [pltpu.VMEM((HG_HEADS, tb, tb), MXU_DTYPE)] * 2
        + [pltpu.VMEM((HG_HEADS, nck, HG_DIM, HG_DIM), F32)], {4: 0})


def _s5_prep_bwd(a_re, a_im, log_dt, b_re_t, b_im_t, dlam, dbbr, dbbi):
    def body(ar_ref, ai_ref, ldt_ref, br_ref, bi_ref, dlam_ref, dbbr_ref, dbbi_ref,
             dar_ref, dai_ref, dldt_ref, dbr_ref, dbi_ref):
        ar, ai = ar_ref[...], ai_ref[...]
        dt = jnp.exp(ldt_ref[...])
        mag = jnp.exp(ar * dt)
        cs, sn = jnp.cos(ai * dt), jnp.sin(ai * dt)
        lr, li = mag * cs, mag * sn
        den = ar * ar + ai * ai
        nr = lr - 1.0
        sr = (nr * ar + li * ai) / den
        si = (li * ar - nr * ai) / den
        br, bi = br_ref[...], bi_ref[...]
        gbr, gbi = dbbr_ref[...], dbbi_ref[...]
        dbr_ref[...] = sr * gbr + si * gbi
        dbi_ref[...] = sr * gbi - si * gbr
        dsr = jnp.sum(gbr * br + gbi * bi, axis=0, keepdims=True)
        dsi = jnp.sum(gbi * br - gbr * bi, axis=0, keepdims=True)
        dnr = (dsr * ar - dsi * ai) / den
        dli = dlam_ref[1:2, :] + (dsr * ai + dsi * ar) / den
        dlr = dlam_ref[0:1, :] + dnr
        dden = -(dsr * sr + dsi * si) / den
        dar = (dsr * nr + dsi * li) / den + dden * 2.0 * ar
        dai = (dsr * li - dsi * nr) / den + dden * 2.0 * ai
        dmag = dlr * cs + dli * sn
        dth = mag * (dli * cs - dlr * sn)
        dar_ref[...] = dar + dmag * mag * dt
        dai_ref[...] = dai + dth * dt
        ddt = (dmag * mag * ar + dth * ai) * dt
        lane = lax.broadcasted_iota(jnp.int32, (S5_LANES, 128), 0) // S5_STATE
        grp = lax.broadcasted_iota(jnp.int32, (S5_LANES, 128), 1)
        dldt_ref[...] = _dot32(jnp.broadcast_to(ddt, (SUBLANES, S5_LANES)), (lane == grp).astype(F32))

    whole = pl.BlockSpec(memory_space=pltpu.VMEM)
    return pl.pallas_call(
        body, name="s5_prep_bwd", in_specs=[whole] * 8, out_specs=[whole] * 5,
        out_shape=[jax.ShapeDtypeStruct((1, S5_LANES), F32), jax.ShapeDtypeStruct((1, S5_LANES), F32),
                   jax.ShapeDtypeStruct((SUBLANES, 128), F32), jax.ShapeDtypeStruct((S5_GROUP, S5_LANES), F32),
                   jax.ShapeDtypeStruct((S5_GROUP, S5_LANES), F32)])(a_re, a_im, log_dt, b_re_t, b_im_t, dlam, dbbr,
                                                                      dbbi)


def _dgelu(x):
    c, a = 0.7978845608028654, 0.044715
    th = jnp.tanh(c * (x + a * x * x * x))
    return 0.5 * (1.0 + th) + 0.5 * x * (1.0 - th * th) * c * (1.0 + 3.0 * a * x * x)


S5_BLOCKS = 4
S5_BW = S5_WIDTH // S5_BLOCKS
S5_BL = S5_LANES // S5_BLOCKS
S5_LANE_BLOCKS = S5_LANES // 128
S5_SCAN_BLOCKS = 4


def _s5_prep(a_re, a_im, log_dt, b_re_t, b_im_t, seg):
    def body(ar_ref, ai_ref, ldt_ref, br_ref, bi_ref,
             rows_f, pfr_ref, pfi_ref, rows_r, prr_ref, pri_ref, bbr_ref, bbi_ref):
        ar, ai = ar_ref[...], ai_ref[...]
        dt = jnp.exp(ldt_ref[...])
        mag = jnp.exp(ar * dt)
        lr, li = mag * jnp.cos(ai * dt), mag * jnp.sin(ai * dt)
        den = ar * ar + ai * ai
        nr = lr - 1.0
        sr = (nr * ar + li * ai) / den
        si = (li * ar - nr * ai) / den
        wide = (SUBLANES, S5_LANES)
        cr, ci = lr, li
        for i in range(seg):
            pfr_ref[i] = jnp.broadcast_to(cr, wide)
            pfi_ref[i] = jnp.broadcast_to(ci, wide)
            prr_ref[seg - 1 - i] = jnp.broadcast_to(cr, wide)
            pri_ref[seg - 1 - i] = jnp.broadcast_to(-ci, wide)
            if i == seg - 1:
                for rows, sign in ((rows_f, 1.0), (rows_r, -1.0)):
                    rows[0:1, :] = lr
                    rows[1:2, :] = sign * li
                    rows[2:3, :] = cr
                    rows[3:4, :] = sign * ci
            cr, ci = cr * lr - ci * li, cr * li + ci * lr
        br, bi = br_ref[...], bi_ref[...]
        bbr_ref[...] = sr * br - si * bi
        bbi_ref[...] = sr * bi + si * br

    whole = pl.BlockSpec(memory_space=pltpu.VMEM)
    tables = [jax.ShapeDtypeStruct((4, S5_LANES), F32)] + [jax.ShapeDtypeStruct((seg, SUBLANES, S5_LANES), F32)] * 2
    bbar = [jax.ShapeDtypeStruct((S5_GROUP, S5_LANES), F32)] * 2
    res = pl.pallas_call(body, name="s5_prep", in_specs=[whole] * 5, out_specs=[whole] * 8,
                         out_shape=tables + tables + bbar)(a_re, a_im, log_dt, b_re_t, b_im_t)
    return res[0:3], res[3:6], res[6], res[7]


def _lanes(j):
    return pl.ds(j * 128, 128)


def _to_segment_order(v, stage_ref, out_ref, seg):
    nbl = v.shape[1] // 128
    for b in range(nbl):
        stage_ref[b] = v[:, b * 128:(b + 1) * 128]

    def body(t, carry):
        rows = pl.ds(pl.multiple_of(t * SUBLANES, SUBLANES), SUBLANES)
        for b in range(nbl):
            out_ref[rows, _lanes(b)] = stage_ref[b, pl.ds(t, SUBLANES, stride=seg), :]
        return carry

    lax.fori_loop(0, seg, body, 0, unroll=True)


def _from_segment_order(v, stage_ref, out_ref, seg):
    nbl = v.shape[1] // 128
    for b in range(nbl):
        stage_ref[b] = v[:, b * 128:(b + 1) * 128]
    for s in range(SUBLANES):
        def body(k, carry, s=s):
            rows = pl.ds(pl.multiple_of(s * seg + k * SUBLANES, SUBLANES), SUBLANES)
            for b in range(nbl):
                out_ref[rows, _lanes(b)] = stage_ref[b, pl.ds(k * SUBLANES * SUBLANES + s, SUBLANES,
                                                              stride=SUBLANES), :]
            return carry

        lax.fori_loop(0, seg // SUBLANES, body, 0, unroll=True)


def _tile_scan(xr_ref, xi_ref, lam_ref, car_ref, cai_ref, cn_r, cn_i, blocks, seg, reverse):
    shape = (SUBLANES, 128)
    lrs = [jnp.broadcast_to(lam_ref[0:1, _lanes(j)], shape) for j in blocks]
    lis = [jnp.broadcast_to(lam_ref[1:2, _lanes(j)], shape) for j in blocks]

    def step(k, carry):
        t = seg - 1 - k if reverse else k
        rows = pl.ds(pl.multiple_of(t * SUBLANES, SUBLANES), SUBLANES)
        out = []
        for n, j in enumerate(blocks):
            cr, ci = carry[2 * n], carry[2 * n + 1]
            nr = lrs[n] * cr - lis[n] * ci + xr_ref[rows, _lanes(j)]
            ni = lrs[n] * ci + lis[n] * cr + xi_ref[rows, _lanes(j)]
            xr_ref[rows, _lanes(j)] = nr
            xi_ref[rows, _lanes(j)] = ni
            out += [nr, ni]
        return tuple(out)

    zero = jnp.zeros(shape, F32)
    fin = lax.fori_loop(0, seg, step, (zero,) * (2 * len(blocks)), unroll=True)
    for n, j in enumerate(blocks):
        ls = _lanes(j)
        fr, fi = fin[2 * n], fin[2 * n + 1]
        sr, si = lam_ref[2:3, ls], lam_ref[3:4, ls]
        pr, pi = car_ref[:, ls], cai_ref[:, ls]
        for s in (reversed(range(SUBLANES)) if reverse else range(SUBLANES)):
            cn_r[s:s + 1, ls] = pr
            cn_i[s:s + 1, ls] = pi
            pr, pi = fr[s:s + 1, :] + sr * pr - si * pi, fi[s:s + 1, :] + sr * pi + si * pr
        car_ref[:, ls] = pr
        cai_ref[:, ls] = pi


def _s5_fwd(proj, lam_rows, p3_re, p3_im, bbr4, bbi4, crt4, cit4, d_row, t_len, tb):
    seg = tb // SUBLANES

    def body(u_ref, lam_ref, p3r_ref, p3i_ref, bbr_ref, bbi_ref, crt_ref, cit_ref, d_ref,
             hr_ref, hi_ref, ypre_ref, ys_ref, car_ref, cai_ref, cn_r, cn_i, stage_ref, us_ref, yseg_ref):
        @pl.when(pl.program_id(0) == 0)
        def _():
            car_ref[...] = jnp.zeros_like(car_ref)
            cai_ref[...] = jnp.zeros_like(cai_ref)

        _to_segment_order(u_ref[...], stage_ref, us_ref, seg)
        u = us_ref[...]
        for i in range(S5_BLOCKS):
            ui = u[:, i * S5_BW:(i + 1) * S5_BW]
            hr_ref[:, pl.ds(i * S5_BL, S5_BL)] = _dot(ui, bbr_ref[i])
            hi_ref[:, pl.ds(i * S5_BL, S5_BL)] = _dot(ui, bbi_ref[i])
        for lc in range(S5_LANE_BLOCKS // S5_SCAN_BLOCKS):
            blocks = range(lc * S5_SCAN_BLOCKS, (lc + 1) * S5_SCAN_BLOCKS)
            _tile_scan(hr_ref, hi_ref, lam_ref, car_ref, cai_ref, cn_r, cn_i, blocks, seg, False)
            crs = [cn_r[:, _lanes(j)] for j in blocks]
            cis = [cn_i[:, _lanes(j)] for j in blocks]

            def fix(t, carry, blocks=blocks, crs=crs, cis=cis):
                rows = pl.ds(pl.multiple_of(t * SUBLANES, SUBLANES), SUBLANES)
                for n, j in enumerate(blocks):
                    pr, pi = p3r_ref[t, :, _lanes(j)], p3i_ref[t, :, _lanes(j)]
                    hr_ref[rows, _lanes(j)] += pr * crs[n] - pi * cis[n]
                    hi_ref[rows, _lanes(j)] += pr * cis[n] + pi * crs[n]
                return carry

            lax.fori_loop(0, seg, fix, 0, unroll=True)
        for i in range(S5_BLOCKS):
            ws = pl.ds(i * S5_BW, S5_BW)
            bl = pl.ds(i * S5_BL, S5_BL)
            yseg_ref[:, ws] = (_dot(hr_ref[:, bl], crt_ref[i]) - _dot(hi_ref[:, bl], cit_ref[i])
                               + d_ref[:, ws] * u[:, i * S5_BW:(i + 1) * S5_BW])
        _from_segment_order(yseg_ref[...], stage_ref, ypre_ref, seg)
        ys_ref[...] = jax.nn.gelu(ypre_ref[...], approximate=True).astype(ys_ref.dtype)

    whole = pl.BlockSpec(memory_space=pltpu.VMEM)
    return pl.pallas_call(
        body, name="s5_fwd", grid=(t_len // tb,),
        in_specs=[pl.BlockSpec((tb, S5_WIDTH), lambda i: (i, 4096 // S5_WIDTH))] + [whole] * 8,
        out_specs=[pl.BlockSpec((tb, S5_LANES), lambda i: (i, 0)), pl.BlockSpec((tb, S5_LANES), lambda i: (i, 0)),
                   pl.BlockSpec((tb, S5_WIDTH), lambda i: (i, 0)), pl.BlockSpec((tb, S5_WIDTH), lambda i: (i, 0))],
        out_shape=[jax.ShapeDtypeStruct((t_len, S5_LANES), F32), jax.ShapeDtypeStruct((t_len, S5_LANES), F32),
                   jax.ShapeDtypeStruct((t_len, S5_WIDTH), F32), jax.ShapeDtypeStruct((t_len, S5_WIDTH), MXU_DTYPE)],
        scratch_shapes=[pltpu.VMEM((1, S5_LANES), F32), pltpu.VMEM((1, S5_LANES), F32),
                        pltpu.VMEM((SUBLANES, S5_LANES), F32), pltpu.VMEM((SUBLANES, S5_LANES), F32),
                        pltpu.VMEM((S5_WIDTH // 128, tb, 128), F32), pltpu.VMEM((tb, S5_WIDTH), F32),
                        pltpu.VMEM((tb, S5_WIDTH), F32)],
        compiler_params=_params("arbitrary"))(proj, lam_rows, p3_re, p3_im, bbr4, bbi4, crt4, cit4, d_row)


def _s5_bwd(dgelu, y_pre, proj, h_re, h_im, lam_rows, p3_re, p3_im, bbr4, bbi4, cr4, ci4, d_row, dproj, t_len, tb):
    seg = tb // SUBLANES
    nb = t_len // tb

    def body(dg_ref, yp_ref, u_ref, hr_ref, hi_ref, lam_ref, p3r_ref, p3i_ref, bbr_ref, bbi_ref, cr_ref, ci_ref,
             d_ref, _, du_ref, dbbr_ref, dbbi_ref, dcr_ref, dci_ref, dd_ref, dlam_ref,
             gr_ref, gi_ref, car_ref, cai_ref, cn_r, cn_i, stage_ref, us_ref, dys_ref, duseg_ref):
        @pl.when(pl.program_id(0) == 0)
        def _():
            for ref in (car_ref, cai_ref, dbbr_ref, dbbi_ref, dcr_ref, dci_ref, dd_ref, dlam_ref):
                ref[...] = jnp.zeros_like(ref)

        _to_segment_order(u_ref[...], stage_ref, us_ref, seg)
        _to_segment_order(dg_ref[...] * _dgelu(yp_ref[...]), stage_ref, dys_ref, seg)
        u, dy = us_ref[...], dys_ref[...]
        for i in range(S5_BLOCKS):
            dyi = dy[:, i * S5_BW:(i + 1) * S5_BW]
            gr_ref[:, pl.ds(i * S5_BL, S5_BL)] = _dot(dyi, cr_ref[i])
            gi_ref[:, pl.ds(i * S5_BL, S5_BL)] = -_dot(dyi, ci_ref[i])
        for lc in range(S5_LANE_BLOCKS // S5_SCAN_BLOCKS):
            blocks = range(lc * S5_SCAN_BLOCKS, (lc + 1) * S5_SCAN_BLOCKS)
            _tile_scan(gr_ref, gi_ref, lam_ref, car_ref, cai_ref, cn_r, cn_i, blocks, seg, True)
            crs = [cn_r[:, _lanes(j)] for j in blocks]
            cis = [cn_i[:, _lanes(j)] for j in blocks]

            def fix(k, carry, blocks=blocks, crs=crs, cis=cis):
                t = seg - 1 - k
                rows = pl.ds(pl.multiple_of(t * SUBLANES, SUBLANES), SUBLANES)
                out = []
                for n, j in enumerate(blocks):
                    nr, ni, slr, sli = carry[4 * n:4 * n + 4]
                    pr, pi = p3r_ref[t, :, _lanes(j)], p3i_ref[t, :, _lanes(j)]
                    g_r = gr_ref[rows, _lanes(j)] + pr * crs[n] - pi * cis[n]
                    g_i = gi_ref[rows, _lanes(j)] + pr * cis[n] + pi * crs[n]
                    gr_ref[rows, _lanes(j)] = g_r
                    gi_ref[rows, _lanes(j)] = g_i
                    hr, hi = hr_ref[rows, _lanes(j)], hi_ref[rows, _lanes(j)]
                    out += [g_r, g_i, slr + nr * hr + ni * hi, sli + ni * hr - nr * hi]
                return tuple(out)

            zero = jnp.zeros((SUBLANES, 128), F32)
            init = []
            for n in range(len(blocks)):
                init += [crs[n], cis[n], zero, zero]
            fin = lax.fori_loop(0, seg, fix, tuple(init), unroll=True)
            for n, j in enumerate(blocks):
                dlam_ref[0:1, _lanes(j)] += jnp.sum(fin[4 * n + 2], axis=0, keepdims=True)
                dlam_ref[1:2, _lanes(j)] += jnp.sum(fin[4 * n + 3], axis=0, keepdims=True)
        for i in range(S5_BLOCKS):
            ws = pl.ds(i * S5_BW, S5_BW)
            bl = pl.ds(i * S5_BL, S5_BL)
            ui, dyi = u[:, i * S5_BW:(i + 1) * S5_BW], dy[:, i * S5_BW:(i + 1) * S5_BW]
            gr, gi = gr_ref[:, bl], gi_ref[:, bl]
            duseg_ref[:, ws] = _dot(gr, bbr_ref[i], _NT) + _dot(gi, bbi_ref[i], _NT) + d_ref[:, ws] * dyi
            dbbr_ref[i] += _dot(ui, gr, _TN)
            dbbi_ref[i] += _dot(ui, gi, _TN)
            dcr_ref[i] += _dot(hr_ref[:, bl], dyi, _TN)
            dci_ref[i] -= _dot(hi_ref[:, bl], dyi, _TN)
        dd_ref[...] += jnp.sum(dy * u, axis=0, keepdims=True)
        _from_segment_order(duseg_ref[...], stage_ref, duseg_ref, seg)
        du_ref[...] = duseg_ref[...].astype(du_ref.dtype)

    whole = pl.BlockSpec(memory_space=pltpu.VMEM)
    rev = lambda i: (nb - 1 - i, 0)
    const3 = lambda i: (0, 0, 0)
    return pl.pallas_call(
        body, name="s5_bwd", grid=(nb,),
        in_specs=[pl.BlockSpec((tb, S5_WIDTH), rev), pl.BlockSpec((tb, S5_WIDTH), rev),
                  pl.BlockSpec((tb, S5_WIDTH), lambda i: (nb - 1 - i, 4096 // S5_WIDTH)),
                  pl.BlockSpec((tb, S5_LANES), rev), pl.BlockSpec((tb, S5_LANES), rev)] + [whole] * 8
                 + [pl.BlockSpec(memory_space=pl.ANY)],
        out_specs=[pl.BlockSpec((tb, S5_WIDTH), lambda i: (nb - 1 - i, 4096 // S5_WIDTH)),
                   pl.BlockSpec((S5_BLOCKS, S5_BW, S5_BL), const3), pl.BlockSpec((S5_BLOCKS, S5_BW, S5_BL), const3),
                   pl.BlockSpec((S5_BLOCKS, S5_BL, S5_BW), const3), pl.BlockSpec((S5_BLOCKS, S5_BL, S5_BW), const3),
                   pl.BlockSpec((1, S5_WIDTH), lambda i: (0, 0)), pl.BlockSpec((2, S5_LANES), lambda i: (0, 0))],
        out_shape=[jax.ShapeDtypeStruct((t_len, IN_COLS), dproj.dtype),
                   jax.ShapeDtypeStruct((S5_BLOCKS, S5_BW, S5_BL), F32),
                   jax.ShapeDtypeStruct((S5_BLOCKS, S5_BW, S5_BL), F32),
                   jax.ShapeDtypeStruct((S5_BLOCKS, S5_BL, S5_BW), F32),
                   jax.ShapeDtypeStruct((S5_BLOCKS, S5_BL, S5_BW), F32),
                   jax.ShapeDtypeStruct((1, S5_WIDTH), F32), jax.ShapeDtypeStruct((2, S5_LANES), F32)],
        scratch_shapes=[pltpu.VMEM((tb, S5_LANES), F32), pltpu.VMEM((tb, S5_LANES), F32),
                        pltpu.VMEM((1, S5_LANES), F32), pltpu.VMEM((1, S5_LANES), F32),
                        pltpu.VMEM((SUBLANES, S5_LANES), F32), pltpu.VMEM((SUBLANES, S5_LANES), F32),
                        pltpu.VMEM((S5_WIDTH // 128, tb, 128), F32), pltpu.VMEM((tb, S5_WIDTH), F32),
                        pltpu.VMEM((tb, S5_WIDTH), F32), pltpu.VMEM((tb, S5_WIDTH), F32)],
        input_output_aliases={13: 0},
        compiler_params=_params("arbitrary"))(dgelu, y_pre, proj, h_re, h_im, lam_rows, p3_re, p3_im, bbr4, bbi4,
                                              cr4, ci4, d_row, dproj)


def _block_diag(per_group):
    g8 = S5_GROUPS // S5_BLOCKS
    eye = jnp.eye(g8, dtype=bool)[None, :, None, :, None]
    dense = jnp.where(eye, per_group.reshape(S5_BLOCKS, g8, S5_GROUP, 1, S5_STATE), 0.0)
    return dense.reshape(S5_BLOCKS, S5_BW, S5_BL)


def _diag_blocks(dense):
    g8 = S5_GROUPS // S5_BLOCKS
    ar = jnp.arange(g8)
    d5 = dense.reshape(S5_BLOCKS, g8, S5_GROUP, g8, S5_STATE)
    return d5[:, ar, :, ar, :].transpose(1, 0, 2, 3).reshape(S5_GROUPS, S5_GROUP, S5_STATE)


def _hg_gate_bwd(da, o, g, gn):
    dos, dgs, dgns = [], [], []
    for h in range(HG_HEADS):
        sl = slice(h * HG_DIM, (h + 1) * HG_DIM)
        oh, gh, dah, gnh = o[:, sl], g[:, sl], da[:, sl], gn[:, sl]
        rr = lax.rsqrt(jnp.mean(oh * oh, axis=-1, keepdims=True) + NORM_EPS)
        sg = _sig(gh)
        dgs.append(dah * (oh * rr * gnh) * _dsilu(gh, sg))
        don = dah * (gh * sg)
        t = don * gnh
        dos.append(rr * t - oh * (rr * rr * rr) * jnp.mean(t * oh, axis=-1, keepdims=True))
        dgns.append(jnp.sum(don * oh * rr, axis=0, keepdims=True))
    return jnp.concatenate(dos, axis=1), jnp.concatenate(dgs, axis=1), jnp.concatenate(dgns, axis=1)


MIX_BWD_COLS = ((3072, 1024), (4608, 512), (5120, 1024), (6144, 1024))


def _mix_bwd(dgl, h1, dh2, act_hg, ys2, ys_gelu, proj, o_hg, g2, ghn, b_glu, w, t_len, tm):
    nb = t_len // tm

    def body(dgl_ref, h1_ref, dh2_ref, act_ref, ys2_ref, ysg_ref, ghg_ref, z_ref, gh_ref, gs_ref, o_ref, g2_ref, gn_ref,
             bglu_ref, wg_ref, wo_ref, ws5_ref, whg_ref, wglu_ref,
             dh1_ref, dyh_ref, dys_ref, dglu_ref, dgelu_ref, do_ref, dg2_ref, dbglu_ref, dgn_ref, dproj_ref,
             st0, st1, st2, st3, sems):
        i = pl.program_id(0)
        stages = (st0, st1, st2, st3)

        def writes(step):
            rows = pl.ds(pl.multiple_of(step * tm, tm), tm)
            return [pltpu.make_async_copy(st, dproj_ref.at[rows, pl.ds(c0, wd)], sems.at[k])
                    for k, (st, (c0, wd)) in enumerate(zip(stages, MIX_BWD_COLS))]

        @pl.when(i > 0)
        def _():
            for cp in writes(i - 1):
                cp.wait()

        @pl.when(i == 0)
        def _():
            for ref in (dg2_ref, dbglu_ref, dgn_ref):
                ref[...] = jnp.zeros_like(ref)

        dx, dg2 = _rms_bwd(_dot(dgl_ref[...], wg_ref[...], _NT), h1_ref[...], g2_ref[...])
        dh1 = dh2_ref[...] + dx
        dh1_ref[...] = dh1
        dg2_ref[...] += dg2
        dm = _dot(dh1, wo_ref[...], _NT)
        sh, ss = _sig(gh_ref[...]), _sig(gs_ref[...])
        dyh, dys = _mx(dm * sh), _mx(dm * ss)
        dyh_ref[...] = dyh
        dys_ref[...] = dys
        st2[...] = (dm * _dot(act_ref[...], whg_ref[...]) * sh * (1.0 - sh)).astype(st2.dtype)
        st3[...] = (dm * _dot(ys2_ref[...], ws5_ref[...]) * ss * (1.0 - ss)).astype(st3.dtype)
        dys2 = _dot(dys, ws5_ref[...], _NT)
        gl_, z = _dot(ysg_ref[...], wglu_ref[...]) + bglu_ref[...], z_ref[...]
        a, b = gl_[:, :S5_WIDTH], gl_[:, S5_WIDTH:]
        sb, sz = _sig(b), _sig(z)
        silu = z * sz
        dglu = jnp.concatenate([dys2 * sb * silu, dys2 * a * silu * sb * (1.0 - sb)], axis=1)
        st1[...] = (dys2 * a * sb * _dsilu(z, sz)).astype(st1.dtype)
        dbglu_ref[...] += jnp.sum(dglu, axis=0, keepdims=True)
        dglu_ref[...] = _mx(dglu)
        dgelu_ref[...] = _dot(dglu, wglu_ref[...], _NT)
        d_o, dg, dgn = _hg_gate_bwd(_dot(dyh, whg_ref[...], _NT), o_ref[...], ghg_ref[...], gn_ref[...])
        do_ref[...] = d_o.astype(do_ref.dtype)
        st0[...] = dg.astype(st0.dtype)
        dgn_ref[...] += dgn
        for cp in writes(i):
            cp.start()

        @pl.when(i == nb - 1)
        def _():
            for cp in writes(i):
                cp.wait()

    tile = lambda wd, cb=0: pl.BlockSpec((tm, wd), functools.partial(lambda i, cb: (i, cb), cb=cb))
    row = lambda wd: pl.BlockSpec((1, wd), lambda i: (0, 0))
    whole = pl.BlockSpec(memory_space=pltpu.VMEM)
    return pl.pallas_call(
        body, name="mix_bwd", grid=(nb,),
        in_specs=[tile(1024), tile(1024), tile(1024), tile(1024), tile(512), tile(512), tile(1024, 3),
                  tile(512, 4608 // 512), tile(1024, 5), tile(1024, 6), tile(1024), row(1024), row(1024), row(1024)]
                 + [whole] * 5,
        out_specs=[tile(1024), tile(1024), tile(1024), tile(1024), tile(512), tile(1024), row(1024), row(1024),
                   row(1024), _HBM],
        out_shape=[jax.ShapeDtypeStruct((t_len, 1024), F32), jax.ShapeDtypeStruct((t_len, 1024), MXU_DTYPE),
                   jax.ShapeDtypeStruct((t_len, 1024), MXU_DTYPE), jax.ShapeDtypeStruct((t_len, 1024), MXU_DTYPE),
                   jax.ShapeDtypeStruct((t_len, 512), F32), jax.ShapeDtypeStruct((t_len, 1024), MXU_DTYPE),
                   jax.ShapeDtypeStruct((1, 1024), F32), jax.ShapeDtypeStruct((1, 1024), F32),
                   jax.ShapeDtypeStruct((1, 1024), F32), jax.ShapeDtypeStruct((t_len, IN_COLS), MXU_DTYPE)],
        scratch_shapes=[pltpu.VMEM((tm, wd), MXU_DTYPE) for _, wd in MIX_BWD_COLS] + [pltpu.SemaphoreType.DMA((4,))],
        compiler_params=_params("arbitrary"))(dgl, h1, dh2, act_hg, ys2, ys_gelu, proj, proj, proj, proj, o_hg, g2, ghn,
                                              b_glu, w["w_ple_gate"], w["w_out"], w["w_o_s5"], w["w_o_hg"],
                                              w["w_glu"])


def _local_step(x, p, target, w, sm, comm=None):
    t_len = x.shape[0]
    tm = min(256, t_len)
    tmm = min(512, t_len)
    tb_hg = min(256, t_len)
    tb_s5 = min(256, t_len)
    g1, g2, g3, ghn = sm["norm_g"], sm["ple_norm_g"], sm["final_norm_g"].reshape(1, D_MODEL), sm["hg_norm_g"]

    def rms_in(xv, g):
        return xv * lax.rsqrt(jnp.mean(xv * xv, axis=-1, keepdims=True) + NORM_EPS) * g

    in_shard = IN_COLS // N_CHIPS
    if comm is None:
        w_in = w["w_in"]
        proj, u = _mm_nn("mm_in", x, w_in, tmm, in_shard, prologue=rms_in, consts=[g1])
    else:
        proj, u, w_in = comm.input_projection(x, g1, rms_in, tmm)

    lanes = lambda a: a.reshape(1, S5_LANES)
    a_re, a_im = lanes(sm["s5_a_re"]), lanes(sm["s5_a_im"])
    ldt = lanes(jnp.broadcast_to(sm["s5_log_dt"].reshape(S5_GROUPS, 1), (S5_GROUPS, S5_STATE)))
    to_t = lambda b: b.reshape(S5_GROUPS, S5_STATE, S5_GROUP).transpose(2, 0, 1).reshape(S5_GROUP, S5_LANES)
    b_re_t, b_im_t = to_t(sm["s5_b_re"]), to_t(sm["s5_b_im"])
    scan_fwd, scan_rev, bbr_t, bbi_t = _s5_prep(a_re, a_im, ldt, b_re_t, b_im_t, tb_s5 // SUBLANES)
    from_t = lambda b: b.reshape(S5_GROUP, S5_GROUPS, S5_STATE).transpose(1, 0, 2)
    bbr_bd = _block_diag(from_t(bbr_t)).astype(MXU_DTYPE)
    bbi_bd = _block_diag(from_t(bbi_t)).astype(MXU_DTYPE)
    cr_bd = _block_diag(sm["s5_c_re"].reshape(S5_GROUPS, S5_GROUP, S5_STATE)).astype(MXU_DTYPE)
    ci_bd = _block_diag(sm["s5_c_im"].reshape(S5_GROUPS, S5_GROUP, S5_STATE)).astype(MXU_DTYPE)
    d_row = sm["s5_d"].reshape(1, S5_WIDTH)
    if comm is None:
        o_hg, act_hg, s_prev = _hgrn2_fwd(proj, sm["hg_lb"], ghn, t_len, tb_hg)
    else:
        o_hg, act_hg, s_prev, landed = _hgrn2_fwd(proj, sm["hg_lb"], ghn, t_len, tb_hg, riding=comm.gather_rest())
        w = comm.rest_weights(landed)
    h_re, h_im, y_pre, ys_gelu = _s5_fwd(proj, *scan_fwd, bbr_bd, bbi_bd,
                                          cr_bd.transpose(0, 2, 1), ci_bd.transpose(0, 2, 1), d_row, t_len, tb_s5)
    def mix_f(act, ysg, z, gh, gs, xv, w_glu, b_glu, w_o_hg, w_o_s5, w_out):
        gl_ = _dot(ysg, w_glu) + b_glu
        a, b = gl_[:, :S5_WIDTH], gl_[:, S5_WIDTH:]
        ys2_ = (a * _sig(b) * (z * _sig(z))).astype(MXU_DTYPE)
        yh, ys = _dot(act, w_o_hg), _dot(ys2_, w_o_s5)
        mg = (_sig(gh) * yh + _sig(gs) * ys).astype(MXU_DTYPE)
        return (ys2_, mg, xv + _dot(mg, w_out))

    ys2, merged, h1 = _rowwise(
        "mix_out", mix_f, t_len, tm,
        [(act_hg, 1024, 0), (ys_gelu, 512, 0), (proj, 512, 4608 // 512), (proj, 1024, 5), (proj, 1024, 6),
         (x, 1024, 0)], [w["w_glu"], sm["b_glu"], w["w_o_hg"], w["w_o_s5"], w["w_out"]],
        [(512, MXU_DTYPE), (1024, MXU_DTYPE), (1024, F32)])

    def head_f(h1v, pv, tgt, g_ple, g, w_ple, w_gate):
        r2 = lax.rsqrt(jnp.mean(h1v * h1v, axis=-1, keepdims=True) + NORM_EPS)
        n2_ = (h1v * r2 * g_ple).astype(MXU_DTYPE)
        glv, pev = _dot(n2_, w_gate), _dot(pv, w_ple)
        gate = _sig(glv)
        h2 = h1v + pev * gate
        r = lax.rsqrt(jnp.mean(h2 * h2, axis=-1, keepdims=True) + NORM_EPS)
        e = h2 * r * g - tgt
        loss = 0.5 * jnp.sum(jnp.mean(e * e, axis=-1, keepdims=True), axis=0, keepdims=True)
        dy = e * (1.0 / D_MODEL)
        dg = jnp.sum(dy * h2 * r, axis=0, keepdims=True)
        t = dy * g
        dh2 = r * t - h2 * (r * r * r) * jnp.mean(t * h2, axis=-1, keepdims=True)
        return (n2_, dh2, dh2 * gate, dh2 * pev * gate * (1.0 - gate), jnp.broadcast_to(loss, (1, 128)), dg)

    n2, dh2, dpe, dgl, loss_row, d_g3 = _rowwise(
        "ple_loss_head", head_f, t_len, tm, [(h1, 1024, 0), (p, 256, 0), (target, 1024, 0)],
        [g2, g3, w["w_ple"], w["w_ple_gate"]],
        [(1024, MXU_DTYPE), (1024, F32), (1024, MXU_DTYPE), (1024, MXU_DTYPE)], accs=[(1, 128), (1, 1024)])

    gb = {}
    gb["w_ple"] = _mm_tn("mm_d_w_ple", p, dpe, tmm, 1024)
    gb["w_ple_gate"] = _mm_tn("mm_d_w_ple_gate", n2, dgl, tmm, 1024)
    dh1, dy_hg, dy_s5, dglu, dgelu, d_o, d_g2, d_bglu, d_ghn, dproj = _mix_bwd(
        dgl, h1, dh2, act_hg, ys2, ys_gelu, proj, o_hg, g2, ghn, sm["b_glu"], w, t_len, tm)
    gb["w_out"] = _mm_tn("mm_d_w_out", merged, dh1, tmm, 1024)
    gb["w_o_s5"] = _mm_tn("mm_d_w_o_s5", ys2, dy_s5, tmm, 1024)
    gb["w_glu"] = _mm_tn("mm_d_w_glu", ys_gelu, dglu, tmm, 1024)
    dproj, d_bbr, d_bbi, d_crt, d_cit, d_d, d_lam = _s5_bwd(dgelu, y_pre, proj, h_re, h_im,
                                                            *scan_rev, bbr_bd, bbi_bd, cr_bd,
                                                            ci_bd, d_row, dproj, t_len, tb_s5)
    to_t3 = lambda b: b.transpose(1, 0, 2).reshape(S5_GROUP, S5_LANES)
    d_are, d_aim, d_ldt, d_br_t, d_bi_t = _s5_prep_bwd(a_re, a_im, ldt, b_re_t, b_im_t, d_lam,
                                                       to_t3(_diag_blocks(d_bbr)), to_t3(_diag_blocks(d_bbi)))
    gb["w_o_hg"] = _mm_tn("mm_d_w_o_hg", act_hg, dy_hg, tmm, 1024)
    if comm is None:
        dproj, d_lb = _hgrn2_bwd(proj, d_o, s_prev, sm["hg_lb"], dproj, t_len, tb_hg)
    else:
        rest_grads = _pack_rest_full(gb)
        dproj, d_lb, rest_theirs = _hgrn2_bwd(proj, d_o, s_prev, sm["hg_lb"], dproj, t_len, tb_hg,
                                               riding=comm.swap(rest_grads))

    def in_b(duv, xv, dh, g):
        dx, dg = _rms_bwd(duv, xv, g)
        return (dh + dx, dg)

    in_args = ("mm_d_u_rms_in_bwd", dproj, w_in, tmm, in_shard, in_b, [(x, 1024, 0), (dh1, 1024, 0)], [g1],
               [(1024, F32)])
    if comm is None:
        gb["w_in"] = _mm_tn("mm_d_w_in", u, dproj, tmm, in_shard, col_shards=True)
        grad_x, d_g1 = _mm_nt_then(*in_args, accs=[(1, 1024)])
    else:
        gb["w_in"], landed = _mm_tn("mm_d_w_in", u, dproj, tmm, in_shard, col_shards=True,
                                    riding=comm.scatter("rest", rest_grads, rest_theirs))
        comm.landed["rest"] = landed
        grad_x, d_g1, landed = _mm_nt_then(*in_args, accs=[(1, 1024)], riding=comm.scatter(
            "in", gb["w_in"].reshape(N_CHIPS, 2, D_MODEL // 2, in_shard)))
        comm.landed["in"] = landed

    back_t = lambda b: b.reshape(S5_GROUP, S5_GROUPS, S5_STATE).transpose(1, 2, 0).reshape(1, S5_GROUPS, S5_STATE,
                                                                                           S5_GROUP)
    gs = {
        "norm_g": d_g1, "hg_lb": d_lb, "hg_norm_g": d_ghn,
        "s5_a_re": d_are.reshape(1, S5_GROUPS, S5_STATE), "s5_a_im": d_aim.reshape(1, S5_GROUPS, S5_STATE),
        "s5_log_dt": d_ldt[0:1, :S5_GROUPS],
        "s5_b_re": back_t(d_br_t), "s5_b_im": back_t(d_bi_t),
        "s5_c_re": _diag_blocks(d_crt.transpose(0, 2, 1)).reshape(1, S5_GROUPS, S5_GROUP, S5_STATE),
        "s5_c_im": _diag_blocks(d_cit.transpose(0, 2, 1)).reshape(1, S5_GROUPS, S5_GROUP, S5_STATE),
        "s5_d": d_d.reshape(1, S5_GROUPS, S5_GROUP), "b_glu": d_bglu, "ple_norm_g": d_g2,
        "final_norm_g": d_g3.reshape(D_MODEL),
    }
    return loss_row, grad_x, gb, gs


def _shard_shape(name):
    r, c = BIG_SHAPE[name]
    return (r, c // N_CHIPS) if name in BIG_COL_SHARDED else (r // N_CHIPS, c)


def _pack_small(parts, last):
    flat = jnp.concatenate([parts[n].reshape(-1) for n in SMALL] + [last.reshape(-1)])
    return jnp.pad(flat, (0, SMALL_ROWS * PACK_W - flat.shape[0])).reshape(SMALL_ROWS, PACK_W)


def _unpack_small(packed):
    flat, out, off = packed.reshape(-1), {}, 0
    for n in SMALL:
        size = 1
        for d in SMALL_SHAPE[n]:
            size *= d
        out[n] = flat[off:off + size].reshape(SMALL_SHAPE[n])
        off += size
    return out, flat[off]


def _place():
    x, y, c = lax.axis_index("x"), lax.axis_index("y"), lax.axis_index("c")
    return x, y, c, [(1 - x, y), (x, 1 - y), (1 - x, 1 - y)]


def _remote(src, dst, send_sems, recv_sems, k, to):
    return pltpu.make_async_remote_copy(src_ref=src, dst_ref=dst, send_sem=send_sems.at[k], recv_sem=recv_sems.at[k],
                                        device_id=to, device_id_type=MESH)


REST = tuple(n for n in BIG if n != "w_in")
REST_ROWS = sum(BIG_SHAPE[n][0] * BIG_SHAPE[n][1] for n in REST) // (N_CHIPS * PACK_W)
IN_SHARD = IN_COLS // N_CHIPS
IN_TILE, REST_TILE = 256, 272


def _pack_rest(parts):
    return jnp.concatenate([parts[n].reshape(-1, PACK_W) for n in REST], axis=0)


def _unpack_rest(packed):
    out, off = {}, 0
    for n in REST:
        r, c = _shard_shape(n)
        rows = r * c // PACK_W
        out[n] = packed[off:off + rows].reshape(1, r, c)
        off += rows
    return out


def _unpack_rest_full(gathered):
    out, off = {}, 0
    for n in REST:
        r, c = _shard_shape(n)
        rows = r * c // PACK_W
        sh = gathered[:, off:off + rows].reshape(N_CHIPS, r, c)
        out[n] = sh.transpose(1, 0, 2).reshape(BIG_SHAPE[n]) if n in BIG_COL_SHARDED else sh.reshape(BIG_SHAPE[n])
        off += rows
    return out


def _pack_rest_full(full):
    parts = []
    for n in REST:
        r, c = _shard_shape(n)
        g = full[n]
        sh = g.reshape(BIG_SHAPE[n][0], N_CHIPS, c).transpose(1, 0, 2) if n in BIG_COL_SHARDED else g
        parts.append(sh.reshape(N_CHIPS, r * c // PACK_W, PACK_W))
    return jnp.concatenate(parts, axis=1).reshape(N_CHIPS, 2, REST_ROWS // 2, PACK_W)


def _swap_halves(pgs, name="exchange_halves"):
    n = len(pgs)

    def body(*refs):
        pg_refs, out_refs, (send_sems, recv_sems) = refs[:n], refs[n:2 * n], refs[2 * n:]
        x, y, c, _ = _place()
        cps = [_remote(pg_ref.at[j, 1 - c], out_ref.at[j], send_sems, recv_sems, N_CHIPS * g + j, (x, y, 1 - c))
               for g, (pg_ref, out_ref) in enumerate(zip(pg_refs, out_refs)) for j in range(N_CHIPS)]
        for cp in cps:
            cp.start()
        for cp in cps:
            cp.wait()

    return pl.pallas_call(
        body, name=name, in_specs=[_HBM] * n, out_specs=[_HBM] * n,
        out_shape=[jax.ShapeDtypeStruct((N_CHIPS,) + pg.shape[2:], pg.dtype) for pg in pgs],
        scratch_shapes=[pltpu.SemaphoreType.DMA((N_CHIPS * n,)), pltpu.SemaphoreType.DMA((N_CHIPS * n,))])(*pgs)


def _share_halves(gs):
    n = len(gs)

    def body(*refs):
        g_refs, out_refs, (send_sems, recv_sems) = refs[:n], refs[n:2 * n], refs[2 * n:]
        x, y, c, _ = _place()
        cps = [_remote(g_ref, out_ref.at[c], send_sems, recv_sems, g, (x, y, 1 - c))
               for g, (g_ref, out_ref) in enumerate(zip(g_refs, out_refs))]
        for cp in cps:
            cp.start()
        for g, (g_ref, out_ref) in enumerate(zip(g_refs, out_refs)):
            _remote(g_ref, out_ref.at[1 - c], send_sems, recv_sems, g, (x, y, 1 - c)).wait_recv()
        for cp in cps:
            cp.wait_send()

    return pl.pallas_call(
        body, name="share_half", in_specs=[_HBM] * n, out_specs=[_HBM] * n,
        out_shape=[jax.ShapeDtypeStruct((2,) + g.shape, g.dtype) for g in gs],
        scratch_shapes=[pltpu.SemaphoreType.DMA((n,)), pltpu.SemaphoreType.DMA((n,))])(*gs)


def _pair_sum(name, pg, theirs, c, tile):
    _, _, rows, width = pg.shape

    def body(c_ref, a_ref, b_ref, o_ref):
        o_ref[...] = (a_ref[...] + b_ref[...]).astype(o_ref.dtype)

    return pl.pallas_call(
        body, name=name,
        grid_spec=pltpu.PrefetchScalarGridSpec(
            num_scalar_prefetch=1, grid=(N_CHIPS, rows // tile),
            in_specs=[pl.BlockSpec((None, None, tile, width), lambda j, i, c_ref: (j, c_ref[0], i, 0)),
                      pl.BlockSpec((None, tile, width), lambda j, i, c_ref: (j, i, 0))],
            out_specs=pl.BlockSpec((None, tile, width), lambda j, i, c_ref: (j, i, 0))),
        out_shape=jax.ShapeDtypeStruct((N_CHIPS, rows, width), WIRE_DTYPE),
        compiler_params=_params("arbitrary", "arbitrary"))(c.reshape(1), pg, theirs)


def _chip_sum(name, ps, others, k, tile):
    _, rows, width = ps.shape

    def body(k_ref, a_ref, b_ref, o_ref):
        o_ref[...] = ((a_ref[...].astype(F32) + b_ref[0].astype(F32)) + b_ref[1].astype(F32)) + b_ref[2].astype(F32)

    return pl.pallas_call(
        body, name=name,
        grid_spec=pltpu.PrefetchScalarGridSpec(
            num_scalar_prefetch=1, grid=(rows // tile,),
            in_specs=[pl.BlockSpec((None, tile, width), lambda i, k_ref: (k_ref[0], i, 0)),
                      pl.BlockSpec((3, tile, width), lambda i, k_ref: (0, i, 0))],
            out_specs=pl.BlockSpec((tile, width), lambda i, k_ref: (i, 0))),
        out_shape=jax.ShapeDtypeStruct((rows, width), F32),
        compiler_params=_params("arbitrary"))(k.reshape(1), ps, others)


def _mm_in_gathering(x, g1, prologue, in_wire, chip, tm):
    m, k = x.shape
    half, ns = in_wire.shape[1:]
    nrow = m // tm

    def flip(j):
        return jnp.where(j == 1, 2, jnp.where(j == 2, 1, j))

    def body(k_ref, x_ref, g_ref, wire_ref, proj_ref, u_ref, all_ref, kept, b_ref, load_sems, send_sems, recv_sems):
        j, i = pl.program_id(0), pl.program_id(1)
        px, py, c, chips = _place()
        sibling = (px, py, 1 - c)

        def over_ici(r, chip_slot):
            cx, cy = chips[r]
            return _remote(wire_ref.at[c], all_ref.at[chip_slot, c], send_sems, recv_sems, r, (cx, cy, c))

        def to_sibling(r, half_slot):
            cx, cy = chips[r]
            return _remote(all_ref.at[2 * cx + cy, c], all_ref.at[2 * cx + cy, half_slot], send_sems, recv_sems,
                           3 + r, sibling)

        def load(src):
            cps = [pltpu.make_async_copy(src.at[h], b_ref.at[pl.ds(h * half, half)], load_sems.at[h])
                   for h in range(2)]
            for cp in cps:
                cp.start()
            for cp in cps:
                cp.wait()

        @pl.when((j == 0) & (i == 0))
        def _():
            for r in range(2):
                over_ici(r, 2 * px + py).start()
            load(wire_ref)

        for r in range(3):
            @pl.when((j == r + 1) & (i == 0))
            def _(r=r):
                cx, cy = chips[r]
                over_ici(r, 2 * cx + cy).wait_recv()
                if r == 0:
                    over_ici(2, 2 * px + py).start()
                to_sibling(r, c).start()
                to_sibling(r, 1 - c).wait_recv()
                load(all_ref.at[2 * cx + cy])

        rows = pl.ds(pl.multiple_of(i * tm, tm), tm)

        @pl.when(j == 0)
        def _():
            tile = _mx(prologue(x_ref[...], g_ref[...]))
            kept[rows, :] = tile
            u_ref[...] = tile

        proj_ref[...] = _dot(kept[rows, :], b_ref[...])

        @pl.when((j == N_CHIPS - 1) & (i == nrow - 1))
        def _():
            for r in range(3):
                over_ici(r, 2 * px + py).wait_send()
                to_sibling(r, c).wait_send()

    once = lambda j, i, k_ref: (jnp.where(j == 0, i, nrow - 1), 0)
    return pl.pallas_call(
        body, name="mm_in",
        grid_spec=pltpu.PrefetchScalarGridSpec(
            num_scalar_prefetch=1, grid=(N_CHIPS, nrow),
            in_specs=[pl.BlockSpec((tm, k), once), pl.BlockSpec(g1.shape, lambda j, i, k_ref: (0, 0)), _HBM],
            out_specs=[pl.BlockSpec((tm, ns), lambda j, i, k_ref: (i, jnp.bitwise_xor(k_ref[0], flip(j)))),
                       pl.BlockSpec((tm, k), once), _HBM],
            scratch_shapes=[pltpu.VMEM((m, k), MXU_DTYPE), pltpu.VMEM((2 * half, ns), in_wire.dtype),
                            pltpu.SemaphoreType.DMA((2,)), pltpu.SemaphoreType.DMA((6,)),
                            pltpu.SemaphoreType.DMA((6,))]),
        out_shape=[jax.ShapeDtypeStruct((m, N_CHIPS * ns), F32), jax.ShapeDtypeStruct((m, k), MXU_DTYPE),
                   jax.ShapeDtypeStruct((N_CHIPS,) + in_wire.shape, in_wire.dtype)],
        compiler_params=_params("arbitrary", "arbitrary"))(chip.reshape(1), x, g1, in_wire)


class _StepComm:
    TILES = {"in": IN_TILE, "rest": REST_TILE}

    def __init__(self, in_wire, rest_wire, chip, core):
        self.in_wire, self.rest_wire, self.chip, self.core = in_wire, rest_wire, chip, core
        self.sums, self.landed = {}, {}

    def input_projection(self, x, g1, prologue, tm):
        proj, u, shards = _mm_in_gathering(x, g1, prologue, self.in_wire, self.chip, tm)
        shards = lax.dynamic_update_slice(shards, self.in_wire[None], (self.chip, 0, 0, 0))
        return proj, u, shards.reshape(N_CHIPS, D_MODEL, IN_SHARD)

    def gather_rest(self):
        wire = self.rest_wire

        def sends(ins, outs, send_sems, recv_sems):
            (w_ref,), (out_ref,) = ins, outs
            x, y, c, chips = _place()
            return [_remote(w_ref.at[c], out_ref.at[2 * x + y, c], send_sems, recv_sems, 4 * j + 2 * c + to,
                            (cx, cy, to)) for j, (cx, cy) in enumerate(chips) for to in (0, 1)]

        def recvs(ins, outs, send_sems, recv_sems):
            (w_ref,), (out_ref,) = ins, outs
            _, _, c, chips = _place()
            return [_remote(w_ref.at[c], out_ref.at[2 * cx + cy, by], send_sems, recv_sems, 4 * j + 2 * by + c,
                            (cx, cy, by)) for j, (cx, cy) in enumerate(chips) for by in (0, 1)]

        def start(*refs):
            for cp in sends(*refs):
                cp.start()

        def wait(*refs):
            for cp in recvs(*refs):
                cp.wait_recv()
            for cp in sends(*refs):
                cp.wait_send()

        return _Riding((wire,), (jax.ShapeDtypeStruct((N_CHIPS,) + wire.shape, wire.dtype),), 12, start, wait)

    def rest_weights(self, landed):
        full = lax.dynamic_update_slice(landed, self.rest_wire[None], (self.chip, 0, 0, 0))
        return _unpack_rest_full(full.reshape(N_CHIPS, REST_ROWS, PACK_W))

    def swap(self, pg):
        def copies(ins, outs, send_sems, recv_sems):
            (pg_ref,), (out_ref,) = ins, outs
            x, y, c, _ = _place()
            return [_remote(pg_ref.at[j, 1 - c], out_ref.at[j], send_sems, recv_sems, j, (x, y, 1 - c))
                    for j in range(N_CHIPS)]

        def start(*refs):
            for cp in copies(*refs):
                cp.start()

        def wait(*refs):
            for cp in copies(*refs):
                cp.wait()

        return _Riding((pg,), (jax.ShapeDtypeStruct((N_CHIPS,) + pg.shape[2:], pg.dtype),), N_CHIPS, start, wait)

    def scatter(self, group, pg, theirs=None):
        if theirs is None:
            (theirs,) = _swap_halves([pg], "exchange_halves_" + group)
        ps = _pair_sum("sum_pair_" + group, pg, theirs, self.core, self.TILES[group])
        self.sums[group] = ps

        def copies(ins, outs, send_sems, recv_sems):
            (ps_ref,), (out_ref,) = ins, outs
            _, _, c, chips = _place()
            return [_remote(ps_ref.at[2 * cx + cy], out_ref.at[j], send_sems, recv_sems, j, (cx, cy, c))
                    for j, (cx, cy) in enumerate(chips)]

        def start(*refs):
            for cp in copies(*refs):
                cp.start()

        def wait(*refs):
            for cp in copies(*refs):
                cp.wait()

        return _Riding((ps,), (jax.ShapeDtypeStruct((3,) + ps.shape[1:], ps.dtype),), 3, start, wait)

    def reduced(self, group):
        return _chip_sum("sum_chips_" + group, self.sums[group], self.landed[group], self.chip, self.TILES[group])


def _adamw(w, g, m, v):
    m = ADAM_B1 * m + (1.0 - ADAM_B1) * g
    v = ADAM_B2 * v + (1.0 - ADAM_B2) * (g * g)
    m_hat = m / (1.0 - ADAM_B1 ** ADAM_STEP)
    v_hat = v / (1.0 - ADAM_B2 ** ADAM_STEP)
    return -ADAM_LR * (m_hat / (jnp.sqrt(v_hat) + ADAM_EPS) + ADAM_WD * w), m, v


def _small_reduce_adamw(part, w, m, v):
    def body(part_ref, w_ref, m_ref, v_ref, g_ref, d_ref, nm_ref, nv_ref, all_ref, send_sems, recv_sems):
        x, y, c, chips = _place()
        me, sibling = (x, y, c), (x, y, 1 - c)

        def rows(px, py, pc):
            return all_ref.at[4 * px + 2 * py + pc]

        all_ref[4 * x + 2 * y + c] = part_ref[...]
        first = [_remote(part_ref, rows(*me), send_sems, recv_sems, 0, sibling)]
        first += [_remote(part_ref, rows(*me), send_sems, recv_sems, 1 + j, (cx, cy, c))
                  for j, (cx, cy) in enumerate(chips)]
        for cp in first:
            cp.start()
        passed = []
        for j, (cx, cy) in enumerate(chips):
            _remote(part_ref, rows(cx, cy, c), send_sems, recv_sems, 1 + j, me).wait_recv()
            cp = _remote(rows(cx, cy, c), rows(cx, cy, c), send_sems, recv_sems, 4 + j, sibling)
            cp.start()
            passed.append(cp)
        _remote(part_ref, rows(*sibling), send_sems, recv_sems, 0, me).wait_recv()
        for j, (cx, cy) in enumerate(chips):
            _remote(part_ref, rows(cx, cy, 1 - c), send_sems, recv_sems, 4 + j, me).wait_recv()
        for cp in first + passed:
            cp.wait_send()
        g = all_ref[0]
        for dev in range(1, N_DEV):
            g = g + all_ref[dev]
        delta, nm, nv = _adamw(w_ref[...], g, m_ref[...], v_ref[...])
        g_ref[...] = g
        d_ref[...] = delta
        nm_ref[...] = nm
        nv_ref[...] = nv

    whole = pl.BlockSpec(memory_space=pltpu.VMEM)
    shape = jax.ShapeDtypeStruct((SMALL_ROWS, PACK_W), F32)
    return pl.pallas_call(
        body, name="small_reduce_adamw", in_specs=[whole] * 4, out_specs=[whole] * 4, out_shape=[shape] * 4,
        scratch_shapes=[pltpu.VMEM((N_DEV, SMALL_ROWS, PACK_W), F32), pltpu.SemaphoreType.DMA((7,)),
                        pltpu.SemaphoreType.DMA((7,))],
        compiler_params=pltpu.CompilerParams(vmem_limit_bytes=VMEM_LIMIT))(part, w, m, v)


def kernel(x, p, norm_g, w_in, hg_lb, hg_norm_g, w_o_hg, s5_a_re, s5_a_im, s5_log_dt, s5_b_re, s5_b_im, s5_c_re, s5_c_im, s5_d, w_glu, b_glu, w_o_s5, w_out, ple_norm_g, w_ple, w_ple_gate, final_norm_g, loss_target, m_norm_g, m_w_in, m_hg_lb, m_hg_norm_g, m_w_o_hg, m_s5_a_re, m_s5_a_im, m_s5_log_dt, m_s5_b_re, m_s5_b_im, m_s5_c_re, m_s5_c_im, m_s5_d, m_w_glu, m_b_glu, m_w_o_s5, m_w_out, m_ple_norm_g, m_w_ple, m_w_ple_gate, m_final_norm_g, v_norm_g, v_w_in, v_hg_lb, v_hg_norm_g, v_w_o_hg, v_s5_a_re, v_s5_a_im, v_s5_log_dt, v_s5_b_re, v_s5_b_im, v_s5_c_re, v_s5_c_im, v_s5_d, v_w_glu, v_b_glu, v_w_o_s5, v_w_out, v_ple_norm_g, v_w_ple, v_w_ple_gate, v_final_norm_g):
    given = dict(locals())
    wts = {n: given[n] for n in WEIGHTS}
    mom = {n: given["m_" + n] for n in WEIGHTS}
    var = {n: given["v_" + n] for n in WEIGHTS}
    cx, cy, cc = lax.axis_index("x"), lax.axis_index("y"), lax.axis_index("c")
    chip = (2 * cx + cy).astype(jnp.int32)

    core = cc.astype(jnp.int32)
    rest_shard = _pack_rest({n: wts[n][0] for n in REST})
    comm = _StepComm(wts["w_in"][0].astype(MXU_DTYPE).reshape(2, D_MODEL // 2, IN_SHARD),
                     rest_shard.astype(MXU_DTYPE).reshape(2, REST_ROWS // 2, PACK_W), chip, core)

    t_len = x.shape[1]
    loss_row, grad_x, g_big, g_small = _local_step(x.reshape(t_len, D_MODEL), p.reshape(t_len, -1),
                                                   loss_target.reshape(t_len, D_MODEL), None,
                                                   {n: wts[n] for n in SMALL}, comm)

    zero = jnp.zeros((), F32)
    sg, sd, snm, snv = _small_reduce_adamw(_pack_small(g_small, loss_row[0, 0]),
                                           _pack_small({n: wts[n] for n in SMALL}, zero),
                                           _pack_small({n: mom[n] for n in SMALL}, zero),
                                           _pack_small({n: var[n] for n in SMALL}, zero))
    (sg, loss), (sd, _), (snm, _), (snv, _) = (_unpack_small(a) for a in (sg, sd, snm, snv))

    halves = [comm.reduced("in"), comm.reduced("rest")]
    g_in, g_rest = [lax.dynamic_update_slice(got, mine[None], (core, 0, 0))
                    for got, mine in zip(_share_halves(halves), halves)]
    g_in, g_rest = g_in.reshape(D_MODEL, IN_SHARD), g_rest.reshape(REST_ROWS, PACK_W)

    def adam_f(wv, gv, mv, vv):
        return _adamw(wv, gv, mv, vv)

    d_in, nm_in, nv_in = _rowwise("adamw_in", adam_f, D_MODEL, IN_TILE,
                                  [(wts["w_in"][0], IN_SHARD, 0), (g_in, IN_SHARD, 0), (mom["w_in"][0], IN_SHARD, 0),
                                   (var["w_in"][0], IN_SHARD, 0)], [], [(IN_SHARD, F32)] * 3)
    d_rest, nm_rest, nv_rest = _rowwise("adamw_rest", adam_f, REST_ROWS, REST_TILE,
                                        [(rest_shard, PACK_W, 0), (g_rest, PACK_W, 0),
                                         (_pack_rest({n: mom[n][0] for n in REST}), PACK_W, 0),
                                         (_pack_rest({n: var[n][0] for n in REST}), PACK_W, 0)], [],
                                        [(PACK_W, F32)] * 3)
    bg, bd, bnm, bnv = (dict(_unpack_rest(rest), w_in=a.reshape(1, D_MODEL, IN_SHARD))
                        for rest, a in ((g_rest, g_in), (d_rest, d_in), (nm_rest, nm_in), (nv_rest, nv_in)))

    outs = [loss, grad_x.reshape(x.shape)]
    for small, big in ((sg, bg), (sd, bd), (snm, bnm), (snv, bnv)):
        outs += [big[n] if n in BIG else small[n] for n in WEIGHTS]
    return tuple(outs)
```

```python
import functools
from typing import Callable, NamedTuple

import jax
import jax.numpy as jnp
from jax import lax
from jax.experimental import pallas as pl
from jax.experimental.pallas import tpu as pltpu

F32 = jnp.float32
MXU_DTYPE = jnp.bfloat16
WIRE_DTYPE = jnp.bfloat16
NORM_EPS = 1e-6
D_MODEL = 1024
HG_HEADS = 8
HG_DIM = 128
HG_CHUNK = 64
S5_WIDTH = 512
S5_GROUPS = 32
S5_GROUP = 16
S5_STATE = 64
S5_LANES = S5_GROUPS * S5_STATE
IN_COLS = 7168
SUBLANES = 8
VMEM_LIMIT = 56 * 1024 * 1024
HIGHEST = lax.Precision.HIGHEST
MESH = pl.DeviceIdType.MESH

ADAM_LR, ADAM_B1, ADAM_B2, ADAM_EPS, ADAM_WD, ADAM_STEP = 0.001, 0.9, 0.999, 1e-08, 0.01, 10

BIG = ("w_in", "w_o_hg", "w_glu", "w_o_s5", "w_out", "w_ple", "w_ple_gate")
BIG_SHAPE = {"w_in": (1024, 7168), "w_o_hg": (1024, 1024), "w_glu": (512, 1024), "w_o_s5": (512, 1024),
             "w_out": (1024, 1024), "w_ple": (256, 1024), "w_ple_gate": (1024, 1024)}
BIG_COL_SHARDED = ("w_in", "w_glu", "w_o_s5", "w_ple")
SMALL = ("norm_g", "hg_lb", "hg_norm_g", "s5_a_re", "s5_a_im", "s5_log_dt", "s5_b_re", "s5_b_im", "s5_c_re",
         "s5_c_im", "s5_d", "b_glu", "ple_norm_g", "final_norm_g")
SMALL_SHAPE = {"norm_g": (1, 1024), "hg_lb": (2, 1024), "hg_norm_g": (1, 1024), "s5_a_re": (1, 32, 64),
               "s5_a_im": (1, 32, 64), "s5_log_dt": (1, 32), "s5_b_re": (1, 32, 64, 16), "s5_b_im": (1, 32, 64, 16),
               "s5_c_re": (1, 32, 16, 64), "s5_c_im": (1, 32, 16, 64), "s5_d": (1, 32, 16), "b_glu": (1, 1024),
               "ple_norm_g": (1, 1024), "final_norm_g": (1024,)}
WEIGHTS = ("norm_g", "w_in", "hg_lb", "hg_norm_g", "w_o_hg", "s5_a_re", "s5_a_im", "s5_log_dt", "s5_b_re", "s5_b_im",
           "s5_c_re", "s5_c_im", "s5_d", "w_glu", "b_glu", "w_o_s5", "w_out", "ple_norm_g", "w_ple", "w_ple_gate",
           "final_norm_g")
N_CHIPS = 4
N_DEV = 8
PACK_W = 1024
SMALL_ROWS = 144


def _params(*sem):
    return pltpu.CompilerParams(dimension_semantics=sem, vmem_limit_bytes=VMEM_LIMIT)


def _sig(x):
    return 1.0 / (1.0 + jnp.exp(-x))


def _dsilu(z, s):
    return s * (1.0 + z * (1.0 - s))


def _mx(x):
    return x.astype(MXU_DTYPE)


def _dot(a, b, dims=(((1,), (0,)), ((), ()))):
    return lax.dot_general(_mx(a), _mx(b), dims, preferred_element_type=F32)


_NT = (((1,), (1,)), ((), ()))
_TN = (((0,), (0,)), ((), ()))


def _dot32(a, b):
    return jnp.dot(a, b, precision=HIGHEST, preferred_element_type=F32)


def _rms_bwd(dy, x, g):
    r = lax.rsqrt(jnp.mean(x * x, axis=-1, keepdims=True) + NORM_EPS)
    t = dy * g
    dx = r * t - x * (r * r * r) * jnp.mean(t * x, axis=-1, keepdims=True)
    return dx, jnp.sum(dy * x * r, axis=0, keepdims=True)


def _rowwise(name, fn, n_rows_total, tm, rows, consts, outs, accs=(), alias=None):
    n_r, n_c, n_o, n_a = len(rows), len(consts), len(outs), len(accs)

    def body(*refs):
        row_refs = refs[:n_r]
        const_refs = refs[n_r:n_r + n_c]
        pos = n_r + n_c + (1 if alias is not None else 0)
        out_refs = refs[pos:pos + n_o]
        acc_refs = refs[pos + n_o:pos + n_o + n_a]
        res = fn(*[r[...] for r in row_refs], *[r[...] for r in const_refs])
        for r, v in zip(out_refs, res[:n_o]):
            r[...] = v.astype(r.dtype)
        if n_a:
            @pl.when(pl.program_id(0) == 0)
            def _():
                for r in acc_refs:
                    r[...] = jnp.zeros_like(r)
            for r, v in zip(acc_refs, res[n_o:]):
                r[...] += v

    in_specs = [pl.BlockSpec((tm, w), functools.partial(lambda i, cb: (i, cb), cb=cb)) for (_, w, cb) in rows]
    in_specs += [pl.BlockSpec(c.shape, lambda i: (0, 0)) for c in consts]
    args = [a for (a, _, _) in rows] + list(consts)
    out_shape, out_specs = [], []
    for o in outs:
        w, dt = o[0], o[1]
        cb, total = (o[2], o[3]) if len(o) == 4 else (0, w)
        out_shape.append(jax.ShapeDtypeStruct((n_rows_total, total), dt))
        out_specs.append(pl.BlockSpec((tm, w), functools.partial(lambda i, cb: (i, cb), cb=cb)))
    io_alias = {}
    if alias is not None:
        in_specs.append(pl.BlockSpec(memory_space=pl.ANY))
        args.append(alias[0])
        io_alias = {len(args) - 1: alias[1]}
    for (r, w) in accs:
        out_shape.append(jax.ShapeDtypeStruct((r, w), F32))
        out_specs.append(pl.BlockSpec((r, w), lambda i: (0, 0)))
    res = pl.pallas_call(body, name=name, grid=(n_rows_total // tm,), in_specs=in_specs, out_specs=out_specs,
                         out_shape=out_shape, input_output_aliases=io_alias,
                         compiler_params=_params("arbitrary"))(*args)
    return res


class _Riding(NamedTuple):
    ins: tuple
    outs: tuple
    n_sems: int
    start: Callable
    wait: Callable


_HBM = pl.BlockSpec(memory_space=pl.ANY)


def _ride(riding, refs, n_in, n_out, n_scratch, first, last):
    if riding is None:
        return refs[:n_in], refs[n_in:n_in + n_out], refs[n_in + n_out:]
    r_in, r_out = len(riding.ins), len(riding.outs)
    ins, rins = refs[:n_in], refs[n_in:n_in + r_in]
    pos = n_in + r_in
    outs, routs = refs[pos:pos + n_out], refs[pos + n_out:pos + n_out + r_out]
    pos += n_out + r_out
    scratch, (send_sems, recv_sems) = refs[pos:pos + n_scratch], refs[pos + n_scratch:]

    @pl.when(first)
    def _():
        riding.start(rins, routs, send_sems, recv_sems)

    @pl.when(last)
    def _():
        riding.wait(rins, routs, send_sems, recv_sems)

    return ins, outs, scratch


def _riding_call(riding, body, name, grid, in_specs, args, out_specs, out_shape, scratch, io_alias=None):
    if riding is not None:
        in_specs = list(in_specs) + [_HBM] * len(riding.ins)
        args = list(args) + list(riding.ins)
        out_specs = list(out_specs) + [_HBM] * len(riding.outs)
        out_shape = list(out_shape) + list(riding.outs)
        scratch = list(scratch) + [pltpu.SemaphoreType.DMA((riding.n_sems,))] * 2
    return pl.pallas_call(body, name=name, grid=grid, in_specs=in_specs, out_specs=out_specs, out_shape=out_shape,
                          scratch_shapes=scratch, input_output_aliases=io_alias or {},
                          compiler_params=_params(*(["arbitrary"] * len(grid))))(*args)


def _mm_nn(name, a, b, tm, tn, riding=None, prologue=None, consts=()):
    m, k = a.shape
    n = b.shape[1] if b.ndim == 2 else b.shape[0] * b.shape[2]
    grid = (n // tn, m // tm)
    n_out, scratch = (1, []) if prologue is None else (2, [pltpu.VMEM((m, k), MXU_DTYPE)])

    def body(*refs):
        j, i = pl.program_id(0), pl.program_id(1)
        ins, outs, kept = _ride(riding, refs, 2 + len(consts), n_out, len(scratch), (j == 0) & (i == 0),
                                (j == grid[0] - 1) & (i == grid[1] - 1))
        if prologue is None:
            left = ins[0][...]
        else:
            rows = pl.ds(pl.multiple_of(i * tm, tm), tm)

            @pl.when(j == 0)
            def _():
                tile = _mx(prologue(ins[0][...], *[c[...] for c in ins[2:]]))
                kept[0][rows, :] = tile
                outs[1][...] = tile

            left = kept[0][rows, :]
        outs[0][...] = _dot(left, ins[1][...])

    once = (lambda j, i: (i, 0)) if prologue is None else (lambda j, i: (jnp.where(j == 0, i, grid[1] - 1), 0))
    b_spec = (pl.BlockSpec((k, tn), lambda j, i: (0, j)) if b.ndim == 2
              else pl.BlockSpec((None, k, tn), lambda j, i: (j, 0, 0)))
    in_specs = [pl.BlockSpec((tm, k), once), b_spec]
    in_specs += [pl.BlockSpec(c.shape, lambda j, i: (0, 0)) for c in consts]
    out_specs = [pl.BlockSpec((tm, tn), lambda j, i: (i, j))]
    out_shape = [jax.ShapeDtypeStruct((m, n), F32)]
    if prologue is not None:
        out_specs.append(pl.BlockSpec((tm, k), once))
        out_shape.append(jax.ShapeDtypeStruct((m, k), MXU_DTYPE))
    res = _riding_call(riding, body, name, grid, in_specs, [a, b] + list(consts), out_specs, out_shape, scratch)
    return res[0] if riding is None and prologue is None else res


def _mm_nt_then(name, a, b, tm, tn, fn, rows, consts, outs, accs=(), alias=None, riding=None):
    m, n = a.shape
    k = b.shape[-2]
    steps = n // tn
    n_r, n_c, n_o, n_a = len(rows), len(consts), len(outs), len(accs)

    def body(*refs):
        a_ref, b_ref = refs[:2]
        row_refs = refs[2:2 + n_r]
        const_refs = refs[2 + n_r:2 + n_r + n_c]
        i, s = pl.program_id(0), pl.program_id(1)
        n_in = 2 + n_r + n_c + (1 if alias is not None else 0)
        _, outs_, (mm_ref,) = _ride(riding, refs, n_in, n_o + n_a, 1, (i == 0) & (s == 0),
                                    (i == m // tm - 1) & (s == steps - 1))
        out_refs, acc_refs = outs_[:n_o], outs_[n_o:]
        part = _dot(a_ref[...], b_ref[...] if b.ndim == 2 else b_ref[s], _NT)
        if steps > 1:
            @pl.when(s == 0)
            def _():
                mm_ref[...] = jnp.zeros_like(mm_ref)
            mm_ref[...] += part

        @pl.when(s == steps - 1)
        def _():
            res = fn(mm_ref[...] if steps > 1 else part, *[r[...] for r in row_refs], *[r[...] for r in const_refs])
            for r, v in zip(out_refs, res[:n_o]):
                r[...] = v.astype(r.dtype)
            if n_a:
                @pl.when(i == 0)
                def _():
                    for r in acc_refs:
                        r[...] = jnp.zeros_like(r)
                for r, v in zip(acc_refs, res[n_o:]):
                    r[...] += v

    b_spec = (pl.BlockSpec((k, tn), lambda i, s: (0, s)) if b.ndim == 2
              else pl.BlockSpec(memory_space=pltpu.VMEM))
    in_specs = [pl.BlockSpec((tm, tn), lambda i, s: (i, s)), b_spec]
    in_specs += [pl.BlockSpec((tm, w), functools.partial(lambda i, s, cb: (i, cb), cb=cb)) for (_, w, cb) in rows]
    in_specs += [pl.BlockSpec(c.shape, lambda i, s: (0, 0)) for c in consts]
    args = [a, b] + [r[0] for r in rows] + list(consts)
    out_shape, out_specs = [], []
    for o in outs:
        w, dt = o[0], o[1]
        cb, total = (o[2], o[3]) if len(o) == 4 else (0, w)
        out_shape.append(jax.ShapeDtypeStruct((m, total), dt))
        out_specs.append(pl.BlockSpec((tm, w), functools.partial(lambda i, s, cb: (i, cb), cb=cb)))
    io_alias = {}
    if alias is not None:
        in_specs.append(pl.BlockSpec(memory_space=pl.ANY))
        args.append(alias[0])
        io_alias = {len(args) - 1: alias[1]}
    for (r, w) in accs:
        out_shape.append(jax.ShapeDtypeStruct((r, w), F32))
        out_specs.append(pl.BlockSpec((r, w), lambda i, s: (0, 0)))
    return _riding_call(riding, body, name, (m // tm, steps), in_specs, args, out_specs, out_shape,
                        [pltpu.VMEM((tm, k), F32)], io_alias)


def _mm_tn(name, a, b, tk, tn, col_shards=False, riding=None, a_resident=False):
    t, k = a.shape
    n = b.shape[1]
    steps = t // tk

    def body(*refs):
        j, s = pl.program_id(0), pl.program_id(1)
        (a_ref, b_ref), (o_ref,), (acc_ref,) = _ride(riding, refs, 2, 1, 1, (j == 0) & (s == 0),
                                                     (j == n // tn - 1) & (s == steps - 1))

        @pl.when(s == 0)
        def _():
            acc_ref[...] = jnp.zeros_like(acc_ref)

        left = a_ref[pl.ds(pl.multiple_of(s * tk, tk), tk), :] if a_resident else a_ref[...]
        acc_ref[...] += _dot(left, b_ref[...], _TN)

        @pl.when(s == steps - 1)
        def _():
            o_ref[...] = acc_ref[...]

    if col_shards:
        out_spec = pl.BlockSpec((None, k, tn), lambda j, s: (j, 0, 0))
        out_shape = jax.ShapeDtypeStruct((n // tn, k, tn), F32)
    else:
        out_spec = pl.BlockSpec((k, tn), lambda j, s: (0, j))
        out_shape = jax.ShapeDtypeStruct((k, n), F32)
    a_spec = pl.BlockSpec(memory_space=pltpu.VMEM) if a_resident else pl.BlockSpec((tk, k), lambda j, s: (s, 0))
    res = _riding_call(riding, body, name, (n // tn, steps), [a_spec, pl.BlockSpec((tk, tn), lambda j, s: (s, j))],
                       [a, b], [out_spec], [out_shape], [pltpu.VMEM((k, tn), F32)])
    return res[0] if riding is None else res


def _dot01(m01, x):
    m = m01.astype(MXU_DTYPE)
    hi = x.astype(MXU_DTYPE)
    r1 = x - hi.astype(F32)
    mid = r1.astype(MXU_DTYPE)
    lo = (r1 - mid.astype(F32)).astype(MXU_DTYPE)
    dot = lambda v: jnp.dot(m, v, preferred_element_type=F32)
    return dot(hi) + dot(mid) + dot(lo)


def _chunk_rows(x, offset, nck):
    return jnp.concatenate([jnp.broadcast_to(x[c * HG_CHUNK + offset:c * HG_CHUNK + offset + 1, :],
                                             (HG_CHUNK, x.shape[1])) for c in range(nck)], axis=0)


def _hg_block_terms(q, f, lb, tb):
    nck = tb // HG_CHUNK
    sig = _sig(f)
    fv = lb + (1.0 - lb) * sig
    kk = (1.0 - lb) * (1.0 - sig)
    row = lax.broadcasted_iota(jnp.int32, (tb, tb), 0)
    col = lax.broadcasted_iota(jnp.int32, (tb, tb), 1)
    same = jnp.right_shift(row, 6) == jnp.right_shift(col, 6)
    causal, anti = same & (row >= col), same & (row <= col)
    b = _dot01(causal, jnp.log(fv))
    b_mid, b_last = _chunk_rows(b, HG_CHUNK // 2 - 1, nck), _chunk_rows(b, HG_CHUNK - 1, nck)
    e_mid, e_mid_inv = jnp.exp(b - b_mid), jnp.exp(b_mid - b)
    e_b, e_last = jnp.exp(b), jnp.exp(b_last - b)
    dcs = [jnp.exp(b[c * HG_CHUNK + HG_CHUNK - 1:(c + 1) * HG_CHUNK, :]) for c in range(nck)]
    return sig, fv, kk, causal, anti, e_mid, e_mid_inv, e_b, e_last, dcs


def _hgrn2_fwd(proj, hg_lb, hg_norm_g, t_len, tb, riding=None):
    nck = tb // HG_CHUNK
    nb = t_len // tb

    def body(*refs):
        step = pl.program_id(0)
        ((p_ref, lb_ref, gn_ref), (o_ref, act_ref, sp_ref),
         (st_ref, a_s, bm_s, qd_s, kd_s, v_s, sc_s, inc_s)) = _ride(riding, refs, 3, 3, 8, step == 0, step == nb - 1)

        @pl.when(pl.program_id(0) == 0)
        def _():
            st_ref[...] = jnp.zeros_like(st_ref)

        lb = _sig(lb_ref[0:1, :] - lb_ref[1:2, :])
        q = p_ref[:, pl.ds(0, 1024)]
        _, _, kk, causal, _, e_mid, e_mid_inv, e_b, e_last, dcs = _hg_block_terms(q, p_ref[:, pl.ds(1024, 1024)],
                                                                                   lb, tb)
        a_s[...] = _mx(q * e_mid)
        bm_s[...] = _mx(kk * e_mid_inv)
        qd_s[...] = _mx(q * e_b)
        kd_s[...] = _mx(kk * e_last)
        v_s[...] = _mx(p_ref[:, pl.ds(2048, 1024)])
        heads = [pl.ds(h * HG_DIM, HG_DIM) for h in range(HG_HEADS)]
        chunks = [pl.ds(c * HG_CHUNK, HG_CHUNK) for c in range(nck)]
        for h, hs in enumerate(heads):
            sc_s[h] = _mx(jnp.where(causal, _dot(a_s[:, hs], bm_s[:, hs], _NT), 0.0))
        for h, hs in enumerate(heads):
            o_ref[:, hs] = _dot(sc_s[h], v_s[:, hs])
        for h, hs in enumerate(heads):
            for c, r in enumerate(chunks):
                inc_s[h, c] = _dot(v_s[r, hs], kd_s[r, hs], _TN)
        for c in range(nck):
            for h in range(HG_HEADS):
                st = st_ref[h]
                sp_ref[h, c] = st
                st_ref[h] = dcs[c][:, h * HG_DIM:(h + 1) * HG_DIM] * st + inc_s[h, c]
        for c, r in enumerate(chunks):
            for h, hs in enumerate(heads):
                o_ref[r, hs] += _dot(qd_s[r, hs], sp_ref[h, c], _NT)
        for h, hs in enumerate(heads):
            o = o_ref[:, hs]
            rr = lax.rsqrt(jnp.mean(o * o, axis=-1, keepdims=True) + NORM_EPS)
            g = p_ref[:, pl.ds(3072 + h * HG_DIM, HG_DIM)]
            act_ref[:, hs] = (o * rr * gn_ref[:, hs] * (g * _sig(g))).astype(act_ref.dtype)

    return _riding_call(
        riding, body, "hgrn2_fwd", (nb,),
        [pl.BlockSpec((tb, 4096), lambda i: (i, 0)), pl.BlockSpec((2, 1024), lambda i: (0, 0)),
         pl.BlockSpec((1, 1024), lambda i: (0, 0))],
        [proj, hg_lb, hg_norm_g],
        [pl.BlockSpec((tb, 1024), lambda i: (i, 0)), pl.BlockSpec((tb, 1024), lambda i: (i, 0)),
         pl.BlockSpec((HG_HEADS, nck, HG_DIM, HG_DIM), lambda i: (0, i, 0, 0))],
        [jax.ShapeDtypeStruct((t_len, 1024), F32), jax.ShapeDtypeStruct((t_len, 1024), MXU_DTYPE),
         jax.ShapeDtypeStruct((HG_HEADS, t_len // HG_CHUNK, HG_DIM, HG_DIM), F32)],
        [pltpu.VMEM((HG_HEADS, HG_DIM, HG_DIM), F32)] + [pltpu.VMEM((tb, 1024), MXU_DTYPE)] * 5
        + [pltpu.VMEM((HG_HEADS, tb, tb), MXU_DTYPE), pltpu.VMEM((HG_HEADS, nck, HG_DIM, HG_DIM), F32)])


def _hgrn2_bwd(proj, d_o, s_prev, hg_lb, dproj, t_len, tb, riding=None):
    nck = tb // HG_CHUNK
    nb = t_len // tb

    def body(*refs):
        step = pl.program_id(0)
        ((p_ref, do_ref, sp_ref, lb_ref, _), (dp_ref, dlb_ref),
         (ds_ref, acc_ref, a_s, bm_s, qd_s, kd_s, v_s, do_s, da_s, dbm_s, dqd_s, dkd_s, dv_s, ex_s, sc_s, dsc_s,
          up_s)) = _ride(riding, refs, 5, 2, 17, step == 0, step == nb - 1)

        @pl.when(pl.program_id(0) == 0)
        def _():
            ds_ref[...] = jnp.zeros_like(ds_ref)
            acc_ref[...] = jnp.zeros_like(acc_ref)

        lb = _sig(lb_ref[0:1, :] - lb_ref[1:2, :])
        q = p_ref[:, pl.ds(0, 1024)]
        sig, fv, kk, causal, anti, e_mid, e_mid_inv, e_b, e_last, dcs = _hg_block_terms(
            q, p_ref[:, pl.ds(1024, 1024)], lb, tb)
        a, bm, qd, kd = q * e_mid, kk * e_mid_inv, q * e_b, kk * e_last
        a_s[...] = _mx(a)
        bm_s[...] = _mx(bm)
        qd_s[...] = _mx(qd)
        kd_s[...] = _mx(kd)
        v_s[...] = _mx(p_ref[:, pl.ds(2048, 1024)])
        do_s[...] = _mx(do_ref[...])
        heads = [pl.ds(h * HG_DIM, HG_DIM) for h in range(HG_HEADS)]
        chunks = [pl.ds(c * HG_CHUNK, HG_CHUNK) for c in range(nck)]
        for h, hs in enumerate(heads):
            sc_s[h] = _mx(jnp.where(causal, _dot(a_s[:, hs], bm_s[:, hs], _NT), 0.0))
            dsc_s[h] = _mx(jnp.where(causal, _dot(do_s[:, hs], v_s[:, hs], _NT), 0.0))
        for h, hs in enumerate(heads):
            dv_s[:, hs] = _dot(sc_s[h], do_s[:, hs], _TN)
            da_s[:, hs] = _dot(dsc_s[h], bm_s[:, hs])
            dbm_s[:, hs] = _dot(dsc_s[h], a_s[:, hs], _TN)
        for h, hs in enumerate(heads):
            for c, r in enumerate(chunks):
                up_s[h, c] = _dot(do_s[r, hs], qd_s[r, hs], _TN)
                dqd_s[r, hs] = _dot(do_s[r, hs], sp_ref[h, c])
        for c in reversed(range(nck)):
            r = chunks[c]
            for h, hs in enumerate(heads):
                dst = ds_ref[h]
                dc = dcs[c][:, h * HG_DIM:(h + 1) * HG_DIM]
                dv_s[r, hs] += _dot(kd_s[r, hs], dst, _NT)
                dkd_s[r, hs] = _dot(v_s[r, hs], dst)
                ex_s[c:c + 1, hs] = jnp.sum(dst * sp_ref[h, c], axis=0, keepdims=True) * dc
                ds_ref[h] = up_s[h, c] + dc * dst
        da, dbm, dqd, dkd = da_s[...], dbm_s[...], dqd_s[...], dkd_s[...]
        dq = da * e_mid + dqd * e_b
        dk = dbm * e_mid_inv + dkd * e_last
        db = da * a - dbm * bm + dqd * qd - dkd * kd
        dkk = dkd * kd
        extra = jnp.concatenate(
            [jnp.broadcast_to(jnp.sum(dkk[c * HG_CHUNK:(c + 1) * HG_CHUNK], axis=0, keepdims=True)
                              + ex_s[c:c + 1, :], (HG_CHUNK, 1024)) for c in range(nck)], axis=0)
        dlogf = _dot01(anti, db) + extra
        dfv_k = dlogf / fv - dk
        dp_ref[:, pl.ds(0, 1024)] = dq.astype(dp_ref.dtype)
        dp_ref[:, pl.ds(1024, 1024)] = (dfv_k * (1.0 - lb) * sig * (1.0 - sig)).astype(dp_ref.dtype)
        dp_ref[:, pl.ds(2048, 1024)] = dv_s[...].astype(dp_ref.dtype)
        acc_ref[...] += jnp.sum(dfv_k * (1.0 - sig), axis=0, keepdims=True)

        @pl.when(pl.program_id(0) == nb - 1)
        def _():
            g0 = acc_ref[...] * lb * (1.0 - lb)
            dlb_ref[0:1, :] = g0
            dlb_ref[1:2, :] = -g0

    return _riding_call(
        riding, body, "hgrn2_bwd", (nb,),
        [pl.BlockSpec((tb, 3072), lambda i: (nb - 1 - i, 0)),
         pl.BlockSpec((tb, 1024), lambda i: (nb - 1 - i, 0)),
         pl.BlockSpec((HG_HEADS, nck, HG_DIM, HG_DIM), lambda i: (0, nb - 1 - i, 0, 0)),
         pl.BlockSpec((2, 1024), lambda i: (0, 0)),
         pl.BlockSpec(memory_space=pl.ANY)],
        [proj, d_o, s_prev, hg_lb, dproj],
        [pl.BlockSpec((tb, 3072), lambda i: (nb - 1 - i, 0)), pl.BlockSpec((2, 1024), lambda i: (0, 0))],
        [jax.ShapeDtypeStruct((t_len, IN_COLS), dproj.dtype), jax.ShapeDtypeStruct((2, 1024), F32)],
        [pltpu.VMEM((HG_HEADS, HG_DIM, HG_DIM), F32), pltpu.VMEM((1, 1024), F32)]
        + [pltpu.VMEM((tb, 1024), MXU_DTYPE)] * 6 + [pltpu.VMEM((tb, 1024), F32)] * 5
        + [pltpu.VMEM((SUBLANES, 1024), F32)] + [pltpu.VMEM((HG_HEADS, tb, tb), MXU_DTYPE)] * 2
        + [pltpu.VMEM((HG_HEADS, nck, HG_DIM, HG_DIM), F32)], {4: 0})


def _s5_prep_bwd(a_re, a_im, log_dt, b_re_t, b_im_t, dlam, dbbr, dbbi):
    def body(ar_ref, ai_ref, ldt_ref, br_ref, bi_ref, dlam_ref, dbbr_ref, dbbi_ref,
             dar_ref, dai_ref, dldt_ref, dbr_ref, dbi_ref):
        ar, ai = ar_ref[...], ai_ref[...]
        dt = jnp.exp(ldt_ref[...])
        mag = jnp.exp(ar * dt)
        cs, sn = jnp.cos(ai * dt), jnp.sin(ai * dt)
        lr, li = mag * cs, mag * sn
        den = ar * ar + ai * ai
        nr = lr - 1.0
        sr = (nr * ar + li * ai) / den
        si = (li * ar - nr * ai) / den
        br, bi = br_ref[...], bi_ref[...]
        gbr, gbi = dbbr_ref[...], dbbi_ref[...]
        dbr_ref[...] = sr * gbr + si * gbi
        dbi_ref[...] = sr * gbi - si * gbr
        dsr = jnp.sum(gbr * br + gbi * bi, axis=0, keepdims=True)
        dsi = jnp.sum(gbi * br - gbr * bi, axis=0, keepdims=True)
        dnr = (dsr * ar - dsi * ai) / den
        dli = dlam_ref[1:2, :] + (dsr * ai + dsi * ar) / den
        dlr = dlam_ref[0:1, :] + dnr
        dden = -(dsr * sr + dsi * si) / den
        dar = (dsr * nr + dsi * li) / den + dden * 2.0 * ar
        dai = (dsr * li - dsi * nr) / den + dden * 2.0 * ai
        dmag = dlr * cs + dli * sn
        dth = mag * (dli * cs - dlr * sn)
        dar_ref[...] = dar + dmag * mag * dt
        dai_ref[...] = dai + dth * dt
        ddt = (dmag * mag * ar + dth * ai) * dt
        lane = lax.broadcasted_iota(jnp.int32, (S5_LANES, 128), 0) // S5_STATE
        grp = lax.broadcasted_iota(jnp.int32, (S5_LANES, 128), 1)
        dldt_ref[...] = _dot32(jnp.broadcast_to(ddt, (SUBLANES, S5_LANES)), (lane == grp).astype(F32))

    whole = pl.BlockSpec(memory_space=pltpu.VMEM)
    return pl.pallas_call(
        body, name="s5_prep_bwd", in_specs=[whole] * 8, out_specs=[whole] * 5,
        out_shape=[jax.ShapeDtypeStruct((1, S5_LANES), F32), jax.ShapeDtypeStruct((1, S5_LANES), F32),
                   jax.ShapeDtypeStruct((SUBLANES, 128), F32), jax.ShapeDtypeStruct((S5_GROUP, S5_LANES), F32),
                   jax.ShapeDtypeStruct((S5_GROUP, S5_LANES), F32)])(a_re, a_im, log_dt, b_re_t, b_im_t, dlam, dbbr,
                                                                      dbbi)


def _dgelu(x):
    c, a = 0.7978845608028654, 0.044715
    th = jnp.tanh(c * (x + a * x * x * x))
    return 0.5 * (1.0 + th) + 0.5 * x * (1.0 - th * th) * c * (1.0 + 3.0 * a * x * x)


S5_BLOCKS = 4
S5_BW = S5_WIDTH // S5_BLOCKS
S5_BL = S5_LANES // S5_BLOCKS
S5_LANE_BLOCKS = S5_LANES // 128
S5_SCAN_BLOCKS = 4


def _s5_prep(a_re, a_im, log_dt, b_re_t, b_im_t, seg):
    def body(ar_ref, ai_ref, ldt_ref, br_ref, bi_ref,
             rows_f, pfr_ref, pfi_ref, rows_r, prr_ref, pri_ref, bbr_ref, bbi_ref):
        ar, ai = ar_ref[...], ai_ref[...]
        dt = jnp.exp(ldt_ref[...])
        mag = jnp.exp(ar * dt)
        lr, li = mag * jnp.cos(ai * dt), mag * jnp.sin(ai * dt)
        den = ar * ar + ai * ai
        nr = lr - 1.0
        sr = (nr * ar + li * ai) / den
        si = (li * ar - nr * ai) / den
        wide = (SUBLANES, S5_LANES)
        cr, ci = lr, li
        for i in range(seg):
            pfr_ref[i] = jnp.broadcast_to(cr, wide)
            pfi_ref[i] = jnp.broadcast_to(ci, wide)
            prr_ref[seg - 1 - i] = jnp.broadcast_to(cr, wide)
            pri_ref[seg - 1 - i] = jnp.broadcast_to(-ci, wide)
            if i == seg - 1:
                for rows, sign in ((rows_f, 1.0), (rows_r, -1.0)):
                    rows[0:1, :] = lr
                    rows[1:2, :] = sign * li
                    rows[2:3, :] = cr
                    rows[3:4, :] = sign * ci
            cr, ci = cr * lr - ci * li, cr * li + ci * lr
        br, bi = br_ref[...], bi_ref[...]
        bbr_ref[...] = sr * br - si * bi
        bbi_ref[...] = sr * bi + si * br

    whole = pl.BlockSpec(memory_space=pltpu.VMEM)
    tables = [jax.ShapeDtypeStruct((4, S5_LANES), F32)] + [jax.ShapeDtypeStruct((seg, SUBLANES, S5_LANES), F32)] * 2
    bbar = [jax.ShapeDtypeStruct((S5_GROUP, S5_LANES), F32)] * 2
    res = pl.pallas_call(body, name="s5_prep", in_specs=[whole] * 5, out_specs=[whole] * 8,
                         out_shape=tables + tables + bbar)(a_re, a_im, log_dt, b_re_t, b_im_t)
    return res[0:3], res[3:6], res[6], res[7]


def _lanes(j):
    return pl.ds(j * 128, 128)


def _to_segment_order(v, stage_ref, out_ref, seg):
    nbl = v.shape[1] // 128
    for b in range(nbl):
        stage_ref[b] = v[:, b * 128:(b + 1) * 128]

    def body(t, carry):
        rows = pl.ds(pl.multiple_of(t * SUBLANES, SUBLANES), SUBLANES)
        for b in range(nbl):
            out_ref[rows, _lanes(b)] = stage_ref[b, pl.ds(t, SUBLANES, stride=seg), :]
        return carry

    lax.fori_loop(0, seg, body, 0, unroll=True)


def _from_segment_order(v, stage_ref, out_ref, seg):
    nbl = v.shape[1] // 128
    for b in range(nbl):
        stage_ref[b] = v[:, b * 128:(b + 1) * 128]
    for s in range(SUBLANES):
        def body(k, carry, s=s):
            rows = pl.ds(pl.multiple_of(s * seg + k * SUBLANES, SUBLANES), SUBLANES)
            for b in range(nbl):
                out_ref[rows, _lanes(b)] = stage_ref[b, pl.ds(k * SUBLANES * SUBLANES + s, SUBLANES,
                                                              stride=SUBLANES), :]
            return carry

        lax.fori_loop(0, seg // SUBLANES, body, 0, unroll=True)


def _tile_scan(xr_ref, xi_ref, lam_ref, car_ref, cai_ref, cn_r, cn_i, blocks, seg, reverse):
    shape = (SUBLANES, 128)
    lrs = [jnp.broadcast_to(lam_ref[0:1, _lanes(j)], shape) for j in blocks]
    lis = [jnp.broadcast_to(lam_ref[1:2, _lanes(j)], shape) for j in blocks]

    def step(k, carry):
        t = seg - 1 - k if reverse else k
        rows = pl.ds(pl.multiple_of(t * SUBLANES, SUBLANES), SUBLANES)
        out = []
        for n, j in enumerate(blocks):
            cr, ci = carry[2 * n], carry[2 * n + 1]
            nr = lrs[n] * cr - lis[n] * ci + xr_ref[rows, _lanes(j)]
            ni = lrs[n] * ci + lis[n] * cr + xi_ref[rows, _lanes(j)]
            xr_ref[rows, _lanes(j)] = nr
            xi_ref[rows, _lanes(j)] = ni
            out += [nr, ni]
        return tuple(out)

    zero = jnp.zeros(shape, F32)
    fin = lax.fori_loop(0, seg, step, (zero,) * (2 * len(blocks)), unroll=True)
    for n, j in enumerate(blocks):
        ls = _lanes(j)
        fr, fi = fin[2 * n], fin[2 * n + 1]
        sr, si = lam_ref[2:3, ls], lam_ref[3:4, ls]
        pr, pi = car_ref[:, ls], cai_ref[:, ls]
        for s in (reversed(range(SUBLANES)) if reverse else range(SUBLANES)):
            cn_r[s:s + 1, ls] = pr
            cn_i[s:s + 1, ls] = pi
            pr, pi = fr[s:s + 1, :] + sr * pr - si * pi, fi[s:s + 1, :] + sr * pi + si * pr
        car_ref[:, ls] = pr
        cai_ref[:, ls] = pi


def _s5_fwd(proj, lam_rows, p3_re, p3_im, bbr4, bbi4, crt4, cit4, d_row, t_len, tb):
    seg = tb // SUBLANES

    def body(u_ref, lam_ref, p3r_ref, p3i_ref, bbr_ref, bbi_ref, crt_ref, cit_ref, d_ref,
             hr_ref, hi_ref, ypre_ref, ys_ref, car_ref, cai_ref, cn_r, cn_i, stage_ref, us_ref, yseg_ref):
        @pl.when(pl.program_id(0) == 0)
        def _():
            car_ref[...] = jnp.zeros_like(car_ref)
            cai_ref[...] = jnp.zeros_like(cai_ref)

        _to_segment_order(u_ref[...], stage_ref, us_ref, seg)
        u = us_ref[...]
        for i in range(S5_BLOCKS):
            ui = u[:, i * S5_BW:(i + 1) * S5_BW]
            hr_ref[:, pl.ds(i * S5_BL, S5_BL)] = _dot(ui, bbr_ref[i])
            hi_ref[:, pl.ds(i * S5_BL, S5_BL)] = _dot(ui, bbi_ref[i])
        for lc in range(S5_LANE_BLOCKS // S5_SCAN_BLOCKS):
            blocks = range(lc * S5_SCAN_BLOCKS, (lc + 1) * S5_SCAN_BLOCKS)
            _tile_scan(hr_ref, hi_ref, lam_ref, car_ref, cai_ref, cn_r, cn_i, blocks, seg, False)
            crs = [cn_r[:, _lanes(j)] for j in blocks]
            cis = [cn_i[:, _lanes(j)] for j in blocks]

            def fix(t, carry, blocks=blocks, crs=crs, cis=cis):
                rows = pl.ds(pl.multiple_of(t * SUBLANES, SUBLANES), SUBLANES)
                for n, j in enumerate(blocks):
                    pr, pi = p3r_ref[t, :, _lanes(j)], p3i_ref[t, :, _lanes(j)]
                    hr_ref[rows, _lanes(j)] += pr * crs[n] - pi * cis[n]
                    hi_ref[rows, _lanes(j)] += pr * cis[n] + pi * crs[n]
                return carry

            lax.fori_loop(0, seg, fix, 0, unroll=True)
        for i in range(S5_BLOCKS):
            ws = pl.ds(i * S5_BW, S5_BW)
            bl = pl.ds(i * S5_BL, S5_BL)
            yseg_ref[:, ws] = (_dot(hr_ref[:, bl], crt_ref[i]) - _dot(hi_ref[:, bl], cit_ref[i])
                               + d_ref[:, ws] * u[:, i * S5_BW:(i + 1) * S5_BW])
        _from_segment_order(yseg_ref[...], stage_ref, ypre_ref, seg)
        ys_ref[...] = jax.nn.gelu(ypre_ref[...], approximate=True).astype(ys_ref.dtype)

    whole = pl.BlockSpec(memory_space=pltpu.VMEM)
    return pl.pallas_call(
        body, name="s5_fwd", grid=(t_len // tb,),
        in_specs=[pl.BlockSpec((tb, S5_WIDTH), lambda i: (i, 4096 // S5_WIDTH))] + [whole] * 8,
        out_specs=[pl.BlockSpec((tb, S5_LANES), lambda i: (i, 0)), pl.BlockSpec((tb, S5_LANES), lambda i: (i, 0)),
                   pl.BlockSpec((tb, S5_WIDTH), lambda i: (i, 0)), pl.BlockSpec((tb, S5_WIDTH), lambda i: (i, 0))],
        out_shape=[jax.ShapeDtypeStruct((t_len, S5_LANES), F32), jax.ShapeDtypeStruct((t_len, S5_LANES), F32),
                   jax.ShapeDtypeStruct((t_len, S5_WIDTH), F32), jax.ShapeDtypeStruct((t_len, S5_WIDTH), MXU_DTYPE)],
        scratch_shapes=[pltpu.VMEM((1, S5_LANES), F32), pltpu.VMEM((1, S5_LANES), F32),
                        pltpu.VMEM((SUBLANES, S5_LANES), F32), pltpu.VMEM((SUBLANES, S5_LANES), F32),
                        pltpu.VMEM((S5_WIDTH // 128, tb, 128), F32), pltpu.VMEM((tb, S5_WIDTH), F32),
                        pltpu.VMEM((tb, S5_WIDTH), F32)],
        compiler_params=_params("arbitrary"))(proj, lam_rows, p3_re, p3_im, bbr4, bbi4, crt4, cit4, d_row)


def _s5_bwd(dgelu, y_pre, proj, h_re, h_im, lam_rows, p3_re, p3_im, bbr4, bbi4, cr4, ci4, d_row, dproj, t_len, tb):
    seg = tb // SUBLANES
    nb = t_len // tb

    def body(dg_ref, yp_ref, u_ref, hr_ref, hi_ref, lam_ref, p3r_ref, p3i_ref, bbr_ref, bbi_ref, cr_ref, ci_ref,
             d_ref, _, du_ref, dbbr_ref, dbbi_ref, dcr_ref, dci_ref, dd_ref, dlam_ref,
             gr_ref, gi_ref, car_ref, cai_ref, cn_r, cn_i, stage_ref, us_ref, dys_ref, duseg_ref):
        @pl.when(pl.program_id(0) == 0)
        def _():
            for ref in (car_ref, cai_ref, dbbr_ref, dbbi_ref, dcr_ref, dci_ref, dd_ref, dlam_ref):
                ref[...] = jnp.zeros_like(ref)

        _to_segment_order(u_ref[...], stage_ref, us_ref, seg)
        _to_segment_order(dg_ref[...] * _dgelu(yp_ref[...]), stage_ref, dys_ref, seg)
        u, dy = us_ref[...], dys_ref[...]
        for i in range(S5_BLOCKS):
            dyi = dy[:, i * S5_BW:(i + 1) * S5_BW]
            gr_ref[:, pl.ds(i * S5_BL, S5_BL)] = _dot(dyi, cr_ref[i])
            gi_ref[:, pl.ds(i * S5_BL, S5_BL)] = -_dot(dyi, ci_ref[i])
        for lc in range(S5_LANE_BLOCKS // S5_SCAN_BLOCKS):
            blocks = range(lc * S5_SCAN_BLOCKS, (lc + 1) * S5_SCAN_BLOCKS)
            _tile_scan(gr_ref, gi_ref, lam_ref, car_ref, cai_ref, cn_r, cn_i, blocks, seg, True)
            crs = [cn_r[:, _lanes(j)] for j in blocks]
            cis = [cn_i[:, _lanes(j)] for j in blocks]

            def fix(k, carry, blocks=blocks, crs=crs, cis=cis):
                t = seg - 1 - k
                rows = pl.ds(pl.multiple_of(t * SUBLANES, SUBLANES), SUBLANES)
                out = []
                for n, j in enumerate(blocks):
                    nr, ni, slr, sli = carry[4 * n:4 * n + 4]
                    pr, pi = p3r_ref[t, :, _lanes(j)], p3i_ref[t, :, _lanes(j)]
                    g_r = gr_ref[rows, _lanes(j)] + pr * crs[n] - pi * cis[n]
                    g_i = gi_ref[rows, _lanes(j)] + pr * cis[n] + pi * crs[n]
                    gr_ref[rows, _lanes(j)] = g_r
                    gi_ref[rows, _lanes(j)] = g_i
                    hr, hi = hr_ref[rows, _lanes(j)], hi_ref[rows, _lanes(j)]
                    out += [g_r, g_i, slr + nr * hr + ni * hi, sli + ni * hr - nr * hi]
                return tuple(out)

            zero = jnp.zeros((SUBLANES, 128), F32)
            init = []
            for n in range(len(blocks)):
                init += [crs[n], cis[n], zero, zero]
            fin = lax.fori_loop(0, seg, fix, tuple(init), unroll=True)
            for n, j in enumerate(blocks):
                dlam_ref[0:1, _lanes(j)] += jnp.sum(fin[4 * n + 2], axis=0, keepdims=True)
                dlam_ref[1:2, _lanes(j)] += jnp.sum(fin[4 * n + 3], axis=0, keepdims=True)
        for i in range(S5_BLOCKS):
            ws = pl.ds(i * S5_BW, S5_BW)
            bl = pl.ds(i * S5_BL, S5_BL)
            ui, dyi = u[:, i * S5_BW:(i + 1) * S5_BW], dy[:, i * S5_BW:(i + 1) * S5_BW]
            gr, gi = gr_ref[:, bl], gi_ref[:, bl]
            duseg_ref[:, ws] = _dot(gr, bbr_ref[i], _NT) + _dot(gi, bbi_ref[i], _NT) + d_ref[:, ws] * dyi
            dbbr_ref[i] += _dot(ui, gr, _TN)
            dbbi_ref[i] += _dot(ui, gi, _TN)
            dcr_ref[i] += _dot(hr_ref[:, bl], dyi, _TN)
            dci_ref[i] -= _dot(hi_ref[:, bl], dyi, _TN)
        dd_ref[...] += jnp.sum(dy * u, axis=0, keepdims=True)
        _from_segment_order(duseg_ref[...], stage_ref, duseg_ref, seg)
        du_ref[...] = duseg_ref[...].astype(du_ref.dtype)

    whole = pl.BlockSpec(memory_space=pltpu.VMEM)
    rev = lambda i: (nb - 1 - i, 0)
    const3 = lambda i: (0, 0, 0)
    return pl.pallas_call(
        body, name="s5_bwd", grid=(nb,),
        in_specs=[pl.BlockSpec((tb, S5_WIDTH), rev), pl.BlockSpec((tb, S5_WIDTH), rev),
                  pl.BlockSpec((tb, S5_WIDTH), lambda i: (nb - 1 - i, 4096 // S5_WIDTH)),
                  pl.BlockSpec((tb, S5_LANES), rev), pl.BlockSpec((tb, S5_LANES), rev)] + [whole] * 8
                 + [pl.BlockSpec(memory_space=pl.ANY)],
        out_specs=[pl.BlockSpec((tb, S5_WIDTH), lambda i: (nb - 1 - i, 4096 // S5_WIDTH)),
                   pl.BlockSpec((S5_BLOCKS, S5_BW, S5_BL), const3), pl.BlockSpec((S5_BLOCKS, S5_BW, S5_BL), const3),
                   pl.BlockSpec((S5_BLOCKS, S5_BL, S5_BW), const3), pl.BlockSpec((S5_BLOCKS, S5_BL, S5_BW), const3),
                   pl.BlockSpec((1, S5_WIDTH), lambda i: (0, 0)), pl.BlockSpec((2, S5_LANES), lambda i: (0, 0))],
        out_shape=[jax.ShapeDtypeStruct((t_len, IN_COLS), dproj.dtype),
                   jax.ShapeDtypeStruct((S5_BLOCKS, S5_BW, S5_BL), F32),
                   jax.ShapeDtypeStruct((S5_BLOCKS, S5_BW, S5_BL), F32),
                   jax.ShapeDtypeStruct((S5_BLOCKS, S5_BL, S5_BW), F32),
                   jax.ShapeDtypeStruct((S5_BLOCKS, S5_BL, S5_BW), F32),
                   jax.ShapeDtypeStruct((1, S5_WIDTH), F32), jax.ShapeDtypeStruct((2, S5_LANES), F32)],
        scratch_shapes=[pltpu.VMEM((tb, S5_LANES), F32), pltpu.VMEM((tb, S5_LANES), F32),
                        pltpu.VMEM((1, S5_LANES), F32), pltpu.VMEM((1, S5_LANES), F32),
                        pltpu.VMEM((SUBLANES, S5_LANES), F32), pltpu.VMEM((SUBLANES, S5_LANES), F32),
                        pltpu.VMEM((S5_WIDTH // 128, tb, 128), F32), pltpu.VMEM((tb, S5_WIDTH), F32),
                        pltpu.VMEM((tb, S5_WIDTH), F32), pltpu.VMEM((tb, S5_WIDTH), F32)],
        input_output_aliases={13: 0},
        compiler_params=_params("arbitrary"))(dgelu, y_pre, proj, h_re, h_im, lam_rows, p3_re, p3_im, bbr4, bbi4,
                                              cr4, ci4, d_row, dproj)


def _block_diag(per_group):
    g8 = S5_GROUPS // S5_BLOCKS
    eye = jnp.eye(g8, dtype=bool)[None, :, None, :, None]
    dense = jnp.where(eye, per_group.reshape(S5_BLOCKS, g8, S5_GROUP, 1, S5_STATE), 0.0)
    return dense.reshape(S5_BLOCKS, S5_BW, S5_BL)


def _diag_blocks(dense):
    g8 = S5_GROUPS // S5_BLOCKS
    ar = jnp.arange(g8)
    d5 = dense.reshape(S5_BLOCKS, g8, S5_GROUP, g8, S5_STATE)
    return d5[:, ar, :, ar, :].transpose(1, 0, 2, 3).reshape(S5_GROUPS, S5_GROUP, S5_STATE)


def _hg_gate_bwd(da, o, g, gn):
    dos, dgs, dgns = [], [], []
    for h in range(HG_HEADS):
        sl = slice(h * HG_DIM, (h + 1) * HG_DIM)
        oh, gh, dah, gnh = o[:, sl], g[:, sl], da[:, sl], gn[:, sl]
        rr = lax.rsqrt(jnp.mean(oh * oh, axis=-1, keepdims=True) + NORM_EPS)
        sg = _sig(gh)
        dgs.append(dah * (oh * rr * gnh) * _dsilu(gh, sg))
        don = dah * (gh * sg)
        t = don * gnh
        dos.append(rr * t - oh * (rr * rr * rr) * jnp.mean(t * oh, axis=-1, keepdims=True))
        dgns.append(jnp.sum(don * oh * rr, axis=0, keepdims=True))
    return jnp.concatenate(dos, axis=1), jnp.concatenate(dgs, axis=1), jnp.concatenate(dgns, axis=1)


MIX_BWD_COLS = ((3072, 1024), (4608, 512), (5120, 1024), (6144, 1024))


def _mix_bwd(dgl, h1, dh2, act_hg, ys2, ys_gelu, proj, o_hg, g2, ghn, b_glu, w, t_len, tm):
    nb = t_len // tm

    def body(dgl_ref, h1_ref, dh2_ref, act_ref, ys2_ref, ysg_ref, ghg_ref, z_ref, gh_ref, gs_ref, o_ref, g2_ref, gn_ref,
             bglu_ref, wg_ref, wo_ref, ws5_ref, whg_ref, wglu_ref,
             dh1_ref, dyh_ref, dys_ref, dglu_ref, dgelu_ref, do_ref, dg2_ref, dbglu_ref, dgn_ref, dproj_ref,
             st0, st1, st2, st3, sems):
        i = pl.program_id(0)
        stages = (st0, st1, st2, st3)

        def writes(step):
            rows = pl.ds(pl.multiple_of(step * tm, tm), tm)
            return [pltpu.make_async_copy(st, dproj_ref.at[rows, pl.ds(c0, wd)], sems.at[k])
                    for k, (st, (c0, wd)) in enumerate(zip(stages, MIX_BWD_COLS))]

        @pl.when(i > 0)
        def _():
            for cp in writes(i - 1):
                cp.wait()

        @pl.when(i == 0)
        def _():
            for ref in (dg2_ref, dbglu_ref, dgn_ref):
                ref[...] = jnp.zeros_like(ref)

        dx, dg2 = _rms_bwd(_dot(dgl_ref[...], wg_ref[...], _NT), h1_ref[...], g2_ref[...])
        dh1 = dh2_ref[...] + dx
        dh1_ref[...] = dh1
        dg2_ref[...] += dg2
        dm = _dot(dh1, wo_ref[...], _NT)
        sh, ss = _sig(gh_ref[...]), _sig(gs_ref[...])
        dyh, dys = _mx(dm * sh), _mx(dm * ss)
        dyh_ref[...] = dyh
        dys_ref[...] = dys
        st2[...] = (dm * _dot(act_ref[...], whg_ref[...]) * sh * (1.0 - sh)).astype(st2.dtype)
        st3[...] = (dm * _dot(ys2_ref[...], ws5_ref[...]) * ss * (1.0 - ss)).astype(st3.dtype)
        dys2 = _dot(dys, ws5_ref[...], _NT)
        gl_, z = _dot(ysg_ref[...], wglu_ref[...]) + bglu_ref[...], z_ref[...]
        a, b = gl_[:, :S5_WIDTH], gl_[:, S5_WIDTH:]
        sb, sz = _sig(b), _sig(z)
        silu = z * sz
        dglu = jnp.concatenate([dys2 * sb * silu, dys2 * a * silu * sb * (1.0 - sb)], axis=1)
        st1[...] = (dys2 * a * sb * _dsilu(z, sz)).astype(st1.dtype)
        dbglu_ref[...] += jnp.sum(dglu, axis=0, keepdims=True)
        dglu_ref[...] = _mx(dglu)
        dgelu_ref[...] = _dot(dglu, wglu_ref[...], _NT)
        d_o, dg, dgn = _hg_gate_bwd(_dot(dyh, whg_ref[...], _NT), o_ref[...], ghg_ref[...], gn_ref[...])
        do_ref[...] = d_o.astype(do_ref.dtype)
        st0[...] = dg.astype(st0.dtype)
        dgn_ref[...] += dgn
        for cp in writes(i):
            cp.start()

        @pl.when(i == nb - 1)
        def _():
            for cp in writes(i):
                cp.wait()

    tile = lambda wd, cb=0: pl.BlockSpec((tm, wd), functools.partial(lambda i, cb: (i, cb), cb=cb))
    row = lambda wd: pl.BlockSpec((1, wd), lambda i: (0, 0))
    whole = pl.BlockSpec(memory_space=pltpu.VMEM)
    return pl.pallas_call(
        body, name="mix_bwd", grid=(nb,),
        in_specs=[tile(1024), tile(1024), tile(1024), tile(1024), tile(512), tile(512), tile(1024, 3),
                  tile(512, 4608 // 512), tile(1024, 5), tile(1024, 6), tile(1024), row(1024), row(1024), row(1024)]
                 + [whole] * 5,
        out_specs=[tile(1024), tile(1024), tile(1024), tile(1024), tile(512), tile(1024), row(1024), row(1024),
                   row(1024), _HBM],
        out_shape=[jax.ShapeDtypeStruct((t_len, 1024), F32), jax.ShapeDtypeStruct((t_len, 1024), MXU_DTYPE),
                   jax.ShapeDtypeStruct((t_len, 1024), MXU_DTYPE), jax.ShapeDtypeStruct((t_len, 1024), MXU_DTYPE),
                   jax.ShapeDtypeStruct((t_len, 512), F32), jax.ShapeDtypeStruct((t_len, 1024), MXU_DTYPE),
                   jax.ShapeDtypeStruct((1, 1024), F32), jax.ShapeDtypeStruct((1, 1024), F32),
                   jax.ShapeDtypeStruct((1, 1024), F32), jax.ShapeDtypeStruct((t_len, IN_COLS), MXU_DTYPE)],
        scratch_shapes=[pltpu.VMEM((tm, wd), MXU_DTYPE) for _, wd in MIX_BWD_COLS] + [pltpu.SemaphoreType.DMA((4,))],
        compiler_params=_params("arbitrary"))(dgl, h1, dh2, act_hg, ys2, ys_gelu, proj, proj, proj, proj, o_hg, g2, ghn,
                                              b_glu, w["w_ple_gate"], w["w_out"], w["w_o_s5"], w["w_o_hg"],
                                              w["w_glu"])


def _local_step(x, p, target, w, sm, comm=None):
    t_len = x.shape[0]
    tm = min(256, t_len)
    tmm = min(512, t_len)
    tb_hg = min(256, t_len)
    tb_s5 = min(256, t_len)
    g1, g2, g3, ghn = sm["norm_g"], sm["ple_norm_g"], sm["final_norm_g"].reshape(1, D_MODEL), sm["hg_norm_g"]

    def rms_in(xv, g):
        return xv * lax.rsqrt(jnp.mean(xv * xv, axis=-1, keepdims=True) + NORM_EPS) * g

    in_shard = IN_COLS // N_CHIPS
    if comm is None:
        w_in = w["w_in"]
        proj, u = _mm_nn("mm_in", x, w_in, tmm, in_shard, prologue=rms_in, consts=[g1])
    else:
        proj, u, w_in = comm.input_projection(x, g1, rms_in, tmm)

    lanes = lambda a: a.reshape(1, S5_LANES)
    a_re, a_im = lanes(sm["s5_a_re"]), lanes(sm["s5_a_im"])
    ldt = lanes(jnp.broadcast_to(sm["s5_log_dt"].reshape(S5_GROUPS, 1), (S5_GROUPS, S5_STATE)))
    to_t = lambda b: b.reshape(S5_GROUPS, S5_STATE, S5_GROUP).transpose(2, 0, 1).reshape(S5_GROUP, S5_LANES)
    b_re_t, b_im_t = to_t(sm["s5_b_re"]), to_t(sm["s5_b_im"])
    scan_fwd, scan_rev, bbr_t, bbi_t = _s5_prep(a_re, a_im, ldt, b_re_t, b_im_t, tb_s5 // SUBLANES)
    from_t = lambda b: b.reshape(S5_GROUP, S5_GROUPS, S5_STATE).transpose(1, 0, 2)
    bbr_bd = _block_diag(from_t(bbr_t)).astype(MXU_DTYPE)
    bbi_bd = _block_diag(from_t(bbi_t)).astype(MXU_DTYPE)
    cr_bd = _block_diag(sm["s5_c_re"].reshape(S5_GROUPS, S5_GROUP, S5_STATE)).astype(MXU_DTYPE)
    ci_bd = _block_diag(sm["s5_c_im"].reshape(S5_GROUPS, S5_GROUP, S5_STATE)).astype(MXU_DTYPE)
    d_row = sm["s5_d"].reshape(1, S5_WIDTH)
    if comm is None:
        o_hg, act_hg, s_prev = _hgrn2_fwd(proj, sm["hg_lb"], ghn, t_len, tb_hg)
    else:
        o_hg, act_hg, s_prev, landed = _hgrn2_fwd(proj, sm["hg_lb"], ghn, t_len, tb_hg, riding=comm.gather_rest())
        w = comm.rest_weights(landed)
    h_re, h_im, y_pre, ys_gelu = _s5_fwd(proj, *scan_fwd, bbr_bd, bbi_bd,
                                          cr_bd.transpose(0, 2, 1), ci_bd.transpose(0, 2, 1), d_row, t_len, tb_s5)
    def mix_f(act, ysg, z, gh, gs, xv, w_glu, b_glu, w_o_hg, w_o_s5, w_out):
        gl_ = _dot(ysg, w_glu) + b_glu
        a, b = gl_[:, :S5_WIDTH], gl_[:, S5_WIDTH:]
        ys2_ = (a * _sig(b) * (z * _sig(z))).astype(MXU_DTYPE)
        yh, ys = _dot(act, w_o_hg), _dot(ys2_, w_o_s5)
        mg = (_sig(gh) * yh + _sig(gs) * ys).astype(MXU_DTYPE)
        return (ys2_, mg, xv + _dot(mg, w_out))

    ys2, merged, h1 = _rowwise(
        "mix_out", mix_f, t_len, tm,
        [(act_hg, 1024, 0), (ys_gelu, 512, 0), (proj, 512, 4608 // 512), (proj, 1024, 5), (proj, 1024, 6),
         (x, 1024, 0)], [w["w_glu"], sm["b_glu"], w["w_o_hg"], w["w_o_s5"], w["w_out"]],
        [(512, MXU_DTYPE), (1024, MXU_DTYPE), (1024, F32)])

    def head_f(h1v, pv, tgt, g_ple, g, w_ple, w_gate):
        r2 = lax.rsqrt(jnp.mean(h1v * h1v, axis=-1, keepdims=True) + NORM_EPS)
        n2_ = (h1v * r2 * g_ple).astype(MXU_DTYPE)
        glv, pev = _dot(n2_, w_gate), _dot(pv, w_ple)
        gate = _sig(glv)
        h2 = h1v + pev * gate
        r = lax.rsqrt(jnp.mean(h2 * h2, axis=-1, keepdims=True) + NORM_EPS)
        e = h2 * r * g - tgt
        loss = 0.5 * jnp.sum(jnp.mean(e * e, axis=-1, keepdims=True), axis=0, keepdims=True)
        dy = e * (1.0 / D_MODEL)
        dg = jnp.sum(dy * h2 * r, axis=0, keepdims=True)
        t = dy * g
        dh2 = r * t - h2 * (r * r * r) * jnp.mean(t * h2, axis=-1, keepdims=True)
        return (n2_, dh2, dh2 * gate, dh2 * pev * gate * (1.0 - gate), jnp.broadcast_to(loss, (1, 128)), dg)

    n2, dh2, dpe, dgl, loss_row, d_g3 = _rowwise(
        "ple_loss_head", head_f, t_len, tm, [(h1, 1024, 0), (p, 256, 0), (target, 1024, 0)],
        [g2, g3, w["w_ple"], w["w_ple_gate"]],
        [(1024, MXU_DTYPE), (1024, F32), (1024, MXU_DTYPE), (1024, MXU_DTYPE)], accs=[(1, 128), (1, 1024)])

    gb = {}
    gb["w_ple"] = _mm_tn("mm_d_w_ple", p, dpe, tmm, 1024)
    gb["w_ple_gate"] = _mm_tn("mm_d_w_ple_gate", n2, dgl, tmm, 1024)
    dh1, dy_hg, dy_s5, dglu, dgelu, d_o, d_g2, d_bglu, d_ghn, dproj = _mix_bwd(
        dgl, h1, dh2, act_hg, ys2, ys_gelu, proj, o_hg, g2, ghn, sm["b_glu"], w, t_len, tm)
    gb["w_out"] = _mm_tn("mm_d_w_out", merged, dh1, tmm, 1024)
    gb["w_o_s5"] = _mm_tn("mm_d_w_o_s5", ys2, dy_s5, tmm, 1024)
    gb["w_glu"] = _mm_tn("mm_d_w_glu", ys_gelu, dglu, tmm, 1024)
    dproj, d_bbr, d_bbi, d_crt, d_cit, d_d, d_lam = _s5_bwd(dgelu, y_pre, proj, h_re, h_im,
                                                            *scan_rev, bbr_bd, bbi_bd, cr_bd,
                                                            ci_bd, d_row, dproj, t_len, tb_s5)
    to_t3 = lambda b: b.transpose(1, 0, 2).reshape(S5_GROUP, S5_LANES)
    d_are, d_aim, d_ldt, d_br_t, d_bi_t = _s5_prep_bwd(a_re, a_im, ldt, b_re_t, b_im_t, d_lam,
                                                       to_t3(_diag_blocks(d_bbr)), to_t3(_diag_blocks(d_bbi)))
    gb["w_o_hg"] = _mm_tn("mm_d_w_o_hg", act_hg, dy_hg, tmm, 1024)
    if comm is None:
        dproj, d_lb = _hgrn2_bwd(proj, d_o, s_prev, sm["hg_lb"], dproj, t_len, tb_hg)
    else:
        rest_grads = _pack_rest_full(gb)
        dproj, d_lb, rest_theirs = _hgrn2_bwd(proj, d_o, s_prev, sm["hg_lb"], dproj, t_len, tb_hg,
                                               riding=comm.swap(rest_grads))

    def in_b(duv, xv, dh, g):
        dx, dg = _rms_bwd(duv, xv, g)
        return (dh + dx, dg)

    in_args = ("mm_d_u_rms_in_bwd", dproj, w_in, tmm, in_shard, in_b, [(x, 1024, 0), (dh1, 1024, 0)], [g1],
               [(1024, F32)])
    if comm is None:
        gb["w_in"] = _mm_tn("mm_d_w_in", u, dproj, tmm, in_shard, col_shards=True, a_resident=True)
        grad_x, d_g1 = _mm_nt_then(*in_args, accs=[(1, 1024)])
    else:
        gb["w_in"], landed = _mm_tn("mm_d_w_in", u, dproj, tmm, in_shard, col_shards=True, a_resident=True,
                                    riding=comm.scatter("rest", rest_grads, rest_theirs))
        comm.landed["rest"] = landed
        grad_x, d_g1, landed = _mm_nt_then(*in_args, accs=[(1, 1024)], riding=comm.scatter(
            "in", gb["w_in"].reshape(N_CHIPS, 2, D_MODEL // 2, in_shard)))
        comm.landed["in"] = landed

    back_t = lambda b: b.reshape(S5_GROUP, S5_GROUPS, S5_STATE).transpose(1, 2, 0).reshape(1, S5_GROUPS, S5_STATE,
                                                                                           S5_GROUP)
    gs = {
        "norm_g": d_g1, "hg_lb": d_lb, "hg_norm_g": d_ghn,
        "s5_a_re": d_are.reshape(1, S5_GROUPS, S5_STATE), "s5_a_im": d_aim.reshape(1, S5_GROUPS, S5_STATE),
        "s5_log_dt": d_ldt[0:1, :S5_GROUPS],
        "s5_b_re": back_t(d_br_t), "s5_b_im": back_t(d_bi_t),
        "s5_c_re": _diag_blocks(d_crt.transpose(0, 2, 1)).reshape(1, S5_GROUPS, S5_GROUP, S5_STATE),
        "s5_c_im": _diag_blocks(d_cit.transpose(0, 2, 1)).reshape(1, S5_GROUPS, S5_GROUP, S5_STATE),
        "s5_d": d_d.reshape(1, S5_GROUPS, S5_GROUP), "b_glu": d_bglu, "ple_norm_g": d_g2,
        "final_norm_g": d_g3.reshape(D_MODEL),
    }
    return loss_row, grad_x, gb, gs


def _shard_shape(name):
    r, c = BIG_SHAPE[name]
    return (r, c // N_CHIPS) if name in BIG_COL_SHARDED else (r // N_CHIPS, c)


def _pack_small(parts, last):
    flat = jnp.concatenate([parts[n].reshape(-1) for n in SMALL] + [last.reshape(-1)])
    return jnp.pad(flat, (0, SMALL_ROWS * PACK_W - flat.shape[0])).reshape(SMALL_ROWS, PACK_W)


def _unpack_small(packed):
    flat, out, off = packed.reshape(-1), {}, 0
    for n in SMALL:
        size = 1
        for d in SMALL_SHAPE[n]:
            size *= d
        out[n] = flat[off:off + size].reshape(SMALL_SHAPE[n])
        off += size
    return out, flat[off]


def _place():
    x, y, c = lax.axis_index("x"), lax.axis_index("y"), lax.axis_index("c")
    return x, y, c, [(1 - x, y), (x, 1 - y), (1 - x, 1 - y)]


def _remote(src, dst, send_sems, recv_sems, k, to):
    return pltpu.make_async_remote_copy(src_ref=src, dst_ref=dst, send_sem=send_sems.at[k], recv_sem=recv_sems.at[k],
                                        device_id=to, device_id_type=MESH)


REST = tuple(n for n in BIG if n != "w_in")
REST_ROWS = sum(BIG_SHAPE[n][0] * BIG_SHAPE[n][1] for n in REST) // (N_CHIPS * PACK_W)
IN_SHARD = IN_COLS // N_CHIPS
IN_TILE, REST_TILE = 256, 272


def _pack_rest(parts):
    return jnp.concatenate([parts[n].reshape(-1, PACK_W) for n in REST], axis=0)


def _unpack_rest(packed):
    out, off = {}, 0
    for n in REST:
        r, c = _shard_shape(n)
        rows = r * c // PACK_W
        out[n] = packed[off:off + rows].reshape(1, r, c)
        off += rows
    return out


def _unpack_rest_full(gathered):
    out, off = {}, 0
    for n in REST:
        r, c = _shard_shape(n)
        rows = r * c // PACK_W
        sh = gathered[:, off:off + rows].reshape(N_CHIPS, r, c)
        out[n] = sh.transpose(1, 0, 2).reshape(BIG_SHAPE[n]) if n in BIG_COL_SHARDED else sh.reshape(BIG_SHAPE[n])
        off += rows
    return out


def _pack_rest_full(full):
    parts = []
    for n in REST:
        r, c = _shard_shape(n)
        g = full[n]
        sh = g.reshape(BIG_SHAPE[n][0], N_CHIPS, c).transpose(1, 0, 2) if n in BIG_COL_SHARDED else g
        parts.append(sh.reshape(N_CHIPS, r * c // PACK_W, PACK_W))
    return jnp.concatenate(parts, axis=1).reshape(N_CHIPS, 2, REST_ROWS // 2, PACK_W)


def _swap_halves(pgs, name="exchange_halves"):
    n = len(pgs)

    def body(*refs):
        pg_refs, out_refs, (send_sems, recv_sems) = refs[:n], refs[n:2 * n], refs[2 * n:]
        x, y, c, _ = _place()
        cps = [_remote(pg_ref.at[j, 1 - c], out_ref.at[j], send_sems, recv_sems, N_CHIPS * g + j, (x, y, 1 - c))
               for g, (pg_ref, out_ref) in enumerate(zip(pg_refs, out_refs)) for j in range(N_CHIPS)]
        for cp in cps:
            cp.start()
        for cp in cps:
            cp.wait()

    return pl.pallas_call(
        body, name=name, in_specs=[_HBM] * n, out_specs=[_HBM] * n,
        out_shape=[jax.ShapeDtypeStruct((N_CHIPS,) + pg.shape[2:], pg.dtype) for pg in pgs],
        scratch_shapes=[pltpu.SemaphoreType.DMA((N_CHIPS * n,)), pltpu.SemaphoreType.DMA((N_CHIPS * n,))])(*pgs)


def _share_halves(gs):
    n = len(gs)

    def body(*refs):
        g_refs, out_refs, (send_sems, recv_sems) = refs[:n], refs[n:2 * n], refs[2 * n:]
        x, y, c, _ = _place()
        cps = [_remote(g_ref, out_ref.at[c], send_sems, recv_sems, g, (x, y, 1 - c))
               for g, (g_ref, out_ref) in enumerate(zip(g_refs, out_refs))]
        for cp in cps:
            cp.start()
        for g, (g_ref, out_ref) in enumerate(zip(g_refs, out_refs)):
            _remote(g_ref, out_ref.at[1 - c], send_sems, recv_sems, g, (x, y, 1 - c)).wait_recv()
        for cp in cps:
            cp.wait_send()

    return pl.pallas_call(
        body, name="share_half", in_specs=[_HBM] * n, out_specs=[_HBM] * n,
        out_shape=[jax.ShapeDtypeStruct((2,) + g.shape, g.dtype) for g in gs],
        scratch_shapes=[pltpu.SemaphoreType.DMA((n,)), pltpu.SemaphoreType.DMA((n,))])(*gs)


def _pair_sum(name, pg, theirs, c, tile):
    _, _, rows, width = pg.shape

    def body(c_ref, a_ref, b_ref, o_ref):
        o_ref[...] = (a_ref[...] + b_ref[...]).astype(o_ref.dtype)

    return pl.pallas_call(
        body, name=name,
        grid_spec=pltpu.PrefetchScalarGridSpec(
            num_scalar_prefetch=1, grid=(N_CHIPS, rows // tile),
            in_specs=[pl.BlockSpec((None, None, tile, width), lambda j, i, c_ref: (j, c_ref[0], i, 0)),
                      pl.BlockSpec((None, tile, width), lambda j, i, c_ref: (j, i, 0))],
            out_specs=pl.BlockSpec((None, tile, width), lambda j, i, c_ref: (j, i, 0))),
        out_shape=jax.ShapeDtypeStruct((N_CHIPS, rows, width), WIRE_DTYPE),
        compiler_params=_params("arbitrary", "arbitrary"))(c.reshape(1), pg, theirs)


def _chip_sum(name, ps, others, k, tile):
    _, rows, width = ps.shape

    def body(k_ref, a_ref, b_ref, o_ref):
        o_ref[...] = ((a_ref[...].astype(F32) + b_ref[0].astype(F32)) + b_ref[1].astype(F32)) + b_ref[2].astype(F32)

    return pl.pallas_call(
        body, name=name,
        grid_spec=pltpu.PrefetchScalarGridSpec(
            num_scalar_prefetch=1, grid=(rows // tile,),
            in_specs=[pl.BlockSpec((None, tile, width), lambda i, k_ref: (k_ref[0], i, 0)),
                      pl.BlockSpec((3, tile, width), lambda i, k_ref: (0, i, 0))],
            out_specs=pl.BlockSpec((tile, width), lambda i, k_ref: (i, 0))),
        out_shape=jax.ShapeDtypeStruct((rows, width), F32),
        compiler_params=_params("arbitrary"))(k.reshape(1), ps, others)


def _mm_in_gathering(x, g1, prologue, in_wire, chip, tm):
    m, k = x.shape
    half, ns = in_wire.shape[1:]
    nrow = m // tm

    def flip(j):
        return jnp.where(j == 1, 2, jnp.where(j == 2, 1, j))

    def body(k_ref, x_ref, g_ref, wire_ref, proj_ref, u_ref, all_ref, kept, b_ref, load_sems, send_sems, recv_sems):
        j, i = pl.program_id(0), pl.program_id(1)
        px, py, c, chips = _place()
        sibling = (px, py, 1 - c)

        def over_ici(r, chip_slot):
            cx, cy = chips[r]
            return _remote(wire_ref.at[c], all_ref.at[chip_slot, c], send_sems, recv_sems, r, (cx, cy, c))

        def to_sibling(r, half_slot):
            cx, cy = chips[r]
            return _remote(all_ref.at[2 * cx + cy, c], all_ref.at[2 * cx + cy, half_slot], send_sems, recv_sems,
                           3 + r, sibling)

        def load(src):
            cps = [pltpu.make_async_copy(src.at[h], b_ref.at[pl.ds(h * half, half)], load_sems.at[h])
                   for h in range(2)]
            for cp in cps:
                cp.start()
            for cp in cps:
                cp.wait()

        @pl.when((j == 0) & (i == 0))
        def _():
            for r in range(2):
                over_ici(r, 2 * px + py).start()
            load(wire_ref)

        for r in range(3):
            @pl.when((j == r + 1) & (i == 0))
            def _(r=r):
                cx, cy = chips[r]
                over_ici(r, 2 * cx + cy).wait_recv()
                if r == 0:
                    over_ici(2, 2 * px + py).start()
                to_sibling(r, c).start()
                to_sibling(r, 1 - c).wait_recv()
                load(all_ref.at[2 * cx + cy])

        rows = pl.ds(pl.multiple_of(i * tm, tm), tm)

        @pl.when(j == 0)
        def _():
            tile = _mx(prologue(x_ref[...], g_ref[...]))
            kept[rows, :] = tile
            u_ref[...] = tile

        proj_ref[...] = _dot(kept[rows, :], b_ref[...])

        @pl.when((j == N_CHIPS - 1) & (i == nrow - 1))
        def _():
            for r in range(3):
                over_ici(r, 2 * px + py).wait_send()
                to_sibling(r, c).wait_send()

    once = lambda j, i, k_ref: (jnp.where(j == 0, i, nrow - 1), 0)
    return pl.pallas_call(
        body, name="mm_in",
        grid_spec=pltpu.PrefetchScalarGridSpec(
            num_scalar_prefetch=1, grid=(N_CHIPS, nrow),
            in_specs=[pl.BlockSpec((tm, k), once), pl.BlockSpec(g1.shape, lambda j, i, k_ref: (0, 0)), _HBM],
            out_specs=[pl.BlockSpec((tm, ns), lambda j, i, k_ref: (i, jnp.bitwise_xor(k_ref[0], flip(j)))),
                       pl.BlockSpec((tm, k), once), _HBM],
            scratch_shapes=[pltpu.VMEM((m, k), MXU_DTYPE), pltpu.VMEM((2 * half, ns), in_wire.dtype),
                            pltpu.SemaphoreType.DMA((2,)), pltpu.SemaphoreType.DMA((6,)),
                            pltpu.SemaphoreType.DMA((6,))]),
        out_shape=[jax.ShapeDtypeStruct((m, N_CHIPS * ns), F32), jax.ShapeDtypeStruct((m, k), MXU_DTYPE),
                   jax.ShapeDtypeStruct((N_CHIPS,) + in_wire.shape, in_wire.dtype)],
        compiler_params=_params("arbitrary", "arbitrary"))(chip.reshape(1), x, g1, in_wire)


class _StepComm:
    TILES = {"in": IN_TILE, "rest": REST_TILE}

    def __init__(self, in_wire, rest_wire, chip, core):
        self.in_wire, self.rest_wire, self.chip, self.core = in_wire, rest_wire, chip, core
        self.sums, self.landed = {}, {}

    def input_projection(self, x, g1, prologue, tm):
        proj, u, shards = _mm_in_gathering(x, g1, prologue, self.in_wire, self.chip, tm)
        shards = lax.dynamic_update_slice(shards, self.in_wire[None], (self.chip, 0, 0, 0))
        return proj, u, shards.reshape(N_CHIPS, D_MODEL, IN_SHARD)

    def gather_rest(self):
        wire = self.rest_wire

        def sends(ins, outs, send_sems, recv_sems):
            (w_ref,), (out_ref,) = ins, outs
            x, y, c, chips = _place()
            return [_remote(w_ref.at[c], out_ref.at[2 * x + y, c], send_sems, recv_sems, 4 * j + 2 * c + to,
                            (cx, cy, to)) for j, (cx, cy) in enumerate(chips) for to in (0, 1)]

        def recvs(ins, outs, send_sems, recv_sems):
            (w_ref,), (out_ref,) = ins, outs
            _, _, c, chips = _place()
            return [_remote(w_ref.at[c], out_ref.at[2 * cx + cy, by], send_sems, recv_sems, 4 * j + 2 * by + c,
                            (cx, cy, by)) for j, (cx, cy) in enumerate(chips) for by in (0, 1)]

        def start(*refs):
            for cp in sends(*refs):
                cp.start()

        def wait(*refs):
            for cp in recvs(*refs):
                cp.wait_recv()
            for cp in sends(*refs):
                cp.wait_send()

        return _Riding((wire,), (jax.ShapeDtypeStruct((N_CHIPS,) + wire.shape, wire.dtype),), 12, start, wait)

    def rest_weights(self, landed):
        full = lax.dynamic_update_slice(landed, self.rest_wire[None], (self.chip, 0, 0, 0))
        return _unpack_rest_full(full.reshape(N_CHIPS, REST_ROWS, PACK_W))

    def swap(self, pg):
        def copies(ins, outs, send_sems, recv_sems):
            (pg_ref,), (out_ref,) = ins, outs
            x, y, c, _ = _place()
            return [_remote(pg_ref.at[j, 1 - c], out_ref.at[j], send_sems, recv_sems, j, (x, y, 1 - c))
                    for j in range(N_CHIPS)]

        def start(*refs):
            for cp in copies(*refs):
                cp.start()

        def wait(*refs):
            for cp in copies(*refs):
                cp.wait()

        return _Riding((pg,), (jax.ShapeDtypeStruct((N_CHIPS,) + pg.shape[2:], pg.dtype),), N_CHIPS, start, wait)

    def scatter(self, group, pg, theirs=None):
        if theirs is None:
            (theirs,) = _swap_halves([pg], "exchange_halves_" + group)
        ps = _pair_sum("sum_pair_" + group, pg, theirs, self.core, self.TILES[group])
        self.sums[group] = ps

        def copies(ins, outs, send_sems, recv_sems):
            (ps_ref,), (out_ref,) = ins, outs
            _, _, c, chips = _place()
            return [_remote(ps_ref.at[2 * cx + cy], out_ref.at[j], send_sems, recv_sems, j, (cx, cy, c))
                    for j, (cx, cy) in enumerate(chips)]

        def start(*refs):
            for cp in copies(*refs):
                cp.start()

        def wait(*refs):
            for cp in copies(*refs):
                cp.wait()

        return _Riding((ps,), (jax.ShapeDtypeStruct((3,) + ps.shape[1:], ps.dtype),), 3, start, wait)

    def reduced(self, group):
        return _chip_sum("sum_chips_" + group, self.sums[group], self.landed[group], self.chip, self.TILES[group])


def _adamw(w, g, m, v):
    m = ADAM_B1 * m + (1.0 - ADAM_B1) * g
    v = ADAM_B2 * v + (1.0 - ADAM_B2) * (g * g)
    m_hat = m / (1.0 - ADAM_B1 ** ADAM_STEP)
    v_hat = v / (1.0 - ADAM_B2 ** ADAM_STEP)
    return -ADAM_LR * (m_hat / (jnp.sqrt(v_hat) + ADAM_EPS) + ADAM_WD * w), m, v


def _small_reduce_adamw(part, w, m, v):
    def body(part_ref, w_ref, m_ref, v_ref, g_ref, d_ref, nm_ref, nv_ref, all_ref, send_sems, recv_sems):
        x, y, c, chips = _place()
        me, sibling = (x, y, c), (x, y, 1 - c)

        def rows(px, py, pc):
            return all_ref.at[4 * px + 2 * py + pc]

        all_ref[4 * x + 2 * y + c] = part_ref[...]
        first = [_remote(part_ref, rows(*me), send_sems, recv_sems, 0, sibling)]
        first += [_remote(part_ref, rows(*me), send_sems, recv_sems, 1 + j, (cx, cy, c))
                  for j, (cx, cy) in enumerate(chips)]
        for cp in first:
            cp.start()
        passed = []
        for j, (cx, cy) in enumerate(chips):
            _remote(part_ref, rows(cx, cy, c), send_sems, recv_sems, 1 + j, me).wait_recv()
            cp = _remote(rows(cx, cy, c), rows(cx, cy, c), send_sems, recv_sems, 4 + j, sibling)
            cp.start()
            passed.append(cp)
        _remote(part_ref, rows(*sibling), send_sems, recv_sems, 0, me).wait_recv()
        for j, (cx, cy) in enumerate(chips):
            _remote(part_ref, rows(cx, cy, 1 - c), send_sems, recv_sems, 4 + j, me).wait_recv()
        for cp in first + passed:
            cp.wait_send()
        g = all_ref[0]
        for dev in range(1, N_DEV):
            g = g + all_ref[dev]
        delta, nm, nv = _adamw(w_ref[...], g, m_ref[...], v_ref[...])
        g_ref[...] = g
        d_ref[...] = delta
        nm_ref[...] = nm
        nv_ref[...] = nv

    whole = pl.BlockSpec(memory_space=pltpu.VMEM)
    shape = jax.ShapeDtypeStruct((SMALL_ROWS, PACK_W), F32)
    return pl.pallas_call(
        body, name="small_reduce_adamw", in_specs=[whole] * 4, out_specs=[whole] * 4, out_shape=[shape] * 4,
        scratch_shapes=[pltpu.VMEM((N_DEV, SMALL_ROWS, PACK_W), F32), pltpu.SemaphoreType.DMA((7,)),
                        pltpu.SemaphoreType.DMA((7,))],
        compiler_params=pltpu.CompilerParams(vmem_limit_bytes=VMEM_LIMIT))(part, w, m, v)


def kernel(x, p, norm_g, w_in, hg_lb, hg_norm_g, w_o_hg, s5_a_re, s5_a_im, s5_log_dt, s5_b_re, s5_b_im, s5_c_re, s5_c_im, s5_d, w_glu, b_glu, w_o_s5, w_out, ple_norm_g, w_ple, w_ple_gate, final_norm_g, loss_target, m_norm_g, m_w_in, m_hg_lb, m_hg_norm_g, m_w_o_hg, m_s5_a_re, m_s5_a_im, m_s5_log_dt, m_s5_b_re, m_s5_b_im, m_s5_c_re, m_s5_c_im, m_s5_d, m_w_glu, m_b_glu, m_w_o_s5, m_w_out, m_ple_norm_g, m_w_ple, m_w_ple_gate, m_final_norm_g, v_norm_g, v_w_in, v_hg_lb, v_hg_norm_g, v_w_o_hg, v_s5_a_re, v_s5_a_im, v_s5_log_dt, v_s5_b_re, v_s5_b_im, v_s5_c_re, v_s5_c_im, v_s5_d, v_w_glu, v_b_glu, v_w_o_s5, v_w_out, v_ple_norm_g, v_w_ple, v_w_ple_gate, v_final_norm_g):
    given = dict(locals())
    wts = {n: given[n] for n in WEIGHTS}
    mom = {n: given["m_" + n] for n in WEIGHTS}
    var = {n: given["v_" + n] for n in WEIGHTS}
    cx, cy, cc = lax.axis_index("x"), lax.axis_index("y"), lax.axis_index("c")
    chip = (2 * cx + cy).astype(jnp.int32)

    core = cc.astype(jnp.int32)
    rest_shard = _pack_rest({n: wts[n][0] for n in REST})
    comm = _StepComm(wts["w_in"][0].astype(MXU_DTYPE).reshape(2, D_MODEL // 2, IN_SHARD),
                     rest_shard.astype(MXU_DTYPE).reshape(2, REST_ROWS // 2, PACK_W), chip, core)

    t_len = x.shape[1]
    loss_row, grad_x, g_big, g_small = _local_step(x.reshape(t_len, D_MODEL), p.reshape(t_len, -1),
                                                   loss_target.reshape(t_len, D_MODEL), None,
                                                   {n: wts[n] for n in SMALL}, comm)

    zero = jnp.zeros((), F32)
    sg, sd, snm, snv = _small_reduce_adamw(_pack_small(g_small, loss_row[0, 0]),
                                           _pack_small({n: wts[n] for n in SMALL}, zero),
                                           _pack_small({n: mom[n] for n in SMALL}, zero),
                                           _pack_small({n: var[n] for n in SMALL}, zero))
    (sg, loss), (sd, _), (snm, _), (snv, _) = (_unpack_small(a) for a in (sg, sd, snm, snv))

    halves = [comm.reduced("in"), comm.reduced("rest")]
    g_in, g_rest = [lax.dynamic_update_slice(got, mine[None], (core, 0, 0))
                    for got, mine in zip(_share_halves(halves), halves)]
    g_in, g_rest = g_in.reshape(D_MODEL, IN_SHARD), g_rest.reshape(REST_ROWS, PACK_W)

    def adam_f(wv, gv, mv, vv):
        return _adamw(wv, gv, mv, vv)

    d_in, nm_in, nv_in = _rowwise("adamw_in", adam_f, D_MODEL, IN_TILE,
                                  [(wts["w_in"][0], IN_SHARD, 0), (g_in, IN_SHARD, 0), (mom["w_in"][0], IN_SHARD, 0),
                                   (var["w_in"][0], IN_SHARD, 0)], [], [(IN_SHARD, F32)] * 3)
    d_rest, nm_rest, nv_rest = _rowwise("adamw_rest", adam_f, REST_ROWS, REST_TILE,
                                        [(rest_shard, PACK_W, 0), (g_rest, PACK_W, 0),
                                         (_pack_rest({n: mom[n][0] for n in REST}), PACK_W, 0),
                                         (_pack_rest({n: var[n][0] for n in REST}), PACK_W, 0)], [],
                                        [(PACK_W, F32)] * 3)
    bg, bd, bnm, bnv = (dict(_unpack_rest(rest), w_in=a.reshape(1, D_MODEL, IN_SHARD))
                        for rest, a in ((g_rest, g_in), (d_rest, d_in), (nm_rest, nm_in), (nv_rest, nv_in)))

    outs = [loss, grad_x.reshape(x.shape)]
    for small, big in ((sg, bg), (sd, bd), (snm, bnm), (snv, bnv)):
        outs += [big[n] if n in BIG else small[n] for n in WEIGHTS]
    return tuple(outs)
```

```python
import functools
from typing import Callable, NamedTuple

import jax
import jax.numpy as jnp
from jax import lax
from jax.experimental import pallas as pl
from jax.experimental.pallas import tpu as pltpu

F32 = jnp.float32
MXU_DTYPE = jnp.bfloat16
WIRE_DTYPE = jnp.bfloat16
NORM_EPS = 1e-6
D_MODEL = 1024
HG_HEADS = 8
HG_DIM = 128
HG_CHUNK = 64
S5_WIDTH = 512
S5_GROUPS = 32
S5_GROUP = 16
S5_STATE = 64
S5_LANES = S5_GROUPS * S5_STATE
IN_COLS = 7168
SUBLANES = 8
VMEM_LIMIT = 56 * 1024 * 1024
HIGHEST = lax.Precision.HIGHEST
MESH = pl.DeviceIdType.MESH

ADAM_LR, ADAM_B1, ADAM_B2, ADAM_EPS, ADAM_WD, ADAM_STEP = 0.001, 0.9, 0.999, 1e-08, 0.01, 10

BIG = ("w_in", "w_o_hg", "w_glu", "w_o_s5", "w_out", "w_ple", "w_ple_gate")
BIG_SHAPE = {"w_in": (1024, 7168), "w_o_hg": (1024, 1024), "w_glu": (512, 1024), "w_o_s5": (512, 1024),
             "w_out": (1024, 1024), "w_ple": (256, 1024), "w_ple_gate": (1024, 1024)}
BIG_COL_SHARDED = ("w_in", "w_glu", "w_o_s5", "w_ple")
SMALL = ("norm_g", "hg_lb", "hg_norm_g", "s5_a_re", "s5_a_im", "s5_log_dt", "s5_b_re", "s5_b_im", "s5_c_re",
         "s5_c_im", "s5_d", "b_glu", "ple_norm_g", "final_norm_g")
SMALL_SHAPE = {"norm_g": (1, 1024), "hg_lb": (2, 1024), "hg_norm_g": (1, 1024), "s5_a_re": (1, 32, 64),
               "s5_a_im": (1, 32, 64), "s5_log_dt": (1, 32), "s5_b_re": (1, 32, 64, 16), "s5_b_im": (1, 32, 64, 16),
               "s5_c_re": (1, 32, 16, 64), "s5_c_im": (1, 32, 16, 64), "s5_d": (1, 32, 16), "b_glu": (1, 1024),
               "ple_norm_g": (1, 1024), "final_norm_g": (1024,)}
WEIGHTS = ("norm_g", "w_in", "hg_lb", "hg_norm_g", "w_o_hg", "s5_a_re", "s5_a_im", "s5_log_dt", "s5_b_re", "s5_b_im",
           "s5_c_re", "s5_c_im", "s5_d", "w_glu", "b_glu", "w_o_s5", "w_out", "ple_norm_g", "w_ple", "w_ple_gate",
           "final_norm_g")
N_CHIPS = 4
N_DEV = 8
PACK_W = 1024
SMALL_ROWS = 144


def _params(*sem):
    return pltpu.CompilerParams(dimension_semantics=sem, vmem_limit_bytes=VMEM_LIMIT)


def _sig(x):
    return 1.0 / (1.0 + jnp.exp(-x))


def _dsilu(z, s):
    return s * (1.0 + z * (1.0 - s))


def _mx(x):
    return x.astype(MXU_DTYPE)


def _dot(a, b, dims=(((1,), (0,)), ((), ()))):
    return lax.dot_general(_mx(a), _mx(b), dims, preferred_element_type=F32)


_NT = (((1,), (1,)), ((), ()))
_TN = (((0,), (0,)), ((), ()))


def _dot32(a, b):
    return jnp.dot(a, b, precision=HIGHEST, preferred_element_type=F32)


def _rms_bwd(dy, x, g):
    r = lax.rsqrt(jnp.mean(x * x, axis=-1, keepdims=True) + NORM_EPS)
    t = dy * g
    dx = r * t - x * (r * r * r) * jnp.mean(t * x, axis=-1, keepdims=True)
    return dx, jnp.sum(dy * x * r, axis=0, keepdims=True)


def _rowwise(name, fn, n_rows_total, tm, rows, consts, outs, accs=(), alias=None):
    n_r, n_c, n_o, n_a = len(rows), len(consts), len(outs), len(accs)

    def body(*refs):
        row_refs = refs[:n_r]
        const_refs = refs[n_r:n_r + n_c]
        pos = n_r + n_c + (1 if alias is not None else 0)
        out_refs = refs[pos:pos + n_o]
        acc_refs = refs[pos + n_o:pos + n_o + n_a]
        res = fn(*[r[...] for r in row_refs], *[r[...] for r in const_refs])
        for r, v in zip(out_refs, res[:n_o]):
            r[...] = v.astype(r.dtype)
        if n_a:
            @pl.when(pl.program_id(0) == 0)
            def _():
                for r in acc_refs:
                    r[...] = jnp.zeros_like(r)
            for r, v in zip(acc_refs, res[n_o:]):
                r[...] += v

    in_specs = [pl.BlockSpec((tm, w), functools.partial(lambda i, cb: (i, cb), cb=cb)) for (_, w, cb) in rows]
    in_specs += [pl.BlockSpec(c.shape, lambda i: (0, 0)) for c in consts]
    args = [a for (a, _, _) in rows] + list(consts)
    out_shape, out_specs = [], []
    for o in outs:
        w, dt = o[0], o[1]
        cb, total = (o[2], o[3]) if len(o) == 4 else (0, w)
        out_shape.append(jax.ShapeDtypeStruct((n_rows_total, total), dt))
        out_specs.append(pl.BlockSpec((tm, w), functools.partial(lambda i, cb: (i, cb), cb=cb)))
    io_alias = {}
    if alias is not None:
        in_specs.append(pl.BlockSpec(memory_space=pl.ANY))
        args.append(alias[0])
        io_alias = {len(args) - 1: alias[1]}
    for (r, w) in accs:
        out_shape.append(jax.ShapeDtypeStruct((r, w), F32))
        out_specs.append(pl.BlockSpec((r, w), lambda i: (0, 0)))
    res = pl.pallas_call(body, name=name, grid=(n_rows_total // tm,), in_specs=in_specs, out_specs=out_specs,
                         out_shape=out_shape, input_output_aliases=io_alias,
                         compiler_params=_params("arbitrary"))(*args)
    return res


class _Riding(NamedTuple):
    ins: tuple
    outs: tuple
    n_sems: int
    start: Callable
    wait: Callable


_HBM = pl.BlockSpec(memory_space=pl.ANY)


def _ride(riding, refs, n_in, n_out, n_scratch, first, last):
    if riding is None:
        return refs[:n_in], refs[n_in:n_in + n_out], refs[n_in + n_out:]
    r_in, r_out = len(riding.ins), len(riding.outs)
    ins, rins = refs[:n_in], refs[n_in:n_in + r_in]
    pos = n_in + r_in
    outs, routs = refs[pos:pos + n_out], refs[pos + n_out:pos + n_out + r_out]
    pos += n_out + r_out
    scratch, (send_sems, recv_sems) = refs[pos:pos + n_scratch], refs[pos + n_scratch:]

    @pl.when(first)
    def _():
        riding.start(rins, routs, send_sems, recv_sems)

    @pl.when(last)
    def _():
        riding.wait(rins, routs, send_sems, recv_sems)

    return ins, outs, scratch


def _riding_call(riding, body, name, grid, in_specs, args, out_specs, out_shape, scratch, io_alias=None):
    if riding is not None:
        in_specs = list(in_specs) + [_HBM] * len(riding.ins)
        args = list(args) + list(riding.ins)
        out_specs = list(out_specs) + [_HBM] * len(riding.outs)
        out_shape = list(out_shape) + list(riding.outs)
        scratch = list(scratch) + [pltpu.SemaphoreType.DMA((riding.n_sems,))] * 2
    return pl.pallas_call(body, name=name, grid=grid, in_specs=in_specs, out_specs=out_specs, out_shape=out_shape,
                          scratch_shapes=scratch, input_output_aliases=io_alias or {},
                          compiler_params=_params(*(["arbitrary"] * len(grid))))(*args)


def _mm_nn(name, a, b, tm, tn, riding=None, prologue=None, consts=()):
    m, k = a.shape
    n = b.shape[1] if b.ndim == 2 else b.shape[0] * b.shape[2]
    grid = (n // tn, m // tm)
    n_out, scratch = (1, []) if prologue is None else (2, [pltpu.VMEM((m, k), MXU_DTYPE)])

    def body(*refs):
        j, i = pl.program_id(0), pl.program_id(1)
        ins, outs, kept = _ride(riding, refs, 2 + len(consts), n_out, len(scratch), (j == 0) & (i == 0),
                                (j == grid[0] - 1) & (i == grid[1] - 1))
        if prologue is None:
            left = ins[0][...]
        else:
            rows = pl.ds(pl.multiple_of(i * tm, tm), tm)

            @pl.when(j == 0)
            def _():
                tile = _mx(prologue(ins[0][...], *[c[...] for c in ins[2:]]))
                kept[0][rows, :] = tile
                outs[1][...] = tile

            left = kept[0][rows, :]
        outs[0][...] = _dot(left, ins[1][...])

    once = (lambda j, i: (i, 0)) if prologue is None else (lambda j, i: (jnp.where(j == 0, i, grid[1] - 1), 0))
    b_spec = (pl.BlockSpec((k, tn), lambda j, i: (0, j)) if b.ndim == 2
              else pl.BlockSpec((None, k, tn), lambda j, i: (j, 0, 0)))
    in_specs = [pl.BlockSpec((tm, k), once), b_spec]
    in_specs += [pl.BlockSpec(c.shape, lambda j, i: (0, 0)) for c in consts]
    out_specs = [pl.BlockSpec((tm, tn), lambda j, i: (i, j))]
    out_shape = [jax.ShapeDtypeStruct((m, n), F32)]
    if prologue is not None:
        out_specs.append(pl.BlockSpec((tm, k), once))
        out_shape.append(jax.ShapeDtypeStruct((m, k), MXU_DTYPE))
    res = _riding_call(riding, body, name, grid, in_specs, [a, b] + list(consts), out_specs, out_shape, scratch)
    return res[0] if riding is None and prologue is None else res


def _mm_nt_then(name, a, b, tm, tn, fn, rows, consts, outs, accs=(), alias=None, riding=None):
    m, n = a.shape
    k = b.shape[-2]
    steps = n // tn
    n_r, n_c, n_o, n_a = len(rows), len(consts), len(outs), len(accs)

    def body(*refs):
        a_ref, b_ref = refs[:2]
        row_refs = refs[2:2 + n_r]
        const_refs = refs[2 + n_r:2 + n_r + n_c]
        i, s = pl.program_id(0), pl.program_id(1)
        n_in = 2 + n_r + n_c + (1 if alias is not None else 0)
        _, outs_, (mm_ref,) = _ride(riding, refs, n_in, n_o + n_a, 1, (i == 0) & (s == 0),
                                    (i == m // tm - 1) & (s == steps - 1))
        out_refs, acc_refs = outs_[:n_o], outs_[n_o:]
        part = _dot(a_ref[...], b_ref[...] if b.ndim == 2 else b_ref[s], _NT)
        if steps > 1:
            @pl.when(s == 0)
            def _():
                mm_ref[...] = jnp.zeros_like(mm_ref)
            mm_ref[...] += part

        @pl.when(s == steps - 1)
        def _():
            res = fn(mm_ref[...] if steps > 1 else part, *[r[...] for r in row_refs], *[r[...] for r in const_refs])
            for r, v in zip(out_refs, res[:n_o]):
                r[...] = v.astype(r.dtype)
            if n_a:
                @pl.when(i == 0)
                def _():
                    for r in acc_refs:
                        r[...] = jnp.zeros_like(r)
                for r, v in zip(acc_refs, res[n_o:]):
                    r[...] += v

    b_spec = (pl.BlockSpec((k, tn), lambda i, s: (0, s)) if b.ndim == 2
              else pl.BlockSpec(memory_space=pltpu.VMEM))
    in_specs = [pl.BlockSpec((tm, tn), lambda i, s: (i, s)), b_spec]
    in_specs += [pl.BlockSpec((tm, w), functools.partial(lambda i, s, cb: (i, cb), cb=cb)) for (_, w, cb) in rows]
    in_specs += [pl.BlockSpec(c.shape, lambda i, s: (0, 0)) for c in consts]
    args = [a, b] + [r[0] for r in rows] + list(consts)
    out_shape, out_specs = [], []
    for o in outs:
        w, dt = o[0], o[1]
        cb, total = (o[2], o[3]) if len(o) == 4 else (0, w)
        out_shape.append(jax.ShapeDtypeStruct((m, total), dt))
        out_specs.append(pl.BlockSpec((tm, w), functools.partial(lambda i, s, cb: (i, cb), cb=cb)))
    io_alias = {}
    if alias is not None:
        in_specs.append(pl.BlockSpec(memory_space=pl.ANY))
        args.append(alias[0])
        io_alias = {len(args) - 1: alias[1]}
    for (r, w) in accs:
        out_shape.append(jax.ShapeDtypeStruct((r, w), F32))
        out_specs.append(pl.BlockSpec((r, w), lambda i, s: (0, 0)))
    return _riding_call(riding, body, name, (m // tm, steps), in_specs, args, out_specs, out_shape,
                        [pltpu.VMEM((tm, k), F32)], io_alias)


def _mm_tn(name, a, b, tk, tn, col_shards=False, riding=None):
    t, k = a.shape
    n = b.shape[1]
    steps = t // tk

    def body(*refs):
        j, s = pl.program_id(0), pl.program_id(1)
        (a_ref, b_ref), (o_ref,), (acc_ref,) = _ride(riding, refs, 2, 1, 1, (j == 0) & (s == 0),
                                                     (j == n // tn - 1) & (s == steps - 1))

        @pl.when(s == 0)
        def _():
            acc_ref[...] = jnp.zeros_like(acc_ref)

        acc_ref[...] += _dot(a_ref[...], b_ref[...], _TN)

        @pl.when(s == steps - 1)
        def _():
            o_ref[...] = acc_ref[...]

    if col_shards:
        out_spec = pl.BlockSpec((None, k, tn), lambda j, s: (j, 0, 0))
        out_shape = jax.ShapeDtypeStruct((n // tn, k, tn), F32)
    else:
        out_spec = pl.BlockSpec((k, tn), lambda j, s: (0, j))
        out_shape = jax.ShapeDtypeStruct((k, n), F32)
    res = _riding_call(riding, body, name, (n // tn, steps),
                       [pl.BlockSpec((tk, k), lambda j, s: (s, 0)), pl.BlockSpec((tk, tn), lambda j, s: (s, j))],
                       [a, b], [out_spec], [out_shape], [pltpu.VMEM((k, tn), F32)])
    return res[0] if riding is None else res


def _dot01(m01, x):
    m = m01.astype(MXU_DTYPE)
    hi = x.astype(MXU_DTYPE)
    r1 = x - hi.astype(F32)
    mid = r1.astype(MXU_DTYPE)
    lo = (r1 - mid.astype(F32)).astype(MXU_DTYPE)
    dot = lambda v: jnp.dot(m, v, preferred_element_type=F32)
    return dot(hi) + dot(mid) + dot(lo)


def _chunk_rows(x, offset, nck):
    return jnp.concatenate([jnp.broadcast_to(x[c * HG_CHUNK + offset:c * HG_CHUNK + offset + 1, :],
                                             (HG_CHUNK, x.shape[1])) for c in range(nck)], axis=0)


def _hg_block_terms(q, f, lb, tb):
    nck = tb // HG_CHUNK
    sig = _sig(f)
    fv = lb + (1.0 - lb) * sig
    kk = (1.0 - lb) * (1.0 - sig)
    row = lax.broadcasted_iota(jnp.int32, (tb, tb), 0)
    col = lax.broadcasted_iota(jnp.int32, (tb, tb), 1)
    same = jnp.right_shift(row, 6) == jnp.right_shift(col, 6)
    causal, anti = same & (row >= col), same & (row <= col)
    b = _dot01(causal, jnp.log(fv))
    b_mid, b_last = _chunk_rows(b, HG_CHUNK // 2 - 1, nck), _chunk_rows(b, HG_CHUNK - 1, nck)
    e_mid, e_mid_inv = jnp.exp(b - b_mid), jnp.exp(b_mid - b)
    e_b, e_last = jnp.exp(b), jnp.exp(b_last - b)
    dcs = [jnp.exp(b[c * HG_CHUNK + HG_CHUNK - 1:(c + 1) * HG_CHUNK, :]) for c in range(nck)]
    return sig, fv, kk, causal, anti, e_mid, e_mid_inv, e_b, e_last, dcs


def _hgrn2_fwd(proj, hg_lb, hg_norm_g, t_len, tb, riding=None):
    nck = tb // HG_CHUNK
    nb = t_len // tb

    def body(*refs):
        step = pl.program_id(0)
        ((p_ref, lb_ref, gn_ref), (o_ref, act_ref, sp_ref),
         (st_ref, a_s, bm_s, qd_s, kd_s, v_s, sc_s, inc_s)) = _ride(riding, refs, 3, 3, 8, step == 0, step == nb - 1)

        @pl.when(pl.program_id(0) == 0)
        def _():
            st_ref[...] = jnp.zeros_like(st_ref)

        lb = _sig(lb_ref[0:1, :] - lb_ref[1:2, :])
        q = p_ref[:, pl.ds(0, 1024)]
        _, _, kk, causal, _, e_mid, e_mid_inv, e_b, e_last, dcs = _hg_block_terms(q, p_ref[:, pl.ds(1024, 1024)],
                                                                                   lb, tb)
        a_s[...] = _mx(q * e_mid)
        bm_s[...] = _mx(kk * e_mid_inv)
        qd_s[...] = _mx(q * e_b)
        kd_s[...] = _mx(kk * e_last)
        v_s[...] = _mx(p_ref[:, pl.ds(2048, 1024)])
        heads = [pl.ds(h * HG_DIM, HG_DIM) for h in range(HG_HEADS)]
        chunks = [pl.ds(c * HG_CHUNK, HG_CHUNK) for c in range(nck)]
        for h, hs in enumerate(heads):
            sc_s[h] = _mx(jnp.where(causal, _dot(a_s[:, hs], bm_s[:, hs], _NT), 0.0))
        for h, hs in enumerate(heads):
            o_ref[:, hs] = _dot(sc_s[h], v_s[:, hs])
        for h, hs in enumerate(heads):
            for c, r in enumerate(chunks):
                inc_s[h, c] = _dot(v_s[r, hs], kd_s[r, hs], _TN)
        for c in range(nck):
            for h in range(HG_HEADS):
                st = st_ref[h]
                sp_ref[h, c] = st
                st_ref[h] = dcs[c][:, h * HG_DIM:(h + 1) * HG_DIM] * st + inc_s[h, c]
        for c, r in enumerate(chunks):
            for h, hs in enumerate(heads):
                o_ref[r, hs] += _dot(qd_s[r, hs], sp_ref[h, c], _NT)
        for h, hs in enumerate(heads):
            o = o_ref[:, hs]
            rr = lax.rsqrt(jnp.mean(o * o, axis=-1, keepdims=True) + NORM_EPS)
            g = p_ref[:, pl.ds(3072 + h * HG_DIM, HG_DIM)]
            act_ref[:, hs] = (o * rr * gn_ref[:, hs] * (g * _sig(g))).astype(act_ref.dtype)

    return _riding_call(
        riding, body, "hgrn2_fwd", (nb,),
        [pl.BlockSpec((tb, 4096), lambda i: (i, 0)), pl.BlockSpec((2, 1024), lambda i: (0, 0)),
         pl.BlockSpec((1, 1024), lambda i: (0, 0))],
        [proj, hg_lb, hg_norm_g],
        [pl.BlockSpec((tb, 1024), lambda i: (i, 0)), pl.BlockSpec((tb, 1024), lambda i: (i, 0)),
         pl.BlockSpec((HG_HEADS, nck, HG_DIM, HG_DIM), lambda i: (0, i, 0, 0))],
        [jax.ShapeDtypeStruct((t_len, 1024), F32), jax.ShapeDtypeStruct((t_len, 1024), MXU_DTYPE),
         jax.ShapeDtypeStruct((HG_HEADS, t_len // HG_CHUNK, HG_DIM, HG_DIM), F32)],
        [pltpu.VMEM((HG_HEADS, HG_DIM, HG_DIM), F32)] + [pltpu.VMEM((tb, 1024), MXU_DTYPE)] * 5
        + [pltpu.VMEM((HG_HEADS, tb, tb), MXU_DTYPE), pltpu.VMEM((HG_HEADS, nck, HG_DIM, HG_DIM), F32)])


def _hgrn2_bwd(proj, d_o, s_prev, hg_lb, dproj, t_len, tb, riding=None):
    nck = tb // HG_CHUNK
    nb = t_len // tb

    def body(*refs):
        step = pl.program_id(0)
        ((p_ref, do_ref, sp_ref, lb_ref, _), (dp_ref, dlb_ref),
         (ds_ref, acc_ref, a_s, bm_s, qd_s, kd_s, v_s, do_s, da_s, dbm_s, dqd_s, dkd_s, dv_s, ex_s, sc_s, dsc_s,
          up_s)) = _ride(riding, refs, 5, 2, 17, step == 0, step == nb - 1)

        @pl.when(pl.program_id(0) == 0)
        def _():
            ds_ref[...] = jnp.zeros_like(ds_ref)
            acc_ref[...] = jnp.zeros_like(acc_ref)

        lb = _sig(lb_ref[0:1, :] - lb_ref[1:2, :])
        q = p_ref[:, pl.ds(0, 1024)]
        sig, fv, kk, causal, anti, e_mid, e_mid_inv, e_b, e_last, dcs = _hg_block_terms(
            q, p_ref[:, pl.ds(1024, 1024)], lb, tb)
        a, bm, qd, kd = q * e_mid, kk * e_mid_inv, q * e_b, kk * e_last
        a_s[...] = _mx(a)
        bm_s[...] = _mx(bm)
        qd_s[...] = _mx(qd)
        kd_s[...] = _mx(kd)
        v_s[...] = _mx(p_ref[:, pl.ds(2048, 1024)])
        do_s[...] = _mx(do_ref[...])
        heads = [pl.ds(h * HG_DIM, HG_DIM) for h in range(HG_HEADS)]
        chunks = [pl.ds(c * HG_CHUNK, HG_CHUNK) for c in range(nck)]
        for h, hs in enumerate(heads):
            sc_s[h] = _mx(jnp.where(causal, _dot(a_s[:, hs], bm_s[:, hs], _NT), 0.0))
            dsc_s[h] = _mx(jnp.where(causal, _dot(do_s[:, hs], v_s[:, hs], _NT), 0.0))
        for h, hs in enumerate(heads):
            dv_s[:, hs] = _dot(sc_s[h], do_s[:, hs], _TN)
            da_s[:, hs] = _dot(dsc_s[h], bm_s[:, hs])
            dbm_s[:, hs] = _dot(dsc_s[h], a_s[:, hs], _TN)
        for h, hs in enumerate(heads):
            for c, r in enumerate(chunks):
                up_s[h, c] = _dot(do_s[r, hs], qd_s[r, hs], _TN)
                dqd_s[r, hs] = _dot(do_s[r, hs], sp_ref[h, c])
        for c in reversed(range(nck)):
            r = chunks[c]
            for h, hs in enumerate(heads):
                dst = ds_ref[h]
                dc = dcs[c][:, h * HG_DIM:(h + 1) * HG_DIM]
                dv_s[r, hs] += _dot(kd_s[r, hs], dst, _NT)
                dkd_s[r, hs] = _dot(v_s[r, hs], dst)
                ex_s[c:c + 1, hs] = jnp.sum(dst * sp_ref[h, c], axis=0, keepdims=True) * dc
                ds_ref[h] = up_s[h, c] + dc * dst
        da, dbm, dqd, dkd = da_s[...], dbm_s[...], dqd_s[...], dkd_s[...]
        dq = da * e_mid + dqd * e_b
        dk = dbm * e_mid_inv + dkd * e_last
        db = da * a - dbm * bm + dqd * qd - dkd * kd
        dkk = dkd * kd
        extra = jnp.concatenate(
            [jnp.broadcast_to(jnp.sum(dkk[c * HG_CHUNK:(c + 1) * HG_CHUNK], axis=0, keepdims=True)
                              + ex_s[c:c + 1, :], (HG_CHUNK, 1024)) for c in range(nck)], axis=0)
        dlogf = _dot01(anti, db) + extra
        dfv_k = dlogf / fv - dk
        dp_ref[:, pl.ds(0, 1024)] = dq.astype(dp_ref.dtype)
        dp_ref[:, pl.ds(1024, 1024)] = (dfv_k * (1.0 - lb) * sig * (1.0 - sig)).astype(dp_ref.dtype)
        dp_ref[:, pl.ds(2048, 1024)] = dv_s[...].astype(dp_ref.dtype)
        acc_ref[...] += jnp.sum(dfv_k * (1.0 - sig), axis=0, keepdims=True)

        @pl.when(pl.program_id(0) == nb - 1)
        def _():
            g0 = acc_ref[...] * lb * (1.0 - lb)
            dlb_ref[0:1, :] = g0
            dlb_ref[1:2, :] = -g0

    return _riding_call(
        riding, body, "hgrn2_bwd", (nb,),
        [pl.BlockSpec((tb, 3072), lambda i: (nb - 1 - i, 0)),
         pl.BlockSpec((tb, 1024), lambda i: (nb - 1 - i, 0)),
         pl.BlockSpec((HG_HEADS, nck, HG_DIM, HG_DIM), lambda i: (0, nb - 1 - i, 0, 0)),
         pl.BlockSpec((2, 1024), lambda i: (0, 0)),
         pl.BlockSpec(memory_space=pl.ANY)],
        [proj, d_o, s_prev, hg_lb, dproj],
        [pl.BlockSpec((tb, 3072), lambda i: (nb - 1 - i, 0)), pl.BlockSpec((2, 1024), lambda i: (0, 0))],
        [jax.ShapeDtypeStruct((t_len, IN_COLS), dproj.dtype), jax.ShapeDtypeStruct((2, 1024), F32)],
        [pltpu.VMEM((HG_HEADS, HG_DIM, HG_DIM), F32), pltpu.VMEM((1, 1024), F32)]
        + [pltpu.VMEM((tb, 1024), MXU_DTYPE)] * 6 + [pltpu.VMEM((tb, 1024), F32)] * 5
        + [pltpu.VMEM((SUBLANES, 1024), F32)] + [pltpu.VMEM((HG_HEADS, tb, tb), MXU_DTYPE)] * 2
        + [pltpu.VMEM((HG_HEADS, nck, HG_DIM, HG_DIM), F32)], {4: 0})


def _s5_prep_bwd(a_re, a_im, log_dt, b_re_t, b_im_t, dlam, dbbr, dbbi):
    def body(ar_ref, ai_ref, ldt_ref, br_ref, bi_ref, dlam_ref, dbbr_ref, dbbi_ref,
             dar_ref, dai_ref, dldt_ref, dbr_ref, dbi_ref):
        ar, ai = ar_ref[...], ai_ref[...]
        dt = jnp.exp(ldt_ref[...])
        mag = jnp.exp(ar * dt)
        cs, sn = jnp.cos(ai * dt), jnp.sin(ai * dt)
        lr, li = mag * cs, mag * sn
        den = ar * ar + ai * ai
        nr = lr - 1.0
        sr = (nr * ar + li * ai) / den
        si = (li * ar - nr * ai) / den
        br, bi = br_ref[...], bi_ref[...]
        gbr, gbi = dbbr_ref[...], dbbi_ref[...]
        dbr_ref[...] = sr * gbr + si * gbi
        dbi_ref[...] = sr * gbi - si * gbr
        dsr = jnp.sum(gbr * br + gbi * bi, axis=0, keepdims=True)
        dsi = jnp.sum(gbi * br - gbr * bi, axis=0, keepdims=True)
        dnr = (dsr * ar - dsi * ai) / den
        dli = dlam_ref[1:2, :] + (dsr * ai + dsi * ar) / den
        dlr = dlam_ref[0:1, :] + dnr
        dden = -(dsr * sr + dsi * si) / den
        dar = (dsr * nr + dsi * li) / den + dden * 2.0 * ar
        dai = (dsr * li - dsi * nr) / den + dden * 2.0 * ai
        dmag = dlr * cs + dli * sn
        dth = mag * (dli * cs - dlr * sn)
        dar_ref[...] = dar + dmag * mag * dt
        dai_ref[...] = dai + dth * dt
        ddt = (dmag * mag * ar + dth * ai) * dt
        lane = lax.broadcasted_iota(jnp.int32, (S5_LANES, 128), 0) // S5_STATE
        grp = lax.broadcasted_iota(jnp.int32, (S5_LANES, 128), 1)
        dldt_ref[...] = _dot32(jnp.broadcast_to(ddt, (SUBLANES, S5_LANES)), (lane == grp).astype(F32))

    whole = pl.BlockSpec(memory_space=pltpu.VMEM)
    return pl.pallas_call(
        body, name="s5_prep_bwd", in_specs=[whole] * 8, out_specs=[whole] * 5,
        out_shape=[jax.ShapeDtypeStruct((1, S5_LANES), F32), jax.ShapeDtypeStruct((1, S5_LANES), F32),
                   jax.ShapeDtypeStruct((SUBLANES, 128), F32), jax.ShapeDtypeStruct((S5_GROUP, S5_LANES), F32),
                   jax.ShapeDtypeStruct((S5_GROUP, S5_LANES), F32)])(a_re, a_im, log_dt, b_re_t, b_im_t, dlam, dbbr,
                                                                      dbbi)


def _dgelu(x):
    c, a = 0.7978845608028654, 0.044715
    th = jnp.tanh(c * (x + a * x * x * x))
    return 0.5 * (1.0 + th) + 0.5 * x * (1.0 - th * th) * c * (1.0 + 3.0 * a * x * x)


S5_BLOCKS = 4
S5_BW = S5_WIDTH // S5_BLOCKS
S5_BL = S5_LANES // S5_BLOCKS
S5_LANE_BLOCKS = S5_LANES // 128
S5_SCAN_BLOCKS = 4


def _s5_prep(a_re, a_im, log_dt, b_re_t, b_im_t, seg):
    def body(ar_ref, ai_ref, ldt_ref, br_ref, bi_ref,
             rows_f, pfr_ref, pfi_ref, rows_r, prr_ref, pri_ref, bbr_ref, bbi_ref):
        ar, ai = ar_ref[...], ai_ref[...]
        dt = jnp.exp(ldt_ref[...])
        mag = jnp.exp(ar * dt)
        lr, li = mag * jnp.cos(ai * dt), mag * jnp.sin(ai * dt)
        den = ar * ar + ai * ai
        nr = lr - 1.0
        sr = (nr * ar + li * ai) / den
        si = (li * ar - nr * ai) / den
        wide = (SUBLANES, S5_LANES)
        cr, ci = lr, li
        for i in range(seg):
            pfr_ref[i] = jnp.broadcast_to(cr, wide)
            pfi_ref[i] = jnp.broadcast_to(ci, wide)
            prr_ref[seg - 1 - i] = jnp.broadcast_to(cr, wide)
            pri_ref[seg - 1 - i] = jnp.broadcast_to(-ci, wide)
            if i == seg - 1:
                for rows, sign in ((rows_f, 1.0), (rows_r, -1.0)):
                    rows[0:1, :] = lr
                    rows[1:2, :] = sign * li
                    rows[2:3, :] = cr
                    rows[3:4, :] = sign * ci
            cr, ci = cr * lr - ci * li, cr * li + ci * lr
        br, bi = br_ref[...], bi_ref[...]
        bbr_ref[...] = sr * br - si * bi
        bbi_ref[...] = sr * bi + si * br

    whole = pl.BlockSpec(memory_space=pltpu.VMEM)
    tables = [jax.ShapeDtypeStruct((4, S5_LANES), F32)] + [jax.ShapeDtypeStruct((seg, SUBLANES, S5_LANES), F32)] * 2
    bbar = [jax.ShapeDtypeStruct((S5_GROUP, S5_LANES), F32)] * 2
    res = pl.pallas_call(body, name="s5_prep", in_specs=[whole] * 5, out_specs=[whole] * 8,
                         out_shape=tables + tables + bbar)(a_re, a_im, log_dt, b_re_t, b_im_t)
    return res[0:3], res[3:6], res[6], res[7]


def _lanes(j):
    return pl.ds(j * 128, 128)


def _to_segment_order(v, stage_ref, out_ref, seg):
    nbl = v.shape[1] // 128
    for b in range(nbl):
        stage_ref[b] = v[:, b * 128:(b + 1) * 128]

    def body(t, carry):
        rows = pl.ds(pl.multiple_of(t * SUBLANES, SUBLANES), SUBLANES)
        for b in range(nbl):
            out_ref[rows, _lanes(b)] = stage_ref[b, pl.ds(t, SUBLANES, stride=seg), :]
        return carry

    lax.fori_loop(0, seg, body, 0, unroll=True)


def _from_segment_order(v, stage_ref, out_ref, seg):
    nbl = v.shape[1] // 128
    for b in range(nbl):
        stage_ref[b] = v[:, b * 128:(b + 1) * 128]
    for s in range(SUBLANES):
        def body(k, carry, s=s):
            rows = pl.ds(pl.multiple_of(s * seg + k * SUBLANES, SUBLANES), SUBLANES)
            for b in range(nbl):
                out_ref[rows, _lanes(b)] = stage_ref[b, pl.ds(k * SUBLANES * SUBLANES + s, SUBLANES,
                                                              stride=SUBLANES), :]
            return carry

        lax.fori_loop(0, seg // SUBLANES, body, 0, unroll=True)


def _tile_scan(xr_ref, xi_ref, lam_ref, car_ref, cai_ref, cn_r, cn_i, blocks, seg, reverse):
    shape = (SUBLANES, 128)
    lrs = [jnp.broadcast_to(lam_ref[0:1, _lanes(j)], shape) for j in blocks]
    lis = [jnp.broadcast_to(lam_ref[1:2, _lanes(j)], shape) for j in blocks]

    def step(k, carry):
        t = seg - 1 - k if reverse else k
        rows = pl.ds(pl.multiple_of(t * SUBLANES, SUBLANES), SUBLANES)
        out = []
        for n, j in enumerate(blocks):
            cr, ci = carry[2 * n], carry[2 * n + 1]
            nr = lrs[n] * cr - lis[n] * ci + xr_ref[rows, _lanes(j)]
            ni = lrs[n] * ci + lis[n] * cr + xi_ref[rows, _lanes(j)]
            xr_ref[rows, _lanes(j)] = nr
            xi_ref[rows, _lanes(j)] = ni
            out += [nr, ni]
        return tuple(out)

    zero = jnp.zeros(shape, F32)
    fin = lax.fori_loop(0, seg, step, (zero,) * (2 * len(blocks)), unroll=True)
    for n, j in enumerate(blocks):
        ls = _lanes(j)
        fr, fi = fin[2 * n], fin[2 * n + 1]
        sr, si = lam_ref[2:3, ls], lam_ref[3:4, ls]
        pr, pi = car_ref[:, ls], cai_ref[:, ls]
        for s in (reversed(range(SUBLANES)) if reverse else range(SUBLANES)):
            cn_r[s:s + 1, ls] = pr
            cn_i[s:s + 1, ls] = pi
            pr, pi = fr[s:s + 1, :] + sr * pr - si * pi, fi[s:s + 1, :] + sr * pi + si * pr
        car_ref[:, ls] = pr
        cai_ref[:, ls] = pi


def _s5_fwd(proj, lam_rows, p3_re, p3_im, bbr4, bbi4, crt4, cit4, d_row, t_len, tb):
    seg = tb // SUBLANES

    def body(u_ref, lam_ref, p3r_ref, p3i_ref, bbr_ref, bbi_ref, crt_ref, cit_ref, d_ref,
             hr_ref, hi_ref, ypre_ref, ys_ref, car_ref, cai_ref, cn_r, cn_i, stage_ref, us_ref, yseg_ref):
        @pl.when(pl.program_id(0) == 0)
        def _():
            car_ref[...] = jnp.zeros_like(car_ref)
            cai_ref[...] = jnp.zeros_like(cai_ref)

        _to_segment_order(u_ref[...], stage_ref, us_ref, seg)
        u = us_ref[...]
        for i in range(S5_BLOCKS):
            ui = u[:, i * S5_BW:(i + 1) * S5_BW]
            hr_ref[:, pl.ds(i * S5_BL, S5_BL)] = _dot(ui, bbr_ref[i])
            hi_ref[:, pl.ds(i * S5_BL, S5_BL)] = _dot(ui, bbi_ref[i])
        for lc in range(S5_LANE_BLOCKS // S5_SCAN_BLOCKS):
            blocks = range(lc * S5_SCAN_BLOCKS, (lc + 1) * S5_SCAN_BLOCKS)
            _tile_scan(hr_ref, hi_ref, lam_ref, car_ref, cai_ref, cn_r, cn_i, blocks, seg, False)
            crs = [cn_r[:, _lanes(j)] for j in blocks]
            cis = [cn_i[:, _lanes(j)] for j in blocks]

            def fix(t, carry, blocks=blocks, crs=crs, cis=cis):
                rows = pl.ds(pl.multiple_of(t * SUBLANES, SUBLANES), SUBLANES)
                for n, j in enumerate(blocks):
                    pr, pi = p3r_ref[t, :, _lanes(j)], p3i_ref[t, :, _lanes(j)]
                    hr_ref[rows, _lanes(j)] += pr * crs[n] - pi * cis[n]
                    hi_ref[rows, _lanes(j)] += pr * cis[n] + pi * crs[n]
                return carry

            lax.fori_loop(0, seg, fix, 0, unroll=True)
        for i in range(S5_BLOCKS):
            ws = pl.ds(i * S5_BW, S5_BW)
            bl = pl.ds(i * S5_BL, S5_BL)
            yseg_ref[:, ws] = (_dot(hr_ref[:, bl], crt_ref[i]) - _dot(hi_ref[:, bl], cit_ref[i])
                               + d_ref[:, ws] * u[:, i * S5_BW:(i + 1) * S5_BW])
        _from_segment_order(yseg_ref[...], stage_ref, ypre_ref, seg)
        ys_ref[...] = jax.nn.gelu(ypre_ref[...], approximate=True).astype(ys_ref.dtype)

    whole = pl.BlockSpec(memory_space=pltpu.VMEM)
    return pl.pallas_call(
        body, name="s5_fwd", grid=(t_len // tb,),
        in_specs=[pl.BlockSpec((tb, S5_WIDTH), lambda i: (i, 4096 // S5_WIDTH))] + [whole] * 8,
        out_specs=[pl.BlockSpec((tb, S5_LANES), lambda i: (i, 0)), pl.BlockSpec((tb, S5_LANES), lambda i: (i, 0)),
                   pl.BlockSpec((tb, S5_WIDTH), lambda i: (i, 0)), pl.BlockSpec((tb, S5_WIDTH), lambda i: (i, 0))],
        out_shape=[jax.ShapeDtypeStruct((t_len, S5_LANES), F32), jax.ShapeDtypeStruct((t_len, S5_LANES), F32),
                   jax.ShapeDtypeStruct((t_len, S5_WIDTH), F32), jax.ShapeDtypeStruct((t_len, S5_WIDTH), MXU_DTYPE)],
        scratch_shapes=[pltpu.VMEM((1, S5_LANES), F32), pltpu.VMEM((1, S5_LANES), F32),
                        pltpu.VMEM((SUBLANES, S5_LANES), F32), pltpu.VMEM((SUBLANES, S5_LANES), F32),
                        pltpu.VMEM((S5_WIDTH // 128, tb, 128), F32), pltpu.VMEM((tb, S5_WIDTH), F32),
                        pltpu.VMEM((tb, S5_WIDTH), F32)],
        compiler_params=_params("arbitrary"))(proj, lam_rows, p3_re, p3_im, bbr4, bbi4, crt4, cit4, d_row)


def _s5_bwd(dgelu, y_pre, proj, h_re, h_im, lam_rows, p3_re, p3_im, bbr4, bbi4, cr4, ci4, d_row, dproj, t_len, tb):
    seg = tb // SUBLANES
    nb = t_len // tb

    def body(dg_ref, yp_ref, u_ref, hr_ref, hi_ref, lam_ref, p3r_ref, p3i_ref, bbr_ref, bbi_ref, cr_ref, ci_ref,
             d_ref, _, du_ref, dbbr_ref, dbbi_ref, dcr_ref, dci_ref, dd_ref, dlam_ref,
             gr_ref, gi_ref, car_ref, cai_ref, cn_r, cn_i, stage_ref, us_ref, dys_ref, duseg_ref):
        @pl.when(pl.program_id(0) == 0)
        def _():
            for ref in (car_ref, cai_ref, dbbr_ref, dbbi_ref, dcr_ref, dci_ref, dd_ref, dlam_ref):
                ref[...] = jnp.zeros_like(ref)

        _to_segment_order(u_ref[...], stage_ref, us_ref, seg)
        _to_segment_order(dg_ref[...] * _dgelu(yp_ref[...]), stage_ref, dys_ref, seg)
        u, dy = us_ref[...], dys_ref[...]
        for i in range(S5_BLOCKS):
            dyi = dy[:, i * S5_BW:(i + 1) * S5_BW]
            gr_ref[:, pl.ds(i * S5_BL, S5_BL)] = _dot(dyi, cr_ref[i])
            gi_ref[:, pl.ds(i * S5_BL, S5_BL)] = -_dot(dyi, ci_ref[i])
        for lc in range(S5_LANE_BLOCKS // S5_SCAN_BLOCKS):
            blocks = range(lc * S5_SCAN_BLOCKS, (lc + 1) * S5_SCAN_BLOCKS)
            _tile_scan(gr_ref, gi_ref, lam_ref, car_ref, cai_ref, cn_r, cn_i, blocks, seg, True)
            crs = [cn_r[:, _lanes(j)] for j in blocks]
            cis = [cn_i[:, _lanes(j)] for j in blocks]

            def fix(k, carry, blocks=blocks, crs=crs, cis=cis):
                t = seg - 1 - k
                rows = pl.ds(pl.multiple_of(t * SUBLANES, SUBLANES), SUBLANES)
                out = []
                for n, j in enumerate(blocks):
                    nr, ni, slr, sli = carry[4 * n:4 * n + 4]
                    pr, pi = p3r_ref[t, :, _lanes(j)], p3i_ref[t, :, _lanes(j)]
                    g_r = gr_ref[rows, _lanes(j)] + pr * crs[n] - pi * cis[n]
                    g_i = gi_ref[rows, _lanes(j)] + pr * cis[n] + pi * crs[n]
                    gr_ref[rows, _lanes(j)] = g_r
                    gi_ref[rows, _lanes(j)] = g_i
                    hr, hi = hr_ref[rows, _lanes(j)], hi_ref[rows, _lanes(j)]
                    out += [g_r, g_i, slr + nr * hr + ni * hi, sli + ni * hr - nr * hi]
                return tuple(out)

            zero = jnp.zeros((SUBLANES, 128), F32)
            init = []
            for n in range(len(blocks)):
                init += [crs[n], cis[n], zero, zero]
            fin = lax.fori_loop(0, seg, fix, tuple(init), unroll=True)
            for n, j in enumerate(blocks):
                dlam_ref[0:1, _lanes(j)] += jnp.sum(fin[4 * n + 2], axis=0, keepdims=True)
                dlam_ref[1:2, _lanes(j)] += jnp.sum(fin[4 * n + 3], axis=0, keepdims=True)
        for i in range(S5_BLOCKS):
            ws = pl.ds(i * S5_BW, S5_BW)
            bl = pl.ds(i * S5_BL, S5_BL)
            ui, dyi = u[:, i * S5_BW:(i + 1) * S5_BW], dy[:, i * S5_BW:(i + 1) * S5_BW]
            gr, gi = gr_ref[:, bl], gi_ref[:, bl]
            duseg_ref[:, ws] = _dot(gr, bbr_ref[i], _NT) + _dot(gi, bbi_ref[i], _NT) + d_ref[:, ws] * dyi
            dbbr_ref[i] += _dot(ui, gr, _TN)
            dbbi_ref[i] += _dot(ui, gi, _TN)
            dcr_ref[i] += _dot(hr_ref[:, bl], dyi, _TN)
            dci_ref[i] -= _dot(hi_ref[:, bl], dyi, _TN)
        dd_ref[...] += jnp.sum(dy * u, axis=0, keepdims=True)
        _from_segment_order(duseg_ref[...], stage_ref, duseg_ref, seg)
        du_ref[...] = duseg_ref[...].astype(du_ref.dtype)

    whole = pl.BlockSpec(memory_space=pltpu.VMEM)
    rev = lambda i: (nb - 1 - i, 0)
    const3 = lambda i: (0, 0, 0)
    return pl.pallas_call(
        body, name="s5_bwd", grid=(nb,),
        in_specs=[pl.BlockSpec((tb, S5_WIDTH), rev), pl.BlockSpec((tb, S5_WIDTH), rev),
                  pl.BlockSpec((tb, S5_WIDTH), lambda i: (nb - 1 - i, 4096 // S5_WIDTH)),
                  pl.BlockSpec((tb, S5_LANES), rev), pl.BlockSpec((tb, S5_LANES), rev)] + [whole] * 8
                 + [pl.BlockSpec(memory_space=pl.ANY)],
        out_specs=[pl.BlockSpec((tb, S5_WIDTH), lambda i: (nb - 1 - i, 4096 // S5_WIDTH)),
                   pl.BlockSpec((S5_BLOCKS, S5_BW, S5_BL), const3), pl.BlockSpec((S5_BLOCKS, S5_BW, S5_BL), const3),
                   pl.BlockSpec((S5_BLOCKS, S5_BL, S5_BW), const3), pl.BlockSpec((S5_BLOCKS, S5_BL, S5_BW), const3),
                   pl.BlockSpec((1, S5_WIDTH), lambda i: (0, 0)), pl.BlockSpec((2, S5_LANES), lambda i: (0, 0))],
        out_shape=[jax.ShapeDtypeStruct((t_len, IN_COLS), dproj.dtype),
                   jax.ShapeDtypeStruct((S5_BLOCKS, S5_BW, S5_BL), F32),
                   jax.ShapeDtypeStruct((S5_BLOCKS, S5_BW, S5_BL), F32),
                   jax.ShapeDtypeStruct((S5_BLOCKS, S5_BL, S5_BW), F32),
                   jax.ShapeDtypeStruct((S5_BLOCKS, S5_BL, S5_BW), F32),
                   jax.ShapeDtypeStruct((1, S5_WIDTH), F32), jax.ShapeDtypeStruct((2, S5_LANES), F32)],
        scratch_shapes=[pltpu.VMEM((tb, S5_LANES), F32), pltpu.VMEM((tb, S5_LANES), F32),
                        pltpu.VMEM((1, S5_LANES), F32), pltpu.VMEM((1, S5_LANES), F32),
                        pltpu.VMEM((SUBLANES, S5_LANES), F32), pltpu.VMEM((SUBLANES, S5_LANES), F32),
                        pltpu.VMEM((S5_WIDTH // 128, tb, 128), F32), pltpu.VMEM((tb, S5_WIDTH), F32),
                        pltpu.VMEM((tb, S5_WIDTH), F32), pltpu.VMEM((tb, S5_WIDTH), F32)],
        input_output_aliases={13: 0},
        compiler_params=_params("arbitrary"))(dgelu, y_pre, proj, h_re, h_im, lam_rows, p3_re, p3_im, bbr4, bbi4,
                                              cr4, ci4, d_row, dproj)


def _block_diag(per_group):
    g8 = S5_GROUPS // S5_BLOCKS
    eye = jnp.eye(g8, dtype=bool)[None, :, None, :, None]
    dense = jnp.where(eye, per_group.reshape(S5_BLOCKS, g8, S5_GROUP, 1, S5_STATE), 0.0)
    return dense.reshape(S5_BLOCKS, S5_BW, S5_BL)


def _diag_blocks(dense):
    g8 = S5_GROUPS // S5_BLOCKS
    ar = jnp.arange(g8)
    d5 = dense.reshape(S5_BLOCKS, g8, S5_GROUP, g8, S5_STATE)
    return d5[:, ar, :, ar, :].transpose(1, 0, 2, 3).reshape(S5_GROUPS, S5_GROUP, S5_STATE)


def _hg_gate_bwd(da, o, g, gn):
    dos, dgs, dgns = [], [], []
    for h in range(HG_HEADS):
        sl = slice(h * HG_DIM, (h + 1) * HG_DIM)
        oh, gh, dah, gnh = o[:, sl], g[:, sl], da[:, sl], gn[:, sl]
        rr = lax.rsqrt(jnp.mean(oh * oh, axis=-1, keepdims=True) + NORM_EPS)
        sg = _sig(gh)
        dgs.append(dah * (oh * rr * gnh) * _dsilu(gh, sg))
        don = dah * (gh * sg)
        t = don * gnh
        dos.append(rr * t - oh * (rr * rr * rr) * jnp.mean(t * oh, axis=-1, keepdims=True))
        dgns.append(jnp.sum(don * oh * rr, axis=0, keepdims=True))
    return jnp.concatenate(dos, axis=1), jnp.concatenate(dgs, axis=1), jnp.concatenate(dgns, axis=1)


MIX_BWD_COLS = ((3072, 1024), (4608, 512), (5120, 1024), (6144, 1024))


def _mix_bwd(dgl, h1, dh2, act_hg, ys2, ys_gelu, proj, o_hg, g2, ghn, b_glu, w, t_len, tm):
    nb = t_len // tm

    def body(dgl_ref, h1_ref, dh2_ref, act_ref, ys2_ref, ysg_ref, ghg_ref, z_ref, gh_ref, gs_ref, o_ref, g2_ref, gn_ref,
             bglu_ref, wg_ref, wo_ref, ws5_ref, whg_ref, wglu_ref,
             dh1_ref, dyh_ref, dys_ref, dglu_ref, dgelu_ref, do_ref, dg2_ref, dbglu_ref, dgn_ref, dproj_ref,
             st0, st1, st2, st3, sems):
        i = pl.program_id(0)
        stages = (st0, st1, st2, st3)

        def writes(step):
            rows = pl.ds(pl.multiple_of(step * tm, tm), tm)
            return [pltpu.make_async_copy(st, dproj_ref.at[rows, pl.ds(c0, wd)], sems.at[k])
                    for k, (st, (c0, wd)) in enumerate(zip(stages, MIX_BWD_COLS))]

        @pl.when(i > 0)
        def _():
            for cp in writes(i - 1):
                cp.wait()

        @pl.when(i == 0)
        def _():
            for ref in (dg2_ref, dbglu_ref, dgn_ref):
                ref[...] = jnp.zeros_like(ref)

        dx, dg2 = _rms_bwd(_dot(dgl_ref[...], wg_ref[...], _NT), h1_ref[...], g2_ref[...])
        dh1 = dh2_ref[...] + dx
        dh1_ref[...] = dh1
        dg2_ref[...] += dg2
        dm = _dot(dh1, wo_ref[...], _NT)
        sh, ss = _sig(gh_ref[...]), _sig(gs_ref[...])
        dyh, dys = _mx(dm * sh), _mx(dm * ss)
        dyh_ref[...] = dyh
        dys_ref[...] = dys
        st2[...] = (dm * _dot(act_ref[...], whg_ref[...]) * sh * (1.0 - sh)).astype(st2.dtype)
        st3[...] = (dm * _dot(ys2_ref[...], ws5_ref[...]) * ss * (1.0 - ss)).astype(st3.dtype)
        dys2 = _dot(dys, ws5_ref[...], _NT)
        gl_, z = _dot(ysg_ref[...], wglu_ref[...]) + bglu_ref[...], z_ref[...]
        a, b = gl_[:, :S5_WIDTH], gl_[:, S5_WIDTH:]
        sb, sz = _sig(b), _sig(z)
        silu = z * sz
        dglu = jnp.concatenate([dys2 * sb * silu, dys2 * a * silu * sb * (1.0 - sb)], axis=1)
        st1[...] = (dys2 * a * sb * _dsilu(z, sz)).astype(st1.dtype)
        dbglu_ref[...] += jnp.sum(dglu, axis=0, keepdims=True)
        dglu_ref[...] = _mx(dglu)
        dgelu_ref[...] = _dot(dglu, wglu_ref[...], _NT)
        d_o, dg, dgn = _hg_gate_bwd(_dot(dyh, whg_ref[...], _NT), o_ref[...], ghg_ref[...], gn_ref[...])
        do_ref[...] = d_o.astype(do_ref.dtype)
        st0[...] = dg.astype(st0.dtype)
        dgn_ref[...] += dgn
        for cp in writes(i):
            cp.start()

        @pl.when(i == nb - 1)
        def _():
            for cp in writes(i):
                cp.wait()

    tile = lambda wd, cb=0: pl.BlockSpec((tm, wd), functools.partial(lambda i, cb: (i, cb), cb=cb))
    row = lambda wd: pl.BlockSpec((1, wd), lambda i: (0, 0))
    whole = pl.BlockSpec(memory_space=pltpu.VMEM)
    return pl.pallas_call(
        body, name="mix_bwd", grid=(nb,),
        in_specs=[tile(1024), tile(1024), tile(1024), tile(1024), tile(512), tile(512), tile(1024, 3),
                  tile(512, 4608 // 512), tile(1024, 5), tile(1024, 6), tile(1024), row(1024), row(1024), row(1024)]
                 + [whole] * 5,
        out_specs=[tile(1024), tile(1024), tile(1024), tile(1024), tile(512), tile(1024), row(1024), row(1024),
                   row(1024), _HBM],
        out_shape=[jax.ShapeDtypeStruct((t_len, 1024), F32), jax.ShapeDtypeStruct((t_len, 1024), MXU_DTYPE),
                   jax.ShapeDtypeStruct((t_len, 1024), MXU_DTYPE), jax.ShapeDtypeStruct((t_len, 1024), MXU_DTYPE),
                   jax.ShapeDtypeStruct((t_len, 512), F32), jax.ShapeDtypeStruct((t_len, 1024), MXU_DTYPE),
                   jax.ShapeDtypeStruct((1, 1024), F32), jax.ShapeDtypeStruct((1, 1024), F32),
                   jax.ShapeDtypeStruct((1, 1024), F32), jax.ShapeDtypeStruct((t_len, IN_COLS), MXU_DTYPE)],
        scratch_shapes=[pltpu.VMEM((tm, wd), MXU_DTYPE) for _, wd in MIX_BWD_COLS] + [pltpu.SemaphoreType.DMA((4,))],
        compiler_params=_params("arbitrary"))(dgl, h1, dh2, act_hg, ys2, ys_gelu, proj, proj, proj, proj, o_hg, g2, ghn,
                                              b_glu, w["w_ple_gate"], w["w_out"], w["w_o_s5"], w["w_o_hg"],
                                              w["w_glu"])


def _local_step(x, p, target, w, sm, comm=None):
    t_len = x.shape[0]
    tm = min(256, t_len)
    tmm = min(512, t_len)
    tb_hg = min(256, t_len)
    tb_s5 = min(256, t_len)
    g1, g2, g3, ghn = sm["norm_g"], sm["ple_norm_g"], sm["final_norm_g"].reshape(1, D_MODEL), sm["hg_norm_g"]

    def rms_in(xv, g):
        return xv * lax.rsqrt(jnp.mean(xv * xv, axis=-1, keepdims=True) + NORM_EPS) * g

    in_shard = IN_COLS // N_CHIPS
    if comm is None:
        w_in = w["w_in"]
        proj, u = _mm_nn("mm_in", x, w_in, tmm, in_shard, prologue=rms_in, consts=[g1])
    else:
        proj, u, w_in = comm.input_projection(x, g1, rms_in, tmm)

    lanes = lambda a: a.reshape(1, S5_LANES)
    a_re, a_im = lanes(sm["s5_a_re"]), lanes(sm["s5_a_im"])
    ldt = lanes(jnp.broadcast_to(sm["s5_log_dt"].reshape(S5_GROUPS, 1), (S5_GROUPS, S5_STATE)))
    to_t = lambda b: b.reshape(S5_GROUPS, S5_STATE, S5_GROUP).transpose(2, 0, 1).reshape(S5_GROUP, S5_LANES)
    b_re_t, b_im_t = to_t(sm["s5_b_re"]), to_t(sm["s5_b_im"])
    scan_fwd, scan_rev, bbr_t, bbi_t = _s5_prep(a_re, a_im, ldt, b_re_t, b_im_t, tb_s5 // SUBLANES)
    from_t = lambda b: b.reshape(S5_GROUP, S5_GROUPS, S5_STATE).transpose(1, 0, 2)
    bbr_bd = _block_diag(from_t(bbr_t)).astype(MXU_DTYPE)
    bbi_bd = _block_diag(from_t(bbi_t)).astype(MXU_DTYPE)
    cr_bd = _block_diag(sm["s5_c_re"].reshape(S5_GROUPS, S5_GROUP, S5_STATE)).astype(MXU_DTYPE)
    ci_bd = _block_diag(sm["s5_c_im"].reshape(S5_GROUPS, S5_GROUP, S5_STATE)).astype(MXU_DTYPE)
    d_row = sm["s5_d"].reshape(1, S5_WIDTH)
    if comm is None:
        o_hg, act_hg, s_prev = _hgrn2_fwd(proj, sm["hg_lb"], ghn, t_len, tb_hg)
    else:
        o_hg, act_hg, s_prev, landed = _hgrn2_fwd(proj, sm["hg_lb"], ghn, t_len, tb_hg, riding=comm.gather_rest())
        w = comm.rest_weights(landed)
    h_re, h_im, y_pre, ys_gelu = _s5_fwd(proj, *scan_fwd, bbr_bd, bbi_bd,
                                          cr_bd.transpose(0, 2, 1), ci_bd.transpose(0, 2, 1), d_row, t_len, tb_s5)
    def mix_f(act, ysg, z, gh, gs, xv, w_glu, b_glu, w_o_hg, w_o_s5, w_out):
        yh = _dot(act, w_o_hg)
        gl_ = _dot(ysg, w_glu) + b_glu
        a, b = gl_[:, :S5_WIDTH], gl_[:, S5_WIDTH:]
        ys2_ = (a * _sig(b) * (z * _sig(z))).astype(MXU_DTYPE)
        ys = _dot(ys2_, w_o_s5)
        mg = (_sig(gh) * yh + _sig(gs) * ys).astype(MXU_DTYPE)
        return (ys2_, mg, xv + _dot(mg, w_out))

    ys2, merged, h1 = _rowwise(
        "mix_out", mix_f, t_len, tm,
        [(act_hg, 1024, 0), (ys_gelu, 512, 0), (proj, 512, 4608 // 512), (proj, 1024, 5), (proj, 1024, 6),
         (x, 1024, 0)], [w["w_glu"], sm["b_glu"], w["w_o_hg"], w["w_o_s5"], w["w_out"]],
        [(512, MXU_DTYPE), (1024, MXU_DTYPE), (1024, F32)])

    def head_f(h1v, pv, tgt, g_ple, g, w_ple, w_gate):
        r2 = lax.rsqrt(jnp.mean(h1v * h1v, axis=-1, keepdims=True) + NORM_EPS)
        n2_ = (h1v * r2 * g_ple).astype(MXU_DTYPE)
        glv, pev = _dot(n2_, w_gate), _dot(pv, w_ple)
        gate = _sig(glv)
        h2 = h1v + pev * gate
        r = lax.rsqrt(jnp.mean(h2 * h2, axis=-1, keepdims=True) + NORM_EPS)
        e = h2 * r * g - tgt
        loss = 0.5 * jnp.sum(jnp.mean(e * e, axis=-1, keepdims=True), axis=0, keepdims=True)
        dy = e * (1.0 / D_MODEL)
        dg = jnp.sum(dy * h2 * r, axis=0, keepdims=True)
        t = dy * g
        dh2 = r * t - h2 * (r * r * r) * jnp.mean(t * h2, axis=-1, keepdims=True)
        return (n2_, dh2, dh2 * gate, dh2 * pev * gate * (1.0 - gate), jnp.broadcast_to(loss, (1, 128)), dg)

    n2, dh2, dpe, dgl, loss_row, d_g3 = _rowwise(
        "ple_loss_head", head_f, t_len, tm, [(h1, 1024, 0), (p, 256, 0), (target, 1024, 0)],
        [g2, g3, w["w_ple"], w["w_ple_gate"]],
        [(1024, MXU_DTYPE), (1024, F32), (1024, MXU_DTYPE), (1024, MXU_DTYPE)], accs=[(1, 128), (1, 1024)])

    gb = {}
    gb["w_ple"] = _mm_tn("mm_d_w_ple", p, dpe, tmm, 1024)
    gb["w_ple_gate"] = _mm_tn("mm_d_w_ple_gate", n2, dgl, tmm, 1024)
    dh1, dy_hg, dy_s5, dglu, dgelu, d_o, d_g2, d_bglu, d_ghn, dproj = _mix_bwd(
        dgl, h1, dh2, act_hg, ys2, ys_gelu, proj, o_hg, g2, ghn, sm["b_glu"], w, t_len, tm)
    gb["w_out"] = _mm_tn("mm_d_w_out", merged, dh1, tmm, 1024)
    gb["w_o_s5"] = _mm_tn("mm_d_w_o_s5", ys2, dy_s5, tmm, 1024)
    gb["w_glu"] = _mm_tn("mm_d_w_glu", ys_gelu, dglu, tmm, 1024)
    dproj, d_bbr, d_bbi, d_crt, d_cit, d_d, d_lam = _s5_bwd(dgelu, y_pre, proj, h_re, h_im,
                                                            *scan_rev, bbr_bd, bbi_bd, cr_bd,
                                                            ci_bd, d_row, dproj, t_len, tb_s5)
    to_t3 = lambda b: b.transpose(1, 0, 2).reshape(S5_GROUP, S5_LANES)
    d_are, d_aim, d_ldt, d_br_t, d_bi_t = _s5_prep_bwd(a_re, a_im, ldt, b_re_t, b_im_t, d_lam,
                                                       to_t3(_diag_blocks(d_bbr)), to_t3(_diag_blocks(d_bbi)))
    gb["w_o_hg"] = _mm_tn("mm_d_w_o_hg", act_hg, dy_hg, tmm, 1024)
    if comm is None:
        dproj, d_lb = _hgrn2_bwd(proj, d_o, s_prev, sm["hg_lb"], dproj, t_len, tb_hg)
    else:
        rest_grads = _pack_rest_full(gb)
        dproj, d_lb, rest_theirs = _hgrn2_bwd(proj, d_o, s_prev, sm["hg_lb"], dproj, t_len, tb_hg,
                                               riding=comm.swap(rest_grads))

    def in_b(duv, xv, dh, g):
        dx, dg = _rms_bwd(duv, xv, g)
        return (dh + dx, dg)

    in_args = ("mm_d_u_rms_in_bwd", dproj, w_in, tmm, in_shard, in_b, [(x, 1024, 0), (dh1, 1024, 0)], [g1],
               [(1024, F32)])
    if comm is None:
        gb["w_in"] = _mm_tn("mm_d_w_in", u, dproj, tmm, in_shard, col_shards=True)
        grad_x, d_g1 = _mm_nt_then(*in_args, accs=[(1, 1024)])
    else:
        gb["w_in"], landed = _mm_tn("mm_d_w_in", u, dproj, tmm, in_shard, col_shards=True,
                                    riding=comm.scatter("rest", rest_grads, rest_theirs))
        comm.landed["rest"] = landed
        grad_x, d_g1, landed = _mm_nt_then(*in_args, accs=[(1, 1024)], riding=comm.scatter(
            "in", gb["w_in"].reshape(N_CHIPS, 2, D_MODEL // 2, in_shard)))
        comm.landed["in"] = landed

    back_t = lambda b: b.reshape(S5_GROUP, S5_GROUPS, S5_STATE).transpose(1, 2, 0).reshape(1, S5_GROUPS, S5_STATE,
                                                                                           S5_GROUP)
    gs = {
        "norm_g": d_g1, "hg_lb": d_lb, "hg_norm_g": d_ghn,
        "s5_a_re": d_are.reshape(1, S5_GROUPS, S5_STATE), "s5_a_im": d_aim.reshape(1, S5_GROUPS, S5_STATE),
        "s5_log_dt": d_ldt[0:1, :S5_GROUPS],
        "s5_b_re": back_t(d_br_t), "s5_b_im": back_t(d_bi_t),
        "s5_c_re": _diag_blocks(d_crt.transpose(0, 2, 1)).reshape(1, S5_GROUPS, S5_GROUP, S5_STATE),
        "s5_c_im": _diag_blocks(d_cit.transpose(0, 2, 1)).reshape(1, S5_GROUPS, S5_GROUP, S5_STATE),
        "s5_d": d_d.reshape(1, S5_GROUPS, S5_GROUP), "b_glu": d_bglu, "ple_norm_g": d_g2,
        "final_norm_g": d_g3.reshape(D_MODEL),
    }
    return loss_row, grad_x, gb, gs


def _shard_shape(name):
    r, c = BIG_SHAPE[name]
    return (r, c // N_CHIPS) if name in BIG_COL_SHARDED else (r // N_CHIPS, c)


def _pack_small(parts, last):
    flat = jnp.concatenate([parts[n].reshape(-1) for n in SMALL] + [last.reshape(-1)])
    return jnp.pad(flat, (0, SMALL_ROWS * PACK_W - flat.shape[0])).reshape(SMALL_ROWS, PACK_W)


def _unpack_small(packed):
    flat, out, off = packed.reshape(-1), {}, 0
    for n in SMALL:
        size = 1
        for d in SMALL_SHAPE[n]:
            size *= d
        out[n] = flat[off:off + size].reshape(SMALL_SHAPE[n])
        off += size
    return out, flat[off]


def _place():
    x, y, c = lax.axis_index("x"), lax.axis_index("y"), lax.axis_index("c")
    return x, y, c, [(1 - x, y), (x, 1 - y), (1 - x, 1 - y)]


def _remote(src, dst, send_sems, recv_sems, k, to):
    return pltpu.make_async_remote_copy(src_ref=src, dst_ref=dst, send_sem=send_sems.at[k], recv_sem=recv_sems.at[k],
                                        device_id=to, device_id_type=MESH)


REST = tuple(n for n in BIG if n != "w_in")
REST_ROWS = sum(BIG_SHAPE[n][0] * BIG_SHAPE[n][1] for n in REST) // (N_CHIPS * PACK_W)
IN_SHARD = IN_COLS // N_CHIPS
IN_TILE, REST_TILE = 256, 272


def _pack_rest(parts):
    return jnp.concatenate([parts[n].reshape(-1, PACK_W) for n in REST], axis=0)


def _unpack_rest(packed):
    out, off = {}, 0
    for n in REST:
        r, c = _shard_shape(n)
        rows = r * c // PACK_W
        out[n] = packed[off:off + rows].reshape(1, r, c)
        off += rows
    return out


def _unpack_rest_full(gathered):
    out, off = {}, 0
    for n in REST:
        r, c = _shard_shape(n)
        rows = r * c // PACK_W
        sh = gathered[:, off:off + rows].reshape(N_CHIPS, r, c)
        out[n] = sh.transpose(1, 0, 2).reshape(BIG_SHAPE[n]) if n in BIG_COL_SHARDED else sh.reshape(BIG_SHAPE[n])
        off += rows
    return out


def _pack_rest_full(full):
    parts = []
    for n in REST:
        r, c = _shard_shape(n)
        g = full[n]
        sh = g.reshape(BIG_SHAPE[n][0], N_CHIPS, c).transpose(1, 0, 2) if n in BIG_COL_SHARDED else g
        parts.append(sh.reshape(N_CHIPS, r * c // PACK_W, PACK_W))
    return jnp.concatenate(parts, axis=1).reshape(N_CHIPS, 2, REST_ROWS // 2, PACK_W)


def _swap_halves(pgs, name="exchange_halves"):
    n = len(pgs)

    def body(*refs):
        pg_refs, out_refs, (send_sems, recv_sems) = refs[:n], refs[n:2 * n], refs[2 * n:]
        x, y, c, _ = _place()
        cps = [_remote(pg_ref.at[j, 1 - c], out_ref.at[j], send_sems, recv_sems, N_CHIPS * g + j, (x, y, 1 - c))
               for g, (pg_ref, out_ref) in enumerate(zip(pg_refs, out_refs)) for j in range(N_CHIPS)]
        for cp in cps:
            cp.start()
        for cp in cps:
            cp.wait()

    return pl.pallas_call(
        body, name=name, in_specs=[_HBM] * n, out_specs=[_HBM] * n,
        out_shape=[jax.ShapeDtypeStruct((N_CHIPS,) + pg.shape[2:], pg.dtype) for pg in pgs],
        scratch_shapes=[pltpu.SemaphoreType.DMA((N_CHIPS * n,)), pltpu.SemaphoreType.DMA((N_CHIPS * n,))])(*pgs)


def _share_halves(gs):
    n = len(gs)

    def body(*refs):
        g_refs, out_refs, (send_sems, recv_sems) = refs[:n], refs[n:2 * n], refs[2 * n:]
        x, y, c, _ = _place()
        cps = [_remote(g_ref, out_ref.at[c], send_sems, recv_sems, g, (x, y, 1 - c))
               for g, (g_ref, out_ref) in enumerate(zip(g_refs, out_refs))]
        for cp in cps:
            cp.start()
        for g, (g_ref, out_ref) in enumerate(zip(g_refs, out_refs)):
            _remote(g_ref, out_ref.at[1 - c], send_sems, recv_sems, g, (x, y, 1 - c)).wait_recv()
        for cp in cps:
            cp.wait_send()

    return pl.pallas_call(
        body, name="share_half", in_specs=[_HBM] * n, out_specs=[_HBM] * n,
        out_shape=[jax.ShapeDtypeStruct((2,) + g.shape, g.dtype) for g in gs],
        scratch_shapes=[pltpu.SemaphoreType.DMA((n,)), pltpu.SemaphoreType.DMA((n,))])(*gs)


def _pair_sum(name, pg, theirs, c, tile):
    _, _, rows, width = pg.shape

    def body(c_ref, a_ref, b_ref, o_ref):
        o_ref[...] = (a_ref[...] + b_ref[...]).astype(o_ref.dtype)

    return pl.pallas_call(
        body, name=name,
        grid_spec=pltpu.PrefetchScalarGridSpec(
            num_scalar_prefetch=1, grid=(N_CHIPS, rows // tile),
            in_specs=[pl.BlockSpec((None, None, tile, width), lambda j, i, c_ref: (j, c_ref[0], i, 0)),
                      pl.BlockSpec((None, tile, width), lambda j, i, c_ref: (j, i, 0))],
            out_specs=pl.BlockSpec((None, tile, width), lambda j, i, c_ref: (j, i, 0))),
        out_shape=jax.ShapeDtypeStruct((N_CHIPS, rows, width), WIRE_DTYPE),
        compiler_params=_params("arbitrary", "arbitrary"))(c.reshape(1), pg, theirs)


def _chip_sum(name, ps, others, k, tile):
    _, rows, width = ps.shape

    def body(k_ref, a_ref, b_ref, o_ref):
        o_ref[...] = ((a_ref[...].astype(F32) + b_ref[0].astype(F32)) + b_ref[1].astype(F32)) + b_ref[2].astype(F32)

    return pl.pallas_call(
        body, name=name,
        grid_spec=pltpu.PrefetchScalarGridSpec(
            num_scalar_prefetch=1, grid=(rows // tile,),
            in_specs=[pl.BlockSpec((None, tile, width), lambda i, k_ref: (k_ref[0], i, 0)),
                      pl.BlockSpec((3, tile, width), lambda i, k_ref: (0, i, 0))],
            out_specs=pl.BlockSpec((tile, width), lambda i, k_ref: (i, 0))),
        out_shape=jax.ShapeDtypeStruct((rows, width), F32),
        compiler_params=_params("arbitrary"))(k.reshape(1), ps, others)


def _mm_in_gathering(x, g1, prologue, in_wire, chip, tm):
    m, k = x.shape
    half, ns = in_wire.shape[1:]
    nrow = m // tm

    def flip(j):
        return jnp.where(j == 1, 2, jnp.where(j == 2, 1, j))

    def body(k_ref, x_ref, g_ref, wire_ref, proj_ref, u_ref, all_ref, kept, b_ref, load_sems, send_sems, recv_sems):
        j, i = pl.program_id(0), pl.program_id(1)
        px, py, c, chips = _place()
        sibling = (px, py, 1 - c)

        def over_ici(r, chip_slot):
            cx, cy = chips[r]
            return _remote(wire_ref.at[c], all_ref.at[chip_slot, c], send_sems, recv_sems, r, (cx, cy, c))

        def to_sibling(r, half_slot):
            cx, cy = chips[r]
            return _remote(all_ref.at[2 * cx + cy, c], all_ref.at[2 * cx + cy, half_slot], send_sems, recv_sems,
                           3 + r, sibling)

        def loads(src, slot):
            return [pltpu.make_async_copy(src.at[h], b_ref.at[slot, pl.ds(h * half, half)], load_sems.at[h])
                    for h in range(2)]

        def shard(r):
            cx, cy = chips[r]
            over_ici(r, 2 * cx + cy).wait_recv()
            if r == 0:
                over_ici(2, 2 * px + py).start()
            to_sibling(r, c).start()
            to_sibling(r, 1 - c).wait_recv()
            return all_ref.at[2 * cx + cy]

        @pl.when((j == 0) & (i == 0))
        def _():
            for r in range(2):
                over_ici(r, 2 * px + py).start()
            for cp in loads(wire_ref, 0):
                cp.start()
            for cp in loads(wire_ref, 0):
                cp.wait()

        @pl.when((j == 1) & (i == 0))
        def _():
            cps = loads(shard(0), 1)
            for cp in cps:
                cp.start()
            for cp in cps:
                cp.wait()

        for nxt in (2, 3):
            @pl.when((j == nxt - 1) & (i == nrow // 2))
            def _(nxt=nxt):
                for cp in loads(shard(nxt - 1), nxt % 2):
                    cp.start()

            @pl.when((j == nxt) & (i == 0))
            def _(nxt=nxt):
                for cp in loads(wire_ref, nxt % 2):
                    cp.wait()

        rows = pl.ds(pl.multiple_of(i * tm, tm), tm)

        @pl.when(j == 0)
        def _():
            tile = _mx(prologue(x_ref[...], g_ref[...]))
            kept[rows, :] = tile
            u_ref[...] = tile

        proj_ref[...] = _dot(kept[rows, :], b_ref[lax.rem(j, 2)])

        @pl.when((j == N_CHIPS - 1) & (i == nrow - 1))
        def _():
            for r in range(3):
                over_ici(r, 2 * px + py).wait_send()
                to_sibling(r, c).wait_send()

    once = lambda j, i, k_ref: (jnp.where(j == 0, i, nrow - 1), 0)
    return pl.pallas_call(
        body, name="mm_in",
        grid_spec=pltpu.PrefetchScalarGridSpec(
            num_scalar_prefetch=1, grid=(N_CHIPS, nrow),
            in_specs=[pl.BlockSpec((tm, k), once), pl.BlockSpec(g1.shape, lambda j, i, k_ref: (0, 0)), _HBM],
            out_specs=[pl.BlockSpec((tm, ns), lambda j, i, k_ref: (i, jnp.bitwise_xor(k_ref[0], flip(j)))),
                       pl.BlockSpec((tm, k), once), _HBM],
            scratch_shapes=[pltpu.VMEM((m, k), MXU_DTYPE), pltpu.VMEM((2, 2 * half, ns), in_wire.dtype),
                            pltpu.SemaphoreType.DMA((2,)), pltpu.SemaphoreType.DMA((6,)),
                            pltpu.SemaphoreType.DMA((6,))]),
        out_shape=[jax.ShapeDtypeStruct((m, N_CHIPS * ns), F32), jax.ShapeDtypeStruct((m, k), MXU_DTYPE),
                   jax.ShapeDtypeStruct((N_CHIPS,) + in_wire.shape, in_wire.dtype)],
        compiler_params=_params("arbitrary", "arbitrary"))(chip.reshape(1), x, g1, in_wire)


class _StepComm:
    TILES = {"in": IN_TILE, "rest": REST_TILE}

    def __init__(self, in_wire, rest_wire, chip, core):
        self.in_wire, self.rest_wire, self.chip, self.core = in_wire, rest_wire, chip, core
        self.sums, self.landed = {}, {}

    def input_projection(self, x, g1, prologue, tm):
        proj, u, shards = _mm_in_gathering(x, g1, prologue, self.in_wire, self.chip, tm)
        shards = lax.dynamic_update_slice(shards, self.in_wire[None], (self.chip, 0, 0, 0))
        return proj, u, shards.reshape(N_CHIPS, D_MODEL, IN_SHARD)

    def gather_rest(self):
        wire = self.rest_wire

        def sends(ins, outs, send_sems, recv_sems):
            (w_ref,), (out_ref,) = ins, outs
            x, y, c, chips = _place()
            return [_remote(w_ref.at[c], out_ref.at[2 * x + y, c], send_sems, recv_sems, 4 * j + 2 * c + to,
                            (cx, cy, to)) for j, (cx, cy) in enumerate(chips) for to in (0, 1)]

        def recvs(ins, outs, send_sems, recv_sems):
            (w_ref,), (out_ref,) = ins, outs
            _, _, c, chips = _place()
            return [_remote(w_ref.at[c], out_ref.at[2 * cx + cy, by], send_sems, recv_sems, 4 * j + 2 * by + c,
                            (cx, cy, by)) for j, (cx, cy) in enumerate(chips) for by in (0, 1)]

        def start(*refs):
            for cp in sends(*refs):
                cp.start()

        def wait(*refs):
            for cp in recvs(*refs):
                cp.wait_recv()
            for cp in sends(*refs):
                cp.wait_send()

        return _Riding((wire,), (jax.ShapeDtypeStruct((N_CHIPS,) + wire.shape, wire.dtype),), 12, start, wait)

    def rest_weights(self, landed):
        full = lax.dynamic_update_slice(landed, self.rest_wire[None], (self.chip, 0, 0, 0))
        return _unpack_rest_full(full.reshape(N_CHIPS, REST_ROWS, PACK_W))

    def swap(self, pg):
        def copies(ins, outs, send_sems, recv_sems):
            (pg_ref,), (out_ref,) = ins, outs
            x, y, c, _ = _place()
            return [_remote(pg_ref.at[j, 1 - c], out_ref.at[j], send_sems, recv_sems, j, (x, y, 1 - c))
                    for j in range(N_CHIPS)]

        def start(*refs):
            for cp in copies(*refs):
                cp.start()

        def wait(*refs):
            for cp in copies(*refs):
                cp.wait()

        return _Riding((pg,), (jax.ShapeDtypeStruct((N_CHIPS,) + pg.shape[2:], pg.dtype),), N_CHIPS, start, wait)

    def scatter(self, group, pg, theirs=None):
        if theirs is None:
            (theirs,) = _swap_halves([pg], "exchange_halves_" + group)
        ps = _pair_sum("sum_pair_" + group, pg, theirs, self.core, self.TILES[group])
        self.sums[group] = ps

        def copies(ins, outs, send_sems, recv_sems):
            (ps_ref,), (out_ref,) = ins, outs
            _, _, c, chips = _place()
            return [_remote(ps_ref.at[2 * cx + cy], out_ref.at[j], send_sems, recv_sems, j, (cx, cy, c))
                    for j, (cx, cy) in enumerate(chips)]

        def start(*refs):
            for cp in copies(*refs):
                cp.start()

        def wait(*refs):
            for cp in copies(*refs):
                cp.wait()

        return _Riding((ps,), (jax.ShapeDtypeStruct((3,) + ps.shape[1:], ps.dtype),), 3, start, wait)

    def reduced(self, group):
        return _chip_sum("sum_chips_" + group, self.sums[group], self.landed[group], self.chip, self.TILES[group])


def _adamw(w, g, m, v):
    m = ADAM_B1 * m + (1.0 - ADAM_B1) * g
    v = ADAM_B2 * v + (1.0 - ADAM_B2) * (g * g)
    m_hat = m / (1.0 - ADAM_B1 ** ADAM_STEP)
    v_hat = v / (1.0 - ADAM_B2 ** ADAM_STEP)
    return -ADAM_LR * (m_hat / (jnp.sqrt(v_hat) + ADAM_EPS) + ADAM_WD * w), m, v


def _small_reduce_adamw(part, w, m, v):
    def body(part_ref, w_ref, m_ref, v_ref, g_ref, d_ref, nm_ref, nv_ref, all_ref, send_sems, recv_sems):
        x, y, c, chips = _place()
        me, sibling = (x, y, c), (x, y, 1 - c)

        def rows(px, py, pc):
            return all_ref.at[4 * px + 2 * py + pc]

        all_ref[4 * x + 2 * y + c] = part_ref[...]
        first = [_remote(part_ref, rows(*me), send_sems, recv_sems, 0, sibling)]
        first += [_remote(part_ref, rows(*me), send_sems, recv_sems, 1 + j, (cx, cy, c))
                  for j, (cx, cy) in enumerate(chips)]
        for cp in first:
            cp.start()
        passed = []
        for j, (cx, cy) in enumerate(chips):
            _remote(part_ref, rows(cx, cy, c), send_sems, recv_sems, 1 + j, me).wait_recv()
            cp = _remote(rows(cx, cy, c), rows(cx, cy, c), send_sems, recv_sems, 4 + j, sibling)
            cp.start()
            passed.append(cp)
        _remote(part_ref, rows(*sibling), send_sems, recv_sems, 0, me).wait_recv()
        for j, (cx, cy) in enumerate(chips):
            _remote(part_ref, rows(cx, cy, 1 - c), send_sems, recv_sems, 4 + j, me).wait_recv()
        for cp in first + passed:
            cp.wait_send()
        g = all_ref[0]
        for dev in range(1, N_DEV):
            g = g + all_ref[dev]
        delta, nm, nv = _adamw(w_ref[...], g, m_ref[...], v_ref[...])
        g_ref[...] = g
        d_ref[...] = delta
        nm_ref[...] = nm
        nv_ref[...] = nv

    whole = pl.BlockSpec(memory_space=pltpu.VMEM)
    shape = jax.ShapeDtypeStruct((SMALL_ROWS, PACK_W), F32)
    return pl.pallas_call(
        body, name="small_reduce_adamw", in_specs=[whole] * 4, out_specs=[whole] * 4, out_shape=[shape] * 4,
        scratch_shapes=[pltpu.VMEM((N_DEV, SMALL_ROWS, PACK_W), F32), pltpu.SemaphoreType.DMA((7,)),
                        pltpu.SemaphoreType.DMA((7,))],
        compiler_params=pltpu.CompilerParams(vmem_limit_bytes=VMEM_LIMIT))(part, w, m, v)


def kernel(x, p, norm_g, w_in, hg_lb, hg_norm_g, w_o_hg, s5_a_re, s5_a_im, s5_log_dt, s5_b_re, s5_b_im, s5_c_re, s5_c_im, s5_d, w_glu, b_glu, w_o_s5, w_out, ple_norm_g, w_ple, w_ple_gate, final_norm_g, loss_target, m_norm_g, m_w_in, m_hg_lb, m_hg_norm_g, m_w_o_hg, m_s5_a_re, m_s5_a_im, m_s5_log_dt, m_s5_b_re, m_s5_b_im, m_s5_c_re, m_s5_c_im, m_s5_d, m_w_glu, m_b_glu, m_w_o_s5, m_w_out, m_ple_norm_g, m_w_ple, m_w_ple_gate, m_final_norm_g, v_norm_g, v_w_in, v_hg_lb, v_hg_norm_g, v_w_o_hg, v_s5_a_re, v_s5_a_im, v_s5_log_dt, v_s5_b_re, v_s5_b_im, v_s5_c_re, v_s5_c_im, v_s5_d, v_w_glu, v_b_glu, v_w_o_s5, v_w_out, v_ple_norm_g, v_w_ple, v_w_ple_gate, v_final_norm_g):
    given = dict(locals())
    wts = {n: given[n] for n in WEIGHTS}
    mom = {n: given["m_" + n] for n in WEIGHTS}
    var = {n: given["v_" + n] for n in WEIGHTS}
    cx, cy, cc = lax.axis_index("x"), lax.axis_index("y"), lax.axis_index("c")
    chip = (2 * cx + cy).astype(jnp.int32)

    core = cc.astype(jnp.int32)
    rest_shard = _pack_rest({n: wts[n][0] for n in REST})
    comm = _StepComm(wts["w_in"][0].astype(MXU_DTYPE).reshape(2, D_MODEL // 2, IN_SHARD),
                     rest_shard.astype(MXU_DTYPE).reshape(2, REST_ROWS // 2, PACK_W), chip, core)

    t_len = x.shape[1]
    loss_row, grad_x, g_big, g_small = _local_step(x.reshape(t_len, D_MODEL), p.reshape(t_len, -1),
                                                   loss_target.reshape(t_len, D_MODEL), None,
                                                   {n: wts[n] for n in SMALL}, comm)

    zero = jnp.zeros((), F32)
    sg, sd, snm, snv = _small_reduce_adamw(_pack_small(g_small, loss_row[0, 0]),
                                           _pack_small({n: wts[n] for n in SMALL}, zero),
                                           _pack_small({n: mom[n] for n in SMALL}, zero),
                                           _pack_small({n: var[n] for n in SMALL}, zero))
    (sg, loss), (sd, _), (snm, _), (snv, _) = (_unpack_small(a) for a in (sg, sd, snm, snv))

    halves = [comm.reduced("in"), comm.reduced("rest")]
    g_in, g_rest = [lax.dynamic_update_slice(got, mine[None], (core, 0, 0))
                    for got, mine in zip(_share_halves(halves), halves)]
    g_in, g_rest = g_in.reshape(D_MODEL, IN_SHARD), g_rest.reshape(REST_ROWS, PACK_W)

    def adam_f(wv, gv, mv, vv):
        return _adamw(wv, gv, mv, vv)

    d_in, nm_in, nv_in = _rowwise("adamw_in", adam_f, D_MODEL, IN_TILE,
                                  [(wts["w_in"][0], IN_SHARD, 0), (g_in, IN_SHARD, 0), (mom["w_in"][0], IN_SHARD, 0),
                                   (var["w_in"][0], IN_SHARD, 0)], [], [(IN_SHARD, F32)] * 3)
    d_rest, nm_rest, nv_rest = _rowwise("adamw_rest", adam_f, REST_ROWS, REST_TILE,
                                        [(rest_shard, PACK_W, 0), (g_rest, PACK_W, 0),
                                         (_pack_rest({n: mom[n][0] for n in REST}), PACK_W, 0),
                                         (_pack_rest({n: var[n][0] for n in REST}), PACK_W, 0)], [],
                                        [(PACK_W, F32)] * 3)
    bg, bd, bnm, bnv = (dict(_unpack_rest(rest), w_in=a.reshape(1, D_MODEL, IN_SHARD))
                        for rest, a in ((g_rest, g_in), (d_rest, d_in), (nm_rest, nm_in), (nv_rest, nv_in)))

    outs = [loss, grad_x.reshape(x.shape)]
    for small, big in ((sg, bg), (sd, bd), (snm, bnm), (snv, bnv)):
        outs += [big[n] if n in BIG else small[n] for n in WEIGHTS]
    return tuple(outs)
```

```python
import functools
from typing import Callable, NamedTuple

import jax
import jax.numpy as jnp
from jax import lax
from jax.experimental import pallas as pl
from jax.experimental.pallas import tpu as pltpu

F32 = jnp.float32
MXU_DTYPE = jnp.bfloat16
WIRE_DTYPE = jnp.bfloat16
NORM_EPS = 1e-6
D_MODEL = 1024
HG_HEADS = 8
HG_DIM = 128
HG_CHUNK = 64
S5_WIDTH = 512
S5_GROUPS = 32
S5_GROUP = 16
S5_STATE = 64
S5_LANES = S5_GROUPS * S5_STATE
IN_COLS = 7168
SUBLANES = 8
VMEM_LIMIT = 56 * 1024 * 1024
HIGHEST = lax.Precision.HIGHEST
MESH = pl.DeviceIdType.MESH

ADAM_LR, ADAM_B1, ADAM_B2, ADAM_EPS, ADAM_WD, ADAM_STEP = 0.001, 0.9, 0.999, 1e-08, 0.01, 10

BIG = ("w_in", "w_o_hg", "w_glu", "w_o_s5", "w_out", "w_ple", "w_ple_gate")
BIG_SHAPE = {"w_in": (1024, 7168), "w_o_hg": (1024, 1024), "w_glu": (512, 1024), "w_o_s5": (512, 1024),
             "w_out": (1024, 1024), "w_ple": (256, 1024), "w_ple_gate": (1024, 1024)}
BIG_COL_SHARDED = ("w_in", "w_glu", "w_o_s5", "w_ple")
SMALL = ("norm_g", "hg_lb", "hg_norm_g", "s5_a_re", "s5_a_im", "s5_log_dt", "s5_b_re", "s5_b_im", "s5_c_re",
         "s5_c_im", "s5_d", "b_glu", "ple_norm_g", "final_norm_g")
SMALL_SHAPE = {"norm_g": (1, 1024), "hg_lb": (2, 1024), "hg_norm_g": (1, 1024), "s5_a_re": (1, 32, 64),
               "s5_a_im": (1, 32, 64), "s5_log_dt": (1, 32), "s5_b_re": (1, 32, 64, 16), "s5_b_im": (1, 32, 64, 16),
               "s5_c_re": (1, 32, 16, 64), "s5_c_im": (1, 32, 16, 64), "s5_d": (1, 32, 16), "b_glu": (1, 1024),
               "ple_norm_g": (1, 1024), "final_norm_g": (1024,)}
WEIGHTS = ("norm_g", "w_in", "hg_lb", "hg_norm_g", "w_o_hg", "s5_a_re", "s5_a_im", "s5_log_dt", "s5_b_re", "s5_b_im",
           "s5_c_re", "s5_c_im", "s5_d", "w_glu", "b_glu", "w_o_s5", "w_out", "ple_norm_g", "w_ple", "w_ple_gate",
           "final_norm_g")
N_CHIPS = 4
N_DEV = 8
PACK_W = 1024
SMALL_ROWS = 144


def _params(*sem):
    return pltpu.CompilerParams(dimension_semantics=sem, vmem_limit_bytes=VMEM_LIMIT)


def _sig(x):
    return 1.0 / (1.0 + jnp.exp(-x))


def _dsilu(z, s):
    return s * (1.0 + z * (1.0 - s))


def _mx(x):
    return x.astype(MXU_DTYPE)


def _dot(a, b, dims=(((1,), (0,)), ((), ()))):
    return lax.dot_general(_mx(a), _mx(b), dims, preferred_element_type=F32)


_NT = (((1,), (1,)), ((), ()))
_TN = (((0,), (0,)), ((), ()))


def _dot32(a, b):
    return jnp.dot(a, b, precision=HIGHEST, preferred_element_type=F32)


def _rms_bwd(dy, x, g):
    r = lax.rsqrt(jnp.mean(x * x, axis=-1, keepdims=True) + NORM_EPS)
    t = dy * g
    dx = r * t - x * (r * r * r) * jnp.mean(t * x, axis=-1, keepdims=True)
    return dx, jnp.sum(dy * x * r, axis=0, keepdims=True)


def _rowwise(name, fn, n_rows_total, tm, rows, consts, outs, accs=(), alias=None):
    n_r, n_c, n_o, n_a = len(rows), len(consts), len(outs), len(accs)

    def body(*refs):
        row_refs = refs[:n_r]
        const_refs = refs[n_r:n_r + n_c]
        pos = n_r + n_c + (1 if alias is not None else 0)
        out_refs = refs[pos:pos + n_o]
        acc_refs = refs[pos + n_o:pos + n_o + n_a]
        res = fn(*[r[...] for r in row_refs], *[r[...] for r in const_refs])
        for r, v in zip(out_refs, res[:n_o]):
            r[...] = v.astype(r.dtype)
        if n_a:
            @pl.when(pl.program_id(0) == 0)
            def _():
                for r in acc_refs:
                    r[...] = jnp.zeros_like(r)
            for r, v in zip(acc_refs, res[n_o:]):
                r[...] += v

    in_specs = [pl.BlockSpec((tm, w), functools.partial(lambda i, cb: (i, cb), cb=cb)) for (_, w, cb) in rows]
    in_specs += [pl.BlockSpec(c.shape, lambda i: (0, 0)) for c in consts]
    args = [a for (a, _, _) in rows] + list(consts)
    out_shape, out_specs = [], []
    for o in outs:
        w, dt = o[0], o[1]
        cb, total = (o[2], o[3]) if len(o) == 4 else (0, w)
        out_shape.append(jax.ShapeDtypeStruct((n_rows_total, total), dt))
        out_specs.append(pl.BlockSpec((tm, w), functools.partial(lambda i, cb: (i, cb), cb=cb)))
    io_alias = {}
    if alias is not None:
        in_specs.append(pl.BlockSpec(memory_space=pl.ANY))
        args.append(alias[0])
        io_alias = {len(args) - 1: alias[1]}
    for (r, w) in accs:
        out_shape.append(jax.ShapeDtypeStruct((r, w), F32))
        out_specs.append(pl.BlockSpec((r, w), lambda i: (0, 0)))
    res = pl.pallas_call(body, name=name, grid=(n_rows_total // tm,), in_specs=in_specs, out_specs=out_specs,
                         out_shape=out_shape, input_output_aliases=io_alias,
                         compiler_params=_params("arbitrary"))(*args)
    return res


class _Riding(NamedTuple):
    ins: tuple
    outs: tuple
    n_sems: int
    start: Callable
    wait: Callable


_HBM = pl.BlockSpec(memory_space=pl.ANY)


def _ride(riding, refs, n_in, n_out, n_scratch, first, last):
    if riding is None:
        return refs[:n_in], refs[n_in:n_in + n_out], refs[n_in + n_out:]
    r_in, r_out = len(riding.ins), len(riding.outs)
    ins, rins = refs[:n_in], refs[n_in:n_in + r_in]
    pos = n_in + r_in
    outs, routs = refs[pos:pos + n_out], refs[pos + n_out:pos + n_out + r_out]
    pos += n_out + r_out
    scratch, (send_sems, recv_sems) = refs[pos:pos + n_scratch], refs[pos + n_scratch:]

    @pl.when(first)
    def _():
        riding.start(rins, routs, send_sems, recv_sems)

    @pl.when(last)
    def _():
        riding.wait(rins, routs, send_sems, recv_sems)

    return ins, outs, scratch


def _riding_call(riding, body, name, grid, in_specs, args, out_specs, out_shape, scratch, io_alias=None):
    if riding is not None:
        in_specs = list(in_specs) + [_HBM] * len(riding.ins)
        args = list(args) + list(riding.ins)
        out_specs = list(out_specs) + [_HBM] * len(riding.outs)
        out_shape = list(out_shape) + list(riding.outs)
        scratch = list(scratch) + [pltpu.SemaphoreType.DMA((riding.n_sems,))] * 2
    return pl.pallas_call(body, name=name, grid=grid, in_specs=in_specs, out_specs=out_specs, out_shape=out_shape,
                          scratch_shapes=scratch, input_output_aliases=io_alias or {},
                          compiler_params=_params(*(["arbitrary"] * len(grid))))(*args)


def _mm_nn(name, a, b, tm, tn, riding=None, prologue=None, consts=()):
    m, k = a.shape
    n = b.shape[1] if b.ndim == 2 else b.shape[0] * b.shape[2]
    grid = (n // tn, m // tm)
    n_out, scratch = (1, []) if prologue is None else (2, [pltpu.VMEM((m, k), MXU_DTYPE)])

    def body(*refs):
        j, i = pl.program_id(0), pl.program_id(1)
        ins, outs, kept = _ride(riding, refs, 2 + len(consts), n_out, len(scratch), (j == 0) & (i == 0),
                                (j == grid[0] - 1) & (i == grid[1] - 1))
        if prologue is None:
            left = ins[0][...]
        else:
            rows = pl.ds(pl.multiple_of(i * tm, tm), tm)

            @pl.when(j == 0)
            def _():
                tile = _mx(prologue(ins[0][...], *[c[...] for c in ins[2:]]))
                kept[0][rows, :] = tile
                outs[1][...] = tile

            left = kept[0][rows, :]
        outs[0][...] = _dot(left, ins[1][...])

    once = (lambda j, i: (i, 0)) if prologue is None else (lambda j, i: (jnp.where(j == 0, i, grid[1] - 1), 0))
    b_spec = (pl.BlockSpec((k, tn), lambda j, i: (0, j)) if b.ndim == 2
              else pl.BlockSpec((None, k, tn), lambda j, i: (j, 0, 0)))
    in_specs = [pl.BlockSpec((tm, k), once), b_spec]
    in_specs += [pl.BlockSpec(c.shape, lambda j, i: (0, 0)) for c in consts]
    out_specs = [pl.BlockSpec((tm, tn), lambda j, i: (i, j))]
    out_shape = [jax.ShapeDtypeStruct((m, n), F32)]
    if prologue is not None:
        out_specs.append(pl.BlockSpec((tm, k), once))
        out_shape.append(jax.ShapeDtypeStruct((m, k), MXU_DTYPE))
    res = _riding_call(riding, body, name, grid, in_specs, [a, b] + list(consts), out_specs, out_shape, scratch)
    return res[0] if riding is None and prologue is None else res


def _mm_nt_then(name, a, b, tm, tn, fn, rows, consts, outs, accs=(), alias=None, riding=None):
    m, n = a.shape
    k = b.shape[-2]
    steps = n // tn
    n_r, n_c, n_o, n_a = len(rows), len(consts), len(outs), len(accs)

    def body(*refs):
        a_ref, b_ref = refs[:2]
        row_refs = refs[2:2 + n_r]
        const_refs = refs[2 + n_r:2 + n_r + n_c]
        i, s = pl.program_id(0), pl.program_id(1)
        n_in = 2 + n_r + n_c + (1 if alias is not None else 0)
        _, outs_, (mm_ref,) = _ride(riding, refs, n_in, n_o + n_a, 1, (i == 0) & (s == 0),
                                    (i == m // tm - 1) & (s == steps - 1))
        out_refs, acc_refs = outs_[:n_o], outs_[n_o:]
        part = _dot(a_ref[...], b_ref[...] if b.ndim == 2 else b_ref[s], _NT)
        if steps > 1:
            @pl.when(s == 0)
            def _():
                mm_ref[...] = jnp.zeros_like(mm_ref)
            mm_ref[...] += part

        @pl.when(s == steps - 1)
        def _():
            res = fn(mm_ref[...] if steps > 1 else part, *[r[...] for r in row_refs], *[r[...] for r in const_refs])
            for r, v in zip(out_refs, res[:n_o]):
                r[...] = v.astype(r.dtype)
            if n_a:
                @pl.when(i == 0)
                def _():
                    for r in acc_refs:
                        r[...] = jnp.zeros_like(r)
                for r, v in zip(acc_refs, res[n_o:]):
                    r[...] += v

    b_spec = (pl.BlockSpec((k, tn), lambda i, s: (0, s)) if b.ndim == 2
              else pl.BlockSpec(memory_space=pltpu.VMEM))
    in_specs = [pl.BlockSpec((tm, tn), lambda i, s: (i, s)), b_spec]
    in_specs += [pl.BlockSpec((tm, w), functools.partial(lambda i, s, cb: (i, cb), cb=cb)) for (_, w, cb) in rows]
    in_specs += [pl.BlockSpec(c.shape, lambda i, s: (0, 0)) for c in consts]
    args = [a, b] + [r[0] for r in rows] + list(consts)
    out_shape, out_specs = [], []
    for o in outs:
        w, dt = o[0], o[1]
        cb, total = (o[2], o[3]) if len(o) == 4 else (0, w)
        out_shape.append(jax.ShapeDtypeStruct((m, total), dt))
        out_specs.append(pl.BlockSpec((tm, w), functools.partial(lambda i, s, cb: (i, cb), cb=cb)))
    io_alias = {}
    if alias is not None:
        in_specs.append(pl.BlockSpec(memory_space=pl.ANY))
        args.append(alias[0])
        io_alias = {len(args) - 1: alias[1]}
    for (r, w) in accs:
        out_shape.append(jax.ShapeDtypeStruct((r, w), F32))
        out_specs.append(pl.BlockSpec((r, w), lambda i, s: (0, 0)))
    return _riding_call(riding, body, name, (m // tm, steps), in_specs, args, out_specs, out_shape,
                        [pltpu.VMEM((tm, k), F32)], io_alias)


def _mm_tn(name, a, b, tk, tn, col_shards=False, riding=None):
    t, k = a.shape
    n = b.shape[1]
    steps = t // tk

    def body(*refs):
        j, s = pl.program_id(0), pl.program_id(1)
        (a_ref, b_ref), (o_ref,), (acc_ref,) = _ride(riding, refs, 2, 1, 1, (j == 0) & (s == 0),
                                                     (j == n // tn - 1) & (s == steps - 1))

        @pl.when(s == 0)
        def _():
            acc_ref[...] = jnp.zeros_like(acc_ref)

        acc_ref[...] += _dot(a_ref[...], b_ref[...], _TN)

        @pl.when(s == steps - 1)
        def _():
            o_ref[...] = acc_ref[...]

    if col_shards:
        out_spec = pl.BlockSpec((None, k, tn), lambda j, s: (j, 0, 0))
        out_shape = jax.ShapeDtypeStruct((n // tn, k, tn), F32)
    else:
        out_spec = pl.BlockSpec((k, tn), lambda j, s: (0, j))
        out_shape = jax.ShapeDtypeStruct((k, n), F32)
    res = _riding_call(riding, body, name, (n // tn, steps),
                       [pl.BlockSpec((tk, k), lambda j, s: (s, 0)), pl.BlockSpec((tk, tn), lambda j, s: (s, j))],
                       [a, b], [out_spec], [out_shape], [pltpu.VMEM((k, tn), F32)])
    return res[0] if riding is None else res


def _dot01(m01, x):
    m = m01.astype(MXU_DTYPE)
    hi = x.astype(MXU_DTYPE)
    r1 = x - hi.astype(F32)
    mid = r1.astype(MXU_DTYPE)
    lo = (r1 - mid.astype(F32)).astype(MXU_DTYPE)
    dot = lambda v: jnp.dot(m, v, preferred_element_type=F32)
    return dot(hi) + dot(mid) + dot(lo)


def _chunk_rows(x, offset, nck):
    return jnp.concatenate([jnp.broadcast_to(x[c * HG_CHUNK + offset:c * HG_CHUNK + offset + 1, :],
                                             (HG_CHUNK, x.shape[1])) for c in range(nck)], axis=0)


def _hg_block_terms(q, f, lb, tb):
    nck = tb // HG_CHUNK
    sig = _sig(f)
    fv = lb + (1.0 - lb) * sig
    kk = (1.0 - lb) * (1.0 - sig)
    row = lax.broadcasted_iota(jnp.int32, (tb, tb), 0)
    col = lax.broadcasted_iota(jnp.int32, (tb, tb), 1)
    same = jnp.right_shift(row, 6) == jnp.right_shift(col, 6)
    causal, anti = same & (row >= col), same & (row <= col)
    b = _dot01(causal, jnp.log(fv))
    b_mid, b_last = _chunk_rows(b, HG_CHUNK // 2 - 1, nck), _chunk_rows(b, HG_CHUNK - 1, nck)
    e_mid, e_mid_inv = jnp.exp(b - b_mid), jnp.exp(b_mid - b)
    e_b, e_last = jnp.exp(b), jnp.exp(b_last - b)
    dcs = [jnp.exp(b[c * HG_CHUNK + HG_CHUNK - 1:(c + 1) * HG_CHUNK, :]) for c in range(nck)]
    return sig, fv, kk, causal, anti, e_mid, e_mid_inv, e_b, e_last, dcs


def _hgrn2_fwd(proj, hg_lb, hg_norm_g, t_len, tb, riding=None):
    nck = tb // HG_CHUNK
    nb = t_len // tb

    def body(*refs):
        step = pl.program_id(0)
        ((p_ref, lb_ref, gn_ref), (o_ref, act_ref, sp_ref),
         (st_ref, a_s, bm_s, qd_s, kd_s, v_s, sc_s, inc_s)) = _ride(riding, refs, 3, 3, 8, step == 0, step == nb - 1)

        @pl.when(pl.program_id(0) == 0)
        def _():
            st_ref[...] = jnp.zeros_like(st_ref)

        lb = _sig(lb_ref[0:1, :] - lb_ref[1:2, :])
        q = p_ref[:, pl.ds(0, 1024)]
        _, _, kk, causal, _, e_mid, e_mid_inv, e_b, e_last, dcs = _hg_block_terms(q, p_ref[:, pl.ds(1024, 1024)],
                                                                                   lb, tb)
        a_s[...] = _mx(q * e_mid)
        bm_s[...] = _mx(kk * e_mid_inv)
        qd_s[...] = _mx(q * e_b)
        kd_s[...] = _mx(kk * e_last)
        v_s[...] = _mx(p_ref[:, pl.ds(2048, 1024)])
        heads = [pl.ds(h * HG_DIM, HG_DIM) for h in range(HG_HEADS)]
        chunks = [pl.ds(c * HG_CHUNK, HG_CHUNK) for c in range(nck)]
        for h, hs in enumerate(heads):
            sc_s[h] = _mx(jnp.where(causal, _dot(a_s[:, hs], bm_s[:, hs], _NT), 0.0))
        for h, hs in enumerate(heads):
            o_ref[:, hs] = _dot(sc_s[h], v_s[:, hs])
        for h, hs in enumerate(heads):
            for c, r in enumerate(chunks):
                inc_s[h, c] = _dot(v_s[r, hs], kd_s[r, hs], _TN)
        for c in range(nck):
            for h in range(HG_HEADS):
                st = st_ref[h]
                sp_ref[h, c] = st
                st_ref[h] = dcs[c][:, h * HG_DIM:(h + 1) * HG_DIM] * st + inc_s[h, c]
        for c, r in enumerate(chunks):
            for h, hs in enumerate(heads):
                o_ref[r, hs] += _dot(qd_s[r, hs], sp_ref[h, c], _NT)
        for h, hs in enumerate(heads):
            o = o_ref[:, hs]
            rr = lax.rsqrt(jnp.mean(o * o, axis=-1, keepdims=True) + NORM_EPS)
            g = p_ref[:, pl.ds(3072 + h * HG_DIM, HG_DIM)]
            act_ref[:, hs] = (o * rr * gn_ref[:, hs] * (g * _sig(g))).astype(act_ref.dtype)

    return _riding_call(
        riding, body, "hgrn2_fwd", (nb,),
        [pl.BlockSpec((tb, 4096), lambda i: (i, 0)), pl.BlockSpec((2, 1024), lambda i: (0, 0)),
         pl.BlockSpec((1, 1024), lambda i: (0, 0))],
        [proj, hg_lb, hg_norm_g],
        [pl.BlockSpec((tb, 1024), lambda i: (i, 0)), pl.BlockSpec((tb, 1024), lambda i: (i, 0)),
         pl.BlockSpec((HG_HEADS, nck, HG_DIM, HG_DIM), lambda i: (0, i, 0, 0))],
        [jax.ShapeDtypeStruct((t_len, 1024), F32), jax.ShapeDtypeStruct((t_len, 1024), MXU_DTYPE),
         jax.ShapeDtypeStruct((HG_HEADS, t_len // HG_CHUNK, HG_DIM, HG_DIM), F32)],
        [pltpu.VMEM((HG_HEADS, HG_DIM, HG_DIM), F32)] + [pltpu.VMEM((tb, 1024), MXU_DTYPE)] * 5
        + [pltpu.VMEM((HG_HEADS, tb, tb), MXU_DTYPE), pltpu.VMEM((HG_HEADS, nck, HG_DIM, HG_DIM), F32)])


def _hgrn2_bwd(proj, d_o, s_prev, hg_lb, dproj, t_len, tb, riding=None):
    nck = tb // HG_CHUNK
    nb = t_len // tb

    def body(*refs):
        step = pl.program_id(0)
        ((p_ref, do_ref, sp_ref, lb_ref, _), (dp_ref, dlb_ref),
         (ds_ref, acc_ref, a_s, bm_s, qd_s, kd_s, v_s, do_s, da_s, dbm_s, dqd_s, dkd_s, dv_s, ex_s, sc_s, dsc_s,
          up_s)) = _ride(riding, refs, 5, 2, 17, step == 0, step == nb - 1)

        @pl.when(pl.program_id(0) == 0)
        def _():
            ds_ref[...] = jnp.zeros_like(ds_ref)
            acc_ref[...] = jnp.zeros_like(acc_ref)

        lb = _sig(lb_ref[0:1, :] - lb_ref[1:2, :])
        q = p_ref[:, pl.ds(0, 1024)]
        sig, fv, kk, causal, anti, e_mid, e_mid_inv, e_b, e_last, dcs = _hg_block_terms(
            q, p_ref[:, pl.ds(1024, 1024)], lb, tb)
        a, bm, qd, kd = q * e_mid, kk * e_mid_inv, q * e_b, kk * e_last
        a_s[...] = _mx(a)
        bm_s[...] = _mx(bm)
        qd_s[...] = _mx(qd)
        kd_s[...] = _mx(kd)
        v_s[...] = _mx(p_ref[:, pl.ds(2048, 1024)])
        do_s[...] = _mx(do_ref[...])
        heads = [pl.ds(h * HG_DIM, HG_DIM) for h in range(HG_HEADS)]
        chunks = [pl.ds(c * HG_CHUNK, HG_CHUNK) for c in range(nck)]
        for h, hs in enumerate(heads):
            sc_s[h] = _mx(jnp.where(causal, _dot(a_s[:, hs], bm_s[:, hs], _NT), 0.0))
            dsc_s[h] = _mx(jnp.where(causal, _dot(do_s[:, hs], v_s[:, hs], _NT), 0.0))
        for h, hs in enumerate(heads):
            dv_s[:, hs] = _dot(sc_s[h], do_s[:, hs], _TN)
            da_s[:, hs] = _dot(dsc_s[h], bm_s[:, hs])
            dbm_s[:, hs] = _dot(dsc_s[h], a_s[:, hs], _TN)
        for h, hs in enumerate(heads):
            for c, r in enumerate(chunks):
                up_s[h, c] = _dot(do_s[r, hs], qd_s[r, hs], _TN)
                dqd_s[r, hs] = _dot(do_s[r, hs], sp_ref[h, c])
        for c in reversed(range(nck)):
            r = chunks[c]
            for h, hs in enumerate(heads):
                dst = ds_ref[h]
                dc = dcs[c][:, h * HG_DIM:(h + 1) * HG_DIM]
                dv_s[r, hs] += _dot(kd_s[r, hs], dst, _NT)
                dkd_s[r, hs] = _dot(v_s[r, hs], dst)
                ex_s[c:c + 1, hs] = jnp.sum(dst * sp_ref[h, c], axis=0, keepdims=True) * dc
                ds_ref[h] = up_s[h, c] + dc * dst
        da, dbm, dqd, dkd = da_s[...], dbm_s[...], dqd_s[...], dkd_s[...]
        dq = da * e_mid + dqd * e_b
        dk = dbm * e_mid_inv + dkd * e_last
        db = da * a - dbm * bm + dqd * qd - dkd * kd
        dkk = dkd * kd
        extra = jnp.concatenate(
            [jnp.broadcast_to(jnp.sum(dkk[c * HG_CHUNK:(c + 1) * HG_CHUNK], axis=0, keepdims=True)
                              + ex_s[c:c + 1, :], (HG_CHUNK, 1024)) for c in range(nck)], axis=0)
        dlogf = _dot01(anti, db) + extra
        dfv_k = dlogf / fv - dk
        dp_ref[:, pl.ds(0, 1024)] = dq.astype(dp_ref.dtype)
        dp_ref[:, pl.ds(1024, 1024)] = (dfv_k * (1.0 - lb) * sig * (1.0 - sig)).astype(dp_ref.dtype)
        dp_ref[:, pl.ds(2048, 1024)] = dv_s[...].astype(dp_ref.dtype)
        acc_ref[...] += jnp.sum(dfv_k * (1.0 - sig), axis=0, keepdims=True)

        @pl.when(pl.program_id(0) == nb - 1)
        def _():
            g0 = acc_ref[...] * lb * (1.0 - lb)
            dlb_ref[0:1, :] = g0
            dlb_ref[1:2, :] = -g0

    return _riding_call(
        riding, body, "hgrn2_bwd", (nb,),
        [pl.BlockSpec((tb, 3072), lambda i: (nb - 1 - i, 0)),
         pl.BlockSpec((tb, 1024), lambda i: (nb - 1 - i, 0)),
         pl.BlockSpec((HG_HEADS, nck, HG_DIM, HG_DIM), lambda i: (0, nb - 1 - i, 0, 0)),
         pl.BlockSpec((2, 1024), lambda i: (0, 0)),
         pl.BlockSpec(memory_space=pl.ANY)],
        [proj, d_o, s_prev, hg_lb, dproj],
        [pl.BlockSpec((tb, 3072), lambda i: (nb - 1 - i, 0)), pl.BlockSpec((2, 1024), lambda i: (0, 0))],
        [jax.ShapeDtypeStruct((t_len, IN_COLS), dproj.dtype), jax.ShapeDtypeStruct((2, 1024), F32)],
        [pltpu.VMEM((HG_HEADS, HG_DIM, HG_DIM), F32), pltpu.VMEM((1, 1024), F32)]
        + [pltpu.VMEM((tb, 1024), MXU_DTYPE)] * 6 + [pltpu.VMEM((tb, 1024), F32)] * 5
        + [pltpu.VMEM((SUBLANES, 1024), F32)] + [pltpu.VMEM((HG_HEADS, tb, tb), MXU_DTYPE)] * 2
        + [pltpu.VMEM((HG_HEADS, nck, HG_DIM, HG_DIM), F32)], {4: 0})


def _s5_prep_bwd(a_re, a_im, log_dt, b_re_t, b_im_t, dlam, dbbr, dbbi):
    def body(ar_ref, ai_ref, ldt_ref, br_ref, bi_ref, dlam_ref, dbbr_ref, dbbi_ref,
             dar_ref, dai_ref, dldt_ref, dbr_ref, dbi_ref):
        ar, ai = ar_ref[...], ai_ref[...]
        dt = jnp.exp(ldt_ref[...])
        mag = jnp.exp(ar * dt)
        cs, sn = jnp.cos(ai * dt), jnp.sin(ai * dt)
        lr, li = mag * cs, mag * sn
        den = ar * ar + ai * ai
        nr = lr - 1.0
        sr = (nr * ar + li * ai) / den
        si = (li * ar - nr * ai) / den
        br, bi = br_ref[...], bi_ref[...]
        gbr, gbi = dbbr_ref[...], dbbi_ref[...]
        dbr_ref[...] = sr * gbr + si * gbi
        dbi_ref[...] = sr * gbi - si * gbr
        dsr = jnp.sum(gbr * br + gbi * bi, axis=0, keepdims=True)
        dsi = jnp.sum(gbi * br - gbr * bi, axis=0, keepdims=True)
        dnr = (dsr * ar - dsi * ai) / den
        dli = dlam_ref[1:2, :] + (dsr * ai + dsi * ar) / den
        dlr = dlam_ref[0:1, :] + dnr
        dden = -(dsr * sr + dsi * si) / den
        dar = (dsr * nr + dsi * li) / den + dden * 2.0 * ar
        dai = (dsr * li - dsi * nr) / den + dden * 2.0 * ai
        dmag = dlr * cs + dli * sn
        dth = mag * (dli * cs - dlr * sn)
        dar_ref[...] = dar + dmag * mag * dt
        dai_ref[...] = dai + dth * dt
        ddt = (dmag * mag * ar + dth * ai) * dt
        lane = lax.broadcasted_iota(jnp.int32, (S5_LANES, 128), 0) // S5_STATE
        grp = lax.broadcasted_iota(jnp.int32, (S5_LANES, 128), 1)
        dldt_ref[...] = _dot32(jnp.broadcast_to(ddt, (SUBLANES, S5_LANES)), (lane == grp).astype(F32))

    whole = pl.BlockSpec(memory_space=pltpu.VMEM)
    return pl.pallas_call(
        body, name="s5_prep_bwd", in_specs=[whole] * 8, out_specs=[whole] * 5,
        out_shape=[jax.ShapeDtypeStruct((1, S5_LANES), F32), jax.ShapeDtypeStruct((1, S5_LANES), F32),
                   jax.ShapeDtypeStruct((SUBLANES, 128), F32), jax.ShapeDtypeStruct((S5_GROUP, S5_LANES), F32),
                   jax.ShapeDtypeStruct((S5_GROUP, S5_LANES), F32)])(a_re, a_im, log_dt, b_re_t, b_im_t, dlam, dbbr,
                                                                      dbbi)


def _dgelu(x):
    c, a = 0.7978845608028654, 0.044715
    th = jnp.tanh(c * (x + a * x * x * x))
    return 0.5 * (1.0 + th) + 0.5 * x * (1.0 - th * th) * c * (1.0 + 3.0 * a * x * x)


S5_BLOCKS = 4
S5_BW = S5_WIDTH // S5_BLOCKS
S5_BL = S5_LANES // S5_BLOCKS
S5_LANE_BLOCKS = S5_LANES // 128
S5_SCAN_BLOCKS = 4


def _s5_prep(a_re, a_im, log_dt, b_re_t, b_im_t, seg):
    def body(ar_ref, ai_ref, ldt_ref, br_ref, bi_ref,
             rows_f, pfr_ref, pfi_ref, rows_r, prr_ref, pri_ref, bbr_ref, bbi_ref):
        ar, ai = ar_ref[...], ai_ref[...]
        dt = jnp.exp(ldt_ref[...])
        mag = jnp.exp(ar * dt)
        lr, li = mag * jnp.cos(ai * dt), mag * jnp.sin(ai * dt)
        den = ar * ar + ai * ai
        nr = lr - 1.0
        sr = (nr * ar + li * ai) / den
        si = (li * ar - nr * ai) / den
        wide = (SUBLANES, S5_LANES)
        cr, ci = lr, li
        for i in range(seg):
            pfr_ref[i] = jnp.broadcast_to(cr, wide)
            pfi_ref[i] = jnp.broadcast_to(ci, wide)
            prr_ref[seg - 1 - i] = jnp.broadcast_to(cr, wide)
            pri_ref[seg - 1 - i] = jnp.broadcast_to(-ci, wide)
            if i == seg - 1:
                for rows, sign in ((rows_f, 1.0), (rows_r, -1.0)):
                    rows[0:1, :] = lr
                    rows[1:2, :] = sign * li
                    rows[2:3, :] = cr
                    rows[3:4, :] = sign * ci
            cr, ci = cr * lr - ci * li, cr * li + ci * lr
        br, bi = br_ref[...], bi_ref[...]
        bbr_ref[...] = sr * br - si * bi
        bbi_ref[...] = sr * bi + si * br

    whole = pl.BlockSpec(memory_space=pltpu.VMEM)
    tables = [jax.ShapeDtypeStruct((4, S5_LANES), F32)] + [jax.ShapeDtypeStruct((seg, SUBLANES, S5_LANES), F32)] * 2
    bbar = [jax.ShapeDtypeStruct((S5_GROUP, S5_LANES), F32)] * 2
    res = pl.pallas_call(body, name="s5_prep", in_specs=[whole] * 5, out_specs=[whole] * 8,
                         out_shape=tables + tables + bbar)(a_re, a_im, log_dt, b_re_t, b_im_t)
    return res[0:3], res[3:6], res[6], res[7]


def _lanes(j):
    return pl.ds(j * 128, 128)


def _to_segment_order(v, stage_ref, out_ref, seg):
    nbl = v.shape[1] // 128
    for b in range(nbl):
        stage_ref[b] = v[:, b * 128:(b + 1) * 128]

    def body(t, carry):
        rows = pl.ds(pl.multiple_of(t * SUBLANES, SUBLANES), SUBLANES)
        for b in range(nbl):
            out_ref[rows, _lanes(b)] = stage_ref[b, pl.ds(t, SUBLANES, stride=seg), :]
        return carry

    lax.fori_loop(0, seg, body, 0, unroll=True)


def _from_segment_order(v, stage_ref, out_ref, seg):
    nbl = v.shape[1] // 128
    for b in range(nbl):
        stage_ref[b] = v[:, b * 128:(b + 1) * 128]
    for s in range(SUBLANES):
        def body(k, carry, s=s):
            rows = pl.ds(pl.multiple_of(s * seg + k * SUBLANES, SUBLANES), SUBLANES)
            for b in range(nbl):
                out_ref[rows, _lanes(b)] = stage_ref[b, pl.ds(k * SUBLANES * SUBLANES + s, SUBLANES,
                                                              stride=SUBLANES), :]
            return carry

        lax.fori_loop(0, seg // SUBLANES, body, 0, unroll=True)


def _tile_scan(xr_ref, xi_ref, lam_ref, car_ref, cai_ref, cn_r, cn_i, blocks, seg, reverse):
    shape = (SUBLANES, 128)
    lrs = [jnp.broadcast_to(lam_ref[0:1, _lanes(j)], shape) for j in blocks]
    lis = [jnp.broadcast_to(lam_ref[1:2, _lanes(j)], shape) for j in blocks]

    def step(k, carry):
        t = seg - 1 - k if reverse else k
        rows = pl.ds(pl.multiple_of(t * SUBLANES, SUBLANES), SUBLANES)
        out = []
        for n, j in enumerate(blocks):
            cr, ci = carry[2 * n], carry[2 * n + 1]
            nr = lrs[n] * cr - lis[n] * ci + xr_ref[rows, _lanes(j)]
            ni = lrs[n] * ci + lis[n] * cr + xi_ref[rows, _lanes(j)]
            xr_ref[rows, _lanes(j)] = nr
            xi_ref[rows, _lanes(j)] = ni
            out += [nr, ni]
        return tuple(out)

    zero = jnp.zeros(shape, F32)
    fin = lax.fori_loop(0, seg, step, (zero,) * (2 * len(blocks)), unroll=True)
    for n, j in enumerate(blocks):
        ls = _lanes(j)
        fr, fi = fin[2 * n], fin[2 * n + 1]
        sr, si = lam_ref[2:3, ls], lam_ref[3:4, ls]
        pr, pi = car_ref[:, ls], cai_ref[:, ls]
        for s in (reversed(range(SUBLANES)) if reverse else range(SUBLANES)):
            cn_r[s:s + 1, ls] = pr
            cn_i[s:s + 1, ls] = pi
            pr, pi = fr[s:s + 1, :] + sr * pr - si * pi, fi[s:s + 1, :] + sr * pi + si * pr
        car_ref[:, ls] = pr
        cai_ref[:, ls] = pi


def _s5_fwd(proj, lam_rows, p3_re, p3_im, bbr4, bbi4, crt4, cit4, d_row, t_len, tb):
    seg = tb // SUBLANES

    def body(u_ref, lam_ref, p3r_ref, p3i_ref, bbr_ref, bbi_ref, crt_ref, cit_ref, d_ref,
             hr_ref, hi_ref, ypre_ref, ys_ref, car_ref, cai_ref, cn_r, cn_i, stage_ref, us_ref, yseg_ref):
        @pl.when(pl.program_id(0) == 0)
        def _():
            car_ref[...] = jnp.zeros_like(car_ref)
            cai_ref[...] = jnp.zeros_like(cai_ref)

        _to_segment_order(u_ref[...], stage_ref, us_ref, seg)
        u = us_ref[...]
        for i in range(S5_BLOCKS):
            ui = u[:, i * S5_BW:(i + 1) * S5_BW]
            hr_ref[:, pl.ds(i * S5_BL, S5_BL)] = _dot(ui, bbr_ref[i])
            hi_ref[:, pl.ds(i * S5_BL, S5_BL)] = _dot(ui, bbi_ref[i])
        for lc in range(S5_LANE_BLOCKS // S5_SCAN_BLOCKS):
            blocks = range(lc * S5_SCAN_BLOCKS, (lc + 1) * S5_SCAN_BLOCKS)
            _tile_scan(hr_ref, hi_ref, lam_ref, car_ref, cai_ref, cn_r, cn_i, blocks, seg, False)
            crs = [cn_r[:, _lanes(j)] for j in blocks]
            cis = [cn_i[:, _lanes(j)] for j in blocks]

            def fix(t, carry, blocks=blocks, crs=crs, cis=cis):
                rows = pl.ds(pl.multiple_of(t * SUBLANES, SUBLANES), SUBLANES)
                for n, j in enumerate(blocks):
                    pr, pi = p3r_ref[t, :, _lanes(j)], p3i_ref[t, :, _lanes(j)]
                    hr_ref[rows, _lanes(j)] += pr * crs[n] - pi * cis[n]
                    hi_ref[rows, _lanes(j)] += pr * cis[n] + pi * crs[n]
                return carry

            lax.fori_loop(0, seg, fix, 0, unroll=True)
        for i in range(S5_BLOCKS):
            ws = pl.ds(i * S5_BW, S5_BW)
            bl = pl.ds(i * S5_BL, S5_BL)
            yseg_ref[:, ws] = (_dot(hr_ref[:, bl], crt_ref[i]) - _dot(hi_ref[:, bl], cit_ref[i])
                               + d_ref[:, ws] * u[:, i * S5_BW:(i + 1) * S5_BW])
        _from_segment_order(yseg_ref[...], stage_ref, ypre_ref, seg)
        ys_ref[...] = jax.nn.gelu(ypre_ref[...], approximate=True).astype(ys_ref.dtype)

    whole = pl.BlockSpec(memory_space=pltpu.VMEM)
    return pl.pallas_call(
        body, name="s5_fwd", grid=(t_len // tb,),
        in_specs=[pl.BlockSpec((tb, S5_WIDTH), lambda i: (i, 4096 // S5_WIDTH))] + [whole] * 8,
        out_specs=[pl.BlockSpec((tb, S5_LANES), lambda i: (i, 0)), pl.BlockSpec((tb, S5_LANES), lambda i: (i, 0)),
                   pl.BlockSpec((tb, S5_WIDTH), lambda i: (i, 0)), pl.BlockSpec((tb, S5_WIDTH), lambda i: (i, 0))],
        out_shape=[jax.ShapeDtypeStruct((t_len, S5_LANES), F32), jax.ShapeDtypeStruct((t_len, S5_LANES), F32),
                   jax.ShapeDtypeStruct((t_len, S5_WIDTH), F32), jax.ShapeDtypeStruct((t_len, S5_WIDTH), MXU_DTYPE)],
        scratch_shapes=[pltpu.VMEM((1, S5_LANES), F32), pltpu.VMEM((1, S5_LANES), F32),
                        pltpu.VMEM((SUBLANES, S5_LANES), F32), pltpu.VMEM((SUBLANES, S5_LANES), F32),
                        pltpu.VMEM((S5_WIDTH // 128, tb, 128), F32), pltpu.VMEM((tb, S5_WIDTH), F32),
                        pltpu.VMEM((tb, S5_WIDTH), F32)],
        compiler_params=_params("arbitrary"))(proj, lam_rows, p3_re, p3_im, bbr4, bbi4, crt4, cit4, d_row)


def _s5_bwd(dgelu, y_pre, proj, h_re, h_im, lam_rows, p3_re, p3_im, bbr4, bbi4, cr4, ci4, d_row, dproj, t_len, tb):
    seg = tb // SUBLANES
    nb = t_len // tb

    def body(dg_ref, yp_ref, u_ref, hr_ref, hi_ref, lam_ref, p3r_ref, p3i_ref, bbr_ref, bbi_ref, cr_ref, ci_ref,
             d_ref, _, du_ref, dbbr_ref, dbbi_ref, dcr_ref, dci_ref, dd_ref, dlam_ref,
             gr_ref, gi_ref, car_ref, cai_ref, cn_r, cn_i, stage_ref, us_ref, dys_ref, duseg_ref):
        @pl.when(pl.program_id(0) == 0)
        def _():
            for ref in (car_ref, cai_ref, dbbr_ref, dbbi_ref, dcr_ref, dci_ref, dd_ref, dlam_ref):
                ref[...] = jnp.zeros_like(ref)

        _to_segment_order(u_ref[...], stage_ref, us_ref, seg)
        _to_segment_order(dg_ref[...] * _dgelu(yp_ref[...]), stage_ref, dys_ref, seg)
        u, dy = us_ref[...], dys_ref[...]
        for i in range(S5_BLOCKS):
            dyi = dy[:, i * S5_BW:(i + 1) * S5_BW]
            gr_ref[:, pl.ds(i * S5_BL, S5_BL)] = _dot(dyi, cr_ref[i])
            gi_ref[:, pl.ds(i * S5_BL, S5_BL)] = -_dot(dyi, ci_ref[i])
        for lc in range(S5_LANE_BLOCKS // S5_SCAN_BLOCKS):
            blocks = range(lc * S5_SCAN_BLOCKS, (lc + 1) * S5_SCAN_BLOCKS)
            _tile_scan(gr_ref, gi_ref, lam_ref, car_ref, cai_ref, cn_r, cn_i, blocks, seg, True)
            crs = [cn_r[:, _lanes(j)] for j in blocks]
            cis = [cn_i[:, _lanes(j)] for j in blocks]

            def fix(k, carry, blocks=blocks, crs=crs, cis=cis):
                t = seg - 1 - k
                rows = pl.ds(pl.multiple_of(t * SUBLANES, SUBLANES), SUBLANES)
                out = []
                for n, j in enumerate(blocks):
                    nr, ni, slr, sli = carry[4 * n:4 * n + 4]
                    pr, pi = p3r_ref[t, :, _lanes(j)], p3i_ref[t, :, _lanes(j)]
                    g_r = gr_ref[rows, _lanes(j)] + pr * crs[n] - pi * cis[n]
                    g_i = gi_ref[rows, _lanes(j)] + pr * cis[n] + pi * crs[n]
                    gr_ref[rows, _lanes(j)] = g_r
                    gi_ref[rows, _lanes(j)] = g_i
                    hr, hi = hr_ref[rows, _lanes(j)], hi_ref[rows, _lanes(j)]
                    out += [g_r, g_i, slr + nr * hr + ni * hi, sli + ni * hr - nr * hi]
                return tuple(out)

            zero = jnp.zeros((SUBLANES, 128), F32)
            init = []
            for n in range(len(blocks)):
                init += [crs[n], cis[n], zero, zero]
            fin = lax.fori_loop(0, seg, fix, tuple(init), unroll=True)
            for n, j in enumerate(blocks):
                dlam_ref[0:1, _lanes(j)] += jnp.sum(fin[4 * n + 2], axis=0, keepdims=True)
                dlam_ref[1:2, _lanes(j)] += jnp.sum(fin[4 * n + 3], axis=0, keepdims=True)
        for i in range(S5_BLOCKS):
            ws = pl.ds(i * S5_BW, S5_BW)
            bl = pl.ds(i * S5_BL, S5_BL)
            ui, dyi = u[:, i * S5_BW:(i + 1) * S5_BW], dy[:, i * S5_BW:(i + 1) * S5_BW]
            gr, gi = gr_ref[:, bl], gi_ref[:, bl]
            duseg_ref[:, ws] = _dot(gr, bbr_ref[i], _NT) + _dot(gi, bbi_ref[i], _NT) + d_ref[:, ws] * dyi
            dbbr_ref[i] += _dot(ui, gr, _TN)
            dbbi_ref[i] += _dot(ui, gi, _TN)
            dcr_ref[i] += _dot(hr_ref[:, bl], dyi, _TN)
            dci_ref[i] -= _dot(hi_ref[:, bl], dyi, _TN)
        dd_ref[...] += jnp.sum(dy * u, axis=0, keepdims=True)
        _from_segment_order(duseg_ref[...], stage_ref, duseg_ref, seg)
        du_ref[...] = duseg_ref[...].astype(du_ref.dtype)

    whole = pl.BlockSpec(memory_space=pltpu.VMEM)
    rev = lambda i: (nb - 1 - i, 0)
    const3 = lambda i: (0, 0, 0)
    return pl.pallas_call(
        body, name="s5_bwd", grid=(nb,),
        in_specs=[pl.BlockSpec((tb, S5_WIDTH), rev), pl.BlockSpec((tb, S5_WIDTH), rev),
                  pl.BlockSpec((tb, S5_WIDTH), lambda i: (nb - 1 - i, 4096 // S5_WIDTH)),
                  pl.BlockSpec((tb, S5_LANES), rev), pl.BlockSpec((tb, S5_LANES), rev)] + [whole] * 8
                 + [pl.BlockSpec(memory_space=pl.ANY)],
        out_specs=[pl.BlockSpec((tb, S5_WIDTH), lambda i: (nb - 1 - i, 4096 // S5_WIDTH)),
                   pl.BlockSpec((S5_BLOCKS, S5_BW, S5_BL), const3), pl.BlockSpec((S5_BLOCKS, S5_BW, S5_BL), const3),
                   pl.BlockSpec((S5_BLOCKS, S5_BL, S5_BW), const3), pl.BlockSpec((S5_BLOCKS, S5_BL, S5_BW), const3),
                   pl.BlockSpec((1, S5_WIDTH), lambda i: (0, 0)), pl.BlockSpec((2, S5_LANES), lambda i: (0, 0))],
        out_shape=[jax.ShapeDtypeStruct((t_len, IN_COLS), dproj.dtype),
                   jax.ShapeDtypeStruct((S5_BLOCKS, S5_BW, S5_BL), F32),
                   jax.ShapeDtypeStruct((S5_BLOCKS, S5_BW, S5_BL), F32),
                   jax.ShapeDtypeStruct((S5_BLOCKS, S5_BL, S5_BW), F32),
                   jax.ShapeDtypeStruct((S5_BLOCKS, S5_BL, S5_BW), F32),
                   jax.ShapeDtypeStruct((1, S5_WIDTH), F32), jax.ShapeDtypeStruct((2, S5_LANES), F32)],
        scratch_shapes=[pltpu.VMEM((tb, S5_LANES), F32), pltpu.VMEM((tb, S5_LANES), F32),
                        pltpu.VMEM((1, S5_LANES), F32), pltpu.VMEM((1, S5_LANES), F32),
                        pltpu.VMEM((SUBLANES, S5_LANES), F32), pltpu.VMEM((SUBLANES, S5_LANES), F32),
                        pltpu.VMEM((S5_WIDTH // 128, tb, 128), F32), pltpu.VMEM((tb, S5_WIDTH), F32),
                        pltpu.VMEM((tb, S5_WIDTH), F32), pltpu.VMEM((tb, S5_WIDTH), F32)],
        input_output_aliases={13: 0},
        compiler_params=_params("arbitrary"))(dgelu, y_pre, proj, h_re, h_im, lam_rows, p3_re, p3_im, bbr4, bbi4,
                                              cr4, ci4, d_row, dproj)


def _block_diag(per_group):
    g8 = S5_GROUPS // S5_BLOCKS
    eye = jnp.eye(g8, dtype=bool)[None, :, None, :, None]
    dense = jnp.where(eye, per_group.reshape(S5_BLOCKS, g8, S5_GROUP, 1, S5_STATE), 0.0)
    return dense.reshape(S5_BLOCKS, S5_BW, S5_BL)


def _diag_blocks(dense):
    g8 = S5_GROUPS // S5_BLOCKS
    ar = jnp.arange(g8)
    d5 = dense.reshape(S5_BLOCKS, g8, S5_GROUP, g8, S5_STATE)
    return d5[:, ar, :, ar, :].transpose(1, 0, 2, 3).reshape(S5_GROUPS, S5_GROUP, S5_STATE)


def _hg_gate_bwd(da, o, g, gn):
    dos, dgs, dgns = [], [], []
    for h in range(HG_HEADS):
        sl = slice(h * HG_DIM, (h + 1) * HG_DIM)
        oh, gh, dah, gnh = o[:, sl], g[:, sl], da[:, sl], gn[:, sl]
        rr = lax.rsqrt(jnp.mean(oh * oh, axis=-1, keepdims=True) + NORM_EPS)
        sg = _sig(gh)
        dgs.append(dah * (oh * rr * gnh) * _dsilu(gh, sg))
        don = dah * (gh * sg)
        t = don * gnh
        dos.append(rr * t - oh * (rr * rr * rr) * jnp.mean(t * oh, axis=-1, keepdims=True))
        dgns.append(jnp.sum(don * oh * rr, axis=0, keepdims=True))
    return jnp.concatenate(dos, axis=1), jnp.concatenate(dgs, axis=1), jnp.concatenate(dgns, axis=1)


MIX_BWD_COLS = ((3072, 1024), (4608, 512), (5120, 1024), (6144, 1024))


def _mix_bwd(dgl, h1, dh2, act_hg, ys2, ys_gelu, merged, proj, o_hg, g2, ghn, b_glu, w, t_len, tm):
    nb = t_len // tm

    def body(dgl_ref, h1_ref, dh2_ref, act_ref, ys2_ref, ysg_ref, mg_ref, ghg_ref, z_ref, gh_ref, gs_ref, o_ref, g2_ref,
             gn_ref, bglu_ref, wg_ref, wo_ref, ws5_ref, whg_ref, wglu_ref,
             dh1_ref, dgelu_ref, do_ref, dg2_ref, dbglu_ref, dgn_ref, dproj_ref, dwo_ref, dws5_ref, dwglu_ref, dwhg_ref,
             st0, st1, st2, st3, a_out, a_s5, a_glu, a_hg, sems):
        i = pl.program_id(0)
        stages = (st0, st1, st2, st3)
        sums = ((a_out, dwo_ref), (a_s5, dws5_ref), (a_glu, dwglu_ref), (a_hg, dwhg_ref))

        def writes(step):
            rows = pl.ds(pl.multiple_of(step * tm, tm), tm)
            return [pltpu.make_async_copy(st, dproj_ref.at[rows, pl.ds(c0, wd)], sems.at[k])
                    for k, (st, (c0, wd)) in enumerate(zip(stages, MIX_BWD_COLS))]

        @pl.when(i > 0)
        def _():
            for cp in writes(i - 1):
                cp.wait()

        @pl.when(i == 0)
        def _():
            for ref in (dg2_ref, dbglu_ref, dgn_ref, a_out, a_s5, a_glu, a_hg):
                ref[...] = jnp.zeros_like(ref)

        dx, dg2 = _rms_bwd(_dot(dgl_ref[...], wg_ref[...], _NT), h1_ref[...], g2_ref[...])
        dh1 = dh2_ref[...] + dx
        dh1_ref[...] = dh1
        dg2_ref[...] += dg2
        a_out[...] += _dot(mg_ref[...], dh1, _TN)
        dm = _dot(dh1, wo_ref[...], _NT)
        sh, ss = _sig(gh_ref[...]), _sig(gs_ref[...])
        dyh, dys = _mx(dm * sh), _mx(dm * ss)
        a_hg[...] += _dot(act_ref[...], dyh, _TN)
        a_s5[...] += _dot(ys2_ref[...], dys, _TN)
        st2[...] = (dm * _dot(act_ref[...], whg_ref[...]) * sh * (1.0 - sh)).astype(st2.dtype)
        st3[...] = (dm * _dot(ys2_ref[...], ws5_ref[...]) * ss * (1.0 - ss)).astype(st3.dtype)
        dys2 = _dot(dys, ws5_ref[...], _NT)
        gl_, z = _dot(ysg_ref[...], wglu_ref[...]) + bglu_ref[...], z_ref[...]
        a, b = gl_[:, :S5_WIDTH], gl_[:, S5_WIDTH:]
        sb, sz = _sig(b), _sig(z)
        silu = z * sz
        dglu = jnp.concatenate([dys2 * sb * silu, dys2 * a * silu * sb * (1.0 - sb)], axis=1)
        st1[...] = (dys2 * a * sb * _dsilu(z, sz)).astype(st1.dtype)
        dbglu_ref[...] += jnp.sum(dglu, axis=0, keepdims=True)
        dglu = _mx(dglu)
        a_glu[...] += _dot(ysg_ref[...], dglu, _TN)
        dgelu_ref[...] = _dot(dglu, wglu_ref[...], _NT)
        d_o, dg, dgn = _hg_gate_bwd(_dot(dyh, whg_ref[...], _NT), o_ref[...], ghg_ref[...], gn_ref[...])
        do_ref[...] = d_o.astype(do_ref.dtype)
        st0[...] = dg.astype(st0.dtype)
        dgn_ref[...] += dgn
        for cp in writes(i):
            cp.start()

        @pl.when(i == nb - 1)
        def _():
            outs = [pltpu.make_async_copy(acc, out, sems.at[4 + k]) for k, (acc, out) in enumerate(sums)]
            for cp in outs:
                cp.start()
            for cp in writes(i) + outs:
                cp.wait()

    tile = lambda wd, cb=0: pl.BlockSpec((tm, wd), functools.partial(lambda i, cb: (i, cb), cb=cb))
    row = lambda wd: pl.BlockSpec((1, wd), lambda i: (0, 0))
    whole = pl.BlockSpec(memory_space=pltpu.VMEM)
    sums_shape = [(1024, 1024), (S5_WIDTH, 1024), (S5_WIDTH, 1024), (1024, 1024)]
    return pl.pallas_call(
        body, name="mix_bwd", grid=(nb,),
        in_specs=[tile(1024), tile(1024), tile(1024), tile(1024), tile(512), tile(512), tile(1024), tile(1024, 3),
                  tile(512, 4608 // 512), tile(1024, 5), tile(1024, 6), tile(1024), row(1024), row(1024), row(1024)]
                 + [whole] * 5,
        out_specs=[tile(1024), tile(512), tile(1024), row(1024), row(1024), row(1024)] + [_HBM] * 5,
        out_shape=[jax.ShapeDtypeStruct((t_len, 1024), F32), jax.ShapeDtypeStruct((t_len, 512), F32),
                   jax.ShapeDtypeStruct((t_len, 1024), MXU_DTYPE),
                   jax.ShapeDtypeStruct((1, 1024), F32), jax.ShapeDtypeStruct((1, 1024), F32),
                   jax.ShapeDtypeStruct((1, 1024), F32), jax.ShapeDtypeStruct((t_len, IN_COLS), MXU_DTYPE)]
                  + [jax.ShapeDtypeStruct(sh, F32) for sh in sums_shape],
        scratch_shapes=[pltpu.VMEM((tm, wd), MXU_DTYPE) for _, wd in MIX_BWD_COLS]
                       + [pltpu.VMEM(sh, F32) for sh in sums_shape] + [pltpu.SemaphoreType.DMA((8,))],
        compiler_params=_params("arbitrary"))(dgl, h1, dh2, act_hg, ys2, ys_gelu, merged, proj, proj, proj, proj, o_hg,
                                              g2, ghn, b_glu, w["w_ple_gate"], w["w_out"], w["w_o_s5"], w["w_o_hg"],
                                              w["w_glu"])


def _local_step(x, p, target, w, sm, comm=None):
    t_len = x.shape[0]
    tm = min(256, t_len)
    tmm = min(512, t_len)
    tb_hg = min(256, t_len)
    tb_s5 = min(256, t_len)
    g1, g2, g3, ghn = sm["norm_g"], sm["ple_norm_g"], sm["final_norm_g"].reshape(1, D_MODEL), sm["hg_norm_g"]

    def rms_in(xv, g):
        return xv * lax.rsqrt(jnp.mean(xv * xv, axis=-1, keepdims=True) + NORM_EPS) * g

    in_shard = IN_COLS // N_CHIPS
    if comm is None:
        w_in = w["w_in"]
        proj, u = _mm_nn("mm_in", x, w_in, tmm, in_shard, prologue=rms_in, consts=[g1])
    else:
        proj, u, w_in = comm.input_projection(x, g1, rms_in, tmm)

    lanes = lambda a: a.reshape(1, S5_LANES)
    a_re, a_im = lanes(sm["s5_a_re"]), lanes(sm["s5_a_im"])
    ldt = lanes(jnp.broadcast_to(sm["s5_log_dt"].reshape(S5_GROUPS, 1), (S5_GROUPS, S5_STATE)))
    to_t = lambda b: b.reshape(S5_GROUPS, S5_STATE, S5_GROUP).transpose(2, 0, 1).reshape(S5_GROUP, S5_LANES)
    b_re_t, b_im_t = to_t(sm["s5_b_re"]), to_t(sm["s5_b_im"])
    scan_fwd, scan_rev, bbr_t, bbi_t = _s5_prep(a_re, a_im, ldt, b_re_t, b_im_t, tb_s5 // SUBLANES)
    from_t = lambda b: b.reshape(S5_GROUP, S5_GROUPS, S5_STATE).transpose(1, 0, 2)
    bbr_bd = _block_diag(from_t(bbr_t)).astype(MXU_DTYPE)
    bbi_bd = _block_diag(from_t(bbi_t)).astype(MXU_DTYPE)
    cr_bd = _block_diag(sm["s5_c_re"].reshape(S5_GROUPS, S5_GROUP, S5_STATE)).astype(MXU_DTYPE)
    ci_bd = _block_diag(sm["s5_c_im"].reshape(S5_GROUPS, S5_GROUP, S5_STATE)).astype(MXU_DTYPE)
    d_row = sm["s5_d"].reshape(1, S5_WIDTH)
    if comm is None:
        o_hg, act_hg, s_prev = _hgrn2_fwd(proj, sm["hg_lb"], ghn, t_len, tb_hg)
    else:
        o_hg, act_hg, s_prev, landed = _hgrn2_fwd(proj, sm["hg_lb"], ghn, t_len, tb_hg, riding=comm.gather_rest())
        w = comm.rest_weights(landed)
    h_re, h_im, y_pre, ys_gelu = _s5_fwd(proj, *scan_fwd, bbr_bd, bbi_bd,
                                          cr_bd.transpose(0, 2, 1), ci_bd.transpose(0, 2, 1), d_row, t_len, tb_s5)
    def mix_f(act, ysg, z, gh, gs, xv, w_glu, b_glu, w_o_hg, w_o_s5, w_out):
        yh = _dot(act, w_o_hg)
        gl_ = _dot(ysg, w_glu) + b_glu
        a, b = gl_[:, :S5_WIDTH], gl_[:, S5_WIDTH:]
        ys2_ = (a * _sig(b) * (z * _sig(z))).astype(MXU_DTYPE)
        ys = _dot(ys2_, w_o_s5)
        mg = (_sig(gh) * yh + _sig(gs) * ys).astype(MXU_DTYPE)
        return (ys2_, mg, xv + _dot(mg, w_out))

    ys2, merged, h1 = _rowwise(
        "mix_out", mix_f, t_len, tm,
        [(act_hg, 1024, 0), (ys_gelu, 512, 0), (proj, 512, 4608 // 512), (proj, 1024, 5), (proj, 1024, 6),
         (x, 1024, 0)], [w["w_glu"], sm["b_glu"], w["w_o_hg"], w["w_o_s5"], w["w_out"]],
        [(512, MXU_DTYPE), (1024, MXU_DTYPE), (1024, F32)])

    def head_f(h1v, pv, tgt, g_ple, g, w_ple, w_gate):
        r2 = lax.rsqrt(jnp.mean(h1v * h1v, axis=-1, keepdims=True) + NORM_EPS)
        n2_ = (h1v * r2 * g_ple).astype(MXU_DTYPE)
        glv, pev = _dot(n2_, w_gate), _dot(pv, w_ple)
        gate = _sig(glv)
        h2 = h1v + pev * gate
        r = lax.rsqrt(jnp.mean(h2 * h2, axis=-1, keepdims=True) + NORM_EPS)
        e = h2 * r * g - tgt
        loss = 0.5 * jnp.sum(jnp.mean(e * e, axis=-1, keepdims=True), axis=0, keepdims=True)
        dy = e * (1.0 / D_MODEL)
        dg = jnp.sum(dy * h2 * r, axis=0, keepdims=True)
        t = dy * g
        dh2 = r * t - h2 * (r * r * r) * jnp.mean(t * h2, axis=-1, keepdims=True)
        dpe, dgl_ = _mx(dh2 * gate), _mx(dh2 * pev * gate * (1.0 - gate))
        return (dh2, dgl_, jnp.broadcast_to(loss, (1, 128)), dg, _dot(pv, dpe, _TN), _dot(n2_, dgl_, _TN))

    gb = {}
    dh2, dgl, loss_row, d_g3, gb["w_ple"], gb["w_ple_gate"] = _rowwise(
        "ple_loss_head", head_f, t_len, tm, [(h1, 1024, 0), (p, 256, 0), (target, 1024, 0)],
        [g2, g3, w["w_ple"], w["w_ple_gate"]], [(1024, F32), (1024, MXU_DTYPE)],
        accs=[(1, 128), (1, 1024), (256, 1024), (1024, 1024)])

    dh1, dgelu, d_o, d_g2, d_bglu, d_ghn, dproj, gb["w_out"], gb["w_o_s5"], gb["w_glu"], gb["w_o_hg"] = _mix_bwd(
        dgl, h1, dh2, act_hg, ys2, ys_gelu, merged, proj, o_hg, g2, ghn, sm["b_glu"], w, t_len, min(128, t_len))
    dproj, d_bbr, d_bbi, d_crt, d_cit, d_d, d_lam = _s5_bwd(dgelu, y_pre, proj, h_re, h_im,
                                                            *scan_rev, bbr_bd, bbi_bd, cr_bd,
                                                            ci_bd, d_row, dproj, t_len, tb_s5)
    to_t3 = lambda b: b.transpose(1, 0, 2).reshape(S5_GROUP, S5_LANES)
    d_are, d_aim, d_ldt, d_br_t, d_bi_t = _s5_prep_bwd(a_re, a_im, ldt, b_re_t, b_im_t, d_lam,
                                                       to_t3(_diag_blocks(d_bbr)), to_t3(_diag_blocks(d_bbi)))
    if comm is None:
        dproj, d_lb = _hgrn2_bwd(proj, d_o, s_prev, sm["hg_lb"], dproj, t_len, tb_hg)
    else:
        rest_grads = _pack_rest_full(gb)
        dproj, d_lb, rest_theirs = _hgrn2_bwd(proj, d_o, s_prev, sm["hg_lb"], dproj, t_len, tb_hg,
                                               riding=comm.swap(rest_grads))

    def in_b(duv, xv, dh, g):
        dx, dg = _rms_bwd(duv, xv, g)
        return (dh + dx, dg)

    in_args = ("mm_d_u_rms_in_bwd", dproj, w_in, tmm, in_shard, in_b, [(x, 1024, 0), (dh1, 1024, 0)], [g1],
               [(1024, F32)])
    if comm is None:
        gb["w_in"] = _mm_tn("mm_d_w_in", u, dproj, tmm, in_shard, col_shards=True)
        grad_x, d_g1 = _mm_nt_then(*in_args, accs=[(1, 1024)])
    else:
        gb["w_in"], landed = _mm_tn("mm_d_w_in", u, dproj, tmm, in_shard, col_shards=True,
                                    riding=comm.scatter("rest", rest_grads, rest_theirs))
        comm.landed["rest"] = landed
        grad_x, d_g1, landed = _mm_nt_then(*in_args, accs=[(1, 1024)], riding=comm.scatter(
            "in", gb["w_in"].reshape(N_CHIPS, 2, D_MODEL // 2, in_shard)))
        comm.landed["in"] = landed

    back_t = lambda b: b.reshape(S5_GROUP, S5_GROUPS, S5_STATE).transpose(1, 2, 0).reshape(1, S5_GROUPS, S5_STATE,
                                                                                           S5_GROUP)
    gs = {
        "norm_g": d_g1, "hg_lb": d_lb, "hg_norm_g": d_ghn,
        "s5_a_re": d_are.reshape(1, S5_GROUPS, S5_STATE), "s5_a_im": d_aim.reshape(1, S5_GROUPS, S5_STATE),
        "s5_log_dt": d_ldt[0:1, :S5_GROUPS],
        "s5_b_re": back_t(d_br_t), "s5_b_im": back_t(d_bi_t),
        "s5_c_re": _diag_blocks(d_crt.transpose(0, 2, 1)).reshape(1, S5_GROUPS, S5_GROUP, S5_STATE),
        "s5_c_im": _diag_blocks(d_cit.transpose(0, 2, 1)).reshape(1, S5_GROUPS, S5_GROUP, S5_STATE),
        "s5_d": d_d.reshape(1, S5_GROUPS, S5_GROUP), "b_glu": d_bglu, "ple_norm_g": d_g2,
        "final_norm_g": d_g3.reshape(D_MODEL),
    }
    return loss_row, grad_x, gb, gs


def _shard_shape(name):
    r, c = BIG_SHAPE[name]
    return (r, c // N_CHIPS) if name in BIG_COL_SHARDED else (r // N_CHIPS, c)


def _pack_small(parts, last):
    flat = jnp.concatenate([parts[n].reshape(-1) for n in SMALL] + [last.reshape(-1)])
    return jnp.pad(flat, (0, SMALL_ROWS * PACK_W - flat.shape[0])).reshape(SMALL_ROWS, PACK_W)


def _unpack_small(packed):
    flat, out, off = packed.reshape(-1), {}, 0
    for n in SMALL:
        size = 1
        for d in SMALL_SHAPE[n]:
            size *= d
        out[n] = flat[off:off + size].reshape(SMALL_SHAPE[n])
        off += size
    return out, flat[off]


def _place():
    x, y, c = lax.axis_index("x"), lax.axis_index("y"), lax.axis_index("c")
    return x, y, c, [(1 - x, y), (x, 1 - y), (1 - x, 1 - y)]


def _remote(src, dst, send_sems, recv_sems, k, to):
    return pltpu.make_async_remote_copy(src_ref=src, dst_ref=dst, send_sem=send_sems.at[k], recv_sem=recv_sems.at[k],
                                        device_id=to, device_id_type=MESH)


REST = tuple(n for n in BIG if n != "w_in")
REST_ROWS = sum(BIG_SHAPE[n][0] * BIG_SHAPE[n][1] for n in REST) // (N_CHIPS * PACK_W)
IN_SHARD = IN_COLS // N_CHIPS
IN_TILE, REST_TILE = 256, 272


def _pack_rest(parts):
    return jnp.concatenate([parts[n].reshape(-1, PACK_W) for n in REST], axis=0)


def _unpack_rest(packed):
    out, off = {}, 0
    for n in REST:
        r, c = _shard_shape(n)
        rows = r * c // PACK_W
        out[n] = packed[off:off + rows].reshape(1, r, c)
        off += rows
    return out


def _unpack_rest_full(gathered):
    out, off = {}, 0
    for n in REST:
        r, c = _shard_shape(n)
        rows = r * c // PACK_W
        sh = gathered[:, off:off + rows].reshape(N_CHIPS, r, c)
        out[n] = sh.transpose(1, 0, 2).reshape(BIG_SHAPE[n]) if n in BIG_COL_SHARDED else sh.reshape(BIG_SHAPE[n])
        off += rows
    return out


def _pack_rest_full(full):
    parts = []
    for n in REST:
        r, c = _shard_shape(n)
        g = full[n]
        sh = g.reshape(BIG_SHAPE[n][0], N_CHIPS, c).transpose(1, 0, 2) if n in BIG_COL_SHARDED else g
        parts.append(sh.reshape(N_CHIPS, r * c // PACK_W, PACK_W))
    return jnp.concatenate(parts, axis=1).reshape(N_CHIPS, 2, REST_ROWS // 2, PACK_W)


def _swap_halves(pgs, name="exchange_halves"):
    n = len(pgs)

    def body(*refs):
        pg_refs, out_refs, (send_sems, recv_sems) = refs[:n], refs[n:2 * n], refs[2 * n:]
        x, y, c, _ = _place()
        cps = [_remote(pg_ref.at[j, 1 - c], out_ref.at[j], send_sems, recv_sems, N_CHIPS * g + j, (x, y, 1 - c))
               for g, (pg_ref, out_ref) in enumerate(zip(pg_refs, out_refs)) for j in range(N_CHIPS)]
        for cp in cps:
            cp.start()
        for cp in cps:
            cp.wait()

    return pl.pallas_call(
        body, name=name, in_specs=[_HBM] * n, out_specs=[_HBM] * n,
        out_shape=[jax.ShapeDtypeStruct((N_CHIPS,) + pg.shape[2:], pg.dtype) for pg in pgs],
        scratch_shapes=[pltpu.SemaphoreType.DMA((N_CHIPS * n,)), pltpu.SemaphoreType.DMA((N_CHIPS * n,))])(*pgs)


def _share_halves(gs):
    n = len(gs)

    def body(*refs):
        g_refs, out_refs, (send_sems, recv_sems) = refs[:n], refs[n:2 * n], refs[2 * n:]
        x, y, c, _ = _place()
        cps = [_remote(g_ref, out_ref.at[c], send_sems, recv_sems, g, (x, y, 1 - c))
               for g, (g_ref, out_ref) in enumerate(zip(g_refs, out_refs))]
        for cp in cps:
            cp.start()
        for g, (g_ref, out_ref) in enumerate(zip(g_refs, out_refs)):
            _remote(g_ref, out_ref.at[1 - c], send_sems, recv_sems, g, (x, y, 1 - c)).wait_recv()
        for cp in cps:
            cp.wait_send()

    return pl.pallas_call(
        body, name="share_half", in_specs=[_HBM] * n, out_specs=[_HBM] * n,
        out_shape=[jax.ShapeDtypeStruct((2,) + g.shape, g.dtype) for g in gs],
        scratch_shapes=[pltpu.SemaphoreType.DMA((n,)), pltpu.SemaphoreType.DMA((n,))])(*gs)


def _pair_sum(name, pg, theirs, c, tile):
    _, _, rows, width = pg.shape

    def body(c_ref, a_ref, b_ref, o_ref):
        o_ref[...] = (a_ref[...] + b_ref[...]).astype(o_ref.dtype)

    return pl.pallas_call(
        body, name=name,
        grid_spec=pltpu.PrefetchScalarGridSpec(
            num_scalar_prefetch=1, grid=(N_CHIPS, rows // tile),
            in_specs=[pl.BlockSpec((None, None, tile, width), lambda j, i, c_ref: (j, c_ref[0], i, 0)),
                      pl.BlockSpec((None, tile, width), lambda j, i, c_ref: (j, i, 0))],
            out_specs=pl.BlockSpec((None, tile, width), lambda j, i, c_ref: (j, i, 0))),
        out_shape=jax.ShapeDtypeStruct((N_CHIPS, rows, width), WIRE_DTYPE),
        compiler_params=_params("arbitrary", "arbitrary"))(c.reshape(1), pg, theirs)


def _chip_sum(name, ps, others, k, tile):
    _, rows, width = ps.shape

    def body(k_ref, a_ref, b_ref, o_ref):
        o_ref[...] = ((a_ref[...].astype(F32) + b_ref[0].astype(F32)) + b_ref[1].astype(F32)) + b_ref[2].astype(F32)

    return pl.pallas_call(
        body, name=name,
        grid_spec=pltpu.PrefetchScalarGridSpec(
            num_scalar_prefetch=1, grid=(rows // tile,),
            in_specs=[pl.BlockSpec((None, tile, width), lambda i, k_ref: (k_ref[0], i, 0)),
                      pl.BlockSpec((3, tile, width), lambda i, k_ref: (0, i, 0))],
            out_specs=pl.BlockSpec((tile, width), lambda i, k_ref: (i, 0))),
        out_shape=jax.ShapeDtypeStruct((rows, width), F32),
        compiler_params=_params("arbitrary"))(k.reshape(1), ps, others)


def _mm_in_gathering(x, g1, prologue, in_wire, chip, tm):
    m, k = x.shape
    half, ns = in_wire.shape[1:]
    nrow = m // tm

    def flip(j):
        return jnp.where(j == 1, 2, jnp.where(j == 2, 1, j))

    def body(k_ref, x_ref, g_ref, wire_ref, proj_ref, u_ref, all_ref, kept, b_ref, load_sems, send_sems, recv_sems):
        j, i = pl.program_id(0), pl.program_id(1)
        px, py, c, chips = _place()
        sibling = (px, py, 1 - c)

        def over_ici(r, chip_slot):
            cx, cy = chips[r]
            return _remote(wire_ref.at[c], all_ref.at[chip_slot, c], send_sems, recv_sems, r, (cx, cy, c))

        def to_sibling(r, half_slot):
            cx, cy = chips[r]
            return _remote(all_ref.at[2 * cx + cy, c], all_ref.at[2 * cx + cy, half_slot], send_sems, recv_sems,
                           3 + r, sibling)

        def loads(src, slot):
            return [pltpu.make_async_copy(src.at[h], b_ref.at[slot, pl.ds(h * half, half)], load_sems.at[h])
                    for h in range(2)]

        def shard(r):
            cx, cy = chips[r]
            over_ici(r, 2 * cx + cy).wait_recv()
            if r == 0:
                over_ici(2, 2 * px + py).start()
            to_sibling(r, c).start()
            to_sibling(r, 1 - c).wait_recv()
            return all_ref.at[2 * cx + cy]

        @pl.when((j == 0) & (i == 0))
        def _():
            for r in range(2):
                over_ici(r, 2 * px + py).start()
            for cp in loads(wire_ref, 0):
                cp.start()
            for cp in loads(wire_ref, 0):
                cp.wait()

        @pl.when((j == 1) & (i == 0))
        def _():
            cps = loads(shard(0), 1)
            for cp in cps:
                cp.start()
            for cp in cps:
                cp.wait()

        for nxt in (2, 3):
            @pl.when((j == nxt - 1) & (i == nrow // 2))
            def _(nxt=nxt):
                for cp in loads(shard(nxt - 1), nxt % 2):
                    cp.start()

            @pl.when((j == nxt) & (i == 0))
            def _(nxt=nxt):
                for cp in loads(wire_ref, nxt % 2):
                    cp.wait()

        rows = pl.ds(pl.multiple_of(i * tm, tm), tm)

        @pl.when(j == 0)
        def _():
            tile = _mx(prologue(x_ref[...], g_ref[...]))
            kept[rows, :] = tile
            u_ref[...] = tile

        proj_ref[...] = _dot(kept[rows, :], b_ref[lax.rem(j, 2)])

        @pl.when((j == N_CHIPS - 1) & (i == nrow - 1))
        def _():
            for r in range(3):
                over_ici(r, 2 * px + py).wait_send()
                to_sibling(r, c).wait_send()

    once = lambda j, i, k_ref: (jnp.where(j == 0, i, nrow - 1), 0)
    return pl.pallas_call(
        body, name="mm_in",
        grid_spec=pltpu.PrefetchScalarGridSpec(
            num_scalar_prefetch=1, grid=(N_CHIPS, nrow),
            in_specs=[pl.BlockSpec((tm, k), once), pl.BlockSpec(g1.shape, lambda j, i, k_ref: (0, 0)), _HBM],
            out_specs=[pl.BlockSpec((tm, ns), lambda j, i, k_ref: (i, jnp.bitwise_xor(k_ref[0], flip(j)))),
                       pl.BlockSpec((tm, k), once), _HBM],
            scratch_shapes=[pltpu.VMEM((m, k), MXU_DTYPE), pltpu.VMEM((2, 2 * half, ns), in_wire.dtype),
                            pltpu.SemaphoreType.DMA((2,)), pltpu.SemaphoreType.DMA((6,)),
                            pltpu.SemaphoreType.DMA((6,))]),
        out_shape=[jax.ShapeDtypeStruct((m, N_CHIPS * ns), F32), jax.ShapeDtypeStruct((m, k), MXU_DTYPE),
                   jax.ShapeDtypeStruct((N_CHIPS,) + in_wire.shape, in_wire.dtype)],
        compiler_params=_params("arbitrary", "arbitrary"))(chip.reshape(1), x, g1, in_wire)


class _StepComm:
    TILES = {"in": IN_TILE, "rest": REST_TILE}

    def __init__(self, in_wire, rest_wire, chip, core):
        self.in_wire, self.rest_wire, self.chip, self.core = in_wire, rest_wire, chip, core
        self.sums, self.landed = {}, {}

    def input_projection(self, x, g1, prologue, tm):
        proj, u, shards = _mm_in_gathering(x, g1, prologue, self.in_wire, self.chip, tm)
        shards = lax.dynamic_update_slice(shards, self.in_wire[None], (self.chip, 0, 0, 0))
        return proj, u, shards.reshape(N_CHIPS, D_MODEL, IN_SHARD)

    def gather_rest(self):
        wire = self.rest_wire

        def sends(ins, outs, send_sems, recv_sems):
            (w_ref,), (out_ref,) = ins, outs
            x, y, c, chips = _place()
            return [_remote(w_ref.at[c], out_ref.at[2 * x + y, c], send_sems, recv_sems, 4 * j + 2 * c + to,
                            (cx, cy, to)) for j, (cx, cy) in enumerate(chips) for to in (0, 1)]

        def recvs(ins, outs, send_sems, recv_sems):
            (w_ref,), (out_ref,) = ins, outs
            _, _, c, chips = _place()
            return [_remote(w_ref.at[c], out_ref.at[2 * cx + cy, by], send_sems, recv_sems, 4 * j + 2 * by + c,
                            (cx, cy, by)) for j, (cx, cy) in enumerate(chips) for by in (0, 1)]

        def start(*refs):
            for cp in sends(*refs):
                cp.start()

        def wait(*refs):
            for cp in recvs(*refs):
                cp.wait_recv()
            for cp in sends(*refs):
                cp.wait_send()

        return _Riding((wire,), (jax.ShapeDtypeStruct((N_CHIPS,) + wire.shape, wire.dtype),), 12, start, wait)

    def rest_weights(self, landed):
        full = lax.dynamic_update_slice(landed, self.rest_wire[None], (self.chip, 0, 0, 0))
        return _unpack_rest_full(full.reshape(N_CHIPS, REST_ROWS, PACK_W))

    def swap(self, pg):
        def copies(ins, outs, send_sems, recv_sems):
            (pg_ref,), (out_ref,) = ins, outs
            x, y, c, _ = _place()
            return [_remote(pg_ref.at[j, 1 - c], out_ref.at[j], send_sems, recv_sems, j, (x, y, 1 - c))
                    for j in range(N_CHIPS)]

        def start(*refs):
            for cp in copies(*refs):
                cp.start()

        def wait(*refs):
            for cp in copies(*refs):
                cp.wait()

        return _Riding((pg,), (jax.ShapeDtypeStruct((N_CHIPS,) + pg.shape[2:], pg.dtype),), N_CHIPS, start, wait)

    def scatter(self, group, pg, theirs=None):
        if theirs is None:
            (theirs,) = _swap_halves([pg], "exchange_halves_" + group)
        ps = _pair_sum("sum_pair_" + group, pg, theirs, self.core, self.TILES[group])
        self.sums[group] = ps

        def copies(ins, outs, send_sems, recv_sems):
            (ps_ref,), (out_ref,) = ins, outs
            _, _, c, chips = _place()
            return [_remote(ps_ref.at[2 * cx + cy], out_ref.at[j], send_sems, recv_sems, j, (cx, cy, c))
                    for j, (cx, cy) in enumerate(chips)]

        def start(*refs):
            for cp in copies(*refs):
                cp.start()

        def wait(*refs):
            for cp in copies(*refs):
                cp.wait()

        return _Riding((ps,), (jax.ShapeDtypeStruct((3,) + ps.shape[1:], ps.dtype),), 3, start, wait)

    def reduced(self, group):
        return _chip_sum("sum_chips_" + group, self.sums[group], self.landed[group], self.chip, self.TILES[group])


def _adamw(w, g, m, v):
    m = ADAM_B1 * m + (1.0 - ADAM_B1) * g
    v = ADAM_B2 * v + (1.0 - ADAM_B2) * (g * g)
    m_hat = m / (1.0 - ADAM_B1 ** ADAM_STEP)
    v_hat = v / (1.0 - ADAM_B2 ** ADAM_STEP)
    return -ADAM_LR * (m_hat / (jnp.sqrt(v_hat) + ADAM_EPS) + ADAM_WD * w), m, v


def _small_reduce_adamw(part, w, m, v):
    def body(part_ref, w_ref, m_ref, v_ref, g_ref, d_ref, nm_ref, nv_ref, all_ref, send_sems, recv_sems):
        x, y, c, chips = _place()
        me, sibling = (x, y, c), (x, y, 1 - c)

        def rows(px, py, pc):
            return all_ref.at[4 * px + 2 * py + pc]

        all_ref[4 * x + 2 * y + c] = part_ref[...]
        first = [_remote(part_ref, rows(*me), send_sems, recv_sems, 0, sibling)]
        first += [_remote(part_ref, rows(*me), send_sems, recv_sems, 1 + j, (cx, cy, c))
                  for j, (cx, cy) in enumerate(chips)]
        for cp in first:
            cp.start()
        passed = []
        for j, (cx, cy) in enumerate(chips):
            _remote(part_ref, rows(cx, cy, c), send_sems, recv_sems, 1 + j, me).wait_recv()
            cp = _remote(rows(cx, cy, c), rows(cx, cy, c), send_sems, recv_sems, 4 + j, sibling)
            cp.start()
            passed.append(cp)
        _remote(part_ref, rows(*sibling), send_sems, recv_sems, 0, me).wait_recv()
        for j, (cx, cy) in enumerate(chips):
            _remote(part_ref, rows(cx, cy, 1 - c), send_sems, recv_sems, 4 + j, me).wait_recv()
        for cp in first + passed:
            cp.wait_send()
        g = all_ref[0]
        for dev in range(1, N_DEV):
            g = g + all_ref[dev]
        delta, nm, nv = _adamw(w_ref[...], g, m_ref[...], v_ref[...])
        g_ref[...] = g
        d_ref[...] = delta
        nm_ref[...] = nm
        nv_ref[...] = nv

    whole = pl.BlockSpec(memory_space=pltpu.VMEM)
    shape = jax.ShapeDtypeStruct((SMALL_ROWS, PACK_W), F32)
    return pl.pallas_call(
        body, name="small_reduce_adamw", in_specs=[whole] * 4, out_specs=[whole] * 4, out_shape=[shape] * 4,
        scratch_shapes=[pltpu.VMEM((N_DEV, SMALL_ROWS, PACK_W), F32), pltpu.SemaphoreType.DMA((7,)),
                        pltpu.SemaphoreType.DMA((7,))],
        compiler_params=pltpu.CompilerParams(vmem_limit_bytes=VMEM_LIMIT))(part, w, m, v)


def kernel(x, p, norm_g, w_in, hg_lb, hg_norm_g, w_o_hg, s5_a_re, s5_a_im, s5_log_dt, s5_b_re, s5_b_im, s5_c_re, s5_c_im, s5_d, w_glu, b_glu, w_o_s5, w_out, ple_norm_g, w_ple, w_ple_gate, final_norm_g, loss_target, m_norm_g, m_w_in, m_hg_lb, m_hg_norm_g, m_w_o_hg, m_s5_a_re, m_s5_a_im, m_s5_log_dt, m_s5_b_re, m_s5_b_im, m_s5_c_re, m_s5_c_im, m_s5_d, m_w_glu, m_b_glu, m_w_o_s5, m_w_out, m_ple_norm_g, m_w_ple, m_w_ple_gate, m_final_norm_g, v_norm_g, v_w_in, v_hg_lb, v_hg_norm_g, v_w_o_hg, v_s5_a_re, v_s5_a_im, v_s5_log_dt, v_s5_b_re, v_s5_b_im, v_s5_c_re, v_s5_c_im, v_s5_d, v_w_glu, v_b_glu, v_w_o_s5, v_w_out, v_ple_norm_g, v_w_ple, v_w_ple_gate, v_final_norm_g):
    given = dict(locals())
    wts = {n: given[n] for n in WEIGHTS}
    mom = {n: given["m_" + n] for n in WEIGHTS}
    var = {n: given["v_" + n] for n in WEIGHTS}
    cx, cy, cc = lax.axis_index("x"), lax.axis_index("y"), lax.axis_index("c")
    chip = (2 * cx + cy).astype(jnp.int32)

    core = cc.astype(jnp.int32)
    rest_shard = _pack_rest({n: wts[n][0] for n in REST})
    comm = _StepComm(wts["w_in"][0].astype(MXU_DTYPE).reshape(2, D_MODEL // 2, IN_SHARD),
                     rest_shard.astype(MXU_DTYPE).reshape(2, REST_ROWS // 2, PACK_W), chip, core)

    t_len = x.shape[1]
    loss_row, grad_x, g_big, g_small = _local_step(x.reshape(t_len, D_MODEL), p.reshape(t_len, -1),
                                                   loss_target.reshape(t_len, D_MODEL), None,
                                                   {n: wts[n] for n in SMALL}, comm)

    zero = jnp.zeros((), F32)
    sg, sd, snm, snv = _small_reduce_adamw(_pack_small(g_small, loss_row[0, 0]),
                                           _pack_small({n: wts[n] for n in SMALL}, zero),
                                           _pack_small({n: mom[n] for n in SMALL}, zero),
                                           _pack_small({n: var[n] for n in SMALL}, zero))
    (sg, loss), (sd, _), (snm, _), (snv, _) = (_unpack_small(a) for a in (sg, sd, snm, snv))

    halves = [comm.reduced("in"), comm.reduced("rest")]
    g_in, g_rest = [lax.dynamic_update_slice(got, mine[None], (core, 0, 0))
                    for got, mine in zip(_share_halves(halves), halves)]
    g_in, g_rest = g_in.reshape(D_MODEL, IN_SHARD), g_rest.reshape(REST_ROWS, PACK_W)

    def adam_f(wv, gv, mv, vv):
        return _adamw(wv, gv, mv, vv)

    d_in, nm_in, nv_in = _rowwise("adamw_in", adam_f, D_MODEL, IN_TILE,
                                  [(wts["w_in"][0], IN_SHARD, 0), (g_in, IN_SHARD, 0), (mom["w_in"][0], IN_SHARD, 0),
                                   (var["w_in"][0], IN_SHARD, 0)], [], [(IN_SHARD, F32)] * 3)
    d_rest, nm_rest, nv_rest = _rowwise("adamw_rest", adam_f, REST_ROWS, REST_TILE,
                                        [(rest_shard, PACK_W, 0), (g_rest, PACK_W, 0),
                                         (_pack_rest({n: mom[n][0] for n in REST}), PACK_W, 0),
                                         (_pack_rest({n: var[n][0] for n in REST}), PACK_W, 0)], [],
                                        [(PACK_W, F32)] * 3)
    bg, bd, bnm, bnv = (dict(_unpack_rest(rest), w_in=a.reshape(1, D_MODEL, IN_SHARD))
                        for rest, a in ((g_rest, g_in), (d_rest, d_in), (nm_rest, nm_in), (nv_rest, nv_in)))

    outs = [loss, grad_x.reshape(x.shape)]
    for small, big in ((sg, bg), (sd, bd), (snm, bnm), (snv, bnv)):
        outs += [big[n] if n in BIG else small[n] for n in WEIGHTS]
    return tuple(outs)
```

```python
import functools
from typing import Callable, NamedTuple

import jax
import jax.numpy as jnp
from jax import lax
from jax.experimental import pallas as pl
from jax.experimental.pallas import tpu as pltpu

F32 = jnp.float32
MXU_DTYPE = jnp.bfloat16
WIRE_DTYPE = jnp.bfloat16
NORM_EPS = 1e-6
D_MODEL = 1024
HG_HEADS = 8
HG_DIM = 128
HG_CHUNK = 64
S5_WIDTH = 512
S5_GROUPS = 32
S5_GROUP = 16
S5_STATE = 64
S5_LANES = S5_GROUPS * S5_STATE
IN_COLS = 7168
SUBLANES = 8
VMEM_LIMIT = 56 * 1024 * 1024
HIGHEST = lax.Precision.HIGHEST
MESH = pl.DeviceIdType.MESH

ADAM_LR, ADAM_B1, ADAM_B2, ADAM_EPS, ADAM_WD, ADAM_STEP = 0.001, 0.9, 0.999, 1e-08, 0.01, 10

BIG = ("w_in", "w_o_hg", "w_glu", "w_o_s5", "w_out", "w_ple", "w_ple_gate")
BIG_SHAPE = {"w_in": (1024, 7168), "w_o_hg": (1024, 1024), "w_glu": (512, 1024), "w_o_s5": (512, 1024),
             "w_out": (1024, 1024), "w_ple": (256, 1024), "w_ple_gate": (1024, 1024)}
BIG_COL_SHARDED = ("w_in", "w_glu", "w_o_s5", "w_ple")
SMALL = ("norm_g", "hg_lb", "hg_norm_g", "s5_a_re", "s5_a_im", "s5_log_dt", "s5_b_re", "s5_b_im", "s5_c_re",
         "s5_c_im", "s5_d", "b_glu", "ple_norm_g", "final_norm_g")
SMALL_SHAPE = {"norm_g": (1, 1024), "hg_lb": (2, 1024), "hg_norm_g": (1, 1024), "s5_a_re": (1, 32, 64),
               "s5_a_im": (1, 32, 64), "s5_log_dt": (1, 32), "s5_b_re": (1, 32, 64, 16), "s5_b_im": (1, 32, 64, 16),
               "s5_c_re": (1, 32, 16, 64), "s5_c_im": (1, 32, 16, 64), "s5_d": (1, 32, 16), "b_glu": (1, 1024),
               "ple_norm_g": (1, 1024), "final_norm_g": (1024,)}
WEIGHTS = ("norm_g", "w_in", "hg_lb", "hg_norm_g", "w_o_hg", "s5_a_re", "s5_a_im", "s5_log_dt", "s5_b_re", "s5_b_im",
           "s5_c_re", "s5_c_im", "s5_d", "w_glu", "b_glu", "w_o_s5", "w_out", "ple_norm_g", "w_ple", "w_ple_gate",
           "final_norm_g")
N_CHIPS = 4
N_DEV = 8
PACK_W = 1024
SMALL_ROWS = 144


def _params(*sem):
    return pltpu.CompilerParams(dimension_semantics=sem, vmem_limit_bytes=VMEM_LIMIT)


def _sig(x):
    return 1.0 / (1.0 + jnp.exp(-x))


def _dsilu(z, s):
    return s * (1.0 + z * (1.0 - s))


def _mx(x):
    return x.astype(MXU_DTYPE)


def _dot(a, b, dims=(((1,), (0,)), ((), ()))):
    return lax.dot_general(_mx(a), _mx(b), dims, preferred_element_type=F32)


_NT = (((1,), (1,)), ((), ()))
_TN = (((0,), (0,)), ((), ()))


def _dot32(a, b):
    return jnp.dot(a, b, precision=HIGHEST, preferred_element_type=F32)


def _rms_bwd(dy, x, g):
    r = lax.rsqrt(jnp.mean(x * x, axis=-1, keepdims=True) + NORM_EPS)
    t = dy * g
    dx = r * t - x * (r * r * r) * jnp.mean(t * x, axis=-1, keepdims=True)
    return dx, jnp.sum(dy * x * r, axis=0, keepdims=True)


def _rowwise(name, fn, n_rows_total, tm, rows, consts, outs, accs=(), alias=None):
    n_r, n_c, n_o, n_a = len(rows), len(consts), len(outs), len(accs)

    def body(*refs):
        row_refs = refs[:n_r]
        const_refs = refs[n_r:n_r + n_c]
        pos = n_r + n_c + (1 if alias is not None else 0)
        out_refs = refs[pos:pos + n_o]
        acc_refs = refs[pos + n_o:pos + n_o + n_a]
        res = fn(*[r[...] for r in row_refs], *[r[...] for r in const_refs])
        for r, v in zip(out_refs, res[:n_o]):
            r[...] = v.astype(r.dtype)
        if n_a:
            @pl.when(pl.program_id(0) == 0)
            def _():
                for r in acc_refs:
                    r[...] = jnp.zeros_like(r)
            for r, v in zip(acc_refs, res[n_o:]):
                r[...] += v

    in_specs = [pl.BlockSpec((tm, w), functools.partial(lambda i, cb: (i, cb), cb=cb)) for (_, w, cb) in rows]
    in_specs += [pl.BlockSpec(c.shape, lambda i: (0, 0)) for c in consts]
    args = [a for (a, _, _) in rows] + list(consts)
    out_shape, out_specs = [], []
    for o in outs:
        w, dt = o[0], o[1]
        cb, total = (o[2], o[3]) if len(o) == 4 else (0, w)
        out_shape.append(jax.ShapeDtypeStruct((n_rows_total, total), dt))
        out_specs.append(pl.BlockSpec((tm, w), functools.partial(lambda i, cb: (i, cb), cb=cb)))
    io_alias = {}
    if alias is not None:
        in_specs.append(pl.BlockSpec(memory_space=pl.ANY))
        args.append(alias[0])
        io_alias = {len(args) - 1: alias[1]}
    for (r, w) in accs:
        out_shape.append(jax.ShapeDtypeStruct((r, w), F32))
        out_specs.append(pl.BlockSpec((r, w), lambda i: (0, 0)))
    res = pl.pallas_call(body, name=name, grid=(n_rows_total // tm,), in_specs=in_specs, out_specs=out_specs,
                         out_shape=out_shape, input_output_aliases=io_alias,
                         compiler_params=_params("arbitrary"))(*args)
    return res


class _Riding(NamedTuple):
    ins: tuple
    outs: tuple
    n_sems: int
    start: Callable
    wait: Callable


_HBM = pl.BlockSpec(memory_space=pl.ANY)


def _ride(riding, refs, n_in, n_out, n_scratch, first, last):
    if riding is None:
        return refs[:n_in], refs[n_in:n_in + n_out], refs[n_in + n_out:]
    r_in, r_out = len(riding.ins), len(riding.outs)
    ins, rins = refs[:n_in], refs[n_in:n_in + r_in]
    pos = n_in + r_in
    outs, routs = refs[pos:pos + n_out], refs[pos + n_out:pos + n_out + r_out]
    pos += n_out + r_out
    scratch, (send_sems, recv_sems) = refs[pos:pos + n_scratch], refs[pos + n_scratch:]

    @pl.when(first)
    def _():
        riding.start(rins, routs, send_sems, recv_sems)

    @pl.when(last)
    def _():
        riding.wait(rins, routs, send_sems, recv_sems)

    return ins, outs, scratch


def _riding_call(riding, body, name, grid, in_specs, args, out_specs, out_shape, scratch, io_alias=None):
    if riding is not None:
        in_specs = list(in_specs) + [_HBM] * len(riding.ins)
        args = list(args) + list(riding.ins)
        out_specs = list(out_specs) + [_HBM] * len(riding.outs)
        out_shape = list(out_shape) + list(riding.outs)
        scratch = list(scratch) + [pltpu.SemaphoreType.DMA((riding.n_sems,))] * 2
    return pl.pallas_call(body, name=name, grid=grid, in_specs=in_specs, out_specs=out_specs, out_shape=out_shape,
                          scratch_shapes=scratch, input_output_aliases=io_alias or {},
                          compiler_params=_params(*(["arbitrary"] * len(grid))))(*args)


def _mm_nn(name, a, b, tm, tn, riding=None, prologue=None, consts=()):
    m, k = a.shape
    n = b.shape[1] if b.ndim == 2 else b.shape[0] * b.shape[2]
    grid = (n // tn, m // tm)
    n_out, scratch = (1, []) if prologue is None else (2, [pltpu.VMEM((m, k), MXU_DTYPE)])

    def body(*refs):
        j, i = pl.program_id(0), pl.program_id(1)
        ins, outs, kept = _ride(riding, refs, 2 + len(consts), n_out, len(scratch), (j == 0) & (i == 0),
                                (j == grid[0] - 1) & (i == grid[1] - 1))
        if prologue is None:
            left = ins[0][...]
        else:
            rows = pl.ds(pl.multiple_of(i * tm, tm), tm)

            @pl.when(j == 0)
            def _():
                tile = _mx(prologue(ins[0][...], *[c[...] for c in ins[2:]]))
                kept[0][rows, :] = tile
                outs[1][...] = tile

            left = kept[0][rows, :]
        outs[0][...] = _dot(left, ins[1][...])

    once = (lambda j, i: (i, 0)) if prologue is None else (lambda j, i: (jnp.where(j == 0, i, grid[1] - 1), 0))
    b_spec = (pl.BlockSpec((k, tn), lambda j, i: (0, j)) if b.ndim == 2
              else pl.BlockSpec((None, k, tn), lambda j, i: (j, 0, 0)))
    in_specs = [pl.BlockSpec((tm, k), once), b_spec]
    in_specs += [pl.BlockSpec(c.shape, lambda j, i: (0, 0)) for c in consts]
    out_specs = [pl.BlockSpec((tm, tn), lambda j, i: (i, j))]
    out_shape = [jax.ShapeDtypeStruct((m, n), F32)]
    if prologue is not None:
        out_specs.append(pl.BlockSpec((tm, k), once))
        out_shape.append(jax.ShapeDtypeStruct((m, k), MXU_DTYPE))
    res = _riding_call(riding, body, name, grid, in_specs, [a, b] + list(consts), out_specs, out_shape, scratch)
    return res[0] if riding is None and prologue is None else res


def _mm_nt_then(name, a, b, tm, tn, fn, rows, consts, outs, accs=(), alias=None, riding=None):
    m, n = a.shape
    k = b.shape[-2]
    steps = n // tn
    n_r, n_c, n_o, n_a = len(rows), len(consts), len(outs), len(accs)

    def body(*refs):
        a_ref, b_ref = refs[:2]
        row_refs = refs[2:2 + n_r]
        const_refs = refs[2 + n_r:2 + n_r + n_c]
        i, s = pl.program_id(0), pl.program_id(1)
        n_in = 2 + n_r + n_c + (1 if alias is not None else 0)
        _, outs_, (mm_ref,) = _ride(riding, refs, n_in, n_o + n_a, 1, (i == 0) & (s == 0),
                                    (i == m // tm - 1) & (s == steps - 1))
        out_refs, acc_refs = outs_[:n_o], outs_[n_o:]
        part = _dot(a_ref[...], b_ref[...] if b.ndim == 2 else b_ref[s], _NT)
        if steps > 1:
            @pl.when(s == 0)
            def _():
                mm_ref[...] = jnp.zeros_like(mm_ref)
            mm_ref[...] += part

        @pl.when(s == steps - 1)
        def _():
            res = fn(mm_ref[...] if steps > 1 else part, *[r[...] for r in row_refs], *[r[...] for r in const_refs])
            for r, v in zip(out_refs, res[:n_o]):
                r[...] = v.astype(r.dtype)
            if n_a:
                @pl.when(i == 0)
                def _():
                    for r in acc_refs:
                        r[...] = jnp.zeros_like(r)
                for r, v in zip(acc_refs, res[n_o:]):
                    r[...] += v

    b_spec = (pl.BlockSpec((k, tn), lambda i, s: (0, s)) if b.ndim == 2
              else pl.BlockSpec(memory_space=pltpu.VMEM))
    in_specs = [pl.BlockSpec((tm, tn), lambda i, s: (i, s)), b_spec]
    in_specs += [pl.BlockSpec((tm, w), functools.partial(lambda i, s, cb: (i, cb), cb=cb)) for (_, w, cb) in rows]
    in_specs += [pl.BlockSpec(c.shape, lambda i, s: (0, 0)) for c in consts]
    args = [a, b] + [r[0] for r in rows] + list(consts)
    out_shape, out_specs = [], []
    for o in outs:
        w, dt = o[0], o[1]
        cb, total = (o[2], o[3]) if len(o) == 4 else (0, w)
        out_shape.append(jax.ShapeDtypeStruct((m, total), dt))
        out_specs.append(pl.BlockSpec((tm, w), functools.partial(lambda i, s, cb: (i, cb), cb=cb)))
    io_alias = {}
    if alias is not None:
        in_specs.append(pl.BlockSpec(memory_space=pl.ANY))
        args.append(alias[0])
        io_alias = {len(args) - 1: alias[1]}
    for (r, w) in accs:
        out_shape.append(jax.ShapeDtypeStruct((r, w), F32))
        out_specs.append(pl.BlockSpec((r, w), lambda i, s: (0, 0)))
    return _riding_call(riding, body, name, (m // tm, steps), in_specs, args, out_specs, out_shape,
                        [pltpu.VMEM((tm, k), F32)], io_alias)


def _mm_tn(name, a, b, tk, tn, col_shards=False, riding=None):
    t, k = a.shape
    n = b.shape[1]
    steps = t // tk

    def body(*refs):
        j, s = pl.program_id(0), pl.program_id(1)
        (a_ref, b_ref), (o_ref,), (acc_ref,) = _ride(riding, refs, 2, 1, 1, (j == 0) & (s == 0),
                                                     (j == n // tn - 1) & (s == steps - 1))

        @pl.when(s == 0)
        def _():
            acc_ref[...] = jnp.zeros_like(acc_ref)

        acc_ref[...] += _dot(a_ref[...], b_ref[...], _TN)

        @pl.when(s == steps - 1)
        def _():
            o_ref[...] = acc_ref[...]

    if col_shards:
        out_spec = pl.BlockSpec((None, k, tn), lambda j, s: (j, 0, 0))
        out_shape = jax.ShapeDtypeStruct((n // tn, k, tn), F32)
    else:
        out_spec = pl.BlockSpec((k, tn), lambda j, s: (0, j))
        out_shape = jax.ShapeDtypeStruct((k, n), F32)
    res = _riding_call(riding, body, name, (n // tn, steps),
                       [pl.BlockSpec((tk, k), lambda j, s: (s, 0)), pl.BlockSpec((tk, tn), lambda j, s: (s, j))],
                       [a, b], [out_spec], [out_shape], [pltpu.VMEM((k, tn), F32)])
    return res[0] if riding is None else res


def _dot01(m01, x):
    m = m01.astype(MXU_DTYPE)
    hi = x.astype(MXU_DTYPE)
    r1 = x - hi.astype(F32)
    mid = r1.astype(MXU_DTYPE)
    lo = (r1 - mid.astype(F32)).astype(MXU_DTYPE)
    dot = lambda v: jnp.dot(m, v, preferred_element_type=F32)
    return dot(hi) + dot(mid) + dot(lo)


def _chunk_rows(x, offset, nck):
    return jnp.concatenate([jnp.broadcast_to(x[c * HG_CHUNK + offset:c * HG_CHUNK + offset + 1, :],
                                             (HG_CHUNK, x.shape[1])) for c in range(nck)], axis=0)


def _hg_block_terms(q, f, lb, tb):
    nck = tb // HG_CHUNK
    sig = _sig(f)
    fv = lb + (1.0 - lb) * sig
    kk = (1.0 - lb) * (1.0 - sig)
    row = lax.broadcasted_iota(jnp.int32, (tb, tb), 0)
    col = lax.broadcasted_iota(jnp.int32, (tb, tb), 1)
    same = jnp.right_shift(row, 6) == jnp.right_shift(col, 6)
    causal, anti = same & (row >= col), same & (row <= col)
    b = _dot01(causal, jnp.log(fv))
    b_mid, b_last = _chunk_rows(b, HG_CHUNK // 2 - 1, nck), _chunk_rows(b, HG_CHUNK - 1, nck)
    e_mid, e_mid_inv = jnp.exp(b - b_mid), jnp.exp(b_mid - b)
    e_b, e_last = jnp.exp(b), jnp.exp(b_last - b)
    dcs = [jnp.exp(b[c * HG_CHUNK + HG_CHUNK - 1:(c + 1) * HG_CHUNK, :]) for c in range(nck)]
    return sig, fv, kk, causal, anti, e_mid, e_mid_inv, e_b, e_last, dcs


def _hgrn2_fwd(proj, hg_lb, hg_norm_g, t_len, tb, riding=None):
    nck = tb // HG_CHUNK
    nb = t_len // tb

    def body(*refs):
        step = pl.program_id(0)
        ((p_ref, lb_ref, gn_ref), (o_ref, act_ref, sp_ref),
         (st_ref, a_s, bm_s, qd_s, kd_s, v_s, sc_s, inc_s)) = _ride(riding, refs, 3, 3, 8, step == 0, step == nb - 1)

        @pl.when(pl.program_id(0) == 0)
        def _():
            st_ref[...] = jnp.zeros_like(st_ref)

        lb = _sig(lb_ref[0:1, :] - lb_ref[1:2, :])
        q = p_ref[:, pl.ds(0, 1024)]
        _, _, kk, causal, _, e_mid, e_mid_inv, e_b, e_last, dcs = _hg_block_terms(q, p_ref[:, pl.ds(1024, 1024)],
                                                                                   lb, tb)
        a_s[...] = _mx(q * e_mid)
        bm_s[...] = _mx(kk * e_mid_inv)
        qd_s[...] = _mx(q * e_b)
        kd_s[...] = _mx(kk * e_last)
        v_s[...] = _mx(p_ref[:, pl.ds(2048, 1024)])
        heads = [pl.ds(h * HG_DIM, HG_DIM) for h in range(HG_HEADS)]
        chunks = [pl.ds(c * HG_CHUNK, HG_CHUNK) for c in range(nck)]
        for h, hs in enumerate(heads):
            sc_s[h] = _mx(jnp.where(causal, _dot(a_s[:, hs], bm_s[:, hs], _NT), 0.0))
        for h, hs in enumerate(heads):
            o_ref[:, hs] = _dot(sc_s[h], v_s[:, hs])
        for h, hs in enumerate(heads):
            for c, r in enumerate(chunks):
                inc_s[h, c] = _dot(v_s[r, hs], kd_s[r, hs], _TN)
        for c in range(nck):
            for h in range(HG_HEADS):
                st = st_ref[h]
                sp_ref[h, c] = st
                st_ref[h] = dcs[c][:, h * HG_DIM:(h + 1) * HG_DIM] * st + inc_s[h, c]
        for c, r in enumerate(chunks):
            for h, hs in enumerate(heads):
                o_ref[r, hs] += _dot(qd_s[r, hs], sp_ref[h, c], _NT)
        for h, hs in enumerate(heads):
            o = o_ref[:, hs]
            rr = lax.rsqrt(jnp.mean(o * o, axis=-1, keepdims=True) + NORM_EPS)
            g = p_ref[:, pl.ds(3072 + h * HG_DIM, HG_DIM)]
            act_ref[:, hs] = (o * rr * gn_ref[:, hs] * (g * _sig(g))).astype(act_ref.dtype)

    return _riding_call(
        riding, body, "hgrn2_fwd", (nb,),
        [pl.BlockSpec((tb, 4096), lambda i: (i, 0)), pl.BlockSpec((2, 1024), lambda i: (0, 0)),
         pl.BlockSpec((1, 1024), lambda i: (0, 0))],
        [proj, hg_lb, hg_norm_g],
        [pl.BlockSpec((tb, 1024), lambda i: (i, 0)), pl.BlockSpec((tb, 1024), lambda i: (i, 0)),
         pl.BlockSpec((HG_HEADS, nck, HG_DIM, HG_DIM), lambda i: (0, i, 0, 0))],
        [jax.ShapeDtypeStruct((t_len, 1024), F32), jax.ShapeDtypeStruct((t_len, 1024), MXU_DTYPE),
         jax.ShapeDtypeStruct((HG_HEADS, t_len // HG_CHUNK, HG_DIM, HG_DIM), F32)],
        [pltpu.VMEM((HG_HEADS, HG_DIM, HG_DIM), F32)] + [pltpu.VMEM((tb, 1024), MXU_DTYPE)] * 5
        + [pltpu.VMEM((HG_HEADS, tb, tb), MXU_DTYPE), pltpu.VMEM((HG_HEADS, nck, HG_DIM, HG_DIM), F32)])


def _hgrn2_bwd(proj, d_o, s_prev, hg_lb, dproj, t_len, tb, riding=None):
    nck = tb // HG_CHUNK
    nb = t_len // tb

    def body(*refs):
        step = pl.program_id(0)
        ((p_ref, do_ref, sp_ref, lb_ref, _), (dp_ref, dlb_ref),
         (ds_ref, acc_ref, a_s, bm_s, qd_s, kd_s, v_s, do_s, da_s, dbm_s, dqd_s, dkd_s, dv_s, ex_s, sc_s, dsc_s,
          up_s)) = _ride(riding, refs, 5, 2, 17, step == 0, step == nb - 1)

        @pl.when(pl.program_id(0) == 0)
        def _():
            ds_ref[...] = jnp.zeros_like(ds_ref)
            acc_ref[...] = jnp.zeros_like(acc_ref)

        lb = _sig(lb_ref[0:1, :] - lb_ref[1:2, :])
        q = p_ref[:, pl.ds(0, 1024)]
        sig, fv, kk, causal, anti, e_mid, e_mid_inv, e_b, e_last, dcs = _hg_block_terms(
            q, p_ref[:, pl.ds(1024, 1024)], lb, tb)
        a, bm, qd, kd = q * e_mid, kk * e_mid_inv, q * e_b, kk * e_last
        a_s[...] = _mx(a)
        bm_s[...] = _mx(bm)
        qd_s[...] = _mx(qd)
        kd_s[...] = _mx(kd)
        v_s[...] = _mx(p_ref[:, pl.ds(2048, 1024)])
        do_s[...] = _mx(do_ref[...])
        heads = [pl.ds(h * HG_DIM, HG_DIM) for h in range(HG_HEADS)]
        chunks = [pl.ds(c * HG_CHUNK, HG_CHUNK) for c in range(nck)]
        for h, hs in enumerate(heads):
            sc_s[h] = _mx(jnp.where(causal, _dot(a_s[:, hs], bm_s[:, hs], _NT), 0.0))
            dsc_s[h] = _mx(jnp.where(causal, _dot(do_s[:, hs], v_s[:, hs], _NT), 0.0))
        for h, hs in enumerate(heads):
            dv_s[:, hs] = _dot(sc_s[h], do_s[:, hs], _TN)
            da_s[:, hs] = _dot(dsc_s[h], bm_s[:, hs])
            dbm_s[:, hs] = _dot(dsc_s[h], a_s[:, hs], _TN)
        for h, hs in enumerate(heads):
            for c, r in enumerate(chunks):
                up_s[h, c] = _dot(do_s[r, hs], qd_s[r, hs], _TN)
                dqd_s[r, hs] = _dot(do_s[r, hs], sp_ref[h, c])
        for c in reversed(range(nck)):
            r = chunks[c]
            for h, hs in enumerate(heads):
                dst = ds_ref[h]
                dc = dcs[c][:, h * HG_DIM:(h + 1) * HG_DIM]
                dv_s[r, hs] += _dot(kd_s[r, hs], dst, _NT)
                dkd_s[r, hs] = _dot(v_s[r, hs], dst)
                ex_s[c:c + 1, hs] = jnp.sum(dst * sp_ref[h, c], axis=0, keepdims=True) * dc
                ds_ref[h] = up_s[h, c] + dc * dst
        da, dbm, dqd, dkd = da_s[...], dbm_s[...], dqd_s[...], dkd_s[...]
        dq = da * e_mid + dqd * e_b
        dk = dbm * e_mid_inv + dkd * e_last
        db = da * a - dbm * bm + dqd * qd - dkd * kd
        dkk = dkd * kd
        extra = jnp.concatenate(
            [jnp.broadcast_to(jnp.sum(dkk[c * HG_CHUNK:(c + 1) * HG_CHUNK], axis=0, keepdims=True)
                              + ex_s[c:c + 1, :], (HG_CHUNK, 1024)) for c in range(nck)], axis=0)
        dlogf = _dot01(anti, db) + extra
        dfv_k = dlogf / fv - dk
        dp_ref[:, pl.ds(0, 1024)] = dq.astype(dp_ref.dtype)
        dp_ref[:, pl.ds(1024, 1024)] = (dfv_k * (1.0 - lb) * sig * (1.0 - sig)).astype(dp_ref.dtype)
        dp_ref[:, pl.ds(2048, 1024)] = dv_s[...].astype(dp_ref.dtype)
        acc_ref[...] += jnp.sum(dfv_k * (1.0 - sig), axis=0, keepdims=True)

        @pl.when(pl.program_id(0) == nb - 1)
        def _():
            g0 = acc_ref[...] * lb * (1.0 - lb)
            dlb_ref[0:1, :] = g0
            dlb_ref[1:2, :] = -g0

    return _riding_call(
        riding, body, "hgrn2_bwd", (nb,),
        [pl.BlockSpec((tb, 3072), lambda i: (nb - 1 - i, 0)),
         pl.BlockSpec((tb, 1024), lambda i: (nb - 1 - i, 0)),
         pl.BlockSpec((HG_HEADS, nck, HG_DIM, HG_DIM), lambda i: (0, nb - 1 - i, 0, 0)),
         pl.BlockSpec((2, 1024), lambda i: (0, 0)),
         pl.BlockSpec(memory_space=pl.ANY)],
        [proj, d_o, s_prev, hg_lb, dproj],
        [pl.BlockSpec((tb, 3072), lambda i: (nb - 1 - i, 0)), pl.BlockSpec((2, 1024), lambda i: (0, 0))],
        [jax.ShapeDtypeStruct((t_len, IN_COLS), dproj.dtype), jax.ShapeDtypeStruct((2, 1024), F32)],
        [pltpu.VMEM((HG_HEADS, HG_DIM, HG_DIM), F32), pltpu.VMEM((1, 1024), F32)]
        + [pltpu.VMEM((tb, 1024), MXU_DTYPE)] * 6 + [pltpu.VMEM((tb, 1024), F32)] * 5
        + [pltpu.VMEM((SUBLANES, 1024), F32)] + [pltpu.VMEM((HG_HEADS, tb, tb), MXU_DTYPE)] * 2
        + [pltpu.VMEM((HG_HEADS, nck, HG_DIM, HG_DIM), F32)], {4: 0})


def _s5_prep_bwd(a_re, a_im, log_dt, b_re_t, b_im_t, dlam, dbbr, dbbi):
    def body(ar_ref, ai_ref, ldt_ref, br_ref, bi_ref, dlam_ref, dbbr_ref, dbbi_ref,
             dar_ref, dai_ref, dldt_ref, dbr_ref, dbi_ref):
        ar, ai = ar_ref[...], ai_ref[...]
        dt = jnp.exp(ldt_ref[...])
        mag = jnp.exp(ar * dt)
        cs, sn = jnp.cos(ai * dt), jnp.sin(ai * dt)
        lr, li = mag * cs, mag * sn
        den = ar * ar + ai * ai
        nr = lr - 1.0
        sr = (nr * ar + li * ai) / den
        si = (li * ar - nr * ai) / den
        br, bi = br_ref[...], bi_ref[...]
        gbr, gbi = dbbr_ref[...], dbbi_ref[...]
        dbr_ref[...] = sr * gbr + si * gbi
        dbi_ref[...] = sr * gbi - si * gbr
        dsr = jnp.sum(gbr * br + gbi * bi, axis=0, keepdims=True)
        dsi = jnp.sum(gbi * br - gbr * bi, axis=0, keepdims=True)
        dnr = (dsr * ar - dsi * ai) / den
        dli = dlam_ref[1:2, :] + (dsr * ai + dsi * ar) / den
        dlr = dlam_ref[0:1, :] + dnr
        dden = -(dsr * sr + dsi * si) / den
        dar = (dsr * nr + dsi * li) / den + dden * 2.0 * ar
        dai = (dsr * li - dsi * nr) / den + dden * 2.0 * ai
        dmag = dlr * cs + dli * sn
        dth = mag * (dli * cs - dlr * sn)
        dar_ref[...] = dar + dmag * mag * dt
        dai_ref[...] = dai + dth * dt
        ddt = (dmag * mag * ar + dth * ai) * dt
        lane = lax.broadcasted_iota(jnp.int32, (S5_LANES, 128), 0) // S5_STATE
        grp = lax.broadcasted_iota(jnp.int32, (S5_LANES, 128), 1)
        dldt_ref[...] = _dot32(jnp.broadcast_to(ddt, (SUBLANES, S5_LANES)), (lane == grp).astype(F32))

    whole = pl.BlockSpec(memory_space=pltpu.VMEM)
    return pl.pallas_call(
        body, name="s5_prep_bwd", in_specs=[whole] * 8, out_specs=[whole] * 5,
        out_shape=[jax.ShapeDtypeStruct((1, S5_LANES), F32), jax.ShapeDtypeStruct((1, S5_LANES), F32),
                   jax.ShapeDtypeStruct((SUBLANES, 128), F32), jax.ShapeDtypeStruct((S5_GROUP, S5_LANES), F32),
                   jax.ShapeDtypeStruct((S5_GROUP, S5_LANES), F32)])(a_re, a_im, log_dt, b_re_t, b_im_t, dlam, dbbr,
                                                                      dbbi)


def _dgelu(x):
    c, a = 0.7978845608028654, 0.044715
    th = jnp.tanh(c * (x + a * x * x * x))
    return 0.5 * (1.0 + th) + 0.5 * x * (1.0 - th * th) * c * (1.0 + 3.0 * a * x * x)


S5_BLOCKS = 4
S5_BW = S5_WIDTH // S5_BLOCKS
S5_BL = S5_LANES // S5_BLOCKS
S5_LANE_BLOCKS = S5_LANES // 128
S5_SCAN_BLOCKS = 4


def _s5_prep(a_re, a_im, log_dt, b_re_t, b_im_t, seg):
    def body(ar_ref, ai_ref, ldt_ref, br_ref, bi_ref,
             rows_f, pfr_ref, pfi_ref, rows_r, prr_ref, pri_ref, bbr_ref, bbi_ref):
        ar, ai = ar_ref[...], ai_ref[...]
        dt = jnp.exp(ldt_ref[...])
        mag = jnp.exp(ar * dt)
        lr, li = mag * jnp.cos(ai * dt), mag * jnp.sin(ai * dt)
        den = ar * ar + ai * ai
        nr = lr - 1.0
        sr = (nr * ar + li * ai) / den
        si = (li * ar - nr * ai) / den
        wide = (SUBLANES, S5_LANES)
        cr, ci = lr, li
        for i in range(seg):
            pfr_ref[i] = jnp.broadcast_to(cr, wide)
            pfi_ref[i] = jnp.broadcast_to(ci, wide)
            prr_ref[seg - 1 - i] = jnp.broadcast_to(cr, wide)
            pri_ref[seg - 1 - i] = jnp.broadcast_to(-ci, wide)
            if i == seg - 1:
                for rows, sign in ((rows_f, 1.0), (rows_r, -1.0)):
                    rows[0:1, :] = lr
                    rows[1:2, :] = sign * li
                    rows[2:3, :] = cr
                    rows[3:4, :] = sign * ci
            cr, ci = cr * lr - ci * li, cr * li + ci * lr
        br, bi = br_ref[...], bi_ref[...]
        bbr_ref[...] = sr * br - si * bi
        bbi_ref[...] = sr * bi + si * br

    whole = pl.BlockSpec(memory_space=pltpu.VMEM)
    tables = [jax.ShapeDtypeStruct((4, S5_LANES), F32)] + [jax.ShapeDtypeStruct((seg, SUBLANES, S5_LANES), F32)] * 2
    bbar = [jax.ShapeDtypeStruct((S5_GROUP, S5_LANES), F32)] * 2
    res = pl.pallas_call(body, name="s5_prep", in_specs=[whole] * 5, out_specs=[whole] * 8,
                         out_shape=tables + tables + bbar)(a_re, a_im, log_dt, b_re_t, b_im_t)
    return res[0:3], res[3:6], res[6], res[7]


def _lanes(j):
    return pl.ds(j * 128, 128)


def _to_segment_order(v, stage_ref, out_ref, seg):
    nbl = v.shape[1] // 128
    for b in range(nbl):
        stage_ref[b] = v[:, b * 128:(b + 1) * 128]

    def body(t, carry):
        rows = pl.ds(pl.multiple_of(t * SUBLANES, SUBLANES), SUBLANES)
        for b in range(nbl):
            out_ref[rows, _lanes(b)] = stage_ref[b, pl.ds(t, SUBLANES, stride=seg), :]
        return carry

    lax.fori_loop(0, seg, body, 0, unroll=True)


def _from_segment_order(v, stage_ref, out_ref, seg):
    nbl = v.shape[1] // 128
    for b in range(nbl):
        stage_ref[b] = v[:, b * 128:(b + 1) * 128]
    for s in range(SUBLANES):
        def body(k, carry, s=s):
            rows = pl.ds(pl.multiple_of(s * seg + k * SUBLANES, SUBLANES), SUBLANES)
            for b in range(nbl):
                out_ref[rows, _lanes(b)] = stage_ref[b, pl.ds(k * SUBLANES * SUBLANES + s, SUBLANES,
                                                              stride=SUBLANES), :]
            return carry

        lax.fori_loop(0, seg // SUBLANES, body, 0, unroll=True)


def _tile_scan(xr_ref, xi_ref, lam_ref, car_ref, cai_ref, cn_r, cn_i, blocks, seg, reverse):
    shape = (SUBLANES, 128)
    lrs = [jnp.broadcast_to(lam_ref[0:1, _lanes(j)], shape) for j in blocks]
    lis = [jnp.broadcast_to(lam_ref[1:2, _lanes(j)], shape) for j in blocks]

    def step(k, carry):
        t = seg - 1 - k if reverse else k
        rows = pl.ds(pl.multiple_of(t * SUBLANES, SUBLANES), SUBLANES)
        out = []
        for n, j in enumerate(blocks):
            cr, ci = carry[2 * n], carry[2 * n + 1]
            nr = lrs[n] * cr - lis[n] * ci + xr_ref[rows, _lanes(j)]
            ni = lrs[n] * ci + lis[n] * cr + xi_ref[rows, _lanes(j)]
            xr_ref[rows, _lanes(j)] = nr
            xi_ref[rows, _lanes(j)] = ni
            out += [nr, ni]
        return tuple(out)

    zero = jnp.zeros(shape, F32)
    fin = lax.fori_loop(0, seg, step, (zero,) * (2 * len(blocks)), unroll=True)
    for n, j in enumerate(blocks):
        ls = _lanes(j)
        fr, fi = fin[2 * n], fin[2 * n + 1]
        sr, si = lam_ref[2:3, ls], lam_ref[3:4, ls]
        pr, pi = car_ref[:, ls], cai_ref[:, ls]
        for s in (reversed(range(SUBLANES)) if reverse else range(SUBLANES)):
            cn_r[s:s + 1, ls] = pr
            cn_i[s:s + 1, ls] = pi
            pr, pi = fr[s:s + 1, :] + sr * pr - si * pi, fi[s:s + 1, :] + sr * pi + si * pr
        car_ref[:, ls] = pr
        cai_ref[:, ls] = pi


def _s5_fwd(proj, lam_rows, p3_re, p3_im, bbr4, bbi4, crt4, cit4, d_row, t_len, tb):
    seg = tb // SUBLANES

    def body(u_ref, lam_ref, p3r_ref, p3i_ref, bbr_ref, bbi_ref, crt_ref, cit_ref, d_ref,
             hr_ref, hi_ref, ypre_ref, ys_ref, car_ref, cai_ref, cn_r, cn_i, stage_ref, us_ref, yseg_ref):
        @pl.when(pl.program_id(0) == 0)
        def _():
            car_ref[...] = jnp.zeros_like(car_ref)
            cai_ref[...] = jnp.zeros_like(cai_ref)

        _to_segment_order(u_ref[...], stage_ref, us_ref, seg)
        u = us_ref[...]
        for i in range(S5_BLOCKS):
            ui = u[:, i * S5_BW:(i + 1) * S5_BW]
            hr_ref[:, pl.ds(i * S5_BL, S5_BL)] = _dot(ui, bbr_ref[i])
            hi_ref[:, pl.ds(i * S5_BL, S5_BL)] = _dot(ui, bbi_ref[i])
        for lc in range(S5_LANE_BLOCKS // S5_SCAN_BLOCKS):
            blocks = range(lc * S5_SCAN_BLOCKS, (lc + 1) * S5_SCAN_BLOCKS)
            _tile_scan(hr_ref, hi_ref, lam_ref, car_ref, cai_ref, cn_r, cn_i, blocks, seg, False)
            crs = [cn_r[:, _lanes(j)] for j in blocks]
            cis = [cn_i[:, _lanes(j)] for j in blocks]

            def fix(t, carry, blocks=blocks, crs=crs, cis=cis):
                rows = pl.ds(pl.multiple_of(t * SUBLANES, SUBLANES), SUBLANES)
                for n, j in enumerate(blocks):
                    pr, pi = p3r_ref[t, :, _lanes(j)], p3i_ref[t, :, _lanes(j)]
                    hr_ref[rows, _lanes(j)] += pr * crs[n] - pi * cis[n]
                    hi_ref[rows, _lanes(j)] += pr * cis[n] + pi * crs[n]
                return carry

            lax.fori_loop(0, seg, fix, 0, unroll=True)
        for i in range(S5_BLOCKS):
            ws = pl.ds(i * S5_BW, S5_BW)
            bl = pl.ds(i * S5_BL, S5_BL)
            yseg_ref[:, ws] = (_dot(hr_ref[:, bl], crt_ref[i]) - _dot(hi_ref[:, bl], cit_ref[i])
                               + d_ref[:, ws] * u[:, i * S5_BW:(i + 1) * S5_BW])
        _from_segment_order(yseg_ref[...], stage_ref, ypre_ref, seg)
        ys_ref[...] = jax.nn.gelu(ypre_ref[...], approximate=True).astype(ys_ref.dtype)

    whole = pl.BlockSpec(memory_space=pltpu.VMEM)
    return pl.pallas_call(
        body, name="s5_fwd", grid=(t_len // tb,),
        in_specs=[pl.BlockSpec((tb, S5_WIDTH), lambda i: (i, 4096 // S5_WIDTH))] + [whole] * 8,
        out_specs=[pl.BlockSpec((tb, S5_LANES), lambda i: (i, 0)), pl.BlockSpec((tb, S5_LANES), lambda i: (i, 0)),
                   pl.BlockSpec((tb, S5_WIDTH), lambda i: (i, 0)), pl.BlockSpec((tb, S5_WIDTH), lambda i: (i, 0))],
        out_shape=[jax.ShapeDtypeStruct((t_len, S5_LANES), F32), jax.ShapeDtypeStruct((t_len, S5_LANES), F32),
                   jax.ShapeDtypeStruct((t_len, S5_WIDTH), F32), jax.ShapeDtypeStruct((t_len, S5_WIDTH), MXU_DTYPE)],
        scratch_shapes=[pltpu.VMEM((1, S5_LANES), F32), pltpu.VMEM((1, S5_LANES), F32),
                        pltpu.VMEM((SUBLANES, S5_LANES), F32), pltpu.VMEM((SUBLANES, S5_LANES), F32),
                        pltpu.VMEM((S5_WIDTH // 128, tb, 128), F32), pltpu.VMEM((tb, S5_WIDTH), F32),
                        pltpu.VMEM((tb, S5_WIDTH), F32)],
        compiler_params=_params("arbitrary"))(proj, lam_rows, p3_re, p3_im, bbr4, bbi4, crt4, cit4, d_row)


def _s5_bwd(dgelu, y_pre, proj, h_re, h_im, lam_rows, p3_re, p3_im, bbr4, bbi4, cr4, ci4, d_row, dproj, t_len, tb):
    seg = tb // SUBLANES
    nb = t_len // tb

    def body(dg_ref, yp_ref, u_ref, hr_ref, hi_ref, lam_ref, p3r_ref, p3i_ref, bbr_ref, bbi_ref, cr_ref, ci_ref,
             d_ref, _, du_ref, dbbr_ref, dbbi_ref, dcr_ref, dci_ref, dd_ref, dlam_ref,
             gr_ref, gi_ref, car_ref, cai_ref, cn_r, cn_i, stage_ref, us_ref, dys_ref, duseg_ref):
        @pl.when(pl.program_id(0) == 0)
        def _():
            for ref in (car_ref, cai_ref, dbbr_ref, dbbi_ref, dcr_ref, dci_ref, dd_ref, dlam_ref):
                ref[...] = jnp.zeros_like(ref)

        _to_segment_order(u_ref[...], stage_ref, us_ref, seg)
        _to_segment_order(dg_ref[...] * _dgelu(yp_ref[...]), stage_ref, dys_ref, seg)
        u, dy = us_ref[...], dys_ref[...]
        for i in range(S5_BLOCKS):
            dyi = dy[:, i * S5_BW:(i + 1) * S5_BW]
            gr_ref[:, pl.ds(i * S5_BL, S5_BL)] = _dot(dyi, cr_ref[i])
            gi_ref[:, pl.ds(i * S5_BL, S5_BL)] = -_dot(dyi, ci_ref[i])
        for lc in range(S5_LANE_BLOCKS // S5_SCAN_BLOCKS):
            blocks = range(lc * S5_SCAN_BLOCKS, (lc + 1) * S5_SCAN_BLOCKS)
            _tile_scan(gr_ref, gi_ref, lam_ref, car_ref, cai_ref, cn_r, cn_i, blocks, seg, True)
            crs = [cn_r[:, _lanes(j)] for j in blocks]
            cis = [cn_i[:, _lanes(j)] for j in blocks]

            def fix(k, carry, blocks=blocks, crs=crs, cis=cis):
                t = seg - 1 - k
                rows = pl.ds(pl.multiple_of(t * SUBLANES, SUBLANES), SUBLANES)
                out = []
                for n, j in enumerate(blocks):
                    nr, ni, slr, sli = carry[4 * n:4 * n + 4]
                    pr, pi = p3r_ref[t, :, _lanes(j)], p3i_ref[t, :, _lanes(j)]
                    g_r = gr_ref[rows, _lanes(j)] + pr * crs[n] - pi * cis[n]
                    g_i = gi_ref[rows, _lanes(j)] + pr * cis[n] + pi * crs[n]
                    gr_ref[rows, _lanes(j)] = g_r
                    gi_ref[rows, _lanes(j)] = g_i
                    hr, hi = hr_ref[rows, _lanes(j)], hi_ref[rows, _lanes(j)]
                    out += [g_r, g_i, slr + nr * hr + ni * hi, sli + ni * hr - nr * hi]
                return tuple(out)

            zero = jnp.zeros((SUBLANES, 128), F32)
            init = []
            for n in range(len(blocks)):
                init += [crs[n], cis[n], zero, zero]
            fin = lax.fori_loop(0, seg, fix, tuple(init), unroll=True)
            for n, j in enumerate(blocks):
                dlam_ref[0:1, _lanes(j)] += jnp.sum(fin[4 * n + 2], axis=0, keepdims=True)
                dlam_ref[1:2, _lanes(j)] += jnp.sum(fin[4 * n + 3], axis=0, keepdims=True)
        for i in range(S5_BLOCKS):
            ws = pl.ds(i * S5_BW, S5_BW)
            bl = pl.ds(i * S5_BL, S5_BL)
            ui, dyi = u[:, i * S5_BW:(i + 1) * S5_BW], dy[:, i * S5_BW:(i + 1) * S5_BW]
            gr, gi = gr_ref[:, bl], gi_ref[:, bl]
            duseg_ref[:, ws] = _dot(gr, bbr_ref[i], _NT) + _dot(gi, bbi_ref[i], _NT) + d_ref[:, ws] * dyi
            dbbr_ref[i] += _dot(ui, gr, _TN)
            dbbi_ref[i] += _dot(ui, gi, _TN)
            dcr_ref[i] += _dot(hr_ref[:, bl], dyi, _TN)
            dci_ref[i] -= _dot(hi_ref[:, bl], dyi, _TN)
        dd_ref[...] += jnp.sum(dy * u, axis=0, keepdims=True)
        _from_segment_order(duseg_ref[...], stage_ref, duseg_ref, seg)
        du_ref[...] = duseg_ref[...].astype(du_ref.dtype)

    whole = pl.BlockSpec(memory_space=pltpu.VMEM)
    rev = lambda i: (nb - 1 - i, 0)
    const3 = lambda i: (0, 0, 0)
    return pl.pallas_call(
        body, name="s5_bwd", grid=(nb,),
        in_specs=[pl.BlockSpec((tb, S5_WIDTH), rev), pl.BlockSpec((tb, S5_WIDTH), rev),
                  pl.BlockSpec((tb, S5_WIDTH), lambda i: (nb - 1 - i, 4096 // S5_WIDTH)),
                  pl.BlockSpec((tb, S5_LANES), rev), pl.BlockSpec((tb, S5_LANES), rev)] + [whole] * 8
                 + [pl.BlockSpec(memory_space=pl.ANY)],
        out_specs=[pl.BlockSpec((tb, S5_WIDTH), lambda i: (nb - 1 - i, 4096 // S5_WIDTH)),
                   pl.BlockSpec((S5_BLOCKS, S5_BW, S5_BL), const3), pl.BlockSpec((S5_BLOCKS, S5_BW, S5_BL), const3),
                   pl.BlockSpec((S5_BLOCKS, S5_BL, S5_BW), const3), pl.BlockSpec((S5_BLOCKS, S5_BL, S5_BW), const3),
                   pl.BlockSpec((1, S5_WIDTH), lambda i: (0, 0)), pl.BlockSpec((2, S5_LANES), lambda i: (0, 0))],
        out_shape=[jax.ShapeDtypeStruct((t_len, IN_COLS), dproj.dtype),
                   jax.ShapeDtypeStruct((S5_BLOCKS, S5_BW, S5_BL), F32),
                   jax.ShapeDtypeStruct((S5_BLOCKS, S5_BW, S5_BL), F32),
                   jax.ShapeDtypeStruct((S5_BLOCKS, S5_BL, S5_BW), F32),
                   jax.ShapeDtypeStruct((S5_BLOCKS, S5_BL, S5_BW), F32),
                   jax.ShapeDtypeStruct((1, S5_WIDTH), F32), jax.ShapeDtypeStruct((2, S5_LANES), F32)],
        scratch_shapes=[pltpu.VMEM((tb, S5_LANES), F32), pltpu.VMEM((tb, S5_LANES), F32),
                        pltpu.VMEM((1, S5_LANES), F32), pltpu.VMEM((1, S5_LANES), F32),
                        pltpu.VMEM((SUBLANES, S5_LANES), F32), pltpu.VMEM((SUBLANES, S5_LANES), F32),
                        pltpu.VMEM((S5_WIDTH // 128, tb, 128), F32), pltpu.VMEM((tb, S5_WIDTH), F32),
                        pltpu.VMEM((tb, S5_WIDTH), F32), pltpu.VMEM((tb, S5_WIDTH), F32)],
        input_output_aliases={13: 0},
        compiler_params=_params("arbitrary"))(dgelu, y_pre, proj, h_re, h_im, lam_rows, p3_re, p3_im, bbr4, bbi4,
                                              cr4, ci4, d_row, dproj)


def _block_diag(per_group):
    g8 = S5_GROUPS // S5_BLOCKS
    eye = jnp.eye(g8, dtype=bool)[None, :, None, :, None]
    dense = jnp.where(eye, per_group.reshape(S5_BLOCKS, g8, S5_GROUP, 1, S5_STATE), 0.0)
    return dense.reshape(S5_BLOCKS, S5_BW, S5_BL)


def _diag_blocks(dense):
    g8 = S5_GROUPS // S5_BLOCKS
    ar = jnp.arange(g8)
    d5 = dense.reshape(S5_BLOCKS, g8, S5_GROUP, g8, S5_STATE)
    return d5[:, ar, :, ar, :].transpose(1, 0, 2, 3).reshape(S5_GROUPS, S5_GROUP, S5_STATE)


def _hg_gate_bwd(da, o, g, gn):
    dos, dgs, dgns = [], [], []
    for h in range(HG_HEADS):
        sl = slice(h * HG_DIM, (h + 1) * HG_DIM)
        oh, gh, dah, gnh = o[:, sl], g[:, sl], da[:, sl], gn[:, sl]
        rr = lax.rsqrt(jnp.mean(oh * oh, axis=-1, keepdims=True) + NORM_EPS)
        sg = _sig(gh)
        dgs.append(dah * (oh * rr * gnh) * _dsilu(gh, sg))
        don = dah * (gh * sg)
        t = don * gnh
        dos.append(rr * t - oh * (rr * rr * rr) * jnp.mean(t * oh, axis=-1, keepdims=True))
        dgns.append(jnp.sum(don * oh * rr, axis=0, keepdims=True))
    return jnp.concatenate(dos, axis=1), jnp.concatenate(dgs, axis=1), jnp.concatenate(dgns, axis=1)


MIX_BWD_COLS = ((3072, 1024), (4608, 512), (5120, 1024), (6144, 1024))


def _mix_bwd(dgl, h1, dh2, act_hg, ys2, ys_gelu, proj, o_hg, g2, ghn, b_glu, w, t_len, tm):
    nb = t_len // tm

    def body(dgl_ref, h1_ref, dh2_ref, act_ref, ys2_ref, ysg_ref, ghg_ref, z_ref, gh_ref, gs_ref, o_ref, g2_ref, gn_ref,
             bglu_ref, wg_ref, wo_ref, ws5_ref, whg_ref, wglu_ref,
             dh1_ref, dyh_ref, dys_ref, dglu_ref, dgelu_ref, do_ref, dg2_ref, dbglu_ref, dgn_ref, dproj_ref,
             st0, st1, st2, st3, sems):
        i = pl.program_id(0)
        stages = (st0, st1, st2, st3)

        def writes(step):
            rows = pl.ds(pl.multiple_of(step * tm, tm), tm)
            return [pltpu.make_async_copy(st, dproj_ref.at[rows, pl.ds(c0, wd)], sems.at[k])
                    for k, (st, (c0, wd)) in enumerate(zip(stages, MIX_BWD_COLS))]

        @pl.when(i > 0)
        def _():
            for cp in writes(i - 1):
                cp.wait()

        @pl.when(i == 0)
        def _():
            for ref in (dg2_ref, dbglu_ref, dgn_ref):
                ref[...] = jnp.zeros_like(ref)

        dx, dg2 = _rms_bwd(_dot(dgl_ref[...], wg_ref[...], _NT), h1_ref[...], g2_ref[...])
        dh1 = dh2_ref[...] + dx
        dh1_ref[...] = dh1
        dg2_ref[...] += dg2
        dm = _dot(dh1, wo_ref[...], _NT)
        sh, ss = _sig(gh_ref[...]), _sig(gs_ref[...])
        dyh, dys = _mx(dm * sh), _mx(dm * ss)
        dyh_ref[...] = dyh
        dys_ref[...] = dys
        st2[...] = (dm * _dot(act_ref[...], whg_ref[...]) * sh * (1.0 - sh)).astype(st2.dtype)
        st3[...] = (dm * _dot(ys2_ref[...], ws5_ref[...]) * ss * (1.0 - ss)).astype(st3.dtype)
        dys2 = _dot(dys, ws5_ref[...], _NT)
        gl_, z = _dot(ysg_ref[...], wglu_ref[...]) + bglu_ref[...], z_ref[...]
        a, b = gl_[:, :S5_WIDTH], gl_[:, S5_WIDTH:]
        sb, sz = _sig(b), _sig(z)
        silu = z * sz
        dglu = jnp.concatenate([dys2 * sb * silu, dys2 * a * silu * sb * (1.0 - sb)], axis=1)
        st1[...] = (dys2 * a * sb * _dsilu(z, sz)).astype(st1.dtype)
        dbglu_ref[...] += jnp.sum(dglu, axis=0, keepdims=True)
        dglu_ref[...] = _mx(dglu)
        dgelu_ref[...] = _dot(dglu, wglu_ref[...], _NT)
        d_o, dg, dgn = _hg_gate_bwd(_dot(dyh, whg_ref[...], _NT), o_ref[...], ghg_ref[...], gn_ref[...])
        do_ref[...] = d_o.astype(do_ref.dtype)
        st0[...] = dg.astype(st0.dtype)
        dgn_ref[...] += dgn
        for cp in writes(i):
            cp.start()

        @pl.when(i == nb - 1)
        def _():
            for cp in writes(i):
                cp.wait()

    tile = lambda wd, cb=0: pl.BlockSpec((tm, wd), functools.partial(lambda i, cb: (i, cb), cb=cb))
    row = lambda wd: pl.BlockSpec((1, wd), lambda i: (0, 0))
    whole = pl.BlockSpec(memory_space=pltpu.VMEM)
    return pl.pallas_call(
        body, name="mix_bwd", grid=(nb,),
        in_specs=[tile(1024), tile(1024), tile(1024), tile(1024), tile(512), tile(512), tile(1024, 3),
                  tile(512, 4608 // 512), tile(1024, 5), tile(1024, 6), tile(1024), row(1024), row(1024), row(1024)]
                 + [whole] * 5,
        out_specs=[tile(1024), tile(1024), tile(1024), tile(1024), tile(512), tile(1024), row(1024), row(1024),
                   row(1024), _HBM],
        out_shape=[jax.ShapeDtypeStruct((t_len, 1024), F32), jax.ShapeDtypeStruct((t_len, 1024), MXU_DTYPE),
                   jax.ShapeDtypeStruct((t_len, 1024), MXU_DTYPE), jax.ShapeDtypeStruct((t_len, 1024), MXU_DTYPE),
                   jax.ShapeDtypeStruct((t_len, 512), F32), jax.ShapeDtypeStruct((t_len, 1024), MXU_DTYPE),
                   jax.ShapeDtypeStruct((1, 1024), F32), jax.ShapeDtypeStruct((1, 1024), F32),
                   jax.ShapeDtypeStruct((1, 1024), F32), jax.ShapeDtypeStruct((t_len, IN_COLS), MXU_DTYPE)],
        scratch_shapes=[pltpu.VMEM((tm, wd), MXU_DTYPE) for _, wd in MIX_BWD_COLS] + [pltpu.SemaphoreType.DMA((4,))],
        compiler_params=_params("arbitrary"))(dgl, h1, dh2, act_hg, ys2, ys_gelu, proj, proj, proj, proj, o_hg, g2, ghn,
                                              b_glu, w["w_ple_gate"], w["w_out"], w["w_o_s5"], w["w_o_hg"],
                                              w["w_glu"])


def _local_step(x, p, target, w, sm, comm=None):
    t_len = x.shape[0]
    tm = min(256, t_len)
    tmm = min(512, t_len)
    tb_hg = min(256, t_len)
    tb_s5 = min(256, t_len)
    g1, g2, g3, ghn = sm["norm_g"], sm["ple_norm_g"], sm["final_norm_g"].reshape(1, D_MODEL), sm["hg_norm_g"]

    def rms_in(xv, g):
        return xv * lax.rsqrt(jnp.mean(xv * xv, axis=-1, keepdims=True) + NORM_EPS) * g

    in_shard = IN_COLS // N_CHIPS
    if comm is None:
        w_in = w["w_in"]
        proj, u = _mm_nn("mm_in", x, w_in, tmm, in_shard, prologue=rms_in, consts=[g1])
    else:
        proj, u, w_in = comm.input_projection(x, g1, rms_in, tmm)

    lanes = lambda a: a.reshape(1, S5_LANES)
    a_re, a_im = lanes(sm["s5_a_re"]), lanes(sm["s5_a_im"])
    ldt = lanes(jnp.broadcast_to(sm["s5_log_dt"].reshape(S5_GROUPS, 1), (S5_GROUPS, S5_STATE)))
    to_t = lambda b: b.reshape(S5_GROUPS, S5_STATE, S5_GROUP).transpose(2, 0, 1).reshape(S5_GROUP, S5_LANES)
    b_re_t, b_im_t = to_t(sm["s5_b_re"]), to_t(sm["s5_b_im"])
    scan_fwd, scan_rev, bbr_t, bbi_t = _s5_prep(a_re, a_im, ldt, b_re_t, b_im_t, tb_s5 // SUBLANES)
    from_t = lambda b: b.reshape(S5_GROUP, S5_GROUPS, S5_STATE).transpose(1, 0, 2)
    bbr_bd = _block_diag(from_t(bbr_t)).astype(MXU_DTYPE)
    bbi_bd = _block_diag(from_t(bbi_t)).astype(MXU_DTYPE)
    cr_bd = _block_diag(sm["s5_c_re"].reshape(S5_GROUPS, S5_GROUP, S5_STATE)).astype(MXU_DTYPE)
    ci_bd = _block_diag(sm["s5_c_im"].reshape(S5_GROUPS, S5_GROUP, S5_STATE)).astype(MXU_DTYPE)
    d_row = sm["s5_d"].reshape(1, S5_WIDTH)
    if comm is None:
        o_hg, act_hg, s_prev = _hgrn2_fwd(proj, sm["hg_lb"], ghn, t_len, tb_hg)
    else:
        o_hg, act_hg, s_prev, landed = _hgrn2_fwd(proj, sm["hg_lb"], ghn, t_len, tb_hg, riding=comm.gather_rest())
        w = comm.rest_weights(landed)
    h_re, h_im, y_pre, ys_gelu = _s5_fwd(proj, *scan_fwd, bbr_bd, bbi_bd,
                                          cr_bd.transpose(0, 2, 1), ci_bd.transpose(0, 2, 1), d_row, t_len, tb_s5)
    def mix_f(act, ysg, z, gh, gs, xv, w_glu, b_glu, w_o_hg, w_o_s5, w_out):
        yh = _dot(act, w_o_hg)
        gl_ = _dot(ysg, w_glu) + b_glu
        a, b = gl_[:, :S5_WIDTH], gl_[:, S5_WIDTH:]
        ys2_ = (a * _sig(b) * (z * _sig(z))).astype(MXU_DTYPE)
        ys = _dot(ys2_, w_o_s5)
        mg = (_sig(gh) * yh + _sig(gs) * ys).astype(MXU_DTYPE)
        return (ys2_, mg, xv + _dot(mg, w_out))

    ys2, merged, h1 = _rowwise(
        "mix_out", mix_f, t_len, tm,
        [(act_hg, 1024, 0), (ys_gelu, 512, 0), (proj, 512, 4608 // 512), (proj, 1024, 5), (proj, 1024, 6),
         (x, 1024, 0)], [w["w_glu"], sm["b_glu"], w["w_o_hg"], w["w_o_s5"], w["w_out"]],
        [(512, MXU_DTYPE), (1024, MXU_DTYPE), (1024, F32)])

    def head_f(h1v, pv, tgt, g_ple, g, w_ple, w_gate):
        r2 = lax.rsqrt(jnp.mean(h1v * h1v, axis=-1, keepdims=True) + NORM_EPS)
        n2_ = (h1v * r2 * g_ple).astype(MXU_DTYPE)
        glv, pev = _dot(n2_, w_gate), _dot(pv, w_ple)
        gate = _sig(glv)
        h2 = h1v + pev * gate
        r = lax.rsqrt(jnp.mean(h2 * h2, axis=-1, keepdims=True) + NORM_EPS)
        e = h2 * r * g - tgt
        loss = 0.5 * jnp.sum(jnp.mean(e * e, axis=-1, keepdims=True), axis=0, keepdims=True)
        dy = e * (1.0 / D_MODEL)
        dg = jnp.sum(dy * h2 * r, axis=0, keepdims=True)
        t = dy * g
        dh2 = r * t - h2 * (r * r * r) * jnp.mean(t * h2, axis=-1, keepdims=True)
        dpe, dgl_ = _mx(dh2 * gate), _mx(dh2 * pev * gate * (1.0 - gate))
        return (dh2, dgl_, jnp.broadcast_to(loss, (1, 128)), dg, _dot(pv, dpe, _TN), _dot(n2_, dgl_, _TN))

    gb = {}
    dh2, dgl, loss_row, d_g3, gb["w_ple"], gb["w_ple_gate"] = _rowwise(
        "ple_loss_head", head_f, t_len, tmm, [(h1, 1024, 0), (p, 256, 0), (target, 1024, 0)],
        [g2, g3, w["w_ple"], w["w_ple_gate"]], [(1024, F32), (1024, MXU_DTYPE)],
        accs=[(1, 128), (1, 1024), (256, 1024), (1024, 1024)])

    dh1, dy_hg, dy_s5, dglu, dgelu, d_o, d_g2, d_bglu, d_ghn, dproj = _mix_bwd(
        dgl, h1, dh2, act_hg, ys2, ys_gelu, proj, o_hg, g2, ghn, sm["b_glu"], w, t_len, tm)
    gb["w_out"] = _mm_tn("mm_d_w_out", merged, dh1, tmm, 1024)
    gb["w_o_s5"] = _mm_tn("mm_d_w_o_s5", ys2, dy_s5, tmm, 1024)
    gb["w_glu"] = _mm_tn("mm_d_w_glu", ys_gelu, dglu, tmm, 1024)
    dproj, d_bbr, d_bbi, d_crt, d_cit, d_d, d_lam = _s5_bwd(dgelu, y_pre, proj, h_re, h_im,
                                                            *scan_rev, bbr_bd, bbi_bd, cr_bd,
                                                            ci_bd, d_row, dproj, t_len, tb_s5)
    to_t3 = lambda b: b.transpose(1, 0, 2).reshape(S5_GROUP, S5_LANES)
    d_are, d_aim, d_ldt, d_br_t, d_bi_t = _s5_prep_bwd(a_re, a_im, ldt, b_re_t, b_im_t, d_lam,
                                                       to_t3(_diag_blocks(d_bbr)), to_t3(_diag_blocks(d_bbi)))
    gb["w_o_hg"] = _mm_tn("mm_d_w_o_hg", act_hg, dy_hg, tmm, 1024)
    if comm is None:
        dproj, d_lb = _hgrn2_bwd(proj, d_o, s_prev, sm["hg_lb"], dproj, t_len, tb_hg)
    else:
        rest_grads = _pack_rest_full(gb)
        dproj, d_lb, rest_theirs = _hgrn2_bwd(proj, d_o, s_prev, sm["hg_lb"], dproj, t_len, tb_hg,
                                               riding=comm.swap(rest_grads))

    def in_b(duv, xv, dh, g):
        dx, dg = _rms_bwd(duv, xv, g)
        return (dh + dx, dg)

    in_args = ("mm_d_u_rms_in_bwd", dproj, w_in, tmm, in_shard, in_b, [(x, 1024, 0), (dh1, 1024, 0)], [g1],
               [(1024, F32)])
    if comm is None:
        gb["w_in"] = _mm_tn("mm_d_w_in", u, dproj, tmm, in_shard, col_shards=True)
        grad_x, d_g1 = _mm_nt_then(*in_args, accs=[(1, 1024)])
    else:
        gb["w_in"], landed = _mm_tn("mm_d_w_in", u, dproj, tmm, in_shard, col_shards=True,
                                    riding=comm.scatter("rest", rest_grads, rest_theirs))
        comm.landed["rest"] = landed
        grad_x, d_g1, landed = _mm_nt_then(*in_args, accs=[(1, 1024)], riding=comm.scatter(
            "in", gb["w_in"].reshape(N_CHIPS, 2, D_MODEL // 2, in_shard)))
        comm.landed["in"] = landed

    back_t = lambda b: b.reshape(S5_GROUP, S5_GROUPS, S5_STATE).transpose(1, 2, 0).reshape(1, S5_GROUPS, S5_STATE,
                                                                                           S5_GROUP)
    gs = {
        "norm_g": d_g1, "hg_lb": d_lb, "hg_norm_g": d_ghn,
        "s5_a_re": d_are.reshape(1, S5_GROUPS, S5_STATE), "s5_a_im": d_aim.reshape(1, S5_GROUPS, S5_STATE),
        "s5_log_dt": d_ldt[0:1, :S5_GROUPS],
        "s5_b_re": back_t(d_br_t), "s5_b_im": back_t(d_bi_t),
        "s5_c_re": _diag_blocks(d_crt.transpose(0, 2, 1)).reshape(1, S5_GROUPS, S5_GROUP, S5_STATE),
        "s5_c_im": _diag_blocks(d_cit.transpose(0, 2, 1)).reshape(1, S5_GROUPS, S5_GROUP, S5_STATE),
        "s5_d": d_d.reshape(1, S5_GROUPS, S5_GROUP), "b_glu": d_bglu, "ple_norm_g": d_g2,
        "final_norm_g": d_g3.reshape(D_MODEL),
    }
    return loss_row, grad_x, gb, gs


def _shard_shape(name):
    r, c = BIG_SHAPE[name]
    return (r, c // N_CHIPS) if name in BIG_COL_SHARDED else (r // N_CHIPS, c)


def _pack_small(parts, last):
    flat = jnp.concatenate([parts[n].reshape(-1) for n in SMALL] + [last.reshape(-1)])
    return jnp.pad(flat, (0, SMALL_ROWS * PACK_W - flat.shape[0])).reshape(SMALL_ROWS, PACK_W)


def _unpack_small(packed):
    flat, out, off = packed.reshape(-1), {}, 0
    for n in SMALL:
        size = 1
        for d in SMALL_SHAPE[n]:
            size *= d
        out[n] = flat[off:off + size].reshape(SMALL_SHAPE[n])
        off += size
    return out, flat[off]


def _place():
    x, y, c = lax.axis_index("x"), lax.axis_index("y"), lax.axis_index("c")
    return x, y, c, [(1 - x, y), (x, 1 - y), (1 - x, 1 - y)]


def _remote(src, dst, send_sems, recv_sems, k, to):
    return pltpu.make_async_remote_copy(src_ref=src, dst_ref=dst, send_sem=send_sems.at[k], recv_sem=recv_sems.at[k],
                                        device_id=to, device_id_type=MESH)


REST = tuple(n for n in BIG if n != "w_in")
REST_ROWS = sum(BIG_SHAPE[n][0] * BIG_SHAPE[n][1] for n in REST) // (N_CHIPS * PACK_W)
IN_SHARD = IN_COLS // N_CHIPS
IN_TILE, REST_TILE = 256, 272


def _pack_rest(parts):
    return jnp.concatenate([parts[n].reshape(-1, PACK_W) for n in REST], axis=0)


def _unpack_rest(packed):
    out, off = {}, 0
    for n in REST:
        r, c = _shard_shape(n)
        rows = r * c // PACK_W
        out[n] = packed[off:off + rows].reshape(1, r, c)
        off += rows
    return out


def _unpack_rest_full(gathered):
    out, off = {}, 0
    for n in REST:
        r, c = _shard_shape(n)
        rows = r * c // PACK_W
        sh = gathered[:, off:off + rows].reshape(N_CHIPS, r, c)
        out[n] = sh.transpose(1, 0, 2).reshape(BIG_SHAPE[n]) if n in BIG_COL_SHARDED else sh.reshape(BIG_SHAPE[n])
        off += rows
    return out


def _pack_rest_full(full):
    parts = []
    for n in REST:
        r, c = _shard_shape(n)
        g = full[n]
        sh = g.reshape(BIG_SHAPE[n][0], N_CHIPS, c).transpose(1, 0, 2) if n in BIG_COL_SHARDED else g
        parts.append(sh.reshape(N_CHIPS, r * c // PACK_W, PACK_W))
    return jnp.concatenate(parts, axis=1).reshape(N_CHIPS, 2, REST_ROWS // 2, PACK_W)


def _swap_halves(pgs, name="exchange_halves"):
    n = len(pgs)

    def body(*refs):
        pg_refs, out_refs, (send_sems, recv_sems) = refs[:n], refs[n:2 * n], refs[2 * n:]
        x, y, c, _ = _place()
        cps = [_remote(pg_ref.at[j, 1 - c], out_ref.at[j], send_sems, recv_sems, N_CHIPS * g + j, (x, y, 1 - c))
               for g, (pg_ref, out_ref) in enumerate(zip(pg_refs, out_refs)) for j in range(N_CHIPS)]
        for cp in cps:
            cp.start()
        for cp in cps:
            cp.wait()

    return pl.pallas_call(
        body, name=name, in_specs=[_HBM] * n, out_specs=[_HBM] * n,
        out_shape=[jax.ShapeDtypeStruct((N_CHIPS,) + pg.shape[2:], pg.dtype) for pg in pgs],
        scratch_shapes=[pltpu.SemaphoreType.DMA((N_CHIPS * n,)), pltpu.SemaphoreType.DMA((N_CHIPS * n,))])(*pgs)


def _share_halves(gs):
    n = len(gs)

    def body(*refs):
        g_refs, out_refs, (send_sems, recv_sems) = refs[:n], refs[n:2 * n], refs[2 * n:]
        x, y, c, _ = _place()
        cps = [_remote(g_ref, out_ref.at[c], send_sems, recv_sems, g, (x, y, 1 - c))
               for g, (g_ref, out_ref) in enumerate(zip(g_refs, out_refs))]
        for cp in cps:
            cp.start()
        for g, (g_ref, out_ref) in enumerate(zip(g_refs, out_refs)):
            _remote(g_ref, out_ref.at[1 - c], send_sems, recv_sems, g, (x, y, 1 - c)).wait_recv()
        for cp in cps:
            cp.wait_send()

    return pl.pallas_call(
        body, name="share_half", in_specs=[_HBM] * n, out_specs=[_HBM] * n,
        out_shape=[jax.ShapeDtypeStruct((2,) + g.shape, g.dtype) for g in gs],
        scratch_shapes=[pltpu.SemaphoreType.DMA((n,)), pltpu.SemaphoreType.DMA((n,))])(*gs)


def _pair_sum(name, pg, theirs, c, tile):
    _, _, rows, width = pg.shape

    def body(c_ref, a_ref, b_ref, o_ref):
        o_ref[...] = (a_ref[...] + b_ref[...]).astype(o_ref.dtype)

    return pl.pallas_call(
        body, name=name,
        grid_spec=pltpu.PrefetchScalarGridSpec(
            num_scalar_prefetch=1, grid=(N_CHIPS, rows // tile),
            in_specs=[pl.BlockSpec((None, None, tile, width), lambda j, i, c_ref: (j, c_ref[0], i, 0)),
                      pl.BlockSpec((None, tile, width), lambda j, i, c_ref: (j, i, 0))],
            out_specs=pl.BlockSpec((None, tile, width), lambda j, i, c_ref: (j, i, 0))),
        out_shape=jax.ShapeDtypeStruct((N_CHIPS, rows, width), WIRE_DTYPE),
        compiler_params=_params("arbitrary", "arbitrary"))(c.reshape(1), pg, theirs)


def _chip_sum(name, ps, others, k, tile):
    _, rows, width = ps.shape

    def body(k_ref, a_ref, b_ref, o_ref):
        o_ref[...] = ((a_ref[...].astype(F32) + b_ref[0].astype(F32)) + b_ref[1].astype(F32)) + b_ref[2].astype(F32)

    return pl.pallas_call(
        body, name=name,
        grid_spec=pltpu.PrefetchScalarGridSpec(
            num_scalar_prefetch=1, grid=(rows // tile,),
            in_specs=[pl.BlockSpec((None, tile, width), lambda i, k_ref: (k_ref[0], i, 0)),
                      pl.BlockSpec((3, tile, width), lambda i, k_ref: (0, i, 0))],
            out_specs=pl.BlockSpec((tile, width), lambda i, k_ref: (i, 0))),
        out_shape=jax.ShapeDtypeStruct((rows, width), F32),
        compiler_params=_params("arbitrary"))(k.reshape(1), ps, others)


def _mm_in_gathering(x, g1, prologue, in_wire, chip, tm):
    m, k = x.shape
    half, ns = in_wire.shape[1:]
    nrow = m // tm

    def flip(j):
        return jnp.where(j == 1, 2, jnp.where(j == 2, 1, j))

    def body(k_ref, x_ref, g_ref, wire_ref, proj_ref, u_ref, all_ref, kept, b_ref, load_sems, send_sems, recv_sems):
        j, i = pl.program_id(0), pl.program_id(1)
        px, py, c, chips = _place()
        sibling = (px, py, 1 - c)

        def over_ici(r, chip_slot):
            cx, cy = chips[r]
            return _remote(wire_ref.at[c], all_ref.at[chip_slot, c], send_sems, recv_sems, r, (cx, cy, c))

        def to_sibling(r, half_slot):
            cx, cy = chips[r]
            return _remote(all_ref.at[2 * cx + cy, c], all_ref.at[2 * cx + cy, half_slot], send_sems, recv_sems,
                           3 + r, sibling)

        def loads(src, slot):
            return [pltpu.make_async_copy(src.at[h], b_ref.at[slot, pl.ds(h * half, half)], load_sems.at[h])
                    for h in range(2)]

        def shard(r):
            cx, cy = chips[r]
            over_ici(r, 2 * cx + cy).wait_recv()
            if r == 0:
                over_ici(2, 2 * px + py).start()
            to_sibling(r, c).start()
            to_sibling(r, 1 - c).wait_recv()
            return all_ref.at[2 * cx + cy]

        @pl.when((j == 0) & (i == 0))
        def _():
            for r in range(2):
                over_ici(r, 2 * px + py).start()
            for cp in loads(wire_ref, 0):
                cp.start()
            for cp in loads(wire_ref, 0):
                cp.wait()

        @pl.when((j == 1) & (i == 0))
        def _():
            cps = loads(shard(0), 1)
            for cp in cps:
                cp.start()
            for cp in cps:
                cp.wait()

        for nxt in (2, 3):
            @pl.when((j == nxt - 1) & (i == nrow // 2))
            def _(nxt=nxt):
                for cp in loads(shard(nxt - 1), nxt % 2):
                    cp.start()

            @pl.when((j == nxt) & (i == 0))
            def _(nxt=nxt):
                for cp in loads(wire_ref, nxt % 2):
                    cp.wait()

        rows = pl.ds(pl.multiple_of(i * tm, tm), tm)

        @pl.when(j == 0)
        def _():
            tile = _mx(prologue(x_ref[...], g_ref[...]))
            kept[rows, :] = tile
            u_ref[...] = tile

        proj_ref[...] = _dot(kept[rows, :], b_ref[lax.rem(j, 2)])

        @pl.when((j == N_CHIPS - 1) & (i == nrow - 1))
        def _():
            for r in range(3):
                over_ici(r, 2 * px + py).wait_send()
                to_sibling(r, c).wait_send()

    once = lambda j, i, k_ref: (jnp.where(j == 0, i, nrow - 1), 0)
    return pl.pallas_call(
        body, name="mm_in",
        grid_spec=pltpu.PrefetchScalarGridSpec(
            num_scalar_prefetch=1, grid=(N_CHIPS, nrow),
            in_specs=[pl.BlockSpec((tm, k), once), pl.BlockSpec(g1.shape, lambda j, i, k_ref: (0, 0)), _HBM],
            out_specs=[pl.BlockSpec((tm, ns), lambda j, i, k_ref: (i, jnp.bitwise_xor(k_ref[0], flip(j)))),
                       pl.BlockSpec((tm, k), once), _HBM],
            scratch_shapes=[pltpu.VMEM((m, k), MXU_DTYPE), pltpu.VMEM((2, 2 * half, ns), in_wire.dtype),
                            pltpu.SemaphoreType.DMA((2,)), pltpu.SemaphoreType.DMA((6,)),
                            pltpu.SemaphoreType.DMA((6,))]),
        out_shape=[jax.ShapeDtypeStruct((m, N_CHIPS * ns), F32), jax.ShapeDtypeStruct((m, k), MXU_DTYPE),
                   jax.ShapeDtypeStruct((N_CHIPS,) + in_wire.shape, in_wire.dtype)],
        compiler_params=_params("arbitrary", "arbitrary"))(chip.reshape(1), x, g1, in_wire)


class _StepComm:
    TILES = {"in": IN_TILE, "rest": REST_TILE}

    def __init__(self, in_wire, rest_wire, chip, core):
        self.in_wire, self.rest_wire, self.chip, self.core = in_wire, rest_wire, chip, core
        self.sums, self.landed = {}, {}

    def input_projection(self, x, g1, prologue, tm):
        proj, u, shards = _mm_in_gathering(x, g1, prologue, self.in_wire, self.chip, tm)
        shards = lax.dynamic_update_slice(shards, self.in_wire[None], (self.chip, 0, 0, 0))
        return proj, u, shards.reshape(N_CHIPS, D_MODEL, IN_SHARD)

    def gather_rest(self):
        wire = self.rest_wire

        def sends(ins, outs, send_sems, recv_sems):
            (w_ref,), (out_ref,) = ins, outs
            x, y, c, chips = _place()
            return [_remote(w_ref.at[c], out_ref.at[2 * x + y, c], send_sems, recv_sems, 4 * j + 2 * c + to,
                            (cx, cy, to)) for j, (cx, cy) in enumerate(chips) for to in (0, 1)]

        def recvs(ins, outs, send_sems, recv_sems):
            (w_ref,), (out_ref,) = ins, outs
            _, _, c, chips = _place()
            return [_remote(w_ref.at[c], out_ref.at[2 * cx + cy, by], send_sems, recv_sems, 4 * j + 2 * by + c,
                            (cx, cy, by)) for j, (cx, cy) in enumerate(chips) for by in (0, 1)]

        def start(*refs):
            for cp in sends(*refs):
                cp.start()

        def wait(*refs):
            for cp in recvs(*refs):
                cp.wait_recv()
            for cp in sends(*refs):
                cp.wait_send()

        return _Riding((wire,), (jax.ShapeDtypeStruct((N_CHIPS,) + wire.shape, wire.dtype),), 12, start, wait)

    def rest_weights(self, landed):
        full = lax.dynamic_update_slice(landed, self.rest_wire[None], (self.chip, 0, 0, 0))
        return _unpack_rest_full(full.reshape(N_CHIPS, REST_ROWS, PACK_W))

    def swap(self, pg):
        def copies(ins, outs, send_sems, recv_sems):
            (pg_ref,), (out_ref,) = ins, outs
            x, y, c, _ = _place()
            return [_remote(pg_ref.at[j, 1 - c], out_ref.at[j], send_sems, recv_sems, j, (x, y, 1 - c))
                    for j in range(N_CHIPS)]

        def start(*refs):
            for cp in copies(*refs):
                cp.start()

        def wait(*refs):
            for cp in copies(*refs):
                cp.wait()

        return _Riding((pg,), (jax.ShapeDtypeStruct((N_CHIPS,) + pg.shape[2:], pg.dtype),), N_CHIPS, start, wait)

    def scatter(self, group, pg, theirs=None):
        if theirs is None:
            (theirs,) = _swap_halves([pg], "exchange_halves_" + group)
        ps = _pair_sum("sum_pair_" + group, pg, theirs, self.core, self.TILES[group])
        self.sums[group] = ps

        def copies(ins, outs, send_sems, recv_sems):
            (ps_ref,), (out_ref,) = ins, outs
            _, _, c, chips = _place()
            return [_remote(ps_ref.at[2 * cx + cy], out_ref.at[j], send_sems, recv_sems, j, (cx, cy, c))
                    for j, (cx, cy) in enumerate(chips)]

        def start(*refs):
            for cp in copies(*refs):
                cp.start()

        def wait(*refs):
            for cp in copies(*refs):
                cp.wait()

        return _Riding((ps,), (jax.ShapeDtypeStruct((3,) + ps.shape[1:], ps.dtype),), 3, start, wait)

    def reduced(self, group):
        return _chip_sum("sum_chips_" + group, self.sums[group], self.landed[group], self.chip, self.TILES[group])


def _adamw(w, g, m, v):
    m = ADAM_B1 * m + (1.0 - ADAM_B1) * g
    v = ADAM_B2 * v + (1.0 - ADAM_B2) * (g * g)
    m_hat = m / (1.0 - ADAM_B1 ** ADAM_STEP)
    v_hat = v / (1.0 - ADAM_B2 ** ADAM_STEP)
    return -ADAM_LR * (m_hat / (jnp.sqrt(v_hat) + ADAM_EPS) + ADAM_WD * w), m, v


def _small_reduce_adamw(part, w, m, v):
    def body(part_ref, w_ref, m_ref, v_ref, g_ref, d_ref, nm_ref, nv_ref, all_ref, send_sems, recv_sems):
        x, y, c, chips = _place()
        me, sibling = (x, y, c), (x, y, 1 - c)

        def rows(px, py, pc):
            return all_ref.at[4 * px + 2 * py + pc]

        all_ref[4 * x + 2 * y + c] = part_ref[...]
        first = [_remote(part_ref, rows(*me), send_sems, recv_sems, 0, sibling)]
        first += [_remote(part_ref, rows(*me), send_sems, recv_sems, 1 + j, (cx, cy, c))
                  for j, (cx, cy) in enumerate(chips)]
        for cp in first:
            cp.start()
        passed = []
        for j, (cx, cy) in enumerate(chips):
            _remote(part_ref, rows(cx, cy, c), send_sems, recv_sems, 1 + j, me).wait_recv()
            cp = _remote(rows(cx, cy, c), rows(cx, cy, c), send_sems, recv_sems, 4 + j, sibling)
            cp.start()
            passed.append(cp)
        _remote(part_ref, rows(*sibling), send_sems, recv_sems, 0, me).wait_recv()
        for j, (cx, cy) in enumerate(chips):
            _remote(part_ref, rows(cx, cy, 1 - c), send_sems, recv_sems, 4 + j, me).wait_recv()
        for cp in first + passed:
            cp.wait_send()
        g = all_ref[0]
        for dev in range(1, N_DEV):
            g = g + all_ref[dev]
        delta, nm, nv = _adamw(w_ref[...], g, m_ref[...], v_ref[...])
        g_ref[...] = g
        d_ref[...] = delta
        nm_ref[...] = nm
        nv_ref[...] = nv

    whole = pl.BlockSpec(memory_space=pltpu.VMEM)
    shape = jax.ShapeDtypeStruct((SMALL_ROWS, PACK_W), F32)
    return pl.pallas_call(
        body, name="small_reduce_adamw", in_specs=[whole] * 4, out_specs=[whole] * 4, out_shape=[shape] * 4,
        scratch_shapes=[pltpu.VMEM((N_DEV, SMALL_ROWS, PACK_W), F32), pltpu.SemaphoreType.DMA((7,)),
                        pltpu.SemaphoreType.DMA((7,))],
        compiler_params=pltpu.CompilerParams(vmem_limit_bytes=VMEM_LIMIT))(part, w, m, v)


def kernel(x, p, norm_g, w_in, hg_lb, hg_norm_g, w_o_hg, s5_a_re, s5_a_im, s5_log_dt, s5_b_re, s5_b_im, s5_c_re, s5_c_im, s5_d, w_glu, b_glu, w_o_s5, w_out, ple_norm_g, w_ple, w_ple_gate, final_norm_g, loss_target, m_norm_g, m_w_in, m_hg_lb, m_hg_norm_g, m_w_o_hg, m_s5_a_re, m_s5_a_im, m_s5_log_dt, m_s5_b_re, m_s5_b_im, m_s5_c_re, m_s5_c_im, m_s5_d, m_w_glu, m_b_glu, m_w_o_s5, m_w_out, m_ple_norm_g, m_w_ple, m_w_ple_gate, m_final_norm_g, v_norm_g, v_w_in, v_hg_lb, v_hg_norm_g, v_w_o_hg, v_s5_a_re, v_s5_a_im, v_s5_log_dt, v_s5_b_re, v_s5_b_im, v_s5_c_re, v_s5_c_im, v_s5_d, v_w_glu, v_b_glu, v_w_o_s5, v_w_out, v_ple_norm_g, v_w_ple, v_w_ple_gate, v_final_norm_g):
    given = dict(locals())
    wts = {n: given[n] for n in WEIGHTS}
    mom = {n: given["m_" + n] for n in WEIGHTS}
    var = {n: given["v_" + n] for n in WEIGHTS}
    cx, cy, cc = lax.axis_index("x"), lax.axis_index("y"), lax.axis_index("c")
    chip = (2 * cx + cy).astype(jnp.int32)

    core = cc.astype(jnp.int32)
    rest_shard = _pack_rest({n: wts[n][0] for n in REST})
    comm = _StepComm(wts["w_in"][0].astype(MXU_DTYPE).reshape(2, D_MODEL // 2, IN_SHARD),
                     rest_shard.astype(MXU_DTYPE).reshape(2, REST_ROWS // 2, PACK_W), chip, core)

    t_len = x.shape[1]
    loss_row, grad_x, g_big, g_small = _local_step(x.reshape(t_len, D_MODEL), p.reshape(t_len, -1),
                                                   loss_target.reshape(t_len, D_MODEL), None,
                                                   {n: wts[n] for n in SMALL}, comm)

    zero = jnp.zeros((), F32)
    sg, sd, snm, snv = _small_reduce_adamw(_pack_small(g_small, loss_row[0, 0]),
                                           _pack_small({n: wts[n] for n in SMALL}, zero),
                                           _pack_small({n: mom[n] for n in SMALL}, zero),
                                           _pack_small({n: var[n] for n in SMALL}, zero))
    (sg, loss), (sd, _), (snm, _), (snv, _) = (_unpack_small(a) for a in (sg, sd, snm, snv))

    halves = [comm.reduced("in"), comm.reduced("rest")]
    g_in, g_rest = [lax.dynamic_update_slice(got, mine[None], (core, 0, 0))
                    for got, mine in zip(_share_halves(halves), halves)]
    g_in, g_rest = g_in.reshape(D_MODEL, IN_SHARD), g_rest.reshape(REST_ROWS, PACK_W)

    def adam_f(wv, gv, mv, vv):
        return _adamw(wv, gv, mv, vv)

    d_in, nm_in, nv_in = _rowwise("adamw_in", adam_f, D_MODEL, IN_TILE,
                                  [(wts["w_in"][0], IN_SHARD, 0), (g_in, IN_SHARD, 0), (mom["w_in"][0], IN_SHARD, 0),
                                   (var["w_in"][0], IN_SHARD, 0)], [], [(IN_SHARD, F32)] * 3)
    d_rest, nm_rest, nv_rest = _rowwise("adamw_rest", adam_f, REST_ROWS, REST_TILE,
                                        [(rest_shard, PACK_W, 0), (g_rest, PACK_W, 0),
                                         (_pack_rest({n: mom[n][0] for n in REST}), PACK_W, 0),
                                         (_pack_rest({n: var[n][0] for n in REST}), PACK_W, 0)], [],
                                        [(PACK_W, F32)] * 3)
    bg, bd, bnm, bnv = (dict(_unpack_rest(rest), w_in=a.reshape(1, D_MODEL, IN_SHARD))
                        for rest, a in ((g_rest, g_in), (d_rest, d_in), (nm_rest, nm_in), (nv_rest, nv_in)))

    outs = [loss, grad_x.reshape(x.shape)]
    for small, big in ((sg, bg), (sd, bd), (snm, bnm), (snv, bnv)):
        outs += [big[n] if n in BIG else small[n] for n in WEIGHTS]
    return tuple(outs)
```

```python
import functools
from typing import Callable, NamedTuple

import jax
import jax.numpy as jnp
from jax import lax
from jax.experimental import pallas as pl
from jax.experimental.pallas import tpu as pltpu

F32 = jnp.float32
MXU_DTYPE = jnp.bfloat16
WIRE_DTYPE = jnp.bfloat16
NORM_EPS = 1e-6
D_MODEL = 1024
HG_HEADS = 8
HG_DIM = 128
HG_CHUNK = 64
S5_WIDTH = 512
S5_GROUPS = 32
S5_GROUP = 16
S5_STATE = 64
S5_LANES = S5_GROUPS * S5_STATE
IN_COLS = 7168
SUBLANES = 8
VMEM_LIMIT = 56 * 1024 * 1024
HIGHEST = lax.Precision.HIGHEST
MESH = pl.DeviceIdType.MESH

ADAM_LR, ADAM_B1, ADAM_B2, ADAM_EPS, ADAM_WD, ADAM_STEP = 0.001, 0.9, 0.999, 1e-08, 0.01, 10

BIG = ("w_in", "w_o_hg", "w_glu", "w_o_s5", "w_out", "w_ple", "w_ple_gate")
BIG_SHAPE = {"w_in": (1024, 7168), "w_o_hg": (1024, 1024), "w_glu": (512, 1024), "w_o_s5": (512, 1024),
             "w_out": (1024, 1024), "w_ple": (256, 1024), "w_ple_gate": (1024, 1024)}
BIG_COL_SHARDED = ("w_in", "w_glu", "w_o_s5", "w_ple")
SMALL = ("norm_g", "hg_lb", "hg_norm_g", "s5_a_re", "s5_a_im", "s5_log_dt", "s5_b_re", "s5_b_im", "s5_c_re",
         "s5_c_im", "s5_d", "b_glu", "ple_norm_g", "final_norm_g")
SMALL_SHAPE = {"norm_g": (1, 1024), "hg_lb": (2, 1024), "hg_norm_g": (1, 1024), "s5_a_re": (1, 32, 64),
               "s5_a_im": (1, 32, 64), "s5_log_dt": (1, 32), "s5_b_re": (1, 32, 64, 16), "s5_b_im": (1, 32, 64, 16),
               "s5_c_re": (1, 32, 16, 64), "s5_c_im": (1, 32, 16, 64), "s5_d": (1, 32, 16), "b_glu": (1, 1024),
               "ple_norm_g": (1, 1024), "final_norm_g": (1024,)}
WEIGHTS = ("norm_g", "w_in", "hg_lb", "hg_norm_g", "w_o_hg", "s5_a_re", "s5_a_im", "s5_log_dt", "s5_b_re", "s5_b_im",
           "s5_c_re", "s5_c_im", "s5_d", "w_glu", "b_glu", "w_o_s5", "w_out", "ple_norm_g", "w_ple", "w_ple_gate",
           "final_norm_g")
N_CHIPS = 4
N_DEV = 8
PACK_W = 1024
SMALL_ROWS = 144


def _params(*sem):
    return pltpu.CompilerParams(dimension_semantics=sem, vmem_limit_bytes=VMEM_LIMIT)


def _sig(x):
    return 1.0 / (1.0 + jnp.exp(-x))


def _dsilu(z, s):
    return s * (1.0 + z * (1.0 - s))


def _mx(x):
    return x.astype(MXU_DTYPE)


def _dot(a, b, dims=(((1,), (0,)), ((), ()))):
    return lax.dot_general(_mx(a), _mx(b), dims, preferred_element_type=F32)


_NT = (((1,), (1,)), ((), ()))
_TN = (((0,), (0,)), ((), ()))


def _dot32(a, b):
    return jnp.dot(a, b, precision=HIGHEST, preferred_element_type=F32)


def _rms_bwd(dy, x, g):
    r = lax.rsqrt(jnp.mean(x * x, axis=-1, keepdims=True) + NORM_EPS)
    t = dy * g
    dx = r * t - x * (r * r * r) * jnp.mean(t * x, axis=-1, keepdims=True)
    return dx, jnp.sum(dy * x * r, axis=0, keepdims=True)


def _rowwise(name, fn, n_rows_total, tm, rows, consts, outs, accs=(), alias=None):
    n_r, n_c, n_o, n_a = len(rows), len(consts), len(outs), len(accs)

    def body(*refs):
        row_refs = refs[:n_r]
        const_refs = refs[n_r:n_r + n_c]
        pos = n_r + n_c + (1 if alias is not None else 0)
        out_refs = refs[pos:pos + n_o]
        acc_refs = refs[pos + n_o:pos + n_o + n_a]
        res = fn(*[r[...] for r in row_refs], *[r[...] for r in const_refs])
        for r, v in zip(out_refs, res[:n_o]):
            r[...] = v.astype(r.dtype)
        if n_a:
            @pl.when(pl.program_id(0) == 0)
            def _():
                for r in acc_refs:
                    r[...] = jnp.zeros_like(r)
            for r, v in zip(acc_refs, res[n_o:]):
                r[...] += v

    in_specs = [pl.BlockSpec((tm, w), functools.partial(lambda i, cb: (i, cb), cb=cb)) for (_, w, cb) in rows]
    in_specs += [pl.BlockSpec(c.shape, lambda i: (0, 0)) for c in consts]
    args = [a for (a, _, _) in rows] + list(consts)
    out_shape, out_specs = [], []
    for o in outs:
        w, dt = o[0], o[1]
        cb, total = (o[2], o[3]) if len(o) == 4 else (0, w)
        out_shape.append(jax.ShapeDtypeStruct((n_rows_total, total), dt))
        out_specs.append(pl.BlockSpec((tm, w), functools.partial(lambda i, cb: (i, cb), cb=cb)))
    io_alias = {}
    if alias is not None:
        in_specs.append(pl.BlockSpec(memory_space=pl.ANY))
        args.append(alias[0])
        io_alias = {len(args) - 1: alias[1]}
    for (r, w) in accs:
        out_shape.append(jax.ShapeDtypeStruct((r, w), F32))
        out_specs.append(pl.BlockSpec((r, w), lambda i: (0, 0)))
    res = pl.pallas_call(body, name=name, grid=(n_rows_total // tm,), in_specs=in_specs, out_specs=out_specs,
                         out_shape=out_shape, input_output_aliases=io_alias,
                         compiler_params=_params("arbitrary"))(*args)
    return res


class _Riding(NamedTuple):
    ins: tuple
    outs: tuple
    n_sems: int
    start: Callable
    wait: Callable


_HBM = pl.BlockSpec(memory_space=pl.ANY)


def _ride(riding, refs, n_in, n_out, n_scratch, first, last):
    if riding is None:
        return refs[:n_in], refs[n_in:n_in + n_out], refs[n_in + n_out:]
    r_in, r_out = len(riding.ins), len(riding.outs)
    ins, rins = refs[:n_in], refs[n_in:n_in + r_in]
    pos = n_in + r_in
    outs, routs = refs[pos:pos + n_out], refs[pos + n_out:pos + n_out + r_out]
    pos += n_out + r_out
    scratch, (send_sems, recv_sems) = refs[pos:pos + n_scratch], refs[pos + n_scratch:]

    @pl.when(first)
    def _():
        riding.start(rins, routs, send_sems, recv_sems)

    @pl.when(last)
    def _():
        riding.wait(rins, routs, send_sems, recv_sems)

    return ins, outs, scratch


def _riding_call(riding, body, name, grid, in_specs, args, out_specs, out_shape, scratch, io_alias=None):
    if riding is not None:
        in_specs = list(in_specs) + [_HBM] * len(riding.ins)
        args = list(args) + list(riding.ins)
        out_specs = list(out_specs) + [_HBM] * len(riding.outs)
        out_shape = list(out_shape) + list(riding.outs)
        scratch = list(scratch) + [pltpu.SemaphoreType.DMA((riding.n_sems,))] * 2
    return pl.pallas_call(body, name=name, grid=grid, in_specs=in_specs, out_specs=out_specs, out_shape=out_shape,
                          scratch_shapes=scratch, input_output_aliases=io_alias or {},
                          compiler_params=_params(*(["arbitrary"] * len(grid))))(*args)


def _mm_nn(name, a, b, tm, tn, riding=None, prologue=None, consts=()):
    m, k = a.shape
    n = b.shape[1] if b.ndim == 2 else b.shape[0] * b.shape[2]
    grid = (n // tn, m // tm)
    n_out, scratch = (1, []) if prologue is None else (2, [pltpu.VMEM((m, k), MXU_DTYPE)])

    def body(*refs):
        j, i = pl.program_id(0), pl.program_id(1)
        ins, outs, kept = _ride(riding, refs, 2 + len(consts), n_out, len(scratch), (j == 0) & (i == 0),
                                (j == grid[0] - 1) & (i == grid[1] - 1))
        if prologue is None:
            left = ins[0][...]
        else:
            rows = pl.ds(pl.multiple_of(i * tm, tm), tm)

            @pl.when(j == 0)
            def _():
                tile = _mx(prologue(ins[0][...], *[c[...] for c in ins[2:]]))
                kept[0][rows, :] = tile
                outs[1][...] = tile

            left = kept[0][rows, :]
        outs[0][...] = _dot(left, ins[1][...])

    once = (lambda j, i: (i, 0)) if prologue is None else (lambda j, i: (jnp.where(j == 0, i, grid[1] - 1), 0))
    b_spec = (pl.BlockSpec((k, tn), lambda j, i: (0, j)) if b.ndim == 2
              else pl.BlockSpec((None, k, tn), lambda j, i: (j, 0, 0)))
    in_specs = [pl.BlockSpec((tm, k), once), b_spec]
    in_specs += [pl.BlockSpec(c.shape, lambda j, i: (0, 0)) for c in consts]
    out_specs = [pl.BlockSpec((tm, tn), lambda j, i: (i, j))]
    out_shape = [jax.ShapeDtypeStruct((m, n), F32)]
    if prologue is not None:
        out_specs.append(pl.BlockSpec((tm, k), once))
        out_shape.append(jax.ShapeDtypeStruct((m, k), MXU_DTYPE))
    res = _riding_call(riding, body, name, grid, in_specs, [a, b] + list(consts), out_specs, out_shape, scratch)
    return res[0] if riding is None and prologue is None else res


def _mm_nt_then(name, a, b, tm, tn, fn, rows, consts, outs, accs=(), alias=None, riding=None):
    m, n = a.shape
    k = b.shape[-2]
    steps = n // tn
    n_r, n_c, n_o, n_a = len(rows), len(consts), len(outs), len(accs)

    def body(*refs):
        a_ref, b_ref = refs[:2]
        row_refs = refs[2:2 + n_r]
        const_refs = refs[2 + n_r:2 + n_r + n_c]
        i, s = pl.program_id(0), pl.program_id(1)
        n_in = 2 + n_r + n_c + (1 if alias is not None else 0)
        _, outs_, (mm_ref,) = _ride(riding, refs, n_in, n_o + n_a, 1, (i == 0) & (s == 0),
                                    (i == m // tm - 1) & (s == steps - 1))
        out_refs, acc_refs = outs_[:n_o], outs_[n_o:]
        part = _dot(a_ref[...], b_ref[...] if b.ndim == 2 else b_ref[s], _NT)
        if steps > 1:
            @pl.when(s == 0)
            def _():
                mm_ref[...] = jnp.zeros_like(mm_ref)
            mm_ref[...] += part

        @pl.when(s == steps - 1)
        def _():
            res = fn(mm_ref[...] if steps > 1 else part, *[r[...] for r in row_refs], *[r[...] for r in const_refs])
            for r, v in zip(out_refs, res[:n_o]):
                r[...] = v.astype(r.dtype)
            if n_a:
                @pl.when(i == 0)
                def _():
                    for r in acc_refs:
                        r[...] = jnp.zeros_like(r)
                for r, v in zip(acc_refs, res[n_o:]):
                    r[...] += v

    b_spec = (pl.BlockSpec((k, tn), lambda i, s: (0, s)) if b.ndim == 2
              else pl.BlockSpec(memory_space=pltpu.VMEM))
    in_specs = [pl.BlockSpec((tm, tn), lambda i, s: (i, s)), b_spec]
    in_specs += [pl.BlockSpec((tm, w), functools.partial(lambda i, s, cb: (i, cb), cb=cb)) for (_, w, cb) in rows]
    in_specs += [pl.BlockSpec(c.shape, lambda i, s: (0, 0)) for c in consts]
    args = [a, b] + [r[0] for r in rows] + list(consts)
    out_shape, out_specs = [], []
    for o in outs:
        w, dt = o[0], o[1]
        cb, total = (o[2], o[3]) if len(o) == 4 else (0, w)
        out_shape.append(jax.ShapeDtypeStruct((m, total), dt))
        out_specs.append(pl.BlockSpec((tm, w), functools.partial(lambda i, s, cb: (i, cb), cb=cb)))
    io_alias = {}
    if alias is not None:
        in_specs.append(pl.BlockSpec(memory_space=pl.ANY))
        args.append(alias[0])
        io_alias = {len(args) - 1: alias[1]}
    for (r, w) in accs:
        out_shape.append(jax.ShapeDtypeStruct((r, w), F32))
        out_specs.append(pl.BlockSpec((r, w), lambda i, s: (0, 0)))
    return _riding_call(riding, body, name, (m // tm, steps), in_specs, args, out_specs, out_shape,
                        [pltpu.VMEM((tm, k), F32)], io_alias)


def _mm_tn(name, a, b, tk, tn, col_shards=False, riding=None):
    t, k = a.shape
    n = b.shape[1]
    steps = t // tk

    def body(*refs):
        j, s = pl.program_id(0), pl.program_id(1)
        (a_ref, b_ref), (o_ref,), (acc_ref,) = _ride(riding, refs, 2, 1, 1, (j == 0) & (s == 0),
                                                     (j == n // tn - 1) & (s == steps - 1))

        @pl.when(s == 0)
        def _():
            acc_ref[...] = jnp.zeros_like(acc_ref)

        acc_ref[...] += _dot(a_ref[...], b_ref[...], _TN)

        @pl.when(s == steps - 1)
        def _():
            o_ref[...] = acc_ref[...]

    if col_shards:
        out_spec = pl.BlockSpec((None, k, tn), lambda j, s: (j, 0, 0))
        out_shape = jax.ShapeDtypeStruct((n // tn, k, tn), F32)
    else:
        out_spec = pl.BlockSpec((k, tn), lambda j, s: (0, j))
        out_shape = jax.ShapeDtypeStruct((k, n), F32)
    res = _riding_call(riding, body, name, (n // tn, steps),
                       [pl.BlockSpec((tk, k), lambda j, s: (s, 0)), pl.BlockSpec((tk, tn), lambda j, s: (s, j))],
                       [a, b], [out_spec], [out_shape], [pltpu.VMEM((k, tn), F32)])
    return res[0] if riding is None else res


def _dot01(m01, x):
    m = m01.astype(MXU_DTYPE)
    hi = x.astype(MXU_DTYPE)
    r1 = x - hi.astype(F32)
    mid = r1.astype(MXU_DTYPE)
    lo = (r1 - mid.astype(F32)).astype(MXU_DTYPE)
    dot = lambda v: jnp.dot(m, v, preferred_element_type=F32)
    return dot(hi) + dot(mid) + dot(lo)


def _chunk_rows(x, offset, nck):
    return jnp.concatenate([jnp.broadcast_to(x[c * HG_CHUNK + offset:c * HG_CHUNK + offset + 1, :],
                                             (HG_CHUNK, x.shape[1])) for c in range(nck)], axis=0)


def _hg_block_terms(q, f, lb, tb):
    nck = tb // HG_CHUNK
    sig = _sig(f)
    fv = lb + (1.0 - lb) * sig
    kk = (1.0 - lb) * (1.0 - sig)
    row = lax.broadcasted_iota(jnp.int32, (tb, tb), 0)
    col = lax.broadcasted_iota(jnp.int32, (tb, tb), 1)
    same = jnp.right_shift(row, 6) == jnp.right_shift(col, 6)
    causal, anti = same & (row >= col), same & (row <= col)
    b = _dot01(causal, jnp.log(fv))
    b_mid, b_last = _chunk_rows(b, HG_CHUNK // 2 - 1, nck), _chunk_rows(b, HG_CHUNK - 1, nck)
    e_mid, e_mid_inv = jnp.exp(b - b_mid), jnp.exp(b_mid - b)
    e_b, e_last = jnp.exp(b), jnp.exp(b_last - b)
    dcs = [jnp.exp(b[c * HG_CHUNK + HG_CHUNK - 1:(c + 1) * HG_CHUNK, :]) for c in range(nck)]
    return sig, fv, kk, causal, anti, e_mid, e_mid_inv, e_b, e_last, dcs


def _hgrn2_fwd(proj, hg_lb, hg_norm_g, t_len, tb, riding=None):
    nck = tb // HG_CHUNK
    nb = t_len // tb

    def body(*refs):
        step = pl.program_id(0)
        ((p_ref, lb_ref, gn_ref), (o_ref, act_ref, sp_ref),
         (st_ref, a_s, bm_s, qd_s, kd_s, v_s, sc_s, inc_s)) = _ride(riding, refs, 3, 3, 8, step == 0, step == nb - 1)

        @pl.when(pl.program_id(0) == 0)
        def _():
            st_ref[...] = jnp.zeros_like(st_ref)

        lb = _sig(lb_ref[0:1, :] - lb_ref[1:2, :])
        q = p_ref[:, pl.ds(0, 1024)]
        _, _, kk, causal, _, e_mid, e_mid_inv, e_b, e_last, dcs = _hg_block_terms(q, p_ref[:, pl.ds(1024, 1024)],
                                                                                   lb, tb)
        a_s[...] = _mx(q * e_mid)
        bm_s[...] = _mx(kk * e_mid_inv)
        qd_s[...] = _mx(q * e_b)
        kd_s[...] = _mx(kk * e_last)
        v_s[...] = _mx(p_ref[:, pl.ds(2048, 1024)])
        heads = [pl.ds(h * HG_DIM, HG_DIM) for h in range(HG_HEADS)]
        chunks = [pl.ds(c * HG_CHUNK, HG_CHUNK) for c in range(nck)]
        for h, hs in enumerate(heads):
            sc_s[h] = _mx(jnp.where(causal, _dot(a_s[:, hs], bm_s[:, hs], _NT), 0.0))
        for h, hs in enumerate(heads):
            o_ref[:, hs] = _dot(sc_s[h], v_s[:, hs])
        for h, hs in enumerate(heads):
            for c, r in enumerate(chunks):
                inc_s[h, c] = _dot(v_s[r, hs], kd_s[r, hs], _TN)
        for c in range(nck):
            for h in range(HG_HEADS):
                st = st_ref[h]
                sp_ref[h, c] = st
                st_ref[h] = dcs[c][:, h * HG_DIM:(h + 1) * HG_DIM] * st + inc_s[h, c]
        for c, r in enumerate(chunks):
            for h, hs in enumerate(heads):
                o_ref[r, hs] += _dot(qd_s[r, hs], sp_ref[h, c], _NT)
        for h, hs in enumerate(heads):
            o = o_ref[:, hs]
            rr = lax.rsqrt(jnp.mean(o * o, axis=-1, keepdims=True) + NORM_EPS)
            g = p_ref[:, pl.ds(3072 + h * HG_DIM, HG_DIM)]
            act_ref[:, hs] = (o * rr * gn_ref[:, hs] * (g * _sig(g))).astype(act_ref.dtype)

    return _riding_call(
        riding, body, "hgrn2_fwd", (nb,),
        [pl.BlockSpec((tb, 4096), lambda i: (i, 0)), pl.BlockSpec((2, 1024), lambda i: (0, 0)),
         pl.BlockSpec((1, 1024), lambda i: (0, 0))],
        [proj, hg_lb, hg_norm_g],
        [pl.BlockSpec((tb, 1024), lambda i: (i, 0)), pl.BlockSpec((tb, 1024), lambda i: (i, 0)),
         pl.BlockSpec((HG_HEADS, nck, HG_DIM, HG_DIM), lambda i: (0, i, 0, 0))],
        [jax.ShapeDtypeStruct((t_len, 1024), F32), jax.ShapeDtypeStruct((t_len, 1024), MXU_DTYPE),
         jax.ShapeDtypeStruct((HG_HEADS, t_len // HG_CHUNK, HG_DIM, HG_DIM), F32)],
        [pltpu.VMEM((HG_HEADS, HG_DIM, HG_DIM), F32)] + [pltpu.VMEM((tb, 1024), MXU_DTYPE)] * 5
        + [pltpu.VMEM((HG_HEADS, tb, tb), MXU_DTYPE), pltpu.VMEM((HG_HEADS, nck, HG_DIM, HG_DIM), F32)])


def _hgrn2_bwd(proj, d_o, s_prev, hg_lb, dproj, t_len, tb, riding=None):
    nck = tb // HG_CHUNK
    nb = t_len // tb

    def body(*refs):
        step = pl.program_id(0)
        ((p_ref, do_ref, sp_ref, lb_ref, _), (dp_ref, dlb_ref),
         (ds_ref, acc_ref, a_s, bm_s, qd_s, kd_s, v_s, do_s, da_s, dbm_s, dqd_s, dkd_s, dv_s, ex_s, sc_s, dsc_s,
          up_s)) = _ride(riding, refs, 5, 2, 17, step == 0, step == nb - 1)

        @pl.when(pl.program_id(0) == 0)
        def _():
            ds_ref[...] = jnp.zeros_like(ds_ref)
            acc_ref[...] = jnp.zeros_like(acc_ref)

        lb = _sig(lb_ref[0:1, :] - lb_ref[1:2, :])
        q = p_ref[:, pl.ds(0, 1024)]
        sig, fv, kk, causal, anti, e_mid, e_mid_inv, e_b, e_last, dcs = _hg_block_terms(
            q, p_ref[:, pl.ds(1024, 1024)], lb, tb)
        a, bm, qd, kd = q * e_mid, kk * e_mid_inv, q * e_b, kk * e_last
        a_s[...] = _mx(a)
        bm_s[...] = _mx(bm)
        qd_s[...] = _mx(qd)
        kd_s[...] = _mx(kd)
        v_s[...] = _mx(p_ref[:, pl.ds(2048, 1024)])
        do_s[...] = _mx(do_ref[...])
        heads = [pl.ds(h * HG_DIM, HG_DIM) for h in range(HG_HEADS)]
        chunks = [pl.ds(c * HG_CHUNK, HG_CHUNK) for c in range(nck)]
        for h, hs in enumerate(heads):
            sc_s[h] = _mx(jnp.where(causal, _dot(a_s[:, hs], bm_s[:, hs], _NT), 0.0))
            dsc_s[h] = _mx(jnp.where(causal, _dot(do_s[:, hs], v_s[:, hs], _NT), 0.0))
        for h, hs in enumerate(heads):
            dv_s[:, hs] = _dot(sc_s[h], do_s[:, hs], _TN)
            da_s[:, hs] = _dot(dsc_s[h], bm_s[:, hs])
            dbm_s[:, hs] = _dot(dsc_s[h], a_s[:, hs], _TN)
        for h, hs in enumerate(heads):
            for c, r in enumerate(chunks):
                up_s[h, c] = _dot(do_s[r, hs], qd_s[r, hs], _TN)
                dqd_s[r, hs] = _dot(do_s[r, hs], sp_ref[h, c])
        for c in reversed(range(nck)):
            r = chunks[c]
            for h, hs in enumerate(heads):
                dst = ds_ref[h]
                dc = dcs[c][:, h * HG_DIM:(h + 1) * HG_DIM]
                dv_s[r, hs] += _dot(kd_s[r, hs], dst, _NT)
                dkd_s[r, hs] = _dot(v_s[r, hs], dst)
                ex_s[c:c + 1, hs] = jnp.sum(dst * sp_ref[h, c], axis=0, keepdims=True) * dc
                ds_ref[h] = up_s[h, c] + dc * dst
        da, dbm, dqd, dkd = da_s[...], dbm_s[...], dqd_s[...], dkd_s[...]
        dq = da * e_mid + dqd * e_b
        dk = dbm * e_mid_inv + dkd * e_last
        db = da * a - dbm * bm + dqd * qd - dkd * kd
        dkk = dkd * kd
        extra = jnp.concatenate(
            [jnp.broadcast_to(jnp.sum(dkk[c * HG_CHUNK:(c + 1) * HG_CHUNK], axis=0, keepdims=True)
                              + ex_s[c:c + 1, :], (HG_CHUNK, 1024)) for c in range(nck)], axis=0)
        dlogf = _dot01(anti, db) + extra
        dfv_k = dlogf / fv - dk
        dp_ref[:, pl.ds(0, 1024)] = dq.astype(dp_ref.dtype)
        dp_ref[:, pl.ds(1024, 1024)] = (dfv_k * (1.0 - lb) * sig * (1.0 - sig)).astype(dp_ref.dtype)
        dp_ref[:, pl.ds(2048, 1024)] = dv_s[...].astype(dp_ref.dtype)
        acc_ref[...] += jnp.sum(dfv_k * (1.0 - sig), axis=0, keepdims=True)

        @pl.when(pl.program_id(0) == nb - 1)
        def _():
            g0 = acc_ref[...] * lb * (1.0 - lb)
            dlb_ref[0:1, :] = g0
            dlb_ref[1:2, :] = -g0

    return _riding_call(
        riding, body, "hgrn2_bwd", (nb,),
        [pl.BlockSpec((tb, 3072), lambda i: (nb - 1 - i, 0)),
         pl.BlockSpec((tb, 1024), lambda i: (nb - 1 - i, 0)),
         pl.BlockSpec((HG_HEADS, nck, HG_DIM, HG_DIM), lambda i: (0, nb - 1 - i, 0, 0)),
         pl.BlockSpec((2, 1024), lambda i: (0, 0)),
         pl.BlockSpec(memory_space=pl.ANY)],
        [proj, d_o, s_prev, hg_lb, dproj],
        [pl.BlockSpec((tb, 3072), lambda i: (nb - 1 - i, 0)), pl.BlockSpec((2, 1024), lambda i: (0, 0))],
        [jax.ShapeDtypeStruct((t_len, IN_COLS), dproj.dtype), jax.ShapeDtypeStruct((2, 1024), F32)],
        [pltpu.VMEM((HG_HEADS, HG_DIM, HG_DIM), F32), pltpu.VMEM((1, 1024), F32)]
        + [pltpu.VMEM((tb, 1024), MXU_DTYPE)] * 6 + [pltpu.VMEM((tb, 1024), F32)] * 5
        + [pltpu.VMEM((SUBLANES, 1024), F32)] + [pltpu.VMEM((HG_HEADS, tb, tb), MXU_DTYPE)] * 2
        + [pltpu.VMEM((HG_HEADS, nck, HG_DIM, HG_DIM), F32)], {4: 0})


def _s5_prep_bwd(a_re, a_im, log_dt, b_re_t, b_im_t, dlam, dbbr, dbbi):
    def body(ar_ref, ai_ref, ldt_ref, br_ref, bi_ref, dlam_ref, dbbr_ref, dbbi_ref,
             dar_ref, dai_ref, dldt_ref, dbr_ref, dbi_ref):
        ar, ai = ar_ref[...], ai_ref[...]
        dt = jnp.exp(ldt_ref[...])
        mag = jnp.exp(ar * dt)
        cs, sn = jnp.cos(ai * dt), jnp.sin(ai * dt)
        lr, li = mag * cs, mag * sn
        den = ar * ar + ai * ai
        nr = lr - 1.0
        sr = (nr * ar + li * ai) / den
        si = (li * ar - nr * ai) / den
        br, bi = br_ref[...], bi_ref[...]
        gbr, gbi = dbbr_ref[...], dbbi_ref[...]
        dbr_ref[...] = sr * gbr + si * gbi
        dbi_ref[...] = sr * gbi - si * gbr
        dsr = jnp.sum(gbr * br + gbi * bi, axis=0, keepdims=True)
        dsi = jnp.sum(gbi * br - gbr * bi, axis=0, keepdims=True)
        dnr = (dsr * ar - dsi * ai) / den
        dli = dlam_ref[1:2, :] + (dsr * ai + dsi * ar) / den
        dlr = dlam_ref[0:1, :] + dnr
        dden = -(dsr * sr + dsi * si) / den
        dar = (dsr * nr + dsi * li) / den + dden * 2.0 * ar
        dai = (dsr * li - dsi * nr) / den + dden * 2.0 * ai
        dmag = dlr * cs + dli * sn
        dth = mag * (dli * cs - dlr * sn)
        dar_ref[...] = dar + dmag * mag * dt
        dai_ref[...] = dai + dth * dt
        ddt = (dmag * mag * ar + dth * ai) * dt
        lane = lax.broadcasted_iota(jnp.int32, (S5_LANES, 128), 0) // S5_STATE
        grp = lax.broadcasted_iota(jnp.int32, (S5_LANES, 128), 1)
        dldt_ref[...] = _dot32(jnp.broadcast_to(ddt, (SUBLANES, S5_LANES)), (lane == grp).astype(F32))

    whole = pl.BlockSpec(memory_space=pltpu.VMEM)
    return pl.pallas_call(
        body, name="s5_prep_bwd", in_specs=[whole] * 8, out_specs=[whole] * 5,
        out_shape=[jax.ShapeDtypeStruct((1, S5_LANES), F32), jax.ShapeDtypeStruct((1, S5_LANES), F32),
                   jax.ShapeDtypeStruct((SUBLANES, 128), F32), jax.ShapeDtypeStruct((S5_GROUP, S5_LANES), F32),
                   jax.ShapeDtypeStruct((S5_GROUP, S5_LANES), F32)])(a_re, a_im, log_dt, b_re_t, b_im_t, dlam, dbbr,
                                                                      dbbi)


def _dgelu(x):
    c, a = 0.7978845608028654, 0.044715
    th = jnp.tanh(c * (x + a * x * x * x))
    return 0.5 * (1.0 + th) + 0.5 * x * (1.0 - th * th) * c * (1.0 + 3.0 * a * x * x)


S5_BLOCKS = 4
S5_BW = S5_WIDTH // S5_BLOCKS
S5_BL = S5_LANES // S5_BLOCKS
S5_LANE_BLOCKS = S5_LANES // 128
S5_SCAN_BLOCKS = 4


def _s5_prep(a_re, a_im, log_dt, b_re_t, b_im_t, seg):
    def body(ar_ref, ai_ref, ldt_ref, br_ref, bi_ref,
             rows_f, pfr_ref, pfi_ref, rows_r, prr_ref, pri_ref, bbr_ref, bbi_ref):
        ar, ai = ar_ref[...], ai_ref[...]
        dt = jnp.exp(ldt_ref[...])
        mag = jnp.exp(ar * dt)
        lr, li = mag * jnp.cos(ai * dt), mag * jnp.sin(ai * dt)
        den = ar * ar + ai * ai
        nr = lr - 1.0
        sr = (nr * ar + li * ai) / den
        si = (li * ar - nr * ai) / den
        wide = (SUBLANES, S5_LANES)
        cr, ci = lr, li
        for i in range(seg):
            pfr_ref[i] = jnp.broadcast_to(cr, wide)
            pfi_ref[i] = jnp.broadcast_to(ci, wide)
            prr_ref[seg - 1 - i] = jnp.broadcast_to(cr, wide)
            pri_ref[seg - 1 - i] = jnp.broadcast_to(-ci, wide)
            if i == seg - 1:
                for rows, sign in ((rows_f, 1.0), (rows_r, -1.0)):
                    rows[0:1, :] = lr
                    rows[1:2, :] = sign * li
                    rows[2:3, :] = cr
                    rows[3:4, :] = sign * ci
            cr, ci = cr * lr - ci * li, cr * li + ci * lr
        br, bi = br_ref[...], bi_ref[...]
        bbr_ref[...] = sr * br - si * bi
        bbi_ref[...] = sr * bi + si * br

    whole = pl.BlockSpec(memory_space=pltpu.VMEM)
    tables = [jax.ShapeDtypeStruct((4, S5_LANES), F32)] + [jax.ShapeDtypeStruct((seg, SUBLANES, S5_LANES), F32)] * 2
    bbar = [jax.ShapeDtypeStruct((S5_GROUP, S5_LANES), F32)] * 2
    res = pl.pallas_call(body, name="s5_prep", in_specs=[whole] * 5, out_specs=[whole] * 8,
                         out_shape=tables + tables + bbar)(a_re, a_im, log_dt, b_re_t, b_im_t)
    return res[0:3], res[3:6], res[6], res[7]


def _lanes(j):
    return pl.ds(j * 128, 128)


def _to_segment_order(v, stage_ref, out_ref, seg):
    nbl = v.shape[1] // 128
    for b in range(nbl):
        stage_ref[b] = v[:, b * 128:(b + 1) * 128]

    def body(t, carry):
        rows = pl.ds(pl.multiple_of(t * SUBLANES, SUBLANES), SUBLANES)
        for b in range(nbl):
            out_ref[rows, _lanes(b)] = stage_ref[b, pl.ds(t, SUBLANES, stride=seg), :]
        return carry

    lax.fori_loop(0, seg, body, 0, unroll=True)


def _from_segment_order(v, stage_ref, out_ref, seg):
    nbl = v.shape[1] // 128
    for b in range(nbl):
        stage_ref[b] = v[:, b * 128:(b + 1) * 128]
    for s in range(SUBLANES):
        def body(k, carry, s=s):
            rows = pl.ds(pl.multiple_of(s * seg + k * SUBLANES, SUBLANES), SUBLANES)
            for b in range(nbl):
                out_ref[rows, _lanes(b)] = stage_ref[b, pl.ds(k * SUBLANES * SUBLANES + s, SUBLANES,
                                                              stride=SUBLANES), :]
            return carry

        lax.fori_loop(0, seg // SUBLANES, body, 0, unroll=True)


def _tile_scan(xr_ref, xi_ref, lam_ref, car_ref, cai_ref, cn_r, cn_i, blocks, seg, reverse):
    shape = (SUBLANES, 128)
    lrs = [jnp.broadcast_to(lam_ref[0:1, _lanes(j)], shape) for j in blocks]
    lis = [jnp.broadcast_to(lam_ref[1:2, _lanes(j)], shape) for j in blocks]

    def step(k, carry):
        t = seg - 1 - k if reverse else k
        rows = pl.ds(pl.multiple_of(t * SUBLANES, SUBLANES), SUBLANES)
        out = []
        for n, j in enumerate(blocks):
            cr, ci = carry[2 * n], carry[2 * n + 1]
            nr = lrs[n] * cr - lis[n] * ci + xr_ref[rows, _lanes(j)]
            ni = lrs[n] * ci + lis[n] * cr + xi_ref[rows, _lanes(j)]
            xr_ref[rows, _lanes(j)] = nr
            xi_ref[rows, _lanes(j)] = ni
            out += [nr, ni]
        return tuple(out)

    zero = jnp.zeros(shape, F32)
    fin = lax.fori_loop(0, seg, step, (zero,) * (2 * len(blocks)), unroll=True)
    for n, j in enumerate(blocks):
        ls = _lanes(j)
        fr, fi = fin[2 * n], fin[2 * n + 1]
        sr, si = lam_ref[2:3, ls], lam_ref[3:4, ls]
        pr, pi = car_ref[:, ls], cai_ref[:, ls]
        for s in (reversed(range(SUBLANES)) if reverse else range(SUBLANES)):
            cn_r[s:s + 1, ls] = pr
            cn_i[s:s + 1, ls] = pi
            pr, pi = fr[s:s + 1, :] + sr * pr - si * pi, fi[s:s + 1, :] + sr * pi + si * pr
        car_ref[:, ls] = pr
        cai_ref[:, ls] = pi


def _s5_fwd(proj, lam_rows, p3_re, p3_im, bbr4, bbi4, crt4, cit4, d_row, t_len, tb):
    seg = tb // SUBLANES

    def body(u_ref, lam_ref, p3r_ref, p3i_ref, bbr_ref, bbi_ref, crt_ref, cit_ref, d_ref,
             hr_ref, hi_ref, ypre_ref, ys_ref, car_ref, cai_ref, cn_r, cn_i, stage_ref, us_ref, yseg_ref):
        @pl.when(pl.program_id(0) == 0)
        def _():
            car_ref[...] = jnp.zeros_like(car_ref)
            cai_ref[...] = jnp.zeros_like(cai_ref)

        _to_segment_order(u_ref[...], stage_ref, us_ref, seg)
        u = us_ref[...]
        for i in range(S5_BLOCKS):
            ui = u[:, i * S5_BW:(i + 1) * S5_BW]
            hr_ref[:, pl.ds(i * S5_BL, S5_BL)] = _dot(ui, bbr_ref[i])
            hi_ref[:, pl.ds(i * S5_BL, S5_BL)] = _dot(ui, bbi_ref[i])
        for lc in range(S5_LANE_BLOCKS // S5_SCAN_BLOCKS):
            blocks = range(lc * S5_SCAN_BLOCKS, (lc + 1) * S5_SCAN_BLOCKS)
            _tile_scan(hr_ref, hi_ref, lam_ref, car_ref, cai_ref, cn_r, cn_i, blocks, seg, False)
            crs = [cn_r[:, _lanes(j)] for j in blocks]
            cis = [cn_i[:, _lanes(j)] for j in blocks]

            def fix(t, carry, blocks=blocks, crs=crs, cis=cis):
                rows = pl.ds(pl.multiple_of(t * SUBLANES, SUBLANES), SUBLANES)
                for n, j in enumerate(blocks):
                    pr, pi = p3r_ref[t, :, _lanes(j)], p3i_ref[t, :, _lanes(j)]
                    hr_ref[rows, _lanes(j)] += pr * crs[n] - pi * cis[n]
                    hi_ref[rows, _lanes(j)] += pr * cis[n] + pi * crs[n]
                return carry

            lax.fori_loop(0, seg, fix, 0, unroll=True)
        for i in range(S5_BLOCKS):
            ws = pl.ds(i * S5_BW, S5_BW)
            bl = pl.ds(i * S5_BL, S5_BL)
            yseg_ref[:, ws] = (_dot(hr_ref[:, bl], crt_ref[i]) - _dot(hi_ref[:, bl], cit_ref[i])
                               + d_ref[:, ws] * u[:, i * S5_BW:(i + 1) * S5_BW])
        _from_segment_order(yseg_ref[...], stage_ref, ypre_ref, seg)
        ys_ref[...] = jax.nn.gelu(ypre_ref[...], approximate=True).astype(ys_ref.dtype)

    whole = pl.BlockSpec(memory_space=pltpu.VMEM)
    return pl.pallas_call(
        body, name="s5_fwd", grid=(t_len // tb,),
        in_specs=[pl.BlockSpec((tb, S5_WIDTH), lambda i: (i, 4096 // S5_WIDTH))] + [whole] * 8,
        out_specs=[pl.BlockSpec((tb, S5_LANES), lambda i: (i, 0)), pl.BlockSpec((tb, S5_LANES), lambda i: (i, 0)),
                   pl.BlockSpec((tb, S5_WIDTH), lambda i: (i, 0)), pl.BlockSpec((tb, S5_WIDTH), lambda i: (i, 0))],
        out_shape=[jax.ShapeDtypeStruct((t_len, S5_LANES), F32), jax.ShapeDtypeStruct((t_len, S5_LANES), F32),
                   jax.ShapeDtypeStruct((t_len, S5_WIDTH), F32), jax.ShapeDtypeStruct((t_len, S5_WIDTH), MXU_DTYPE)],
        scratch_shapes=[pltpu.VMEM((1, S5_LANES), F32), pltpu.VMEM((1, S5_LANES), F32),
                        pltpu.VMEM((SUBLANES, S5_LANES), F32), pltpu.VMEM((SUBLANES, S5_LANES), F32),
                        pltpu.VMEM((S5_WIDTH // 128, tb, 128), F32), pltpu.VMEM((tb, S5_WIDTH), F32),
                        pltpu.VMEM((tb, S5_WIDTH), F32)],
        compiler_params=_params("arbitrary"))(proj, lam_rows, p3_re, p3_im, bbr4, bbi4, crt4, cit4, d_row)


def _s5_bwd(dgelu, y_pre, proj, h_re, h_im, lam_rows, p3_re, p3_im, bbr4, bbi4, cr4, ci4, d_row, dproj, t_len, tb):
    seg = tb // SUBLANES
    nb = t_len // tb

    def body(dg_ref, yp_ref, u_ref, hr_ref, hi_ref, lam_ref, p3r_ref, p3i_ref, bbr_ref, bbi_ref, cr_ref, ci_ref,
             d_ref, _, du_ref, dbbr_ref, dbbi_ref, dcr_ref, dci_ref, dd_ref, dlam_ref,
             gr_ref, gi_ref, car_ref, cai_ref, cn_r, cn_i, stage_ref, us_ref, dys_ref, duseg_ref):
        @pl.when(pl.program_id(0) == 0)
        def _():
            for ref in (car_ref, cai_ref, dbbr_ref, dbbi_ref, dcr_ref, dci_ref, dd_ref, dlam_ref):
                ref[...] = jnp.zeros_like(ref)

        _to_segment_order(u_ref[...], stage_ref, us_ref, seg)
        _to_segment_order(dg_ref[...] * _dgelu(yp_ref[...]), stage_ref, dys_ref, seg)
        u, dy = us_ref[...], dys_ref[...]
        for i in range(S5_BLOCKS):
            dyi = dy[:, i * S5_BW:(i + 1) * S5_BW]
            gr_ref[:, pl.ds(i * S5_BL, S5_BL)] = _dot(dyi, cr_ref[i])
            gi_ref[:, pl.ds(i * S5_BL, S5_BL)] = -_dot(dyi, ci_ref[i])
        for lc in range(S5_LANE_BLOCKS // S5_SCAN_BLOCKS):
            blocks = range(lc * S5_SCAN_BLOCKS, (lc + 1) * S5_SCAN_BLOCKS)
            _tile_scan(gr_ref, gi_ref, lam_ref, car_ref, cai_ref, cn_r, cn_i, blocks, seg, True)
            crs = [cn_r[:, _lanes(j)] for j in blocks]
            cis = [cn_i[:, _lanes(j)] for j in blocks]

            def fix(k, carry, blocks=blocks, crs=crs, cis=cis):
                t = seg - 1 - k
                rows = pl.ds(pl.multiple_of(t * SUBLANES, SUBLANES), SUBLANES)
                out = []
                for n, j in enumerate(blocks):
                    nr, ni, slr, sli = carry[4 * n:4 * n + 4]
                    pr, pi = p3r_ref[t, :, _lanes(j)], p3i_ref[t, :, _lanes(j)]
                    g_r = gr_ref[rows, _lanes(j)] + pr * crs[n] - pi * cis[n]
                    g_i = gi_ref[rows, _lanes(j)] + pr * cis[n] + pi * crs[n]
                    gr_ref[rows, _lanes(j)] = g_r
                    gi_ref[rows, _lanes(j)] = g_i
                    hr, hi = hr_ref[rows, _lanes(j)], hi_ref[rows, _lanes(j)]
                    out += [g_r, g_i, slr + nr * hr + ni * hi, sli + ni * hr - nr * hi]
                return tuple(out)

            zero = jnp.zeros((SUBLANES, 128), F32)
            init = []
            for n in range(len(blocks)):
                init += [crs[n], cis[n], zero, zero]
            fin = lax.fori_loop(0, seg, fix, tuple(init), unroll=True)
            for n, j in enumerate(blocks):
                dlam_ref[0:1, _lanes(j)] += jnp.sum(fin[4 * n + 2], axis=0, keepdims=True)
                dlam_ref[1:2, _lanes(j)] += jnp.sum(fin[4 * n + 3], axis=0, keepdims=True)
        for i in range(S5_BLOCKS):
            ws = pl.ds(i * S5_BW, S5_BW)
            bl = pl.ds(i * S5_BL, S5_BL)
            ui, dyi = u[:, i * S5_BW:(i + 1) * S5_BW], dy[:, i * S5_BW:(i + 1) * S5_BW]
            gr, gi = gr_ref[:, bl], gi_ref[:, bl]
            duseg_ref[:, ws] = _dot(gr, bbr_ref[i], _NT) + _dot(gi, bbi_ref[i], _NT) + d_ref[:, ws] * dyi
            dbbr_ref[i] += _dot(ui, gr, _TN)
            dbbi_ref[i] += _dot(ui, gi, _TN)
            dcr_ref[i] += _dot(hr_ref[:, bl], dyi, _TN)
            dci_ref[i] -= _dot(hi_ref[:, bl], dyi, _TN)
        dd_ref[...] += jnp.sum(dy * u, axis=0, keepdims=True)
        _from_segment_order(duseg_ref[...], stage_ref, duseg_ref, seg)
        du_ref[...] = duseg_ref[...].astype(du_ref.dtype)

    whole = pl.BlockSpec(memory_space=pltpu.VMEM)
    rev = lambda i: (nb - 1 - i, 0)
    const3 = lambda i: (0, 0, 0)
    return pl.pallas_call(
        body, name="s5_bwd", grid=(nb,),
        in_specs=[pl.BlockSpec((tb, S5_WIDTH), rev), pl.BlockSpec((tb, S5_WIDTH), rev),
                  pl.BlockSpec((tb, S5_WIDTH), lambda i: (nb - 1 - i, 4096 // S5_WIDTH)),
                  pl.BlockSpec((tb, S5_LANES), rev), pl.BlockSpec((tb, S5_LANES), rev)] + [whole] * 8
                 + [pl.BlockSpec(memory_space=pl.ANY)],
        out_specs=[pl.BlockSpec((tb, S5_WIDTH), lambda i: (nb - 1 - i, 4096 // S5_WIDTH)),
                   pl.BlockSpec((S5_BLOCKS, S5_BW, S5_BL), const3), pl.BlockSpec((S5_BLOCKS, S5_BW, S5_BL), const3),
                   pl.BlockSpec((S5_BLOCKS, S5_BL, S5_BW), const3), pl.BlockSpec((S5_BLOCKS, S5_BL, S5_BW), const3),
                   pl.BlockSpec((1, S5_WIDTH), lambda i: (0, 0)), pl.BlockSpec((2, S5_LANES), lambda i: (0, 0))],
        out_shape=[jax.ShapeDtypeStruct((t_len, IN_COLS), dproj.dtype),
                   jax.ShapeDtypeStruct((S5_BLOCKS, S5_BW, S5_BL), F32),
                   jax.ShapeDtypeStruct((S5_BLOCKS, S5_BW, S5_BL), F32),
                   jax.ShapeDtypeStruct((S5_BLOCKS, S5_BL, S5_BW), F32),
                   jax.ShapeDtypeStruct((S5_BLOCKS, S5_BL, S5_BW), F32),
                   jax.ShapeDtypeStruct((1, S5_WIDTH), F32), jax.ShapeDtypeStruct((2, S5_LANES), F32)],
        scratch_shapes=[pltpu.VMEM((tb, S5_LANES), F32), pltpu.VMEM((tb, S5_LANES), F32),
                        pltpu.VMEM((1, S5_LANES), F32), pltpu.VMEM((1, S5_LANES), F32),
                        pltpu.VMEM((SUBLANES, S5_LANES), F32), pltpu.VMEM((SUBLANES, S5_LANES), F32),
                        pltpu.VMEM((S5_WIDTH // 128, tb, 128), F32), pltpu.VMEM((tb, S5_WIDTH), F32),
                        pltpu.VMEM((tb, S5_WIDTH), F32), pltpu.VMEM((tb, S5_WIDTH), F32)],
        input_output_aliases={13: 0},
        compiler_params=_params("arbitrary"))(dgelu, y_pre, proj, h_re, h_im, lam_rows, p3_re, p3_im, bbr4, bbi4,
                                              cr4, ci4, d_row, dproj)


def _block_diag(per_group):
    g8 = S5_GROUPS // S5_BLOCKS
    eye = jnp.eye(g8, dtype=bool)[None, :, None, :, None]
    dense = jnp.where(eye, per_group.reshape(S5_BLOCKS, g8, S5_GROUP, 1, S5_STATE), 0.0)
    return dense.reshape(S5_BLOCKS, S5_BW, S5_BL)


def _diag_blocks(dense):
    g8 = S5_GROUPS // S5_BLOCKS
    ar = jnp.arange(g8)
    d5 = dense.reshape(S5_BLOCKS, g8, S5_GROUP, g8, S5_STATE)
    return d5[:, ar, :, ar, :].transpose(1, 0, 2, 3).reshape(S5_GROUPS, S5_GROUP, S5_STATE)


def _hg_gate_bwd(da, o, g, gn):
    dos, dgs, dgns = [], [], []
    for h in range(HG_HEADS):
        sl = slice(h * HG_DIM, (h + 1) * HG_DIM)
        oh, gh, dah, gnh = o[:, sl], g[:, sl], da[:, sl], gn[:, sl]
        rr = lax.rsqrt(jnp.mean(oh * oh, axis=-1, keepdims=True) + NORM_EPS)
        sg = _sig(gh)
        dgs.append(dah * (oh * rr * gnh) * _dsilu(gh, sg))
        don = dah * (gh * sg)
        t = don * gnh
        dos.append(rr * t - oh * (rr * rr * rr) * jnp.mean(t * oh, axis=-1, keepdims=True))
        dgns.append(jnp.sum(don * oh * rr, axis=0, keepdims=True))
    return jnp.concatenate(dos, axis=1), jnp.concatenate(dgs, axis=1), jnp.concatenate(dgns, axis=1)


MIX_BWD_COLS = ((3072, 1024), (4608, 512), (5120, 1024), (6144, 1024))


def _mix_bwd(dgl, h1, dh2, act_hg, ys2, ys_gelu, proj, o_hg, g2, ghn, b_glu, w, t_len, tm):
    nb = t_len // tm

    def body(dgl_ref, h1_ref, dh2_ref, act_ref, ys2_ref, ysg_ref, ghg_ref, z_ref, gh_ref, gs_ref, o_ref, g2_ref, gn_ref,
             bglu_ref, wg_ref, wo_ref, ws5_ref, whg_ref, wglu_ref,
             dh1_ref, dyh_ref, dys_ref, dglu_ref, dgelu_ref, do_ref, dg2_ref, dbglu_ref, dgn_ref, dproj_ref,
             st0, st1, st2, st3, sems):
        i = pl.program_id(0)
        stages = (st0, st1, st2, st3)

        def writes(step):
            rows = pl.ds(pl.multiple_of(step * tm, tm), tm)
            return [pltpu.make_async_copy(st, dproj_ref.at[rows, pl.ds(c0, wd)], sems.at[k])
                    for k, (st, (c0, wd)) in enumerate(zip(stages, MIX_BWD_COLS))]

        @pl.when(i > 0)
        def _():
            for cp in writes(i - 1):
                cp.wait()

        @pl.when(i == 0)
        def _():
            for ref in (dg2_ref, dbglu_ref, dgn_ref):
                ref[...] = jnp.zeros_like(ref)

        dx, dg2 = _rms_bwd(_dot(dgl_ref[...], wg_ref[...], _NT), h1_ref[...], g2_ref[...])
        dh1 = dh2_ref[...] + dx
        dh1_ref[...] = dh1
        dg2_ref[...] += dg2
        dm = _dot(dh1, wo_ref[...], _NT)
        sh, ss = _sig(gh_ref[...]), _sig(gs_ref[...])
        dyh, dys = _mx(dm * sh), _mx(dm * ss)
        dyh_ref[...] = dyh
        dys_ref[...] = dys
        st2[...] = (dm * _dot(act_ref[...], whg_ref[...]) * sh * (1.0 - sh)).astype(st2.dtype)
        st3[...] = (dm * _dot(ys2_ref[...], ws5_ref[...]) * ss * (1.0 - ss)).astype(st3.dtype)
        dys2 = _dot(dys, ws5_ref[...], _NT)
        gl_, z = _dot(ysg_ref[...], wglu_ref[...]) + bglu_ref[...], z_ref[...]
        a, b = gl_[:, :S5_WIDTH], gl_[:, S5_WIDTH:]
        sb, sz = _sig(b), _sig(z)
        silu = z * sz
        dglu = jnp.concatenate([dys2 * sb * silu, dys2 * a * silu * sb * (1.0 - sb)], axis=1)
        st1[...] = (dys2 * a * sb * _dsilu(z, sz)).astype(st1.dtype)
        dbglu_ref[...] += jnp.sum(dglu, axis=0, keepdims=True)
        dglu_ref[...] = _mx(dglu)
        dgelu_ref[...] = _dot(dglu, wglu_ref[...], _NT)
        d_o, dg, dgn = _hg_gate_bwd(_dot(dyh, whg_ref[...], _NT), o_ref[...], ghg_ref[...], gn_ref[...])
        do_ref[...] = d_o.astype(do_ref.dtype)
        st0[...] = dg.astype(st0.dtype)
        dgn_ref[...] += dgn
        for cp in writes(i):
            cp.start()

        @pl.when(i == nb - 1)
        def _():
            for cp in writes(i):
                cp.wait()

    tile = lambda wd, cb=0: pl.BlockSpec((tm, wd), functools.partial(lambda i, cb: (i, cb), cb=cb))
    row = lambda wd: pl.BlockSpec((1, wd), lambda i: (0, 0))
    whole = pl.BlockSpec(memory_space=pltpu.VMEM)
    return pl.pallas_call(
        body, name="mix_bwd", grid=(nb,),
        in_specs=[tile(1024), tile(1024), tile(1024), tile(1024), tile(512), tile(512), tile(1024, 3),
                  tile(512, 4608 // 512), tile(1024, 5), tile(1024, 6), tile(1024), row(1024), row(1024), row(1024)]
                 + [whole] * 5,
        out_specs=[tile(1024), tile(1024), tile(1024), tile(1024), tile(512), tile(1024), row(1024), row(1024),
                   row(1024), _HBM],
        out_shape=[jax.ShapeDtypeStruct((t_len, 1024), F32), jax.ShapeDtypeStruct((t_len, 1024), MXU_DTYPE),
                   jax.ShapeDtypeStruct((t_len, 1024), MXU_DTYPE), jax.ShapeDtypeStruct((t_len, 1024), MXU_DTYPE),
                   jax.ShapeDtypeStruct((t_len, 512), F32), jax.ShapeDtypeStruct((t_len, 1024), MXU_DTYPE),
                   jax.ShapeDtypeStruct((1, 1024), F32), jax.ShapeDtypeStruct((1, 1024), F32),
                   jax.ShapeDtypeStruct((1, 1024), F32), jax.ShapeDtypeStruct((t_len, IN_COLS), MXU_DTYPE)],
        scratch_shapes=[pltpu.VMEM((tm, wd), MXU_DTYPE) for _, wd in MIX_BWD_COLS] + [pltpu.SemaphoreType.DMA((4,))],
        compiler_params=_params("arbitrary"))(dgl, h1, dh2, act_hg, ys2, ys_gelu, proj, proj, proj, proj, o_hg, g2, ghn,
                                              b_glu, w["w_ple_gate"], w["w_out"], w["w_o_s5"], w["w_o_hg"],
                                              w["w_glu"])


def _local_step(x, p, target, w, sm, comm=None):
    t_len = x.shape[0]
    tm = min(256, t_len)
    tmm = min(512, t_len)
    tk = min(2048, t_len)
    tb_hg = min(256, t_len)
    tb_s5 = min(256, t_len)
    g1, g2, g3, ghn = sm["norm_g"], sm["ple_norm_g"], sm["final_norm_g"].reshape(1, D_MODEL), sm["hg_norm_g"]

    def rms_in(xv, g):
        return xv * lax.rsqrt(jnp.mean(xv * xv, axis=-1, keepdims=True) + NORM_EPS) * g

    in_shard = IN_COLS // N_CHIPS
    if comm is None:
        w_in = w["w_in"]
        proj, u = _mm_nn("mm_in", x, w_in, tmm, in_shard, prologue=rms_in, consts=[g1])
    else:
        proj, u, w_in = comm.input_projection(x, g1, rms_in, tmm)

    lanes = lambda a: a.reshape(1, S5_LANES)
    a_re, a_im = lanes(sm["s5_a_re"]), lanes(sm["s5_a_im"])
    ldt = lanes(jnp.broadcast_to(sm["s5_log_dt"].reshape(S5_GROUPS, 1), (S5_GROUPS, S5_STATE)))
    to_t = lambda b: b.reshape(S5_GROUPS, S5_STATE, S5_GROUP).transpose(2, 0, 1).reshape(S5_GROUP, S5_LANES)
    b_re_t, b_im_t = to_t(sm["s5_b_re"]), to_t(sm["s5_b_im"])
    scan_fwd, scan_rev, bbr_t, bbi_t = _s5_prep(a_re, a_im, ldt, b_re_t, b_im_t, tb_s5 // SUBLANES)
    from_t = lambda b: b.reshape(S5_GROUP, S5_GROUPS, S5_STATE).transpose(1, 0, 2)
    bbr_bd = _block_diag(from_t(bbr_t)).astype(MXU_DTYPE)
    bbi_bd = _block_diag(from_t(bbi_t)).astype(MXU_DTYPE)
    cr_bd = _block_diag(sm["s5_c_re"].reshape(S5_GROUPS, S5_GROUP, S5_STATE)).astype(MXU_DTYPE)
    ci_bd = _block_diag(sm["s5_c_im"].reshape(S5_GROUPS, S5_GROUP, S5_STATE)).astype(MXU_DTYPE)
    d_row = sm["s5_d"].reshape(1, S5_WIDTH)
    if comm is None:
        o_hg, act_hg, s_prev = _hgrn2_fwd(proj, sm["hg_lb"], ghn, t_len, tb_hg)
    else:
        o_hg, act_hg, s_prev, landed = _hgrn2_fwd(proj, sm["hg_lb"], ghn, t_len, tb_hg, riding=comm.gather_rest())
        w = comm.rest_weights(landed)
    h_re, h_im, y_pre, ys_gelu = _s5_fwd(proj, *scan_fwd, bbr_bd, bbi_bd,
                                          cr_bd.transpose(0, 2, 1), ci_bd.transpose(0, 2, 1), d_row, t_len, tb_s5)
    def mix_f(act, ysg, z, gh, gs, xv, w_glu, b_glu, w_o_hg, w_o_s5, w_out):
        yh = _dot(act, w_o_hg)
        gl_ = _dot(ysg, w_glu) + b_glu
        a, b = gl_[:, :S5_WIDTH], gl_[:, S5_WIDTH:]
        ys2_ = (a * _sig(b) * (z * _sig(z))).astype(MXU_DTYPE)
        ys = _dot(ys2_, w_o_s5)
        mg = (_sig(gh) * yh + _sig(gs) * ys).astype(MXU_DTYPE)
        return (ys2_, mg, xv + _dot(mg, w_out))

    ys2, merged, h1 = _rowwise(
        "mix_out", mix_f, t_len, tm,
        [(act_hg, 1024, 0), (ys_gelu, 512, 0), (proj, 512, 4608 // 512), (proj, 1024, 5), (proj, 1024, 6),
         (x, 1024, 0)], [w["w_glu"], sm["b_glu"], w["w_o_hg"], w["w_o_s5"], w["w_out"]],
        [(512, MXU_DTYPE), (1024, MXU_DTYPE), (1024, F32)])

    def head_f(h1v, pv, tgt, g_ple, g, w_ple, w_gate):
        r2 = lax.rsqrt(jnp.mean(h1v * h1v, axis=-1, keepdims=True) + NORM_EPS)
        n2_ = (h1v * r2 * g_ple).astype(MXU_DTYPE)
        glv, pev = _dot(n2_, w_gate), _dot(pv, w_ple)
        gate = _sig(glv)
        h2 = h1v + pev * gate
        r = lax.rsqrt(jnp.mean(h2 * h2, axis=-1, keepdims=True) + NORM_EPS)
        e = h2 * r * g - tgt
        loss = 0.5 * jnp.sum(jnp.mean(e * e, axis=-1, keepdims=True), axis=0, keepdims=True)
        dy = e * (1.0 / D_MODEL)
        dg = jnp.sum(dy * h2 * r, axis=0, keepdims=True)
        t = dy * g
        dh2 = r * t - h2 * (r * r * r) * jnp.mean(t * h2, axis=-1, keepdims=True)
        dpe, dgl_ = _mx(dh2 * gate), _mx(dh2 * pev * gate * (1.0 - gate))
        return (dh2, dgl_, jnp.broadcast_to(loss, (1, 128)), dg, _dot(pv, dpe, _TN), _dot(n2_, dgl_, _TN))

    gb = {}
    dh2, dgl, loss_row, d_g3, gb["w_ple"], gb["w_ple_gate"] = _rowwise(
        "ple_loss_head", head_f, t_len, tmm, [(h1, 1024, 0), (p, 256, 0), (target, 1024, 0)],
        [g2, g3, w["w_ple"], w["w_ple_gate"]], [(1024, F32), (1024, MXU_DTYPE)],
        accs=[(1, 128), (1, 1024), (256, 1024), (1024, 1024)])

    dh1, dy_hg, dy_s5, dglu, dgelu, d_o, d_g2, d_bglu, d_ghn, dproj = _mix_bwd(
        dgl, h1, dh2, act_hg, ys2, ys_gelu, proj, o_hg, g2, ghn, sm["b_glu"], w, t_len, tm)
    gb["w_out"] = _mm_tn("mm_d_w_out", merged, dh1, tk, 1024)
    gb["w_o_s5"] = _mm_tn("mm_d_w_o_s5", ys2, dy_s5, tk, 1024)
    gb["w_glu"] = _mm_tn("mm_d_w_glu", ys_gelu, dglu, tk, 1024)
    dproj, d_bbr, d_bbi, d_crt, d_cit, d_d, d_lam = _s5_bwd(dgelu, y_pre, proj, h_re, h_im,
                                                            *scan_rev, bbr_bd, bbi_bd, cr_bd,
                                                            ci_bd, d_row, dproj, t_len, tb_s5)
    to_t3 = lambda b: b.transpose(1, 0, 2).reshape(S5_GROUP, S5_LANES)
    d_are, d_aim, d_ldt, d_br_t, d_bi_t = _s5_prep_bwd(a_re, a_im, ldt, b_re_t, b_im_t, d_lam,
                                                       to_t3(_diag_blocks(d_bbr)), to_t3(_diag_blocks(d_bbi)))
    gb["w_o_hg"] = _mm_tn("mm_d_w_o_hg", act_hg, dy_hg, tk, 1024)
    if comm is None:
        dproj, d_lb = _hgrn2_bwd(proj, d_o, s_prev, sm["hg_lb"], dproj, t_len, tb_hg)
    else:
        rest_grads = _pack_rest_full(gb)
        dproj, d_lb, rest_theirs = _hgrn2_bwd(proj, d_o, s_prev, sm["hg_lb"], dproj, t_len, tb_hg,
                                               riding=comm.swap(rest_grads))

    def in_b(duv, xv, dh, g):
        dx, dg = _rms_bwd(duv, xv, g)
        return (dh + dx, dg)

    in_args = ("mm_d_u_rms_in_bwd", dproj, w_in, tmm, in_shard, in_b, [(x, 1024, 0), (dh1, 1024, 0)], [g1],
               [(1024, F32)])
    if comm is None:
        gb["w_in"] = _mm_tn("mm_d_w_in", u, dproj, tk, in_shard, col_shards=True)
        grad_x, d_g1 = _mm_nt_then(*in_args, accs=[(1, 1024)])
    else:
        gb["w_in"], landed = _mm_tn("mm_d_w_in", u, dproj, tk, in_shard, col_shards=True,
                                    riding=comm.scatter("rest", rest_grads, rest_theirs))
        comm.landed["rest"] = landed
        grad_x, d_g1, landed = _mm_nt_then(*in_args, accs=[(1, 1024)], riding=comm.scatter(
            "in", gb["w_in"].reshape(N_CHIPS, 2, D_MODEL // 2, in_shard)))
        comm.landed["in"] = landed

    back_t = lambda b: b.reshape(S5_GROUP, S5_GROUPS, S5_STATE).transpose(1, 2, 0).reshape(1, S5_GROUPS, S5_STATE,
                                                                                           S5_GROUP)
    gs = {
        "norm_g": d_g1, "hg_lb": d_lb, "hg_norm_g": d_ghn,
        "s5_a_re": d_are.reshape(1, S5_GROUPS, S5_STATE), "s5_a_im": d_aim.reshape(1, S5_GROUPS, S5_STATE),
        "s5_log_dt": d_ldt[0:1, :S5_GROUPS],
        "s5_b_re": back_t(d_br_t), "s5_b_im": back_t(d_bi_t),
        "s5_c_re": _diag_blocks(d_crt.transpose(0, 2, 1)).reshape(1, S5_GROUPS, S5_GROUP, S5_STATE),
        "s5_c_im": _diag_blocks(d_cit.transpose(0, 2, 1)).reshape(1, S5_GROUPS, S5_GROUP, S5_STATE),
        "s5_d": d_d.reshape(1, S5_GROUPS, S5_GROUP), "b_glu": d_bglu, "ple_norm_g": d_g2,
        "final_norm_g": d_g3.reshape(D_MODEL),
    }
    return loss_row, grad_x, gb, gs


def _shard_shape(name):
    r, c = BIG_SHAPE[name]
    return (r, c // N_CHIPS) if name in BIG_COL_SHARDED else (r // N_CHIPS, c)


def _pack_small(parts, last):
    flat = jnp.concatenate([parts[n].reshape(-1) for n in SMALL] + [last.reshape(-1)])
    return jnp.pad(flat, (0, SMALL_ROWS * PACK_W - flat.shape[0])).reshape(SMALL_ROWS, PACK_W)


def _unpack_small(packed):
    flat, out, off = packed.reshape(-1), {}, 0
    for n in SMALL:
        size = 1
        for d in SMALL_SHAPE[n]:
            size *= d
        out[n] = flat[off:off + size].reshape(SMALL_SHAPE[n])
        off += size
    return out, flat[off]


def _place():
    x, y, c = lax.axis_index("x"), lax.axis_index("y"), lax.axis_index("c")
    return x, y, c, [(1 - x, y), (x, 1 - y), (1 - x, 1 - y)]


def _remote(src, dst, send_sems, recv_sems, k, to):
    return pltpu.make_async_remote_copy(src_ref=src, dst_ref=dst, send_sem=send_sems.at[k], recv_sem=recv_sems.at[k],
                                        device_id=to, device_id_type=MESH)


REST = tuple(n for n in BIG if n != "w_in")
REST_ROWS = sum(BIG_SHAPE[n][0] * BIG_SHAPE[n][1] for n in REST) // (N_CHIPS * PACK_W)
IN_SHARD = IN_COLS // N_CHIPS
IN_TILE, REST_TILE = 256, 272


def _pack_rest(parts):
    return jnp.concatenate([parts[n].reshape(-1, PACK_W) for n in REST], axis=0)


def _unpack_rest(packed):
    out, off = {}, 0
    for n in REST:
        r, c = _shard_shape(n)
        rows = r * c // PACK_W
        out[n] = packed[off:off + rows].reshape(1, r, c)
        off += rows
    return out


def _unpack_rest_full(gathered):
    out, off = {}, 0
    for n in REST:
        r, c = _shard_shape(n)
        rows = r * c // PACK_W
        sh = gathered[:, off:off + rows].reshape(N_CHIPS, r, c)
        out[n] = sh.transpose(1, 0, 2).reshape(BIG_SHAPE[n]) if n in BIG_COL_SHARDED else sh.reshape(BIG_SHAPE[n])
        off += rows
    return out


def _pack_rest_full(full):
    parts = []
    for n in REST:
        r, c = _shard_shape(n)
        g = full[n]
        sh = g.reshape(BIG_SHAPE[n][0], N_CHIPS, c).transpose(1, 0, 2) if n in BIG_COL_SHARDED else g
        parts.append(sh.reshape(N_CHIPS, r * c // PACK_W, PACK_W))
    return jnp.concatenate(parts, axis=1).reshape(N_CHIPS, 2, REST_ROWS // 2, PACK_W)


def _swap_halves(pgs, name="exchange_halves"):
    n = len(pgs)

    def body(*refs):
        pg_refs, out_refs, (send_sems, recv_sems) = refs[:n], refs[n:2 * n], refs[2 * n:]
        x, y, c, _ = _place()
        cps = [_remote(pg_ref.at[j, 1 - c], out_ref.at[j], send_sems, recv_sems, N_CHIPS * g + j, (x, y, 1 - c))
               for g, (pg_ref, out_ref) in enumerate(zip(pg_refs, out_refs)) for j in range(N_CHIPS)]
        for cp in cps:
            cp.start()
        for cp in cps:
            cp.wait()

    return pl.pallas_call(
        body, name=name, in_specs=[_HBM] * n, out_specs=[_HBM] * n,
        out_shape=[jax.ShapeDtypeStruct((N_CHIPS,) + pg.shape[2:], pg.dtype) for pg in pgs],
        scratch_shapes=[pltpu.SemaphoreType.DMA((N_CHIPS * n,)), pltpu.SemaphoreType.DMA((N_CHIPS * n,))])(*pgs)


def _share_halves(gs):
    n = len(gs)

    def body(*refs):
        g_refs, out_refs, (send_sems, recv_sems) = refs[:n], refs[n:2 * n], refs[2 * n:]
        x, y, c, _ = _place()
        cps = [_remote(g_ref, out_ref.at[c], send_sems, recv_sems, g, (x, y, 1 - c))
               for g, (g_ref, out_ref) in enumerate(zip(g_refs, out_refs))]
        for cp in cps:
            cp.start()
        for g, (g_ref, out_ref) in enumerate(zip(g_refs, out_refs)):
            _remote(g_ref, out_ref.at[1 - c], send_sems, recv_sems, g, (x, y, 1 - c)).wait_recv()
        for cp in cps:
            cp.wait_send()

    return pl.pallas_call(
        body, name="share_half", in_specs=[_HBM] * n, out_specs=[_HBM] * n,
        out_shape=[jax.ShapeDtypeStruct((2,) + g.shape, g.dtype) for g in gs],
        scratch_shapes=[pltpu.SemaphoreType.DMA((n,)), pltpu.SemaphoreType.DMA((n,))])(*gs)


def _pair_sum(name, pg, theirs, c, tile):
    _, _, rows, width = pg.shape

    def body(c_ref, a_ref, b_ref, o_ref):
        o_ref[...] = (a_ref[...] + b_ref[...]).astype(o_ref.dtype)

    return pl.pallas_call(
        body, name=name,
        grid_spec=pltpu.PrefetchScalarGridSpec(
            num_scalar_prefetch=1, grid=(N_CHIPS, rows // tile),
            in_specs=[pl.BlockSpec((None, None, tile, width), lambda j, i, c_ref: (j, c_ref[0], i, 0)),
                      pl.BlockSpec((None, tile, width), lambda j, i, c_ref: (j, i, 0))],
            out_specs=pl.BlockSpec((None, tile, width), lambda j, i, c_ref: (j, i, 0))),
        out_shape=jax.ShapeDtypeStruct((N_CHIPS, rows, width), WIRE_DTYPE),
        compiler_params=_params("arbitrary", "arbitrary"))(c.reshape(1), pg, theirs)


def _chip_sum(name, ps, others, k, tile):
    _, rows, width = ps.shape

    def body(k_ref, a_ref, b_ref, o_ref):
        o_ref[...] = ((a_ref[...].astype(F32) + b_ref[0].astype(F32)) + b_ref[1].astype(F32)) + b_ref[2].astype(F32)

    return pl.pallas_call(
        body, name=name,
        grid_spec=pltpu.PrefetchScalarGridSpec(
            num_scalar_prefetch=1, grid=(rows // tile,),
            in_specs=[pl.BlockSpec((None, tile, width), lambda i, k_ref: (k_ref[0], i, 0)),
                      pl.BlockSpec((3, tile, width), lambda i, k_ref: (0, i, 0))],
            out_specs=pl.BlockSpec((tile, width), lambda i, k_ref: (i, 0))),
        out_shape=jax.ShapeDtypeStruct((rows, width), F32),
        compiler_params=_params("arbitrary"))(k.reshape(1), ps, others)


def _mm_in_gathering(x, g1, prologue, in_wire, chip, tm):
    m, k = x.shape
    half, ns = in_wire.shape[1:]
    nrow = m // tm

    def flip(j):
        return jnp.where(j == 1, 2, jnp.where(j == 2, 1, j))

    def body(k_ref, x_ref, g_ref, wire_ref, proj_ref, u_ref, all_ref, kept, b_ref, load_sems, send_sems, recv_sems):
        j, i = pl.program_id(0), pl.program_id(1)
        px, py, c, chips = _place()
        sibling = (px, py, 1 - c)

        def over_ici(r, chip_slot):
            cx, cy = chips[r]
            return _remote(wire_ref.at[c], all_ref.at[chip_slot, c], send_sems, recv_sems, r, (cx, cy, c))

        def to_sibling(r, half_slot):
            cx, cy = chips[r]
            return _remote(all_ref.at[2 * cx + cy, c], all_ref.at[2 * cx + cy, half_slot], send_sems, recv_sems,
                           3 + r, sibling)

        def loads(src, slot):
            return [pltpu.make_async_copy(src.at[h], b_ref.at[slot, pl.ds(h * half, half)], load_sems.at[h])
                    for h in range(2)]

        def shard(r):
            cx, cy = chips[r]
            over_ici(r, 2 * cx + cy).wait_recv()
            if r == 0:
                over_ici(2, 2 * px + py).start()
            to_sibling(r, c).start()
            to_sibling(r, 1 - c).wait_recv()
            return all_ref.at[2 * cx + cy]

        @pl.when((j == 0) & (i == 0))
        def _():
            for r in range(2):
                over_ici(r, 2 * px + py).start()
            for cp in loads(wire_ref, 0):
                cp.start()
            for cp in loads(wire_ref, 0):
                cp.wait()

        @pl.when((j == 1) & (i == 0))
        def _():
            cps = loads(shard(0), 1)
            for cp in cps:
                cp.start()
            for cp in cps:
                cp.wait()

        for nxt in (2, 3):
            @pl.when((j == nxt - 1) & (i == nrow // 2))
            def _(nxt=nxt):
                for cp in loads(shard(nxt - 1), nxt % 2):
                    cp.start()

            @pl.when((j == nxt) & (i == 0))
            def _(nxt=nxt):
                for cp in loads(wire_ref, nxt % 2):
                    cp.wait()

        rows = pl.ds(pl.multiple_of(i * tm, tm), tm)

        @pl.when(j == 0)
        def _():
            tile = _mx(prologue(x_ref[...], g_ref[...]))
            kept[rows, :] = tile
            u_ref[...] = tile

        proj_ref[...] = _dot(kept[rows, :], b_ref[lax.rem(j, 2)])

        @pl.when((j == N_CHIPS - 1) & (i == nrow - 1))
        def _():
            for r in range(3):
                over_ici(r, 2 * px + py).wait_send()
                to_sibling(r, c).wait_send()

    once = lambda j, i, k_ref: (jnp.where(j == 0, i, nrow - 1), 0)
    return pl.pallas_call(
        body, name="mm_in",
        grid_spec=pltpu.PrefetchScalarGridSpec(
            num_scalar_prefetch=1, grid=(N_CHIPS, nrow),
            in_specs=[pl.BlockSpec((tm, k), once), pl.BlockSpec(g1.shape, lambda j, i, k_ref: (0, 0)), _HBM],
            out_specs=[pl.BlockSpec((tm, ns), lambda j, i, k_ref: (i, jnp.bitwise_xor(k_ref[0], flip(j)))),
                       pl.BlockSpec((tm, k), once), _HBM],
            scratch_shapes=[pltpu.VMEM((m, k), MXU_DTYPE), pltpu.VMEM((2, 2 * half, ns), in_wire.dtype),
                            pltpu.SemaphoreType.DMA((2,)), pltpu.SemaphoreType.DMA((6,)),
                            pltpu.SemaphoreType.DMA((6,))]),
        out_shape=[jax.ShapeDtypeStruct((m, N_CHIPS * ns), F32), jax.ShapeDtypeStruct((m, k), MXU_DTYPE),
                   jax.ShapeDtypeStruct((N_CHIPS,) + in_wire.shape, in_wire.dtype)],
        compiler_params=_params("arbitrary", "arbitrary"))(chip.reshape(1), x, g1, in_wire)


class _StepComm:
    TILES = {"in": IN_TILE, "rest": REST_TILE}

    def __init__(self, in_wire, rest_wire, chip, core):
        self.in_wire, self.rest_wire, self.chip, self.core = in_wire, rest_wire, chip, core
        self.sums, self.landed = {}, {}

    def input_projection(self, x, g1, prologue, tm):
        proj, u, shards = _mm_in_gathering(x, g1, prologue, self.in_wire, self.chip, tm)
        shards = lax.dynamic_update_slice(shards, self.in_wire[None], (self.chip, 0, 0, 0))
        return proj, u, shards.reshape(N_CHIPS, D_MODEL, IN_SHARD)

    def gather_rest(self):
        wire = self.rest_wire

        def sends(ins, outs, send_sems, recv_sems):
            (w_ref,), (out_ref,) = ins, outs
            x, y, c, chips = _place()
            return [_remote(w_ref.at[c], out_ref.at[2 * x + y, c], send_sems, recv_sems, 4 * j + 2 * c + to,
                            (cx, cy, to)) for j, (cx, cy) in enumerate(chips) for to in (0, 1)]

        def recvs(ins, outs, send_sems, recv_sems):
            (w_ref,), (out_ref,) = ins, outs
            _, _, c, chips = _place()
            return [_remote(w_ref.at[c], out_ref.at[2 * cx + cy, by], send_sems, recv_sems, 4 * j + 2 * by + c,
                            (cx, cy, by)) for j, (cx, cy) in enumerate(chips) for by in (0, 1)]

        def start(*refs):
            for cp in sends(*refs):
                cp.start()

        def wait(*refs):
            for cp in recvs(*refs):
                cp.wait_recv()
            for cp in sends(*refs):
                cp.wait_send()

        return _Riding((wire,), (jax.ShapeDtypeStruct((N_CHIPS,) + wire.shape, wire.dtype),), 12, start, wait)

    def rest_weights(self, landed):
        full = lax.dynamic_update_slice(landed, self.rest_wire[None], (self.chip, 0, 0, 0))
        return _unpack_rest_full(full.reshape(N_CHIPS, REST_ROWS, PACK_W))

    def swap(self, pg):
        def copies(ins, outs, send_sems, recv_sems):
            (pg_ref,), (out_ref,) = ins, outs
            x, y, c, _ = _place()
            return [_remote(pg_ref.at[j, 1 - c], out_ref.at[j], send_sems, recv_sems, j, (x, y, 1 - c))
                    for j in range(N_CHIPS)]

        def start(*refs):
            for cp in copies(*refs):
                cp.start()

        def wait(*refs):
            for cp in copies(*refs):
                cp.wait()

        return _Riding((pg,), (jax.ShapeDtypeStruct((N_CHIPS,) + pg.shape[2:], pg.dtype),), N_CHIPS, start, wait)

    def scatter(self, group, pg, theirs=None):
        if theirs is None:
            (theirs,) = _swap_halves([pg], "exchange_halves_" + group)
        ps = _pair_sum("sum_pair_" + group, pg, theirs, self.core, self.TILES[group])
        self.sums[group] = ps

        def copies(ins, outs, send_sems, recv_sems):
            (ps_ref,), (out_ref,) = ins, outs
            _, _, c, chips = _place()
            return [_remote(ps_ref.at[2 * cx + cy], out_ref.at[j], send_sems, recv_sems, j, (cx, cy, c))
                    for j, (cx, cy) in enumerate(chips)]

        def start(*refs):
            for cp in copies(*refs):
                cp.start()

        def wait(*refs):
            for cp in copies(*refs):
                cp.wait()

        return _Riding((ps,), (jax.ShapeDtypeStruct((3,) + ps.shape[1:], ps.dtype),), 3, start, wait)

    def reduced(self, group):
        return _chip_sum("sum_chips_" + group, self.sums[group], self.landed[group], self.chip, self.TILES[group])


def _adamw(w, g, m, v):
    m = ADAM_B1 * m + (1.0 - ADAM_B1) * g
    v = ADAM_B2 * v + (1.0 - ADAM_B2) * (g * g)
    m_hat = m / (1.0 - ADAM_B1 ** ADAM_STEP)
    v_hat = v / (1.0 - ADAM_B2 ** ADAM_STEP)
    return -ADAM_LR * (m_hat / (jnp.sqrt(v_hat) + ADAM_EPS) + ADAM_WD * w), m, v


def _small_reduce_adamw(part, w, m, v):
    def body(part_ref, w_ref, m_ref, v_ref, g_ref, d_ref, nm_ref, nv_ref, all_ref, send_sems, recv_sems):
        x, y, c, chips = _place()
        me, sibling = (x, y, c), (x, y, 1 - c)

        def rows(px, py, pc):
            return all_ref.at[4 * px + 2 * py + pc]

        all_ref[4 * x + 2 * y + c] = part_ref[...]
        first = [_remote(part_ref, rows(*me), send_sems, recv_sems, 0, sibling)]
        first += [_remote(part_ref, rows(*me), send_sems, recv_sems, 1 + j, (cx, cy, c))
                  for j, (cx, cy) in enumerate(chips)]
        for cp in first:
            cp.start()
        passed = []
        for j, (cx, cy) in enumerate(chips):
            _remote(part_ref, rows(cx, cy, c), send_sems, recv_sems, 1 + j, me).wait_recv()
            cp = _remote(rows(cx, cy, c), rows(cx, cy, c), send_sems, recv_sems, 4 + j, sibling)
            cp.start()
            passed.append(cp)
        _remote(part_ref, rows(*sibling), send_sems, recv_sems, 0, me).wait_recv()
        for j, (cx, cy) in enumerate(chips):
            _remote(part_ref, rows(cx, cy, 1 - c), send_sems, recv_sems, 4 + j, me).wait_recv()
        for cp in first + passed:
            cp.wait_send()
        g = all_ref[0]
        for dev in range(1, N_DEV):
            g = g + all_ref[dev]
        delta, nm, nv = _adamw(w_ref[...], g, m_ref[...], v_ref[...])
        g_ref[...] = g
        d_ref[...] = delta
        nm_ref[...] = nm
        nv_ref[...] = nv

    whole = pl.BlockSpec(memory_space=pltpu.VMEM)
    shape = jax.ShapeDtypeStruct((SMALL_ROWS, PACK_W), F32)
    return pl.pallas_call(
        body, name="small_reduce_adamw", in_specs=[whole] * 4, out_specs=[whole] * 4, out_shape=[shape] * 4,
        scratch_shapes=[pltpu.VMEM((N_DEV, SMALL_ROWS, PACK_W), F32), pltpu.SemaphoreType.DMA((7,)),
                        pltpu.SemaphoreType.DMA((7,))],
        compiler_params=pltpu.CompilerParams(vmem_limit_bytes=VMEM_LIMIT))(part, w, m, v)


def kernel(x, p, norm_g, w_in, hg_lb, hg_norm_g, w_o_hg, s5_a_re, s5_a_im, s5_log_dt, s5_b_re, s5_b_im, s5_c_re, s5_c_im, s5_d, w_glu, b_glu, w_o_s5, w_out, ple_norm_g, w_ple, w_ple_gate, final_norm_g, loss_target, m_norm_g, m_w_in, m_hg_lb, m_hg_norm_g, m_w_o_hg, m_s5_a_re, m_s5_a_im, m_s5_log_dt, m_s5_b_re, m_s5_b_im, m_s5_c_re, m_s5_c_im, m_s5_d, m_w_glu, m_b_glu, m_w_o_s5, m_w_out, m_ple_norm_g, m_w_ple, m_w_ple_gate, m_final_norm_g, v_norm_g, v_w_in, v_hg_lb, v_hg_norm_g, v_w_o_hg, v_s5_a_re, v_s5_a_im, v_s5_log_dt, v_s5_b_re, v_s5_b_im, v_s5_c_re, v_s5_c_im, v_s5_d, v_w_glu, v_b_glu, v_w_o_s5, v_w_out, v_ple_norm_g, v_w_ple, v_w_ple_gate, v_final_norm_g):
    given = dict(locals())
    wts = {n: given[n] for n in WEIGHTS}
    mom = {n: given["m_" + n] for n in WEIGHTS}
    var = {n: given["v_" + n] for n in WEIGHTS}
    cx, cy, cc = lax.axis_index("x"), lax.axis_index("y"), lax.axis_index("c")
    chip = (2 * cx + cy).astype(jnp.int32)

    core = cc.astype(jnp.int32)
    rest_shard = _pack_rest({n: wts[n][0] for n in REST})
    comm = _StepComm(wts["w_in"][0].astype(MXU_DTYPE).reshape(2, D_MODEL // 2, IN_SHARD),
                     rest_shard.astype(MXU_DTYPE).reshape(2, REST_ROWS // 2, PACK_W), chip, core)

    t_len = x.shape[1]
    loss_row, grad_x, g_big, g_small = _local_step(x.reshape(t_len, D_MODEL), p.reshape(t_len, -1),
                                                   loss_target.reshape(t_len, D_MODEL), None,
                                                   {n: wts[n] for n in SMALL}, comm)

    zero = jnp.zeros((), F32)
    sg, sd, snm, snv = _small_reduce_adamw(_pack_small(g_small, loss_row[0, 0]),
                                           _pack_small({n: wts[n] for n in SMALL}, zero),
                                           _pack_small({n: mom[n] for n in SMALL}, zero),
                                           _pack_small({n: var[n] for n in SMALL}, zero))
    (sg, loss), (sd, _), (snm, _), (snv, _) = (_unpack_small(a) for a in (sg, sd, snm, snv))

    halves = [comm.reduced("in"), comm.reduced("rest")]
    g_in, g_rest = [lax.dynamic_update_slice(got, mine[None], (core, 0, 0))
                    for got, mine in zip(_share_halves(halves), halves)]
    g_in, g_rest = g_in.reshape(D_MODEL, IN_SHARD), g_rest.reshape(REST_ROWS, PACK_W)

    def adam_f(wv, gv, mv, vv):
        return _adamw(wv, gv, mv, vv)

    d_in, nm_in, nv_in = _rowwise("adamw_in", adam_f, D_MODEL, IN_TILE,
                                  [(wts["w_in"][0], IN_SHARD, 0), (g_in, IN_SHARD, 0), (mom["w_in"][0], IN_SHARD, 0),
                                   (var["w_in"][0], IN_SHARD, 0)], [], [(IN_SHARD, F32)] * 3)
    d_rest, nm_rest, nv_rest = _rowwise("adamw_rest", adam_f, REST_ROWS, REST_TILE,
                                        [(rest_shard, PACK_W, 0), (g_rest, PACK_W, 0),
                                         (_pack_rest({n: mom[n][0] for n in REST}), PACK_W, 0),
                                         (_pack_rest({n: var[n][0] for n in REST}), PACK_W, 0)], [],
                                        [(PACK_W, F32)] * 3)
    bg, bd, bnm, bnv = (dict(_unpack_rest(rest), w_in=a.reshape(1, D_MODEL, IN_SHARD))
                        for rest, a in ((g_rest, g_in), (d_rest, d_in), (nm_rest, nm_in), (nv_rest, nv_in)))

    outs = [loss, grad_x.reshape(x.shape)]
    for small, big in ((sg, bg), (sd, bd), (snm, bnm), (snv, bnv)):
        outs += [big[n] if n in BIG else small[n] for n in WEIGHTS]
    return tuple(outs)
```

```python
import functools
from typing import Callable, NamedTuple

import jax
import jax.numpy as jnp
from jax import lax
from jax.experimental import pallas as pl
from jax.experimental.pallas import tpu as pltpu

F32 = jnp.float32
MXU_DTYPE = jnp.bfloat16
WIRE_DTYPE = jnp.bfloat16
NORM_EPS = 1e-6
D_MODEL = 1024
HG_HEADS = 8
HG_DIM = 128
HG_CHUNK = 64
S5_WIDTH = 512
S5_GROUPS = 32
S5_GROUP = 16
S5_STATE = 64
S5_LANES = S5_GROUPS * S5_STATE
IN_COLS = 7168
SUBLANES = 8
VMEM_LIMIT = 56 * 1024 * 1024
HIGHEST = lax.Precision.HIGHEST
MESH = pl.DeviceIdType.MESH

ADAM_LR, ADAM_B1, ADAM_B2, ADAM_EPS, ADAM_WD, ADAM_STEP = 0.001, 0.9, 0.999, 1e-08, 0.01, 10

BIG = ("w_in", "w_o_hg", "w_glu", "w_o_s5", "w_out", "w_ple", "w_ple_gate")
BIG_SHAPE = {"w_in": (1024, 7168), "w_o_hg": (1024, 1024), "w_glu": (512, 1024), "w_o_s5": (512, 1024),
             "w_out": (1024, 1024), "w_ple": (256, 1024), "w_ple_gate": (1024, 1024)}
BIG_COL_SHARDED = ("w_in", "w_glu", "w_o_s5", "w_ple")
SMALL = ("norm_g", "hg_lb", "hg_norm_g", "s5_a_re", "s5_a_im", "s5_log_dt", "s5_b_re", "s5_b_im", "s5_c_re",
         "s5_c_im", "s5_d", "b_glu", "ple_norm_g", "final_norm_g")
SMALL_SHAPE = {"norm_g": (1, 1024), "hg_lb": (2, 1024), "hg_norm_g": (1, 1024), "s5_a_re": (1, 32, 64),
               "s5_a_im": (1, 32, 64), "s5_log_dt": (1, 32), "s5_b_re": (1, 32, 64, 16), "s5_b_im": (1, 32, 64, 16),
               "s5_c_re": (1, 32, 16, 64), "s5_c_im": (1, 32, 16, 64), "s5_d": (1, 32, 16), "b_glu": (1, 1024),
               "ple_norm_g": (1, 1024), "final_norm_g": (1024,)}
WEIGHTS = ("norm_g", "w_in", "hg_lb", "hg_norm_g", "w_o_hg", "s5_a_re", "s5_a_im", "s5_log_dt", "s5_b_re", "s5_b_im",
           "s5_c_re", "s5_c_im", "s5_d", "w_glu", "b_glu", "w_o_s5", "w_out", "ple_norm_g", "w_ple", "w_ple_gate",
           "final_norm_g")
N_CHIPS = 4
N_DEV = 8
PACK_W = 1024
SMALL_ROWS = 144


def _params(*sem):
    return pltpu.CompilerParams(dimension_semantics=sem, vmem_limit_bytes=VMEM_LIMIT)


def _sig(x):
    return 1.0 / (1.0 + jnp.exp(-x))


def _dsilu(z, s):
    return s * (1.0 + z * (1.0 - s))


def _mx(x):
    return x.astype(MXU_DTYPE)


def _dot(a, b, dims=(((1,), (0,)), ((), ()))):
    return lax.dot_general(_mx(a), _mx(b), dims, preferred_element_type=F32)


_NT = (((1,), (1,)), ((), ()))
_TN = (((0,), (0,)), ((), ()))


def _dot32(a, b):
    return jnp.dot(a, b, precision=HIGHEST, preferred_element_type=F32)


def _rms_bwd(dy, x, g):
    r = lax.rsqrt(jnp.mean(x * x, axis=-1, keepdims=True) + NORM_EPS)
    t = dy * g
    dx = r * t - x * (r * r * r) * jnp.mean(t * x, axis=-1, keepdims=True)
    return dx, jnp.sum(dy * x * r, axis=0, keepdims=True)


def _rowwise(name, fn, n_rows_total, tm, rows, consts, outs, accs=(), alias=None):
    n_r, n_c, n_o, n_a = len(rows), len(consts), len(outs), len(accs)

    def body(*refs):
        row_refs = refs[:n_r]
        const_refs = refs[n_r:n_r + n_c]
        pos = n_r + n_c + (1 if alias is not None else 0)
        out_refs = refs[pos:pos + n_o]
        acc_refs = refs[pos + n_o:pos + n_o + n_a]
        res = fn(*[r[...] for r in row_refs], *[r[...] for r in const_refs])
        for r, v in zip(out_refs, res[:n_o]):
            r[...] = v.astype(r.dtype)
        if n_a:
            @pl.when(pl.program_id(0) == 0)
            def _():
                for r in acc_refs:
                    r[...] = jnp.zeros_like(r)
            for r, v in zip(acc_refs, res[n_o:]):
                r[...] += v

    in_specs = [pl.BlockSpec((tm, w), functools.partial(lambda i, cb: (i, cb), cb=cb)) for (_, w, cb) in rows]
    in_specs += [pl.BlockSpec(c.shape, lambda i: (0, 0)) for c in consts]
    args = [a for (a, _, _) in rows] + list(consts)
    out_shape, out_specs = [], []
    for o in outs:
        w, dt = o[0], o[1]
        cb, total = (o[2], o[3]) if len(o) == 4 else (0, w)
        out_shape.append(jax.ShapeDtypeStruct((n_rows_total, total), dt))
        out_specs.append(pl.BlockSpec((tm, w), functools.partial(lambda i, cb: (i, cb), cb=cb)))
    io_alias = {}
    if alias is not None:
        in_specs.append(pl.BlockSpec(memory_space=pl.ANY))
        args.append(alias[0])
        io_alias = {len(args) - 1: alias[1]}
    for (r, w) in accs:
        out_shape.append(jax.ShapeDtypeStruct((r, w), F32))
        out_specs.append(pl.BlockSpec((r, w), lambda i: (0, 0)))
    res = pl.pallas_call(body, name=name, grid=(n_rows_total // tm,), in_specs=in_specs, out_specs=out_specs,
                         out_shape=out_shape, input_output_aliases=io_alias,
                         compiler_params=_params("arbitrary"))(*args)
    return res


class _Riding(NamedTuple):
    ins: tuple
    outs: tuple
    n_sems: int
    start: Callable
    wait: Callable


_HBM = pl.BlockSpec(memory_space=pl.ANY)


def _ride(riding, refs, n_in, n_out, n_scratch, first, last):
    if riding is None:
        return refs[:n_in], refs[n_in:n_in + n_out], refs[n_in + n_out:]
    r_in, r_out = len(riding.ins), len(riding.outs)
    ins, rins = refs[:n_in], refs[n_in:n_in + r_in]
    pos = n_in + r_in
    outs, routs = refs[pos:pos + n_out], refs[pos + n_out:pos + n_out + r_out]
    pos += n_out + r_out
    scratch, (send_sems, recv_sems) = refs[pos:pos + n_scratch], refs[pos + n_scratch:]

    @pl.when(first)
    def _():
        riding.start(rins, routs, send_sems, recv_sems)

    @pl.when(last)
    def _():
        riding.wait(rins, routs, send_sems, recv_sems)

    return ins, outs, scratch


def _riding_call(riding, body, name, grid, in_specs, args, out_specs, out_shape, scratch, io_alias=None):
    if riding is not None:
        in_specs = list(in_specs) + [_HBM] * len(riding.ins)
        args = list(args) + list(riding.ins)
        out_specs = list(out_specs) + [_HBM] * len(riding.outs)
        out_shape = list(out_shape) + list(riding.outs)
        scratch = list(scratch) + [pltpu.SemaphoreType.DMA((riding.n_sems,))] * 2
    return pl.pallas_call(body, name=name, grid=grid, in_specs=in_specs, out_specs=out_specs, out_shape=out_shape,
                          scratch_shapes=scratch, input_output_aliases=io_alias or {},
                          compiler_params=_params(*(["arbitrary"] * len(grid))))(*args)


def _mm_nn(name, a, b, tm, tn, riding=None, prologue=None, consts=()):
    m, k = a.shape
    n = b.shape[1] if b.ndim == 2 else b.shape[0] * b.shape[2]
    grid = (n // tn, m // tm)
    n_out, scratch = (1, []) if prologue is None else (2, [pltpu.VMEM((m, k), MXU_DTYPE)])

    def body(*refs):
        j, i = pl.program_id(0), pl.program_id(1)
        ins, outs, kept = _ride(riding, refs, 2 + len(consts), n_out, len(scratch), (j == 0) & (i == 0),
                                (j == grid[0] - 1) & (i == grid[1] - 1))
        if prologue is None:
            left = ins[0][...]
        else:
            rows = pl.ds(pl.multiple_of(i * tm, tm), tm)

            @pl.when(j == 0)
            def _():
                tile = _mx(prologue(ins[0][...], *[c[...] for c in ins[2:]]))
                kept[0][rows, :] = tile
                outs[1][...] = tile

            left = kept[0][rows, :]
        outs[0][...] = _dot(left, ins[1][...])

    once = (lambda j, i: (i, 0)) if prologue is None else (lambda j, i: (jnp.where(j == 0, i, grid[1] - 1), 0))
    b_spec = (pl.BlockSpec((k, tn), lambda j, i: (0, j)) if b.ndim == 2
              else pl.BlockSpec((None, k, tn), lambda j, i: (j, 0, 0)))
    in_specs = [pl.BlockSpec((tm, k), once), b_spec]
    in_specs += [pl.BlockSpec(c.shape, lambda j, i: (0, 0)) for c in consts]
    out_specs = [pl.BlockSpec((tm, tn), lambda j, i: (i, j))]
    out_shape = [jax.ShapeDtypeStruct((m, n), F32)]
    if prologue is not None:
        out_specs.append(pl.BlockSpec((tm, k), once))
        out_shape.append(jax.ShapeDtypeStruct((m, k), MXU_DTYPE))
    res = _riding_call(riding, body, name, grid, in_specs, [a, b] + list(consts), out_specs, out_shape, scratch)
    return res[0] if riding is None and prologue is None else res


def _mm_nt_then(name, a, b, tm, tn, fn, rows, consts, outs, accs=(), alias=None, riding=None):
    m, n = a.shape
    k = b.shape[-2]
    steps = n // tn
    n_r, n_c, n_o, n_a = len(rows), len(consts), len(outs), len(accs)

    def body(*refs):
        a_ref, b_ref = refs[:2]
        row_refs = refs[2:2 + n_r]
        const_refs = refs[2 + n_r:2 + n_r + n_c]
        i, s = pl.program_id(0), pl.program_id(1)
        n_in = 2 + n_r + n_c + (1 if alias is not None else 0)
        _, outs_, (mm_ref,) = _ride(riding, refs, n_in, n_o + n_a, 1, (i == 0) & (s == 0),
                                    (i == m // tm - 1) & (s == steps - 1))
        out_refs, acc_refs = outs_[:n_o], outs_[n_o:]
        part = _dot(a_ref[...], b_ref[...] if b.ndim == 2 else b_ref[s], _NT)
        if steps > 1:
            @pl.when(s == 0)
            def _():
                mm_ref[...] = jnp.zeros_like(mm_ref)
            mm_ref[...] += part

        @pl.when(s == steps - 1)
        def _():
            res = fn(mm_ref[...] if steps > 1 else part, *[r[...] for r in row_refs], *[r[...] for r in const_refs])
            for r, v in zip(out_refs, res[:n_o]):
                r[...] = v.astype(r.dtype)
            if n_a:
                @pl.when(i == 0)
                def _():
                    for r in acc_refs:
                        r[...] = jnp.zeros_like(r)
                for r, v in zip(acc_refs, res[n_o:]):
                    r[...] += v

    b_spec = (pl.BlockSpec((k, tn), lambda i, s: (0, s)) if b.ndim == 2
              else pl.BlockSpec(memory_space=pltpu.VMEM))
    in_specs = [pl.BlockSpec((tm, tn), lambda i, s: (i, s)), b_spec]
    in_specs += [pl.BlockSpec((tm, w), functools.partial(lambda i, s, cb: (i, cb), cb=cb)) for (_, w, cb) in rows]
    in_specs += [pl.BlockSpec(c.shape, lambda i, s: (0, 0)) for c in consts]
    args = [a, b] + [r[0] for r in rows] + list(consts)
    out_shape, out_specs = [], []
    for o in outs:
        w, dt = o[0], o[1]
        cb, total = (o[2], o[3]) if len(o) == 4 else (0, w)
        out_shape.append(jax.ShapeDtypeStruct((m, total), dt))
        out_specs.append(pl.BlockSpec((tm, w), functools.partial(lambda i, s, cb: (i, cb), cb=cb)))
    io_alias = {}
    if alias is not None:
        in_specs.append(pl.BlockSpec(memory_space=pl.ANY))
        args.append(alias[0])
        io_alias = {len(args) - 1: alias[1]}
    for (r, w) in accs:
        out_shape.append(jax.ShapeDtypeStruct((r, w), F32))
        out_specs.append(pl.BlockSpec((r, w), lambda i, s: (0, 0)))
    return _riding_call(riding, body, name, (m // tm, steps), in_specs, args, out_specs, out_shape,
                        [pltpu.VMEM((tm, k), F32)], io_alias)


def _mm_tn(name, a, b, tk, tn, col_shards=False, riding=None):
    t, k = a.shape
    n = b.shape[1]
    steps = t // tk

    def body(*refs):
        j, s = pl.program_id(0), pl.program_id(1)
        (a_ref, b_ref), (o_ref,), (acc_ref,) = _ride(riding, refs, 2, 1, 1, (j == 0) & (s == 0),
                                                     (j == n // tn - 1) & (s == steps - 1))

        @pl.when(s == 0)
        def _():
            acc_ref[...] = jnp.zeros_like(acc_ref)

        acc_ref[...] += _dot(a_ref[...], b_ref[...], _TN)

        @pl.when(s == steps - 1)
        def _():
            o_ref[...] = acc_ref[...]

    if col_shards:
        out_spec = pl.BlockSpec((None, k, tn), lambda j, s: (j, 0, 0))
        out_shape = jax.ShapeDtypeStruct((n // tn, k, tn), F32)
    else:
        out_spec = pl.BlockSpec((k, tn), lambda j, s: (0, j))
        out_shape = jax.ShapeDtypeStruct((k, n), F32)
    res = _riding_call(riding, body, name, (n // tn, steps),
                       [pl.BlockSpec((tk, k), lambda j, s: (s, 0)), pl.BlockSpec((tk, tn), lambda j, s: (s, j))],
                       [a, b], [out_spec], [out_shape], [pltpu.VMEM((k, tn), F32)])
    return res[0] if riding is None else res


def _dot01(m01, x):
    m = m01.astype(MXU_DTYPE)
    hi = x.astype(MXU_DTYPE)
    r1 = x - hi.astype(F32)
    mid = r1.astype(MXU_DTYPE)
    lo = (r1 - mid.astype(F32)).astype(MXU_DTYPE)
    dot = lambda v: jnp.dot(m, v, preferred_element_type=F32)
    return dot(hi) + dot(mid) + dot(lo)


def _chunk_rows(x, offset, nck):
    return jnp.concatenate([jnp.broadcast_to(x[c * HG_CHUNK + offset:c * HG_CHUNK + offset + 1, :],
                                             (HG_CHUNK, x.shape[1])) for c in range(nck)], axis=0)


def _hg_block_terms(q, f, lb, tb):
    nck = tb // HG_CHUNK
    sig = _sig(f)
    fv = lb + (1.0 - lb) * sig
    kk = (1.0 - lb) * (1.0 - sig)
    row = lax.broadcasted_iota(jnp.int32, (tb, tb), 0)
    col = lax.broadcasted_iota(jnp.int32, (tb, tb), 1)
    same = jnp.right_shift(row, 6) == jnp.right_shift(col, 6)
    causal, anti = same & (row >= col), same & (row <= col)
    b = _dot01(causal, jnp.log(fv))
    b_mid, b_last = _chunk_rows(b, HG_CHUNK // 2 - 1, nck), _chunk_rows(b, HG_CHUNK - 1, nck)
    e_mid, e_mid_inv = jnp.exp(b - b_mid), jnp.exp(b_mid - b)
    e_b, e_last = jnp.exp(b), jnp.exp(b_last - b)
    dcs = [jnp.exp(b[c * HG_CHUNK + HG_CHUNK - 1:(c + 1) * HG_CHUNK, :]) for c in range(nck)]
    return sig, fv, kk, causal, anti, e_mid, e_mid_inv, e_b, e_last, dcs


def _hgrn2_fwd(proj, hg_lb, hg_norm_g, t_len, tb, riding=None):
    nck = tb // HG_CHUNK
    nb = t_len // tb

    def body(*refs):
        step = pl.program_id(0)
        ((p_ref, lb_ref, gn_ref), (o_ref, act_ref, sp_ref),
         (st_ref, a_s, bm_s, qd_s, kd_s, v_s, sc_s, inc_s)) = _ride(riding, refs, 3, 3, 8, step == 0, step == nb - 1)

        @pl.when(pl.program_id(0) == 0)
        def _():
            st_ref[...] = jnp.zeros_like(st_ref)

        lb = _sig(lb_ref[0:1, :] - lb_ref[1:2, :])
        q = p_ref[:, pl.ds(0, 1024)]
        _, _, kk, causal, _, e_mid, e_mid_inv, e_b, e_last, dcs = _hg_block_terms(q, p_ref[:, pl.ds(1024, 1024)],
                                                                                   lb, tb)
        a_s[...] = _mx(q * e_mid)
        bm_s[...] = _mx(kk * e_mid_inv)
        qd_s[...] = _mx(q * e_b)
        kd_s[...] = _mx(kk * e_last)
        v_s[...] = _mx(p_ref[:, pl.ds(2048, 1024)])
        heads = [pl.ds(h * HG_DIM, HG_DIM) for h in range(HG_HEADS)]
        chunks = [pl.ds(c * HG_CHUNK, HG_CHUNK) for c in range(nck)]
        for h, hs in enumerate(heads):
            sc_s[h] = _mx(jnp.where(causal, _dot(a_s[:, hs], bm_s[:, hs], _NT), 0.0))
        for h, hs in enumerate(heads):
            o_ref[:, hs] = _dot(sc_s[h], v_s[:, hs])
        for h, hs in enumerate(heads):
            for c, r in enumerate(chunks):
                inc_s[h, c] = _dot(v_s[r, hs], kd_s[r, hs], _TN)
        for c in range(nck):
            for h in range(HG_HEADS):
                st = st_ref[h]
                sp_ref[h, c] = st
                st_ref[h] = dcs[c][:, h * HG_DIM:(h + 1) * HG_DIM] * st + inc_s[h, c]
        for c, r in enumerate(chunks):
            for h, hs in enumerate(heads):
                o_ref[r, hs] += _dot(qd_s[r, hs], sp_ref[h, c], _NT)
        for h, hs in enumerate(heads):
            o = o_ref[:, hs]
            rr = lax.rsqrt(jnp.mean(o * o, axis=-1, keepdims=True) + NORM_EPS)
            g = p_ref[:, pl.ds(3072 + h * HG_DIM, HG_DIM)]
            act_ref[:, hs] = (o * rr * gn_ref[:, hs] * (g * _sig(g))).astype(act_ref.dtype)

    return _riding_call(
        riding, body, "hgrn2_fwd", (nb,),
        [pl.BlockSpec((tb, 4096), lambda i: (i, 0)), pl.BlockSpec((2, 1024), lambda i: (0, 0)),
         pl.BlockSpec((1, 1024), lambda i: (0, 0))],
        [proj, hg_lb, hg_norm_g],
        [pl.BlockSpec((tb, 1024), lambda i: (i, 0)), pl.BlockSpec((tb, 1024), lambda i: (i, 0)),
         pl.BlockSpec((HG_HEADS, nck, HG_DIM, HG_DIM), lambda i: (0, i, 0, 0))],
        [jax.ShapeDtypeStruct((t_len, 1024), F32), jax.ShapeDtypeStruct((t_len, 1024), MXU_DTYPE),
         jax.ShapeDtypeStruct((HG_HEADS, t_len // HG_CHUNK, HG_DIM, HG_DIM), F32)],
        [pltpu.VMEM((HG_HEADS, HG_DIM, HG_DIM), F32)] + [pltpu.VMEM((tb, 1024), MXU_DTYPE)] * 5
        + [pltpu.VMEM((HG_HEADS, tb, tb), MXU_DTYPE), pltpu.VMEM((HG_HEADS, nck, HG_DIM, HG_DIM), F32)])


def _hgrn2_bwd(proj, d_o, s_prev, hg_lb, dproj, t_len, tb, riding=None):
    nck = tb // HG_CHUNK
    nb = t_len // tb

    def body(*refs):
        step = pl.program_id(0)
        ((p_ref, do_ref, sp_ref, lb_ref, _), (dp_ref, dlb_ref),
         (ds_ref, acc_ref, a_s, bm_s, qd_s, kd_s, v_s, do_s, da_s, dbm_s, dqd_s, dkd_s, dv_s, ex_s, sc_s, dsc_s,
          up_s)) = _ride(riding, refs, 5, 2, 17, step == 0, step == nb - 1)

        @pl.when(pl.program_id(0) == 0)
        def _():
            ds_ref[...] = jnp.zeros_like(ds_ref)
            acc_ref[...] = jnp.zeros_like(acc_ref)

        lb = _sig(lb_ref[0:1, :] - lb_ref[1:2, :])
        q = p_ref[:, pl.ds(0, 1024)]
        sig, fv, kk, causal, anti, e_mid, e_mid_inv, e_b, e_last, dcs = _hg_block_terms(
            q, p_ref[:, pl.ds(1024, 1024)], lb, tb)
        a, bm, qd, kd = q * e_mid, kk * e_mid_inv, q * e_b, kk * e_last
        a_s[...] = _mx(a)
        bm_s[...] = _mx(bm)
        qd_s[...] = _mx(qd)
        kd_s[...] = _mx(kd)
        v_s[...] = _mx(p_ref[:, pl.ds(2048, 1024)])
        do_s[...] = _mx(do_ref[...])
        heads = [pl.ds(h * HG_DIM, HG_DIM) for h in range(HG_HEADS)]
        chunks = [pl.ds(c * HG_CHUNK, HG_CHUNK) for c in range(nck)]
        for h, hs in enumerate(heads):
            sc_s[h] = _mx(jnp.where(causal, _dot(a_s[:, hs], bm_s[:, hs], _NT), 0.0))
            dsc_s[h] = _mx(jnp.where(causal, _dot(do_s[:, hs], v_s[:, hs], _NT), 0.0))
        for h, hs in enumerate(heads):
            dv_s[:, hs] = _dot(sc_s[h], do_s[:, hs], _TN)
            da_s[:, hs] = _dot(dsc_s[h], bm_s[:, hs])
            dbm_s[:, hs] = _dot(dsc_s[h], a_s[:, hs], _TN)
        for h, hs in enumerate(heads):
            for c, r in enumerate(chunks):
                up_s[h, c] = _dot(do_s[r, hs], qd_s[r, hs], _TN)
                dqd_s[r, hs] = _dot(do_s[r, hs], sp_ref[h, c])
        for c in reversed(range(nck)):
            r = chunks[c]
            for h, hs in enumerate(heads):
                dst = ds_ref[h]
                dc = dcs[c][:, h * HG_DIM:(h + 1) * HG_DIM]
                dv_s[r, hs] += _dot(kd_s[r, hs], dst, _NT)
                dkd_s[r, hs] = _dot(v_s[r, hs], dst)
                ex_s[c:c + 1, hs] = jnp.sum(dst * sp_ref[h, c], axis=0, keepdims=True) * dc
                ds_ref[h] = up_s[h, c] + dc * dst
        da, dbm, dqd, dkd = da_s[...], dbm_s[...], dqd_s[...], dkd_s[...]
        dq = da * e_mid + dqd * e_b
        dk = dbm * e_mid_inv + dkd * e_last
        db = da * a - dbm * bm + dqd * qd - dkd * kd
        dkk = dkd * kd
        extra = jnp.concatenate(
            [jnp.broadcast_to(jnp.sum(dkk[c * HG_CHUNK:(c + 1) * HG_CHUNK], axis=0, keepdims=True)
                              + ex_s[c:c + 1, :], (HG_CHUNK, 1024)) for c in range(nck)], axis=0)
        dlogf = _dot01(anti, db) + extra
        dfv_k = dlogf / fv - dk
        dp_ref[:, pl.ds(0, 1024)] = dq.astype(dp_ref.dtype)
        dp_ref[:, pl.ds(1024, 1024)] = (dfv_k * (1.0 - lb) * sig * (1.0 - sig)).astype(dp_ref.dtype)
        dp_ref[:, pl.ds(2048, 1024)] = dv_s[...].astype(dp_ref.dtype)
        acc_ref[...] += jnp.sum(dfv_k * (1.0 - sig), axis=0, keepdims=True)

        @pl.when(pl.program_id(0) == nb - 1)
        def _():
            g0 = acc_ref[...] * lb * (1.0 - lb)
            dlb_ref[0:1, :] = g0
            dlb_ref[1:2, :] = -g0

    return _riding_call(
        riding, body, "hgrn2_bwd", (nb,),
        [pl.BlockSpec((tb, 3072), lambda i: (nb - 1 - i, 0)),
         pl.BlockSpec((tb, 1024), lambda i: (nb - 1 - i, 0)),
         pl.BlockSpec((HG_HEADS, nck, HG_DIM, HG_DIM), lambda i: (0, nb - 1 - i, 0, 0)),
         pl.BlockSpec((2, 1024), lambda i: (0, 0)),
         pl.BlockSpec(memory_space=pl.ANY)],
        [proj, d_o, s_prev, hg_lb, dproj],
        [pl.BlockSpec((tb, 3072), lambda i: (nb - 1 - i, 0)), pl.BlockSpec((2, 1024), lambda i: (0, 0))],
        [jax.ShapeDtypeStruct((t_len, IN_COLS), dproj.dtype), jax.ShapeDtypeStruct((2, 1024), F32)],
        [pltpu.VMEM((HG_HEADS, HG_DIM, HG_DIM), F32), pltpu.VMEM((1, 1024), F32)]
        + [pltpu.VMEM((tb, 1024), MXU_DTYPE)] * 6 + [pltpu.VMEM((tb, 1024), F32)] * 5
        + [pltpu.VMEM((SUBLANES, 1024), F32)] + [pltpu.VMEM((HG_HEADS, tb, tb), MXU_DTYPE)] * 2
        + [pltpu.VMEM((HG_HEADS, nck, HG_DIM, HG_DIM), F32)], {4: 0})


def _s5_prep_bwd(a_re, a_im, log_dt, b_re_t, b_im_t, dlam, dbbr, dbbi):
    def body(ar_ref, ai_ref, ldt_ref, br_ref, bi_ref, dlam_ref, dbbr_ref, dbbi_ref,
             dar_ref, dai_ref, dldt_ref, dbr_ref, dbi_ref):
        ar, ai = ar_ref[...], ai_ref[...]
        dt = jnp.exp(ldt_ref[...])
        mag = jnp.exp(ar * dt)
        cs, sn = jnp.cos(ai * dt), jnp.sin(ai * dt)
        lr, li = mag * cs, mag * sn
        den = ar * ar + ai * ai
        nr = lr - 1.0
        sr = (nr * ar + li * ai) / den
        si = (li * ar - nr * ai) / den
        br, bi = br_ref[...], bi_ref[...]
        gbr, gbi = dbbr_ref[...], dbbi_ref[...]
        dbr_ref[...] = sr * gbr + si * gbi
        dbi_ref[...] = sr * gbi - si * gbr
        dsr = jnp.sum(gbr * br + gbi * bi, axis=0, keepdims=True)
        dsi = jnp.sum(gbi * br - gbr * bi, axis=0, keepdims=True)
        dnr = (dsr * ar - dsi * ai) / den
        dli = dlam_ref[1:2, :] + (dsr * ai + dsi * ar) / den
        dlr = dlam_ref[0:1, :] + dnr
        dden = -(dsr * sr + dsi * si) / den
        dar = (dsr * nr + dsi * li) / den + dden * 2.0 * ar
        dai = (dsr * li - dsi * nr) / den + dden * 2.0 * ai
        dmag = dlr * cs + dli * sn
        dth = mag * (dli * cs - dlr * sn)
        dar_ref[...] = dar + dmag * mag * dt
        dai_ref[...] = dai + dth * dt
        ddt = (dmag * mag * ar + dth * ai) * dt
        lane = lax.broadcasted_iota(jnp.int32, (S5_LANES, 128), 0) // S5_STATE
        grp = lax.broadcasted_iota(jnp.int32, (S5_LANES, 128), 1)
        dldt_ref[...] = _dot32(jnp.broadcast_to(ddt, (SUBLANES, S5_LANES)), (lane == grp).astype(F32))

    whole = pl.BlockSpec(memory_space=pltpu.VMEM)
    return pl.pallas_call(
        body, name="s5_prep_bwd", in_specs=[whole] * 8, out_specs=[whole] * 5,
        out_shape=[jax.ShapeDtypeStruct((1, S5_LANES), F32), jax.ShapeDtypeStruct((1, S5_LANES), F32),
                   jax.ShapeDtypeStruct((SUBLANES, 128), F32), jax.ShapeDtypeStruct((S5_GROUP, S5_LANES), F32),
                   jax.ShapeDtypeStruct((S5_GROUP, S5_LANES), F32)])(a_re, a_im, log_dt, b_re_t, b_im_t, dlam, dbbr,
                                                                      dbbi)


def _dgelu(x):
    c, a = 0.7978845608028654, 0.044715
    th = jnp.tanh(c * (x + a * x * x * x))
    return 0.5 * (1.0 + th) + 0.5 * x * (1.0 - th * th) * c * (1.0 + 3.0 * a * x * x)


S5_BLOCKS = 4
S5_BW = S5_WIDTH // S5_BLOCKS
S5_BL = S5_LANES // S5_BLOCKS
S5_LANE_BLOCKS = S5_LANES // 128
S5_SCAN_BLOCKS = 4


def _s5_prep(a_re, a_im, log_dt, b_re_t, b_im_t, seg):
    def body(ar_ref, ai_ref, ldt_ref, br_ref, bi_ref,
             rows_f, pfr_ref, pfi_ref, rows_r, prr_ref, pri_ref, bbr_ref, bbi_ref):
        ar, ai = ar_ref[...], ai_ref[...]
        dt = jnp.exp(ldt_ref[...])
        mag = jnp.exp(ar * dt)
        lr, li = mag * jnp.cos(ai * dt), mag * jnp.sin(ai * dt)
        den = ar * ar + ai * ai
        nr = lr - 1.0
        sr = (nr * ar + li * ai) / den
        si = (li * ar - nr * ai) / den
        wide = (SUBLANES, S5_LANES)
        cr, ci = lr, li
        for i in range(seg):
            pfr_ref[i] = jnp.broadcast_to(cr, wide)
            pfi_ref[i] = jnp.broadcast_to(ci, wide)
            prr_ref[seg - 1 - i] = jnp.broadcast_to(cr, wide)
            pri_ref[seg - 1 - i] = jnp.broadcast_to(-ci, wide)
            if i == seg - 1:
                for rows, sign in ((rows_f, 1.0), (rows_r, -1.0)):
                    rows[0:1, :] = lr
                    rows[1:2, :] = sign * li
                    rows[2:3, :] = cr
                    rows[3:4, :] = sign * ci
            cr, ci = cr * lr - ci * li, cr * li + ci * lr
        br, bi = br_ref[...], bi_ref[...]
        bbr_ref[...] = sr * br - si * bi
        bbi_ref[...] = sr * bi + si * br

    whole = pl.BlockSpec(memory_space=pltpu.VMEM)
    tables = [jax.ShapeDtypeStruct((4, S5_LANES), F32)] + [jax.ShapeDtypeStruct((seg, SUBLANES, S5_LANES), F32)] * 2
    bbar = [jax.ShapeDtypeStruct((S5_GROUP, S5_LANES), F32)] * 2
    res = pl.pallas_call(body, name="s5_prep", in_specs=[whole] * 5, out_specs=[whole] * 8,
                         out_shape=tables + tables + bbar)(a_re, a_im, log_dt, b_re_t, b_im_t)
    return res[0:3], res[3:6], res[6], res[7]


def _lanes(j):
    return pl.ds(j * 128, 128)


def _to_segment_order(v, stage_ref, out_ref, seg):
    nbl = v.shape[1] // 128
    for b in range(nbl):
        stage_ref[b] = v[:, b * 128:(b + 1) * 128]

    def body(t, carry):
        rows = pl.ds(pl.multiple_of(t * SUBLANES, SUBLANES), SUBLANES)
        for b in range(nbl):
            out_ref[rows, _lanes(b)] = stage_ref[b, pl.ds(t, SUBLANES, stride=seg), :]
        return carry

    lax.fori_loop(0, seg, body, 0, unroll=True)


def _from_segment_order(v, stage_ref, out_ref, seg):
    nbl = v.shape[1] // 128
    for b in range(nbl):
        stage_ref[b] = v[:, b * 128:(b + 1) * 128]
    for s in range(SUBLANES):
        def body(k, carry, s=s):
            rows = pl.ds(pl.multiple_of(s * seg + k * SUBLANES, SUBLANES), SUBLANES)
            for b in range(nbl):
                out_ref[rows, _lanes(b)] = stage_ref[b, pl.ds(k * SUBLANES * SUBLANES + s, SUBLANES,
                                                              stride=SUBLANES), :]
            return carry

        lax.fori_loop(0, seg // SUBLANES, body, 0, unroll=True)


def _tile_scan(xr_ref, xi_ref, lam_ref, car_ref, cai_ref, cn_r, cn_i, blocks, seg, reverse):
    shape = (SUBLANES, 128)
    lrs = [jnp.broadcast_to(lam_ref[0:1, _lanes(j)], shape) for j in blocks]
    lis = [jnp.broadcast_to(lam_ref[1:2, _lanes(j)], shape) for j in blocks]

    def step(k, carry):
        t = seg - 1 - k if reverse else k
        rows = pl.ds(pl.multiple_of(t * SUBLANES, SUBLANES), SUBLANES)
        out = []
        for n, j in enumerate(blocks):
            cr, ci = carry[2 * n], carry[2 * n + 1]
            nr = lrs[n] * cr - lis[n] * ci + xr_ref[rows, _lanes(j)]
            ni = lrs[n] * ci + lis[n] * cr + xi_ref[rows, _lanes(j)]
            xr_ref[rows, _lanes(j)] = nr
            xi_ref[rows, _lanes(j)] = ni
            out += [nr, ni]
        return tuple(out)

    zero = jnp.zeros(shape, F32)
    fin = lax.fori_loop(0, seg, step, (zero,) * (2 * len(blocks)), unroll=True)
    for n, j in enumerate(blocks):
        ls = _lanes(j)
        fr, fi = fin[2 * n], fin[2 * n + 1]
        sr, si = lam_ref[2:3, ls], lam_ref[3:4, ls]
        pr, pi = car_ref[:, ls], cai_ref[:, ls]
        for s in (reversed(range(SUBLANES)) if reverse else range(SUBLANES)):
            cn_r[s:s + 1, ls] = pr
            cn_i[s:s + 1, ls] = pi
            pr, pi = fr[s:s + 1, :] + sr * pr - si * pi, fi[s:s + 1, :] + sr * pi + si * pr
        car_ref[:, ls] = pr
        cai_ref[:, ls] = pi


def _s5_fwd(proj, lam_rows, p3_re, p3_im, bbr4, bbi4, crt4, cit4, d_row, t_len, tb):
    seg = tb // SUBLANES

    def body(u_ref, lam_ref, p3r_ref, p3i_ref, bbr_ref, bbi_ref, crt_ref, cit_ref, d_ref,
             hr_ref, hi_ref, ypre_ref, ys_ref, car_ref, cai_ref, cn_r, cn_i, stage_ref, us_ref, yseg_ref):
        @pl.when(pl.program_id(0) == 0)
        def _():
            car_ref[...] = jnp.zeros_like(car_ref)
            cai_ref[...] = jnp.zeros_like(cai_ref)

        _to_segment_order(u_ref[...], stage_ref, us_ref, seg)
        u = us_ref[...]
        for i in range(S5_BLOCKS):
            ui = u[:, i * S5_BW:(i + 1) * S5_BW]
            hr_ref[:, pl.ds(i * S5_BL, S5_BL)] = _dot(ui, bbr_ref[i])
            hi_ref[:, pl.ds(i * S5_BL, S5_BL)] = _dot(ui, bbi_ref[i])
        for lc in range(S5_LANE_BLOCKS // S5_SCAN_BLOCKS):
            blocks = range(lc * S5_SCAN_BLOCKS, (lc + 1) * S5_SCAN_BLOCKS)
            _tile_scan(hr_ref, hi_ref, lam_ref, car_ref, cai_ref, cn_r, cn_i, blocks, seg, False)
            crs = [cn_r[:, _lanes(j)] for j in blocks]
            cis = [cn_i[:, _lanes(j)] for j in blocks]

            def fix(t, carry, blocks=blocks, crs=crs, cis=cis):
                rows = pl.ds(pl.multiple_of(t * SUBLANES, SUBLANES), SUBLANES)
                for n, j in enumerate(blocks):
                    pr, pi = p3r_ref[t, :, _lanes(j)], p3i_ref[t, :, _lanes(j)]
                    hr_ref[rows, _lanes(j)] += pr * crs[n] - pi * cis[n]
                    hi_ref[rows, _lanes(j)] += pr * cis[n] + pi * crs[n]
                return carry

            lax.fori_loop(0, seg, fix, 0, unroll=True)
        for i in range(S5_BLOCKS):
            ws = pl.ds(i * S5_BW, S5_BW)
            bl = pl.ds(i * S5_BL, S5_BL)
            yseg_ref[:, ws] = (_dot(hr_ref[:, bl], crt_ref[i]) - _dot(hi_ref[:, bl], cit_ref[i])
                               + d_ref[:, ws] * u[:, i * S5_BW:(i + 1) * S5_BW])
        _from_segment_order(yseg_ref[...], stage_ref, ypre_ref, seg)
        ys_ref[...] = jax.nn.gelu(ypre_ref[...], approximate=True).astype(ys_ref.dtype)

    whole = pl.BlockSpec(memory_space=pltpu.VMEM)
    return pl.pallas_call(
        body, name="s5_fwd", grid=(t_len // tb,),
        in_specs=[pl.BlockSpec((tb, S5_WIDTH), lambda i: (i, 4096 // S5_WIDTH))] + [whole] * 8,
        out_specs=[pl.BlockSpec((tb, S5_LANES), lambda i: (i, 0)), pl.BlockSpec((tb, S5_LANES), lambda i: (i, 0)),
                   pl.BlockSpec((tb, S5_WIDTH), lambda i: (i, 0)), pl.BlockSpec((tb, S5_WIDTH), lambda i: (i, 0))],
        out_shape=[jax.ShapeDtypeStruct((t_len, S5_LANES), F32), jax.ShapeDtypeStruct((t_len, S5_LANES), F32),
                   jax.ShapeDtypeStruct((t_len, S5_WIDTH), F32), jax.ShapeDtypeStruct((t_len, S5_WIDTH), MXU_DTYPE)],
        scratch_shapes=[pltpu.VMEM((1, S5_LANES), F32), pltpu.VMEM((1, S5_LANES), F32),
                        pltpu.VMEM((SUBLANES, S5_LANES), F32), pltpu.VMEM((SUBLANES, S5_LANES), F32),
                        pltpu.VMEM((S5_WIDTH // 128, tb, 128), F32), pltpu.VMEM((tb, S5_WIDTH), F32),
                        pltpu.VMEM((tb, S5_WIDTH), F32)],
        compiler_params=_params("arbitrary"))(proj, lam_rows, p3_re, p3_im, bbr4, bbi4, crt4, cit4, d_row)


def _s5_bwd(dgelu, y_pre, proj, h_re, h_im, lam_rows, p3_re, p3_im, bbr4, bbi4, cr4, ci4, d_row, dproj, t_len, tb):
    seg = tb // SUBLANES
    nb = t_len // tb

    def body(dg_ref, yp_ref, u_ref, hr_ref, hi_ref, lam_ref, p3r_ref, p3i_ref, bbr_ref, bbi_ref, cr_ref, ci_ref,
             d_ref, _, du_ref, dbbr_ref, dbbi_ref, dcr_ref, dci_ref, dd_ref, dlam_ref,
             gr_ref, gi_ref, car_ref, cai_ref, cn_r, cn_i, stage_ref, us_ref, dys_ref, duseg_ref):
        @pl.when(pl.program_id(0) == 0)
        def _():
            for ref in (car_ref, cai_ref, dbbr_ref, dbbi_ref, dcr_ref, dci_ref, dd_ref, dlam_ref):
                ref[...] = jnp.zeros_like(ref)

        _to_segment_order(u_ref[...], stage_ref, us_ref, seg)
        _to_segment_order(dg_ref[...] * _dgelu(yp_ref[...]), stage_ref, dys_ref, seg)
        u, dy = us_ref[...], dys_ref[...]
        for i in range(S5_BLOCKS):
            dyi = dy[:, i * S5_BW:(i + 1) * S5_BW]
            gr_ref[:, pl.ds(i * S5_BL, S5_BL)] = _dot(dyi, cr_ref[i])
            gi_ref[:, pl.ds(i * S5_BL, S5_BL)] = -_dot(dyi, ci_ref[i])
        for lc in range(S5_LANE_BLOCKS // S5_SCAN_BLOCKS):
            blocks = range(lc * S5_SCAN_BLOCKS, (lc + 1) * S5_SCAN_BLOCKS)
            _tile_scan(gr_ref, gi_ref, lam_ref, car_ref, cai_ref, cn_r, cn_i, blocks, seg, True)
            crs = [cn_r[:, _lanes(j)] for j in blocks]
            cis = [cn_i[:, _lanes(j)] for j in blocks]

            def fix(k, carry, blocks=blocks, crs=crs, cis=cis):
                t = seg - 1 - k
                rows = pl.ds(pl.multiple_of(t * SUBLANES, SUBLANES), SUBLANES)
                out = []
                for n, j in enumerate(blocks):
                    nr, ni, slr, sli = carry[4 * n:4 * n + 4]
                    pr, pi = p3r_ref[t, :, _lanes(j)], p3i_ref[t, :, _lanes(j)]
                    g_r = gr_ref[rows, _lanes(j)] + pr * crs[n] - pi * cis[n]
                    g_i = gi_ref[rows, _lanes(j)] + pr * cis[n] + pi * crs[n]
                    gr_ref[rows, _lanes(j)] = g_r
                    gi_ref[rows, _lanes(j)] = g_i
                    hr, hi = hr_ref[rows, _lanes(j)], hi_ref[rows, _lanes(j)]
                    out += [g_r, g_i, slr + nr * hr + ni * hi, sli + ni * hr - nr * hi]
                return tuple(out)

            zero = jnp.zeros((SUBLANES, 128), F32)
            init = []
            for n in range(len(blocks)):
                init += [crs[n], cis[n], zero, zero]
            fin = lax.fori_loop(0, seg, fix, tuple(init), unroll=True)
            for n, j in enumerate(blocks):
                dlam_ref[0:1, _lanes(j)] += jnp.sum(fin[4 * n + 2], axis=0, keepdims=True)
                dlam_ref[1:2, _lanes(j)] += jnp.sum(fin[4 * n + 3], axis=0, keepdims=True)
        for i in range(S5_BLOCKS):
            ws = pl.ds(i * S5_BW, S5_BW)
            bl = pl.ds(i * S5_BL, S5_BL)
            ui, dyi = u[:, i * S5_BW:(i + 1) * S5_BW], dy[:, i * S5_BW:(i + 1) * S5_BW]
            gr, gi = gr_ref[:, bl], gi_ref[:, bl]
            duseg_ref[:, ws] = _dot(gr, bbr_ref[i], _NT) + _dot(gi, bbi_ref[i], _NT) + d_ref[:, ws] * dyi
            dbbr_ref[i] += _dot(ui, gr, _TN)
            dbbi_ref[i] += _dot(ui, gi, _TN)
            dcr_ref[i] += _dot(hr_ref[:, bl], dyi, _TN)
            dci_ref[i] -= _dot(hi_ref[:, bl], dyi, _TN)
        dd_ref[...] += jnp.sum(dy * u, axis=0, keepdims=True)
        _from_segment_order(duseg_ref[...], stage_ref, duseg_ref, seg)
        du_ref[...] = duseg_ref[...].astype(du_ref.dtype)

    whole = pl.BlockSpec(memory_space=pltpu.VMEM)
    rev = lambda i: (nb - 1 - i, 0)
    const3 = lambda i: (0, 0, 0)
    return pl.pallas_call(
        body, name="s5_bwd", grid=(nb,),
        in_specs=[pl.BlockSpec((tb, S5_WIDTH), rev), pl.BlockSpec((tb, S5_WIDTH), rev),
                  pl.BlockSpec((tb, S5_WIDTH), lambda i: (nb - 1 - i, 4096 // S5_WIDTH)),
                  pl.BlockSpec((tb, S5_LANES), rev), pl.BlockSpec((tb, S5_LANES), rev)] + [whole] * 8
                 + [pl.BlockSpec(memory_space=pl.ANY)],
        out_specs=[pl.BlockSpec((tb, S5_WIDTH), lambda i: (nb - 1 - i, 4096 // S5_WIDTH)),
                   pl.BlockSpec((S5_BLOCKS, S5_BW, S5_BL), const3), pl.BlockSpec((S5_BLOCKS, S5_BW, S5_BL), const3),
                   pl.BlockSpec((S5_BLOCKS, S5_BL, S5_BW), const3), pl.BlockSpec((S5_BLOCKS, S5_BL, S5_BW), const3),
                   pl.BlockSpec((1, S5_WIDTH), lambda i: (0, 0)), pl.BlockSpec((2, S5_LANES), lambda i: (0, 0))],
        out_shape=[jax.ShapeDtypeStruct((t_len, IN_COLS), dproj.dtype),
                   jax.ShapeDtypeStruct((S5_BLOCKS, S5_BW, S5_BL), F32),
                   jax.ShapeDtypeStruct((S5_BLOCKS, S5_BW, S5_BL), F32),
                   jax.ShapeDtypeStruct((S5_BLOCKS, S5_BL, S5_BW), F32),
                   jax.ShapeDtypeStruct((S5_BLOCKS, S5_BL, S5_BW), F32),
                   jax.ShapeDtypeStruct((1, S5_WIDTH), F32), jax.ShapeDtypeStruct((2, S5_LANES), F32)],
        scratch_shapes=[pltpu.VMEM((tb, S5_LANES), F32), pltpu.VMEM((tb, S5_LANES), F32),
                        pltpu.VMEM((1, S5_LANES), F32), pltpu.VMEM((1, S5_LANES), F32),
                        pltpu.VMEM((SUBLANES, S5_LANES), F32), pltpu.VMEM((SUBLANES, S5_LANES), F32),
                        pltpu.VMEM((S5_WIDTH // 128, tb, 128), F32), pltpu.VMEM((tb, S5_WIDTH), F32),
                        pltpu.VMEM((tb, S5_WIDTH), F32), pltpu.VMEM((tb, S5_WIDTH), F32)],
        input_output_aliases={13: 0},
        compiler_params=_params("arbitrary"))(dgelu, y_pre, proj, h_re, h_im, lam_rows, p3_re, p3_im, bbr4, bbi4,
                                              cr4, ci4, d_row, dproj)


def _block_diag(per_group):
    g8 = S5_GROUPS // S5_BLOCKS
    eye = jnp.eye(g8, dtype=bool)[None, :, None, :, None]
    dense = jnp.where(eye, per_group.reshape(S5_BLOCKS, g8, S5_GROUP, 1, S5_STATE), 0.0)
    return dense.reshape(S5_BLOCKS, S5_BW, S5_BL)


def _diag_blocks(dense):
    g8 = S5_GROUPS // S5_BLOCKS
    ar = jnp.arange(g8)
    d5 = dense.reshape(S5_BLOCKS, g8, S5_GROUP, g8, S5_STATE)
    return d5[:, ar, :, ar, :].transpose(1, 0, 2, 3).reshape(S5_GROUPS, S5_GROUP, S5_STATE)


def _hg_gate_bwd(da, o, g, gn):
    dos, dgs, dgns = [], [], []
    for h in range(HG_HEADS):
        sl = slice(h * HG_DIM, (h + 1) * HG_DIM)
        oh, gh, dah, gnh = o[:, sl], g[:, sl], da[:, sl], gn[:, sl]
        rr = lax.rsqrt(jnp.mean(oh * oh, axis=-1, keepdims=True) + NORM_EPS)
        sg = _sig(gh)
        dgs.append(dah * (oh * rr * gnh) * _dsilu(gh, sg))
        don = dah * (gh * sg)
        t = don * gnh
        dos.append(rr * t - oh * (rr * rr * rr) * jnp.mean(t * oh, axis=-1, keepdims=True))
        dgns.append(jnp.sum(don * oh * rr, axis=0, keepdims=True))
    return jnp.concatenate(dos, axis=1), jnp.concatenate(dgs, axis=1), jnp.concatenate(dgns, axis=1)


MIX_BWD_COLS = ((3072, 1024), (4608, 512), (5120, 1024), (6144, 1024))


def _mix_bwd(dgl, h1, dh2, act_hg, ys2, ys_gelu, proj, o_hg, g2, ghn, b_glu, w, t_len, tm):
    nb = t_len // tm

    def body(dgl_ref, h1_ref, dh2_ref, act_ref, ys2_ref, ysg_ref, ghg_ref, z_ref, gh_ref, gs_ref, o_ref, g2_ref, gn_ref,
             bglu_ref, wg_ref, wo_ref, ws5_ref, whg_ref, wglu_ref,
             dh1_ref, dyh_ref, dys_ref, dglu_ref, dgelu_ref, do_ref, dg2_ref, dbglu_ref, dgn_ref, dproj_ref,
             st0, st1, st2, st3, sems):
        i = pl.program_id(0)
        stages = (st0, st1, st2, st3)

        def writes(step):
            rows = pl.ds(pl.multiple_of(step * tm, tm), tm)
            return [pltpu.make_async_copy(st, dproj_ref.at[rows, pl.ds(c0, wd)], sems.at[k])
                    for k, (st, (c0, wd)) in enumerate(zip(stages, MIX_BWD_COLS))]

        @pl.when(i > 0)
        def _():
            for cp in writes(i - 1):
                cp.wait()

        @pl.when(i == 0)
        def _():
            for ref in (dg2_ref, dbglu_ref, dgn_ref):
                ref[...] = jnp.zeros_like(ref)

        dx, dg2 = _rms_bwd(_dot(dgl_ref[...], wg_ref[...], _NT), h1_ref[...], g2_ref[...])
        dh1 = dh2_ref[...] + dx
        dh1_ref[...] = dh1
        dg2_ref[...] += dg2
        dm = _dot(dh1, wo_ref[...], _NT)
        sh, ss = _sig(gh_ref[...]), _sig(gs_ref[...])
        dyh, dys = _mx(dm * sh), _mx(dm * ss)
        dyh_ref[...] = dyh
        dys_ref[...] = dys
        st2[...] = (dm * _dot(act_ref[...], whg_ref[...]) * sh * (1.0 - sh)).astype(st2.dtype)
        st3[...] = (dm * _dot(ys2_ref[...], ws5_ref[...]) * ss * (1.0 - ss)).astype(st3.dtype)
        dys2 = _dot(dys, ws5_ref[...], _NT)
        gl_, z = _dot(ysg_ref[...], wglu_ref[...]) + bglu_ref[...], z_ref[...]
        a, b = gl_[:, :S5_WIDTH], gl_[:, S5_WIDTH:]
        sb, sz = _sig(b), _sig(z)
        silu = z * sz
        dglu = jnp.concatenate([dys2 * sb * silu, dys2 * a * silu * sb * (1.0 - sb)], axis=1)
        st1[...] = (dys2 * a * sb * _dsilu(z, sz)).astype(st1.dtype)
        dbglu_ref[...] += jnp.sum(dglu, axis=0, keepdims=True)
        dglu_ref[...] = _mx(dglu)
        dgelu_ref[...] = _dot(dglu, wglu_ref[...], _NT)
        d_o, dg, dgn = _hg_gate_bwd(_dot(dyh, whg_ref[...], _NT), o_ref[...], ghg_ref[...], gn_ref[...])
        do_ref[...] = d_o.astype(do_ref.dtype)
        st0[...] = dg.astype(st0.dtype)
        dgn_ref[...] += dgn
        for cp in writes(i):
            cp.start()

        @pl.when(i == nb - 1)
        def _():
            for cp in writes(i):
                cp.wait()

    tile = lambda wd, cb=0: pl.BlockSpec((tm, wd), functools.partial(lambda i, cb: (i, cb), cb=cb))
    row = lambda wd: pl.BlockSpec((1, wd), lambda i: (0, 0))
    whole = pl.BlockSpec(memory_space=pltpu.VMEM)
    return pl.pallas_call(
        body, name="mix_bwd", grid=(nb,),
        in_specs=[tile(1024), tile(1024), tile(1024), tile(1024), tile(512), tile(512), tile(1024, 3),
                  tile(512, 4608 // 512), tile(1024, 5), tile(1024, 6), tile(1024), row(1024), row(1024), row(1024)]
                 + [whole] * 5,
        out_specs=[tile(1024), tile(1024), tile(1024), tile(1024), tile(512), tile(1024), row(1024), row(1024),
                   row(1024), _HBM],
        out_shape=[jax.ShapeDtypeStruct((t_len, 1024), F32), jax.ShapeDtypeStruct((t_len, 1024), MXU_DTYPE),
                   jax.ShapeDtypeStruct((t_len, 1024), MXU_DTYPE), jax.ShapeDtypeStruct((t_len, 1024), MXU_DTYPE),
                   jax.ShapeDtypeStruct((t_len, 512), F32), jax.ShapeDtypeStruct((t_len, 1024), MXU_DTYPE),
                   jax.ShapeDtypeStruct((1, 1024), F32), jax.ShapeDtypeStruct((1, 1024), F32),
                   jax.ShapeDtypeStruct((1, 1024), F32), jax.ShapeDtypeStruct((t_len, IN_COLS), MXU_DTYPE)],
        scratch_shapes=[pltpu.VMEM((tm, wd), MXU_DTYPE) for _, wd in MIX_BWD_COLS] + [pltpu.SemaphoreType.DMA((4,))],
        compiler_params=_params("arbitrary"))(dgl, h1, dh2, act_hg, ys2, ys_gelu, proj, proj, proj, proj, o_hg, g2, ghn,
                                              b_glu, w["w_ple_gate"], w["w_out"], w["w_o_s5"], w["w_o_hg"],
                                              w["w_glu"])


def _local_step(x, p, target, w, sm, comm=None):
    t_len = x.shape[0]
    tm = min(256, t_len)
    tmm = min(512, t_len)
    tk = min(2048, t_len)
    tb_hg = min(256, t_len)
    tb_s5 = min(512, t_len)
    g1, g2, g3, ghn = sm["norm_g"], sm["ple_norm_g"], sm["final_norm_g"].reshape(1, D_MODEL), sm["hg_norm_g"]

    def rms_in(xv, g):
        return xv * lax.rsqrt(jnp.mean(xv * xv, axis=-1, keepdims=True) + NORM_EPS) * g

    in_shard = IN_COLS // N_CHIPS
    if comm is None:
        w_in = w["w_in"]
        proj, u = _mm_nn("mm_in", x, w_in, tmm, in_shard, prologue=rms_in, consts=[g1])
    else:
        proj, u, w_in = comm.input_projection(x, g1, rms_in, tmm)

    lanes = lambda a: a.reshape(1, S5_LANES)
    a_re, a_im = lanes(sm["s5_a_re"]), lanes(sm["s5_a_im"])
    ldt = lanes(jnp.broadcast_to(sm["s5_log_dt"].reshape(S5_GROUPS, 1), (S5_GROUPS, S5_STATE)))
    to_t = lambda b: b.reshape(S5_GROUPS, S5_STATE, S5_GROUP).transpose(2, 0, 1).reshape(S5_GROUP, S5_LANES)
    b_re_t, b_im_t = to_t(sm["s5_b_re"]), to_t(sm["s5_b_im"])
    scan_fwd, scan_rev, bbr_t, bbi_t = _s5_prep(a_re, a_im, ldt, b_re_t, b_im_t, tb_s5 // SUBLANES)
    from_t = lambda b: b.reshape(S5_GROUP, S5_GROUPS, S5_STATE).transpose(1, 0, 2)
    bbr_bd = _block_diag(from_t(bbr_t)).astype(MXU_DTYPE)
    bbi_bd = _block_diag(from_t(bbi_t)).astype(MXU_DTYPE)
    cr_bd = _block_diag(sm["s5_c_re"].reshape(S5_GROUPS, S5_GROUP, S5_STATE)).astype(MXU_DTYPE)
    ci_bd = _block_diag(sm["s5_c_im"].reshape(S5_GROUPS, S5_GROUP, S5_STATE)).astype(MXU_DTYPE)
    d_row = sm["s5_d"].reshape(1, S5_WIDTH)
    if comm is None:
        o_hg, act_hg, s_prev = _hgrn2_fwd(proj, sm["hg_lb"], ghn, t_len, tb_hg)
    else:
        o_hg, act_hg, s_prev, landed = _hgrn2_fwd(proj, sm["hg_lb"], ghn, t_len, tb_hg, riding=comm.gather_rest())
        w = comm.rest_weights(landed)
    h_re, h_im, y_pre, ys_gelu = _s5_fwd(proj, *scan_fwd, bbr_bd, bbi_bd,
                                          cr_bd.transpose(0, 2, 1), ci_bd.transpose(0, 2, 1), d_row, t_len, tb_s5)
    def mix_f(act, ysg, z, gh, gs, xv, w_glu, b_glu, w_o_hg, w_o_s5, w_out):
        yh = _dot(act, w_o_hg)
        gl_ = _dot(ysg, w_glu) + b_glu
        a, b = gl_[:, :S5_WIDTH], gl_[:, S5_WIDTH:]
        ys2_ = (a * _sig(b) * (z * _sig(z))).astype(MXU_DTYPE)
        ys = _dot(ys2_, w_o_s5)
        mg = (_sig(gh) * yh + _sig(gs) * ys).astype(MXU_DTYPE)
        return (ys2_, mg, xv + _dot(mg, w_out))

    ys2, merged, h1 = _rowwise(
        "mix_out", mix_f, t_len, tmm,
        [(act_hg, 1024, 0), (ys_gelu, 512, 0), (proj, 512, 4608 // 512), (proj, 1024, 5), (proj, 1024, 6),
         (x, 1024, 0)], [w["w_glu"], sm["b_glu"], w["w_o_hg"], w["w_o_s5"], w["w_out"]],
        [(512, MXU_DTYPE), (1024, MXU_DTYPE), (1024, F32)])

    def head_f(h1v, pv, tgt, g_ple, g, w_ple, w_gate):
        r2 = lax.rsqrt(jnp.mean(h1v * h1v, axis=-1, keepdims=True) + NORM_EPS)
        n2_ = (h1v * r2 * g_ple).astype(MXU_DTYPE)
        glv, pev = _dot(n2_, w_gate), _dot(pv, w_ple)
        gate = _sig(glv)
        h2 = h1v + pev * gate
        r = lax.rsqrt(jnp.mean(h2 * h2, axis=-1, keepdims=True) + NORM_EPS)
        e = h2 * r * g - tgt
        loss = 0.5 * jnp.sum(jnp.mean(e * e, axis=-1, keepdims=True), axis=0, keepdims=True)
        dy = e * (1.0 / D_MODEL)
        dg = jnp.sum(dy * h2 * r, axis=0, keepdims=True)
        t = dy * g
        dh2 = r * t - h2 * (r * r * r) * jnp.mean(t * h2, axis=-1, keepdims=True)
        dpe, dgl_ = _mx(dh2 * gate), _mx(dh2 * pev * gate * (1.0 - gate))
        return (dh2, dgl_, jnp.broadcast_to(loss, (1, 128)), dg, _dot(pv, dpe, _TN), _dot(n2_, dgl_, _TN))

    gb = {}
    dh2, dgl, loss_row, d_g3, gb["w_ple"], gb["w_ple_gate"] = _rowwise(
        "ple_loss_head", head_f, t_len, tmm, [(h1, 1024, 0), (p, 256, 0), (target, 1024, 0)],
        [g2, g3, w["w_ple"], w["w_ple_gate"]], [(1024, F32), (1024, MXU_DTYPE)],
        accs=[(1, 128), (1, 1024), (256, 1024), (1024, 1024)])

    dh1, dy_hg, dy_s5, dglu, dgelu, d_o, d_g2, d_bglu, d_ghn, dproj = _mix_bwd(
        dgl, h1, dh2, act_hg, ys2, ys_gelu, proj, o_hg, g2, ghn, sm["b_glu"], w, t_len, tm)
    gb["w_out"] = _mm_tn("mm_d_w_out", merged, dh1, tk, 1024)
    gb["w_o_s5"] = _mm_tn("mm_d_w_o_s5", ys2, dy_s5, tk, 1024)
    gb["w_glu"] = _mm_tn("mm_d_w_glu", ys_gelu, dglu, tk, 1024)
    dproj, d_bbr, d_bbi, d_crt, d_cit, d_d, d_lam = _s5_bwd(dgelu, y_pre, proj, h_re, h_im,
                                                            *scan_rev, bbr_bd, bbi_bd, cr_bd,
                                                            ci_bd, d_row, dproj, t_len, tb_s5)
    to_t3 = lambda b: b.transpose(1, 0, 2).reshape(S5_GROUP, S5_LANES)
    d_are, d_aim, d_ldt, d_br_t, d_bi_t = _s5_prep_bwd(a_re, a_im, ldt, b_re_t, b_im_t, d_lam,
                                                       to_t3(_diag_blocks(d_bbr)), to_t3(_diag_blocks(d_bbi)))
    gb["w_o_hg"] = _mm_tn("mm_d_w_o_hg", act_hg, dy_hg, tk, 1024)
    if comm is None:
        dproj, d_lb = _hgrn2_bwd(proj, d_o, s_prev, sm["hg_lb"], dproj, t_len, tb_hg)
    else:
        rest_grads = _pack_rest_full(gb)
        dproj, d_lb, rest_theirs = _hgrn2_bwd(proj, d_o, s_prev, sm["hg_lb"], dproj, t_len, tb_hg,
                                               riding=comm.swap(rest_grads))

    def in_b(duv, xv, dh, g):
        dx, dg = _rms_bwd(duv, xv, g)
        return (dh + dx, dg)

    in_args = ("mm_d_u_rms_in_bwd", dproj, w_in, tmm, in_shard, in_b, [(x, 1024, 0), (dh1, 1024, 0)], [g1],
               [(1024, F32)])
    if comm is None:
        gb["w_in"] = _mm_tn("mm_d_w_in", u, dproj, tk, in_shard, col_shards=True)
        grad_x, d_g1 = _mm_nt_then(*in_args, accs=[(1, 1024)])
    else:
        gb["w_in"], landed = _mm_tn("mm_d_w_in", u, dproj, tk, in_shard, col_shards=True,
                                    riding=comm.scatter("rest", rest_grads, rest_theirs))
        comm.landed["rest"] = landed
        grad_x, d_g1, landed = _mm_nt_then(*in_args, accs=[(1, 1024)], riding=comm.scatter(
            "in", gb["w_in"].reshape(N_CHIPS, 2, D_MODEL // 2, in_shard)))
        comm.landed["in"] = landed

    back_t = lambda b: b.reshape(S5_GROUP, S5_GROUPS, S5_STATE).transpose(1, 2, 0).reshape(1, S5_GROUPS, S5_STATE,
                                                                                           S5_GROUP)
    gs = {
        "norm_g": d_g1, "hg_lb": d_lb, "hg_norm_g": d_ghn,
        "s5_a_re": d_are.reshape(1, S5_GROUPS, S5_STATE), "s5_a_im": d_aim.reshape(1, S5_GROUPS, S5_STATE),
        "s5_log_dt": d_ldt[0:1, :S5_GROUPS],
        "s5_b_re": back_t(d_br_t), "s5_b_im": back_t(d_bi_t),
        "s5_c_re": _diag_blocks(d_crt.transpose(0, 2, 1)).reshape(1, S5_GROUPS, S5_GROUP, S5_STATE),
        "s5_c_im": _diag_blocks(d_cit.transpose(0, 2, 1)).reshape(1, S5_GROUPS, S5_GROUP, S5_STATE),
        "s5_d": d_d.reshape(1, S5_GROUPS, S5_GROUP), "b_glu": d_bglu, "ple_norm_g": d_g2,
        "final_norm_g": d_g3.reshape(D_MODEL),
    }
    return loss_row, grad_x, gb, gs


def _shard_shape(name):
    r, c = BIG_SHAPE[name]
    return (r, c // N_CHIPS) if name in BIG_COL_SHARDED else (r // N_CHIPS, c)


def _pack_small(parts, last):
    flat = jnp.concatenate([parts[n].reshape(-1) for n in SMALL] + [last.reshape(-1)])
    return jnp.pad(flat, (0, SMALL_ROWS * PACK_W - flat.shape[0])).reshape(SMALL_ROWS, PACK_W)


def _unpack_small(packed):
    flat, out, off = packed.reshape(-1), {}, 0
    for n in SMALL:
        size = 1
        for d in SMALL_SHAPE[n]:
            size *= d
        out[n] = flat[off:off + size].reshape(SMALL_SHAPE[n])
        off += size
    return out, flat[off]


def _place():
    x, y, c = lax.axis_index("x"), lax.axis_index("y"), lax.axis_index("c")
    return x, y, c, [(1 - x, y), (x, 1 - y), (1 - x, 1 - y)]


def _remote(src, dst, send_sems, recv_sems, k, to):
    return pltpu.make_async_remote_copy(src_ref=src, dst_ref=dst, send_sem=send_sems.at[k], recv_sem=recv_sems.at[k],
                                        device_id=to, device_id_type=MESH)


REST = tuple(n for n in BIG if n != "w_in")
REST_ROWS = sum(BIG_SHAPE[n][0] * BIG_SHAPE[n][1] for n in REST) // (N_CHIPS * PACK_W)
IN_SHARD = IN_COLS // N_CHIPS
IN_TILE, REST_TILE = 256, 272


def _pack_rest(parts):
    return jnp.concatenate([parts[n].reshape(-1, PACK_W) for n in REST], axis=0)


def _unpack_rest(packed):
    out, off = {}, 0
    for n in REST:
        r, c = _shard_shape(n)
        rows = r * c // PACK_W
        out[n] = packed[off:off + rows].reshape(1, r, c)
        off += rows
    return out


def _unpack_rest_full(gathered):
    out, off = {}, 0
    for n in REST:
        r, c = _shard_shape(n)
        rows = r * c // PACK_W
        sh = gathered[:, off:off + rows].reshape(N_CHIPS, r, c)
        out[n] = sh.transpose(1, 0, 2).reshape(BIG_SHAPE[n]) if n in BIG_COL_SHARDED else sh.reshape(BIG_SHAPE[n])
        off += rows
    return out


def _pack_rest_full(full):
    parts = []
    for n in REST:
        r, c = _shard_shape(n)
        g = full[n]
        sh = g.reshape(BIG_SHAPE[n][0], N_CHIPS, c).transpose(1, 0, 2) if n in BIG_COL_SHARDED else g
        parts.append(sh.reshape(N_CHIPS, r * c // PACK_W, PACK_W))
    return jnp.concatenate(parts, axis=1).reshape(N_CHIPS, 2, REST_ROWS // 2, PACK_W)


def _swap_halves(pgs, name="exchange_halves"):
    n = len(pgs)

    def body(*refs):
        pg_refs, out_refs, (send_sems, recv_sems) = refs[:n], refs[n:2 * n], refs[2 * n:]
        x, y, c, _ = _place()
        cps = [_remote(pg_ref.at[j, 1 - c], out_ref.at[j], send_sems, recv_sems, N_CHIPS * g + j, (x, y, 1 - c))
               for g, (pg_ref, out_ref) in enumerate(zip(pg_refs, out_refs)) for j in range(N_CHIPS)]
        for cp in cps:
            cp.start()
        for cp in cps:
            cp.wait()

    return pl.pallas_call(
        body, name=name, in_specs=[_HBM] * n, out_specs=[_HBM] * n,
        out_shape=[jax.ShapeDtypeStruct((N_CHIPS,) + pg.shape[2:], pg.dtype) for pg in pgs],
        scratch_shapes=[pltpu.SemaphoreType.DMA((N_CHIPS * n,)), pltpu.SemaphoreType.DMA((N_CHIPS * n,))])(*pgs)


def _share_halves(gs):
    n = len(gs)

    def body(*refs):
        g_refs, out_refs, (send_sems, recv_sems) = refs[:n], refs[n:2 * n], refs[2 * n:]
        x, y, c, _ = _place()
        cps = [_remote(g_ref, out_ref.at[c], send_sems, recv_sems, g, (x, y, 1 - c))
               for g, (g_ref, out_ref) in enumerate(zip(g_refs, out_refs))]
        for cp in cps:
            cp.start()
        for g, (g_ref, out_ref) in enumerate(zip(g_refs, out_refs)):
            _remote(g_ref, out_ref.at[1 - c], send_sems, recv_sems, g, (x, y, 1 - c)).wait_recv()
        for cp in cps:
            cp.wait_send()

    return pl.pallas_call(
        body, name="share_half", in_specs=[_HBM] * n, out_specs=[_HBM] * n,
        out_shape=[jax.ShapeDtypeStruct((2,) + g.shape, g.dtype) for g in gs],
        scratch_shapes=[pltpu.SemaphoreType.DMA((n,)), pltpu.SemaphoreType.DMA((n,))])(*gs)


def _pair_sum(name, pg, theirs, c, tile):
    _, _, rows, width = pg.shape

    def body(c_ref, a_ref, b_ref, o_ref):
        o_ref[...] = (a_ref[...] + b_ref[...]).astype(o_ref.dtype)

    return pl.pallas_call(
        body, name=name,
        grid_spec=pltpu.PrefetchScalarGridSpec(
            num_scalar_prefetch=1, grid=(N_CHIPS, rows // tile),
            in_specs=[pl.BlockSpec((None, None, tile, width), lambda j, i, c_ref: (j, c_ref[0], i, 0)),
                      pl.BlockSpec((None, tile, width), lambda j, i, c_ref: (j, i, 0))],
            out_specs=pl.BlockSpec((None, tile, width), lambda j, i, c_ref: (j, i, 0))),
        out_shape=jax.ShapeDtypeStruct((N_CHIPS, rows, width), WIRE_DTYPE),
        compiler_params=_params("arbitrary", "arbitrary"))(c.reshape(1), pg, theirs)


def _chip_sum(name, ps, others, k, tile):
    _, rows, width = ps.shape

    def body(k_ref, a_ref, b_ref, o_ref):
        o_ref[...] = ((a_ref[...].astype(F32) + b_ref[0].astype(F32)) + b_ref[1].astype(F32)) + b_ref[2].astype(F32)

    return pl.pallas_call(
        body, name=name,
        grid_spec=pltpu.PrefetchScalarGridSpec(
            num_scalar_prefetch=1, grid=(rows // tile,),
            in_specs=[pl.BlockSpec((None, tile, width), lambda i, k_ref: (k_ref[0], i, 0)),
                      pl.BlockSpec((3, tile, width), lambda i, k_ref: (0, i, 0))],
            out_specs=pl.BlockSpec((tile, width), lambda i, k_ref: (i, 0))),
        out_shape=jax.ShapeDtypeStruct((rows, width), F32),
        compiler_params=_params("arbitrary"))(k.reshape(1), ps, others)


def _mm_in_gathering(x, g1, prologue, in_wire, chip, tm):
    m, k = x.shape
    half, ns = in_wire.shape[1:]
    nrow = m // tm

    def flip(j):
        return jnp.where(j == 1, 2, jnp.where(j == 2, 1, j))

    def body(k_ref, x_ref, g_ref, wire_ref, proj_ref, u_ref, all_ref, kept, b_ref, load_sems, send_sems, recv_sems):
        j, i = pl.program_id(0), pl.program_id(1)
        px, py, c, chips = _place()
        sibling = (px, py, 1 - c)

        def over_ici(r, chip_slot):
            cx, cy = chips[r]
            return _remote(wire_ref.at[c], all_ref.at[chip_slot, c], send_sems, recv_sems, r, (cx, cy, c))

        def to_sibling(r, half_slot):
            cx, cy = chips[r]
            return _remote(all_ref.at[2 * cx + cy, c], all_ref.at[2 * cx + cy, half_slot], send_sems, recv_sems,
                           3 + r, sibling)

        def loads(src, slot):
            return [pltpu.make_async_copy(src.at[h], b_ref.at[slot, pl.ds(h * half, half)], load_sems.at[h])
                    for h in range(2)]

        def shard(r):
            cx, cy = chips[r]
            over_ici(r, 2 * cx + cy).wait_recv()
            if r == 0:
                over_ici(2, 2 * px + py).start()
            to_sibling(r, c).start()
            to_sibling(r, 1 - c).wait_recv()
            return all_ref.at[2 * cx + cy]

        @pl.when((j == 0) & (i == 0))
        def _():
            for r in range(2):
                over_ici(r, 2 * px + py).start()
            for cp in loads(wire_ref, 0):
                cp.start()
            for cp in loads(wire_ref, 0):
                cp.wait()

        @pl.when((j == 1) & (i == 0))
        def _():
            cps = loads(shard(0), 1)
            for cp in cps:
                cp.start()
            for cp in cps:
                cp.wait()

        for nxt in (2, 3):
            @pl.when((j == nxt - 1) & (i == nrow // 2))
            def _(nxt=nxt):
                for cp in loads(shard(nxt - 1), nxt % 2):
                    cp.start()

            @pl.when((j == nxt) & (i == 0))
            def _(nxt=nxt):
                for cp in loads(wire_ref, nxt % 2):
                    cp.wait()

        rows = pl.ds(pl.multiple_of(i * tm, tm), tm)

        @pl.when(j == 0)
        def _():
            tile = _mx(prologue(x_ref[...], g_ref[...]))
            kept[rows, :] = tile
            u_ref[...] = tile

        proj_ref[...] = _dot(kept[rows, :], b_ref[lax.rem(j, 2)])

        @pl.when((j == N_CHIPS - 1) & (i == nrow - 1))
        def _():
            for r in range(3):
                over_ici(r, 2 * px + py).wait_send()
                to_sibling(r, c).wait_send()

    once = lambda j, i, k_ref: (jnp.where(j == 0, i, nrow - 1), 0)
    return pl.pallas_call(
        body, name="mm_in",
        grid_spec=pltpu.PrefetchScalarGridSpec(
            num_scalar_prefetch=1, grid=(N_CHIPS, nrow),
            in_specs=[pl.BlockSpec((tm, k), once), pl.BlockSpec(g1.shape, lambda j, i, k_ref: (0, 0)), _HBM],
            out_specs=[pl.BlockSpec((tm, ns), lambda j, i, k_ref: (i, jnp.bitwise_xor(k_ref[0], flip(j)))),
                       pl.BlockSpec((tm, k), once), _HBM],
            scratch_shapes=[pltpu.VMEM((m, k), MXU_DTYPE), pltpu.VMEM((2, 2 * half, ns), in_wire.dtype),
                            pltpu.SemaphoreType.DMA((2,)), pltpu.SemaphoreType.DMA((6,)),
                            pltpu.SemaphoreType.DMA((6,))]),
        out_shape=[jax.ShapeDtypeStruct((m, N_CHIPS * ns), F32), jax.ShapeDtypeStruct((m, k), MXU_DTYPE),
                   jax.ShapeDtypeStruct((N_CHIPS,) + in_wire.shape, in_wire.dtype)],
        compiler_params=_params("arbitrary", "arbitrary"))(chip.reshape(1), x, g1, in_wire)


class _StepComm:
    TILES = {"in": IN_TILE, "rest": REST_TILE}

    def __init__(self, in_wire, rest_wire, chip, core):
        self.in_wire, self.rest_wire, self.chip, self.core = in_wire, rest_wire, chip, core
        self.sums, self.landed = {}, {}

    def input_projection(self, x, g1, prologue, tm):
        proj, u, shards = _mm_in_gathering(x, g1, prologue, self.in_wire, self.chip, tm)
        shards = lax.dynamic_update_slice(shards, self.in_wire[None], (self.chip, 0, 0, 0))
        return proj, u, shards.reshape(N_CHIPS, D_MODEL, IN_SHARD)

    def gather_rest(self):
        wire = self.rest_wire

        def sends(ins, outs, send_sems, recv_sems):
            (w_ref,), (out_ref,) = ins, outs
            x, y, c, chips = _place()
            return [_remote(w_ref.at[c], out_ref.at[2 * x + y, c], send_sems, recv_sems, 4 * j + 2 * c + to,
                            (cx, cy, to)) for j, (cx, cy) in enumerate(chips) for to in (0, 1)]

        def recvs(ins, outs, send_sems, recv_sems):
            (w_ref,), (out_ref,) = ins, outs
            _, _, c, chips = _place()
            return [_remote(w_ref.at[c], out_ref.at[2 * cx + cy, by], send_sems, recv_sems, 4 * j + 2 * by + c,
                            (cx, cy, by)) for j, (cx, cy) in enumerate(chips) for by in (0, 1)]

        def start(*refs):
            for cp in sends(*refs):
                cp.start()

        def wait(*refs):
            for cp in recvs(*refs):
                cp.wait_recv()
            for cp in sends(*refs):
                cp.wait_send()

        return _Riding((wire,), (jax.ShapeDtypeStruct((N_CHIPS,) + wire.shape, wire.dtype),), 12, start, wait)

    def rest_weights(self, landed):
        full = lax.dynamic_update_slice(landed, self.rest_wire[None], (self.chip, 0, 0, 0))
        return _unpack_rest_full(full.reshape(N_CHIPS, REST_ROWS, PACK_W))

    def swap(self, pg):
        def copies(ins, outs, send_sems, recv_sems):
            (pg_ref,), (out_ref,) = ins, outs
            x, y, c, _ = _place()
            return [_remote(pg_ref.at[j, 1 - c], out_ref.at[j], send_sems, recv_sems, j, (x, y, 1 - c))
                    for j in range(N_CHIPS)]

        def start(*refs):
            for cp in copies(*refs):
                cp.start()

        def wait(*refs):
            for cp in copies(*refs):
                cp.wait()

        return _Riding((pg,), (jax.ShapeDtypeStruct((N_CHIPS,) + pg.shape[2:], pg.dtype),), N_CHIPS, start, wait)

    def scatter(self, group, pg, theirs=None):
        if theirs is None:
            (theirs,) = _swap_halves([pg], "exchange_halves_" + group)
        ps = _pair_sum("sum_pair_" + group, pg, theirs, self.core, self.TILES[group])
        self.sums[group] = ps

        def copies(ins, outs, send_sems, recv_sems):
            (ps_ref,), (out_ref,) = ins, outs
            _, _, c, chips = _place()
            return [_remote(ps_ref.at[2 * cx + cy], out_ref.at[j], send_sems, recv_sems, j, (cx, cy, c))
                    for j, (cx, cy) in enumerate(chips)]

        def start(*refs):
            for cp in copies(*refs):
                cp.start()

        def wait(*refs):
            for cp in copies(*refs):
                cp.wait()

        return _Riding((ps,), (jax.ShapeDtypeStruct((3,) + ps.shape[1:], ps.dtype),), 3, start, wait)

    def reduced(self, group):
        return _chip_sum("sum_chips_" + group, self.sums[group], self.landed[group], self.chip, self.TILES[group])


def _adamw(w, g, m, v):
    m = ADAM_B1 * m + (1.0 - ADAM_B1) * g
    v = ADAM_B2 * v + (1.0 - ADAM_B2) * (g * g)
    m_hat = m / (1.0 - ADAM_B1 ** ADAM_STEP)
    v_hat = v / (1.0 - ADAM_B2 ** ADAM_STEP)
    return -ADAM_LR * (m_hat / (jnp.sqrt(v_hat) + ADAM_EPS) + ADAM_WD * w), m, v


def _small_reduce_adamw(part, w, m, v):
    def body(part_ref, w_ref, m_ref, v_ref, g_ref, d_ref, nm_ref, nv_ref, all_ref, send_sems, recv_sems):
        x, y, c, chips = _place()
        me, sibling = (x, y, c), (x, y, 1 - c)

        def rows(px, py, pc):
            return all_ref.at[4 * px + 2 * py + pc]

        all_ref[4 * x + 2 * y + c] = part_ref[...]
        first = [_remote(part_ref, rows(*me), send_sems, recv_sems, 0, sibling)]
        first += [_remote(part_ref, rows(*me), send_sems, recv_sems, 1 + j, (cx, cy, c))
                  for j, (cx, cy) in enumerate(chips)]
        for cp in first:
            cp.start()
        passed = []
        for j, (cx, cy) in enumerate(chips):
            _remote(part_ref, rows(cx, cy, c), send_sems, recv_sems, 1 + j, me).wait_recv()
            cp = _remote(rows(cx, cy, c), rows(cx, cy, c), send_sems, recv_sems, 4 + j, sibling)
            cp.start()
            passed.append(cp)
        _remote(part_ref, rows(*sibling), send_sems, recv_sems, 0, me).wait_recv()
        for j, (cx, cy) in enumerate(chips):
            _remote(part_ref, rows(cx, cy, 1 - c), send_sems, recv_sems, 4 + j, me).wait_recv()
        for cp in first + passed:
            cp.wait_send()
        g = all_ref[0]
        for dev in range(1, N_DEV):
            g = g + all_ref[dev]
        delta, nm, nv = _adamw(w_ref[...], g, m_ref[...], v_ref[...])
        g_ref[...] = g
        d_ref[...] = delta
        nm_ref[...] = nm
        nv_ref[...] = nv

    whole = pl.BlockSpec(memory_space=pltpu.VMEM)
    shape = jax.ShapeDtypeStruct((SMALL_ROWS, PACK_W), F32)
    return pl.pallas_call(
        body, name="small_reduce_adamw", in_specs=[whole] * 4, out_specs=[whole] * 4, out_shape=[shape] * 4,
        scratch_shapes=[pltpu.VMEM((N_DEV, SMALL_ROWS, PACK_W), F32), pltpu.SemaphoreType.DMA((7,)),
                        pltpu.SemaphoreType.DMA((7,))],
        compiler_params=pltpu.CompilerParams(vmem_limit_bytes=VMEM_LIMIT))(part, w, m, v)


def kernel(x, p, norm_g, w_in, hg_lb, hg_norm_g, w_o_hg, s5_a_re, s5_a_im, s5_log_dt, s5_b_re, s5_b_im, s5_c_re, s5_c_im, s5_d, w_glu, b_glu, w_o_s5, w_out, ple_norm_g, w_ple, w_ple_gate, final_norm_g, loss_target, m_norm_g, m_w_in, m_hg_lb, m_hg_norm_g, m_w_o_hg, m_s5_a_re, m_s5_a_im, m_s5_log_dt, m_s5_b_re, m_s5_b_im, m_s5_c_re, m_s5_c_im, m_s5_d, m_w_glu, m_b_glu, m_w_o_s5, m_w_out, m_ple_norm_g, m_w_ple, m_w_ple_gate, m_final_norm_g, v_norm_g, v_w_in, v_hg_lb, v_hg_norm_g, v_w_o_hg, v_s5_a_re, v_s5_a_im, v_s5_log_dt, v_s5_b_re, v_s5_b_im, v_s5_c_re, v_s5_c_im, v_s5_d, v_w_glu, v_b_glu, v_w_o_s5, v_w_out, v_ple_norm_g, v_w_ple, v_w_ple_gate, v_final_norm_g):
    given = dict(locals())
    wts = {n: given[n] for n in WEIGHTS}
    mom = {n: given["m_" + n] for n in WEIGHTS}
    var = {n: given["v_" + n] for n in WEIGHTS}
    cx, cy, cc = lax.axis_index("x"), lax.axis_index("y"), lax.axis_index("c")
    chip = (2 * cx + cy).astype(jnp.int32)

    core = cc.astype(jnp.int32)
    rest_shard = _pack_rest({n: wts[n][0] for n in REST})
    comm = _StepComm(wts["w_in"][0].astype(MXU_DTYPE).reshape(2, D_MODEL // 2, IN_SHARD),
                     rest_shard.astype(MXU_DTYPE).reshape(2, REST_ROWS // 2, PACK_W), chip, core)

    t_len = x.shape[1]
    loss_row, grad_x, g_big, g_small = _local_step(x.reshape(t_len, D_MODEL), p.reshape(t_len, -1),
                                                   loss_target.reshape(t_len, D_MODEL), None,
                                                   {n: wts[n] for n in SMALL}, comm)

    zero = jnp.zeros((), F32)
    sg, sd, snm, snv = _small_reduce_adamw(_pack_small(g_small, loss_row[0, 0]),
                                           _pack_small({n: wts[n] for n in SMALL}, zero),
                                           _pack_small({n: mom[n] for n in SMALL}, zero),
                                           _pack_small({n: var[n] for n in SMALL}, zero))
    (sg, loss), (sd, _), (snm, _), (snv, _) = (_unpack_small(a) for a in (sg, sd, snm, snv))

    halves = [comm.reduced("in"), comm.reduced("rest")]
    g_in, g_rest = [lax.dynamic_update_slice(got, mine[None], (core, 0, 0))
                    for got, mine in zip(_share_halves(halves), halves)]
    g_in, g_rest = g_in.reshape(D_MODEL, IN_SHARD), g_rest.reshape(REST_ROWS, PACK_W)

    def adam_f(wv, gv, mv, vv):
        return _adamw(wv, gv, mv, vv)

    d_in, nm_in, nv_in = _rowwise("adamw_in", adam_f, D_MODEL, IN_TILE,
                                  [(wts["w_in"][0], IN_SHARD, 0), (g_in, IN_SHARD, 0), (mom["w_in"][0], IN_SHARD, 0),
                                   (var["w_in"][0], IN_SHARD, 0)], [], [(IN_SHARD, F32)] * 3)
    d_rest, nm_rest, nv_rest = _rowwise("adamw_rest", adam_f, REST_ROWS, REST_TILE,
                                        [(rest_shard, PACK_W, 0), (g_rest, PACK_W, 0),
                                         (_pack_rest({n: mom[n][0] for n in REST}), PACK_W, 0),
                                         (_pack_rest({n: var[n][0] for n in REST}), PACK_W, 0)], [],
                                        [(PACK_W, F32)] * 3)
    bg, bd, bnm, bnv = (dict(_unpack_rest(rest), w_in=a.reshape(1, D_MODEL, IN_SHARD))
                        for rest, a in ((g_rest, g_in), (d_rest, d_in), (nm_rest, nm_in), (nv_rest, nv_in)))

    outs = [loss, grad_x.reshape(x.shape)]
    for small, big in ((sg, bg), (sd, bd), (snm, bnm), (snv, bnv)):
        outs += [big[n] if n in BIG else small[n] for n in WEIGHTS]
    return tuple(outs)
```

```python
import functools
from typing import Callable, NamedTuple

import jax
import jax.numpy as jnp
from jax import lax
from jax.experimental import pallas as pl
from jax.experimental.pallas import tpu as pltpu

F32 = jnp.float32
MXU_DTYPE = jnp.bfloat16
WIRE_DTYPE = jnp.bfloat16
NORM_EPS = 1e-6
D_MODEL = 1024
HG_HEADS = 8
HG_DIM = 128
HG_CHUNK = 64
S5_WIDTH = 512
S5_GROUPS = 32
S5_GROUP = 16
S5_STATE = 64
S5_LANES = S5_GROUPS * S5_STATE
IN_COLS = 7168
SUBLANES = 8
VMEM_LIMIT = 56 * 1024 * 1024
HIGHEST = lax.Precision.HIGHEST
MESH = pl.DeviceIdType.MESH

ADAM_LR, ADAM_B1, ADAM_B2, ADAM_EPS, ADAM_WD, ADAM_STEP = 0.001, 0.9, 0.999, 1e-08, 0.01, 10

BIG = ("w_in", "w_o_hg", "w_glu", "w_o_s5", "w_out", "w_ple", "w_ple_gate")
BIG_SHAPE = {"w_in": (1024, 7168), "w_o_hg": (1024, 1024), "w_glu": (512, 1024), "w_o_s5": (512, 1024),
             "w_out": (1024, 1024), "w_ple": (256, 1024), "w_ple_gate": (1024, 1024)}
BIG_COL_SHARDED = ("w_in", "w_glu", "w_o_s5", "w_ple")
SMALL = ("norm_g", "hg_lb", "hg_norm_g", "s5_a_re", "s5_a_im", "s5_log_dt", "s5_b_re", "s5_b_im", "s5_c_re",
         "s5_c_im", "s5_d", "b_glu", "ple_norm_g", "final_norm_g")
SMALL_SHAPE = {"norm_g": (1, 1024), "hg_lb": (2, 1024), "hg_norm_g": (1, 1024), "s5_a_re": (1, 32, 64),
               "s5_a_im": (1, 32, 64), "s5_log_dt": (1, 32), "s5_b_re": (1, 32, 64, 16), "s5_b_im": (1, 32, 64, 16),
               "s5_c_re": (1, 32, 16, 64), "s5_c_im": (1, 32, 16, 64), "s5_d": (1, 32, 16), "b_glu": (1, 1024),
               "ple_norm_g": (1, 1024), "final_norm_g": (1024,)}
WEIGHTS = ("norm_g", "w_in", "hg_lb", "hg_norm_g", "w_o_hg", "s5_a_re", "s5_a_im", "s5_log_dt", "s5_b_re", "s5_b_im",
           "s5_c_re", "s5_c_im", "s5_d", "w_glu", "b_glu", "w_o_s5", "w_out", "ple_norm_g", "w_ple", "w_ple_gate",
           "final_norm_g")
N_CHIPS = 4
N_DEV = 8
PACK_W = 1024
SMALL_ROWS = 144


def _params(*sem):
    return pltpu.CompilerParams(dimension_semantics=sem, vmem_limit_bytes=VMEM_LIMIT)


def _sig(x):
    return 1.0 / (1.0 + jnp.exp(-x))


def _dsilu(z, s):
    return s * (1.0 + z * (1.0 - s))


def _mx(x):
    return x.astype(MXU_DTYPE)


def _dot(a, b, dims=(((1,), (0,)), ((), ()))):
    return lax.dot_general(_mx(a), _mx(b), dims, preferred_element_type=F32)


_NT = (((1,), (1,)), ((), ()))
_TN = (((0,), (0,)), ((), ()))


def _dot32(a, b):
    return jnp.dot(a, b, precision=HIGHEST, preferred_element_type=F32)


def _rms_bwd(dy, x, g):
    r = lax.rsqrt(jnp.mean(x * x, axis=-1, keepdims=True) + NORM_EPS)
    t = dy * g
    dx = r * t - x * (r * r * r) * jnp.mean(t * x, axis=-1, keepdims=True)
    return dx, jnp.sum(dy * x * r, axis=0, keepdims=True)


def _rowwise(name, fn, n_rows_total, tm, rows, consts, outs, accs=(), alias=None):
    n_r, n_c, n_o, n_a = len(rows), len(consts), len(outs), len(accs)

    def body(*refs):
        row_refs = refs[:n_r]
        const_refs = refs[n_r:n_r + n_c]
        pos = n_r + n_c + (1 if alias is not None else 0)
        out_refs = refs[pos:pos + n_o]
        acc_refs = refs[pos + n_o:pos + n_o + n_a]
        res = fn(*[r[...] for r in row_refs], *[r[...] for r in const_refs])
        for r, v in zip(out_refs, res[:n_o]):
            r[...] = v.astype(r.dtype)
        if n_a:
            @pl.when(pl.program_id(0) == 0)
            def _():
                for r in acc_refs:
                    r[...] = jnp.zeros_like(r)
            for r, v in zip(acc_refs, res[n_o:]):
                r[...] += v

    in_specs = [pl.BlockSpec((tm, w), functools.partial(lambda i, cb: (i, cb), cb=cb)) for (_, w, cb) in rows]
    in_specs += [pl.BlockSpec(c.shape, lambda i: (0, 0)) for c in consts]
    args = [a for (a, _, _) in rows] + list(consts)
    out_shape, out_specs = [], []
    for o in outs:
        w, dt = o[0], o[1]
        cb, total = (o[2], o[3]) if len(o) == 4 else (0, w)
        out_shape.append(jax.ShapeDtypeStruct((n_rows_total, total), dt))
        out_specs.append(pl.BlockSpec((tm, w), functools.partial(lambda i, cb: (i, cb), cb=cb)))
    io_alias = {}
    if alias is not None:
        in_specs.append(pl.BlockSpec(memory_space=pl.ANY))
        args.append(alias[0])
        io_alias = {len(args) - 1: alias[1]}
    for (r, w) in accs:
        out_shape.append(jax.ShapeDtypeStruct((r, w), F32))
        out_specs.append(pl.BlockSpec((r, w), lambda i: (0, 0)))
    res = pl.pallas_call(body, name=name, grid=(n_rows_total // tm,), in_specs=in_specs, out_specs=out_specs,
                         out_shape=out_shape, input_output_aliases=io_alias,
                         compiler_params=_params("arbitrary"))(*args)
    return res


class _Riding(NamedTuple):
    ins: tuple
    outs: tuple
    n_sems: int
    start: Callable
    wait: Callable


_HBM = pl.BlockSpec(memory_space=pl.ANY)


def _ride(riding, refs, n_in, n_out, n_scratch, first, last):
    if riding is None:
        return refs[:n_in], refs[n_in:n_in + n_out], refs[n_in + n_out:]
    r_in, r_out = len(riding.ins), len(riding.outs)
    ins, rins = refs[:n_in], refs[n_in:n_in + r_in]
    pos = n_in + r_in
    outs, routs = refs[pos:pos + n_out], refs[pos + n_out:pos + n_out + r_out]
    pos += n_out + r_out
    scratch, (send_sems, recv_sems) = refs[pos:pos + n_scratch], refs[pos + n_scratch:]

    @pl.when(first)
    def _():
        riding.start(rins, routs, send_sems, recv_sems)

    @pl.when(last)
    def _():
        riding.wait(rins, routs, send_sems, recv_sems)

    return ins, outs, scratch


def _riding_call(riding, body, name, grid, in_specs, args, out_specs, out_shape, scratch, io_alias=None):
    if riding is not None:
        in_specs = list(in_specs) + [_HBM] * len(riding.ins)
        args = list(args) + list(riding.ins)
        out_specs = list(out_specs) + [_HBM] * len(riding.outs)
        out_shape = list(out_shape) + list(riding.outs)
        scratch = list(scratch) + [pltpu.SemaphoreType.DMA((riding.n_sems,))] * 2
    return pl.pallas_call(body, name=name, grid=grid, in_specs=in_specs, out_specs=out_specs, out_shape=out_shape,
                          scratch_shapes=scratch, input_output_aliases=io_alias or {},
                          compiler_params=_params(*(["arbitrary"] * len(grid))))(*args)


def _mm_nn(name, a, b, tm, tn, riding=None, prologue=None, consts=()):
    m, k = a.shape
    n = b.shape[1] if b.ndim == 2 else b.shape[0] * b.shape[2]
    grid = (n // tn, m // tm)
    n_out, scratch = (1, []) if prologue is None else (2, [pltpu.VMEM((m, k), MXU_DTYPE)])

    def body(*refs):
        j, i = pl.program_id(0), pl.program_id(1)
        ins, outs, kept = _ride(riding, refs, 2 + len(consts), n_out, len(scratch), (j == 0) & (i == 0),
                                (j == grid[0] - 1) & (i == grid[1] - 1))
        if prologue is None:
            left = ins[0][...]
        else:
            rows = pl.ds(pl.multiple_of(i * tm, tm), tm)

            @pl.when(j == 0)
            def _():
                tile = _mx(prologue(ins[0][...], *[c[...] for c in ins[2:]]))
                kept[0][rows, :] = tile
                outs[1][...] = tile

            left = kept[0][rows, :]
        outs[0][...] = _dot(left, ins[1][...])

    once = (lambda j, i: (i, 0)) if prologue is None else (lambda j, i: (jnp.where(j == 0, i, grid[1] - 1), 0))
    b_spec = (pl.BlockSpec((k, tn), lambda j, i: (0, j)) if b.ndim == 2
              else pl.BlockSpec((None, k, tn), lambda j, i: (j, 0, 0)))
    in_specs = [pl.BlockSpec((tm, k), once), b_spec]
    in_specs += [pl.BlockSpec(c.shape, lambda j, i: (0, 0)) for c in consts]
    out_specs = [pl.BlockSpec((tm, tn), lambda j, i: (i, j))]
    out_shape = [jax.ShapeDtypeStruct((m, n), F32)]
    if prologue is not None:
        out_specs.append(pl.BlockSpec((tm, k), once))
        out_shape.append(jax.ShapeDtypeStruct((m, k), MXU_DTYPE))
    res = _riding_call(riding, body, name, grid, in_specs, [a, b] + list(consts), out_specs, out_shape, scratch)
    return res[0] if riding is None and prologue is None else res


def _mm_nt_then(name, a, b, tm, tn, fn, rows, consts, outs, accs=(), alias=None, riding=None):
    m, n = a.shape
    k = b.shape[-2]
    steps = n // tn
    n_r, n_c, n_o, n_a = len(rows), len(consts), len(outs), len(accs)

    def body(*refs):
        a_ref, b_ref = refs[:2]
        row_refs = refs[2:2 + n_r]
        const_refs = refs[2 + n_r:2 + n_r + n_c]
        i, s = pl.program_id(0), pl.program_id(1)
        n_in = 2 + n_r + n_c + (1 if alias is not None else 0)
        _, outs_, (mm_ref,) = _ride(riding, refs, n_in, n_o + n_a, 1, (i == 0) & (s == 0),
                                    (i == m // tm - 1) & (s == steps - 1))
        out_refs, acc_refs = outs_[:n_o], outs_[n_o:]
        part = _dot(a_ref[...], b_ref[...] if b.ndim == 2 else b_ref[s], _NT)
        if steps > 1:
            @pl.when(s == 0)
            def _():
                mm_ref[...] = jnp.zeros_like(mm_ref)
            mm_ref[...] += part

        @pl.when(s == steps - 1)
        def _():
            res = fn(mm_ref[...] if steps > 1 else part, *[r[...] for r in row_refs], *[r[...] for r in const_refs])
            for r, v in zip(out_refs, res[:n_o]):
                r[...] = v.astype(r.dtype)
            if n_a:
                @pl.when(i == 0)
                def _():
                    for r in acc_refs:
                        r[...] = jnp.zeros_like(r)
                for r, v in zip(acc_refs, res[n_o:]):
                    r[...] += v

    b_spec = (pl.BlockSpec((k, tn), lambda i, s: (0, s)) if b.ndim == 2
              else pl.BlockSpec(memory_space=pltpu.VMEM))
    in_specs = [pl.BlockSpec((tm, tn), lambda i, s: (i, s)), b_spec]
    in_specs += [pl.BlockSpec((tm, w), functools.partial(lambda i, s, cb: (i, cb), cb=cb)) for (_, w, cb) in rows]
    in_specs += [pl.BlockSpec(c.shape, lambda i, s: (0, 0)) for c in consts]
    args = [a, b] + [r[0] for r in rows] + list(consts)
    out_shape, out_specs = [], []
    for o in outs:
        w, dt = o[0], o[1]
        cb, total = (o[2], o[3]) if len(o) == 4 else (0, w)
        out_shape.append(jax.ShapeDtypeStruct((m, total), dt))
        out_specs.append(pl.BlockSpec((tm, w), functools.partial(lambda i, s, cb: (i, cb), cb=cb)))
    io_alias = {}
    if alias is not None:
        in_specs.append(pl.BlockSpec(memory_space=pl.ANY))
        args.append(alias[0])
        io_alias = {len(args) - 1: alias[1]}
    for (r, w) in accs:
        out_shape.append(jax.ShapeDtypeStruct((r, w), F32))
        out_specs.append(pl.BlockSpec((r, w), lambda i, s: (0, 0)))
    return _riding_call(riding, body, name, (m // tm, steps), in_specs, args, out_specs, out_shape,
                        [pltpu.VMEM((tm, k), F32)], io_alias)


def _mm_tn(name, a, b, tk, tn, col_shards=False, riding=None):
    t, k = a.shape
    n = b.shape[1]
    steps = t // tk

    def body(*refs):
        j, s = pl.program_id(0), pl.program_id(1)
        (a_ref, b_ref), (o_ref,), (acc_ref,) = _ride(riding, refs, 2, 1, 1, (j == 0) & (s == 0),
                                                     (j == n // tn - 1) & (s == steps - 1))

        @pl.when(s == 0)
        def _():
            acc_ref[...] = jnp.zeros_like(acc_ref)

        acc_ref[...] += _dot(a_ref[...], b_ref[...], _TN)

        @pl.when(s == steps - 1)
        def _():
            o_ref[...] = acc_ref[...]

    if col_shards:
        out_spec = pl.BlockSpec((None, k, tn), lambda j, s: (j, 0, 0))
        out_shape = jax.ShapeDtypeStruct((n // tn, k, tn), F32)
    else:
        out_spec = pl.BlockSpec((k, tn), lambda j, s: (0, j))
        out_shape = jax.ShapeDtypeStruct((k, n), F32)
    res = _riding_call(riding, body, name, (n // tn, steps),
                       [pl.BlockSpec((tk, k), lambda j, s: (s, 0)), pl.BlockSpec((tk, tn), lambda j, s: (s, j))],
                       [a, b], [out_spec], [out_shape], [pltpu.VMEM((k, tn), F32)])
    return res[0] if riding is None else res


def _dot01(m01, x):
    m = m01.astype(MXU_DTYPE)
    hi = x.astype(MXU_DTYPE)
    r1 = x - hi.astype(F32)
    mid = r1.astype(MXU_DTYPE)
    lo = (r1 - mid.astype(F32)).astype(MXU_DTYPE)
    dot = lambda v: jnp.dot(m, v, preferred_element_type=F32)
    return dot(hi) + dot(mid) + dot(lo)


def _chunk_rows(x, offset, nck):
    return jnp.concatenate([jnp.broadcast_to(x[c * HG_CHUNK + offset:c * HG_CHUNK + offset + 1, :],
                                             (HG_CHUNK, x.shape[1])) for c in range(nck)], axis=0)


def _hg_block_terms(q, f, lb, tb):
    nck = tb // HG_CHUNK
    sig = _sig(f)
    fv = lb + (1.0 - lb) * sig
    kk = (1.0 - lb) * (1.0 - sig)
    row = lax.broadcasted_iota(jnp.int32, (tb, tb), 0)
    col = lax.broadcasted_iota(jnp.int32, (tb, tb), 1)
    same = jnp.right_shift(row, 6) == jnp.right_shift(col, 6)
    causal, anti = same & (row >= col), same & (row <= col)
    b = _dot01(causal, jnp.log(fv))
    b_mid, b_last = _chunk_rows(b, HG_CHUNK // 2 - 1, nck), _chunk_rows(b, HG_CHUNK - 1, nck)
    e_mid, e_mid_inv = jnp.exp(b - b_mid), jnp.exp(b_mid - b)
    e_b, e_last = jnp.exp(b), jnp.exp(b_last - b)
    dcs = [jnp.exp(b[c * HG_CHUNK + HG_CHUNK - 1:(c + 1) * HG_CHUNK, :]) for c in range(nck)]
    return sig, fv, kk, causal, anti, e_mid, e_mid_inv, e_b, e_last, dcs


def _hgrn2_fwd(proj, hg_lb, hg_norm_g, t_len, tb, riding=None):
    nck = tb // HG_CHUNK
    nb = t_len // tb

    def body(*refs):
        step = pl.program_id(0)
        ((p_ref, lb_ref, gn_ref), (o_ref, act_ref, sp_ref),
         (st_ref, a_s, bm_s, qd_s, kd_s, v_s, sc_s, inc_s)) = _ride(riding, refs, 3, 3, 8, step == 0, step == nb - 1)

        @pl.when(pl.program_id(0) == 0)
        def _():
            st_ref[...] = jnp.zeros_like(st_ref)

        lb = _sig(lb_ref[0:1, :] - lb_ref[1:2, :])
        q = p_ref[:, pl.ds(0, 1024)]
        _, _, kk, causal, _, e_mid, e_mid_inv, e_b, e_last, dcs = _hg_block_terms(q, p_ref[:, pl.ds(1024, 1024)],
                                                                                   lb, tb)
        a_s[...] = _mx(q * e_mid)
        bm_s[...] = _mx(kk * e_mid_inv)
        qd_s[...] = _mx(q * e_b)
        kd_s[...] = _mx(kk * e_last)
        v_s[...] = _mx(p_ref[:, pl.ds(2048, 1024)])
        heads = [pl.ds(h * HG_DIM, HG_DIM) for h in range(HG_HEADS)]
        chunks = [pl.ds(c * HG_CHUNK, HG_CHUNK) for c in range(nck)]
        for h, hs in enumerate(heads):
            sc_s[h] = _mx(jnp.where(causal, _dot(a_s[:, hs], bm_s[:, hs], _NT), 0.0))
        for h, hs in enumerate(heads):
            o_ref[:, hs] = _dot(sc_s[h], v_s[:, hs])
        for h, hs in enumerate(heads):
            for c, r in enumerate(chunks):
                inc_s[h, c] = _dot(v_s[r, hs], kd_s[r, hs], _TN)
        for c in range(nck):
            for h in range(HG_HEADS):
                st = st_ref[h]
                sp_ref[h, c] = st
                st_ref[h] = dcs[c][:, h * HG_DIM:(h + 1) * HG_DIM] * st + inc_s[h, c]
        for c, r in enumerate(chunks):
            for h, hs in enumerate(heads):
                o_ref[r, hs] += _dot(qd_s[r, hs], sp_ref[h, c], _NT)
        for h, hs in enumerate(heads):
            o = o_ref[:, hs]
            rr = lax.rsqrt(jnp.mean(o * o, axis=-1, keepdims=True) + NORM_EPS)
            g = p_ref[:, pl.ds(3072 + h * HG_DIM, HG_DIM)]
            act_ref[:, hs] = (o * rr * gn_ref[:, hs] * (g * _sig(g))).astype(act_ref.dtype)

    return _riding_call(
        riding, body, "hgrn2_fwd", (nb,),
        [pl.BlockSpec((tb, 4096), lambda i: (i, 0)), pl.BlockSpec((2, 1024), lambda i: (0, 0)),
         pl.BlockSpec((1, 1024), lambda i: (0, 0))],
        [proj, hg_lb, hg_norm_g],
        [pl.BlockSpec((tb, 1024), lambda i: (i, 0)), pl.BlockSpec((tb, 1024), lambda i: (i, 0)),
         pl.BlockSpec((HG_HEADS, nck, HG_DIM, HG_DIM), lambda i: (0, i, 0, 0))],
        [jax.ShapeDtypeStruct((t_len, 1024), F32), jax.ShapeDtypeStruct((t_len, 1024), MXU_DTYPE),
         jax.ShapeDtypeStruct((HG_HEADS, t_len // HG_CHUNK, HG_DIM, HG_DIM), F32)],
        [pltpu.VMEM((HG_HEADS, HG_DIM, HG_DIM), F32)] + [pltpu.VMEM((tb, 1024), MXU_DTYPE)] * 5
        + [pltpu.VMEM((HG_HEADS, tb, tb), MXU_DTYPE), pltpu.VMEM((HG_HEADS, nck, HG_DIM, HG_DIM), F32)])


def _hgrn2_bwd(proj, d_o, s_prev, hg_lb, dproj, t_len, tb, riding=None):
    nck = tb // HG_CHUNK
    nb = t_len // tb

    def body(*refs):
        step = pl.program_id(0)
        ((p_ref, do_ref, sp_ref, lb_ref, _), (dp_ref, dlb_ref),
         (ds_ref, acc_ref, a_s, bm_s, qd_s, kd_s, v_s, do_s, da_s, dbm_s, dqd_s, dkd_s, dv_s, ex_s, sc_s, dsc_s,
          up_s)) = _ride(riding, refs, 5, 2, 17, step == 0, step == nb - 1)

        @pl.when(pl.program_id(0) == 0)
        def _():
            ds_ref[...] = jnp.zeros_like(ds_ref)
            acc_ref[...] = jnp.zeros_like(acc_ref)

        lb = _sig(lb_ref[0:1, :] - lb_ref[1:2, :])
        q = p_ref[:, pl.ds(0, 1024)]
        sig, fv, kk, causal, anti, e_mid, e_mid_inv, e_b, e_last, dcs = _hg_block_terms(
            q, p_ref[:, pl.ds(1024, 1024)], lb, tb)
        a, bm, qd, kd = q * e_mid, kk * e_mid_inv, q * e_b, kk * e_last
        a_s[...] = _mx(a)
        bm_s[...] = _mx(bm)
        qd_s[...] = _mx(qd)
        kd_s[...] = _mx(kd)
        v_s[...] = _mx(p_ref[:, pl.ds(2048, 1024)])
        do_s[...] = _mx(do_ref[...])
        heads = [pl.ds(h * HG_DIM, HG_DIM) for h in range(HG_HEADS)]
        chunks = [pl.ds(c * HG_CHUNK, HG_CHUNK) for c in range(nck)]
        for h, hs in enumerate(heads):
            sc_s[h] = _mx(jnp.where(causal, _dot(a_s[:, hs], bm_s[:, hs], _NT), 0.0))
            dsc_s[h] = _mx(jnp.where(causal, _dot(do_s[:, hs], v_s[:, hs], _NT), 0.0))
        for h, hs in enumerate(heads):
            dv_s[:, hs] = _dot(sc_s[h], do_s[:, hs], _TN)
            da_s[:, hs] = _dot(dsc_s[h], bm_s[:, hs])
            dbm_s[:, hs] = _dot(dsc_s[h], a_s[:, hs], _TN)
        for h, hs in enumerate(heads):
            for c, r in enumerate(chunks):
                up_s[h, c] = _dot(do_s[r, hs], qd_s[r, hs], _TN)
                dqd_s[r, hs] = _dot(do_s[r, hs], sp_ref[h, c])
        for c in reversed(range(nck)):
            r = chunks[c]
            for h, hs in enumerate(heads):
                dst = ds_ref[h]
                dc = dcs[c][:, h * HG_DIM:(h + 1) * HG_DIM]
                dv_s[r, hs] += _dot(kd_s[r, hs], dst, _NT)
                dkd_s[r, hs] = _dot(v_s[r, hs], dst)
                ex_s[c:c + 1, hs] = jnp.sum(dst * sp_ref[h, c], axis=0, keepdims=True) * dc
                ds_ref[h] = up_s[h, c] + dc * dst
        da, dbm, dqd, dkd = da_s[...], dbm_s[...], dqd_s[...], dkd_s[...]
        dq = da * e_mid + dqd * e_b
        dk = dbm * e_mid_inv + dkd * e_last
        db = da * a - dbm * bm + dqd * qd - dkd * kd
        dkk = dkd * kd
        extra = jnp.concatenate(
            [jnp.broadcast_to(jnp.sum(dkk[c * HG_CHUNK:(c + 1) * HG_CHUNK], axis=0, keepdims=True)
                              + ex_s[c:c + 1, :], (HG_CHUNK, 1024)) for c in range(nck)], axis=0)
        dlogf = _dot01(anti, db) + extra
        dfv_k = dlogf / fv - dk
        dp_ref[:, pl.ds(0, 1024)] = dq.astype(dp_ref.dtype)
        dp_ref[:, pl.ds(1024, 1024)] = (dfv_k * (1.0 - lb) * sig * (1.0 - sig)).astype(dp_ref.dtype)
        dp_ref[:, pl.ds(2048, 1024)] = dv_s[...].astype(dp_ref.dtype)
        acc_ref[...] += jnp.sum(dfv_k * (1.0 - sig), axis=0, keepdims=True)

        @pl.when(pl.program_id(0) == nb - 1)
        def _():
            g0 = acc_ref[...] * lb * (1.0 - lb)
            dlb_ref[0:1, :] = g0
            dlb_ref[1:2, :] = -g0

    return _riding_call(
        riding, body, "hgrn2_bwd", (nb,),
        [pl.BlockSpec((tb, 3072), lambda i: (nb - 1 - i, 0)),
         pl.BlockSpec((tb, 1024), lambda i: (nb - 1 - i, 0)),
         pl.BlockSpec((HG_HEADS, nck, HG_DIM, HG_DIM), lambda i: (0, nb - 1 - i, 0, 0)),
         pl.BlockSpec((2, 1024), lambda i: (0, 0)),
         pl.BlockSpec(memory_space=pl.ANY)],
        [proj, d_o, s_prev, hg_lb, dproj],
        [pl.BlockSpec((tb, 3072), lambda i: (nb - 1 - i, 0)), pl.BlockSpec((2, 1024), lambda i: (0, 0))],
        [jax.ShapeDtypeStruct((t_len, IN_COLS), dproj.dtype), jax.ShapeDtypeStruct((2, 1024), F32)],
        [pltpu.VMEM((HG_HEADS, HG_DIM, HG_DIM), F32), pltpu.VMEM((1, 1024), F32)]
        + [pltpu.VMEM((tb, 1024), MXU_DTYPE)] * 6 + [pltpu.VMEM((tb, 1024), F32)] * 5
        + [pltpu.VMEM((SUBLANES, 1024), F32)] + [pltpu.VMEM((HG_HEADS, tb, tb), MXU_DTYPE)] * 2
        + [pltpu.VMEM((HG_HEADS, nck, HG_DIM, HG_DIM), F32)], {4: 0})


def _s5_prep_bwd(a_re, a_im, log_dt, b_re_t, b_im_t, dlam, dbbr, dbbi):
    def body(ar_ref, ai_ref, ldt_ref, br_ref, bi_ref, dlam_ref, dbbr_ref, dbbi_ref,
             dar_ref, dai_ref, dldt_ref, dbr_ref, dbi_ref):
        ar, ai = ar_ref[...], ai_ref[...]
        dt = jnp.exp(ldt_ref[...])
        mag = jnp.exp(ar * dt)
        cs, sn = jnp.cos(ai * dt), jnp.sin(ai * dt)
        lr, li = mag * cs, mag * sn
        den = ar * ar + ai * ai
        nr = lr - 1.0
        sr = (nr * ar + li * ai) / den
        si = (li * ar - nr * ai) / den
        br, bi = br_ref[...], bi_ref[...]
        gbr, gbi = dbbr_ref[...], dbbi_ref[...]
        dbr_ref[...] = sr * gbr + si * gbi
        dbi_ref[...] = sr * gbi - si * gbr
        dsr = jnp.sum(gbr * br + gbi * bi, axis=0, keepdims=True)
        dsi = jnp.sum(gbi * br - gbr * bi, axis=0, keepdims=True)
        dnr = (dsr * ar - dsi * ai) / den
        dli = dlam_ref[1:2, :] + (dsr * ai + dsi * ar) / den
        dlr = dlam_ref[0:1, :] + dnr
        dden = -(dsr * sr + dsi * si) / den
        dar = (dsr * nr + dsi * li) / den + dden * 2.0 * ar
        dai = (dsr * li - dsi * nr) / den + dden * 2.0 * ai
        dmag = dlr * cs + dli * sn
        dth = mag * (dli * cs - dlr * sn)
        dar_ref[...] = dar + dmag * mag * dt
        dai_ref[...] = dai + dth * dt
        ddt = (dmag * mag * ar + dth * ai) * dt
        lane = lax.broadcasted_iota(jnp.int32, (S5_LANES, 128), 0) // S5_STATE
        grp = lax.broadcasted_iota(jnp.int32, (S5_LANES, 128), 1)
        dldt_ref[...] = _dot32(jnp.broadcast_to(ddt, (SUBLANES, S5_LANES)), (lane == grp).astype(F32))

    whole = pl.BlockSpec(memory_space=pltpu.VMEM)
    return pl.pallas_call(
        body, name="s5_prep_bwd", in_specs=[whole] * 8, out_specs=[whole] * 5,
        out_shape=[jax.ShapeDtypeStruct((1, S5_LANES), F32), jax.ShapeDtypeStruct((1, S5_LANES), F32),
                   jax.ShapeDtypeStruct((SUBLANES, 128), F32), jax.ShapeDtypeStruct((S5_GROUP, S5_LANES), F32),
                   jax.ShapeDtypeStruct((S5_GROUP, S5_LANES), F32)])(a_re, a_im, log_dt, b_re_t, b_im_t, dlam, dbbr,
                                                                      dbbi)


def _dgelu(x):
    c, a = 0.7978845608028654, 0.044715
    th = jnp.tanh(c * (x + a * x * x * x))
    return 0.5 * (1.0 + th) + 0.5 * x * (1.0 - th * th) * c * (1.0 + 3.0 * a * x * x)


S5_BLOCKS = 4
S5_BW = S5_WIDTH // S5_BLOCKS
S5_BL = S5_LANES // S5_BLOCKS
S5_LANE_BLOCKS = S5_LANES // 128
S5_SCAN_BLOCKS = 4


def _s5_prep(a_re, a_im, log_dt, b_re_t, b_im_t, seg):
    def body(ar_ref, ai_ref, ldt_ref, br_ref, bi_ref,
             rows_f, pfr_ref, pfi_ref, rows_r, prr_ref, pri_ref, bbr_ref, bbi_ref):
        ar, ai = ar_ref[...], ai_ref[...]
        dt = jnp.exp(ldt_ref[...])
        mag = jnp.exp(ar * dt)
        lr, li = mag * jnp.cos(ai * dt), mag * jnp.sin(ai * dt)
        den = ar * ar + ai * ai
        nr = lr - 1.0
        sr = (nr * ar + li * ai) / den
        si = (li * ar - nr * ai) / den
        wide = (SUBLANES, S5_LANES)
        cr, ci = lr, li
        for i in range(seg):
            pfr_ref[i] = jnp.broadcast_to(cr, wide)
            pfi_ref[i] = jnp.broadcast_to(ci, wide)
            prr_ref[seg - 1 - i] = jnp.broadcast_to(cr, wide)
            pri_ref[seg - 1 - i] = jnp.broadcast_to(-ci, wide)
            if i == seg - 1:
                for rows, sign in ((rows_f, 1.0), (rows_r, -1.0)):
                    rows[0:1, :] = lr
                    rows[1:2, :] = sign * li
                    rows[2:3, :] = cr
                    rows[3:4, :] = sign * ci
            cr, ci = cr * lr - ci * li, cr * li + ci * lr
        br, bi = br_ref[...], bi_ref[...]
        bbr_ref[...] = sr * br - si * bi
        bbi_ref[...] = sr * bi + si * br

    whole = pl.BlockSpec(memory_space=pltpu.VMEM)
    tables = [jax.ShapeDtypeStruct((4, S5_LANES), F32)] + [jax.ShapeDtypeStruct((seg, SUBLANES, S5_LANES), F32)] * 2
    bbar = [jax.ShapeDtypeStruct((S5_GROUP, S5_LANES), F32)] * 2
    res = pl.pallas_call(body, name="s5_prep", in_specs=[whole] * 5, out_specs=[whole] * 8,
                         out_shape=tables + tables + bbar)(a_re, a_im, log_dt, b_re_t, b_im_t)
    return res[0:3], res[3:6], res[6], res[7]


def _lanes(j):
    return pl.ds(j * 128, 128)


def _to_segment_order(v, stage_ref, out_ref, seg):
    nbl = v.shape[1] // 128
    for b in range(nbl):
        stage_ref[b] = v[:, b * 128:(b + 1) * 128]

    def body(t, carry):
        rows = pl.ds(pl.multiple_of(t * SUBLANES, SUBLANES), SUBLANES)
        for b in range(nbl):
            out_ref[rows, _lanes(b)] = stage_ref[b, pl.ds(t, SUBLANES, stride=seg), :]
        return carry

    lax.fori_loop(0, seg, body, 0, unroll=True)


def _from_segment_order(v, stage_ref, out_ref, seg):
    nbl = v.shape[1] // 128
    for b in range(nbl):
        stage_ref[b] = v[:, b * 128:(b + 1) * 128]
    for s in range(SUBLANES):
        def body(k, carry, s=s):
            rows = pl.ds(pl.multiple_of(s * seg + k * SUBLANES, SUBLANES), SUBLANES)
            for b in range(nbl):
                out_ref[rows, _lanes(b)] = stage_ref[b, pl.ds(k * SUBLANES * SUBLANES + s, SUBLANES,
                                                              stride=SUBLANES), :]
            return carry

        lax.fori_loop(0, seg // SUBLANES, body, 0, unroll=True)


def _tile_scan(xr_ref, xi_ref, lam_ref, car_ref, cai_ref, cn_r, cn_i, blocks, seg, reverse):
    shape = (SUBLANES, 128)
    lrs = [jnp.broadcast_to(lam_ref[0:1, _lanes(j)], shape) for j in blocks]
    lis = [jnp.broadcast_to(lam_ref[1:2, _lanes(j)], shape) for j in blocks]

    def step(k, carry):
        t = seg - 1 - k if reverse else k
        rows = pl.ds(pl.multiple_of(t * SUBLANES, SUBLANES), SUBLANES)
        out = []
        for n, j in enumerate(blocks):
            cr, ci = carry[2 * n], carry[2 * n + 1]
            nr = lrs[n] * cr - lis[n] * ci + xr_ref[rows, _lanes(j)]
            ni = lrs[n] * ci + lis[n] * cr + xi_ref[rows, _lanes(j)]
            xr_ref[rows, _lanes(j)] = nr
            xi_ref[rows, _lanes(j)] = ni
            out += [nr, ni]
        return tuple(out)

    zero = jnp.zeros(shape, F32)
    fin = lax.fori_loop(0, seg, step, (zero,) * (2 * len(blocks)), unroll=True)
    for n, j in enumerate(blocks):
        ls = _lanes(j)
        fr, fi = fin[2 * n], fin[2 * n + 1]
        sr, si = lam_ref[2:3, ls], lam_ref[3:4, ls]
        pr, pi = car_ref[:, ls], cai_ref[:, ls]
        for s in (reversed(range(SUBLANES)) if reverse else range(SUBLANES)):
            cn_r[s:s + 1, ls] = pr
            cn_i[s:s + 1, ls] = pi
            pr, pi = fr[s:s + 1, :] + sr * pr - si * pi, fi[s:s + 1, :] + sr * pi + si * pr
        car_ref[:, ls] = pr
        cai_ref[:, ls] = pi


def _s5_fwd(proj, lam_rows, p3_re, p3_im, bbr4, bbi4, crt4, cit4, d_row, t_len, tb):
    seg = tb // SUBLANES

    def body(u_ref, lam_ref, p3r_ref, p3i_ref, bbr_ref, bbi_ref, crt_ref, cit_ref, d_ref,
             hr_ref, hi_ref, ypre_ref, ys_ref, car_ref, cai_ref, cn_r, cn_i, stage_ref, us_ref, yseg_ref):
        @pl.when(pl.program_id(0) == 0)
        def _():
            car_ref[...] = jnp.zeros_like(car_ref)
            cai_ref[...] = jnp.zeros_like(cai_ref)

        _to_segment_order(u_ref[...], stage_ref, us_ref, seg)
        u = us_ref[...]
        for i in range(S5_BLOCKS):
            ui = u[:, i * S5_BW:(i + 1) * S5_BW]
            hr_ref[:, pl.ds(i * S5_BL, S5_BL)] = _dot(ui, bbr_ref[i])
            hi_ref[:, pl.ds(i * S5_BL, S5_BL)] = _dot(ui, bbi_ref[i])
        for lc in range(S5_LANE_BLOCKS // S5_SCAN_BLOCKS):
            blocks = range(lc * S5_SCAN_BLOCKS, (lc + 1) * S5_SCAN_BLOCKS)
            _tile_scan(hr_ref, hi_ref, lam_ref, car_ref, cai_ref, cn_r, cn_i, blocks, seg, False)
            crs = [cn_r[:, _lanes(j)] for j in blocks]
            cis = [cn_i[:, _lanes(j)] for j in blocks]

            def fix(t, carry, blocks=blocks, crs=crs, cis=cis):
                rows = pl.ds(pl.multiple_of(t * SUBLANES, SUBLANES), SUBLANES)
                for n, j in enumerate(blocks):
                    pr, pi = p3r_ref[t, :, _lanes(j)], p3i_ref[t, :, _lanes(j)]
                    hr_ref[rows, _lanes(j)] += pr * crs[n] - pi * cis[n]
                    hi_ref[rows, _lanes(j)] += pr * cis[n] + pi * crs[n]
                return carry

            lax.fori_loop(0, seg, fix, 0, unroll=True)
        for i in range(S5_BLOCKS):
            ws = pl.ds(i * S5_BW, S5_BW)
            bl = pl.ds(i * S5_BL, S5_BL)
            yseg_ref[:, ws] = (_dot(hr_ref[:, bl], crt_ref[i]) - _dot(hi_ref[:, bl], cit_ref[i])
                               + d_ref[:, ws] * u[:, i * S5_BW:(i + 1) * S5_BW])
        _from_segment_order(yseg_ref[...], stage_ref, ypre_ref, seg)
        ys_ref[...] = jax.nn.gelu(ypre_ref[...], approximate=True).astype(ys_ref.dtype)

    whole = pl.BlockSpec(memory_space=pltpu.VMEM)
    return pl.pallas_call(
        body, name="s5_fwd", grid=(t_len // tb,),
        in_specs=[pl.BlockSpec((tb, S5_WIDTH), lambda i: (i, 4096 // S5_WIDTH))] + [whole] * 8,
        out_specs=[pl.BlockSpec((tb, S5_LANES), lambda i: (i, 0)), pl.BlockSpec((tb, S5_LANES), lambda i: (i, 0)),
                   pl.BlockSpec((tb, S5_WIDTH), lambda i: (i, 0)), pl.BlockSpec((tb, S5_WIDTH), lambda i: (i, 0))],
        out_shape=[jax.ShapeDtypeStruct((t_len, S5_LANES), F32), jax.ShapeDtypeStruct((t_len, S5_LANES), F32),
                   jax.ShapeDtypeStruct((t_len, S5_WIDTH), F32), jax.ShapeDtypeStruct((t_len, S5_WIDTH), MXU_DTYPE)],
        scratch_shapes=[pltpu.VMEM((1, S5_LANES), F32), pltpu.VMEM((1, S5_LANES), F32),
                        pltpu.VMEM((SUBLANES, S5_LANES), F32), pltpu.VMEM((SUBLANES, S5_LANES), F32),
                        pltpu.VMEM((S5_WIDTH // 128, tb, 128), F32), pltpu.VMEM((tb, S5_WIDTH), F32),
                        pltpu.VMEM((tb, S5_WIDTH), F32)],
        compiler_params=_params("arbitrary"))(proj, lam_rows, p3_re, p3_im, bbr4, bbi4, crt4, cit4, d_row)


def _s5_bwd(dgelu, y_pre, proj, h_re, h_im, lam_rows, p3_re, p3_im, bbr4, bbi4, cr4, ci4, d_row, dproj, t_len, tb):
    seg = tb // SUBLANES
    nb = t_len // tb

    def body(dg_ref, yp_ref, u_ref, hr_ref, hi_ref, lam_ref, p3r_ref, p3i_ref, bbr_ref, bbi_ref, cr_ref, ci_ref,
             d_ref, _, du_ref, dbbr_ref, dbbi_ref, dcr_ref, dci_ref, dd_ref, dlam_ref,
             gr_ref, gi_ref, car_ref, cai_ref, cn_r, cn_i, stage_ref, us_ref, dys_ref, duseg_ref):
        @pl.when(pl.program_id(0) == 0)
        def _():
            for ref in (car_ref, cai_ref, dbbr_ref, dbbi_ref, dcr_ref, dci_ref, dd_ref, dlam_ref):
                ref[...] = jnp.zeros_like(ref)

        _to_segment_order(u_ref[...], stage_ref, us_ref, seg)
        _to_segment_order(dg_ref[...] * _dgelu(yp_ref[...]), stage_ref, dys_ref, seg)
        u, dy = us_ref[...], dys_ref[...]
        for i in range(S5_BLOCKS):
            dyi = dy[:, i * S5_BW:(i + 1) * S5_BW]
            gr_ref[:, pl.ds(i * S5_BL, S5_BL)] = _dot(dyi, cr_ref[i])
            gi_ref[:, pl.ds(i * S5_BL, S5_BL)] = -_dot(dyi, ci_ref[i])
        for lc in range(S5_LANE_BLOCKS // S5_SCAN_BLOCKS):
            blocks = range(lc * S5_SCAN_BLOCKS, (lc + 1) * S5_SCAN_BLOCKS)
            _tile_scan(gr_ref, gi_ref, lam_ref, car_ref, cai_ref, cn_r, cn_i, blocks, seg, True)
            crs = [cn_r[:, _lanes(j)] for j in blocks]
            cis = [cn_i[:, _lanes(j)] for j in blocks]

            def fix(k, carry, blocks=blocks, crs=crs, cis=cis):
                t = seg - 1 - k
                rows = pl.ds(pl.multiple_of(t * SUBLANES, SUBLANES), SUBLANES)
                out = []
                for n, j in enumerate(blocks):
                    nr, ni, slr, sli = carry[4 * n:4 * n + 4]
                    pr, pi = p3r_ref[t, :, _lanes(j)], p3i_ref[t, :, _lanes(j)]
                    g_r = gr_ref[rows, _lanes(j)] + pr * crs[n] - pi * cis[n]
                    g_i = gi_ref[rows, _lanes(j)] + pr * cis[n] + pi * crs[n]
                    gr_ref[rows, _lanes(j)] = g_r
                    gi_ref[rows, _lanes(j)] = g_i
                    hr, hi = hr_ref[rows, _lanes(j)], hi_ref[rows, _lanes(j)]
                    out += [g_r, g_i, slr + nr * hr + ni * hi, sli + ni * hr - nr * hi]
                return tuple(out)

            zero = jnp.zeros((SUBLANES, 128), F32)
            init = []
            for n in range(len(blocks)):
                init += [crs[n], cis[n], zero, zero]
            fin = lax.fori_loop(0, seg, fix, tuple(init), unroll=True)
            for n, j in enumerate(blocks):
                dlam_ref[0:1, _lanes(j)] += jnp.sum(fin[4 * n + 2], axis=0, keepdims=True)
                dlam_ref[1:2, _lanes(j)] += jnp.sum(fin[4 * n + 3], axis=0, keepdims=True)
        for i in range(S5_BLOCKS):
            ws = pl.ds(i * S5_BW, S5_BW)
            bl = pl.ds(i * S5_BL, S5_BL)
            ui, dyi = u[:, i * S5_BW:(i + 1) * S5_BW], dy[:, i * S5_BW:(i + 1) * S5_BW]
            gr, gi = gr_ref[:, bl], gi_ref[:, bl]
            duseg_ref[:, ws] = _dot(gr, bbr_ref[i], _NT) + _dot(gi, bbi_ref[i], _NT) + d_ref[:, ws] * dyi
            dbbr_ref[i] += _dot(ui, gr, _TN)
            dbbi_ref[i] += _dot(ui, gi, _TN)
            dcr_ref[i] += _dot(hr_ref[:, bl], dyi, _TN)
            dci_ref[i] -= _dot(hi_ref[:, bl], dyi, _TN)
        dd_ref[...] += jnp.sum(dy * u, axis=0, keepdims=True)
        _from_segment_order(duseg_ref[...], stage_ref, duseg_ref, seg)
        du_ref[...] = duseg_ref[...].astype(du_ref.dtype)

    whole = pl.BlockSpec(memory_space=pltpu.VMEM)
    rev = lambda i: (nb - 1 - i, 0)
    const3 = lambda i: (0, 0, 0)
    return pl.pallas_call(
        body, name="s5_bwd", grid=(nb,),
        in_specs=[pl.BlockSpec((tb, S5_WIDTH), rev), pl.BlockSpec((tb, S5_WIDTH), rev),
                  pl.BlockSpec((tb, S5_WIDTH), lambda i: (nb - 1 - i, 4096 // S5_WIDTH)),
                  pl.BlockSpec((tb, S5_LANES), rev), pl.BlockSpec((tb, S5_LANES), rev)] + [whole] * 8
                 + [pl.BlockSpec(memory_space=pl.ANY)],
        out_specs=[pl.BlockSpec((tb, S5_WIDTH), lambda i: (nb - 1 - i, 4096 // S5_WIDTH)),
                   pl.BlockSpec((S5_BLOCKS, S5_BW, S5_BL), const3), pl.BlockSpec((S5_BLOCKS, S5_BW, S5_BL), const3),
                   pl.BlockSpec((S5_BLOCKS, S5_BL, S5_BW), const3), pl.BlockSpec((S5_BLOCKS, S5_BL, S5_BW), const3),
                   pl.BlockSpec((1, S5_WIDTH), lambda i: (0, 0)), pl.BlockSpec((2, S5_LANES), lambda i: (0, 0))],
        out_shape=[jax.ShapeDtypeStruct((t_len, IN_COLS), dproj.dtype),
                   jax.ShapeDtypeStruct((S5_BLOCKS, S5_BW, S5_BL), F32),
                   jax.ShapeDtypeStruct((S5_BLOCKS, S5_BW, S5_BL), F32),
                   jax.ShapeDtypeStruct((S5_BLOCKS, S5_BL, S5_BW), F32),
                   jax.ShapeDtypeStruct((S5_BLOCKS, S5_BL, S5_BW), F32),
                   jax.ShapeDtypeStruct((1, S5_WIDTH), F32), jax.ShapeDtypeStruct((2, S5_LANES), F32)],
        scratch_shapes=[pltpu.VMEM((tb, S5_LANES), F32), pltpu.VMEM((tb, S5_LANES), F32),
                        pltpu.VMEM((1, S5_LANES), F32), pltpu.VMEM((1, S5_LANES), F32),
                        pltpu.VMEM((SUBLANES, S5_LANES), F32), pltpu.VMEM((SUBLANES, S5_LANES), F32),
                        pltpu.VMEM((S5_WIDTH // 128, tb, 128), F32), pltpu.VMEM((tb, S5_WIDTH), F32),
                        pltpu.VMEM((tb, S5_WIDTH), F32), pltpu.VMEM((tb, S5_WIDTH), F32)],
        input_output_aliases={13: 0},
        compiler_params=_params("arbitrary"))(dgelu, y_pre, proj, h_re, h_im, lam_rows, p3_re, p3_im, bbr4, bbi4,
                                              cr4, ci4, d_row, dproj)


def _block_diag(per_group):
    g8 = S5_GROUPS // S5_BLOCKS
    eye = jnp.eye(g8, dtype=bool)[None, :, None, :, None]
    dense = jnp.where(eye, per_group.reshape(S5_BLOCKS, g8, S5_GROUP, 1, S5_STATE), 0.0)
    return dense.reshape(S5_BLOCKS, S5_BW, S5_BL)


def _diag_blocks(dense):
    g8 = S5_GROUPS // S5_BLOCKS
    ar = jnp.arange(g8)
    d5 = dense.reshape(S5_BLOCKS, g8, S5_GROUP, g8, S5_STATE)
    return d5[:, ar, :, ar, :].transpose(1, 0, 2, 3).reshape(S5_GROUPS, S5_GROUP, S5_STATE)


def _hg_gate_bwd(da, o, g, gn):
    dos, dgs, dgns = [], [], []
    for h in range(HG_HEADS):
        sl = slice(h * HG_DIM, (h + 1) * HG_DIM)
        oh, gh, dah, gnh = o[:, sl], g[:, sl], da[:, sl], gn[:, sl]
        rr = lax.rsqrt(jnp.mean(oh * oh, axis=-1, keepdims=True) + NORM_EPS)
        sg = _sig(gh)
        dgs.append(dah * (oh * rr * gnh) * _dsilu(gh, sg))
        don = dah * (gh * sg)
        t = don * gnh
        dos.append(rr * t - oh * (rr * rr * rr) * jnp.mean(t * oh, axis=-1, keepdims=True))
        dgns.append(jnp.sum(don * oh * rr, axis=0, keepdims=True))
    return jnp.concatenate(dos, axis=1), jnp.concatenate(dgs, axis=1), jnp.concatenate(dgns, axis=1)


MIX_BWD_COLS = ((3072, 1024), (4608, 512), (5120, 1024), (6144, 1024))


def _mix_bwd(dgl, h1, dh2, act_hg, ys2, ys_gelu, proj, o_hg, g2, ghn, b_glu, w, t_len, tm):
    nb = t_len // tm

    def body(dgl_ref, h1_ref, dh2_ref, act_ref, ys2_ref, ysg_ref, ghg_ref, z_ref, gh_ref, gs_ref, o_ref, g2_ref, gn_ref,
             bglu_ref, wg_ref, wo_ref, ws5_ref, whg_ref, wglu_ref,
             dh1_ref, dyh_ref, dys_ref, dglu_ref, dgelu_ref, do_ref, dg2_ref, dbglu_ref, dgn_ref, dproj_ref,
             st0, st1, st2, st3, sems):
        i = pl.program_id(0)
        stages = (st0, st1, st2, st3)

        def writes(step):
            rows = pl.ds(pl.multiple_of(step * tm, tm), tm)
            return [pltpu.make_async_copy(st, dproj_ref.at[rows, pl.ds(c0, wd)], sems.at[k])
                    for k, (st, (c0, wd)) in enumerate(zip(stages, MIX_BWD_COLS))]

        @pl.when(i > 0)
        def _():
            for cp in writes(i - 1):
                cp.wait()

        @pl.when(i == 0)
        def _():
            for ref in (dg2_ref, dbglu_ref, dgn_ref):
                ref[...] = jnp.zeros_like(ref)

        dx, dg2 = _rms_bwd(_dot(dgl_ref[...], wg_ref[...], _NT), h1_ref[...], g2_ref[...])
        dh1 = dh2_ref[...] + dx
        dh1_ref[...] = dh1
        dg2_ref[...] += dg2
        dm = _dot(dh1, wo_ref[...], _NT)
        sh, ss = _sig(gh_ref[...]), _sig(gs_ref[...])
        dyh, dys = _mx(dm * sh), _mx(dm * ss)
        dyh_ref[...] = dyh
        dys_ref[...] = dys
        st2[...] = (dm * _dot(act_ref[...], whg_ref[...]) * sh * (1.0 - sh)).astype(st2.dtype)
        st3[...] = (dm * _dot(ys2_ref[...], ws5_ref[...]) * ss * (1.0 - ss)).astype(st3.dtype)
        dys2 = _dot(dys, ws5_ref[...], _NT)
        gl_, z = _dot(ysg_ref[...], wglu_ref[...]) + bglu_ref[...], z_ref[...]
        a, b = gl_[:, :S5_WIDTH], gl_[:, S5_WIDTH:]
        sb, sz = _sig(b), _sig(z)
        silu = z * sz
        dglu = jnp.concatenate([dys2 * sb * silu, dys2 * a * silu * sb * (1.0 - sb)], axis=1)
        st1[...] = (dys2 * a * sb * _dsilu(z, sz)).astype(st1.dtype)
        dbglu_ref[...] += jnp.sum(dglu, axis=0, keepdims=True)
        dglu_ref[...] = _mx(dglu)
        dgelu_ref[...] = _dot(dglu, wglu_ref[...], _NT)
        d_o, dg, dgn = _hg_gate_bwd(_dot(dyh, whg_ref[...], _NT), o_ref[...], ghg_ref[...], gn_ref[...])
        do_ref[...] = d_o.astype(do_ref.dtype)
        st0[...] = dg.astype(st0.dtype)
        dgn_ref[...] += dgn
        for cp in writes(i):
            cp.start()

        @pl.when(i == nb - 1)
        def _():
            for cp in writes(i):
                cp.wait()

    tile = lambda wd, cb=0: pl.BlockSpec((tm, wd), functools.partial(lambda i, cb: (i, cb), cb=cb))
    row = lambda wd: pl.BlockSpec((1, wd), lambda i: (0, 0))
    whole = pl.BlockSpec(memory_space=pltpu.VMEM)
    return pl.pallas_call(
        body, name="mix_bwd", grid=(nb,),
        in_specs=[tile(1024), tile(1024), tile(1024), tile(1024), tile(512), tile(512), tile(1024, 3),
                  tile(512, 4608 // 512), tile(1024, 5), tile(1024, 6), tile(1024), row(1024), row(1024), row(1024)]
                 + [whole] * 5,
        out_specs=[tile(1024), tile(1024), tile(1024), tile(1024), tile(512), tile(1024), row(1024), row(1024),
                   row(1024), _HBM],
        out_shape=[jax.ShapeDtypeStruct((t_len, 1024), F32), jax.ShapeDtypeStruct((t_len, 1024), MXU_DTYPE),
                   jax.ShapeDtypeStruct((t_len, 1024), MXU_DTYPE), jax.ShapeDtypeStruct((t_len, 1024), MXU_DTYPE),
                   jax.ShapeDtypeStruct((t_len, 512), F32), jax.ShapeDtypeStruct((t_len, 1024), MXU_DTYPE),
                   jax.ShapeDtypeStruct((1, 1024), F32), jax.ShapeDtypeStruct((1, 1024), F32),
                   jax.ShapeDtypeStruct((1, 1024), F32), jax.ShapeDtypeStruct((t_len, IN_COLS), MXU_DTYPE)],
        scratch_shapes=[pltpu.VMEM((tm, wd), MXU_DTYPE) for _, wd in MIX_BWD_COLS] + [pltpu.SemaphoreType.DMA((4,))],
        compiler_params=_params("arbitrary"))(dgl, h1, dh2, act_hg, ys2, ys_gelu, proj, proj, proj, proj, o_hg, g2, ghn,
                                              b_glu, w["w_ple_gate"], w["w_out"], w["w_o_s5"], w["w_o_hg"],
                                              w["w_glu"])


def _local_step(x, p, target, w, sm, comm=None):
    t_len = x.shape[0]
    tm = min(256, t_len)
    tmm = min(512, t_len)
    tm_in = min(1024, t_len)
    tk = min(2048, t_len)
    tb_hg = min(256, t_len)
    tb_s5 = min(512, t_len)
    g1, g2, g3, ghn = sm["norm_g"], sm["ple_norm_g"], sm["final_norm_g"].reshape(1, D_MODEL), sm["hg_norm_g"]

    def rms_in(xv, g):
        return xv * lax.rsqrt(jnp.mean(xv * xv, axis=-1, keepdims=True) + NORM_EPS) * g

    in_shard = IN_COLS // N_CHIPS
    if comm is None:
        w_in = w["w_in"]
        proj, u = _mm_nn("mm_in", x, w_in, tm_in, in_shard, prologue=rms_in, consts=[g1])
    else:
        proj, u, w_in = comm.input_projection(x, g1, rms_in, tm_in)

    lanes = lambda a: a.reshape(1, S5_LANES)
    a_re, a_im = lanes(sm["s5_a_re"]), lanes(sm["s5_a_im"])
    ldt = lanes(jnp.broadcast_to(sm["s5_log_dt"].reshape(S5_GROUPS, 1), (S5_GROUPS, S5_STATE)))
    to_t = lambda b: b.reshape(S5_GROUPS, S5_STATE, S5_GROUP).transpose(2, 0, 1).reshape(S5_GROUP, S5_LANES)
    b_re_t, b_im_t = to_t(sm["s5_b_re"]), to_t(sm["s5_b_im"])
    scan_fwd, scan_rev, bbr_t, bbi_t = _s5_prep(a_re, a_im, ldt, b_re_t, b_im_t, tb_s5 // SUBLANES)
    from_t = lambda b: b.reshape(S5_GROUP, S5_GROUPS, S5_STATE).transpose(1, 0, 2)
    bbr_bd = _block_diag(from_t(bbr_t)).astype(MXU_DTYPE)
    bbi_bd = _block_diag(from_t(bbi_t)).astype(MXU_DTYPE)
    cr_bd = _block_diag(sm["s5_c_re"].reshape(S5_GROUPS, S5_GROUP, S5_STATE)).astype(MXU_DTYPE)
    ci_bd = _block_diag(sm["s5_c_im"].reshape(S5_GROUPS, S5_GROUP, S5_STATE)).astype(MXU_DTYPE)
    d_row = sm["s5_d"].reshape(1, S5_WIDTH)
    if comm is None:
        o_hg, act_hg, s_prev = _hgrn2_fwd(proj, sm["hg_lb"], ghn, t_len, tb_hg)
    else:
        o_hg, act_hg, s_prev, landed = _hgrn2_fwd(proj, sm["hg_lb"], ghn, t_len, tb_hg, riding=comm.gather_rest())
        w = comm.rest_weights(landed)
    h_re, h_im, y_pre, ys_gelu = _s5_fwd(proj, *scan_fwd, bbr_bd, bbi_bd,
                                          cr_bd.transpose(0, 2, 1), ci_bd.transpose(0, 2, 1), d_row, t_len, tb_s5)
    def mix_f(act, ysg, z, gh, gs, xv, w_glu, b_glu, w_o_hg, w_o_s5, w_out):
        yh = _dot(act, w_o_hg)
        gl_ = _dot(ysg, w_glu) + b_glu
        a, b = gl_[:, :S5_WIDTH], gl_[:, S5_WIDTH:]
        ys2_ = (a * _sig(b) * (z * _sig(z))).astype(MXU_DTYPE)
        ys = _dot(ys2_, w_o_s5)
        mg = (_sig(gh) * yh + _sig(gs) * ys).astype(MXU_DTYPE)
        return (ys2_, mg, xv + _dot(mg, w_out))

    ys2, merged, h1 = _rowwise(
        "mix_out", mix_f, t_len, tmm,
        [(act_hg, 1024, 0), (ys_gelu, 512, 0), (proj, 512, 4608 // 512), (proj, 1024, 5), (proj, 1024, 6),
         (x, 1024, 0)], [w["w_glu"], sm["b_glu"], w["w_o_hg"], w["w_o_s5"], w["w_out"]],
        [(512, MXU_DTYPE), (1024, MXU_DTYPE), (1024, F32)])

    def head_f(h1v, pv, tgt, g_ple, g, w_ple, w_gate):
        r2 = lax.rsqrt(jnp.mean(h1v * h1v, axis=-1, keepdims=True) + NORM_EPS)
        n2_ = (h1v * r2 * g_ple).astype(MXU_DTYPE)
        glv, pev = _dot(n2_, w_gate), _dot(pv, w_ple)
        gate = _sig(glv)
        h2 = h1v + pev * gate
        r = lax.rsqrt(jnp.mean(h2 * h2, axis=-1, keepdims=True) + NORM_EPS)
        e = h2 * r * g - tgt
        loss = 0.5 * jnp.sum(jnp.mean(e * e, axis=-1, keepdims=True), axis=0, keepdims=True)
        dy = e * (1.0 / D_MODEL)
        dg = jnp.sum(dy * h2 * r, axis=0, keepdims=True)
        t = dy * g
        dh2 = r * t - h2 * (r * r * r) * jnp.mean(t * h2, axis=-1, keepdims=True)
        dpe, dgl_ = _mx(dh2 * gate), _mx(dh2 * pev * gate * (1.0 - gate))
        return (dh2, dgl_, jnp.broadcast_to(loss, (1, 128)), dg, _dot(pv, dpe, _TN), _dot(n2_, dgl_, _TN))

    gb = {}
    dh2, dgl, loss_row, d_g3, gb["w_ple"], gb["w_ple_gate"] = _rowwise(
        "ple_loss_head", head_f, t_len, tmm, [(h1, 1024, 0), (p, 256, 0), (target, 1024, 0)],
        [g2, g3, w["w_ple"], w["w_ple_gate"]], [(1024, F32), (1024, MXU_DTYPE)],
        accs=[(1, 128), (1, 1024), (256, 1024), (1024, 1024)])

    dh1, dy_hg, dy_s5, dglu, dgelu, d_o, d_g2, d_bglu, d_ghn, dproj = _mix_bwd(
        dgl, h1, dh2, act_hg, ys2, ys_gelu, proj, o_hg, g2, ghn, sm["b_glu"], w, t_len, tm)
    gb["w_out"] = _mm_tn("mm_d_w_out", merged, dh1, tk, 1024)
    gb["w_o_s5"] = _mm_tn("mm_d_w_o_s5", ys2, dy_s5, tk, 1024)
    gb["w_glu"] = _mm_tn("mm_d_w_glu", ys_gelu, dglu, tk, 1024)
    dproj, d_bbr, d_bbi, d_crt, d_cit, d_d, d_lam = _s5_bwd(dgelu, y_pre, proj, h_re, h_im,
                                                            *scan_rev, bbr_bd, bbi_bd, cr_bd,
                                                            ci_bd, d_row, dproj, t_len, tb_s5)
    to_t3 = lambda b: b.transpose(1, 0, 2).reshape(S5_GROUP, S5_LANES)
    d_are, d_aim, d_ldt, d_br_t, d_bi_t = _s5_prep_bwd(a_re, a_im, ldt, b_re_t, b_im_t, d_lam,
                                                       to_t3(_diag_blocks(d_bbr)), to_t3(_diag_blocks(d_bbi)))
    gb["w_o_hg"] = _mm_tn("mm_d_w_o_hg", act_hg, dy_hg, tk, 1024)
    if comm is None:
        dproj, d_lb = _hgrn2_bwd(proj, d_o, s_prev, sm["hg_lb"], dproj, t_len, tb_hg)
    else:
        rest_grads = _pack_rest_full(gb)
        dproj, d_lb, rest_theirs = _hgrn2_bwd(proj, d_o, s_prev, sm["hg_lb"], dproj, t_len, tb_hg,
                                               riding=comm.swap(rest_grads))

    def in_b(duv, xv, dh, g):
        dx, dg = _rms_bwd(duv, xv, g)
        return (dh + dx, dg)

    in_args = ("mm_d_u_rms_in_bwd", dproj, w_in, tm_in, in_shard, in_b, [(x, 1024, 0), (dh1, 1024, 0)], [g1],
               [(1024, F32)])
    if comm is None:
        gb["w_in"] = _mm_tn("mm_d_w_in", u, dproj, tk, in_shard, col_shards=True)
        grad_x, d_g1 = _mm_nt_then(*in_args, accs=[(1, 1024)])
    else:
        gb["w_in"], landed = _mm_tn("mm_d_w_in", u, dproj, tk, in_shard, col_shards=True,
                                    riding=comm.scatter("rest", rest_grads, rest_theirs))
        comm.landed["rest"] = landed
        grad_x, d_g1, landed = _mm_nt_then(*in_args, accs=[(1, 1024)], riding=comm.scatter(
            "in", gb["w_in"].reshape(N_CHIPS, 2, D_MODEL // 2, in_shard)))
        comm.landed["in"] = landed

    back_t = lambda b: b.reshape(S5_GROUP, S5_GROUPS, S5_STATE).transpose(1, 2, 0).reshape(1, S5_GROUPS, S5_STATE,
                                                                                           S5_GROUP)
    gs = {
        "norm_g": d_g1, "hg_lb": d_lb, "hg_norm_g": d_ghn,
        "s5_a_re": d_are.reshape(1, S5_GROUPS, S5_STATE), "s5_a_im": d_aim.reshape(1, S5_GROUPS, S5_STATE),
        "s5_log_dt": d_ldt[0:1, :S5_GROUPS],
        "s5_b_re": back_t(d_br_t), "s5_b_im": back_t(d_bi_t),
        "s5_c_re": _diag_blocks(d_crt.transpose(0, 2, 1)).reshape(1, S5_GROUPS, S5_GROUP, S5_STATE),
        "s5_c_im": _diag_blocks(d_cit.transpose(0, 2, 1)).reshape(1, S5_GROUPS, S5_GROUP, S5_STATE),
        "s5_d": d_d.reshape(1, S5_GROUPS, S5_GROUP), "b_glu": d_bglu, "ple_norm_g": d_g2,
        "final_norm_g": d_g3.reshape(D_MODEL),
    }
    return loss_row, grad_x, gb, gs


def _shard_shape(name):
    r, c = BIG_SHAPE[name]
    return (r, c // N_CHIPS) if name in BIG_COL_SHARDED else (r // N_CHIPS, c)


def _pack_small(parts, last):
    flat = jnp.concatenate([parts[n].reshape(-1) for n in SMALL] + [last.reshape(-1)])
    return jnp.pad(flat, (0, SMALL_ROWS * PACK_W - flat.shape[0])).reshape(SMALL_ROWS, PACK_W)


def _unpack_small(packed):
    flat, out, off = packed.reshape(-1), {}, 0
    for n in SMALL:
        size = 1
        for d in SMALL_SHAPE[n]:
            size *= d
        out[n] = flat[off:off + size].reshape(SMALL_SHAPE[n])
        off += size
    return out, flat[off]


def _place():
    x, y, c = lax.axis_index("x"), lax.axis_index("y"), lax.axis_index("c")
    return x, y, c, [(1 - x, y), (x, 1 - y), (1 - x, 1 - y)]


def _remote(src, dst, send_sems, recv_sems, k, to):
    return pltpu.make_async_remote_copy(src_ref=src, dst_ref=dst, send_sem=send_sems.at[k], recv_sem=recv_sems.at[k],
                                        device_id=to, device_id_type=MESH)


REST = tuple(n for n in BIG if n != "w_in")
REST_ROWS = sum(BIG_SHAPE[n][0] * BIG_SHAPE[n][1] for n in REST) // (N_CHIPS * PACK_W)
IN_SHARD = IN_COLS // N_CHIPS
IN_TILE, REST_TILE = 256, 272


def _pack_rest(parts):
    return jnp.concatenate([parts[n].reshape(-1, PACK_W) for n in REST], axis=0)


def _unpack_rest(packed):
    out, off = {}, 0
    for n in REST:
        r, c = _shard_shape(n)
        rows = r * c // PACK_W
        out[n] = packed[off:off + rows].reshape(1, r, c)
        off += rows
    return out


def _unpack_rest_full(gathered):
    out, off = {}, 0
    for n in REST:
        r, c = _shard_shape(n)
        rows = r * c // PACK_W
        sh = gathered[:, off:off + rows].reshape(N_CHIPS, r, c)
        out[n] = sh.transpose(1, 0, 2).reshape(BIG_SHAPE[n]) if n in BIG_COL_SHARDED else sh.reshape(BIG_SHAPE[n])
        off += rows
    return out


def _pack_rest_full(full):
    parts = []
    for n in REST:
        r, c = _shard_shape(n)
        g = full[n]
        sh = g.reshape(BIG_SHAPE[n][0], N_CHIPS, c).transpose(1, 0, 2) if n in BIG_COL_SHARDED else g
        parts.append(sh.reshape(N_CHIPS, r * c // PACK_W, PACK_W))
    return jnp.concatenate(parts, axis=1).reshape(N_CHIPS, 2, REST_ROWS // 2, PACK_W)


def _swap_halves(pgs, name="exchange_halves"):
    n = len(pgs)

    def body(*refs):
        pg_refs, out_refs, (send_sems, recv_sems) = refs[:n], refs[n:2 * n], refs[2 * n:]
        x, y, c, _ = _place()
        cps = [_remote(pg_ref.at[j, 1 - c], out_ref.at[j], send_sems, recv_sems, N_CHIPS * g + j, (x, y, 1 - c))
               for g, (pg_ref, out_ref) in enumerate(zip(pg_refs, out_refs)) for j in range(N_CHIPS)]
        for cp in cps:
            cp.start()
        for cp in cps:
            cp.wait()

    return pl.pallas_call(
        body, name=name, in_specs=[_HBM] * n, out_specs=[_HBM] * n,
        out_shape=[jax.ShapeDtypeStruct((N_CHIPS,) + pg.shape[2:], pg.dtype) for pg in pgs],
        scratch_shapes=[pltpu.SemaphoreType.DMA((N_CHIPS * n,)), pltpu.SemaphoreType.DMA((N_CHIPS * n,))])(*pgs)


def _share_halves(gs):
    n = len(gs)

    def body(*refs):
        g_refs, out_refs, (send_sems, recv_sems) = refs[:n], refs[n:2 * n], refs[2 * n:]
        x, y, c, _ = _place()
        cps = [_remote(g_ref, out_ref.at[c], send_sems, recv_sems, g, (x, y, 1 - c))
               for g, (g_ref, out_ref) in enumerate(zip(g_refs, out_refs))]
        for cp in cps:
            cp.start()
        for g, (g_ref, out_ref) in enumerate(zip(g_refs, out_refs)):
            _remote(g_ref, out_ref.at[1 - c], send_sems, recv_sems, g, (x, y, 1 - c)).wait_recv()
        for cp in cps:
            cp.wait_send()

    return pl.pallas_call(
        body, name="share_half", in_specs=[_HBM] * n, out_specs=[_HBM] * n,
        out_shape=[jax.ShapeDtypeStruct((2,) + g.shape, g.dtype) for g in gs],
        scratch_shapes=[pltpu.SemaphoreType.DMA((n,)), pltpu.SemaphoreType.DMA((n,))])(*gs)


def _pair_sum(name, pg, theirs, c, tile):
    _, _, rows, width = pg.shape

    def body(c_ref, a_ref, b_ref, o_ref):
        o_ref[...] = (a_ref[...] + b_ref[...]).astype(o_ref.dtype)

    return pl.pallas_call(
        body, name=name,
        grid_spec=pltpu.PrefetchScalarGridSpec(
            num_scalar_prefetch=1, grid=(N_CHIPS, rows // tile),
            in_specs=[pl.BlockSpec((None, None, tile, width), lambda j, i, c_ref: (j, c_ref[0], i, 0)),
                      pl.BlockSpec((None, tile, width), lambda j, i, c_ref: (j, i, 0))],
            out_specs=pl.BlockSpec((None, tile, width), lambda j, i, c_ref: (j, i, 0))),
        out_shape=jax.ShapeDtypeStruct((N_CHIPS, rows, width), WIRE_DTYPE),
        compiler_params=_params("arbitrary", "arbitrary"))(c.reshape(1), pg, theirs)


def _chip_sum(name, ps, others, k, tile):
    _, rows, width = ps.shape

    def body(k_ref, a_ref, b_ref, o_ref):
        o_ref[...] = ((a_ref[...].astype(F32) + b_ref[0].astype(F32)) + b_ref[1].astype(F32)) + b_ref[2].astype(F32)

    return pl.pallas_call(
        body, name=name,
        grid_spec=pltpu.PrefetchScalarGridSpec(
            num_scalar_prefetch=1, grid=(rows // tile,),
            in_specs=[pl.BlockSpec((None, tile, width), lambda i, k_ref: (k_ref[0], i, 0)),
                      pl.BlockSpec((3, tile, width), lambda i, k_ref: (0, i, 0))],
            out_specs=pl.BlockSpec((tile, width), lambda i, k_ref: (i, 0))),
        out_shape=jax.ShapeDtypeStruct((rows, width), F32),
        compiler_params=_params("arbitrary"))(k.reshape(1), ps, others)


def _mm_in_gathering(x, g1, prologue, in_wire, chip, tm):
    m, k = x.shape
    half, ns = in_wire.shape[1:]
    nrow = m // tm

    def flip(j):
        return jnp.where(j == 1, 2, jnp.where(j == 2, 1, j))

    def body(k_ref, x_ref, g_ref, wire_ref, proj_ref, u_ref, all_ref, kept, b_ref, load_sems, send_sems, recv_sems):
        j, i = pl.program_id(0), pl.program_id(1)
        px, py, c, chips = _place()
        sibling = (px, py, 1 - c)

        def over_ici(r, chip_slot):
            cx, cy = chips[r]
            return _remote(wire_ref.at[c], all_ref.at[chip_slot, c], send_sems, recv_sems, r, (cx, cy, c))

        def to_sibling(r, half_slot):
            cx, cy = chips[r]
            return _remote(all_ref.at[2 * cx + cy, c], all_ref.at[2 * cx + cy, half_slot], send_sems, recv_sems,
                           3 + r, sibling)

        def loads(src, slot):
            return [pltpu.make_async_copy(src.at[h], b_ref.at[slot, pl.ds(h * half, half)], load_sems.at[h])
                    for h in range(2)]

        def shard(r):
            cx, cy = chips[r]
            over_ici(r, 2 * cx + cy).wait_recv()
            if r == 0:
                over_ici(2, 2 * px + py).start()
            to_sibling(r, c).start()
            to_sibling(r, 1 - c).wait_recv()
            return all_ref.at[2 * cx + cy]

        @pl.when((j == 0) & (i == 0))
        def _():
            for r in range(2):
                over_ici(r, 2 * px + py).start()
            for cp in loads(wire_ref, 0):
                cp.start()
            for cp in loads(wire_ref, 0):
                cp.wait()

        @pl.when((j == 1) & (i == 0))
        def _():
            cps = loads(shard(0), 1)
            for cp in cps:
                cp.start()
            for cp in cps:
                cp.wait()

        for nxt in (2, 3):
            @pl.when((j == nxt - 1) & (i == nrow // 2))
            def _(nxt=nxt):
                for cp in loads(shard(nxt - 1), nxt % 2):
                    cp.start()

            @pl.when((j == nxt) & (i == 0))
            def _(nxt=nxt):
                for cp in loads(wire_ref, nxt % 2):
                    cp.wait()

        rows = pl.ds(pl.multiple_of(i * tm, tm), tm)

        @pl.when(j == 0)
        def _():
            tile = _mx(prologue(x_ref[...], g_ref[...]))
            kept[rows, :] = tile
            u_ref[...] = tile

        proj_ref[...] = _dot(kept[rows, :], b_ref[lax.rem(j, 2)])

        @pl.when((j == N_CHIPS - 1) & (i == nrow - 1))
        def _():
            for r in range(3):
                over_ici(r, 2 * px + py).wait_send()
                to_sibling(r, c).wait_send()

    once = lambda j, i, k_ref: (jnp.where(j == 0, i, nrow - 1), 0)
    return pl.pallas_call(
        body, name="mm_in",
        grid_spec=pltpu.PrefetchScalarGridSpec(
            num_scalar_prefetch=1, grid=(N_CHIPS, nrow),
            in_specs=[pl.BlockSpec((tm, k), once), pl.BlockSpec(g1.shape, lambda j, i, k_ref: (0, 0)), _HBM],
            out_specs=[pl.BlockSpec((tm, ns), lambda j, i, k_ref: (i, jnp.bitwise_xor(k_ref[0], flip(j)))),
                       pl.BlockSpec((tm, k), once), _HBM],
            scratch_shapes=[pltpu.VMEM((m, k), MXU_DTYPE), pltpu.VMEM((2, 2 * half, ns), in_wire.dtype),
                            pltpu.SemaphoreType.DMA((2,)), pltpu.SemaphoreType.DMA((6,)),
                            pltpu.SemaphoreType.DMA((6,))]),
        out_shape=[jax.ShapeDtypeStruct((m, N_CHIPS * ns), F32), jax.ShapeDtypeStruct((m, k), MXU_DTYPE),
                   jax.ShapeDtypeStruct((N_CHIPS,) + in_wire.shape, in_wire.dtype)],
        compiler_params=_params("arbitrary", "arbitrary"))(chip.reshape(1), x, g1, in_wire)


class _StepComm:
    TILES = {"in": IN_TILE, "rest": REST_TILE}

    def __init__(self, in_wire, rest_wire, chip, core):
        self.in_wire, self.rest_wire, self.chip, self.core = in_wire, rest_wire, chip, core
        self.sums, self.landed = {}, {}

    def input_projection(self, x, g1, prologue, tm):
        proj, u, shards = _mm_in_gathering(x, g1, prologue, self.in_wire, self.chip, tm)
        shards = lax.dynamic_update_slice(shards, self.in_wire[None], (self.chip, 0, 0, 0))
        return proj, u, shards.reshape(N_CHIPS, D_MODEL, IN_SHARD)

    def gather_rest(self):
        wire = self.rest_wire

        def sends(ins, outs, send_sems, recv_sems):
            (w_ref,), (out_ref,) = ins, outs
            x, y, c, chips = _place()
            return [_remote(w_ref.at[c], out_ref.at[2 * x + y, c], send_sems, recv_sems, 4 * j + 2 * c + to,
                            (cx, cy, to)) for j, (cx, cy) in enumerate(chips) for to in (0, 1)]

        def recvs(ins, outs, send_sems, recv_sems):
            (w_ref,), (out_ref,) = ins, outs
            _, _, c, chips = _place()
            return [_remote(w_ref.at[c], out_ref.at[2 * cx + cy, by], send_sems, recv_sems, 4 * j + 2 * by + c,
                            (cx, cy, by)) for j, (cx, cy) in enumerate(chips) for by in (0, 1)]

        def start(*refs):
            for cp in sends(*refs):
                cp.start()

        def wait(*refs):
            for cp in recvs(*refs):
                cp.wait_recv()
            for cp in sends(*refs):
                cp.wait_send()

        return _Riding((wire,), (jax.ShapeDtypeStruct((N_CHIPS,) + wire.shape, wire.dtype),), 12, start, wait)

    def rest_weights(self, landed):
        full = lax.dynamic_update_slice(landed, self.rest_wire[None], (self.chip, 0, 0, 0))
        return _unpack_rest_full(full.reshape(N_CHIPS, REST_ROWS, PACK_W))

    def swap(self, pg):
        def copies(ins, outs, send_sems, recv_sems):
            (pg_ref,), (out_ref,) = ins, outs
            x, y, c, _ = _place()
            return [_remote(pg_ref.at[j, 1 - c], out_ref.at[j], send_sems, recv_sems, j, (x, y, 1 - c))
                    for j in range(N_CHIPS)]

        def start(*refs):
            for cp in copies(*refs):
                cp.start()

        def wait(*refs):
            for cp in copies(*refs):
                cp.wait()

        return _Riding((pg,), (jax.ShapeDtypeStruct((N_CHIPS,) + pg.shape[2:], pg.dtype),), N_CHIPS, start, wait)

    def scatter(self, group, pg, theirs=None):
        if theirs is None:
            (theirs,) = _swap_halves([pg], "exchange_halves_" + group)
        ps = _pair_sum("sum_pair_" + group, pg, theirs, self.core, self.TILES[group])
        self.sums[group] = ps

        def copies(ins, outs, send_sems, recv_sems):
            (ps_ref,), (out_ref,) = ins, outs
            _, _, c, chips = _place()
            return [_remote(ps_ref.at[2 * cx + cy], out_ref.at[j], send_sems, recv_sems, j, (cx, cy, c))
                    for j, (cx, cy) in enumerate(chips)]

        def start(*refs):
            for cp in copies(*refs):
                cp.start()

        def wait(*refs):
            for cp in copies(*refs):
                cp.wait()

        return _Riding((ps,), (jax.ShapeDtypeStruct((3,) + ps.shape[1:], ps.dtype),), 3, start, wait)

    def reduced(self, group):
        return _chip_sum("sum_chips_" + group, self.sums[group], self.landed[group], self.chip, self.TILES[group])


def _adamw(w, g, m, v):
    m = ADAM_B1 * m + (1.0 - ADAM_B1) * g
    v = ADAM_B2 * v + (1.0 - ADAM_B2) * (g * g)
    m_hat = m / (1.0 - ADAM_B1 ** ADAM_STEP)
    v_hat = v / (1.0 - ADAM_B2 ** ADAM_STEP)
    return -ADAM_LR * (m_hat / (jnp.sqrt(v_hat) + ADAM_EPS) + ADAM_WD * w), m, v


def _small_reduce_adamw(part, w, m, v):
    def body(part_ref, w_ref, m_ref, v_ref, g_ref, d_ref, nm_ref, nv_ref, all_ref, send_sems, recv_sems):
        x, y, c, chips = _place()
        me, sibling = (x, y, c), (x, y, 1 - c)

        def rows(px, py, pc):
            return all_ref.at[4 * px + 2 * py + pc]

        all_ref[4 * x + 2 * y + c] = part_ref[...]
        first = [_remote(part_ref, rows(*me), send_sems, recv_sems, 0, sibling)]
        first += [_remote(part_ref, rows(*me), send_sems, recv_sems, 1 + j, (cx, cy, c))
                  for j, (cx, cy) in enumerate(chips)]
        for cp in first:
            cp.start()
        passed = []
        for j, (cx, cy) in enumerate(chips):
            _remote(part_ref, rows(cx, cy, c), send_sems, recv_sems, 1 + j, me).wait_recv()
            cp = _remote(rows(cx, cy, c), rows(cx, cy, c), send_sems, recv_sems, 4 + j, sibling)
            cp.start()
            passed.append(cp)
        _remote(part_ref, rows(*sibling), send_sems, recv_sems, 0, me).wait_recv()
        for j, (cx, cy) in enumerate(chips):
            _remote(part_ref, rows(cx, cy, 1 - c), send_sems, recv_sems, 4 + j, me).wait_recv()
        for cp in first + passed:
            cp.wait_send()
        g = all_ref[0]
        for dev in range(1, N_DEV):
            g = g + all_ref[dev]
        delta, nm, nv = _adamw(w_ref[...], g, m_ref[...], v_ref[...])
        g_ref[...] = g
        d_ref[...] = delta
        nm_ref[...] = nm
        nv_ref[...] = nv

    whole = pl.BlockSpec(memory_space=pltpu.VMEM)
    shape = jax.ShapeDtypeStruct((SMALL_ROWS, PACK_W), F32)
    return pl.pallas_call(
        body, name="small_reduce_adamw", in_specs=[whole] * 4, out_specs=[whole] * 4, out_shape=[shape] * 4,
        scratch_shapes=[pltpu.VMEM((N_DEV, SMALL_ROWS, PACK_W), F32), pltpu.SemaphoreType.DMA((7,)),
                        pltpu.SemaphoreType.DMA((7,))],
        compiler_params=pltpu.CompilerParams(vmem_limit_bytes=VMEM_LIMIT))(part, w, m, v)


def kernel(x, p, norm_g, w_in, hg_lb, hg_norm_g, w_o_hg, s5_a_re, s5_a_im, s5_log_dt, s5_b_re, s5_b_im, s5_c_re, s5_c_im, s5_d, w_glu, b_glu, w_o_s5, w_out, ple_norm_g, w_ple, w_ple_gate, final_norm_g, loss_target, m_norm_g, m_w_in, m_hg_lb, m_hg_norm_g, m_w_o_hg, m_s5_a_re, m_s5_a_im, m_s5_log_dt, m_s5_b_re, m_s5_b_im, m_s5_c_re, m_s5_c_im, m_s5_d, m_w_glu, m_b_glu, m_w_o_s5, m_w_out, m_ple_norm_g, m_w_ple, m_w_ple_gate, m_final_norm_g, v_norm_g, v_w_in, v_hg_lb, v_hg_norm_g, v_w_o_hg, v_s5_a_re, v_s5_a_im, v_s5_log_dt, v_s5_b_re, v_s5_b_im, v_s5_c_re, v_s5_c_im, v_s5_d, v_w_glu, v_b_glu, v_w_o_s5, v_w_out, v_ple_norm_g, v_w_ple, v_w_ple_gate, v_final_norm_g):
    given = dict(locals())
    wts = {n: given[n] for n in WEIGHTS}
    mom = {n: given["m_" + n] for n in WEIGHTS}
    var = {n: given["v_" + n] for n in WEIGHTS}
    cx, cy, cc = lax.axis_index("x"), lax.axis_index("y"), lax.axis_index("c")
    chip = (2 * cx + cy).astype(jnp.int32)

    core = cc.astype(jnp.int32)
    rest_shard = _pack_rest({n: wts[n][0] for n in REST})
    comm = _StepComm(wts["w_in"][0].astype(MXU_DTYPE).reshape(2, D_MODEL // 2, IN_SHARD),
                     rest_shard.astype(MXU_DTYPE).reshape(2, REST_ROWS // 2, PACK_W), chip, core)

    t_len = x.shape[1]
    loss_row, grad_x, g_big, g_small = _local_step(x.reshape(t_len, D_MODEL), p.reshape(t_len, -1),
                                                   loss_target.reshape(t_len, D_MODEL), None,
                                                   {n: wts[n] for n in SMALL}, comm)

    zero = jnp.zeros((), F32)
    sg, sd, snm, snv = _small_reduce_adamw(_pack_small(g_small, loss_row[0, 0]),
                                           _pack_small({n: wts[n] for n in SMALL}, zero),
                                           _pack_small({n: mom[n] for n in SMALL}, zero),
                                           _pack_small({n: var[n] for n in SMALL}, zero))
    (sg, loss), (sd, _), (snm, _), (snv, _) = (_unpack_small(a) for a in (sg, sd, snm, snv))

    halves = [comm.reduced("in"), comm.reduced("rest")]
    g_in, g_rest = [lax.dynamic_update_slice(got, mine[None], (core, 0, 0))
                    for got, mine in zip(_share_halves(halves), halves)]
    g_in, g_rest = g_in.reshape(D_MODEL, IN_SHARD), g_rest.reshape(REST_ROWS, PACK_W)

    def adam_f(wv, gv, mv, vv):
        return _adamw(wv, gv, mv, vv)

    d_in, nm_in, nv_in = _rowwise("adamw_in", adam_f, D_MODEL, IN_TILE,
                                  [(wts["w_in"][0], IN_SHARD, 0), (g_in, IN_SHARD, 0), (mom["w_in"][0], IN_SHARD, 0),
                                   (var["w_in"][0], IN_SHARD, 0)], [], [(IN_SHARD, F32)] * 3)
    d_rest, nm_rest, nv_rest = _rowwise("adamw_rest", adam_f, REST_ROWS, REST_TILE,
                                        [(rest_shard, PACK_W, 0), (g_rest, PACK_W, 0),
                                         (_pack_rest({n: mom[n][0] for n in REST}), PACK_W, 0),
                                         (_pack_rest({n: var[n][0] for n in REST}), PACK_W, 0)], [],
                                        [(PACK_W, F32)] * 3)
    bg, bd, bnm, bnv = (dict(_unpack_rest(rest), w_in=a.reshape(1, D_MODEL, IN_SHARD))
                        for rest, a in ((g_rest, g_in), (d_rest, d_in), (nm_rest, nm_in), (nv_rest, nv_in)))

    outs = [loss, grad_x.reshape(x.shape)]
    for small, big in ((sg, bg), (sd, bd), (snm, bnm), (snv, bnv)):
        outs += [big[n] if n in BIG else small[n] for n in WEIGHTS]
    return tuple(outs)
```

```python
import functools
from typing import Callable, NamedTuple

import jax
import jax.numpy as jnp
from jax import lax
from jax.experimental import pallas as pl
from jax.experimental.pallas import tpu as pltpu

F32 = jnp.float32
MXU_DTYPE = jnp.bfloat16
WIRE_DTYPE = jnp.bfloat16
NORM_EPS = 1e-6
D_MODEL = 1024
HG_HEADS = 8
HG_DIM = 128
HG_CHUNK = 64
S5_WIDTH = 512
S5_GROUPS = 32
S5_GROUP = 16
S5_STATE = 64
S5_LANES = S5_GROUPS * S5_STATE
IN_COLS = 7168
SUBLANES = 8
VMEM_LIMIT = 56 * 1024 * 1024
HIGHEST = lax.Precision.HIGHEST
MESH = pl.DeviceIdType.MESH

ADAM_LR, ADAM_B1, ADAM_B2, ADAM_EPS, ADAM_WD, ADAM_STEP = 0.001, 0.9, 0.999, 1e-08, 0.01, 10

BIG = ("w_in", "w_o_hg", "w_glu", "w_o_s5", "w_out", "w_ple", "w_ple_gate")
BIG_SHAPE = {"w_in": (1024, 7168), "w_o_hg": (1024, 1024), "w_glu": (512, 1024), "w_o_s5": (512, 1024),
             "w_out": (1024, 1024), "w_ple": (256, 1024), "w_ple_gate": (1024, 1024)}
BIG_COL_SHARDED = ("w_in", "w_glu", "w_o_s5", "w_ple")
SMALL = ("norm_g", "hg_lb", "hg_norm_g", "s5_a_re", "s5_a_im", "s5_log_dt", "s5_b_re", "s5_b_im", "s5_c_re",
         "s5_c_im", "s5_d", "b_glu", "ple_norm_g", "final_norm_g")
SMALL_SHAPE = {"norm_g": (1, 1024), "hg_lb": (2, 1024), "hg_norm_g": (1, 1024), "s5_a_re": (1, 32, 64),
               "s5_a_im": (1, 32, 64), "s5_log_dt": (1, 32), "s5_b_re": (1, 32, 64, 16), "s5_b_im": (1, 32, 64, 16),
               "s5_c_re": (1, 32, 16, 64), "s5_c_im": (1, 32, 16, 64), "s5_d": (1, 32, 16), "b_glu": (1, 1024),
               "ple_norm_g": (1, 1024), "final_norm_g": (1024,)}
WEIGHTS = ("norm_g", "w_in", "hg_lb", "hg_norm_g", "w_o_hg", "s5_a_re", "s5_a_im", "s5_log_dt", "s5_b_re", "s5_b_im",
           "s5_c_re", "s5_c_im", "s5_d", "w_glu", "b_glu", "w_o_s5", "w_out", "ple_norm_g", "w_ple", "w_ple_gate",
           "final_norm_g")
N_CHIPS = 4
N_DEV = 8
PACK_W = 1024
SMALL_ROWS = 144


def _params(*sem):
    return pltpu.CompilerParams(dimension_semantics=sem, vmem_limit_bytes=VMEM_LIMIT)


def _sig(x):
    return 1.0 / (1.0 + jnp.exp(-x))


def _dsilu(z, s):
    return s * (1.0 + z * (1.0 - s))


def _mx(x):
    return x.astype(MXU_DTYPE)


def _dot(a, b, dims=(((1,), (0,)), ((), ()))):
    return lax.dot_general(_mx(a), _mx(b), dims, preferred_element_type=F32)


_NT = (((1,), (1,)), ((), ()))
_TN = (((0,), (0,)), ((), ()))


def _dot32(a, b):
    return jnp.dot(a, b, precision=HIGHEST, preferred_element_type=F32)


def _rms_bwd(dy, x, g):
    r = lax.rsqrt(jnp.mean(x * x, axis=-1, keepdims=True) + NORM_EPS)
    t = dy * g
    dx = r * t - x * (r * r * r) * jnp.mean(t * x, axis=-1, keepdims=True)
    return dx, jnp.sum(dy * x * r, axis=0, keepdims=True)


RING = 3


def _rowwise(name, fn, n_rows_total, tm, rows, consts, outs, accs=(), alias=None, ring=False):
    n_r, n_c, n_o, n_a = len(rows), len(consts), len(outs), len(accs)
    n_steps = n_rows_total // tm

    def body(*refs):
        row_refs = refs[:n_r]
        const_refs = refs[n_r:n_r + n_c]
        pos = n_r + n_c + (1 if alias is not None else 0)
        out_refs = refs[pos:pos + n_o]
        acc_refs = refs[pos + n_o:pos + n_o + n_a]
        if ring:
            slot_refs, sems = refs[pos + n_o + n_a:pos + n_o + n_a + n_r], refs[pos + n_o + n_a + n_r]
            i = pl.program_id(0)

            def fetch(step):
                first = step * tm if isinstance(step, int) else pl.multiple_of(step * tm, tm)
                return [pltpu.make_async_copy(src.at[pl.ds(first, tm), pl.ds(cb * w, w)], dst.at[step % RING],
                                              sems.at[k, step % RING])
                        for k, (src, dst, (_, w, cb)) in enumerate(zip(row_refs, slot_refs, rows))]

            @pl.when(i == 0)
            def _():
                for step in range(min(RING - 1, n_steps)):
                    for cp in fetch(step):
                        cp.start()

            @pl.when(i + (RING - 1) < n_steps)
            def _():
                for cp in fetch(i + (RING - 1)):
                    cp.start()

            for cp in fetch(i):
                cp.wait()
            tiles = [dst[i % RING] for dst in slot_refs]
        else:
            tiles = [r[...] for r in row_refs]
        res = fn(*tiles, *[r[...] for r in const_refs])
        for r, v in zip(out_refs, res[:n_o]):
            r[...] = v.astype(r.dtype)
        if n_a:
            @pl.when(pl.program_id(0) == 0)
            def _():
                for r in acc_refs:
                    r[...] = jnp.zeros_like(r)
            for r, v in zip(acc_refs, res[n_o:]):
                r[...] += v

    if ring:
        in_specs = [pl.BlockSpec(memory_space=pl.ANY)] * n_r
        scratch = [pltpu.VMEM((RING, tm, w), a.dtype) for (a, w, _) in rows] + [pltpu.SemaphoreType.DMA((n_r, RING))]
    else:
        in_specs = [pl.BlockSpec((tm, w), functools.partial(lambda i, cb: (i, cb), cb=cb)) for (_, w, cb) in rows]
        scratch = []
    in_specs += [pl.BlockSpec(c.shape, lambda i: (0, 0)) for c in consts]
    args = [a for (a, _, _) in rows] + list(consts)
    out_shape, out_specs = [], []
    for o in outs:
        w, dt = o[0], o[1]
        cb, total = (o[2], o[3]) if len(o) == 4 else (0, w)
        out_shape.append(jax.ShapeDtypeStruct((n_rows_total, total), dt))
        out_specs.append(pl.BlockSpec((tm, w), functools.partial(lambda i, cb: (i, cb), cb=cb)))
    io_alias = {}
    if alias is not None:
        in_specs.append(pl.BlockSpec(memory_space=pl.ANY))
        args.append(alias[0])
        io_alias = {len(args) - 1: alias[1]}
    for (r, w) in accs:
        out_shape.append(jax.ShapeDtypeStruct((r, w), F32))
        out_specs.append(pl.BlockSpec((r, w), lambda i: (0, 0)))
    res = pl.pallas_call(body, name=name, grid=(n_steps,), in_specs=in_specs, out_specs=out_specs,
                         out_shape=out_shape, scratch_shapes=scratch, input_output_aliases=io_alias,
                         compiler_params=_params("arbitrary"))(*args)
    return res


class _Riding(NamedTuple):
    ins: tuple
    outs: tuple
    n_sems: int
    start: Callable
    wait: Callable


_HBM = pl.BlockSpec(memory_space=pl.ANY)


def _ride(riding, refs, n_in, n_out, n_scratch, first, last):
    if riding is None:
        return refs[:n_in], refs[n_in:n_in + n_out], refs[n_in + n_out:]
    r_in, r_out = len(riding.ins), len(riding.outs)
    ins, rins = refs[:n_in], refs[n_in:n_in + r_in]
    pos = n_in + r_in
    outs, routs = refs[pos:pos + n_out], refs[pos + n_out:pos + n_out + r_out]
    pos += n_out + r_out
    scratch, (send_sems, recv_sems) = refs[pos:pos + n_scratch], refs[pos + n_scratch:]

    @pl.when(first)
    def _():
        riding.start(rins, routs, send_sems, recv_sems)

    @pl.when(last)
    def _():
        riding.wait(rins, routs, send_sems, recv_sems)

    return ins, outs, scratch


def _riding_call(riding, body, name, grid, in_specs, args, out_specs, out_shape, scratch, io_alias=None):
    if riding is not None:
        in_specs = list(in_specs) + [_HBM] * len(riding.ins)
        args = list(args) + list(riding.ins)
        out_specs = list(out_specs) + [_HBM] * len(riding.outs)
        out_shape = list(out_shape) + list(riding.outs)
        scratch = list(scratch) + [pltpu.SemaphoreType.DMA((riding.n_sems,))] * 2
    return pl.pallas_call(body, name=name, grid=grid, in_specs=in_specs, out_specs=out_specs, out_shape=out_shape,
                          scratch_shapes=scratch, input_output_aliases=io_alias or {},
                          compiler_params=_params(*(["arbitrary"] * len(grid))))(*args)


def _mm_nn(name, a, b, tm, tn, riding=None, prologue=None, consts=()):
    m, k = a.shape
    n = b.shape[1] if b.ndim == 2 else b.shape[0] * b.shape[2]
    grid = (n // tn, m // tm)
    n_out, scratch = (1, []) if prologue is None else (2, [pltpu.VMEM((m, k), MXU_DTYPE)])

    def body(*refs):
        j, i = pl.program_id(0), pl.program_id(1)
        ins, outs, kept = _ride(riding, refs, 2 + len(consts), n_out, len(scratch), (j == 0) & (i == 0),
                                (j == grid[0] - 1) & (i == grid[1] - 1))
        if prologue is None:
            left = ins[0][...]
        else:
            rows = pl.ds(pl.multiple_of(i * tm, tm), tm)

            @pl.when(j == 0)
            def _():
                tile = _mx(prologue(ins[0][...], *[c[...] for c in ins[2:]]))
                kept[0][rows, :] = tile
                outs[1][...] = tile

            left = kept[0][rows, :]
        outs[0][...] = _dot(left, ins[1][...])

    once = (lambda j, i: (i, 0)) if prologue is None else (lambda j, i: (jnp.where(j == 0, i, grid[1] - 1), 0))
    b_spec = (pl.BlockSpec((k, tn), lambda j, i: (0, j)) if b.ndim == 2
              else pl.BlockSpec((None, k, tn), lambda j, i: (j, 0, 0)))
    in_specs = [pl.BlockSpec((tm, k), once), b_spec]
    in_specs += [pl.BlockSpec(c.shape, lambda j, i: (0, 0)) for c in consts]
    out_specs = [pl.BlockSpec((tm, tn), lambda j, i: (i, j))]
    out_shape = [jax.ShapeDtypeStruct((m, n), F32)]
    if prologue is not None:
        out_specs.append(pl.BlockSpec((tm, k), once))
        out_shape.append(jax.ShapeDtypeStruct((m, k), MXU_DTYPE))
    res = _riding_call(riding, body, name, grid, in_specs, [a, b] + list(consts), out_specs, out_shape, scratch)
    return res[0] if riding is None and prologue is None else res


def _mm_nt_then(name, a, b, tm, tn, fn, rows, consts, outs, accs=(), alias=None, riding=None):
    m, n = a.shape
    k = b.shape[-2]
    steps = n // tn
    n_r, n_c, n_o, n_a = len(rows), len(consts), len(outs), len(accs)

    def body(*refs):
        a_ref, b_ref = refs[:2]
        row_refs = refs[2:2 + n_r]
        const_refs = refs[2 + n_r:2 + n_r + n_c]
        i, s = pl.program_id(0), pl.program_id(1)
        n_in = 2 + n_r + n_c + (1 if alias is not None else 0)
        _, outs_, (mm_ref,) = _ride(riding, refs, n_in, n_o + n_a, 1, (i == 0) & (s == 0),
                                    (i == m // tm - 1) & (s == steps - 1))
        out_refs, acc_refs = outs_[:n_o], outs_[n_o:]
        part = _dot(a_ref[...], b_ref[...] if b.ndim == 2 else b_ref[s], _NT)
        if steps > 1:
            @pl.when(s == 0)
            def _():
                mm_ref[...] = jnp.zeros_like(mm_ref)
            mm_ref[...] += part

        @pl.when(s == steps - 1)
        def _():
            res = fn(mm_ref[...] if steps > 1 else part, *[r[...] for r in row_refs], *[r[...] for r in const_refs])
            for r, v in zip(out_refs, res[:n_o]):
                r[...] = v.astype(r.dtype)
            if n_a:
                @pl.when(i == 0)
                def _():
                    for r in acc_refs:
                        r[...] = jnp.zeros_like(r)
                for r, v in zip(acc_refs, res[n_o:]):
                    r[...] += v

    b_spec = (pl.BlockSpec((k, tn), lambda i, s: (0, s)) if b.ndim == 2
              else pl.BlockSpec(memory_space=pltpu.VMEM))
    in_specs = [pl.BlockSpec((tm, tn), lambda i, s: (i, s)), b_spec]
    in_specs += [pl.BlockSpec((tm, w), functools.partial(lambda i, s, cb: (i, cb), cb=cb)) for (_, w, cb) in rows]
    in_specs += [pl.BlockSpec(c.shape, lambda i, s: (0, 0)) for c in consts]
    args = [a, b] + [r[0] for r in rows] + list(consts)
    out_shape, out_specs = [], []
    for o in outs:
        w, dt = o[0], o[1]
        cb, total = (o[2], o[3]) if len(o) == 4 else (0, w)
        out_shape.append(jax.ShapeDtypeStruct((m, total), dt))
        out_specs.append(pl.BlockSpec((tm, w), functools.partial(lambda i, s, cb: (i, cb), cb=cb)))
    io_alias = {}
    if alias is not None:
        in_specs.append(pl.BlockSpec(memory_space=pl.ANY))
        args.append(alias[0])
        io_alias = {len(args) - 1: alias[1]}
    for (r, w) in accs:
        out_shape.append(jax.ShapeDtypeStruct((r, w), F32))
        out_specs.append(pl.BlockSpec((r, w), lambda i, s: (0, 0)))
    return _riding_call(riding, body, name, (m // tm, steps), in_specs, args, out_specs, out_shape,
                        [pltpu.VMEM((tm, k), F32)], io_alias)


def _mm_tn(name, a, b, tk, tn, col_shards=False, riding=None):
    t, k = a.shape
    n = b.shape[1]
    steps = t // tk

    def body(*refs):
        j, s = pl.program_id(0), pl.program_id(1)
        (a_ref, b_ref), (o_ref,), (acc_ref,) = _ride(riding, refs, 2, 1, 1, (j == 0) & (s == 0),
                                                     (j == n // tn - 1) & (s == steps - 1))

        @pl.when(s == 0)
        def _():
            acc_ref[...] = jnp.zeros_like(acc_ref)

        acc_ref[...] += _dot(a_ref[...], b_ref[...], _TN)

        @pl.when(s == steps - 1)
        def _():
            o_ref[...] = acc_ref[...]

    if col_shards:
        out_spec = pl.BlockSpec((None, k, tn), lambda j, s: (j, 0, 0))
        out_shape = jax.ShapeDtypeStruct((n // tn, k, tn), F32)
    else:
        out_spec = pl.BlockSpec((k, tn), lambda j, s: (0, j))
        out_shape = jax.ShapeDtypeStruct((k, n), F32)
    res = _riding_call(riding, body, name, (n // tn, steps),
                       [pl.BlockSpec((tk, k), lambda j, s: (s, 0)), pl.BlockSpec((tk, tn), lambda j, s: (s, j))],
                       [a, b], [out_spec], [out_shape], [pltpu.VMEM((k, tn), F32)])
    return res[0] if riding is None else res


def _dot01(m01, x):
    m = m01.astype(MXU_DTYPE)
    hi = x.astype(MXU_DTYPE)
    r1 = x - hi.astype(F32)
    mid = r1.astype(MXU_DTYPE)
    lo = (r1 - mid.astype(F32)).astype(MXU_DTYPE)
    dot = lambda v: jnp.dot(m, v, preferred_element_type=F32)
    return dot(hi) + dot(mid) + dot(lo)


def _chunk_rows(x, offset, nck):
    return jnp.concatenate([jnp.broadcast_to(x[c * HG_CHUNK + offset:c * HG_CHUNK + offset + 1, :],
                                             (HG_CHUNK, x.shape[1])) for c in range(nck)], axis=0)


def _hg_block_terms(q, f, lb, tb):
    nck = tb // HG_CHUNK
    sig = _sig(f)
    fv = lb + (1.0 - lb) * sig
    kk = (1.0 - lb) * (1.0 - sig)
    row = lax.broadcasted_iota(jnp.int32, (tb, tb), 0)
    col = lax.broadcasted_iota(jnp.int32, (tb, tb), 1)
    same = jnp.right_shift(row, 6) == jnp.right_shift(col, 6)
    causal, anti = same & (row >= col), same & (row <= col)
    b = _dot01(causal, jnp.log(fv))
    b_mid, b_last = _chunk_rows(b, HG_CHUNK // 2 - 1, nck), _chunk_rows(b, HG_CHUNK - 1, nck)
    e_mid, e_mid_inv = jnp.exp(b - b_mid), jnp.exp(b_mid - b)
    e_b, e_last = jnp.exp(b), jnp.exp(b_last - b)
    dcs = [jnp.exp(b[c * HG_CHUNK + HG_CHUNK - 1:(c + 1) * HG_CHUNK, :]) for c in range(nck)]
    return sig, fv, kk, causal, anti, e_mid, e_mid_inv, e_b, e_last, dcs


def _hgrn2_fwd(proj, hg_lb, hg_norm_g, t_len, tb, riding=None):
    nck = tb // HG_CHUNK
    nb = t_len // tb

    def body(*refs):
        step = pl.program_id(0)
        ((p_ref, lb_ref, gn_ref), (o_ref, act_ref, sp_ref),
         (st_ref, a_s, bm_s, qd_s, kd_s, v_s, sc_s, inc_s)) = _ride(riding, refs, 3, 3, 8, step == 0, step == nb - 1)

        @pl.when(pl.program_id(0) == 0)
        def _():
            st_ref[...] = jnp.zeros_like(st_ref)

        lb = _sig(lb_ref[0:1, :] - lb_ref[1:2, :])
        q = p_ref[:, pl.ds(0, 1024)]
        _, _, kk, causal, _, e_mid, e_mid_inv, e_b, e_last, dcs = _hg_block_terms(q, p_ref[:, pl.ds(1024, 1024)],
                                                                                   lb, tb)
        a_s[...] = _mx(q * e_mid)
        bm_s[...] = _mx(kk * e_mid_inv)
        qd_s[...] = _mx(q * e_b)
        kd_s[...] = _mx(kk * e_last)
        v_s[...] = _mx(p_ref[:, pl.ds(2048, 1024)])
        heads = [pl.ds(h * HG_DIM, HG_DIM) for h in range(HG_HEADS)]
        chunks = [pl.ds(c * HG_CHUNK, HG_CHUNK) for c in range(nck)]
        for h, hs in enumerate(heads):
            sc_s[h] = _mx(jnp.where(causal, _dot(a_s[:, hs], bm_s[:, hs], _NT), 0.0))
        for h, hs in enumerate(heads):
            o_ref[:, hs] = _dot(sc_s[h], v_s[:, hs])
        for h, hs in enumerate(heads):
            for c, r in enumerate(chunks):
                inc_s[h, c] = _dot(v_s[r, hs], kd_s[r, hs], _TN)
        for c in range(nck):
            for h in range(HG_HEADS):
                st = st_ref[h]
                sp_ref[h, c] = st
                st_ref[h] = dcs[c][:, h * HG_DIM:(h + 1) * HG_DIM] * st + inc_s[h, c]
        for c, r in enumerate(chunks):
            for h, hs in enumerate(heads):
                o_ref[r, hs] += _dot(qd_s[r, hs], sp_ref[h, c], _NT)
        for h, hs in enumerate(heads):
            o = o_ref[:, hs]
            rr = lax.rsqrt(jnp.mean(o * o, axis=-1, keepdims=True) + NORM_EPS)
            g = p_ref[:, pl.ds(3072 + h * HG_DIM, HG_DIM)]
            act_ref[:, hs] = (o * rr * gn_ref[:, hs] * (g * _sig(g))).astype(act_ref.dtype)

    return _riding_call(
        riding, body, "hgrn2_fwd", (nb,),
        [pl.BlockSpec((tb, 4096), lambda i: (i, 0)), pl.BlockSpec((2, 1024), lambda i: (0, 0)),
         pl.BlockSpec((1, 1024), lambda i: (0, 0))],
        [proj, hg_lb, hg_norm_g],
        [pl.BlockSpec((tb, 1024), lambda i: (i, 0)), pl.BlockSpec((tb, 1024), lambda i: (i, 0)),
         pl.BlockSpec((HG_HEADS, nck, HG_DIM, HG_DIM), lambda i: (0, i, 0, 0))],
        [jax.ShapeDtypeStruct((t_len, 1024), F32), jax.ShapeDtypeStruct((t_len, 1024), MXU_DTYPE),
         jax.ShapeDtypeStruct((HG_HEADS, t_len // HG_CHUNK, HG_DIM, HG_DIM), F32)],
        [pltpu.VMEM((HG_HEADS, HG_DIM, HG_DIM), F32)] + [pltpu.VMEM((tb, 1024), MXU_DTYPE)] * 5
        + [pltpu.VMEM((HG_HEADS, tb, tb), MXU_DTYPE), pltpu.VMEM((HG_HEADS, nck, HG_DIM, HG_DIM), F32)])


def _hgrn2_bwd(proj, d_o, s_prev, hg_lb, dproj, t_len, tb, riding=None):
    nck = tb // HG_CHUNK
    nb = t_len // tb

    def body(*refs):
        step = pl.program_id(0)
        ((p_ref, do_ref, sp_ref, lb_ref, _), (dp_ref, dlb_ref),
         (ds_ref, acc_ref, a_s, bm_s, qd_s, kd_s, v_s, do_s, da_s, dbm_s, dqd_s, dkd_s, dv_s, ex_s, sc_s, dsc_s,
          up_s)) = _ride(riding, refs, 5, 2, 17, step == 0, step == nb - 1)

        @pl.when(pl.program_id(0) == 0)
        def _():
            ds_ref[...] = jnp.zeros_like(ds_ref)
            acc_ref[...] = jnp.zeros_like(acc_ref)

        lb = _sig(lb_ref[0:1, :] - lb_ref[1:2, :])
        q = p_ref[:, pl.ds(0, 1024)]
        sig, fv, kk, causal, anti, e_mid, e_mid_inv, e_b, e_last, dcs = _hg_block_terms(
            q, p_ref[:, pl.ds(1024, 1024)], lb, tb)
        a, bm, qd, kd = q * e_mid, kk * e_mid_inv, q * e_b, kk * e_last
        a_s[...] = _mx(a)
        bm_s[...] = _mx(bm)
        qd_s[...] = _mx(qd)
        kd_s[...] = _mx(kd)
        v_s[...] = _mx(p_ref[:, pl.ds(2048, 1024)])
        do_s[...] = _mx(do_ref[...])
        heads = [pl.ds(h * HG_DIM, HG_DIM) for h in range(HG_HEADS)]
        chunks = [pl.ds(c * HG_CHUNK, HG_CHUNK) for c in range(nck)]
        for h, hs in enumerate(heads):
            sc_s[h] = _mx(jnp.where(causal, _dot(a_s[:, hs], bm_s[:, hs], _NT), 0.0))
            dsc_s[h] = _mx(jnp.where(causal, _dot(do_s[:, hs], v_s[:, hs], _NT), 0.0))
        for h, hs in enumerate(heads):
            dv_s[:, hs] = _dot(sc_s[h], do_s[:, hs], _TN)
            da_s[:, hs] = _dot(dsc_s[h], bm_s[:, hs])
            dbm_s[:, hs] = _dot(dsc_s[h], a_s[:, hs], _TN)
        for h, hs in enumerate(heads):
            for c, r in enumerate(chunks):
                up_s[h, c] = _dot(do_s[r, hs], qd_s[r, hs], _TN)
                dqd_s[r, hs] = _dot(do_s[r, hs], sp_ref[h, c])
        for c in reversed(range(nck)):
            r = chunks[c]
            for h, hs in enumerate(heads):
                dst = ds_ref[h]
                dc = dcs[c][:, h * HG_DIM:(h + 1) * HG_DIM]
                dv_s[r, hs] += _dot(kd_s[r, hs], dst, _NT)
                dkd_s[r, hs] = _dot(v_s[r, hs], dst)
                ex_s[c:c + 1, hs] = jnp.sum(dst * sp_ref[h, c], axis=0, keepdims=True) * dc
                ds_ref[h] = up_s[h, c] + dc * dst
        da, dbm, dqd, dkd = da_s[...], dbm_s[...], dqd_s[...], dkd_s[...]
        dq = da * e_mid + dqd * e_b
        dk = dbm * e_mid_inv + dkd * e_last
        db = da * a - dbm * bm + dqd * qd - dkd * kd
        dkk = dkd * kd
        extra = jnp.concatenate(
            [jnp.broadcast_to(jnp.sum(dkk[c * HG_CHUNK:(c + 1) * HG_CHUNK], axis=0, keepdims=True)
                              + ex_s[c:c + 1, :], (HG_CHUNK, 1024)) for c in range(nck)], axis=0)
        dlogf = _dot01(anti, db) + extra
        dfv_k = dlogf / fv - dk
        dp_ref[:, pl.ds(0, 1024)] = dq.astype(dp_ref.dtype)
        dp_ref[:, pl.ds(1024, 1024)] = (dfv_k * (1.0 - lb) * sig * (1.0 - sig)).astype(dp_ref.dtype)
        dp_ref[:, pl.ds(2048, 1024)] = dv_s[...].astype(dp_ref.dtype)
        acc_ref[...] += jnp.sum(dfv_k * (1.0 - sig), axis=0, keepdims=True)

        @pl.when(pl.program_id(0) == nb - 1)
        def _():
            g0 = acc_ref[...] * lb * (1.0 - lb)
            dlb_ref[0:1, :] = g0
            dlb_ref[1:2, :] = -g0

    return _riding_call(
        riding, body, "hgrn2_bwd", (nb,),
        [pl.BlockSpec((tb, 3072), lambda i: (nb - 1 - i, 0)),
         pl.BlockSpec((tb, 1024), lambda i: (nb - 1 - i, 0)),
         pl.BlockSpec((HG_HEADS, nck, HG_DIM, HG_DIM), lambda i: (0, nb - 1 - i, 0, 0)),
         pl.BlockSpec((2, 1024), lambda i: (0, 0)),
         pl.BlockSpec(memory_space=pl.ANY)],
        [proj, d_o, s_prev, hg_lb, dproj],
        [pl.BlockSpec((tb, 3072), lambda i: (nb - 1 - i, 0)), pl.BlockSpec((2, 1024), lambda i: (0, 0))],
        [jax.ShapeDtypeStruct((t_len, IN_COLS), dproj.dtype), jax.ShapeDtypeStruct((2, 1024), F32)],
        [pltpu.VMEM((HG_HEADS, HG_DIM, HG_DIM), F32), pltpu.VMEM((1, 1024), F32)]
        + [pltpu.VMEM((tb, 1024), MXU_DTYPE)] * 6 + [pltpu.VMEM((tb, 1024), F32)] * 5
        + [pltpu.VMEM((SUBLANES, 1024), F32)] + [pltpu.VMEM((HG_HEADS, tb, tb), MXU_DTYPE)] * 2
        + [pltpu.VMEM((HG_HEADS, nck, HG_DIM, HG_DIM), F32)], {4: 0})


def _s5_prep_bwd(a_re, a_im, log_dt, b_re_t, b_im_t, dlam, dbbr, dbbi):
    def body(ar_ref, ai_ref, ldt_ref, br_ref, bi_ref, dlam_ref, dbbr_ref, dbbi_ref,
             dar_ref, dai_ref, dldt_ref, dbr_ref, dbi_ref):
        ar, ai = ar_ref[...], ai_ref[...]
        dt = jnp.exp(ldt_ref[...])
        mag = jnp.exp(ar * dt)
        cs, sn = jnp.cos(ai * dt), jnp.sin(ai * dt)
        lr, li = mag * cs, mag * sn
        den = ar * ar + ai * ai
        nr = lr - 1.0
        sr = (nr * ar + li * ai) / den
        si = (li * ar - nr * ai) / den
        br, bi = br_ref[...], bi_ref[...]
        gbr, gbi = dbbr_ref[...], dbbi_ref[...]
        dbr_ref[...] = sr * gbr + si * gbi
        dbi_ref[...] = sr * gbi - si * gbr
        dsr = jnp.sum(gbr * br + gbi * bi, axis=0, keepdims=True)
        dsi = jnp.sum(gbi * br - gbr * bi, axis=0, keepdims=True)
        dnr = (dsr * ar - dsi * ai) / den
        dli = dlam_ref[1:2, :] + (dsr * ai + dsi * ar) / den
        dlr = dlam_ref[0:1, :] + dnr
        dden = -(dsr * sr + dsi * si) / den
        dar = (dsr * nr + dsi * li) / den + dden * 2.0 * ar
        dai = (dsr * li - dsi * nr) / den + dden * 2.0 * ai
        dmag = dlr * cs + dli * sn
        dth = mag * (dli * cs - dlr * sn)
        dar_ref[...] = dar + dmag * mag * dt
        dai_ref[...] = dai + dth * dt
        ddt = (dmag * mag * ar + dth * ai) * dt
        lane = lax.broadcasted_iota(jnp.int32, (S5_LANES, 128), 0) // S5_STATE
        grp = lax.broadcasted_iota(jnp.int32, (S5_LANES, 128), 1)
        dldt_ref[...] = _dot32(jnp.broadcast_to(ddt, (SUBLANES, S5_LANES)), (lane == grp).astype(F32))

    whole = pl.BlockSpec(memory_space=pltpu.VMEM)
    return pl.pallas_call(
        body, name="s5_prep_bwd", in_specs=[whole] * 8, out_specs=[whole] * 5,
        out_shape=[jax.ShapeDtypeStruct((1, S5_LANES), F32), jax.ShapeDtypeStruct((1, S5_LANES), F32),
                   jax.ShapeDtypeStruct((SUBLANES, 128), F32), jax.ShapeDtypeStruct((S5_GROUP, S5_LANES), F32),
                   jax.ShapeDtypeStruct((S5_GROUP, S5_LANES), F32)])(a_re, a_im, log_dt, b_re_t, b_im_t, dlam, dbbr,
                                                                      dbbi)


def _dgelu(x):
    c, a = 0.7978845608028654, 0.044715
    th = jnp.tanh(c * (x + a * x * x * x))
    return 0.5 * (1.0 + th) + 0.5 * x * (1.0 - th * th) * c * (1.0 + 3.0 * a * x * x)


S5_BLOCKS = 4
S5_BW = S5_WIDTH // S5_BLOCKS
S5_BL = S5_LANES // S5_BLOCKS
S5_LANE_BLOCKS = S5_LANES // 128
S5_SCAN_BLOCKS = 4


def _s5_prep(a_re, a_im, log_dt, b_re_t, b_im_t, seg):
    def body(ar_ref, ai_ref, ldt_ref, br_ref, bi_ref,
             rows_f, pfr_ref, pfi_ref, rows_r, prr_ref, pri_ref, bbr_ref, bbi_ref):
        ar, ai = ar_ref[...], ai_ref[...]
        dt = jnp.exp(ldt_ref[...])
        mag = jnp.exp(ar * dt)
        lr, li = mag * jnp.cos(ai * dt), mag * jnp.sin(ai * dt)
        den = ar * ar + ai * ai
        nr = lr - 1.0
        sr = (nr * ar + li * ai) / den
        si = (li * ar - nr * ai) / den
        wide = (SUBLANES, S5_LANES)
        cr, ci = lr, li
        for i in range(seg):
            pfr_ref[i] = jnp.broadcast_to(cr, wide)
            pfi_ref[i] = jnp.broadcast_to(ci, wide)
            prr_ref[seg - 1 - i] = jnp.broadcast_to(cr, wide)
            pri_ref[seg - 1 - i] = jnp.broadcast_to(-ci, wide)
            if i == seg - 1:
                for rows, sign in ((rows_f, 1.0), (rows_r, -1.0)):
                    rows[0:1, :] = lr
                    rows[1:2, :] = sign * li
                    rows[2:3, :] = cr
                    rows[3:4, :] = sign * ci
            cr, ci = cr * lr - ci * li, cr * li + ci * lr
        br, bi = br_ref[...], bi_ref[...]
        bbr_ref[...] = sr * br - si * bi
        bbi_ref[...] = sr * bi + si * br

    whole = pl.BlockSpec(memory_space=pltpu.VMEM)
    tables = [jax.ShapeDtypeStruct((4, S5_LANES), F32)] + [jax.ShapeDtypeStruct((seg, SUBLANES, S5_LANES), F32)] * 2
    bbar = [jax.ShapeDtypeStruct((S5_GROUP, S5_LANES), F32)] * 2
    res = pl.pallas_call(body, name="s5_prep", in_specs=[whole] * 5, out_specs=[whole] * 8,
                         out_shape=tables + tables + bbar)(a_re, a_im, log_dt, b_re_t, b_im_t)
    return res[0:3], res[3:6], res[6], res[7]


def _lanes(j):
    return pl.ds(j * 128, 128)


def _to_segment_order(v, stage_ref, out_ref, seg):
    nbl = v.shape[1] // 128
    for b in range(nbl):
        stage_ref[b] = v[:, b * 128:(b + 1) * 128]

    def body(t, carry):
        rows = pl.ds(pl.multiple_of(t * SUBLANES, SUBLANES), SUBLANES)
        for b in range(nbl):
            out_ref[rows, _lanes(b)] = stage_ref[b, pl.ds(t, SUBLANES, stride=seg), :]
        return carry

    lax.fori_loop(0, seg, body, 0, unroll=True)


def _from_segment_order(v, stage_ref, out_ref, seg):
    nbl = v.shape[1] // 128
    for b in range(nbl):
        stage_ref[b] = v[:, b * 128:(b + 1) * 128]
    for s in range(SUBLANES):
        def body(k, carry, s=s):
            rows = pl.ds(pl.multiple_of(s * seg + k * SUBLANES, SUBLANES), SUBLANES)
            for b in range(nbl):
                out_ref[rows, _lanes(b)] = stage_ref[b, pl.ds(k * SUBLANES * SUBLANES + s, SUBLANES,
                                                              stride=SUBLANES), :]
            return carry

        lax.fori_loop(0, seg // SUBLANES, body, 0, unroll=True)


def _tile_scan(xr_ref, xi_ref, lam_ref, car_ref, cai_ref, cn_r, cn_i, blocks, seg, reverse):
    shape = (SUBLANES, 128)
    lrs = [jnp.broadcast_to(lam_ref[0:1, _lanes(j)], shape) for j in blocks]
    lis = [jnp.broadcast_to(lam_ref[1:2, _lanes(j)], shape) for j in blocks]

    def step(k, carry):
        t = seg - 1 - k if reverse else k
        rows = pl.ds(pl.multiple_of(t * SUBLANES, SUBLANES), SUBLANES)
        out = []
        for n, j in enumerate(blocks):
            cr, ci = carry[2 * n], carry[2 * n + 1]
            nr = lrs[n] * cr - lis[n] * ci + xr_ref[rows, _lanes(j)]
            ni = lrs[n] * ci + lis[n] * cr + xi_ref[rows, _lanes(j)]
            xr_ref[rows, _lanes(j)] = nr
            xi_ref[rows, _lanes(j)] = ni
            out += [nr, ni]
        return tuple(out)

    zero = jnp.zeros(shape, F32)
    fin = lax.fori_loop(0, seg, step, (zero,) * (2 * len(blocks)), unroll=True)
    for n, j in enumerate(blocks):
        ls = _lanes(j)
        fr, fi = fin[2 * n], fin[2 * n + 1]
        sr, si = lam_ref[2:3, ls], lam_ref[3:4, ls]
        pr, pi = car_ref[:, ls], cai_ref[:, ls]
        for s in (reversed(range(SUBLANES)) if reverse else range(SUBLANES)):
            cn_r[s:s + 1, ls] = pr
            cn_i[s:s + 1, ls] = pi
            pr, pi = fr[s:s + 1, :] + sr * pr - si * pi, fi[s:s + 1, :] + sr * pi + si * pr
        car_ref[:, ls] = pr
        cai_ref[:, ls] = pi


def _s5_fwd(proj, lam_rows, p3_re, p3_im, bbr4, bbi4, crt4, cit4, d_row, t_len, tb):
    seg = tb // SUBLANES

    def body(u_ref, lam_ref, p3r_ref, p3i_ref, bbr_ref, bbi_ref, crt_ref, cit_ref, d_ref,
             hr_ref, hi_ref, ypre_ref, ys_ref, car_ref, cai_ref, cn_r, cn_i, stage_ref, us_ref, yseg_ref):
        @pl.when(pl.program_id(0) == 0)
        def _():
            car_ref[...] = jnp.zeros_like(car_ref)
            cai_ref[...] = jnp.zeros_like(cai_ref)

        _to_segment_order(u_ref[...], stage_ref, us_ref, seg)
        u = us_ref[...]
        for i in range(S5_BLOCKS):
            ui = u[:, i * S5_BW:(i + 1) * S5_BW]
            hr_ref[:, pl.ds(i * S5_BL, S5_BL)] = _dot(ui, bbr_ref[i])
            hi_ref[:, pl.ds(i * S5_BL, S5_BL)] = _dot(ui, bbi_ref[i])
        for lc in range(S5_LANE_BLOCKS // S5_SCAN_BLOCKS):
            blocks = range(lc * S5_SCAN_BLOCKS, (lc + 1) * S5_SCAN_BLOCKS)
            _tile_scan(hr_ref, hi_ref, lam_ref, car_ref, cai_ref, cn_r, cn_i, blocks, seg, False)
            crs = [cn_r[:, _lanes(j)] for j in blocks]
            cis = [cn_i[:, _lanes(j)] for j in blocks]

            def fix(t, carry, blocks=blocks, crs=crs, cis=cis):
                rows = pl.ds(pl.multiple_of(t * SUBLANES, SUBLANES), SUBLANES)
                for n, j in enumerate(blocks):
                    pr, pi = p3r_ref[t, :, _lanes(j)], p3i_ref[t, :, _lanes(j)]
                    hr_ref[rows, _lanes(j)] += pr * crs[n] - pi * cis[n]
                    hi_ref[rows, _lanes(j)] += pr * cis[n] + pi * crs[n]
                return carry

            lax.fori_loop(0, seg, fix, 0, unroll=True)
        for i in range(S5_BLOCKS):
            ws = pl.ds(i * S5_BW, S5_BW)
            bl = pl.ds(i * S5_BL, S5_BL)
            yseg_ref[:, ws] = (_dot(hr_ref[:, bl], crt_ref[i]) - _dot(hi_ref[:, bl], cit_ref[i])
                               + d_ref[:, ws] * u[:, i * S5_BW:(i + 1) * S5_BW])
        _from_segment_order(yseg_ref[...], stage_ref, ypre_ref, seg)
        ys_ref[...] = jax.nn.gelu(ypre_ref[...], approximate=True).astype(ys_ref.dtype)

    whole = pl.BlockSpec(memory_space=pltpu.VMEM)
    return pl.pallas_call(
        body, name="s5_fwd", grid=(t_len // tb,),
        in_specs=[pl.BlockSpec((tb, S5_WIDTH), lambda i: (i, 4096 // S5_WIDTH))] + [whole] * 8,
        out_specs=[pl.BlockSpec((tb, S5_LANES), lambda i: (i, 0)), pl.BlockSpec((tb, S5_LANES), lambda i: (i, 0)),
                   pl.BlockSpec((tb, S5_WIDTH), lambda i: (i, 0)), pl.BlockSpec((tb, S5_WIDTH), lambda i: (i, 0))],
        out_shape=[jax.ShapeDtypeStruct((t_len, S5_LANES), F32), jax.ShapeDtypeStruct((t_len, S5_LANES), F32),
                   jax.ShapeDtypeStruct((t_len, S5_WIDTH), F32), jax.ShapeDtypeStruct((t_len, S5_WIDTH), MXU_DTYPE)],
        scratch_shapes=[pltpu.VMEM((1, S5_LANES), F32), pltpu.VMEM((1, S5_LANES), F32),
                        pltpu.VMEM((SUBLANES, S5_LANES), F32), pltpu.VMEM((SUBLANES, S5_LANES), F32),
                        pltpu.VMEM((S5_WIDTH // 128, tb, 128), F32), pltpu.VMEM((tb, S5_WIDTH), F32),
                        pltpu.VMEM((tb, S5_WIDTH), F32)],
        compiler_params=_params("arbitrary"))(proj, lam_rows, p3_re, p3_im, bbr4, bbi4, crt4, cit4, d_row)


def _s5_bwd(dgelu, y_pre, proj, h_re, h_im, lam_rows, p3_re, p3_im, bbr4, bbi4, cr4, ci4, d_row, dproj, t_len, tb):
    seg = tb // SUBLANES
    nb = t_len // tb

    def body(dg_ref, yp_ref, u_ref, hr_ref, hi_ref, lam_ref, p3r_ref, p3i_ref, bbr_ref, bbi_ref, cr_ref, ci_ref,
             d_ref, _, du_ref, dbbr_ref, dbbi_ref, dcr_ref, dci_ref, dd_ref, dlam_ref,
             gr_ref, gi_ref, car_ref, cai_ref, cn_r, cn_i, stage_ref, us_ref, dys_ref, duseg_ref):
        @pl.when(pl.program_id(0) == 0)
        def _():
            for ref in (car_ref, cai_ref, dbbr_ref, dbbi_ref, dcr_ref, dci_ref, dd_ref, dlam_ref):
                ref[...] = jnp.zeros_like(ref)

        _to_segment_order(u_ref[...], stage_ref, us_ref, seg)
        _to_segment_order(dg_ref[...] * _dgelu(yp_ref[...]), stage_ref, dys_ref, seg)
        u, dy = us_ref[...], dys_ref[...]
        for i in range(S5_BLOCKS):
            dyi = dy[:, i * S5_BW:(i + 1) * S5_BW]
            gr_ref[:, pl.ds(i * S5_BL, S5_BL)] = _dot(dyi, cr_ref[i])
            gi_ref[:, pl.ds(i * S5_BL, S5_BL)] = -_dot(dyi, ci_ref[i])
        for lc in range(S5_LANE_BLOCKS // S5_SCAN_BLOCKS):
            blocks = range(lc * S5_SCAN_BLOCKS, (lc + 1) * S5_SCAN_BLOCKS)
            _tile_scan(gr_ref, gi_ref, lam_ref, car_ref, cai_ref, cn_r, cn_i, blocks, seg, True)
            crs = [cn_r[:, _lanes(j)] for j in blocks]
            cis = [cn_i[:, _lanes(j)] for j in blocks]

            def fix(k, carry, blocks=blocks, crs=crs, cis=cis):
                t = seg - 1 - k
                rows = pl.ds(pl.multiple_of(t * SUBLANES, SUBLANES), SUBLANES)
                out = []
                for n, j in enumerate(blocks):
                    nr, ni, slr, sli = carry[4 * n:4 * n + 4]
                    pr, pi = p3r_ref[t, :, _lanes(j)], p3i_ref[t, :, _lanes(j)]
                    g_r = gr_ref[rows, _lanes(j)] + pr * crs[n] - pi * cis[n]
                    g_i = gi_ref[rows, _lanes(j)] + pr * cis[n] + pi * crs[n]
                    gr_ref[rows, _lanes(j)] = g_r
                    gi_ref[rows, _lanes(j)] = g_i
                    hr, hi = hr_ref[rows, _lanes(j)], hi_ref[rows, _lanes(j)]
                    out += [g_r, g_i, slr + nr * hr + ni * hi, sli + ni * hr - nr * hi]
                return tuple(out)

            zero = jnp.zeros((SUBLANES, 128), F32)
            init = []
            for n in range(len(blocks)):
                init += [crs[n], cis[n], zero, zero]
            fin = lax.fori_loop(0, seg, fix, tuple(init), unroll=True)
            for n, j in enumerate(blocks):
                dlam_ref[0:1, _lanes(j)] += jnp.sum(fin[4 * n + 2], axis=0, keepdims=True)
                dlam_ref[1:2, _lanes(j)] += jnp.sum(fin[4 * n + 3], axis=0, keepdims=True)
        for i in range(S5_BLOCKS):
            ws = pl.ds(i * S5_BW, S5_BW)
            bl = pl.ds(i * S5_BL, S5_BL)
            ui, dyi = u[:, i * S5_BW:(i + 1) * S5_BW], dy[:, i * S5_BW:(i + 1) * S5_BW]
            gr, gi = gr_ref[:, bl], gi_ref[:, bl]
            duseg_ref[:, ws] = _dot(gr, bbr_ref[i], _NT) + _dot(gi, bbi_ref[i], _NT) + d_ref[:, ws] * dyi
            dbbr_ref[i] += _dot(ui, gr, _TN)
            dbbi_ref[i] += _dot(ui, gi, _TN)
            dcr_ref[i] += _dot(hr_ref[:, bl], dyi, _TN)
            dci_ref[i] -= _dot(hi_ref[:, bl], dyi, _TN)
        dd_ref[...] += jnp.sum(dy * u, axis=0, keepdims=True)
        _from_segment_order(duseg_ref[...], stage_ref, duseg_ref, seg)
        du_ref[...] = duseg_ref[...].astype(du_ref.dtype)

    whole = pl.BlockSpec(memory_space=pltpu.VMEM)
    rev = lambda i: (nb - 1 - i, 0)
    const3 = lambda i: (0, 0, 0)
    return pl.pallas_call(
        body, name="s5_bwd", grid=(nb,),
        in_specs=[pl.BlockSpec((tb, S5_WIDTH), rev), pl.BlockSpec((tb, S5_WIDTH), rev),
                  pl.BlockSpec((tb, S5_WIDTH), lambda i: (nb - 1 - i, 4096 // S5_WIDTH)),
                  pl.BlockSpec((tb, S5_LANES), rev), pl.BlockSpec((tb, S5_LANES), rev)] + [whole] * 8
                 + [pl.BlockSpec(memory_space=pl.ANY)],
        out_specs=[pl.BlockSpec((tb, S5_WIDTH), lambda i: (nb - 1 - i, 4096 // S5_WIDTH)),
                   pl.BlockSpec((S5_BLOCKS, S5_BW, S5_BL), const3), pl.BlockSpec((S5_BLOCKS, S5_BW, S5_BL), const3),
                   pl.BlockSpec((S5_BLOCKS, S5_BL, S5_BW), const3), pl.BlockSpec((S5_BLOCKS, S5_BL, S5_BW), const3),
                   pl.BlockSpec((1, S5_WIDTH), lambda i: (0, 0)), pl.BlockSpec((2, S5_LANES), lambda i: (0, 0))],
        out_shape=[jax.ShapeDtypeStruct((t_len, IN_COLS), dproj.dtype),
                   jax.ShapeDtypeStruct((S5_BLOCKS, S5_BW, S5_BL), F32),
                   jax.ShapeDtypeStruct((S5_BLOCKS, S5_BW, S5_BL), F32),
                   jax.ShapeDtypeStruct((S5_BLOCKS, S5_BL, S5_BW), F32),
                   jax.ShapeDtypeStruct((S5_BLOCKS, S5_BL, S5_BW), F32),
                   jax.ShapeDtypeStruct((1, S5_WIDTH), F32), jax.ShapeDtypeStruct((2, S5_LANES), F32)],
        scratch_shapes=[pltpu.VMEM((tb, S5_LANES), F32), pltpu.VMEM((tb, S5_LANES), F32),
                        pltpu.VMEM((1, S5_LANES), F32), pltpu.VMEM((1, S5_LANES), F32),
                        pltpu.VMEM((SUBLANES, S5_LANES), F32), pltpu.VMEM((SUBLANES, S5_LANES), F32),
                        pltpu.VMEM((S5_WIDTH // 128, tb, 128), F32), pltpu.VMEM((tb, S5_WIDTH), F32),
                        pltpu.VMEM((tb, S5_WIDTH), F32), pltpu.VMEM((tb, S5_WIDTH), F32)],
        input_output_aliases={13: 0},
        compiler_params=_params("arbitrary"))(dgelu, y_pre, proj, h_re, h_im, lam_rows, p3_re, p3_im, bbr4, bbi4,
                                              cr4, ci4, d_row, dproj)


def _block_diag(per_group):
    g8 = S5_GROUPS // S5_BLOCKS
    eye = jnp.eye(g8, dtype=bool)[None, :, None, :, None]
    dense = jnp.where(eye, per_group.reshape(S5_BLOCKS, g8, S5_GROUP, 1, S5_STATE), 0.0)
    return dense.reshape(S5_BLOCKS, S5_BW, S5_BL)


def _diag_blocks(dense):
    g8 = S5_GROUPS // S5_BLOCKS
    ar = jnp.arange(g8)
    d5 = dense.reshape(S5_BLOCKS, g8, S5_GROUP, g8, S5_STATE)
    return d5[:, ar, :, ar, :].transpose(1, 0, 2, 3).reshape(S5_GROUPS, S5_GROUP, S5_STATE)


def _hg_gate_bwd(da, o, g, gn):
    dos, dgs, dgns = [], [], []
    for h in range(HG_HEADS):
        sl = slice(h * HG_DIM, (h + 1) * HG_DIM)
        oh, gh, dah, gnh = o[:, sl], g[:, sl], da[:, sl], gn[:, sl]
        rr = lax.rsqrt(jnp.mean(oh * oh, axis=-1, keepdims=True) + NORM_EPS)
        sg = _sig(gh)
        dgs.append(dah * (oh * rr * gnh) * _dsilu(gh, sg))
        don = dah * (gh * sg)
        t = don * gnh
        dos.append(rr * t - oh * (rr * rr * rr) * jnp.mean(t * oh, axis=-1, keepdims=True))
        dgns.append(jnp.sum(don * oh * rr, axis=0, keepdims=True))
    return jnp.concatenate(dos, axis=1), jnp.concatenate(dgs, axis=1), jnp.concatenate(dgns, axis=1)


MIX_BWD_COLS = ((3072, 1024), (4608, 512), (5120, 1024), (6144, 1024))


def _mix_bwd(dgl, h1, dh2, act_hg, ys2, ys_gelu, proj, o_hg, g2, ghn, b_glu, w, t_len, tm):
    nb = t_len // tm

    def body(dgl_ref, h1_ref, dh2_ref, act_ref, ys2_ref, ysg_ref, ghg_ref, z_ref, gh_ref, gs_ref, o_ref, g2_ref, gn_ref,
             bglu_ref, wg_ref, wo_ref, ws5_ref, whg_ref, wglu_ref,
             dh1_ref, dyh_ref, dys_ref, dglu_ref, dgelu_ref, do_ref, dg2_ref, dbglu_ref, dgn_ref, dproj_ref,
             st0, st1, st2, st3, sems):
        i = pl.program_id(0)
        stages = (st0, st1, st2, st3)

        def writes(step):
            rows = pl.ds(pl.multiple_of(step * tm, tm), tm)
            return [pltpu.make_async_copy(st, dproj_ref.at[rows, pl.ds(c0, wd)], sems.at[k])
                    for k, (st, (c0, wd)) in enumerate(zip(stages, MIX_BWD_COLS))]

        @pl.when(i > 0)
        def _():
            for cp in writes(i - 1):
                cp.wait()

        @pl.when(i == 0)
        def _():
            for ref in (dg2_ref, dbglu_ref, dgn_ref):
                ref[...] = jnp.zeros_like(ref)

        dx, dg2 = _rms_bwd(_dot(dgl_ref[...], wg_ref[...], _NT), h1_ref[...], g2_ref[...])
        dh1 = dh2_ref[...] + dx
        dh1_ref[...] = dh1
        dg2_ref[...] += dg2
        dm = _dot(dh1, wo_ref[...], _NT)
        sh, ss = _sig(gh_ref[...]), _sig(gs_ref[...])
        dyh, dys = _mx(dm * sh), _mx(dm * ss)
        dyh_ref[...] = dyh
        dys_ref[...] = dys
        st2[...] = (dm * _dot(act_ref[...], whg_ref[...]) * sh * (1.0 - sh)).astype(st2.dtype)
        st3[...] = (dm * _dot(ys2_ref[...], ws5_ref[...]) * ss * (1.0 - ss)).astype(st3.dtype)
        dys2 = _dot(dys, ws5_ref[...], _NT)
        gl_, z = _dot(ysg_ref[...], wglu_ref[...]) + bglu_ref[...], z_ref[...]
        a, b = gl_[:, :S5_WIDTH], gl_[:, S5_WIDTH:]
        sb, sz = _sig(b), _sig(z)
        silu = z * sz
        dglu = jnp.concatenate([dys2 * sb * silu, dys2 * a * silu * sb * (1.0 - sb)], axis=1)
        st1[...] = (dys2 * a * sb * _dsilu(z, sz)).astype(st1.dtype)
        dbglu_ref[...] += jnp.sum(dglu, axis=0, keepdims=True)
        dglu_ref[...] = _mx(dglu)
        dgelu_ref[...] = _dot(dglu, wglu_ref[...], _NT)
        d_o, dg, dgn = _hg_gate_bwd(_dot(dyh, whg_ref[...], _NT), o_ref[...], ghg_ref[...], gn_ref[...])
        do_ref[...] = d_o.astype(do_ref.dtype)
        st0[...] = dg.astype(st0.dtype)
        dgn_ref[...] += dgn
        for cp in writes(i):
            cp.start()

        @pl.when(i == nb - 1)
        def _():
            for cp in writes(i):
                cp.wait()

    tile = lambda wd, cb=0: pl.BlockSpec((tm, wd), functools.partial(lambda i, cb: (i, cb), cb=cb))
    row = lambda wd: pl.BlockSpec((1, wd), lambda i: (0, 0))
    whole = pl.BlockSpec(memory_space=pltpu.VMEM)
    return pl.pallas_call(
        body, name="mix_bwd", grid=(nb,),
        in_specs=[tile(1024), tile(1024), tile(1024), tile(1024), tile(512), tile(512), tile(1024, 3),
                  tile(512, 4608 // 512), tile(1024, 5), tile(1024, 6), tile(1024), row(1024), row(1024), row(1024)]
                 + [whole] * 5,
        out_specs=[tile(1024), tile(1024), tile(1024), tile(1024), tile(512), tile(1024), row(1024), row(1024),
                   row(1024), _HBM],
        out_shape=[jax.ShapeDtypeStruct((t_len, 1024), F32), jax.ShapeDtypeStruct((t_len, 1024), MXU_DTYPE),
                   jax.ShapeDtypeStruct((t_len, 1024), MXU_DTYPE), jax.ShapeDtypeStruct((t_len, 1024), MXU_DTYPE),
                   jax.ShapeDtypeStruct((t_len, 512), F32), jax.ShapeDtypeStruct((t_len, 1024), MXU_DTYPE),
                   jax.ShapeDtypeStruct((1, 1024), F32), jax.ShapeDtypeStruct((1, 1024), F32),
                   jax.ShapeDtypeStruct((1, 1024), F32), jax.ShapeDtypeStruct((t_len, IN_COLS), MXU_DTYPE)],
        scratch_shapes=[pltpu.VMEM((tm, wd), MXU_DTYPE) for _, wd in MIX_BWD_COLS] + [pltpu.SemaphoreType.DMA((4,))],
        compiler_params=_params("arbitrary"))(dgl, h1, dh2, act_hg, ys2, ys_gelu, proj, proj, proj, proj, o_hg, g2, ghn,
                                              b_glu, w["w_ple_gate"], w["w_out"], w["w_o_s5"], w["w_o_hg"],
                                              w["w_glu"])


def _local_step(x, p, target, w, sm, comm=None):
    t_len = x.shape[0]
    tm = min(256, t_len)
    tmm = min(512, t_len)
    tm_in = min(1024, t_len)
    tk = min(2048, t_len)
    tb_hg = min(256, t_len)
    tb_s5 = min(512, t_len)
    g1, g2, g3, ghn = sm["norm_g"], sm["ple_norm_g"], sm["final_norm_g"].reshape(1, D_MODEL), sm["hg_norm_g"]

    def rms_in(xv, g):
        return xv * lax.rsqrt(jnp.mean(xv * xv, axis=-1, keepdims=True) + NORM_EPS) * g

    in_shard = IN_COLS // N_CHIPS
    if comm is None:
        w_in = w["w_in"]
        proj, u = _mm_nn("mm_in", x, w_in, tm_in, in_shard, prologue=rms_in, consts=[g1])
    else:
        proj, u, w_in = comm.input_projection(x, g1, rms_in, tm_in)

    lanes = lambda a: a.reshape(1, S5_LANES)
    a_re, a_im = lanes(sm["s5_a_re"]), lanes(sm["s5_a_im"])
    ldt = lanes(jnp.broadcast_to(sm["s5_log_dt"].reshape(S5_GROUPS, 1), (S5_GROUPS, S5_STATE)))
    to_t = lambda b: b.reshape(S5_GROUPS, S5_STATE, S5_GROUP).transpose(2, 0, 1).reshape(S5_GROUP, S5_LANES)
    b_re_t, b_im_t = to_t(sm["s5_b_re"]), to_t(sm["s5_b_im"])
    scan_fwd, scan_rev, bbr_t, bbi_t = _s5_prep(a_re, a_im, ldt, b_re_t, b_im_t, tb_s5 // SUBLANES)
    from_t = lambda b: b.reshape(S5_GROUP, S5_GROUPS, S5_STATE).transpose(1, 0, 2)
    bbr_bd = _block_diag(from_t(bbr_t)).astype(MXU_DTYPE)
    bbi_bd = _block_diag(from_t(bbi_t)).astype(MXU_DTYPE)
    cr_bd = _block_diag(sm["s5_c_re"].reshape(S5_GROUPS, S5_GROUP, S5_STATE)).astype(MXU_DTYPE)
    ci_bd = _block_diag(sm["s5_c_im"].reshape(S5_GROUPS, S5_GROUP, S5_STATE)).astype(MXU_DTYPE)
    d_row = sm["s5_d"].reshape(1, S5_WIDTH)
    if comm is None:
        o_hg, act_hg, s_prev = _hgrn2_fwd(proj, sm["hg_lb"], ghn, t_len, tb_hg)
    else:
        o_hg, act_hg, s_prev, landed = _hgrn2_fwd(proj, sm["hg_lb"], ghn, t_len, tb_hg, riding=comm.gather_rest())
        w = comm.rest_weights(landed)
    h_re, h_im, y_pre, ys_gelu = _s5_fwd(proj, *scan_fwd, bbr_bd, bbi_bd,
                                          cr_bd.transpose(0, 2, 1), ci_bd.transpose(0, 2, 1), d_row, t_len, tb_s5)
    def mix_f(act, ysg, z, gh, gs, xv, w_glu, b_glu, w_o_hg, w_o_s5, w_out):
        yh = _dot(act, w_o_hg)
        gl_ = _dot(ysg, w_glu) + b_glu
        a, b = gl_[:, :S5_WIDTH], gl_[:, S5_WIDTH:]
        ys2_ = (a * _sig(b) * (z * _sig(z))).astype(MXU_DTYPE)
        ys = _dot(ys2_, w_o_s5)
        mg = (_sig(gh) * yh + _sig(gs) * ys).astype(MXU_DTYPE)
        return (ys2_, mg, xv + _dot(mg, w_out))

    ys2, merged, h1 = _rowwise(
        "mix_out", mix_f, t_len, tmm,
        [(act_hg, 1024, 0), (ys_gelu, 512, 0), (proj, 512, 4608 // 512), (proj, 1024, 5), (proj, 1024, 6),
         (x, 1024, 0)], [w["w_glu"], sm["b_glu"], w["w_o_hg"], w["w_o_s5"], w["w_out"]],
        [(512, MXU_DTYPE), (1024, MXU_DTYPE), (1024, F32)], ring=True)

    def head_f(h1v, pv, tgt, g_ple, g, w_ple, w_gate):
        r2 = lax.rsqrt(jnp.mean(h1v * h1v, axis=-1, keepdims=True) + NORM_EPS)
        n2_ = (h1v * r2 * g_ple).astype(MXU_DTYPE)
        glv, pev = _dot(n2_, w_gate), _dot(pv, w_ple)
        gate = _sig(glv)
        h2 = h1v + pev * gate
        r = lax.rsqrt(jnp.mean(h2 * h2, axis=-1, keepdims=True) + NORM_EPS)
        e = h2 * r * g - tgt
        loss = 0.5 * jnp.sum(jnp.mean(e * e, axis=-1, keepdims=True), axis=0, keepdims=True)
        dy = e * (1.0 / D_MODEL)
        dg = jnp.sum(dy * h2 * r, axis=0, keepdims=True)
        t = dy * g
        dh2 = r * t - h2 * (r * r * r) * jnp.mean(t * h2, axis=-1, keepdims=True)
        dpe, dgl_ = _mx(dh2 * gate), _mx(dh2 * pev * gate * (1.0 - gate))
        return (dh2, dgl_, jnp.broadcast_to(loss, (1, 128)), dg, _dot(pv, dpe, _TN), _dot(n2_, dgl_, _TN))

    gb = {}
    dh2, dgl, loss_row, d_g3, gb["w_ple"], gb["w_ple_gate"] = _rowwise(
        "ple_loss_head", head_f, t_len, tmm, [(h1, 1024, 0), (p, 256, 0), (target, 1024, 0)],
        [g2, g3, w["w_ple"], w["w_ple_gate"]], [(1024, F32), (1024, MXU_DTYPE)],
        accs=[(1, 128), (1, 1024), (256, 1024), (1024, 1024)], ring=True)

    dh1, dy_hg, dy_s5, dglu, dgelu, d_o, d_g2, d_bglu, d_ghn, dproj = _mix_bwd(
        dgl, h1, dh2, act_hg, ys2, ys_gelu, proj, o_hg, g2, ghn, sm["b_glu"], w, t_len, tm)
    gb["w_out"] = _mm_tn("mm_d_w_out", merged, dh1, tk, 1024)
    gb["w_o_s5"] = _mm_tn("mm_d_w_o_s5", ys2, dy_s5, tk, 1024)
    gb["w_glu"] = _mm_tn("mm_d_w_glu", ys_gelu, dglu, tk, 1024)
    dproj, d_bbr, d_bbi, d_crt, d_cit, d_d, d_lam = _s5_bwd(dgelu, y_pre, proj, h_re, h_im,
                                                            *scan_rev, bbr_bd, bbi_bd, cr_bd,
                                                            ci_bd, d_row, dproj, t_len, tb_s5)
    to_t3 = lambda b: b.transpose(1, 0, 2).reshape(S5_GROUP, S5_LANES)
    d_are, d_aim, d_ldt, d_br_t, d_bi_t = _s5_prep_bwd(a_re, a_im, ldt, b_re_t, b_im_t, d_lam,
                                                       to_t3(_diag_blocks(d_bbr)), to_t3(_diag_blocks(d_bbi)))
    gb["w_o_hg"] = _mm_tn("mm_d_w_o_hg", act_hg, dy_hg, tk, 1024)
    if comm is None:
        dproj, d_lb = _hgrn2_bwd(proj, d_o, s_prev, sm["hg_lb"], dproj, t_len, tb_hg)
    else:
        rest_grads = _pack_rest_full(gb)
        dproj, d_lb, rest_theirs = _hgrn2_bwd(proj, d_o, s_prev, sm["hg_lb"], dproj, t_len, tb_hg,
                                               riding=comm.swap(rest_grads))

    def in_b(duv, xv, dh, g):
        dx, dg = _rms_bwd(duv, xv, g)
        return (dh + dx, dg)

    in_args = ("mm_d_u_rms_in_bwd", dproj, w_in, tm_in, in_shard, in_b, [(x, 1024, 0), (dh1, 1024, 0)], [g1],
               [(1024, F32)])
    if comm is None:
        gb["w_in"] = _mm_tn("mm_d_w_in", u, dproj, tk, in_shard, col_shards=True)
        grad_x, d_g1 = _mm_nt_then(*in_args, accs=[(1, 1024)])
    else:
        gb["w_in"], landed = _mm_tn("mm_d_w_in", u, dproj, tk, in_shard, col_shards=True,
                                    riding=comm.scatter("rest", rest_grads, rest_theirs))
        comm.landed["rest"] = landed
        grad_x, d_g1, landed = _mm_nt_then(*in_args, accs=[(1, 1024)], riding=comm.scatter(
            "in", gb["w_in"].reshape(N_CHIPS, 2, D_MODEL // 2, in_shard)))
        comm.landed["in"] = landed

    back_t = lambda b: b.reshape(S5_GROUP, S5_GROUPS, S5_STATE).transpose(1, 2, 0).reshape(1, S5_GROUPS, S5_STATE,
                                                                                           S5_GROUP)
    gs = {
        "norm_g": d_g1, "hg_lb": d_lb, "hg_norm_g": d_ghn,
        "s5_a_re": d_are.reshape(1, S5_GROUPS, S5_STATE), "s5_a_im": d_aim.reshape(1, S5_GROUPS, S5_STATE),
        "s5_log_dt": d_ldt[0:1, :S5_GROUPS],
        "s5_b_re": back_t(d_br_t), "s5_b_im": back_t(d_bi_t),
        "s5_c_re": _diag_blocks(d_crt.transpose(0, 2, 1)).reshape(1, S5_GROUPS, S5_GROUP, S5_STATE),
        "s5_c_im": _diag_blocks(d_cit.transpose(0, 2, 1)).reshape(1, S5_GROUPS, S5_GROUP, S5_STATE),
        "s5_d": d_d.reshape(1, S5_GROUPS, S5_GROUP), "b_glu": d_bglu, "ple_norm_g": d_g2,
        "final_norm_g": d_g3.reshape(D_MODEL),
    }
    return loss_row, grad_x, gb, gs


def _shard_shape(name):
    r, c = BIG_SHAPE[name]
    return (r, c // N_CHIPS) if name in BIG_COL_SHARDED else (r // N_CHIPS, c)


def _pack_small(parts, last):
    flat = jnp.concatenate([parts[n].reshape(-1) for n in SMALL] + [last.reshape(-1)])
    return jnp.pad(flat, (0, SMALL_ROWS * PACK_W - flat.shape[0])).reshape(SMALL_ROWS, PACK_W)


def _unpack_small(packed):
    flat, out, off = packed.reshape(-1), {}, 0
    for n in SMALL:
        size = 1
        for d in SMALL_SHAPE[n]:
            size *= d
        out[n] = flat[off:off + size].reshape(SMALL_SHAPE[n])
        off += size
    return out, flat[off]


def _place():
    x, y, c = lax.axis_index("x"), lax.axis_index("y"), lax.axis_index("c")
    return x, y, c, [(1 - x, y), (x, 1 - y), (1 - x, 1 - y)]


def _remote(src, dst, send_sems, recv_sems, k, to):
    return pltpu.make_async_remote_copy(src_ref=src, dst_ref=dst, send_sem=send_sems.at[k], recv_sem=recv_sems.at[k],
                                        device_id=to, device_id_type=MESH)


REST = tuple(n for n in BIG if n != "w_in")
REST_ROWS = sum(BIG_SHAPE[n][0] * BIG_SHAPE[n][1] for n in REST) // (N_CHIPS * PACK_W)
IN_SHARD = IN_COLS // N_CHIPS
IN_TILE, REST_TILE = 256, 272


def _pack_rest(parts):
    return jnp.concatenate([parts[n].reshape(-1, PACK_W) for n in REST], axis=0)


def _unpack_rest(packed):
    out, off = {}, 0
    for n in REST:
        r, c = _shard_shape(n)
        rows = r * c // PACK_W
        out[n] = packed[off:off + rows].reshape(1, r, c)
        off += rows
    return out


def _unpack_rest_full(gathered):
    out, off = {}, 0
    for n in REST:
        r, c = _shard_shape(n)
        rows = r * c // PACK_W
        sh = gathered[:, off:off + rows].reshape(N_CHIPS, r, c)
        out[n] = sh.transpose(1, 0, 2).reshape(BIG_SHAPE[n]) if n in BIG_COL_SHARDED else sh.reshape(BIG_SHAPE[n])
        off += rows
    return out


def _pack_rest_full(full):
    parts = []
    for n in REST:
        r, c = _shard_shape(n)
        g = full[n]
        sh = g.reshape(BIG_SHAPE[n][0], N_CHIPS, c).transpose(1, 0, 2) if n in BIG_COL_SHARDED else g
        parts.append(sh.reshape(N_CHIPS, r * c // PACK_W, PACK_W))
    return jnp.concatenate(parts, axis=1).reshape(N_CHIPS, 2, REST_ROWS // 2, PACK_W)


def _swap_halves(pgs, name="exchange_halves"):
    n = len(pgs)

    def body(*refs):
        pg_refs, out_refs, (send_sems, recv_sems) = refs[:n], refs[n:2 * n], refs[2 * n:]
        x, y, c, _ = _place()
        cps = [_remote(pg_ref.at[j, 1 - c], out_ref.at[j], send_sems, recv_sems, N_CHIPS * g + j, (x, y, 1 - c))
               for g, (pg_ref, out_ref) in enumerate(zip(pg_refs, out_refs)) for j in range(N_CHIPS)]
        for cp in cps:
            cp.start()
        for cp in cps:
            cp.wait()

    return pl.pallas_call(
        body, name=name, in_specs=[_HBM] * n, out_specs=[_HBM] * n,
        out_shape=[jax.ShapeDtypeStruct((N_CHIPS,) + pg.shape[2:], pg.dtype) for pg in pgs],
        scratch_shapes=[pltpu.SemaphoreType.DMA((N_CHIPS * n,)), pltpu.SemaphoreType.DMA((N_CHIPS * n,))])(*pgs)


def _share_halves(gs):
    n = len(gs)

    def body(*refs):
        g_refs, out_refs, (send_sems, recv_sems) = refs[:n], refs[n:2 * n], refs[2 * n:]
        x, y, c, _ = _place()
        cps = [_remote(g_ref, out_ref.at[c], send_sems, recv_sems, g, (x, y, 1 - c))
               for g, (g_ref, out_ref) in enumerate(zip(g_refs, out_refs))]
        for cp in cps:
            cp.start()
        for g, (g_ref, out_ref) in enumerate(zip(g_refs, out_refs)):
            _remote(g_ref, out_ref.at[1 - c], send_sems, recv_sems, g, (x, y, 1 - c)).wait_recv()
        for cp in cps:
            cp.wait_send()

    return pl.pallas_call(
        body, name="share_half", in_specs=[_HBM] * n, out_specs=[_HBM] * n,
        out_shape=[jax.ShapeDtypeStruct((2,) + g.shape, g.dtype) for g in gs],
        scratch_shapes=[pltpu.SemaphoreType.DMA((n,)), pltpu.SemaphoreType.DMA((n,))])(*gs)


def _pair_sum(name, pg, theirs, c, tile):
    _, _, rows, width = pg.shape

    def body(c_ref, a_ref, b_ref, o_ref):
        o_ref[...] = (a_ref[...] + b_ref[...]).astype(o_ref.dtype)

    return pl.pallas_call(
        body, name=name,
        grid_spec=pltpu.PrefetchScalarGridSpec(
            num_scalar_prefetch=1, grid=(N_CHIPS, rows // tile),
            in_specs=[pl.BlockSpec((None, None, tile, width), lambda j, i, c_ref: (j, c_ref[0], i, 0)),
                      pl.BlockSpec((None, tile, width), lambda j, i, c_ref: (j, i, 0))],
            out_specs=pl.BlockSpec((None, tile, width), lambda j, i, c_ref: (j, i, 0))),
        out_shape=jax.ShapeDtypeStruct((N_CHIPS, rows, width), WIRE_DTYPE),
        compiler_params=_params("arbitrary", "arbitrary"))(c.reshape(1), pg, theirs)


def _chip_sum(name, ps, others, k, tile):
    _, rows, width = ps.shape

    def body(k_ref, a_ref, b_ref, o_ref):
        o_ref[...] = ((a_ref[...].astype(F32) + b_ref[0].astype(F32)) + b_ref[1].astype(F32)) + b_ref[2].astype(F32)

    return pl.pallas_call(
        body, name=name,
        grid_spec=pltpu.PrefetchScalarGridSpec(
            num_scalar_prefetch=1, grid=(rows // tile,),
            in_specs=[pl.BlockSpec((None, tile, width), lambda i, k_ref: (k_ref[0], i, 0)),
                      pl.BlockSpec((3, tile, width), lambda i, k_ref: (0, i, 0))],
            out_specs=pl.BlockSpec((tile, width), lambda i, k_ref: (i, 0))),
        out_shape=jax.ShapeDtypeStruct((rows, width), F32),
        compiler_params=_params("arbitrary"))(k.reshape(1), ps, others)


def _mm_in_gathering(x, g1, prologue, in_wire, chip, tm):
    m, k = x.shape
    half, ns = in_wire.shape[1:]
    nrow = m // tm

    def flip(j):
        return jnp.where(j == 1, 2, jnp.where(j == 2, 1, j))

    def body(k_ref, x_ref, g_ref, wire_ref, proj_ref, u_ref, all_ref, kept, b_ref, load_sems, send_sems, recv_sems):
        j, i = pl.program_id(0), pl.program_id(1)
        px, py, c, chips = _place()
        sibling = (px, py, 1 - c)

        def over_ici(r, chip_slot):
            cx, cy = chips[r]
            return _remote(wire_ref.at[c], all_ref.at[chip_slot, c], send_sems, recv_sems, r, (cx, cy, c))

        def to_sibling(r, half_slot):
            cx, cy = chips[r]
            return _remote(all_ref.at[2 * cx + cy, c], all_ref.at[2 * cx + cy, half_slot], send_sems, recv_sems,
                           3 + r, sibling)

        def loads(src, slot):
            return [pltpu.make_async_copy(src.at[h], b_ref.at[slot, pl.ds(h * half, half)], load_sems.at[h])
                    for h in range(2)]

        def shard(r):
            cx, cy = chips[r]
            over_ici(r, 2 * cx + cy).wait_recv()
            if r == 0:
                over_ici(2, 2 * px + py).start()
            to_sibling(r, c).start()
            to_sibling(r, 1 - c).wait_recv()
            return all_ref.at[2 * cx + cy]

        @pl.when((j == 0) & (i == 0))
        def _():
            for r in range(2):
                over_ici(r, 2 * px + py).start()
            for cp in loads(wire_ref, 0):
                cp.start()
            for cp in loads(wire_ref, 0):
                cp.wait()

        @pl.when((j == 1) & (i == 0))
        def _():
            cps = loads(shard(0), 1)
            for cp in cps:
                cp.start()
            for cp in cps:
                cp.wait()

        for nxt in (2, 3):
            @pl.when((j == nxt - 1) & (i == nrow // 2))
            def _(nxt=nxt):
                for cp in loads(shard(nxt - 1), nxt % 2):
                    cp.start()

            @pl.when((j == nxt) & (i == 0))
            def _(nxt=nxt):
                for cp in loads(wire_ref, nxt % 2):
                    cp.wait()

        rows = pl.ds(pl.multiple_of(i * tm, tm), tm)

        @pl.when(j == 0)
        def _():
            tile = _mx(prologue(x_ref[...], g_ref[...]))
            kept[rows, :] = tile
            u_ref[...] = tile

        proj_ref[...] = _dot(kept[rows, :], b_ref[lax.rem(j, 2)])

        @pl.when((j == N_CHIPS - 1) & (i == nrow - 1))
        def _():
            for r in range(3):
                over_ici(r, 2 * px + py).wait_send()
                to_sibling(r, c).wait_send()

    once = lambda j, i, k_ref: (jnp.where(j == 0, i, nrow - 1), 0)
    return pl.pallas_call(
        body, name="mm_in",
        grid_spec=pltpu.PrefetchScalarGridSpec(
            num_scalar_prefetch=1, grid=(N_CHIPS, nrow),
            in_specs=[pl.BlockSpec((tm, k), once), pl.BlockSpec(g1.shape, lambda j, i, k_ref: (0, 0)), _HBM],
            out_specs=[pl.BlockSpec((tm, ns), lambda j, i, k_ref: (i, jnp.bitwise_xor(k_ref[0], flip(j)))),
                       pl.BlockSpec((tm, k), once), _HBM],
            scratch_shapes=[pltpu.VMEM((m, k), MXU_DTYPE), pltpu.VMEM((2, 2 * half, ns), in_wire.dtype),
                            pltpu.SemaphoreType.DMA((2,)), pltpu.SemaphoreType.DMA((6,)),
                            pltpu.SemaphoreType.DMA((6,))]),
        out_shape=[jax.ShapeDtypeStruct((m, N_CHIPS * ns), F32), jax.ShapeDtypeStruct((m, k), MXU_DTYPE),
                   jax.ShapeDtypeStruct((N_CHIPS,) + in_wire.shape, in_wire.dtype)],
        compiler_params=_params("arbitrary", "arbitrary"))(chip.reshape(1), x, g1, in_wire)


class _StepComm:
    TILES = {"in": IN_TILE, "rest": REST_TILE}

    def __init__(self, in_wire, rest_wire, chip, core):
        self.in_wire, self.rest_wire, self.chip, self.core = in_wire, rest_wire, chip, core
        self.sums, self.landed = {}, {}

    def input_projection(self, x, g1, prologue, tm):
        proj, u, shards = _mm_in_gathering(x, g1, prologue, self.in_wire, self.chip, tm)
        shards = lax.dynamic_update_slice(shards, self.in_wire[None], (self.chip, 0, 0, 0))
        return proj, u, shards.reshape(N_CHIPS, D_MODEL, IN_SHARD)

    def gather_rest(self):
        wire = self.rest_wire

        def sends(ins, outs, send_sems, recv_sems):
            (w_ref,), (out_ref,) = ins, outs
            x, y, c, chips = _place()
            return [_remote(w_ref.at[c], out_ref.at[2 * x + y, c], send_sems, recv_sems, 4 * j + 2 * c + to,
                            (cx, cy, to)) for j, (cx, cy) in enumerate(chips) for to in (0, 1)]

        def recvs(ins, outs, send_sems, recv_sems):
            (w_ref,), (out_ref,) = ins, outs
            _, _, c, chips = _place()
            return [_remote(w_ref.at[c], out_ref.at[2 * cx + cy, by], send_sems, recv_sems, 4 * j + 2 * by + c,
                            (cx, cy, by)) for j, (cx, cy) in enumerate(chips) for by in (0, 1)]

        def start(*refs):
            for cp in sends(*refs):
                cp.start()

        def wait(*refs):
            for cp in recvs(*refs):
                cp.wait_recv()
            for cp in sends(*refs):
                cp.wait_send()

        return _Riding((wire,), (jax.ShapeDtypeStruct((N_CHIPS,) + wire.shape, wire.dtype),), 12, start, wait)

    def rest_weights(self, landed):
        full = lax.dynamic_update_slice(landed, self.rest_wire[None], (self.chip, 0, 0, 0))
        return _unpack_rest_full(full.reshape(N_CHIPS, REST_ROWS, PACK_W))

    def swap(self, pg):
        def copies(ins, outs, send_sems, recv_sems):
            (pg_ref,), (out_ref,) = ins, outs
            x, y, c, _ = _place()
            return [_remote(pg_ref.at[j, 1 - c], out_ref.at[j], send_sems, recv_sems, j, (x, y, 1 - c))
                    for j in range(N_CHIPS)]

        def start(*refs):
            for cp in copies(*refs):
                cp.start()

        def wait(*refs):
            for cp in copies(*refs):
                cp.wait()

        return _Riding((pg,), (jax.ShapeDtypeStruct((N_CHIPS,) + pg.shape[2:], pg.dtype),), N_CHIPS, start, wait)

    def scatter(self, group, pg, theirs=None):
        if theirs is None:
            (theirs,) = _swap_halves([pg], "exchange_halves_" + group)
        ps = _pair_sum("sum_pair_" + group, pg, theirs, self.core, self.TILES[group])
        self.sums[group] = ps

        def copies(ins, outs, send_sems, recv_sems):
            (ps_ref,), (out_ref,) = ins, outs
            _, _, c, chips = _place()
            return [_remote(ps_ref.at[2 * cx + cy], out_ref.at[j], send_sems, recv_sems, j, (cx, cy, c))
                    for j, (cx, cy) in enumerate(chips)]

        def start(*refs):
            for cp in copies(*refs):
                cp.start()

        def wait(*refs):
            for cp in copies(*refs):
                cp.wait()

        return _Riding((ps,), (jax.ShapeDtypeStruct((3,) + ps.shape[1:], ps.dtype),), 3, start, wait)

    def reduced(self, group):
        return _chip_sum("sum_chips_" + group, self.sums[group], self.landed[group], self.chip, self.TILES[group])


def _adamw(w, g, m, v):
    m = ADAM_B1 * m + (1.0 - ADAM_B1) * g
    v = ADAM_B2 * v + (1.0 - ADAM_B2) * (g * g)
    m_hat = m / (1.0 - ADAM_B1 ** ADAM_STEP)
    v_hat = v / (1.0 - ADAM_B2 ** ADAM_STEP)
    return -ADAM_LR * (m_hat / (jnp.sqrt(v_hat) + ADAM_EPS) + ADAM_WD * w), m, v


def _small_reduce_adamw(part, w, m, v):
    def body(part_ref, w_ref, m_ref, v_ref, g_ref, d_ref, nm_ref, nv_ref, all_ref, send_sems, recv_sems):
        x, y, c, chips = _place()
        me, sibling = (x, y, c), (x, y, 1 - c)

        def rows(px, py, pc):
            return all_ref.at[4 * px + 2 * py + pc]

        all_ref[4 * x + 2 * y + c] = part_ref[...]
        first = [_remote(part_ref, rows(*me), send_sems, recv_sems, 0, sibling)]
        first += [_remote(part_ref, rows(*me), send_sems, recv_sems, 1 + j, (cx, cy, c))
                  for j, (cx, cy) in enumerate(chips)]
        for cp in first:
            cp.start()
        passed = []
        for j, (cx, cy) in enumerate(chips):
            _remote(part_ref, rows(cx, cy, c), send_sems, recv_sems, 1 + j, me).wait_recv()
            cp = _remote(rows(cx, cy, c), rows(cx, cy, c), send_sems, recv_sems, 4 + j, sibling)
            cp.start()
            passed.append(cp)
        _remote(part_ref, rows(*sibling), send_sems, recv_sems, 0, me).wait_recv()
        for j, (cx, cy) in enumerate(chips):
            _remote(part_ref, rows(cx, cy, 1 - c), send_sems, recv_sems, 4 + j, me).wait_recv()
        for cp in first + passed:
            cp.wait_send()
        g = all_ref[0]
        for dev in range(1, N_DEV):
            g = g + all_ref[dev]
        delta, nm, nv = _adamw(w_ref[...], g, m_ref[...], v_ref[...])
        g_ref[...] = g
        d_ref[...] = delta
        nm_ref[...] = nm
        nv_ref[...] = nv

    whole = pl.BlockSpec(memory_space=pltpu.VMEM)
    shape = jax.ShapeDtypeStruct((SMALL_ROWS, PACK_W), F32)
    return pl.pallas_call(
        body, name="small_reduce_adamw", in_specs=[whole] * 4, out_specs=[whole] * 4, out_shape=[shape] * 4,
        scratch_shapes=[pltpu.VMEM((N_DEV, SMALL_ROWS, PACK_W), F32), pltpu.SemaphoreType.DMA((7,)),
                        pltpu.SemaphoreType.DMA((7,))],
        compiler_params=pltpu.CompilerParams(vmem_limit_bytes=VMEM_LIMIT))(part, w, m, v)


def kernel(x, p, norm_g, w_in, hg_lb, hg_norm_g, w_o_hg, s5_a_re, s5_a_im, s5_log_dt, s5_b_re, s5_b_im, s5_c_re, s5_c_im, s5_d, w_glu, b_glu, w_o_s5, w_out, ple_norm_g, w_ple, w_ple_gate, final_norm_g, loss_target, m_norm_g, m_w_in, m_hg_lb, m_hg_norm_g, m_w_o_hg, m_s5_a_re, m_s5_a_im, m_s5_log_dt, m_s5_b_re, m_s5_b_im, m_s5_c_re, m_s5_c_im, m_s5_d, m_w_glu, m_b_glu, m_w_o_s5, m_w_out, m_ple_norm_g, m_w_ple, m_w_ple_gate, m_final_norm_g, v_norm_g, v_w_in, v_hg_lb, v_hg_norm_g, v_w_o_hg, v_s5_a_re, v_s5_a_im, v_s5_log_dt, v_s5_b_re, v_s5_b_im, v_s5_c_re, v_s5_c_im, v_s5_d, v_w_glu, v_b_glu, v_w_o_s5, v_w_out, v_ple_norm_g, v_w_ple, v_w_ple_gate, v_final_norm_g):
    given = dict(locals())
    wts = {n: given[n] for n in WEIGHTS}
    mom = {n: given["m_" + n] for n in WEIGHTS}
    var = {n: given["v_" + n] for n in WEIGHTS}
    cx, cy, cc = lax.axis_index("x"), lax.axis_index("y"), lax.axis_index("c")
    chip = (2 * cx + cy).astype(jnp.int32)

    core = cc.astype(jnp.int32)
    rest_shard = _pack_rest({n: wts[n][0] for n in REST})
    comm = _StepComm(wts["w_in"][0].astype(MXU_DTYPE).reshape(2, D_MODEL // 2, IN_SHARD),
                     rest_shard.astype(MXU_DTYPE).reshape(2, REST_ROWS // 2, PACK_W), chip, core)

    t_len = x.shape[1]
    loss_row, grad_x, g_big, g_small = _local_step(x.reshape(t_len, D_MODEL), p.reshape(t_len, -1),
                                                   loss_target.reshape(t_len, D_MODEL), None,
                                                   {n: wts[n] for n in SMALL}, comm)

    zero = jnp.zeros((), F32)
    sg, sd, snm, snv = _small_reduce_adamw(_pack_small(g_small, loss_row[0, 0]),
                                           _pack_small({n: wts[n] for n in SMALL}, zero),
                                           _pack_small({n: mom[n] for n in SMALL}, zero),
                                           _pack_small({n: var[n] for n in SMALL}, zero))
    (sg, loss), (sd, _), (snm, _), (snv, _) = (_unpack_small(a) for a in (sg, sd, snm, snv))

    halves = [comm.reduced("in"), comm.reduced("rest")]
    g_in, g_rest = [lax.dynamic_update_slice(got, mine[None], (core, 0, 0))
                    for got, mine in zip(_share_halves(halves), halves)]
    g_in, g_rest = g_in.reshape(D_MODEL, IN_SHARD), g_rest.reshape(REST_ROWS, PACK_W)

    def adam_f(wv, gv, mv, vv):
        return _adamw(wv, gv, mv, vv)

    d_in, nm_in, nv_in = _rowwise("adamw_in", adam_f, D_MODEL, IN_TILE,
                                  [(wts["w_in"][0], IN_SHARD, 0), (g_in, IN_SHARD, 0), (mom["w_in"][0], IN_SHARD, 0),
                                   (var["w_in"][0], IN_SHARD, 0)], [], [(IN_SHARD, F32)] * 3)
    d_rest, nm_rest, nv_rest = _rowwise("adamw_rest", adam_f, REST_ROWS, REST_TILE,
                                        [(rest_shard, PACK_W, 0), (g_rest, PACK_W, 0),
                                         (_pack_rest({n: mom[n][0] for n in REST}), PACK_W, 0),
                                         (_pack_rest({n: var[n][0] for n in REST}), PACK_W, 0)], [],
                                        [(PACK_W, F32)] * 3)
    bg, bd, bnm, bnv = (dict(_unpack_rest(rest), w_in=a.reshape(1, D_MODEL, IN_SHARD))
                        for rest, a in ((g_rest, g_in), (d_rest, d_in), (nm_rest, nm_in), (nv_rest, nv_in)))

    outs = [loss, grad_x.reshape(x.shape)]
    for small, big in ((sg, bg), (sd, bd), (snm, bnm), (snv, bnv)):
        outs += [big[n] if n in BIG else small[n] for n in WEIGHTS]
    return tuple(outs)
```

```python
import functools
from typing import Callable, NamedTuple

import jax
import jax.numpy as jnp
from jax import lax
from jax.experimental import pallas as pl
from jax.experimental.pallas import tpu as pltpu

F32 = jnp.float32
MXU_DTYPE = jnp.bfloat16
WIRE_DTYPE = jnp.bfloat16
NORM_EPS = 1e-6
D_MODEL = 1024
HG_HEADS = 8
HG_DIM = 128
HG_CHUNK = 64
S5_WIDTH = 512
S5_GROUPS = 32
S5_GROUP = 16
S5_STATE = 64
S5_LANES = S5_GROUPS * S5_STATE
IN_COLS = 7168
SUBLANES = 8
VMEM_LIMIT = 56 * 1024 * 1024
HIGHEST = lax.Precision.HIGHEST
MESH = pl.DeviceIdType.MESH

ADAM_LR, ADAM_B1, ADAM_B2, ADAM_EPS, ADAM_WD, ADAM_STEP = 0.001, 0.9, 0.999, 1e-08, 0.01, 10

BIG = ("w_in", "w_o_hg", "w_glu", "w_o_s5", "w_out", "w_ple", "w_ple_gate")
BIG_SHAPE = {"w_in": (1024, 7168), "w_o_hg": (1024, 1024), "w_glu": (512, 1024), "w_o_s5": (512, 1024),
             "w_out": (1024, 1024), "w_ple": (256, 1024), "w_ple_gate": (1024, 1024)}
BIG_COL_SHARDED = ("w_in", "w_glu", "w_o_s5", "w_ple")
SMALL = ("norm_g", "hg_lb", "hg_norm_g", "s5_a_re", "s5_a_im", "s5_log_dt", "s5_b_re", "s5_b_im", "s5_c_re",
         "s5_c_im", "s5_d", "b_glu", "ple_norm_g", "final_norm_g")
SMALL_SHAPE = {"norm_g": (1, 1024), "hg_lb": (2, 1024), "hg_norm_g": (1, 1024), "s5_a_re": (1, 32, 64),
               "s5_a_im": (1, 32, 64), "s5_log_dt": (1, 32), "s5_b_re": (1, 32, 64, 16), "s5_b_im": (1, 32, 64, 16),
               "s5_c_re": (1, 32, 16, 64), "s5_c_im": (1, 32, 16, 64), "s5_d": (1, 32, 16), "b_glu": (1, 1024),
               "ple_norm_g": (1, 1024), "final_norm_g": (1024,)}
WEIGHTS = ("norm_g", "w_in", "hg_lb", "hg_norm_g", "w_o_hg", "s5_a_re", "s5_a_im", "s5_log_dt", "s5_b_re", "s5_b_im",
           "s5_c_re", "s5_c_im", "s5_d", "w_glu", "b_glu", "w_o_s5", "w_out", "ple_norm_g", "w_ple", "w_ple_gate",
           "final_norm_g")
N_CHIPS = 4
N_DEV = 8
PACK_W = 1024
SMALL_ROWS = 144


def _params(*sem):
    return pltpu.CompilerParams(dimension_semantics=sem, vmem_limit_bytes=VMEM_LIMIT)


def _sig(x):
    return 1.0 / (1.0 + jnp.exp(-x))


def _dsilu(z, s):
    return s * (1.0 + z * (1.0 - s))


def _mx(x):
    return x.astype(MXU_DTYPE)


def _dot(a, b, dims=(((1,), (0,)), ((), ()))):
    return lax.dot_general(_mx(a), _mx(b), dims, preferred_element_type=F32)


_NT = (((1,), (1,)), ((), ()))
_TN = (((0,), (0,)), ((), ()))


def _dot32(a, b):
    return jnp.dot(a, b, precision=HIGHEST, preferred_element_type=F32)


def _rms_bwd(dy, x, g):
    r = lax.rsqrt(jnp.mean(x * x, axis=-1, keepdims=True) + NORM_EPS)
    t = dy * g
    dx = r * t - x * (r * r * r) * jnp.mean(t * x, axis=-1, keepdims=True)
    return dx, jnp.sum(dy * x * r, axis=0, keepdims=True)


RING = 3


def _rowwise(name, fn, n_rows_total, tm, rows, consts, outs, accs=(), alias=None, ring=False):
    n_r, n_c, n_o, n_a = len(rows), len(consts), len(outs), len(accs)
    n_steps = n_rows_total // tm

    def body(*refs):
        row_refs = refs[:n_r]
        const_refs = refs[n_r:n_r + n_c]
        pos = n_r + n_c + (1 if alias is not None else 0)
        out_refs = refs[pos:pos + n_o]
        acc_refs = refs[pos + n_o:pos + n_o + n_a]
        if ring:
            slot_refs, sems = refs[pos + n_o + n_a:pos + n_o + n_a + n_r], refs[pos + n_o + n_a + n_r]
            i = pl.program_id(0)

            def fetch(step):
                first = step * tm if isinstance(step, int) else pl.multiple_of(step * tm, tm)
                return [pltpu.make_async_copy(src.at[pl.ds(first, tm), pl.ds(cb * w, w)], dst.at[step % RING],
                                              sems.at[k, step % RING])
                        for k, (src, dst, (_, w, cb)) in enumerate(zip(row_refs, slot_refs, rows))]

            @pl.when(i == 0)
            def _():
                for step in range(min(RING - 1, n_steps)):
                    for cp in fetch(step):
                        cp.start()

            @pl.when(i + (RING - 1) < n_steps)
            def _():
                for cp in fetch(i + (RING - 1)):
                    cp.start()

            for cp in fetch(i):
                cp.wait()
            tiles = [dst[i % RING] for dst in slot_refs]
        else:
            tiles = [r[...] for r in row_refs]
        res = fn(*tiles, *[r[...] for r in const_refs])
        for r, v in zip(out_refs, res[:n_o]):
            r[...] = v.astype(r.dtype)
        if n_a:
            @pl.when(pl.program_id(0) == 0)
            def _():
                for r in acc_refs:
                    r[...] = jnp.zeros_like(r)
            for r, v in zip(acc_refs, res[n_o:]):
                r[...] += v

    if ring:
        in_specs = [pl.BlockSpec(memory_space=pl.ANY)] * n_r
        scratch = [pltpu.VMEM((RING, tm, w), a.dtype) for (a, w, _) in rows] + [pltpu.SemaphoreType.DMA((n_r, RING))]
    else:
        in_specs = [pl.BlockSpec((tm, w), functools.partial(lambda i, cb: (i, cb), cb=cb)) for (_, w, cb) in rows]
        scratch = []
    in_specs += [pl.BlockSpec(c.shape, lambda i: (0, 0)) for c in consts]
    args = [a for (a, _, _) in rows] + list(consts)
    out_shape, out_specs = [], []
    for o in outs:
        w, dt = o[0], o[1]
        cb, total = (o[2], o[3]) if len(o) == 4 else (0, w)
        out_shape.append(jax.ShapeDtypeStruct((n_rows_total, total), dt))
        out_specs.append(pl.BlockSpec((tm, w), functools.partial(lambda i, cb: (i, cb), cb=cb)))
    io_alias = {}
    if alias is not None:
        in_specs.append(pl.BlockSpec(memory_space=pl.ANY))
        args.append(alias[0])
        io_alias = {len(args) - 1: alias[1]}
    for (r, w) in accs:
        out_shape.append(jax.ShapeDtypeStruct((r, w), F32))
        out_specs.append(pl.BlockSpec((r, w), lambda i: (0, 0)))
    res = pl.pallas_call(body, name=name, grid=(n_steps,), in_specs=in_specs, out_specs=out_specs,
                         out_shape=out_shape, scratch_shapes=scratch, input_output_aliases=io_alias,
                         compiler_params=_params("arbitrary"))(*args)
    return res


class _Riding(NamedTuple):
    ins: tuple
    outs: tuple
    n_sems: int
    start: Callable
    wait: Callable


_HBM = pl.BlockSpec(memory_space=pl.ANY)


def _ride(riding, refs, n_in, n_out, n_scratch, first, last):
    if riding is None:
        return refs[:n_in], refs[n_in:n_in + n_out], refs[n_in + n_out:]
    r_in, r_out = len(riding.ins), len(riding.outs)
    ins, rins = refs[:n_in], refs[n_in:n_in + r_in]
    pos = n_in + r_in
    outs, routs = refs[pos:pos + n_out], refs[pos + n_out:pos + n_out + r_out]
    pos += n_out + r_out
    scratch, (send_sems, recv_sems) = refs[pos:pos + n_scratch], refs[pos + n_scratch:]

    @pl.when(first)
    def _():
        riding.start(rins, routs, send_sems, recv_sems)

    @pl.when(last)
    def _():
        riding.wait(rins, routs, send_sems, recv_sems)

    return ins, outs, scratch


def _riding_call(riding, body, name, grid, in_specs, args, out_specs, out_shape, scratch, io_alias=None):
    if riding is not None:
        in_specs = list(in_specs) + [_HBM] * len(riding.ins)
        args = list(args) + list(riding.ins)
        out_specs = list(out_specs) + [_HBM] * len(riding.outs)
        out_shape = list(out_shape) + list(riding.outs)
        scratch = list(scratch) + [pltpu.SemaphoreType.DMA((riding.n_sems,))] * 2
    return pl.pallas_call(body, name=name, grid=grid, in_specs=in_specs, out_specs=out_specs, out_shape=out_shape,
                          scratch_shapes=scratch, input_output_aliases=io_alias or {},
                          compiler_params=_params(*(["arbitrary"] * len(grid))))(*args)


def _mm_nn(name, a, b, tm, tn, riding=None, prologue=None, consts=()):
    m, k = a.shape
    n = b.shape[1] if b.ndim == 2 else b.shape[0] * b.shape[2]
    grid = (n // tn, m // tm)
    n_out, scratch = (1, []) if prologue is None else (2, [pltpu.VMEM((m, k), MXU_DTYPE)])

    def body(*refs):
        j, i = pl.program_id(0), pl.program_id(1)
        ins, outs, kept = _ride(riding, refs, 2 + len(consts), n_out, len(scratch), (j == 0) & (i == 0),
                                (j == grid[0] - 1) & (i == grid[1] - 1))
        if prologue is None:
            left = ins[0][...]
        else:
            rows = pl.ds(pl.multiple_of(i * tm, tm), tm)

            @pl.when(j == 0)
            def _():
                tile = _mx(prologue(ins[0][...], *[c[...] for c in ins[2:]]))
                kept[0][rows, :] = tile
                outs[1][...] = tile

            left = kept[0][rows, :]
        outs[0][...] = _dot(left, ins[1][...])

    once = (lambda j, i: (i, 0)) if prologue is None else (lambda j, i: (jnp.where(j == 0, i, grid[1] - 1), 0))
    b_spec = (pl.BlockSpec((k, tn), lambda j, i: (0, j)) if b.ndim == 2
              else pl.BlockSpec((None, k, tn), lambda j, i: (j, 0, 0)))
    in_specs = [pl.BlockSpec((tm, k), once), b_spec]
    in_specs += [pl.BlockSpec(c.shape, lambda j, i: (0, 0)) for c in consts]
    out_specs = [pl.BlockSpec((tm, tn), lambda j, i: (i, j))]
    out_shape = [jax.ShapeDtypeStruct((m, n), F32)]
    if prologue is not None:
        out_specs.append(pl.BlockSpec((tm, k), once))
        out_shape.append(jax.ShapeDtypeStruct((m, k), MXU_DTYPE))
    res = _riding_call(riding, body, name, grid, in_specs, [a, b] + list(consts), out_specs, out_shape, scratch)
    return res[0] if riding is None and prologue is None else res


def _mm_nt_then(name, a, b, tm, tn, fn, rows, consts, outs, accs=(), alias=None, riding=None):
    m, n = a.shape
    k = b.shape[-2]
    steps = n // tn
    n_r, n_c, n_o, n_a = len(rows), len(consts), len(outs), len(accs)

    def body(*refs):
        a_ref, b_ref = refs[:2]
        row_refs = refs[2:2 + n_r]
        const_refs = refs[2 + n_r:2 + n_r + n_c]
        i, s = pl.program_id(0), pl.program_id(1)
        n_in = 2 + n_r + n_c + (1 if alias is not None else 0)
        _, outs_, (mm_ref,) = _ride(riding, refs, n_in, n_o + n_a, 1, (i == 0) & (s == 0),
                                    (i == m // tm - 1) & (s == steps - 1))
        out_refs, acc_refs = outs_[:n_o], outs_[n_o:]
        part = _dot(a_ref[...], b_ref[...] if b.ndim == 2 else b_ref[s], _NT)
        if steps > 1:
            @pl.when(s == 0)
            def _():
                mm_ref[...] = jnp.zeros_like(mm_ref)
            mm_ref[...] += part

        @pl.when(s == steps - 1)
        def _():
            res = fn(mm_ref[...] if steps > 1 else part, *[r[...] for r in row_refs], *[r[...] for r in const_refs])
            for r, v in zip(out_refs, res[:n_o]):
                r[...] = v.astype(r.dtype)
            if n_a:
                @pl.when(i == 0)
                def _():
                    for r in acc_refs:
                        r[...] = jnp.zeros_like(r)
                for r, v in zip(acc_refs, res[n_o:]):
                    r[...] += v

    b_spec = (pl.BlockSpec((k, tn), lambda i, s: (0, s)) if b.ndim == 2
              else pl.BlockSpec(memory_space=pltpu.VMEM))
    in_specs = [pl.BlockSpec((tm, tn), lambda i, s: (i, s)), b_spec]
    in_specs += [pl.BlockSpec((tm, w), functools.partial(lambda i, s, cb: (i, cb), cb=cb)) for (_, w, cb) in rows]
    in_specs += [pl.BlockSpec(c.shape, lambda i, s: (0, 0)) for c in consts]
    args = [a, b] + [r[0] for r in rows] + list(consts)
    out_shape, out_specs = [], []
    for o in outs:
        w, dt = o[0], o[1]
        cb, total = (o[2], o[3]) if len(o) == 4 else (0, w)
        out_shape.append(jax.ShapeDtypeStruct((m, total), dt))
        out_specs.append(pl.BlockSpec((tm, w), functools.partial(lambda i, s, cb: (i, cb), cb=cb)))
    io_alias = {}
    if alias is not None:
        in_specs.append(pl.BlockSpec(memory_space=pl.ANY))
        args.append(alias[0])
        io_alias = {len(args) - 1: alias[1]}
    for (r, w) in accs:
        out_shape.append(jax.ShapeDtypeStruct((r, w), F32))
        out_specs.append(pl.BlockSpec((r, w), lambda i, s: (0, 0)))
    return _riding_call(riding, body, name, (m // tm, steps), in_specs, args, out_specs, out_shape,
                        [pltpu.VMEM((tm, k), F32)], io_alias)


def _mm_tn(name, a, b, tk, tn, col_shards=False, riding=None):
    t, k = a.shape
    n = b.shape[1]
    steps = t // tk

    def body(*refs):
        j, s = pl.program_id(0), pl.program_id(1)
        (a_ref, b_ref), (o_ref,), (acc_ref,) = _ride(riding, refs, 2, 1, 1, (j == 0) & (s == 0),
                                                     (j == n // tn - 1) & (s == steps - 1))

        @pl.when(s == 0)
        def _():
            acc_ref[...] = jnp.zeros_like(acc_ref)

        acc_ref[...] += _dot(a_ref[...], b_ref[...], _TN)

        @pl.when(s == steps - 1)
        def _():
            o_ref[...] = acc_ref[...]

    if col_shards:
        out_spec = pl.BlockSpec((None, k, tn), lambda j, s: (j, 0, 0))
        out_shape = jax.ShapeDtypeStruct((n // tn, k, tn), F32)
    else:
        out_spec = pl.BlockSpec((k, tn), lambda j, s: (0, j))
        out_shape = jax.ShapeDtypeStruct((k, n), F32)
    res = _riding_call(riding, body, name, (n // tn, steps),
                       [pl.BlockSpec((tk, k), lambda j, s: (s, 0)), pl.BlockSpec((tk, tn), lambda j, s: (s, j))],
                       [a, b], [out_spec], [out_shape], [pltpu.VMEM((k, tn), F32)])
    return res[0] if riding is None else res


def _dot01(m01, x):
    m = m01.astype(MXU_DTYPE)
    hi = x.astype(MXU_DTYPE)
    r1 = x - hi.astype(F32)
    mid = r1.astype(MXU_DTYPE)
    lo = (r1 - mid.astype(F32)).astype(MXU_DTYPE)
    dot = lambda v: jnp.dot(m, v, preferred_element_type=F32)
    return dot(hi) + dot(mid) + dot(lo)


def _chunk_rows(x, offset, nck):
    return jnp.concatenate([jnp.broadcast_to(x[c * HG_CHUNK + offset:c * HG_CHUNK + offset + 1, :],
                                             (HG_CHUNK, x.shape[1])) for c in range(nck)], axis=0)


def _hg_block_terms(q, f, lb, tb):
    nck = tb // HG_CHUNK
    sig = _sig(f)
    fv = lb + (1.0 - lb) * sig
    kk = (1.0 - lb) * (1.0 - sig)
    row = lax.broadcasted_iota(jnp.int32, (tb, tb), 0)
    col = lax.broadcasted_iota(jnp.int32, (tb, tb), 1)
    same = jnp.right_shift(row, 6) == jnp.right_shift(col, 6)
    causal, anti = same & (row >= col), same & (row <= col)
    b = _dot01(causal, jnp.log(fv))
    b_mid, b_last = _chunk_rows(b, HG_CHUNK // 2 - 1, nck), _chunk_rows(b, HG_CHUNK - 1, nck)
    e_mid, e_mid_inv = jnp.exp(b - b_mid), jnp.exp(b_mid - b)
    e_b, e_last = jnp.exp(b), jnp.exp(b_last - b)
    dcs = [jnp.exp(b[c * HG_CHUNK + HG_CHUNK - 1:(c + 1) * HG_CHUNK, :]) for c in range(nck)]
    return sig, fv, kk, causal, anti, e_mid, e_mid_inv, e_b, e_last, dcs


def _hgrn2_fwd(proj, hg_lb, hg_norm_g, t_len, tb, riding=None):
    nck = tb // HG_CHUNK
    nb = t_len // tb

    def body(*refs):
        step = pl.program_id(0)
        ((p_ref, lb_ref, gn_ref), (o_ref, act_ref, sp_ref),
         (st_ref, a_s, bm_s, qd_s, kd_s, v_s, sc_s, inc_s)) = _ride(riding, refs, 3, 3, 8, step == 0, step == nb - 1)

        @pl.when(pl.program_id(0) == 0)
        def _():
            st_ref[...] = jnp.zeros_like(st_ref)

        lb = _sig(lb_ref[0:1, :] - lb_ref[1:2, :])
        q = p_ref[:, pl.ds(0, 1024)]
        _, _, kk, causal, _, e_mid, e_mid_inv, e_b, e_last, dcs = _hg_block_terms(q, p_ref[:, pl.ds(1024, 1024)],
                                                                                   lb, tb)
        a_s[...] = _mx(q * e_mid)
        bm_s[...] = _mx(kk * e_mid_inv)
        qd_s[...] = _mx(q * e_b)
        kd_s[...] = _mx(kk * e_last)
        v_s[...] = _mx(p_ref[:, pl.ds(2048, 1024)])
        heads = [pl.ds(h * HG_DIM, HG_DIM) for h in range(HG_HEADS)]
        chunks = [pl.ds(c * HG_CHUNK, HG_CHUNK) for c in range(nck)]
        for h, hs in enumerate(heads):
            sc_s[h] = _mx(jnp.where(causal, _dot(a_s[:, hs], bm_s[:, hs], _NT), 0.0))
        for h, hs in enumerate(heads):
            o_ref[:, hs] = _dot(sc_s[h], v_s[:, hs])
        for h, hs in enumerate(heads):
            for c, r in enumerate(chunks):
                inc_s[h, c] = _dot(v_s[r, hs], kd_s[r, hs], _TN)
        for c in range(nck):
            for h in range(HG_HEADS):
                st = st_ref[h]
                sp_ref[h, c] = st
                st_ref[h] = dcs[c][:, h * HG_DIM:(h + 1) * HG_DIM] * st + inc_s[h, c]
        for c, r in enumerate(chunks):
            for h, hs in enumerate(heads):
                o_ref[r, hs] += _dot(qd_s[r, hs], sp_ref[h, c], _NT)
        for h, hs in enumerate(heads):
            o = o_ref[:, hs]
            rr = lax.rsqrt(jnp.mean(o * o, axis=-1, keepdims=True) + NORM_EPS)
            g = p_ref[:, pl.ds(3072 + h * HG_DIM, HG_DIM)]
            act_ref[:, hs] = (o * rr * gn_ref[:, hs] * (g * _sig(g))).astype(act_ref.dtype)

    return _riding_call(
        riding, body, "hgrn2_fwd", (nb,),
        [pl.BlockSpec((tb, 4096), lambda i: (i, 0)), pl.BlockSpec((2, 1024), lambda i: (0, 0)),
         pl.BlockSpec((1, 1024), lambda i: (0, 0))],
        [proj, hg_lb, hg_norm_g],
        [pl.BlockSpec((tb, 1024), lambda i: (i, 0)), pl.BlockSpec((tb, 1024), lambda i: (i, 0)),
         pl.BlockSpec((HG_HEADS, nck, HG_DIM, HG_DIM), lambda i: (0, i, 0, 0))],
        [jax.ShapeDtypeStruct((t_len, 1024), F32), jax.ShapeDtypeStruct((t_len, 1024), MXU_DTYPE),
         jax.ShapeDtypeStruct((HG_HEADS, t_len // HG_CHUNK, HG_DIM, HG_DIM), F32)],
        [pltpu.VMEM((HG_HEADS, HG_DIM, HG_DIM), F32)] + [pltpu.VMEM((tb, 1024), MXU_DTYPE)] * 5
        + [pltpu.VMEM((HG_HEADS, tb, tb), MXU_DTYPE), pltpu.VMEM((HG_HEADS, nck, HG_DIM, HG_DIM), F32)])


def _hgrn2_bwd(proj, d_o, s_prev, hg_lb, dproj, t_len, tb, riding=None):
    nck = tb // HG_CHUNK
    nb = t_len // tb

    def body(*refs):
        step = pl.program_id(0)
        ((p_ref, do_ref, sp_ref, lb_ref, _), (dp_ref, dlb_ref),
         (ds_ref, acc_ref, a_s, bm_s, qd_s, kd_s, v_s, do_s, da_s, dbm_s, dqd_s, dkd_s, dv_s, ex_s, sc_s, dsc_s,
          up_s)) = _ride(riding, refs, 5, 2, 17, step == 0, step == nb - 1)

        @pl.when(pl.program_id(0) == 0)
        def _():
            ds_ref[...] = jnp.zeros_like(ds_ref)
            acc_ref[...] = jnp.zeros_like(acc_ref)

        lb = _sig(lb_ref[0:1, :] - lb_ref[1:2, :])
        q = p_ref[:, pl.ds(0, 1024)]
        sig, fv, kk, causal, anti, e_mid, e_mid_inv, e_b, e_last, dcs = _hg_block_terms(
            q, p_ref[:, pl.ds(1024, 1024)], lb, tb)
        a, bm, qd, kd = q * e_mid, kk * e_mid_inv, q * e_b, kk * e_last
        a_s[...] = _mx(a)
        bm_s[...] = _mx(bm)
        qd_s[...] = _mx(qd)
        kd_s[...] = _mx(kd)
        v_s[...] = _mx(p_ref[:, pl.ds(2048, 1024)])
        do_s[...] = _mx(do_ref[...])
        heads = [pl.ds(h * HG_DIM, HG_DIM) for h in range(HG_HEADS)]
        chunks = [pl.ds(c * HG_CHUNK, HG_CHUNK) for c in range(nck)]
        for h, hs in enumerate(heads):
            sc_s[h] = _mx(jnp.where(causal, _dot(a_s[:, hs], bm_s[:, hs], _NT), 0.0))
            dsc_s[h] = _mx(jnp.where(causal, _dot(do_s[:, hs], v_s[:, hs], _NT), 0.0))
        for h, hs in enumerate(heads):
            dv_s[:, hs] = _dot(sc_s[h], do_s[:, hs], _TN)
            da_s[:, hs] = _dot(dsc_s[h], bm_s[:, hs])
            dbm_s[:, hs] = _dot(dsc_s[h], a_s[:, hs], _TN)
        for h, hs in enumerate(heads):
            for c, r in enumerate(chunks):
                up_s[h, c] = _dot(do_s[r, hs], qd_s[r, hs], _TN)
                dqd_s[r, hs] = _dot(do_s[r, hs], sp_ref[h, c])
        for c in reversed(range(nck)):
            r = chunks[c]
            for h, hs in enumerate(heads):
                dst = ds_ref[h]
                dc = dcs[c][:, h * HG_DIM:(h + 1) * HG_DIM]
                dv_s[r, hs] += _dot(kd_s[r, hs], dst, _NT)
                dkd_s[r, hs] = _dot(v_s[r, hs], dst)
                ex_s[c:c + 1, hs] = jnp.sum(dst * sp_ref[h, c], axis=0, keepdims=True) * dc
                ds_ref[h] = up_s[h, c] + dc * dst
        da, dbm, dqd, dkd = da_s[...], dbm_s[...], dqd_s[...], dkd_s[...]
        dq = da * e_mid + dqd * e_b
        dk = dbm * e_mid_inv + dkd * e_last
        db = da * a - dbm * bm + dqd * qd - dkd * kd
        dkk = dkd * kd
        extra = jnp.concatenate(
            [jnp.broadcast_to(jnp.sum(dkk[c * HG_CHUNK:(c + 1) * HG_CHUNK], axis=0, keepdims=True)
                              + ex_s[c:c + 1, :], (HG_CHUNK, 1024)) for c in range(nck)], axis=0)
        dlogf = _dot01(anti, db) + extra
        dfv_k = dlogf / fv - dk
        dp_ref[:, pl.ds(0, 1024)] = dq.astype(dp_ref.dtype)
        dp_ref[:, pl.ds(1024, 1024)] = (dfv_k * (1.0 - lb) * sig * (1.0 - sig)).astype(dp_ref.dtype)
        dp_ref[:, pl.ds(2048, 1024)] = dv_s[...].astype(dp_ref.dtype)
        acc_ref[...] += jnp.sum(dfv_k * (1.0 - sig), axis=0, keepdims=True)

        @pl.when(pl.program_id(0) == nb - 1)
        def _():
            g0 = acc_ref[...] * lb * (1.0 - lb)
            dlb_ref[0:1, :] = g0
            dlb_ref[1:2, :] = -g0

    return _riding_call(
        riding, body, "hgrn2_bwd", (nb,),
        [pl.BlockSpec((tb, 3072), lambda i: (nb - 1 - i, 0)),
         pl.BlockSpec((tb, 1024), lambda i: (nb - 1 - i, 0)),
         pl.BlockSpec((HG_HEADS, nck, HG_DIM, HG_DIM), lambda i: (0, nb - 1 - i, 0, 0)),
         pl.BlockSpec((2, 1024), lambda i: (0, 0)),
         pl.BlockSpec(memory_space=pl.ANY)],
        [proj, d_o, s_prev, hg_lb, dproj],
        [pl.BlockSpec((tb, 3072), lambda i: (nb - 1 - i, 0)), pl.BlockSpec((2, 1024), lambda i: (0, 0))],
        [jax.ShapeDtypeStruct((t_len, IN_COLS), dproj.dtype), jax.ShapeDtypeStruct((2, 1024), F32)],
        [pltpu.VMEM((HG_HEADS, HG_DIM, HG_DIM), F32), pltpu.VMEM((1, 1024), F32)]
        + [pltpu.VMEM((tb, 1024), MXU_DTYPE)] * 6 + [pltpu.VMEM((tb, 1024), F32)] * 5
        + [pltpu.VMEM((SUBLANES, 1024), F32)] + [pltpu.VMEM((HG_HEADS, tb, tb), MXU_DTYPE)] * 2
        + [pltpu.VMEM((HG_HEADS, nck, HG_DIM, HG_DIM), F32)], {4: 0})


def _s5_prep_bwd(a_re, a_im, log_dt, b_re_t, b_im_t, dlam, dbbr, dbbi):
    def body(ar_ref, ai_ref, ldt_ref, br_ref, bi_ref, dlam_ref, dbbr_ref, dbbi_ref,
             dar_ref, dai_ref, dldt_ref, dbr_ref, dbi_ref):
        ar, ai = ar_ref[...], ai_ref[...]
        dt = jnp.exp(ldt_ref[...])
        mag = jnp.exp(ar * dt)
        cs, sn = jnp.cos(ai * dt), jnp.sin(ai * dt)
        lr, li = mag * cs, mag * sn
        den = ar * ar + ai * ai
        nr = lr - 1.0
        sr = (nr * ar + li * ai) / den
        si = (li * ar - nr * ai) / den
        br, bi = br_ref[...], bi_ref[...]
        gbr, gbi = dbbr_ref[...], dbbi_ref[...]
        dbr_ref[...] = sr * gbr + si * gbi
        dbi_ref[...] = sr * gbi - si * gbr
        dsr = jnp.sum(gbr * br + gbi * bi, axis=0, keepdims=True)
        dsi = jnp.sum(gbi * br - gbr * bi, axis=0, keepdims=True)
        dnr = (dsr * ar - dsi * ai) / den
        dli = dlam_ref[1:2, :] + (dsr * ai + dsi * ar) / den
        dlr = dlam_ref[0:1, :] + dnr
        dden = -(dsr * sr + dsi * si) / den
        dar = (dsr * nr + dsi * li) / den + dden * 2.0 * ar
        dai = (dsr * li - dsi * nr) / den + dden * 2.0 * ai
        dmag = dlr * cs + dli * sn
        dth = mag * (dli * cs - dlr * sn)
        dar_ref[...] = dar + dmag * mag * dt
        dai_ref[...] = dai + dth * dt
        ddt = (dmag * mag * ar + dth * ai) * dt
        lane = lax.broadcasted_iota(jnp.int32, (S5_LANES, 128), 0) // S5_STATE
        grp = lax.broadcasted_iota(jnp.int32, (S5_LANES, 128), 1)
        dldt_ref[...] = _dot32(jnp.broadcast_to(ddt, (SUBLANES, S5_LANES)), (lane == grp).astype(F32))

    whole = pl.BlockSpec(memory_space=pltpu.VMEM)
    return pl.pallas_call(
        body, name="s5_prep_bwd", in_specs=[whole] * 8, out_specs=[whole] * 5,
        out_shape=[jax.ShapeDtypeStruct((1, S5_LANES), F32), jax.ShapeDtypeStruct((1, S5_LANES), F32),
                   jax.ShapeDtypeStruct((SUBLANES, 128), F32), jax.ShapeDtypeStruct((S5_GROUP, S5_LANES), F32),
                   jax.ShapeDtypeStruct((S5_GROUP, S5_LANES), F32)])(a_re, a_im, log_dt, b_re_t, b_im_t, dlam, dbbr,
                                                                      dbbi)


def _dgelu(x):
    c, a = 0.7978845608028654, 0.044715
    th = jnp.tanh(c * (x + a * x * x * x))
    return 0.5 * (1.0 + th) + 0.5 * x * (1.0 - th * th) * c * (1.0 + 3.0 * a * x * x)


S5_BLOCKS = 4
S5_BW = S5_WIDTH // S5_BLOCKS
S5_BL = S5_LANES // S5_BLOCKS
S5_LANE_BLOCKS = S5_LANES // 128
S5_SCAN_BLOCKS = 4


def _s5_prep(a_re, a_im, log_dt, b_re_t, b_im_t, seg):
    def body(ar_ref, ai_ref, ldt_ref, br_ref, bi_ref,
             rows_f, pfr_ref, pfi_ref, rows_r, prr_ref, pri_ref, bbr_ref, bbi_ref):
        ar, ai = ar_ref[...], ai_ref[...]
        dt = jnp.exp(ldt_ref[...])
        mag = jnp.exp(ar * dt)
        lr, li = mag * jnp.cos(ai * dt), mag * jnp.sin(ai * dt)
        den = ar * ar + ai * ai
        nr = lr - 1.0
        sr = (nr * ar + li * ai) / den
        si = (li * ar - nr * ai) / den
        wide = (SUBLANES, S5_LANES)
        cr, ci = lr, li
        for i in range(seg):
            pfr_ref[i] = jnp.broadcast_to(cr, wide)
            pfi_ref[i] = jnp.broadcast_to(ci, wide)
            prr_ref[seg - 1 - i] = jnp.broadcast_to(cr, wide)
            pri_ref[seg - 1 - i] = jnp.broadcast_to(-ci, wide)
            if i == seg - 1:
                for rows, sign in ((rows_f, 1.0), (rows_r, -1.0)):
                    rows[0:1, :] = lr
                    rows[1:2, :] = sign * li
                    rows[2:3, :] = cr
                    rows[3:4, :] = sign * ci
            cr, ci = cr * lr - ci * li, cr * li + ci * lr
        br, bi = br_ref[...], bi_ref[...]
        bbr_ref[...] = sr * br - si * bi
        bbi_ref[...] = sr * bi + si * br

    whole = pl.BlockSpec(memory_space=pltpu.VMEM)
    tables = [jax.ShapeDtypeStruct((4, S5_LANES), F32)] + [jax.ShapeDtypeStruct((seg, SUBLANES, S5_LANES), F32)] * 2
    bbar = [jax.ShapeDtypeStruct((S5_GROUP, S5_LANES), F32)] * 2
    res = pl.pallas_call(body, name="s5_prep", in_specs=[whole] * 5, out_specs=[whole] * 8,
                         out_shape=tables + tables + bbar)(a_re, a_im, log_dt, b_re_t, b_im_t)
    return res[0:3], res[3:6], res[6], res[7]


def _lanes(j):
    return pl.ds(j * 128, 128)


def _to_segment_order(v, stage_ref, out_ref, seg):
    nbl = v.shape[1] // 128
    for b in range(nbl):
        stage_ref[b] = v[:, b * 128:(b + 1) * 128]

    def body(t, carry):
        rows = pl.ds(pl.multiple_of(t * SUBLANES, SUBLANES), SUBLANES)
        for b in range(nbl):
            out_ref[rows, _lanes(b)] = stage_ref[b, pl.ds(t, SUBLANES, stride=seg), :]
        return carry

    lax.fori_loop(0, seg, body, 0, unroll=True)


def _from_segment_order(v, stage_ref, out_ref, seg):
    nbl = v.shape[1] // 128
    for b in range(nbl):
        stage_ref[b] = v[:, b * 128:(b + 1) * 128]
    for s in range(SUBLANES):
        def body(k, carry, s=s):
            rows = pl.ds(pl.multiple_of(s * seg + k * SUBLANES, SUBLANES), SUBLANES)
            for b in range(nbl):
                out_ref[rows, _lanes(b)] = stage_ref[b, pl.ds(k * SUBLANES * SUBLANES + s, SUBLANES,
                                                              stride=SUBLANES), :]
            return carry

        lax.fori_loop(0, seg // SUBLANES, body, 0, unroll=True)


def _tile_scan(xr_ref, xi_ref, lam_ref, car_ref, cai_ref, cn_r, cn_i, blocks, seg, reverse):
    shape = (SUBLANES, 128)
    lrs = [jnp.broadcast_to(lam_ref[0:1, _lanes(j)], shape) for j in blocks]
    lis = [jnp.broadcast_to(lam_ref[1:2, _lanes(j)], shape) for j in blocks]

    def step(k, carry):
        t = seg - 1 - k if reverse else k
        rows = pl.ds(pl.multiple_of(t * SUBLANES, SUBLANES), SUBLANES)
        out = []
        for n, j in enumerate(blocks):
            cr, ci = carry[2 * n], carry[2 * n + 1]
            nr = lrs[n] * cr - lis[n] * ci + xr_ref[rows, _lanes(j)]
            ni = lrs[n] * ci + lis[n] * cr + xi_ref[rows, _lanes(j)]
            xr_ref[rows, _lanes(j)] = nr
            xi_ref[rows, _lanes(j)] = ni
            out += [nr, ni]
        return tuple(out)

    zero = jnp.zeros(shape, F32)
    fin = lax.fori_loop(0, seg, step, (zero,) * (2 * len(blocks)), unroll=True)
    for n, j in enumerate(blocks):
        ls = _lanes(j)
        fr, fi = fin[2 * n], fin[2 * n + 1]
        sr, si = lam_ref[2:3, ls], lam_ref[3:4, ls]
        pr, pi = car_ref[:, ls], cai_ref[:, ls]
        for s in (reversed(range(SUBLANES)) if reverse else range(SUBLANES)):
            cn_r[s:s + 1, ls] = pr
            cn_i[s:s + 1, ls] = pi
            pr, pi = fr[s:s + 1, :] + sr * pr - si * pi, fi[s:s + 1, :] + sr * pi + si * pr
        car_ref[:, ls] = pr
        cai_ref[:, ls] = pi


def _s5_fwd(proj, lam_rows, p3_re, p3_im, bbr4, bbi4, crt4, cit4, d_row, t_len, tb):
    seg = tb // SUBLANES

    def body(u_ref, lam_ref, p3r_ref, p3i_ref, bbr_ref, bbi_ref, crt_ref, cit_ref, d_ref,
             hr_ref, hi_ref, ypre_ref, ys_ref, car_ref, cai_ref, cn_r, cn_i, stage_ref, us_ref, yseg_ref):
        @pl.when(pl.program_id(0) == 0)
        def _():
            car_ref[...] = jnp.zeros_like(car_ref)
            cai_ref[...] = jnp.zeros_like(cai_ref)

        _to_segment_order(u_ref[...], stage_ref, us_ref, seg)
        u = us_ref[...]
        for i in range(S5_BLOCKS):
            ui = u[:, i * S5_BW:(i + 1) * S5_BW]
            hr_ref[:, pl.ds(i * S5_BL, S5_BL)] = _dot(ui, bbr_ref[i])
            hi_ref[:, pl.ds(i * S5_BL, S5_BL)] = _dot(ui, bbi_ref[i])
        for lc in range(S5_LANE_BLOCKS // S5_SCAN_BLOCKS):
            blocks = range(lc * S5_SCAN_BLOCKS, (lc + 1) * S5_SCAN_BLOCKS)
            _tile_scan(hr_ref, hi_ref, lam_ref, car_ref, cai_ref, cn_r, cn_i, blocks, seg, False)
            crs = [cn_r[:, _lanes(j)] for j in blocks]
            cis = [cn_i[:, _lanes(j)] for j in blocks]

            def fix(t, carry, blocks=blocks, crs=crs, cis=cis):
                rows = pl.ds(pl.multiple_of(t * SUBLANES, SUBLANES), SUBLANES)
                for n, j in enumerate(blocks):
                    pr, pi = p3r_ref[t, :, _lanes(j)], p3i_ref[t, :, _lanes(j)]
                    hr_ref[rows, _lanes(j)] += pr * crs[n] - pi * cis[n]
                    hi_ref[rows, _lanes(j)] += pr * cis[n] + pi * crs[n]
                return carry

            lax.fori_loop(0, seg, fix, 0, unroll=True)
        for i in range(S5_BLOCKS):
            ws = pl.ds(i * S5_BW, S5_BW)
            bl = pl.ds(i * S5_BL, S5_BL)
            yseg_ref[:, ws] = (_dot(hr_ref[:, bl], crt_ref[i]) - _dot(hi_ref[:, bl], cit_ref[i])
                               + d_ref[:, ws] * u[:, i * S5_BW:(i + 1) * S5_BW])
        _from_segment_order(yseg_ref[...], stage_ref, ypre_ref, seg)
        ys_ref[...] = jax.nn.gelu(ypre_ref[...], approximate=True).astype(ys_ref.dtype)

    whole = pl.BlockSpec(memory_space=pltpu.VMEM)
    return pl.pallas_call(
        body, name="s5_fwd", grid=(t_len // tb,),
        in_specs=[pl.BlockSpec((tb, S5_WIDTH), lambda i: (i, 4096 // S5_WIDTH))] + [whole] * 8,
        out_specs=[pl.BlockSpec((tb, S5_LANES), lambda i: (i, 0)), pl.BlockSpec((tb, S5_LANES), lambda i: (i, 0)),
                   pl.BlockSpec((tb, S5_WIDTH), lambda i: (i, 0)), pl.BlockSpec((tb, S5_WIDTH), lambda i: (i, 0))],
        out_shape=[jax.ShapeDtypeStruct((t_len, S5_LANES), F32), jax.ShapeDtypeStruct((t_len, S5_LANES), F32),
                   jax.ShapeDtypeStruct((t_len, S5_WIDTH), F32), jax.ShapeDtypeStruct((t_len, S5_WIDTH), MXU_DTYPE)],
        scratch_shapes=[pltpu.VMEM((1, S5_LANES), F32), pltpu.VMEM((1, S5_LANES), F32),
                        pltpu.VMEM((SUBLANES, S5_LANES), F32), pltpu.VMEM((SUBLANES, S5_LANES), F32),
                        pltpu.VMEM((S5_WIDTH // 128, tb, 128), F32), pltpu.VMEM((tb, S5_WIDTH), F32),
                        pltpu.VMEM((tb, S5_WIDTH), F32)],
        compiler_params=_params("arbitrary"))(proj, lam_rows, p3_re, p3_im, bbr4, bbi4, crt4, cit4, d_row)


def _s5_bwd(dgelu, y_pre, proj, h_re, h_im, lam_rows, p3_re, p3_im, bbr4, bbi4, cr4, ci4, d_row, dproj, t_len, tb):
    seg = tb // SUBLANES
    nb = t_len // tb

    def body(dg_ref, yp_ref, u_ref, hr_ref, hi_ref, lam_ref, p3r_ref, p3i_ref, bbr_ref, bbi_ref, cr_ref, ci_ref,
             d_ref, _, du_ref, dbbr_ref, dbbi_ref, dcr_ref, dci_ref, dd_ref, dlam_ref,
             gr_ref, gi_ref, car_ref, cai_ref, cn_r, cn_i, stage_ref, us_ref, dys_ref, duseg_ref):
        @pl.when(pl.program_id(0) == 0)
        def _():
            for ref in (car_ref, cai_ref, dbbr_ref, dbbi_ref, dcr_ref, dci_ref, dd_ref, dlam_ref):
                ref[...] = jnp.zeros_like(ref)

        _to_segment_order(u_ref[...], stage_ref, us_ref, seg)
        _to_segment_order(dg_ref[...] * _dgelu(yp_ref[...]), stage_ref, dys_ref, seg)
        u, dy = us_ref[...], dys_ref[...]
        for i in range(S5_BLOCKS):
            dyi = dy[:, i * S5_BW:(i + 1) * S5_BW]
            gr_ref[:, pl.ds(i * S5_BL, S5_BL)] = _dot(dyi, cr_ref[i])
            gi_ref[:, pl.ds(i * S5_BL, S5_BL)] = -_dot(dyi, ci_ref[i])
        for lc in range(S5_LANE_BLOCKS // S5_SCAN_BLOCKS):
            blocks = range(lc * S5_SCAN_BLOCKS, (lc + 1) * S5_SCAN_BLOCKS)
            _tile_scan(gr_ref, gi_ref, lam_ref, car_ref, cai_ref, cn_r, cn_i, blocks, seg, True)
            crs = [cn_r[:, _lanes(j)] for j in blocks]
            cis = [cn_i[:, _lanes(j)] for j in blocks]

            def fix(k, carry, blocks=blocks, crs=crs, cis=cis):
                t = seg - 1 - k
                rows = pl.ds(pl.multiple_of(t * SUBLANES, SUBLANES), SUBLANES)
                out = []
                for n, j in enumerate(blocks):
                    nr, ni, slr, sli = carry[4 * n:4 * n + 4]
                    pr, pi = p3r_ref[t, :, _lanes(j)], p3i_ref[t, :, _lanes(j)]
                    g_r = gr_ref[rows, _lanes(j)] + pr * crs[n] - pi * cis[n]
                    g_i = gi_ref[rows, _lanes(j)] + pr * cis[n] + pi * crs[n]
                    gr_ref[rows, _lanes(j)] = g_r
                    gi_ref[rows, _lanes(j)] = g_i
                    hr, hi = hr_ref[rows, _lanes(j)], hi_ref[rows, _lanes(j)]
                    out += [g_r, g_i, slr + nr * hr + ni * hi, sli + ni * hr - nr * hi]
                return tuple(out)

            zero = jnp.zeros((SUBLANES, 128), F32)
            init = []
            for n in range(len(blocks)):
                init += [crs[n], cis[n], zero, zero]
            fin = lax.fori_loop(0, seg, fix, tuple(init), unroll=True)
            for n, j in enumerate(blocks):
                dlam_ref[0:1, _lanes(j)] += jnp.sum(fin[4 * n + 2], axis=0, keepdims=True)
                dlam_ref[1:2, _lanes(j)] += jnp.sum(fin[4 * n + 3], axis=0, keepdims=True)
        for i in range(S5_BLOCKS):
            ws = pl.ds(i * S5_BW, S5_BW)
            bl = pl.ds(i * S5_BL, S5_BL)
            ui, dyi = u[:, i * S5_BW:(i + 1) * S5_BW], dy[:, i * S5_BW:(i + 1) * S5_BW]
            gr, gi = gr_ref[:, bl], gi_ref[:, bl]
            duseg_ref[:, ws] = _dot(gr, bbr_ref[i], _NT) + _dot(gi, bbi_ref[i], _NT) + d_ref[:, ws] * dyi
            dbbr_ref[i] += _dot(ui, gr, _TN)
            dbbi_ref[i] += _dot(ui, gi, _TN)
            dcr_ref[i] += _dot(hr_ref[:, bl], dyi, _TN)
            dci_ref[i] -= _dot(hi_ref[:, bl], dyi, _TN)
        dd_ref[...] += jnp.sum(dy * u, axis=0, keepdims=True)
        _from_segment_order(duseg_ref[...], stage_ref, duseg_ref, seg)
        du_ref[...] = duseg_ref[...].astype(du_ref.dtype)

    whole = pl.BlockSpec(memory_space=pltpu.VMEM)
    rev = lambda i: (nb - 1 - i, 0)
    const3 = lambda i: (0, 0, 0)
    return pl.pallas_call(
        body, name="s5_bwd", grid=(nb,),
        in_specs=[pl.BlockSpec((tb, S5_WIDTH), rev), pl.BlockSpec((tb, S5_WIDTH), rev),
                  pl.BlockSpec((tb, S5_WIDTH), lambda i: (nb - 1 - i, 4096 // S5_WIDTH)),
                  pl.BlockSpec((tb, S5_LANES), rev), pl.BlockSpec((tb, S5_LANES), rev)] + [whole] * 8
                 + [pl.BlockSpec(memory_space=pl.ANY)],
        out_specs=[pl.BlockSpec((tb, S5_WIDTH), lambda i: (nb - 1 - i, 4096 // S5_WIDTH)),
                   pl.BlockSpec((S5_BLOCKS, S5_BW, S5_BL), const3), pl.BlockSpec((S5_BLOCKS, S5_BW, S5_BL), const3),
                   pl.BlockSpec((S5_BLOCKS, S5_BL, S5_BW), const3), pl.BlockSpec((S5_BLOCKS, S5_BL, S5_BW), const3),
                   pl.BlockSpec((1, S5_WIDTH), lambda i: (0, 0)), pl.BlockSpec((2, S5_LANES), lambda i: (0, 0))],
        out_shape=[jax.ShapeDtypeStruct((t_len, IN_COLS), dproj.dtype),
                   jax.ShapeDtypeStruct((S5_BLOCKS, S5_BW, S5_BL), F32),
                   jax.ShapeDtypeStruct((S5_BLOCKS, S5_BW, S5_BL), F32),
                   jax.ShapeDtypeStruct((S5_BLOCKS, S5_BL, S5_BW), F32),
                   jax.ShapeDtypeStruct((S5_BLOCKS, S5_BL, S5_BW), F32),
                   jax.ShapeDtypeStruct((1, S5_WIDTH), F32), jax.ShapeDtypeStruct((2, S5_LANES), F32)],
        scratch_shapes=[pltpu.VMEM((tb, S5_LANES), F32), pltpu.VMEM((tb, S5_LANES), F32),
                        pltpu.VMEM((1, S5_LANES), F32), pltpu.VMEM((1, S5_LANES), F32),
                        pltpu.VMEM((SUBLANES, S5_LANES), F32), pltpu.VMEM((SUBLANES, S5_LANES), F32),
                        pltpu.VMEM((S5_WIDTH // 128, tb, 128), F32), pltpu.VMEM((tb, S5_WIDTH), F32),
                        pltpu.VMEM((tb, S5_WIDTH), F32), pltpu.VMEM((tb, S5_WIDTH), F32)],
        input_output_aliases={13: 0},
        compiler_params=_params("arbitrary"))(dgelu, y_pre, proj, h_re, h_im, lam_rows, p3_re, p3_im, bbr4, bbi4,
                                              cr4, ci4, d_row, dproj)


def _block_diag(per_group):
    g8 = S5_GROUPS // S5_BLOCKS
    eye = jnp.eye(g8, dtype=bool)[None, :, None, :, None]
    dense = jnp.where(eye, per_group.reshape(S5_BLOCKS, g8, S5_GROUP, 1, S5_STATE), 0.0)
    return dense.reshape(S5_BLOCKS, S5_BW, S5_BL)


def _diag_blocks(dense):
    g8 = S5_GROUPS // S5_BLOCKS
    ar = jnp.arange(g8)
    d5 = dense.reshape(S5_BLOCKS, g8, S5_GROUP, g8, S5_STATE)
    return d5[:, ar, :, ar, :].transpose(1, 0, 2, 3).reshape(S5_GROUPS, S5_GROUP, S5_STATE)


def _hg_gate_bwd(da, o, g, gn):
    dos, dgs, dgns = [], [], []
    for h in range(HG_HEADS):
        sl = slice(h * HG_DIM, (h + 1) * HG_DIM)
        oh, gh, dah, gnh = o[:, sl], g[:, sl], da[:, sl], gn[:, sl]
        rr = lax.rsqrt(jnp.mean(oh * oh, axis=-1, keepdims=True) + NORM_EPS)
        sg = _sig(gh)
        dgs.append(dah * (oh * rr * gnh) * _dsilu(gh, sg))
        don = dah * (gh * sg)
        t = don * gnh
        dos.append(rr * t - oh * (rr * rr * rr) * jnp.mean(t * oh, axis=-1, keepdims=True))
        dgns.append(jnp.sum(don * oh * rr, axis=0, keepdims=True))
    return jnp.concatenate(dos, axis=1), jnp.concatenate(dgs, axis=1), jnp.concatenate(dgns, axis=1)


MIX_BWD_COLS = ((3072, 1024), (4608, 512), (5120, 1024), (6144, 1024))


def _mix_bwd(dgl, h1, dh2, act_hg, ys2, ys_gelu, proj, o_hg, g2, ghn, b_glu, w, t_len, tm):
    nb = t_len // tm

    row_ops = ((1024, 0), (1024, 0), (1024, 0), (1024, 0), (512, 0), (512, 0), (1024, 3), (512, 4608 // 512),
               (1024, 5), (1024, 6), (1024, 0))
    row_args = (dgl, h1, dh2, act_hg, ys2, ys_gelu, proj, proj, proj, proj, o_hg)
    n_r = len(row_ops)

    def body(*refs):
        srcs = refs[:n_r]
        (g2_ref, gn_ref, bglu_ref, wg_ref, wo_ref, ws5_ref, whg_ref, wglu_ref,
         dh1_ref, dyh_ref, dys_ref, dglu_ref, dgelu_ref, do_ref, dg2_ref, dbglu_ref, dgn_ref, dproj_ref,
         st0, st1, st2, st3, sems) = refs[n_r:n_r + 23]
        slots, in_sems = refs[n_r + 23:2 * n_r + 23], refs[2 * n_r + 23]
        i = pl.program_id(0)
        stages = (st0, st1, st2, st3)

        def fetch(step):
            first = step * tm if isinstance(step, int) else pl.multiple_of(step * tm, tm)
            return [pltpu.make_async_copy(src.at[pl.ds(first, tm), pl.ds(cb * wd, wd)], dst.at[step % RING],
                                          in_sems.at[k, step % RING])
                    for k, (src, dst, (wd, cb)) in enumerate(zip(srcs, slots, row_ops))]

        @pl.when(i == 0)
        def _():
            for step in range(min(RING - 1, nb)):
                for cp in fetch(step):
                    cp.start()

        @pl.when(i + (RING - 1) < nb)
        def _():
            for cp in fetch(i + (RING - 1)):
                cp.start()

        for cp in fetch(i):
            cp.wait()
        (dgl_ref, h1_ref, dh2_ref, act_ref, ys2_ref, ysg_ref, ghg_ref, z_ref, gh_ref, gs_ref,
         o_ref) = [dst.at[i % RING] for dst in slots]

        def writes(step):
            rows = pl.ds(pl.multiple_of(step * tm, tm), tm)
            return [pltpu.make_async_copy(st, dproj_ref.at[rows, pl.ds(c0, wd)], sems.at[k])
                    for k, (st, (c0, wd)) in enumerate(zip(stages, MIX_BWD_COLS))]

        @pl.when(i > 0)
        def _():
            for cp in writes(i - 1):
                cp.wait()

        @pl.when(i == 0)
        def _():
            for ref in (dg2_ref, dbglu_ref, dgn_ref):
                ref[...] = jnp.zeros_like(ref)

        dx, dg2 = _rms_bwd(_dot(dgl_ref[...], wg_ref[...], _NT), h1_ref[...], g2_ref[...])
        dh1 = dh2_ref[...] + dx
        dh1_ref[...] = dh1
        dg2_ref[...] += dg2
        dm = _dot(dh1, wo_ref[...], _NT)
        sh, ss = _sig(gh_ref[...]), _sig(gs_ref[...])
        dyh, dys = _mx(dm * sh), _mx(dm * ss)
        dyh_ref[...] = dyh
        dys_ref[...] = dys
        st2[...] = (dm * _dot(act_ref[...], whg_ref[...]) * sh * (1.0 - sh)).astype(st2.dtype)
        st3[...] = (dm * _dot(ys2_ref[...], ws5_ref[...]) * ss * (1.0 - ss)).astype(st3.dtype)
        dys2 = _dot(dys, ws5_ref[...], _NT)
        gl_, z = _dot(ysg_ref[...], wglu_ref[...]) + bglu_ref[...], z_ref[...]
        a, b = gl_[:, :S5_WIDTH], gl_[:, S5_WIDTH:]
        sb, sz = _sig(b), _sig(z)
        silu = z * sz
        dglu = jnp.concatenate([dys2 * sb * silu, dys2 * a * silu * sb * (1.0 - sb)], axis=1)
        st1[...] = (dys2 * a * sb * _dsilu(z, sz)).astype(st1.dtype)
        dbglu_ref[...] += jnp.sum(dglu, axis=0, keepdims=True)
        dglu_ref[...] = _mx(dglu)
        dgelu_ref[...] = _dot(dglu, wglu_ref[...], _NT)
        d_o, dg, dgn = _hg_gate_bwd(_dot(dyh, whg_ref[...], _NT), o_ref[...], ghg_ref[...], gn_ref[...])
        do_ref[...] = d_o.astype(do_ref.dtype)
        st0[...] = dg.astype(st0.dtype)
        dgn_ref[...] += dgn
        for cp in writes(i):
            cp.start()

        @pl.when(i == nb - 1)
        def _():
            for cp in writes(i):
                cp.wait()

    tile = lambda wd, cb=0: pl.BlockSpec((tm, wd), functools.partial(lambda i, cb: (i, cb), cb=cb))
    row = lambda wd: pl.BlockSpec((1, wd), lambda i: (0, 0))
    whole = pl.BlockSpec(memory_space=pltpu.VMEM)
    return pl.pallas_call(
        body, name="mix_bwd", grid=(nb,),
        in_specs=[_HBM] * n_r + [row(1024), row(1024), row(1024)] + [whole] * 5,
        out_specs=[tile(1024), tile(1024), tile(1024), tile(1024), tile(512), tile(1024), row(1024), row(1024),
                   row(1024), _HBM],
        out_shape=[jax.ShapeDtypeStruct((t_len, 1024), F32), jax.ShapeDtypeStruct((t_len, 1024), MXU_DTYPE),
                   jax.ShapeDtypeStruct((t_len, 1024), MXU_DTYPE), jax.ShapeDtypeStruct((t_len, 1024), MXU_DTYPE),
                   jax.ShapeDtypeStruct((t_len, 512), F32), jax.ShapeDtypeStruct((t_len, 1024), MXU_DTYPE),
                   jax.ShapeDtypeStruct((1, 1024), F32), jax.ShapeDtypeStruct((1, 1024), F32),
                   jax.ShapeDtypeStruct((1, 1024), F32), jax.ShapeDtypeStruct((t_len, IN_COLS), MXU_DTYPE)],
        scratch_shapes=[pltpu.VMEM((tm, wd), MXU_DTYPE) for _, wd in MIX_BWD_COLS] + [pltpu.SemaphoreType.DMA((4,))]
                       + [pltpu.VMEM((RING, tm, wd), a.dtype) for a, (wd, _) in zip(row_args, row_ops)]
                       + [pltpu.SemaphoreType.DMA((n_r, RING))],
        compiler_params=_params("arbitrary"))(*row_args, g2, ghn, b_glu, w["w_ple_gate"], w["w_out"], w["w_o_s5"],
                                              w["w_o_hg"], w["w_glu"])


def _local_step(x, p, target, w, sm, comm=None):
    t_len = x.shape[0]
    tm = min(256, t_len)
    tmm = min(512, t_len)
    tm_in = min(1024, t_len)
    tk = min(2048, t_len)
    tb_hg = min(256, t_len)
    tb_s5 = min(512, t_len)
    g1, g2, g3, ghn = sm["norm_g"], sm["ple_norm_g"], sm["final_norm_g"].reshape(1, D_MODEL), sm["hg_norm_g"]

    def rms_in(xv, g):
        return xv * lax.rsqrt(jnp.mean(xv * xv, axis=-1, keepdims=True) + NORM_EPS) * g

    in_shard = IN_COLS // N_CHIPS
    if comm is None:
        w_in = w["w_in"]
        proj, u = _mm_nn("mm_in", x, w_in, tm_in, in_shard, prologue=rms_in, consts=[g1])
    else:
        proj, u, w_in = comm.input_projection(x, g1, rms_in, tm_in)

    lanes = lambda a: a.reshape(1, S5_LANES)
    a_re, a_im = lanes(sm["s5_a_re"]), lanes(sm["s5_a_im"])
    ldt = lanes(jnp.broadcast_to(sm["s5_log_dt"].reshape(S5_GROUPS, 1), (S5_GROUPS, S5_STATE)))
    to_t = lambda b: b.reshape(S5_GROUPS, S5_STATE, S5_GROUP).transpose(2, 0, 1).reshape(S5_GROUP, S5_LANES)
    b_re_t, b_im_t = to_t(sm["s5_b_re"]), to_t(sm["s5_b_im"])
    scan_fwd, scan_rev, bbr_t, bbi_t = _s5_prep(a_re, a_im, ldt, b_re_t, b_im_t, tb_s5 // SUBLANES)
    from_t = lambda b: b.reshape(S5_GROUP, S5_GROUPS, S5_STATE).transpose(1, 0, 2)
    bbr_bd = _block_diag(from_t(bbr_t)).astype(MXU_DTYPE)
    bbi_bd = _block_diag(from_t(bbi_t)).astype(MXU_DTYPE)
    cr_bd = _block_diag(sm["s5_c_re"].reshape(S5_GROUPS, S5_GROUP, S5_STATE)).astype(MXU_DTYPE)
    ci_bd = _block_diag(sm["s5_c_im"].reshape(S5_GROUPS, S5_GROUP, S5_STATE)).astype(MXU_DTYPE)
    d_row = sm["s5_d"].reshape(1, S5_WIDTH)
    if comm is None:
        o_hg, act_hg, s_prev = _hgrn2_fwd(proj, sm["hg_lb"], ghn, t_len, tb_hg)
    else:
        o_hg, act_hg, s_prev, landed = _hgrn2_fwd(proj, sm["hg_lb"], ghn, t_len, tb_hg, riding=comm.gather_rest())
        w = comm.rest_weights(landed)
    h_re, h_im, y_pre, ys_gelu = _s5_fwd(proj, *scan_fwd, bbr_bd, bbi_bd,
                                          cr_bd.transpose(0, 2, 1), ci_bd.transpose(0, 2, 1), d_row, t_len, tb_s5)
    def mix_f(act, ysg, z, gh, gs, xv, w_glu, b_glu, w_o_hg, w_o_s5, w_out):
        yh = _dot(act, w_o_hg)
        gl_ = _dot(ysg, w_glu) + b_glu
        a, b = gl_[:, :S5_WIDTH], gl_[:, S5_WIDTH:]
        ys2_ = (a * _sig(b) * (z * _sig(z))).astype(MXU_DTYPE)
        ys = _dot(ys2_, w_o_s5)
        mg = (_sig(gh) * yh + _sig(gs) * ys).astype(MXU_DTYPE)
        return (ys2_, mg, xv + _dot(mg, w_out))

    ys2, merged, h1 = _rowwise(
        "mix_out", mix_f, t_len, tmm,
        [(act_hg, 1024, 0), (ys_gelu, 512, 0), (proj, 512, 4608 // 512), (proj, 1024, 5), (proj, 1024, 6),
         (x, 1024, 0)], [w["w_glu"], sm["b_glu"], w["w_o_hg"], w["w_o_s5"], w["w_out"]],
        [(512, MXU_DTYPE), (1024, MXU_DTYPE), (1024, F32)], ring=True)

    def head_f(h1v, pv, tgt, g_ple, g, w_ple, w_gate):
        r2 = lax.rsqrt(jnp.mean(h1v * h1v, axis=-1, keepdims=True) + NORM_EPS)
        n2_ = (h1v * r2 * g_ple).astype(MXU_DTYPE)
        glv, pev = _dot(n2_, w_gate), _dot(pv, w_ple)
        gate = _sig(glv)
        h2 = h1v + pev * gate
        r = lax.rsqrt(jnp.mean(h2 * h2, axis=-1, keepdims=True) + NORM_EPS)
        e = h2 * r * g - tgt
        loss = 0.5 * jnp.sum(jnp.mean(e * e, axis=-1, keepdims=True), axis=0, keepdims=True)
        dy = e * (1.0 / D_MODEL)
        dg = jnp.sum(dy * h2 * r, axis=0, keepdims=True)
        t = dy * g
        dh2 = r * t - h2 * (r * r * r) * jnp.mean(t * h2, axis=-1, keepdims=True)
        dpe, dgl_ = _mx(dh2 * gate), _mx(dh2 * pev * gate * (1.0 - gate))
        return (dh2, dgl_, jnp.broadcast_to(loss, (1, 128)), dg, _dot(pv, dpe, _TN), _dot(n2_, dgl_, _TN))

    gb = {}
    dh2, dgl, loss_row, d_g3, gb["w_ple"], gb["w_ple_gate"] = _rowwise(
        "ple_loss_head", head_f, t_len, tmm, [(h1, 1024, 0), (p, 256, 0), (target, 1024, 0)],
        [g2, g3, w["w_ple"], w["w_ple_gate"]], [(1024, F32), (1024, MXU_DTYPE)],
        accs=[(1, 128), (1, 1024), (256, 1024), (1024, 1024)], ring=True)

    dh1, dy_hg, dy_s5, dglu, dgelu, d_o, d_g2, d_bglu, d_ghn, dproj = _mix_bwd(
        dgl, h1, dh2, act_hg, ys2, ys_gelu, proj, o_hg, g2, ghn, sm["b_glu"], w, t_len, tm)
    gb["w_out"] = _mm_tn("mm_d_w_out", merged, dh1, tk, 1024)
    gb["w_o_s5"] = _mm_tn("mm_d_w_o_s5", ys2, dy_s5, tk, 1024)
    gb["w_glu"] = _mm_tn("mm_d_w_glu", ys_gelu, dglu, tk, 1024)
    dproj, d_bbr, d_bbi, d_crt, d_cit, d_d, d_lam = _s5_bwd(dgelu, y_pre, proj, h_re, h_im,
                                                            *scan_rev, bbr_bd, bbi_bd, cr_bd,
                                                            ci_bd, d_row, dproj, t_len, tb_s5)
    to_t3 = lambda b: b.transpose(1, 0, 2).reshape(S5_GROUP, S5_LANES)
    d_are, d_aim, d_ldt, d_br_t, d_bi_t = _s5_prep_bwd(a_re, a_im, ldt, b_re_t, b_im_t, d_lam,
                                                       to_t3(_diag_blocks(d_bbr)), to_t3(_diag_blocks(d_bbi)))
    gb["w_o_hg"] = _mm_tn("mm_d_w_o_hg", act_hg, dy_hg, tk, 1024)
    if comm is None:
        dproj, d_lb = _hgrn2_bwd(proj, d_o, s_prev, sm["hg_lb"], dproj, t_len, tb_hg)
    else:
        rest_grads = _pack_rest_full(gb)
        dproj, d_lb, rest_theirs = _hgrn2_bwd(proj, d_o, s_prev, sm["hg_lb"], dproj, t_len, tb_hg,
                                               riding=comm.swap(rest_grads))

    def in_b(duv, xv, dh, g):
        dx, dg = _rms_bwd(duv, xv, g)
        return (dh + dx, dg)

    in_args = ("mm_d_u_rms_in_bwd", dproj, w_in, tm_in, in_shard, in_b, [(x, 1024, 0), (dh1, 1024, 0)], [g1],
               [(1024, F32)])
    if comm is None:
        gb["w_in"] = _mm_tn("mm_d_w_in", u, dproj, tk, in_shard, col_shards=True)
        grad_x, d_g1 = _mm_nt_then(*in_args, accs=[(1, 1024)])
    else:
        gb["w_in"], landed = _mm_tn("mm_d_w_in", u, dproj, tk, in_shard, col_shards=True,
                                    riding=comm.scatter("rest", rest_grads, rest_theirs))
        comm.landed["rest"] = landed
        grad_x, d_g1, landed = _mm_nt_then(*in_args, accs=[(1, 1024)], riding=comm.scatter(
            "in", gb["w_in"].reshape(N_CHIPS, 2, D_MODEL // 2, in_shard)))
        comm.landed["in"] = landed

    back_t = lambda b: b.reshape(S5_GROUP, S5_GROUPS, S5_STATE).transpose(1, 2, 0).reshape(1, S5_GROUPS, S5_STATE,
                                                                                           S5_GROUP)
    gs = {
        "norm_g": d_g1, "hg_lb": d_lb, "hg_norm_g": d_ghn,
        "s5_a_re": d_are.reshape(1, S5_GROUPS, S5_STATE), "s5_a_im": d_aim.reshape(1, S5_GROUPS, S5_STATE),
        "s5_log_dt": d_ldt[0:1, :S5_GROUPS],
        "s5_b_re": back_t(d_br_t), "s5_b_im": back_t(d_bi_t),
        "s5_c_re": _diag_blocks(d_crt.transpose(0, 2, 1)).reshape(1, S5_GROUPS, S5_GROUP, S5_STATE),
        "s5_c_im": _diag_blocks(d_cit.transpose(0, 2, 1)).reshape(1, S5_GROUPS, S5_GROUP, S5_STATE),
        "s5_d": d_d.reshape(1, S5_GROUPS, S5_GROUP), "b_glu": d_bglu, "ple_norm_g": d_g2,
        "final_norm_g": d_g3.reshape(D_MODEL),
    }
    return loss_row, grad_x, gb, gs


def _shard_shape(name):
    r, c = BIG_SHAPE[name]
    return (r, c // N_CHIPS) if name in BIG_COL_SHARDED else (r // N_CHIPS, c)


def _pack_small(parts, last):
    flat = jnp.concatenate([parts[n].reshape(-1) for n in SMALL] + [last.reshape(-1)])
    return jnp.pad(flat, (0, SMALL_ROWS * PACK_W - flat.shape[0])).reshape(SMALL_ROWS, PACK_W)


def _unpack_small(packed):
    flat, out, off = packed.reshape(-1), {}, 0
    for n in SMALL:
        size = 1
        for d in SMALL_SHAPE[n]:
            size *= d
        out[n] = flat[off:off + size].reshape(SMALL_SHAPE[n])
        off += size
    return out, flat[off]


def _place():
    x, y, c = lax.axis_index("x"), lax.axis_index("y"), lax.axis_index("c")
    return x, y, c, [(1 - x, y), (x, 1 - y), (1 - x, 1 - y)]


def _remote(src, dst, send_sems, recv_sems, k, to):
    return pltpu.make_async_remote_copy(src_ref=src, dst_ref=dst, send_sem=send_sems.at[k], recv_sem=recv_sems.at[k],
                                        device_id=to, device_id_type=MESH)


REST = tuple(n for n in BIG if n != "w_in")
REST_ROWS = sum(BIG_SHAPE[n][0] * BIG_SHAPE[n][1] for n in REST) // (N_CHIPS * PACK_W)
IN_SHARD = IN_COLS // N_CHIPS
IN_TILE, REST_TILE = 256, 272


def _pack_rest(parts):
    return jnp.concatenate([parts[n].reshape(-1, PACK_W) for n in REST], axis=0)


def _unpack_rest(packed):
    out, off = {}, 0
    for n in REST:
        r, c = _shard_shape(n)
        rows = r * c // PACK_W
        out[n] = packed[off:off + rows].reshape(1, r, c)
        off += rows
    return out


def _unpack_rest_full(gathered):
    out, off = {}, 0
    for n in REST:
        r, c = _shard_shape(n)
        rows = r * c // PACK_W
        sh = gathered[:, off:off + rows].reshape(N_CHIPS, r, c)
        out[n] = sh.transpose(1, 0, 2).reshape(BIG_SHAPE[n]) if n in BIG_COL_SHARDED else sh.reshape(BIG_SHAPE[n])
        off += rows
    return out


def _pack_rest_full(full):
    parts = []
    for n in REST:
        r, c = _shard_shape(n)
        g = full[n]
        sh = g.reshape(BIG_SHAPE[n][0], N_CHIPS, c).transpose(1, 0, 2) if n in BIG_COL_SHARDED else g
        parts.append(sh.reshape(N_CHIPS, r * c // PACK_W, PACK_W))
    return jnp.concatenate(parts, axis=1).reshape(N_CHIPS, 2, REST_ROWS // 2, PACK_W)


def _swap_halves(pgs, name="exchange_halves"):
    n = len(pgs)

    def body(*refs):
        pg_refs, out_refs, (send_sems, recv_sems) = refs[:n], refs[n:2 * n], refs[2 * n:]
        x, y, c, _ = _place()
        cps = [_remote(pg_ref.at[j, 1 - c], out_ref.at[j], send_sems, recv_sems, N_CHIPS * g + j, (x, y, 1 - c))
               for g, (pg_ref, out_ref) in enumerate(zip(pg_refs, out_refs)) for j in range(N_CHIPS)]
        for cp in cps:
            cp.start()
        for cp in cps:
            cp.wait()

    return pl.pallas_call(
        body, name=name, in_specs=[_HBM] * n, out_specs=[_HBM] * n,
        out_shape=[jax.ShapeDtypeStruct((N_CHIPS,) + pg.shape[2:], pg.dtype) for pg in pgs],
        scratch_shapes=[pltpu.SemaphoreType.DMA((N_CHIPS * n,)), pltpu.SemaphoreType.DMA((N_CHIPS * n,))])(*pgs)


def _share_halves(gs):
    n = len(gs)

    def body(*refs):
        g_refs, out_refs, (send_sems, recv_sems) = refs[:n], refs[n:2 * n], refs[2 * n:]
        x, y, c, _ = _place()
        cps = [_remote(g_ref, out_ref.at[c], send_sems, recv_sems, g, (x, y, 1 - c))
               for g, (g_ref, out_ref) in enumerate(zip(g_refs, out_refs))]
        for cp in cps:
            cp.start()
        for g, (g_ref, out_ref) in enumerate(zip(g_refs, out_refs)):
            _remote(g_ref, out_ref.at[1 - c], send_sems, recv_sems, g, (x, y, 1 - c)).wait_recv()
        for cp in cps:
            cp.wait_send()

    return pl.pallas_call(
        body, name="share_half", in_specs=[_HBM] * n, out_specs=[_HBM] * n,
        out_shape=[jax.ShapeDtypeStruct((2,) + g.shape, g.dtype) for g in gs],
        scratch_shapes=[pltpu.SemaphoreType.DMA((n,)), pltpu.SemaphoreType.DMA((n,))])(*gs)


def _pair_sum(name, pg, theirs, c, tile):
    _, _, rows, width = pg.shape

    def body(c_ref, a_ref, b_ref, o_ref):
        o_ref[...] = (a_ref[...] + b_ref[...]).astype(o_ref.dtype)

    return pl.pallas_call(
        body, name=name,
        grid_spec=pltpu.PrefetchScalarGridSpec(
            num_scalar_prefetch=1, grid=(N_CHIPS, rows // tile),
            in_specs=[pl.BlockSpec((None, None, tile, width), lambda j, i, c_ref: (j, c_ref[0], i, 0)),
                      pl.BlockSpec((None, tile, width), lambda j, i, c_ref: (j, i, 0))],
            out_specs=pl.BlockSpec((None, tile, width), lambda j, i, c_ref: (j, i, 0))),
        out_shape=jax.ShapeDtypeStruct((N_CHIPS, rows, width), WIRE_DTYPE),
        compiler_params=_params("arbitrary", "arbitrary"))(c.reshape(1), pg, theirs)


def _chip_sum(name, ps, others, k, tile):
    _, rows, width = ps.shape

    def body(k_ref, a_ref, b_ref, o_ref):
        o_ref[...] = ((a_ref[...].astype(F32) + b_ref[0].astype(F32)) + b_ref[1].astype(F32)) + b_ref[2].astype(F32)

    return pl.pallas_call(
        body, name=name,
        grid_spec=pltpu.PrefetchScalarGridSpec(
            num_scalar_prefetch=1, grid=(rows // tile,),
            in_specs=[pl.BlockSpec((None, tile, width), lambda i, k_ref: (k_ref[0], i, 0)),
                      pl.BlockSpec((3, tile, width), lambda i, k_ref: (0, i, 0))],
            out_specs=pl.BlockSpec((tile, width), lambda i, k_ref: (i, 0))),
        out_shape=jax.ShapeDtypeStruct((rows, width), F32),
        compiler_params=_params("arbitrary"))(k.reshape(1), ps, others)


def _mm_in_gathering(x, g1, prologue, in_wire, chip, tm):
    m, k = x.shape
    half, ns = in_wire.shape[1:]
    nrow = m // tm

    def flip(j):
        return jnp.where(j == 1, 2, jnp.where(j == 2, 1, j))

    def body(k_ref, x_ref, g_ref, wire_ref, proj_ref, u_ref, all_ref, kept, b_ref, load_sems, send_sems, recv_sems):
        j, i = pl.program_id(0), pl.program_id(1)
        px, py, c, chips = _place()
        sibling = (px, py, 1 - c)

        def over_ici(r, chip_slot):
            cx, cy = chips[r]
            return _remote(wire_ref.at[c], all_ref.at[chip_slot, c], send_sems, recv_sems, r, (cx, cy, c))

        def to_sibling(r, half_slot):
            cx, cy = chips[r]
            return _remote(all_ref.at[2 * cx + cy, c], all_ref.at[2 * cx + cy, half_slot], send_sems, recv_sems,
                           3 + r, sibling)

        def loads(src, slot):
            return [pltpu.make_async_copy(src.at[h], b_ref.at[slot, pl.ds(h * half, half)], load_sems.at[h])
                    for h in range(2)]

        def shard(r):
            cx, cy = chips[r]
            over_ici(r, 2 * cx + cy).wait_recv()
            if r == 0:
                over_ici(2, 2 * px + py).start()
            to_sibling(r, c).start()
            to_sibling(r, 1 - c).wait_recv()
            return all_ref.at[2 * cx + cy]

        @pl.when((j == 0) & (i == 0))
        def _():
            for r in range(2):
                over_ici(r, 2 * px + py).start()
            for cp in loads(wire_ref, 0):
                cp.start()
            for cp in loads(wire_ref, 0):
                cp.wait()

        @pl.when((j == 1) & (i == 0))
        def _():
            cps = loads(shard(0), 1)
            for cp in cps:
                cp.start()
            for cp in cps:
                cp.wait()

        for nxt in (2, 3):
            @pl.when((j == nxt - 1) & (i == nrow // 2))
            def _(nxt=nxt):
                for cp in loads(shard(nxt - 1), nxt % 2):
                    cp.start()

            @pl.when((j == nxt) & (i == 0))
            def _(nxt=nxt):
                for cp in loads(wire_ref, nxt % 2):
                    cp.wait()

        rows = pl.ds(pl.multiple_of(i * tm, tm), tm)

        @pl.when(j == 0)
        def _():
            tile = _mx(prologue(x_ref[...], g_ref[...]))
            kept[rows, :] = tile
            u_ref[...] = tile

        proj_ref[...] = _dot(kept[rows, :], b_ref[lax.rem(j, 2)])

        @pl.when((j == N_CHIPS - 1) & (i == nrow - 1))
        def _():
            for r in range(3):
                over_ici(r, 2 * px + py).wait_send()
                to_sibling(r, c).wait_send()

    once = lambda j, i, k_ref: (jnp.where(j == 0, i, nrow - 1), 0)
    return pl.pallas_call(
        body, name="mm_in",
        grid_spec=pltpu.PrefetchScalarGridSpec(
            num_scalar_prefetch=1, grid=(N_CHIPS, nrow),
            in_specs=[pl.BlockSpec((tm, k), once), pl.BlockSpec(g1.shape, lambda j, i, k_ref: (0, 0)), _HBM],
            out_specs=[pl.BlockSpec((tm, ns), lambda j, i, k_ref: (i, jnp.bitwise_xor(k_ref[0], flip(j)))),
                       pl.BlockSpec((tm, k), once), _HBM],
            scratch_shapes=[pltpu.VMEM((m, k), MXU_DTYPE), pltpu.VMEM((2, 2 * half, ns), in_wire.dtype),
                            pltpu.SemaphoreType.DMA((2,)), pltpu.SemaphoreType.DMA((6,)),
                            pltpu.SemaphoreType.DMA((6,))]),
        out_shape=[jax.ShapeDtypeStruct((m, N_CHIPS * ns), F32), jax.ShapeDtypeStruct((m, k), MXU_DTYPE),
                   jax.ShapeDtypeStruct((N_CHIPS,) + in_wire.shape, in_wire.dtype)],
        compiler_params=_params("arbitrary", "arbitrary"))(chip.reshape(1), x, g1, in_wire)


class _StepComm:
    TILES = {"in": IN_TILE, "rest": REST_TILE}

    def __init__(self, in_wire, rest_wire, chip, core):
        self.in_wire, self.rest_wire, self.chip, self.core = in_wire, rest_wire, chip, core
        self.sums, self.landed = {}, {}

    def input_projection(self, x, g1, prologue, tm):
        proj, u, shards = _mm_in_gathering(x, g1, prologue, self.in_wire, self.chip, tm)
        shards = lax.dynamic_update_slice(shards, self.in_wire[None], (self.chip, 0, 0, 0))
        return proj, u, shards.reshape(N_CHIPS, D_MODEL, IN_SHARD)

    def gather_rest(self):
        wire = self.rest_wire

        def sends(ins, outs, send_sems, recv_sems):
            (w_ref,), (out_ref,) = ins, outs
            x, y, c, chips = _place()
            return [_remote(w_ref.at[c], out_ref.at[2 * x + y, c], send_sems, recv_sems, 4 * j + 2 * c + to,
                            (cx, cy, to)) for j, (cx, cy) in enumerate(chips) for to in (0, 1)]

        def recvs(ins, outs, send_sems, recv_sems):
            (w_ref,), (out_ref,) = ins, outs
            _, _, c, chips = _place()
            return [_remote(w_ref.at[c], out_ref.at[2 * cx + cy, by], send_sems, recv_sems, 4 * j + 2 * by + c,
                            (cx, cy, by)) for j, (cx, cy) in enumerate(chips) for by in (0, 1)]

        def start(*refs):
            for cp in sends(*refs):
                cp.start()

        def wait(*refs):
            for cp in recvs(*refs):
                cp.wait_recv()
            for cp in sends(*refs):
                cp.wait_send()

        return _Riding((wire,), (jax.ShapeDtypeStruct((N_CHIPS,) + wire.shape, wire.dtype),), 12, start, wait)

    def rest_weights(self, landed):
        full = lax.dynamic_update_slice(landed, self.rest_wire[None], (self.chip, 0, 0, 0))
        return _unpack_rest_full(full.reshape(N_CHIPS, REST_ROWS, PACK_W))

    def swap(self, pg):
        def copies(ins, outs, send_sems, recv_sems):
            (pg_ref,), (out_ref,) = ins, outs
            x, y, c, _ = _place()
            return [_remote(pg_ref.at[j, 1 - c], out_ref.at[j], send_sems, recv_sems, j, (x, y, 1 - c))
                    for j in range(N_CHIPS)]

        def start(*refs):
            for cp in copies(*refs):
                cp.start()

        def wait(*refs):
            for cp in copies(*refs):
                cp.wait()

        return _Riding((pg,), (jax.ShapeDtypeStruct((N_CHIPS,) + pg.shape[2:], pg.dtype),), N_CHIPS, start, wait)

    def scatter(self, group, pg, theirs=None):
        if theirs is None:
            (theirs,) = _swap_halves([pg], "exchange_halves_" + group)
        ps = _pair_sum("sum_pair_" + group, pg, theirs, self.core, self.TILES[group])
        self.sums[group] = ps

        def copies(ins, outs, send_sems, recv_sems):
            (ps_ref,), (out_ref,) = ins, outs
            _, _, c, chips = _place()
            return [_remote(ps_ref.at[2 * cx + cy], out_ref.at[j], send_sems, recv_sems, j, (cx, cy, c))
                    for j, (cx, cy) in enumerate(chips)]

        def start(*refs):
            for cp in copies(*refs):
                cp.start()

        def wait(*refs):
            for cp in copies(*refs):
                cp.wait()

        return _Riding((ps,), (jax.ShapeDtypeStruct((3,) + ps.shape[1:], ps.dtype),), 3, start, wait)

    def reduced(self, group):
        return _chip_sum("sum_chips_" + group, self.sums[group], self.landed[group], self.chip, self.TILES[group])


def _adamw(w, g, m, v):
    m = ADAM_B1 * m + (1.0 - ADAM_B1) * g
    v = ADAM_B2 * v + (1.0 - ADAM_B2) * (g * g)
    m_hat = m / (1.0 - ADAM_B1 ** ADAM_STEP)
    v_hat = v / (1.0 - ADAM_B2 ** ADAM_STEP)
    return -ADAM_LR * (m_hat / (jnp.sqrt(v_hat) + ADAM_EPS) + ADAM_WD * w), m, v


def _small_reduce_adamw(part, w, m, v):
    def body(part_ref, w_ref, m_ref, v_ref, g_ref, d_ref, nm_ref, nv_ref, all_ref, send_sems, recv_sems):
        x, y, c, chips = _place()
        me, sibling = (x, y, c), (x, y, 1 - c)

        def rows(px, py, pc):
            return all_ref.at[4 * px + 2 * py + pc]

        all_ref[4 * x + 2 * y + c] = part_ref[...]
        first = [_remote(part_ref, rows(*me), send_sems, recv_sems, 0, sibling)]
        first += [_remote(part_ref, rows(*me), send_sems, recv_sems, 1 + j, (cx, cy, c))
                  for j, (cx, cy) in enumerate(chips)]
        for cp in first:
            cp.start()
        passed = []
        for j, (cx, cy) in enumerate(chips):
            _remote(part_ref, rows(cx, cy, c), send_sems, recv_sems, 1 + j, me).wait_recv()
            cp = _remote(rows(cx, cy, c), rows(cx, cy, c), send_sems, recv_sems, 4 + j, sibling)
            cp.start()
            passed.append(cp)
        _remote(part_ref, rows(*sibling), send_sems, recv_sems, 0, me).wait_recv()
        for j, (cx, cy) in enumerate(chips):
            _remote(part_ref, rows(cx, cy, 1 - c), send_sems, recv_sems, 4 + j, me).wait_recv()
        for cp in first + passed:
            cp.wait_send()
        g = all_ref[0]
        for dev in range(1, N_DEV):
            g = g + all_ref[dev]
        delta, nm, nv = _adamw(w_ref[...], g, m_ref[...], v_ref[...])
        g_ref[...] = g
        d_ref[...] = delta
        nm_ref[...] = nm
        nv_ref[...] = nv

    whole = pl.BlockSpec(memory_space=pltpu.VMEM)
    shape = jax.ShapeDtypeStruct((SMALL_ROWS, PACK_W), F32)
    return pl.pallas_call(
        body, name="small_reduce_adamw", in_specs=[whole] * 4, out_specs=[whole] * 4, out_shape=[shape] * 4,
        scratch_shapes=[pltpu.VMEM((N_DEV, SMALL_ROWS, PACK_W), F32), pltpu.SemaphoreType.DMA((7,)),
                        pltpu.SemaphoreType.DMA((7,))],
        compiler_params=pltpu.CompilerParams(vmem_limit_bytes=VMEM_LIMIT))(part, w, m, v)


def kernel(x, p, norm_g, w_in, hg_lb, hg_norm_g, w_o_hg, s5_a_re, s5_a_im, s5_log_dt, s5_b_re, s5_b_im, s5_c_re, s5_c_im, s5_d, w_glu, b_glu, w_o_s5, w_out, ple_norm_g, w_ple, w_ple_gate, final_norm_g, loss_target, m_norm_g, m_w_in, m_hg_lb, m_hg_norm_g, m_w_o_hg, m_s5_a_re, m_s5_a_im, m_s5_log_dt, m_s5_b_re, m_s5_b_im, m_s5_c_re, m_s5_c_im, m_s5_d, m_w_glu, m_b_glu, m_w_o_s5, m_w_out, m_ple_norm_g, m_w_ple, m_w_ple_gate, m_final_norm_g, v_norm_g, v_w_in, v_hg_lb, v_hg_norm_g, v_w_o_hg, v_s5_a_re, v_s5_a_im, v_s5_log_dt, v_s5_b_re, v_s5_b_im, v_s5_c_re, v_s5_c_im, v_s5_d, v_w_glu, v_b_glu, v_w_o_s5, v_w_out, v_ple_norm_g, v_w_ple, v_w_ple_gate, v_final_norm_g):
    given = dict(locals())
    wts = {n: given[n] for n in WEIGHTS}
    mom = {n: given["m_" + n] for n in WEIGHTS}
    var = {n: given["v_" + n] for n in WEIGHTS}
    cx, cy, cc = lax.axis_index("x"), lax.axis_index("y"), lax.axis_index("c")
    chip = (2 * cx + cy).astype(jnp.int32)

    core = cc.astype(jnp.int32)
    rest_shard = _pack_rest({n: wts[n][0] for n in REST})
    comm = _StepComm(wts["w_in"][0].astype(MXU_DTYPE).reshape(2, D_MODEL // 2, IN_SHARD),
                     rest_shard.astype(MXU_DTYPE).reshape(2, REST_ROWS // 2, PACK_W), chip, core)

    t_len = x.shape[1]
    loss_row, grad_x, g_big, g_small = _local_step(x.reshape(t_len, D_MODEL), p.reshape(t_len, -1),
                                                   loss_target.reshape(t_len, D_MODEL), None,
                                                   {n: wts[n] for n in SMALL}, comm)

    zero = jnp.zeros((), F32)
    sg, sd, snm, snv = _small_reduce_adamw(_pack_small(g_small, loss_row[0, 0]),
                                           _pack_small({n: wts[n] for n in SMALL}, zero),
                                           _pack_small({n: mom[n] for n in SMALL}, zero),
                                           _pack_small({n: var[n] for n in SMALL}, zero))
    (sg, loss), (sd, _), (snm, _), (snv, _) = (_unpack_small(a) for a in (sg, sd, snm, snv))

    halves = [comm.reduced("in"), comm.reduced("rest")]
    g_in, g_rest = [lax.dynamic_update_slice(got, mine[None], (core, 0, 0))
                    for got, mine in zip(_share_halves(halves), halves)]
    g_in, g_rest = g_in.reshape(D_MODEL, IN_SHARD), g_rest.reshape(REST_ROWS, PACK_W)

    def adam_f(wv, gv, mv, vv):
        return _adamw(wv, gv, mv, vv)

    d_in, nm_in, nv_in = _rowwise("adamw_in", adam_f, D_MODEL, IN_TILE,
                                  [(wts["w_in"][0], IN_SHARD, 0), (g_in, IN_SHARD, 0), (mom["w_in"][0], IN_SHARD, 0),
                                   (var["w_in"][0], IN_SHARD, 0)], [], [(IN_SHARD, F32)] * 3)
    d_rest, nm_rest, nv_rest = _rowwise("adamw_rest", adam_f, REST_ROWS, REST_TILE,
                                        [(rest_shard, PACK_W, 0), (g_rest, PACK_W, 0),
                                         (_pack_rest({n: mom[n][0] for n in REST}), PACK_W, 0),
                                         (_pack_rest({n: var[n][0] for n in REST}), PACK_W, 0)], [],
                                        [(PACK_W, F32)] * 3)
    bg, bd, bnm, bnv = (dict(_unpack_rest(rest), w_in=a.reshape(1, D_MODEL, IN_SHARD))
                        for rest, a in ((g_rest, g_in), (d_rest, d_in), (nm_rest, nm_in), (nv_rest, nv_in)))

    outs = [loss, grad_x.reshape(x.shape)]
    for small, big in ((sg, bg), (sd, bd), (snm, bnm), (snv, bnv)):
        outs += [big[n] if n in BIG else small[n] for n in WEIGHTS]
    return tuple(outs)
```

```python
import functools
from typing import Callable, NamedTuple

import jax
import jax.numpy as jnp
from jax import lax
from jax.experimental import pallas as pl
from jax.experimental.pallas import tpu as pltpu

F32 = jnp.float32
MXU_DTYPE = jnp.bfloat16
WIRE_DTYPE = jnp.bfloat16
NORM_EPS = 1e-6
D_MODEL = 1024
HG_HEADS = 8
HG_DIM = 128
HG_CHUNK = 64
S5_WIDTH = 512
S5_GROUPS = 32
S5_GROUP = 16
S5_STATE = 64
S5_LANES = S5_GROUPS * S5_STATE
IN_COLS = 7168
SUBLANES = 8
VMEM_LIMIT = 56 * 1024 * 1024
HIGHEST = lax.Precision.HIGHEST
MESH = pl.DeviceIdType.MESH

ADAM_LR, ADAM_B1, ADAM_B2, ADAM_EPS, ADAM_WD, ADAM_STEP = 0.001, 0.9, 0.999, 1e-08, 0.01, 10

BIG = ("w_in", "w_o_hg", "w_glu", "w_o_s5", "w_out", "w_ple", "w_ple_gate")
BIG_SHAPE = {"w_in": (1024, 7168), "w_o_hg": (1024, 1024), "w_glu": (512, 1024), "w_o_s5": (512, 1024),
             "w_out": (1024, 1024), "w_ple": (256, 1024), "w_ple_gate": (1024, 1024)}
BIG_COL_SHARDED = ("w_in", "w_glu", "w_o_s5", "w_ple")
SMALL = ("norm_g", "hg_lb", "hg_norm_g", "s5_a_re", "s5_a_im", "s5_log_dt", "s5_b_re", "s5_b_im", "s5_c_re",
         "s5_c_im", "s5_d", "b_glu", "ple_norm_g", "final_norm_g")
SMALL_SHAPE = {"norm_g": (1, 1024), "hg_lb": (2, 1024), "hg_norm_g": (1, 1024), "s5_a_re": (1, 32, 64),
               "s5_a_im": (1, 32, 64), "s5_log_dt": (1, 32), "s5_b_re": (1, 32, 64, 16), "s5_b_im": (1, 32, 64, 16),
               "s5_c_re": (1, 32, 16, 64), "s5_c_im": (1, 32, 16, 64), "s5_d": (1, 32, 16), "b_glu": (1, 1024),
               "ple_norm_g": (1, 1024), "final_norm_g": (1024,)}
WEIGHTS = ("norm_g", "w_in", "hg_lb", "hg_norm_g", "w_o_hg", "s5_a_re", "s5_a_im", "s5_log_dt", "s5_b_re", "s5_b_im",
           "s5_c_re", "s5_c_im", "s5_d", "w_glu", "b_glu", "w_o_s5", "w_out", "ple_norm_g", "w_ple", "w_ple_gate",
           "final_norm_g")
N_CHIPS = 4
N_DEV = 8
PACK_W = 1024
SMALL_ROWS = 144


def _params(*sem):
    return pltpu.CompilerParams(dimension_semantics=sem, vmem_limit_bytes=VMEM_LIMIT)


def _sig(x):
    return 1.0 / (1.0 + jnp.exp(-x))


def _dsilu(z, s):
    return s * (1.0 + z * (1.0 - s))


def _mx(x):
    return x.astype(MXU_DTYPE)


def _dot(a, b, dims=(((1,), (0,)), ((), ()))):
    return lax.dot_general(_mx(a), _mx(b), dims, preferred_element_type=F32)


_NT = (((1,), (1,)), ((), ()))
_TN = (((0,), (0,)), ((), ()))


def _dot32(a, b):
    return jnp.dot(a, b, precision=HIGHEST, preferred_element_type=F32)


def _rms_bwd(dy, x, g):
    r = lax.rsqrt(jnp.mean(x * x, axis=-1, keepdims=True) + NORM_EPS)
    t = dy * g
    dx = r * t - x * (r * r * r) * jnp.mean(t * x, axis=-1, keepdims=True)
    return dx, jnp.sum(dy * x * r, axis=0, keepdims=True)


RING = 3


def _rowwise(name, fn, n_rows_total, tm, rows, consts, outs, accs=(), alias=None, ring=False):
    n_r, n_c, n_o, n_a = len(rows), len(consts), len(outs), len(accs)
    n_steps = n_rows_total // tm

    def body(*refs):
        row_refs = refs[:n_r]
        const_refs = refs[n_r:n_r + n_c]
        pos = n_r + n_c + (1 if alias is not None else 0)
        out_refs = refs[pos:pos + n_o]
        acc_refs = refs[pos + n_o:pos + n_o + n_a]
        if ring:
            slot_refs, sems = refs[pos + n_o + n_a:pos + n_o + n_a + n_r], refs[pos + n_o + n_a + n_r]
            i = pl.program_id(0)

            def fetch(step):
                first = step * tm if isinstance(step, int) else pl.multiple_of(step * tm, tm)
                return [pltpu.make_async_copy(src.at[pl.ds(first, tm), pl.ds(cb * w, w)], dst.at[step % RING],
                                              sems.at[k, step % RING])
                        for k, (src, dst, (_, w, cb)) in enumerate(zip(row_refs, slot_refs, rows))]

            @pl.when(i == 0)
            def _():
                for step in range(min(RING - 1, n_steps)):
                    for cp in fetch(step):
                        cp.start()

            @pl.when(i + (RING - 1) < n_steps)
            def _():
                for cp in fetch(i + (RING - 1)):
                    cp.start()

            for cp in fetch(i):
                cp.wait()
            tiles = [dst[i % RING] for dst in slot_refs]
        else:
            tiles = [r[...] for r in row_refs]
        res = fn(*tiles, *[r[...] for r in const_refs])
        for r, v in zip(out_refs, res[:n_o]):
            r[...] = v.astype(r.dtype)
        if n_a:
            @pl.when(pl.program_id(0) == 0)
            def _():
                for r in acc_refs:
                    r[...] = jnp.zeros_like(r)
            for r, v in zip(acc_refs, res[n_o:]):
                r[...] += v

    if ring:
        in_specs = [pl.BlockSpec(memory_space=pl.ANY)] * n_r
        scratch = [pltpu.VMEM((RING, tm, w), a.dtype) for (a, w, _) in rows] + [pltpu.SemaphoreType.DMA((n_r, RING))]
    else:
        in_specs = [pl.BlockSpec((tm, w), functools.partial(lambda i, cb: (i, cb), cb=cb)) for (_, w, cb) in rows]
        scratch = []
    in_specs += [pl.BlockSpec(c.shape, lambda i: (0, 0)) for c in consts]
    args = [a for (a, _, _) in rows] + list(consts)
    out_shape, out_specs = [], []
    for o in outs:
        w, dt = o[0], o[1]
        cb, total = (o[2], o[3]) if len(o) == 4 else (0, w)
        out_shape.append(jax.ShapeDtypeStruct((n_rows_total, total), dt))
        out_specs.append(pl.BlockSpec((tm, w), functools.partial(lambda i, cb: (i, cb), cb=cb)))
    io_alias = {}
    if alias is not None:
        in_specs.append(pl.BlockSpec(memory_space=pl.ANY))
        args.append(alias[0])
        io_alias = {len(args) - 1: alias[1]}
    for (r, w) in accs:
        out_shape.append(jax.ShapeDtypeStruct((r, w), F32))
        out_specs.append(pl.BlockSpec((r, w), lambda i: (0, 0)))
    res = pl.pallas_call(body, name=name, grid=(n_steps,), in_specs=in_specs, out_specs=out_specs,
                         out_shape=out_shape, scratch_shapes=scratch, input_output_aliases=io_alias,
                         compiler_params=_params("arbitrary"))(*args)
    return res


class _Riding(NamedTuple):
    ins: tuple
    outs: tuple
    n_sems: int
    start: Callable
    wait: Callable


_HBM = pl.BlockSpec(memory_space=pl.ANY)


def _ride(riding, refs, n_in, n_out, n_scratch, first, last):
    if riding is None:
        return refs[:n_in], refs[n_in:n_in + n_out], refs[n_in + n_out:]
    r_in, r_out = len(riding.ins), len(riding.outs)
    ins, rins = refs[:n_in], refs[n_in:n_in + r_in]
    pos = n_in + r_in
    outs, routs = refs[pos:pos + n_out], refs[pos + n_out:pos + n_out + r_out]
    pos += n_out + r_out
    scratch, (send_sems, recv_sems) = refs[pos:pos + n_scratch], refs[pos + n_scratch:]

    @pl.when(first)
    def _():
        riding.start(rins, routs, send_sems, recv_sems)

    @pl.when(last)
    def _():
        riding.wait(rins, routs, send_sems, recv_sems)

    return ins, outs, scratch


def _riding_call(riding, body, name, grid, in_specs, args, out_specs, out_shape, scratch, io_alias=None):
    if riding is not None:
        in_specs = list(in_specs) + [_HBM] * len(riding.ins)
        args = list(args) + list(riding.ins)
        out_specs = list(out_specs) + [_HBM] * len(riding.outs)
        out_shape = list(out_shape) + list(riding.outs)
        scratch = list(scratch) + [pltpu.SemaphoreType.DMA((riding.n_sems,))] * 2
    return pl.pallas_call(body, name=name, grid=grid, in_specs=in_specs, out_specs=out_specs, out_shape=out_shape,
                          scratch_shapes=scratch, input_output_aliases=io_alias or {},
                          compiler_params=_params(*(["arbitrary"] * len(grid))))(*args)


def _mm_nn(name, a, b, tm, tn, riding=None, prologue=None, consts=()):
    m, k = a.shape
    n = b.shape[1] if b.ndim == 2 else b.shape[0] * b.shape[2]
    grid = (n // tn, m // tm)
    n_out, scratch = (1, []) if prologue is None else (2, [pltpu.VMEM((m, k), MXU_DTYPE)])

    def body(*refs):
        j, i = pl.program_id(0), pl.program_id(1)
        ins, outs, kept = _ride(riding, refs, 2 + len(consts), n_out, len(scratch), (j == 0) & (i == 0),
                                (j == grid[0] - 1) & (i == grid[1] - 1))
        if prologue is None:
            left = ins[0][...]
        else:
            rows = pl.ds(pl.multiple_of(i * tm, tm), tm)

            @pl.when(j == 0)
            def _():
                tile = _mx(prologue(ins[0][...], *[c[...] for c in ins[2:]]))
                kept[0][rows, :] = tile
                outs[1][...] = tile

            left = kept[0][rows, :]
        outs[0][...] = _dot(left, ins[1][...])

    once = (lambda j, i: (i, 0)) if prologue is None else (lambda j, i: (jnp.where(j == 0, i, grid[1] - 1), 0))
    b_spec = (pl.BlockSpec((k, tn), lambda j, i: (0, j)) if b.ndim == 2
              else pl.BlockSpec((None, k, tn), lambda j, i: (j, 0, 0)))
    in_specs = [pl.BlockSpec((tm, k), once), b_spec]
    in_specs += [pl.BlockSpec(c.shape, lambda j, i: (0, 0)) for c in consts]
    out_specs = [pl.BlockSpec((tm, tn), lambda j, i: (i, j))]
    out_shape = [jax.ShapeDtypeStruct((m, n), F32)]
    if prologue is not None:
        out_specs.append(pl.BlockSpec((tm, k), once))
        out_shape.append(jax.ShapeDtypeStruct((m, k), MXU_DTYPE))
    res = _riding_call(riding, body, name, grid, in_specs, [a, b] + list(consts), out_specs, out_shape, scratch)
    return res[0] if riding is None and prologue is None else res


def _mm_nt_then(name, a, b, tm, tn, fn, rows, consts, outs, accs=(), alias=None, riding=None):
    m, n = a.shape
    k = b.shape[-2]
    steps = n // tn
    n_r, n_c, n_o, n_a = len(rows), len(consts), len(outs), len(accs)

    def body(*refs):
        a_ref, b_ref = refs[:2]
        row_refs = refs[2:2 + n_r]
        const_refs = refs[2 + n_r:2 + n_r + n_c]
        i, s = pl.program_id(0), pl.program_id(1)
        n_in = 2 + n_r + n_c + (1 if alias is not None else 0)
        _, outs_, (mm_ref,) = _ride(riding, refs, n_in, n_o + n_a, 1, (i == 0) & (s == 0),
                                    (i == m // tm - 1) & (s == steps - 1))
        out_refs, acc_refs = outs_[:n_o], outs_[n_o:]
        part = _dot(a_ref[...], b_ref[...] if b.ndim == 2 else b_ref[s], _NT)
        if steps > 1:
            @pl.when(s == 0)
            def _():
                mm_ref[...] = jnp.zeros_like(mm_ref)
            mm_ref[...] += part

        @pl.when(s == steps - 1)
        def _():
            res = fn(mm_ref[...] if steps > 1 else part, *[r[...] for r in row_refs], *[r[...] for r in const_refs])
            for r, v in zip(out_refs, res[:n_o]):
                r[...] = v.astype(r.dtype)
            if n_a:
                @pl.when(i == 0)
                def _():
                    for r in acc_refs:
                        r[...] = jnp.zeros_like(r)
                for r, v in zip(acc_refs, res[n_o:]):
                    r[...] += v

    b_spec = (pl.BlockSpec((k, tn), lambda i, s: (0, s)) if b.ndim == 2
              else pl.BlockSpec(memory_space=pltpu.VMEM))
    in_specs = [pl.BlockSpec((tm, tn), lambda i, s: (i, s)), b_spec]
    in_specs += [pl.BlockSpec((tm, w), functools.partial(lambda i, s, cb: (i, cb), cb=cb)) for (_, w, cb) in rows]
    in_specs += [pl.BlockSpec(c.shape, lambda i, s: (0, 0)) for c in consts]
    args = [a, b] + [r[0] for r in rows] + list(consts)
    out_shape, out_specs = [], []
    for o in outs:
        w, dt = o[0], o[1]
        cb, total = (o[2], o[3]) if len(o) == 4 else (0, w)
        out_shape.append(jax.ShapeDtypeStruct((m, total), dt))
        out_specs.append(pl.BlockSpec((tm, w), functools.partial(lambda i, s, cb: (i, cb), cb=cb)))
    io_alias = {}
    if alias is not None:
        in_specs.append(pl.BlockSpec(memory_space=pl.ANY))
        args.append(alias[0])
        io_alias = {len(args) - 1: alias[1]}
    for (r, w) in accs:
        out_shape.append(jax.ShapeDtypeStruct((r, w), F32))
        out_specs.append(pl.BlockSpec((r, w), lambda i, s: (0, 0)))
    return _riding_call(riding, body, name, (m // tm, steps), in_specs, args, out_specs, out_shape,
                        [pltpu.VMEM((tm, k), F32)], io_alias)


def _mm_tn(name, a, b, tk, tn, col_shards=False, riding=None):
    t, k = a.shape
    n = b.shape[1]
    steps = t // tk

    def body(*refs):
        j, s = pl.program_id(0), pl.program_id(1)
        (a_ref, b_ref), (o_ref,), (acc_ref,) = _ride(riding, refs, 2, 1, 1, (j == 0) & (s == 0),
                                                     (j == n // tn - 1) & (s == steps - 1))

        @pl.when(s == 0)
        def _():
            acc_ref[...] = jnp.zeros_like(acc_ref)

        acc_ref[...] += _dot(a_ref[...], b_ref[...], _TN)

        @pl.when(s == steps - 1)
        def _():
            o_ref[...] = acc_ref[...]

    if col_shards:
        out_spec = pl.BlockSpec((None, k, tn), lambda j, s: (j, 0, 0))
        out_shape = jax.ShapeDtypeStruct((n // tn, k, tn), F32)
    else:
        out_spec = pl.BlockSpec((k, tn), lambda j, s: (0, j))
        out_shape = jax.ShapeDtypeStruct((k, n), F32)
    res = _riding_call(riding, body, name, (n // tn, steps),
                       [pl.BlockSpec((tk, k), lambda j, s: (s, 0)), pl.BlockSpec((tk, tn), lambda j, s: (s, j))],
                       [a, b], [out_spec], [out_shape], [pltpu.VMEM((k, tn), F32)])
    return res[0] if riding is None else res


def _dot01(m01, x):
    m = m01.astype(MXU_DTYPE)
    hi = x.astype(MXU_DTYPE)
    r1 = x - hi.astype(F32)
    mid = r1.astype(MXU_DTYPE)
    lo = (r1 - mid.astype(F32)).astype(MXU_DTYPE)
    dot = lambda v: jnp.dot(m, v, preferred_element_type=F32)
    return dot(hi) + dot(mid) + dot(lo)


def _chunk_rows(x, offset, nck):
    return jnp.concatenate([jnp.broadcast_to(x[c * HG_CHUNK + offset:c * HG_CHUNK + offset + 1, :],
                                             (HG_CHUNK, x.shape[1])) for c in range(nck)], axis=0)


def _hg_block_terms(q, f, lb, tb):
    nck = tb // HG_CHUNK
    sig = _sig(f)
    fv = lb + (1.0 - lb) * sig
    kk = (1.0 - lb) * (1.0 - sig)
    row = lax.broadcasted_iota(jnp.int32, (tb, tb), 0)
    col = lax.broadcasted_iota(jnp.int32, (tb, tb), 1)
    same = jnp.right_shift(row, 6) == jnp.right_shift(col, 6)
    causal, anti = same & (row >= col), same & (row <= col)
    b = _dot01(causal, jnp.log(fv))
    b_mid, b_last = _chunk_rows(b, HG_CHUNK // 2 - 1, nck), _chunk_rows(b, HG_CHUNK - 1, nck)
    e_mid, e_mid_inv = jnp.exp(b - b_mid), jnp.exp(b_mid - b)
    e_b, e_last = jnp.exp(b), jnp.exp(b_last - b)
    dcs = [jnp.exp(b[c * HG_CHUNK + HG_CHUNK - 1:(c + 1) * HG_CHUNK, :]) for c in range(nck)]
    return sig, fv, kk, causal, anti, e_mid, e_mid_inv, e_b, e_last, dcs


def _hgrn2_fwd(proj, hg_lb, hg_norm_g, t_len, tb, riding=None):
    nck = tb // HG_CHUNK
    nb = t_len // tb

    def body(*refs):
        step = pl.program_id(0)
        ((p_ref, lb_ref, gn_ref), (o_ref, act_ref, sp_ref),
         (st_ref, a_s, bm_s, qd_s, kd_s, v_s, sc_s, inc_s)) = _ride(riding, refs, 3, 3, 8, step == 0, step == nb - 1)

        @pl.when(pl.program_id(0) == 0)
        def _():
            st_ref[...] = jnp.zeros_like(st_ref)

        lb = _sig(lb_ref[0:1, :] - lb_ref[1:2, :])
        q = p_ref[:, pl.ds(0, 1024)]
        _, _, kk, causal, _, e_mid, e_mid_inv, e_b, e_last, dcs = _hg_block_terms(q, p_ref[:, pl.ds(1024, 1024)],
                                                                                   lb, tb)
        a_s[...] = _mx(q * e_mid)
        bm_s[...] = _mx(kk * e_mid_inv)
        qd_s[...] = _mx(q * e_b)
        kd_s[...] = _mx(kk * e_last)
        v_s[...] = _mx(p_ref[:, pl.ds(2048, 1024)])
        heads = [pl.ds(h * HG_DIM, HG_DIM) for h in range(HG_HEADS)]
        chunks = [pl.ds(c * HG_CHUNK, HG_CHUNK) for c in range(nck)]
        for h, hs in enumerate(heads):
            sc_s[h] = _mx(jnp.where(causal, _dot(a_s[:, hs], bm_s[:, hs], _NT), 0.0))
        for h, hs in enumerate(heads):
            o_ref[:, hs] = _dot(sc_s[h], v_s[:, hs])
        for h, hs in enumerate(heads):
            for c, r in enumerate(chunks):
                inc_s[h, c] = _dot(v_s[r, hs], kd_s[r, hs], _TN)
        for c in range(nck):
            for h in range(HG_HEADS):
                st = st_ref[h]
                sp_ref[h, c] = st
                st_ref[h] = dcs[c][:, h * HG_DIM:(h + 1) * HG_DIM] * st + inc_s[h, c]
        for c, r in enumerate(chunks):
            for h, hs in enumerate(heads):
                o_ref[r, hs] += _dot(qd_s[r, hs], sp_ref[h, c], _NT)
        for h, hs in enumerate(heads):
            o = o_ref[:, hs]
            rr = lax.rsqrt(jnp.mean(o * o, axis=-1, keepdims=True) + NORM_EPS)
            g = p_ref[:, pl.ds(3072 + h * HG_DIM, HG_DIM)]
            act_ref[:, hs] = (o * rr * gn_ref[:, hs] * (g * _sig(g))).astype(act_ref.dtype)

    return _riding_call(
        riding, body, "hgrn2_fwd", (nb,),
        [pl.BlockSpec((tb, 4096), lambda i: (i, 0)), pl.BlockSpec((2, 1024), lambda i: (0, 0)),
         pl.BlockSpec((1, 1024), lambda i: (0, 0))],
        [proj, hg_lb, hg_norm_g],
        [pl.BlockSpec((tb, 1024), lambda i: (i, 0)), pl.BlockSpec((tb, 1024), lambda i: (i, 0)),
         pl.BlockSpec((HG_HEADS, nck, HG_DIM, HG_DIM), lambda i: (0, i, 0, 0))],
        [jax.ShapeDtypeStruct((t_len, 1024), F32), jax.ShapeDtypeStruct((t_len, 1024), MXU_DTYPE),
         jax.ShapeDtypeStruct((HG_HEADS, t_len // HG_CHUNK, HG_DIM, HG_DIM), F32)],
        [pltpu.VMEM((HG_HEADS, HG_DIM, HG_DIM), F32)] + [pltpu.VMEM((tb, 1024), MXU_DTYPE)] * 5
        + [pltpu.VMEM((HG_HEADS, tb, tb), MXU_DTYPE), pltpu.VMEM((HG_HEADS, nck, HG_DIM, HG_DIM), F32)])


def _hgrn2_bwd(proj, d_o, s_prev, hg_lb, dproj, t_len, tb, riding=None):
    nck = tb // HG_CHUNK
    nb = t_len // tb

    def body(*refs):
        step = pl.program_id(0)
        ((p_ref, do_ref, sp_ref, lb_ref, _), (dp_ref, dlb_ref),
         (ds_ref, acc_ref, a_s, bm_s, qd_s, kd_s, v_s, do_s, da_s, dbm_s, dqd_s, dkd_s, dv_s, ex_s, sc_s, dsc_s,
          up_s)) = _ride(riding, refs, 5, 2, 17, step == 0, step == nb - 1)

        @pl.when(pl.program_id(0) == 0)
        def _():
            ds_ref[...] = jnp.zeros_like(ds_ref)
            acc_ref[...] = jnp.zeros_like(acc_ref)

        lb = _sig(lb_ref[0:1, :] - lb_ref[1:2, :])
        q = p_ref[:, pl.ds(0, 1024)]
        sig, fv, kk, causal, anti, e_mid, e_mid_inv, e_b, e_last, dcs = _hg_block_terms(
            q, p_ref[:, pl.ds(1024, 1024)], lb, tb)
        a, bm, qd, kd = q * e_mid, kk * e_mid_inv, q * e_b, kk * e_last
        a_s[...] = _mx(a)
        bm_s[...] = _mx(bm)
        qd_s[...] = _mx(qd)
        kd_s[...] = _mx(kd)
        v_s[...] = _mx(p_ref[:, pl.ds(2048, 1024)])
        do_s[...] = _mx(do_ref[...])
        heads = [pl.ds(h * HG_DIM, HG_DIM) for h in range(HG_HEADS)]
        chunks = [pl.ds(c * HG_CHUNK, HG_CHUNK) for c in range(nck)]
        for h, hs in enumerate(heads):
            sc_s[h] = _mx(jnp.where(causal, _dot(a_s[:, hs], bm_s[:, hs], _NT), 0.0))
            dsc_s[h] = _mx(jnp.where(causal, _dot(do_s[:, hs], v_s[:, hs], _NT), 0.0))
        for h, hs in enumerate(heads):
            dv_s[:, hs] = _dot(sc_s[h], do_s[:, hs], _TN)
            da_s[:, hs] = _dot(dsc_s[h], bm_s[:, hs])
            dbm_s[:, hs] = _dot(dsc_s[h], a_s[:, hs], _TN)
        for h, hs in enumerate(heads):
            for c, r in enumerate(chunks):
                up_s[h, c] = _dot(do_s[r, hs], qd_s[r, hs], _TN)
                dqd_s[r, hs] = _dot(do_s[r, hs], sp_ref[h, c])
        for c in reversed(range(nck)):
            r = chunks[c]
            for h, hs in enumerate(heads):
                dst = ds_ref[h]
                dc = dcs[c][:, h * HG_DIM:(h + 1) * HG_DIM]
                dv_s[r, hs] += _dot(kd_s[r, hs], dst, _NT)
                dkd_s[r, hs] = _dot(v_s[r, hs], dst)
                ex_s[c:c + 1, hs] = jnp.sum(dst * sp_ref[h, c], axis=0, keepdims=True) * dc
                ds_ref[h] = up_s[h, c] + dc * dst
        da, dbm, dqd, dkd = da_s[...], dbm_s[...], dqd_s[...], dkd_s[...]
        dq = da * e_mid + dqd * e_b
        dk = dbm * e_mid_inv + dkd * e_last
        db = da * a - dbm * bm + dqd * qd - dkd * kd
        dkk = dkd * kd
        extra = jnp.concatenate(
            [jnp.broadcast_to(jnp.sum(dkk[c * HG_CHUNK:(c + 1) * HG_CHUNK], axis=0, keepdims=True)
                              + ex_s[c:c + 1, :], (HG_CHUNK, 1024)) for c in range(nck)], axis=0)
        dlogf = _dot01(anti, db) + extra
        dfv_k = dlogf / fv - dk
        dp_ref[:, pl.ds(0, 1024)] = dq.astype(dp_ref.dtype)
        dp_ref[:, pl.ds(1024, 1024)] = (dfv_k * (1.0 - lb) * sig * (1.0 - sig)).astype(dp_ref.dtype)
        dp_ref[:, pl.ds(2048, 1024)] = dv_s[...].astype(dp_ref.dtype)
        acc_ref[...] += jnp.sum(dfv_k * (1.0 - sig), axis=0, keepdims=True)

        @pl.when(pl.program_id(0) == nb - 1)
        def _():
            g0 = acc_ref[...] * lb * (1.0 - lb)
            dlb_ref[0:1, :] = g0
            dlb_ref[1:2, :] = -g0

    return _riding_call(
        riding, body, "hgrn2_bwd", (nb,),
        [pl.BlockSpec((tb, 3072), lambda i: (nb - 1 - i, 0)),
         pl.BlockSpec((tb, 1024), lambda i: (nb - 1 - i, 0)),
         pl.BlockSpec((HG_HEADS, nck, HG_DIM, HG_DIM), lambda i: (0, nb - 1 - i, 0, 0)),
         pl.BlockSpec((2, 1024), lambda i: (0, 0)),
         pl.BlockSpec(memory_space=pl.ANY)],
        [proj, d_o, s_prev, hg_lb, dproj],
        [pl.BlockSpec((tb, 3072), lambda i: (nb - 1 - i, 0)), pl.BlockSpec((2, 1024), lambda i: (0, 0))],
        [jax.ShapeDtypeStruct((t_len, IN_COLS), dproj.dtype), jax.ShapeDtypeStruct((2, 1024), F32)],
        [pltpu.VMEM((HG_HEADS, HG_DIM, HG_DIM), F32), pltpu.VMEM((1, 1024), F32)]
        + [pltpu.VMEM((tb, 1024), MXU_DTYPE)] * 6 + [pltpu.VMEM((tb, 1024), F32)] * 5
        + [pltpu.VMEM((SUBLANES, 1024), F32)] + [pltpu.VMEM((HG_HEADS, tb, tb), MXU_DTYPE)] * 2
        + [pltpu.VMEM((HG_HEADS, nck, HG_DIM, HG_DIM), F32)], {4: 0})


def _s5_prep_bwd(a_re, a_im, log_dt, b_re_t, b_im_t, dlam, dbbr, dbbi):
    def body(ar_ref, ai_ref, ldt_ref, br_ref, bi_ref, dlam_ref, dbbr_ref, dbbi_ref,
             dar_ref, dai_ref, dldt_ref, dbr_ref, dbi_ref):
        ar, ai = ar_ref[...], ai_ref[...]
        dt = jnp.exp(ldt_ref[...])
        mag = jnp.exp(ar * dt)
        cs, sn = jnp.cos(ai * dt), jnp.sin(ai * dt)
        lr, li = mag * cs, mag * sn
        den = ar * ar + ai * ai
        nr = lr - 1.0
        sr = (nr * ar + li * ai) / den
        si = (li * ar - nr * ai) / den
        br, bi = br_ref[...], bi_ref[...]
        gbr, gbi = dbbr_ref[...], dbbi_ref[...]
        dbr_ref[...] = sr * gbr + si * gbi
        dbi_ref[...] = sr * gbi - si * gbr
        dsr = jnp.sum(gbr * br + gbi * bi, axis=0, keepdims=True)
        dsi = jnp.sum(gbi * br - gbr * bi, axis=0, keepdims=True)
        dnr = (dsr * ar - dsi * ai) / den
        dli = dlam_ref[1:2, :] + (dsr * ai + dsi * ar) / den
        dlr = dlam_ref[0:1, :] + dnr
        dden = -(dsr * sr + dsi * si) / den
        dar = (dsr * nr + dsi * li) / den + dden * 2.0 * ar
        dai = (dsr * li - dsi * nr) / den + dden * 2.0 * ai
        dmag = dlr * cs + dli * sn
        dth = mag * (dli * cs - dlr * sn)
        dar_ref[...] = dar + dmag * mag * dt
        dai_ref[...] = dai + dth * dt
        ddt = (dmag * mag * ar + dth * ai) * dt
        lane = lax.broadcasted_iota(jnp.int32, (S5_LANES, 128), 0) // S5_STATE
        grp = lax.broadcasted_iota(jnp.int32, (S5_LANES, 128), 1)
        dldt_ref[...] = _dot32(jnp.broadcast_to(ddt, (SUBLANES, S5_LANES)), (lane == grp).astype(F32))

    whole = pl.BlockSpec(memory_space=pltpu.VMEM)
    return pl.pallas_call(
        body, name="s5_prep_bwd", in_specs=[whole] * 8, out_specs=[whole] * 5,
        out_shape=[jax.ShapeDtypeStruct((1, S5_LANES), F32), jax.ShapeDtypeStruct((1, S5_LANES), F32),
                   jax.ShapeDtypeStruct((SUBLANES, 128), F32), jax.ShapeDtypeStruct((S5_GROUP, S5_LANES), F32),
                   jax.ShapeDtypeStruct((S5_GROUP, S5_LANES), F32)])(a_re, a_im, log_dt, b_re_t, b_im_t, dlam, dbbr,
                                                                      dbbi)


def _dgelu(x):
    c, a = 0.7978845608028654, 0.044715
    th = jnp.tanh(c * (x + a * x * x * x))
    return 0.5 * (1.0 + th) + 0.5 * x * (1.0 - th * th) * c * (1.0 + 3.0 * a * x * x)


S5_BLOCKS = 4
S5_BW = S5_WIDTH // S5_BLOCKS
S5_BL = S5_LANES // S5_BLOCKS
S5_LANE_BLOCKS = S5_LANES // 128
S5_SCAN_BLOCKS = 4


def _s5_prep(a_re, a_im, log_dt, b_re_t, b_im_t, seg):
    def body(ar_ref, ai_ref, ldt_ref, br_ref, bi_ref,
             rows_f, pfr_ref, pfi_ref, rows_r, prr_ref, pri_ref, bbr_ref, bbi_ref):
        ar, ai = ar_ref[...], ai_ref[...]
        dt = jnp.exp(ldt_ref[...])
        mag = jnp.exp(ar * dt)
        lr, li = mag * jnp.cos(ai * dt), mag * jnp.sin(ai * dt)
        den = ar * ar + ai * ai
        nr = lr - 1.0
        sr = (nr * ar + li * ai) / den
        si = (li * ar - nr * ai) / den
        wide = (SUBLANES, S5_LANES)
        cr, ci = lr, li
        for i in range(seg):
            pfr_ref[i] = jnp.broadcast_to(cr, wide)
            pfi_ref[i] = jnp.broadcast_to(ci, wide)
            prr_ref[seg - 1 - i] = jnp.broadcast_to(cr, wide)
            pri_ref[seg - 1 - i] = jnp.broadcast_to(-ci, wide)
            if i == seg - 1:
                for rows, sign in ((rows_f, 1.0), (rows_r, -1.0)):
                    rows[0:1, :] = lr
                    rows[1:2, :] = sign * li
                    rows[2:3, :] = cr
                    rows[3:4, :] = sign * ci
            cr, ci = cr * lr - ci * li, cr * li + ci * lr
        br, bi = br_ref[...], bi_ref[...]
        bbr_ref[...] = sr * br - si * bi
        bbi_ref[...] = sr * bi + si * br

    whole = pl.BlockSpec(memory_space=pltpu.VMEM)
    tables = [jax.ShapeDtypeStruct((4, S5_LANES), F32)] + [jax.ShapeDtypeStruct((seg, SUBLANES, S5_LANES), F32)] * 2
    bbar = [jax.ShapeDtypeStruct((S5_GROUP, S5_LANES), F32)] * 2
    res = pl.pallas_call(body, name="s5_prep", in_specs=[whole] * 5, out_specs=[whole] * 8,
                         out_shape=tables + tables + bbar)(a_re, a_im, log_dt, b_re_t, b_im_t)
    return res[0:3], res[3:6], res[6], res[7]


def _lanes(j):
    return pl.ds(j * 128, 128)


def _to_segment_order(v, stage_ref, out_ref, seg):
    nbl = v.shape[1] // 128
    for b in range(nbl):
        stage_ref[b] = v[:, b * 128:(b + 1) * 128]

    def body(t, carry):
        rows = pl.ds(pl.multiple_of(t * SUBLANES, SUBLANES), SUBLANES)
        for b in range(nbl):
            out_ref[rows, _lanes(b)] = stage_ref[b, pl.ds(t, SUBLANES, stride=seg), :]
        return carry

    lax.fori_loop(0, seg, body, 0, unroll=True)


def _from_segment_order(v, stage_ref, out_ref, seg):
    nbl = v.shape[1] // 128
    for b in range(nbl):
        stage_ref[b] = v[:, b * 128:(b + 1) * 128]
    for s in range(SUBLANES):
        def body(k, carry, s=s):
            rows = pl.ds(pl.multiple_of(s * seg + k * SUBLANES, SUBLANES), SUBLANES)
            for b in range(nbl):
                out_ref[rows, _lanes(b)] = stage_ref[b, pl.ds(k * SUBLANES * SUBLANES + s, SUBLANES,
                                                              stride=SUBLANES), :]
            return carry

        lax.fori_loop(0, seg // SUBLANES, body, 0, unroll=True)


def _tile_scan(xr_ref, xi_ref, lam_ref, car_ref, cai_ref, cn_r, cn_i, blocks, seg, reverse):
    shape = (SUBLANES, 128)
    lrs = [jnp.broadcast_to(lam_ref[0:1, _lanes(j)], shape) for j in blocks]
    lis = [jnp.broadcast_to(lam_ref[1:2, _lanes(j)], shape) for j in blocks]

    def step(k, carry):
        t = seg - 1 - k if reverse else k
        rows = pl.ds(pl.multiple_of(t * SUBLANES, SUBLANES), SUBLANES)
        out = []
        for n, j in enumerate(blocks):
            cr, ci = carry[2 * n], carry[2 * n + 1]
            nr = lrs[n] * cr - lis[n] * ci + xr_ref[rows, _lanes(j)]
            ni = lrs[n] * ci + lis[n] * cr + xi_ref[rows, _lanes(j)]
            xr_ref[rows, _lanes(j)] = nr
            xi_ref[rows, _lanes(j)] = ni
            out += [nr, ni]
        return tuple(out)

    zero = jnp.zeros(shape, F32)
    fin = lax.fori_loop(0, seg, step, (zero,) * (2 * len(blocks)), unroll=True)
    for n, j in enumerate(blocks):
        ls = _lanes(j)
        fr, fi = fin[2 * n], fin[2 * n + 1]
        sr, si = lam_ref[2:3, ls], lam_ref[3:4, ls]
        pr, pi = car_ref[:, ls], cai_ref[:, ls]
        for s in (reversed(range(SUBLANES)) if reverse else range(SUBLANES)):
            cn_r[s:s + 1, ls] = pr
            cn_i[s:s + 1, ls] = pi
            pr, pi = fr[s:s + 1, :] + sr * pr - si * pi, fi[s:s + 1, :] + sr * pi + si * pr
        car_ref[:, ls] = pr
        cai_ref[:, ls] = pi


def _s5_fwd(proj, lam_rows, p3_re, p3_im, bbr4, bbi4, crt4, cit4, d_row, t_len, tb):
    seg = tb // SUBLANES

    def body(u_ref, lam_ref, p3r_ref, p3i_ref, bbr_ref, bbi_ref, crt_ref, cit_ref, d_ref,
             hr_ref, hi_ref, ypre_ref, ys_ref, car_ref, cai_ref, cn_r, cn_i, stage_ref, us_ref, yseg_ref):
        @pl.when(pl.program_id(0) == 0)
        def _():
            car_ref[...] = jnp.zeros_like(car_ref)
            cai_ref[...] = jnp.zeros_like(cai_ref)

        _to_segment_order(u_ref[...], stage_ref, us_ref, seg)
        u = us_ref[...]
        for i in range(S5_BLOCKS):
            ui = u[:, i * S5_BW:(i + 1) * S5_BW]
            hr_ref[:, pl.ds(i * S5_BL, S5_BL)] = _dot(ui, bbr_ref[i])
            hi_ref[:, pl.ds(i * S5_BL, S5_BL)] = _dot(ui, bbi_ref[i])
        for lc in range(S5_LANE_BLOCKS // S5_SCAN_BLOCKS):
            blocks = range(lc * S5_SCAN_BLOCKS, (lc + 1) * S5_SCAN_BLOCKS)
            _tile_scan(hr_ref, hi_ref, lam_ref, car_ref, cai_ref, cn_r, cn_i, blocks, seg, False)
            crs = [cn_r[:, _lanes(j)] for j in blocks]
            cis = [cn_i[:, _lanes(j)] for j in blocks]

            def fix(t, carry, blocks=blocks, crs=crs, cis=cis):
                rows = pl.ds(pl.multiple_of(t * SUBLANES, SUBLANES), SUBLANES)
                for n, j in enumerate(blocks):
                    pr, pi = p3r_ref[t, :, _lanes(j)], p3i_ref[t, :, _lanes(j)]
                    hr_ref[rows, _lanes(j)] += pr * crs[n] - pi * cis[n]
                    hi_ref[rows, _lanes(j)] += pr * cis[n] + pi * crs[n]
                return carry

            lax.fori_loop(0, seg, fix, 0, unroll=True)
        for i in range(S5_BLOCKS):
            ws = pl.ds(i * S5_BW, S5_BW)
            bl = pl.ds(i * S5_BL, S5_BL)
            yseg_ref[:, ws] = (_dot(hr_ref[:, bl], crt_ref[i]) - _dot(hi_ref[:, bl], cit_ref[i])
                               + d_ref[:, ws] * u[:, i * S5_BW:(i + 1) * S5_BW])
        _from_segment_order(yseg_ref[...], stage_ref, ypre_ref, seg)
        ys_ref[...] = jax.nn.gelu(ypre_ref[...], approximate=True).astype(ys_ref.dtype)

    whole = pl.BlockSpec(memory_space=pltpu.VMEM)
    return pl.pallas_call(
        body, name="s5_fwd", grid=(t_len // tb,),
        in_specs=[pl.BlockSpec((tb, S5_WIDTH), lambda i: (i, 4096 // S5_WIDTH))] + [whole] * 8,
        out_specs=[pl.BlockSpec((tb, S5_LANES), lambda i: (i, 0)), pl.BlockSpec((tb, S5_LANES), lambda i: (i, 0)),
                   pl.BlockSpec((tb, S5_WIDTH), lambda i: (i, 0)), pl.BlockSpec((tb, S5_WIDTH), lambda i: (i, 0))],
        out_shape=[jax.ShapeDtypeStruct((t_len, S5_LANES), F32), jax.ShapeDtypeStruct((t_len, S5_LANES), F32),
                   jax.ShapeDtypeStruct((t_len, S5_WIDTH), F32), jax.ShapeDtypeStruct((t_len, S5_WIDTH), MXU_DTYPE)],
        scratch_shapes=[pltpu.VMEM((1, S5_LANES), F32), pltpu.VMEM((1, S5_LANES), F32),
                        pltpu.VMEM((SUBLANES, S5_LANES), F32), pltpu.VMEM((SUBLANES, S5_LANES), F32),
                        pltpu.VMEM((S5_WIDTH // 128, tb, 128), F32), pltpu.VMEM((tb, S5_WIDTH), F32),
                        pltpu.VMEM((tb, S5_WIDTH), F32)],
        compiler_params=_params("arbitrary"))(proj, lam_rows, p3_re, p3_im, bbr4, bbi4, crt4, cit4, d_row)


def _s5_bwd(dgelu, y_pre, proj, h_re, h_im, lam_rows, p3_re, p3_im, bbr4, bbi4, cr4, ci4, d_row, dproj, t_len, tb):
    seg = tb // SUBLANES
    nb = t_len // tb

    def body(dg_ref, yp_ref, u_ref, hr_ref, hi_ref, lam_ref, p3r_ref, p3i_ref, bbr_ref, bbi_ref, cr_ref, ci_ref,
             d_ref, _, du_ref, dbbr_ref, dbbi_ref, dcr_ref, dci_ref, dd_ref, dlam_ref,
             gr_ref, gi_ref, car_ref, cai_ref, cn_r, cn_i, stage_ref, us_ref, dys_ref, duseg_ref):
        @pl.when(pl.program_id(0) == 0)
        def _():
            for ref in (car_ref, cai_ref, dbbr_ref, dbbi_ref, dcr_ref, dci_ref, dd_ref, dlam_ref):
                ref[...] = jnp.zeros_like(ref)

        _to_segment_order(u_ref[...], stage_ref, us_ref, seg)
        _to_segment_order(dg_ref[...] * _dgelu(yp_ref[...]), stage_ref, dys_ref, seg)
        u, dy = us_ref[...], dys_ref[...]
        for i in range(S5_BLOCKS):
            dyi = dy[:, i * S5_BW:(i + 1) * S5_BW]
            gr_ref[:, pl.ds(i * S5_BL, S5_BL)] = _dot(dyi, cr_ref[i])
            gi_ref[:, pl.ds(i * S5_BL, S5_BL)] = -_dot(dyi, ci_ref[i])
        for lc in range(S5_LANE_BLOCKS // S5_SCAN_BLOCKS):
            blocks = range(lc * S5_SCAN_BLOCKS, (lc + 1) * S5_SCAN_BLOCKS)
            _tile_scan(gr_ref, gi_ref, lam_ref, car_ref, cai_ref, cn_r, cn_i, blocks, seg, True)
            crs = [cn_r[:, _lanes(j)] for j in blocks]
            cis = [cn_i[:, _lanes(j)] for j in blocks]

            def fix(k, carry, blocks=blocks, crs=crs, cis=cis):
                t = seg - 1 - k
                rows = pl.ds(pl.multiple_of(t * SUBLANES, SUBLANES), SUBLANES)
                out = []
                for n, j in enumerate(blocks):
                    nr, ni, slr, sli = carry[4 * n:4 * n + 4]
                    pr, pi = p3r_ref[t, :, _lanes(j)], p3i_ref[t, :, _lanes(j)]
                    g_r = gr_ref[rows, _lanes(j)] + pr * crs[n] - pi * cis[n]
                    g_i = gi_ref[rows, _lanes(j)] + pr * cis[n] + pi * crs[n]
                    gr_ref[rows, _lanes(j)] = g_r
                    gi_ref[rows, _lanes(j)] = g_i
                    hr, hi = hr_ref[rows, _lanes(j)], hi_ref[rows, _lanes(j)]
                    out += [g_r, g_i, slr + nr * hr + ni * hi, sli + ni * hr - nr * hi]
                return tuple(out)

            zero = jnp.zeros((SUBLANES, 128), F32)
            init = []
            for n in range(len(blocks)):
                init += [crs[n], cis[n], zero, zero]
            fin = lax.fori_loop(0, seg, fix, tuple(init), unroll=True)
            for n, j in enumerate(blocks):
                dlam_ref[0:1, _lanes(j)] += jnp.sum(fin[4 * n + 2], axis=0, keepdims=True)
                dlam_ref[1:2, _lanes(j)] += jnp.sum(fin[4 * n + 3], axis=0, keepdims=True)
        for i in range(S5_BLOCKS):
            ws = pl.ds(i * S5_BW, S5_BW)
            bl = pl.ds(i * S5_BL, S5_BL)
            ui, dyi = u[:, i * S5_BW:(i + 1) * S5_BW], dy[:, i * S5_BW:(i + 1) * S5_BW]
            gr, gi = gr_ref[:, bl], gi_ref[:, bl]
            duseg_ref[:, ws] = _dot(gr, bbr_ref[i], _NT) + _dot(gi, bbi_ref[i], _NT) + d_ref[:, ws] * dyi
            dbbr_ref[i] += _dot(ui, gr, _TN)
            dbbi_ref[i] += _dot(ui, gi, _TN)
            dcr_ref[i] += _dot(hr_ref[:, bl], dyi, _TN)
            dci_ref[i] -= _dot(hi_ref[:, bl], dyi, _TN)
        dd_ref[...] += jnp.sum(dy * u, axis=0, keepdims=True)
        _from_segment_order(duseg_ref[...], stage_ref, duseg_ref, seg)
        du_ref[...] = duseg_ref[...].astype(du_ref.dtype)

    whole = pl.BlockSpec(memory_space=pltpu.VMEM)
    rev = lambda i: (nb - 1 - i, 0)
    const3 = lambda i: (0, 0, 0)
    return pl.pallas_call(
        body, name="s5_bwd", grid=(nb,),
        in_specs=[pl.BlockSpec((tb, S5_WIDTH), rev), pl.BlockSpec((tb, S5_WIDTH), rev),
                  pl.BlockSpec((tb, S5_WIDTH), lambda i: (nb - 1 - i, 4096 // S5_WIDTH)),
                  pl.BlockSpec((tb, S5_LANES), rev), pl.BlockSpec((tb, S5_LANES), rev)] + [whole] * 8
                 + [pl.BlockSpec(memory_space=pl.ANY)],
        out_specs=[pl.BlockSpec((tb, S5_WIDTH), lambda i: (nb - 1 - i, 4096 // S5_WIDTH)),
                   pl.BlockSpec((S5_BLOCKS, S5_BW, S5_BL), const3), pl.BlockSpec((S5_BLOCKS, S5_BW, S5_BL), const3),
                   pl.BlockSpec((S5_BLOCKS, S5_BL, S5_BW), const3), pl.BlockSpec((S5_BLOCKS, S5_BL, S5_BW), const3),
                   pl.BlockSpec((1, S5_WIDTH), lambda i: (0, 0)), pl.BlockSpec((2, S5_LANES), lambda i: (0, 0))],
        out_shape=[jax.ShapeDtypeStruct((t_len, IN_COLS), dproj.dtype),
                   jax.ShapeDtypeStruct((S5_BLOCKS, S5_BW, S5_BL), F32),
                   jax.ShapeDtypeStruct((S5_BLOCKS, S5_BW, S5_BL), F32),
                   jax.ShapeDtypeStruct((S5_BLOCKS, S5_BL, S5_BW), F32),
                   jax.ShapeDtypeStruct((S5_BLOCKS, S5_BL, S5_BW), F32),
                   jax.ShapeDtypeStruct((1, S5_WIDTH), F32), jax.ShapeDtypeStruct((2, S5_LANES), F32)],
        scratch_shapes=[pltpu.VMEM((tb, S5_LANES), F32), pltpu.VMEM((tb, S5_LANES), F32),
                        pltpu.VMEM((1, S5_LANES), F32), pltpu.VMEM((1, S5_LANES), F32),
                        pltpu.VMEM((SUBLANES, S5_LANES), F32), pltpu.VMEM((SUBLANES, S5_LANES), F32),
                        pltpu.VMEM((S5_WIDTH // 128, tb, 128), F32), pltpu.VMEM((tb, S5_WIDTH), F32),
                        pltpu.VMEM((tb, S5_WIDTH), F32), pltpu.VMEM((tb, S5_WIDTH), F32)],
        input_output_aliases={13: 0},
        compiler_params=_params("arbitrary"))(dgelu, y_pre, proj, h_re, h_im, lam_rows, p3_re, p3_im, bbr4, bbi4,
                                              cr4, ci4, d_row, dproj)


def _block_diag(per_group):
    g8 = S5_GROUPS // S5_BLOCKS
    eye = jnp.eye(g8, dtype=bool)[None, :, None, :, None]
    dense = jnp.where(eye, per_group.reshape(S5_BLOCKS, g8, S5_GROUP, 1, S5_STATE), 0.0)
    return dense.reshape(S5_BLOCKS, S5_BW, S5_BL)


def _diag_blocks(dense):
    g8 = S5_GROUPS // S5_BLOCKS
    ar = jnp.arange(g8)
    d5 = dense.reshape(S5_BLOCKS, g8, S5_GROUP, g8, S5_STATE)
    return d5[:, ar, :, ar, :].transpose(1, 0, 2, 3).reshape(S5_GROUPS, S5_GROUP, S5_STATE)


def _hg_gate_bwd(da, o, g, gn):
    dos, dgs, dgns = [], [], []
    for h in range(HG_HEADS):
        sl = slice(h * HG_DIM, (h + 1) * HG_DIM)
        oh, gh, dah, gnh = o[:, sl], g[:, sl], da[:, sl], gn[:, sl]
        rr = lax.rsqrt(jnp.mean(oh * oh, axis=-1, keepdims=True) + NORM_EPS)
        sg = _sig(gh)
        dgs.append(dah * (oh * rr * gnh) * _dsilu(gh, sg))
        don = dah * (gh * sg)
        t = don * gnh
        dos.append(rr * t - oh * (rr * rr * rr) * jnp.mean(t * oh, axis=-1, keepdims=True))
        dgns.append(jnp.sum(don * oh * rr, axis=0, keepdims=True))
    return jnp.concatenate(dos, axis=1), jnp.concatenate(dgs, axis=1), jnp.concatenate(dgns, axis=1)


MIX_BWD_COLS = ((3072, 1024), (4608, 512), (5120, 1024), (6144, 1024))


def _mix_bwd(dgl, h1, dh2, act_hg, ys2, ys_gelu, proj, o_hg, g2, ghn, b_glu, w, t_len, tm):
    nb = t_len // tm

    def body(dgl_ref, h1_ref, dh2_ref, act_ref, ys2_ref, ysg_ref, ghg_ref, z_ref, gh_ref, gs_ref, o_ref, g2_ref, gn_ref,
             bglu_ref, wg_ref, wo_ref, ws5_ref, whg_ref, wglu_ref,
             dh1_ref, dyh_ref, dys_ref, dglu_ref, dgelu_ref, do_ref, dg2_ref, dbglu_ref, dgn_ref, dproj_ref,
             st0, st1, st2, st3, sems):
        i = pl.program_id(0)
        stages = (st0, st1, st2, st3)

        def writes(step):
            rows = pl.ds(pl.multiple_of(step * tm, tm), tm)
            return [pltpu.make_async_copy(st, dproj_ref.at[rows, pl.ds(c0, wd)], sems.at[k])
                    for k, (st, (c0, wd)) in enumerate(zip(stages, MIX_BWD_COLS))]

        @pl.when(i > 0)
        def _():
            for cp in writes(i - 1):
                cp.wait()

        @pl.when(i == 0)
        def _():
            for ref in (dg2_ref, dbglu_ref, dgn_ref):
                ref[...] = jnp.zeros_like(ref)

        dx, dg2 = _rms_bwd(_dot(dgl_ref[...], wg_ref[...], _NT), h1_ref[...], g2_ref[...])
        dh1 = dh2_ref[...] + dx
        dh1_ref[...] = dh1
        dg2_ref[...] += dg2
        dm = _dot(dh1, wo_ref[...], _NT)
        sh, ss = _sig(gh_ref[...]), _sig(gs_ref[...])
        dyh, dys = _mx(dm * sh), _mx(dm * ss)
        dyh_ref[...] = dyh
        dys_ref[...] = dys
        st2[...] = (dm * _dot(act_ref[...], whg_ref[...]) * sh * (1.0 - sh)).astype(st2.dtype)
        st3[...] = (dm * _dot(ys2_ref[...], ws5_ref[...]) * ss * (1.0 - ss)).astype(st3.dtype)
        dys2 = _dot(dys, ws5_ref[...], _NT)
        gl_, z = _dot(ysg_ref[...], wglu_ref[...]) + bglu_ref[...], z_ref[...]
        a, b = gl_[:, :S5_WIDTH], gl_[:, S5_WIDTH:]
        sb, sz = _sig(b), _sig(z)
        silu = z * sz
        dglu = jnp.concatenate([dys2 * sb * silu, dys2 * a * silu * sb * (1.0 - sb)], axis=1)
        st1[...] = (dys2 * a * sb * _dsilu(z, sz)).astype(st1.dtype)
        dbglu_ref[...] += jnp.sum(dglu, axis=0, keepdims=True)
        dglu_ref[...] = _mx(dglu)
        dgelu_ref[...] = _dot(dglu, wglu_ref[...], _NT)
        d_o, dg, dgn = _hg_gate_bwd(_dot(dyh, whg_ref[...], _NT), o_ref[...], ghg_ref[...], gn_ref[...])
        do_ref[...] = d_o.astype(do_ref.dtype)
        st0[...] = dg.astype(st0.dtype)
        dgn_ref[...] += dgn
        for cp in writes(i):
            cp.start()

        @pl.when(i == nb - 1)
        def _():
            for cp in writes(i):
                cp.wait()

    tile = lambda wd, cb=0: pl.BlockSpec((tm, wd), functools.partial(lambda i, cb: (i, cb), cb=cb))
    row = lambda wd: pl.BlockSpec((1, wd), lambda i: (0, 0))
    whole = pl.BlockSpec(memory_space=pltpu.VMEM)
    return pl.pallas_call(
        body, name="mix_bwd", grid=(nb,),
        in_specs=[tile(1024), tile(1024), tile(1024), tile(1024), tile(512), tile(512), tile(1024, 3),
                  tile(512, 4608 // 512), tile(1024, 5), tile(1024, 6), tile(1024), row(1024), row(1024), row(1024)]
                 + [whole] * 5,
        out_specs=[tile(1024), tile(1024), tile(1024), tile(1024), tile(512), tile(1024), row(1024), row(1024),
                   row(1024), _HBM],
        out_shape=[jax.ShapeDtypeStruct((t_len, 1024), F32), jax.ShapeDtypeStruct((t_len, 1024), MXU_DTYPE),
                   jax.ShapeDtypeStruct((t_len, 1024), MXU_DTYPE), jax.ShapeDtypeStruct((t_len, 1024), MXU_DTYPE),
                   jax.ShapeDtypeStruct((t_len, 512), F32), jax.ShapeDtypeStruct((t_len, 1024), MXU_DTYPE),
                   jax.ShapeDtypeStruct((1, 1024), F32), jax.ShapeDtypeStruct((1, 1024), F32),
                   jax.ShapeDtypeStruct((1, 1024), F32), jax.ShapeDtypeStruct((t_len, IN_COLS), MXU_DTYPE)],
        scratch_shapes=[pltpu.VMEM((tm, wd), MXU_DTYPE) for _, wd in MIX_BWD_COLS] + [pltpu.SemaphoreType.DMA((4,))],
        compiler_params=_params("arbitrary"))(dgl, h1, dh2, act_hg, ys2, ys_gelu, proj, proj, proj, proj, o_hg, g2, ghn,
                                              b_glu, w["w_ple_gate"], w["w_out"], w["w_o_s5"], w["w_o_hg"],
                                              w["w_glu"])


def _local_step(x, p, target, w, sm, comm=None):
    t_len = x.shape[0]
    tm = min(256, t_len)
    tmm = min(512, t_len)
    tm_in = min(1024, t_len)
    tk = min(2048, t_len)
    tb_hg = min(256, t_len)
    tb_s5 = min(512, t_len)
    g1, g2, g3, ghn = sm["norm_g"], sm["ple_norm_g"], sm["final_norm_g"].reshape(1, D_MODEL), sm["hg_norm_g"]

    def rms_in(xv, g):
        return xv * lax.rsqrt(jnp.mean(xv * xv, axis=-1, keepdims=True) + NORM_EPS) * g

    in_shard = IN_COLS // N_CHIPS
    if comm is None:
        w_in = w["w_in"]
        proj, u = _mm_nn("mm_in", x, w_in, tm_in, in_shard, prologue=rms_in, consts=[g1])
    else:
        proj, u, w_in = comm.input_projection(x, g1, rms_in, tm_in)

    lanes = lambda a: a.reshape(1, S5_LANES)
    a_re, a_im = lanes(sm["s5_a_re"]), lanes(sm["s5_a_im"])
    ldt = lanes(jnp.broadcast_to(sm["s5_log_dt"].reshape(S5_GROUPS, 1), (S5_GROUPS, S5_STATE)))
    to_t = lambda b: b.reshape(S5_GROUPS, S5_STATE, S5_GROUP).transpose(2, 0, 1).reshape(S5_GROUP, S5_LANES)
    b_re_t, b_im_t = to_t(sm["s5_b_re"]), to_t(sm["s5_b_im"])
    scan_fwd, scan_rev, bbr_t, bbi_t = _s5_prep(a_re, a_im, ldt, b_re_t, b_im_t, tb_s5 // SUBLANES)
    from_t = lambda b: b.reshape(S5_GROUP, S5_GROUPS, S5_STATE).transpose(1, 0, 2)
    bbr_bd = _block_diag(from_t(bbr_t)).astype(MXU_DTYPE)
    bbi_bd = _block_diag(from_t(bbi_t)).astype(MXU_DTYPE)
    cr_bd = _block_diag(sm["s5_c_re"].reshape(S5_GROUPS, S5_GROUP, S5_STATE)).astype(MXU_DTYPE)
    ci_bd = _block_diag(sm["s5_c_im"].reshape(S5_GROUPS, S5_GROUP, S5_STATE)).astype(MXU_DTYPE)
    d_row = sm["s5_d"].reshape(1, S5_WIDTH)
    if comm is None:
        o_hg, act_hg, s_prev = _hgrn2_fwd(proj, sm["hg_lb"], ghn, t_len, tb_hg)
    else:
        o_hg, act_hg, s_prev, landed = _hgrn2_fwd(proj, sm["hg_lb"], ghn, t_len, tb_hg, riding=comm.gather_rest())
        w = comm.rest_weights(landed)
    h_re, h_im, y_pre, ys_gelu = _s5_fwd(proj, *scan_fwd, bbr_bd, bbi_bd,
                                          cr_bd.transpose(0, 2, 1), ci_bd.transpose(0, 2, 1), d_row, t_len, tb_s5)
    def mix_f(act, ysg, z, gh, gs, xv, w_glu, b_glu, w_o_hg, w_o_s5, w_out):
        yh = _dot(act, w_o_hg)
        gl_ = _dot(ysg, w_glu) + b_glu
        a, b = gl_[:, :S5_WIDTH], gl_[:, S5_WIDTH:]
        ys2_ = (a * _sig(b) * (z * _sig(z))).astype(MXU_DTYPE)
        ys = _dot(ys2_, w_o_s5)
        mg = (_sig(gh) * yh + _sig(gs) * ys).astype(MXU_DTYPE)
        return (ys2_, mg, xv + _dot(mg, w_out))

    ys2, merged, h1 = _rowwise(
        "mix_out", mix_f, t_len, tmm,
        [(act_hg, 1024, 0), (ys_gelu, 512, 0), (proj, 512, 4608 // 512), (proj, 1024, 5), (proj, 1024, 6),
         (x, 1024, 0)], [w["w_glu"], sm["b_glu"], w["w_o_hg"], w["w_o_s5"], w["w_out"]],
        [(512, MXU_DTYPE), (1024, MXU_DTYPE), (1024, F32)], ring=True)

    def head_f(h1v, pv, tgt, g_ple, g, w_ple, w_gate):
        r2 = lax.rsqrt(jnp.mean(h1v * h1v, axis=-1, keepdims=True) + NORM_EPS)
        n2_ = (h1v * r2 * g_ple).astype(MXU_DTYPE)
        glv, pev = _dot(n2_, w_gate), _dot(pv, w_ple)
        gate = _sig(glv)
        h2 = h1v + pev * gate
        r = lax.rsqrt(jnp.mean(h2 * h2, axis=-1, keepdims=True) + NORM_EPS)
        e = h2 * r * g - tgt
        loss = 0.5 * jnp.sum(jnp.mean(e * e, axis=-1, keepdims=True), axis=0, keepdims=True)
        dy = e * (1.0 / D_MODEL)
        dg = jnp.sum(dy * h2 * r, axis=0, keepdims=True)
        t = dy * g
        dh2 = r * t - h2 * (r * r * r) * jnp.mean(t * h2, axis=-1, keepdims=True)
        dpe, dgl_ = _mx(dh2 * gate), _mx(dh2 * pev * gate * (1.0 - gate))
        return (dh2, dgl_, jnp.broadcast_to(loss, (1, 128)), dg, _dot(pv, dpe, _TN), _dot(n2_, dgl_, _TN))

    gb = {}
    dh2, dgl, loss_row, d_g3, gb["w_ple"], gb["w_ple_gate"] = _rowwise(
        "ple_loss_head", head_f, t_len, tmm, [(h1, 1024, 0), (p, 256, 0), (target, 1024, 0)],
        [g2, g3, w["w_ple"], w["w_ple_gate"]], [(1024, F32), (1024, MXU_DTYPE)],
        accs=[(1, 128), (1, 1024), (256, 1024), (1024, 1024)])

    dh1, dy_hg, dy_s5, dglu, dgelu, d_o, d_g2, d_bglu, d_ghn, dproj = _mix_bwd(
        dgl, h1, dh2, act_hg, ys2, ys_gelu, proj, o_hg, g2, ghn, sm["b_glu"], w, t_len, tm)
    gb["w_out"] = _mm_tn("mm_d_w_out", merged, dh1, tk, 1024)
    gb["w_o_s5"] = _mm_tn("mm_d_w_o_s5", ys2, dy_s5, tk, 1024)
    gb["w_glu"] = _mm_tn("mm_d_w_glu", ys_gelu, dglu, tk, 1024)
    dproj, d_bbr, d_bbi, d_crt, d_cit, d_d, d_lam = _s5_bwd(dgelu, y_pre, proj, h_re, h_im,
                                                            *scan_rev, bbr_bd, bbi_bd, cr_bd,
                                                            ci_bd, d_row, dproj, t_len, tb_s5)
    to_t3 = lambda b: b.transpose(1, 0, 2).reshape(S5_GROUP, S5_LANES)
    d_are, d_aim, d_ldt, d_br_t, d_bi_t = _s5_prep_bwd(a_re, a_im, ldt, b_re_t, b_im_t, d_lam,
                                                       to_t3(_diag_blocks(d_bbr)), to_t3(_diag_blocks(d_bbi)))
    gb["w_o_hg"] = _mm_tn("mm_d_w_o_hg", act_hg, dy_hg, tk, 1024)
    if comm is None:
        dproj, d_lb = _hgrn2_bwd(proj, d_o, s_prev, sm["hg_lb"], dproj, t_len, tb_hg)
    else:
        rest_grads = _pack_rest_full(gb)
        dproj, d_lb, rest_theirs = _hgrn2_bwd(proj, d_o, s_prev, sm["hg_lb"], dproj, t_len, tb_hg,
                                               riding=comm.swap(rest_grads))

    def in_b(duv, xv, dh, g):
        dx, dg = _rms_bwd(duv, xv, g)
        return (dh + dx, dg)

    in_args = ("mm_d_u_rms_in_bwd", dproj, w_in, tm_in, in_shard, in_b, [(x, 1024, 0), (dh1, 1024, 0)], [g1],
               [(1024, F32)])
    if comm is None:
        gb["w_in"] = _mm_tn("mm_d_w_in", u, dproj, tk, in_shard, col_shards=True)
        grad_x, d_g1 = _mm_nt_then(*in_args, accs=[(1, 1024)])
    else:
        gb["w_in"], landed = _mm_tn("mm_d_w_in", u, dproj, tk, in_shard, col_shards=True,
                                    riding=comm.scatter("rest", rest_grads, rest_theirs))
        comm.landed["rest"] = landed
        grad_x, d_g1, landed = _mm_nt_then(*in_args, accs=[(1, 1024)], riding=comm.scatter(
            "in", gb["w_in"].reshape(N_CHIPS, 2, D_MODEL // 2, in_shard)))
        comm.landed["in"] = landed

    back_t = lambda b: b.reshape(S5_GROUP, S5_GROUPS, S5_STATE).transpose(1, 2, 0).reshape(1, S5_GROUPS, S5_STATE,
                                                                                           S5_GROUP)
    gs = {
        "norm_g": d_g1, "hg_lb": d_lb, "hg_norm_g": d_ghn,
        "s5_a_re": d_are.reshape(1, S5_GROUPS, S5_STATE), "s5_a_im": d_aim.reshape(1, S5_GROUPS, S5_STATE),
        "s5_log_dt": d_ldt[0:1, :S5_GROUPS],
        "s5_b_re": back_t(d_br_t), "s5_b_im": back_t(d_bi_t),
        "s5_c_re": _diag_blocks(d_crt.transpose(0, 2, 1)).reshape(1, S5_GROUPS, S5_GROUP, S5_STATE),
        "s5_c_im": _diag_blocks(d_cit.transpose(0, 2, 1)).reshape(1, S5_GROUPS, S5_GROUP, S5_STATE),
        "s5_d": d_d.reshape(1, S5_GROUPS, S5_GROUP), "b_glu": d_bglu, "ple_norm_g": d_g2,
        "final_norm_g": d_g3.reshape(D_MODEL),
    }
    return loss_row, grad_x, gb, gs


def _shard_shape(name):
    r, c = BIG_SHAPE[name]
    return (r, c // N_CHIPS) if name in BIG_COL_SHARDED else (r // N_CHIPS, c)


def _pack_small(parts, last):
    flat = jnp.concatenate([parts[n].reshape(-1) for n in SMALL] + [last.reshape(-1)])
    return jnp.pad(flat, (0, SMALL_ROWS * PACK_W - flat.shape[0])).reshape(SMALL_ROWS, PACK_W)


def _unpack_small(packed):
    flat, out, off = packed.reshape(-1), {}, 0
    for n in SMALL:
        size = 1
        for d in SMALL_SHAPE[n]:
            size *= d
        out[n] = flat[off:off + size].reshape(SMALL_SHAPE[n])
        off += size
    return out, flat[off]


def _place():
    x, y, c = lax.axis_index("x"), lax.axis_index("y"), lax.axis_index("c")
    return x, y, c, [(1 - x, y), (x, 1 - y), (1 - x, 1 - y)]


def _remote(src, dst, send_sems, recv_sems, k, to):
    return pltpu.make_async_remote_copy(src_ref=src, dst_ref=dst, send_sem=send_sems.at[k], recv_sem=recv_sems.at[k],
                                        device_id=to, device_id_type=MESH)


REST = tuple(n for n in BIG if n != "w_in")
REST_ROWS = sum(BIG_SHAPE[n][0] * BIG_SHAPE[n][1] for n in REST) // (N_CHIPS * PACK_W)
IN_SHARD = IN_COLS // N_CHIPS
IN_TILE, REST_TILE = 256, 272


def _pack_rest(parts):
    return jnp.concatenate([parts[n].reshape(-1, PACK_W) for n in REST], axis=0)


def _unpack_rest(packed):
    out, off = {}, 0
    for n in REST:
        r, c = _shard_shape(n)
        rows = r * c // PACK_W
        out[n] = packed[off:off + rows].reshape(1, r, c)
        off += rows
    return out


def _unpack_rest_full(gathered):
    out, off = {}, 0
    for n in REST:
        r, c = _shard_shape(n)
        rows = r * c // PACK_W
        sh = gathered[:, off:off + rows].reshape(N_CHIPS, r, c)
        out[n] = sh.transpose(1, 0, 2).reshape(BIG_SHAPE[n]) if n in BIG_COL_SHARDED else sh.reshape(BIG_SHAPE[n])
        off += rows
    return out


def _pack_rest_full(full):
    parts = []
    for n in REST:
        r, c = _shard_shape(n)
        g = full[n]
        sh = g.reshape(BIG_SHAPE[n][0], N_CHIPS, c).transpose(1, 0, 2) if n in BIG_COL_SHARDED else g
        parts.append(sh.reshape(N_CHIPS, r * c // PACK_W, PACK_W))
    return jnp.concatenate(parts, axis=1).reshape(N_CHIPS, 2, REST_ROWS // 2, PACK_W)


def _swap_halves(pgs, name="exchange_halves"):
    n = len(pgs)

    def body(*refs):
        pg_refs, out_refs, (send_sems, recv_sems) = refs[:n], refs[n:2 * n], refs[2 * n:]
        x, y, c, _ = _place()
        cps = [_remote(pg_ref.at[j, 1 - c], out_ref.at[j], send_sems, recv_sems, N_CHIPS * g + j, (x, y, 1 - c))
               for g, (pg_ref, out_ref) in enumerate(zip(pg_refs, out_refs)) for j in range(N_CHIPS)]
        for cp in cps:
            cp.start()
        for cp in cps:
            cp.wait()

    return pl.pallas_call(
        body, name=name, in_specs=[_HBM] * n, out_specs=[_HBM] * n,
        out_shape=[jax.ShapeDtypeStruct((N_CHIPS,) + pg.shape[2:], pg.dtype) for pg in pgs],
        scratch_shapes=[pltpu.SemaphoreType.DMA((N_CHIPS * n,)), pltpu.SemaphoreType.DMA((N_CHIPS * n,))])(*pgs)


def _share_halves(gs):
    n = len(gs)

    def body(*refs):
        g_refs, out_refs, (send_sems, recv_sems) = refs[:n], refs[n:2 * n], refs[2 * n:]
        x, y, c, _ = _place()
        cps = [_remote(g_ref, out_ref.at[c], send_sems, recv_sems, g, (x, y, 1 - c))
               for g, (g_ref, out_ref) in enumerate(zip(g_refs, out_refs))]
        for cp in cps:
            cp.start()
        for g, (g_ref, out_ref) in enumerate(zip(g_refs, out_refs)):
            _remote(g_ref, out_ref.at[1 - c], send_sems, recv_sems, g, (x, y, 1 - c)).wait_recv()
        for cp in cps:
            cp.wait_send()

    return pl.pallas_call(
        body, name="share_half", in_specs=[_HBM] * n, out_specs=[_HBM] * n,
        out_shape=[jax.ShapeDtypeStruct((2,) + g.shape, g.dtype) for g in gs],
        scratch_shapes=[pltpu.SemaphoreType.DMA((n,)), pltpu.SemaphoreType.DMA((n,))])(*gs)


def _pair_sum(name, pg, theirs, c, tile):
    _, _, rows, width = pg.shape

    def body(c_ref, a_ref, b_ref, o_ref):
        o_ref[...] = (a_ref[...] + b_ref[...]).astype(o_ref.dtype)

    return pl.pallas_call(
        body, name=name,
        grid_spec=pltpu.PrefetchScalarGridSpec(
            num_scalar_prefetch=1, grid=(N_CHIPS, rows // tile),
            in_specs=[pl.BlockSpec((None, None, tile, width), lambda j, i, c_ref: (j, c_ref[0], i, 0)),
                      pl.BlockSpec((None, tile, width), lambda j, i, c_ref: (j, i, 0))],
            out_specs=pl.BlockSpec((None, tile, width), lambda j, i, c_ref: (j, i, 0))),
        out_shape=jax.ShapeDtypeStruct((N_CHIPS, rows, width), WIRE_DTYPE),
        compiler_params=_params("arbitrary", "arbitrary"))(c.reshape(1), pg, theirs)


def _chip_sum(name, ps, others, k, tile):
    _, rows, width = ps.shape

    def body(k_ref, a_ref, b_ref, o_ref):
        o_ref[...] = ((a_ref[...].astype(F32) + b_ref[0].astype(F32)) + b_ref[1].astype(F32)) + b_ref[2].astype(F32)

    return pl.pallas_call(
        body, name=name,
        grid_spec=pltpu.PrefetchScalarGridSpec(
            num_scalar_prefetch=1, grid=(rows // tile,),
            in_specs=[pl.BlockSpec((None, tile, width), lambda i, k_ref: (k_ref[0], i, 0)),
                      pl.BlockSpec((3, tile, width), lambda i, k_ref: (0, i, 0))],
            out_specs=pl.BlockSpec((tile, width), lambda i, k_ref: (i, 0))),
        out_shape=jax.ShapeDtypeStruct((rows, width), F32),
        compiler_params=_params("arbitrary"))(k.reshape(1), ps, others)


def _mm_in_gathering(x, g1, prologue, in_wire, chip, tm):
    m, k = x.shape
    half, ns = in_wire.shape[1:]
    nrow = m // tm

    def flip(j):
        return jnp.where(j == 1, 2, jnp.where(j == 2, 1, j))

    def body(k_ref, x_ref, g_ref, wire_ref, proj_ref, u_ref, all_ref, kept, b_ref, load_sems, send_sems, recv_sems):
        j, i = pl.program_id(0), pl.program_id(1)
        px, py, c, chips = _place()
        sibling = (px, py, 1 - c)

        def over_ici(r, chip_slot):
            cx, cy = chips[r]
            return _remote(wire_ref.at[c], all_ref.at[chip_slot, c], send_sems, recv_sems, r, (cx, cy, c))

        def to_sibling(r, half_slot):
            cx, cy = chips[r]
            return _remote(all_ref.at[2 * cx + cy, c], all_ref.at[2 * cx + cy, half_slot], send_sems, recv_sems,
                           3 + r, sibling)

        def loads(src, slot):
            return [pltpu.make_async_copy(src.at[h], b_ref.at[slot, pl.ds(h * half, half)], load_sems.at[h])
                    for h in range(2)]

        def shard(r):
            cx, cy = chips[r]
            over_ici(r, 2 * cx + cy).wait_recv()
            if r == 0:
                over_ici(2, 2 * px + py).start()
            to_sibling(r, c).start()
            to_sibling(r, 1 - c).wait_recv()
            return all_ref.at[2 * cx + cy]

        @pl.when((j == 0) & (i == 0))
        def _():
            for r in range(2):
                over_ici(r, 2 * px + py).start()
            for cp in loads(wire_ref, 0):
                cp.start()
            for cp in loads(wire_ref, 0):
                cp.wait()

        @pl.when((j == 1) & (i == 0))
        def _():
            cps = loads(shard(0), 1)
            for cp in cps:
                cp.start()
            for cp in cps:
                cp.wait()

        for nxt in (2, 3):
            @pl.when((j == nxt - 1) & (i == nrow // 2))
            def _(nxt=nxt):
                for cp in loads(shard(nxt - 1), nxt % 2):
                    cp.start()

            @pl.when((j == nxt) & (i == 0))
            def _(nxt=nxt):
                for cp in loads(wire_ref, nxt % 2):
                    cp.wait()

        rows = pl.ds(pl.multiple_of(i * tm, tm), tm)

        @pl.when(j == 0)
        def _():
            tile = _mx(prologue(x_ref[...], g_ref[...]))
            kept[rows, :] = tile
            u_ref[...] = tile

        proj_ref[...] = _dot(kept[rows, :], b_ref[lax.rem(j, 2)])

        @pl.when((j == N_CHIPS - 1) & (i == nrow - 1))
        def _():
            for r in range(3):
                over_ici(r, 2 * px + py).wait_send()
                to_sibling(r, c).wait_send()

    once = lambda j, i, k_ref: (jnp.where(j == 0, i, nrow - 1), 0)
    return pl.pallas_call(
        body, name="mm_in",
        grid_spec=pltpu.PrefetchScalarGridSpec(
            num_scalar_prefetch=1, grid=(N_CHIPS, nrow),
            in_specs=[pl.BlockSpec((tm, k), once), pl.BlockSpec(g1.shape, lambda j, i, k_ref: (0, 0)), _HBM],
            out_specs=[pl.BlockSpec((tm, ns), lambda j, i, k_ref: (i, jnp.bitwise_xor(k_ref[0], flip(j)))),
                       pl.BlockSpec((tm, k), once), _HBM],
            scratch_shapes=[pltpu.VMEM((m, k), MXU_DTYPE), pltpu.VMEM((2, 2 * half, ns), in_wire.dtype),
                            pltpu.SemaphoreType.DMA((2,)), pltpu.SemaphoreType.DMA((6,)),
                            pltpu.SemaphoreType.DMA((6,))]),
        out_shape=[jax.ShapeDtypeStruct((m, N_CHIPS * ns), F32), jax.ShapeDtypeStruct((m, k), MXU_DTYPE),
                   jax.ShapeDtypeStruct((N_CHIPS,) + in_wire.shape, in_wire.dtype)],
        compiler_params=_params("arbitrary", "arbitrary"))(chip.reshape(1), x, g1, in_wire)


class _StepComm:
    TILES = {"in": IN_TILE, "rest": REST_TILE}

    def __init__(self, in_wire, rest_wire, chip, core):
        self.in_wire, self.rest_wire, self.chip, self.core = in_wire, rest_wire, chip, core
        self.sums, self.landed = {}, {}

    def input_projection(self, x, g1, prologue, tm):
        proj, u, shards = _mm_in_gathering(x, g1, prologue, self.in_wire, self.chip, tm)
        shards = lax.dynamic_update_slice(shards, self.in_wire[None], (self.chip, 0, 0, 0))
        return proj, u, shards.reshape(N_CHIPS, D_MODEL, IN_SHARD)

    def gather_rest(self):
        wire = self.rest_wire

        def sends(ins, outs, send_sems, recv_sems):
            (w_ref,), (out_ref,) = ins, outs
            x, y, c, chips = _place()
            return [_remote(w_ref.at[c], out_ref.at[2 * x + y, c], send_sems, recv_sems, 4 * j + 2 * c + to,
                            (cx, cy, to)) for j, (cx, cy) in enumerate(chips) for to in (0, 1)]

        def recvs(ins, outs, send_sems, recv_sems):
            (w_ref,), (out_ref,) = ins, outs
            _, _, c, chips = _place()
            return [_remote(w_ref.at[c], out_ref.at[2 * cx + cy, by], send_sems, recv_sems, 4 * j + 2 * by + c,
                            (cx, cy, by)) for j, (cx, cy) in enumerate(chips) for by in (0, 1)]

        def start(*refs):
            for cp in sends(*refs):
                cp.start()

        def wait(*refs):
            for cp in recvs(*refs):
                cp.wait_recv()
            for cp in sends(*refs):
                cp.wait_send()

        return _Riding((wire,), (jax.ShapeDtypeStruct((N_CHIPS,) + wire.shape, wire.dtype),), 12, start, wait)

    def rest_weights(self, landed):
        full = lax.dynamic_update_slice(landed, self.rest_wire[None], (self.chip, 0, 0, 0))
        return _unpack_rest_full(full.reshape(N_CHIPS, REST_ROWS, PACK_W))

    def swap(self, pg):
        def copies(ins, outs, send_sems, recv_sems):
            (pg_ref,), (out_ref,) = ins, outs
            x, y, c, _ = _place()
            return [_remote(pg_ref.at[j, 1 - c], out_ref.at[j], send_sems, recv_sems, j, (x, y, 1 - c))
                    for j in range(N_CHIPS)]

        def start(*refs):
            for cp in copies(*refs):
                cp.start()

        def wait(*refs):
            for cp in copies(*refs):
                cp.wait()

        return _Riding((pg,), (jax.ShapeDtypeStruct((N_CHIPS,) + pg.shape[2:], pg.dtype),), N_CHIPS, start, wait)

    def scatter(self, group, pg, theirs=None):
        if theirs is None:
            (theirs,) = _swap_halves([pg], "exchange_halves_" + group)
        ps = _pair_sum("sum_pair_" + group, pg, theirs, self.core, self.TILES[group])
        self.sums[group] = ps

        def copies(ins, outs, send_sems, recv_sems):
            (ps_ref,), (out_ref,) = ins, outs
            _, _, c, chips = _place()
            return [_remote(ps_ref.at[2 * cx + cy], out_ref.at[j], send_sems, recv_sems, j, (cx, cy, c))
                    for j, (cx, cy) in enumerate(chips)]

        def start(*refs):
            for cp in copies(*refs):
                cp.start()

        def wait(*refs):
            for cp in copies(*refs):
                cp.wait()

        return _Riding((ps,), (jax.ShapeDtypeStruct((3,) + ps.shape[1:], ps.dtype),), 3, start, wait)

    def reduced(self, group):
        return _chip_sum("sum_chips_" + group, self.sums[group], self.landed[group], self.chip, self.TILES[group])


def _adamw(w, g, m, v):
    m = ADAM_B1 * m + (1.0 - ADAM_B1) * g
    v = ADAM_B2 * v + (1.0 - ADAM_B2) * (g * g)
    m_hat = m / (1.0 - ADAM_B1 ** ADAM_STEP)
    v_hat = v / (1.0 - ADAM_B2 ** ADAM_STEP)
    return -ADAM_LR * (m_hat / (jnp.sqrt(v_hat) + ADAM_EPS) + ADAM_WD * w), m, v


def _small_reduce_adamw(part, w, m, v):
    def body(part_ref, w_ref, m_ref, v_ref, g_ref, d_ref, nm_ref, nv_ref, all_ref, send_sems, recv_sems):
        x, y, c, chips = _place()
        me, sibling = (x, y, c), (x, y, 1 - c)

        def rows(px, py, pc):
            return all_ref.at[4 * px + 2 * py + pc]

        all_ref[4 * x + 2 * y + c] = part_ref[...]
        first = [_remote(part_ref, rows(*me), send_sems, recv_sems, 0, sibling)]
        first += [_remote(part_ref, rows(*me), send_sems, recv_sems, 1 + j, (cx, cy, c))
                  for j, (cx, cy) in enumerate(chips)]
        for cp in first:
            cp.start()
        passed = []
        for j, (cx, cy) in enumerate(chips):
            _remote(part_ref, rows(cx, cy, c), send_sems, recv_sems, 1 + j, me).wait_recv()
            cp = _remote(rows(cx, cy, c), rows(cx, cy, c), send_sems, recv_sems, 4 + j, sibling)
            cp.start()
            passed.append(cp)
        _remote(part_ref, rows(*sibling), send_sems, recv_sems, 0, me).wait_recv()
        for j, (cx, cy) in enumerate(chips):
            _remote(part_ref, rows(cx, cy, 1 - c), send_sems, recv_sems, 4 + j, me).wait_recv()
        for cp in first + passed:
            cp.wait_send()
        g = all_ref[0]
        for dev in range(1, N_DEV):
            g = g + all_ref[dev]
        delta, nm, nv = _adamw(w_ref[...], g, m_ref[...], v_ref[...])
        g_ref[...] = g
        d_ref[...] = delta
        nm_ref[...] = nm
        nv_ref[...] = nv

    whole = pl.BlockSpec(memory_space=pltpu.VMEM)
    shape = jax.ShapeDtypeStruct((SMALL_ROWS, PACK_W), F32)
    return pl.pallas_call(
        body, name="small_reduce_adamw", in_specs=[whole] * 4, out_specs=[whole] * 4, out_shape=[shape] * 4,
        scratch_shapes=[pltpu.VMEM((N_DEV, SMALL_ROWS, PACK_W), F32), pltpu.SemaphoreType.DMA((7,)),
                        pltpu.SemaphoreType.DMA((7,))],
        compiler_params=pltpu.CompilerParams(vmem_limit_bytes=VMEM_LIMIT))(part, w, m, v)


def kernel(x, p, norm_g, w_in, hg_lb, hg_norm_g, w_o_hg, s5_a_re, s5_a_im, s5_log_dt, s5_b_re, s5_b_im, s5_c_re, s5_c_im, s5_d, w_glu, b_glu, w_o_s5, w_out, ple_norm_g, w_ple, w_ple_gate, final_norm_g, loss_target, m_norm_g, m_w_in, m_hg_lb, m_hg_norm_g, m_w_o_hg, m_s5_a_re, m_s5_a_im, m_s5_log_dt, m_s5_b_re, m_s5_b_im, m_s5_c_re, m_s5_c_im, m_s5_d, m_w_glu, m_b_glu, m_w_o_s5, m_w_out, m_ple_norm_g, m_w_ple, m_w_ple_gate, m_final_norm_g, v_norm_g, v_w_in, v_hg_lb, v_hg_norm_g, v_w_o_hg, v_s5_a_re, v_s5_a_im, v_s5_log_dt, v_s5_b_re, v_s5_b_im, v_s5_c_re, v_s5_c_im, v_s5_d, v_w_glu, v_b_glu, v_w_o_s5, v_w_out, v_ple_norm_g, v_w_ple, v_w_ple_gate, v_final_norm_g):
    given = dict(locals())
    wts = {n: given[n] for n in WEIGHTS}
    mom = {n: given["m_" + n] for n in WEIGHTS}
    var = {n: given["v_" + n] for n in WEIGHTS}
    cx, cy, cc = lax.axis_index("x"), lax.axis_index("y"), lax.axis_index("c")
    chip = (2 * cx + cy).astype(jnp.int32)

    core = cc.astype(jnp.int32)
    rest_shard = _pack_rest({n: wts[n][0] for n in REST})
    comm = _StepComm(wts["w_in"][0].astype(MXU_DTYPE).reshape(2, D_MODEL // 2, IN_SHARD),
                     rest_shard.astype(MXU_DTYPE).reshape(2, REST_ROWS // 2, PACK_W), chip, core)

    t_len = x.shape[1]
    loss_row, grad_x, g_big, g_small = _local_step(x.reshape(t_len, D_MODEL), p.reshape(t_len, -1),
                                                   loss_target.reshape(t_len, D_MODEL), None,
                                                   {n: wts[n] for n in SMALL}, comm)

    zero = jnp.zeros((), F32)
    sg, sd, snm, snv = _small_reduce_adamw(_pack_small(g_small, loss_row[0, 0]),
                                           _pack_small({n: wts[n] for n in SMALL}, zero),
                                           _pack_small({n: mom[n] for n in SMALL}, zero),
                                           _pack_small({n: var[n] for n in SMALL}, zero))
    (sg, loss), (sd, _), (snm, _), (snv, _) = (_unpack_small(a) for a in (sg, sd, snm, snv))

    halves = [comm.reduced("in"), comm.reduced("rest")]
    g_in, g_rest = [lax.dynamic_update_slice(got, mine[None], (core, 0, 0))
                    for got, mine in zip(_share_halves(halves), halves)]
    g_in, g_rest = g_in.reshape(D_MODEL, IN_SHARD), g_rest.reshape(REST_ROWS, PACK_W)

    def adam_f(wv, gv, mv, vv):
        return _adamw(wv, gv, mv, vv)

    d_in, nm_in, nv_in = _rowwise("adamw_in", adam_f, D_MODEL, IN_TILE,
                                  [(wts["w_in"][0], IN_SHARD, 0), (g_in, IN_SHARD, 0), (mom["w_in"][0], IN_SHARD, 0),
                                   (var["w_in"][0], IN_SHARD, 0)], [], [(IN_SHARD, F32)] * 3)
    d_rest, nm_rest, nv_rest = _rowwise("adamw_rest", adam_f, REST_ROWS, REST_TILE,
                                        [(rest_shard, PACK_W, 0), (g_rest, PACK_W, 0),
                                         (_pack_rest({n: mom[n][0] for n in REST}), PACK_W, 0),
                                         (_pack_rest({n: var[n][0] for n in REST}), PACK_W, 0)], [],
                                        [(PACK_W, F32)] * 3)
    bg, bd, bnm, bnv = (dict(_unpack_rest(rest), w_in=a.reshape(1, D_MODEL, IN_SHARD))
                        for rest, a in ((g_rest, g_in), (d_rest, d_in), (nm_rest, nm_in), (nv_rest, nv_in)))

    outs = [loss, grad_x.reshape(x.shape)]
    for small, big in ((sg, bg), (sd, bd), (snm, bnm), (snv, bnv)):
        outs += [big[n] if n in BIG else small[n] for n in WEIGHTS]
    return tuple(outs)
```
